```python
import jax, jax.numpy as jnp
from jax import lax
import numpy as np

D_MODEL = 1024
BATCH = 8
SEQ = 8192
DEPTH = 1

MEM_LEN = 256
CHUNK = 128
SG_GROUPS = 8
SG_GROUP_DIM = 64
SG_WIDTH = SG_GROUPS * SG_GROUP_DIM
MLA_HEADS = 8
MLA_NOPE = 64
MLA_ROPE = 32
MLA_V = 64
MLA_QK = MLA_NOPE + MLA_ROPE
MLA_Q_RANK = 384
MLA_KV_RANK = 256
MLA_WIDTH = MLA_HEADS * MLA_V
MEM_HEADS = 4
MEM_HEAD_DIM = 128
MEM_WIDTH = MEM_HEADS * MEM_HEAD_DIM
N_BRANCH = 3
D_FF = 2816
ROPE_BASE = 10000.0
EPS = 1e-6
Q_BLOCK = 128
NEG = -1e30

COL_U = 0
COL_V = COL_U + SG_WIDTH
COL_CQ = COL_V + SG_WIDTH
COL_CKV = COL_CQ + MLA_Q_RANK
COL_KR = COL_CKV + MLA_KV_RANK
COL_QM = COL_KR + MLA_ROPE
COL_GATE = COL_QM + MEM_WIDTH
IN_COLS = COL_GATE + N_BRANCH * D_MODEL

kernel_name = "hybrid_gated_sgu_mla_memxattn_macaron"


def rmsnorm(x, g):
    xf = x.astype(jnp.float32)
    y = xf * lax.rsqrt(jnp.mean(xf * xf, axis=-1, keepdims=True) + EPS)
    return (y * g.astype(jnp.float32)).astype(x.dtype)


def layernorm(x, g, b):
    xf = x.astype(jnp.float32)
    mu = jnp.mean(xf, axis=-1, keepdims=True)
    xc = xf - mu
    y = xc * lax.rsqrt(jnp.mean(xc * xc, axis=-1, keepdims=True) + EPS)
    return (y * g.astype(jnp.float32) + b.astype(jnp.float32)).astype(x.dtype)


def rope(x, positions):
    half = x.shape[-1] // 2
    inv = ROPE_BASE ** (-jnp.arange(half, dtype=jnp.float32) / half)
    ang = positions.astype(jnp.float32)[:, :, None] * inv
    cos = jnp.cos(ang)[:, :, None, :]
    sin = jnp.sin(ang)[:, :, None, :]
    x1 = x[..., :half].astype(jnp.float32)
    x2 = x[..., half:].astype(jnp.float32)
    return jnp.concatenate([x1 * cos - x2 * sin, x2 * cos + x1 * sin], axis=-1).astype(x.dtype)


def swiglu(x, w_gu, w_down):
    g, u = jnp.split(x @ w_gu, 2, axis=-1)
    return (jax.nn.silu(g) * u) @ w_down


def spatial_gating(u, v, ln_g, ln_b, w_s, b_s):
    B, S, _ = v.shape
    nc = S // CHUNK
    v = layernorm(v, ln_g, ln_b).reshape(B, nc, CHUNK, SG_GROUPS, SG_GROUP_DIM)
    causal = jnp.tril(jnp.ones((CHUNK, CHUNK), dtype=bool))
    w = jnp.where(causal[None], w_s, jnp.zeros_like(w_s))
    mixed = jnp.einsum('gts,bcsgd->bctgd', w, v) + b_s.T[None, None, :, :, None]
    return u * mixed.reshape(B, S, SG_WIDTH)


def causal_block_attention(q, k, v):
    B, S, H, Dqk = q.shape
    Dv = v.shape[-1]
    nb = S // Q_BLOCK
    scale = Dqk ** -0.5
    qb = q.reshape(B, nb, Q_BLOCK, H, Dqk).transpose(1, 0, 2, 3, 4)
    kpos = jnp.arange(S)

    def one_block(args):
        i, qi = args
        s = jnp.einsum('bqhd,bkhd->bhqk', qi, k).astype(jnp.float32) * scale
        qpos = i * Q_BLOCK + jnp.arange(Q_BLOCK)
        mask = kpos[None, :] <= qpos[:, None]
        s = jnp.where(mask[None, None], s, NEG)
        p = jax.nn.softmax(s, axis=-1)
        return jnp.einsum('bhqk,bkhd->bqhd', p.astype(v.dtype), v)

    out = lax.map(one_block, (jnp.arange(nb), qb))
    return out.transpose(1, 0, 2, 3, 4).reshape(B, S, H * Dv)


def mla(c_q, c_kv, k_rope, positions, cq_norm, w_uq, ckv_norm, w_ukv, q_norm, k_norm):
    B, S, _ = c_q.shape
    q = (rmsnorm(c_q, cq_norm) @ w_uq).reshape(B, S, MLA_HEADS, MLA_QK)
    q = rmsnorm(q, q_norm)
    q = jnp.concatenate([q[..., :MLA_NOPE], rope(q[..., MLA_NOPE:], positions)], axis=-1)
    kv = (rmsnorm(c_kv, ckv_norm) @ w_ukv).reshape(B, S, MLA_HEADS, MLA_NOPE + MLA_V)
    k_nope, v = kv[..., :MLA_NOPE], kv[..., MLA_NOPE:]
    k_pe = jnp.broadcast_to(k_rope[:, :, None, :], (B, S, MLA_HEADS, MLA_ROPE))
    k = rmsnorm(jnp.concatenate([k_nope, k_pe], axis=-1), k_norm)
    k = jnp.concatenate([k[..., :MLA_NOPE], rope(k[..., MLA_NOPE:], positions)], axis=-1)
    return causal_block_attention(q, k, v)


def memory_attention(q_m, mem, mem_norm, w_kv, q_norm, k_norm):
    B, S, _ = q_m.shape
    q = rmsnorm(q_m.reshape(B, S, MEM_HEADS, MEM_HEAD_DIM), q_norm)
    kv = rmsnorm(mem, mem_norm) @ w_kv
    M = mem.shape[1]
    k = rmsnorm(kv[..., :MEM_WIDTH].reshape(B, M, MEM_HEADS, MEM_HEAD_DIM), k_norm)
    v = kv[..., MEM_WIDTH:].reshape(B, M, MEM_HEADS, MEM_HEAD_DIM)
    s = jnp.einsum('bshd,bmhd->bhsm', q, k).astype(jnp.float32) * (MEM_HEAD_DIM ** -0.5)
    p = jax.nn.softmax(s, axis=-1)
    return jnp.einsum('bhsm,bmhd->bshd', p.astype(v.dtype), v).reshape(B, S, MEM_WIDTH)


def _fwd_setup_inputs(seed: int = 0) -> dict:
    key = jax.random.key(seed)
    ks = iter(jax.random.split(key, 40))

    def nrm(shape, scale):
        return jax.random.normal(next(ks), shape, jnp.float32) * scale

    def gain(n):
        return 1.0 + nrm((DEPTH, n), 0.05)

    L = DEPTH
    d = D_MODEL
    x = nrm((BATCH, SEQ, d), 1.0)
    mem = nrm((BATCH, MEM_LEN, d), 1.0)
    offset = jax.random.randint(next(ks), (BATCH, 1), 0, 1024, dtype=jnp.int32)
    positions = offset + jnp.arange(SEQ, dtype=jnp.int32)[None, :]
    return {
        "x": x,
        "mem": mem,
        "positions": positions,
        "ffn1_norm": gain(d),
        "ffn1_w_gu": nrm((L, d, 2 * D_FF), d ** -0.5),
        "ffn1_w_down": nrm((L, D_FF, d), D_FF ** -0.5),
        "mix_norm": gain(d),
        "w_in": nrm((L, d, IN_COLS), d ** -0.5),
        "b_gate": nrm((L, N_BRANCH * d), 0.02),
        "sg_ln_g": gain(SG_WIDTH),
        "sg_ln_b": nrm((L, SG_WIDTH), 0.02),
        "sg_w": nrm((L, SG_GROUPS, CHUNK, CHUNK), 0.5 * CHUNK ** -0.5),
        "sg_b": 1.0 + nrm((L, SG_GROUPS, CHUNK), 0.1),
        "mla_cq_norm": gain(MLA_Q_RANK),
        "mla_w_uq": nrm((L, MLA_Q_RANK, MLA_HEADS * MLA_QK), MLA_Q_RANK ** -0.5),
        "mla_ckv_norm": gain(MLA_KV_RANK),
        "mla_w_ukv": nrm((L, MLA_KV_RANK, MLA_HEADS * (MLA_NOPE + MLA_V)), MLA_KV_RANK ** -0.5),
        "mla_q_norm": gain(MLA_QK),
        "mla_k_norm": gain(MLA_QK),
        "mem_norm": gain(d),
        "mem_w_kv": nrm((L, d, 2 * MEM_WIDTH), d ** -0.5),
        "mem_q_norm": gain(MEM_HEAD_DIM),
        "mem_k_norm": gain(MEM_HEAD_DIM),
        "w_branch_a": nrm((L, SG_WIDTH, d), SG_WIDTH ** -0.5),
        "w_branch_b": nrm((L, MLA_WIDTH, d), MLA_WIDTH ** -0.5),
        "w_branch_c": nrm((L, MEM_WIDTH, d), MEM_WIDTH ** -0.5),
        "w_out": nrm((L, d, d), d ** -0.5),
        "ffn2_norm": gain(d),
        "ffn2_w_gu": nrm((L, d, 2 * D_FF), d ** -0.5),
        "ffn2_w_down": nrm((L, D_FF, d), D_FF ** -0.5),
    }


def _fwd_reference(x, mem, positions, ffn1_norm, ffn1_w_gu, ffn1_w_down, mix_norm, w_in, b_gate,
              sg_ln_g, sg_ln_b, sg_w, sg_b, mla_cq_norm, mla_w_uq, mla_ckv_norm, mla_w_ukv,
              mla_q_norm, mla_k_norm, mem_norm, mem_w_kv, mem_q_norm, mem_k_norm,
              w_branch_a, w_branch_b, w_branch_c, w_out, ffn2_norm, ffn2_w_gu, ffn2_w_down):
    B, S, _ = x.shape
    for l in range(DEPTH):
        x = x + 0.5 * swiglu(rmsnorm(x, ffn1_norm[l]), ffn1_w_gu[l], ffn1_w_down[l])
        h = rmsnorm(x, mix_norm[l])
        z = h @ w_in[l]
        u = jax.nn.gelu(z[..., COL_U:COL_V], approximate=False)
        v = jax.nn.gelu(z[..., COL_V:COL_CQ], approximate=False)
        y_a = spatial_gating(u, v, sg_ln_g[l], sg_ln_b[l], sg_w[l], sg_b[l])
        y_b = mla(z[..., COL_CQ:COL_CKV], z[..., COL_CKV:COL_KR], z[..., COL_KR:COL_QM], positions,
                  mla_cq_norm[l], mla_w_uq[l], mla_ckv_norm[l], mla_w_ukv[l],
                  mla_q_norm[l], mla_k_norm[l])
        y_c = memory_attention(z[..., COL_QM:COL_GATE], mem, mem_norm[l], mem_w_kv[l],
                               mem_q_norm[l], mem_k_norm[l])
        gates = jax.nn.sigmoid(z[..., COL_GATE:] + b_gate[l]).reshape(B, S, N_BRANCH, D_MODEL)
        merged = (gates[:, :, 0] * (y_a @ w_branch_a[l])
                  + gates[:, :, 1] * (y_b @ w_branch_b[l])
                  + gates[:, :, 2] * (y_c @ w_branch_c[l]))
        x = x + merged @ w_out[l]
        x = x + 0.5 * swiglu(rmsnorm(x, ffn2_norm[l]), ffn2_w_gu[l], ffn2_w_down[l])
    return x


import jax as _jax
import jax.numpy as _jnp

TWIN_FORMAT = 'train_step'
FWD_PARAMS = ['x', 'mem', 'positions', 'ffn1_norm', 'ffn1_w_gu', 'ffn1_w_down', 'mix_norm', 'w_in', 'b_gate', 'sg_ln_g', 'sg_ln_b', 'sg_w', 'sg_b', 'mla_cq_norm', 'mla_w_uq', 'mla_ckv_norm', 'mla_w_ukv', 'mla_q_norm', 'mla_k_norm', 'mem_norm', 'mem_w_kv', 'mem_q_norm', 'mem_k_norm', 'w_branch_a', 'w_branch_b', 'w_branch_c', 'w_out', 'ffn2_norm', 'ffn2_w_gu', 'ffn2_w_down']
TWIN_WEIGHTS = ['ffn1_norm', 'ffn1_w_gu', 'ffn1_w_down', 'mix_norm', 'w_in', 'b_gate', 'sg_ln_g', 'sg_ln_b', 'sg_w', 'sg_b', 'mla_cq_norm', 'mla_w_uq', 'mla_ckv_norm', 'mla_w_ukv', 'mla_q_norm', 'mla_k_norm', 'mem_norm', 'mem_w_kv', 'mem_q_norm', 'mem_k_norm', 'w_branch_a', 'w_branch_b', 'w_branch_c', 'w_out', 'ffn2_norm', 'ffn2_w_gu', 'ffn2_w_down']
TWIN_DIFF_INPUT = 'x'
TWIN_INPUTS = ['x', 'mem', 'positions', 'ffn1_norm', 'ffn1_w_gu', 'ffn1_w_down', 'mix_norm', 'w_in', 'b_gate', 'sg_ln_g', 'sg_ln_b', 'sg_w', 'sg_b', 'mla_cq_norm', 'mla_w_uq', 'mla_ckv_norm', 'mla_w_ukv', 'mla_q_norm', 'mla_k_norm', 'mem_norm', 'mem_w_kv', 'mem_q_norm', 'mem_k_norm', 'w_branch_a', 'w_branch_b', 'w_branch_c', 'w_out', 'ffn2_norm', 'ffn2_w_gu', 'ffn2_w_down', 'loss_target', 'm_ffn1_norm', 'm_ffn1_w_gu', 'm_ffn1_w_down', 'm_mix_norm', 'm_w_in', 'm_b_gate', 'm_sg_ln_g', 'm_sg_ln_b', 'm_sg_w', 'm_sg_b', 'm_mla_cq_norm', 'm_mla_w_uq', 'm_mla_ckv_norm', 'm_mla_w_ukv', 'm_mla_q_norm', 'm_mla_k_norm', 'm_mem_norm', 'm_mem_w_kv', 'm_mem_q_norm', 'm_mem_k_norm', 'm_w_branch_a', 'm_w_branch_b', 'm_w_branch_c', 'm_w_out', 'm_ffn2_norm', 'm_ffn2_w_gu', 'm_ffn2_w_down', 'v_ffn1_norm', 'v_ffn1_w_gu', 'v_ffn1_w_down', 'v_mix_norm', 'v_w_in', 'v_b_gate', 'v_sg_ln_g', 'v_sg_ln_b', 'v_sg_w', 'v_sg_b', 'v_mla_cq_norm', 'v_mla_w_uq', 'v_mla_ckv_norm', 'v_mla_w_ukv', 'v_mla_q_norm', 'v_mla_k_norm', 'v_mem_norm', 'v_mem_w_kv', 'v_mem_q_norm', 'v_mem_k_norm', 'v_w_branch_a', 'v_w_branch_b', 'v_w_branch_c', 'v_w_out', 'v_ffn2_norm', 'v_ffn2_w_gu', 'v_ffn2_w_down']
TWIN_OUTPUTS = ['loss', 'grad_x', 'grad_ffn1_norm', 'grad_ffn1_w_gu', 'grad_ffn1_w_down', 'grad_mix_norm', 'grad_w_in', 'grad_b_gate', 'grad_sg_ln_g', 'grad_sg_ln_b', 'grad_sg_w', 'grad_sg_b', 'grad_mla_cq_norm', 'grad_mla_w_uq', 'grad_mla_ckv_norm', 'grad_mla_w_ukv', 'grad_mla_q_norm', 'grad_mla_k_norm', 'grad_mem_norm', 'grad_mem_w_kv', 'grad_mem_q_norm', 'grad_mem_k_norm', 'grad_w_branch_a', 'grad_w_branch_b', 'grad_w_branch_c', 'grad_w_out', 'grad_ffn2_norm', 'grad_ffn2_w_gu', 'grad_ffn2_w_down', 'delta_ffn1_norm', 'delta_ffn1_w_gu', 'delta_ffn1_w_down', 'delta_mix_norm', 'delta_w_in', 'delta_b_gate', 'delta_sg_ln_g', 'delta_sg_ln_b', 'delta_sg_w', 'delta_sg_b', 'delta_mla_cq_norm', 'delta_mla_w_uq', 'delta_mla_ckv_norm', 'delta_mla_w_ukv', 'delta_mla_q_norm', 'delta_mla_k_norm', 'delta_mem_norm', 'delta_mem_w_kv', 'delta_mem_q_norm', 'delta_mem_k_norm', 'delta_w_branch_a', 'delta_w_branch_b', 'delta_w_branch_c', 'delta_w_out', 'delta_ffn2_norm', 'delta_ffn2_w_gu', 'delta_ffn2_w_down', 'new_m_ffn1_norm', 'new_m_ffn1_w_gu', 'new_m_ffn1_w_down', 'new_m_mix_norm', 'new_m_w_in', 'new_m_b_gate', 'new_m_sg_ln_g', 'new_m_sg_ln_b', 'new_m_sg_w', 'new_m_sg_b', 'new_m_mla_cq_norm', 'new_m_mla_w_uq', 'new_m_mla_ckv_norm', 'new_m_mla_w_ukv', 'new_m_mla_q_norm', 'new_m_mla_k_norm', 'new_m_mem_norm', 'new_m_mem_w_kv', 'new_m_mem_q_norm', 'new_m_mem_k_norm', 'new_m_w_branch_a', 'new_m_w_branch_b', 'new_m_w_branch_c', 'new_m_w_out', 'new_m_ffn2_norm', 'new_m_ffn2_w_gu', 'new_m_ffn2_w_down', 'new_v_ffn1_norm', 'new_v_ffn1_w_gu', 'new_v_ffn1_w_down', 'new_v_mix_norm', 'new_v_w_in', 'new_v_b_gate', 'new_v_sg_ln_g', 'new_v_sg_ln_b', 'new_v_sg_w', 'new_v_sg_b', 'new_v_mla_cq_norm', 'new_v_mla_w_uq', 'new_v_mla_ckv_norm', 'new_v_mla_w_ukv', 'new_v_mla_q_norm', 'new_v_mla_k_norm', 'new_v_mem_norm', 'new_v_mem_w_kv', 'new_v_mem_q_norm', 'new_v_mem_k_norm', 'new_v_w_branch_a', 'new_v_w_branch_b', 'new_v_w_branch_c', 'new_v_w_out', 'new_v_ffn2_norm', 'new_v_ffn2_w_gu', 'new_v_ffn2_w_down']
TWIN_LEAF_KINDS = {'loss': 'loss', 'grad_x': 'grad_x', 'grad_ffn1_norm': 'grad_w', 'grad_ffn1_w_gu': 'grad_w', 'grad_ffn1_w_down': 'grad_w', 'grad_mix_norm': 'grad_w', 'grad_w_in': 'grad_w', 'grad_b_gate': 'grad_w', 'grad_sg_ln_g': 'grad_w', 'grad_sg_ln_b': 'grad_w', 'grad_sg_w': 'grad_w', 'grad_sg_b': 'grad_w', 'grad_mla_cq_norm': 'grad_w', 'grad_mla_w_uq': 'grad_w', 'grad_mla_ckv_norm': 'grad_w', 'grad_mla_w_ukv': 'grad_w', 'grad_mla_q_norm': 'grad_w', 'grad_mla_k_norm': 'grad_w', 'grad_mem_norm': 'grad_w', 'grad_mem_w_kv': 'grad_w', 'grad_mem_q_norm': 'grad_w', 'grad_mem_k_norm': 'grad_w', 'grad_w_branch_a': 'grad_w', 'grad_w_branch_b': 'grad_w', 'grad_w_branch_c': 'grad_w', 'grad_w_out': 'grad_w', 'grad_ffn2_norm': 'grad_w', 'grad_ffn2_w_gu': 'grad_w', 'grad_ffn2_w_down': 'grad_w', 'delta_ffn1_norm': 'delta_w', 'delta_ffn1_w_gu': 'delta_w', 'delta_ffn1_w_down': 'delta_w', 'delta_mix_norm': 'delta_w', 'delta_w_in': 'delta_w', 'delta_b_gate': 'delta_w', 'delta_sg_ln_g': 'delta_w', 'delta_sg_ln_b': 'delta_w', 'delta_sg_w': 'delta_w', 'delta_sg_b': 'delta_w', 'delta_mla_cq_norm': 'delta_w', 'delta_mla_w_uq': 'delta_w', 'delta_mla_ckv_norm': 'delta_w', 'delta_mla_w_ukv': 'delta_w', 'delta_mla_q_norm': 'delta_w', 'delta_mla_k_norm': 'delta_w', 'delta_mem_norm': 'delta_w', 'delta_mem_w_kv': 'delta_w', 'delta_mem_q_norm': 'delta_w', 'delta_mem_k_norm': 'delta_w', 'delta_w_branch_a': 'delta_w', 'delta_w_branch_b': 'delta_w', 'delta_w_branch_c': 'delta_w', 'delta_w_out': 'delta_w', 'delta_ffn2_norm': 'delta_w', 'delta_ffn2_w_gu': 'delta_w', 'delta_ffn2_w_down': 'delta_w', 'new_m_ffn1_norm': 'new_m', 'new_m_ffn1_w_gu': 'new_m', 'new_m_ffn1_w_down': 'new_m', 'new_m_mix_norm': 'new_m', 'new_m_w_in': 'new_m', 'new_m_b_gate': 'new_m', 'new_m_sg_ln_g': 'new_m', 'new_m_sg_ln_b': 'new_m', 'new_m_sg_w': 'new_m', 'new_m_sg_b': 'new_m', 'new_m_mla_cq_norm': 'new_m', 'new_m_mla_w_uq': 'new_m', 'new_m_mla_ckv_norm': 'new_m', 'new_m_mla_w_ukv': 'new_m', 'new_m_mla_q_norm': 'new_m', 'new_m_mla_k_norm': 'new_m', 'new_m_mem_norm': 'new_m', 'new_m_mem_w_kv': 'new_m', 'new_m_mem_q_norm': 'new_m', 'new_m_mem_k_norm': 'new_m', 'new_m_w_branch_a': 'new_m', 'new_m_w_branch_b': 'new_m', 'new_m_w_branch_c': 'new_m', 'new_m_w_out': 'new_m', 'new_m_ffn2_norm': 'new_m', 'new_m_ffn2_w_gu': 'new_m', 'new_m_ffn2_w_down': 'new_m', 'new_v_ffn1_norm': 'new_v', 'new_v_ffn1_w_gu': 'new_v', 'new_v_ffn1_w_down': 'new_v', 'new_v_mix_norm': 'new_v', 'new_v_w_in': 'new_v', 'new_v_b_gate': 'new_v', 'new_v_sg_ln_g': 'new_v', 'new_v_sg_ln_b': 'new_v', 'new_v_sg_w': 'new_v', 'new_v_sg_b': 'new_v', 'new_v_mla_cq_norm': 'new_v', 'new_v_mla_w_uq': 'new_v', 'new_v_mla_ckv_norm': 'new_v', 'new_v_mla_w_ukv': 'new_v', 'new_v_mla_q_norm': 'new_v', 'new_v_mla_k_norm': 'new_v', 'new_v_mem_norm': 'new_v', 'new_v_mem_w_kv': 'new_v', 'new_v_mem_q_norm': 'new_v', 'new_v_mem_k_norm': 'new_v', 'new_v_w_branch_a': 'new_v', 'new_v_w_branch_b': 'new_v', 'new_v_w_branch_c': 'new_v', 'new_v_w_out': 'new_v', 'new_v_ffn2_norm': 'new_v', 'new_v_ffn2_w_gu': 'new_v', 'new_v_ffn2_w_down': 'new_v'}


def _forward(args):
    return _fwd_reference(*[args[k] for k in FWD_PARAMS])


def _output_shape():
    def fwd():
        inp = _fwd_setup_inputs(0)
        return _fwd_reference(*[inp[k] for k in FWD_PARAMS])
    out = _jax.eval_shape(fwd)
    return out.shape, out.dtype

N_MICROBATCH = 1
ADAM_LR = 0.001
ADAM_B1 = 0.9
ADAM_B2 = 0.999
ADAM_EPS = 1e-08
ADAM_WD = 0.01
ADAM_STEP = 10
PER_EXAMPLE_BATCH_AXIS = {'x': 0, 'mem': 0, 'positions': 0, 'loss_target': 0}
SHARED_INPUTS = []
_WEIGHT_DTYPES = {'ffn1_norm': _jnp.float32, 'ffn1_w_gu': _jnp.float32, 'ffn1_w_down': _jnp.float32, 'mix_norm': _jnp.float32, 'w_in': _jnp.float32, 'b_gate': _jnp.float32, 'sg_ln_g': _jnp.float32, 'sg_ln_b': _jnp.float32, 'sg_w': _jnp.float32, 'sg_b': _jnp.float32, 'mla_cq_norm': _jnp.float32, 'mla_w_uq': _jnp.float32, 'mla_ckv_norm': _jnp.float32, 'mla_w_ukv': _jnp.float32, 'mla_q_norm': _jnp.float32, 'mla_k_norm': _jnp.float32, 'mem_norm': _jnp.float32, 'mem_w_kv': _jnp.float32, 'mem_q_norm': _jnp.float32, 'mem_k_norm': _jnp.float32, 'w_branch_a': _jnp.float32, 'w_branch_b': _jnp.float32, 'w_branch_c': _jnp.float32, 'w_out': _jnp.float32, 'ffn2_norm': _jnp.float32, 'ffn2_w_gu': _jnp.float32, 'ffn2_w_down': _jnp.float32}
MOMENT_SCALE = {'ffn1_norm': 1.217092e+01, 'ffn1_w_gu': 1.176658e-01, 'ffn1_w_down': 2.135333e-01, 'mix_norm': 1.169308e+01, 'w_in': 2.348335e-01, 'b_gate': 1.985077e+00, 'sg_ln_g': 1.995743e+00, 'sg_ln_b': 4.666918e-01, 'sg_w': 2.012362e-01, 'sg_b': 8.206870e+00, 'mla_cq_norm': 7.872426e-02, 'mla_w_uq': 5.578088e-02, 'mla_ckv_norm': 7.016409e-01, 'mla_w_ukv': 8.728561e-02, 'mla_q_norm': 1.032864e+00, 'mla_k_norm': 1.018398e+00, 'mem_norm': 2.140494e-01, 'mem_w_kv': 1.294023e-01, 'mem_q_norm': 1.466674e+00, 'mem_k_norm': 1.476942e+00, 'w_branch_a': 2.174843e+00, 'w_branch_b': 7.144564e-02, 'w_branch_c': 1.254953e-01, 'w_out': 1.475607e+00, 'ffn2_norm': 1.228624e+01, 'ffn2_w_gu': 1.648220e-01, 'ffn2_w_down': 2.392012e-01}


def _to_microbatches(a, axis):
    t = _jnp.moveaxis(a, axis, 0)
    t = t.reshape((N_MICROBATCH, t.shape[0] // N_MICROBATCH) + t.shape[1:])
    return _jnp.moveaxis(t, 1, axis + 1)


def setup_inputs(seed: int = 0) -> dict:
    inp = _fwd_setup_inputs(seed)
    key = _jax.random.fold_in(_jax.random.key(seed), 7919)
    shape, _ = _output_shape()
    out = dict(inp)
    out["loss_target"] = _jax.random.normal(_jax.random.fold_in(key, 0), shape, _jnp.float32)
    for i, name in enumerate(TWIN_WEIGHTS):
        w = inp[name].astype(_jnp.float32)
        if MOMENT_SCALE is None:
            s = _jnp.sqrt(_jnp.mean(_jnp.square(w)) + 1e-30)
        else:
            s = MOMENT_SCALE[name]
        km, kv = _jax.random.split(_jax.random.fold_in(key, i + 1))
        out[name] = w
        out["m_" + name] = s * _jax.random.normal(km, w.shape, _jnp.float32)
        out["v_" + name] = (s * s) * _jax.random.uniform(kv, w.shape, _jnp.float32, 0.5, 1.5)
    if N_MICROBATCH > 1:
        for name, axis in PER_EXAMPLE_BATCH_AXIS.items():
            out[name] = _to_microbatches(out[name], axis)
    return {'x': out['x'], 'mem': out['mem'], 'positions': out['positions'], 'ffn1_norm': out['ffn1_norm'], 'ffn1_w_gu': out['ffn1_w_gu'], 'ffn1_w_down': out['ffn1_w_down'], 'mix_norm': out['mix_norm'], 'w_in': out['w_in'], 'b_gate': out['b_gate'], 'sg_ln_g': out['sg_ln_g'], 'sg_ln_b': out['sg_ln_b'], 'sg_w': out['sg_w'], 'sg_b': out['sg_b'], 'mla_cq_norm': out['mla_cq_norm'], 'mla_w_uq': out['mla_w_uq'], 'mla_ckv_norm': out['mla_ckv_norm'], 'mla_w_ukv': out['mla_w_ukv'], 'mla_q_norm': out['mla_q_norm'], 'mla_k_norm': out['mla_k_norm'], 'mem_norm': out['mem_norm'], 'mem_w_kv': out['mem_w_kv'], 'mem_q_norm': out['mem_q_norm'], 'mem_k_norm': out['mem_k_norm'], 'w_branch_a': out['w_branch_a'], 'w_branch_b': out['w_branch_b'], 'w_branch_c': out['w_branch_c'], 'w_out': out['w_out'], 'ffn2_norm': out['ffn2_norm'], 'ffn2_w_gu': out['ffn2_w_gu'], 'ffn2_w_down': out['ffn2_w_down'], 'loss_target': out['loss_target'], 'm_ffn1_norm': out['m_ffn1_norm'], 'm_ffn1_w_gu': out['m_ffn1_w_gu'], 'm_ffn1_w_down': out['m_ffn1_w_down'], 'm_mix_norm': out['m_mix_norm'], 'm_w_in': out['m_w_in'], 'm_b_gate': out['m_b_gate'], 'm_sg_ln_g': out['m_sg_ln_g'], 'm_sg_ln_b': out['m_sg_ln_b'], 'm_sg_w': out['m_sg_w'], 'm_sg_b': out['m_sg_b'], 'm_mla_cq_norm': out['m_mla_cq_norm'], 'm_mla_w_uq': out['m_mla_w_uq'], 'm_mla_ckv_norm': out['m_mla_ckv_norm'], 'm_mla_w_ukv': out['m_mla_w_ukv'], 'm_mla_q_norm': out['m_mla_q_norm'], 'm_mla_k_norm': out['m_mla_k_norm'], 'm_mem_norm': out['m_mem_norm'], 'm_mem_w_kv': out['m_mem_w_kv'], 'm_mem_q_norm': out['m_mem_q_norm'], 'm_mem_k_norm': out['m_mem_k_norm'], 'm_w_branch_a': out['m_w_branch_a'], 'm_w_branch_b': out['m_w_branch_b'], 'm_w_branch_c': out['m_w_branch_c'], 'm_w_out': out['m_w_out'], 'm_ffn2_norm': out['m_ffn2_norm'], 'm_ffn2_w_gu': out['m_ffn2_w_gu'], 'm_ffn2_w_down': out['m_ffn2_w_down'], 'v_ffn1_norm': out['v_ffn1_norm'], 'v_ffn1_w_gu': out['v_ffn1_w_gu'], 'v_ffn1_w_down': out['v_ffn1_w_down'], 'v_mix_norm': out['v_mix_norm'], 'v_w_in': out['v_w_in'], 'v_b_gate': out['v_b_gate'], 'v_sg_ln_g': out['v_sg_ln_g'], 'v_sg_ln_b': out['v_sg_ln_b'], 'v_sg_w': out['v_sg_w'], 'v_sg_b': out['v_sg_b'], 'v_mla_cq_norm': out['v_mla_cq_norm'], 'v_mla_w_uq': out['v_mla_w_uq'], 'v_mla_ckv_norm': out['v_mla_ckv_norm'], 'v_mla_w_ukv': out['v_mla_w_ukv'], 'v_mla_q_norm': out['v_mla_q_norm'], 'v_mla_k_norm': out['v_mla_k_norm'], 'v_mem_norm': out['v_mem_norm'], 'v_mem_w_kv': out['v_mem_w_kv'], 'v_mem_q_norm': out['v_mem_q_norm'], 'v_mem_k_norm': out['v_mem_k_norm'], 'v_w_branch_a': out['v_w_branch_a'], 'v_w_branch_b': out['v_w_branch_b'], 'v_w_branch_c': out['v_w_branch_c'], 'v_w_out': out['v_w_out'], 'v_ffn2_norm': out['v_ffn2_norm'], 'v_ffn2_w_gu': out['v_ffn2_w_gu'], 'v_ffn2_w_down': out['v_ffn2_w_down']}


def _loss(weights, diff, rest, loss_target):
    with _jax.named_scope("forward"):
        args = {**rest, TWIN_DIFF_INPUT: diff, **{k: w.astype(_WEIGHT_DTYPES[k]) for k, w in weights.items()}}
        y = _forward(args)
    with _jax.named_scope("loss_head"):
        err = _jnp.square(y.astype(_jnp.float32) - loss_target)
        return 0.5 * _jnp.sum(_jnp.mean(err, axis=-1)) if err.ndim else 0.5 * err


def _adamw(w, g, m, v):
    m = ADAM_B1 * m + (1.0 - ADAM_B1) * g
    v = ADAM_B2 * v + (1.0 - ADAM_B2) * _jnp.square(g)
    m_hat = m / (1.0 - ADAM_B1 ** ADAM_STEP)
    v_hat = v / (1.0 - ADAM_B2 ** ADAM_STEP)
    delta = -ADAM_LR * (m_hat / (_jnp.sqrt(v_hat) + ADAM_EPS) + ADAM_WD * w)
    return delta, m, v


def reference(x, mem, positions, ffn1_norm, ffn1_w_gu, ffn1_w_down, mix_norm, w_in, b_gate, sg_ln_g, sg_ln_b, sg_w, sg_b, mla_cq_norm, mla_w_uq, mla_ckv_norm, mla_w_ukv, mla_q_norm, mla_k_norm, mem_norm, mem_w_kv, mem_q_norm, mem_k_norm, w_branch_a, w_branch_b, w_branch_c, w_out, ffn2_norm, ffn2_w_gu, ffn2_w_down, loss_target, m_ffn1_norm, m_ffn1_w_gu, m_ffn1_w_down, m_mix_norm, m_w_in, m_b_gate, m_sg_ln_g, m_sg_ln_b, m_sg_w, m_sg_b, m_mla_cq_norm, m_mla_w_uq, m_mla_ckv_norm, m_mla_w_ukv, m_mla_q_norm, m_mla_k_norm, m_mem_norm, m_mem_w_kv, m_mem_q_norm, m_mem_k_norm, m_w_branch_a, m_w_branch_b, m_w_branch_c, m_w_out, m_ffn2_norm, m_ffn2_w_gu, m_ffn2_w_down, v_ffn1_norm, v_ffn1_w_gu, v_ffn1_w_down, v_mix_norm, v_w_in, v_b_gate, v_sg_ln_g, v_sg_ln_b, v_sg_w, v_sg_b, v_mla_cq_norm, v_mla_w_uq, v_mla_ckv_norm, v_mla_w_ukv, v_mla_q_norm, v_mla_k_norm, v_mem_norm, v_mem_w_kv, v_mem_q_norm, v_mem_k_norm, v_w_branch_a, v_w_branch_b, v_w_branch_c, v_w_out, v_ffn2_norm, v_ffn2_w_gu, v_ffn2_w_down):
    given = dict(x=x, mem=mem, positions=positions, ffn1_norm=ffn1_norm, ffn1_w_gu=ffn1_w_gu, ffn1_w_down=ffn1_w_down, mix_norm=mix_norm, w_in=w_in, b_gate=b_gate, sg_ln_g=sg_ln_g, sg_ln_b=sg_ln_b, sg_w=sg_w, sg_b=sg_b, mla_cq_norm=mla_cq_norm, mla_w_uq=mla_w_uq, mla_ckv_norm=mla_ckv_norm, mla_w_ukv=mla_w_ukv, mla_q_norm=mla_q_norm, mla_k_norm=mla_k_norm, mem_norm=mem_norm, mem_w_kv=mem_w_kv, mem_q_norm=mem_q_norm, mem_k_norm=mem_k_norm, w_branch_a=w_branch_a, w_branch_b=w_branch_b, w_branch_c=w_branch_c, w_out=w_out, ffn2_norm=ffn2_norm, ffn2_w_gu=ffn2_w_gu, ffn2_w_down=ffn2_w_down, loss_target=loss_target, m_ffn1_norm=m_ffn1_norm, m_ffn1_w_gu=m_ffn1_w_gu, m_ffn1_w_down=m_ffn1_w_down, m_mix_norm=m_mix_norm, m_w_in=m_w_in, m_b_gate=m_b_gate, m_sg_ln_g=m_sg_ln_g, m_sg_ln_b=m_sg_ln_b, m_sg_w=m_sg_w, m_sg_b=m_sg_b, m_mla_cq_norm=m_mla_cq_norm, m_mla_w_uq=m_mla_w_uq, m_mla_ckv_norm=m_mla_ckv_norm, m_mla_w_ukv=m_mla_w_ukv, m_mla_q_norm=m_mla_q_norm, m_mla_k_norm=m_mla_k_norm, m_mem_norm=m_mem_norm, m_mem_w_kv=m_mem_w_kv, m_mem_q_norm=m_mem_q_norm, m_mem_k_norm=m_mem_k_norm, m_w_branch_a=m_w_branch_a, m_w_branch_b=m_w_branch_b, m_w_branch_c=m_w_branch_c, m_w_out=m_w_out, m_ffn2_norm=m_ffn2_norm, m_ffn2_w_gu=m_ffn2_w_gu, m_ffn2_w_down=m_ffn2_w_down, v_ffn1_norm=v_ffn1_norm, v_ffn1_w_gu=v_ffn1_w_gu, v_ffn1_w_down=v_ffn1_w_down, v_mix_norm=v_mix_norm, v_w_in=v_w_in, v_b_gate=v_b_gate, v_sg_ln_g=v_sg_ln_g, v_sg_ln_b=v_sg_ln_b, v_sg_w=v_sg_w, v_sg_b=v_sg_b, v_mla_cq_norm=v_mla_cq_norm, v_mla_w_uq=v_mla_w_uq, v_mla_ckv_norm=v_mla_ckv_norm, v_mla_w_ukv=v_mla_w_ukv, v_mla_q_norm=v_mla_q_norm, v_mla_k_norm=v_mla_k_norm, v_mem_norm=v_mem_norm, v_mem_w_kv=v_mem_w_kv, v_mem_q_norm=v_mem_q_norm, v_mem_k_norm=v_mem_k_norm, v_w_branch_a=v_w_branch_a, v_w_branch_b=v_w_branch_b, v_w_branch_c=v_w_branch_c, v_w_out=v_w_out, v_ffn2_norm=v_ffn2_norm, v_ffn2_w_gu=v_ffn2_w_gu, v_ffn2_w_down=v_ffn2_w_down)
    weights = {n: given[n] for n in TWIN_WEIGHTS}
    shared = {n: given[n] for n in SHARED_INPUTS}
    per_example = {n: given[n] for n in ['x', 'mem', 'positions']}
    grad_fn = _jax.value_and_grad(_loss, argnums=(0, 1))

    def one_microbatch(ex, loss_target):
        ex = dict(ex)
        diff = ex.pop(TWIN_DIFF_INPUT)
        return grad_fn(weights, diff, {**shared, **ex}, loss_target)

    if N_MICROBATCH == 1:
        loss, (grad_w, grad_x) = one_microbatch(per_example, given["loss_target"])
    else:
        def body(carry, xs):
            loss_sum, grad_sum = carry
            l_k, (gw_k, gx_k) = one_microbatch(xs[0], xs[1])
            with _jax.named_scope("update"):
                return (loss_sum + l_k, _jax.tree.map(_jnp.add, grad_sum, gw_k)), gx_k

        init = (_jnp.zeros((), _jnp.float32), _jax.tree.map(_jnp.zeros_like, weights))
        (loss, grad_w), grad_x = _jax.lax.scan(body, init, (per_example, given["loss_target"]))
    with _jax.named_scope("update"):
        delta_w, new_m, new_v = {}, {}, {}
        for n in TWIN_WEIGHTS:
            delta_w[n], new_m[n], new_v[n] = _adamw(weights[n], grad_w[n], given["m_" + n], given["v_" + n])
    return (loss, grad_x, *[grad_w[n] for n in TWIN_WEIGHTS], *[delta_w[n] for n in TWIN_WEIGHTS],
            *[new_m[n] for n in TWIN_WEIGHTS], *[new_v[n] for n in TWIN_WEIGHTS])
```

```python
import functools

import numpy as np
import jax
import jax.numpy as jnp
from jax import lax
from jax.experimental import pallas as pl
from jax.experimental.pallas import tpu as pltpu

F32, BF16 = jnp.float32, jnp.bfloat16

D_MODEL = 1024
SG_GROUPS, SG_GROUP_DIM, SG_WIDTH, CHUNK = 8, 64, 512, 128
MLA_HEADS, MLA_NOPE, MLA_ROPE, MLA_V, MLA_QK = 8, 64, 32, 64, 96
MLA_Q_RANK, MLA_KV_RANK = 384, 256
MEM_HEADS, MEM_HEAD_DIM, MEM_WIDTH = 4, 128, 512
D_FF = 2816
ROPE_BASE = 10000.0
EPS = 1e-6
NEG = -1e30
ADAM_LR, ADAM_B1, ADAM_B2, ADAM_EPS, ADAM_WD, ADAM_STEP = 0.001, 0.9, 0.999, 1e-08, 0.01, 10

N_DEV = 8
LANES = 128
V7X_VMEM_LIMIT = 56 * 1024 * 1024
HP = MLA_HEADS * LANES

Z_G, Z_U, Z_V, Z_QM, Z_CKV, Z_KR, Z_CQ = 0, 3072, 3584, 4096, 4608, 4864, 4992
Z_COLS = 5376
KR_LANE = 64


def _tile(dim, pref):
    if dim <= pref:
        return dim
    for t in range(pref - pref % LANES, LANES - 1, -LANES):
        if dim % t == 0:
            return t
    for t in range(pref - pref % 8, 7, -8):
        if dim % t == 0:
            return t
    return dim


def _cparams(sem):
    return pltpu.CompilerParams(dimension_semantics=sem, vmem_limit_bytes=V7X_VMEM_LIMIT)


_DN = {"nn": ((1,), (0,)), "nt": ((1,), (1,)), "tn": ((0,), (0,))}


def _dot(a, b, mode="nn"):
    return lax.dot_general(a.astype(BF16), b.astype(BF16), (_DN[mode], ((), ())),
                           preferred_element_type=F32)


def _mm(a, b, mode, out_dtype, name, tm=512, tn=512, tk=2048):
    if mode == "tn":
        K, M = a.shape
    else:
        M, K = a.shape
    N = b.shape[0] if mode == "nt" else b.shape[1]
    tm, tn, tk = _tile(M, tm), _tile(N, tn), _tile(K, tk)
    nk = K // tk
    if mode == "tn":
        a_spec = pl.BlockSpec((tk, tm), lambda i, j, k: (k, i))
    else:
        a_spec = pl.BlockSpec((tm, tk), lambda i, j, k: (i, k))
    if mode == "nt":
        b_spec = pl.BlockSpec((tn, tk), lambda i, j, k: (j, k))
    else:
        b_spec = pl.BlockSpec((tk, tn), lambda i, j, k: (k, j))

    def body(a_ref, b_ref, o_ref, *scratch):
        p = _dot(a_ref[...], b_ref[...], mode)
        if nk == 1:
            o_ref[...] = p.astype(o_ref.dtype)
        else:
            acc_ref, = scratch
            k = pl.program_id(2)

            @pl.when(k == 0)
            def _():
                acc_ref[...] = p

            @pl.when(k > 0)
            def _():
                acc_ref[...] += p

            @pl.when(k == nk - 1)
            def _():
                o_ref[...] = acc_ref[...].astype(o_ref.dtype)

    return pl.pallas_call(
        body, name=name, grid=(M // tm, N // tn, nk),
        in_specs=[a_spec, b_spec],
        out_specs=pl.BlockSpec((tm, tn), lambda i, j, k: (i, j)),
        out_shape=jax.ShapeDtypeStruct((M, N), out_dtype),
        scratch_shapes=[] if nk == 1 else [pltpu.VMEM((tm, tn), F32)],
        compiler_params=_cparams(("parallel", "parallel", "arbitrary")),
    )(a, b)


def _rowwise(fn, name, tr, row_ins, bc_ins, row_outs, acc_outs=()):
    norm = [it if isinstance(it, tuple) else (it, it.shape[1], 0) for it in row_ins]
    rows = norm[0][0].shape[0]
    tr = _tile(rows, tr)
    arrays, in_specs = [], []
    for arr, w, cb in norm:
        arrays.append(arr)
        in_specs.append(pl.BlockSpec((tr, w), lambda i, cb=cb: (i, cb)))
    for arr in bc_ins:
        arrays.append(arr)
        in_specs.append(pl.BlockSpec(arr.shape, lambda i, nd=arr.ndim: (0,) * nd))
    out_shape, out_specs = [], []
    for w, dt in row_outs:
        out_shape.append(jax.ShapeDtypeStruct((rows, w), dt))
        out_specs.append(pl.BlockSpec((tr, w), lambda i: (i, 0)))
    for shp, dt in acc_outs:
        out_shape.append(jax.ShapeDtypeStruct(shp, dt))
        out_specs.append(pl.BlockSpec(shp, lambda i, nd=len(shp): (0,) * nd))
    n_in, n_row = len(arrays), len(row_outs)

    def body(*refs):
        vals = fn(*[r[...] for r in refs[:n_in]])
        if not isinstance(vals, (tuple, list)):
            vals = (vals,)
        outs = refs[n_in:]
        for r, v in zip(outs[:n_row], vals[:n_row]):
            r[...] = v.astype(r.dtype)
        if acc_outs:
            accs = list(zip(outs[n_row:], vals[n_row:]))
            i = pl.program_id(0)

            @pl.when(i == 0)
            def _():
                for r, v in accs:
                    r[...] = v.astype(r.dtype)

            @pl.when(i > 0)
            def _():
                for r, v in accs:
                    r[...] += v.astype(r.dtype)

    res = pl.pallas_call(
        body, name=name, grid=(rows // tr,), in_specs=in_specs, out_specs=out_specs,
        out_shape=out_shape, compiler_params=_cparams(("arbitrary",)),
    )(*arrays)
    return res


def _rsum(x):
    return jnp.sum(x, axis=0, keepdims=True)


def _rms(x, g, n=None):
    n = x.shape[-1] if n is None else n
    r = lax.rsqrt(jnp.sum(x * x, axis=-1, keepdims=True) * (1.0 / n) + EPS)
    return x * r * g


def _rms_bwd(x, g, dy, n=None):
    n = x.shape[-1] if n is None else n
    r = lax.rsqrt(jnp.sum(x * x, axis=-1, keepdims=True) * (1.0 / n) + EPS)
    xh = x * r
    dxh = dy * g
    dx = r * (dxh - xh * (jnp.sum(dxh * xh, axis=-1, keepdims=True) * (1.0 / n)))
    return dx, _rsum(dy * xh)


def _gelu(x):
    return 0.5 * x * (1.0 + lax.erf(x * 0.7071067811865476))


def _gelu_grad(x):
    return 0.5 * (1.0 + lax.erf(x * 0.7071067811865476)) + x * jnp.exp(-0.5 * x * x) * 0.3989422804014327


def _sigmoid(x):
    return 1.0 / (1.0 + jnp.exp(-x))


def _ffn_fwd(h, w_gu, w_down, tag):
    gu = _mm(h, w_gu, "nn", BF16, f"{tag}_gu", tm=1024, tn=512)

    def act(gu):
        g = gu[:, :D_FF].astype(F32)
        u = gu[:, D_FF:].astype(F32)
        return g * _sigmoid(g) * u

    a, = _rowwise(act, f"{tag}_act", 256, [gu], [], [(D_FF, BF16)])
    o = _mm(a, w_down, "nn", F32, f"{tag}_down", tm=1024, tn=512, tk=2816)
    return gu, a, o


def _ffn_bwd(do, h, gu, a, w_gu, w_down, tag):
    dw_down = _mm(a, do, "tn", F32, f"{tag}_dwdown", tm=512, tn=1024, tk=512)
    da = _mm(do, w_down, "nt", BF16, f"{tag}_da", tm=1024, tn=512)

    def act_bwd(gu, da):
        g = gu[:, :D_FF].astype(F32)
        u = gu[:, D_FF:].astype(F32)
        da = da.astype(F32)
        s = _sigmoid(g)
        dg = da * u * s * (1.0 + g * (1.0 - s))
        du = da * g * s
        return jnp.concatenate([dg, du], axis=1)

    dgu, = _rowwise(act_bwd, f"{tag}_actbwd", 256, [gu, da], [], [(2 * D_FF, BF16)])
    dw_gu = _mm(h, dgu, "tn", F32, f"{tag}_dwgu", tm=512, tn=512, tk=512)
    dh = _mm(dgu, w_gu, "nt", F32, f"{tag}_dh", tm=1024, tn=512, tk=2816)
    return dh, dw_gu, dw_down


def _sg_common(u_pre, v_pre, ln_g, ln_b):
    u = _gelu(u_pre)
    v = _gelu(v_pre)
    mu = jnp.mean(v, axis=-1, keepdims=True)
    vc = v - mu
    rstd = lax.rsqrt(jnp.mean(vc * vc, axis=-1, keepdims=True) + EPS)
    vhat = vc * rstd
    vl = vhat * ln_g + ln_b
    return u, vhat, rstd, vl


def _sg_masked_pairs(w):
    t = lax.broadcasted_iota(jnp.int32, (CHUNK, CHUNK), 0)
    s = lax.broadcasted_iota(jnp.int32, (CHUNK, CHUNK), 1)
    causal = s <= t
    wm = [jnp.where(causal, w[g], 0.0).astype(BF16) for g in range(SG_GROUPS)]
    return [jnp.concatenate([wm[2 * j], wm[2 * j + 1]], axis=0) for j in range(SG_GROUPS // 2)], causal


def _sg_mix(vl, pairs, bias):
    tr = vl.shape[0]
    low = lax.broadcasted_iota(jnp.int32, (CHUNK, LANES), 1) < SG_GROUP_DIM
    vb = vl.astype(BF16)
    rows = []
    for c in range(tr // CHUNK):
        slabs = []
        for j in range(SG_GROUPS // 2):
            slab = vb[c * CHUNK:(c + 1) * CHUNK, j * LANES:(j + 1) * LANES]
            m = _dot(pairs[j], slab)
            slabs.append(jnp.where(low, m[:CHUNK], m[CHUNK:]))
        rows.append(jnp.concatenate(slabs, axis=1) + bias)
    return jnp.concatenate(rows, axis=0)


def _sg_fwd(z, ln_g, ln_b, sg_w, bias_full):
    def fn(u_pre, v_pre, ln_g, ln_b, w, bias):
        u, _, _, vl = _sg_common(u_pre, v_pre, ln_g, ln_b)
        pairs, _ = _sg_masked_pairs(w)
        return u * _sg_mix(vl, pairs, bias)

    y, = _rowwise(fn, "sg_fwd", 512, [(z, SG_WIDTH, Z_U // SG_WIDTH), (z, SG_WIDTH, Z_V // SG_WIDTH)],
                  [ln_g, ln_b, sg_w, bias_full], [(SG_WIDTH, BF16)])
    return y


def _sg_bwd(z, dy, ln_g, ln_b, sg_w, bias_full, group_ind):
    def fn(u_pre, v_pre, dy, ln_g, ln_b, w, bias, ind):
        dy = dy.astype(F32)
        u, vhat, rstd, vl = _sg_common(u_pre, v_pre, ln_g, ln_b)
        pairs, causal = _sg_masked_pairs(w)
        mixed = _sg_mix(vl, pairs, bias)
        du_pre = dy * mixed * _gelu_grad(u_pre)
        dmix = dy * u
        tr = dy.shape[0]
        low = lax.broadcasted_iota(jnp.int32, (CHUNK, LANES), 1) < SG_GROUP_DIM
        vb = vl.astype(BF16)
        dw = [jnp.zeros((CHUNK, CHUNK), F32) for _ in range(SG_GROUPS)]
        dbias = jnp.zeros((CHUNK, SG_WIDTH), F32)
        dvl_rows = []
        for c in range(tr // CHUNK):
            dm_c = dmix[c * CHUNK:(c + 1) * CHUNK]
            dbias = dbias + dm_c
            slabs = []
            for j in range(SG_GROUPS // 2):
                slab = vb[c * CHUNK:(c + 1) * CHUNK, j * LANES:(j + 1) * LANES]
                dm = dm_c[:, j * LANES:(j + 1) * LANES]
                d0 = jnp.where(low, dm, 0.0).astype(BF16)
                d1 = jnp.where(low, 0.0, dm).astype(BF16)
                dw[2 * j] = dw[2 * j] + _dot(d0, slab, "nt")
                dw[2 * j + 1] = dw[2 * j + 1] + _dot(d1, slab, "nt")
                slabs.append(_dot(pairs[j], jnp.concatenate([d0, d1], axis=0), "tn"))
            dvl_rows.append(jnp.concatenate(slabs, axis=1))
        dvl = jnp.concatenate(dvl_rows, axis=0)
        dln_g = _rsum(dvl * vhat)
        dln_b = _rsum(dvl)
        dvh = dvl * ln_g
        dv = rstd * (dvh - jnp.mean(dvh, axis=-1, keepdims=True)
                     - vhat * jnp.mean(dvh * vhat, axis=-1, keepdims=True))
        dv_pre = dv * _gelu_grad(v_pre)
        dw = jnp.stack([jnp.where(causal, d, 0.0) for d in dw], axis=0)
        dbias_t = lax.dot_general(dbias, ind, (((1,), (0,)), ((), ())), precision=lax.Precision.HIGHEST,
                                  preferred_element_type=F32)
        return du_pre, dv_pre, dw, dbias_t, dln_g, dln_b

    return _rowwise(fn, "sg_bwd", 512,
                    [(z, SG_WIDTH, Z_U // SG_WIDTH), (z, SG_WIDTH, Z_V // SG_WIDTH), dy],
                    [ln_g, ln_b, sg_w, bias_full, group_ind],
                    [(SG_WIDTH, BF16), (SG_WIDTH, BF16)],
                    [((SG_GROUPS, CHUNK, CHUNK), F32), ((CHUNK, SG_GROUPS), F32), ((1, SG_WIDTH), F32), ((1, SG_WIDTH), F32)])


def _rope(x, c, s1, s2):
    return x * c + pltpu.roll(x, LANES - MLA_ROPE // 2, 1) * s1 + pltpu.roll(x, MLA_ROPE // 2, 1) * s2


def _rope_t(d, c, s1, s2):
    return d * c + pltpu.roll(d * s1, MLA_ROPE // 2, 1) + pltpu.roll(d * s2, LANES - MLA_ROPE // 2, 1)


def _mla_post(q_pre, kv_pre, z, tabs, gq, gk):
    scale = MLA_QK ** -0.5

    def fn(q_pre, k_pre, v_pre, kr, c, s1, s2, gq, gk):
        qs, ks = [], []
        for h in range(MLA_HEADS):
            sl = slice(h * LANES, (h + 1) * LANES)
            qs.append(_rope(_rms(q_pre[:, sl], gq, MLA_QK), c, s1, s2) * scale)
            ks.append(_rope(_rms(k_pre[:, sl] + kr, gk, MLA_QK), c, s1, s2))
        return jnp.concatenate(qs, axis=1), jnp.concatenate(ks, axis=1), v_pre

    return _rowwise(fn, "mla_post", 256,
                    [q_pre, (kv_pre, HP, 0), (kv_pre, HP, 1), (z, LANES, Z_KR // LANES), *tabs],
                    [gq, gk], [(HP, BF16)] * 3)


def _mla_post_bwd(q_pre, kv_pre, z, tabs, gq, gk, dq, dk, dv):
    scale = MLA_QK ** -0.5

    def fn(q_pre, k_pre, kr, c, s1, s2, dq, dk, dv, gq, gk):
        lane = lax.broadcasted_iota(jnp.int32, (1, LANES), 1)
        kr_mask = (lane >= KR_LANE) & (lane < KR_LANE + MLA_ROPE)
        dqs, dks = [], []
        dgq = jnp.zeros((1, LANES), F32)
        dgk = jnp.zeros((1, LANES), F32)
        dkr = jnp.zeros(kr.shape, F32)
        for h in range(MLA_HEADS):
            sl = slice(h * LANES, (h + 1) * LANES)
            dqn = _rope_t(dq[:, sl].astype(F32), c, s1, s2) * scale
            dx, dg = _rms_bwd(q_pre[:, sl], gq, dqn, MLA_QK)
            dqs.append(dx)
            dgq = dgq + dg
            dkn = _rope_t(dk[:, sl].astype(F32), c, s1, s2)
            dx, dg = _rms_bwd(k_pre[:, sl] + kr, gk, dkn, MLA_QK)
            dks.append(dx)
            dgk = dgk + dg
            dkr = dkr + dx
        dkr = jnp.where(kr_mask, dkr, 0.0)
        dkv = jnp.concatenate(dks + [dv.astype(F32)], axis=1)
        return jnp.concatenate(dqs, axis=1), dkv, dkr, dgq, dgk

    return _rowwise(fn, "mla_post_bwd", 256,
                    [q_pre, (kv_pre, HP, 0), (z, LANES, Z_KR // LANES), *tabs, dq, dk, dv],
                    [gq, gk], [(HP, BF16), (2 * HP, BF16), (LANES, BF16)],
                    [((1, LANES), F32), ((1, LANES), F32)])


def _pairs(n, lower):
    a, b = [], []
    for o in range(n):
        inner = range(o + 1) if lower else range(o, n)
        for t in inner:
            a.append(o)
            b.append(t)
    return jnp.asarray(np.array(a, np.int32)), jnp.asarray(np.array(b, np.int32))


def _causal_mask(t):
    r = lax.broadcasted_iota(jnp.int32, (t, t), 0)
    c = lax.broadcasted_iota(jnp.int32, (t, t), 1)
    return r, c


def _flash_fwd(q, k, v, tq):
    T = q.shape[0]
    tq = _tile(T, tq)
    n = T // tq
    ii, jj = _pairs(n, True)

    def body(ii_ref, jj_ref, q_ref, k_ref, v_ref, o_ref, lse_ref, m_sc, l_sc, acc_sc):
        p_ = pl.program_id(1)
        i, j = ii_ref[p_], jj_ref[p_]

        @pl.when(j == 0)
        def _():
            m_sc[...] = jnp.full(m_sc.shape, NEG, F32)
            l_sc[...] = jnp.zeros(l_sc.shape, F32)
            acc_sc[...] = jnp.zeros(acc_sc.shape, F32)

        def step(masked):
            s = _dot(q_ref[...], k_ref[...], "nt")
            if masked:
                r, c = _causal_mask(tq)
                s = jnp.where(c <= r, s, NEG)
            m_prev = m_sc[...]
            m_new = jnp.maximum(m_prev, jnp.max(s, axis=1, keepdims=True))
            alpha = jnp.exp(m_prev - m_new)
            p = jnp.exp(s - m_new[:, :1])
            l_sc[...] = alpha * l_sc[...] + jnp.sum(p, axis=1, keepdims=True)
            acc_sc[...] = alpha * acc_sc[...] + _dot(p, v_ref[...])
            m_sc[...] = m_new

        @pl.when(j < i)
        def _():
            step(False)

        @pl.when(j == i)
        def _():
            step(True)
            l = l_sc[...]
            o_ref[...] = (acc_sc[...] / l).astype(o_ref.dtype)
            lse_ref[...] = m_sc[...] + jnp.log(l)

    blk = lambda which: pl.BlockSpec((tq, LANES), which)
    qmap = lambda h, p, ii, jj: (ii[p], h)
    kmap = lambda h, p, ii, jj: (jj[p], h)
    return pl.pallas_call(
        body, name="mla_flash_fwd",
        grid_spec=pltpu.PrefetchScalarGridSpec(
            num_scalar_prefetch=2, grid=(MLA_HEADS, int(ii.shape[0])),
            in_specs=[blk(qmap), blk(kmap), blk(kmap)],
            out_specs=[blk(qmap), blk(qmap)],
            scratch_shapes=[pltpu.VMEM((tq, LANES), F32)] * 3),
        out_shape=[jax.ShapeDtypeStruct((T, HP), BF16), jax.ShapeDtypeStruct((T, HP), F32)],
        compiler_params=_cparams(("parallel", "arbitrary")),
    )(ii, jj, q, k, v)


def _flash_dq(q, k, v, do, lse, delta, tq):
    T = q.shape[0]
    tq = _tile(T, tq)
    n = T // tq
    ii, jj = _pairs(n, True)

    def body(ii_ref, jj_ref, q_ref, k_ref, v_ref, do_ref, lse_ref, dl_ref, dq_ref, acc_sc):
        p_ = pl.program_id(1)
        i, j = ii_ref[p_], jj_ref[p_]

        @pl.when(j == 0)
        def _():
            acc_sc[...] = jnp.zeros(acc_sc.shape, F32)

        def step(masked):
            s = _dot(q_ref[...], k_ref[...], "nt")
            p = jnp.exp(s - lse_ref[...][:, :1])
            if masked:
                r, c = _causal_mask(tq)
                p = jnp.where(c <= r, p, 0.0)
            dp = _dot(do_ref[...], v_ref[...], "nt")
            ds = p * (dp - dl_ref[...][:, :1])
            acc_sc[...] += _dot(ds, k_ref[...])

        @pl.when(j < i)
        def _():
            step(False)

        @pl.when(j == i)
        def _():
            step(True)
            dq_ref[...] = acc_sc[...]

    blk = lambda which: pl.BlockSpec((tq, LANES), which)
    qmap = lambda h, p, ii, jj: (ii[p], h)
    kmap = lambda h, p, ii, jj: (jj[p], h)
    return pl.pallas_call(
        body, name="mla_flash_dq",
        grid_spec=pltpu.PrefetchScalarGridSpec(
            num_scalar_prefetch=2, grid=(MLA_HEADS, int(ii.shape[0])),
            in_specs=[blk(qmap), blk(kmap), blk(kmap), blk(qmap), blk(qmap), blk(qmap)],
            out_specs=blk(qmap),
            scratch_shapes=[pltpu.VMEM((tq, LANES), F32)]),
        out_shape=jax.ShapeDtypeStruct((T, HP), F32),
        compiler_params=_cparams(("parallel", "arbitrary")),
    )(ii, jj, q, k, v, do, lse, delta)


def _flash_dkv(q, k, v, do, lse_row, delta_row, tq):
    T = q.shape[0]
    tq = _tile(T, tq)
    n = T // tq
    jj, ii = _pairs(n, False)

    def body(jj_ref, ii_ref, q_ref, k_ref, v_ref, do_ref, lse_ref, dl_ref, dk_ref, dv_ref, dk_sc, dv_sc):
        p_ = pl.program_id(1)
        j, i = jj_ref[p_], ii_ref[p_]

        @pl.when(i == j)
        def _():
            dk_sc[...] = jnp.zeros(dk_sc.shape, F32)
            dv_sc[...] = jnp.zeros(dv_sc.shape, F32)

        def step(masked):
            st = _dot(k_ref[...], q_ref[...], "nt")
            pt = jnp.exp(st - lse_ref[...])
            if masked:
                r, c = _causal_mask(tq)
                pt = jnp.where(r <= c, pt, 0.0)
            dpt = _dot(v_ref[...], do_ref[...], "nt")
            dst = pt * (dpt - dl_ref[...])
            dv_sc[...] += _dot(pt, do_ref[...])
            dk_sc[...] += _dot(dst, q_ref[...])

        @pl.when(i == j)
        def _():
            step(True)

        @pl.when(i > j)
        def _():
            step(False)

        @pl.when(i == n - 1)
        def _():
            dk_ref[...] = dk_sc[...]
            dv_ref[...] = dv_sc[...]

    blk = lambda which: pl.BlockSpec((tq, LANES), which)
    qmap = lambda h, p, jj, ii: (ii[p], h)
    kmap = lambda h, p, jj, ii: (jj[p], h)
    row = pl.BlockSpec((None, 1, tq), lambda h, p, jj, ii: (h, 0, ii[p]))
    return pl.pallas_call(
        body, name="mla_flash_dkv",
        grid_spec=pltpu.PrefetchScalarGridSpec(
            num_scalar_prefetch=2, grid=(MLA_HEADS, int(ii.shape[0])),
            in_specs=[blk(qmap), blk(kmap), blk(kmap), blk(qmap), row, row],
            out_specs=[blk(kmap), blk(kmap)],
            scratch_shapes=[pltpu.VMEM((tq, LANES), F32)] * 2),
        out_shape=[jax.ShapeDtypeStruct((T, HP), F32)] * 2,
        compiler_params=_cparams(("parallel", "arbitrary")),
    )(jj, ii, q, k, v, do, lse_row, delta_row)


def _mem_fwd(z, km, vm, gq):
    scale = MEM_HEAD_DIM ** -0.5

    def fn(qm, km, vm, gq):
        ys = []
        for h in range(MEM_HEADS):
            sl = slice(h * LANES, (h + 1) * LANES)
            q = _rms(qm[:, sl], gq) * scale
            s = _dot(q, km[:, sl], "nt")
            p = jnp.exp(s - jnp.max(s, axis=1, keepdims=True))
            p = p / jnp.sum(p, axis=1, keepdims=True)
            ys.append(_dot(p, vm[:, sl]))
        return jnp.concatenate(ys, axis=1)

    y, = _rowwise(fn, "mem_fwd", 512, [(z, MEM_WIDTH, Z_QM // MEM_WIDTH)], [km, vm, gq], [(MEM_WIDTH, BF16)])
    return y


def _mem_bwd(z, dy, km, vm, gq):
    scale = MEM_HEAD_DIM ** -0.5

    def fn(qm, dy, km, vm, gq):
        dqs, dks, dvs = [], [], []
        dgq = jnp.zeros((1, LANES), F32)
        for h in range(MEM_HEADS):
            sl = slice(h * LANES, (h + 1) * LANES)
            q = (_rms(qm[:, sl], gq) * scale).astype(BF16)
            dyh = dy[:, sl]
            kh, vh = km[:, sl], vm[:, sl]
            s = _dot(q, kh, "nt")
            p = jnp.exp(s - jnp.max(s, axis=1, keepdims=True))
            p = p / jnp.sum(p, axis=1, keepdims=True)
            dp = _dot(dyh, vh, "nt")
            ds = p * (dp - jnp.sum(p * dp, axis=1, keepdims=True))
            dq = _dot(ds, kh) * scale
            dx, dg = _rms_bwd(qm[:, sl], gq, dq)
            dqs.append(dx)
            dgq = dgq + dg
            st = _dot(kh, q, "nt")
            pt = jnp.exp(st - jnp.max(st, axis=0, keepdims=True))
            pt = pt / jnp.sum(pt, axis=0, keepdims=True)
            dpt = _dot(vh, dyh, "nt")
            dst = pt * (dpt - jnp.sum(pt * dpt, axis=0, keepdims=True))
            dvs.append(_dot(pt, dyh))
            dks.append(_dot(dst, q))
        return jnp.concatenate(dqs, axis=1), jnp.concatenate(dks, axis=1), jnp.concatenate(dvs, axis=1), dgq

    m = km.shape[0]
    return _rowwise(fn, "mem_bwd", 512, [(z, MEM_WIDTH, Z_QM // MEM_WIDTH), dy], [km, vm, gq],
                    [(MEM_WIDTH, BF16)], [((m, MEM_WIDTH), F32), ((m, MEM_WIDTH), F32), ((1, LANES), F32)])


def _local_step(x, mem, positions, loss_target, P, W):
    T = x.shape[0]
    G = {}

    half = MLA_ROPE // 2
    inv = ROPE_BASE ** (-jnp.arange(half, dtype=F32) / half)
    ang = positions.astype(F32)[:, None] * inv
    cos, sin = jnp.cos(ang), jnp.sin(ang)
    one, zero = jnp.ones((T, MLA_NOPE), F32), jnp.zeros((T, half), F32)
    pad = LANES - MLA_QK
    tabs = (jnp.concatenate([one, cos, cos, jnp.ones((T, pad), F32)], axis=1),
            jnp.concatenate([jnp.zeros((T, MLA_NOPE), F32), -sin, zero, jnp.zeros((T, pad), F32)], axis=1),
            jnp.concatenate([jnp.zeros((T, MLA_NOPE), F32), zero, sin, jnp.zeros((T, pad), F32)], axis=1))
    gq_p = jnp.pad(P["mla_q_norm"], ((0, 0), (0, pad)))
    gk_p = jnp.pad(P["mla_k_norm"], ((0, 0), (0, pad)))
    bias_full = jnp.repeat(P["sg_b"].T, SG_GROUP_DIM, axis=1)
    group_ind = jnp.repeat(jnp.eye(SG_GROUPS, dtype=F32), SG_GROUP_DIM, axis=0)

    h1, = _rowwise(lambda x, g: _rms(x, g), "ffn1_norm", 512, [x], [P["ffn1_norm"]], [(D_MODEL, BF16)])
    gu1, a1, o1 = _ffn_fwd(h1, W["ffn1_w_gu"], W["ffn1_w_down"], "ffn1")

    def resid_norm(x, o, g):
        xn = x + 0.5 * o
        return xn, _rms(xn, g)

    x1, hm = _rowwise(resid_norm, "mix_norm", 512, [x, o1], [P["mix_norm"]], [(D_MODEL, F32), (D_MODEL, BF16)])
    z = _mm(hm, W["w_in"], "nn", F32, "w_in", tm=1024, tn=768)

    y_a = _sg_fwd(z, P["sg_ln_g"], P["sg_ln_b"], P["sg_w"], bias_full)

    def c_norm(cq, ckv, gq, gkv):
        return _rms(cq, gq), _rms(ckv, gkv)

    cqn, ckvn = _rowwise(c_norm, "mla_cnorm", 512,
                         [(z, MLA_Q_RANK, Z_CQ // MLA_Q_RANK), (z, MLA_KV_RANK, Z_CKV // MLA_KV_RANK)],
                         [P["mla_cq_norm"], P["mla_ckv_norm"]], [(MLA_Q_RANK, BF16), (MLA_KV_RANK, BF16)])
    q_pre = _mm(cqn, W["mla_w_uq"], "nn", F32, "mla_uq", tm=1024, tn=1024)
    kv_pre = _mm(ckvn, W["mla_w_ukv"], "nn", F32, "mla_ukv", tm=1024, tn=1024)
    q, k, v = _mla_post(q_pre, kv_pre, z, tabs, gq_p, gk_p)
    y_b, lse = _flash_fwd(q, k, v, 512)

    memn, = _rowwise(lambda m, g: _rms(m, g), "mem_norm", 256, [mem], [P["mem_norm"]], [(D_MODEL, BF16)])
    kvm = _mm(memn, W["mem_w_kv"], "nn", F32, "mem_kv")

    def mem_k(kvm, gk):
        ks = [_rms(kvm[:, h * LANES:(h + 1) * LANES], gk) for h in range(MEM_HEADS)]
        return jnp.concatenate(ks, axis=1), kvm[:, MEM_WIDTH:]

    km, vm = _rowwise(mem_k, "mem_knorm", 256, [kvm], [P["mem_k_norm"]], [(MEM_WIDTH, BF16), (MEM_WIDTH, BF16)])
    y_c = _mem_fwd(z, km, vm, P["mem_q_norm"])

    pa = _mm(y_a, W["w_branch_a"], "nn", F32, "branch_a", tm=1024, tn=1024)
    pb = _mm(y_b, W["w_branch_b"], "nn", F32, "branch_b", tm=1024, tn=1024)
    pc = _mm(y_c, W["w_branch_c"], "nn", F32, "branch_c", tm=1024, tn=1024)

    def merge(zg, pa, pb, pc, b):
        g = _sigmoid(zg + b)
        return g[:, :D_MODEL] * pa + g[:, D_MODEL:2 * D_MODEL] * pb + g[:, 2 * D_MODEL:] * pc

    merged, = _rowwise(merge, "merge", 256, [(z, 3 * D_MODEL, 0), pa, pb, pc], [P["b_gate"]], [(D_MODEL, BF16)])
    om = _mm(merged, W["w_out"], "nn", F32, "w_out", tm=1024, tn=1024)

    def resid_norm1(x, o, g):
        xn = x + o
        return xn, _rms(xn, g)

    x2, h2 = _rowwise(resid_norm1, "ffn2_norm", 512, [x1, om], [P["ffn2_norm"]], [(D_MODEL, F32), (D_MODEL, BF16)])
    gu2, a2, o2 = _ffn_fwd(h2, W["ffn2_w_gu"], W["ffn2_w_down"], "ffn2")

    def loss_fn(x2, o2, t):
        e = x2 + 0.5 * o2 - t
        return e * (1.0 / D_MODEL), (e * (0.5 / D_MODEL)).astype(BF16), _rsum(e * e) * (0.5 / D_MODEL)

    dx3, do2, loss_part = _rowwise(loss_fn, "loss", 512, [x2, o2, loss_target], [],
                                   [(D_MODEL, F32), (D_MODEL, BF16)], [((1, D_MODEL), F32)])

    dh2, G["ffn2_w_gu"], G["ffn2_w_down"] = _ffn_bwd(do2, h2, gu2, a2, W["ffn2_w_gu"], W["ffn2_w_down"], "ffn2")

    def norm_bwd(x, dh, dxo, g):
        dx, dg = _rms_bwd(x, g, dh)
        dx = dx + dxo
        return dx, dx, dg

    dx2, dx2b, G["ffn2_norm"] = _rowwise(norm_bwd, "ffn2_norm_bwd", 512, [x2, dh2, dx3], [P["ffn2_norm"]],
                                         [(D_MODEL, F32), (D_MODEL, BF16)], [((1, D_MODEL), F32)])

    G["w_out"] = _mm(merged, dx2b, "tn", F32, "w_out_dw", tm=1024, tn=1024, tk=512)
    dmerged = _mm(dx2b, W["w_out"], "nt", F32, "w_out_dx", tm=1024, tn=1024)

    def merge_bwd(zg, pa, pb, pc, dm, b):
        g = _sigmoid(zg + b)
        ps = jnp.concatenate([pa, pb, pc], axis=1)
        dm3 = jnp.concatenate([dm, dm, dm], axis=1)
        dzg = dm3 * ps * g * (1.0 - g)
        dp = dm3 * g
        return dzg, dp[:, :D_MODEL], dp[:, D_MODEL:2 * D_MODEL], dp[:, 2 * D_MODEL:], _rsum(dzg)

    dzg, dpa, dpb, dpc, G["b_gate"] = _rowwise(
        merge_bwd, "merge_bwd", 256, [(z, 3 * D_MODEL, 0), pa, pb, pc, dmerged], [P["b_gate"]],
        [(3 * D_MODEL, BF16), (D_MODEL, BF16), (D_MODEL, BF16), (D_MODEL, BF16)], [((1, 3 * D_MODEL), F32)])

    G["w_branch_a"] = _mm(y_a, dpa, "tn", F32, "branch_a_dw", tm=512, tn=1024, tk=512)
    G["w_branch_b"] = _mm(y_b, dpb, "tn", F32, "branch_b_dw", tm=1024, tn=1024, tk=512)
    G["w_branch_c"] = _mm(y_c, dpc, "tn", F32, "branch_c_dw", tm=512, tn=1024, tk=512)
    dy_a = _mm(dpa, W["w_branch_a"], "nt", BF16, "branch_a_dx", tm=1024, tn=512)
    dy_b = _mm(dpb, W["w_branch_b"], "nt", BF16, "branch_b_dx", tm=1024, tn=1024)
    dy_c = _mm(dpc, W["w_branch_c"], "nt", BF16, "branch_c_dx", tm=1024, tn=512)

    du_pre, dv_pre, G["sg_w"], dbias_t, G["sg_ln_g"], G["sg_ln_b"] = _sg_bwd(
        z, dy_a, P["sg_ln_g"], P["sg_ln_b"], P["sg_w"], bias_full, group_ind)
    G["sg_b"] = dbias_t.T

    dqm, dkm, dvm, G["mem_q_norm"] = _mem_bwd(z, dy_c, km, vm, P["mem_q_norm"])

    def mem_k_bwd(kvm, dkm, dvm, gk):
        dks = []
        dg = jnp.zeros((1, LANES), F32)
        for h in range(MEM_HEADS):
            sl = slice(h * LANES, (h + 1) * LANES)
            dx, d = _rms_bwd(kvm[:, sl], gk, dkm[:, sl])
            dks.append(dx)
            dg = dg + d
        return jnp.concatenate(dks + [dvm], axis=1), dg

    dkvm, G["mem_k_norm"] = _rowwise(mem_k_bwd, "mem_knorm_bwd", 256, [kvm, dkm, dvm], [P["mem_k_norm"]],
                                     [(2 * MEM_WIDTH, BF16)], [((1, LANES), F32)])
    G["mem_w_kv"] = _mm(memn, dkvm, "tn", F32, "mem_kv_dw")
    dmemn = _mm(dkvm, W["mem_w_kv"], "nt", F32, "mem_kv_dx")
    _, G["mem_norm"] = _rowwise(lambda m, d, g: _rms_bwd(m, g, d), "mem_norm_bwd", 256, [mem, dmemn],
                                [P["mem_norm"]], [(D_MODEL, BF16)], [((1, D_MODEL), F32)])

    def delta_fn(o, do):
        od = o.astype(F32) * do.astype(F32)
        ds = [jnp.broadcast_to(jnp.sum(od[:, h * LANES:(h + 1) * LANES], axis=1, keepdims=True), (od.shape[0], LANES))
              for h in range(MLA_HEADS)]
        return jnp.concatenate(ds, axis=1)

    delta, = _rowwise(delta_fn, "mla_delta", 512, [y_b, dy_b], [], [(HP, F32)])
    rowform = lambda a: a.reshape(T, MLA_HEADS, LANES)[:, :, 0].T.reshape(MLA_HEADS, 1, T)
    dq = _flash_dq(q, k, v, dy_b, lse, delta, 512)
    dk, dv = _flash_dkv(q, k, v, dy_b, rowform(lse), rowform(delta), 512)
    dq_pre, dkv_pre, dkr, dgq, dgk = _mla_post_bwd(q_pre, kv_pre, z, tabs, gq_p, gk_p, dq, dk, dv)
    G["mla_q_norm"], G["mla_k_norm"] = dgq[:, :MLA_QK], dgk[:, :MLA_QK]
    G["mla_w_uq"] = _mm(cqn, dq_pre, "tn", F32, "mla_uq_dw", tm=384, tn=1024, tk=512)
    G["mla_w_ukv"] = _mm(ckvn, dkv_pre, "tn", F32, "mla_ukv_dw", tm=256, tn=1024, tk=512)
    dcqn = _mm(dq_pre, W["mla_w_uq"], "nt", F32, "mla_uq_dx", tm=1024)
    dckvn = _mm(dkv_pre, W["mla_w_ukv"], "nt", F32, "mla_ukv_dx", tm=1024)

    def c_norm_bwd(cq, ckv, dcqn, dckvn, gq, gkv):
        dcq, dgq = _rms_bwd(cq, gq, dcqn)
        dckv, dgkv = _rms_bwd(ckv, gkv, dckvn)
        return dcq, dckv, dgq, dgkv

    dcq, dckv, G["mla_cq_norm"], G["mla_ckv_norm"] = _rowwise(
        c_norm_bwd, "mla_cnorm_bwd", 512,
        [(z, MLA_Q_RANK, Z_CQ // MLA_Q_RANK), (z, MLA_KV_RANK, Z_CKV // MLA_KV_RANK), dcqn, dckvn],
        [P["mla_cq_norm"], P["mla_ckv_norm"]], [(MLA_Q_RANK, BF16), (MLA_KV_RANK, BF16)],
        [((1, MLA_Q_RANK), F32), ((1, MLA_KV_RANK), F32)])

    dz = jnp.concatenate([dzg, du_pre, dv_pre, dqm, dckv, dkr, dcq], axis=1)
    G["w_in"] = _mm(hm, dz, "tn", F32, "w_in_dw", tm=512, tn=768, tk=512)
    dhm = _mm(dz, W["w_in"], "nt", F32, "w_in_dx", tm=1024, tn=1024, tk=2688)

    def norm_bwd_half(x, dh, dxo, g):
        dx, dg = _rms_bwd(x, g, dh)
        dx = dx + dxo
        return dx, (0.5 * dx), dg

    dx1, do1, G["mix_norm"] = _rowwise(norm_bwd_half, "mix_norm_bwd", 512, [x1, dhm, dx2], [P["mix_norm"]],
                                       [(D_MODEL, F32), (D_MODEL, BF16)], [((1, D_MODEL), F32)])
    dh1, G["ffn1_w_gu"], G["ffn1_w_down"] = _ffn_bwd(do1, h1, gu1, a1, W["ffn1_w_gu"], W["ffn1_w_down"], "ffn1")

    def norm_bwd_last(x, dh, dxo, g):
        dx, dg = _rms_bwd(x, g, dh)
        return dx + dxo, dg

    grad_x, G["ffn1_norm"] = _rowwise(norm_bwd_last, "ffn1_norm_bwd", 512, [x, dh1, dx1], [P["ffn1_norm"]],
                                      [(D_MODEL, F32)], [((1, D_MODEL), F32)])
    return loss_part, grad_x, G


SHARDED = ["ffn1_w_gu", "ffn1_w_down", "w_in", "mla_w_uq", "mla_w_ukv", "mem_w_kv",
           "w_branch_a", "w_branch_b", "w_branch_c", "w_out", "ffn2_w_gu", "ffn2_w_down"]
ROW_SHARDED = {"ffn1_w_down", "mem_w_kv", "w_out", "ffn2_w_down"}
SMALL = ["ffn1_norm", "mix_norm", "b_gate", "sg_ln_g", "sg_ln_b", "sg_w", "sg_b", "mla_cq_norm",
         "mla_ckv_norm", "mla_q_norm", "mla_k_norm", "mem_norm", "mem_q_norm", "mem_k_norm", "ffn2_norm"]
ORDER = ["ffn1_norm", "ffn1_w_gu", "ffn1_w_down", "mix_norm", "w_in", "b_gate", "sg_ln_g", "sg_ln_b", "sg_w",
         "sg_b", "mla_cq_norm", "mla_w_uq", "mla_ckv_norm", "mla_w_ukv", "mla_q_norm", "mla_k_norm", "mem_norm",
         "mem_w_kv", "mem_q_norm", "mem_k_norm", "w_branch_a", "w_branch_b", "w_branch_c", "w_out", "ffn2_norm",
         "ffn2_w_gu", "ffn2_w_down"]

_IN_U, _IN_V, _IN_CQ, _IN_CKV, _IN_KR, _IN_QM, _IN_G = 0, 512, 1024, 1408, 1664, 1696, 2208
IN_COLS = 5280


def _full_from_slabs(name, slabs):
    n, r, c = slabs.shape
    if name in ROW_SHARDED:
        return slabs.reshape(n * r, c)
    return slabs.transpose(1, 0, 2).reshape(r, n * c)


def _slabs_from_full(name, full):
    if name in ROW_SHARDED:
        return full.reshape(N_DEV, full.shape[0] // N_DEV, full.shape[1])
    r, c = full.shape
    return full.reshape(r, N_DEV, c // N_DEV).transpose(1, 0, 2)


def _compute_layout(full):
    W = dict(full)
    w = full["w_in"]
    kr = jnp.pad(w[:, _IN_KR:_IN_QM], ((0, 0), (KR_LANE, LANES - KR_LANE - MLA_ROPE)))
    W["w_in"] = jnp.concatenate([w[:, _IN_G:], w[:, _IN_U:_IN_CQ], w[:, _IN_QM:_IN_G], w[:, _IN_CKV:_IN_KR], kr,
                                 w[:, _IN_CQ:_IN_CKV]], axis=1)
    uq = full["mla_w_uq"].reshape(MLA_Q_RANK, MLA_HEADS, MLA_QK)
    W["mla_w_uq"] = jnp.pad(uq, ((0, 0), (0, 0), (0, LANES - MLA_QK))).reshape(MLA_Q_RANK, HP)
    ukv = full["mla_w_ukv"].reshape(MLA_KV_RANK, MLA_HEADS, MLA_NOPE + MLA_V)
    padh = lambda a: jnp.pad(a, ((0, 0), (0, 0), (0, LANES - a.shape[2]))).reshape(MLA_KV_RANK, HP)
    W["mla_w_ukv"] = jnp.concatenate([padh(ukv[:, :, :MLA_NOPE]), padh(ukv[:, :, MLA_NOPE:])], axis=1)
    wb = full["w_branch_b"].reshape(MLA_HEADS, MLA_V, D_MODEL)
    W["w_branch_b"] = jnp.pad(wb, ((0, 0), (0, LANES - MLA_V), (0, 0))).reshape(HP, D_MODEL)
    return W


def _reference_layout(G):
    out = dict(G)
    g = G["w_in"]
    out["w_in"] = jnp.concatenate([
        g[:, Z_U:Z_QM], g[:, Z_CQ:Z_COLS], g[:, Z_CKV:Z_KR], g[:, Z_KR + KR_LANE:Z_KR + KR_LANE + MLA_ROPE],
        g[:, Z_QM:Z_CKV], g[:, Z_G:Z_U]], axis=1)
    out["mla_w_uq"] = G["mla_w_uq"].reshape(MLA_Q_RANK, MLA_HEADS, LANES)[:, :, :MLA_QK].reshape(MLA_Q_RANK, -1)
    gk = G["mla_w_ukv"][:, :HP].reshape(MLA_KV_RANK, MLA_HEADS, LANES)[:, :, :MLA_NOPE]
    gv = G["mla_w_ukv"][:, HP:].reshape(MLA_KV_RANK, MLA_HEADS, LANES)[:, :, :MLA_V]
    out["mla_w_ukv"] = jnp.concatenate([gk, gv], axis=2).reshape(MLA_KV_RANK, -1)
    out["w_branch_b"] = G["w_branch_b"].reshape(MLA_HEADS, LANES, D_MODEL)[:, :MLA_V].reshape(-1, D_MODEL)
    return out


def _pack(parts):
    flat = []
    for a in parts:
        a = a.reshape(-1)
        flat.append(jnp.pad(a, (0, (-a.shape[0]) % LANES)))
    return jnp.concatenate(flat).reshape(-1, LANES)


def _unpack(packed, shapes):
    flat = packed.reshape(-1)
    out, off = [], 0
    for shp in shapes:
        n = int(np.prod(shp))
        out.append(flat[off:off + n].reshape(shp))
        off += n + (-n) % LANES
    return out


MESH = pl.DeviceIdType.MESH
HBM = pl.BlockSpec(memory_space=pltpu.HBM)


def _all_gather(shard):
    rows, lanes = shard.shape

    def body(x_ref, out_ref, send_sems, recv_sems, local_sem):
        x, y, c = lax.axis_index("x"), lax.axis_index("y"), lax.axis_index("c")
        me, sibling = (x, y, c), (x, y, 1 - c)
        chips = [(1 - x, y), (x, 1 - y), (1 - x, 1 - y)]

        def slot(px, py, pc):
            return out_ref.at[4 * px + 2 * py + pc]

        def copy(k, block, to, src=None):
            return pltpu.make_async_remote_copy(
                src_ref=slot(*block) if src is None else src, dst_ref=slot(*block),
                send_sem=send_sems.at[k], recv_sem=recv_sems.at[k], device_id=to, device_id_type=MESH)

        mine = pltpu.make_async_copy(x_ref, slot(*me), local_sem)
        mine.start()
        first = [copy(0, me, sibling, src=x_ref)]
        first += [copy(1 + j, me, (*chip, c), src=x_ref) for j, chip in enumerate(chips)]
        for cp in first:
            cp.start()
        passed = [copy(4 + j, (*chip, c), sibling) for j, chip in enumerate(chips)]
        for j, chip in enumerate(chips):
            copy(1 + j, (*chip, c), me).wait_recv()
            passed[j].start()
        copy(0, sibling, me).wait_recv()
        for j, chip in enumerate(chips):
            copy(4 + j, (*chip, 1 - c), me).wait_recv()
        for cp in first + passed:
            cp.wait_send()
        mine.wait()

    return pl.pallas_call(
        body, name="all_gather_weights",
        out_shape=jax.ShapeDtypeStruct((N_DEV, rows, lanes), shard.dtype),
        in_specs=[HBM], out_specs=HBM,
        scratch_shapes=[pltpu.SemaphoreType.DMA((7,)), pltpu.SemaphoreType.DMA((7,)), pltpu.SemaphoreType.DMA],
    )(shard)


def _exchange(big, small):
    def body(big_ref, small_ref, bout_ref, sout_ref, send_sems, recv_sems, local_sems):
        x, y, c = lax.axis_index("x"), lax.axis_index("y"), lax.axis_index("c")
        me = 4 * x + 2 * y + c
        own = [pltpu.make_async_copy(big_ref.at[me], bout_ref.at[me], local_sems.at[0]),
               pltpu.make_async_copy(small_ref, sout_ref.at[me], local_sems.at[1])]
        for cp in own:
            cp.start()
        copies = []
        for k in range(1, N_DEV):
            px = 1 - x if k & 4 else x
            py = 1 - y if k & 2 else y
            pc = 1 - c if k & 1 else c
            peer = 4 * px + 2 * py + pc
            copies.append(pltpu.make_async_remote_copy(
                src_ref=big_ref.at[peer], dst_ref=bout_ref.at[me], send_sem=send_sems.at[k - 1],
                recv_sem=recv_sems.at[k - 1], device_id=(px, py, pc), device_id_type=MESH))
            copies.append(pltpu.make_async_remote_copy(
                src_ref=small_ref, dst_ref=sout_ref.at[me], send_sem=send_sems.at[7 + k - 1],
                recv_sem=recv_sems.at[7 + k - 1], device_id=(px, py, pc), device_id_type=MESH))
        for cp in copies:
            cp.start()
        for cp in copies:
            cp.wait()
        for cp in own:
            cp.wait()

    return pl.pallas_call(
        body, name="exchange_grads",
        out_shape=[jax.ShapeDtypeStruct(big.shape, big.dtype),
                   jax.ShapeDtypeStruct((N_DEV,) + small.shape, small.dtype)],
        in_specs=[HBM, HBM], out_specs=[HBM, HBM],
        scratch_shapes=[pltpu.SemaphoreType.DMA((14,)), pltpu.SemaphoreType.DMA((14,)),
                        pltpu.SemaphoreType.DMA((2,))],
    )(big, small)


def _sum_slots(recv, name, tr):
    n, rows, lanes = recv.shape
    tr = _tile(rows, tr)

    def body(r_ref, o_ref):
        acc = r_ref[0].astype(F32)
        for i in range(1, n):
            acc = acc + r_ref[i].astype(F32)
        o_ref[...] = acc

    return pl.pallas_call(
        body, name=name, grid=(rows // tr,),
        in_specs=[pl.BlockSpec((n, tr, lanes), lambda i: (0, i, 0))],
        out_specs=pl.BlockSpec((tr, lanes), lambda i: (i, 0)),
        out_shape=jax.ShapeDtypeStruct((rows, lanes), F32),
        compiler_params=_cparams(("parallel",)),
    )(recv)


def _adamw(w, g, m, v, name, tr=256):
    c1 = 1.0 - ADAM_B1 ** ADAM_STEP
    c2 = 1.0 - ADAM_B2 ** ADAM_STEP

    def fn(w, g, m, v):
        m = ADAM_B1 * m + (1.0 - ADAM_B1) * g
        v = ADAM_B2 * v + (1.0 - ADAM_B2) * (g * g)
        delta = -ADAM_LR * ((m / c1) / (jnp.sqrt(v / c2) + ADAM_EPS) + ADAM_WD * w)
        return delta, m, v

    return _rowwise(fn, name, tr, [w, g, m, v], [], [(w.shape[1], F32)] * 3)


def kernel(x, mem, positions, ffn1_norm, ffn1_w_gu, ffn1_w_down, mix_norm, w_in, b_gate, sg_ln_g, sg_ln_b, sg_w, sg_b, mla_cq_norm, mla_w_uq, mla_ckv_norm, mla_w_ukv, mla_q_norm, mla_k_norm, mem_norm, mem_w_kv, mem_q_norm, mem_k_norm, w_branch_a, w_branch_b, w_branch_c, w_out, ffn2_norm, ffn2_w_gu, ffn2_w_down, loss_target, m_ffn1_norm, m_ffn1_w_gu, m_ffn1_w_down, m_mix_norm, m_w_in, m_b_gate, m_sg_ln_g, m_sg_ln_b, m_sg_w, m_sg_b, m_mla_cq_norm, m_mla_w_uq, m_mla_ckv_norm, m_mla_w_ukv, m_mla_q_norm, m_mla_k_norm, m_mem_norm, m_mem_w_kv, m_mem_q_norm, m_mem_k_norm, m_w_branch_a, m_w_branch_b, m_w_branch_c, m_w_out, m_ffn2_norm, m_ffn2_w_gu, m_ffn2_w_down, v_ffn1_norm, v_ffn1_w_gu, v_ffn1_w_down, v_mix_norm, v_w_in, v_b_gate, v_sg_ln_g, v_sg_ln_b, v_sg_w, v_sg_b, v_mla_cq_norm, v_mla_w_uq, v_mla_ckv_norm, v_mla_w_ukv, v_mla_q_norm, v_mla_k_norm, v_mem_norm, v_mem_w_kv, v_mem_q_norm, v_mem_k_norm, v_w_branch_a, v_w_branch_b, v_w_branch_c, v_w_out, v_ffn2_norm, v_ffn2_w_gu, v_ffn2_w_down):
    given = dict(locals())
    wts = {n: given[n] for n in ORDER}
    mom = {n: given["m_" + n] for n in ORDER}
    var = {n: given["v_" + n] for n in ORDER}
    shard_shapes = {n: wts[n].shape[1:] for n in SHARDED}

    gathered = _all_gather(_pack([wts[n][0].astype(BF16) for n in SHARDED]))
    per_dev = [_unpack(gathered[d], [shard_shapes[n] for n in SHARDED]) for d in range(N_DEV)]
    full = {n: _full_from_slabs(n, jnp.stack([per_dev[d][i] for d in range(N_DEV)]))
            for i, n in enumerate(SHARDED)}
    W = _compute_layout(full)
    P = {n: wts[n] if wts[n].ndim == 2 else wts[n][0] for n in SMALL}

    loss_part, grad_x, G = _local_step(x[0], mem[0], positions[0], loss_target[0], P, W)
    G = _reference_layout(G)

    slab_sets = [_slabs_from_full(n, G[n]) for n in SHARDED]
    big = jnp.stack([_pack([s[d].astype(BF16) for s in slab_sets]) for d in range(N_DEV)])
    small_shapes = [wts[n].shape[1:] for n in SMALL]
    small = _pack([G[n].reshape(s) for n, s in zip(SMALL, small_shapes)])
    small = jnp.pad(small, ((0, (-small.shape[0]) % 8), (0, 0)))
    big_recv, small_recv = _exchange(big, small)
    g_big = _unpack(_sum_slots(big_recv, "sum_big", 1024), [shard_shapes[n] for n in SHARDED])
    g_small_packed = _sum_slots(small_recv, "sum_small", 2048)
    g_small = _unpack(g_small_packed, small_shapes)
    grads = dict(zip(SHARDED, g_big))
    grads.update(zip(SMALL, g_small))

    delta, new_m, new_v = {}, {}, {}
    for n in SHARDED:
        delta[n], new_m[n], new_v[n] = _adamw(wts[n][0], grads[n], mom[n][0], var[n][0], "adamw_" + n)

    def pack_small(d):
        p = _pack([d[n].reshape(s) for n, s in zip(SMALL, small_shapes)])
        return jnp.pad(p, ((0, (-p.shape[0]) % 8), (0, 0)))

    ds, ms, vs = _adamw(pack_small(wts), g_small_packed, pack_small(mom), pack_small(var), "adamw_small", tr=2048)
    for dst, packed in ((delta, ds), (new_m, ms), (new_v, vs)):
        dst.update(zip(SMALL, _unpack(packed, small_shapes)))

    loss = lax.psum(jnp.sum(loss_part), ("x", "y", "c"))
    lead = lambda d: [d[n].reshape(wts[n].shape) for n in ORDER]
    return (loss, grad_x[None], *lead(grads), *lead(delta), *lead(new_m), *lead(new_v))
```

```python
import functools

import numpy as np
import jax
import jax.numpy as jnp
from jax import lax
from jax.experimental import pallas as pl
from jax.experimental.pallas import tpu as pltpu

F32, BF16 = jnp.float32, jnp.bfloat16

D_MODEL = 1024
SG_GROUPS, SG_GROUP_DIM, SG_WIDTH, CHUNK = 8, 64, 512, 128
MLA_HEADS, MLA_NOPE, MLA_ROPE, MLA_V, MLA_QK = 8, 64, 32, 64, 96
MLA_Q_RANK, MLA_KV_RANK = 384, 256
MEM_HEADS, MEM_HEAD_DIM, MEM_WIDTH = 4, 128, 512
D_FF = 2816
ROPE_BASE = 10000.0
EPS = 1e-6
NEG = -1e30
ADAM_LR, ADAM_B1, ADAM_B2, ADAM_EPS, ADAM_WD, ADAM_STEP = 0.001, 0.9, 0.999, 1e-08, 0.01, 10

N_DEV = 8
LANES = 128
V7X_VMEM_LIMIT = 56 * 1024 * 1024
HP = MLA_HEADS * LANES

Z_G, Z_U, Z_V, Z_QM, Z_CKV, Z_KR, Z_CQ = 0, 3072, 3584, 4096, 4608, 4864, 4992
Z_COLS = 5376
KR_LANE = 64


def _tile(dim, pref):
    if dim <= pref:
        return dim
    for t in range(pref - pref % LANES, LANES - 1, -LANES):
        if dim % t == 0:
            return t
    for t in range(pref - pref % 8, 7, -8):
        if dim % t == 0:
            return t
    return dim


def _cparams(sem):
    return pltpu.CompilerParams(dimension_semantics=sem, vmem_limit_bytes=V7X_VMEM_LIMIT)


_DN = {"nn": ((1,), (0,)), "nt": ((1,), (1,)), "tn": ((0,), (0,))}


def _dot(a, b, mode="nn"):
    return lax.dot_general(a.astype(BF16), b.astype(BF16), (_DN[mode], ((), ())),
                           preferred_element_type=F32)


def _mm(a, b, mode, out_dtype, name, tm=512, tn=512, tk=2048):
    if mode == "tn":
        K, M = a.shape
    else:
        M, K = a.shape
    N = b.shape[0] if mode == "nt" else b.shape[1]
    tm, tn, tk = _tile(M, tm), _tile(N, tn), _tile(K, tk)
    nk = K // tk
    if mode == "tn":
        a_spec = pl.BlockSpec((tk, tm), lambda i, j, k: (k, i))
    else:
        a_spec = pl.BlockSpec((tm, tk), lambda i, j, k: (i, k))
    if mode == "nt":
        b_spec = pl.BlockSpec((tn, tk), lambda i, j, k: (j, k))
    else:
        b_spec = pl.BlockSpec((tk, tn), lambda i, j, k: (k, j))

    def body(a_ref, b_ref, o_ref, *scratch):
        p = _dot(a_ref[...], b_ref[...], mode)
        if nk == 1:
            o_ref[...] = p.astype(o_ref.dtype)
        else:
            acc_ref, = scratch
            k = pl.program_id(2)

            @pl.when(k == 0)
            def _():
                acc_ref[...] = p

            @pl.when(k > 0)
            def _():
                acc_ref[...] += p

            @pl.when(k == nk - 1)
            def _():
                o_ref[...] = acc_ref[...].astype(o_ref.dtype)

    return pl.pallas_call(
        body, name=name, grid=(M // tm, N // tn, nk),
        in_specs=[a_spec, b_spec],
        out_specs=pl.BlockSpec((tm, tn), lambda i, j, k: (i, j)),
        out_shape=jax.ShapeDtypeStruct((M, N), out_dtype),
        scratch_shapes=[] if nk == 1 else [pltpu.VMEM((tm, tn), F32)],
        compiler_params=_cparams(("parallel", "parallel", "arbitrary")),
    )(a, b)


def _mm_t(a, b, name, tm, tn, tk=1024):
    return _mm(a.T, b, "nn", F32, name, tm=tm, tn=tn, tk=tk)


def _rowwise(fn, name, tr, row_ins, bc_ins, row_outs, acc_outs=()):
    norm = [it if isinstance(it, tuple) else (it, it.shape[1], 0) for it in row_ins]
    rows = norm[0][0].shape[0]
    tr = _tile(rows, tr)
    arrays, in_specs = [], []
    for arr, w, cb in norm:
        arrays.append(arr)
        in_specs.append(pl.BlockSpec((tr, w), lambda i, cb=cb: (i, cb)))
    for arr in bc_ins:
        arrays.append(arr)
        in_specs.append(pl.BlockSpec(arr.shape, lambda i, nd=arr.ndim: (0,) * nd))
    out_shape, out_specs = [], []
    for w, dt in row_outs:
        out_shape.append(jax.ShapeDtypeStruct((rows, w), dt))
        out_specs.append(pl.BlockSpec((tr, w), lambda i: (i, 0)))
    for shp, dt in acc_outs:
        out_shape.append(jax.ShapeDtypeStruct(shp, dt))
        out_specs.append(pl.BlockSpec(shp, lambda i, nd=len(shp): (0,) * nd))
    n_in, n_row = len(arrays), len(row_outs)

    def body(*refs):
        vals = fn(*[r[...] for r in refs[:n_in]])
        if not isinstance(vals, (tuple, list)):
            vals = (vals,)
        outs = refs[n_in:]
        for r, v in zip(outs[:n_row], vals[:n_row]):
            r[...] = v.astype(r.dtype)
        if acc_outs:
            accs = list(zip(outs[n_row:], vals[n_row:]))
            i = pl.program_id(0)

            @pl.when(i == 0)
            def _():
                for r, v in accs:
                    r[...] = v.astype(r.dtype)

            @pl.when(i > 0)
            def _():
                for r, v in accs:
                    r[...] += v.astype(r.dtype)

    res = pl.pallas_call(
        body, name=name, grid=(rows // tr,), in_specs=in_specs, out_specs=out_specs,
        out_shape=out_shape, compiler_params=_cparams(("arbitrary",)),
    )(*arrays)
    return res


def _rsum(x):
    return jnp.sum(x, axis=0, keepdims=True)


def _rms(x, g, n=None):
    n = x.shape[-1] if n is None else n
    r = lax.rsqrt(jnp.sum(x * x, axis=-1, keepdims=True) * (1.0 / n) + EPS)
    return x * r * g


def _rms_bwd(x, g, dy, n=None):
    n = x.shape[-1] if n is None else n
    r = lax.rsqrt(jnp.sum(x * x, axis=-1, keepdims=True) * (1.0 / n) + EPS)
    xh = x * r
    dxh = dy * g
    dx = r * (dxh - xh * (jnp.sum(dxh * xh, axis=-1, keepdims=True) * (1.0 / n)))
    return dx, _rsum(dy * xh)


def _gelu(x):
    return 0.5 * x * (1.0 + lax.erf(x * 0.7071067811865476))


def _gelu_grad(x):
    return 0.5 * (1.0 + lax.erf(x * 0.7071067811865476)) + x * jnp.exp(-0.5 * x * x) * 0.3989422804014327


def _sigmoid(x):
    return 1.0 / (1.0 + jnp.exp(-x))


def _ffn_fwd(h, w_gu, w_down, tag):
    gu = _mm(h, w_gu, "nn", BF16, f"{tag}_gu", tm=1024, tn=512)

    def act(gu):
        g = gu[:, :D_FF].astype(F32)
        u = gu[:, D_FF:].astype(F32)
        return g * _sigmoid(g) * u

    a, = _rowwise(act, f"{tag}_act", 256, [gu], [], [(D_FF, BF16)])
    o = _mm(a, w_down, "nn", F32, f"{tag}_down", tm=1024, tn=512, tk=2816)
    return gu, a, o


def _ffn_bwd(do, h, gu, a, w_gu, w_down, tag):
    dw_down = _mm_t(a, do, f"{tag}_dwdown", tm=1408, tn=1024)
    da = _mm(do, w_down, "nt", BF16, f"{tag}_da", tm=1024, tn=512)

    def act_bwd(gu, da):
        g = gu[:, :D_FF].astype(F32)
        u = gu[:, D_FF:].astype(F32)
        da = da.astype(F32)
        s = _sigmoid(g)
        dg = da * u * s * (1.0 + g * (1.0 - s))
        du = da * g * s
        return jnp.concatenate([dg, du], axis=1)

    dgu, = _rowwise(act_bwd, f"{tag}_actbwd", 256, [gu, da], [], [(2 * D_FF, BF16)])
    dw_gu = _mm_t(h, dgu, f"{tag}_dwgu", tm=1024, tn=1408)
    dh = _mm(dgu, w_gu, "nt", F32, f"{tag}_dh", tm=1024, tn=512, tk=2816)
    return dh, dw_gu, dw_down


def _sg_common(u_pre, v_pre, ln_g, ln_b):
    u = _gelu(u_pre)
    v = _gelu(v_pre)
    mu = jnp.mean(v, axis=-1, keepdims=True)
    vc = v - mu
    rstd = lax.rsqrt(jnp.mean(vc * vc, axis=-1, keepdims=True) + EPS)
    vhat = vc * rstd
    vl = vhat * ln_g + ln_b
    return u, vhat, rstd, vl


def _sg_masked_pairs(w):
    t = lax.broadcasted_iota(jnp.int32, (CHUNK, CHUNK), 0)
    s = lax.broadcasted_iota(jnp.int32, (CHUNK, CHUNK), 1)
    causal = s <= t
    wm = [jnp.where(causal, w[g], 0.0).astype(BF16) for g in range(SG_GROUPS)]
    return [jnp.concatenate([wm[2 * j], wm[2 * j + 1]], axis=0) for j in range(SG_GROUPS // 2)], causal


def _sg_mix(vl, pairs, bias):
    tr = vl.shape[0]
    low = lax.broadcasted_iota(jnp.int32, (CHUNK, LANES), 1) < SG_GROUP_DIM
    vb = vl.astype(BF16)
    rows = []
    for c in range(tr // CHUNK):
        slabs = []
        for j in range(SG_GROUPS // 2):
            slab = vb[c * CHUNK:(c + 1) * CHUNK, j * LANES:(j + 1) * LANES]
            m = _dot(pairs[j], slab)
            slabs.append(jnp.where(low, m[:CHUNK], m[CHUNK:]))
        rows.append(jnp.concatenate(slabs, axis=1) + bias)
    return jnp.concatenate(rows, axis=0)


def _sg_fwd(z, ln_g, ln_b, sg_w, bias_full):
    def fn(u_pre, v_pre, ln_g, ln_b, w, bias):
        u, _, _, vl = _sg_common(u_pre, v_pre, ln_g, ln_b)
        pairs, _ = _sg_masked_pairs(w)
        return u * _sg_mix(vl, pairs, bias)

    y, = _rowwise(fn, "sg_fwd", 512, [(z, SG_WIDTH, Z_U // SG_WIDTH), (z, SG_WIDTH, Z_V // SG_WIDTH)],
                  [ln_g, ln_b, sg_w, bias_full], [(SG_WIDTH, BF16)])
    return y


def _sg_bwd(z, dy, ln_g, ln_b, sg_w, bias_full, group_ind):
    def fn(u_pre, v_pre, dy, ln_g, ln_b, w, bias, ind):
        dy = dy.astype(F32)
        u, vhat, rstd, vl = _sg_common(u_pre, v_pre, ln_g, ln_b)
        pairs, causal = _sg_masked_pairs(w)
        mixed = _sg_mix(vl, pairs, bias)
        du_pre = dy * mixed * _gelu_grad(u_pre)
        dmix = dy * u
        tr = dy.shape[0]
        low = lax.broadcasted_iota(jnp.int32, (CHUNK, LANES), 1) < SG_GROUP_DIM
        vb = vl.astype(BF16)
        dw = [jnp.zeros((CHUNK, CHUNK), F32) for _ in range(SG_GROUPS)]
        dbias = jnp.zeros((CHUNK, SG_WIDTH), F32)
        dvl_rows = []
        for c in range(tr // CHUNK):
            dm_c = dmix[c * CHUNK:(c + 1) * CHUNK]
            dbias = dbias + dm_c
            slabs = []
            for j in range(SG_GROUPS // 2):
                slab = vb[c * CHUNK:(c + 1) * CHUNK, j * LANES:(j + 1) * LANES]
                dm = dm_c[:, j * LANES:(j + 1) * LANES]
                d0 = jnp.where(low, dm, 0.0).astype(BF16)
                d1 = jnp.where(low, 0.0, dm).astype(BF16)
                dw[2 * j] = dw[2 * j] + _dot(d0, slab, "nt")
                dw[2 * j + 1] = dw[2 * j + 1] + _dot(d1, slab, "nt")
                slabs.append(_dot(pairs[j], jnp.concatenate([d0, d1], axis=0), "tn"))
            dvl_rows.append(jnp.concatenate(slabs, axis=1))
        dvl = jnp.concatenate(dvl_rows, axis=0)
        dln_g = _rsum(dvl * vhat)
        dln_b = _rsum(dvl)
        dvh = dvl * ln_g
        dv = rstd * (dvh - jnp.mean(dvh, axis=-1, keepdims=True)
                     - vhat * jnp.mean(dvh * vhat, axis=-1, keepdims=True))
        dv_pre = dv * _gelu_grad(v_pre)
        dw = jnp.stack([jnp.where(causal, d, 0.0) for d in dw], axis=0)
        dbias_t = lax.dot_general(dbias, ind, (((1,), (0,)), ((), ())), precision=lax.Precision.HIGHEST,
                                  preferred_element_type=F32)
        return du_pre, dv_pre, dw, dbias_t, dln_g, dln_b

    return _rowwise(fn, "sg_bwd", 512,
                    [(z, SG_WIDTH, Z_U // SG_WIDTH), (z, SG_WIDTH, Z_V // SG_WIDTH), dy],
                    [ln_g, ln_b, sg_w, bias_full, group_ind],
                    [(SG_WIDTH, BF16), (SG_WIDTH, BF16)],
                    [((SG_GROUPS, CHUNK, CHUNK), F32), ((CHUNK, SG_GROUPS), F32), ((1, SG_WIDTH), F32), ((1, SG_WIDTH), F32)])


def _rope(x, c, s1, s2):
    return x * c + pltpu.roll(x, LANES - MLA_ROPE // 2, 1) * s1 + pltpu.roll(x, MLA_ROPE // 2, 1) * s2


def _rope_t(d, c, s1, s2):
    return d * c + pltpu.roll(d * s1, MLA_ROPE // 2, 1) + pltpu.roll(d * s2, LANES - MLA_ROPE // 2, 1)


def _mla_post(q_pre, kv_pre, z, tabs, gq, gk):
    scale = MLA_QK ** -0.5 * LOG2E

    def fn(q_pre, k_pre, v_pre, kr, c, s1, s2, gq, gk):
        qs, ks = [], []
        for h in range(MLA_HEADS):
            sl = slice(h * LANES, (h + 1) * LANES)
            qs.append(_rope(_rms(q_pre[:, sl], gq, MLA_QK), c, s1, s2) * scale)
            ks.append(_rope(_rms(k_pre[:, sl] + kr, gk, MLA_QK), c, s1, s2))
        return jnp.concatenate(qs, axis=1), jnp.concatenate(ks, axis=1), v_pre

    return _rowwise(fn, "mla_post", 256,
                    [q_pre, (kv_pre, HP, 0), (kv_pre, HP, 1), (z, LANES, Z_KR // LANES), *tabs],
                    [gq, gk], [(HP, BF16)] * 3)


def _mla_post_bwd(q_pre, kv_pre, z, tabs, gq, gk, dq, dk, dv):
    scale = MLA_QK ** -0.5

    def fn(q_pre, k_pre, kr, c, s1, s2, dq, dk, dv, gq, gk):
        lane = lax.broadcasted_iota(jnp.int32, (1, LANES), 1)
        kr_mask = (lane >= KR_LANE) & (lane < KR_LANE + MLA_ROPE)
        dqs, dks = [], []
        dgq = jnp.zeros((1, LANES), F32)
        dgk = jnp.zeros((1, LANES), F32)
        dkr = jnp.zeros(kr.shape, F32)
        for h in range(MLA_HEADS):
            sl = slice(h * LANES, (h + 1) * LANES)
            dqn = _rope_t(dq[:, sl].astype(F32), c, s1, s2) * scale
            dx, dg = _rms_bwd(q_pre[:, sl], gq, dqn, MLA_QK)
            dqs.append(dx)
            dgq = dgq + dg
            dkn = _rope_t(dk[:, sl].astype(F32), c, s1, s2)
            dx, dg = _rms_bwd(k_pre[:, sl] + kr, gk, dkn, MLA_QK)
            dks.append(dx)
            dgk = dgk + dg
            dkr = dkr + dx
        dkr = jnp.where(kr_mask, dkr, 0.0)
        dkv = jnp.concatenate(dks + [dv.astype(F32)], axis=1)
        return jnp.concatenate(dqs, axis=1), dkv, dkr, dgq, dgk

    return _rowwise(fn, "mla_post_bwd", 256,
                    [q_pre, (kv_pre, HP, 0), (z, LANES, Z_KR // LANES), *tabs, dq, dk, dv],
                    [gq, gk], [(HP, BF16), (2 * HP, BF16), (LANES, BF16)],
                    [((1, LANES), F32), ((1, LANES), F32)])


def _pairs(n, lower):
    a, b = [], []
    for o in range(n):
        inner = range(o + 1) if lower else range(o, n)
        for t in inner:
            a.append(o)
            b.append(t)
    return jnp.asarray(np.array(a, np.int32)), jnp.asarray(np.array(b, np.int32))


FLASH_TILE, FLASH_SUB_ROWS, FLASH_SUB_COLS = 1024, 512, 256
LOG2E, LN2 = 1.4426950408889634, 0.6931471805599453


def _flash_tiles(T):
    tq = _tile(T, FLASH_TILE)
    return tq, _tile(tq, FLASH_SUB_ROWS), _tile(tq, FLASH_SUB_COLS)


def _sub_blocks(t, sr, sc, diag, key_major):
    out = []
    for rb in range(t // sr):
        for cb in range(t // sc):
            r0, r1, c0, c1 = rb * sr, rb * sr + sr - 1, cb * sc, cb * sc + sc - 1
            if not diag:
                out.append((rb, cb, False))
            elif key_major:
                if r0 <= c1:
                    out.append((rb, cb, r1 > c0))
            elif c0 <= r1:
                out.append((rb, cb, c1 > r0))
    return out


def _sub_iota(sr, sc, rb, cb):
    r = lax.broadcasted_iota(jnp.int32, (sr, sc), 0) + rb * sr
    c = lax.broadcasted_iota(jnp.int32, (sr, sc), 1) + cb * sc
    return r, c


def _flash_fwd(q, k, v):
    T = q.shape[0]
    tq, sr, sc = _flash_tiles(T)
    n = T // tq
    ii, jj = _pairs(n, True)

    def body(ii_ref, jj_ref, q_ref, k_ref, v_ref, o_ref, lse_ref, m_sc, l_sc, acc_sc):
        p_ = pl.program_id(1)
        i, j = ii_ref[p_], jj_ref[p_]

        @pl.when(j == 0)
        def _():
            m_sc[...] = jnp.full(m_sc.shape, NEG, F32)
            l_sc[...] = jnp.zeros(l_sc.shape, F32)
            acc_sc[...] = jnp.zeros(acc_sc.shape, F32)

        def tile(diag):
            blocks = _sub_blocks(tq, sr, sc, diag, False)
            for rb in range(tq // sr):
                rows = slice(rb * sr, (rb + 1) * sr)
                qs = q_ref[rows, :]
                m = m_sc[rows, :][:, :1]
                l = l_sc[rows, :][:, :1]
                acc = acc_sc[rows, :]
                for rb2, cb, masked in blocks:
                    if rb2 != rb:
                        continue
                    cols = slice(cb * sc, (cb + 1) * sc)
                    s = _dot(qs, k_ref[cols, :], "nt")
                    if masked:
                        r, c = _sub_iota(sr, sc, rb, cb)
                        s = jnp.where(c <= r, s, NEG)
                    m_new = jnp.maximum(m, jnp.max(s, axis=1, keepdims=True))
                    alpha = jnp.exp2(m - m_new)
                    p = jnp.exp2(s - m_new)
                    l = alpha * l + jnp.sum(p, axis=1, keepdims=True)
                    acc = alpha * acc + _dot(p, v_ref[cols, :])
                    m = m_new
                m_sc[rows, :] = jnp.broadcast_to(m, (sr, LANES))
                l_sc[rows, :] = jnp.broadcast_to(l, (sr, LANES))
                acc_sc[rows, :] = acc

        @pl.when(j < i)
        def _():
            tile(False)

        @pl.when(j == i)
        def _():
            tile(True)
            l = l_sc[...]
            o_ref[...] = (acc_sc[...] / l).astype(o_ref.dtype)
            lse_ref[...] = m_sc[...] + jnp.log2(l)

    blk = lambda which: pl.BlockSpec((tq, LANES), which)
    qmap = lambda h, p, ii, jj: (ii[p], h)
    kmap = lambda h, p, ii, jj: (jj[p], h)
    return pl.pallas_call(
        body, name="mla_flash_fwd",
        grid_spec=pltpu.PrefetchScalarGridSpec(
            num_scalar_prefetch=2, grid=(MLA_HEADS, int(ii.shape[0])),
            in_specs=[blk(qmap), blk(kmap), blk(kmap)],
            out_specs=[blk(qmap), blk(qmap)],
            scratch_shapes=[pltpu.VMEM((tq, LANES), F32)] * 3),
        out_shape=[jax.ShapeDtypeStruct((T, HP), BF16), jax.ShapeDtypeStruct((T, HP), F32)],
        compiler_params=_cparams(("parallel", "arbitrary")),
    )(ii, jj, q, k, v)


def _flash_dq(q, k, v, do, lse, delta):
    T = q.shape[0]
    tq, sr, sc = _flash_tiles(T)
    n = T // tq
    ii, jj = _pairs(n, True)

    def body(ii_ref, jj_ref, q_ref, k_ref, v_ref, do_ref, lse_ref, dl_ref, dq_ref, acc_sc):
        p_ = pl.program_id(1)
        i, j = ii_ref[p_], jj_ref[p_]

        @pl.when(j == 0)
        def _():
            acc_sc[...] = jnp.zeros(acc_sc.shape, F32)

        def tile(diag):
            blocks = _sub_blocks(tq, sr, sc, diag, False)
            for rb in range(tq // sr):
                rows = slice(rb * sr, (rb + 1) * sr)
                qs, dos = q_ref[rows, :], do_ref[rows, :]
                lse = lse_ref[rows, :][:, :1]
                dl = dl_ref[rows, :][:, :1]
                acc = acc_sc[rows, :]
                for rb2, cb, masked in blocks:
                    if rb2 != rb:
                        continue
                    cols = slice(cb * sc, (cb + 1) * sc)
                    ks = k_ref[cols, :]
                    p = jnp.exp2(_dot(qs, ks, "nt") - lse)
                    if masked:
                        r, c = _sub_iota(sr, sc, rb, cb)
                        p = jnp.where(c <= r, p, 0.0)
                    dp = _dot(dos, v_ref[cols, :], "nt")
                    acc = acc + _dot(p * (dp - dl), ks)
                acc_sc[rows, :] = acc

        @pl.when(j < i)
        def _():
            tile(False)

        @pl.when(j == i)
        def _():
            tile(True)
            dq_ref[...] = acc_sc[...]

    blk = lambda which: pl.BlockSpec((tq, LANES), which)
    qmap = lambda h, p, ii, jj: (ii[p], h)
    kmap = lambda h, p, ii, jj: (jj[p], h)
    return pl.pallas_call(
        body, name="mla_flash_dq",
        grid_spec=pltpu.PrefetchScalarGridSpec(
            num_scalar_prefetch=2, grid=(MLA_HEADS, int(ii.shape[0])),
            in_specs=[blk(qmap), blk(kmap), blk(kmap), blk(qmap), blk(qmap), blk(qmap)],
            out_specs=blk(qmap),
            scratch_shapes=[pltpu.VMEM((tq, LANES), F32)]),
        out_shape=jax.ShapeDtypeStruct((T, HP), F32),
        compiler_params=_cparams(("parallel", "arbitrary")),
    )(ii, jj, q, k, v, do, lse, delta)


def _flash_dkv(q, k, v, do, lse_row, delta_row):
    T = q.shape[0]
    tq, sr, sc = _flash_tiles(T)
    n = T // tq
    jj, ii = _pairs(n, False)

    def body(jj_ref, ii_ref, q_ref, k_ref, v_ref, do_ref, lse_ref, dl_ref, dk_ref, dv_ref, dk_sc, dv_sc):
        p_ = pl.program_id(1)
        j, i = jj_ref[p_], ii_ref[p_]

        @pl.when(i == j)
        def _():
            dk_sc[...] = jnp.zeros(dk_sc.shape, F32)
            dv_sc[...] = jnp.zeros(dv_sc.shape, F32)

        def tile(diag):
            blocks = _sub_blocks(tq, sr, sc, diag, True)
            for rb in range(tq // sr):
                rows = slice(rb * sr, (rb + 1) * sr)
                ks, vs = k_ref[rows, :], v_ref[rows, :]
                dk, dv = dk_sc[rows, :], dv_sc[rows, :]
                for rb2, cb, masked in blocks:
                    if rb2 != rb:
                        continue
                    cols = slice(cb * sc, (cb + 1) * sc)
                    qs, dos = q_ref[cols, :], do_ref[cols, :]
                    pt = jnp.exp2(_dot(ks, qs, "nt") - lse_ref[:, cols])
                    if masked:
                        r, c = _sub_iota(sr, sc, rb, cb)
                        pt = jnp.where(r <= c, pt, 0.0)
                    dpt = _dot(vs, dos, "nt")
                    dv = dv + _dot(pt, dos)
                    dk = dk + _dot(pt * (dpt - dl_ref[:, cols]), qs)
                dk_sc[rows, :] = dk
                dv_sc[rows, :] = dv

        @pl.when(i == j)
        def _():
            tile(True)

        @pl.when(i > j)
        def _():
            tile(False)

        @pl.when(i == n - 1)
        def _():
            dk_ref[...] = dk_sc[...] * LN2
            dv_ref[...] = dv_sc[...]

    blk = lambda which: pl.BlockSpec((tq, LANES), which)
    qmap = lambda h, p, jj, ii: (ii[p], h)
    kmap = lambda h, p, jj, ii: (jj[p], h)
    row = pl.BlockSpec((None, 1, tq), lambda h, p, jj, ii: (h, 0, ii[p]))
    return pl.pallas_call(
        body, name="mla_flash_dkv",
        grid_spec=pltpu.PrefetchScalarGridSpec(
            num_scalar_prefetch=2, grid=(MLA_HEADS, int(ii.shape[0])),
            in_specs=[blk(qmap), blk(kmap), blk(kmap), blk(qmap), row, row],
            out_specs=[blk(kmap), blk(kmap)],
            scratch_shapes=[pltpu.VMEM((tq, LANES), F32)] * 2),
        out_shape=[jax.ShapeDtypeStruct((T, HP), F32)] * 2,
        compiler_params=_cparams(("parallel", "arbitrary")),
    )(jj, ii, q, k, v, do, lse_row, delta_row)


def _mem_fwd(z, km, vm, gq):
    scale = MEM_HEAD_DIM ** -0.5

    def fn(qm, km, vm, gq):
        ys = []
        for h in range(MEM_HEADS):
            sl = slice(h * LANES, (h + 1) * LANES)
            q = _rms(qm[:, sl], gq) * scale
            s = _dot(q, km[:, sl], "nt")
            p = jnp.exp(s - jnp.max(s, axis=1, keepdims=True))
            p = p / jnp.sum(p, axis=1, keepdims=True)
            ys.append(_dot(p, vm[:, sl]))
        return jnp.concatenate(ys, axis=1)

    y, = _rowwise(fn, "mem_fwd", 512, [(z, MEM_WIDTH, Z_QM // MEM_WIDTH)], [km, vm, gq], [(MEM_WIDTH, BF16)])
    return y


def _mem_bwd(z, dy, km, vm, gq):
    scale = MEM_HEAD_DIM ** -0.5

    def fn(qm, dy, km, vm, gq):
        dqs, dks, dvs = [], [], []
        dgq = jnp.zeros((1, LANES), F32)
        for h in range(MEM_HEADS):
            sl = slice(h * LANES, (h + 1) * LANES)
            q = (_rms(qm[:, sl], gq) * scale).astype(BF16)
            dyh = dy[:, sl]
            kh, vh = km[:, sl], vm[:, sl]
            s = _dot(q, kh, "nt")
            p = jnp.exp(s - jnp.max(s, axis=1, keepdims=True))
            p = p / jnp.sum(p, axis=1, keepdims=True)
            dp = _dot(dyh, vh, "nt")
            ds = p * (dp - jnp.sum(p * dp, axis=1, keepdims=True))
            dq = _dot(ds, kh) * scale
            dx, dg = _rms_bwd(qm[:, sl], gq, dq)
            dqs.append(dx)
            dgq = dgq + dg
            st = _dot(kh, q, "nt")
            pt = jnp.exp(st - jnp.max(st, axis=0, keepdims=True))
            pt = pt / jnp.sum(pt, axis=0, keepdims=True)
            dpt = _dot(vh, dyh, "nt")
            dst = pt * (dpt - jnp.sum(pt * dpt, axis=0, keepdims=True))
            dvs.append(_dot(pt, dyh))
            dks.append(_dot(dst, q))
        return jnp.concatenate(dqs, axis=1), jnp.concatenate(dks, axis=1), jnp.concatenate(dvs, axis=1), dgq

    m = km.shape[0]
    return _rowwise(fn, "mem_bwd", 512, [(z, MEM_WIDTH, Z_QM // MEM_WIDTH), dy], [km, vm, gq],
                    [(MEM_WIDTH, BF16)], [((m, MEM_WIDTH), F32), ((m, MEM_WIDTH), F32), ((1, LANES), F32)])


def _local_step(x, mem, positions, loss_target, P, W):
    T = x.shape[0]
    G = {}

    half = MLA_ROPE // 2
    inv = ROPE_BASE ** (-jnp.arange(half, dtype=F32) / half)
    ang = positions.astype(F32)[:, None] * inv
    cos, sin = jnp.cos(ang), jnp.sin(ang)
    one, zero = jnp.ones((T, MLA_NOPE), F32), jnp.zeros((T, half), F32)
    pad = LANES - MLA_QK
    tabs = (jnp.concatenate([one, cos, cos, jnp.ones((T, pad), F32)], axis=1),
            jnp.concatenate([jnp.zeros((T, MLA_NOPE), F32), -sin, zero, jnp.zeros((T, pad), F32)], axis=1),
            jnp.concatenate([jnp.zeros((T, MLA_NOPE), F32), zero, sin, jnp.zeros((T, pad), F32)], axis=1))
    gq_p = jnp.pad(P["mla_q_norm"], ((0, 0), (0, pad)))
    gk_p = jnp.pad(P["mla_k_norm"], ((0, 0), (0, pad)))
    bias_full = jnp.repeat(P["sg_b"].T, SG_GROUP_DIM, axis=1)
    group_ind = jnp.repeat(jnp.eye(SG_GROUPS, dtype=F32), SG_GROUP_DIM, axis=0)

    h1, = _rowwise(lambda x, g: _rms(x, g), "ffn1_norm", 512, [x], [P["ffn1_norm"]], [(D_MODEL, BF16)])
    gu1, a1, o1 = _ffn_fwd(h1, W["ffn1_w_gu"], W["ffn1_w_down"], "ffn1")

    def resid_norm(x, o, g):
        xn = x + 0.5 * o
        return xn, _rms(xn, g)

    x1, hm = _rowwise(resid_norm, "mix_norm", 512, [x, o1], [P["mix_norm"]], [(D_MODEL, F32), (D_MODEL, BF16)])
    z = _mm(hm, W["w_in"], "nn", F32, "w_in", tm=1024, tn=768)

    y_a = _sg_fwd(z, P["sg_ln_g"], P["sg_ln_b"], P["sg_w"], bias_full)

    def c_norm(cq, ckv, gq, gkv):
        return _rms(cq, gq), _rms(ckv, gkv)

    cqn, ckvn = _rowwise(c_norm, "mla_cnorm", 512,
                         [(z, MLA_Q_RANK, Z_CQ // MLA_Q_RANK), (z, MLA_KV_RANK, Z_CKV // MLA_KV_RANK)],
                         [P["mla_cq_norm"], P["mla_ckv_norm"]], [(MLA_Q_RANK, BF16), (MLA_KV_RANK, BF16)])
    q_pre = _mm(cqn, W["mla_w_uq"], "nn", F32, "mla_uq", tm=1024, tn=1024)
    kv_pre = _mm(ckvn, W["mla_w_ukv"], "nn", F32, "mla_ukv", tm=1024, tn=1024)
    q, k, v = _mla_post(q_pre, kv_pre, z, tabs, gq_p, gk_p)
    y_b, lse = _flash_fwd(q, k, v)

    memn, = _rowwise(lambda m, g: _rms(m, g), "mem_norm", 256, [mem], [P["mem_norm"]], [(D_MODEL, BF16)])
    kvm = _mm(memn, W["mem_w_kv"], "nn", F32, "mem_kv")

    def mem_k(kvm, gk):
        ks = [_rms(kvm[:, h * LANES:(h + 1) * LANES], gk) for h in range(MEM_HEADS)]
        return jnp.concatenate(ks, axis=1), kvm[:, MEM_WIDTH:]

    km, vm = _rowwise(mem_k, "mem_knorm", 256, [kvm], [P["mem_k_norm"]], [(MEM_WIDTH, BF16), (MEM_WIDTH, BF16)])
    y_c = _mem_fwd(z, km, vm, P["mem_q_norm"])

    pa = _mm(y_a, W["w_branch_a"], "nn", F32, "branch_a", tm=1024, tn=1024)
    pb = _mm(y_b, W["w_branch_b"], "nn", F32, "branch_b", tm=1024, tn=1024)
    pc = _mm(y_c, W["w_branch_c"], "nn", F32, "branch_c", tm=1024, tn=1024)

    def merge(zg, pa, pb, pc, b):
        g = _sigmoid(zg + b)
        return g[:, :D_MODEL] * pa + g[:, D_MODEL:2 * D_MODEL] * pb + g[:, 2 * D_MODEL:] * pc

    merged, = _rowwise(merge, "merge", 256, [(z, 3 * D_MODEL, 0), pa, pb, pc], [P["b_gate"]], [(D_MODEL, BF16)])
    om = _mm(merged, W["w_out"], "nn", F32, "w_out", tm=1024, tn=1024)

    def resid_norm1(x, o, g):
        xn = x + o
        return xn, _rms(xn, g)

    x2, h2 = _rowwise(resid_norm1, "ffn2_norm", 512, [x1, om], [P["ffn2_norm"]], [(D_MODEL, F32), (D_MODEL, BF16)])
    gu2, a2, o2 = _ffn_fwd(h2, W["ffn2_w_gu"], W["ffn2_w_down"], "ffn2")

    def loss_fn(x2, o2, t):
        e = x2 + 0.5 * o2 - t
        return e * (1.0 / D_MODEL), (e * (0.5 / D_MODEL)).astype(BF16), _rsum(e * e) * (0.5 / D_MODEL)

    dx3, do2, loss_part = _rowwise(loss_fn, "loss", 512, [x2, o2, loss_target], [],
                                   [(D_MODEL, F32), (D_MODEL, BF16)], [((1, D_MODEL), F32)])

    dh2, G["ffn2_w_gu"], G["ffn2_w_down"] = _ffn_bwd(do2, h2, gu2, a2, W["ffn2_w_gu"], W["ffn2_w_down"], "ffn2")

    def norm_bwd(x, dh, dxo, g):
        dx, dg = _rms_bwd(x, g, dh)
        dx = dx + dxo
        return dx, dx, dg

    dx2, dx2b, G["ffn2_norm"] = _rowwise(norm_bwd, "ffn2_norm_bwd", 512, [x2, dh2, dx3], [P["ffn2_norm"]],
                                         [(D_MODEL, F32), (D_MODEL, BF16)], [((1, D_MODEL), F32)])

    G["w_out"] = _mm_t(merged, dx2b, "w_out_dw", tm=1024, tn=1024)
    dmerged = _mm(dx2b, W["w_out"], "nt", F32, "w_out_dx", tm=1024, tn=1024)

    def merge_bwd(zg, pa, pb, pc, dm, b):
        g = _sigmoid(zg + b)
        ps = jnp.concatenate([pa, pb, pc], axis=1)
        dm3 = jnp.concatenate([dm, dm, dm], axis=1)
        dzg = dm3 * ps * g * (1.0 - g)
        dp = dm3 * g
        return dzg, dp[:, :D_MODEL], dp[:, D_MODEL:2 * D_MODEL], dp[:, 2 * D_MODEL:], _rsum(dzg)

    dzg, dpa, dpb, dpc, G["b_gate"] = _rowwise(
        merge_bwd, "merge_bwd", 256, [(z, 3 * D_MODEL, 0), pa, pb, pc, dmerged], [P["b_gate"]],
        [(3 * D_MODEL, BF16), (D_MODEL, BF16), (D_MODEL, BF16), (D_MODEL, BF16)], [((1, 3 * D_MODEL), F32)])

    G["w_branch_a"] = _mm_t(y_a, dpa, "branch_a_dw", tm=512, tn=1024)
    G["w_branch_b"] = _mm_t(y_b, dpb, "branch_b_dw", tm=1024, tn=1024)
    G["w_branch_c"] = _mm_t(y_c, dpc, "branch_c_dw", tm=512, tn=1024)
    dy_a = _mm(dpa, W["w_branch_a"], "nt", BF16, "branch_a_dx", tm=1024, tn=512)
    dy_b = _mm(dpb, W["w_branch_b"], "nt", BF16, "branch_b_dx", tm=1024, tn=1024)
    dy_c = _mm(dpc, W["w_branch_c"], "nt", BF16, "branch_c_dx", tm=1024, tn=512)

    du_pre, dv_pre, G["sg_w"], dbias_t, G["sg_ln_g"], G["sg_ln_b"] = _sg_bwd(
        z, dy_a, P["sg_ln_g"], P["sg_ln_b"], P["sg_w"], bias_full, group_ind)
    G["sg_b"] = dbias_t.T

    dqm, dkm, dvm, G["mem_q_norm"] = _mem_bwd(z, dy_c, km, vm, P["mem_q_norm"])

    def mem_k_bwd(kvm, dkm, dvm, gk):
        dks = []
        dg = jnp.zeros((1, LANES), F32)
        for h in range(MEM_HEADS):
            sl = slice(h * LANES, (h + 1) * LANES)
            dx, d = _rms_bwd(kvm[:, sl], gk, dkm[:, sl])
            dks.append(dx)
            dg = dg + d
        return jnp.concatenate(dks + [dvm], axis=1), dg

    dkvm, G["mem_k_norm"] = _rowwise(mem_k_bwd, "mem_knorm_bwd", 256, [kvm, dkm, dvm], [P["mem_k_norm"]],
                                     [(2 * MEM_WIDTH, BF16)], [((1, LANES), F32)])
    G["mem_w_kv"] = _mm(memn, dkvm, "tn", F32, "mem_kv_dw")
    dmemn = _mm(dkvm, W["mem_w_kv"], "nt", F32, "mem_kv_dx")
    _, G["mem_norm"] = _rowwise(lambda m, d, g: _rms_bwd(m, g, d), "mem_norm_bwd", 256, [mem, dmemn],
                                [P["mem_norm"]], [(D_MODEL, BF16)], [((1, D_MODEL), F32)])

    def delta_fn(o, do):
        od = o.astype(F32) * do.astype(F32)
        ds = [jnp.broadcast_to(jnp.sum(od[:, h * LANES:(h + 1) * LANES], axis=1, keepdims=True), (od.shape[0], LANES))
              for h in range(MLA_HEADS)]
        return jnp.concatenate(ds, axis=1)

    delta, = _rowwise(delta_fn, "mla_delta", 512, [y_b, dy_b], [], [(HP, F32)])
    rowform = lambda a: a.reshape(T, MLA_HEADS, LANES)[:, :, 0].T.reshape(MLA_HEADS, 1, T)
    dq = _flash_dq(q, k, v, dy_b, lse, delta)
    dk, dv = _flash_dkv(q, k, v, dy_b, rowform(lse), rowform(delta))
    dq_pre, dkv_pre, dkr, dgq, dgk = _mla_post_bwd(q_pre, kv_pre, z, tabs, gq_p, gk_p, dq, dk, dv)
    G["mla_q_norm"], G["mla_k_norm"] = dgq[:, :MLA_QK], dgk[:, :MLA_QK]
    G["mla_w_uq"] = _mm_t(cqn, dq_pre, "mla_uq_dw", tm=384, tn=1024)
    G["mla_w_ukv"] = _mm_t(ckvn, dkv_pre, "mla_ukv_dw", tm=256, tn=2048)
    dcqn = _mm(dq_pre, W["mla_w_uq"], "nt", F32, "mla_uq_dx", tm=1024)
    dckvn = _mm(dkv_pre, W["mla_w_ukv"], "nt", F32, "mla_ukv_dx", tm=1024)

    def c_norm_bwd(cq, ckv, dcqn, dckvn, gq, gkv):
        dcq, dgq = _rms_bwd(cq, gq, dcqn)
        dckv, dgkv = _rms_bwd(ckv, gkv, dckvn)
        return dcq, dckv, dgq, dgkv

    dcq, dckv, G["mla_cq_norm"], G["mla_ckv_norm"] = _rowwise(
        c_norm_bwd, "mla_cnorm_bwd", 512,
        [(z, MLA_Q_RANK, Z_CQ // MLA_Q_RANK), (z, MLA_KV_RANK, Z_CKV // MLA_KV_RANK), dcqn, dckvn],
        [P["mla_cq_norm"], P["mla_ckv_norm"]], [(MLA_Q_RANK, BF16), (MLA_KV_RANK, BF16)],
        [((1, MLA_Q_RANK), F32), ((1, MLA_KV_RANK), F32)])

    dz = jnp.concatenate([dzg, du_pre, dv_pre, dqm, dckv, dkr, dcq], axis=1)
    G["w_in"] = _mm_t(hm, dz, "w_in_dw", tm=1024, tn=768)
    dhm = _mm(dz, W["w_in"], "nt", F32, "w_in_dx", tm=1024, tn=1024, tk=2688)

    def norm_bwd_half(x, dh, dxo, g):
        dx, dg = _rms_bwd(x, g, dh)
        dx = dx + dxo
        return dx, (0.5 * dx), dg

    dx1, do1, G["mix_norm"] = _rowwise(norm_bwd_half, "mix_norm_bwd", 512, [x1, dhm, dx2], [P["mix_norm"]],
                                       [(D_MODEL, F32), (D_MODEL, BF16)], [((1, D_MODEL), F32)])
    dh1, G["ffn1_w_gu"], G["ffn1_w_down"] = _ffn_bwd(do1, h1, gu1, a1, W["ffn1_w_gu"], W["ffn1_w_down"], "ffn1")

    def norm_bwd_last(x, dh, dxo, g):
        dx, dg = _rms_bwd(x, g, dh)
        return dx + dxo, dg

    grad_x, G["ffn1_norm"] = _rowwise(norm_bwd_last, "ffn1_norm_bwd", 512, [x, dh1, dx1], [P["ffn1_norm"]],
                                      [(D_MODEL, F32)], [((1, D_MODEL), F32)])
    return loss_part, grad_x, G


SHARDED = ["ffn1_w_gu", "ffn1_w_down", "w_in", "mla_w_uq", "mla_w_ukv", "mem_w_kv",
           "w_branch_a", "w_branch_b", "w_branch_c", "w_out", "ffn2_w_gu", "ffn2_w_down"]
ROW_SHARDED = {"ffn1_w_down", "mem_w_kv", "w_out", "ffn2_w_down"}
SMALL = ["ffn1_norm", "mix_norm", "b_gate", "sg_ln_g", "sg_ln_b", "sg_w", "sg_b", "mla_cq_norm",
         "mla_ckv_norm", "mla_q_norm", "mla_k_norm", "mem_norm", "mem_q_norm", "mem_k_norm", "ffn2_norm"]
ORDER = ["ffn1_norm", "ffn1_w_gu", "ffn1_w_down", "mix_norm", "w_in", "b_gate", "sg_ln_g", "sg_ln_b", "sg_w",
         "sg_b", "mla_cq_norm", "mla_w_uq", "mla_ckv_norm", "mla_w_ukv", "mla_q_norm", "mla_k_norm", "mem_norm",
         "mem_w_kv", "mem_q_norm", "mem_k_norm", "w_branch_a", "w_branch_b", "w_branch_c", "w_out", "ffn2_norm",
         "ffn2_w_gu", "ffn2_w_down"]

_IN_U, _IN_V, _IN_CQ, _IN_CKV, _IN_KR, _IN_QM, _IN_G = 0, 512, 1024, 1408, 1664, 1696, 2208
IN_COLS = 5280


def _full_from_slabs(name, slabs):
    n, r, c = slabs.shape
    if name in ROW_SHARDED:
        return slabs.reshape(n * r, c)
    return slabs.transpose(1, 0, 2).reshape(r, n * c)


def _slabs_from_full(name, full):
    if name in ROW_SHARDED:
        return full.reshape(N_DEV, full.shape[0] // N_DEV, full.shape[1])
    r, c = full.shape
    return full.reshape(r, N_DEV, c // N_DEV).transpose(1, 0, 2)


def _compute_layout(full):
    W = dict(full)
    w = full["w_in"]
    kr = jnp.pad(w[:, _IN_KR:_IN_QM], ((0, 0), (KR_LANE, LANES - KR_LANE - MLA_ROPE)))
    W["w_in"] = jnp.concatenate([w[:, _IN_G:], w[:, _IN_U:_IN_CQ], w[:, _IN_QM:_IN_G], w[:, _IN_CKV:_IN_KR], kr,
                                 w[:, _IN_CQ:_IN_CKV]], axis=1)
    uq = full["mla_w_uq"].reshape(MLA_Q_RANK, MLA_HEADS, MLA_QK)
    W["mla_w_uq"] = jnp.pad(uq, ((0, 0), (0, 0), (0, LANES - MLA_QK))).reshape(MLA_Q_RANK, HP)
    ukv = full["mla_w_ukv"].reshape(MLA_KV_RANK, MLA_HEADS, MLA_NOPE + MLA_V)
    padh = lambda a: jnp.pad(a, ((0, 0), (0, 0), (0, LANES - a.shape[2]))).reshape(MLA_KV_RANK, HP)
    W["mla_w_ukv"] = jnp.concatenate([padh(ukv[:, :, :MLA_NOPE]), padh(ukv[:, :, MLA_NOPE:])], axis=1)
    wb = full["w_branch_b"].reshape(MLA_HEADS, MLA_V, D_MODEL)
    W["w_branch_b"] = jnp.pad(wb, ((0, 0), (0, LANES - MLA_V), (0, 0))).reshape(HP, D_MODEL)
    return W


def _reference_layout(G):
    out = dict(G)
    g = G["w_in"]
    out["w_in"] = jnp.concatenate([
        g[:, Z_U:Z_QM], g[:, Z_CQ:Z_COLS], g[:, Z_CKV:Z_KR], g[:, Z_KR + KR_LANE:Z_KR + KR_LANE + MLA_ROPE],
        g[:, Z_QM:Z_CKV], g[:, Z_G:Z_U]], axis=1)
    out["mla_w_uq"] = G["mla_w_uq"].reshape(MLA_Q_RANK, MLA_HEADS, LANES)[:, :, :MLA_QK].reshape(MLA_Q_RANK, -1)
    gk = G["mla_w_ukv"][:, :HP].reshape(MLA_KV_RANK, MLA_HEADS, LANES)[:, :, :MLA_NOPE]
    gv = G["mla_w_ukv"][:, HP:].reshape(MLA_KV_RANK, MLA_HEADS, LANES)[:, :, :MLA_V]
    out["mla_w_ukv"] = jnp.concatenate([gk, gv], axis=2).reshape(MLA_KV_RANK, -1)
    out["w_branch_b"] = G["w_branch_b"].reshape(MLA_HEADS, LANES, D_MODEL)[:, :MLA_V].reshape(-1, D_MODEL)
    return out


def _pack(parts):
    flat = []
    for a in parts:
        a = a.reshape(-1)
        flat.append(jnp.pad(a, (0, (-a.shape[0]) % LANES)))
    return jnp.concatenate(flat).reshape(-1, LANES)


def _unpack(packed, shapes):
    flat = packed.reshape(-1)
    out, off = [], 0
    for shp in shapes:
        n = int(np.prod(shp))
        out.append(flat[off:off + n].reshape(shp))
        off += n + (-n) % LANES
    return out


MESH = pl.DeviceIdType.MESH
HBM = pl.BlockSpec(memory_space=pltpu.HBM)


def _all_gather(shard):
    rows, lanes = shard.shape

    def body(x_ref, out_ref, send_sems, recv_sems, local_sem):
        x, y, c = lax.axis_index("x"), lax.axis_index("y"), lax.axis_index("c")
        me, sibling = (x, y, c), (x, y, 1 - c)
        chips = [(1 - x, y), (x, 1 - y), (1 - x, 1 - y)]

        def slot(px, py, pc):
            return out_ref.at[4 * px + 2 * py + pc]

        def copy(k, block, to, src=None):
            return pltpu.make_async_remote_copy(
                src_ref=slot(*block) if src is None else src, dst_ref=slot(*block),
                send_sem=send_sems.at[k], recv_sem=recv_sems.at[k], device_id=to, device_id_type=MESH)

        mine = pltpu.make_async_copy(x_ref, slot(*me), local_sem)
        mine.start()
        first = [copy(0, me, sibling, src=x_ref)]
        first += [copy(1 + j, me, (*chip, c), src=x_ref) for j, chip in enumerate(chips)]
        for cp in first:
            cp.start()
        passed = [copy(4 + j, (*chip, c), sibling) for j, chip in enumerate(chips)]
        for j, chip in enumerate(chips):
            copy(1 + j, (*chip, c), me).wait_recv()
            passed[j].start()
        copy(0, sibling, me).wait_recv()
        for j, chip in enumerate(chips):
            copy(4 + j, (*chip, 1 - c), me).wait_recv()
        for cp in first + passed:
            cp.wait_send()
        mine.wait()

    return pl.pallas_call(
        body, name="all_gather_weights",
        out_shape=jax.ShapeDtypeStruct((N_DEV, rows, lanes), shard.dtype),
        in_specs=[HBM], out_specs=HBM,
        scratch_shapes=[pltpu.SemaphoreType.DMA((7,)), pltpu.SemaphoreType.DMA((7,)), pltpu.SemaphoreType.DMA],
    )(shard)


def _exchange(big, small):
    def body(big_ref, small_ref, bout_ref, sout_ref, send_sems, recv_sems, local_sems):
        x, y, c = lax.axis_index("x"), lax.axis_index("y"), lax.axis_index("c")
        me = 4 * x + 2 * y + c
        own = [pltpu.make_async_copy(big_ref.at[me], bout_ref.at[me], local_sems.at[0]),
               pltpu.make_async_copy(small_ref, sout_ref.at[me], local_sems.at[1])]
        for cp in own:
            cp.start()
        copies = []
        for k in range(1, N_DEV):
            px = 1 - x if k & 4 else x
            py = 1 - y if k & 2 else y
            pc = 1 - c if k & 1 else c
            peer = 4 * px + 2 * py + pc
            copies.append(pltpu.make_async_remote_copy(
                src_ref=big_ref.at[peer], dst_ref=bout_ref.at[me], send_sem=send_sems.at[k - 1],
                recv_sem=recv_sems.at[k - 1], device_id=(px, py, pc), device_id_type=MESH))
            copies.append(pltpu.make_async_remote_copy(
                src_ref=small_ref, dst_ref=sout_ref.at[me], send_sem=send_sems.at[7 + k - 1],
                recv_sem=recv_sems.at[7 + k - 1], device_id=(px, py, pc), device_id_type=MESH))
        for cp in copies:
            cp.start()
        for cp in copies:
            cp.wait()
        for cp in own:
            cp.wait()

    return pl.pallas_call(
        body, name="exchange_grads",
        out_shape=[jax.ShapeDtypeStruct(big.shape, big.dtype),
                   jax.ShapeDtypeStruct((N_DEV,) + small.shape, small.dtype)],
        in_specs=[HBM, HBM], out_specs=[HBM, HBM],
        scratch_shapes=[pltpu.SemaphoreType.DMA((14,)), pltpu.SemaphoreType.DMA((14,)),
                        pltpu.SemaphoreType.DMA((2,))],
    )(big, small)


def _sum_slots(recv, name, tr):
    n, rows, lanes = recv.shape
    tr = _tile(rows, tr)

    def body(r_ref, o_ref):
        acc = r_ref[0].astype(F32)
        for i in range(1, n):
            acc = acc + r_ref[i].astype(F32)
        o_ref[...] = acc

    return pl.pallas_call(
        body, name=name, grid=(rows // tr,),
        in_specs=[pl.BlockSpec((n, tr, lanes), lambda i: (0, i, 0))],
        out_specs=pl.BlockSpec((tr, lanes), lambda i: (i, 0)),
        out_shape=jax.ShapeDtypeStruct((rows, lanes), F32),
        compiler_params=_cparams(("parallel",)),
    )(recv)


def _adamw(w, g, m, v, name, tr=256):
    c1 = 1.0 - ADAM_B1 ** ADAM_STEP
    c2 = 1.0 - ADAM_B2 ** ADAM_STEP

    def fn(w, g, m, v):
        m = ADAM_B1 * m + (1.0 - ADAM_B1) * g
        v = ADAM_B2 * v + (1.0 - ADAM_B2) * (g * g)
        delta = -ADAM_LR * ((m / c1) / (jnp.sqrt(v / c2) + ADAM_EPS) + ADAM_WD * w)
        return delta, m, v

    return _rowwise(fn, name, tr, [w, g, m, v], [], [(w.shape[1], F32)] * 3)


def kernel(x, mem, positions, ffn1_norm, ffn1_w_gu, ffn1_w_down, mix_norm, w_in, b_gate, sg_ln_g, sg_ln_b, sg_w, sg_b, mla_cq_norm, mla_w_uq, mla_ckv_norm, mla_w_ukv, mla_q_norm, mla_k_norm, mem_norm, mem_w_kv, mem_q_norm, mem_k_norm, w_branch_a, w_branch_b, w_branch_c, w_out, ffn2_norm, ffn2_w_gu, ffn2_w_down, loss_target, m_ffn1_norm, m_ffn1_w_gu, m_ffn1_w_down, m_mix_norm, m_w_in, m_b_gate, m_sg_ln_g, m_sg_ln_b, m_sg_w, m_sg_b, m_mla_cq_norm, m_mla_w_uq, m_mla_ckv_norm, m_mla_w_ukv, m_mla_q_norm, m_mla_k_norm, m_mem_norm, m_mem_w_kv, m_mem_q_norm, m_mem_k_norm, m_w_branch_a, m_w_branch_b, m_w_branch_c, m_w_out, m_ffn2_norm, m_ffn2_w_gu, m_ffn2_w_down, v_ffn1_norm, v_ffn1_w_gu, v_ffn1_w_down, v_mix_norm, v_w_in, v_b_gate, v_sg_ln_g, v_sg_ln_b, v_sg_w, v_sg_b, v_mla_cq_norm, v_mla_w_uq, v_mla_ckv_norm, v_mla_w_ukv, v_mla_q_norm, v_mla_k_norm, v_mem_norm, v_mem_w_kv, v_mem_q_norm, v_mem_k_norm, v_w_branch_a, v_w_branch_b, v_w_branch_c, v_w_out, v_ffn2_norm, v_ffn2_w_gu, v_ffn2_w_down):
    given = dict(locals())
    wts = {n: given[n] for n in ORDER}
    mom = {n: given["m_" + n] for n in ORDER}
    var = {n: given["v_" + n] for n in ORDER}
    shard_shapes = {n: wts[n].shape[1:] for n in SHARDED}
    shard_rows = {n: int(np.prod(shard_shapes[n])) // LANES for n in SHARDED}
    row_off, off = {}, 0
    for n in SHARDED:
        row_off[n] = off
        off += shard_rows[n]

    gathered = _all_gather(jnp.concatenate(
        [wts[n][0].astype(BF16).reshape(shard_rows[n], LANES) for n in SHARDED], axis=0))
    full = {n: _full_from_slabs(n, gathered[:, row_off[n]:row_off[n] + shard_rows[n]].reshape(N_DEV, *shard_shapes[n]))
            for n in SHARDED}
    W = _compute_layout(full)
    P = {n: wts[n] if wts[n].ndim == 2 else wts[n][0] for n in SMALL}

    loss_part, grad_x, G = _local_step(x[0], mem[0], positions[0], loss_target[0], P, W)
    G = _reference_layout(G)

    big = jnp.concatenate(
        [_slabs_from_full(n, G[n]).astype(BF16).reshape(N_DEV, shard_rows[n], LANES) for n in SHARDED], axis=1)
    small_shapes = [wts[n].shape[1:] for n in SMALL]
    small = _pack([G[n].reshape(s) for n, s in zip(SMALL, small_shapes)])
    small = jnp.pad(small, ((0, (-small.shape[0]) % 8), (0, 0)))
    big_recv, small_recv = _exchange(big, small)
    g_big_packed = _sum_slots(big_recv, "sum_big", 1024)
    g_small_packed = _sum_slots(small_recv, "sum_small", 2048)
    g_small = _unpack(g_small_packed, small_shapes)
    grads = {n: g_big_packed[row_off[n]:row_off[n] + shard_rows[n]].reshape(shard_shapes[n]) for n in SHARDED}
    grads.update(zip(SMALL, g_small))

    delta, new_m, new_v = {}, {}, {}
    for n in SHARDED:
        delta[n], new_m[n], new_v[n] = _adamw(wts[n][0], grads[n], mom[n][0], var[n][0], "adamw_" + n)

    def pack_small(d):
        p = _pack([d[n].reshape(s) for n, s in zip(SMALL, small_shapes)])
        return jnp.pad(p, ((0, (-p.shape[0]) % 8), (0, 0)))

    ds, ms, vs = _adamw(pack_small(wts), g_small_packed, pack_small(mom), pack_small(var), "adamw_small", tr=2048)
    for dst, packed in ((delta, ds), (new_m, ms), (new_v, vs)):
        dst.update(zip(SMALL, _unpack(packed, small_shapes)))

    loss = lax.psum(jnp.sum(loss_part), ("x", "y", "c"))
    lead = lambda d: [d[n].reshape(wts[n].shape) for n in ORDER]
    return (loss, grad_x[None], *lead(grads), *lead(delta), *lead(new_m), *lead(new_v))
```

```python
import functools

import numpy as np
import jax
import jax.numpy as jnp
from jax import lax
from jax.experimental import pallas as pl
from jax.experimental.pallas import tpu as pltpu

F32, BF16 = jnp.float32, jnp.bfloat16

D_MODEL = 1024
SG_GROUPS, SG_GROUP_DIM, SG_WIDTH, CHUNK = 8, 64, 512, 128
MLA_HEADS, MLA_NOPE, MLA_ROPE, MLA_V, MLA_QK = 8, 64, 32, 64, 96
MLA_Q_RANK, MLA_KV_RANK = 384, 256
MEM_HEADS, MEM_HEAD_DIM, MEM_WIDTH = 4, 128, 512
D_FF = 2816
ROPE_BASE = 10000.0
EPS = 1e-6
NEG = -1e30
ADAM_LR, ADAM_B1, ADAM_B2, ADAM_EPS, ADAM_WD, ADAM_STEP = 0.001, 0.9, 0.999, 1e-08, 0.01, 10

N_DEV = 8
LANES = 128
V7X_VMEM_LIMIT = 56 * 1024 * 1024
HP = MLA_HEADS * LANES

Z_G, Z_U, Z_V, Z_QM, Z_CKV, Z_KR, Z_CQ = 0, 3072, 3584, 4096, 4608, 4864, 4992
Z_COLS = 5376
KR_LANE = 64


def _tile(dim, pref):
    if dim <= pref:
        return dim
    for t in range(pref - pref % LANES, LANES - 1, -LANES):
        if dim % t == 0:
            return t
    for t in range(pref - pref % 8, 7, -8):
        if dim % t == 0:
            return t
    return dim


def _cparams(sem):
    return pltpu.CompilerParams(dimension_semantics=sem, vmem_limit_bytes=V7X_VMEM_LIMIT)


_DN = {"nn": ((1,), (0,)), "nt": ((1,), (1,)), "tn": ((0,), (0,))}


def _dot(a, b, mode="nn"):
    return lax.dot_general(a.astype(BF16), b.astype(BF16), (_DN[mode], ((), ())),
                           preferred_element_type=F32)


def _mm(a, b, mode, out_dtype, name, tm=512, tn=512, tk=2048):
    if mode == "tn":
        K, M = a.shape
    else:
        M, K = a.shape
    N = b.shape[0] if mode == "nt" else b.shape[1]
    tm, tn, tk = _tile(M, tm), _tile(N, tn), _tile(K, tk)
    nk = K // tk
    if mode == "tn":
        a_spec = pl.BlockSpec((tk, tm), lambda i, j, k: (k, i))
    else:
        a_spec = pl.BlockSpec((tm, tk), lambda i, j, k: (i, k))
    if mode == "nt":
        b_spec = pl.BlockSpec((tn, tk), lambda i, j, k: (j, k))
    else:
        b_spec = pl.BlockSpec((tk, tn), lambda i, j, k: (k, j))

    def body(a_ref, b_ref, o_ref, *scratch):
        p = _dot(a_ref[...], b_ref[...], mode)
        if nk == 1:
            o_ref[...] = p.astype(o_ref.dtype)
        else:
            acc_ref, = scratch
            k = pl.program_id(2)

            @pl.when(k == 0)
            def _():
                acc_ref[...] = p

            @pl.when(k > 0)
            def _():
                acc_ref[...] += p

            @pl.when(k == nk - 1)
            def _():
                o_ref[...] = acc_ref[...].astype(o_ref.dtype)

    return pl.pallas_call(
        body, name=name, grid=(M // tm, N // tn, nk),
        in_specs=[a_spec, b_spec],
        out_specs=pl.BlockSpec((tm, tn), lambda i, j, k: (i, j)),
        out_shape=jax.ShapeDtypeStruct((M, N), out_dtype),
        scratch_shapes=[] if nk == 1 else [pltpu.VMEM((tm, tn), F32)],
        compiler_params=_cparams(("parallel", "parallel", "arbitrary")),
    )(a, b)


def _mm_t(at, b, name, tm, tn, tk=1024):
    return _mm(at, b, "nn", F32, name, tm=tm, tn=tn, tk=tk)


def _rowwise(fn, name, tr, row_ins, bc_ins, row_outs, acc_outs=()):
    norm = [it if isinstance(it, tuple) else (it, it.shape[1], 0) for it in row_ins]
    rows = norm[0][0].shape[0]
    tr = _tile(rows, tr)
    arrays, in_specs = [], []
    for arr, w, cb in norm:
        arrays.append(arr)
        in_specs.append(pl.BlockSpec((tr, w), lambda i, cb=cb: (i, cb)))
    for arr in bc_ins:
        arrays.append(arr)
        in_specs.append(pl.BlockSpec(arr.shape, lambda i, nd=arr.ndim: (0,) * nd))
    out_shape, out_specs = [], []
    transposed = [len(o) == 3 for o in row_outs]
    for (w, dt, *_), t in zip(row_outs, transposed):
        out_shape.append(jax.ShapeDtypeStruct((w, rows) if t else (rows, w), dt))
        out_specs.append(pl.BlockSpec((w, tr), lambda i: (0, i)) if t else pl.BlockSpec((tr, w), lambda i: (i, 0)))
    for shp, dt in acc_outs:
        out_shape.append(jax.ShapeDtypeStruct(shp, dt))
        out_specs.append(pl.BlockSpec(shp, lambda i, nd=len(shp): (0,) * nd))
    n_in, n_row = len(arrays), len(row_outs)

    def body(*refs):
        vals = fn(*[r[...] for r in refs[:n_in]])
        if not isinstance(vals, (tuple, list)):
            vals = (vals,)
        outs = refs[n_in:]
        for r, v, t in zip(outs[:n_row], vals[:n_row], transposed):
            r[...] = v.astype(F32).T.astype(r.dtype) if t else v.astype(r.dtype)
        if acc_outs:
            accs = list(zip(outs[n_row:], vals[n_row:]))
            i = pl.program_id(0)

            @pl.when(i == 0)
            def _():
                for r, v in accs:
                    r[...] = v.astype(r.dtype)

            @pl.when(i > 0)
            def _():
                for r, v in accs:
                    r[...] += v.astype(r.dtype)

    res = pl.pallas_call(
        body, name=name, grid=(rows // tr,), in_specs=in_specs, out_specs=out_specs,
        out_shape=out_shape, compiler_params=_cparams(("arbitrary",)),
    )(*arrays)
    return res


def _rsum(x):
    return jnp.sum(x, axis=0, keepdims=True)


def _rms(x, g, n=None):
    n = x.shape[-1] if n is None else n
    r = lax.rsqrt(jnp.sum(x * x, axis=-1, keepdims=True) * (1.0 / n) + EPS)
    return x * r * g


def _rms_bwd(x, g, dy, n=None):
    n = x.shape[-1] if n is None else n
    r = lax.rsqrt(jnp.sum(x * x, axis=-1, keepdims=True) * (1.0 / n) + EPS)
    xh = x * r
    dxh = dy * g
    dx = r * (dxh - xh * (jnp.sum(dxh * xh, axis=-1, keepdims=True) * (1.0 / n)))
    return dx, _rsum(dy * xh)


def _gelu(x):
    return 0.5 * x * (1.0 + lax.erf(x * 0.7071067811865476))


def _gelu_grad(x):
    return 0.5 * (1.0 + lax.erf(x * 0.7071067811865476)) + x * jnp.exp(-0.5 * x * x) * 0.3989422804014327


def _sigmoid(x):
    return 1.0 / (1.0 + jnp.exp(-x))


def _ffn_fwd(h, w_gu, w_down, tag):
    gu = _mm(h, w_gu, "nn", BF16, f"{tag}_gu", tm=1024, tn=512)

    def act(gu):
        g = gu[:, :D_FF].astype(F32)
        u = gu[:, D_FF:].astype(F32)
        a = g * _sigmoid(g) * u
        return a, a

    a, at = _rowwise(act, f"{tag}_act", 256, [gu], [], [(D_FF, BF16), (D_FF, BF16, "T")])
    o = _mm(a, w_down, "nn", F32, f"{tag}_down", tm=1024, tn=512, tk=2816)
    return gu, at, o


def _ffn_bwd(do, ht, gu, at, w_gu, w_down, tag):
    dw_down = _mm_t(at, do, f"{tag}_dwdown", tm=1408, tn=1024)
    da = _mm(do, w_down, "nt", BF16, f"{tag}_da", tm=1024, tn=512)

    def act_bwd(gu, da):
        g = gu[:, :D_FF].astype(F32)
        u = gu[:, D_FF:].astype(F32)
        da = da.astype(F32)
        s = _sigmoid(g)
        dg = da * u * s * (1.0 + g * (1.0 - s))
        du = da * g * s
        return jnp.concatenate([dg, du], axis=1)

    dgu, = _rowwise(act_bwd, f"{tag}_actbwd", 256, [gu, da], [], [(2 * D_FF, BF16)])
    dw_gu = _mm_t(ht, dgu, f"{tag}_dwgu", tm=1024, tn=1408)
    dh = _mm(dgu, w_gu, "nt", F32, f"{tag}_dh", tm=1024, tn=512, tk=2816)
    return dh, dw_gu, dw_down


def _sg_common(u_pre, v_pre, ln_g, ln_b):
    u = _gelu(u_pre)
    v = _gelu(v_pre)
    mu = jnp.mean(v, axis=-1, keepdims=True)
    vc = v - mu
    rstd = lax.rsqrt(jnp.mean(vc * vc, axis=-1, keepdims=True) + EPS)
    vhat = vc * rstd
    vl = vhat * ln_g + ln_b
    return u, vhat, rstd, vl


def _sg_masked_pairs(w):
    t = lax.broadcasted_iota(jnp.int32, (CHUNK, CHUNK), 0)
    s = lax.broadcasted_iota(jnp.int32, (CHUNK, CHUNK), 1)
    causal = s <= t
    wm = [jnp.where(causal, w[g], 0.0).astype(BF16) for g in range(SG_GROUPS)]
    return [jnp.concatenate([wm[2 * j], wm[2 * j + 1]], axis=0) for j in range(SG_GROUPS // 2)], causal


def _sg_mix(vl, pairs, bias):
    tr = vl.shape[0]
    low = lax.broadcasted_iota(jnp.int32, (CHUNK, LANES), 1) < SG_GROUP_DIM
    vb = vl.astype(BF16)
    rows = []
    for c in range(tr // CHUNK):
        slabs = []
        for j in range(SG_GROUPS // 2):
            slab = vb[c * CHUNK:(c + 1) * CHUNK, j * LANES:(j + 1) * LANES]
            m = _dot(pairs[j], slab)
            slabs.append(jnp.where(low, m[:CHUNK], m[CHUNK:]))
        rows.append(jnp.concatenate(slabs, axis=1) + bias)
    return jnp.concatenate(rows, axis=0)


def _sg_fwd(z, ln_g, ln_b, sg_w, bias_full):
    def fn(u_pre, v_pre, ln_g, ln_b, w, bias):
        u, _, _, vl = _sg_common(u_pre, v_pre, ln_g, ln_b)
        pairs, _ = _sg_masked_pairs(w)
        y = u * _sg_mix(vl, pairs, bias)
        return y, y

    return _rowwise(fn, "sg_fwd", 512, [(z, SG_WIDTH, Z_U // SG_WIDTH), (z, SG_WIDTH, Z_V // SG_WIDTH)],
                    [ln_g, ln_b, sg_w, bias_full], [(SG_WIDTH, BF16), (SG_WIDTH, BF16, "T")])


def _sg_bwd(z, dy, ln_g, ln_b, sg_w, bias_full, group_ind):
    def fn(u_pre, v_pre, dy, ln_g, ln_b, w, bias, ind):
        dy = dy.astype(F32)
        u, vhat, rstd, vl = _sg_common(u_pre, v_pre, ln_g, ln_b)
        pairs, causal = _sg_masked_pairs(w)
        mixed = _sg_mix(vl, pairs, bias)
        du_pre = dy * mixed * _gelu_grad(u_pre)
        dmix = dy * u
        tr = dy.shape[0]
        low = lax.broadcasted_iota(jnp.int32, (CHUNK, LANES), 1) < SG_GROUP_DIM
        vb = vl.astype(BF16)
        dw = [jnp.zeros((CHUNK, CHUNK), F32) for _ in range(SG_GROUPS)]
        dbias = jnp.zeros((CHUNK, SG_WIDTH), F32)
        dvl_rows = []
        for c in range(tr // CHUNK):
            dm_c = dmix[c * CHUNK:(c + 1) * CHUNK]
            dbias = dbias + dm_c
            slabs = []
            for j in range(SG_GROUPS // 2):
                slab = vb[c * CHUNK:(c + 1) * CHUNK, j * LANES:(j + 1) * LANES]
                dm = dm_c[:, j * LANES:(j + 1) * LANES]
                d0 = jnp.where(low, dm, 0.0).astype(BF16)
                d1 = jnp.where(low, 0.0, dm).astype(BF16)
                dw[2 * j] = dw[2 * j] + _dot(d0, slab, "nt")
                dw[2 * j + 1] = dw[2 * j + 1] + _dot(d1, slab, "nt")
                slabs.append(_dot(pairs[j], jnp.concatenate([d0, d1], axis=0), "tn"))
            dvl_rows.append(jnp.concatenate(slabs, axis=1))
        dvl = jnp.concatenate(dvl_rows, axis=0)
        dln_g = _rsum(dvl * vhat)
        dln_b = _rsum(dvl)
        dvh = dvl * ln_g
        dv = rstd * (dvh - jnp.mean(dvh, axis=-1, keepdims=True)
                     - vhat * jnp.mean(dvh * vhat, axis=-1, keepdims=True))
        dv_pre = dv * _gelu_grad(v_pre)
        dw = jnp.stack([jnp.where(causal, d, 0.0) for d in dw], axis=0)
        dbias_t = lax.dot_general(dbias, ind, (((1,), (0,)), ((), ())), precision=lax.Precision.HIGHEST,
                                  preferred_element_type=F32)
        return du_pre, dv_pre, dw, dbias_t, dln_g, dln_b

    return _rowwise(fn, "sg_bwd", 512,
                    [(z, SG_WIDTH, Z_U // SG_WIDTH), (z, SG_WIDTH, Z_V // SG_WIDTH), dy],
                    [ln_g, ln_b, sg_w, bias_full, group_ind],
                    [(SG_WIDTH, BF16), (SG_WIDTH, BF16)],
                    [((SG_GROUPS, CHUNK, CHUNK), F32), ((CHUNK, SG_GROUPS), F32), ((1, SG_WIDTH), F32), ((1, SG_WIDTH), F32)])


def _rope(x, c, s1, s2):
    return x * c + pltpu.roll(x, LANES - MLA_ROPE // 2, 1) * s1 + pltpu.roll(x, MLA_ROPE // 2, 1) * s2


def _rope_t(d, c, s1, s2):
    return d * c + pltpu.roll(d * s1, MLA_ROPE // 2, 1) + pltpu.roll(d * s2, LANES - MLA_ROPE // 2, 1)


def _mla_post(q_pre, kv_pre, z, tabs, gq, gk):
    scale = MLA_QK ** -0.5 * LOG2E

    def fn(q_pre, k_pre, v_pre, kr, c, s1, s2, gq, gk):
        qs, ks = [], []
        for h in range(MLA_HEADS):
            sl = slice(h * LANES, (h + 1) * LANES)
            qs.append(_rope(_rms(q_pre[:, sl], gq, MLA_QK), c, s1, s2) * scale)
            ks.append(_rope(_rms(k_pre[:, sl] + kr, gk, MLA_QK), c, s1, s2))
        lane = lax.broadcasted_iota(jnp.int32, v_pre.shape, 1) & (LANES - 1)
        return jnp.concatenate(qs, axis=1), jnp.concatenate(ks, axis=1), jnp.where(lane == ONES_LANE, 1.0, v_pre)

    return _rowwise(fn, "mla_post", 256,
                    [q_pre, (kv_pre, HP, 0), (kv_pre, HP, 1), (z, LANES, Z_KR // LANES), *tabs],
                    [gq, gk], [(HP, BF16)] * 3)


def _mla_post_bwd(q_pre, kv_pre, z, tabs, gq, gk, dq, dk, dv):
    scale = MLA_QK ** -0.5

    def fn(q_pre, k_pre, kr, c, s1, s2, dq, dk, dv, gq, gk):
        lane = lax.broadcasted_iota(jnp.int32, (1, LANES), 1)
        kr_mask = (lane >= KR_LANE) & (lane < KR_LANE + MLA_ROPE)
        dqs, dks = [], []
        dgq = jnp.zeros((1, LANES), F32)
        dgk = jnp.zeros((1, LANES), F32)
        dkr = jnp.zeros(kr.shape, F32)
        for h in range(MLA_HEADS):
            sl = slice(h * LANES, (h + 1) * LANES)
            dqn = _rope_t(dq[:, sl].astype(F32), c, s1, s2) * scale
            dx, dg = _rms_bwd(q_pre[:, sl], gq, dqn, MLA_QK)
            dqs.append(dx)
            dgq = dgq + dg
            dkn = _rope_t(dk[:, sl].astype(F32), c, s1, s2)
            dx, dg = _rms_bwd(k_pre[:, sl] + kr, gk, dkn, MLA_QK)
            dks.append(dx)
            dgk = dgk + dg
            dkr = dkr + dx
        dkr = jnp.where(kr_mask, dkr, 0.0)
        dkv = jnp.concatenate(dks + [dv.astype(F32)], axis=1)
        return jnp.concatenate(dqs, axis=1), dkv, dkr, dgq, dgk

    return _rowwise(fn, "mla_post_bwd", 256,
                    [q_pre, (kv_pre, HP, 0), (z, LANES, Z_KR // LANES), *tabs, dq, dk, dv],
                    [gq, gk], [(HP, BF16), (2 * HP, BF16), (LANES, BF16)],
                    [((1, LANES), F32), ((1, LANES), F32)])


def _pairs(n, lower):
    a, b = [], []
    for o in range(n):
        inner = range(o + 1) if lower else range(o, n)
        for t in inner:
            a.append(o)
            b.append(t)
    return jnp.asarray(np.array(a, np.int32)), jnp.asarray(np.array(b, np.int32))


FLASH_TILE, FLASH_SUB_ROWS = 1024, 512
LOG2E, LN2 = 1.4426950408889634, 0.6931471805599453
ONES_LANE = MLA_V


def _flash_tiles(T):
    tq = _tile(T, FLASH_TILE)
    return tq, _tile(tq, FLASH_SUB_ROWS)


def _col_span(t, sr, rb, diag, key_major):
    if not diag:
        return 0, t
    return (rb * sr, t) if key_major else (0, (rb + 1) * sr)


def _span_iota(sr, rb, c0, c1):
    r = lax.broadcasted_iota(jnp.int32, (sr, c1 - c0), 0) + rb * sr
    c = lax.broadcasted_iota(jnp.int32, (sr, c1 - c0), 1) + c0
    return r, c


def _lanes(x, width):
    return jnp.concatenate([x] * (width // LANES), axis=1)


def _flash_fwd(q, k, v):
    T = q.shape[0]
    tq, sr = _flash_tiles(T)
    n = T // tq
    ii, jj = _pairs(n, True)

    def body(ii_ref, jj_ref, q_ref, k_ref, v_ref, o_ref, ot_ref, lse_ref, m_sc, acc_sc):
        p_ = pl.program_id(1)
        i, j = ii_ref[p_], jj_ref[p_]

        @pl.when(j == 0)
        def _():
            m_sc[...] = jnp.full(m_sc.shape, NEG, F32)
            acc_sc[...] = jnp.zeros(acc_sc.shape, F32)

        def tile(diag):
            for rb in range(tq // sr):
                rows = slice(rb * sr, (rb + 1) * sr)
                c0, c1 = _col_span(tq, sr, rb, diag, False)
                s = _dot(q_ref[rows, :], k_ref[c0:c1, :], "nt")
                if diag:
                    r, c = _span_iota(sr, rb, c0, c1)
                    s = jnp.where(c <= r, s, NEG)
                m = m_sc[rows, :]
                m_new = jnp.maximum(m, jnp.max(s, axis=1, keepdims=True))
                p = jnp.exp2(s - _lanes(m_new, c1 - c0))
                acc_sc[rows, :] = jnp.exp2(m - m_new) * acc_sc[rows, :] + _dot(p, v_ref[c0:c1, :])
                m_sc[rows, :] = m_new

        @pl.when(j < i)
        def _():
            tile(False)

        @pl.when(j == i)
        def _():
            tile(True)
            acc = acc_sc[...]
            lane = lax.broadcasted_iota(jnp.int32, acc.shape, 1)
            l = jnp.sum(jnp.where(lane == ONES_LANE, acc, 0.0), axis=1, keepdims=True)
            o = jnp.where(lane < MLA_V, acc / l, 0.0)
            o_ref[...] = o.astype(o_ref.dtype)
            ot_ref[...] = o.T.astype(ot_ref.dtype)
            lse_ref[...] = m_sc[...] + jnp.log2(l)

    blk = lambda which: pl.BlockSpec((tq, LANES), which)
    qmap = lambda h, p, ii, jj: (ii[p], h)
    kmap = lambda h, p, ii, jj: (jj[p], h)
    return pl.pallas_call(
        body, name="mla_flash_fwd",
        grid_spec=pltpu.PrefetchScalarGridSpec(
            num_scalar_prefetch=2, grid=(MLA_HEADS, int(ii.shape[0])),
            in_specs=[blk(qmap), blk(kmap), blk(kmap)],
            out_specs=[blk(qmap), pl.BlockSpec((LANES, tq), lambda h, p, ii, jj: (h, ii[p])), blk(qmap)],
            scratch_shapes=[pltpu.VMEM((tq, LANES), F32)] * 2),
        out_shape=[jax.ShapeDtypeStruct((T, HP), BF16), jax.ShapeDtypeStruct((HP, T), BF16),
                   jax.ShapeDtypeStruct((T, HP), F32)],
        compiler_params=_cparams(("parallel", "arbitrary")),
    )(ii, jj, q, k, v)


def _flash_dq(q, k, v, do, lse, delta):
    T = q.shape[0]
    tq, sr = _flash_tiles(T)
    n = T // tq
    ii, jj = _pairs(n, True)

    def body(ii_ref, jj_ref, q_ref, k_ref, v_ref, do_ref, lse_ref, dl_ref, dq_ref, acc_sc):
        p_ = pl.program_id(1)
        i, j = ii_ref[p_], jj_ref[p_]

        @pl.when(j == 0)
        def _():
            acc_sc[...] = jnp.zeros(acc_sc.shape, F32)

        def tile(diag):
            for rb in range(tq // sr):
                rows = slice(rb * sr, (rb + 1) * sr)
                c0, c1 = _col_span(tq, sr, rb, diag, False)
                ks = k_ref[c0:c1, :]
                p = jnp.exp2(_dot(q_ref[rows, :], ks, "nt") - _lanes(lse_ref[rows, :], c1 - c0))
                if diag:
                    r, c = _span_iota(sr, rb, c0, c1)
                    p = jnp.where(c <= r, p, 0.0)
                dp = _dot(do_ref[rows, :], v_ref[c0:c1, :], "nt")
                acc_sc[rows, :] += _dot(p * (dp - _lanes(dl_ref[rows, :], c1 - c0)), ks)

        @pl.when(j < i)
        def _():
            tile(False)

        @pl.when(j == i)
        def _():
            tile(True)
            dq_ref[...] = acc_sc[...]

    blk = lambda which: pl.BlockSpec((tq, LANES), which)
    qmap = lambda h, p, ii, jj: (ii[p], h)
    kmap = lambda h, p, ii, jj: (jj[p], h)
    return pl.pallas_call(
        body, name="mla_flash_dq",
        grid_spec=pltpu.PrefetchScalarGridSpec(
            num_scalar_prefetch=2, grid=(MLA_HEADS, int(ii.shape[0])),
            in_specs=[blk(qmap), blk(kmap), blk(kmap), blk(qmap), blk(qmap), blk(qmap)],
            out_specs=blk(qmap),
            scratch_shapes=[pltpu.VMEM((tq, LANES), F32)]),
        out_shape=jax.ShapeDtypeStruct((T, HP), F32),
        compiler_params=_cparams(("parallel", "arbitrary")),
    )(ii, jj, q, k, v, do, lse, delta)


def _flash_dkv(q, k, v, do, lse_row, delta_row):
    T = q.shape[0]
    tq, sr = _flash_tiles(T)
    n = T // tq
    jj, ii = _pairs(n, False)

    def body(jj_ref, ii_ref, q_ref, k_ref, v_ref, do_ref, lse_ref, dl_ref, dk_ref, dv_ref, dk_sc, dv_sc):
        p_ = pl.program_id(1)
        j, i = jj_ref[p_], ii_ref[p_]

        @pl.when(i == j)
        def _():
            dk_sc[...] = jnp.zeros(dk_sc.shape, F32)
            dv_sc[...] = jnp.zeros(dv_sc.shape, F32)

        def tile(diag):
            for rb in range(tq // sr):
                rows = slice(rb * sr, (rb + 1) * sr)
                c0, c1 = _col_span(tq, sr, rb, diag, True)
                qs, dos = q_ref[c0:c1, :], do_ref[c0:c1, :]
                pt = jnp.exp2(_dot(k_ref[rows, :], qs, "nt") - lse_ref[:, c0:c1])
                if diag:
                    r, c = _span_iota(sr, rb, c0, c1)
                    pt = jnp.where(r <= c, pt, 0.0)
                dpt = _dot(v_ref[rows, :], dos, "nt")
                dv_sc[rows, :] += _dot(pt, dos)
                dk_sc[rows, :] += _dot(pt * (dpt - dl_ref[:, c0:c1]), qs)

        @pl.when(i == j)
        def _():
            tile(True)

        @pl.when(i > j)
        def _():
            tile(False)

        @pl.when(i == n - 1)
        def _():
            dk_ref[...] = dk_sc[...] * LN2
            dv_ref[...] = dv_sc[...]

    blk = lambda which: pl.BlockSpec((tq, LANES), which)
    qmap = lambda h, p, jj, ii: (ii[p], h)
    kmap = lambda h, p, jj, ii: (jj[p], h)
    row = pl.BlockSpec((None, 1, tq), lambda h, p, jj, ii: (h, 0, ii[p]))
    return pl.pallas_call(
        body, name="mla_flash_dkv",
        grid_spec=pltpu.PrefetchScalarGridSpec(
            num_scalar_prefetch=2, grid=(MLA_HEADS, int(ii.shape[0])),
            in_specs=[blk(qmap), blk(kmap), blk(kmap), blk(qmap), row, row],
            out_specs=[blk(kmap), blk(kmap)],
            scratch_shapes=[pltpu.VMEM((tq, LANES), F32)] * 2),
        out_shape=[jax.ShapeDtypeStruct((T, HP), F32)] * 2,
        compiler_params=_cparams(("parallel", "arbitrary")),
    )(jj, ii, q, k, v, do, lse_row, delta_row)


def _mem_fwd(z, km, vm, gq):
    scale = MEM_HEAD_DIM ** -0.5

    def fn(qm, km, vm, gq):
        ys = []
        for h in range(MEM_HEADS):
            sl = slice(h * LANES, (h + 1) * LANES)
            q = _rms(qm[:, sl], gq) * scale
            s = _dot(q, km[:, sl], "nt")
            p = jnp.exp(s - jnp.max(s, axis=1, keepdims=True))
            p = p / jnp.sum(p, axis=1, keepdims=True)
            ys.append(_dot(p, vm[:, sl]))
        y = jnp.concatenate(ys, axis=1)
        return y, y

    return _rowwise(fn, "mem_fwd", 512, [(z, MEM_WIDTH, Z_QM // MEM_WIDTH)], [km, vm, gq],
                    [(MEM_WIDTH, BF16), (MEM_WIDTH, BF16, "T")])


def _mem_bwd(z, dy, km, vm, gq):
    scale = MEM_HEAD_DIM ** -0.5

    def fn(qm, dy, km, vm, gq):
        dqs, dks, dvs = [], [], []
        dgq = jnp.zeros((1, LANES), F32)
        for h in range(MEM_HEADS):
            sl = slice(h * LANES, (h + 1) * LANES)
            q = (_rms(qm[:, sl], gq) * scale).astype(BF16)
            dyh = dy[:, sl]
            kh, vh = km[:, sl], vm[:, sl]
            s = _dot(q, kh, "nt")
            p = jnp.exp(s - jnp.max(s, axis=1, keepdims=True))
            p = p / jnp.sum(p, axis=1, keepdims=True)
            dp = _dot(dyh, vh, "nt")
            ds = p * (dp - jnp.sum(p * dp, axis=1, keepdims=True))
            dq = _dot(ds, kh) * scale
            dx, dg = _rms_bwd(qm[:, sl], gq, dq)
            dqs.append(dx)
            dgq = dgq + dg
            st = _dot(kh, q, "nt")
            pt = jnp.exp(st - jnp.max(st, axis=0, keepdims=True))
            pt = pt / jnp.sum(pt, axis=0, keepdims=True)
            dpt = _dot(vh, dyh, "nt")
            dst = pt * (dpt - jnp.sum(pt * dpt, axis=0, keepdims=True))
            dvs.append(_dot(pt, dyh))
            dks.append(_dot(dst, q))
        return jnp.concatenate(dqs, axis=1), jnp.concatenate(dks, axis=1), jnp.concatenate(dvs, axis=1), dgq

    m = km.shape[0]
    return _rowwise(fn, "mem_bwd", 512, [(z, MEM_WIDTH, Z_QM // MEM_WIDTH), dy], [km, vm, gq],
                    [(MEM_WIDTH, BF16)], [((m, MEM_WIDTH), F32), ((m, MEM_WIDTH), F32), ((1, LANES), F32)])


def _local_step(x, mem, positions, loss_target, P, W):
    T = x.shape[0]
    G = {}

    half = MLA_ROPE // 2
    inv = ROPE_BASE ** (-jnp.arange(half, dtype=F32) / half)
    ang = positions.astype(F32)[:, None] * inv
    cos, sin = jnp.cos(ang), jnp.sin(ang)
    one, zero = jnp.ones((T, MLA_NOPE), F32), jnp.zeros((T, half), F32)
    pad = LANES - MLA_QK
    tabs = (jnp.concatenate([one, cos, cos, jnp.ones((T, pad), F32)], axis=1),
            jnp.concatenate([jnp.zeros((T, MLA_NOPE), F32), -sin, zero, jnp.zeros((T, pad), F32)], axis=1),
            jnp.concatenate([jnp.zeros((T, MLA_NOPE), F32), zero, sin, jnp.zeros((T, pad), F32)], axis=1))
    gq_p = jnp.pad(P["mla_q_norm"], ((0, 0), (0, pad)))
    gk_p = jnp.pad(P["mla_k_norm"], ((0, 0), (0, pad)))
    bias_full = jnp.repeat(P["sg_b"].T, SG_GROUP_DIM, axis=1)
    group_ind = jnp.repeat(jnp.eye(SG_GROUPS, dtype=F32), SG_GROUP_DIM, axis=0)

    HT = (D_MODEL, BF16, "T")

    def norm2(x, g):
        h = _rms(x, g)
        return h, h

    h1, h1t = _rowwise(norm2, "ffn1_norm", 512, [x], [P["ffn1_norm"]], [(D_MODEL, BF16), HT])
    gu1, a1t, o1 = _ffn_fwd(h1, W["ffn1_w_gu"], W["ffn1_w_down"], "ffn1")

    def resid_norm(x, o, g):
        xn = x + 0.5 * o
        h = _rms(xn, g)
        return xn, h, h

    x1, hm, hmt = _rowwise(resid_norm, "mix_norm", 512, [x, o1], [P["mix_norm"]],
                           [(D_MODEL, F32), (D_MODEL, BF16), HT])
    z = _mm(hm, W["w_in"], "nn", F32, "w_in", tm=1024, tn=768)

    y_a, y_at = _sg_fwd(z, P["sg_ln_g"], P["sg_ln_b"], P["sg_w"], bias_full)

    def c_norm(cq, ckv, gq, gkv):
        a, b = _rms(cq, gq), _rms(ckv, gkv)
        return a, b, a, b

    cqn, ckvn, cqnt, ckvnt = _rowwise(
        c_norm, "mla_cnorm", 512, [(z, MLA_Q_RANK, Z_CQ // MLA_Q_RANK), (z, MLA_KV_RANK, Z_CKV // MLA_KV_RANK)],
        [P["mla_cq_norm"], P["mla_ckv_norm"]],
        [(MLA_Q_RANK, BF16), (MLA_KV_RANK, BF16), (MLA_Q_RANK, BF16, "T"), (MLA_KV_RANK, BF16, "T")])
    q_pre = _mm(cqn, W["mla_w_uq"], "nn", F32, "mla_uq", tm=1024, tn=1024)
    kv_pre = _mm(ckvn, W["mla_w_ukv"], "nn", F32, "mla_ukv", tm=1024, tn=1024)
    q, k, v = _mla_post(q_pre, kv_pre, z, tabs, gq_p, gk_p)
    y_b, y_bt, lse = _flash_fwd(q, k, v)

    memn, = _rowwise(lambda m, g: _rms(m, g), "mem_norm", 256, [mem], [P["mem_norm"]], [(D_MODEL, BF16)])
    kvm = _mm(memn, W["mem_w_kv"], "nn", F32, "mem_kv")

    def mem_k(kvm, gk):
        ks = [_rms(kvm[:, h * LANES:(h + 1) * LANES], gk) for h in range(MEM_HEADS)]
        return jnp.concatenate(ks, axis=1), kvm[:, MEM_WIDTH:]

    km, vm = _rowwise(mem_k, "mem_knorm", 256, [kvm], [P["mem_k_norm"]], [(MEM_WIDTH, BF16), (MEM_WIDTH, BF16)])
    y_c, y_ct = _mem_fwd(z, km, vm, P["mem_q_norm"])

    pa = _mm(y_a, W["w_branch_a"], "nn", F32, "branch_a", tm=1024, tn=1024)
    pb = _mm(y_b, W["w_branch_b"], "nn", F32, "branch_b", tm=1024, tn=1024)
    pc = _mm(y_c, W["w_branch_c"], "nn", F32, "branch_c", tm=1024, tn=1024)

    def merge(zg, pa, pb, pc, b):
        g = _sigmoid(zg + b)
        m = g[:, :D_MODEL] * pa + g[:, D_MODEL:2 * D_MODEL] * pb + g[:, 2 * D_MODEL:] * pc
        return m, m

    merged, mergedt = _rowwise(merge, "merge", 256, [(z, 3 * D_MODEL, 0), pa, pb, pc], [P["b_gate"]],
                               [(D_MODEL, BF16), HT])
    om = _mm(merged, W["w_out"], "nn", F32, "w_out", tm=1024, tn=1024)

    def resid_norm1(x, o, g):
        xn = x + o
        h = _rms(xn, g)
        return xn, h, h

    x2, h2, h2t = _rowwise(resid_norm1, "ffn2_norm", 512, [x1, om], [P["ffn2_norm"]],
                           [(D_MODEL, F32), (D_MODEL, BF16), HT])
    gu2, a2t, o2 = _ffn_fwd(h2, W["ffn2_w_gu"], W["ffn2_w_down"], "ffn2")

    def loss_fn(x2, o2, t):
        e = x2 + 0.5 * o2 - t
        return e * (1.0 / D_MODEL), (e * (0.5 / D_MODEL)).astype(BF16), _rsum(e * e) * (0.5 / D_MODEL)

    dx3, do2, loss_part = _rowwise(loss_fn, "loss", 512, [x2, o2, loss_target], [],
                                   [(D_MODEL, F32), (D_MODEL, BF16)], [((1, D_MODEL), F32)])

    dh2, G["ffn2_w_gu"], G["ffn2_w_down"] = _ffn_bwd(do2, h2t, gu2, a2t, W["ffn2_w_gu"], W["ffn2_w_down"], "ffn2")

    def norm_bwd(x, dh, dxo, g):
        dx, dg = _rms_bwd(x, g, dh)
        dx = dx + dxo
        return dx, dx, dg

    dx2, dx2b, G["ffn2_norm"] = _rowwise(norm_bwd, "ffn2_norm_bwd", 512, [x2, dh2, dx3], [P["ffn2_norm"]],
                                         [(D_MODEL, F32), (D_MODEL, BF16)], [((1, D_MODEL), F32)])

    G["w_out"] = _mm_t(mergedt, dx2b, "w_out_dw", tm=1024, tn=1024)
    dmerged = _mm(dx2b, W["w_out"], "nt", F32, "w_out_dx", tm=1024, tn=1024)

    def merge_bwd(zg, pa, pb, pc, dm, b):
        g = _sigmoid(zg + b)
        ps = jnp.concatenate([pa, pb, pc], axis=1)
        dm3 = jnp.concatenate([dm, dm, dm], axis=1)
        dzg = dm3 * ps * g * (1.0 - g)
        dp = dm3 * g
        return dzg, dp[:, :D_MODEL], dp[:, D_MODEL:2 * D_MODEL], dp[:, 2 * D_MODEL:], _rsum(dzg)

    dzg, dpa, dpb, dpc, G["b_gate"] = _rowwise(
        merge_bwd, "merge_bwd", 256, [(z, 3 * D_MODEL, 0), pa, pb, pc, dmerged], [P["b_gate"]],
        [(3 * D_MODEL, BF16), (D_MODEL, BF16), (D_MODEL, BF16), (D_MODEL, BF16)], [((1, 3 * D_MODEL), F32)])

    G["w_branch_a"] = _mm_t(y_at, dpa, "branch_a_dw", tm=512, tn=1024)
    G["w_branch_b"] = _mm_t(y_bt, dpb, "branch_b_dw", tm=1024, tn=1024)
    G["w_branch_c"] = _mm_t(y_ct, dpc, "branch_c_dw", tm=512, tn=1024)
    dy_a = _mm(dpa, W["w_branch_a"], "nt", BF16, "branch_a_dx", tm=1024, tn=512)
    dy_b = _mm(dpb, W["w_branch_b"], "nt", BF16, "branch_b_dx", tm=1024, tn=1024)
    dy_c = _mm(dpc, W["w_branch_c"], "nt", BF16, "branch_c_dx", tm=1024, tn=512)

    du_pre, dv_pre, G["sg_w"], dbias_t, G["sg_ln_g"], G["sg_ln_b"] = _sg_bwd(
        z, dy_a, P["sg_ln_g"], P["sg_ln_b"], P["sg_w"], bias_full, group_ind)
    G["sg_b"] = dbias_t.T

    dqm, dkm, dvm, G["mem_q_norm"] = _mem_bwd(z, dy_c, km, vm, P["mem_q_norm"])

    def mem_k_bwd(kvm, dkm, dvm, gk):
        dks = []
        dg = jnp.zeros((1, LANES), F32)
        for h in range(MEM_HEADS):
            sl = slice(h * LANES, (h + 1) * LANES)
            dx, d = _rms_bwd(kvm[:, sl], gk, dkm[:, sl])
            dks.append(dx)
            dg = dg + d
        return jnp.concatenate(dks + [dvm], axis=1), dg

    dkvm, G["mem_k_norm"] = _rowwise(mem_k_bwd, "mem_knorm_bwd", 256, [kvm, dkm, dvm], [P["mem_k_norm"]],
                                     [(2 * MEM_WIDTH, BF16)], [((1, LANES), F32)])
    G["mem_w_kv"] = _mm(memn, dkvm, "tn", F32, "mem_kv_dw")
    dmemn = _mm(dkvm, W["mem_w_kv"], "nt", F32, "mem_kv_dx")
    _, G["mem_norm"] = _rowwise(lambda m, d, g: _rms_bwd(m, g, d), "mem_norm_bwd", 256, [mem, dmemn],
                                [P["mem_norm"]], [(D_MODEL, BF16)], [((1, D_MODEL), F32)])

    def delta_fn(o, do):
        od = o.astype(F32) * do.astype(F32)
        ds = [jnp.broadcast_to(jnp.sum(od[:, h * LANES:(h + 1) * LANES], axis=1, keepdims=True), (od.shape[0], LANES))
              for h in range(MLA_HEADS)]
        return jnp.concatenate(ds, axis=1)

    delta, = _rowwise(delta_fn, "mla_delta", 512, [y_b, dy_b], [], [(HP, F32)])
    rowform = lambda a: a.reshape(T, MLA_HEADS, LANES)[:, :, 0].T.reshape(MLA_HEADS, 1, T)
    dq = _flash_dq(q, k, v, dy_b, lse, delta)
    dk, dv = _flash_dkv(q, k, v, dy_b, rowform(lse), rowform(delta))
    dq_pre, dkv_pre, dkr, dgq, dgk = _mla_post_bwd(q_pre, kv_pre, z, tabs, gq_p, gk_p, dq, dk, dv)
    G["mla_q_norm"], G["mla_k_norm"] = dgq[:, :MLA_QK], dgk[:, :MLA_QK]
    G["mla_w_uq"] = _mm_t(cqnt, dq_pre, "mla_uq_dw", tm=384, tn=1024)
    G["mla_w_ukv"] = _mm_t(ckvnt, dkv_pre, "mla_ukv_dw", tm=256, tn=2048)
    dcqn = _mm(dq_pre, W["mla_w_uq"], "nt", F32, "mla_uq_dx", tm=1024)
    dckvn = _mm(dkv_pre, W["mla_w_ukv"], "nt", F32, "mla_ukv_dx", tm=1024)

    def c_norm_bwd(cq, ckv, dcqn, dckvn, gq, gkv):
        dcq, dgq = _rms_bwd(cq, gq, dcqn)
        dckv, dgkv = _rms_bwd(ckv, gkv, dckvn)
        return dcq, dckv, dgq, dgkv

    dcq, dckv, G["mla_cq_norm"], G["mla_ckv_norm"] = _rowwise(
        c_norm_bwd, "mla_cnorm_bwd", 512,
        [(z, MLA_Q_RANK, Z_CQ // MLA_Q_RANK), (z, MLA_KV_RANK, Z_CKV // MLA_KV_RANK), dcqn, dckvn],
        [P["mla_cq_norm"], P["mla_ckv_norm"]], [(MLA_Q_RANK, BF16), (MLA_KV_RANK, BF16)],
        [((1, MLA_Q_RANK), F32), ((1, MLA_KV_RANK), F32)])

    dz = jnp.concatenate([dzg, du_pre, dv_pre, dqm, dckv, dkr, dcq], axis=1)
    G["w_in"] = _mm_t(hmt, dz, "w_in_dw", tm=1024, tn=768)
    dhm = _mm(dz, W["w_in"], "nt", F32, "w_in_dx", tm=1024, tn=1024, tk=2688)

    def norm_bwd_half(x, dh, dxo, g):
        dx, dg = _rms_bwd(x, g, dh)
        dx = dx + dxo
        return dx, (0.5 * dx), dg

    dx1, do1, G["mix_norm"] = _rowwise(norm_bwd_half, "mix_norm_bwd", 512, [x1, dhm, dx2], [P["mix_norm"]],
                                       [(D_MODEL, F32), (D_MODEL, BF16)], [((1, D_MODEL), F32)])
    dh1, G["ffn1_w_gu"], G["ffn1_w_down"] = _ffn_bwd(do1, h1t, gu1, a1t, W["ffn1_w_gu"], W["ffn1_w_down"], "ffn1")

    def norm_bwd_last(x, dh, dxo, g):
        dx, dg = _rms_bwd(x, g, dh)
        return dx + dxo, dg

    grad_x, G["ffn1_norm"] = _rowwise(norm_bwd_last, "ffn1_norm_bwd", 512, [x, dh1, dx1], [P["ffn1_norm"]],
                                      [(D_MODEL, F32)], [((1, D_MODEL), F32)])
    return loss_part, grad_x, G


SHARDED = ["ffn1_w_gu", "ffn1_w_down", "w_in", "mla_w_uq", "mla_w_ukv", "mem_w_kv",
           "w_branch_a", "w_branch_b", "w_branch_c", "w_out", "ffn2_w_gu", "ffn2_w_down"]
ROW_SHARDED = {"ffn1_w_down", "mem_w_kv", "w_out", "ffn2_w_down"}
SMALL = ["ffn1_norm", "mix_norm", "b_gate", "sg_ln_g", "sg_ln_b", "sg_w", "sg_b", "mla_cq_norm",
         "mla_ckv_norm", "mla_q_norm", "mla_k_norm", "mem_norm", "mem_q_norm", "mem_k_norm", "ffn2_norm"]
ORDER = ["ffn1_norm", "ffn1_w_gu", "ffn1_w_down", "mix_norm", "w_in", "b_gate", "sg_ln_g", "sg_ln_b", "sg_w",
         "sg_b", "mla_cq_norm", "mla_w_uq", "mla_ckv_norm", "mla_w_ukv", "mla_q_norm", "mla_k_norm", "mem_norm",
         "mem_w_kv", "mem_q_norm", "mem_k_norm", "w_branch_a", "w_branch_b", "w_branch_c", "w_out", "ffn2_norm",
         "ffn2_w_gu", "ffn2_w_down"]

_IN_U, _IN_V, _IN_CQ, _IN_CKV, _IN_KR, _IN_QM, _IN_G = 0, 512, 1024, 1408, 1664, 1696, 2208
IN_COLS = 5280


def _full_from_slabs(name, slabs):
    n, r, c = slabs.shape
    if name in ROW_SHARDED:
        return slabs.reshape(n * r, c)
    return slabs.transpose(1, 0, 2).reshape(r, n * c)


def _slabs_from_full(name, full):
    if name in ROW_SHARDED:
        return full.reshape(N_DEV, full.shape[0] // N_DEV, full.shape[1])
    r, c = full.shape
    return full.reshape(r, N_DEV, c // N_DEV).transpose(1, 0, 2)


def _compute_layout(full):
    W = dict(full)
    w = full["w_in"]
    kr = jnp.pad(w[:, _IN_KR:_IN_QM], ((0, 0), (KR_LANE, LANES - KR_LANE - MLA_ROPE)))
    W["w_in"] = jnp.concatenate([w[:, _IN_G:], w[:, _IN_U:_IN_CQ], w[:, _IN_QM:_IN_G], w[:, _IN_CKV:_IN_KR], kr,
                                 w[:, _IN_CQ:_IN_CKV]], axis=1)
    uq = full["mla_w_uq"].reshape(MLA_Q_RANK, MLA_HEADS, MLA_QK)
    W["mla_w_uq"] = jnp.pad(uq, ((0, 0), (0, 0), (0, LANES - MLA_QK))).reshape(MLA_Q_RANK, HP)
    ukv = full["mla_w_ukv"].reshape(MLA_KV_RANK, MLA_HEADS, MLA_NOPE + MLA_V)
    padh = lambda a: jnp.pad(a, ((0, 0), (0, 0), (0, LANES - a.shape[2]))).reshape(MLA_KV_RANK, HP)
    W["mla_w_ukv"] = jnp.concatenate([padh(ukv[:, :, :MLA_NOPE]), padh(ukv[:, :, MLA_NOPE:])], axis=1)
    wb = full["w_branch_b"].reshape(MLA_HEADS, MLA_V, D_MODEL)
    W["w_branch_b"] = jnp.pad(wb, ((0, 0), (0, LANES - MLA_V), (0, 0))).reshape(HP, D_MODEL)
    return W


def _reference_layout(G):
    out = dict(G)
    g = G["w_in"]
    out["w_in"] = jnp.concatenate([
        g[:, Z_U:Z_QM], g[:, Z_CQ:Z_COLS], g[:, Z_CKV:Z_KR], g[:, Z_KR + KR_LANE:Z_KR + KR_LANE + MLA_ROPE],
        g[:, Z_QM:Z_CKV], g[:, Z_G:Z_U]], axis=1)
    out["mla_w_uq"] = G["mla_w_uq"].reshape(MLA_Q_RANK, MLA_HEADS, LANES)[:, :, :MLA_QK].reshape(MLA_Q_RANK, -1)
    gk = G["mla_w_ukv"][:, :HP].reshape(MLA_KV_RANK, MLA_HEADS, LANES)[:, :, :MLA_NOPE]
    gv = G["mla_w_ukv"][:, HP:].reshape(MLA_KV_RANK, MLA_HEADS, LANES)[:, :, :MLA_V]
    out["mla_w_ukv"] = jnp.concatenate([gk, gv], axis=2).reshape(MLA_KV_RANK, -1)
    out["w_branch_b"] = G["w_branch_b"].reshape(MLA_HEADS, LANES, D_MODEL)[:, :MLA_V].reshape(-1, D_MODEL)
    return out


def _pack(parts):
    flat = []
    for a in parts:
        a = a.reshape(-1)
        flat.append(jnp.pad(a, (0, (-a.shape[0]) % LANES)))
    return jnp.concatenate(flat).reshape(-1, LANES)


def _unpack(packed, shapes):
    flat = packed.reshape(-1)
    out, off = [], 0
    for shp in shapes:
        n = int(np.prod(shp))
        out.append(flat[off:off + n].reshape(shp))
        off += n + (-n) % LANES
    return out


MESH = pl.DeviceIdType.MESH
HBM = pl.BlockSpec(memory_space=pltpu.HBM)


def _all_gather(shard):
    rows, lanes = shard.shape

    def body(x_ref, out_ref, send_sems, recv_sems, local_sem):
        x, y, c = lax.axis_index("x"), lax.axis_index("y"), lax.axis_index("c")
        me, sibling = (x, y, c), (x, y, 1 - c)
        chips = [(1 - x, y), (x, 1 - y), (1 - x, 1 - y)]

        def slot(px, py, pc):
            return out_ref.at[4 * px + 2 * py + pc]

        def copy(k, block, to, src=None):
            return pltpu.make_async_remote_copy(
                src_ref=slot(*block) if src is None else src, dst_ref=slot(*block),
                send_sem=send_sems.at[k], recv_sem=recv_sems.at[k], device_id=to, device_id_type=MESH)

        mine = pltpu.make_async_copy(x_ref, slot(*me), local_sem)
        mine.start()
        first = [copy(0, me, sibling, src=x_ref)]
        first += [copy(1 + j, me, (*chip, c), src=x_ref) for j, chip in enumerate(chips)]
        for cp in first:
            cp.start()
        passed = [copy(4 + j, (*chip, c), sibling) for j, chip in enumerate(chips)]
        for j, chip in enumerate(chips):
            copy(1 + j, (*chip, c), me).wait_recv()
            passed[j].start()
        copy(0, sibling, me).wait_recv()
        for j, chip in enumerate(chips):
            copy(4 + j, (*chip, 1 - c), me).wait_recv()
        for cp in first + passed:
            cp.wait_send()
        mine.wait()

    return pl.pallas_call(
        body, name="all_gather_weights",
        out_shape=jax.ShapeDtypeStruct((N_DEV, rows, lanes), shard.dtype),
        in_specs=[HBM], out_specs=HBM,
        scratch_shapes=[pltpu.SemaphoreType.DMA((7,)), pltpu.SemaphoreType.DMA((7,)), pltpu.SemaphoreType.DMA],
    )(shard)


def _exchange(big, small):
    def body(big_ref, small_ref, bout_ref, sout_ref, send_sems, recv_sems, local_sems):
        x, y, c = lax.axis_index("x"), lax.axis_index("y"), lax.axis_index("c")
        me = 4 * x + 2 * y + c
        own = [pltpu.make_async_copy(big_ref.at[me], bout_ref.at[me], local_sems.at[0]),
               pltpu.make_async_copy(small_ref, sout_ref.at[me], local_sems.at[1])]
        for cp in own:
            cp.start()
        copies = []
        for k in range(1, N_DEV):
            px = 1 - x if k & 4 else x
            py = 1 - y if k & 2 else y
            pc = 1 - c if k & 1 else c
            peer = 4 * px + 2 * py + pc
            copies.append(pltpu.make_async_remote_copy(
                src_ref=big_ref.at[peer], dst_ref=bout_ref.at[me], send_sem=send_sems.at[k - 1],
                recv_sem=recv_sems.at[k - 1], device_id=(px, py, pc), device_id_type=MESH))
            copies.append(pltpu.make_async_remote_copy(
                src_ref=small_ref, dst_ref=sout_ref.at[me], send_sem=send_sems.at[7 + k - 1],
                recv_sem=recv_sems.at[7 + k - 1], device_id=(px, py, pc), device_id_type=MESH))
        for cp in copies:
            cp.start()
        for cp in copies:
            cp.wait()
        for cp in own:
            cp.wait()

    return pl.pallas_call(
        body, name="exchange_grads",
        out_shape=[jax.ShapeDtypeStruct(big.shape, big.dtype),
                   jax.ShapeDtypeStruct((N_DEV,) + small.shape, small.dtype)],
        in_specs=[HBM, HBM], out_specs=[HBM, HBM],
        scratch_shapes=[pltpu.SemaphoreType.DMA((14,)), pltpu.SemaphoreType.DMA((14,)),
                        pltpu.SemaphoreType.DMA((2,))],
    )(big, small)


def _sum_slots(recv, name, tr):
    n, rows, lanes = recv.shape
    tr = _tile(rows, tr)

    def body(r_ref, o_ref):
        acc = r_ref[0].astype(F32)
        for i in range(1, n):
            acc = acc + r_ref[i].astype(F32)
        o_ref[...] = acc

    return pl.pallas_call(
        body, name=name, grid=(rows // tr,),
        in_specs=[pl.BlockSpec((n, tr, lanes), lambda i: (0, i, 0))],
        out_specs=pl.BlockSpec((tr, lanes), lambda i: (i, 0)),
        out_shape=jax.ShapeDtypeStruct((rows, lanes), F32),
        compiler_params=_cparams(("parallel",)),
    )(recv)


def _adamw(w, g, m, v, name, tr=256):
    c1 = 1.0 - ADAM_B1 ** ADAM_STEP
    c2 = 1.0 - ADAM_B2 ** ADAM_STEP

    def fn(w, g, m, v):
        m = ADAM_B1 * m + (1.0 - ADAM_B1) * g
        v = ADAM_B2 * v + (1.0 - ADAM_B2) * (g * g)
        delta = -ADAM_LR * ((m / c1) / (jnp.sqrt(v / c2) + ADAM_EPS) + ADAM_WD * w)
        return delta, m, v

    return _rowwise(fn, name, tr, [w, g, m, v], [], [(w.shape[1], F32)] * 3)


def kernel(x, mem, positions, ffn1_norm, ffn1_w_gu, ffn1_w_down, mix_norm, w_in, b_gate, sg_ln_g, sg_ln_b, sg_w, sg_b, mla_cq_norm, mla_w_uq, mla_ckv_norm, mla_w_ukv, mla_q_norm, mla_k_norm, mem_norm, mem_w_kv, mem_q_norm, mem_k_norm, w_branch_a, w_branch_b, w_branch_c, w_out, ffn2_norm, ffn2_w_gu, ffn2_w_down, loss_target, m_ffn1_norm, m_ffn1_w_gu, m_ffn1_w_down, m_mix_norm, m_w_in, m_b_gate, m_sg_ln_g, m_sg_ln_b, m_sg_w, m_sg_b, m_mla_cq_norm, m_mla_w_uq, m_mla_ckv_norm, m_mla_w_ukv, m_mla_q_norm, m_mla_k_norm, m_mem_norm, m_mem_w_kv, m_mem_q_norm, m_mem_k_norm, m_w_branch_a, m_w_branch_b, m_w_branch_c, m_w_out, m_ffn2_norm, m_ffn2_w_gu, m_ffn2_w_down, v_ffn1_norm, v_ffn1_w_gu, v_ffn1_w_down, v_mix_norm, v_w_in, v_b_gate, v_sg_ln_g, v_sg_ln_b, v_sg_w, v_sg_b, v_mla_cq_norm, v_mla_w_uq, v_mla_ckv_norm, v_mla_w_ukv, v_mla_q_norm, v_mla_k_norm, v_mem_norm, v_mem_w_kv, v_mem_q_norm, v_mem_k_norm, v_w_branch_a, v_w_branch_b, v_w_branch_c, v_w_out, v_ffn2_norm, v_ffn2_w_gu, v_ffn2_w_down):
    given = dict(locals())
    wts = {n: given[n] for n in ORDER}
    mom = {n: given["m_" + n] for n in ORDER}
    var = {n: given["v_" + n] for n in ORDER}
    shard_shapes = {n: wts[n].shape[1:] for n in SHARDED}
    shard_rows = {n: int(np.prod(shard_shapes[n])) // LANES for n in SHARDED}
    row_off, off = {}, 0
    for n in SHARDED:
        row_off[n] = off
        off += shard_rows[n]

    gathered = _all_gather(jnp.concatenate(
        [wts[n][0].astype(BF16).reshape(shard_rows[n], LANES) for n in SHARDED], axis=0))
    full = {n: _full_from_slabs(n, gathered[:, row_off[n]:row_off[n] + shard_rows[n]].reshape(N_DEV, *shard_shapes[n]))
            for n in SHARDED}
    W = _compute_layout(full)
    P = {n: wts[n] if wts[n].ndim == 2 else wts[n][0] for n in SMALL}

    loss_part, grad_x, G = _local_step(x[0], mem[0], positions[0], loss_target[0], P, W)
    G = _reference_layout(G)

    big = jnp.concatenate(
        [_slabs_from_full(n, G[n]).astype(BF16).reshape(N_DEV, shard_rows[n], LANES) for n in SHARDED], axis=1)
    small_shapes = [wts[n].shape[1:] for n in SMALL]
    small = _pack([G[n].reshape(s) for n, s in zip(SMALL, small_shapes)])
    small = jnp.pad(small, ((0, (-small.shape[0]) % 8), (0, 0)))
    big_recv, small_recv = _exchange(big, small)
    g_big_packed = _sum_slots(big_recv, "sum_big", 1024)
    g_small_packed = _sum_slots(small_recv, "sum_small", 2048)
    g_small = _unpack(g_small_packed, small_shapes)
    grads = {n: g_big_packed[row_off[n]:row_off[n] + shard_rows[n]].reshape(shard_shapes[n]) for n in SHARDED}
    grads.update(zip(SMALL, g_small))

    delta, new_m, new_v = {}, {}, {}
    for n in SHARDED:
        delta[n], new_m[n], new_v[n] = _adamw(wts[n][0], grads[n], mom[n][0], var[n][0], "adamw_" + n)

    def pack_small(d):
        p = _pack([d[n].reshape(s) for n, s in zip(SMALL, small_shapes)])
        return jnp.pad(p, ((0, (-p.shape[0]) % 8), (0, 0)))

    ds, ms, vs = _adamw(pack_small(wts), g_small_packed, pack_small(mom), pack_small(var), "adamw_small", tr=2048)
    for dst, packed in ((delta, ds), (new_m, ms), (new_v, vs)):
        dst.update(zip(SMALL, _unpack(packed, small_shapes)))

    loss = lax.psum(jnp.sum(loss_part), ("x", "y", "c"))
    lead = lambda d: [d[n].reshape(wts[n].shape) for n in ORDER]
    return (loss, grad_x[None], *lead(grads), *lead(delta), *lead(new_m), *lead(new_v))
```

```python
import functools

import numpy as np
import jax
import jax.numpy as jnp
from jax import lax
from jax.experimental import pallas as pl
from jax.experimental.pallas import tpu as pltpu

F32, BF16 = jnp.float32, jnp.bfloat16

D_MODEL = 1024
SG_GROUPS, SG_GROUP_DIM, SG_WIDTH, CHUNK = 8, 64, 512, 128
MLA_HEADS, MLA_NOPE, MLA_ROPE, MLA_V, MLA_QK = 8, 64, 32, 64, 96
MLA_Q_RANK, MLA_KV_RANK = 384, 256
MEM_HEADS, MEM_HEAD_DIM, MEM_WIDTH = 4, 128, 512
D_FF = 2816
ROPE_BASE = 10000.0
EPS = 1e-6
NEG = -1e30
ADAM_LR, ADAM_B1, ADAM_B2, ADAM_EPS, ADAM_WD, ADAM_STEP = 0.001, 0.9, 0.999, 1e-08, 0.01, 10

N_DEV = 8
LANES = 128
V7X_VMEM_LIMIT = 56 * 1024 * 1024
HP = MLA_HEADS * LANES

Z_G, Z_U, Z_V, Z_QM, Z_CKV, Z_KR, Z_CQ = 0, 3072, 3584, 4096, 4608, 4864, 4992
Z_COLS = 5376
KR_LANE = 64


def _tile(dim, pref):
    if dim <= pref:
        return dim
    for t in range(pref - pref % LANES, LANES - 1, -LANES):
        if dim % t == 0:
            return t
    for t in range(pref - pref % 8, 7, -8):
        if dim % t == 0:
            return t
    return dim


def _cparams(sem):
    return pltpu.CompilerParams(dimension_semantics=sem, vmem_limit_bytes=V7X_VMEM_LIMIT)


_DN = {"nn": ((1,), (0,)), "nt": ((1,), (1,)), "tn": ((0,), (0,))}


def _dot(a, b, mode="nn"):
    return lax.dot_general(a.astype(BF16), b.astype(BF16), (_DN[mode], ((), ())),
                           preferred_element_type=F32)


def _mm(a, b, mode, out_dtype, name, tm=512, tn=512, tk=2048):
    if mode == "tn":
        K, M = a.shape
    else:
        M, K = a.shape
    N = b.shape[0] if mode == "nt" else b.shape[1]
    tm, tn, tk = _tile(M, tm), _tile(N, tn), _tile(K, tk)
    nk = K // tk
    if mode == "tn":
        a_spec = pl.BlockSpec((tk, tm), lambda i, j, k: (k, i))
    else:
        a_spec = pl.BlockSpec((tm, tk), lambda i, j, k: (i, k))
    if mode == "nt":
        b_spec = pl.BlockSpec((tn, tk), lambda i, j, k: (j, k))
    else:
        b_spec = pl.BlockSpec((tk, tn), lambda i, j, k: (k, j))

    def body(a_ref, b_ref, o_ref, *scratch):
        p = _dot(a_ref[...], b_ref[...], mode)
        if nk == 1:
            o_ref[...] = p.astype(o_ref.dtype)
        else:
            acc_ref, = scratch
            k = pl.program_id(2)

            @pl.when(k == 0)
            def _():
                acc_ref[...] = p

            @pl.when(k > 0)
            def _():
                acc_ref[...] += p

            @pl.when(k == nk - 1)
            def _():
                o_ref[...] = acc_ref[...].astype(o_ref.dtype)

    return pl.pallas_call(
        body, name=name, grid=(M // tm, N // tn, nk),
        in_specs=[a_spec, b_spec],
        out_specs=pl.BlockSpec((tm, tn), lambda i, j, k: (i, j)),
        out_shape=jax.ShapeDtypeStruct((M, N), out_dtype),
        scratch_shapes=[] if nk == 1 else [pltpu.VMEM((tm, tn), F32)],
        compiler_params=_cparams(("parallel", "parallel", "arbitrary")),
    )(a, b)


def _mm_t(at, b, name, tm, tn, tk=1024):
    return _mm(at, b, "nn", F32, name, tm=tm, tn=tn, tk=tk)


def _rowwise(fn, name, tr, row_ins, bc_ins, row_outs, acc_outs=()):
    norm = [it if isinstance(it, tuple) else (it, it.shape[1], 0) for it in row_ins]
    rows = norm[0][0].shape[0]
    tr = _tile(rows, tr)
    arrays, in_specs = [], []
    for arr, w, cb in norm:
        arrays.append(arr)
        in_specs.append(pl.BlockSpec((tr, w), lambda i, cb=cb: (i, cb)))
    for arr in bc_ins:
        arrays.append(arr)
        in_specs.append(pl.BlockSpec(arr.shape, lambda i, nd=arr.ndim: (0,) * nd))
    out_shape, out_specs = [], []
    transposed = [len(o) == 3 for o in row_outs]
    for (w, dt, *_), t in zip(row_outs, transposed):
        out_shape.append(jax.ShapeDtypeStruct((w, rows) if t else (rows, w), dt))
        out_specs.append(pl.BlockSpec((w, tr), lambda i: (0, i)) if t else pl.BlockSpec((tr, w), lambda i: (i, 0)))
    for shp, dt in acc_outs:
        out_shape.append(jax.ShapeDtypeStruct(shp, dt))
        out_specs.append(pl.BlockSpec(shp, lambda i, nd=len(shp): (0,) * nd))
    n_in, n_row = len(arrays), len(row_outs)

    def body(*refs):
        vals = fn(*[r[...] for r in refs[:n_in]])
        if not isinstance(vals, (tuple, list)):
            vals = (vals,)
        outs = refs[n_in:]
        for r, v, t in zip(outs[:n_row], vals[:n_row], transposed):
            r[...] = v.astype(F32).T.astype(r.dtype) if t else v.astype(r.dtype)
        if acc_outs:
            accs = list(zip(outs[n_row:], vals[n_row:]))
            i = pl.program_id(0)

            @pl.when(i == 0)
            def _():
                for r, v in accs:
                    r[...] = v.astype(r.dtype)

            @pl.when(i > 0)
            def _():
                for r, v in accs:
                    r[...] += v.astype(r.dtype)

    res = pl.pallas_call(
        body, name=name, grid=(rows // tr,), in_specs=in_specs, out_specs=out_specs,
        out_shape=out_shape, compiler_params=_cparams(("arbitrary",)),
    )(*arrays)
    return res


def _rsum(x):
    return jnp.sum(x, axis=0, keepdims=True)


def _rms(x, g, n=None):
    n = x.shape[-1] if n is None else n
    r = lax.rsqrt(jnp.sum(x * x, axis=-1, keepdims=True) * (1.0 / n) + EPS)
    return x * r * g


def _rms_bwd(x, g, dy, n=None):
    n = x.shape[-1] if n is None else n
    r = lax.rsqrt(jnp.sum(x * x, axis=-1, keepdims=True) * (1.0 / n) + EPS)
    xh = x * r
    dxh = dy * g
    dx = r * (dxh - xh * (jnp.sum(dxh * xh, axis=-1, keepdims=True) * (1.0 / n)))
    return dx, _rsum(dy * xh)


def _gelu(x):
    return 0.5 * x * (1.0 + lax.erf(x * 0.7071067811865476))


def _gelu_grad(x):
    return 0.5 * (1.0 + lax.erf(x * 0.7071067811865476)) + x * jnp.exp(-0.5 * x * x) * 0.3989422804014327


def _sigmoid(x):
    return 1.0 / (1.0 + jnp.exp(-x))


def _ffn_fwd(h, w_gu, w_down, tag):
    gu = _mm(h, w_gu, "nn", BF16, f"{tag}_gu", tm=1024, tn=512)

    def act(gu):
        g = gu[:, :D_FF].astype(F32)
        u = gu[:, D_FF:].astype(F32)
        a = g * _sigmoid(g) * u
        return a, a

    a, at = _rowwise(act, f"{tag}_act", 256, [gu], [], [(D_FF, BF16), (D_FF, BF16, "T")])
    o = _mm(a, w_down, "nn", F32, f"{tag}_down", tm=1024, tn=512, tk=2816)
    return gu, at, o


def _ffn_bwd(do, ht, gu, at, w_gu, w_down, tag):
    dw_down = _mm_t(at, do, f"{tag}_dwdown", tm=1408, tn=1024)
    da = _mm(do, w_down, "nt", BF16, f"{tag}_da", tm=1024, tn=512)

    def act_bwd(gu, da):
        g = gu[:, :D_FF].astype(F32)
        u = gu[:, D_FF:].astype(F32)
        da = da.astype(F32)
        s = _sigmoid(g)
        dg = da * u * s * (1.0 + g * (1.0 - s))
        du = da * g * s
        return jnp.concatenate([dg, du], axis=1)

    dgu, = _rowwise(act_bwd, f"{tag}_actbwd", 256, [gu, da], [], [(2 * D_FF, BF16)])
    dw_gu = _mm_t(ht, dgu, f"{tag}_dwgu", tm=1024, tn=1408)
    dh = _mm(dgu, w_gu, "nt", F32, f"{tag}_dh", tm=1024, tn=512, tk=2816)
    return dh, dw_gu, dw_down


def _sg_common(u_pre, v_pre, ln_g, ln_b):
    u = _gelu(u_pre)
    v = _gelu(v_pre)
    mu = jnp.mean(v, axis=-1, keepdims=True)
    vc = v - mu
    rstd = lax.rsqrt(jnp.mean(vc * vc, axis=-1, keepdims=True) + EPS)
    vhat = vc * rstd
    vl = vhat * ln_g + ln_b
    return u, vhat, rstd, vl


def _sg_masked_pairs(w):
    t = lax.broadcasted_iota(jnp.int32, (CHUNK, CHUNK), 0)
    s = lax.broadcasted_iota(jnp.int32, (CHUNK, CHUNK), 1)
    causal = s <= t
    wm = [jnp.where(causal, w[g], 0.0).astype(BF16) for g in range(SG_GROUPS)]
    return [jnp.concatenate([wm[2 * j], wm[2 * j + 1]], axis=0) for j in range(SG_GROUPS // 2)], causal


def _sg_mix(vl, pairs, bias):
    tr = vl.shape[0]
    low = lax.broadcasted_iota(jnp.int32, (CHUNK, LANES), 1) < SG_GROUP_DIM
    vb = vl.astype(BF16)
    rows = []
    for c in range(tr // CHUNK):
        slabs = []
        for j in range(SG_GROUPS // 2):
            slab = vb[c * CHUNK:(c + 1) * CHUNK, j * LANES:(j + 1) * LANES]
            m = _dot(pairs[j], slab)
            slabs.append(jnp.where(low, m[:CHUNK], m[CHUNK:]))
        rows.append(jnp.concatenate(slabs, axis=1) + bias)
    return jnp.concatenate(rows, axis=0)


def _sg_fwd(z, ln_g, ln_b, sg_w, bias_full):
    def fn(u_pre, v_pre, ln_g, ln_b, w, bias):
        u, _, _, vl = _sg_common(u_pre, v_pre, ln_g, ln_b)
        pairs, _ = _sg_masked_pairs(w)
        y = u * _sg_mix(vl, pairs, bias)
        return y, y

    return _rowwise(fn, "sg_fwd", 512, [(z, SG_WIDTH, Z_U // SG_WIDTH), (z, SG_WIDTH, Z_V // SG_WIDTH)],
                    [ln_g, ln_b, sg_w, bias_full], [(SG_WIDTH, BF16), (SG_WIDTH, BF16, "T")])


def _sg_bwd(z, dy, ln_g, ln_b, sg_w, bias_full, group_ind):
    def fn(u_pre, v_pre, dy, ln_g, ln_b, w, bias, ind):
        dy = dy.astype(F32)
        u, vhat, rstd, vl = _sg_common(u_pre, v_pre, ln_g, ln_b)
        pairs, causal = _sg_masked_pairs(w)
        mixed = _sg_mix(vl, pairs, bias)
        du_pre = dy * mixed * _gelu_grad(u_pre)
        dmix = dy * u
        tr = dy.shape[0]
        low = lax.broadcasted_iota(jnp.int32, (CHUNK, LANES), 1) < SG_GROUP_DIM
        vb = vl.astype(BF16)
        dw = [jnp.zeros((CHUNK, CHUNK), F32) for _ in range(SG_GROUPS)]
        dbias = jnp.zeros((CHUNK, SG_WIDTH), F32)
        dvl_rows = []
        for c in range(tr // CHUNK):
            dm_c = dmix[c * CHUNK:(c + 1) * CHUNK]
            dbias = dbias + dm_c
            slabs = []
            for j in range(SG_GROUPS // 2):
                slab = vb[c * CHUNK:(c + 1) * CHUNK, j * LANES:(j + 1) * LANES]
                dm = dm_c[:, j * LANES:(j + 1) * LANES]
                d0 = jnp.where(low, dm, 0.0).astype(BF16)
                d1 = jnp.where(low, 0.0, dm).astype(BF16)
                dw[2 * j] = dw[2 * j] + _dot(d0, slab, "nt")
                dw[2 * j + 1] = dw[2 * j + 1] + _dot(d1, slab, "nt")
                slabs.append(_dot(pairs[j], jnp.concatenate([d0, d1], axis=0), "tn"))
            dvl_rows.append(jnp.concatenate(slabs, axis=1))
        dvl = jnp.concatenate(dvl_rows, axis=0)
        dln_g = _rsum(dvl * vhat)
        dln_b = _rsum(dvl)
        dvh = dvl * ln_g
        dv = rstd * (dvh - jnp.mean(dvh, axis=-1, keepdims=True)
                     - vhat * jnp.mean(dvh * vhat, axis=-1, keepdims=True))
        dv_pre = dv * _gelu_grad(v_pre)
        dw = jnp.stack([jnp.where(causal, d, 0.0) for d in dw], axis=0)
        dbias_t = lax.dot_general(dbias, ind, (((1,), (0,)), ((), ())), precision=lax.Precision.HIGHEST,
                                  preferred_element_type=F32)
        return du_pre, dv_pre, dw, dbias_t, dln_g, dln_b

    return _rowwise(fn, "sg_bwd", 512,
                    [(z, SG_WIDTH, Z_U // SG_WIDTH), (z, SG_WIDTH, Z_V // SG_WIDTH), dy],
                    [ln_g, ln_b, sg_w, bias_full, group_ind],
                    [(SG_WIDTH, BF16), (SG_WIDTH, BF16)],
                    [((SG_GROUPS, CHUNK, CHUNK), F32), ((CHUNK, SG_GROUPS), F32), ((1, SG_WIDTH), F32), ((1, SG_WIDTH), F32)])


def _rope(x, c, s1, s2):
    return x * c + pltpu.roll(x, LANES - MLA_ROPE // 2, 1) * s1 + pltpu.roll(x, MLA_ROPE // 2, 1) * s2


def _rope_t(d, c, s1, s2):
    return d * c + pltpu.roll(d * s1, MLA_ROPE // 2, 1) + pltpu.roll(d * s2, LANES - MLA_ROPE // 2, 1)


def _mla_post(q_pre, kv_pre, z, tabs, gq, gk):
    scale = MLA_QK ** -0.5 * LOG2E

    def fn(q_pre, k_pre, v_pre, kr, c, s1, s2, gq, gk):
        qs, ks = [], []
        for h in range(MLA_HEADS):
            sl = slice(h * LANES, (h + 1) * LANES)
            qs.append(_rope(_rms(q_pre[:, sl], gq, MLA_QK), c, s1, s2) * scale)
            ks.append(_rope(_rms(k_pre[:, sl] + kr, gk, MLA_QK), c, s1, s2))
        lane = lax.broadcasted_iota(jnp.int32, v_pre.shape, 1) & (LANES - 1)
        return jnp.concatenate(qs, axis=1), jnp.concatenate(ks, axis=1), jnp.where(lane == ONES_LANE, 1.0, v_pre)

    return _rowwise(fn, "mla_post", 256,
                    [q_pre, (kv_pre, HP, 0), (kv_pre, HP, 1), (z, LANES, Z_KR // LANES), *tabs],
                    [gq, gk], [(HP, BF16)] * 3)


def _mla_post_bwd(q_pre, kv_pre, z, tabs, gq, gk, dq, dk, dv):
    scale = MLA_QK ** -0.5

    def fn(q_pre, k_pre, kr, c, s1, s2, dq, dk, dv, gq, gk):
        lane = lax.broadcasted_iota(jnp.int32, (1, LANES), 1)
        kr_mask = (lane >= KR_LANE) & (lane < KR_LANE + MLA_ROPE)
        dqs, dks = [], []
        dgq = jnp.zeros((1, LANES), F32)
        dgk = jnp.zeros((1, LANES), F32)
        dkr = jnp.zeros(kr.shape, F32)
        for h in range(MLA_HEADS):
            sl = slice(h * LANES, (h + 1) * LANES)
            dqn = _rope_t(dq[:, sl].astype(F32), c, s1, s2) * scale
            dx, dg = _rms_bwd(q_pre[:, sl], gq, dqn, MLA_QK)
            dqs.append(dx)
            dgq = dgq + dg
            dkn = _rope_t(dk[:, sl].astype(F32), c, s1, s2)
            dx, dg = _rms_bwd(k_pre[:, sl] + kr, gk, dkn, MLA_QK)
            dks.append(dx)
            dgk = dgk + dg
            dkr = dkr + dx
        dkr = jnp.where(kr_mask, dkr, 0.0)
        dkv = jnp.concatenate(dks + [dv.astype(F32)], axis=1)
        return jnp.concatenate(dqs, axis=1), dkv, dkr, dgq, dgk

    return _rowwise(fn, "mla_post_bwd", 256,
                    [q_pre, (kv_pre, HP, 0), (z, LANES, Z_KR // LANES), *tabs, dq, dk, dv],
                    [gq, gk], [(HP, BF16), (2 * HP, BF16), (LANES, BF16)],
                    [((1, LANES), F32), ((1, LANES), F32)])


def _pairs(n, lower):
    a, b = [], []
    for o in range(n):
        inner = range(o + 1) if lower else range(o, n)
        for t in inner:
            a.append(o)
            b.append(t)
    return jnp.asarray(np.array(a, np.int32)), jnp.asarray(np.array(b, np.int32))


FLASH_TILE, FLASH_SUB_ROWS = 1024, 512
LOG2E, LN2 = 1.4426950408889634, 0.6931471805599453
ONES_LANE = MLA_V


def _flash_tiles(T):
    tq = _tile(T, FLASH_TILE)
    return tq, _tile(tq, FLASH_SUB_ROWS)


def _col_span(t, sr, rb, diag, key_major):
    if not diag:
        return 0, t
    return (rb * sr, t) if key_major else (0, (rb + 1) * sr)


def _span_iota(sr, rb, c0, c1):
    r = lax.broadcasted_iota(jnp.int32, (sr, c1 - c0), 0) + rb * sr
    c = lax.broadcasted_iota(jnp.int32, (sr, c1 - c0), 1) + c0
    return r, c


def _lanes(x, width):
    return jnp.concatenate([x] * (width // LANES), axis=1)


def _flash_fwd(q, k, v):
    T = q.shape[0]
    tq, sr = _flash_tiles(T)
    n = T // tq
    ii, jj = _pairs(n, True)

    def body(ii_ref, jj_ref, q_ref, k_ref, v_ref, o_ref, ot_ref, lse_ref, m_sc, acc_sc):
        p_ = pl.program_id(1)
        i, j = ii_ref[p_], jj_ref[p_]

        @pl.when(j == 0)
        def _():
            m_sc[...] = jnp.full(m_sc.shape, NEG, F32)
            acc_sc[...] = jnp.zeros(acc_sc.shape, F32)

        def tile(diag):
            for rb in range(tq // sr):
                rows = slice(rb * sr, (rb + 1) * sr)
                c0, c1 = _col_span(tq, sr, rb, diag, False)
                s = _dot(q_ref[rows, :], k_ref[c0:c1, :], "nt")
                if diag:
                    r, c = _span_iota(sr, rb, c0, c1)
                    s = jnp.where(c <= r, s, NEG)
                m = m_sc[rows, :]
                m_new = jnp.maximum(m, jnp.max(s, axis=1, keepdims=True))
                p = jnp.exp2(s - _lanes(m_new, c1 - c0))
                acc_sc[rows, :] = jnp.exp2(m - m_new) * acc_sc[rows, :] + _dot(p, v_ref[c0:c1, :])
                m_sc[rows, :] = m_new

        @pl.when(j < i)
        def _():
            tile(False)

        @pl.when(j == i)
        def _():
            tile(True)
            acc = acc_sc[...]
            lane = lax.broadcasted_iota(jnp.int32, acc.shape, 1)
            l = jnp.sum(jnp.where(lane == ONES_LANE, acc, 0.0), axis=1, keepdims=True)
            o = jnp.where(lane < MLA_V, acc / l, 0.0)
            o_ref[...] = o.astype(o_ref.dtype)
            ot_ref[...] = o.T.astype(ot_ref.dtype)
            lse_ref[...] = m_sc[...] + jnp.log2(l)

    blk = lambda which: pl.BlockSpec((tq, LANES), which)
    qmap = lambda h, p, ii, jj: (ii[p], h)
    kmap = lambda h, p, ii, jj: (jj[p], h)
    return pl.pallas_call(
        body, name="mla_flash_fwd",
        grid_spec=pltpu.PrefetchScalarGridSpec(
            num_scalar_prefetch=2, grid=(MLA_HEADS, int(ii.shape[0])),
            in_specs=[blk(qmap), blk(kmap), blk(kmap)],
            out_specs=[blk(qmap), pl.BlockSpec((LANES, tq), lambda h, p, ii, jj: (h, ii[p])), blk(qmap)],
            scratch_shapes=[pltpu.VMEM((tq, LANES), F32)] * 2),
        out_shape=[jax.ShapeDtypeStruct((T, HP), BF16), jax.ShapeDtypeStruct((HP, T), BF16),
                   jax.ShapeDtypeStruct((T, HP), F32)],
        compiler_params=_cparams(("parallel", "arbitrary")),
    )(ii, jj, q, k, v)


def _flash_dq(q, k, v, do, lse, delta):
    T = q.shape[0]
    tq, sr = _flash_tiles(T)
    n = T // tq
    ii, jj = _pairs(n, True)

    def body(ii_ref, jj_ref, q_ref, k_ref, v_ref, do_ref, lse_ref, dl_ref, dq_ref, acc_sc):
        p_ = pl.program_id(1)
        i, j = ii_ref[p_], jj_ref[p_]

        @pl.when(j == 0)
        def _():
            acc_sc[...] = jnp.zeros(acc_sc.shape, F32)

        def tile(diag):
            for rb in range(tq // sr):
                rows = slice(rb * sr, (rb + 1) * sr)
                c0, c1 = _col_span(tq, sr, rb, diag, False)
                ks = k_ref[c0:c1, :]
                p = jnp.exp2(_dot(q_ref[rows, :], ks, "nt") - _lanes(lse_ref[rows, :], c1 - c0))
                if diag:
                    r, c = _span_iota(sr, rb, c0, c1)
                    p = jnp.where(c <= r, p, 0.0)
                dp = _dot(do_ref[rows, :], v_ref[c0:c1, :], "nt")
                acc_sc[rows, :] += _dot(p * (dp - _lanes(dl_ref[rows, :], c1 - c0)), ks)

        @pl.when(j < i)
        def _():
            tile(False)

        @pl.when(j == i)
        def _():
            tile(True)
            dq_ref[...] = acc_sc[...]

    blk = lambda which: pl.BlockSpec((tq, LANES), which)
    qmap = lambda h, p, ii, jj: (ii[p], h)
    kmap = lambda h, p, ii, jj: (jj[p], h)
    return pl.pallas_call(
        body, name="mla_flash_dq",
        grid_spec=pltpu.PrefetchScalarGridSpec(
            num_scalar_prefetch=2, grid=(MLA_HEADS, int(ii.shape[0])),
            in_specs=[blk(qmap), blk(kmap), blk(kmap), blk(qmap), blk(qmap), blk(qmap)],
            out_specs=blk(qmap),
            scratch_shapes=[pltpu.VMEM((tq, LANES), F32)]),
        out_shape=jax.ShapeDtypeStruct((T, HP), F32),
        compiler_params=_cparams(("parallel", "arbitrary")),
    )(ii, jj, q, k, v, do, lse, delta)


def _flash_dkv(q, k, v, do, lse_row, delta_row):
    T = q.shape[0]
    tq, sr = _flash_tiles(T)
    n = T // tq
    jj, ii = _pairs(n, False)

    def body(jj_ref, ii_ref, q_ref, k_ref, v_ref, do_ref, lse_ref, dl_ref, dk_ref, dv_ref, dk_sc, dv_sc):
        p_ = pl.program_id(1)
        j, i = jj_ref[p_], ii_ref[p_]

        @pl.when(i == j)
        def _():
            dk_sc[...] = jnp.zeros(dk_sc.shape, F32)
            dv_sc[...] = jnp.zeros(dv_sc.shape, F32)

        def tile(diag):
            for rb in range(tq // sr):
                rows = slice(rb * sr, (rb + 1) * sr)
                c0, c1 = _col_span(tq, sr, rb, diag, True)
                qs, dos = q_ref[c0:c1, :], do_ref[c0:c1, :]
                pt = jnp.exp2(_dot(k_ref[rows, :], qs, "nt") - lse_ref[:, c0:c1])
                if diag:
                    r, c = _span_iota(sr, rb, c0, c1)
                    pt = jnp.where(r <= c, pt, 0.0)
                dpt = _dot(v_ref[rows, :], dos, "nt")
                dv_sc[rows, :] += _dot(pt, dos)
                dk_sc[rows, :] += _dot(pt * (dpt - dl_ref[:, c0:c1]), qs)

        @pl.when(i == j)
        def _():
            tile(True)

        @pl.when(i > j)
        def _():
            tile(False)

        @pl.when(i == n - 1)
        def _():
            dk_ref[...] = dk_sc[...] * LN2
            dv_ref[...] = dv_sc[...]

    blk = lambda which: pl.BlockSpec((tq, LANES), which)
    qmap = lambda h, p, jj, ii: (ii[p], h)
    kmap = lambda h, p, jj, ii: (jj[p], h)
    row = pl.BlockSpec((None, 1, tq), lambda h, p, jj, ii: (h, 0, ii[p]))
    return pl.pallas_call(
        body, name="mla_flash_dkv",
        grid_spec=pltpu.PrefetchScalarGridSpec(
            num_scalar_prefetch=2, grid=(MLA_HEADS, int(ii.shape[0])),
            in_specs=[blk(qmap), blk(kmap), blk(kmap), blk(qmap), row, row],
            out_specs=[blk(kmap), blk(kmap)],
            scratch_shapes=[pltpu.VMEM((tq, LANES), F32)] * 2),
        out_shape=[jax.ShapeDtypeStruct((T, HP), F32)] * 2,
        compiler_params=_cparams(("parallel", "arbitrary")),
    )(jj, ii, q, k, v, do, lse_row, delta_row)


def _mem_fwd(z, km, vm, gq):
    scale = MEM_HEAD_DIM ** -0.5

    def fn(qm, km, vm, gq):
        ys = []
        for h in range(MEM_HEADS):
            sl = slice(h * LANES, (h + 1) * LANES)
            q = _rms(qm[:, sl], gq) * scale
            s = _dot(q, km[:, sl], "nt")
            p = jnp.exp(s - jnp.max(s, axis=1, keepdims=True))
            p = p / jnp.sum(p, axis=1, keepdims=True)
            ys.append(_dot(p, vm[:, sl]))
        y = jnp.concatenate(ys, axis=1)
        return y, y

    return _rowwise(fn, "mem_fwd", 512, [(z, MEM_WIDTH, Z_QM // MEM_WIDTH)], [km, vm, gq],
                    [(MEM_WIDTH, BF16), (MEM_WIDTH, BF16, "T")])


def _mem_bwd(z, dy, km, vm, gq):
    scale = MEM_HEAD_DIM ** -0.5

    def fn(qm, dy, km, vm, gq):
        dqs, dks, dvs = [], [], []
        dgq = jnp.zeros((1, LANES), F32)
        for h in range(MEM_HEADS):
            sl = slice(h * LANES, (h + 1) * LANES)
            q = (_rms(qm[:, sl], gq) * scale).astype(BF16)
            dyh = dy[:, sl]
            kh, vh = km[:, sl], vm[:, sl]
            s = _dot(q, kh, "nt")
            p = jnp.exp(s - jnp.max(s, axis=1, keepdims=True))
            p = p / jnp.sum(p, axis=1, keepdims=True)
            dp = _dot(dyh, vh, "nt")
            ds = p * (dp - jnp.sum(p * dp, axis=1, keepdims=True))
            dq = _dot(ds, kh) * scale
            dx, dg = _rms_bwd(qm[:, sl], gq, dq)
            dqs.append(dx)
            dgq = dgq + dg
            st = _dot(kh, q, "nt")
            pt = jnp.exp(st - jnp.max(st, axis=0, keepdims=True))
            pt = pt / jnp.sum(pt, axis=0, keepdims=True)
            dpt = _dot(vh, dyh, "nt")
            dst = pt * (dpt - jnp.sum(pt * dpt, axis=0, keepdims=True))
            dvs.append(_dot(pt, dyh))
            dks.append(_dot(dst, q))
        return jnp.concatenate(dqs, axis=1), jnp.concatenate(dks, axis=1), jnp.concatenate(dvs, axis=1), dgq

    m = km.shape[0]
    return _rowwise(fn, "mem_bwd", 512, [(z, MEM_WIDTH, Z_QM // MEM_WIDTH), dy], [km, vm, gq],
                    [(MEM_WIDTH, BF16)], [((m, MEM_WIDTH), F32), ((m, MEM_WIDTH), F32), ((1, LANES), F32)])


GROUPS = {"ffn1": ["ffn1_w_gu", "ffn1_w_down"],
          "mix": ["w_in", "mla_w_uq", "mla_w_ukv", "mem_w_kv", "w_branch_a", "w_branch_b", "w_branch_c", "w_out"],
          "ffn2": ["ffn2_w_gu", "ffn2_w_down"]}


def _local_step(x, mem, positions, loss_target, P, weights, grads_out):
    T = x.shape[0]
    G = {}
    W = dict(weights("ffn1", None))

    half = MLA_ROPE // 2
    inv = ROPE_BASE ** (-jnp.arange(half, dtype=F32) / half)
    ang = positions.astype(F32)[:, None] * inv
    cos, sin = jnp.cos(ang), jnp.sin(ang)
    one, zero = jnp.ones((T, MLA_NOPE), F32), jnp.zeros((T, half), F32)
    pad = LANES - MLA_QK
    tabs = (jnp.concatenate([one, cos, cos, jnp.ones((T, pad), F32)], axis=1),
            jnp.concatenate([jnp.zeros((T, MLA_NOPE), F32), -sin, zero, jnp.zeros((T, pad), F32)], axis=1),
            jnp.concatenate([jnp.zeros((T, MLA_NOPE), F32), zero, sin, jnp.zeros((T, pad), F32)], axis=1))
    gq_p = jnp.pad(P["mla_q_norm"], ((0, 0), (0, pad)))
    gk_p = jnp.pad(P["mla_k_norm"], ((0, 0), (0, pad)))
    bias_full = jnp.repeat(P["sg_b"].T, SG_GROUP_DIM, axis=1)
    group_ind = jnp.repeat(jnp.eye(SG_GROUPS, dtype=F32), SG_GROUP_DIM, axis=0)

    HT = (D_MODEL, BF16, "T")

    def norm2(x, g):
        h = _rms(x, g)
        return h, h

    h1, h1t = _rowwise(norm2, "ffn1_norm", 512, [x], [P["ffn1_norm"]], [(D_MODEL, BF16), HT])
    gu1, a1t, o1 = _ffn_fwd(h1, W["ffn1_w_gu"], W["ffn1_w_down"], "ffn1")

    def resid_norm(x, o, g):
        xn = x + 0.5 * o
        h = _rms(xn, g)
        return xn, h, h

    x1, hm, hmt = _rowwise(resid_norm, "mix_norm", 512, [x, o1], [P["mix_norm"]],
                           [(D_MODEL, F32), (D_MODEL, BF16), HT])
    W.update(weights("mix", hm))
    z = _mm(hm, W["w_in"], "nn", F32, "w_in", tm=1024, tn=768)

    y_a, y_at = _sg_fwd(z, P["sg_ln_g"], P["sg_ln_b"], P["sg_w"], bias_full)

    def c_norm(cq, ckv, gq, gkv):
        a, b = _rms(cq, gq), _rms(ckv, gkv)
        return a, b, a, b

    cqn, ckvn, cqnt, ckvnt = _rowwise(
        c_norm, "mla_cnorm", 512, [(z, MLA_Q_RANK, Z_CQ // MLA_Q_RANK), (z, MLA_KV_RANK, Z_CKV // MLA_KV_RANK)],
        [P["mla_cq_norm"], P["mla_ckv_norm"]],
        [(MLA_Q_RANK, BF16), (MLA_KV_RANK, BF16), (MLA_Q_RANK, BF16, "T"), (MLA_KV_RANK, BF16, "T")])
    q_pre = _mm(cqn, W["mla_w_uq"], "nn", F32, "mla_uq", tm=1024, tn=1024)
    kv_pre = _mm(ckvn, W["mla_w_ukv"], "nn", F32, "mla_ukv", tm=1024, tn=1024)
    q, k, v = _mla_post(q_pre, kv_pre, z, tabs, gq_p, gk_p)
    y_b, y_bt, lse = _flash_fwd(q, k, v)

    memn, = _rowwise(lambda m, g: _rms(m, g), "mem_norm", 256, [mem], [P["mem_norm"]], [(D_MODEL, BF16)])
    kvm = _mm(memn, W["mem_w_kv"], "nn", F32, "mem_kv")

    def mem_k(kvm, gk):
        ks = [_rms(kvm[:, h * LANES:(h + 1) * LANES], gk) for h in range(MEM_HEADS)]
        return jnp.concatenate(ks, axis=1), kvm[:, MEM_WIDTH:]

    km, vm = _rowwise(mem_k, "mem_knorm", 256, [kvm], [P["mem_k_norm"]], [(MEM_WIDTH, BF16), (MEM_WIDTH, BF16)])
    y_c, y_ct = _mem_fwd(z, km, vm, P["mem_q_norm"])

    pa = _mm(y_a, W["w_branch_a"], "nn", F32, "branch_a", tm=1024, tn=1024)
    pb = _mm(y_b, W["w_branch_b"], "nn", F32, "branch_b", tm=1024, tn=1024)
    pc = _mm(y_c, W["w_branch_c"], "nn", F32, "branch_c", tm=1024, tn=1024)

    def merge(zg, pa, pb, pc, b):
        g = _sigmoid(zg + b)
        m = g[:, :D_MODEL] * pa + g[:, D_MODEL:2 * D_MODEL] * pb + g[:, 2 * D_MODEL:] * pc
        return m, m

    merged, mergedt = _rowwise(merge, "merge", 256, [(z, 3 * D_MODEL, 0), pa, pb, pc], [P["b_gate"]],
                               [(D_MODEL, BF16), HT])
    om = _mm(merged, W["w_out"], "nn", F32, "w_out", tm=1024, tn=1024)

    def resid_norm1(x, o, g):
        xn = x + o
        h = _rms(xn, g)
        return xn, h, h

    x2, h2, h2t = _rowwise(resid_norm1, "ffn2_norm", 512, [x1, om], [P["ffn2_norm"]],
                           [(D_MODEL, F32), (D_MODEL, BF16), HT])
    W.update(weights("ffn2", h2))
    gu2, a2t, o2 = _ffn_fwd(h2, W["ffn2_w_gu"], W["ffn2_w_down"], "ffn2")

    def loss_fn(x2, o2, t):
        e = x2 + 0.5 * o2 - t
        return e * (1.0 / D_MODEL), (e * (0.5 / D_MODEL)).astype(BF16), _rsum(e * e) * (0.5 / D_MODEL)

    dx3, do2, loss_part = _rowwise(loss_fn, "loss", 512, [x2, o2, loss_target], [],
                                   [(D_MODEL, F32), (D_MODEL, BF16)], [((1, D_MODEL), F32)])

    dh2, G["ffn2_w_gu"], G["ffn2_w_down"] = _ffn_bwd(do2, h2t, gu2, a2t, W["ffn2_w_gu"], W["ffn2_w_down"], "ffn2")
    grads_out("ffn2", G, None)

    def norm_bwd(x, dh, dxo, g):
        dx, dg = _rms_bwd(x, g, dh)
        dx = dx + dxo
        return dx, dx, dg

    dx2, dx2b, G["ffn2_norm"] = _rowwise(norm_bwd, "ffn2_norm_bwd", 512, [x2, dh2, dx3], [P["ffn2_norm"]],
                                         [(D_MODEL, F32), (D_MODEL, BF16)], [((1, D_MODEL), F32)])

    G["w_out"] = _mm_t(mergedt, dx2b, "w_out_dw", tm=1024, tn=1024)
    dmerged = _mm(dx2b, W["w_out"], "nt", F32, "w_out_dx", tm=1024, tn=1024)

    def merge_bwd(zg, pa, pb, pc, dm, b):
        g = _sigmoid(zg + b)
        ps = jnp.concatenate([pa, pb, pc], axis=1)
        dm3 = jnp.concatenate([dm, dm, dm], axis=1)
        dzg = dm3 * ps * g * (1.0 - g)
        dp = dm3 * g
        return dzg, dp[:, :D_MODEL], dp[:, D_MODEL:2 * D_MODEL], dp[:, 2 * D_MODEL:], _rsum(dzg)

    dzg, dpa, dpb, dpc, G["b_gate"] = _rowwise(
        merge_bwd, "merge_bwd", 256, [(z, 3 * D_MODEL, 0), pa, pb, pc, dmerged], [P["b_gate"]],
        [(3 * D_MODEL, BF16), (D_MODEL, BF16), (D_MODEL, BF16), (D_MODEL, BF16)], [((1, 3 * D_MODEL), F32)])

    G["w_branch_a"] = _mm_t(y_at, dpa, "branch_a_dw", tm=512, tn=1024)
    G["w_branch_b"] = _mm_t(y_bt, dpb, "branch_b_dw", tm=1024, tn=1024)
    G["w_branch_c"] = _mm_t(y_ct, dpc, "branch_c_dw", tm=512, tn=1024)
    dy_a = _mm(dpa, W["w_branch_a"], "nt", BF16, "branch_a_dx", tm=1024, tn=512)
    dy_b = _mm(dpb, W["w_branch_b"], "nt", BF16, "branch_b_dx", tm=1024, tn=1024)
    dy_c = _mm(dpc, W["w_branch_c"], "nt", BF16, "branch_c_dx", tm=1024, tn=512)

    du_pre, dv_pre, G["sg_w"], dbias_t, G["sg_ln_g"], G["sg_ln_b"] = _sg_bwd(
        z, dy_a, P["sg_ln_g"], P["sg_ln_b"], P["sg_w"], bias_full, group_ind)
    G["sg_b"] = dbias_t.T

    dqm, dkm, dvm, G["mem_q_norm"] = _mem_bwd(z, dy_c, km, vm, P["mem_q_norm"])

    def mem_k_bwd(kvm, dkm, dvm, gk):
        dks = []
        dg = jnp.zeros((1, LANES), F32)
        for h in range(MEM_HEADS):
            sl = slice(h * LANES, (h + 1) * LANES)
            dx, d = _rms_bwd(kvm[:, sl], gk, dkm[:, sl])
            dks.append(dx)
            dg = dg + d
        return jnp.concatenate(dks + [dvm], axis=1), dg

    dkvm, G["mem_k_norm"] = _rowwise(mem_k_bwd, "mem_knorm_bwd", 256, [kvm, dkm, dvm], [P["mem_k_norm"]],
                                     [(2 * MEM_WIDTH, BF16)], [((1, LANES), F32)])
    G["mem_w_kv"] = _mm(memn, dkvm, "tn", F32, "mem_kv_dw")
    dmemn = _mm(dkvm, W["mem_w_kv"], "nt", F32, "mem_kv_dx")
    _, G["mem_norm"] = _rowwise(lambda m, d, g: _rms_bwd(m, g, d), "mem_norm_bwd", 256, [mem, dmemn],
                                [P["mem_norm"]], [(D_MODEL, BF16)], [((1, D_MODEL), F32)])

    def delta_fn(o, do):
        od = o.astype(F32) * do.astype(F32)
        ds = [jnp.broadcast_to(jnp.sum(od[:, h * LANES:(h + 1) * LANES], axis=1, keepdims=True), (od.shape[0], LANES))
              for h in range(MLA_HEADS)]
        return jnp.concatenate(ds, axis=1)

    delta, = _rowwise(delta_fn, "mla_delta", 512, [y_b, dy_b], [], [(HP, F32)])
    rowform = lambda a: a.reshape(T, MLA_HEADS, LANES)[:, :, 0].T.reshape(MLA_HEADS, 1, T)
    dq = _flash_dq(q, k, v, dy_b, lse, delta)
    dk, dv = _flash_dkv(q, k, v, dy_b, rowform(lse), rowform(delta))
    dq_pre, dkv_pre, dkr, dgq, dgk = _mla_post_bwd(q_pre, kv_pre, z, tabs, gq_p, gk_p, dq, dk, dv)
    G["mla_q_norm"], G["mla_k_norm"] = dgq[:, :MLA_QK], dgk[:, :MLA_QK]
    G["mla_w_uq"] = _mm_t(cqnt, dq_pre, "mla_uq_dw", tm=384, tn=1024)
    G["mla_w_ukv"] = _mm_t(ckvnt, dkv_pre, "mla_ukv_dw", tm=256, tn=2048)
    dcqn = _mm(dq_pre, W["mla_w_uq"], "nt", F32, "mla_uq_dx", tm=1024)
    dckvn = _mm(dkv_pre, W["mla_w_ukv"], "nt", F32, "mla_ukv_dx", tm=1024)

    def c_norm_bwd(cq, ckv, dcqn, dckvn, gq, gkv):
        dcq, dgq = _rms_bwd(cq, gq, dcqn)
        dckv, dgkv = _rms_bwd(ckv, gkv, dckvn)
        return dcq, dckv, dgq, dgkv

    dcq, dckv, G["mla_cq_norm"], G["mla_ckv_norm"] = _rowwise(
        c_norm_bwd, "mla_cnorm_bwd", 512,
        [(z, MLA_Q_RANK, Z_CQ // MLA_Q_RANK), (z, MLA_KV_RANK, Z_CKV // MLA_KV_RANK), dcqn, dckvn],
        [P["mla_cq_norm"], P["mla_ckv_norm"]], [(MLA_Q_RANK, BF16), (MLA_KV_RANK, BF16)],
        [((1, MLA_Q_RANK), F32), ((1, MLA_KV_RANK), F32)])

    dz = jnp.concatenate([dzg, du_pre, dv_pre, dqm, dckv, dkr, dcq], axis=1)
    G["w_in"] = _mm_t(hmt, dz, "w_in_dw", tm=1024, tn=768)
    dhm = _mm(dz, W["w_in"], "nt", F32, "w_in_dx", tm=1024, tn=1024, tk=2688)

    def norm_bwd_half(x, dh, dxo, g):
        dx, dg = _rms_bwd(x, g, dh)
        dx = dx + dxo
        return dx, (0.5 * dx), dg

    dx1, do1, G["mix_norm"] = _rowwise(norm_bwd_half, "mix_norm_bwd", 512, [x1, dhm, dx2], [P["mix_norm"]],
                                       [(D_MODEL, F32), (D_MODEL, BF16)], [((1, D_MODEL), F32)])
    grads_out("mix", G, do1)
    dh1, G["ffn1_w_gu"], G["ffn1_w_down"] = _ffn_bwd(do1, h1t, gu1, a1t, W["ffn1_w_gu"], W["ffn1_w_down"], "ffn1")

    def norm_bwd_last(x, dh, dxo, g):
        dx, dg = _rms_bwd(x, g, dh)
        return dx + dxo, dg

    grad_x, G["ffn1_norm"] = _rowwise(norm_bwd_last, "ffn1_norm_bwd", 512, [x, dh1, dx1], [P["ffn1_norm"]],
                                      [(D_MODEL, F32)], [((1, D_MODEL), F32)])
    grads_out("ffn1", G, grad_x)
    return loss_part, grad_x, G


SHARDED = ["ffn1_w_gu", "ffn1_w_down", "w_in", "mla_w_uq", "mla_w_ukv", "mem_w_kv",
           "w_branch_a", "w_branch_b", "w_branch_c", "w_out", "ffn2_w_gu", "ffn2_w_down"]
ROW_SHARDED = {"ffn1_w_down", "mem_w_kv", "w_out", "ffn2_w_down"}
SMALL = ["ffn1_norm", "mix_norm", "b_gate", "sg_ln_g", "sg_ln_b", "sg_w", "sg_b", "mla_cq_norm",
         "mla_ckv_norm", "mla_q_norm", "mla_k_norm", "mem_norm", "mem_q_norm", "mem_k_norm", "ffn2_norm"]
ORDER = ["ffn1_norm", "ffn1_w_gu", "ffn1_w_down", "mix_norm", "w_in", "b_gate", "sg_ln_g", "sg_ln_b", "sg_w",
         "sg_b", "mla_cq_norm", "mla_w_uq", "mla_ckv_norm", "mla_w_ukv", "mla_q_norm", "mla_k_norm", "mem_norm",
         "mem_w_kv", "mem_q_norm", "mem_k_norm", "w_branch_a", "w_branch_b", "w_branch_c", "w_out", "ffn2_norm",
         "ffn2_w_gu", "ffn2_w_down"]

_IN_U, _IN_V, _IN_CQ, _IN_CKV, _IN_KR, _IN_QM, _IN_G = 0, 512, 1024, 1408, 1664, 1696, 2208
IN_COLS = 5280


def _full_from_slabs(name, slabs):
    n, r, c = slabs.shape
    if name in ROW_SHARDED:
        return slabs.reshape(n * r, c)
    return slabs.transpose(1, 0, 2).reshape(r, n * c)


def _slabs_from_full(name, full):
    if name in ROW_SHARDED:
        return full.reshape(N_DEV, full.shape[0] // N_DEV, full.shape[1])
    r, c = full.shape
    return full.reshape(r, N_DEV, c // N_DEV).transpose(1, 0, 2)


def _compute_layout(full):
    W = dict(full)
    if "w_in" not in full:
        return W
    w = full["w_in"]
    kr = jnp.pad(w[:, _IN_KR:_IN_QM], ((0, 0), (KR_LANE, LANES - KR_LANE - MLA_ROPE)))
    W["w_in"] = jnp.concatenate([w[:, _IN_G:], w[:, _IN_U:_IN_CQ], w[:, _IN_QM:_IN_G], w[:, _IN_CKV:_IN_KR], kr,
                                 w[:, _IN_CQ:_IN_CKV]], axis=1)
    uq = full["mla_w_uq"].reshape(MLA_Q_RANK, MLA_HEADS, MLA_QK)
    W["mla_w_uq"] = jnp.pad(uq, ((0, 0), (0, 0), (0, LANES - MLA_QK))).reshape(MLA_Q_RANK, HP)
    ukv = full["mla_w_ukv"].reshape(MLA_KV_RANK, MLA_HEADS, MLA_NOPE + MLA_V)
    padh = lambda a: jnp.pad(a, ((0, 0), (0, 0), (0, LANES - a.shape[2]))).reshape(MLA_KV_RANK, HP)
    W["mla_w_ukv"] = jnp.concatenate([padh(ukv[:, :, :MLA_NOPE]), padh(ukv[:, :, MLA_NOPE:])], axis=1)
    wb = full["w_branch_b"].reshape(MLA_HEADS, MLA_V, D_MODEL)
    W["w_branch_b"] = jnp.pad(wb, ((0, 0), (0, LANES - MLA_V), (0, 0))).reshape(HP, D_MODEL)
    return W


def _reference_layout(G):
    out = dict(G)
    if "w_in" not in G:
        return out
    g = G["w_in"]
    out["w_in"] = jnp.concatenate([
        g[:, Z_U:Z_QM], g[:, Z_CQ:Z_COLS], g[:, Z_CKV:Z_KR], g[:, Z_KR + KR_LANE:Z_KR + KR_LANE + MLA_ROPE],
        g[:, Z_QM:Z_CKV], g[:, Z_G:Z_U]], axis=1)
    out["mla_w_uq"] = G["mla_w_uq"].reshape(MLA_Q_RANK, MLA_HEADS, LANES)[:, :, :MLA_QK].reshape(MLA_Q_RANK, -1)
    gk = G["mla_w_ukv"][:, :HP].reshape(MLA_KV_RANK, MLA_HEADS, LANES)[:, :, :MLA_NOPE]
    gv = G["mla_w_ukv"][:, HP:].reshape(MLA_KV_RANK, MLA_HEADS, LANES)[:, :, :MLA_V]
    out["mla_w_ukv"] = jnp.concatenate([gk, gv], axis=2).reshape(MLA_KV_RANK, -1)
    out["w_branch_b"] = G["w_branch_b"].reshape(MLA_HEADS, LANES, D_MODEL)[:, :MLA_V].reshape(-1, D_MODEL)
    return out


def _pack(parts):
    flat = []
    for a in parts:
        a = a.reshape(-1)
        flat.append(jnp.pad(a, (0, (-a.shape[0]) % LANES)))
    return jnp.concatenate(flat).reshape(-1, LANES)


def _unpack(packed, shapes):
    flat = packed.reshape(-1)
    out, off = [], 0
    for shp in shapes:
        n = int(np.prod(shp))
        out.append(flat[off:off + n].reshape(shp))
        off += n + (-n) % LANES
    return out


MESH = pl.DeviceIdType.MESH
HBM = pl.BlockSpec(memory_space=pltpu.HBM)


def _all_gather(shard):
    rows, lanes = shard.shape

    def body(x_ref, out_ref, token_ref, send_sems, recv_sems, local_sem):
        x, y, c = lax.axis_index("x"), lax.axis_index("y"), lax.axis_index("c")
        me, sibling = (x, y, c), (x, y, 1 - c)
        chips = [(1 - x, y), (x, 1 - y), (1 - x, 1 - y)]
        token_ref[...] = jnp.zeros_like(token_ref)

        def slot(px, py, pc):
            return out_ref.at[4 * px + 2 * py + pc]

        def copy(k, block, to, src=None):
            return pltpu.make_async_remote_copy(
                src_ref=slot(*block) if src is None else src, dst_ref=slot(*block),
                send_sem=send_sems.at[k], recv_sem=recv_sems.at[k], device_id=to, device_id_type=MESH)

        mine = pltpu.make_async_copy(x_ref, slot(*me), local_sem)
        mine.start()
        first = [copy(0, me, sibling, src=x_ref)]
        first += [copy(1 + j, me, (*chip, c), src=x_ref) for j, chip in enumerate(chips)]
        for cp in first:
            cp.start()
        passed = [copy(4 + j, (*chip, c), sibling) for j, chip in enumerate(chips)]
        for j, chip in enumerate(chips):
            copy(1 + j, (*chip, c), me).wait_recv()
            passed[j].start()
        copy(0, sibling, me).wait_recv()
        for j, chip in enumerate(chips):
            copy(4 + j, (*chip, 1 - c), me).wait_recv()
        for cp in first + passed:
            cp.wait_send()
        mine.wait()

    return pl.pallas_call(
        body, name="all_gather_weights",
        out_shape=[jax.ShapeDtypeStruct((N_DEV, rows, lanes), shard.dtype), jax.ShapeDtypeStruct((8, LANES), F32)],
        in_specs=[HBM], out_specs=[HBM, pl.BlockSpec(memory_space=pltpu.VMEM)],
        scratch_shapes=[pltpu.SemaphoreType.DMA((7,)), pltpu.SemaphoreType.DMA((7,)), pltpu.SemaphoreType.DMA],
    )(shard)


SEM = pl.BlockSpec(memory_space=pltpu.SEMAPHORE)
DATAFLOW = pltpu.SideEffectType.DATAFLOW_SIDE_EFFECTING


def _peers():
    x, y, c = lax.axis_index("x"), lax.axis_index("y"), lax.axis_index("c")
    out = []
    for k in range(1, N_DEV):
        px = 1 - x if k & 4 else x
        py = 1 - y if k & 2 else y
        pc = 1 - c if k & 1 else c
        out.append((k, (px, py, pc), 4 * px + 2 * py + pc))
    return 4 * x + 2 * y + c, out


def _send_start(src, per_peer, name):
    rows = src.shape[-2]
    land = lax.empty((N_DEV, rows, LANES), src.dtype)

    def body(src_ref, land_ref, send_sems, recv_sems, src_thru, land_thru, token):
        me, peers = _peers()
        for k, pid, pflat in peers:
            pltpu.make_async_remote_copy(
                src_ref=src_ref.at[pflat] if per_peer else src_ref, dst_ref=land_ref.at[me],
                send_sem=send_sems.at[k - 1], recv_sem=recv_sems.at[k - 1],
                device_id=pid, device_id_type=MESH).start()
        token[...] = jnp.zeros_like(token)

    res = pl.pallas_call(
        body, name=name,
        out_shape=(pltpu.SemaphoreType.DMA((N_DEV - 1,)), pltpu.SemaphoreType.DMA((N_DEV - 1,)),
                   pltpu.HBM(src.shape, src.dtype), pltpu.HBM(land.shape, land.dtype),
                   jax.ShapeDtypeStruct((8, LANES), F32)),
        in_specs=(HBM, HBM), out_specs=(SEM, SEM, HBM, HBM, pl.BlockSpec(memory_space=pltpu.VMEM)),
        input_output_aliases={0: 2, 1: 3},
        compiler_params=pltpu.CompilerParams(has_side_effects=DATAFLOW),
    )(pltpu.with_memory_space_constraint(src, pltpu.HBM), pltpu.with_memory_space_constraint(land, pltpu.HBM))
    return res[:4]


def _send_wait(started, after, per_peer, name):
    send_sems, recv_sems, src_thru, land_thru = started

    def body(src_ref, land_ref, send_sems, recv_sems, after_ref, src_out, land_out):
        me, peers = _peers()
        for k, pid, pflat in peers:
            copy = pltpu.make_async_remote_copy(
                src_ref=src_ref.at[pflat] if per_peer else src_ref, dst_ref=land_ref.at[pflat],
                send_sem=send_sems.at[k - 1], recv_sem=recv_sems.at[k - 1],
                device_id=pid, device_id_type=MESH)
            copy.wait_send()
            copy.wait_recv()

    src_out, land = pl.pallas_call(
        body, name=name,
        out_shape=(pltpu.HBM(src_thru.shape, src_thru.dtype), pltpu.HBM(land_thru.shape, land_thru.dtype)),
        in_specs=(HBM, HBM, SEM, SEM, pl.BlockSpec(memory_space=pl.ANY)), out_specs=(HBM, HBM),
        input_output_aliases={0: 0, 1: 1},
        compiler_params=pltpu.CompilerParams(has_side_effects=DATAFLOW),
    )(src_thru, land_thru, send_sems, recv_sems, after)
    me = 4 * lax.axis_index("x") + 2 * lax.axis_index("y") + lax.axis_index("c")
    own = lax.dynamic_index_in_dim(src_out, me, 0, keepdims=True) if per_peer else src_out[None]
    return lax.dynamic_update_slice(land, own, (me, 0, 0))


def _exchange(big, small):
    def body(big_ref, small_ref, bout_ref, sout_ref, send_sems, recv_sems, local_sems):
        x, y, c = lax.axis_index("x"), lax.axis_index("y"), lax.axis_index("c")
        me = 4 * x + 2 * y + c
        own = [pltpu.make_async_copy(big_ref.at[me], bout_ref.at[me], local_sems.at[0]),
               pltpu.make_async_copy(small_ref, sout_ref.at[me], local_sems.at[1])]
        for cp in own:
            cp.start()
        copies = []
        for k in range(1, N_DEV):
            px = 1 - x if k & 4 else x
            py = 1 - y if k & 2 else y
            pc = 1 - c if k & 1 else c
            peer = 4 * px + 2 * py + pc
            copies.append(pltpu.make_async_remote_copy(
                src_ref=big_ref.at[peer], dst_ref=bout_ref.at[me], send_sem=send_sems.at[k - 1],
                recv_sem=recv_sems.at[k - 1], device_id=(px, py, pc), device_id_type=MESH))
            copies.append(pltpu.make_async_remote_copy(
                src_ref=small_ref, dst_ref=sout_ref.at[me], send_sem=send_sems.at[7 + k - 1],
                recv_sem=recv_sems.at[7 + k - 1], device_id=(px, py, pc), device_id_type=MESH))
        for cp in copies:
            cp.start()
        for cp in copies:
            cp.wait()
        for cp in own:
            cp.wait()

    return pl.pallas_call(
        body, name="exchange_grads",
        out_shape=[jax.ShapeDtypeStruct(big.shape, big.dtype),
                   jax.ShapeDtypeStruct((N_DEV,) + small.shape, small.dtype)],
        in_specs=[HBM, HBM], out_specs=[HBM, HBM],
        scratch_shapes=[pltpu.SemaphoreType.DMA((14,)), pltpu.SemaphoreType.DMA((14,)),
                        pltpu.SemaphoreType.DMA((2,))],
    )(big, small)


def _sum_slots(recv, name, tr):
    n, rows, lanes = recv.shape
    tr = _tile(rows, tr)

    def body(r_ref, o_ref):
        acc = r_ref[0].astype(F32)
        for i in range(1, n):
            acc = acc + r_ref[i].astype(F32)
        o_ref[...] = acc

    return pl.pallas_call(
        body, name=name, grid=(rows // tr,),
        in_specs=[pl.BlockSpec((n, tr, lanes), lambda i: (0, i, 0))],
        out_specs=pl.BlockSpec((tr, lanes), lambda i: (i, 0)),
        out_shape=jax.ShapeDtypeStruct((rows, lanes), F32),
        compiler_params=_cparams(("parallel",)),
    )(recv)


def _adamw(w, g, m, v, name, tr=256):
    c1 = 1.0 - ADAM_B1 ** ADAM_STEP
    c2 = 1.0 - ADAM_B2 ** ADAM_STEP

    def fn(w, g, m, v):
        m = ADAM_B1 * m + (1.0 - ADAM_B1) * g
        v = ADAM_B2 * v + (1.0 - ADAM_B2) * (g * g)
        delta = -ADAM_LR * ((m / c1) / (jnp.sqrt(v / c2) + ADAM_EPS) + ADAM_WD * w)
        return delta, m, v

    return _rowwise(fn, name, tr, [w, g, m, v], [], [(w.shape[1], F32)] * 3)


def kernel(x, mem, positions, ffn1_norm, ffn1_w_gu, ffn1_w_down, mix_norm, w_in, b_gate, sg_ln_g, sg_ln_b, sg_w, sg_b, mla_cq_norm, mla_w_uq, mla_ckv_norm, mla_w_ukv, mla_q_norm, mla_k_norm, mem_norm, mem_w_kv, mem_q_norm, mem_k_norm, w_branch_a, w_branch_b, w_branch_c, w_out, ffn2_norm, ffn2_w_gu, ffn2_w_down, loss_target, m_ffn1_norm, m_ffn1_w_gu, m_ffn1_w_down, m_mix_norm, m_w_in, m_b_gate, m_sg_ln_g, m_sg_ln_b, m_sg_w, m_sg_b, m_mla_cq_norm, m_mla_w_uq, m_mla_ckv_norm, m_mla_w_ukv, m_mla_q_norm, m_mla_k_norm, m_mem_norm, m_mem_w_kv, m_mem_q_norm, m_mem_k_norm, m_w_branch_a, m_w_branch_b, m_w_branch_c, m_w_out, m_ffn2_norm, m_ffn2_w_gu, m_ffn2_w_down, v_ffn1_norm, v_ffn1_w_gu, v_ffn1_w_down, v_mix_norm, v_w_in, v_b_gate, v_sg_ln_g, v_sg_ln_b, v_sg_w, v_sg_b, v_mla_cq_norm, v_mla_w_uq, v_mla_ckv_norm, v_mla_w_ukv, v_mla_q_norm, v_mla_k_norm, v_mem_norm, v_mem_w_kv, v_mem_q_norm, v_mem_k_norm, v_w_branch_a, v_w_branch_b, v_w_branch_c, v_w_out, v_ffn2_norm, v_ffn2_w_gu, v_ffn2_w_down):
    given = dict(locals())
    wts = {n: given[n] for n in ORDER}
    mom = {n: given["m_" + n] for n in ORDER}
    var = {n: given["v_" + n] for n in ORDER}
    shard_shapes = {n: wts[n].shape[1:] for n in SHARDED}
    shard_rows = {n: int(np.prod(shard_shapes[n])) // LANES for n in SHARDED}

    def pack_shards(group):
        return jnp.concatenate([wts[n][0].astype(BF16).reshape(shard_rows[n], LANES) for n in GROUPS[group]], axis=0)

    def split_rows(group, packed, lead):
        out, off = {}, 0
        for n in GROUPS[group]:
            out[n] = packed[..., off:off + shard_rows[n], :].reshape(*lead, *shard_shapes[n])
            off += shard_rows[n]
        return out

    def full_weights(group, gathered):
        return _compute_layout({n: _full_from_slabs(n, s) for n, s in split_rows(group, gathered, (N_DEV,)).items()})

    def zero_of(a):
        return jnp.minimum(jnp.abs(a.reshape(-1)[0]), 0).astype(BF16)

    gathered_ffn1, token = _all_gather(pack_shards("ffn1"))
    flight = {"mix": _send_start(pack_shards("mix") + token[0, 0].astype(BF16), False, "gather_mix_start")}
    recv = {}

    def weights(group, after):
        if group == "ffn1":
            return full_weights(group, gathered_ffn1)
        landed = _send_wait(flight.pop(group), after, False, f"gather_{group}_wait")
        if group == "mix":
            flight["ffn2"] = _send_start(pack_shards("ffn2") + zero_of(landed), False, "gather_ffn2_start")
        return full_weights(group, landed)

    def grads_out(group, G, after):
        Gr = _reference_layout({n: G[n] for n in GROUPS[group]})
        big = jnp.concatenate([_slabs_from_full(n, Gr[n]).astype(BF16).reshape(N_DEV, shard_rows[n], LANES)
                               for n in GROUPS[group]], axis=1)
        if group == "ffn2":
            flight["g_ffn2"] = _send_start(big, True, "grads_ffn2_start")
        elif group == "mix":
            recv["ffn2"] = _send_wait(flight.pop("g_ffn2"), after, True, "grads_ffn2_wait")
            flight["g_mix"] = _send_start(big, True, "grads_mix_start")
        else:
            recv["mix"] = _send_wait(flight.pop("g_mix"), after, True, "grads_mix_wait")
            flight["g_ffn1"] = big

    P = {n: wts[n] if wts[n].ndim == 2 else wts[n][0] for n in SMALL}
    loss_part, grad_x, G = _local_step(x[0], mem[0], positions[0], loss_target[0], P, weights, grads_out)

    small_shapes = [wts[n].shape[1:] for n in SMALL]
    small = _pack([G[n].reshape(s) for n, s in zip(SMALL, small_shapes)])
    small = jnp.pad(small, ((0, (-small.shape[0]) % 8), (0, 0)))
    recv["ffn1"], small_recv = _exchange(flight.pop("g_ffn1"), small)
    g_small_packed = _sum_slots(small_recv, "sum_small", 2048)
    grads = {}
    for group in GROUPS:
        grads.update(split_rows(group, _sum_slots(recv[group], "sum_" + group, 1024), ()))
    grads.update(zip(SMALL, _unpack(g_small_packed, small_shapes)))

    delta, new_m, new_v = {}, {}, {}
    for n in SHARDED:
        delta[n], new_m[n], new_v[n] = _adamw(wts[n][0], grads[n], mom[n][0], var[n][0], "adamw_" + n)

    def pack_small(d):
        p = _pack([d[n].reshape(s) for n, s in zip(SMALL, small_shapes)])
        return jnp.pad(p, ((0, (-p.shape[0]) % 8), (0, 0)))

    ds, ms, vs = _adamw(pack_small(wts), g_small_packed, pack_small(mom), pack_small(var), "adamw_small", tr=2048)
    for dst, packed in ((delta, ds), (new_m, ms), (new_v, vs)):
        dst.update(zip(SMALL, _unpack(packed, small_shapes)))

    loss = lax.psum(jnp.sum(loss_part), ("x", "y", "c"))
    lead = lambda d: [d[n].reshape(wts[n].shape) for n in ORDER]
    return (loss, grad_x[None], *lead(grads), *lead(delta), *lead(new_m), *lead(new_v))
```

```python
import functools

import numpy as np
import jax
import jax.numpy as jnp
from jax import lax
from jax.experimental import pallas as pl
from jax.experimental.pallas import tpu as pltpu

F32, BF16 = jnp.float32, jnp.bfloat16

D_MODEL = 1024
SG_GROUPS, SG_GROUP_DIM, SG_WIDTH, CHUNK = 8, 64, 512, 128
MLA_HEADS, MLA_NOPE, MLA_ROPE, MLA_V, MLA_QK = 8, 64, 32, 64, 96
MLA_Q_RANK, MLA_KV_RANK = 384, 256
MEM_HEADS, MEM_HEAD_DIM, MEM_WIDTH = 4, 128, 512
D_FF = 2816
ROPE_BASE = 10000.0
EPS = 1e-6
NEG = -1e30
ADAM_LR, ADAM_B1, ADAM_B2, ADAM_EPS, ADAM_WD, ADAM_STEP = 0.001, 0.9, 0.999, 1e-08, 0.01, 10

N_DEV = 8
LANES = 128
V7X_VMEM_LIMIT = 56 * 1024 * 1024
HP = MLA_HEADS * LANES

Z_G, Z_U, Z_V, Z_QM, Z_CKV, Z_KR, Z_CQ = 0, 3072, 3584, 4096, 4608, 4864, 4992
Z_COLS = 5376
KR_LANE = 64


def _tile(dim, pref):
    if dim <= pref:
        return dim
    for t in range(pref - pref % LANES, LANES - 1, -LANES):
        if dim % t == 0:
            return t
    for t in range(pref - pref % 8, 7, -8):
        if dim % t == 0:
            return t
    return dim


def _cparams(sem):
    return pltpu.CompilerParams(dimension_semantics=sem, vmem_limit_bytes=V7X_VMEM_LIMIT)


_DN = {"nn": ((1,), (0,)), "nt": ((1,), (1,)), "tn": ((0,), (0,))}


def _dot(a, b, mode="nn"):
    return lax.dot_general(a.astype(BF16), b.astype(BF16), (_DN[mode], ((), ())),
                           preferred_element_type=F32)


def _mm(a, b, mode, out_dtype, name, tm=512, tn=512, tk=2048):
    if mode == "tn":
        K, M = a.shape
    else:
        M, K = a.shape
    N = b.shape[0] if mode == "nt" else b.shape[1]
    tm, tn, tk = _tile(M, tm), _tile(N, tn), _tile(K, tk)
    nk = K // tk
    if mode == "tn":
        a_spec = pl.BlockSpec((tk, tm), lambda i, j, k: (k, i))
    else:
        a_spec = pl.BlockSpec((tm, tk), lambda i, j, k: (i, k))
    if mode == "nt":
        b_spec = pl.BlockSpec((tn, tk), lambda i, j, k: (j, k))
    else:
        b_spec = pl.BlockSpec((tk, tn), lambda i, j, k: (k, j))

    def body(a_ref, b_ref, o_ref, *scratch):
        p = _dot(a_ref[...], b_ref[...], mode)
        if nk == 1:
            o_ref[...] = p.astype(o_ref.dtype)
        else:
            acc_ref, = scratch
            k = pl.program_id(2)

            @pl.when(k == 0)
            def _():
                acc_ref[...] = p

            @pl.when(k > 0)
            def _():
                acc_ref[...] += p

            @pl.when(k == nk - 1)
            def _():
                o_ref[...] = acc_ref[...].astype(o_ref.dtype)

    return pl.pallas_call(
        body, name=name, grid=(M // tm, N // tn, nk),
        in_specs=[a_spec, b_spec],
        out_specs=pl.BlockSpec((tm, tn), lambda i, j, k: (i, j)),
        out_shape=jax.ShapeDtypeStruct((M, N), out_dtype),
        scratch_shapes=[] if nk == 1 else [pltpu.VMEM((tm, tn), F32)],
        compiler_params=_cparams(("parallel", "parallel", "arbitrary")),
    )(a, b)


def _mm_t(at, b, name, tm, tn, tk=1024):
    return _mm(at, b, "nn", BF16, name, tm=tm, tn=tn, tk=tk)


def _rowwise(fn, name, tr, row_ins, bc_ins, row_outs, acc_outs=()):
    norm = [it if isinstance(it, tuple) else (it, it.shape[1], 0) for it in row_ins]
    rows = norm[0][0].shape[0]
    tr = _tile(rows, tr)
    arrays, in_specs = [], []
    for arr, w, cb in norm:
        arrays.append(arr)
        in_specs.append(pl.BlockSpec((tr, w), lambda i, cb=cb: (i, cb)))
    for arr in bc_ins:
        arrays.append(arr)
        in_specs.append(pl.BlockSpec(arr.shape, lambda i, nd=arr.ndim: (0,) * nd))
    out_shape, out_specs = [], []
    transposed = [len(o) == 3 for o in row_outs]
    for (w, dt, *_), t in zip(row_outs, transposed):
        out_shape.append(jax.ShapeDtypeStruct((w, rows) if t else (rows, w), dt))
        out_specs.append(pl.BlockSpec((w, tr), lambda i: (0, i)) if t else pl.BlockSpec((tr, w), lambda i: (i, 0)))
    for shp, dt in acc_outs:
        out_shape.append(jax.ShapeDtypeStruct(shp, dt))
        out_specs.append(pl.BlockSpec(shp, lambda i, nd=len(shp): (0,) * nd))
    n_in, n_row = len(arrays), len(row_outs)

    def body(*refs):
        vals = fn(*[r[...] for r in refs[:n_in]])
        if not isinstance(vals, (tuple, list)):
            vals = (vals,)
        outs = refs[n_in:]
        for r, v, t in zip(outs[:n_row], vals[:n_row], transposed):
            r[...] = v.astype(F32).T.astype(r.dtype) if t else v.astype(r.dtype)
        if acc_outs:
            accs = list(zip(outs[n_row:], vals[n_row:]))
            i = pl.program_id(0)

            @pl.when(i == 0)
            def _():
                for r, v in accs:
                    r[...] = v.astype(r.dtype)

            @pl.when(i > 0)
            def _():
                for r, v in accs:
                    r[...] += v.astype(r.dtype)

    res = pl.pallas_call(
        body, name=name, grid=(rows // tr,), in_specs=in_specs, out_specs=out_specs,
        out_shape=out_shape, compiler_params=_cparams(("arbitrary",)),
    )(*arrays)
    return res


def _rsum(x):
    return jnp.sum(x, axis=0, keepdims=True)


def _rms(x, g, n=None):
    n = x.shape[-1] if n is None else n
    r = lax.rsqrt(jnp.sum(x * x, axis=-1, keepdims=True) * (1.0 / n) + EPS)
    return x * r * g


def _rms_bwd(x, g, dy, n=None):
    n = x.shape[-1] if n is None else n
    r = lax.rsqrt(jnp.sum(x * x, axis=-1, keepdims=True) * (1.0 / n) + EPS)
    xh = x * r
    dxh = dy * g
    dx = r * (dxh - xh * (jnp.sum(dxh * xh, axis=-1, keepdims=True) * (1.0 / n)))
    return dx, _rsum(dy * xh)


def _gelu(x):
    return 0.5 * x * (1.0 + lax.erf(x * 0.7071067811865476))


def _gelu_grad(x):
    return 0.5 * (1.0 + lax.erf(x * 0.7071067811865476)) + x * jnp.exp(-0.5 * x * x) * 0.3989422804014327


def _sigmoid(x):
    return 1.0 / (1.0 + jnp.exp(-x))


def _ffn_fwd(h, w_gu, w_down, tag):
    gu = _mm(h, w_gu, "nn", BF16, f"{tag}_gu", tm=1024, tn=512)

    def act(gu):
        g = gu[:, :D_FF].astype(F32)
        u = gu[:, D_FF:].astype(F32)
        a = g * _sigmoid(g) * u
        return a, a

    a, at = _rowwise(act, f"{tag}_act", 256, [gu], [], [(D_FF, BF16), (D_FF, BF16, "T")])
    o = _mm(a, w_down, "nn", F32, f"{tag}_down", tm=1024, tn=512, tk=2816)
    return gu, at, o


def _ffn_bwd(do, ht, gu, at, w_gu, w_down, tag, tie=None):
    dw_down = _mm_t(at, do, f"{tag}_dwdown", tm=1408, tn=1024)
    da = _mm(do, w_down, "nt", BF16, f"{tag}_da", tm=1024, tn=512)

    def act_bwd(gu, da, *_):
        g = gu[:, :D_FF].astype(F32)
        u = gu[:, D_FF:].astype(F32)
        da = da.astype(F32)
        s = _sigmoid(g)
        dg = da * u * s * (1.0 + g * (1.0 - s))
        du = da * g * s
        return jnp.concatenate([dg, du], axis=1)

    dgu, = _rowwise(act_bwd, f"{tag}_actbwd", 256, [gu, da], [] if tie is None else [tie], [(2 * D_FF, BF16)])
    dw_gu = _mm_t(ht, dgu, f"{tag}_dwgu", tm=1024, tn=1408)
    dh = _mm(dgu, w_gu, "nt", F32, f"{tag}_dh", tm=1024, tn=512, tk=2816)
    return dh, dw_gu, dw_down


def _sg_common(u_pre, v_pre, ln_g, ln_b):
    u = _gelu(u_pre)
    v = _gelu(v_pre)
    mu = jnp.mean(v, axis=-1, keepdims=True)
    vc = v - mu
    rstd = lax.rsqrt(jnp.mean(vc * vc, axis=-1, keepdims=True) + EPS)
    vhat = vc * rstd
    vl = vhat * ln_g + ln_b
    return u, vhat, rstd, vl


def _sg_masked_pairs(w):
    t = lax.broadcasted_iota(jnp.int32, (CHUNK, CHUNK), 0)
    s = lax.broadcasted_iota(jnp.int32, (CHUNK, CHUNK), 1)
    causal = s <= t
    wm = [jnp.where(causal, w[g], 0.0).astype(BF16) for g in range(SG_GROUPS)]
    return [jnp.concatenate([wm[2 * j], wm[2 * j + 1]], axis=0) for j in range(SG_GROUPS // 2)], causal


def _sg_mix(vl, pairs, bias):
    tr = vl.shape[0]
    low = lax.broadcasted_iota(jnp.int32, (CHUNK, LANES), 1) < SG_GROUP_DIM
    vb = vl.astype(BF16)
    rows = []
    for c in range(tr // CHUNK):
        slabs = []
        for j in range(SG_GROUPS // 2):
            slab = vb[c * CHUNK:(c + 1) * CHUNK, j * LANES:(j + 1) * LANES]
            m = _dot(pairs[j], slab)
            slabs.append(jnp.where(low, m[:CHUNK], m[CHUNK:]))
        rows.append(jnp.concatenate(slabs, axis=1) + bias)
    return jnp.concatenate(rows, axis=0)


def _sg_fwd(z, ln_g, ln_b, sg_w, bias_full):
    def fn(u_pre, v_pre, ln_g, ln_b, w, bias):
        u, _, _, vl = _sg_common(u_pre, v_pre, ln_g, ln_b)
        pairs, _ = _sg_masked_pairs(w)
        y = u * _sg_mix(vl, pairs, bias)
        return y, y

    return _rowwise(fn, "sg_fwd", 512, [(z, SG_WIDTH, Z_U // SG_WIDTH), (z, SG_WIDTH, Z_V // SG_WIDTH)],
                    [ln_g, ln_b, sg_w, bias_full], [(SG_WIDTH, BF16), (SG_WIDTH, BF16, "T")])


def _sg_bwd(z, dy, ln_g, ln_b, sg_w, bias_full, group_ind):
    def fn(u_pre, v_pre, dy, ln_g, ln_b, w, bias, ind):
        dy = dy.astype(F32)
        u, vhat, rstd, vl = _sg_common(u_pre, v_pre, ln_g, ln_b)
        pairs, causal = _sg_masked_pairs(w)
        mixed = _sg_mix(vl, pairs, bias)
        du_pre = dy * mixed * _gelu_grad(u_pre)
        dmix = dy * u
        tr = dy.shape[0]
        low = lax.broadcasted_iota(jnp.int32, (CHUNK, LANES), 1) < SG_GROUP_DIM
        vb = vl.astype(BF16)
        dw = [jnp.zeros((CHUNK, CHUNK), F32) for _ in range(SG_GROUPS)]
        dbias = jnp.zeros((CHUNK, SG_WIDTH), F32)
        dvl_rows = []
        for c in range(tr // CHUNK):
            dm_c = dmix[c * CHUNK:(c + 1) * CHUNK]
            dbias = dbias + dm_c
            slabs = []
            for j in range(SG_GROUPS // 2):
                slab = vb[c * CHUNK:(c + 1) * CHUNK, j * LANES:(j + 1) * LANES]
                dm = dm_c[:, j * LANES:(j + 1) * LANES]
                d0 = jnp.where(low, dm, 0.0).astype(BF16)
                d1 = jnp.where(low, 0.0, dm).astype(BF16)
                dw[2 * j] = dw[2 * j] + _dot(d0, slab, "nt")
                dw[2 * j + 1] = dw[2 * j + 1] + _dot(d1, slab, "nt")
                slabs.append(_dot(pairs[j], jnp.concatenate([d0, d1], axis=0), "tn"))
            dvl_rows.append(jnp.concatenate(slabs, axis=1))
        dvl = jnp.concatenate(dvl_rows, axis=0)
        dln_g = _rsum(dvl * vhat)
        dln_b = _rsum(dvl)
        dvh = dvl * ln_g
        dv = rstd * (dvh - jnp.mean(dvh, axis=-1, keepdims=True)
                     - vhat * jnp.mean(dvh * vhat, axis=-1, keepdims=True))
        dv_pre = dv * _gelu_grad(v_pre)
        dw = jnp.stack([jnp.where(causal, d, 0.0) for d in dw], axis=0)
        dbias_t = lax.dot_general(dbias, ind, (((1,), (0,)), ((), ())), precision=lax.Precision.HIGHEST,
                                  preferred_element_type=F32)
        return du_pre, dv_pre, dw, dbias_t, dln_g, dln_b

    return _rowwise(fn, "sg_bwd", 512,
                    [(z, SG_WIDTH, Z_U // SG_WIDTH), (z, SG_WIDTH, Z_V // SG_WIDTH), dy],
                    [ln_g, ln_b, sg_w, bias_full, group_ind],
                    [(SG_WIDTH, BF16), (SG_WIDTH, BF16)],
                    [((SG_GROUPS, CHUNK, CHUNK), F32), ((CHUNK, SG_GROUPS), F32), ((1, SG_WIDTH), F32), ((1, SG_WIDTH), F32)])


def _rope(x, c, s1, s2):
    return x * c + pltpu.roll(x, LANES - MLA_ROPE // 2, 1) * s1 + pltpu.roll(x, MLA_ROPE // 2, 1) * s2


def _rope_t(d, c, s1, s2):
    return d * c + pltpu.roll(d * s1, MLA_ROPE // 2, 1) + pltpu.roll(d * s2, LANES - MLA_ROPE // 2, 1)


def _mla_post(q_pre, kv_pre, z, tabs, gq, gk):
    scale = MLA_QK ** -0.5 * LOG2E

    def fn(q_pre, k_pre, v_pre, kr, c, s1, s2, gq, gk):
        qs, ks = [], []
        for h in range(MLA_HEADS):
            sl = slice(h * LANES, (h + 1) * LANES)
            qs.append(_rope(_rms(q_pre[:, sl], gq, MLA_QK), c, s1, s2) * scale)
            ks.append(_rope(_rms(k_pre[:, sl] + kr, gk, MLA_QK), c, s1, s2))
        lane = lax.broadcasted_iota(jnp.int32, v_pre.shape, 1) & (LANES - 1)
        return jnp.concatenate(qs, axis=1), jnp.concatenate(ks, axis=1), jnp.where(lane == ONES_LANE, 1.0, v_pre)

    return _rowwise(fn, "mla_post", 256,
                    [q_pre, (kv_pre, HP, 0), (kv_pre, HP, 1), (z, LANES, Z_KR // LANES), *tabs],
                    [gq, gk], [(HP, BF16)] * 3)


def _mla_post_bwd(q_pre, kv_pre, z, tabs, gq, gk, dq, dk, dv):
    scale = MLA_QK ** -0.5

    def fn(q_pre, k_pre, kr, c, s1, s2, dq, dk, dv, gq, gk):
        lane = lax.broadcasted_iota(jnp.int32, (1, LANES), 1)
        kr_mask = (lane >= KR_LANE) & (lane < KR_LANE + MLA_ROPE)
        dqs, dks = [], []
        dgq = jnp.zeros((1, LANES), F32)
        dgk = jnp.zeros((1, LANES), F32)
        dkr = jnp.zeros(kr.shape, F32)
        for h in range(MLA_HEADS):
            sl = slice(h * LANES, (h + 1) * LANES)
            dqn = _rope_t(dq[:, sl].astype(F32), c, s1, s2) * scale
            dx, dg = _rms_bwd(q_pre[:, sl], gq, dqn, MLA_QK)
            dqs.append(dx)
            dgq = dgq + dg
            dkn = _rope_t(dk[:, sl].astype(F32), c, s1, s2)
            dx, dg = _rms_bwd(k_pre[:, sl] + kr, gk, dkn, MLA_QK)
            dks.append(dx)
            dgk = dgk + dg
            dkr = dkr + dx
        dkr = jnp.where(kr_mask, dkr, 0.0)
        dkv = jnp.concatenate(dks + [dv.astype(F32)], axis=1)
        return jnp.concatenate(dqs, axis=1), dkv, dkr, dgq, dgk

    return _rowwise(fn, "mla_post_bwd", 256,
                    [q_pre, (kv_pre, HP, 0), (z, LANES, Z_KR // LANES), *tabs, dq, dk, dv],
                    [gq, gk], [(HP, BF16), (2 * HP, BF16), (LANES, BF16)],
                    [((1, LANES), F32), ((1, LANES), F32)])


def _pairs(n, lower):
    a, b = [], []
    for o in range(n):
        inner = range(o + 1) if lower else range(o, n)
        for t in inner:
            a.append(o)
            b.append(t)
    return jnp.asarray(np.array(a, np.int32)), jnp.asarray(np.array(b, np.int32))


FLASH_TILE, FLASH_SUB_ROWS = 1024, 512
LOG2E, LN2 = 1.4426950408889634, 0.6931471805599453
ONES_LANE = MLA_V


def _flash_tiles(T):
    tq = _tile(T, FLASH_TILE)
    return tq, _tile(tq, FLASH_SUB_ROWS)


def _col_span(t, sr, rb, diag, key_major):
    if not diag:
        return 0, t
    return (rb * sr, t) if key_major else (0, (rb + 1) * sr)


def _span_iota(sr, rb, c0, c1):
    r = lax.broadcasted_iota(jnp.int32, (sr, c1 - c0), 0) + rb * sr
    c = lax.broadcasted_iota(jnp.int32, (sr, c1 - c0), 1) + c0
    return r, c


def _lanes(x, width):
    return jnp.concatenate([x] * (width // LANES), axis=1)


def _flash_fwd(q, k, v):
    T = q.shape[0]
    tq, sr = _flash_tiles(T)
    n = T // tq
    ii, jj = _pairs(n, True)

    def body(ii_ref, jj_ref, q_ref, k_ref, v_ref, o_ref, ot_ref, lse_ref, m_sc, acc_sc):
        p_ = pl.program_id(1)
        i, j = ii_ref[p_], jj_ref[p_]

        @pl.when(j == 0)
        def _():
            m_sc[...] = jnp.full(m_sc.shape, NEG, F32)
            acc_sc[...] = jnp.zeros(acc_sc.shape, F32)

        def tile(diag):
            for rb in range(tq // sr):
                rows = slice(rb * sr, (rb + 1) * sr)
                c0, c1 = _col_span(tq, sr, rb, diag, False)
                s = _dot(q_ref[rows, :], k_ref[c0:c1, :], "nt")
                if diag:
                    r, c = _span_iota(sr, rb, c0, c1)
                    s = jnp.where(c <= r, s, NEG)
                m = m_sc[rows, :]
                m_new = jnp.maximum(m, jnp.max(s, axis=1, keepdims=True))
                p = jnp.exp2(s - _lanes(m_new, c1 - c0))
                acc_sc[rows, :] = jnp.exp2(m - m_new) * acc_sc[rows, :] + _dot(p, v_ref[c0:c1, :])
                m_sc[rows, :] = m_new

        @pl.when(j < i)
        def _():
            tile(False)

        @pl.when(j == i)
        def _():
            tile(True)
            acc = acc_sc[...]
            lane = lax.broadcasted_iota(jnp.int32, acc.shape, 1)
            l = jnp.sum(jnp.where(lane == ONES_LANE, acc, 0.0), axis=1, keepdims=True)
            o = jnp.where(lane < MLA_V, acc / l, 0.0)
            o_ref[...] = o.astype(o_ref.dtype)
            ot_ref[...] = o.T.astype(ot_ref.dtype)
            lse_ref[...] = m_sc[...] + jnp.log2(l)

    blk = lambda which: pl.BlockSpec((tq, LANES), which)
    qmap = lambda h, p, ii, jj: (ii[p], h)
    kmap = lambda h, p, ii, jj: (jj[p], h)
    return pl.pallas_call(
        body, name="mla_flash_fwd",
        grid_spec=pltpu.PrefetchScalarGridSpec(
            num_scalar_prefetch=2, grid=(MLA_HEADS, int(ii.shape[0])),
            in_specs=[blk(qmap), blk(kmap), blk(kmap)],
            out_specs=[blk(qmap), pl.BlockSpec((LANES, tq), lambda h, p, ii, jj: (h, ii[p])), blk(qmap)],
            scratch_shapes=[pltpu.VMEM((tq, LANES), F32)] * 2),
        out_shape=[jax.ShapeDtypeStruct((T, HP), BF16), jax.ShapeDtypeStruct((HP, T), BF16),
                   jax.ShapeDtypeStruct((T, HP), F32)],
        compiler_params=_cparams(("parallel", "arbitrary")),
    )(ii, jj, q, k, v)


def _flash_dq(q, k, v, do, lse, delta):
    T = q.shape[0]
    tq, sr = _flash_tiles(T)
    n = T // tq
    ii, jj = _pairs(n, True)

    def body(ii_ref, jj_ref, q_ref, k_ref, v_ref, do_ref, lse_ref, dl_ref, dq_ref, acc_sc):
        p_ = pl.program_id(1)
        i, j = ii_ref[p_], jj_ref[p_]

        @pl.when(j == 0)
        def _():
            acc_sc[...] = jnp.zeros(acc_sc.shape, F32)

        def tile(diag):
            for rb in range(tq // sr):
                rows = slice(rb * sr, (rb + 1) * sr)
                c0, c1 = _col_span(tq, sr, rb, diag, False)
                ks = k_ref[c0:c1, :]
                p = jnp.exp2(_dot(q_ref[rows, :], ks, "nt") - _lanes(lse_ref[rows, :], c1 - c0))
                if diag:
                    r, c = _span_iota(sr, rb, c0, c1)
                    p = jnp.where(c <= r, p, 0.0)
                dp = _dot(do_ref[rows, :], v_ref[c0:c1, :], "nt")
                acc_sc[rows, :] += _dot(p * (dp - _lanes(dl_ref[rows, :], c1 - c0)), ks)

        @pl.when(j < i)
        def _():
            tile(False)

        @pl.when(j == i)
        def _():
            tile(True)
            dq_ref[...] = acc_sc[...]

    blk = lambda which: pl.BlockSpec((tq, LANES), which)
    qmap = lambda h, p, ii, jj: (ii[p], h)
    kmap = lambda h, p, ii, jj: (jj[p], h)
    return pl.pallas_call(
        body, name="mla_flash_dq",
        grid_spec=pltpu.PrefetchScalarGridSpec(
            num_scalar_prefetch=2, grid=(MLA_HEADS, int(ii.shape[0])),
            in_specs=[blk(qmap), blk(kmap), blk(kmap), blk(qmap), blk(qmap), blk(qmap)],
            out_specs=blk(qmap),
            scratch_shapes=[pltpu.VMEM((tq, LANES), F32)]),
        out_shape=jax.ShapeDtypeStruct((T, HP), F32),
        compiler_params=_cparams(("parallel", "arbitrary")),
    )(ii, jj, q, k, v, do, lse, delta)


def _flash_dkv(q, k, v, do, lse_row, delta_row):
    T = q.shape[0]
    tq, sr = _flash_tiles(T)
    n = T // tq
    jj, ii = _pairs(n, False)

    def body(jj_ref, ii_ref, q_ref, k_ref, v_ref, do_ref, lse_ref, dl_ref, dk_ref, dv_ref, dk_sc, dv_sc):
        p_ = pl.program_id(1)
        j, i = jj_ref[p_], ii_ref[p_]

        @pl.when(i == j)
        def _():
            dk_sc[...] = jnp.zeros(dk_sc.shape, F32)
            dv_sc[...] = jnp.zeros(dv_sc.shape, F32)

        def tile(diag):
            for rb in range(tq // sr):
                rows = slice(rb * sr, (rb + 1) * sr)
                c0, c1 = _col_span(tq, sr, rb, diag, True)
                qs, dos = q_ref[c0:c1, :], do_ref[c0:c1, :]
                pt = jnp.exp2(_dot(k_ref[rows, :], qs, "nt") - lse_ref[:, c0:c1])
                if diag:
                    r, c = _span_iota(sr, rb, c0, c1)
                    pt = jnp.where(r <= c, pt, 0.0)
                dpt = _dot(v_ref[rows, :], dos, "nt")
                dv_sc[rows, :] += _dot(pt, dos)
                dk_sc[rows, :] += _dot(pt * (dpt - dl_ref[:, c0:c1]), qs)

        @pl.when(i == j)
        def _():
            tile(True)

        @pl.when(i > j)
        def _():
            tile(False)

        @pl.when(i == n - 1)
        def _():
            dk_ref[...] = dk_sc[...] * LN2
            dv_ref[...] = dv_sc[...]

    blk = lambda which: pl.BlockSpec((tq, LANES), which)
    qmap = lambda h, p, jj, ii: (ii[p], h)
    kmap = lambda h, p, jj, ii: (jj[p], h)
    row = pl.BlockSpec((None, 1, tq), lambda h, p, jj, ii: (h, 0, ii[p]))
    return pl.pallas_call(
        body, name="mla_flash_dkv",
        grid_spec=pltpu.PrefetchScalarGridSpec(
            num_scalar_prefetch=2, grid=(MLA_HEADS, int(ii.shape[0])),
            in_specs=[blk(qmap), blk(kmap), blk(kmap), blk(qmap), row, row],
            out_specs=[blk(kmap), blk(kmap)],
            scratch_shapes=[pltpu.VMEM((tq, LANES), F32)] * 2),
        out_shape=[jax.ShapeDtypeStruct((T, HP), F32)] * 2,
        compiler_params=_cparams(("parallel", "arbitrary")),
    )(jj, ii, q, k, v, do, lse_row, delta_row)


def _mem_fwd(z, km, vm, gq):
    scale = MEM_HEAD_DIM ** -0.5

    def fn(qm, km, vm, gq):
        ys = []
        for h in range(MEM_HEADS):
            sl = slice(h * LANES, (h + 1) * LANES)
            q = _rms(qm[:, sl], gq) * scale
            s = _dot(q, km[:, sl], "nt")
            p = jnp.exp(s - jnp.max(s, axis=1, keepdims=True))
            p = p / jnp.sum(p, axis=1, keepdims=True)
            ys.append(_dot(p, vm[:, sl]))
        y = jnp.concatenate(ys, axis=1)
        return y, y

    return _rowwise(fn, "mem_fwd", 512, [(z, MEM_WIDTH, Z_QM // MEM_WIDTH)], [km, vm, gq],
                    [(MEM_WIDTH, BF16), (MEM_WIDTH, BF16, "T")])


def _mem_bwd(z, dy, km, vm, gq):
    scale = MEM_HEAD_DIM ** -0.5

    def fn(qm, dy, km, vm, gq):
        dqs, dks, dvs = [], [], []
        dgq = jnp.zeros((1, LANES), F32)
        for h in range(MEM_HEADS):
            sl = slice(h * LANES, (h + 1) * LANES)
            q = (_rms(qm[:, sl], gq) * scale).astype(BF16)
            dyh = dy[:, sl]
            kh, vh = km[:, sl], vm[:, sl]
            s = _dot(q, kh, "nt")
            p = jnp.exp(s - jnp.max(s, axis=1, keepdims=True))
            p = p / jnp.sum(p, axis=1, keepdims=True)
            dp = _dot(dyh, vh, "nt")
            ds = p * (dp - jnp.sum(p * dp, axis=1, keepdims=True))
            dq = _dot(ds, kh) * scale
            dx, dg = _rms_bwd(qm[:, sl], gq, dq)
            dqs.append(dx)
            dgq = dgq + dg
            st = _dot(kh, q, "nt")
            pt = jnp.exp(st - jnp.max(st, axis=0, keepdims=True))
            pt = pt / jnp.sum(pt, axis=0, keepdims=True)
            dpt = _dot(vh, dyh, "nt")
            dst = pt * (dpt - jnp.sum(pt * dpt, axis=0, keepdims=True))
            dvs.append(_dot(pt, dyh))
            dks.append(_dot(dst, q))
        return jnp.concatenate(dqs, axis=1), jnp.concatenate(dks, axis=1), jnp.concatenate(dvs, axis=1), dgq

    m = km.shape[0]
    return _rowwise(fn, "mem_bwd", 512, [(z, MEM_WIDTH, Z_QM // MEM_WIDTH), dy], [km, vm, gq],
                    [(MEM_WIDTH, BF16)], [((m, MEM_WIDTH), F32), ((m, MEM_WIDTH), F32), ((1, LANES), F32)])


GROUPS = {"ffn1": ["ffn1_w_gu", "ffn1_w_down"],
          "mix": ["w_in", "mla_w_uq", "mla_w_ukv", "mem_w_kv", "w_branch_a", "w_branch_b", "w_branch_c", "w_out"],
          "ffn2": ["ffn2_w_gu", "ffn2_w_down"]}


def _local_step(x, mem, positions, loss_target, P, weights, grads_out):
    T = x.shape[0]
    G = {}
    W = dict(weights("ffn1", None))

    half = MLA_ROPE // 2
    inv = ROPE_BASE ** (-jnp.arange(half, dtype=F32) / half)
    ang = positions.astype(F32)[:, None] * inv
    cos, sin = jnp.cos(ang), jnp.sin(ang)
    one, zero = jnp.ones((T, MLA_NOPE), F32), jnp.zeros((T, half), F32)
    pad = LANES - MLA_QK
    tabs = (jnp.concatenate([one, cos, cos, jnp.ones((T, pad), F32)], axis=1),
            jnp.concatenate([jnp.zeros((T, MLA_NOPE), F32), -sin, zero, jnp.zeros((T, pad), F32)], axis=1),
            jnp.concatenate([jnp.zeros((T, MLA_NOPE), F32), zero, sin, jnp.zeros((T, pad), F32)], axis=1))
    gq_p = jnp.pad(P["mla_q_norm"], ((0, 0), (0, pad)))
    gk_p = jnp.pad(P["mla_k_norm"], ((0, 0), (0, pad)))
    bias_full = jnp.repeat(P["sg_b"].T, SG_GROUP_DIM, axis=1)
    group_ind = jnp.repeat(jnp.eye(SG_GROUPS, dtype=F32), SG_GROUP_DIM, axis=0)

    HT = (D_MODEL, BF16, "T")

    def norm2(x, g):
        h = _rms(x, g)
        return h, h

    h1, h1t = _rowwise(norm2, "ffn1_norm", 512, [x], [P["ffn1_norm"]], [(D_MODEL, BF16), HT])
    gu1, a1t, o1 = _ffn_fwd(h1, W["ffn1_w_gu"], W["ffn1_w_down"], "ffn1")

    def resid_norm(x, o, g):
        xn = x + 0.5 * o
        h = _rms(xn, g)
        return xn, h, h

    x1, hm, hmt = _rowwise(resid_norm, "mix_norm", 512, [x, o1], [P["mix_norm"]],
                           [(D_MODEL, F32), (D_MODEL, BF16), HT])
    W.update(weights("mix", hm))
    z = _mm(hm, W["w_in"], "nn", F32, "w_in", tm=1024, tn=768)

    y_a, y_at = _sg_fwd(z, P["sg_ln_g"], P["sg_ln_b"], P["sg_w"], bias_full)

    def c_norm(cq, ckv, gq, gkv):
        a, b = _rms(cq, gq), _rms(ckv, gkv)
        return a, b, a, b

    cqn, ckvn, cqnt, ckvnt = _rowwise(
        c_norm, "mla_cnorm", 512, [(z, MLA_Q_RANK, Z_CQ // MLA_Q_RANK), (z, MLA_KV_RANK, Z_CKV // MLA_KV_RANK)],
        [P["mla_cq_norm"], P["mla_ckv_norm"]],
        [(MLA_Q_RANK, BF16), (MLA_KV_RANK, BF16), (MLA_Q_RANK, BF16, "T"), (MLA_KV_RANK, BF16, "T")])
    q_pre = _mm(cqn, W["mla_w_uq"], "nn", F32, "mla_uq", tm=1024, tn=1024)
    kv_pre = _mm(ckvn, W["mla_w_ukv"], "nn", F32, "mla_ukv", tm=1024, tn=1024)
    q, k, v = _mla_post(q_pre, kv_pre, z, tabs, gq_p, gk_p)
    y_b, y_bt, lse = _flash_fwd(q, k, v)

    memn, = _rowwise(lambda m, g: _rms(m, g), "mem_norm", 256, [mem], [P["mem_norm"]], [(D_MODEL, BF16)])
    kvm = _mm(memn, W["mem_w_kv"], "nn", F32, "mem_kv")

    def mem_k(kvm, gk):
        ks = [_rms(kvm[:, h * LANES:(h + 1) * LANES], gk) for h in range(MEM_HEADS)]
        return jnp.concatenate(ks, axis=1), kvm[:, MEM_WIDTH:]

    km, vm = _rowwise(mem_k, "mem_knorm", 256, [kvm], [P["mem_k_norm"]], [(MEM_WIDTH, BF16), (MEM_WIDTH, BF16)])
    y_c, y_ct = _mem_fwd(z, km, vm, P["mem_q_norm"])

    pa = _mm(y_a, W["w_branch_a"], "nn", F32, "branch_a", tm=1024, tn=1024)
    pb = _mm(y_b, W["w_branch_b"], "nn", F32, "branch_b", tm=1024, tn=1024)
    pc = _mm(y_c, W["w_branch_c"], "nn", F32, "branch_c", tm=1024, tn=1024)

    def merge(zg, pa, pb, pc, b):
        g = _sigmoid(zg + b)
        m = g[:, :D_MODEL] * pa + g[:, D_MODEL:2 * D_MODEL] * pb + g[:, 2 * D_MODEL:] * pc
        return m, m

    merged, mergedt = _rowwise(merge, "merge", 256, [(z, 3 * D_MODEL, 0), pa, pb, pc], [P["b_gate"]],
                               [(D_MODEL, BF16), HT])
    om = _mm(merged, W["w_out"], "nn", F32, "w_out", tm=1024, tn=1024)

    def resid_norm1(x, o, g):
        xn = x + o
        h = _rms(xn, g)
        return xn, h, h

    x2, h2, h2t = _rowwise(resid_norm1, "ffn2_norm", 512, [x1, om], [P["ffn2_norm"]],
                           [(D_MODEL, F32), (D_MODEL, BF16), HT])
    W.update(weights("ffn2", h2))
    gu2, a2t, o2 = _ffn_fwd(h2, W["ffn2_w_gu"], W["ffn2_w_down"], "ffn2")

    def loss_fn(x2, o2, t):
        e = x2 + 0.5 * o2 - t
        return e * (1.0 / D_MODEL), (e * (0.5 / D_MODEL)).astype(BF16), _rsum(e * e) * (0.5 / D_MODEL)

    dx3, do2, loss_part = _rowwise(loss_fn, "loss", 512, [x2, o2, loss_target], [],
                                   [(D_MODEL, F32), (D_MODEL, BF16)], [((1, D_MODEL), F32)])

    dh2, G["ffn2_w_gu"], G["ffn2_w_down"] = _ffn_bwd(do2, h2t, gu2, a2t, W["ffn2_w_gu"], W["ffn2_w_down"], "ffn2")
    tie = grads_out("ffn2", G, None)

    def norm_bwd(x, dh, dxo, g, *_):
        dx, dg = _rms_bwd(x, g, dh)
        dx = dx + dxo
        return dx, dx, dg

    dx2, dx2b, G["ffn2_norm"] = _rowwise(norm_bwd, "ffn2_norm_bwd", 512, [x2, dh2, dx3],
                                         [P["ffn2_norm"]] + ([] if tie is None else [tie]),
                                         [(D_MODEL, F32), (D_MODEL, BF16)], [((1, D_MODEL), F32)])

    G["w_out"] = _mm_t(mergedt, dx2b, "w_out_dw", tm=1024, tn=1024)
    dmerged = _mm(dx2b, W["w_out"], "nt", F32, "w_out_dx", tm=1024, tn=1024)

    def merge_bwd(zg, pa, pb, pc, dm, b):
        g = _sigmoid(zg + b)
        ps = jnp.concatenate([pa, pb, pc], axis=1)
        dm3 = jnp.concatenate([dm, dm, dm], axis=1)
        dzg = dm3 * ps * g * (1.0 - g)
        dp = dm3 * g
        return dzg, dp[:, :D_MODEL], dp[:, D_MODEL:2 * D_MODEL], dp[:, 2 * D_MODEL:], _rsum(dzg)

    dzg, dpa, dpb, dpc, G["b_gate"] = _rowwise(
        merge_bwd, "merge_bwd", 256, [(z, 3 * D_MODEL, 0), pa, pb, pc, dmerged], [P["b_gate"]],
        [(3 * D_MODEL, BF16), (D_MODEL, BF16), (D_MODEL, BF16), (D_MODEL, BF16)], [((1, 3 * D_MODEL), F32)])

    G["w_branch_a"] = _mm_t(y_at, dpa, "branch_a_dw", tm=512, tn=1024)
    G["w_branch_b"] = _mm_t(y_bt, dpb, "branch_b_dw", tm=1024, tn=1024)
    G["w_branch_c"] = _mm_t(y_ct, dpc, "branch_c_dw", tm=512, tn=1024)
    dy_a = _mm(dpa, W["w_branch_a"], "nt", BF16, "branch_a_dx", tm=1024, tn=512)
    dy_b = _mm(dpb, W["w_branch_b"], "nt", BF16, "branch_b_dx", tm=1024, tn=1024)
    dy_c = _mm(dpc, W["w_branch_c"], "nt", BF16, "branch_c_dx", tm=1024, tn=512)

    du_pre, dv_pre, G["sg_w"], dbias_t, G["sg_ln_g"], G["sg_ln_b"] = _sg_bwd(
        z, dy_a, P["sg_ln_g"], P["sg_ln_b"], P["sg_w"], bias_full, group_ind)
    G["sg_b"] = dbias_t.T

    dqm, dkm, dvm, G["mem_q_norm"] = _mem_bwd(z, dy_c, km, vm, P["mem_q_norm"])

    def mem_k_bwd(kvm, dkm, dvm, gk):
        dks = []
        dg = jnp.zeros((1, LANES), F32)
        for h in range(MEM_HEADS):
            sl = slice(h * LANES, (h + 1) * LANES)
            dx, d = _rms_bwd(kvm[:, sl], gk, dkm[:, sl])
            dks.append(dx)
            dg = dg + d
        return jnp.concatenate(dks + [dvm], axis=1), dg

    dkvm, G["mem_k_norm"] = _rowwise(mem_k_bwd, "mem_knorm_bwd", 256, [kvm, dkm, dvm], [P["mem_k_norm"]],
                                     [(2 * MEM_WIDTH, BF16)], [((1, LANES), F32)])
    G["mem_w_kv"] = _mm(memn, dkvm, "tn", BF16, "mem_kv_dw")
    dmemn = _mm(dkvm, W["mem_w_kv"], "nt", F32, "mem_kv_dx")
    _, G["mem_norm"] = _rowwise(lambda m, d, g: _rms_bwd(m, g, d), "mem_norm_bwd", 256, [mem, dmemn],
                                [P["mem_norm"]], [(D_MODEL, BF16)], [((1, D_MODEL), F32)])

    def delta_fn(o, do):
        od = o.astype(F32) * do.astype(F32)
        ds = [jnp.broadcast_to(jnp.sum(od[:, h * LANES:(h + 1) * LANES], axis=1, keepdims=True), (od.shape[0], LANES))
              for h in range(MLA_HEADS)]
        return jnp.concatenate(ds, axis=1)

    delta, = _rowwise(delta_fn, "mla_delta", 512, [y_b, dy_b], [], [(HP, F32)])
    rowform = lambda a: a.reshape(T, MLA_HEADS, LANES)[:, :, 0].T.reshape(MLA_HEADS, 1, T)
    dq = _flash_dq(q, k, v, dy_b, lse, delta)
    dk, dv = _flash_dkv(q, k, v, dy_b, rowform(lse), rowform(delta))
    dq_pre, dkv_pre, dkr, dgq, dgk = _mla_post_bwd(q_pre, kv_pre, z, tabs, gq_p, gk_p, dq, dk, dv)
    G["mla_q_norm"], G["mla_k_norm"] = dgq[:, :MLA_QK], dgk[:, :MLA_QK]
    G["mla_w_uq"] = _mm_t(cqnt, dq_pre, "mla_uq_dw", tm=384, tn=1024)
    G["mla_w_ukv"] = _mm_t(ckvnt, dkv_pre, "mla_ukv_dw", tm=256, tn=2048)
    dcqn = _mm(dq_pre, W["mla_w_uq"], "nt", F32, "mla_uq_dx", tm=1024)
    dckvn = _mm(dkv_pre, W["mla_w_ukv"], "nt", F32, "mla_ukv_dx", tm=1024)

    def c_norm_bwd(cq, ckv, dcqn, dckvn, gq, gkv):
        dcq, dgq = _rms_bwd(cq, gq, dcqn)
        dckv, dgkv = _rms_bwd(ckv, gkv, dckvn)
        return dcq, dckv, dgq, dgkv

    dcq, dckv, G["mla_cq_norm"], G["mla_ckv_norm"] = _rowwise(
        c_norm_bwd, "mla_cnorm_bwd", 512,
        [(z, MLA_Q_RANK, Z_CQ // MLA_Q_RANK), (z, MLA_KV_RANK, Z_CKV // MLA_KV_RANK), dcqn, dckvn],
        [P["mla_cq_norm"], P["mla_ckv_norm"]], [(MLA_Q_RANK, BF16), (MLA_KV_RANK, BF16)],
        [((1, MLA_Q_RANK), F32), ((1, MLA_KV_RANK), F32)])

    dz = jnp.concatenate([dzg, du_pre, dv_pre, dqm, dckv, dkr, dcq], axis=1)
    G["w_in"] = _mm_t(hmt, dz, "w_in_dw", tm=1024, tn=768)
    dhm = _mm(dz, W["w_in"], "nt", F32, "w_in_dx", tm=1024, tn=1024, tk=2688)

    def norm_bwd_half(x, dh, dxo, g):
        dx, dg = _rms_bwd(x, g, dh)
        dx = dx + dxo
        return dx, (0.5 * dx), dg

    dx1, do1, G["mix_norm"] = _rowwise(norm_bwd_half, "mix_norm_bwd", 512, [x1, dhm, dx2], [P["mix_norm"]],
                                       [(D_MODEL, F32), (D_MODEL, BF16)], [((1, D_MODEL), F32)])
    tie = grads_out("mix", G, do1)
    dh1, G["ffn1_w_gu"], G["ffn1_w_down"] = _ffn_bwd(do1, h1t, gu1, a1t, W["ffn1_w_gu"], W["ffn1_w_down"], "ffn1",
                                                     tie)

    def norm_bwd_last(x, dh, dxo, g):
        dx, dg = _rms_bwd(x, g, dh)
        return dx + dxo, dg

    grad_x, G["ffn1_norm"] = _rowwise(norm_bwd_last, "ffn1_norm_bwd", 512, [x, dh1, dx1], [P["ffn1_norm"]],
                                      [(D_MODEL, F32)], [((1, D_MODEL), F32)])
    grads_out("ffn1", G, grad_x)
    return loss_part, grad_x, G


SHARDED = ["ffn1_w_gu", "ffn1_w_down", "w_in", "mla_w_uq", "mla_w_ukv", "mem_w_kv",
           "w_branch_a", "w_branch_b", "w_branch_c", "w_out", "ffn2_w_gu", "ffn2_w_down"]
ROW_SHARDED = {"ffn1_w_down", "mem_w_kv", "w_out", "ffn2_w_down"}
SMALL = ["ffn1_norm", "mix_norm", "b_gate", "sg_ln_g", "sg_ln_b", "sg_w", "sg_b", "mla_cq_norm",
         "mla_ckv_norm", "mla_q_norm", "mla_k_norm", "mem_norm", "mem_q_norm", "mem_k_norm", "ffn2_norm"]
ORDER = ["ffn1_norm", "ffn1_w_gu", "ffn1_w_down", "mix_norm", "w_in", "b_gate", "sg_ln_g", "sg_ln_b", "sg_w",
         "sg_b", "mla_cq_norm", "mla_w_uq", "mla_ckv_norm", "mla_w_ukv", "mla_q_norm", "mla_k_norm", "mem_norm",
         "mem_w_kv", "mem_q_norm", "mem_k_norm", "w_branch_a", "w_branch_b", "w_branch_c", "w_out", "ffn2_norm",
         "ffn2_w_gu", "ffn2_w_down"]

_IN_U, _IN_V, _IN_CQ, _IN_CKV, _IN_KR, _IN_QM, _IN_G = 0, 512, 1024, 1408, 1664, 1696, 2208
IN_COLS = 5280


def _full_from_slabs(name, slabs):
    n, r, c = slabs.shape
    if name in ROW_SHARDED:
        return slabs.reshape(n * r, c)
    return slabs.transpose(1, 0, 2).reshape(r, n * c)


def _slabs_from_full(name, full):
    if name in ROW_SHARDED:
        return full.reshape(N_DEV, full.shape[0] // N_DEV, full.shape[1])
    r, c = full.shape
    return full.reshape(r, N_DEV, c // N_DEV).transpose(1, 0, 2)


def _compute_layout(full):
    W = dict(full)
    if "w_in" not in full:
        return W
    w = full["w_in"]
    kr = jnp.pad(w[:, _IN_KR:_IN_QM], ((0, 0), (KR_LANE, LANES - KR_LANE - MLA_ROPE)))
    W["w_in"] = jnp.concatenate([w[:, _IN_G:], w[:, _IN_U:_IN_CQ], w[:, _IN_QM:_IN_G], w[:, _IN_CKV:_IN_KR], kr,
                                 w[:, _IN_CQ:_IN_CKV]], axis=1)
    uq = full["mla_w_uq"].reshape(MLA_Q_RANK, MLA_HEADS, MLA_QK)
    W["mla_w_uq"] = jnp.pad(uq, ((0, 0), (0, 0), (0, LANES - MLA_QK))).reshape(MLA_Q_RANK, HP)
    ukv = full["mla_w_ukv"].reshape(MLA_KV_RANK, MLA_HEADS, MLA_NOPE + MLA_V)
    padh = lambda a: jnp.pad(a, ((0, 0), (0, 0), (0, LANES - a.shape[2]))).reshape(MLA_KV_RANK, HP)
    W["mla_w_ukv"] = jnp.concatenate([padh(ukv[:, :, :MLA_NOPE]), padh(ukv[:, :, MLA_NOPE:])], axis=1)
    wb = full["w_branch_b"].reshape(MLA_HEADS, MLA_V, D_MODEL)
    W["w_branch_b"] = jnp.pad(wb, ((0, 0), (0, LANES - MLA_V), (0, 0))).reshape(HP, D_MODEL)
    return W


def _reference_layout(G):
    out = dict(G)
    if "w_in" not in G:
        return out
    g = G["w_in"]
    out["w_in"] = jnp.concatenate([
        g[:, Z_U:Z_QM], g[:, Z_CQ:Z_COLS], g[:, Z_CKV:Z_KR], g[:, Z_KR + KR_LANE:Z_KR + KR_LANE + MLA_ROPE],
        g[:, Z_QM:Z_CKV], g[:, Z_G:Z_U]], axis=1)
    out["mla_w_uq"] = G["mla_w_uq"].reshape(MLA_Q_RANK, MLA_HEADS, LANES)[:, :, :MLA_QK].reshape(MLA_Q_RANK, -1)
    gk = G["mla_w_ukv"][:, :HP].reshape(MLA_KV_RANK, MLA_HEADS, LANES)[:, :, :MLA_NOPE]
    gv = G["mla_w_ukv"][:, HP:].reshape(MLA_KV_RANK, MLA_HEADS, LANES)[:, :, :MLA_V]
    out["mla_w_ukv"] = jnp.concatenate([gk, gv], axis=2).reshape(MLA_KV_RANK, -1)
    out["w_branch_b"] = G["w_branch_b"].reshape(MLA_HEADS, LANES, D_MODEL)[:, :MLA_V].reshape(-1, D_MODEL)
    return out


def _pack(parts):
    flat = []
    for a in parts:
        a = a.reshape(-1)
        flat.append(jnp.pad(a, (0, (-a.shape[0]) % LANES)))
    return jnp.concatenate(flat).reshape(-1, LANES)


def _unpack(packed, shapes):
    flat = packed.reshape(-1)
    out, off = [], 0
    for shp in shapes:
        n = int(np.prod(shp))
        out.append(flat[off:off + n].reshape(shp))
        off += n + (-n) % LANES
    return out


MESH = pl.DeviceIdType.MESH
HBM = pl.BlockSpec(memory_space=pltpu.HBM)


def _all_gather(shard):
    rows, lanes = shard.shape

    def body(x_ref, out_ref, token_ref, send_sems, recv_sems, local_sem):
        x, y, c = lax.axis_index("x"), lax.axis_index("y"), lax.axis_index("c")
        me, sibling = (x, y, c), (x, y, 1 - c)
        chips = [(1 - x, y), (x, 1 - y), (1 - x, 1 - y)]
        token_ref[...] = jnp.zeros_like(token_ref)

        def slot(px, py, pc):
            return out_ref.at[4 * px + 2 * py + pc]

        def copy(k, block, to, src=None):
            return pltpu.make_async_remote_copy(
                src_ref=slot(*block) if src is None else src, dst_ref=slot(*block),
                send_sem=send_sems.at[k], recv_sem=recv_sems.at[k], device_id=to, device_id_type=MESH)

        mine = pltpu.make_async_copy(x_ref, slot(*me), local_sem)
        mine.start()
        first = [copy(0, me, sibling, src=x_ref)]
        first += [copy(1 + j, me, (*chip, c), src=x_ref) for j, chip in enumerate(chips)]
        for cp in first:
            cp.start()
        passed = [copy(4 + j, (*chip, c), sibling) for j, chip in enumerate(chips)]
        for j, chip in enumerate(chips):
            copy(1 + j, (*chip, c), me).wait_recv()
            passed[j].start()
        copy(0, sibling, me).wait_recv()
        for j, chip in enumerate(chips):
            copy(4 + j, (*chip, 1 - c), me).wait_recv()
        for cp in first + passed:
            cp.wait_send()
        mine.wait()

    return pl.pallas_call(
        body, name="all_gather_weights",
        out_shape=[jax.ShapeDtypeStruct((N_DEV, rows, lanes), shard.dtype), jax.ShapeDtypeStruct((8, LANES), F32)],
        in_specs=[HBM], out_specs=[HBM, pl.BlockSpec(memory_space=pltpu.VMEM)],
        scratch_shapes=[pltpu.SemaphoreType.DMA((7,)), pltpu.SemaphoreType.DMA((7,)), pltpu.SemaphoreType.DMA],
    )(shard)


SEM = pl.BlockSpec(memory_space=pltpu.SEMAPHORE)
DATAFLOW = pltpu.SideEffectType.DATAFLOW_SIDE_EFFECTING


def _peers():
    x, y, c = lax.axis_index("x"), lax.axis_index("y"), lax.axis_index("c")
    out = []
    for k in range(1, N_DEV):
        px = 1 - x if k & 4 else x
        py = 1 - y if k & 2 else y
        pc = 1 - c if k & 1 else c
        out.append((k, (px, py, pc), 4 * px + 2 * py + pc))
    return 4 * x + 2 * y + c, out


def _send_start(src, per_peer, name):
    rows = src.shape[-2]
    land = lax.empty((N_DEV, rows, LANES), src.dtype)

    def body(src_ref, land_ref, send_sems, recv_sems, src_thru, land_thru, token):
        me, peers = _peers()
        for k, pid, pflat in peers:
            pltpu.make_async_remote_copy(
                src_ref=src_ref.at[pflat] if per_peer else src_ref, dst_ref=land_ref.at[me],
                send_sem=send_sems.at[k - 1], recv_sem=recv_sems.at[k - 1],
                device_id=pid, device_id_type=MESH).start()
        token[...] = jnp.zeros_like(token)

    res = pl.pallas_call(
        body, name=name,
        out_shape=(pltpu.SemaphoreType.DMA((N_DEV - 1,)), pltpu.SemaphoreType.DMA((N_DEV - 1,)),
                   pltpu.HBM(src.shape, src.dtype), pltpu.HBM(land.shape, land.dtype),
                   jax.ShapeDtypeStruct((8, LANES), F32)),
        in_specs=(HBM, HBM), out_specs=(SEM, SEM, HBM, HBM, pl.BlockSpec(memory_space=pltpu.VMEM)),
        input_output_aliases={0: 2, 1: 3},
        compiler_params=pltpu.CompilerParams(has_side_effects=DATAFLOW),
    )(pltpu.with_memory_space_constraint(src, pltpu.HBM), pltpu.with_memory_space_constraint(land, pltpu.HBM))
    return res[:4], res[4]


def _send_wait(started, after, per_peer, name):
    send_sems, recv_sems, src_thru, land_thru = started

    def body(src_ref, land_ref, send_sems, recv_sems, after_ref, src_out, land_out):
        me, peers = _peers()
        for k, pid, pflat in peers:
            copy = pltpu.make_async_remote_copy(
                src_ref=src_ref.at[pflat] if per_peer else src_ref, dst_ref=land_ref.at[pflat],
                send_sem=send_sems.at[k - 1], recv_sem=recv_sems.at[k - 1],
                device_id=pid, device_id_type=MESH)
            copy.wait_send()
            copy.wait_recv()

    src_out, land = pl.pallas_call(
        body, name=name,
        out_shape=(pltpu.HBM(src_thru.shape, src_thru.dtype), pltpu.HBM(land_thru.shape, land_thru.dtype)),
        in_specs=(HBM, HBM, SEM, SEM, pl.BlockSpec(memory_space=pl.ANY)), out_specs=(HBM, HBM),
        input_output_aliases={0: 0, 1: 1},
        compiler_params=pltpu.CompilerParams(has_side_effects=DATAFLOW),
    )(src_thru, land_thru, send_sems, recv_sems, after)
    me = 4 * lax.axis_index("x") + 2 * lax.axis_index("y") + lax.axis_index("c")
    own = lax.dynamic_index_in_dim(src_out, me, 0, keepdims=True) if per_peer else src_out[None]
    return lax.dynamic_update_slice(land, own, (me, 0, 0))


def _exchange(big, small):
    def body(big_ref, small_ref, bout_ref, sout_ref, send_sems, recv_sems, local_sems):
        x, y, c = lax.axis_index("x"), lax.axis_index("y"), lax.axis_index("c")
        me = 4 * x + 2 * y + c
        own = [pltpu.make_async_copy(big_ref.at[me], bout_ref.at[me], local_sems.at[0]),
               pltpu.make_async_copy(small_ref, sout_ref.at[me], local_sems.at[1])]
        for cp in own:
            cp.start()
        copies = []
        for k in range(1, N_DEV):
            px = 1 - x if k & 4 else x
            py = 1 - y if k & 2 else y
            pc = 1 - c if k & 1 else c
            peer = 4 * px + 2 * py + pc
            copies.append(pltpu.make_async_remote_copy(
                src_ref=big_ref.at[peer], dst_ref=bout_ref.at[me], send_sem=send_sems.at[k - 1],
                recv_sem=recv_sems.at[k - 1], device_id=(px, py, pc), device_id_type=MESH))
            copies.append(pltpu.make_async_remote_copy(
                src_ref=small_ref, dst_ref=sout_ref.at[me], send_sem=send_sems.at[7 + k - 1],
                recv_sem=recv_sems.at[7 + k - 1], device_id=(px, py, pc), device_id_type=MESH))
        for cp in copies:
            cp.start()
        for cp in copies:
            cp.wait()
        for cp in own:
            cp.wait()

    return pl.pallas_call(
        body, name="exchange_grads",
        out_shape=[jax.ShapeDtypeStruct(big.shape, big.dtype),
                   jax.ShapeDtypeStruct((N_DEV,) + small.shape, small.dtype)],
        in_specs=[HBM, HBM], out_specs=[HBM, HBM],
        scratch_shapes=[pltpu.SemaphoreType.DMA((14,)), pltpu.SemaphoreType.DMA((14,)),
                        pltpu.SemaphoreType.DMA((2,))],
    )(big, small)


def _sum_slots(recv, name, tr):
    n, rows, lanes = recv.shape
    tr = _tile(rows, tr)

    def body(r_ref, o_ref):
        acc = r_ref[0].astype(F32)
        for i in range(1, n):
            acc = acc + r_ref[i].astype(F32)
        o_ref[...] = acc

    return pl.pallas_call(
        body, name=name, grid=(rows // tr,),
        in_specs=[pl.BlockSpec((n, tr, lanes), lambda i: (0, i, 0))],
        out_specs=pl.BlockSpec((tr, lanes), lambda i: (i, 0)),
        out_shape=jax.ShapeDtypeStruct((rows, lanes), F32),
        compiler_params=_cparams(("parallel",)),
    )(recv)


def _adamw(w, g, m, v, name, tr=256):
    c1 = 1.0 - ADAM_B1 ** ADAM_STEP
    c2 = 1.0 - ADAM_B2 ** ADAM_STEP

    def fn(w, g, m, v):
        m = ADAM_B1 * m + (1.0 - ADAM_B1) * g
        v = ADAM_B2 * v + (1.0 - ADAM_B2) * (g * g)
        delta = -ADAM_LR * ((m / c1) / (jnp.sqrt(v / c2) + ADAM_EPS) + ADAM_WD * w)
        return delta, m, v

    return _rowwise(fn, name, tr, [w, g, m, v], [], [(w.shape[1], F32)] * 3)


def kernel(x, mem, positions, ffn1_norm, ffn1_w_gu, ffn1_w_down, mix_norm, w_in, b_gate, sg_ln_g, sg_ln_b, sg_w, sg_b, mla_cq_norm, mla_w_uq, mla_ckv_norm, mla_w_ukv, mla_q_norm, mla_k_norm, mem_norm, mem_w_kv, mem_q_norm, mem_k_norm, w_branch_a, w_branch_b, w_branch_c, w_out, ffn2_norm, ffn2_w_gu, ffn2_w_down, loss_target, m_ffn1_norm, m_ffn1_w_gu, m_ffn1_w_down, m_mix_norm, m_w_in, m_b_gate, m_sg_ln_g, m_sg_ln_b, m_sg_w, m_sg_b, m_mla_cq_norm, m_mla_w_uq, m_mla_ckv_norm, m_mla_w_ukv, m_mla_q_norm, m_mla_k_norm, m_mem_norm, m_mem_w_kv, m_mem_q_norm, m_mem_k_norm, m_w_branch_a, m_w_branch_b, m_w_branch_c, m_w_out, m_ffn2_norm, m_ffn2_w_gu, m_ffn2_w_down, v_ffn1_norm, v_ffn1_w_gu, v_ffn1_w_down, v_mix_norm, v_w_in, v_b_gate, v_sg_ln_g, v_sg_ln_b, v_sg_w, v_sg_b, v_mla_cq_norm, v_mla_w_uq, v_mla_ckv_norm, v_mla_w_ukv, v_mla_q_norm, v_mla_k_norm, v_mem_norm, v_mem_w_kv, v_mem_q_norm, v_mem_k_norm, v_w_branch_a, v_w_branch_b, v_w_branch_c, v_w_out, v_ffn2_norm, v_ffn2_w_gu, v_ffn2_w_down):
    given = dict(locals())
    wts = {n: given[n] for n in ORDER}
    mom = {n: given["m_" + n] for n in ORDER}
    var = {n: given["v_" + n] for n in ORDER}
    shard_shapes = {n: wts[n].shape[1:] for n in SHARDED}
    shard_rows = {n: int(np.prod(shard_shapes[n])) // LANES for n in SHARDED}

    def pack_shards(group):
        return jnp.concatenate([wts[n][0].astype(BF16).reshape(shard_rows[n], LANES) for n in GROUPS[group]], axis=0)

    def split_rows(group, packed, lead):
        out, off = {}, 0
        for n in GROUPS[group]:
            out[n] = packed[..., off:off + shard_rows[n], :].reshape(*lead, *shard_shapes[n])
            off += shard_rows[n]
        return out

    def full_weights(group, gathered):
        return _compute_layout({n: _full_from_slabs(n, s) for n, s in split_rows(group, gathered, (N_DEV,)).items()})

    def zero_of(a):
        return jnp.minimum(jnp.abs(a.reshape(-1)[0]), 0).astype(BF16)

    gathered_ffn1, token = _all_gather(pack_shards("ffn1"))
    flight = {"mix": _send_start(pack_shards("mix") + token[0, 0].astype(BF16), False, "gather_mix_start")[0]}
    recv = {}

    def weights(group, after):
        if group == "ffn1":
            return full_weights(group, gathered_ffn1)
        landed = _send_wait(flight.pop(group), after, False, f"gather_{group}_wait")
        if group == "mix":
            flight["ffn2"] = _send_start(pack_shards("ffn2") + zero_of(landed), False, "gather_ffn2_start")[0]
        return full_weights(group, landed)

    def grads_out(group, G, after):
        Gr = _reference_layout({n: G[n] for n in GROUPS[group]})
        big = jnp.concatenate([_slabs_from_full(n, Gr[n]).astype(BF16).reshape(N_DEV, shard_rows[n], LANES)
                               for n in GROUPS[group]], axis=1)
        if group == "ffn2":
            flight["g_ffn2"], tie = _send_start(big, True, "grads_ffn2_start")
        elif group == "mix":
            recv["ffn2"] = _send_wait(flight.pop("g_ffn2"), after, True, "grads_ffn2_wait")
            flight["g_mix"], tie = _send_start(big, True, "grads_mix_start")
        else:
            recv["mix"] = _send_wait(flight.pop("g_mix"), after, True, "grads_mix_wait")
            flight["g_ffn1"], tie = big, None
        return tie

    P = {n: wts[n] if wts[n].ndim == 2 else wts[n][0] for n in SMALL}
    loss_part, grad_x, G = _local_step(x[0], mem[0], positions[0], loss_target[0], P, weights, grads_out)

    small_shapes = [wts[n].shape[1:] for n in SMALL]
    small = _pack([G[n].reshape(s) for n, s in zip(SMALL, small_shapes)])
    small = jnp.pad(small, ((0, (-small.shape[0]) % 8), (0, 0)))
    recv["ffn1"], small_recv = _exchange(flight.pop("g_ffn1"), small)
    g_small_packed = _sum_slots(small_recv, "sum_small", 2048)
    grads = {}
    for group in GROUPS:
        grads.update(split_rows(group, _sum_slots(recv[group], "sum_" + group, 1024), ()))
    grads.update(zip(SMALL, _unpack(g_small_packed, small_shapes)))

    delta, new_m, new_v = {}, {}, {}
    for n in SHARDED:
        delta[n], new_m[n], new_v[n] = _adamw(wts[n][0], grads[n], mom[n][0], var[n][0], "adamw_" + n)

    def pack_small(d):
        p = _pack([d[n].reshape(s) for n, s in zip(SMALL, small_shapes)])
        return jnp.pad(p, ((0, (-p.shape[0]) % 8), (0, 0)))

    ds, ms, vs = _adamw(pack_small(wts), g_small_packed, pack_small(mom), pack_small(var), "adamw_small", tr=2048)
    for dst, packed in ((delta, ds), (new_m, ms), (new_v, vs)):
        dst.update(zip(SMALL, _unpack(packed, small_shapes)))

    loss = lax.psum(jnp.sum(loss_part), ("x", "y", "c"))
    lead = lambda d: [d[n].reshape(wts[n].shape) for n in ORDER]
    return (loss, grad_x[None], *lead(grads), *lead(delta), *lead(new_m), *lead(new_v))
```

```python
import functools

import numpy as np
import jax
import jax.numpy as jnp
from jax import lax
from jax.experimental import pallas as pl
from jax.experimental.pallas import tpu as pltpu

F32, BF16 = jnp.float32, jnp.bfloat16

D_MODEL = 1024
SG_GROUPS, SG_GROUP_DIM, SG_WIDTH, CHUNK = 8, 64, 512, 128
MLA_HEADS, MLA_NOPE, MLA_ROPE, MLA_V, MLA_QK = 8, 64, 32, 64, 96
MLA_Q_RANK, MLA_KV_RANK = 384, 256
MEM_HEADS, MEM_HEAD_DIM, MEM_WIDTH = 4, 128, 512
D_FF = 2816
ROPE_BASE = 10000.0
EPS = 1e-6
NEG = -1e30
ADAM_LR, ADAM_B1, ADAM_B2, ADAM_EPS, ADAM_WD, ADAM_STEP = 0.001, 0.9, 0.999, 1e-08, 0.01, 10

N_DEV = 8
LANES = 128
V7X_VMEM_LIMIT = 56 * 1024 * 1024
HP = MLA_HEADS * LANES

Z_G, Z_U, Z_V, Z_QM, Z_CKV, Z_KR, Z_CQ = 0, 3072, 3584, 4096, 4608, 4864, 4992
Z_COLS = 5376
KR_LANE = 64


def _tile(dim, pref):
    if dim <= pref:
        return dim
    for t in range(pref - pref % LANES, LANES - 1, -LANES):
        if dim % t == 0:
            return t
    for t in range(pref - pref % 8, 7, -8):
        if dim % t == 0:
            return t
    return dim


def _cparams(sem):
    return pltpu.CompilerParams(dimension_semantics=sem, vmem_limit_bytes=V7X_VMEM_LIMIT)


_DN = {"nn": ((1,), (0,)), "nt": ((1,), (1,)), "tn": ((0,), (0,))}


def _dot(a, b, mode="nn"):
    return lax.dot_general(a.astype(BF16), b.astype(BF16), (_DN[mode], ((), ())),
                           preferred_element_type=F32)


def _mm(a, b, mode, out_dtype, name, tm=512, tn=512, tk=2048, tie=None):
    if mode == "tn":
        K, M = a.shape
    else:
        M, K = a.shape
    N = b.shape[0] if mode == "nt" else b.shape[1]
    tm, tn, tk = _tile(M, tm), _tile(N, tn), _tile(K, tk)
    nk = K // tk
    if mode == "tn":
        a_spec = pl.BlockSpec((tk, tm), lambda i, j, k: (k, i))
    else:
        a_spec = pl.BlockSpec((tm, tk), lambda i, j, k: (i, k))
    if mode == "nt":
        b_spec = pl.BlockSpec((tn, tk), lambda i, j, k: (j, k))
    else:
        b_spec = pl.BlockSpec((tk, tn), lambda i, j, k: (k, j))

    ties = [] if tie is None else [tie]

    def body(a_ref, b_ref, *rest):
        o_ref, *scratch = rest[len(ties):]
        p = _dot(a_ref[...], b_ref[...], mode)
        if nk == 1:
            o_ref[...] = p.astype(o_ref.dtype)
        else:
            acc_ref, = scratch
            k = pl.program_id(2)

            @pl.when(k == 0)
            def _():
                acc_ref[...] = p

            @pl.when(k > 0)
            def _():
                acc_ref[...] += p

            @pl.when(k == nk - 1)
            def _():
                o_ref[...] = acc_ref[...].astype(o_ref.dtype)

    return pl.pallas_call(
        body, name=name, grid=(M // tm, N // tn, nk),
        in_specs=[a_spec, b_spec] + [pl.BlockSpec(t.shape, lambda i, j, k: (0, 0)) for t in ties],
        out_specs=pl.BlockSpec((tm, tn), lambda i, j, k: (i, j)),
        out_shape=jax.ShapeDtypeStruct((M, N), out_dtype),
        scratch_shapes=[] if nk == 1 else [pltpu.VMEM((tm, tn), F32)],
        compiler_params=_cparams(("parallel", "parallel", "arbitrary")),
    )(a, b, *ties)


def _mm_t(at, b, name, tm, tn, tk=1024, tie=None):
    return _mm(at, b, "nn", BF16, name, tm=tm, tn=tn, tk=tk, tie=tie)


def _rowwise(fn, name, tr, row_ins, bc_ins, row_outs, acc_outs=()):
    norm = [it if isinstance(it, tuple) else (it, it.shape[1], 0) for it in row_ins]
    rows = norm[0][0].shape[0]
    tr = _tile(rows, tr)
    arrays, in_specs = [], []
    for arr, w, cb in norm:
        arrays.append(arr)
        in_specs.append(pl.BlockSpec((tr, w), lambda i, cb=cb: (i, cb)))
    for arr in bc_ins:
        arrays.append(arr)
        in_specs.append(pl.BlockSpec(arr.shape, lambda i, nd=arr.ndim: (0,) * nd))
    out_shape, out_specs = [], []
    transposed = [len(o) == 3 for o in row_outs]
    for (w, dt, *_), t in zip(row_outs, transposed):
        out_shape.append(jax.ShapeDtypeStruct((w, rows) if t else (rows, w), dt))
        out_specs.append(pl.BlockSpec((w, tr), lambda i: (0, i)) if t else pl.BlockSpec((tr, w), lambda i: (i, 0)))
    for shp, dt in acc_outs:
        out_shape.append(jax.ShapeDtypeStruct(shp, dt))
        out_specs.append(pl.BlockSpec(shp, lambda i, nd=len(shp): (0,) * nd))
    n_in, n_row = len(arrays), len(row_outs)

    def body(*refs):
        vals = fn(*[r[...].astype(F32) for r in refs[:n_in]])
        if not isinstance(vals, (tuple, list)):
            vals = (vals,)
        outs = refs[n_in:]
        for r, v, t in zip(outs[:n_row], vals[:n_row], transposed):
            r[...] = v.astype(F32).T.astype(r.dtype) if t else v.astype(r.dtype)
        if acc_outs:
            accs = list(zip(outs[n_row:], vals[n_row:]))
            i = pl.program_id(0)

            @pl.when(i == 0)
            def _():
                for r, v in accs:
                    r[...] = v.astype(r.dtype)

            @pl.when(i > 0)
            def _():
                for r, v in accs:
                    r[...] += v.astype(r.dtype)

    res = pl.pallas_call(
        body, name=name, grid=(rows // tr,), in_specs=in_specs, out_specs=out_specs,
        out_shape=out_shape, compiler_params=_cparams(("arbitrary",)),
    )(*arrays)
    return res


def _rsum(x):
    return jnp.sum(x, axis=0, keepdims=True)


def _rms(x, g, n=None):
    n = x.shape[-1] if n is None else n
    r = lax.rsqrt(jnp.sum(x * x, axis=-1, keepdims=True) * (1.0 / n) + EPS)
    return x * r * g


def _rms_bwd(x, g, dy, n=None):
    n = x.shape[-1] if n is None else n
    r = lax.rsqrt(jnp.sum(x * x, axis=-1, keepdims=True) * (1.0 / n) + EPS)
    xh = x * r
    dxh = dy * g
    dx = r * (dxh - xh * (jnp.sum(dxh * xh, axis=-1, keepdims=True) * (1.0 / n)))
    return dx, _rsum(dy * xh)


def _gelu(x):
    return 0.5 * x * (1.0 + lax.erf(x * 0.7071067811865476))


def _gelu_grad(x):
    return 0.5 * (1.0 + lax.erf(x * 0.7071067811865476)) + x * jnp.exp(-0.5 * x * x) * 0.3989422804014327


def _sigmoid(x):
    return 1.0 / (1.0 + jnp.exp(-x))


def _ffn_fwd(h, w_gu, w_down, tag):
    gu = _mm(h, w_gu, "nn", BF16, f"{tag}_gu", tm=1024, tn=512)

    def act(gu):
        g = gu[:, :D_FF].astype(F32)
        u = gu[:, D_FF:].astype(F32)
        a = g * _sigmoid(g) * u
        return a, a

    a, at = _rowwise(act, f"{tag}_act", 256, [gu], [], [(D_FF, BF16), (D_FF, BF16, "T")])
    o = _mm(a, w_down, "nn", F32, f"{tag}_down", tm=1024, tn=512, tk=2816)
    return gu, at, o


def _ffn_bwd(do, ht, gu, at, w_gu, w_down, tag, tie=None, on_dw=None):
    on_dw = on_dw or (lambda which, dw: None)
    dw_down = _mm_t(at, do, f"{tag}_dwdown", tm=1408, tn=1024, tie=tie)
    da = _mm(do, w_down, "nt", BF16, f"{tag}_da", tm=1024, tn=512, tie=on_dw("down", dw_down))

    def act_bwd(gu, da):
        g = gu[:, :D_FF].astype(F32)
        u = gu[:, D_FF:].astype(F32)
        da = da.astype(F32)
        s = _sigmoid(g)
        dg = da * u * s * (1.0 + g * (1.0 - s))
        du = da * g * s
        return jnp.concatenate([dg, du], axis=1)

    dgu, = _rowwise(act_bwd, f"{tag}_actbwd", 256, [gu, da], [], [(2 * D_FF, BF16)])
    dw_gu = _mm_t(ht, dgu, f"{tag}_dwgu", tm=1024, tn=1408)
    dh = _mm(dgu, w_gu, "nt", F32, f"{tag}_dh", tm=1024, tn=512, tk=2816, tie=on_dw("gu", dw_gu))
    return dh, dw_gu, dw_down


def _sg_common(u_pre, v_pre, ln_g, ln_b):
    u = _gelu(u_pre)
    v = _gelu(v_pre)
    mu = jnp.mean(v, axis=-1, keepdims=True)
    vc = v - mu
    rstd = lax.rsqrt(jnp.mean(vc * vc, axis=-1, keepdims=True) + EPS)
    vhat = vc * rstd
    vl = vhat * ln_g + ln_b
    return u, vhat, rstd, vl


def _sg_masked_pairs(w):
    t = lax.broadcasted_iota(jnp.int32, (CHUNK, CHUNK), 0)
    s = lax.broadcasted_iota(jnp.int32, (CHUNK, CHUNK), 1)
    causal = s <= t
    wm = [jnp.where(causal, w[g], 0.0).astype(BF16) for g in range(SG_GROUPS)]
    return [jnp.concatenate([wm[2 * j], wm[2 * j + 1]], axis=0) for j in range(SG_GROUPS // 2)], causal


def _sg_mix(vl, pairs, bias):
    tr = vl.shape[0]
    low = lax.broadcasted_iota(jnp.int32, (CHUNK, LANES), 1) < SG_GROUP_DIM
    vb = vl.astype(BF16)
    rows = []
    for c in range(tr // CHUNK):
        slabs = []
        for j in range(SG_GROUPS // 2):
            slab = vb[c * CHUNK:(c + 1) * CHUNK, j * LANES:(j + 1) * LANES]
            m = _dot(pairs[j], slab)
            slabs.append(jnp.where(low, m[:CHUNK], m[CHUNK:]))
        rows.append(jnp.concatenate(slabs, axis=1) + bias)
    return jnp.concatenate(rows, axis=0)


def _sg_fwd(z, ln_g, ln_b, sg_w, bias_full):
    def fn(u_pre, v_pre, ln_g, ln_b, w, bias):
        u, _, _, vl = _sg_common(u_pre, v_pre, ln_g, ln_b)
        pairs, _ = _sg_masked_pairs(w)
        y = u * _sg_mix(vl, pairs, bias)
        return y, y

    return _rowwise(fn, "sg_fwd", 512, [(z, SG_WIDTH, Z_U // SG_WIDTH), (z, SG_WIDTH, Z_V // SG_WIDTH)],
                    [ln_g, ln_b, sg_w, bias_full], [(SG_WIDTH, BF16), (SG_WIDTH, BF16, "T")])


def _sg_bwd(z, dy, ln_g, ln_b, sg_w, bias_full, group_ind):
    def fn(u_pre, v_pre, dy, ln_g, ln_b, w, bias, ind):
        dy = dy.astype(F32)
        u, vhat, rstd, vl = _sg_common(u_pre, v_pre, ln_g, ln_b)
        pairs, causal = _sg_masked_pairs(w)
        mixed = _sg_mix(vl, pairs, bias)
        du_pre = dy * mixed * _gelu_grad(u_pre)
        dmix = dy * u
        tr = dy.shape[0]
        low = lax.broadcasted_iota(jnp.int32, (CHUNK, LANES), 1) < SG_GROUP_DIM
        vb = vl.astype(BF16)
        dw = [jnp.zeros((CHUNK, CHUNK), F32) for _ in range(SG_GROUPS)]
        dbias = jnp.zeros((CHUNK, SG_WIDTH), F32)
        dvl_rows = []
        for c in range(tr // CHUNK):
            dm_c = dmix[c * CHUNK:(c + 1) * CHUNK]
            dbias = dbias + dm_c
            slabs = []
            for j in range(SG_GROUPS // 2):
                slab = vb[c * CHUNK:(c + 1) * CHUNK, j * LANES:(j + 1) * LANES]
                dm = dm_c[:, j * LANES:(j + 1) * LANES]
                d0 = jnp.where(low, dm, 0.0).astype(BF16)
                d1 = jnp.where(low, 0.0, dm).astype(BF16)
                dw[2 * j] = dw[2 * j] + _dot(d0, slab, "nt")
                dw[2 * j + 1] = dw[2 * j + 1] + _dot(d1, slab, "nt")
                slabs.append(_dot(pairs[j], jnp.concatenate([d0, d1], axis=0), "tn"))
            dvl_rows.append(jnp.concatenate(slabs, axis=1))
        dvl = jnp.concatenate(dvl_rows, axis=0)
        dln_g = _rsum(dvl * vhat)
        dln_b = _rsum(dvl)
        dvh = dvl * ln_g
        dv = rstd * (dvh - jnp.mean(dvh, axis=-1, keepdims=True)
                     - vhat * jnp.mean(dvh * vhat, axis=-1, keepdims=True))
        dv_pre = dv * _gelu_grad(v_pre)
        dw = jnp.stack([jnp.where(causal, d, 0.0) for d in dw], axis=0)
        dbias_t = lax.dot_general(dbias, ind, (((1,), (0,)), ((), ())), precision=lax.Precision.HIGHEST,
                                  preferred_element_type=F32)
        return du_pre, dv_pre, dw, dbias_t, dln_g, dln_b

    return _rowwise(fn, "sg_bwd", 512,
                    [(z, SG_WIDTH, Z_U // SG_WIDTH), (z, SG_WIDTH, Z_V // SG_WIDTH), dy],
                    [ln_g, ln_b, sg_w, bias_full, group_ind],
                    [(SG_WIDTH, BF16), (SG_WIDTH, BF16)],
                    [((SG_GROUPS, CHUNK, CHUNK), F32), ((CHUNK, SG_GROUPS), F32), ((1, SG_WIDTH), F32), ((1, SG_WIDTH), F32)])


def _rope(x, c, s1, s2):
    return x * c + pltpu.roll(x, LANES - MLA_ROPE // 2, 1) * s1 + pltpu.roll(x, MLA_ROPE // 2, 1) * s2


def _rope_t(d, c, s1, s2):
    return d * c + pltpu.roll(d * s1, MLA_ROPE // 2, 1) + pltpu.roll(d * s2, LANES - MLA_ROPE // 2, 1)


def _mla_post(q_pre, kv_pre, z, tabs, gq, gk):
    scale = MLA_QK ** -0.5 * LOG2E

    def fn(q_pre, k_pre, v_pre, kr, c, s1, s2, gq, gk):
        qs, ks = [], []
        for h in range(MLA_HEADS):
            sl = slice(h * LANES, (h + 1) * LANES)
            qs.append(_rope(_rms(q_pre[:, sl], gq, MLA_QK), c, s1, s2) * scale)
            ks.append(_rope(_rms(k_pre[:, sl] + kr, gk, MLA_QK), c, s1, s2))
        lane = lax.broadcasted_iota(jnp.int32, v_pre.shape, 1) & (LANES - 1)
        return jnp.concatenate(qs, axis=1), jnp.concatenate(ks, axis=1), jnp.where(lane == ONES_LANE, 1.0, v_pre)

    return _rowwise(fn, "mla_post", 256,
                    [q_pre, (kv_pre, HP, 0), (kv_pre, HP, 1), (z, LANES, Z_KR // LANES), *tabs],
                    [gq, gk], [(HP, BF16)] * 3)


def _mla_post_bwd(q_pre, kv_pre, z, tabs, gq, gk, dq, dk, dv):
    scale = MLA_QK ** -0.5

    def fn(q_pre, k_pre, kr, c, s1, s2, dq, dk, dv, gq, gk):
        lane = lax.broadcasted_iota(jnp.int32, (1, LANES), 1)
        kr_mask = (lane >= KR_LANE) & (lane < KR_LANE + MLA_ROPE)
        dqs, dks = [], []
        dgq = jnp.zeros((1, LANES), F32)
        dgk = jnp.zeros((1, LANES), F32)
        dkr = jnp.zeros(kr.shape, F32)
        for h in range(MLA_HEADS):
            sl = slice(h * LANES, (h + 1) * LANES)
            dqn = _rope_t(dq[:, sl].astype(F32), c, s1, s2) * scale
            dx, dg = _rms_bwd(q_pre[:, sl], gq, dqn, MLA_QK)
            dqs.append(dx)
            dgq = dgq + dg
            dkn = _rope_t(dk[:, sl].astype(F32), c, s1, s2)
            dx, dg = _rms_bwd(k_pre[:, sl] + kr, gk, dkn, MLA_QK)
            dks.append(dx)
            dgk = dgk + dg
            dkr = dkr + dx
        dkr = jnp.where(kr_mask, dkr, 0.0)
        dkv = jnp.concatenate(dks + [dv.astype(F32)], axis=1)
        return jnp.concatenate(dqs, axis=1), dkv, dkr, dgq, dgk

    return _rowwise(fn, "mla_post_bwd", 256,
                    [q_pre, (kv_pre, HP, 0), (z, LANES, Z_KR // LANES), *tabs, dq, dk, dv],
                    [gq, gk], [(HP, BF16), (2 * HP, BF16), (LANES, BF16)],
                    [((1, LANES), F32), ((1, LANES), F32)])


def _pairs(n, lower):
    a, b = [], []
    for o in range(n):
        inner = range(o + 1) if lower else range(o, n)
        for t in inner:
            a.append(o)
            b.append(t)
    return jnp.asarray(np.array(a, np.int32)), jnp.asarray(np.array(b, np.int32))


FLASH_TILE, FLASH_SUB_ROWS = 1024, 512
LOG2E, LN2 = 1.4426950408889634, 0.6931471805599453
ONES_LANE = MLA_V


def _flash_tiles(T):
    tq = _tile(T, FLASH_TILE)
    return tq, _tile(tq, FLASH_SUB_ROWS)


def _col_span(t, sr, rb, diag, key_major):
    if not diag:
        return 0, t
    return (rb * sr, t) if key_major else (0, (rb + 1) * sr)


def _span_iota(sr, rb, c0, c1):
    r = lax.broadcasted_iota(jnp.int32, (sr, c1 - c0), 0) + rb * sr
    c = lax.broadcasted_iota(jnp.int32, (sr, c1 - c0), 1) + c0
    return r, c


def _lanes(x, width):
    return jnp.concatenate([x] * (width // LANES), axis=1)


def _flash_fwd(q, k, v):
    T = q.shape[0]
    tq, sr = _flash_tiles(T)
    n = T // tq
    ii, jj = _pairs(n, True)

    def body(ii_ref, jj_ref, q_ref, k_ref, v_ref, o_ref, ot_ref, lse_ref, m_sc, acc_sc):
        p_ = pl.program_id(1)
        i, j = ii_ref[p_], jj_ref[p_]

        @pl.when(j == 0)
        def _():
            m_sc[...] = jnp.full(m_sc.shape, NEG, F32)
            acc_sc[...] = jnp.zeros(acc_sc.shape, F32)

        def tile(diag):
            for rb in range(tq // sr):
                rows = slice(rb * sr, (rb + 1) * sr)
                c0, c1 = _col_span(tq, sr, rb, diag, False)
                s = _dot(q_ref[rows, :], k_ref[c0:c1, :], "nt")
                if diag:
                    r, c = _span_iota(sr, rb, c0, c1)
                    s = jnp.where(c <= r, s, NEG)
                m = m_sc[rows, :]
                m_new = jnp.maximum(m, jnp.max(s, axis=1, keepdims=True))
                p = jnp.exp2(s - _lanes(m_new, c1 - c0))
                acc_sc[rows, :] = jnp.exp2(m - m_new) * acc_sc[rows, :] + _dot(p, v_ref[c0:c1, :])
                m_sc[rows, :] = m_new

        @pl.when(j < i)
        def _():
            tile(False)

        @pl.when(j == i)
        def _():
            tile(True)
            acc = acc_sc[...]
            lane = lax.broadcasted_iota(jnp.int32, acc.shape, 1)
            l = jnp.sum(jnp.where(lane == ONES_LANE, acc, 0.0), axis=1, keepdims=True)
            o = jnp.where(lane < MLA_V, acc / l, 0.0)
            o_ref[...] = o.astype(o_ref.dtype)
            ot_ref[...] = o.T.astype(ot_ref.dtype)
            lse_ref[...] = m_sc[...] + jnp.log2(l)

    blk = lambda which: pl.BlockSpec((tq, LANES), which)
    qmap = lambda h, p, ii, jj: (ii[p], h)
    kmap = lambda h, p, ii, jj: (jj[p], h)
    return pl.pallas_call(
        body, name="mla_flash_fwd",
        grid_spec=pltpu.PrefetchScalarGridSpec(
            num_scalar_prefetch=2, grid=(MLA_HEADS, int(ii.shape[0])),
            in_specs=[blk(qmap), blk(kmap), blk(kmap)],
            out_specs=[blk(qmap), pl.BlockSpec((LANES, tq), lambda h, p, ii, jj: (h, ii[p])), blk(qmap)],
            scratch_shapes=[pltpu.VMEM((tq, LANES), F32)] * 2),
        out_shape=[jax.ShapeDtypeStruct((T, HP), BF16), jax.ShapeDtypeStruct((HP, T), BF16),
                   jax.ShapeDtypeStruct((T, HP), F32)],
        compiler_params=_cparams(("parallel", "arbitrary")),
    )(ii, jj, q, k, v)


def _flash_dq(q, k, v, do, lse, delta):
    T = q.shape[0]
    tq, sr = _flash_tiles(T)
    n = T // tq
    ii, jj = _pairs(n, True)

    def body(ii_ref, jj_ref, q_ref, k_ref, v_ref, do_ref, lse_ref, dl_ref, dq_ref, acc_sc):
        p_ = pl.program_id(1)
        i, j = ii_ref[p_], jj_ref[p_]

        @pl.when(j == 0)
        def _():
            acc_sc[...] = jnp.zeros(acc_sc.shape, F32)

        def tile(diag):
            for rb in range(tq // sr):
                rows = slice(rb * sr, (rb + 1) * sr)
                c0, c1 = _col_span(tq, sr, rb, diag, False)
                ks = k_ref[c0:c1, :]
                p = jnp.exp2(_dot(q_ref[rows, :], ks, "nt") - _lanes(lse_ref[rows, :], c1 - c0))
                if diag:
                    r, c = _span_iota(sr, rb, c0, c1)
                    p = jnp.where(c <= r, p, 0.0)
                dp = _dot(do_ref[rows, :], v_ref[c0:c1, :], "nt")
                acc_sc[rows, :] += _dot(p * (dp - _lanes(dl_ref[rows, :], c1 - c0)), ks)

        @pl.when(j < i)
        def _():
            tile(False)

        @pl.when(j == i)
        def _():
            tile(True)
            dq_ref[...] = acc_sc[...]

    blk = lambda which: pl.BlockSpec((tq, LANES), which)
    qmap = lambda h, p, ii, jj: (ii[p], h)
    kmap = lambda h, p, ii, jj: (jj[p], h)
    return pl.pallas_call(
        body, name="mla_flash_dq",
        grid_spec=pltpu.PrefetchScalarGridSpec(
            num_scalar_prefetch=2, grid=(MLA_HEADS, int(ii.shape[0])),
            in_specs=[blk(qmap), blk(kmap), blk(kmap), blk(qmap), blk(qmap), blk(qmap)],
            out_specs=blk(qmap),
            scratch_shapes=[pltpu.VMEM((tq, LANES), F32)]),
        out_shape=jax.ShapeDtypeStruct((T, HP), F32),
        compiler_params=_cparams(("parallel", "arbitrary")),
    )(ii, jj, q, k, v, do, lse, delta)


def _flash_dkv(q, k, v, do, lse_row, delta_row):
    T = q.shape[0]
    tq, sr = _flash_tiles(T)
    n = T // tq
    jj, ii = _pairs(n, False)

    def body(jj_ref, ii_ref, q_ref, k_ref, v_ref, do_ref, lse_ref, dl_ref, dk_ref, dv_ref, dk_sc, dv_sc):
        p_ = pl.program_id(1)
        j, i = jj_ref[p_], ii_ref[p_]

        @pl.when(i == j)
        def _():
            dk_sc[...] = jnp.zeros(dk_sc.shape, F32)
            dv_sc[...] = jnp.zeros(dv_sc.shape, F32)

        def tile(diag):
            for rb in range(tq // sr):
                rows = slice(rb * sr, (rb + 1) * sr)
                c0, c1 = _col_span(tq, sr, rb, diag, True)
                qs, dos = q_ref[c0:c1, :], do_ref[c0:c1, :]
                pt = jnp.exp2(_dot(k_ref[rows, :], qs, "nt") - lse_ref[:, c0:c1])
                if diag:
                    r, c = _span_iota(sr, rb, c0, c1)
                    pt = jnp.where(r <= c, pt, 0.0)
                dpt = _dot(v_ref[rows, :], dos, "nt")
                dv_sc[rows, :] += _dot(pt, dos)
                dk_sc[rows, :] += _dot(pt * (dpt - dl_ref[:, c0:c1]), qs)

        @pl.when(i == j)
        def _():
            tile(True)

        @pl.when(i > j)
        def _():
            tile(False)

        @pl.when(i == n - 1)
        def _():
            dk_ref[...] = dk_sc[...] * LN2
            dv_ref[...] = dv_sc[...]

    blk = lambda which: pl.BlockSpec((tq, LANES), which)
    qmap = lambda h, p, jj, ii: (ii[p], h)
    kmap = lambda h, p, jj, ii: (jj[p], h)
    row = pl.BlockSpec((None, 1, tq), lambda h, p, jj, ii: (h, 0, ii[p]))
    return pl.pallas_call(
        body, name="mla_flash_dkv",
        grid_spec=pltpu.PrefetchScalarGridSpec(
            num_scalar_prefetch=2, grid=(MLA_HEADS, int(ii.shape[0])),
            in_specs=[blk(qmap), blk(kmap), blk(kmap), blk(qmap), row, row],
            out_specs=[blk(kmap), blk(kmap)],
            scratch_shapes=[pltpu.VMEM((tq, LANES), F32)] * 2),
        out_shape=[jax.ShapeDtypeStruct((T, HP), F32)] * 2,
        compiler_params=_cparams(("parallel", "arbitrary")),
    )(jj, ii, q, k, v, do, lse_row, delta_row)


def _mem_fwd(z, km, vm, gq):
    scale = MEM_HEAD_DIM ** -0.5

    def fn(qm, km, vm, gq):
        ys = []
        for h in range(MEM_HEADS):
            sl = slice(h * LANES, (h + 1) * LANES)
            q = _rms(qm[:, sl], gq) * scale
            s = _dot(q, km[:, sl], "nt")
            p = jnp.exp(s - jnp.max(s, axis=1, keepdims=True))
            p = p / jnp.sum(p, axis=1, keepdims=True)
            ys.append(_dot(p, vm[:, sl]))
        y = jnp.concatenate(ys, axis=1)
        return y, y

    return _rowwise(fn, "mem_fwd", 512, [(z, MEM_WIDTH, Z_QM // MEM_WIDTH)], [km, vm, gq],
                    [(MEM_WIDTH, BF16), (MEM_WIDTH, BF16, "T")])


def _mem_bwd(z, dy, km, vm, gq):
    scale = MEM_HEAD_DIM ** -0.5

    def fn(qm, dy, km, vm, gq):
        dqs, dks, dvs = [], [], []
        dgq = jnp.zeros((1, LANES), F32)
        for h in range(MEM_HEADS):
            sl = slice(h * LANES, (h + 1) * LANES)
            q = (_rms(qm[:, sl], gq) * scale).astype(BF16)
            dyh = dy[:, sl]
            kh, vh = km[:, sl], vm[:, sl]
            s = _dot(q, kh, "nt")
            p = jnp.exp(s - jnp.max(s, axis=1, keepdims=True))
            p = p / jnp.sum(p, axis=1, keepdims=True)
            dp = _dot(dyh, vh, "nt")
            ds = p * (dp - jnp.sum(p * dp, axis=1, keepdims=True))
            dq = _dot(ds, kh) * scale
            dx, dg = _rms_bwd(qm[:, sl], gq, dq)
            dqs.append(dx)
            dgq = dgq + dg
            st = _dot(kh, q, "nt")
            pt = jnp.exp(st - jnp.max(st, axis=0, keepdims=True))
            pt = pt / jnp.sum(pt, axis=0, keepdims=True)
            dpt = _dot(vh, dyh, "nt")
            dst = pt * (dpt - jnp.sum(pt * dpt, axis=0, keepdims=True))
            dvs.append(_dot(pt, dyh))
            dks.append(_dot(dst, q))
        return jnp.concatenate(dqs, axis=1), jnp.concatenate(dks, axis=1), jnp.concatenate(dvs, axis=1), dgq

    m = km.shape[0]
    return _rowwise(fn, "mem_bwd", 512, [(z, MEM_WIDTH, Z_QM // MEM_WIDTH), dy], [km, vm, gq],
                    [(MEM_WIDTH, BF16)], [((m, MEM_WIDTH), F32), ((m, MEM_WIDTH), F32), ((1, LANES), F32)])


GROUPS = {"ffn1": ["ffn1_w_gu", "ffn1_w_down"],
          "mix": ["w_in", "mla_w_uq", "mla_w_ukv", "mem_w_kv", "w_branch_a", "w_branch_b", "w_branch_c", "w_out"],
          "ffn2": ["ffn2_w_gu", "ffn2_w_down"]}
GRAD_GROUPS = {"ffn2": GROUPS["ffn2"], "mix": GROUPS["mix"], "ffn1_down": ["ffn1_w_down"], "ffn1_gu": ["ffn1_w_gu"]}


def _local_step(x, mem, positions, loss_target, P, weights, grads_out):
    T = x.shape[0]
    G = {}
    W = dict(weights("ffn1", None))

    half = MLA_ROPE // 2
    inv = ROPE_BASE ** (-jnp.arange(half, dtype=F32) / half)
    ang = positions.astype(F32)[:, None] * inv
    cos, sin = jnp.cos(ang), jnp.sin(ang)
    one, zero = jnp.ones((T, MLA_NOPE), F32), jnp.zeros((T, half), F32)
    pad = LANES - MLA_QK
    tabs = (jnp.concatenate([one, cos, cos, jnp.ones((T, pad), F32)], axis=1),
            jnp.concatenate([jnp.zeros((T, MLA_NOPE), F32), -sin, zero, jnp.zeros((T, pad), F32)], axis=1),
            jnp.concatenate([jnp.zeros((T, MLA_NOPE), F32), zero, sin, jnp.zeros((T, pad), F32)], axis=1))
    gq_p = jnp.pad(P["mla_q_norm"], ((0, 0), (0, pad)))
    gk_p = jnp.pad(P["mla_k_norm"], ((0, 0), (0, pad)))
    bias_full = jnp.repeat(P["sg_b"].T, SG_GROUP_DIM, axis=1)
    group_ind = jnp.repeat(jnp.eye(SG_GROUPS, dtype=F32), SG_GROUP_DIM, axis=0)

    HT = (D_MODEL, BF16, "T")

    def norm2(x, g):
        h = _rms(x, g)
        return h, h

    h1, h1t = _rowwise(norm2, "ffn1_norm", 512, [x], [P["ffn1_norm"]], [(D_MODEL, BF16), HT])
    gu1, a1t, o1 = _ffn_fwd(h1, W["ffn1_w_gu"], W["ffn1_w_down"], "ffn1")

    def resid_norm(x, o, g):
        xn = x + 0.5 * o
        h = _rms(xn, g)
        return xn, h, h

    x1, hm, hmt = _rowwise(resid_norm, "mix_norm", 512, [x, o1], [P["mix_norm"]],
                           [(D_MODEL, F32), (D_MODEL, BF16), HT])
    W.update(weights("mix", hm))
    z = _mm(hm, W["w_in"], "nn", BF16, "w_in", tm=1024, tn=768)

    y_a, y_at = _sg_fwd(z, P["sg_ln_g"], P["sg_ln_b"], P["sg_w"], bias_full)

    def c_norm(cq, ckv, gq, gkv):
        a, b = _rms(cq, gq), _rms(ckv, gkv)
        return a, b, a, b

    cqn, ckvn, cqnt, ckvnt = _rowwise(
        c_norm, "mla_cnorm", 512, [(z, MLA_Q_RANK, Z_CQ // MLA_Q_RANK), (z, MLA_KV_RANK, Z_CKV // MLA_KV_RANK)],
        [P["mla_cq_norm"], P["mla_ckv_norm"]],
        [(MLA_Q_RANK, BF16), (MLA_KV_RANK, BF16), (MLA_Q_RANK, BF16, "T"), (MLA_KV_RANK, BF16, "T")])
    q_pre = _mm(cqn, W["mla_w_uq"], "nn", F32, "mla_uq", tm=1024, tn=1024)
    kv_pre = _mm(ckvn, W["mla_w_ukv"], "nn", F32, "mla_ukv", tm=1024, tn=1024)
    q, k, v = _mla_post(q_pre, kv_pre, z, tabs, gq_p, gk_p)
    y_b, y_bt, lse = _flash_fwd(q, k, v)

    memn, = _rowwise(lambda m, g: _rms(m, g), "mem_norm", 256, [mem], [P["mem_norm"]], [(D_MODEL, BF16)])
    kvm = _mm(memn, W["mem_w_kv"], "nn", F32, "mem_kv")

    def mem_k(kvm, gk):
        ks = [_rms(kvm[:, h * LANES:(h + 1) * LANES], gk) for h in range(MEM_HEADS)]
        return jnp.concatenate(ks, axis=1), kvm[:, MEM_WIDTH:]

    km, vm = _rowwise(mem_k, "mem_knorm", 256, [kvm], [P["mem_k_norm"]], [(MEM_WIDTH, BF16), (MEM_WIDTH, BF16)])
    y_c, y_ct = _mem_fwd(z, km, vm, P["mem_q_norm"])

    pa = _mm(y_a, W["w_branch_a"], "nn", BF16, "branch_a", tm=1024, tn=1024)
    pb = _mm(y_b, W["w_branch_b"], "nn", BF16, "branch_b", tm=1024, tn=1024)
    pc = _mm(y_c, W["w_branch_c"], "nn", BF16, "branch_c", tm=1024, tn=1024)

    def merge(zg, pa, pb, pc, b):
        g = _sigmoid(zg + b)
        m = g[:, :D_MODEL] * pa + g[:, D_MODEL:2 * D_MODEL] * pb + g[:, 2 * D_MODEL:] * pc
        return m, m

    merged, mergedt = _rowwise(merge, "merge", 256, [(z, 3 * D_MODEL, 0), pa, pb, pc], [P["b_gate"]],
                               [(D_MODEL, BF16), HT])
    om = _mm(merged, W["w_out"], "nn", F32, "w_out", tm=1024, tn=1024)

    def resid_norm1(x, o, g):
        xn = x + o
        h = _rms(xn, g)
        return xn, h, h

    x2, h2, h2t = _rowwise(resid_norm1, "ffn2_norm", 512, [x1, om], [P["ffn2_norm"]],
                           [(D_MODEL, F32), (D_MODEL, BF16), HT])
    W.update(weights("ffn2", h2))
    gu2, a2t, o2 = _ffn_fwd(h2, W["ffn2_w_gu"], W["ffn2_w_down"], "ffn2")

    def loss_fn(x2, o2, t):
        e = x2 + 0.5 * o2 - t
        return e * (1.0 / D_MODEL), (e * (0.5 / D_MODEL)).astype(BF16), _rsum(e * e) * (0.5 / D_MODEL)

    dx3, do2, loss_part = _rowwise(loss_fn, "loss", 512, [x2, o2, loss_target], [],
                                   [(D_MODEL, F32), (D_MODEL, BF16)], [((1, D_MODEL), F32)])

    dh2, G["ffn2_w_gu"], G["ffn2_w_down"] = _ffn_bwd(do2, h2t, gu2, a2t, W["ffn2_w_gu"], W["ffn2_w_down"], "ffn2")
    tie = grads_out("ffn2", G)

    def norm_bwd(x, dh, dxo, g, *_):
        dx, dg = _rms_bwd(x, g, dh)
        dx = dx + dxo
        return dx, dx, dg

    dx2, dx2b, G["ffn2_norm"] = _rowwise(norm_bwd, "ffn2_norm_bwd", 512, [x2, dh2, dx3],
                                         [P["ffn2_norm"]] + ([] if tie is None else [tie]),
                                         [(D_MODEL, F32), (D_MODEL, BF16)], [((1, D_MODEL), F32)])

    G["w_out"] = _mm_t(mergedt, dx2b, "w_out_dw", tm=1024, tn=1024)
    dmerged = _mm(dx2b, W["w_out"], "nt", F32, "w_out_dx", tm=1024, tn=1024)

    def merge_bwd(zg, pa, pb, pc, dm, b):
        g = _sigmoid(zg + b)
        ps = jnp.concatenate([pa, pb, pc], axis=1)
        dm3 = jnp.concatenate([dm, dm, dm], axis=1)
        dzg = dm3 * ps * g * (1.0 - g)
        dp = dm3 * g
        return dzg, dp[:, :D_MODEL], dp[:, D_MODEL:2 * D_MODEL], dp[:, 2 * D_MODEL:], _rsum(dzg)

    dzg, dpa, dpb, dpc, G["b_gate"] = _rowwise(
        merge_bwd, "merge_bwd", 256, [(z, 3 * D_MODEL, 0), pa, pb, pc, dmerged], [P["b_gate"]],
        [(3 * D_MODEL, BF16), (D_MODEL, BF16), (D_MODEL, BF16), (D_MODEL, BF16)], [((1, 3 * D_MODEL), F32)])

    G["w_branch_a"] = _mm_t(y_at, dpa, "branch_a_dw", tm=512, tn=1024)
    G["w_branch_b"] = _mm_t(y_bt, dpb, "branch_b_dw", tm=1024, tn=1024)
    G["w_branch_c"] = _mm_t(y_ct, dpc, "branch_c_dw", tm=512, tn=1024)
    dy_a = _mm(dpa, W["w_branch_a"], "nt", BF16, "branch_a_dx", tm=1024, tn=512)
    dy_b = _mm(dpb, W["w_branch_b"], "nt", BF16, "branch_b_dx", tm=1024, tn=1024)
    dy_c = _mm(dpc, W["w_branch_c"], "nt", BF16, "branch_c_dx", tm=1024, tn=512)

    du_pre, dv_pre, G["sg_w"], dbias_t, G["sg_ln_g"], G["sg_ln_b"] = _sg_bwd(
        z, dy_a, P["sg_ln_g"], P["sg_ln_b"], P["sg_w"], bias_full, group_ind)
    G["sg_b"] = dbias_t.T

    dqm, dkm, dvm, G["mem_q_norm"] = _mem_bwd(z, dy_c, km, vm, P["mem_q_norm"])

    def mem_k_bwd(kvm, dkm, dvm, gk):
        dks = []
        dg = jnp.zeros((1, LANES), F32)
        for h in range(MEM_HEADS):
            sl = slice(h * LANES, (h + 1) * LANES)
            dx, d = _rms_bwd(kvm[:, sl], gk, dkm[:, sl])
            dks.append(dx)
            dg = dg + d
        return jnp.concatenate(dks + [dvm], axis=1), dg

    dkvm, G["mem_k_norm"] = _rowwise(mem_k_bwd, "mem_knorm_bwd", 256, [kvm, dkm, dvm], [P["mem_k_norm"]],
                                     [(2 * MEM_WIDTH, BF16)], [((1, LANES), F32)])
    G["mem_w_kv"] = _mm(memn, dkvm, "tn", BF16, "mem_kv_dw")
    dmemn = _mm(dkvm, W["mem_w_kv"], "nt", F32, "mem_kv_dx")
    _, G["mem_norm"] = _rowwise(lambda m, d, g: _rms_bwd(m, g, d), "mem_norm_bwd", 256, [mem, dmemn],
                                [P["mem_norm"]], [(D_MODEL, BF16)], [((1, D_MODEL), F32)])

    def delta_fn(o, do):
        od = o.astype(F32) * do.astype(F32)
        ds = [jnp.broadcast_to(jnp.sum(od[:, h * LANES:(h + 1) * LANES], axis=1, keepdims=True), (od.shape[0], LANES))
              for h in range(MLA_HEADS)]
        return jnp.concatenate(ds, axis=1)

    delta, = _rowwise(delta_fn, "mla_delta", 512, [y_b, dy_b], [], [(HP, F32)])
    rowform = lambda a: a.reshape(T, MLA_HEADS, LANES)[:, :, 0].T.reshape(MLA_HEADS, 1, T)
    dq = _flash_dq(q, k, v, dy_b, lse, delta)
    dk, dv = _flash_dkv(q, k, v, dy_b, rowform(lse), rowform(delta))
    dq_pre, dkv_pre, dkr, dgq, dgk = _mla_post_bwd(q_pre, kv_pre, z, tabs, gq_p, gk_p, dq, dk, dv)
    G["mla_q_norm"], G["mla_k_norm"] = dgq[:, :MLA_QK], dgk[:, :MLA_QK]
    G["mla_w_uq"] = _mm_t(cqnt, dq_pre, "mla_uq_dw", tm=384, tn=1024)
    G["mla_w_ukv"] = _mm_t(ckvnt, dkv_pre, "mla_ukv_dw", tm=256, tn=2048)
    dcqn = _mm(dq_pre, W["mla_w_uq"], "nt", F32, "mla_uq_dx", tm=1024)
    dckvn = _mm(dkv_pre, W["mla_w_ukv"], "nt", F32, "mla_ukv_dx", tm=1024)

    def c_norm_bwd(cq, ckv, dcqn, dckvn, gq, gkv):
        dcq, dgq = _rms_bwd(cq, gq, dcqn)
        dckv, dgkv = _rms_bwd(ckv, gkv, dckvn)
        return dcq, dckv, dgq, dgkv

    dcq, dckv, G["mla_cq_norm"], G["mla_ckv_norm"] = _rowwise(
        c_norm_bwd, "mla_cnorm_bwd", 512,
        [(z, MLA_Q_RANK, Z_CQ // MLA_Q_RANK), (z, MLA_KV_RANK, Z_CKV // MLA_KV_RANK), dcqn, dckvn],
        [P["mla_cq_norm"], P["mla_ckv_norm"]], [(MLA_Q_RANK, BF16), (MLA_KV_RANK, BF16)],
        [((1, MLA_Q_RANK), F32), ((1, MLA_KV_RANK), F32)])

    dz = jnp.concatenate([dzg, du_pre, dv_pre, dqm, dckv, dkr, dcq], axis=1)
    G["w_in"] = _mm_t(hmt, dz, "w_in_dw", tm=1024, tn=768)
    dhm = _mm(dz, W["w_in"], "nt", F32, "w_in_dx", tm=1024, tn=1024, tk=2688)

    def norm_bwd_half(x, dh, dxo, g):
        dx, dg = _rms_bwd(x, g, dh)
        dx = dx + dxo
        return dx, (0.5 * dx), dg

    dx1, do1, G["mix_norm"] = _rowwise(norm_bwd_half, "mix_norm_bwd", 512, [x1, dhm, dx2], [P["mix_norm"]],
                                       [(D_MODEL, F32), (D_MODEL, BF16)], [((1, D_MODEL), F32)])
    tie = grads_out("mix", G)

    def ffn1_dw(which, dw):
        G["ffn1_w_" + which] = dw
        return grads_out("ffn1_" + which, G)

    dh1, _, _ = _ffn_bwd(do1, h1t, gu1, a1t, W["ffn1_w_gu"], W["ffn1_w_down"], "ffn1", tie, ffn1_dw)

    def norm_bwd_last(x, dh, dxo, g):
        dx, dg = _rms_bwd(x, g, dh)
        return dx + dxo, dg

    grad_x, G["ffn1_norm"] = _rowwise(norm_bwd_last, "ffn1_norm_bwd", 512, [x, dh1, dx1], [P["ffn1_norm"]],
                                      [(D_MODEL, F32)], [((1, D_MODEL), F32)])
    return loss_part, grad_x, G


SHARDED = ["ffn1_w_gu", "ffn1_w_down", "w_in", "mla_w_uq", "mla_w_ukv", "mem_w_kv",
           "w_branch_a", "w_branch_b", "w_branch_c", "w_out", "ffn2_w_gu", "ffn2_w_down"]
ROW_SHARDED = {"ffn1_w_down", "mem_w_kv", "w_out", "ffn2_w_down"}
SMALL = ["ffn1_norm", "mix_norm", "b_gate", "sg_ln_g", "sg_ln_b", "sg_w", "sg_b", "mla_cq_norm",
         "mla_ckv_norm", "mla_q_norm", "mla_k_norm", "mem_norm", "mem_q_norm", "mem_k_norm", "ffn2_norm"]
ORDER = ["ffn1_norm", "ffn1_w_gu", "ffn1_w_down", "mix_norm", "w_in", "b_gate", "sg_ln_g", "sg_ln_b", "sg_w",
         "sg_b", "mla_cq_norm", "mla_w_uq", "mla_ckv_norm", "mla_w_ukv", "mla_q_norm", "mla_k_norm", "mem_norm",
         "mem_w_kv", "mem_q_norm", "mem_k_norm", "w_branch_a", "w_branch_b", "w_branch_c", "w_out", "ffn2_norm",
         "ffn2_w_gu", "ffn2_w_down"]

_IN_U, _IN_V, _IN_CQ, _IN_CKV, _IN_KR, _IN_QM, _IN_G = 0, 512, 1024, 1408, 1664, 1696, 2208
IN_COLS = 5280


def _full_from_slabs(name, slabs):
    n, r, c = slabs.shape
    if name in ROW_SHARDED:
        return slabs.reshape(n * r, c)
    return slabs.transpose(1, 0, 2).reshape(r, n * c)


def _slabs_from_full(name, full):
    if name in ROW_SHARDED:
        return full.reshape(N_DEV, full.shape[0] // N_DEV, full.shape[1])
    r, c = full.shape
    return full.reshape(r, N_DEV, c // N_DEV).transpose(1, 0, 2)


def _compute_layout(full):
    W = dict(full)
    if "w_in" not in full:
        return W
    w = full["w_in"]
    kr = jnp.pad(w[:, _IN_KR:_IN_QM], ((0, 0), (KR_LANE, LANES - KR_LANE - MLA_ROPE)))
    W["w_in"] = jnp.concatenate([w[:, _IN_G:], w[:, _IN_U:_IN_CQ], w[:, _IN_QM:_IN_G], w[:, _IN_CKV:_IN_KR], kr,
                                 w[:, _IN_CQ:_IN_CKV]], axis=1)
    uq = full["mla_w_uq"].reshape(MLA_Q_RANK, MLA_HEADS, MLA_QK)
    W["mla_w_uq"] = jnp.pad(uq, ((0, 0), (0, 0), (0, LANES - MLA_QK))).reshape(MLA_Q_RANK, HP)
    ukv = full["mla_w_ukv"].reshape(MLA_KV_RANK, MLA_HEADS, MLA_NOPE + MLA_V)
    padh = lambda a: jnp.pad(a, ((0, 0), (0, 0), (0, LANES - a.shape[2]))).reshape(MLA_KV_RANK, HP)
    W["mla_w_ukv"] = jnp.concatenate([padh(ukv[:, :, :MLA_NOPE]), padh(ukv[:, :, MLA_NOPE:])], axis=1)
    wb = full["w_branch_b"].reshape(MLA_HEADS, MLA_V, D_MODEL)
    W["w_branch_b"] = jnp.pad(wb, ((0, 0), (0, LANES - MLA_V), (0, 0))).reshape(HP, D_MODEL)
    return W


def _reference_layout(G):
    out = dict(G)
    if "w_in" not in G:
        return out
    g = G["w_in"]
    out["w_in"] = jnp.concatenate([
        g[:, Z_U:Z_QM], g[:, Z_CQ:Z_COLS], g[:, Z_CKV:Z_KR], g[:, Z_KR + KR_LANE:Z_KR + KR_LANE + MLA_ROPE],
        g[:, Z_QM:Z_CKV], g[:, Z_G:Z_U]], axis=1)
    out["mla_w_uq"] = G["mla_w_uq"].reshape(MLA_Q_RANK, MLA_HEADS, LANES)[:, :, :MLA_QK].reshape(MLA_Q_RANK, -1)
    gk = G["mla_w_ukv"][:, :HP].reshape(MLA_KV_RANK, MLA_HEADS, LANES)[:, :, :MLA_NOPE]
    gv = G["mla_w_ukv"][:, HP:].reshape(MLA_KV_RANK, MLA_HEADS, LANES)[:, :, :MLA_V]
    out["mla_w_ukv"] = jnp.concatenate([gk, gv], axis=2).reshape(MLA_KV_RANK, -1)
    out["w_branch_b"] = G["w_branch_b"].reshape(MLA_HEADS, LANES, D_MODEL)[:, :MLA_V].reshape(-1, D_MODEL)
    return out


def _pack(parts):
    flat = []
    for a in parts:
        a = a.reshape(-1)
        flat.append(jnp.pad(a, (0, (-a.shape[0]) % LANES)))
    return jnp.concatenate(flat).reshape(-1, LANES)


def _unpack(packed, shapes):
    flat = packed.reshape(-1)
    out, off = [], 0
    for shp in shapes:
        n = int(np.prod(shp))
        out.append(flat[off:off + n].reshape(shp))
        off += n + (-n) % LANES
    return out


MESH = pl.DeviceIdType.MESH
HBM = pl.BlockSpec(memory_space=pltpu.HBM)


def _all_gather(shard):
    rows, lanes = shard.shape

    def body(x_ref, out_ref, token_ref, send_sems, recv_sems, local_sem):
        x, y, c = lax.axis_index("x"), lax.axis_index("y"), lax.axis_index("c")
        me, sibling = (x, y, c), (x, y, 1 - c)
        chips = [(1 - x, y), (x, 1 - y), (1 - x, 1 - y)]
        token_ref[...] = jnp.zeros_like(token_ref)

        def slot(px, py, pc):
            return out_ref.at[4 * px + 2 * py + pc]

        def copy(k, block, to, src=None):
            return pltpu.make_async_remote_copy(
                src_ref=slot(*block) if src is None else src, dst_ref=slot(*block),
                send_sem=send_sems.at[k], recv_sem=recv_sems.at[k], device_id=to, device_id_type=MESH)

        mine = pltpu.make_async_copy(x_ref, slot(*me), local_sem)
        mine.start()
        first = [copy(0, me, sibling, src=x_ref)]
        first += [copy(1 + j, me, (*chip, c), src=x_ref) for j, chip in enumerate(chips)]
        for cp in first:
            cp.start()
        passed = [copy(4 + j, (*chip, c), sibling) for j, chip in enumerate(chips)]
        for j, chip in enumerate(chips):
            copy(1 + j, (*chip, c), me).wait_recv()
            passed[j].start()
        copy(0, sibling, me).wait_recv()
        for j, chip in enumerate(chips):
            copy(4 + j, (*chip, 1 - c), me).wait_recv()
        for cp in first + passed:
            cp.wait_send()
        mine.wait()

    return pl.pallas_call(
        body, name="all_gather_weights",
        out_shape=[jax.ShapeDtypeStruct((N_DEV, rows, lanes), shard.dtype), jax.ShapeDtypeStruct((8, LANES), F32)],
        in_specs=[HBM], out_specs=[HBM, pl.BlockSpec(memory_space=pltpu.VMEM)],
        scratch_shapes=[pltpu.SemaphoreType.DMA((7,)), pltpu.SemaphoreType.DMA((7,)), pltpu.SemaphoreType.DMA],
    )(shard)


SEM = pl.BlockSpec(memory_space=pltpu.SEMAPHORE)
DATAFLOW = pltpu.SideEffectType.DATAFLOW_SIDE_EFFECTING


def _peers():
    x, y, c = lax.axis_index("x"), lax.axis_index("y"), lax.axis_index("c")
    out = []
    for k in range(1, N_DEV):
        px = 1 - x if k & 4 else x
        py = 1 - y if k & 2 else y
        pc = 1 - c if k & 1 else c
        out.append((k, (px, py, pc), 4 * px + 2 * py + pc))
    return 4 * x + 2 * y + c, out


def _send_start(src, per_peer, name):
    rows = src.shape[-2]
    land = lax.empty((N_DEV, rows, LANES), src.dtype)

    def body(src_ref, land_ref, send_sems, recv_sems, src_thru, land_thru, token):
        me, peers = _peers()
        for k, pid, pflat in peers:
            pltpu.make_async_remote_copy(
                src_ref=src_ref.at[pflat] if per_peer else src_ref, dst_ref=land_ref.at[me],
                send_sem=send_sems.at[k - 1], recv_sem=recv_sems.at[k - 1],
                device_id=pid, device_id_type=MESH).start()
        token[...] = jnp.zeros_like(token)

    res = pl.pallas_call(
        body, name=name,
        out_shape=(pltpu.SemaphoreType.DMA((N_DEV - 1,)), pltpu.SemaphoreType.DMA((N_DEV - 1,)),
                   pltpu.HBM(src.shape, src.dtype), pltpu.HBM(land.shape, land.dtype),
                   jax.ShapeDtypeStruct((8, LANES), F32)),
        in_specs=(HBM, HBM), out_specs=(SEM, SEM, HBM, HBM, pl.BlockSpec(memory_space=pltpu.VMEM)),
        input_output_aliases={0: 2, 1: 3},
        compiler_params=pltpu.CompilerParams(has_side_effects=DATAFLOW),
    )(pltpu.with_memory_space_constraint(src, pltpu.HBM), pltpu.with_memory_space_constraint(land, pltpu.HBM))
    return res[:4], res[4]


def _send_wait(started, after, per_peer, name):
    send_sems, recv_sems, src_thru, land_thru = started

    def body(src_ref, land_ref, send_sems, recv_sems, after_ref, src_out, land_out):
        me, peers = _peers()
        for k, pid, pflat in peers:
            copy = pltpu.make_async_remote_copy(
                src_ref=src_ref.at[pflat] if per_peer else src_ref, dst_ref=land_ref.at[pflat],
                send_sem=send_sems.at[k - 1], recv_sem=recv_sems.at[k - 1],
                device_id=pid, device_id_type=MESH)
            copy.wait_send()
            copy.wait_recv()

    src_out, land = pl.pallas_call(
        body, name=name,
        out_shape=(pltpu.HBM(src_thru.shape, src_thru.dtype), pltpu.HBM(land_thru.shape, land_thru.dtype)),
        in_specs=(HBM, HBM, SEM, SEM, pl.BlockSpec(memory_space=pl.ANY)), out_specs=(HBM, HBM),
        input_output_aliases={0: 0, 1: 1},
        compiler_params=pltpu.CompilerParams(has_side_effects=DATAFLOW),
    )(src_thru, land_thru, send_sems, recv_sems, after)
    me = 4 * lax.axis_index("x") + 2 * lax.axis_index("y") + lax.axis_index("c")
    own = lax.dynamic_index_in_dim(src_out, me, 0, keepdims=True) if per_peer else src_out[None]
    return lax.dynamic_update_slice(land, own, (me, 0, 0))


def _share_rows(block, name):
    def body(src_ref, out_ref, send_sems, recv_sems, local_sem):
        me, peers = _peers()
        own = pltpu.make_async_copy(src_ref, out_ref.at[me], local_sem)
        own.start()
        copies = [pltpu.make_async_remote_copy(
            src_ref=src_ref, dst_ref=out_ref.at[me], send_sem=send_sems.at[k - 1], recv_sem=recv_sems.at[k - 1],
            device_id=pid, device_id_type=MESH) for k, pid, _ in peers]
        for cp in copies:
            cp.start()
        for cp in copies:
            cp.wait()
        own.wait()

    return pl.pallas_call(
        body, name=name, out_shape=jax.ShapeDtypeStruct((N_DEV,) + block.shape, block.dtype),
        in_specs=[HBM], out_specs=HBM,
        scratch_shapes=[pltpu.SemaphoreType.DMA((N_DEV - 1,)), pltpu.SemaphoreType.DMA((N_DEV - 1,)),
                        pltpu.SemaphoreType.DMA],
    )(block)


def _sum_slots(recv, name, tr):
    n, rows, lanes = recv.shape
    tr = _tile(rows, tr)

    def body(r_ref, o_ref):
        acc = r_ref[0].astype(F32)
        for i in range(1, n):
            acc = acc + r_ref[i].astype(F32)
        o_ref[...] = acc

    return pl.pallas_call(
        body, name=name, grid=(rows // tr,),
        in_specs=[pl.BlockSpec((n, tr, lanes), lambda i: (0, i, 0))],
        out_specs=pl.BlockSpec((tr, lanes), lambda i: (i, 0)),
        out_shape=jax.ShapeDtypeStruct((rows, lanes), F32),
        compiler_params=_cparams(("parallel",)),
    )(recv)


def _adamw(w, g, m, v, name, tr=256):
    c1 = 1.0 - ADAM_B1 ** ADAM_STEP
    c2 = 1.0 - ADAM_B2 ** ADAM_STEP

    def fn(w, g, m, v):
        m = ADAM_B1 * m + (1.0 - ADAM_B1) * g
        v = ADAM_B2 * v + (1.0 - ADAM_B2) * (g * g)
        delta = -ADAM_LR * ((m / c1) / (jnp.sqrt(v / c2) + ADAM_EPS) + ADAM_WD * w)
        return delta, m, v

    return _rowwise(fn, name, tr, [w, g, m, v], [], [(w.shape[1], F32)] * 3)


def kernel(x, mem, positions, ffn1_norm, ffn1_w_gu, ffn1_w_down, mix_norm, w_in, b_gate, sg_ln_g, sg_ln_b, sg_w, sg_b, mla_cq_norm, mla_w_uq, mla_ckv_norm, mla_w_ukv, mla_q_norm, mla_k_norm, mem_norm, mem_w_kv, mem_q_norm, mem_k_norm, w_branch_a, w_branch_b, w_branch_c, w_out, ffn2_norm, ffn2_w_gu, ffn2_w_down, loss_target, m_ffn1_norm, m_ffn1_w_gu, m_ffn1_w_down, m_mix_norm, m_w_in, m_b_gate, m_sg_ln_g, m_sg_ln_b, m_sg_w, m_sg_b, m_mla_cq_norm, m_mla_w_uq, m_mla_ckv_norm, m_mla_w_ukv, m_mla_q_norm, m_mla_k_norm, m_mem_norm, m_mem_w_kv, m_mem_q_norm, m_mem_k_norm, m_w_branch_a, m_w_branch_b, m_w_branch_c, m_w_out, m_ffn2_norm, m_ffn2_w_gu, m_ffn2_w_down, v_ffn1_norm, v_ffn1_w_gu, v_ffn1_w_down, v_mix_norm, v_w_in, v_b_gate, v_sg_ln_g, v_sg_ln_b, v_sg_w, v_sg_b, v_mla_cq_norm, v_mla_w_uq, v_mla_ckv_norm, v_mla_w_ukv, v_mla_q_norm, v_mla_k_norm, v_mem_norm, v_mem_w_kv, v_mem_q_norm, v_mem_k_norm, v_w_branch_a, v_w_branch_b, v_w_branch_c, v_w_out, v_ffn2_norm, v_ffn2_w_gu, v_ffn2_w_down):
    given = dict(locals())
    wts = {n: given[n] for n in ORDER}
    mom = {n: given["m_" + n] for n in ORDER}
    var = {n: given["v_" + n] for n in ORDER}
    shard_shapes = {n: wts[n].shape[1:] for n in SHARDED}
    shard_rows = {n: int(np.prod(shard_shapes[n])) // LANES for n in SHARDED}

    def pack_shards(group):
        return jnp.concatenate([wts[n][0].astype(BF16).reshape(shard_rows[n], LANES) for n in GROUPS[group]], axis=0)

    def split_rows(names, packed, lead):
        out, off = {}, 0
        for n in names:
            out[n] = packed[..., off:off + shard_rows[n], :].reshape(*lead, *shard_shapes[n])
            off += shard_rows[n]
        return out

    def full_weights(group, gathered):
        slabs = split_rows(GROUPS[group], gathered, (N_DEV,))
        return _compute_layout({n: _full_from_slabs(n, s) for n, s in slabs.items()})

    def zero_of(a):
        return jnp.minimum(jnp.abs(a.reshape(-1)[0]), 0).astype(BF16)

    gathered_ffn1, token = _all_gather(pack_shards("ffn1"))
    flight = {"mix": _send_start(pack_shards("mix") + token[0, 0].astype(BF16), False, "gather_mix_start")[0]}
    recv = {}

    def weights(group, after):
        if group == "ffn1":
            return full_weights(group, gathered_ffn1)
        landed = _send_wait(flight.pop(group), after, False, f"gather_{group}_wait")
        if group == "mix":
            flight["ffn2"] = _send_start(pack_shards("ffn2") + zero_of(landed), False, "gather_ffn2_start")[0]
        return full_weights(group, landed)

    small_shapes = [wts[n].shape[1:] for n in SMALL]
    early = SMALL[1:]
    assert SMALL[0] == "ffn1_norm"
    early_rows = []

    def grads_out(group, G):
        Gr = _reference_layout({n: G[n] for n in GRAD_GROUPS[group]})
        parts = [_slabs_from_full(n, Gr[n]).astype(BF16).reshape(N_DEV, shard_rows[n], LANES)
                 for n in GRAD_GROUPS[group]]
        if group == "mix":
            small = _pack([G[n].reshape(s) for n, s in zip(early, small_shapes[1:])])
            small = jnp.pad(small, ((0, (-small.shape[0]) % 8), (0, 0)))
            early_rows.append(small.shape[0])
            bits = lax.bitcast_convert_type(small, BF16).reshape(2 * small.shape[0], LANES)
            parts.append(jnp.broadcast_to(bits[None], (N_DEV,) + bits.shape))
        flight["g_" + group], tie = _send_start(jnp.concatenate(parts, axis=1), True, f"grads_{group}_start")
        return tie

    P = {n: wts[n] if wts[n].ndim == 2 else wts[n][0] for n in SMALL}
    loss_part, grad_x, G = _local_step(x[0], mem[0], positions[0], loss_target[0], P, weights, grads_out)

    for group in GRAD_GROUPS:
        recv[group] = _send_wait(flight.pop("g_" + group), grad_x, True, f"grads_{group}_wait")
    last = _share_rows(G["ffn1_norm"].reshape(-1, LANES), "share_ffn1_norm")
    mix_rows = sum(shard_rows[n] for n in GRAD_GROUPS["mix"])
    early_recv = lax.bitcast_convert_type(
        recv["mix"][:, mix_rows:].reshape(N_DEV, early_rows[0], LANES, 2), F32)
    g_small_packed = _sum_slots(jnp.concatenate([last, early_recv], axis=1), "sum_small", 2048)
    grads = {}
    for group, names in GRAD_GROUPS.items():
        rows = sum(shard_rows[n] for n in names)
        grads.update(split_rows(names, _sum_slots(recv[group][:, :rows], "sum_" + group, 1024), ()))
    grads.update(zip(SMALL, _unpack(g_small_packed, small_shapes)))

    delta, new_m, new_v = {}, {}, {}
    for n in SHARDED:
        delta[n], new_m[n], new_v[n] = _adamw(wts[n][0], grads[n], mom[n][0], var[n][0], "adamw_" + n)

    def pack_small(d):
        p = _pack([d[n].reshape(s) for n, s in zip(SMALL, small_shapes)])
        return jnp.pad(p, ((0, (-p.shape[0]) % 8), (0, 0)))

    ds, ms, vs = _adamw(pack_small(wts), g_small_packed, pack_small(mom), pack_small(var), "adamw_small", tr=2048)
    for dst, packed in ((delta, ds), (new_m, ms), (new_v, vs)):
        dst.update(zip(SMALL, _unpack(packed, small_shapes)))

    loss = lax.psum(jnp.sum(loss_part), ("x", "y", "c"))
    lead = lambda d: [d[n].reshape(wts[n].shape) for n in ORDER]
    return (loss, grad_x[None], *lead(grads), *lead(delta), *lead(new_m), *lead(new_v))
```

```python
import functools

import numpy as np
import jax
import jax.numpy as jnp
from jax import lax
from jax.experimental import pallas as pl
from jax.experimental.pallas import tpu as pltpu

F32, BF16 = jnp.float32, jnp.bfloat16

D_MODEL = 1024
SG_GROUPS, SG_GROUP_DIM, SG_WIDTH, CHUNK = 8, 64, 512, 128
MLA_HEADS, MLA_NOPE, MLA_ROPE, MLA_V, MLA_QK = 8, 64, 32, 64, 96
MLA_Q_RANK, MLA_KV_RANK = 384, 256
MEM_HEADS, MEM_HEAD_DIM, MEM_WIDTH = 4, 128, 512
D_FF = 2816
ROPE_BASE = 10000.0
EPS = 1e-6
NEG = -1e30
ADAM_LR, ADAM_B1, ADAM_B2, ADAM_EPS, ADAM_WD, ADAM_STEP = 0.001, 0.9, 0.999, 1e-08, 0.01, 10

N_DEV = 8
LANES = 128
V7X_VMEM_LIMIT = 56 * 1024 * 1024
HP = MLA_HEADS * LANES

Z_G, Z_U, Z_V, Z_QM, Z_CKV, Z_KR, Z_CQ = 0, 3072, 3584, 4096, 4608, 4864, 4992
Z_COLS = 5376
KR_LANE = 64


def _tile(dim, pref):
    if dim <= pref:
        return dim
    for t in range(pref - pref % LANES, LANES - 1, -LANES):
        if dim % t == 0:
            return t
    for t in range(pref - pref % 8, 7, -8):
        if dim % t == 0:
            return t
    return dim


def _cparams(sem):
    return pltpu.CompilerParams(dimension_semantics=sem, vmem_limit_bytes=V7X_VMEM_LIMIT)


_DN = {"nn": ((1,), (0,)), "nt": ((1,), (1,)), "tn": ((0,), (0,))}


def _dot(a, b, mode="nn"):
    return lax.dot_general(a.astype(BF16), b.astype(BF16), (_DN[mode], ((), ())),
                           preferred_element_type=F32)


def _mm(a, b, mode, out_dtype, name, tm=512, tn=512, tk=2048, tie=None):
    if mode == "tn":
        K, M = a.shape
    else:
        M, K = a.shape
    N = b.shape[0] if mode == "nt" else b.shape[1]
    tm, tn, tk = _tile(M, tm), _tile(N, tn), _tile(K, tk)
    nk = K // tk
    if mode == "tn":
        a_spec = pl.BlockSpec((tk, tm), lambda i, j, k: (k, i))
    else:
        a_spec = pl.BlockSpec((tm, tk), lambda i, j, k: (i, k))
    if mode == "nt":
        b_spec = pl.BlockSpec((tn, tk), lambda i, j, k: (j, k))
    else:
        b_spec = pl.BlockSpec((tk, tn), lambda i, j, k: (k, j))

    ties = [] if tie is None else [tie]

    def body(a_ref, b_ref, *rest):
        o_ref, *scratch = rest[len(ties):]
        p = _dot(a_ref[...], b_ref[...], mode)
        if nk == 1:
            o_ref[...] = p.astype(o_ref.dtype)
        else:
            acc_ref, = scratch
            k = pl.program_id(2)

            @pl.when(k == 0)
            def _():
                acc_ref[...] = p

            @pl.when(k > 0)
            def _():
                acc_ref[...] += p

            @pl.when(k == nk - 1)
            def _():
                o_ref[...] = acc_ref[...].astype(o_ref.dtype)

    return pl.pallas_call(
        body, name=name, grid=(M // tm, N // tn, nk),
        in_specs=[a_spec, b_spec] + [pl.BlockSpec(t.shape, lambda i, j, k: (0, 0)) for t in ties],
        out_specs=pl.BlockSpec((tm, tn), lambda i, j, k: (i, j)),
        out_shape=jax.ShapeDtypeStruct((M, N), out_dtype),
        scratch_shapes=[] if nk == 1 else [pltpu.VMEM((tm, tn), F32)],
        compiler_params=_cparams(("parallel", "parallel", "arbitrary")),
    )(a, b, *ties)


def _mm_t(at, b, name, tm, tn, tk=1024, tie=None):
    return _mm(at, b, "nn", BF16, name, tm=tm, tn=tn, tk=tk, tie=tie)


def _rowwise(fn, name, tr, row_ins, bc_ins, row_outs, acc_outs=()):
    norm = [it if isinstance(it, tuple) else (it, it.shape[1], 0) for it in row_ins]
    rows = norm[0][0].shape[0]
    tr = _tile(rows, tr)
    arrays, in_specs = [], []
    for arr, w, cb in norm:
        arrays.append(arr)
        in_specs.append(pl.BlockSpec((tr, w), lambda i, cb=cb: (i, cb)))
    for arr in bc_ins:
        arrays.append(arr)
        in_specs.append(pl.BlockSpec(arr.shape, lambda i, nd=arr.ndim: (0,) * nd))
    out_shape, out_specs = [], []
    transposed = [len(o) == 3 for o in row_outs]
    for (w, dt, *_), t in zip(row_outs, transposed):
        out_shape.append(jax.ShapeDtypeStruct((w, rows) if t else (rows, w), dt))
        out_specs.append(pl.BlockSpec((w, tr), lambda i: (0, i)) if t else pl.BlockSpec((tr, w), lambda i: (i, 0)))
    for shp, dt in acc_outs:
        out_shape.append(jax.ShapeDtypeStruct(shp, dt))
        out_specs.append(pl.BlockSpec(shp, lambda i, nd=len(shp): (0,) * nd))
    n_in, n_row = len(arrays), len(row_outs)

    def body(*refs):
        vals = fn(*[r[...].astype(F32) for r in refs[:n_in]])
        if not isinstance(vals, (tuple, list)):
            vals = (vals,)
        outs = refs[n_in:]
        for r, v, t in zip(outs[:n_row], vals[:n_row], transposed):
            r[...] = v.astype(F32).T.astype(r.dtype) if t else v.astype(r.dtype)
        if acc_outs:
            accs = list(zip(outs[n_row:], vals[n_row:]))
            i = pl.program_id(0)

            @pl.when(i == 0)
            def _():
                for r, v in accs:
                    r[...] = v.astype(r.dtype)

            @pl.when(i > 0)
            def _():
                for r, v in accs:
                    r[...] += v.astype(r.dtype)

    res = pl.pallas_call(
        body, name=name, grid=(rows // tr,), in_specs=in_specs, out_specs=out_specs,
        out_shape=out_shape, compiler_params=_cparams(("arbitrary",)),
    )(*arrays)
    return res


def _rsum(x):
    return jnp.sum(x, axis=0, keepdims=True)


def _rms(x, g, n=None):
    n = x.shape[-1] if n is None else n
    r = lax.rsqrt(jnp.sum(x * x, axis=-1, keepdims=True) * (1.0 / n) + EPS)
    return x * r * g


def _rms_bwd(x, g, dy, n=None):
    n = x.shape[-1] if n is None else n
    r = lax.rsqrt(jnp.sum(x * x, axis=-1, keepdims=True) * (1.0 / n) + EPS)
    xh = x * r
    dxh = dy * g
    dx = r * (dxh - xh * (jnp.sum(dxh * xh, axis=-1, keepdims=True) * (1.0 / n)))
    return dx, _rsum(dy * xh)


def _gelu(x):
    return 0.5 * x * (1.0 + lax.erf(x * 0.7071067811865476))


def _gelu_grad(x):
    return 0.5 * (1.0 + lax.erf(x * 0.7071067811865476)) + x * jnp.exp(-0.5 * x * x) * 0.3989422804014327


def _sigmoid(x):
    return 1.0 / (1.0 + jnp.exp(-x))


def _ffn_fwd(h, w_gu, w_down, tag):
    gu = _mm(h, w_gu, "nn", BF16, f"{tag}_gu", tm=1024, tn=512)

    def act(gu):
        g = gu[:, :D_FF].astype(F32)
        u = gu[:, D_FF:].astype(F32)
        a = g * _sigmoid(g) * u
        return a, a

    a, at = _rowwise(act, f"{tag}_act", 256, [gu], [], [(D_FF, BF16), (D_FF, BF16, "T")])
    o = _mm(a, w_down, "nn", F32, f"{tag}_down", tm=1024, tn=512, tk=2816)
    return gu, at, o


def _ffn_bwd(do, ht, gu, at, w_gu, w_down, tag, tie=None, on_dw=None):
    on_dw = on_dw or (lambda which, dw: None)
    dw_down = _mm_t(at, do, f"{tag}_dwdown", tm=1408, tn=1024, tie=tie)
    da = _mm(do, w_down, "nt", BF16, f"{tag}_da", tm=1024, tn=512, tie=on_dw("down", dw_down))

    def act_bwd(gu, da):
        g = gu[:, :D_FF].astype(F32)
        u = gu[:, D_FF:].astype(F32)
        da = da.astype(F32)
        s = _sigmoid(g)
        dg = da * u * s * (1.0 + g * (1.0 - s))
        du = da * g * s
        return jnp.concatenate([dg, du], axis=1)

    dgu, = _rowwise(act_bwd, f"{tag}_actbwd", 256, [gu, da], [], [(2 * D_FF, BF16)])
    dw_gu = _mm_t(ht, dgu, f"{tag}_dwgu", tm=1024, tn=1408)
    dh = _mm(dgu, w_gu, "nt", F32, f"{tag}_dh", tm=1024, tn=512, tk=2816, tie=on_dw("gu", dw_gu))
    return dh, dw_gu, dw_down


def _sg_common(u_pre, v_pre, ln_g, ln_b):
    u = _gelu(u_pre)
    v = _gelu(v_pre)
    mu = jnp.mean(v, axis=-1, keepdims=True)
    vc = v - mu
    rstd = lax.rsqrt(jnp.mean(vc * vc, axis=-1, keepdims=True) + EPS)
    vhat = vc * rstd
    vl = vhat * ln_g + ln_b
    return u, vhat, rstd, vl


def _sg_masked_pairs(w):
    t = lax.broadcasted_iota(jnp.int32, (CHUNK, CHUNK), 0)
    s = lax.broadcasted_iota(jnp.int32, (CHUNK, CHUNK), 1)
    causal = s <= t
    wm = [jnp.where(causal, w[g], 0.0).astype(BF16) for g in range(SG_GROUPS)]
    return [jnp.concatenate([wm[2 * j], wm[2 * j + 1]], axis=0) for j in range(SG_GROUPS // 2)], causal


def _sg_mix(vl, pairs, bias):
    tr = vl.shape[0]
    low = lax.broadcasted_iota(jnp.int32, (CHUNK, LANES), 1) < SG_GROUP_DIM
    vb = vl.astype(BF16)
    rows = []
    for c in range(tr // CHUNK):
        slabs = []
        for j in range(SG_GROUPS // 2):
            slab = vb[c * CHUNK:(c + 1) * CHUNK, j * LANES:(j + 1) * LANES]
            m = _dot(pairs[j], slab)
            slabs.append(jnp.where(low, m[:CHUNK], m[CHUNK:]))
        rows.append(jnp.concatenate(slabs, axis=1) + bias)
    return jnp.concatenate(rows, axis=0)


def _sg_fwd(z, ln_g, ln_b, sg_w, bias_full):
    def fn(u_pre, v_pre, ln_g, ln_b, w, bias):
        u, _, _, vl = _sg_common(u_pre, v_pre, ln_g, ln_b)
        pairs, _ = _sg_masked_pairs(w)
        y = u * _sg_mix(vl, pairs, bias)
        return y, y

    return _rowwise(fn, "sg_fwd", 512, [(z, SG_WIDTH, Z_U // SG_WIDTH), (z, SG_WIDTH, Z_V // SG_WIDTH)],
                    [ln_g, ln_b, sg_w, bias_full], [(SG_WIDTH, BF16), (SG_WIDTH, BF16, "T")])


def _sg_bwd(z, dy, ln_g, ln_b, sg_w, bias_full, group_ind):
    def fn(u_pre, v_pre, dy, ln_g, ln_b, w, bias, ind):
        dy = dy.astype(F32)
        u, vhat, rstd, vl = _sg_common(u_pre, v_pre, ln_g, ln_b)
        pairs, causal = _sg_masked_pairs(w)
        mixed = _sg_mix(vl, pairs, bias)
        du_pre = dy * mixed * _gelu_grad(u_pre)
        dmix = dy * u
        tr = dy.shape[0]
        low = lax.broadcasted_iota(jnp.int32, (CHUNK, LANES), 1) < SG_GROUP_DIM
        vb = vl.astype(BF16)
        dw = [jnp.zeros((CHUNK, CHUNK), F32) for _ in range(SG_GROUPS)]
        dbias = jnp.zeros((CHUNK, SG_WIDTH), F32)
        dvl_rows = []
        for c in range(tr // CHUNK):
            dm_c = dmix[c * CHUNK:(c + 1) * CHUNK]
            dbias = dbias + dm_c
            slabs = []
            for j in range(SG_GROUPS // 2):
                slab = vb[c * CHUNK:(c + 1) * CHUNK, j * LANES:(j + 1) * LANES]
                dm = dm_c[:, j * LANES:(j + 1) * LANES]
                d0 = jnp.where(low, dm, 0.0).astype(BF16)
                d1 = jnp.where(low, 0.0, dm).astype(BF16)
                dw[2 * j] = dw[2 * j] + _dot(d0, slab, "nt")
                dw[2 * j + 1] = dw[2 * j + 1] + _dot(d1, slab, "nt")
                slabs.append(_dot(pairs[j], jnp.concatenate([d0, d1], axis=0), "tn"))
            dvl_rows.append(jnp.concatenate(slabs, axis=1))
        dvl = jnp.concatenate(dvl_rows, axis=0)
        dln_g = _rsum(dvl * vhat)
        dln_b = _rsum(dvl)
        dvh = dvl * ln_g
        dv = rstd * (dvh - jnp.mean(dvh, axis=-1, keepdims=True)
                     - vhat * jnp.mean(dvh * vhat, axis=-1, keepdims=True))
        dv_pre = dv * _gelu_grad(v_pre)
        dw = jnp.stack([jnp.where(causal, d, 0.0) for d in dw], axis=0)
        dbias_t = lax.dot_general(dbias, ind, (((1,), (0,)), ((), ())), precision=lax.Precision.HIGHEST,
                                  preferred_element_type=F32)
        return du_pre, dv_pre, dw, dbias_t, dln_g, dln_b

    return _rowwise(fn, "sg_bwd", 512,
                    [(z, SG_WIDTH, Z_U // SG_WIDTH), (z, SG_WIDTH, Z_V // SG_WIDTH), dy],
                    [ln_g, ln_b, sg_w, bias_full, group_ind],
                    [(SG_WIDTH, BF16), (SG_WIDTH, BF16)],
                    [((SG_GROUPS, CHUNK, CHUNK), F32), ((CHUNK, SG_GROUPS), F32), ((1, SG_WIDTH), F32), ((1, SG_WIDTH), F32)])


def _rope(x, c, s1, s2):
    return x * c + pltpu.roll(x, LANES - MLA_ROPE // 2, 1) * s1 + pltpu.roll(x, MLA_ROPE // 2, 1) * s2


def _rope_t(d, c, s1, s2):
    return d * c + pltpu.roll(d * s1, MLA_ROPE // 2, 1) + pltpu.roll(d * s2, LANES - MLA_ROPE // 2, 1)


def _mla_post(q_pre, kv_pre, z, tabs, gq, gk):
    scale = MLA_QK ** -0.5 * LOG2E

    def fn(q_pre, k_pre, v_pre, kr, c, s1, s2, gq, gk):
        qs, ks = [], []
        for h in range(MLA_HEADS):
            sl = slice(h * LANES, (h + 1) * LANES)
            qs.append(_rope(_rms(q_pre[:, sl], gq, MLA_QK), c, s1, s2) * scale)
            ks.append(_rope(_rms(k_pre[:, sl] + kr, gk, MLA_QK), c, s1, s2))
        lane = lax.broadcasted_iota(jnp.int32, v_pre.shape, 1) & (LANES - 1)
        return jnp.concatenate(qs, axis=1), jnp.concatenate(ks, axis=1), jnp.where(lane == ONES_LANE, 1.0, v_pre)

    return _rowwise(fn, "mla_post", 256,
                    [q_pre, (kv_pre, HP, 0), (kv_pre, HP, 1), (z, LANES, Z_KR // LANES), *tabs],
                    [gq, gk], [(HP, BF16)] * 3)


def _mla_post_bwd(q_pre, kv_pre, z, tabs, gq, gk, dq, dk, dv):
    scale = MLA_QK ** -0.5

    def fn(q_pre, k_pre, kr, c, s1, s2, dq, dk, dv, gq, gk):
        lane = lax.broadcasted_iota(jnp.int32, (1, LANES), 1)
        kr_mask = (lane >= KR_LANE) & (lane < KR_LANE + MLA_ROPE)
        dqs, dks = [], []
        dgq = jnp.zeros((1, LANES), F32)
        dgk = jnp.zeros((1, LANES), F32)
        dkr = jnp.zeros(kr.shape, F32)
        for h in range(MLA_HEADS):
            sl = slice(h * LANES, (h + 1) * LANES)
            dqn = _rope_t(dq[:, sl].astype(F32), c, s1, s2) * scale
            dx, dg = _rms_bwd(q_pre[:, sl], gq, dqn, MLA_QK)
            dqs.append(dx)
            dgq = dgq + dg
            dkn = _rope_t(dk[:, sl].astype(F32), c, s1, s2)
            dx, dg = _rms_bwd(k_pre[:, sl] + kr, gk, dkn, MLA_QK)
            dks.append(dx)
            dgk = dgk + dg
            dkr = dkr + dx
        dkr = jnp.where(kr_mask, dkr, 0.0)
        dkv = jnp.concatenate(dks + [dv.astype(F32)], axis=1)
        return jnp.concatenate(dqs, axis=1), dkv, dkr, dgq, dgk

    return _rowwise(fn, "mla_post_bwd", 256,
                    [q_pre, (kv_pre, HP, 0), (z, LANES, Z_KR // LANES), *tabs, dq, dk, dv],
                    [gq, gk], [(HP, BF16), (2 * HP, BF16), (LANES, BF16)],
                    [((1, LANES), F32), ((1, LANES), F32)])


def _pairs(n, lower):
    a, b = [], []
    for o in range(n):
        inner = range(o + 1) if lower else range(o, n)
        for t in inner:
            a.append(o)
            b.append(t)
    return jnp.asarray(np.array(a, np.int32)), jnp.asarray(np.array(b, np.int32))


FLASH_TILE, FLASH_SUB_ROWS = 1024, 512
LOG2E, LN2 = 1.4426950408889634, 0.6931471805599453
ONES_LANE = MLA_V


def _flash_tiles(T):
    tq = _tile(T, FLASH_TILE)
    return tq, _tile(tq, FLASH_SUB_ROWS)


def _col_span(t, sr, rb, diag, key_major):
    if not diag:
        return 0, t
    return (rb * sr, t) if key_major else (0, (rb + 1) * sr)


def _span_iota(sr, rb, c0, c1):
    r = lax.broadcasted_iota(jnp.int32, (sr, c1 - c0), 0) + rb * sr
    c = lax.broadcasted_iota(jnp.int32, (sr, c1 - c0), 1) + c0
    return r, c


def _lanes(x, width):
    return jnp.concatenate([x] * (width // LANES), axis=1)


def _flash_fwd(q, k, v):
    T = q.shape[0]
    tq, sr = _flash_tiles(T)
    n = T // tq
    ii, jj = _pairs(n, True)

    def body(ii_ref, jj_ref, q_ref, k_ref, v_ref, o_ref, ot_ref, lse_ref, m_sc, acc_sc):
        p_ = pl.program_id(1)
        i, j = ii_ref[p_], jj_ref[p_]

        @pl.when(j == 0)
        def _():
            m_sc[...] = jnp.full(m_sc.shape, NEG, F32)
            acc_sc[...] = jnp.zeros(acc_sc.shape, F32)

        def tile(diag):
            for rb in range(tq // sr):
                rows = slice(rb * sr, (rb + 1) * sr)
                c0, c1 = _col_span(tq, sr, rb, diag, False)
                s = _dot(q_ref[rows, :], k_ref[c0:c1, :], "nt")
                if diag:
                    r, c = _span_iota(sr, rb, c0, c1)
                    s = jnp.where(c <= r, s, NEG)
                m = m_sc[rows, :]
                m_new = jnp.maximum(m, jnp.max(s, axis=1, keepdims=True))
                p = jnp.exp2(s - _lanes(m_new, c1 - c0))
                acc_sc[rows, :] = jnp.exp2(m - m_new) * acc_sc[rows, :] + _dot(p, v_ref[c0:c1, :])
                m_sc[rows, :] = m_new

        @pl.when(j < i)
        def _():
            tile(False)

        @pl.when(j == i)
        def _():
            tile(True)
            acc = acc_sc[...]
            lane = lax.broadcasted_iota(jnp.int32, acc.shape, 1)
            l = jnp.sum(jnp.where(lane == ONES_LANE, acc, 0.0), axis=1, keepdims=True)
            o = jnp.where(lane < MLA_V, acc / l, 0.0)
            o_ref[...] = o.astype(o_ref.dtype)
            ot_ref[...] = o.T.astype(ot_ref.dtype)
            lse_ref[...] = m_sc[...] + jnp.log2(l)

    blk = lambda which: pl.BlockSpec((tq, LANES), which)
    qmap = lambda h, p, ii, jj: (ii[p], h)
    kmap = lambda h, p, ii, jj: (jj[p], h)
    return pl.pallas_call(
        body, name="mla_flash_fwd",
        grid_spec=pltpu.PrefetchScalarGridSpec(
            num_scalar_prefetch=2, grid=(MLA_HEADS, int(ii.shape[0])),
            in_specs=[blk(qmap), blk(kmap), blk(kmap)],
            out_specs=[blk(qmap), pl.BlockSpec((LANES, tq), lambda h, p, ii, jj: (h, ii[p])), blk(qmap)],
            scratch_shapes=[pltpu.VMEM((tq, LANES), F32)] * 2),
        out_shape=[jax.ShapeDtypeStruct((T, HP), BF16), jax.ShapeDtypeStruct((HP, T), BF16),
                   jax.ShapeDtypeStruct((T, HP), F32)],
        compiler_params=_cparams(("parallel", "arbitrary")),
    )(ii, jj, q, k, v)


def _flash_dq(q, k, v, do, lse, delta):
    T = q.shape[0]
    tq, sr = _flash_tiles(T)
    n = T // tq
    ii, jj = _pairs(n, True)

    def body(ii_ref, jj_ref, q_ref, k_ref, v_ref, do_ref, lse_ref, dl_ref, dq_ref, acc_sc):
        p_ = pl.program_id(1)
        i, j = ii_ref[p_], jj_ref[p_]

        @pl.when(j == 0)
        def _():
            acc_sc[...] = jnp.zeros(acc_sc.shape, F32)

        def tile(diag):
            for rb in range(tq // sr):
                rows = slice(rb * sr, (rb + 1) * sr)
                c0, c1 = _col_span(tq, sr, rb, diag, False)
                ks = k_ref[c0:c1, :]
                p = jnp.exp2(_dot(q_ref[rows, :], ks, "nt") - _lanes(lse_ref[rows, :], c1 - c0))
                if diag:
                    r, c = _span_iota(sr, rb, c0, c1)
                    p = jnp.where(c <= r, p, 0.0)
                dp = _dot(do_ref[rows, :], v_ref[c0:c1, :], "nt")
                acc_sc[rows, :] += _dot(p * (dp - _lanes(dl_ref[rows, :], c1 - c0)), ks)

        @pl.when(j < i)
        def _():
            tile(False)

        @pl.when(j == i)
        def _():
            tile(True)
            dq_ref[...] = acc_sc[...]

    blk = lambda which: pl.BlockSpec((tq, LANES), which)
    qmap = lambda h, p, ii, jj: (ii[p], h)
    kmap = lambda h, p, ii, jj: (jj[p], h)
    return pl.pallas_call(
        body, name="mla_flash_dq",
        grid_spec=pltpu.PrefetchScalarGridSpec(
            num_scalar_prefetch=2, grid=(MLA_HEADS, int(ii.shape[0])),
            in_specs=[blk(qmap), blk(kmap), blk(kmap), blk(qmap), blk(qmap), blk(qmap)],
            out_specs=blk(qmap),
            scratch_shapes=[pltpu.VMEM((tq, LANES), F32)]),
        out_shape=jax.ShapeDtypeStruct((T, HP), F32),
        compiler_params=_cparams(("parallel", "arbitrary")),
    )(ii, jj, q, k, v, do, lse, delta)


def _flash_dkv(q, k, v, do, lse_row, delta_row):
    T = q.shape[0]
    tq, sr = _flash_tiles(T)
    n = T // tq
    jj, ii = _pairs(n, False)

    def body(jj_ref, ii_ref, q_ref, k_ref, v_ref, do_ref, lse_ref, dl_ref, dk_ref, dv_ref, dk_sc, dv_sc):
        p_ = pl.program_id(1)
        j, i = jj_ref[p_], ii_ref[p_]

        @pl.when(i == j)
        def _():
            dk_sc[...] = jnp.zeros(dk_sc.shape, F32)
            dv_sc[...] = jnp.zeros(dv_sc.shape, F32)

        def tile(diag):
            for rb in range(tq // sr):
                rows = slice(rb * sr, (rb + 1) * sr)
                c0, c1 = _col_span(tq, sr, rb, diag, True)
                qs, dos = q_ref[c0:c1, :], do_ref[c0:c1, :]
                pt = jnp.exp2(_dot(k_ref[rows, :], qs, "nt") - lse_ref[:, c0:c1])
                if diag:
                    r, c = _span_iota(sr, rb, c0, c1)
                    pt = jnp.where(r <= c, pt, 0.0)
                dpt = _dot(v_ref[rows, :], dos, "nt")
                dv_sc[rows, :] += _dot(pt, dos)
                dk_sc[rows, :] += _dot(pt * (dpt - dl_ref[:, c0:c1]), qs)

        @pl.when(i == j)
        def _():
            tile(True)

        @pl.when(i > j)
        def _():
            tile(False)

        @pl.when(i == n - 1)
        def _():
            dk_ref[...] = dk_sc[...] * LN2
            dv_ref[...] = dv_sc[...]

    blk = lambda which: pl.BlockSpec((tq, LANES), which)
    qmap = lambda h, p, jj, ii: (ii[p], h)
    kmap = lambda h, p, jj, ii: (jj[p], h)
    row = pl.BlockSpec((None, 1, tq), lambda h, p, jj, ii: (h, 0, ii[p]))
    return pl.pallas_call(
        body, name="mla_flash_dkv",
        grid_spec=pltpu.PrefetchScalarGridSpec(
            num_scalar_prefetch=2, grid=(MLA_HEADS, int(ii.shape[0])),
            in_specs=[blk(qmap), blk(kmap), blk(kmap), blk(qmap), row, row],
            out_specs=[blk(kmap), blk(kmap)],
            scratch_shapes=[pltpu.VMEM((tq, LANES), F32)] * 2),
        out_shape=[jax.ShapeDtypeStruct((T, HP), F32)] * 2,
        compiler_params=_cparams(("parallel", "arbitrary")),
    )(jj, ii, q, k, v, do, lse_row, delta_row)


def _mem_fwd(z, km, vm, gq):
    scale = MEM_HEAD_DIM ** -0.5

    def fn(qm, km, vm, gq):
        ys = []
        for h in range(MEM_HEADS):
            sl = slice(h * LANES, (h + 1) * LANES)
            q = _rms(qm[:, sl], gq) * scale
            s = _dot(q, km[:, sl], "nt")
            p = jnp.exp(s - jnp.max(s, axis=1, keepdims=True))
            p = p / jnp.sum(p, axis=1, keepdims=True)
            ys.append(_dot(p, vm[:, sl]))
        y = jnp.concatenate(ys, axis=1)
        return y, y

    return _rowwise(fn, "mem_fwd", 512, [(z, MEM_WIDTH, Z_QM // MEM_WIDTH)], [km, vm, gq],
                    [(MEM_WIDTH, BF16), (MEM_WIDTH, BF16, "T")])


def _mem_bwd(z, dy, km, vm, gq):
    scale = MEM_HEAD_DIM ** -0.5

    def fn(qm, dy, km, vm, gq):
        dqs, dks, dvs = [], [], []
        dgq = jnp.zeros((1, LANES), F32)
        for h in range(MEM_HEADS):
            sl = slice(h * LANES, (h + 1) * LANES)
            q = (_rms(qm[:, sl], gq) * scale).astype(BF16)
            dyh = dy[:, sl]
            kh, vh = km[:, sl], vm[:, sl]
            s = _dot(q, kh, "nt")
            p = jnp.exp(s - jnp.max(s, axis=1, keepdims=True))
            p = p / jnp.sum(p, axis=1, keepdims=True)
            dp = _dot(dyh, vh, "nt")
            ds = p * (dp - jnp.sum(p * dp, axis=1, keepdims=True))
            dq = _dot(ds, kh) * scale
            dx, dg = _rms_bwd(qm[:, sl], gq, dq)
            dqs.append(dx)
            dgq = dgq + dg
            st = _dot(kh, q, "nt")
            pt = jnp.exp(st - jnp.max(st, axis=0, keepdims=True))
            pt = pt / jnp.sum(pt, axis=0, keepdims=True)
            dpt = _dot(vh, dyh, "nt")
            dst = pt * (dpt - jnp.sum(pt * dpt, axis=0, keepdims=True))
            dvs.append(_dot(pt, dyh))
            dks.append(_dot(dst, q))
        return jnp.concatenate(dqs, axis=1), jnp.concatenate(dks, axis=1), jnp.concatenate(dvs, axis=1), dgq

    m = km.shape[0]
    return _rowwise(fn, "mem_bwd", 512, [(z, MEM_WIDTH, Z_QM // MEM_WIDTH), dy], [km, vm, gq],
                    [(MEM_WIDTH, BF16)], [((m, MEM_WIDTH), F32), ((m, MEM_WIDTH), F32), ((1, LANES), F32)])


GROUPS = {"ffn1": ["ffn1_w_gu", "ffn1_w_down"],
          "mix": ["w_in", "mla_w_uq", "mla_w_ukv", "mem_w_kv", "w_branch_a", "w_branch_b", "w_branch_c", "w_out"],
          "ffn2": ["ffn2_w_gu", "ffn2_w_down"]}
GRAD_GROUPS = {"ffn2": GROUPS["ffn2"], "mix": GROUPS["mix"], "ffn1_down": ["ffn1_w_down"], "ffn1_gu": ["ffn1_w_gu"]}


def _local_step(x, mem, positions, loss_target, P, weights, grads_out):
    T = x.shape[0]
    G = {}
    W = dict(weights("ffn1", None))

    half = MLA_ROPE // 2
    inv = ROPE_BASE ** (-jnp.arange(half, dtype=F32) / half)
    ang = positions.astype(F32)[:, None] * inv
    cos, sin = jnp.cos(ang), jnp.sin(ang)
    one, zero = jnp.ones((T, MLA_NOPE), F32), jnp.zeros((T, half), F32)
    pad = LANES - MLA_QK
    tabs = (jnp.concatenate([one, cos, cos, jnp.ones((T, pad), F32)], axis=1),
            jnp.concatenate([jnp.zeros((T, MLA_NOPE), F32), -sin, zero, jnp.zeros((T, pad), F32)], axis=1),
            jnp.concatenate([jnp.zeros((T, MLA_NOPE), F32), zero, sin, jnp.zeros((T, pad), F32)], axis=1))
    gq_p = jnp.pad(P["mla_q_norm"], ((0, 0), (0, pad)))
    gk_p = jnp.pad(P["mla_k_norm"], ((0, 0), (0, pad)))
    bias_full = jnp.repeat(P["sg_b"].T, SG_GROUP_DIM, axis=1)
    group_ind = jnp.repeat(jnp.eye(SG_GROUPS, dtype=F32), SG_GROUP_DIM, axis=0)

    HT = (D_MODEL, BF16, "T")

    def norm2(x, g):
        h = _rms(x, g)
        return h, h

    h1, h1t = _rowwise(norm2, "ffn1_norm", 512, [x], [P["ffn1_norm"]], [(D_MODEL, BF16), HT])
    gu1, a1t, o1 = _ffn_fwd(h1, W["ffn1_w_gu"], W["ffn1_w_down"], "ffn1")

    def resid_norm(x, o, g):
        xn = x + 0.5 * o
        h = _rms(xn, g)
        return xn, h, h

    x1, hm, hmt = _rowwise(resid_norm, "mix_norm", 512, [x, o1], [P["mix_norm"]],
                           [(D_MODEL, F32), (D_MODEL, BF16), HT])
    W.update(weights("mix", hm))
    z = _mm(hm, W["w_in"], "nn", BF16, "w_in", tm=1024, tn=768)

    y_a, y_at = _sg_fwd(z, P["sg_ln_g"], P["sg_ln_b"], P["sg_w"], bias_full)

    def c_norm(cq, ckv, gq, gkv):
        a, b = _rms(cq, gq), _rms(ckv, gkv)
        return a, b, a, b

    cqn, ckvn, cqnt, ckvnt = _rowwise(
        c_norm, "mla_cnorm", 512, [(z, MLA_Q_RANK, Z_CQ // MLA_Q_RANK), (z, MLA_KV_RANK, Z_CKV // MLA_KV_RANK)],
        [P["mla_cq_norm"], P["mla_ckv_norm"]],
        [(MLA_Q_RANK, BF16), (MLA_KV_RANK, BF16), (MLA_Q_RANK, BF16, "T"), (MLA_KV_RANK, BF16, "T")])
    q_pre = _mm(cqn, W["mla_w_uq"], "nn", F32, "mla_uq", tm=1024, tn=1024)
    kv_pre = _mm(ckvn, W["mla_w_ukv"], "nn", F32, "mla_ukv", tm=1024, tn=1024)
    q, k, v = _mla_post(q_pre, kv_pre, z, tabs, gq_p, gk_p)
    y_b, y_bt, lse = _flash_fwd(q, k, v)

    memn, = _rowwise(lambda m, g: _rms(m, g), "mem_norm", 256, [mem], [P["mem_norm"]], [(D_MODEL, BF16)])
    kvm = _mm(memn, W["mem_w_kv"], "nn", F32, "mem_kv")

    def mem_k(kvm, gk):
        ks = [_rms(kvm[:, h * LANES:(h + 1) * LANES], gk) for h in range(MEM_HEADS)]
        return jnp.concatenate(ks, axis=1), kvm[:, MEM_WIDTH:]

    km, vm = _rowwise(mem_k, "mem_knorm", 256, [kvm], [P["mem_k_norm"]], [(MEM_WIDTH, BF16), (MEM_WIDTH, BF16)])
    y_c, y_ct = _mem_fwd(z, km, vm, P["mem_q_norm"])

    pa = _mm(y_a, W["w_branch_a"], "nn", BF16, "branch_a", tm=1024, tn=1024)
    pb = _mm(y_b, W["w_branch_b"], "nn", BF16, "branch_b", tm=1024, tn=1024)
    pc = _mm(y_c, W["w_branch_c"], "nn", BF16, "branch_c", tm=1024, tn=1024)

    def merge(zg, pa, pb, pc, b):
        g = _sigmoid(zg + b)
        m = g[:, :D_MODEL] * pa + g[:, D_MODEL:2 * D_MODEL] * pb + g[:, 2 * D_MODEL:] * pc
        return m, m

    merged, mergedt = _rowwise(merge, "merge", 256, [(z, 3 * D_MODEL, 0), pa, pb, pc], [P["b_gate"]],
                               [(D_MODEL, BF16), HT])
    om = _mm(merged, W["w_out"], "nn", F32, "w_out", tm=1024, tn=1024)

    def resid_norm1(x, o, g):
        xn = x + o
        h = _rms(xn, g)
        return xn, h, h

    x2, h2, h2t = _rowwise(resid_norm1, "ffn2_norm", 512, [x1, om], [P["ffn2_norm"]],
                           [(D_MODEL, F32), (D_MODEL, BF16), HT])
    W.update(weights("ffn2", h2))
    gu2, a2t, o2 = _ffn_fwd(h2, W["ffn2_w_gu"], W["ffn2_w_down"], "ffn2")

    def loss_fn(x2, o2, t):
        e = x2 + 0.5 * o2 - t
        return e * (1.0 / D_MODEL), (e * (0.5 / D_MODEL)).astype(BF16), _rsum(e * e) * (0.5 / D_MODEL)

    dx3, do2, loss_part = _rowwise(loss_fn, "loss", 512, [x2, o2, loss_target], [],
                                   [(D_MODEL, F32), (D_MODEL, BF16)], [((1, D_MODEL), F32)])

    dh2, G["ffn2_w_gu"], G["ffn2_w_down"] = _ffn_bwd(do2, h2t, gu2, a2t, W["ffn2_w_gu"], W["ffn2_w_down"], "ffn2")
    tie = grads_out("ffn2", G)

    def norm_bwd(x, dh, dxo, g, *_):
        dx, dg = _rms_bwd(x, g, dh)
        dx = dx + dxo
        return dx, dx, dg

    dx2, dx2b, G["ffn2_norm"] = _rowwise(norm_bwd, "ffn2_norm_bwd", 512, [x2, dh2, dx3],
                                         [P["ffn2_norm"]] + ([] if tie is None else [tie]),
                                         [(D_MODEL, F32), (D_MODEL, BF16)], [((1, D_MODEL), F32)])

    G["w_out"] = _mm_t(mergedt, dx2b, "w_out_dw", tm=1024, tn=1024)
    dmerged = _mm(dx2b, W["w_out"], "nt", F32, "w_out_dx", tm=1024, tn=1024)

    def merge_bwd(zg, pa, pb, pc, dm, b):
        g = _sigmoid(zg + b)
        ps = jnp.concatenate([pa, pb, pc], axis=1)
        dm3 = jnp.concatenate([dm, dm, dm], axis=1)
        dzg = dm3 * ps * g * (1.0 - g)
        dp = dm3 * g
        return dzg, dp[:, :D_MODEL], dp[:, D_MODEL:2 * D_MODEL], dp[:, 2 * D_MODEL:], _rsum(dzg)

    dzg, dpa, dpb, dpc, G["b_gate"] = _rowwise(
        merge_bwd, "merge_bwd", 256, [(z, 3 * D_MODEL, 0), pa, pb, pc, dmerged], [P["b_gate"]],
        [(3 * D_MODEL, BF16), (D_MODEL, BF16), (D_MODEL, BF16), (D_MODEL, BF16)], [((1, 3 * D_MODEL), F32)])

    G["w_branch_a"] = _mm_t(y_at, dpa, "branch_a_dw", tm=512, tn=1024)
    G["w_branch_b"] = _mm_t(y_bt, dpb, "branch_b_dw", tm=1024, tn=1024)
    G["w_branch_c"] = _mm_t(y_ct, dpc, "branch_c_dw", tm=512, tn=1024)
    dy_a = _mm(dpa, W["w_branch_a"], "nt", BF16, "branch_a_dx", tm=1024, tn=512)
    dy_b = _mm(dpb, W["w_branch_b"], "nt", BF16, "branch_b_dx", tm=1024, tn=1024)
    dy_c = _mm(dpc, W["w_branch_c"], "nt", BF16, "branch_c_dx", tm=1024, tn=512)

    du_pre, dv_pre, G["sg_w"], dbias_t, G["sg_ln_g"], G["sg_ln_b"] = _sg_bwd(
        z, dy_a, P["sg_ln_g"], P["sg_ln_b"], P["sg_w"], bias_full, group_ind)
    G["sg_b"] = dbias_t.T

    dqm, dkm, dvm, G["mem_q_norm"] = _mem_bwd(z, dy_c, km, vm, P["mem_q_norm"])

    def mem_k_bwd(kvm, dkm, dvm, gk):
        dks = []
        dg = jnp.zeros((1, LANES), F32)
        for h in range(MEM_HEADS):
            sl = slice(h * LANES, (h + 1) * LANES)
            dx, d = _rms_bwd(kvm[:, sl], gk, dkm[:, sl])
            dks.append(dx)
            dg = dg + d
        return jnp.concatenate(dks + [dvm], axis=1), dg

    dkvm, G["mem_k_norm"] = _rowwise(mem_k_bwd, "mem_knorm_bwd", 256, [kvm, dkm, dvm], [P["mem_k_norm"]],
                                     [(2 * MEM_WIDTH, BF16)], [((1, LANES), F32)])
    G["mem_w_kv"] = _mm(memn, dkvm, "tn", BF16, "mem_kv_dw")
    dmemn = _mm(dkvm, W["mem_w_kv"], "nt", F32, "mem_kv_dx")
    _, G["mem_norm"] = _rowwise(lambda m, d, g: _rms_bwd(m, g, d), "mem_norm_bwd", 256, [mem, dmemn],
                                [P["mem_norm"]], [(D_MODEL, BF16)], [((1, D_MODEL), F32)])

    def delta_fn(o, do):
        od = o.astype(F32) * do.astype(F32)
        ds = [jnp.broadcast_to(jnp.sum(od[:, h * LANES:(h + 1) * LANES], axis=1, keepdims=True), (od.shape[0], LANES))
              for h in range(MLA_HEADS)]
        return jnp.concatenate(ds, axis=1)

    delta, = _rowwise(delta_fn, "mla_delta", 512, [y_b, dy_b], [], [(HP, F32)])
    rowform = lambda a: a.reshape(T, MLA_HEADS, LANES)[:, :, 0].T.reshape(MLA_HEADS, 1, T)
    dq = _flash_dq(q, k, v, dy_b, lse, delta)
    dk, dv = _flash_dkv(q, k, v, dy_b, rowform(lse), rowform(delta))
    dq_pre, dkv_pre, dkr, dgq, dgk = _mla_post_bwd(q_pre, kv_pre, z, tabs, gq_p, gk_p, dq, dk, dv)
    G["mla_q_norm"], G["mla_k_norm"] = dgq[:, :MLA_QK], dgk[:, :MLA_QK]
    G["mla_w_uq"] = _mm_t(cqnt, dq_pre, "mla_uq_dw", tm=384, tn=1024)
    G["mla_w_ukv"] = _mm_t(ckvnt, dkv_pre, "mla_ukv_dw", tm=256, tn=2048)
    dcqn = _mm(dq_pre, W["mla_w_uq"], "nt", F32, "mla_uq_dx", tm=1024)
    dckvn = _mm(dkv_pre, W["mla_w_ukv"], "nt", F32, "mla_ukv_dx", tm=1024)

    def c_norm_bwd(cq, ckv, dcqn, dckvn, gq, gkv):
        dcq, dgq = _rms_bwd(cq, gq, dcqn)
        dckv, dgkv = _rms_bwd(ckv, gkv, dckvn)
        return dcq, dckv, dgq, dgkv

    dcq, dckv, G["mla_cq_norm"], G["mla_ckv_norm"] = _rowwise(
        c_norm_bwd, "mla_cnorm_bwd", 512,
        [(z, MLA_Q_RANK, Z_CQ // MLA_Q_RANK), (z, MLA_KV_RANK, Z_CKV // MLA_KV_RANK), dcqn, dckvn],
        [P["mla_cq_norm"], P["mla_ckv_norm"]], [(MLA_Q_RANK, BF16), (MLA_KV_RANK, BF16)],
        [((1, MLA_Q_RANK), F32), ((1, MLA_KV_RANK), F32)])

    dz = jnp.concatenate([dzg, du_pre, dv_pre, dqm, dckv, dkr, dcq], axis=1)
    G["w_in"] = _mm_t(hmt, dz, "w_in_dw", tm=1024, tn=768)
    dhm = _mm(dz, W["w_in"], "nt", F32, "w_in_dx", tm=1024, tn=1024, tk=2688)

    def norm_bwd_half(x, dh, dxo, g):
        dx, dg = _rms_bwd(x, g, dh)
        dx = dx + dxo
        return dx, (0.5 * dx), dg

    dx1, do1, G["mix_norm"] = _rowwise(norm_bwd_half, "mix_norm_bwd", 512, [x1, dhm, dx2], [P["mix_norm"]],
                                       [(D_MODEL, F32), (D_MODEL, BF16)], [((1, D_MODEL), F32)])
    tie = grads_out("mix", G)

    def ffn1_dw(which, dw):
        G["ffn1_w_" + which] = dw
        return grads_out("ffn1_" + which, G)

    dh1, _, _ = _ffn_bwd(do1, h1t, gu1, a1t, W["ffn1_w_gu"], W["ffn1_w_down"], "ffn1", tie, ffn1_dw)

    def norm_bwd_last(x, dh, dxo, g):
        dx, dg = _rms_bwd(x, g, dh)
        return dx + dxo, dg

    grad_x, G["ffn1_norm"] = _rowwise(norm_bwd_last, "ffn1_norm_bwd", 512, [x, dh1, dx1], [P["ffn1_norm"]],
                                      [(D_MODEL, F32)], [((1, D_MODEL), F32)])
    return loss_part, grad_x, G


SHARDED = ["ffn1_w_gu", "ffn1_w_down", "w_in", "mla_w_uq", "mla_w_ukv", "mem_w_kv",
           "w_branch_a", "w_branch_b", "w_branch_c", "w_out", "ffn2_w_gu", "ffn2_w_down"]
ROW_SHARDED = {"ffn1_w_down", "mem_w_kv", "w_out", "ffn2_w_down"}
SMALL = ["ffn1_norm", "mix_norm", "b_gate", "sg_ln_g", "sg_ln_b", "sg_w", "sg_b", "mla_cq_norm",
         "mla_ckv_norm", "mla_q_norm", "mla_k_norm", "mem_norm", "mem_q_norm", "mem_k_norm", "ffn2_norm"]
ORDER = ["ffn1_norm", "ffn1_w_gu", "ffn1_w_down", "mix_norm", "w_in", "b_gate", "sg_ln_g", "sg_ln_b", "sg_w",
         "sg_b", "mla_cq_norm", "mla_w_uq", "mla_ckv_norm", "mla_w_ukv", "mla_q_norm", "mla_k_norm", "mem_norm",
         "mem_w_kv", "mem_q_norm", "mem_k_norm", "w_branch_a", "w_branch_b", "w_branch_c", "w_out", "ffn2_norm",
         "ffn2_w_gu", "ffn2_w_down"]

_IN_U, _IN_V, _IN_CQ, _IN_CKV, _IN_KR, _IN_QM, _IN_G = 0, 512, 1024, 1408, 1664, 1696, 2208
IN_COLS = 5280


def _full_from_slabs(name, slabs):
    n, r, c = slabs.shape
    if name in ROW_SHARDED:
        return slabs.reshape(n * r, c)
    return slabs.transpose(1, 0, 2).reshape(r, n * c)


def _slabs_from_full(name, full):
    if name in ROW_SHARDED:
        return full.reshape(N_DEV, full.shape[0] // N_DEV, full.shape[1])
    r, c = full.shape
    return full.reshape(r, N_DEV, c // N_DEV).transpose(1, 0, 2)


def _compute_layout(full):
    W = dict(full)
    if "w_in" not in full:
        return W
    w = full["w_in"]
    kr = jnp.pad(w[:, _IN_KR:_IN_QM], ((0, 0), (KR_LANE, LANES - KR_LANE - MLA_ROPE)))
    W["w_in"] = jnp.concatenate([w[:, _IN_G:], w[:, _IN_U:_IN_CQ], w[:, _IN_QM:_IN_G], w[:, _IN_CKV:_IN_KR], kr,
                                 w[:, _IN_CQ:_IN_CKV]], axis=1)
    uq = full["mla_w_uq"].reshape(MLA_Q_RANK, MLA_HEADS, MLA_QK)
    W["mla_w_uq"] = jnp.pad(uq, ((0, 0), (0, 0), (0, LANES - MLA_QK))).reshape(MLA_Q_RANK, HP)
    ukv = full["mla_w_ukv"].reshape(MLA_KV_RANK, MLA_HEADS, MLA_NOPE + MLA_V)
    padh = lambda a: jnp.pad(a, ((0, 0), (0, 0), (0, LANES - a.shape[2]))).reshape(MLA_KV_RANK, HP)
    W["mla_w_ukv"] = jnp.concatenate([padh(ukv[:, :, :MLA_NOPE]), padh(ukv[:, :, MLA_NOPE:])], axis=1)
    wb = full["w_branch_b"].reshape(MLA_HEADS, MLA_V, D_MODEL)
    W["w_branch_b"] = jnp.pad(wb, ((0, 0), (0, LANES - MLA_V), (0, 0))).reshape(HP, D_MODEL)
    return W


def _reference_layout(G):
    out = dict(G)
    if "w_in" not in G:
        return out
    g = G["w_in"]
    out["w_in"] = jnp.concatenate([
        g[:, Z_U:Z_QM], g[:, Z_CQ:Z_COLS], g[:, Z_CKV:Z_KR], g[:, Z_KR + KR_LANE:Z_KR + KR_LANE + MLA_ROPE],
        g[:, Z_QM:Z_CKV], g[:, Z_G:Z_U]], axis=1)
    out["mla_w_uq"] = G["mla_w_uq"].reshape(MLA_Q_RANK, MLA_HEADS, LANES)[:, :, :MLA_QK].reshape(MLA_Q_RANK, -1)
    gk = G["mla_w_ukv"][:, :HP].reshape(MLA_KV_RANK, MLA_HEADS, LANES)[:, :, :MLA_NOPE]
    gv = G["mla_w_ukv"][:, HP:].reshape(MLA_KV_RANK, MLA_HEADS, LANES)[:, :, :MLA_V]
    out["mla_w_ukv"] = jnp.concatenate([gk, gv], axis=2).reshape(MLA_KV_RANK, -1)
    out["w_branch_b"] = G["w_branch_b"].reshape(MLA_HEADS, LANES, D_MODEL)[:, :MLA_V].reshape(-1, D_MODEL)
    return out


def _pack(parts):
    flat = []
    for a in parts:
        a = a.reshape(-1)
        flat.append(jnp.pad(a, (0, (-a.shape[0]) % LANES)))
    return jnp.concatenate(flat).reshape(-1, LANES)


def _unpack(packed, shapes):
    flat = packed.reshape(-1)
    out, off = [], 0
    for shp in shapes:
        n = int(np.prod(shp))
        out.append(flat[off:off + n].reshape(shp))
        off += n + (-n) % LANES
    return out


MESH = pl.DeviceIdType.MESH
HBM = pl.BlockSpec(memory_space=pltpu.HBM)


def _all_gather(shard):
    rows, lanes = shard.shape

    def body(x_ref, out_ref, token_ref, send_sems, recv_sems, local_sem):
        x, y, c = lax.axis_index("x"), lax.axis_index("y"), lax.axis_index("c")
        me, sibling = (x, y, c), (x, y, 1 - c)
        chips = [(1 - x, y), (x, 1 - y), (1 - x, 1 - y)]
        token_ref[...] = jnp.zeros_like(token_ref)

        def slot(px, py, pc):
            return out_ref.at[4 * px + 2 * py + pc]

        def copy(k, block, to, src=None):
            return pltpu.make_async_remote_copy(
                src_ref=slot(*block) if src is None else src, dst_ref=slot(*block),
                send_sem=send_sems.at[k], recv_sem=recv_sems.at[k], device_id=to, device_id_type=MESH)

        mine = pltpu.make_async_copy(x_ref, slot(*me), local_sem)
        mine.start()
        first = [copy(0, me, sibling, src=x_ref)]
        first += [copy(1 + j, me, (*chip, c), src=x_ref) for j, chip in enumerate(chips)]
        for cp in first:
            cp.start()
        passed = [copy(4 + j, (*chip, c), sibling) for j, chip in enumerate(chips)]
        for j, chip in enumerate(chips):
            copy(1 + j, (*chip, c), me).wait_recv()
            passed[j].start()
        copy(0, sibling, me).wait_recv()
        for j, chip in enumerate(chips):
            copy(4 + j, (*chip, 1 - c), me).wait_recv()
        for cp in first + passed:
            cp.wait_send()
        mine.wait()

    return pl.pallas_call(
        body, name="all_gather_weights",
        out_shape=[jax.ShapeDtypeStruct((N_DEV, rows, lanes), shard.dtype), jax.ShapeDtypeStruct((8, LANES), F32)],
        in_specs=[HBM], out_specs=[HBM, pl.BlockSpec(memory_space=pltpu.VMEM)],
        scratch_shapes=[pltpu.SemaphoreType.DMA((7,)), pltpu.SemaphoreType.DMA((7,)), pltpu.SemaphoreType.DMA],
    )(shard)


SEM = pl.BlockSpec(memory_space=pltpu.SEMAPHORE)
DATAFLOW = pltpu.SideEffectType.DATAFLOW_SIDE_EFFECTING


def _peers():
    x, y, c = lax.axis_index("x"), lax.axis_index("y"), lax.axis_index("c")
    out = []
    for k in range(1, N_DEV):
        px = 1 - x if k & 4 else x
        py = 1 - y if k & 2 else y
        pc = 1 - c if k & 1 else c
        out.append((k, (px, py, pc), 4 * px + 2 * py + pc))
    return 4 * x + 2 * y + c, out


def _send_start(src, per_peer, name):
    rows = src.shape[-2]
    land = lax.empty((N_DEV, rows, LANES), src.dtype)

    def body(src_ref, land_ref, send_sems, recv_sems, src_thru, land_thru, token):
        me, peers = _peers()
        for k, pid, pflat in peers:
            pltpu.make_async_remote_copy(
                src_ref=src_ref.at[pflat] if per_peer else src_ref, dst_ref=land_ref.at[me],
                send_sem=send_sems.at[k - 1], recv_sem=recv_sems.at[k - 1],
                device_id=pid, device_id_type=MESH).start()
        token[...] = jnp.zeros_like(token)

    res = pl.pallas_call(
        body, name=name,
        out_shape=(pltpu.SemaphoreType.DMA((N_DEV - 1,)), pltpu.SemaphoreType.DMA((N_DEV - 1,)),
                   pltpu.HBM(src.shape, src.dtype), pltpu.HBM(land.shape, land.dtype),
                   jax.ShapeDtypeStruct((8, LANES), F32)),
        in_specs=(HBM, HBM), out_specs=(SEM, SEM, HBM, HBM, pl.BlockSpec(memory_space=pltpu.VMEM)),
        input_output_aliases={0: 2, 1: 3},
        compiler_params=pltpu.CompilerParams(has_side_effects=DATAFLOW),
    )(pltpu.with_memory_space_constraint(src, pltpu.HBM), pltpu.with_memory_space_constraint(land, pltpu.HBM))
    return res[:4], res[4]


def _send_wait(started, after, per_peer, name):
    send_sems, recv_sems, src_thru, land_thru = started

    def body(src_ref, land_ref, send_sems, recv_sems, after_ref, src_out, land_out):
        me, peers = _peers()
        for k, pid, pflat in peers:
            copy = pltpu.make_async_remote_copy(
                src_ref=src_ref.at[pflat] if per_peer else src_ref, dst_ref=land_ref.at[pflat],
                send_sem=send_sems.at[k - 1], recv_sem=recv_sems.at[k - 1],
                device_id=pid, device_id_type=MESH)
            copy.wait_send()
            copy.wait_recv()

    src_out, land = pl.pallas_call(
        body, name=name,
        out_shape=(pltpu.HBM(src_thru.shape, src_thru.dtype), pltpu.HBM(land_thru.shape, land_thru.dtype)),
        in_specs=(HBM, HBM, SEM, SEM, pl.BlockSpec(memory_space=pl.ANY)), out_specs=(HBM, HBM),
        input_output_aliases={0: 0, 1: 1},
        compiler_params=pltpu.CompilerParams(has_side_effects=DATAFLOW),
    )(src_thru, land_thru, send_sems, recv_sems, after)
    me = 4 * lax.axis_index("x") + 2 * lax.axis_index("y") + lax.axis_index("c")
    own = lax.dynamic_index_in_dim(src_out, me, 0, keepdims=True) if per_peer else src_out[None]
    return lax.dynamic_update_slice(land, own, (me, 0, 0))


def _share_rows(block, name):
    def body(src_ref, out_ref, send_sems, recv_sems, local_sem):
        me, peers = _peers()
        own = pltpu.make_async_copy(src_ref, out_ref.at[me], local_sem)
        own.start()
        copies = [pltpu.make_async_remote_copy(
            src_ref=src_ref, dst_ref=out_ref.at[me], send_sem=send_sems.at[k - 1], recv_sem=recv_sems.at[k - 1],
            device_id=pid, device_id_type=MESH) for k, pid, _ in peers]
        for cp in copies:
            cp.start()
        for cp in copies:
            cp.wait()
        own.wait()

    return pl.pallas_call(
        body, name=name, out_shape=jax.ShapeDtypeStruct((N_DEV,) + block.shape, block.dtype),
        in_specs=[HBM], out_specs=HBM,
        scratch_shapes=[pltpu.SemaphoreType.DMA((N_DEV - 1,)), pltpu.SemaphoreType.DMA((N_DEV - 1,)),
                        pltpu.SemaphoreType.DMA],
    )(block)


def _sum_slots(recv, name, tr):
    n, rows, lanes = recv.shape
    tr = _tile(rows, tr)

    def body(r_ref, o_ref):
        acc = r_ref[0].astype(F32)
        for i in range(1, n):
            acc = acc + r_ref[i].astype(F32)
        o_ref[...] = acc

    return pl.pallas_call(
        body, name=name, grid=(rows // tr,),
        in_specs=[pl.BlockSpec((n, tr, lanes), lambda i: (0, i, 0))],
        out_specs=pl.BlockSpec((tr, lanes), lambda i: (i, 0)),
        out_shape=jax.ShapeDtypeStruct((rows, lanes), F32),
        compiler_params=_cparams(("parallel",)),
    )(recv)


def _adamw(w, g, m, v, name, tr=256):
    c1 = 1.0 - ADAM_B1 ** ADAM_STEP
    c2 = 1.0 - ADAM_B2 ** ADAM_STEP

    def fn(w, g, m, v):
        m = ADAM_B1 * m + (1.0 - ADAM_B1) * g
        v = ADAM_B2 * v + (1.0 - ADAM_B2) * (g * g)
        delta = -ADAM_LR * ((m / c1) / (jnp.sqrt(v / c2) + ADAM_EPS) + ADAM_WD * w)
        return delta, m, v

    return _rowwise(fn, name, tr, [w, g, m, v], [], [(w.shape[1], F32)] * 3)


def kernel(x, mem, positions, ffn1_norm, ffn1_w_gu, ffn1_w_down, mix_norm, w_in, b_gate, sg_ln_g, sg_ln_b, sg_w, sg_b, mla_cq_norm, mla_w_uq, mla_ckv_norm, mla_w_ukv, mla_q_norm, mla_k_norm, mem_norm, mem_w_kv, mem_q_norm, mem_k_norm, w_branch_a, w_branch_b, w_branch_c, w_out, ffn2_norm, ffn2_w_gu, ffn2_w_down, loss_target, m_ffn1_norm, m_ffn1_w_gu, m_ffn1_w_down, m_mix_norm, m_w_in, m_b_gate, m_sg_ln_g, m_sg_ln_b, m_sg_w, m_sg_b, m_mla_cq_norm, m_mla_w_uq, m_mla_ckv_norm, m_mla_w_ukv, m_mla_q_norm, m_mla_k_norm, m_mem_norm, m_mem_w_kv, m_mem_q_norm, m_mem_k_norm, m_w_branch_a, m_w_branch_b, m_w_branch_c, m_w_out, m_ffn2_norm, m_ffn2_w_gu, m_ffn2_w_down, v_ffn1_norm, v_ffn1_w_gu, v_ffn1_w_down, v_mix_norm, v_w_in, v_b_gate, v_sg_ln_g, v_sg_ln_b, v_sg_w, v_sg_b, v_mla_cq_norm, v_mla_w_uq, v_mla_ckv_norm, v_mla_w_ukv, v_mla_q_norm, v_mla_k_norm, v_mem_norm, v_mem_w_kv, v_mem_q_norm, v_mem_k_norm, v_w_branch_a, v_w_branch_b, v_w_branch_c, v_w_out, v_ffn2_norm, v_ffn2_w_gu, v_ffn2_w_down):
    given = dict(locals())
    wts = {n: given[n] for n in ORDER}
    mom = {n: given["m_" + n] for n in ORDER}
    var = {n: given["v_" + n] for n in ORDER}
    shard_shapes = {n: wts[n].shape[1:] for n in SHARDED}
    shard_rows = {n: int(np.prod(shard_shapes[n])) // LANES for n in SHARDED}

    def pack_shards(group):
        return jnp.concatenate([wts[n][0].astype(BF16).reshape(shard_rows[n], LANES) for n in GROUPS[group]], axis=0)

    def split_rows(names, packed, lead):
        out, off = {}, 0
        for n in names:
            out[n] = packed[..., off:off + shard_rows[n], :].reshape(*lead, *shard_shapes[n])
            off += shard_rows[n]
        return out

    def full_weights(group, gathered):
        slabs = split_rows(GROUPS[group], gathered, (N_DEV,))
        return _compute_layout({n: _full_from_slabs(n, s) for n, s in slabs.items()})

    def zero_of(a):
        return jnp.minimum(jnp.abs(a.reshape(-1)[0]), 0).astype(BF16)

    gathered_ffn1, token = _all_gather(pack_shards("ffn1"))
    flight = {"mix": _send_start(pack_shards("mix") + token[0, 0].astype(BF16), False, "gather_mix_start")[0]}
    recv = {}

    def weights(group, after):
        if group == "ffn1":
            return full_weights(group, gathered_ffn1)
        landed = _send_wait(flight.pop(group), after, False, f"gather_{group}_wait")
        if group == "mix":
            flight["ffn2"] = _send_start(pack_shards("ffn2") + zero_of(landed), False, "gather_ffn2_start")[0]
        return full_weights(group, landed)

    small_shapes = [wts[n].shape[1:] for n in SMALL]
    early = SMALL[1:]
    assert SMALL[0] == "ffn1_norm"

    def grads_out(group, G):
        Gr = _reference_layout({n: G[n] for n in GRAD_GROUPS[group]})
        parts = [_slabs_from_full(n, Gr[n]).astype(BF16).reshape(N_DEV, shard_rows[n], LANES)
                 for n in GRAD_GROUPS[group]]
        flight["g_" + group], tie = _send_start(jnp.concatenate(parts, axis=1), True, f"grads_{group}_start")
        if group == "mix":
            small = _pack([G[n].reshape(s) for n, s in zip(early, small_shapes[1:])])
            small = jnp.pad(small, ((0, (-small.shape[0]) % 8), (0, 0)))
            flight["small"], tie = _send_start(small + tie[0, 0], False, "grads_small_start")
        return tie

    P = {n: wts[n] if wts[n].ndim == 2 else wts[n][0] for n in SMALL}
    loss_part, grad_x, G = _local_step(x[0], mem[0], positions[0], loss_target[0], P, weights, grads_out)

    for group in GRAD_GROUPS:
        recv[group] = _send_wait(flight.pop("g_" + group), grad_x, True, f"grads_{group}_wait")
    early_recv = _send_wait(flight.pop("small"), grad_x, False, "grads_small_wait")
    last = _share_rows(G["ffn1_norm"].reshape(-1, LANES), "share_ffn1_norm")
    g_small_packed = _sum_slots(jnp.concatenate([last, early_recv], axis=1), "sum_small", 2048)
    grads = {}
    for group, names in GRAD_GROUPS.items():
        grads.update(split_rows(names, _sum_slots(recv[group], "sum_" + group, 1024), ()))
    grads.update(zip(SMALL, _unpack(g_small_packed, small_shapes)))

    delta, new_m, new_v = {}, {}, {}
    for n in SHARDED:
        delta[n], new_m[n], new_v[n] = _adamw(wts[n][0], grads[n], mom[n][0], var[n][0], "adamw_" + n)

    def pack_small(d):
        p = _pack([d[n].reshape(s) for n, s in zip(SMALL, small_shapes)])
        return jnp.pad(p, ((0, (-p.shape[0]) % 8), (0, 0)))

    ds, ms, vs = _adamw(pack_small(wts), g_small_packed, pack_small(mom), pack_small(var), "adamw_small", tr=2048)
    for dst, packed in ((delta, ds), (new_m, ms), (new_v, vs)):
        dst.update(zip(SMALL, _unpack(packed, small_shapes)))

    loss = lax.psum(jnp.sum(loss_part), ("x", "y", "c"))
    lead = lambda d: [d[n].reshape(wts[n].shape) for n in ORDER]
    return (loss, grad_x[None], *lead(grads), *lead(delta), *lead(new_m), *lead(new_v))
```

```python
import functools

import numpy as np
import jax
import jax.numpy as jnp
from jax import lax
from jax.experimental import pallas as pl
from jax.experimental.pallas import tpu as pltpu

F32, BF16 = jnp.float32, jnp.bfloat16

D_MODEL = 1024
SG_GROUPS, SG_GROUP_DIM, SG_WIDTH, CHUNK = 8, 64, 512, 128
MLA_HEADS, MLA_NOPE, MLA_ROPE, MLA_V, MLA_QK = 8, 64, 32, 64, 96
MLA_Q_RANK, MLA_KV_RANK = 384, 256
MEM_HEADS, MEM_HEAD_DIM, MEM_WIDTH = 4, 128, 512
D_FF = 2816
ROPE_BASE = 10000.0
EPS = 1e-6
NEG = -1e30
ADAM_LR, ADAM_B1, ADAM_B2, ADAM_EPS, ADAM_WD, ADAM_STEP = 0.001, 0.9, 0.999, 1e-08, 0.01, 10

N_DEV = 8
LANES = 128
V7X_VMEM_LIMIT = 56 * 1024 * 1024
HP = MLA_HEADS * LANES

Z_G, Z_U, Z_V, Z_QM, Z_CKV, Z_KR, Z_CQ = 0, 3072, 3584, 4096, 4608, 4864, 4992
Z_COLS = 5376
KR_LANE = 64


def _tile(dim, pref):
    if dim <= pref:
        return dim
    for t in range(pref - pref % LANES, LANES - 1, -LANES):
        if dim % t == 0:
            return t
    for t in range(pref - pref % 8, 7, -8):
        if dim % t == 0:
            return t
    return dim


def _cparams(sem):
    return pltpu.CompilerParams(dimension_semantics=sem, vmem_limit_bytes=V7X_VMEM_LIMIT)


_DN = {"nn": ((1,), (0,)), "nt": ((1,), (1,)), "tn": ((0,), (0,))}


def _dot(a, b, mode="nn"):
    return lax.dot_general(a.astype(BF16), b.astype(BF16), (_DN[mode], ((), ())),
                           preferred_element_type=F32)


def _mm(a, b, mode, out_dtype, name, tm=512, tn=512, tk=2048, tie=None):
    if mode == "tn":
        K, M = a.shape
    else:
        M, K = a.shape
    N = b.shape[0] if mode == "nt" else b.shape[1]
    tm, tn, tk = _tile(M, tm), _tile(N, tn), _tile(K, tk)
    nk = K // tk
    if mode == "tn":
        a_spec = pl.BlockSpec((tk, tm), lambda i, j, k: (k, i))
    else:
        a_spec = pl.BlockSpec((tm, tk), lambda i, j, k: (i, k))
    if mode == "nt":
        b_spec = pl.BlockSpec((tn, tk), lambda i, j, k: (j, k))
    else:
        b_spec = pl.BlockSpec((tk, tn), lambda i, j, k: (k, j))

    ties = [] if tie is None else [tie]

    def body(a_ref, b_ref, *rest):
        o_ref, *scratch = rest[len(ties):]
        p = _dot(a_ref[...], b_ref[...], mode)
        if nk == 1:
            o_ref[...] = p.astype(o_ref.dtype)
        else:
            acc_ref, = scratch
            k = pl.program_id(2)

            @pl.when(k == 0)
            def _():
                acc_ref[...] = p

            @pl.when(k > 0)
            def _():
                acc_ref[...] += p

            @pl.when(k == nk - 1)
            def _():
                o_ref[...] = acc_ref[...].astype(o_ref.dtype)

    return pl.pallas_call(
        body, name=name, grid=(M // tm, N // tn, nk),
        in_specs=[a_spec, b_spec] + [pl.BlockSpec(t.shape, lambda i, j, k: (0, 0)) for t in ties],
        out_specs=pl.BlockSpec((tm, tn), lambda i, j, k: (i, j)),
        out_shape=jax.ShapeDtypeStruct((M, N), out_dtype),
        scratch_shapes=[] if nk == 1 else [pltpu.VMEM((tm, tn), F32)],
        compiler_params=_cparams(("parallel", "parallel", "arbitrary")),
    )(a, b, *ties)


def _mm_t(at, b, name, tm, tn, tk=1024, tie=None):
    return _mm(at, b, "nn", BF16, name, tm=tm, tn=tn, tk=tk, tie=tie)


def _rowwise(fn, name, tr, row_ins, bc_ins, row_outs, acc_outs=()):
    norm = [it if isinstance(it, tuple) else (it, it.shape[1], 0) for it in row_ins]
    rows = norm[0][0].shape[0]
    tr = _tile(rows, tr)
    arrays, in_specs = [], []
    for arr, w, cb in norm:
        arrays.append(arr)
        in_specs.append(pl.BlockSpec((tr, w), lambda i, cb=cb: (i, cb)))
    for arr in bc_ins:
        arrays.append(arr)
        in_specs.append(pl.BlockSpec(arr.shape, lambda i, nd=arr.ndim: (0,) * nd))
    out_shape, out_specs = [], []
    transposed = [len(o) == 3 for o in row_outs]
    for (w, dt, *_), t in zip(row_outs, transposed):
        out_shape.append(jax.ShapeDtypeStruct((w, rows) if t else (rows, w), dt))
        out_specs.append(pl.BlockSpec((w, tr), lambda i: (0, i)) if t else pl.BlockSpec((tr, w), lambda i: (i, 0)))
    for shp, dt in acc_outs:
        out_shape.append(jax.ShapeDtypeStruct(shp, dt))
        out_specs.append(pl.BlockSpec(shp, lambda i, nd=len(shp): (0,) * nd))
    n_in, n_row = len(arrays), len(row_outs)

    def body(*refs):
        vals = fn(*[r[...].astype(F32) for r in refs[:n_in]])
        if not isinstance(vals, (tuple, list)):
            vals = (vals,)
        outs = refs[n_in:]
        for r, v, t in zip(outs[:n_row], vals[:n_row], transposed):
            r[...] = v.astype(F32).T.astype(r.dtype) if t else v.astype(r.dtype)
        if acc_outs:
            accs = list(zip(outs[n_row:], vals[n_row:]))
            i = pl.program_id(0)

            @pl.when(i == 0)
            def _():
                for r, v in accs:
                    r[...] = v.astype(r.dtype)

            @pl.when(i > 0)
            def _():
                for r, v in accs:
                    r[...] += v.astype(r.dtype)

    res = pl.pallas_call(
        body, name=name, grid=(rows // tr,), in_specs=in_specs, out_specs=out_specs,
        out_shape=out_shape, compiler_params=_cparams(("arbitrary",)),
    )(*arrays)
    return res


def _rsum(x):
    return jnp.sum(x, axis=0, keepdims=True)


def _rms(x, g, n=None):
    n = x.shape[-1] if n is None else n
    r = lax.rsqrt(jnp.sum(x * x, axis=-1, keepdims=True) * (1.0 / n) + EPS)
    return x * r * g


def _rms_bwd(x, g, dy, n=None):
    n = x.shape[-1] if n is None else n
    r = lax.rsqrt(jnp.sum(x * x, axis=-1, keepdims=True) * (1.0 / n) + EPS)
    xh = x * r
    dxh = dy * g
    dx = r * (dxh - xh * (jnp.sum(dxh * xh, axis=-1, keepdims=True) * (1.0 / n)))
    return dx, _rsum(dy * xh)


def _gelu(x):
    return 0.5 * x * (1.0 + lax.erf(x * 0.7071067811865476))


def _gelu_grad(x):
    return 0.5 * (1.0 + lax.erf(x * 0.7071067811865476)) + x * jnp.exp(-0.5 * x * x) * 0.3989422804014327


def _sigmoid(x):
    return 1.0 / (1.0 + jnp.exp(-x))


def _ffn_fwd(h, w_gu, w_down, tag):
    gu = _mm(h, w_gu, "nn", BF16, f"{tag}_gu", tm=1024, tn=512)

    def act(gu):
        g = gu[:, :D_FF].astype(F32)
        u = gu[:, D_FF:].astype(F32)
        a = g * _sigmoid(g) * u
        return a, a

    a, at = _rowwise(act, f"{tag}_act", 256, [gu], [], [(D_FF, BF16), (D_FF, BF16, "T")])
    o = _mm(a, w_down, "nn", F32, f"{tag}_down", tm=1024, tn=512, tk=2816)
    return gu, at, o


def _ffn_bwd(do, ht, gu, at, w_gu, w_down, tag, tie=None, on_dw=None):
    on_dw = on_dw or (lambda which, dw: None)
    dw_down = _mm_t(at, do, f"{tag}_dwdown", tm=1408, tn=1024, tie=tie)
    da = _mm(do, w_down, "nt", BF16, f"{tag}_da", tm=1024, tn=512, tie=on_dw("down", dw_down))

    def act_bwd(gu, da):
        g = gu[:, :D_FF].astype(F32)
        u = gu[:, D_FF:].astype(F32)
        da = da.astype(F32)
        s = _sigmoid(g)
        dg = da * u * s * (1.0 + g * (1.0 - s))
        du = da * g * s
        return jnp.concatenate([dg, du], axis=1)

    dgu, = _rowwise(act_bwd, f"{tag}_actbwd", 256, [gu, da], [], [(2 * D_FF, BF16)])
    dw_gu = _mm_t(ht, dgu, f"{tag}_dwgu", tm=1024, tn=1408)
    dh = _mm(dgu, w_gu, "nt", F32, f"{tag}_dh", tm=1024, tn=512, tk=2816, tie=on_dw("gu", dw_gu))
    return dh, dw_gu, dw_down


def _sg_common(u_pre, v_pre, ln_g, ln_b):
    u = _gelu(u_pre)
    v = _gelu(v_pre)
    mu = jnp.mean(v, axis=-1, keepdims=True)
    vc = v - mu
    rstd = lax.rsqrt(jnp.mean(vc * vc, axis=-1, keepdims=True) + EPS)
    vhat = vc * rstd
    vl = vhat * ln_g + ln_b
    return u, vhat, rstd, vl


def _sg_masked_pairs(w):
    t = lax.broadcasted_iota(jnp.int32, (CHUNK, CHUNK), 0)
    s = lax.broadcasted_iota(jnp.int32, (CHUNK, CHUNK), 1)
    causal = s <= t
    wm = [jnp.where(causal, w[g], 0.0).astype(BF16) for g in range(SG_GROUPS)]
    return [jnp.concatenate([wm[2 * j], wm[2 * j + 1]], axis=0) for j in range(SG_GROUPS // 2)], causal


def _sg_mix(vl, pairs, bias):
    tr = vl.shape[0]
    low = lax.broadcasted_iota(jnp.int32, (CHUNK, LANES), 1) < SG_GROUP_DIM
    vb = vl.astype(BF16)
    rows = []
    for c in range(tr // CHUNK):
        slabs = []
        for j in range(SG_GROUPS // 2):
            slab = vb[c * CHUNK:(c + 1) * CHUNK, j * LANES:(j + 1) * LANES]
            m = _dot(pairs[j], slab)
            slabs.append(jnp.where(low, m[:CHUNK], m[CHUNK:]))
        rows.append(jnp.concatenate(slabs, axis=1) + bias)
    return jnp.concatenate(rows, axis=0)


def _sg_fwd(z, ln_g, ln_b, sg_w, bias_full):
    def fn(u_pre, v_pre, ln_g, ln_b, w, bias):
        u, _, _, vl = _sg_common(u_pre, v_pre, ln_g, ln_b)
        pairs, _ = _sg_masked_pairs(w)
        y = u * _sg_mix(vl, pairs, bias)
        return y, y

    return _rowwise(fn, "sg_fwd", 512, [(z, SG_WIDTH, Z_U // SG_WIDTH), (z, SG_WIDTH, Z_V // SG_WIDTH)],
                    [ln_g, ln_b, sg_w, bias_full], [(SG_WIDTH, BF16), (SG_WIDTH, BF16, "T")])


def _sg_bwd(z, dy, ln_g, ln_b, sg_w, bias_full, group_ind):
    def fn(u_pre, v_pre, dy, ln_g, ln_b, w, bias, ind):
        dy = dy.astype(F32)
        u, vhat, rstd, vl = _sg_common(u_pre, v_pre, ln_g, ln_b)
        pairs, causal = _sg_masked_pairs(w)
        mixed = _sg_mix(vl, pairs, bias)
        du_pre = dy * mixed * _gelu_grad(u_pre)
        dmix = dy * u
        tr = dy.shape[0]
        low = lax.broadcasted_iota(jnp.int32, (CHUNK, LANES), 1) < SG_GROUP_DIM
        vb = vl.astype(BF16)
        dw = [jnp.zeros((CHUNK, CHUNK), F32) for _ in range(SG_GROUPS)]
        dbias = jnp.zeros((CHUNK, SG_WIDTH), F32)
        dvl_rows = []
        for c in range(tr // CHUNK):
            dm_c = dmix[c * CHUNK:(c + 1) * CHUNK]
            dbias = dbias + dm_c
            slabs = []
            for j in range(SG_GROUPS // 2):
                slab = vb[c * CHUNK:(c + 1) * CHUNK, j * LANES:(j + 1) * LANES]
                dm = dm_c[:, j * LANES:(j + 1) * LANES]
                d0 = jnp.where(low, dm, 0.0).astype(BF16)
                d1 = jnp.where(low, 0.0, dm).astype(BF16)
                dw[2 * j] = dw[2 * j] + _dot(d0, slab, "nt")
                dw[2 * j + 1] = dw[2 * j + 1] + _dot(d1, slab, "nt")
                slabs.append(_dot(pairs[j], jnp.concatenate([d0, d1], axis=0), "tn"))
            dvl_rows.append(jnp.concatenate(slabs, axis=1))
        dvl = jnp.concatenate(dvl_rows, axis=0)
        dln_g = _rsum(dvl * vhat)
        dln_b = _rsum(dvl)
        dvh = dvl * ln_g
        dv = rstd * (dvh - jnp.mean(dvh, axis=-1, keepdims=True)
                     - vhat * jnp.mean(dvh * vhat, axis=-1, keepdims=True))
        dv_pre = dv * _gelu_grad(v_pre)
        dw = jnp.stack([jnp.where(causal, d, 0.0) for d in dw], axis=0)
        dbias_t = lax.dot_general(dbias, ind, (((1,), (0,)), ((), ())), precision=lax.Precision.HIGHEST,
                                  preferred_element_type=F32)
        return du_pre, dv_pre, dw, dbias_t, dln_g, dln_b

    return _rowwise(fn, "sg_bwd", 512,
                    [(z, SG_WIDTH, Z_U // SG_WIDTH), (z, SG_WIDTH, Z_V // SG_WIDTH), dy],
                    [ln_g, ln_b, sg_w, bias_full, group_ind],
                    [(SG_WIDTH, BF16), (SG_WIDTH, BF16)],
                    [((SG_GROUPS, CHUNK, CHUNK), F32), ((CHUNK, SG_GROUPS), F32), ((1, SG_WIDTH), F32), ((1, SG_WIDTH), F32)])


def _rope(x, c, s1, s2):
    return x * c + pltpu.roll(x, LANES - MLA_ROPE // 2, 1) * s1 + pltpu.roll(x, MLA_ROPE // 2, 1) * s2


def _rope_t(d, c, s1, s2):
    return d * c + pltpu.roll(d * s1, MLA_ROPE // 2, 1) + pltpu.roll(d * s2, LANES - MLA_ROPE // 2, 1)


def _mla_post(q_pre, kv_pre, z, tabs, gq, gk):
    scale = MLA_QK ** -0.5 * LOG2E

    def fn(q_pre, k_pre, v_pre, kr, c, s1, s2, gq, gk):
        qs, ks = [], []
        for h in range(MLA_HEADS):
            sl = slice(h * LANES, (h + 1) * LANES)
            qs.append(_rope(_rms(q_pre[:, sl], gq, MLA_QK), c, s1, s2) * scale)
            ks.append(_rope(_rms(k_pre[:, sl] + kr, gk, MLA_QK), c, s1, s2))
        lane = lax.broadcasted_iota(jnp.int32, v_pre.shape, 1) & (LANES - 1)
        return jnp.concatenate(qs, axis=1), jnp.concatenate(ks, axis=1), jnp.where(lane == ONES_LANE, 1.0, v_pre)

    return _rowwise(fn, "mla_post", 256,
                    [q_pre, (kv_pre, HP, 0), (kv_pre, HP, 1), (z, LANES, Z_KR // LANES), *tabs],
                    [gq, gk], [(HP, BF16)] * 3)


def _mla_post_bwd(q_pre, kv_pre, z, tabs, gq, gk, dq, dk, dv):
    scale = MLA_QK ** -0.5

    def fn(q_pre, k_pre, kr, c, s1, s2, dq, dk, dv, gq, gk):
        lane = lax.broadcasted_iota(jnp.int32, (1, LANES), 1)
        kr_mask = (lane >= KR_LANE) & (lane < KR_LANE + MLA_ROPE)
        dqs, dks = [], []
        dgq = jnp.zeros((1, LANES), F32)
        dgk = jnp.zeros((1, LANES), F32)
        dkr = jnp.zeros(kr.shape, F32)
        for h in range(MLA_HEADS):
            sl = slice(h * LANES, (h + 1) * LANES)
            dqn = _rope_t(dq[:, sl].astype(F32), c, s1, s2) * scale
            dx, dg = _rms_bwd(q_pre[:, sl], gq, dqn, MLA_QK)
            dqs.append(dx)
            dgq = dgq + dg
            dkn = _rope_t(dk[:, sl].astype(F32), c, s1, s2)
            dx, dg = _rms_bwd(k_pre[:, sl] + kr, gk, dkn, MLA_QK)
            dks.append(dx)
            dgk = dgk + dg
            dkr = dkr + dx
        dkr = jnp.where(kr_mask, dkr, 0.0)
        dkv = jnp.concatenate(dks + [dv.astype(F32)], axis=1)
        return jnp.concatenate(dqs, axis=1), dkv, dkr, dgq, dgk

    return _rowwise(fn, "mla_post_bwd", 256,
                    [q_pre, (kv_pre, HP, 0), (z, LANES, Z_KR // LANES), *tabs, dq, dk, dv],
                    [gq, gk], [(HP, BF16), (2 * HP, BF16), (LANES, BF16)],
                    [((1, LANES), F32), ((1, LANES), F32)])


def _pairs(n, lower):
    a, b = [], []
    for o in range(n):
        inner = range(o + 1) if lower else range(o, n)
        for t in inner:
            a.append(o)
            b.append(t)
    return jnp.asarray(np.array(a, np.int32)), jnp.asarray(np.array(b, np.int32))


FLASH_TILE, FLASH_SUB_ROWS = 1024, 512
LOG2E, LN2 = 1.4426950408889634, 0.6931471805599453
ONES_LANE = MLA_V


def _flash_tiles(T):
    tq = _tile(T, FLASH_TILE)
    return tq, _tile(tq, FLASH_SUB_ROWS)


def _col_span(t, sr, rb, diag, key_major):
    if not diag:
        return 0, t
    return (rb * sr, t) if key_major else (0, (rb + 1) * sr)


def _span_iota(sr, rb, c0, c1):
    r = lax.broadcasted_iota(jnp.int32, (sr, c1 - c0), 0) + rb * sr
    c = lax.broadcasted_iota(jnp.int32, (sr, c1 - c0), 1) + c0
    return r, c


def _lanes(x, width):
    return jnp.concatenate([x] * (width // LANES), axis=1)


def _flash_fwd(q, k, v):
    T = q.shape[0]
    tq, sr = _flash_tiles(T)
    n = T // tq
    ii, jj = _pairs(n, True)

    def body(ii_ref, jj_ref, q_ref, k_ref, v_ref, o_ref, ot_ref, lse_ref, m_sc, acc_sc):
        p_ = pl.program_id(1)
        i, j = ii_ref[p_], jj_ref[p_]

        @pl.when(j == 0)
        def _():
            m_sc[...] = jnp.full(m_sc.shape, NEG, F32)
            acc_sc[...] = jnp.zeros(acc_sc.shape, F32)

        def tile(diag):
            for rb in range(tq // sr):
                rows = slice(rb * sr, (rb + 1) * sr)
                c0, c1 = _col_span(tq, sr, rb, diag, False)
                s = _dot(q_ref[rows, :], k_ref[c0:c1, :], "nt")
                if diag:
                    r, c = _span_iota(sr, rb, c0, c1)
                    s = jnp.where(c <= r, s, NEG)
                m = m_sc[rows, :]
                m_new = jnp.maximum(m, jnp.max(s, axis=1, keepdims=True))
                p = jnp.exp2(s - _lanes(m_new, c1 - c0))
                acc_sc[rows, :] = jnp.exp2(m - m_new) * acc_sc[rows, :] + _dot(p, v_ref[c0:c1, :])
                m_sc[rows, :] = m_new

        @pl.when(j < i)
        def _():
            tile(False)

        @pl.when(j == i)
        def _():
            tile(True)
            acc = acc_sc[...]
            lane = lax.broadcasted_iota(jnp.int32, acc.shape, 1)
            l = jnp.sum(jnp.where(lane == ONES_LANE, acc, 0.0), axis=1, keepdims=True)
            o = jnp.where(lane < MLA_V, acc / l, 0.0)
            o_ref[...] = o.astype(o_ref.dtype)
            ot_ref[...] = o.T.astype(ot_ref.dtype)
            lse_ref[...] = m_sc[...] + jnp.log2(l)

    blk = lambda which: pl.BlockSpec((tq, LANES), which)
    qmap = lambda h, p, ii, jj: (ii[p], h)
    kmap = lambda h, p, ii, jj: (jj[p], h)
    return pl.pallas_call(
        body, name="mla_flash_fwd",
        grid_spec=pltpu.PrefetchScalarGridSpec(
            num_scalar_prefetch=2, grid=(MLA_HEADS, int(ii.shape[0])),
            in_specs=[blk(qmap), blk(kmap), blk(kmap)],
            out_specs=[blk(qmap), pl.BlockSpec((LANES, tq), lambda h, p, ii, jj: (h, ii[p])), blk(qmap)],
            scratch_shapes=[pltpu.VMEM((tq, LANES), F32)] * 2),
        out_shape=[jax.ShapeDtypeStruct((T, HP), BF16), jax.ShapeDtypeStruct((HP, T), BF16),
                   jax.ShapeDtypeStruct((T, HP), F32)],
        compiler_params=_cparams(("parallel", "arbitrary")),
    )(ii, jj, q, k, v)


def _flash_dq(q, k, v, do, lse, delta):
    T = q.shape[0]
    tq, sr = _flash_tiles(T)
    n = T // tq
    ii, jj = _pairs(n, True)

    def body(ii_ref, jj_ref, q_ref, k_ref, v_ref, do_ref, lse_ref, dl_ref, dq_ref, acc_sc):
        p_ = pl.program_id(1)
        i, j = ii_ref[p_], jj_ref[p_]

        @pl.when(j == 0)
        def _():
            acc_sc[...] = jnp.zeros(acc_sc.shape, F32)

        def tile(diag):
            for rb in range(tq // sr):
                rows = slice(rb * sr, (rb + 1) * sr)
                c0, c1 = _col_span(tq, sr, rb, diag, False)
                ks = k_ref[c0:c1, :]
                p = jnp.exp2(_dot(q_ref[rows, :], ks, "nt") - _lanes(lse_ref[rows, :], c1 - c0))
                if diag:
                    r, c = _span_iota(sr, rb, c0, c1)
                    p = jnp.where(c <= r, p, 0.0)
                dp = _dot(do_ref[rows, :], v_ref[c0:c1, :], "nt")
                acc_sc[rows, :] += _dot(p * (dp - _lanes(dl_ref[rows, :], c1 - c0)), ks)

        @pl.when(j < i)
        def _():
            tile(False)

        @pl.when(j == i)
        def _():
            tile(True)
            dq_ref[...] = acc_sc[...]

    blk = lambda which: pl.BlockSpec((tq, LANES), which)
    qmap = lambda h, p, ii, jj: (ii[p], h)
    kmap = lambda h, p, ii, jj: (jj[p], h)
    return pl.pallas_call(
        body, name="mla_flash_dq",
        grid_spec=pltpu.PrefetchScalarGridSpec(
            num_scalar_prefetch=2, grid=(MLA_HEADS, int(ii.shape[0])),
            in_specs=[blk(qmap), blk(kmap), blk(kmap), blk(qmap), blk(qmap), blk(qmap)],
            out_specs=blk(qmap),
            scratch_shapes=[pltpu.VMEM((tq, LANES), F32)]),
        out_shape=jax.ShapeDtypeStruct((T, HP), F32),
        compiler_params=_cparams(("parallel", "arbitrary")),
    )(ii, jj, q, k, v, do, lse, delta)


def _flash_dkv(q, k, v, do, lse_row, delta_row):
    T = q.shape[0]
    tq, sr = _flash_tiles(T)
    n = T // tq
    jj, ii = _pairs(n, False)

    def body(jj_ref, ii_ref, q_ref, k_ref, v_ref, do_ref, lse_ref, dl_ref, dk_ref, dv_ref, dk_sc, dv_sc):
        p_ = pl.program_id(1)
        j, i = jj_ref[p_], ii_ref[p_]

        @pl.when(i == j)
        def _():
            dk_sc[...] = jnp.zeros(dk_sc.shape, F32)
            dv_sc[...] = jnp.zeros(dv_sc.shape, F32)

        def tile(diag):
            for rb in range(tq // sr):
                rows = slice(rb * sr, (rb + 1) * sr)
                c0, c1 = _col_span(tq, sr, rb, diag, True)
                qs, dos = q_ref[c0:c1, :], do_ref[c0:c1, :]
                pt = jnp.exp2(_dot(k_ref[rows, :], qs, "nt") - lse_ref[:, c0:c1])
                if diag:
                    r, c = _span_iota(sr, rb, c0, c1)
                    pt = jnp.where(r <= c, pt, 0.0)
                dpt = _dot(v_ref[rows, :], dos, "nt")
                dv_sc[rows, :] += _dot(pt, dos)
                dk_sc[rows, :] += _dot(pt * (dpt - dl_ref[:, c0:c1]), qs)

        @pl.when(i == j)
        def _():
            tile(True)

        @pl.when(i > j)
        def _():
            tile(False)

        @pl.when(i == n - 1)
        def _():
            dk_ref[...] = dk_sc[...] * LN2
            dv_ref[...] = dv_sc[...]

    blk = lambda which: pl.BlockSpec((tq, LANES), which)
    qmap = lambda h, p, jj, ii: (ii[p], h)
    kmap = lambda h, p, jj, ii: (jj[p], h)
    row = pl.BlockSpec((None, 1, tq), lambda h, p, jj, ii: (h, 0, ii[p]))
    return pl.pallas_call(
        body, name="mla_flash_dkv",
        grid_spec=pltpu.PrefetchScalarGridSpec(
            num_scalar_prefetch=2, grid=(MLA_HEADS, int(ii.shape[0])),
            in_specs=[blk(qmap), blk(kmap), blk(kmap), blk(qmap), row, row],
            out_specs=[blk(kmap), blk(kmap)],
            scratch_shapes=[pltpu.VMEM((tq, LANES), F32)] * 2),
        out_shape=[jax.ShapeDtypeStruct((T, HP), F32)] * 2,
        compiler_params=_cparams(("parallel", "arbitrary")),
    )(jj, ii, q, k, v, do, lse_row, delta_row)


def _mem_fwd(z, km, vm, gq):
    scale = MEM_HEAD_DIM ** -0.5

    def fn(qm, km, vm, gq):
        ys = []
        for h in range(MEM_HEADS):
            sl = slice(h * LANES, (h + 1) * LANES)
            q = _rms(qm[:, sl], gq) * scale
            s = _dot(q, km[:, sl], "nt")
            p = jnp.exp(s - jnp.max(s, axis=1, keepdims=True))
            p = p / jnp.sum(p, axis=1, keepdims=True)
            ys.append(_dot(p, vm[:, sl]))
        y = jnp.concatenate(ys, axis=1)
        return y, y

    return _rowwise(fn, "mem_fwd", 512, [(z, MEM_WIDTH, Z_QM // MEM_WIDTH)], [km, vm, gq],
                    [(MEM_WIDTH, BF16), (MEM_WIDTH, BF16, "T")])


def _mem_bwd(z, dy, km, vm, gq):
    scale = MEM_HEAD_DIM ** -0.5

    def fn(qm, dy, km, vm, gq):
        dqs, dks, dvs = [], [], []
        dgq = jnp.zeros((1, LANES), F32)
        for h in range(MEM_HEADS):
            sl = slice(h * LANES, (h + 1) * LANES)
            q = (_rms(qm[:, sl], gq) * scale).astype(BF16)
            dyh = dy[:, sl]
            kh, vh = km[:, sl], vm[:, sl]
            s = _dot(q, kh, "nt")
            p = jnp.exp(s - jnp.max(s, axis=1, keepdims=True))
            p = p / jnp.sum(p, axis=1, keepdims=True)
            dp = _dot(dyh, vh, "nt")
            ds = p * (dp - jnp.sum(p * dp, axis=1, keepdims=True))
            dq = _dot(ds, kh) * scale
            dx, dg = _rms_bwd(qm[:, sl], gq, dq)
            dqs.append(dx)
            dgq = dgq + dg
            st = _dot(kh, q, "nt")
            pt = jnp.exp(st - jnp.max(st, axis=0, keepdims=True))
            pt = pt / jnp.sum(pt, axis=0, keepdims=True)
            dpt = _dot(vh, dyh, "nt")
            dst = pt * (dpt - jnp.sum(pt * dpt, axis=0, keepdims=True))
            dvs.append(_dot(pt, dyh))
            dks.append(_dot(dst, q))
        return jnp.concatenate(dqs, axis=1), jnp.concatenate(dks, axis=1), jnp.concatenate(dvs, axis=1), dgq

    m = km.shape[0]
    return _rowwise(fn, "mem_bwd", 512, [(z, MEM_WIDTH, Z_QM // MEM_WIDTH), dy], [km, vm, gq],
                    [(MEM_WIDTH, BF16)], [((m, MEM_WIDTH), F32), ((m, MEM_WIDTH), F32), ((1, LANES), F32)])


GROUPS = {"ffn1": ["ffn1_w_gu", "ffn1_w_down"],
          "mix": ["w_in", "mla_w_uq", "mla_w_ukv", "mem_w_kv", "w_branch_a", "w_branch_b", "w_branch_c", "w_out"],
          "ffn2": ["ffn2_w_gu", "ffn2_w_down"]}
GRAD_GROUPS = {"ffn2": GROUPS["ffn2"], "mix": GROUPS["mix"], "ffn1_down": ["ffn1_w_down"], "ffn1_gu": ["ffn1_w_gu"]}


def _local_step(x, mem, positions, loss_target, P, weights, grads_out):
    T = x.shape[0]
    G = {}
    W = dict(weights("ffn1", None))

    half = MLA_ROPE // 2
    inv = ROPE_BASE ** (-jnp.arange(half, dtype=F32) / half)
    ang = positions.astype(F32)[:, None] * inv
    cos, sin = jnp.cos(ang), jnp.sin(ang)
    one, zero = jnp.ones((T, MLA_NOPE), F32), jnp.zeros((T, half), F32)
    pad = LANES - MLA_QK
    tabs = (jnp.concatenate([one, cos, cos, jnp.ones((T, pad), F32)], axis=1),
            jnp.concatenate([jnp.zeros((T, MLA_NOPE), F32), -sin, zero, jnp.zeros((T, pad), F32)], axis=1),
            jnp.concatenate([jnp.zeros((T, MLA_NOPE), F32), zero, sin, jnp.zeros((T, pad), F32)], axis=1))
    gq_p = jnp.pad(P["mla_q_norm"], ((0, 0), (0, pad)))
    gk_p = jnp.pad(P["mla_k_norm"], ((0, 0), (0, pad)))
    bias_full = jnp.repeat(P["sg_b"].T, SG_GROUP_DIM, axis=1)
    group_ind = jnp.repeat(jnp.eye(SG_GROUPS, dtype=F32), SG_GROUP_DIM, axis=0)

    HT = (D_MODEL, BF16, "T")

    def norm2(x, g):
        h = _rms(x, g)
        return h, h

    h1, h1t = _rowwise(norm2, "ffn1_norm", 512, [x], [P["ffn1_norm"]], [(D_MODEL, BF16), HT])
    gu1, a1t, o1 = _ffn_fwd(h1, W["ffn1_w_gu"], W["ffn1_w_down"], "ffn1")

    def resid_norm(x, o, g):
        xn = x + 0.5 * o
        h = _rms(xn, g)
        return xn, h, h

    x1, hm, hmt = _rowwise(resid_norm, "mix_norm", 512, [x, o1], [P["mix_norm"]],
                           [(D_MODEL, F32), (D_MODEL, BF16), HT])
    W.update(weights("mix", hm))
    z = _mm(hm, W["w_in"], "nn", BF16, "w_in", tm=1024, tn=768)

    y_a, y_at = _sg_fwd(z, P["sg_ln_g"], P["sg_ln_b"], P["sg_w"], bias_full)

    def c_norm(cq, ckv, gq, gkv):
        a, b = _rms(cq, gq), _rms(ckv, gkv)
        return a, b, a, b

    cqn, ckvn, cqnt, ckvnt = _rowwise(
        c_norm, "mla_cnorm", 512, [(z, MLA_Q_RANK, Z_CQ // MLA_Q_RANK), (z, MLA_KV_RANK, Z_CKV // MLA_KV_RANK)],
        [P["mla_cq_norm"], P["mla_ckv_norm"]],
        [(MLA_Q_RANK, BF16), (MLA_KV_RANK, BF16), (MLA_Q_RANK, BF16, "T"), (MLA_KV_RANK, BF16, "T")])
    q_pre = _mm(cqn, W["mla_w_uq"], "nn", F32, "mla_uq", tm=1024, tn=1024)
    kv_pre = _mm(ckvn, W["mla_w_ukv"], "nn", F32, "mla_ukv", tm=1024, tn=1024)
    q, k, v = _mla_post(q_pre, kv_pre, z, tabs, gq_p, gk_p)
    y_b, y_bt, lse = _flash_fwd(q, k, v)

    memn, = _rowwise(lambda m, g: _rms(m, g), "mem_norm", 256, [mem], [P["mem_norm"]], [(D_MODEL, BF16)])
    kvm = _mm(memn, W["mem_w_kv"], "nn", F32, "mem_kv")

    def mem_k(kvm, gk):
        ks = [_rms(kvm[:, h * LANES:(h + 1) * LANES], gk) for h in range(MEM_HEADS)]
        return jnp.concatenate(ks, axis=1), kvm[:, MEM_WIDTH:]

    km, vm = _rowwise(mem_k, "mem_knorm", 256, [kvm], [P["mem_k_norm"]], [(MEM_WIDTH, BF16), (MEM_WIDTH, BF16)])
    y_c, y_ct = _mem_fwd(z, km, vm, P["mem_q_norm"])

    pa = _mm(y_a, W["w_branch_a"], "nn", BF16, "branch_a", tm=1024, tn=1024)
    pb = _mm(y_b, W["w_branch_b"], "nn", BF16, "branch_b", tm=1024, tn=1024)
    pc = _mm(y_c, W["w_branch_c"], "nn", BF16, "branch_c", tm=1024, tn=1024)

    def merge(zg, pa, pb, pc, b):
        g = _sigmoid(zg + b)
        m = g[:, :D_MODEL] * pa + g[:, D_MODEL:2 * D_MODEL] * pb + g[:, 2 * D_MODEL:] * pc
        return m, m

    merged, mergedt = _rowwise(merge, "merge", 256, [(z, 3 * D_MODEL, 0), pa, pb, pc], [P["b_gate"]],
                               [(D_MODEL, BF16), HT])
    om = _mm(merged, W["w_out"], "nn", F32, "w_out", tm=1024, tn=1024)

    def resid_norm1(x, o, g):
        xn = x + o
        h = _rms(xn, g)
        return xn, h, h

    x2, h2, h2t = _rowwise(resid_norm1, "ffn2_norm", 512, [x1, om], [P["ffn2_norm"]],
                           [(D_MODEL, F32), (D_MODEL, BF16), HT])
    W.update(weights("ffn2", h2))
    gu2, a2t, o2 = _ffn_fwd(h2, W["ffn2_w_gu"], W["ffn2_w_down"], "ffn2")

    def loss_fn(x2, o2, t):
        e = x2 + 0.5 * o2 - t
        return e * (1.0 / D_MODEL), (e * (0.5 / D_MODEL)).astype(BF16), _rsum(e * e) * (0.5 / D_MODEL)

    dx3, do2, loss_part = _rowwise(loss_fn, "loss", 512, [x2, o2, loss_target], [],
                                   [(D_MODEL, F32), (D_MODEL, BF16)], [((1, D_MODEL), F32)])

    dh2, G["ffn2_w_gu"], G["ffn2_w_down"] = _ffn_bwd(do2, h2t, gu2, a2t, W["ffn2_w_gu"], W["ffn2_w_down"], "ffn2")
    tie = grads_out("ffn2", G)

    def norm_bwd(x, dh, dxo, g, *_):
        dx, dg = _rms_bwd(x, g, dh)
        dx = dx + dxo
        return dx, dx, dg

    dx2, dx2b, G["ffn2_norm"] = _rowwise(norm_bwd, "ffn2_norm_bwd", 512, [x2, dh2, dx3],
                                         [P["ffn2_norm"]] + ([] if tie is None else [tie]),
                                         [(D_MODEL, F32), (D_MODEL, BF16)], [((1, D_MODEL), F32)])

    G["w_out"] = _mm_t(mergedt, dx2b, "w_out_dw", tm=1024, tn=1024)
    dmerged = _mm(dx2b, W["w_out"], "nt", F32, "w_out_dx", tm=1024, tn=1024)

    def merge_bwd(zg, pa, pb, pc, dm, b):
        g = _sigmoid(zg + b)
        ps = jnp.concatenate([pa, pb, pc], axis=1)
        dm3 = jnp.concatenate([dm, dm, dm], axis=1)
        dzg = dm3 * ps * g * (1.0 - g)
        dp = dm3 * g
        return dzg, dp[:, :D_MODEL], dp[:, D_MODEL:2 * D_MODEL], dp[:, 2 * D_MODEL:], _rsum(dzg)

    dzg, dpa, dpb, dpc, G["b_gate"] = _rowwise(
        merge_bwd, "merge_bwd", 256, [(z, 3 * D_MODEL, 0), pa, pb, pc, dmerged], [P["b_gate"]],
        [(3 * D_MODEL, BF16), (D_MODEL, BF16), (D_MODEL, BF16), (D_MODEL, BF16)], [((1, 3 * D_MODEL), F32)])

    G["w_branch_a"] = _mm_t(y_at, dpa, "branch_a_dw", tm=512, tn=1024)
    G["w_branch_b"] = _mm_t(y_bt, dpb, "branch_b_dw", tm=1024, tn=1024)
    G["w_branch_c"] = _mm_t(y_ct, dpc, "branch_c_dw", tm=512, tn=1024)
    dy_a = _mm(dpa, W["w_branch_a"], "nt", BF16, "branch_a_dx", tm=1024, tn=512)
    dy_b = _mm(dpb, W["w_branch_b"], "nt", BF16, "branch_b_dx", tm=1024, tn=1024)
    dy_c = _mm(dpc, W["w_branch_c"], "nt", BF16, "branch_c_dx", tm=1024, tn=512)

    du_pre, dv_pre, G["sg_w"], dbias_t, G["sg_ln_g"], G["sg_ln_b"] = _sg_bwd(
        z, dy_a, P["sg_ln_g"], P["sg_ln_b"], P["sg_w"], bias_full, group_ind)
    G["sg_b"] = dbias_t.T

    dqm, dkm, dvm, G["mem_q_norm"] = _mem_bwd(z, dy_c, km, vm, P["mem_q_norm"])

    def mem_k_bwd(kvm, dkm, dvm, gk):
        dks = []
        dg = jnp.zeros((1, LANES), F32)
        for h in range(MEM_HEADS):
            sl = slice(h * LANES, (h + 1) * LANES)
            dx, d = _rms_bwd(kvm[:, sl], gk, dkm[:, sl])
            dks.append(dx)
            dg = dg + d
        return jnp.concatenate(dks + [dvm], axis=1), dg

    dkvm, G["mem_k_norm"] = _rowwise(mem_k_bwd, "mem_knorm_bwd", 256, [kvm, dkm, dvm], [P["mem_k_norm"]],
                                     [(2 * MEM_WIDTH, BF16)], [((1, LANES), F32)])
    G["mem_w_kv"] = _mm(memn, dkvm, "tn", BF16, "mem_kv_dw")
    dmemn = _mm(dkvm, W["mem_w_kv"], "nt", F32, "mem_kv_dx")
    _, G["mem_norm"] = _rowwise(lambda m, d, g: _rms_bwd(m, g, d), "mem_norm_bwd", 256, [mem, dmemn],
                                [P["mem_norm"]], [(D_MODEL, BF16)], [((1, D_MODEL), F32)])

    def delta_fn(o, do):
        od = o.astype(F32) * do.astype(F32)
        ds = [jnp.broadcast_to(jnp.sum(od[:, h * LANES:(h + 1) * LANES], axis=1, keepdims=True), (od.shape[0], LANES))
              for h in range(MLA_HEADS)]
        return jnp.concatenate(ds, axis=1)

    delta, = _rowwise(delta_fn, "mla_delta", 512, [y_b, dy_b], [], [(HP, F32)])
    rowform = lambda a: a.reshape(T, MLA_HEADS, LANES)[:, :, 0].T.reshape(MLA_HEADS, 1, T)
    dq = _flash_dq(q, k, v, dy_b, lse, delta)
    dk, dv = _flash_dkv(q, k, v, dy_b, rowform(lse), rowform(delta))
    dq_pre, dkv_pre, dkr, dgq, dgk = _mla_post_bwd(q_pre, kv_pre, z, tabs, gq_p, gk_p, dq, dk, dv)
    G["mla_q_norm"], G["mla_k_norm"] = dgq[:, :MLA_QK], dgk[:, :MLA_QK]
    G["mla_w_uq"] = _mm_t(cqnt, dq_pre, "mla_uq_dw", tm=384, tn=1024)
    G["mla_w_ukv"] = _mm_t(ckvnt, dkv_pre, "mla_ukv_dw", tm=256, tn=2048)
    dcqn = _mm(dq_pre, W["mla_w_uq"], "nt", F32, "mla_uq_dx", tm=1024)
    dckvn = _mm(dkv_pre, W["mla_w_ukv"], "nt", F32, "mla_ukv_dx", tm=1024)

    def c_norm_bwd(cq, ckv, dcqn, dckvn, gq, gkv):
        dcq, dgq = _rms_bwd(cq, gq, dcqn)
        dckv, dgkv = _rms_bwd(ckv, gkv, dckvn)
        return dcq, dckv, dgq, dgkv

    dcq, dckv, G["mla_cq_norm"], G["mla_ckv_norm"] = _rowwise(
        c_norm_bwd, "mla_cnorm_bwd", 512,
        [(z, MLA_Q_RANK, Z_CQ // MLA_Q_RANK), (z, MLA_KV_RANK, Z_CKV // MLA_KV_RANK), dcqn, dckvn],
        [P["mla_cq_norm"], P["mla_ckv_norm"]], [(MLA_Q_RANK, BF16), (MLA_KV_RANK, BF16)],
        [((1, MLA_Q_RANK), F32), ((1, MLA_KV_RANK), F32)])

    dz = jnp.concatenate([dzg, du_pre, dv_pre, dqm, dckv, dkr, dcq], axis=1)
    G["w_in"] = _mm_t(hmt, dz, "w_in_dw", tm=1024, tn=768)
    dhm = _mm(dz, W["w_in"], "nt", F32, "w_in_dx", tm=1024, tn=1024, tk=2688)

    def norm_bwd_half(x, dh, dxo, g):
        dx, dg = _rms_bwd(x, g, dh)
        dx = dx + dxo
        return dx, (0.5 * dx), dg

    dx1, do1, G["mix_norm"] = _rowwise(norm_bwd_half, "mix_norm_bwd", 512, [x1, dhm, dx2], [P["mix_norm"]],
                                       [(D_MODEL, F32), (D_MODEL, BF16)], [((1, D_MODEL), F32)])
    tie = grads_out("mix", G)

    def ffn1_dw(which, dw):
        G["ffn1_w_" + which] = dw
        return grads_out("ffn1_" + which, G)

    dh1, _, _ = _ffn_bwd(do1, h1t, gu1, a1t, W["ffn1_w_gu"], W["ffn1_w_down"], "ffn1", tie, ffn1_dw)

    def norm_bwd_last(x, dh, dxo, g):
        dx, dg = _rms_bwd(x, g, dh)
        return dx + dxo, dg

    grad_x, G["ffn1_norm"] = _rowwise(norm_bwd_last, "ffn1_norm_bwd", 512, [x, dh1, dx1], [P["ffn1_norm"]],
                                      [(D_MODEL, F32)], [((1, D_MODEL), F32)])
    return loss_part, grad_x, G


SHARDED = ["ffn1_w_gu", "ffn1_w_down", "w_in", "mla_w_uq", "mla_w_ukv", "mem_w_kv",
           "w_branch_a", "w_branch_b", "w_branch_c", "w_out", "ffn2_w_gu", "ffn2_w_down"]
ROW_SHARDED = {"ffn1_w_down", "mem_w_kv", "w_out", "ffn2_w_down"}
SMALL = ["ffn1_norm", "mix_norm", "b_gate", "sg_ln_g", "sg_ln_b", "sg_w", "sg_b", "mla_cq_norm",
         "mla_ckv_norm", "mla_q_norm", "mla_k_norm", "mem_norm", "mem_q_norm", "mem_k_norm", "ffn2_norm"]
ORDER = ["ffn1_norm", "ffn1_w_gu", "ffn1_w_down", "mix_norm", "w_in", "b_gate", "sg_ln_g", "sg_ln_b", "sg_w",
         "sg_b", "mla_cq_norm", "mla_w_uq", "mla_ckv_norm", "mla_w_ukv", "mla_q_norm", "mla_k_norm", "mem_norm",
         "mem_w_kv", "mem_q_norm", "mem_k_norm", "w_branch_a", "w_branch_b", "w_branch_c", "w_out", "ffn2_norm",
         "ffn2_w_gu", "ffn2_w_down"]

_IN_U, _IN_V, _IN_CQ, _IN_CKV, _IN_KR, _IN_QM, _IN_G = 0, 512, 1024, 1408, 1664, 1696, 2208
IN_COLS = 5280


def _full_from_slabs(name, slabs):
    n, r, c = slabs.shape
    if name in ROW_SHARDED:
        return slabs.reshape(n * r, c)
    return slabs.transpose(1, 0, 2).reshape(r, n * c)


def _slabs_from_full(name, full):
    if name in ROW_SHARDED:
        return full.reshape(N_DEV, full.shape[0] // N_DEV, full.shape[1])
    r, c = full.shape
    return full.reshape(r, N_DEV, c // N_DEV).transpose(1, 0, 2)


def _compute_layout(full):
    W = dict(full)
    if "w_in" not in full:
        return W
    w = full["w_in"]
    kr = jnp.pad(w[:, _IN_KR:_IN_QM], ((0, 0), (KR_LANE, LANES - KR_LANE - MLA_ROPE)))
    W["w_in"] = jnp.concatenate([w[:, _IN_G:], w[:, _IN_U:_IN_CQ], w[:, _IN_QM:_IN_G], w[:, _IN_CKV:_IN_KR], kr,
                                 w[:, _IN_CQ:_IN_CKV]], axis=1)
    uq = full["mla_w_uq"].reshape(MLA_Q_RANK, MLA_HEADS, MLA_QK)
    W["mla_w_uq"] = jnp.pad(uq, ((0, 0), (0, 0), (0, LANES - MLA_QK))).reshape(MLA_Q_RANK, HP)
    ukv = full["mla_w_ukv"].reshape(MLA_KV_RANK, MLA_HEADS, MLA_NOPE + MLA_V)
    padh = lambda a: jnp.pad(a, ((0, 0), (0, 0), (0, LANES - a.shape[2]))).reshape(MLA_KV_RANK, HP)
    W["mla_w_ukv"] = jnp.concatenate([padh(ukv[:, :, :MLA_NOPE]), padh(ukv[:, :, MLA_NOPE:])], axis=1)
    wb = full["w_branch_b"].reshape(MLA_HEADS, MLA_V, D_MODEL)
    W["w_branch_b"] = jnp.pad(wb, ((0, 0), (0, LANES - MLA_V), (0, 0))).reshape(HP, D_MODEL)
    return W


def _reference_layout(G):
    out = dict(G)
    if "w_in" not in G:
        return out
    g = G["w_in"]
    out["w_in"] = jnp.concatenate([
        g[:, Z_U:Z_QM], g[:, Z_CQ:Z_COLS], g[:, Z_CKV:Z_KR], g[:, Z_KR + KR_LANE:Z_KR + KR_LANE + MLA_ROPE],
        g[:, Z_QM:Z_CKV], g[:, Z_G:Z_U]], axis=1)
    out["mla_w_uq"] = G["mla_w_uq"].reshape(MLA_Q_RANK, MLA_HEADS, LANES)[:, :, :MLA_QK].reshape(MLA_Q_RANK, -1)
    gk = G["mla_w_ukv"][:, :HP].reshape(MLA_KV_RANK, MLA_HEADS, LANES)[:, :, :MLA_NOPE]
    gv = G["mla_w_ukv"][:, HP:].reshape(MLA_KV_RANK, MLA_HEADS, LANES)[:, :, :MLA_V]
    out["mla_w_ukv"] = jnp.concatenate([gk, gv], axis=2).reshape(MLA_KV_RANK, -1)
    out["w_branch_b"] = G["w_branch_b"].reshape(MLA_HEADS, LANES, D_MODEL)[:, :MLA_V].reshape(-1, D_MODEL)
    return out


def _pack(parts):
    flat = []
    for a in parts:
        a = a.reshape(-1)
        flat.append(jnp.pad(a, (0, (-a.shape[0]) % LANES)))
    return jnp.concatenate(flat).reshape(-1, LANES)


def _unpack(packed, shapes):
    flat = packed.reshape(-1)
    out, off = [], 0
    for shp in shapes:
        n = int(np.prod(shp))
        out.append(flat[off:off + n].reshape(shp))
        off += n + (-n) % LANES
    return out


MESH = pl.DeviceIdType.MESH
HBM = pl.BlockSpec(memory_space=pltpu.HBM)


def _all_gather(shards):
    n = len(shards)

    def body(*refs):
        x_refs, out_refs, token_ref = refs[:n], refs[n:2 * n], refs[2 * n]
        send_sems, recv_sems, local_sems = refs[2 * n + 1:]
        x, y, c = lax.axis_index("x"), lax.axis_index("y"), lax.axis_index("c")
        me, sibling = (x, y, c), (x, y, 1 - c)
        chips = [(1 - x, y), (x, 1 - y), (1 - x, 1 - y)]
        token_ref[...] = jnp.zeros_like(token_ref)

        def slot(a, px, py, pc):
            return out_refs[a].at[4 * px + 2 * py + pc]

        def copy(a, k, block, to, src=None):
            return pltpu.make_async_remote_copy(
                src_ref=slot(a, *block) if src is None else src, dst_ref=slot(a, *block),
                send_sem=send_sems.at[7 * a + k], recv_sem=recv_sems.at[7 * a + k], device_id=to, device_id_type=MESH)

        arrays = range(n)
        mine = [pltpu.make_async_copy(x_refs[a], slot(a, *me), local_sems.at[a]) for a in arrays]
        for cp in mine:
            cp.start()
        first = [copy(a, 0, me, sibling, src=x_refs[a]) for a in arrays]
        first += [copy(a, 1 + j, me, (*chip, c), src=x_refs[a]) for j, chip in enumerate(chips) for a in arrays]
        for cp in first:
            cp.start()
        passed = []
        for j, chip in enumerate(chips):
            for a in arrays:
                copy(a, 1 + j, (*chip, c), me).wait_recv()
                passed.append(copy(a, 4 + j, (*chip, c), sibling))
                passed[-1].start()
        for a in arrays:
            copy(a, 0, sibling, me).wait_recv()
        for j, chip in enumerate(chips):
            for a in arrays:
                copy(a, 4 + j, (*chip, 1 - c), me).wait_recv()
        for cp in first + passed:
            cp.wait_send()
        for cp in mine:
            cp.wait()

    res = pl.pallas_call(
        body, name="all_gather_weights",
        out_shape=[jax.ShapeDtypeStruct((N_DEV,) + s.shape, s.dtype) for s in shards]
        + [jax.ShapeDtypeStruct((8, LANES), F32)],
        in_specs=[HBM] * n, out_specs=[HBM] * n + [pl.BlockSpec(memory_space=pltpu.VMEM)],
        scratch_shapes=[pltpu.SemaphoreType.DMA((7 * n,)), pltpu.SemaphoreType.DMA((7 * n,)),
                        pltpu.SemaphoreType.DMA((n,))],
    )(*shards)
    return res[:n], res[n]


SEM = pl.BlockSpec(memory_space=pltpu.SEMAPHORE)
DATAFLOW = pltpu.SideEffectType.DATAFLOW_SIDE_EFFECTING


def _peers():
    x, y, c = lax.axis_index("x"), lax.axis_index("y"), lax.axis_index("c")
    out = []
    for k in range(1, N_DEV):
        px = 1 - x if k & 4 else x
        py = 1 - y if k & 2 else y
        pc = 1 - c if k & 1 else c
        out.append((k, (px, py, pc), 4 * px + 2 * py + pc))
    return 4 * x + 2 * y + c, out


def _send_start(srcs, per_peer, name):
    n = len(srcs)
    lands = [lax.empty((N_DEV,) + (s.shape[1:] if per_peer else s.shape), s.dtype) for s in srcs]

    def body(*refs):
        src_refs, land_refs, send_sems, recv_sems, token = refs[:n], refs[n:2 * n], refs[2 * n], refs[2 * n + 1], refs[-1]
        me, peers = _peers()
        for a in range(n):
            for k, pid, pflat in peers:
                pltpu.make_async_remote_copy(
                    src_ref=src_refs[a].at[pflat] if per_peer else src_refs[a], dst_ref=land_refs[a].at[me],
                    send_sem=send_sems.at[7 * a + k - 1], recv_sem=recv_sems.at[7 * a + k - 1],
                    device_id=pid, device_id_type=MESH).start()
        token[...] = jnp.zeros_like(token)

    hbm = lambda a: pltpu.with_memory_space_constraint(a, pltpu.HBM)
    res = pl.pallas_call(
        body, name=name,
        out_shape=(pltpu.SemaphoreType.DMA((7 * n,)), pltpu.SemaphoreType.DMA((7 * n,)),
                   *[pltpu.HBM(a.shape, a.dtype) for a in srcs + lands], jax.ShapeDtypeStruct((8, LANES), F32)),
        in_specs=(HBM,) * (2 * n), out_specs=(SEM, SEM) + (HBM,) * (2 * n) + (pl.BlockSpec(memory_space=pltpu.VMEM),),
        input_output_aliases={i: 2 + i for i in range(2 * n)},
        compiler_params=pltpu.CompilerParams(has_side_effects=DATAFLOW),
    )(*[hbm(a) for a in srcs + lands])
    return (res[0], res[1], list(res[2:2 + n]), list(res[2 + n:2 + 2 * n])), res[-1]


def _send_wait(started, after, per_peer, name):
    send_sems, recv_sems, srcs_thru, lands_thru = started
    n = len(srcs_thru)

    def body(*refs):
        src_refs, land_refs, send_sems, recv_sems = refs[:n], refs[n:2 * n], refs[2 * n], refs[2 * n + 1]
        me, peers = _peers()
        for a in range(n):
            for k, pid, pflat in peers:
                copy = pltpu.make_async_remote_copy(
                    src_ref=src_refs[a].at[pflat] if per_peer else src_refs[a], dst_ref=land_refs[a].at[pflat],
                    send_sem=send_sems.at[7 * a + k - 1], recv_sem=recv_sems.at[7 * a + k - 1],
                    device_id=pid, device_id_type=MESH)
                copy.wait_send()
                copy.wait_recv()

    outs = pl.pallas_call(
        body, name=name,
        out_shape=tuple(pltpu.HBM(a.shape, a.dtype) for a in srcs_thru + lands_thru),
        in_specs=(HBM,) * (2 * n) + (SEM, SEM, pl.BlockSpec(memory_space=pl.ANY)), out_specs=(HBM,) * (2 * n),
        input_output_aliases={i: i for i in range(2 * n)},
        compiler_params=pltpu.CompilerParams(has_side_effects=DATAFLOW),
    )(*srcs_thru, *lands_thru, send_sems, recv_sems, after)
    me = 4 * lax.axis_index("x") + 2 * lax.axis_index("y") + lax.axis_index("c")
    landed = []
    for src_out, land in zip(outs[:n], outs[n:]):
        own = lax.dynamic_index_in_dim(src_out, me, 0, keepdims=True) if per_peer else src_out[None]
        landed.append(lax.dynamic_update_slice(land, own, (me,) + (0,) * (land.ndim - 1)))
    return landed


def _share_rows(block, name):
    def body(src_ref, out_ref, send_sems, recv_sems, local_sem):
        me, peers = _peers()
        own = pltpu.make_async_copy(src_ref, out_ref.at[me], local_sem)
        own.start()
        copies = [pltpu.make_async_remote_copy(
            src_ref=src_ref, dst_ref=out_ref.at[me], send_sem=send_sems.at[k - 1], recv_sem=recv_sems.at[k - 1],
            device_id=pid, device_id_type=MESH) for k, pid, _ in peers]
        for cp in copies:
            cp.start()
        for cp in copies:
            cp.wait()
        own.wait()

    return pl.pallas_call(
        body, name=name, out_shape=jax.ShapeDtypeStruct((N_DEV,) + block.shape, block.dtype),
        in_specs=[HBM], out_specs=HBM,
        scratch_shapes=[pltpu.SemaphoreType.DMA((N_DEV - 1,)), pltpu.SemaphoreType.DMA((N_DEV - 1,)),
                        pltpu.SemaphoreType.DMA],
    )(block)


def _sum_slots(recv, name, tr):
    n, rows, lanes = recv.shape
    tr = _tile(rows, tr)

    def body(r_ref, o_ref):
        acc = r_ref[0].astype(F32)
        for i in range(1, n):
            acc = acc + r_ref[i].astype(F32)
        o_ref[...] = acc

    return pl.pallas_call(
        body, name=name, grid=(rows // tr,),
        in_specs=[pl.BlockSpec((n, tr, lanes), lambda i: (0, i, 0))],
        out_specs=pl.BlockSpec((tr, lanes), lambda i: (i, 0)),
        out_shape=jax.ShapeDtypeStruct((rows, lanes), F32),
        compiler_params=_cparams(("parallel",)),
    )(recv)


def _adamw_math(w, g, m, v):
    m = ADAM_B1 * m + (1.0 - ADAM_B1) * g
    v = ADAM_B2 * v + (1.0 - ADAM_B2) * (g * g)
    m_hat = m / (1.0 - ADAM_B1 ** ADAM_STEP)
    v_hat = v / (1.0 - ADAM_B2 ** ADAM_STEP)
    return -ADAM_LR * (m_hat / (jnp.sqrt(v_hat) + ADAM_EPS) + ADAM_WD * w), m, v


def _adamw(w, g, m, v, name, tr=256):
    return _rowwise(_adamw_math, name, tr, [w, g, m, v], [], [(w.shape[1], F32)] * 3)


def _sum_adamw(recv, w, m, v, name):
    n, r, c = recv.shape
    tr = _tile(r, 256)

    def body(r_ref, w_ref, m_ref, v_ref, g_ref, d_ref, nm_ref, nv_ref):
        g = r_ref[0].astype(F32)
        for i in range(1, n):
            g = g + r_ref[i].astype(F32)
        g_ref[...] = g
        d_ref[...], nm_ref[...], nv_ref[...] = _adamw_math(w_ref[...], g, m_ref[...], v_ref[...])

    row = pl.BlockSpec((tr, c), lambda i: (i, 0))
    return pl.pallas_call(
        body, name=name, grid=(r // tr,),
        in_specs=[pl.BlockSpec((n, tr, c), lambda i: (0, i, 0)), row, row, row], out_specs=[row] * 4,
        out_shape=[jax.ShapeDtypeStruct((r, c), F32)] * 4, compiler_params=_cparams(("parallel",)),
    )(recv, w, m, v)


def kernel(x, mem, positions, ffn1_norm, ffn1_w_gu, ffn1_w_down, mix_norm, w_in, b_gate, sg_ln_g, sg_ln_b, sg_w, sg_b, mla_cq_norm, mla_w_uq, mla_ckv_norm, mla_w_ukv, mla_q_norm, mla_k_norm, mem_norm, mem_w_kv, mem_q_norm, mem_k_norm, w_branch_a, w_branch_b, w_branch_c, w_out, ffn2_norm, ffn2_w_gu, ffn2_w_down, loss_target, m_ffn1_norm, m_ffn1_w_gu, m_ffn1_w_down, m_mix_norm, m_w_in, m_b_gate, m_sg_ln_g, m_sg_ln_b, m_sg_w, m_sg_b, m_mla_cq_norm, m_mla_w_uq, m_mla_ckv_norm, m_mla_w_ukv, m_mla_q_norm, m_mla_k_norm, m_mem_norm, m_mem_w_kv, m_mem_q_norm, m_mem_k_norm, m_w_branch_a, m_w_branch_b, m_w_branch_c, m_w_out, m_ffn2_norm, m_ffn2_w_gu, m_ffn2_w_down, v_ffn1_norm, v_ffn1_w_gu, v_ffn1_w_down, v_mix_norm, v_w_in, v_b_gate, v_sg_ln_g, v_sg_ln_b, v_sg_w, v_sg_b, v_mla_cq_norm, v_mla_w_uq, v_mla_ckv_norm, v_mla_w_ukv, v_mla_q_norm, v_mla_k_norm, v_mem_norm, v_mem_w_kv, v_mem_q_norm, v_mem_k_norm, v_w_branch_a, v_w_branch_b, v_w_branch_c, v_w_out, v_ffn2_norm, v_ffn2_w_gu, v_ffn2_w_down):
    given = dict(locals())
    wts = {n: given[n] for n in ORDER}
    mom = {n: given["m_" + n] for n in ORDER}
    var = {n: given["v_" + n] for n in ORDER}

    def shards(group, zero):
        out = [wts[n][0].astype(BF16) for n in GROUPS[group]]
        return [out[0] + zero.astype(BF16)] + out[1:]

    def full_weights(group, slabs):
        return _compute_layout({n: _full_from_slabs(n, s) for n, s in zip(GROUPS[group], slabs)})

    def zero_of(a):
        return jnp.minimum(jnp.abs(a.reshape(-1)[0]), 0)

    gathered_ffn1, token = _all_gather([wts[n][0].astype(BF16) for n in GROUPS["ffn1"]])
    flight = {"mix": _send_start(shards("mix", token[0, 0]), False, "gather_mix_start")[0]}
    recv = {}

    def weights(group, after):
        if group == "ffn1":
            return full_weights(group, gathered_ffn1)
        landed = _send_wait(flight.pop(group), after, False, f"gather_{group}_wait")
        if group == "mix":
            flight["ffn2"] = _send_start(shards("ffn2", zero_of(landed[0])), False, "gather_ffn2_start")[0]
        return full_weights(group, landed)

    small_shapes = [wts[n].shape[1:] for n in SMALL]
    early = SMALL[1:]
    assert SMALL[0] == "ffn1_norm"

    def grads_out(group, G):
        Gr = _reference_layout({n: G[n] for n in GRAD_GROUPS[group]})
        parts = [_slabs_from_full(n, Gr[n]).astype(BF16) for n in GRAD_GROUPS[group]]
        flight["g_" + group], tie = _send_start(parts, True, f"grads_{group}_start")
        if group == "mix":
            small = _pack([G[n].reshape(s) for n, s in zip(early, small_shapes[1:])])
            small = jnp.pad(small, ((0, (-small.shape[0]) % 8), (0, 0)))
            flight["small"], tie = _send_start([small + tie[0, 0]], False, "grads_small_start")
        return tie

    P = {n: wts[n] if wts[n].ndim == 2 else wts[n][0] for n in SMALL}
    loss_part, grad_x, G = _local_step(x[0], mem[0], positions[0], loss_target[0], P, weights, grads_out)

    for group, names in GRAD_GROUPS.items():
        recv.update(zip(names, _send_wait(flight.pop("g_" + group), grad_x, True, f"grads_{group}_wait")))
    early_recv, = _send_wait(flight.pop("small"), grad_x, False, "grads_small_wait")
    last = _share_rows(G["ffn1_norm"].reshape(-1, LANES), "share_ffn1_norm")
    g_small_packed = _sum_slots(jnp.concatenate([last, early_recv], axis=1), "sum_small", 2048)

    grads, delta, new_m, new_v = {}, {}, {}, {}
    for n in SHARDED:
        grads[n], delta[n], new_m[n], new_v[n] = _sum_adamw(recv[n], wts[n][0], mom[n][0], var[n][0], "adamw_" + n)
    grads.update(zip(SMALL, _unpack(g_small_packed, small_shapes)))

    def pack_small(d):
        p = _pack([d[n].reshape(s) for n, s in zip(SMALL, small_shapes)])
        return jnp.pad(p, ((0, (-p.shape[0]) % 8), (0, 0)))

    ds, ms, vs = _adamw(pack_small(wts), g_small_packed, pack_small(mom), pack_small(var), "adamw_small", tr=2048)
    for dst, packed in ((delta, ds), (new_m, ms), (new_v, vs)):
        dst.update(zip(SMALL, _unpack(packed, small_shapes)))

    loss = lax.psum(jnp.sum(loss_part), ("x", "y", "c"))
    lead = lambda d: [d[n].reshape(wts[n].shape) for n in ORDER]
    return (loss, grad_x[None], *lead(grads), *lead(delta), *lead(new_m), *lead(new_v))
```

```python
import functools

import numpy as np
import jax
import jax.numpy as jnp
from jax import lax
from jax.experimental import pallas as pl
from jax.experimental.pallas import tpu as pltpu

F32, BF16 = jnp.float32, jnp.bfloat16

D_MODEL = 1024
SG_GROUPS, SG_GROUP_DIM, SG_WIDTH, CHUNK = 8, 64, 512, 128
MLA_HEADS, MLA_NOPE, MLA_ROPE, MLA_V, MLA_QK = 8, 64, 32, 64, 96
MLA_Q_RANK, MLA_KV_RANK = 384, 256
MEM_HEADS, MEM_HEAD_DIM, MEM_WIDTH = 4, 128, 512
D_FF = 2816
ROPE_BASE = 10000.0
EPS = 1e-6
NEG = -1e30
ADAM_LR, ADAM_B1, ADAM_B2, ADAM_EPS, ADAM_WD, ADAM_STEP = 0.001, 0.9, 0.999, 1e-08, 0.01, 10

N_DEV = 8
LANES = 128
V7X_VMEM_LIMIT = 56 * 1024 * 1024
HP = MLA_HEADS * LANES

Z_G, Z_U, Z_V, Z_QM, Z_CKV, Z_KR, Z_CQ = 0, 3072, 3584, 4096, 4608, 4864, 4992
Z_COLS = 5376
KR_LANE = 64


def _tile(dim, pref):
    if dim <= pref:
        return dim
    for t in range(pref - pref % LANES, LANES - 1, -LANES):
        if dim % t == 0:
            return t
    for t in range(pref - pref % 8, 7, -8):
        if dim % t == 0:
            return t
    return dim


def _cparams(sem):
    return pltpu.CompilerParams(dimension_semantics=sem, vmem_limit_bytes=V7X_VMEM_LIMIT)


_DN = {"nn": ((1,), (0,)), "nt": ((1,), (1,)), "tn": ((0,), (0,))}


def _dot(a, b, mode="nn"):
    return lax.dot_general(a.astype(BF16), b.astype(BF16), (_DN[mode], ((), ())),
                           preferred_element_type=F32)


def _mm(a, b, mode, out_dtype, name, tm=512, tn=512, tk=2048, tie=None):
    if mode == "tn":
        K, M = a.shape
    else:
        M, K = a.shape
    N = b.shape[0] if mode == "nt" else b.shape[1]
    tm, tn, tk = _tile(M, tm), _tile(N, tn), _tile(K, tk)
    nk = K // tk
    if mode == "tn":
        a_spec = pl.BlockSpec((tk, tm), lambda i, j, k: (k, i))
    else:
        a_spec = pl.BlockSpec((tm, tk), lambda i, j, k: (i, k))
    if mode == "nt":
        b_spec = pl.BlockSpec((tn, tk), lambda i, j, k: (j, k))
    else:
        b_spec = pl.BlockSpec((tk, tn), lambda i, j, k: (k, j))

    ties = [] if tie is None else [tie]

    def body(a_ref, b_ref, *rest):
        o_ref, *scratch = rest[len(ties):]
        p = _dot(a_ref[...], b_ref[...], mode)
        if nk == 1:
            o_ref[...] = p.astype(o_ref.dtype)
        else:
            acc_ref, = scratch
            k = pl.program_id(2)

            @pl.when(k == 0)
            def _():
                acc_ref[...] = p

            @pl.when(k > 0)
            def _():
                acc_ref[...] += p

            @pl.when(k == nk - 1)
            def _():
                o_ref[...] = acc_ref[...].astype(o_ref.dtype)

    return pl.pallas_call(
        body, name=name, grid=(M // tm, N // tn, nk),
        in_specs=[a_spec, b_spec] + [pl.BlockSpec(t.shape, lambda i, j, k: (0, 0)) for t in ties],
        out_specs=pl.BlockSpec((tm, tn), lambda i, j, k: (i, j)),
        out_shape=jax.ShapeDtypeStruct((M, N), out_dtype),
        scratch_shapes=[] if nk == 1 else [pltpu.VMEM((tm, tn), F32)],
        compiler_params=_cparams(("parallel", "parallel", "arbitrary")),
    )(a, b, *ties)


def _mm_t(at, b, name, tm, tn, tk=1024, tie=None):
    return _mm(at, b, "nn", BF16, name, tm=tm, tn=tn, tk=tk, tie=tie)


def _rowwise(fn, name, tr, row_ins, bc_ins, row_outs, acc_outs=()):
    norm = [it if isinstance(it, tuple) else (it, it.shape[1], 0) for it in row_ins]
    rows = norm[0][0].shape[0]
    tr = _tile(rows, tr)
    arrays, in_specs = [], []
    for arr, w, cb in norm:
        arrays.append(arr)
        in_specs.append(pl.BlockSpec((tr, w), lambda i, cb=cb: (i, cb)))
    for arr in bc_ins:
        arrays.append(arr)
        in_specs.append(pl.BlockSpec(arr.shape, lambda i, nd=arr.ndim: (0,) * nd))
    out_shape, out_specs = [], []
    transposed = [len(o) == 3 for o in row_outs]
    for (w, dt, *_), t in zip(row_outs, transposed):
        out_shape.append(jax.ShapeDtypeStruct((w, rows) if t else (rows, w), dt))
        out_specs.append(pl.BlockSpec((w, tr), lambda i: (0, i)) if t else pl.BlockSpec((tr, w), lambda i: (i, 0)))
    for shp, dt in acc_outs:
        out_shape.append(jax.ShapeDtypeStruct(shp, dt))
        out_specs.append(pl.BlockSpec(shp, lambda i, nd=len(shp): (0,) * nd))
    n_in, n_row = len(arrays), len(row_outs)

    def body(*refs):
        vals = fn(*[r[...].astype(F32) for r in refs[:n_in]])
        if not isinstance(vals, (tuple, list)):
            vals = (vals,)
        outs = refs[n_in:]
        for r, v, t in zip(outs[:n_row], vals[:n_row], transposed):
            r[...] = v.astype(F32).T.astype(r.dtype) if t else v.astype(r.dtype)
        if acc_outs:
            accs = list(zip(outs[n_row:], vals[n_row:]))
            i = pl.program_id(0)

            @pl.when(i == 0)
            def _():
                for r, v in accs:
                    r[...] = v.astype(r.dtype)

            @pl.when(i > 0)
            def _():
                for r, v in accs:
                    r[...] += v.astype(r.dtype)

    res = pl.pallas_call(
        body, name=name, grid=(rows // tr,), in_specs=in_specs, out_specs=out_specs,
        out_shape=out_shape, compiler_params=_cparams(("arbitrary",)),
    )(*arrays)
    return res


def _rsum(x):
    return jnp.sum(x, axis=0, keepdims=True)


def _rms(x, g, n=None):
    n = x.shape[-1] if n is None else n
    r = lax.rsqrt(jnp.sum(x * x, axis=-1, keepdims=True) * (1.0 / n) + EPS)
    return x * r * g


def _rms_bwd(x, g, dy, n=None):
    n = x.shape[-1] if n is None else n
    r = lax.rsqrt(jnp.sum(x * x, axis=-1, keepdims=True) * (1.0 / n) + EPS)
    xh = x * r
    dxh = dy * g
    dx = r * (dxh - xh * (jnp.sum(dxh * xh, axis=-1, keepdims=True) * (1.0 / n)))
    return dx, _rsum(dy * xh)


def _gelu(x):
    return 0.5 * x * (1.0 + lax.erf(x * 0.7071067811865476))


def _gelu_grad(x):
    return 0.5 * (1.0 + lax.erf(x * 0.7071067811865476)) + x * jnp.exp(-0.5 * x * x) * 0.3989422804014327


def _sigmoid(x):
    return 1.0 / (1.0 + jnp.exp(-x))


FFN_TM, FFN_TN = 512, 1408


def _ffn_gu_act(h, w_gu, tag):
    T = h.shape[0]
    tm, tn = _tile(T, FFN_TM), FFN_TN
    nj = D_FF // tn

    def body(h_ref, wg_ref, wu_ref, gu_ref, a_ref, at_ref):
        g = _dot(h_ref[...], wg_ref[...])
        u = _dot(h_ref[...], wu_ref[...])
        gu_ref[0] = g.astype(BF16)
        gu_ref[1] = u.astype(BF16)
        a = g * _sigmoid(g) * u
        a_ref[...] = a.astype(BF16)
        at_ref[...] = a.T.astype(BF16)

    return pl.pallas_call(
        body, name=f"{tag}_gu_act", grid=(T // tm, nj),
        in_specs=[pl.BlockSpec((tm, D_MODEL), lambda i, j: (i, 0)),
                  pl.BlockSpec((D_MODEL, tn), lambda i, j: (0, j)),
                  pl.BlockSpec((D_MODEL, tn), lambda i, j: (0, j + nj))],
        out_specs=[pl.BlockSpec((2, tm, tn), lambda i, j: (0, i, j)),
                   pl.BlockSpec((tm, tn), lambda i, j: (i, j)),
                   pl.BlockSpec((tn, tm), lambda i, j: (j, i))],
        out_shape=[jax.ShapeDtypeStruct((2, T, D_FF), BF16), jax.ShapeDtypeStruct((T, D_FF), BF16),
                   jax.ShapeDtypeStruct((D_FF, T), BF16)],
        compiler_params=_cparams(("parallel", "parallel")),
    )(h, w_gu, w_gu)


def _ffn_da_actbwd(do, w_down, gu, tag, tie=None):
    T = do.shape[0]
    tm, tn = _tile(T, FFN_TM), FFN_TN
    ties = [] if tie is None else [tie]

    def body(do_ref, wd_ref, gu_ref, *rest):
        dgu_ref = rest[-1]
        da = _dot(do_ref[...], wd_ref[...], "nt")
        g = gu_ref[0].astype(F32)
        u = gu_ref[1].astype(F32)
        s = _sigmoid(g)
        dgu_ref[0] = (da * u * s * (1.0 + g * (1.0 - s))).astype(BF16)
        dgu_ref[1] = (da * g * s).astype(BF16)

    return pl.pallas_call(
        body, name=f"{tag}_da_actbwd", grid=(T // tm, D_FF // tn),
        in_specs=[pl.BlockSpec((tm, D_MODEL), lambda i, j: (i, 0)),
                  pl.BlockSpec((tn, D_MODEL), lambda i, j: (j, 0)),
                  pl.BlockSpec((2, tm, tn), lambda i, j: (0, i, j))]
        + [pl.BlockSpec(t.shape, lambda i, j: (0, 0)) for t in ties],
        out_specs=pl.BlockSpec((2, tm, tn), lambda i, j: (0, i, j)),
        out_shape=jax.ShapeDtypeStruct((2, T, D_FF), BF16),
        compiler_params=_cparams(("parallel", "parallel")),
    )(do, w_down, gu, *ties)


def _ffn_dwgu(ht, dgu, tag, tk=1024):
    T = ht.shape[1]
    tn, tk = FFN_TN, _tile(T, tk)
    nj, nk = D_FF // tn, T // tk

    def body(a_ref, b_ref, o_ref, acc_ref):
        k = pl.program_id(1)
        p = _dot(a_ref[...], b_ref[...])

        @pl.when(k == 0)
        def _():
            acc_ref[...] = p

        @pl.when(k > 0)
        def _():
            acc_ref[...] += p

        @pl.when(k == nk - 1)
        def _():
            o_ref[...] = acc_ref[...].astype(o_ref.dtype)

    return pl.pallas_call(
        body, name=f"{tag}_dwgu", grid=(2 * nj, nk),
        in_specs=[pl.BlockSpec((D_MODEL, tk), lambda n, k: (0, k)),
                  pl.BlockSpec((None, tk, tn), lambda n, k: (n // nj, k, n % nj))],
        out_specs=pl.BlockSpec((D_MODEL, tn), lambda n, k: (0, n)),
        out_shape=jax.ShapeDtypeStruct((D_MODEL, 2 * D_FF), BF16),
        scratch_shapes=[pltpu.VMEM((D_MODEL, tn), F32)],
        compiler_params=_cparams(("parallel", "arbitrary")),
    )(ht, dgu)


def _ffn_dh(dgu, w_gu, tag, tm=1024, tie=None):
    T = dgu.shape[1]
    tm, tk = _tile(T, tm), FFN_TN
    nk = D_FF // tk
    ties = [] if tie is None else [tie]

    def body(a_ref, b_ref, *rest):
        o_ref, acc_ref = rest[len(ties):]
        k = pl.program_id(1)
        p = _dot(a_ref[...], b_ref[...], "nt")

        @pl.when(k == 0)
        def _():
            acc_ref[...] = p

        @pl.when(k > 0)
        def _():
            acc_ref[...] += p

        @pl.when(k == 2 * nk - 1)
        def _():
            o_ref[...] = acc_ref[...]

    return pl.pallas_call(
        body, name=f"{tag}_dh", grid=(T // tm, 2 * nk),
        in_specs=[pl.BlockSpec((None, tm, tk), lambda i, k: (k // nk, i, k % nk)),
                  pl.BlockSpec((D_MODEL, tk), lambda i, k: (0, k))]
        + [pl.BlockSpec(t.shape, lambda i, k: (0, 0)) for t in ties],
        out_specs=pl.BlockSpec((tm, D_MODEL), lambda i, k: (i, 0)),
        out_shape=jax.ShapeDtypeStruct((T, D_MODEL), F32),
        scratch_shapes=[pltpu.VMEM((tm, D_MODEL), F32)],
        compiler_params=_cparams(("parallel", "arbitrary")),
    )(dgu, w_gu, *ties)


def _ffn_fwd(h, w_gu, w_down, tag):
    gu, a, at = _ffn_gu_act(h, w_gu, tag)
    o = _mm(a, w_down, "nn", F32, f"{tag}_down", tm=1024, tn=1024, tk=2816)
    return gu, at, o


def _ffn_bwd(do, ht, gu, at, w_gu, w_down, tag, tie=None, on_dw=None):
    on_dw = on_dw or (lambda which, dw: None)
    dw_down = _mm_t(at, do, f"{tag}_dwdown", tm=1408, tn=1024, tie=tie)
    dgu = _ffn_da_actbwd(do, w_down, gu, tag, tie=on_dw("down", dw_down))
    dw_gu = _ffn_dwgu(ht, dgu, tag)
    dh = _ffn_dh(dgu, w_gu, tag, tie=on_dw("gu", dw_gu))
    return dh, dw_gu, dw_down


def _sg_common(u_pre, v_pre, ln_g, ln_b):
    u = _gelu(u_pre)
    v = _gelu(v_pre)
    mu = jnp.mean(v, axis=-1, keepdims=True)
    vc = v - mu
    rstd = lax.rsqrt(jnp.mean(vc * vc, axis=-1, keepdims=True) + EPS)
    vhat = vc * rstd
    vl = vhat * ln_g + ln_b
    return u, vhat, rstd, vl


def _sg_masked_pairs(w):
    t = lax.broadcasted_iota(jnp.int32, (CHUNK, CHUNK), 0)
    s = lax.broadcasted_iota(jnp.int32, (CHUNK, CHUNK), 1)
    causal = s <= t
    wm = [jnp.where(causal, w[g], 0.0).astype(BF16) for g in range(SG_GROUPS)]
    return [jnp.concatenate([wm[2 * j], wm[2 * j + 1]], axis=0) for j in range(SG_GROUPS // 2)], causal


def _sg_mix(vl, pairs, bias):
    tr = vl.shape[0]
    low = lax.broadcasted_iota(jnp.int32, (CHUNK, LANES), 1) < SG_GROUP_DIM
    vb = vl.astype(BF16)
    rows = []
    for c in range(tr // CHUNK):
        slabs = []
        for j in range(SG_GROUPS // 2):
            slab = vb[c * CHUNK:(c + 1) * CHUNK, j * LANES:(j + 1) * LANES]
            m = _dot(pairs[j], slab)
            slabs.append(jnp.where(low, m[:CHUNK], m[CHUNK:]))
        rows.append(jnp.concatenate(slabs, axis=1) + bias)
    return jnp.concatenate(rows, axis=0)


def _sg_fwd(z, ln_g, ln_b, sg_w, bias_full):
    def fn(u_pre, v_pre, ln_g, ln_b, w, bias):
        u, _, _, vl = _sg_common(u_pre, v_pre, ln_g, ln_b)
        pairs, _ = _sg_masked_pairs(w)
        y = u * _sg_mix(vl, pairs, bias)
        return y, y

    return _rowwise(fn, "sg_fwd", 512, [(z, SG_WIDTH, Z_U // SG_WIDTH), (z, SG_WIDTH, Z_V // SG_WIDTH)],
                    [ln_g, ln_b, sg_w, bias_full], [(SG_WIDTH, BF16), (SG_WIDTH, BF16, "T")])


def _sg_bwd(z, dy, ln_g, ln_b, sg_w, bias_full, group_ind):
    def fn(u_pre, v_pre, dy, ln_g, ln_b, w, bias, ind):
        dy = dy.astype(F32)
        u, vhat, rstd, vl = _sg_common(u_pre, v_pre, ln_g, ln_b)
        pairs, causal = _sg_masked_pairs(w)
        mixed = _sg_mix(vl, pairs, bias)
        du_pre = dy * mixed * _gelu_grad(u_pre)
        dmix = dy * u
        tr = dy.shape[0]
        low = lax.broadcasted_iota(jnp.int32, (CHUNK, LANES), 1) < SG_GROUP_DIM
        vb = vl.astype(BF16)
        dw = [jnp.zeros((CHUNK, CHUNK), F32) for _ in range(SG_GROUPS)]
        dbias = jnp.zeros((CHUNK, SG_WIDTH), F32)
        dvl_rows = []
        for c in range(tr // CHUNK):
            dm_c = dmix[c * CHUNK:(c + 1) * CHUNK]
            dbias = dbias + dm_c
            slabs = []
            for j in range(SG_GROUPS // 2):
                slab = vb[c * CHUNK:(c + 1) * CHUNK, j * LANES:(j + 1) * LANES]
                dm = dm_c[:, j * LANES:(j + 1) * LANES]
                d0 = jnp.where(low, dm, 0.0).astype(BF16)
                d1 = jnp.where(low, 0.0, dm).astype(BF16)
                dw[2 * j] = dw[2 * j] + _dot(d0, slab, "nt")
                dw[2 * j + 1] = dw[2 * j + 1] + _dot(d1, slab, "nt")
                slabs.append(_dot(pairs[j], jnp.concatenate([d0, d1], axis=0), "tn"))
            dvl_rows.append(jnp.concatenate(slabs, axis=1))
        dvl = jnp.concatenate(dvl_rows, axis=0)
        dln_g = _rsum(dvl * vhat)
        dln_b = _rsum(dvl)
        dvh = dvl * ln_g
        dv = rstd * (dvh - jnp.mean(dvh, axis=-1, keepdims=True)
                     - vhat * jnp.mean(dvh * vhat, axis=-1, keepdims=True))
        dv_pre = dv * _gelu_grad(v_pre)
        dw = jnp.stack([jnp.where(causal, d, 0.0) for d in dw], axis=0)
        dbias_t = lax.dot_general(dbias, ind, (((1,), (0,)), ((), ())), precision=lax.Precision.HIGHEST,
                                  preferred_element_type=F32)
        return du_pre, dv_pre, dw, dbias_t, dln_g, dln_b

    return _rowwise(fn, "sg_bwd", 512,
                    [(z, SG_WIDTH, Z_U // SG_WIDTH), (z, SG_WIDTH, Z_V // SG_WIDTH), dy],
                    [ln_g, ln_b, sg_w, bias_full, group_ind],
                    [(SG_WIDTH, BF16), (SG_WIDTH, BF16)],
                    [((SG_GROUPS, CHUNK, CHUNK), F32), ((CHUNK, SG_GROUPS), F32), ((1, SG_WIDTH), F32), ((1, SG_WIDTH), F32)])


def _rope(x, c, s1, s2):
    return x * c + pltpu.roll(x, LANES - MLA_ROPE // 2, 1) * s1 + pltpu.roll(x, MLA_ROPE // 2, 1) * s2


def _rope_t(d, c, s1, s2):
    return d * c + pltpu.roll(d * s1, MLA_ROPE // 2, 1) + pltpu.roll(d * s2, LANES - MLA_ROPE // 2, 1)


def _mla_post(q_pre, kv_pre, z, tabs, gq, gk):
    scale = MLA_QK ** -0.5 * LOG2E

    def fn(q_pre, k_pre, v_pre, kr, c, s1, s2, gq, gk):
        qs, ks = [], []
        for h in range(MLA_HEADS):
            sl = slice(h * LANES, (h + 1) * LANES)
            qs.append(_rope(_rms(q_pre[:, sl], gq, MLA_QK), c, s1, s2) * scale)
            ks.append(_rope(_rms(k_pre[:, sl] + kr, gk, MLA_QK), c, s1, s2))
        lane = lax.broadcasted_iota(jnp.int32, v_pre.shape, 1) & (LANES - 1)
        return jnp.concatenate(qs, axis=1), jnp.concatenate(ks, axis=1), jnp.where(lane == ONES_LANE, 1.0, v_pre)

    return _rowwise(fn, "mla_post", 256,
                    [q_pre, (kv_pre, HP, 0), (kv_pre, HP, 1), (z, LANES, Z_KR // LANES), *tabs],
                    [gq, gk], [(HP, BF16)] * 3)


def _mla_post_bwd(q_pre, kv_pre, z, tabs, gq, gk, dq, dk, dv):
    scale = MLA_QK ** -0.5

    def fn(q_pre, k_pre, kr, c, s1, s2, dq, dk, dv, gq, gk):
        lane = lax.broadcasted_iota(jnp.int32, (1, LANES), 1)
        kr_mask = (lane >= KR_LANE) & (lane < KR_LANE + MLA_ROPE)
        dqs, dks = [], []
        dgq = jnp.zeros((1, LANES), F32)
        dgk = jnp.zeros((1, LANES), F32)
        dkr = jnp.zeros(kr.shape, F32)
        for h in range(MLA_HEADS):
            sl = slice(h * LANES, (h + 1) * LANES)
            dqn = _rope_t(dq[:, sl].astype(F32), c, s1, s2) * scale
            dx, dg = _rms_bwd(q_pre[:, sl], gq, dqn, MLA_QK)
            dqs.append(dx)
            dgq = dgq + dg
            dkn = _rope_t(dk[:, sl].astype(F32), c, s1, s2)
            dx, dg = _rms_bwd(k_pre[:, sl] + kr, gk, dkn, MLA_QK)
            dks.append(dx)
            dgk = dgk + dg
            dkr = dkr + dx
        dkr = jnp.where(kr_mask, dkr, 0.0)
        dkv = jnp.concatenate(dks + [dv.astype(F32)], axis=1)
        return jnp.concatenate(dqs, axis=1), dkv, dkr, dgq, dgk

    return _rowwise(fn, "mla_post_bwd", 256,
                    [q_pre, (kv_pre, HP, 0), (z, LANES, Z_KR // LANES), *tabs, dq, dk, dv],
                    [gq, gk], [(HP, BF16), (2 * HP, BF16), (LANES, BF16)],
                    [((1, LANES), F32), ((1, LANES), F32)])


def _pairs(n, lower):
    a, b = [], []
    for o in range(n):
        inner = range(o + 1) if lower else range(o, n)
        for t in inner:
            a.append(o)
            b.append(t)
    return jnp.asarray(np.array(a, np.int32)), jnp.asarray(np.array(b, np.int32))


FLASH_TILE, FLASH_SUB_ROWS = 1024, 512
LOG2E, LN2 = 1.4426950408889634, 0.6931471805599453
ONES_LANE = MLA_V


def _flash_tiles(T):
    tq = _tile(T, FLASH_TILE)
    return tq, _tile(tq, FLASH_SUB_ROWS)


def _col_span(t, sr, rb, diag, key_major):
    if not diag:
        return 0, t
    return (rb * sr, t) if key_major else (0, (rb + 1) * sr)


def _span_iota(sr, rb, c0, c1):
    r = lax.broadcasted_iota(jnp.int32, (sr, c1 - c0), 0) + rb * sr
    c = lax.broadcasted_iota(jnp.int32, (sr, c1 - c0), 1) + c0
    return r, c


def _lanes(x, width):
    return jnp.concatenate([x] * (width // LANES), axis=1)


def _flash_fwd(q, k, v):
    T = q.shape[0]
    tq, sr = _flash_tiles(T)
    n = T // tq
    ii, jj = _pairs(n, True)

    def body(ii_ref, jj_ref, q_ref, k_ref, v_ref, o_ref, ot_ref, lse_ref, m_sc, acc_sc):
        p_ = pl.program_id(1)
        i, j = ii_ref[p_], jj_ref[p_]

        @pl.when(j == 0)
        def _():
            m_sc[...] = jnp.full(m_sc.shape, NEG, F32)
            acc_sc[...] = jnp.zeros(acc_sc.shape, F32)

        def tile(diag):
            for rb in range(tq // sr):
                rows = slice(rb * sr, (rb + 1) * sr)
                c0, c1 = _col_span(tq, sr, rb, diag, False)
                s = _dot(q_ref[rows, :], k_ref[c0:c1, :], "nt")
                if diag:
                    r, c = _span_iota(sr, rb, c0, c1)
                    s = jnp.where(c <= r, s, NEG)
                m = m_sc[rows, :]
                m_new = jnp.maximum(m, jnp.max(s, axis=1, keepdims=True))
                p = jnp.exp2(s - _lanes(m_new, c1 - c0))
                acc_sc[rows, :] = jnp.exp2(m - m_new) * acc_sc[rows, :] + _dot(p, v_ref[c0:c1, :])
                m_sc[rows, :] = m_new

        @pl.when(j < i)
        def _():
            tile(False)

        @pl.when(j == i)
        def _():
            tile(True)
            acc = acc_sc[...]
            lane = lax.broadcasted_iota(jnp.int32, acc.shape, 1)
            l = jnp.sum(jnp.where(lane == ONES_LANE, acc, 0.0), axis=1, keepdims=True)
            o = jnp.where(lane < MLA_V, acc / l, 0.0)
            o_ref[...] = o.astype(o_ref.dtype)
            ot_ref[...] = o.T.astype(ot_ref.dtype)
            lse_ref[...] = m_sc[...] + jnp.log2(l)

    blk = lambda which: pl.BlockSpec((tq, LANES), which)
    qmap = lambda h, p, ii, jj: (ii[p], h)
    kmap = lambda h, p, ii, jj: (jj[p], h)
    return pl.pallas_call(
        body, name="mla_flash_fwd",
        grid_spec=pltpu.PrefetchScalarGridSpec(
            num_scalar_prefetch=2, grid=(MLA_HEADS, int(ii.shape[0])),
            in_specs=[blk(qmap), blk(kmap), blk(kmap)],
            out_specs=[blk(qmap), pl.BlockSpec((LANES, tq), lambda h, p, ii, jj: (h, ii[p])), blk(qmap)],
            scratch_shapes=[pltpu.VMEM((tq, LANES), F32)] * 2),
        out_shape=[jax.ShapeDtypeStruct((T, HP), BF16), jax.ShapeDtypeStruct((HP, T), BF16),
                   jax.ShapeDtypeStruct((T, HP), F32)],
        compiler_params=_cparams(("parallel", "arbitrary")),
    )(ii, jj, q, k, v)


def _flash_dq(q, k, v, do, lse, delta):
    T = q.shape[0]
    tq, sr = _flash_tiles(T)
    n = T // tq
    ii, jj = _pairs(n, True)

    def body(ii_ref, jj_ref, q_ref, k_ref, v_ref, do_ref, lse_ref, dl_ref, dq_ref, acc_sc):
        p_ = pl.program_id(1)
        i, j = ii_ref[p_], jj_ref[p_]

        @pl.when(j == 0)
        def _():
            acc_sc[...] = jnp.zeros(acc_sc.shape, F32)

        def tile(diag):
            for rb in range(tq // sr):
                rows = slice(rb * sr, (rb + 1) * sr)
                c0, c1 = _col_span(tq, sr, rb, diag, False)
                ks = k_ref[c0:c1, :]
                p = jnp.exp2(_dot(q_ref[rows, :], ks, "nt") - _lanes(lse_ref[rows, :], c1 - c0))
                if diag:
                    r, c = _span_iota(sr, rb, c0, c1)
                    p = jnp.where(c <= r, p, 0.0)
                dp = _dot(do_ref[rows, :], v_ref[c0:c1, :], "nt")
                acc_sc[rows, :] += _dot(p * (dp - _lanes(dl_ref[rows, :], c1 - c0)), ks)

        @pl.when(j < i)
        def _():
            tile(False)

        @pl.when(j == i)
        def _():
            tile(True)
            dq_ref[...] = acc_sc[...]

    blk = lambda which: pl.BlockSpec((tq, LANES), which)
    qmap = lambda h, p, ii, jj: (ii[p], h)
    kmap = lambda h, p, ii, jj: (jj[p], h)
    return pl.pallas_call(
        body, name="mla_flash_dq",
        grid_spec=pltpu.PrefetchScalarGridSpec(
            num_scalar_prefetch=2, grid=(MLA_HEADS, int(ii.shape[0])),
            in_specs=[blk(qmap), blk(kmap), blk(kmap), blk(qmap), blk(qmap), blk(qmap)],
            out_specs=blk(qmap),
            scratch_shapes=[pltpu.VMEM((tq, LANES), F32)]),
        out_shape=jax.ShapeDtypeStruct((T, HP), F32),
        compiler_params=_cparams(("parallel", "arbitrary")),
    )(ii, jj, q, k, v, do, lse, delta)


def _flash_dkv(q, k, v, do, lse_row, delta_row):
    T = q.shape[0]
    tq, sr = _flash_tiles(T)
    n = T // tq
    jj, ii = _pairs(n, False)

    def body(jj_ref, ii_ref, q_ref, k_ref, v_ref, do_ref, lse_ref, dl_ref, dk_ref, dv_ref, dk_sc, dv_sc):
        p_ = pl.program_id(1)
        j, i = jj_ref[p_], ii_ref[p_]

        @pl.when(i == j)
        def _():
            dk_sc[...] = jnp.zeros(dk_sc.shape, F32)
            dv_sc[...] = jnp.zeros(dv_sc.shape, F32)

        def tile(diag):
            for rb in range(tq // sr):
                rows = slice(rb * sr, (rb + 1) * sr)
                c0, c1 = _col_span(tq, sr, rb, diag, True)
                qs, dos = q_ref[c0:c1, :], do_ref[c0:c1, :]
                pt = jnp.exp2(_dot(k_ref[rows, :], qs, "nt") - lse_ref[:, c0:c1])
                if diag:
                    r, c = _span_iota(sr, rb, c0, c1)
                    pt = jnp.where(r <= c, pt, 0.0)
                dpt = _dot(v_ref[rows, :], dos, "nt")
                dv_sc[rows, :] += _dot(pt, dos)
                dk_sc[rows, :] += _dot(pt * (dpt - dl_ref[:, c0:c1]), qs)

        @pl.when(i == j)
        def _():
            tile(True)

        @pl.when(i > j)
        def _():
            tile(False)

        @pl.when(i == n - 1)
        def _():
            dk_ref[...] = dk_sc[...] * LN2
            dv_ref[...] = dv_sc[...]

    blk = lambda which: pl.BlockSpec((tq, LANES), which)
    qmap = lambda h, p, jj, ii: (ii[p], h)
    kmap = lambda h, p, jj, ii: (jj[p], h)
    row = pl.BlockSpec((None, 1, tq), lambda h, p, jj, ii: (h, 0, ii[p]))
    return pl.pallas_call(
        body, name="mla_flash_dkv",
        grid_spec=pltpu.PrefetchScalarGridSpec(
            num_scalar_prefetch=2, grid=(MLA_HEADS, int(ii.shape[0])),
            in_specs=[blk(qmap), blk(kmap), blk(kmap), blk(qmap), row, row],
            out_specs=[blk(kmap), blk(kmap)],
            scratch_shapes=[pltpu.VMEM((tq, LANES), F32)] * 2),
        out_shape=[jax.ShapeDtypeStruct((T, HP), F32)] * 2,
        compiler_params=_cparams(("parallel", "arbitrary")),
    )(jj, ii, q, k, v, do, lse_row, delta_row)


def _mem_fwd(z, km, vm, gq):
    scale = MEM_HEAD_DIM ** -0.5

    def fn(qm, km, vm, gq):
        ys = []
        for h in range(MEM_HEADS):
            sl = slice(h * LANES, (h + 1) * LANES)
            q = _rms(qm[:, sl], gq) * scale
            s = _dot(q, km[:, sl], "nt")
            p = jnp.exp(s - jnp.max(s, axis=1, keepdims=True))
            p = p / jnp.sum(p, axis=1, keepdims=True)
            ys.append(_dot(p, vm[:, sl]))
        y = jnp.concatenate(ys, axis=1)
        return y, y

    return _rowwise(fn, "mem_fwd", 512, [(z, MEM_WIDTH, Z_QM // MEM_WIDTH)], [km, vm, gq],
                    [(MEM_WIDTH, BF16), (MEM_WIDTH, BF16, "T")])


def _mem_bwd(z, dy, km, vm, gq):
    scale = MEM_HEAD_DIM ** -0.5

    def fn(qm, dy, km, vm, gq):
        dqs, dks, dvs = [], [], []
        dgq = jnp.zeros((1, LANES), F32)
        for h in range(MEM_HEADS):
            sl = slice(h * LANES, (h + 1) * LANES)
            q = (_rms(qm[:, sl], gq) * scale).astype(BF16)
            dyh = dy[:, sl]
            kh, vh = km[:, sl], vm[:, sl]
            s = _dot(q, kh, "nt")
            p = jnp.exp(s - jnp.max(s, axis=1, keepdims=True))
            p = p / jnp.sum(p, axis=1, keepdims=True)
            dp = _dot(dyh, vh, "nt")
            ds = p * (dp - jnp.sum(p * dp, axis=1, keepdims=True))
            dq = _dot(ds, kh) * scale
            dx, dg = _rms_bwd(qm[:, sl], gq, dq)
            dqs.append(dx)
            dgq = dgq + dg
            st = _dot(kh, q, "nt")
            pt = jnp.exp(st - jnp.max(st, axis=0, keepdims=True))
            pt = pt / jnp.sum(pt, axis=0, keepdims=True)
            dpt = _dot(vh, dyh, "nt")
            dst = pt * (dpt - jnp.sum(pt * dpt, axis=0, keepdims=True))
            dvs.append(_dot(pt, dyh))
            dks.append(_dot(dst, q))
        return jnp.concatenate(dqs, axis=1), jnp.concatenate(dks, axis=1), jnp.concatenate(dvs, axis=1), dgq

    m = km.shape[0]
    return _rowwise(fn, "mem_bwd", 512, [(z, MEM_WIDTH, Z_QM // MEM_WIDTH), dy], [km, vm, gq],
                    [(MEM_WIDTH, BF16)], [((m, MEM_WIDTH), F32), ((m, MEM_WIDTH), F32), ((1, LANES), F32)])


GROUPS = {"ffn1": ["ffn1_w_gu", "ffn1_w_down"],
          "mix": ["w_in", "mla_w_uq", "mla_w_ukv", "mem_w_kv", "w_branch_a", "w_branch_b", "w_branch_c", "w_out"],
          "ffn2": ["ffn2_w_gu", "ffn2_w_down"]}
GRAD_GROUPS = {"ffn2": GROUPS["ffn2"], "mix": GROUPS["mix"], "ffn1_down": ["ffn1_w_down"], "ffn1_gu": ["ffn1_w_gu"]}


def _local_step(x, mem, positions, loss_target, P, weights, grads_out):
    T = x.shape[0]
    G = {}
    W = dict(weights("ffn1", None))

    half = MLA_ROPE // 2
    inv = ROPE_BASE ** (-jnp.arange(half, dtype=F32) / half)
    ang = positions.astype(F32)[:, None] * inv
    cos, sin = jnp.cos(ang), jnp.sin(ang)
    one, zero = jnp.ones((T, MLA_NOPE), F32), jnp.zeros((T, half), F32)
    pad = LANES - MLA_QK
    tabs = (jnp.concatenate([one, cos, cos, jnp.ones((T, pad), F32)], axis=1),
            jnp.concatenate([jnp.zeros((T, MLA_NOPE), F32), -sin, zero, jnp.zeros((T, pad), F32)], axis=1),
            jnp.concatenate([jnp.zeros((T, MLA_NOPE), F32), zero, sin, jnp.zeros((T, pad), F32)], axis=1))
    gq_p = jnp.pad(P["mla_q_norm"], ((0, 0), (0, pad)))
    gk_p = jnp.pad(P["mla_k_norm"], ((0, 0), (0, pad)))
    bias_full = jnp.repeat(P["sg_b"].T, SG_GROUP_DIM, axis=1)
    group_ind = jnp.repeat(jnp.eye(SG_GROUPS, dtype=F32), SG_GROUP_DIM, axis=0)

    HT = (D_MODEL, BF16, "T")

    def norm2(x, g):
        h = _rms(x, g)
        return h, h

    h1, h1t = _rowwise(norm2, "ffn1_norm", 512, [x], [P["ffn1_norm"]], [(D_MODEL, BF16), HT])
    gu1, a1t, o1 = _ffn_fwd(h1, W["ffn1_w_gu"], W["ffn1_w_down"], "ffn1")

    def resid_norm(x, o, g):
        xn = x + 0.5 * o
        h = _rms(xn, g)
        return xn, h, h

    x1, hm, hmt = _rowwise(resid_norm, "mix_norm", 512, [x, o1], [P["mix_norm"]],
                           [(D_MODEL, F32), (D_MODEL, BF16), HT])
    W.update(weights("mix", hm))
    z = _mm(hm, W["w_in"], "nn", BF16, "w_in", tm=1024, tn=1792)

    y_a, y_at = _sg_fwd(z, P["sg_ln_g"], P["sg_ln_b"], P["sg_w"], bias_full)

    def c_norm(cq, ckv, gq, gkv):
        a, b = _rms(cq, gq), _rms(ckv, gkv)
        return a, b, a, b

    cqn, ckvn, cqnt, ckvnt = _rowwise(
        c_norm, "mla_cnorm", 512, [(z, MLA_Q_RANK, Z_CQ // MLA_Q_RANK), (z, MLA_KV_RANK, Z_CKV // MLA_KV_RANK)],
        [P["mla_cq_norm"], P["mla_ckv_norm"]],
        [(MLA_Q_RANK, BF16), (MLA_KV_RANK, BF16), (MLA_Q_RANK, BF16, "T"), (MLA_KV_RANK, BF16, "T")])
    q_pre = _mm(cqn, W["mla_w_uq"], "nn", F32, "mla_uq", tm=1024, tn=1024)
    kv_pre = _mm(ckvn, W["mla_w_ukv"], "nn", F32, "mla_ukv", tm=1024, tn=1024)
    q, k, v = _mla_post(q_pre, kv_pre, z, tabs, gq_p, gk_p)
    y_b, y_bt, lse = _flash_fwd(q, k, v)

    memn, = _rowwise(lambda m, g: _rms(m, g), "mem_norm", 256, [mem], [P["mem_norm"]], [(D_MODEL, BF16)])
    kvm = _mm(memn, W["mem_w_kv"], "nn", F32, "mem_kv")

    def mem_k(kvm, gk):
        ks = [_rms(kvm[:, h * LANES:(h + 1) * LANES], gk) for h in range(MEM_HEADS)]
        return jnp.concatenate(ks, axis=1), kvm[:, MEM_WIDTH:]

    km, vm = _rowwise(mem_k, "mem_knorm", 256, [kvm], [P["mem_k_norm"]], [(MEM_WIDTH, BF16), (MEM_WIDTH, BF16)])
    y_c, y_ct = _mem_fwd(z, km, vm, P["mem_q_norm"])

    pa = _mm(y_a, W["w_branch_a"], "nn", BF16, "branch_a", tm=1024, tn=1024)
    pb = _mm(y_b, W["w_branch_b"], "nn", BF16, "branch_b", tm=1024, tn=1024)
    pc = _mm(y_c, W["w_branch_c"], "nn", BF16, "branch_c", tm=1024, tn=1024)

    def merge(zg, pa, pb, pc, b):
        g = _sigmoid(zg + b)
        m = g[:, :D_MODEL] * pa + g[:, D_MODEL:2 * D_MODEL] * pb + g[:, 2 * D_MODEL:] * pc
        return m, m

    merged, mergedt = _rowwise(merge, "merge", 256, [(z, 3 * D_MODEL, 0), pa, pb, pc], [P["b_gate"]],
                               [(D_MODEL, BF16), HT])
    om = _mm(merged, W["w_out"], "nn", F32, "w_out", tm=1024, tn=1024)

    def resid_norm1(x, o, g):
        xn = x + o
        h = _rms(xn, g)
        return xn, h, h

    x2, h2, h2t = _rowwise(resid_norm1, "ffn2_norm", 512, [x1, om], [P["ffn2_norm"]],
                           [(D_MODEL, F32), (D_MODEL, BF16), HT])
    W.update(weights("ffn2", h2))
    gu2, a2t, o2 = _ffn_fwd(h2, W["ffn2_w_gu"], W["ffn2_w_down"], "ffn2")

    def loss_fn(x2, o2, t):
        e = x2 + 0.5 * o2 - t
        return e * (1.0 / D_MODEL), (e * (0.5 / D_MODEL)).astype(BF16), _rsum(e * e) * (0.5 / D_MODEL)

    dx3, do2, loss_part = _rowwise(loss_fn, "loss", 512, [x2, o2, loss_target], [],
                                   [(D_MODEL, F32), (D_MODEL, BF16)], [((1, D_MODEL), F32)])

    dh2, G["ffn2_w_gu"], G["ffn2_w_down"] = _ffn_bwd(do2, h2t, gu2, a2t, W["ffn2_w_gu"], W["ffn2_w_down"], "ffn2")
    tie = grads_out("ffn2", G)

    def norm_bwd(x, dh, dxo, g, *_):
        dx, dg = _rms_bwd(x, g, dh)
        dx = dx + dxo
        return dx, dx, dg

    dx2, dx2b, G["ffn2_norm"] = _rowwise(norm_bwd, "ffn2_norm_bwd", 512, [x2, dh2, dx3],
                                         [P["ffn2_norm"]] + ([] if tie is None else [tie]),
                                         [(D_MODEL, F32), (D_MODEL, BF16)], [((1, D_MODEL), F32)])

    G["w_out"] = _mm_t(mergedt, dx2b, "w_out_dw", tm=1024, tn=1024)
    dmerged = _mm(dx2b, W["w_out"], "nt", F32, "w_out_dx", tm=1024, tn=1024)

    def merge_bwd(zg, pa, pb, pc, dm, b):
        g = _sigmoid(zg + b)
        ps = jnp.concatenate([pa, pb, pc], axis=1)
        dm3 = jnp.concatenate([dm, dm, dm], axis=1)
        dzg = dm3 * ps * g * (1.0 - g)
        dp = dm3 * g
        return dzg, dp[:, :D_MODEL], dp[:, D_MODEL:2 * D_MODEL], dp[:, 2 * D_MODEL:], _rsum(dzg)

    dzg, dpa, dpb, dpc, G["b_gate"] = _rowwise(
        merge_bwd, "merge_bwd", 256, [(z, 3 * D_MODEL, 0), pa, pb, pc, dmerged], [P["b_gate"]],
        [(3 * D_MODEL, BF16), (D_MODEL, BF16), (D_MODEL, BF16), (D_MODEL, BF16)], [((1, 3 * D_MODEL), F32)])

    G["w_branch_a"] = _mm_t(y_at, dpa, "branch_a_dw", tm=512, tn=1024)
    G["w_branch_b"] = _mm_t(y_bt, dpb, "branch_b_dw", tm=1024, tn=1024)
    G["w_branch_c"] = _mm_t(y_ct, dpc, "branch_c_dw", tm=512, tn=1024)
    dy_a = _mm(dpa, W["w_branch_a"], "nt", BF16, "branch_a_dx", tm=1024, tn=512)
    dy_b = _mm(dpb, W["w_branch_b"], "nt", BF16, "branch_b_dx", tm=1024, tn=1024)
    dy_c = _mm(dpc, W["w_branch_c"], "nt", BF16, "branch_c_dx", tm=1024, tn=512)

    du_pre, dv_pre, G["sg_w"], dbias_t, G["sg_ln_g"], G["sg_ln_b"] = _sg_bwd(
        z, dy_a, P["sg_ln_g"], P["sg_ln_b"], P["sg_w"], bias_full, group_ind)
    G["sg_b"] = dbias_t.T

    dqm, dkm, dvm, G["mem_q_norm"] = _mem_bwd(z, dy_c, km, vm, P["mem_q_norm"])

    def mem_k_bwd(kvm, dkm, dvm, gk):
        dks = []
        dg = jnp.zeros((1, LANES), F32)
        for h in range(MEM_HEADS):
            sl = slice(h * LANES, (h + 1) * LANES)
            dx, d = _rms_bwd(kvm[:, sl], gk, dkm[:, sl])
            dks.append(dx)
            dg = dg + d
        return jnp.concatenate(dks + [dvm], axis=1), dg

    dkvm, G["mem_k_norm"] = _rowwise(mem_k_bwd, "mem_knorm_bwd", 256, [kvm, dkm, dvm], [P["mem_k_norm"]],
                                     [(2 * MEM_WIDTH, BF16)], [((1, LANES), F32)])
    G["mem_w_kv"] = _mm(memn, dkvm, "tn", BF16, "mem_kv_dw")
    dmemn = _mm(dkvm, W["mem_w_kv"], "nt", F32, "mem_kv_dx")
    _, G["mem_norm"] = _rowwise(lambda m, d, g: _rms_bwd(m, g, d), "mem_norm_bwd", 256, [mem, dmemn],
                                [P["mem_norm"]], [(D_MODEL, BF16)], [((1, D_MODEL), F32)])

    def delta_fn(o, do):
        od = o.astype(F32) * do.astype(F32)
        ds = [jnp.broadcast_to(jnp.sum(od[:, h * LANES:(h + 1) * LANES], axis=1, keepdims=True), (od.shape[0], LANES))
              for h in range(MLA_HEADS)]
        return jnp.concatenate(ds, axis=1)

    delta, = _rowwise(delta_fn, "mla_delta", 512, [y_b, dy_b], [], [(HP, F32)])
    rowform = lambda a: a.reshape(T, MLA_HEADS, LANES)[:, :, 0].T.reshape(MLA_HEADS, 1, T)
    dq = _flash_dq(q, k, v, dy_b, lse, delta)
    dk, dv = _flash_dkv(q, k, v, dy_b, rowform(lse), rowform(delta))
    dq_pre, dkv_pre, dkr, dgq, dgk = _mla_post_bwd(q_pre, kv_pre, z, tabs, gq_p, gk_p, dq, dk, dv)
    G["mla_q_norm"], G["mla_k_norm"] = dgq[:, :MLA_QK], dgk[:, :MLA_QK]
    G["mla_w_uq"] = _mm_t(cqnt, dq_pre, "mla_uq_dw", tm=384, tn=1024)
    G["mla_w_ukv"] = _mm_t(ckvnt, dkv_pre, "mla_ukv_dw", tm=256, tn=2048)
    dcqn = _mm(dq_pre, W["mla_w_uq"], "nt", F32, "mla_uq_dx", tm=1024)
    dckvn = _mm(dkv_pre, W["mla_w_ukv"], "nt", F32, "mla_ukv_dx", tm=1024)

    def c_norm_bwd(cq, ckv, dcqn, dckvn, gq, gkv):
        dcq, dgq = _rms_bwd(cq, gq, dcqn)
        dckv, dgkv = _rms_bwd(ckv, gkv, dckvn)
        return dcq, dckv, dgq, dgkv

    dcq, dckv, G["mla_cq_norm"], G["mla_ckv_norm"] = _rowwise(
        c_norm_bwd, "mla_cnorm_bwd", 512,
        [(z, MLA_Q_RANK, Z_CQ // MLA_Q_RANK), (z, MLA_KV_RANK, Z_CKV // MLA_KV_RANK), dcqn, dckvn],
        [P["mla_cq_norm"], P["mla_ckv_norm"]], [(MLA_Q_RANK, BF16), (MLA_KV_RANK, BF16)],
        [((1, MLA_Q_RANK), F32), ((1, MLA_KV_RANK), F32)])

    dz = jnp.concatenate([dzg, du_pre, dv_pre, dqm, dckv, dkr, dcq], axis=1)
    G["w_in"] = _mm_t(hmt, dz, "w_in_dw", tm=1024, tn=1792)
    dhm = _mm(dz, W["w_in"], "nt", F32, "w_in_dx", tm=1024, tn=1024, tk=2688)

    def norm_bwd_half(x, dh, dxo, g):
        dx, dg = _rms_bwd(x, g, dh)
        dx = dx + dxo
        return dx, (0.5 * dx), dg

    dx1, do1, G["mix_norm"] = _rowwise(norm_bwd_half, "mix_norm_bwd", 512, [x1, dhm, dx2], [P["mix_norm"]],
                                       [(D_MODEL, F32), (D_MODEL, BF16)], [((1, D_MODEL), F32)])
    tie = grads_out("mix", G)

    def ffn1_dw(which, dw):
        G["ffn1_w_" + which] = dw
        return grads_out("ffn1_" + which, G)

    dh1, _, _ = _ffn_bwd(do1, h1t, gu1, a1t, W["ffn1_w_gu"], W["ffn1_w_down"], "ffn1", tie, ffn1_dw)

    def norm_bwd_last(x, dh, dxo, g):
        dx, dg = _rms_bwd(x, g, dh)
        return dx + dxo, dg

    grad_x, G["ffn1_norm"] = _rowwise(norm_bwd_last, "ffn1_norm_bwd", 512, [x, dh1, dx1], [P["ffn1_norm"]],
                                      [(D_MODEL, F32)], [((1, D_MODEL), F32)])
    return loss_part, grad_x, G


SHARDED = ["ffn1_w_gu", "ffn1_w_down", "w_in", "mla_w_uq", "mla_w_ukv", "mem_w_kv",
           "w_branch_a", "w_branch_b", "w_branch_c", "w_out", "ffn2_w_gu", "ffn2_w_down"]
ROW_SHARDED = {"ffn1_w_down", "mem_w_kv", "w_out", "ffn2_w_down"}
SMALL = ["ffn1_norm", "mix_norm", "b_gate", "sg_ln_g", "sg_ln_b", "sg_w", "sg_b", "mla_cq_norm",
         "mla_ckv_norm", "mla_q_norm", "mla_k_norm", "mem_norm", "mem_q_norm", "mem_k_norm", "ffn2_norm"]
ORDER = ["ffn1_norm", "ffn1_w_gu", "ffn1_w_down", "mix_norm", "w_in", "b_gate", "sg_ln_g", "sg_ln_b", "sg_w",
         "sg_b", "mla_cq_norm", "mla_w_uq", "mla_ckv_norm", "mla_w_ukv", "mla_q_norm", "mla_k_norm", "mem_norm",
         "mem_w_kv", "mem_q_norm", "mem_k_norm", "w_branch_a", "w_branch_b", "w_branch_c", "w_out", "ffn2_norm",
         "ffn2_w_gu", "ffn2_w_down"]

_IN_U, _IN_V, _IN_CQ, _IN_CKV, _IN_KR, _IN_QM, _IN_G = 0, 512, 1024, 1408, 1664, 1696, 2208
IN_COLS = 5280


def _full_from_slabs(name, slabs):
    n, r, c = slabs.shape
    if name in ROW_SHARDED:
        return slabs.reshape(n * r, c)
    return slabs.transpose(1, 0, 2).reshape(r, n * c)


def _slabs_from_full(name, full):
    if name in ROW_SHARDED:
        return full.reshape(N_DEV, full.shape[0] // N_DEV, full.shape[1])
    r, c = full.shape
    return full.reshape(r, N_DEV, c // N_DEV).transpose(1, 0, 2)


def _compute_layout(full):
    W = dict(full)
    if "w_in" not in full:
        return W
    w = full["w_in"]
    kr = jnp.pad(w[:, _IN_KR:_IN_QM], ((0, 0), (KR_LANE, LANES - KR_LANE - MLA_ROPE)))
    W["w_in"] = jnp.concatenate([w[:, _IN_G:], w[:, _IN_U:_IN_CQ], w[:, _IN_QM:_IN_G], w[:, _IN_CKV:_IN_KR], kr,
                                 w[:, _IN_CQ:_IN_CKV]], axis=1)
    uq = full["mla_w_uq"].reshape(MLA_Q_RANK, MLA_HEADS, MLA_QK)
    W["mla_w_uq"] = jnp.pad(uq, ((0, 0), (0, 0), (0, LANES - MLA_QK))).reshape(MLA_Q_RANK, HP)
    ukv = full["mla_w_ukv"].reshape(MLA_KV_RANK, MLA_HEADS, MLA_NOPE + MLA_V)
    padh = lambda a: jnp.pad(a, ((0, 0), (0, 0), (0, LANES - a.shape[2]))).reshape(MLA_KV_RANK, HP)
    W["mla_w_ukv"] = jnp.concatenate([padh(ukv[:, :, :MLA_NOPE]), padh(ukv[:, :, MLA_NOPE:])], axis=1)
    wb = full["w_branch_b"].reshape(MLA_HEADS, MLA_V, D_MODEL)
    W["w_branch_b"] = jnp.pad(wb, ((0, 0), (0, LANES - MLA_V), (0, 0))).reshape(HP, D_MODEL)
    return W


def _reference_layout(G):
    out = dict(G)
    if "w_in" not in G:
        return out
    g = G["w_in"]
    out["w_in"] = jnp.concatenate([
        g[:, Z_U:Z_QM], g[:, Z_CQ:Z_COLS], g[:, Z_CKV:Z_KR], g[:, Z_KR + KR_LANE:Z_KR + KR_LANE + MLA_ROPE],
        g[:, Z_QM:Z_CKV], g[:, Z_G:Z_U]], axis=1)
    out["mla_w_uq"] = G["mla_w_uq"].reshape(MLA_Q_RANK, MLA_HEADS, LANES)[:, :, :MLA_QK].reshape(MLA_Q_RANK, -1)
    gk = G["mla_w_ukv"][:, :HP].reshape(MLA_KV_RANK, MLA_HEADS, LANES)[:, :, :MLA_NOPE]
    gv = G["mla_w_ukv"][:, HP:].reshape(MLA_KV_RANK, MLA_HEADS, LANES)[:, :, :MLA_V]
    out["mla_w_ukv"] = jnp.concatenate([gk, gv], axis=2).reshape(MLA_KV_RANK, -1)
    out["w_branch_b"] = G["w_branch_b"].reshape(MLA_HEADS, LANES, D_MODEL)[:, :MLA_V].reshape(-1, D_MODEL)
    return out


def _pack(parts):
    flat = []
    for a in parts:
        a = a.reshape(-1)
        flat.append(jnp.pad(a, (0, (-a.shape[0]) % LANES)))
    return jnp.concatenate(flat).reshape(-1, LANES)


def _unpack(packed, shapes):
    flat = packed.reshape(-1)
    out, off = [], 0
    for shp in shapes:
        n = int(np.prod(shp))
        out.append(flat[off:off + n].reshape(shp))
        off += n + (-n) % LANES
    return out


MESH = pl.DeviceIdType.MESH
HBM = pl.BlockSpec(memory_space=pltpu.HBM)


def _all_gather(shards):
    n = len(shards)

    def body(*refs):
        x_refs, out_refs, token_ref = refs[:n], refs[n:2 * n], refs[2 * n]
        send_sems, recv_sems, local_sems = refs[2 * n + 1:]
        x, y, c = lax.axis_index("x"), lax.axis_index("y"), lax.axis_index("c")
        me, sibling = (x, y, c), (x, y, 1 - c)
        chips = [(1 - x, y), (x, 1 - y), (1 - x, 1 - y)]
        token_ref[...] = jnp.zeros_like(token_ref)

        def slot(a, px, py, pc):
            return out_refs[a].at[4 * px + 2 * py + pc]

        def copy(a, k, block, to, src=None):
            return pltpu.make_async_remote_copy(
                src_ref=slot(a, *block) if src is None else src, dst_ref=slot(a, *block),
                send_sem=send_sems.at[7 * a + k], recv_sem=recv_sems.at[7 * a + k], device_id=to, device_id_type=MESH)

        arrays = range(n)
        mine = [pltpu.make_async_copy(x_refs[a], slot(a, *me), local_sems.at[a]) for a in arrays]
        for cp in mine:
            cp.start()
        first = [copy(a, 0, me, sibling, src=x_refs[a]) for a in arrays]
        first += [copy(a, 1 + j, me, (*chip, c), src=x_refs[a]) for j, chip in enumerate(chips) for a in arrays]
        for cp in first:
            cp.start()
        passed = []
        for j, chip in enumerate(chips):
            for a in arrays:
                copy(a, 1 + j, (*chip, c), me).wait_recv()
                passed.append(copy(a, 4 + j, (*chip, c), sibling))
                passed[-1].start()
        for a in arrays:
            copy(a, 0, sibling, me).wait_recv()
        for j, chip in enumerate(chips):
            for a in arrays:
                copy(a, 4 + j, (*chip, 1 - c), me).wait_recv()
        for cp in first + passed:
            cp.wait_send()
        for cp in mine:
            cp.wait()

    res = pl.pallas_call(
        body, name="all_gather_weights",
        out_shape=[jax.ShapeDtypeStruct((N_DEV,) + s.shape, s.dtype) for s in shards]
        + [jax.ShapeDtypeStruct((8, LANES), F32)],
        in_specs=[HBM] * n, out_specs=[HBM] * n + [pl.BlockSpec(memory_space=pltpu.VMEM)],
        scratch_shapes=[pltpu.SemaphoreType.DMA((7 * n,)), pltpu.SemaphoreType.DMA((7 * n,)),
                        pltpu.SemaphoreType.DMA((n,))],
    )(*shards)
    return res[:n], res[n]


SEM = pl.BlockSpec(memory_space=pltpu.SEMAPHORE)
DATAFLOW = pltpu.SideEffectType.DATAFLOW_SIDE_EFFECTING


def _peers():
    x, y, c = lax.axis_index("x"), lax.axis_index("y"), lax.axis_index("c")
    out = []
    for k in range(1, N_DEV):
        px = 1 - x if k & 4 else x
        py = 1 - y if k & 2 else y
        pc = 1 - c if k & 1 else c
        out.append((k, (px, py, pc), 4 * px + 2 * py + pc))
    return 4 * x + 2 * y + c, out


def _send_start(srcs, per_peer, name):
    n = len(srcs)
    lands = [lax.empty((N_DEV,) + (s.shape[1:] if per_peer else s.shape), s.dtype) for s in srcs]

    def body(*refs):
        src_refs, land_refs, send_sems, recv_sems, token = refs[:n], refs[n:2 * n], refs[2 * n], refs[2 * n + 1], refs[-1]
        me, peers = _peers()
        for a in range(n):
            for k, pid, pflat in peers:
                pltpu.make_async_remote_copy(
                    src_ref=src_refs[a].at[pflat] if per_peer else src_refs[a], dst_ref=land_refs[a].at[me],
                    send_sem=send_sems.at[7 * a + k - 1], recv_sem=recv_sems.at[7 * a + k - 1],
                    device_id=pid, device_id_type=MESH).start()
        token[...] = jnp.zeros_like(token)

    hbm = lambda a: pltpu.with_memory_space_constraint(a, pltpu.HBM)
    res = pl.pallas_call(
        body, name=name,
        out_shape=(pltpu.SemaphoreType.DMA((7 * n,)), pltpu.SemaphoreType.DMA((7 * n,)),
                   *[pltpu.HBM(a.shape, a.dtype) for a in srcs + lands], jax.ShapeDtypeStruct((8, LANES), F32)),
        in_specs=(HBM,) * (2 * n), out_specs=(SEM, SEM) + (HBM,) * (2 * n) + (pl.BlockSpec(memory_space=pltpu.VMEM),),
        input_output_aliases={i: 2 + i for i in range(2 * n)},
        compiler_params=pltpu.CompilerParams(has_side_effects=DATAFLOW),
    )(*[hbm(a) for a in srcs + lands])
    return (res[0], res[1], list(res[2:2 + n]), list(res[2 + n:2 + 2 * n])), res[-1]


def _send_wait(started, after, per_peer, name):
    send_sems, recv_sems, srcs_thru, lands_thru = started
    n = len(srcs_thru)

    def body(*refs):
        src_refs, land_refs, send_sems, recv_sems = refs[:n], refs[n:2 * n], refs[2 * n], refs[2 * n + 1]
        me, peers = _peers()
        for a in range(n):
            for k, pid, pflat in peers:
                copy = pltpu.make_async_remote_copy(
                    src_ref=src_refs[a].at[pflat] if per_peer else src_refs[a], dst_ref=land_refs[a].at[pflat],
                    send_sem=send_sems.at[7 * a + k - 1], recv_sem=recv_sems.at[7 * a + k - 1],
                    device_id=pid, device_id_type=MESH)
                copy.wait_send()
                copy.wait_recv()

    outs = pl.pallas_call(
        body, name=name,
        out_shape=tuple(pltpu.HBM(a.shape, a.dtype) for a in srcs_thru + lands_thru),
        in_specs=(HBM,) * (2 * n) + (SEM, SEM, pl.BlockSpec(memory_space=pl.ANY)), out_specs=(HBM,) * (2 * n),
        input_output_aliases={i: i for i in range(2 * n)},
        compiler_params=pltpu.CompilerParams(has_side_effects=DATAFLOW),
    )(*srcs_thru, *lands_thru, send_sems, recv_sems, after)
    me = 4 * lax.axis_index("x") + 2 * lax.axis_index("y") + lax.axis_index("c")
    landed = []
    for src_out, land in zip(outs[:n], outs[n:]):
        own = lax.dynamic_index_in_dim(src_out, me, 0, keepdims=True) if per_peer else src_out[None]
        landed.append(lax.dynamic_update_slice(land, own, (me,) + (0,) * (land.ndim - 1)))
    return landed


def _share_rows(block, name):
    def body(src_ref, out_ref, send_sems, recv_sems, local_sem):
        me, peers = _peers()
        own = pltpu.make_async_copy(src_ref, out_ref.at[me], local_sem)
        own.start()
        copies = [pltpu.make_async_remote_copy(
            src_ref=src_ref, dst_ref=out_ref.at[me], send_sem=send_sems.at[k - 1], recv_sem=recv_sems.at[k - 1],
            device_id=pid, device_id_type=MESH) for k, pid, _ in peers]
        for cp in copies:
            cp.start()
        for cp in copies:
            cp.wait()
        own.wait()

    return pl.pallas_call(
        body, name=name, out_shape=jax.ShapeDtypeStruct((N_DEV,) + block.shape, block.dtype),
        in_specs=[HBM], out_specs=HBM,
        scratch_shapes=[pltpu.SemaphoreType.DMA((N_DEV - 1,)), pltpu.SemaphoreType.DMA((N_DEV - 1,)),
                        pltpu.SemaphoreType.DMA],
    )(block)


def _sum_slots(recv, name, tr):
    n, rows, lanes = recv.shape
    tr = _tile(rows, tr)

    def body(r_ref, o_ref):
        acc = r_ref[0].astype(F32)
        for i in range(1, n):
            acc = acc + r_ref[i].astype(F32)
        o_ref[...] = acc

    return pl.pallas_call(
        body, name=name, grid=(rows // tr,),
        in_specs=[pl.BlockSpec((n, tr, lanes), lambda i: (0, i, 0))],
        out_specs=pl.BlockSpec((tr, lanes), lambda i: (i, 0)),
        out_shape=jax.ShapeDtypeStruct((rows, lanes), F32),
        compiler_params=_cparams(("parallel",)),
    )(recv)


def _adamw_math(w, g, m, v):
    m = ADAM_B1 * m + (1.0 - ADAM_B1) * g
    v = ADAM_B2 * v + (1.0 - ADAM_B2) * (g * g)
    m_hat = m / (1.0 - ADAM_B1 ** ADAM_STEP)
    v_hat = v / (1.0 - ADAM_B2 ** ADAM_STEP)
    return -ADAM_LR * (m_hat / (jnp.sqrt(v_hat) + ADAM_EPS) + ADAM_WD * w), m, v


def _adamw(w, g, m, v, name, tr=256):
    return _rowwise(_adamw_math, name, tr, [w, g, m, v], [], [(w.shape[1], F32)] * 3)


def _sum_adamw(recv, w, m, v, name):
    n, r, c = recv.shape
    tr = _tile(r, 256)

    def body(r_ref, w_ref, m_ref, v_ref, g_ref, d_ref, nm_ref, nv_ref):
        g = r_ref[0].astype(F32)
        for i in range(1, n):
            g = g + r_ref[i].astype(F32)
        g_ref[...] = g
        d_ref[...], nm_ref[...], nv_ref[...] = _adamw_math(w_ref[...], g, m_ref[...], v_ref[...])

    row = pl.BlockSpec((tr, c), lambda i: (i, 0))
    return pl.pallas_call(
        body, name=name, grid=(r // tr,),
        in_specs=[pl.BlockSpec((n, tr, c), lambda i: (0, i, 0)), row, row, row], out_specs=[row] * 4,
        out_shape=[jax.ShapeDtypeStruct((r, c), F32)] * 4, compiler_params=_cparams(("parallel",)),
    )(recv, w, m, v)


def kernel(x, mem, positions, ffn1_norm, ffn1_w_gu, ffn1_w_down, mix_norm, w_in, b_gate, sg_ln_g, sg_ln_b, sg_w, sg_b, mla_cq_norm, mla_w_uq, mla_ckv_norm, mla_w_ukv, mla_q_norm, mla_k_norm, mem_norm, mem_w_kv, mem_q_norm, mem_k_norm, w_branch_a, w_branch_b, w_branch_c, w_out, ffn2_norm, ffn2_w_gu, ffn2_w_down, loss_target, m_ffn1_norm, m_ffn1_w_gu, m_ffn1_w_down, m_mix_norm, m_w_in, m_b_gate, m_sg_ln_g, m_sg_ln_b, m_sg_w, m_sg_b, m_mla_cq_norm, m_mla_w_uq, m_mla_ckv_norm, m_mla_w_ukv, m_mla_q_norm, m_mla_k_norm, m_mem_norm, m_mem_w_kv, m_mem_q_norm, m_mem_k_norm, m_w_branch_a, m_w_branch_b, m_w_branch_c, m_w_out, m_ffn2_norm, m_ffn2_w_gu, m_ffn2_w_down, v_ffn1_norm, v_ffn1_w_gu, v_ffn1_w_down, v_mix_norm, v_w_in, v_b_gate, v_sg_ln_g, v_sg_ln_b, v_sg_w, v_sg_b, v_mla_cq_norm, v_mla_w_uq, v_mla_ckv_norm, v_mla_w_ukv, v_mla_q_norm, v_mla_k_norm, v_mem_norm, v_mem_w_kv, v_mem_q_norm, v_mem_k_norm, v_w_branch_a, v_w_branch_b, v_w_branch_c, v_w_out, v_ffn2_norm, v_ffn2_w_gu, v_ffn2_w_down):
    given = dict(locals())
    wts = {n: given[n] for n in ORDER}
    mom = {n: given["m_" + n] for n in ORDER}
    var = {n: given["v_" + n] for n in ORDER}

    def shards(group, zero):
        out = [wts[n][0].astype(BF16) for n in GROUPS[group]]
        return [out[0] + zero.astype(BF16)] + out[1:]

    def full_weights(group, slabs):
        return _compute_layout({n: _full_from_slabs(n, s) for n, s in zip(GROUPS[group], slabs)})

    def zero_of(a):
        return jnp.minimum(jnp.abs(a.reshape(-1)[0]), 0)

    gathered_ffn1, token = _all_gather([wts[n][0].astype(BF16) for n in GROUPS["ffn1"]])
    flight = {"mix": _send_start(shards("mix", token[0, 0]), False, "gather_mix_start")[0]}
    recv = {}

    def weights(group, after):
        if group == "ffn1":
            return full_weights(group, gathered_ffn1)
        landed = _send_wait(flight.pop(group), after, False, f"gather_{group}_wait")
        if group == "mix":
            flight["ffn2"] = _send_start(shards("ffn2", zero_of(landed[0])), False, "gather_ffn2_start")[0]
        return full_weights(group, landed)

    small_shapes = [wts[n].shape[1:] for n in SMALL]
    early = SMALL[1:]
    assert SMALL[0] == "ffn1_norm"

    def grads_out(group, G):
        Gr = _reference_layout({n: G[n] for n in GRAD_GROUPS[group]})
        parts = [_slabs_from_full(n, Gr[n]).astype(BF16) for n in GRAD_GROUPS[group]]
        flight["g_" + group], tie = _send_start(parts, True, f"grads_{group}_start")
        if group == "mix":
            small = _pack([G[n].reshape(s) for n, s in zip(early, small_shapes[1:])])
            small = jnp.pad(small, ((0, (-small.shape[0]) % 8), (0, 0)))
            flight["small"], tie = _send_start([small + tie[0, 0]], False, "grads_small_start")
        return tie

    P = {n: wts[n] if wts[n].ndim == 2 else wts[n][0] for n in SMALL}
    loss_part, grad_x, G = _local_step(x[0], mem[0], positions[0], loss_target[0], P, weights, grads_out)

    for group, names in GRAD_GROUPS.items():
        recv.update(zip(names, _send_wait(flight.pop("g_" + group), grad_x, True, f"grads_{group}_wait")))
    early_recv, = _send_wait(flight.pop("small"), grad_x, False, "grads_small_wait")
    last = _share_rows(G["ffn1_norm"].reshape(-1, LANES), "share_ffn1_norm")
    g_small_packed = _sum_slots(jnp.concatenate([last, early_recv], axis=1), "sum_small", 2048)

    grads, delta, new_m, new_v = {}, {}, {}, {}
    for n in SHARDED:
        grads[n], delta[n], new_m[n], new_v[n] = _sum_adamw(recv[n], wts[n][0], mom[n][0], var[n][0], "adamw_" + n)
    grads.update(zip(SMALL, _unpack(g_small_packed, small_shapes)))

    def pack_small(d):
        p = _pack([d[n].reshape(s) for n, s in zip(SMALL, small_shapes)])
        return jnp.pad(p, ((0, (-p.shape[0]) % 8), (0, 0)))

    ds, ms, vs = _adamw(pack_small(wts), g_small_packed, pack_small(mom), pack_small(var), "adamw_small", tr=2048)
    for dst, packed in ((delta, ds), (new_m, ms), (new_v, vs)):
        dst.update(zip(SMALL, _unpack(packed, small_shapes)))

    loss = lax.psum(jnp.sum(loss_part), ("x", "y", "c"))
    lead = lambda d: [d[n].reshape(wts[n].shape) for n in ORDER]
    return (loss, grad_x[None], *lead(grads), *lead(delta), *lead(new_m), *lead(new_v))
```

```python
import functools

import numpy as np
import jax
import jax.numpy as jnp
from jax import lax
from jax.experimental import pallas as pl
from jax.experimental.pallas import tpu as pltpu

F32, BF16 = jnp.float32, jnp.bfloat16

D_MODEL = 1024
SG_GROUPS, SG_GROUP_DIM, SG_WIDTH, CHUNK = 8, 64, 512, 128
MLA_HEADS, MLA_NOPE, MLA_ROPE, MLA_V, MLA_QK = 8, 64, 32, 64, 96
MLA_Q_RANK, MLA_KV_RANK = 384, 256
MEM_HEADS, MEM_HEAD_DIM, MEM_WIDTH = 4, 128, 512
D_FF = 2816
ROPE_BASE = 10000.0
EPS = 1e-6
NEG = -1e30
ADAM_LR, ADAM_B1, ADAM_B2, ADAM_EPS, ADAM_WD, ADAM_STEP = 0.001, 0.9, 0.999, 1e-08, 0.01, 10

N_DEV = 8
LANES = 128
V7X_VMEM_LIMIT = 56 * 1024 * 1024
HP = MLA_HEADS * LANES

Z_G, Z_U, Z_V, Z_QM, Z_CKV, Z_KR, Z_CQ = 0, 3072, 3584, 4096, 4608, 4864, 4992
Z_COLS = 5376
KR_LANE = 64


def _tile(dim, pref):
    if dim <= pref:
        return dim
    for t in range(pref - pref % LANES, LANES - 1, -LANES):
        if dim % t == 0:
            return t
    for t in range(pref - pref % 8, 7, -8):
        if dim % t == 0:
            return t
    return dim


def _cparams(sem):
    return pltpu.CompilerParams(dimension_semantics=sem, vmem_limit_bytes=V7X_VMEM_LIMIT)


_DN = {"nn": ((1,), (0,)), "nt": ((1,), (1,)), "tn": ((0,), (0,))}


def _dot(a, b, mode="nn"):
    return lax.dot_general(a.astype(BF16), b.astype(BF16), (_DN[mode], ((), ())),
                           preferred_element_type=F32)


def _mm(a, b, mode, out_dtype, name, tm=512, tn=512, tk=2048, tie=None):
    if mode == "tn":
        K, M = a.shape
    else:
        M, K = a.shape
    N = b.shape[0] if mode == "nt" else b.shape[1]
    tm, tn, tk = _tile(M, tm), _tile(N, tn), _tile(K, tk)
    nk = K // tk
    if mode == "tn":
        a_spec = pl.BlockSpec((tk, tm), lambda i, j, k: (k, i))
    else:
        a_spec = pl.BlockSpec((tm, tk), lambda i, j, k: (i, k))
    if mode == "nt":
        b_spec = pl.BlockSpec((tn, tk), lambda i, j, k: (j, k))
    else:
        b_spec = pl.BlockSpec((tk, tn), lambda i, j, k: (k, j))

    ties = [] if tie is None else [tie]

    def body(a_ref, b_ref, *rest):
        o_ref, *scratch = rest[len(ties):]
        p = _dot(a_ref[...], b_ref[...], mode)
        if nk == 1:
            o_ref[...] = p.astype(o_ref.dtype)
        else:
            acc_ref, = scratch
            k = pl.program_id(2)

            @pl.when(k == 0)
            def _():
                acc_ref[...] = p

            @pl.when(k > 0)
            def _():
                acc_ref[...] += p

            @pl.when(k == nk - 1)
            def _():
                o_ref[...] = acc_ref[...].astype(o_ref.dtype)

    return pl.pallas_call(
        body, name=name, grid=(M // tm, N // tn, nk),
        in_specs=[a_spec, b_spec] + [pl.BlockSpec(t.shape, lambda i, j, k: (0, 0)) for t in ties],
        out_specs=pl.BlockSpec((tm, tn), lambda i, j, k: (i, j)),
        out_shape=jax.ShapeDtypeStruct((M, N), out_dtype),
        scratch_shapes=[] if nk == 1 else [pltpu.VMEM((tm, tn), F32)],
        compiler_params=_cparams(("parallel", "parallel", "arbitrary")),
    )(a, b, *ties)


def _mm_t(at, b, name, tm, tn, tk=1024, tie=None):
    return _mm(at, b, "nn", BF16, name, tm=tm, tn=tn, tk=tk, tie=tie)


def _rowwise(fn, name, tr, row_ins, bc_ins, row_outs, acc_outs=()):
    norm = [it if isinstance(it, tuple) else (it, it.shape[1], 0) for it in row_ins]
    rows = norm[0][0].shape[0]
    tr = _tile(rows, tr)
    arrays, in_specs = [], []
    for arr, w, cb in norm:
        arrays.append(arr)
        in_specs.append(pl.BlockSpec((tr, w), lambda i, cb=cb: (i, cb)))
    for arr in bc_ins:
        arrays.append(arr)
        in_specs.append(pl.BlockSpec(arr.shape, lambda i, nd=arr.ndim: (0,) * nd))
    out_shape, out_specs = [], []
    transposed = [len(o) == 3 for o in row_outs]
    for (w, dt, *_), t in zip(row_outs, transposed):
        out_shape.append(jax.ShapeDtypeStruct((w, rows) if t else (rows, w), dt))
        out_specs.append(pl.BlockSpec((w, tr), lambda i: (0, i)) if t else pl.BlockSpec((tr, w), lambda i: (i, 0)))
    for shp, dt in acc_outs:
        out_shape.append(jax.ShapeDtypeStruct(shp, dt))
        out_specs.append(pl.BlockSpec(shp, lambda i, nd=len(shp): (0,) * nd))
    n_in, n_row = len(arrays), len(row_outs)

    def body(*refs):
        vals = fn(*[r[...].astype(F32) for r in refs[:n_in]])
        if not isinstance(vals, (tuple, list)):
            vals = (vals,)
        outs = refs[n_in:]
        for r, v, t in zip(outs[:n_row], vals[:n_row], transposed):
            r[...] = v.astype(F32).T.astype(r.dtype) if t else v.astype(r.dtype)
        if acc_outs:
            accs = list(zip(outs[n_row:], vals[n_row:]))
            i = pl.program_id(0)

            @pl.when(i == 0)
            def _():
                for r, v in accs:
                    r[...] = v.astype(r.dtype)

            @pl.when(i > 0)
            def _():
                for r, v in accs:
                    r[...] += v.astype(r.dtype)

    res = pl.pallas_call(
        body, name=name, grid=(rows // tr,), in_specs=in_specs, out_specs=out_specs,
        out_shape=out_shape, compiler_params=_cparams(("arbitrary",)),
    )(*arrays)
    return res


def _rsum(x):
    return jnp.sum(x, axis=0, keepdims=True)


def _rms(x, g, n=None):
    n = x.shape[-1] if n is None else n
    r = lax.rsqrt(jnp.sum(x * x, axis=-1, keepdims=True) * (1.0 / n) + EPS)
    return x * r * g


def _rms_bwd(x, g, dy, n=None):
    n = x.shape[-1] if n is None else n
    r = lax.rsqrt(jnp.sum(x * x, axis=-1, keepdims=True) * (1.0 / n) + EPS)
    xh = x * r
    dxh = dy * g
    dx = r * (dxh - xh * (jnp.sum(dxh * xh, axis=-1, keepdims=True) * (1.0 / n)))
    return dx, _rsum(dy * xh)


def _gelu(x):
    return 0.5 * x * (1.0 + lax.erf(x * 0.7071067811865476))


def _gelu_grad(x):
    return 0.5 * (1.0 + lax.erf(x * 0.7071067811865476)) + x * jnp.exp(-0.5 * x * x) * 0.3989422804014327


def _sigmoid(x):
    return 1.0 / (1.0 + jnp.exp(-x))


FFN_TM, FFN_TN = 512, 1408
MXU_WIDTH = 256


def _col_chunks(n):
    return [(c, min(c + MXU_WIDTH, n)) for c in range(0, n, MXU_WIDTH)]


def _ffn_gu_act(h, w_gu, tag):
    T = h.shape[0]
    tm, tn = _tile(T, FFN_TM), FFN_TN
    nj = D_FF // tn

    def body(h_ref, wg_ref, wu_ref, gu_ref, a_ref, at_ref):
        h = h_ref[...]
        for c0, c1 in _col_chunks(tn):
            g = _dot(h, wg_ref[:, c0:c1])
            u = _dot(h, wu_ref[:, c0:c1])
            gu_ref[0, :, c0:c1] = g.astype(BF16)
            gu_ref[1, :, c0:c1] = u.astype(BF16)
            a = g * _sigmoid(g) * u
            a_ref[:, c0:c1] = a.astype(BF16)
            at_ref[c0:c1, :] = a.T.astype(BF16)

    return pl.pallas_call(
        body, name=f"{tag}_gu_act", grid=(T // tm, nj),
        in_specs=[pl.BlockSpec((tm, D_MODEL), lambda i, j: (i, 0)),
                  pl.BlockSpec((D_MODEL, tn), lambda i, j: (0, j)),
                  pl.BlockSpec((D_MODEL, tn), lambda i, j: (0, j + nj))],
        out_specs=[pl.BlockSpec((2, tm, tn), lambda i, j: (0, i, j)),
                   pl.BlockSpec((tm, tn), lambda i, j: (i, j)),
                   pl.BlockSpec((tn, tm), lambda i, j: (j, i))],
        out_shape=[jax.ShapeDtypeStruct((2, T, D_FF), BF16), jax.ShapeDtypeStruct((T, D_FF), BF16),
                   jax.ShapeDtypeStruct((D_FF, T), BF16)],
        compiler_params=_cparams(("parallel", "parallel")),
    )(h, w_gu, w_gu)


def _ffn_da_actbwd(do, w_down, gu, tag, tie=None):
    T = do.shape[0]
    tm, tn = _tile(T, FFN_TM), FFN_TN
    ties = [] if tie is None else [tie]

    def body(do_ref, wd_ref, gu_ref, *rest):
        dgu_ref = rest[-1]
        do = do_ref[...]
        for c0, c1 in _col_chunks(tn):
            da = _dot(do, wd_ref[c0:c1, :], "nt")
            g = gu_ref[0, :, c0:c1].astype(F32)
            u = gu_ref[1, :, c0:c1].astype(F32)
            s = _sigmoid(g)
            dgu_ref[0, :, c0:c1] = (da * u * s * (1.0 + g * (1.0 - s))).astype(BF16)
            dgu_ref[1, :, c0:c1] = (da * g * s).astype(BF16)

    return pl.pallas_call(
        body, name=f"{tag}_da_actbwd", grid=(T // tm, D_FF // tn),
        in_specs=[pl.BlockSpec((tm, D_MODEL), lambda i, j: (i, 0)),
                  pl.BlockSpec((tn, D_MODEL), lambda i, j: (j, 0)),
                  pl.BlockSpec((2, tm, tn), lambda i, j: (0, i, j))]
        + [pl.BlockSpec(t.shape, lambda i, j: (0, 0)) for t in ties],
        out_specs=pl.BlockSpec((2, tm, tn), lambda i, j: (0, i, j)),
        out_shape=jax.ShapeDtypeStruct((2, T, D_FF), BF16),
        compiler_params=_cparams(("parallel", "parallel")),
    )(do, w_down, gu, *ties)


def _ffn_dwgu(ht, dgu, tag, tk=1024):
    T = ht.shape[1]
    tn, tk = FFN_TN, _tile(T, tk)
    nj, nk = D_FF // tn, T // tk

    def body(a_ref, b_ref, o_ref, acc_ref):
        k = pl.program_id(1)
        p = _dot(a_ref[...], b_ref[...])

        @pl.when(k == 0)
        def _():
            acc_ref[...] = p

        @pl.when(k > 0)
        def _():
            acc_ref[...] += p

        @pl.when(k == nk - 1)
        def _():
            o_ref[...] = acc_ref[...].astype(o_ref.dtype)

    return pl.pallas_call(
        body, name=f"{tag}_dwgu", grid=(2 * nj, nk),
        in_specs=[pl.BlockSpec((D_MODEL, tk), lambda n, k: (0, k)),
                  pl.BlockSpec((None, tk, tn), lambda n, k: (n // nj, k, n % nj))],
        out_specs=pl.BlockSpec((D_MODEL, tn), lambda n, k: (0, n)),
        out_shape=jax.ShapeDtypeStruct((D_MODEL, 2 * D_FF), BF16),
        scratch_shapes=[pltpu.VMEM((D_MODEL, tn), F32)],
        compiler_params=_cparams(("parallel", "arbitrary")),
    )(ht, dgu)


def _ffn_dh(dgu, w_gu, tag, tm=1024, tie=None):
    T = dgu.shape[1]
    tm, tk = _tile(T, tm), FFN_TN
    nk = D_FF // tk
    ties = [] if tie is None else [tie]

    def body(a_ref, b_ref, *rest):
        o_ref, acc_ref = rest[len(ties):]
        k = pl.program_id(1)
        p = _dot(a_ref[...], b_ref[...], "nt")

        @pl.when(k == 0)
        def _():
            acc_ref[...] = p

        @pl.when(k > 0)
        def _():
            acc_ref[...] += p

        @pl.when(k == 2 * nk - 1)
        def _():
            o_ref[...] = acc_ref[...]

    return pl.pallas_call(
        body, name=f"{tag}_dh", grid=(T // tm, 2 * nk),
        in_specs=[pl.BlockSpec((None, tm, tk), lambda i, k: (k // nk, i, k % nk)),
                  pl.BlockSpec((D_MODEL, tk), lambda i, k: (0, k))]
        + [pl.BlockSpec(t.shape, lambda i, k: (0, 0)) for t in ties],
        out_specs=pl.BlockSpec((tm, D_MODEL), lambda i, k: (i, 0)),
        out_shape=jax.ShapeDtypeStruct((T, D_MODEL), F32),
        scratch_shapes=[pltpu.VMEM((tm, D_MODEL), F32)],
        compiler_params=_cparams(("parallel", "arbitrary")),
    )(dgu, w_gu, *ties)


def _ffn_fwd(h, w_gu, w_down, tag):
    gu, a, at = _ffn_gu_act(h, w_gu, tag)
    if callable(w_down):
        w_down = w_down(at)
    o = _mm(a, w_down, "nn", F32, f"{tag}_down", tm=1024, tn=1024, tk=2816)
    return gu, at, o


def _ffn_bwd(do, ht, gu, at, w_gu, w_down, tag, tie=None, on_dw=None):
    on_dw = on_dw or (lambda which, dw: None)
    dw_down = _mm_t(at, do, f"{tag}_dwdown", tm=1408, tn=1024, tie=tie)
    dgu = _ffn_da_actbwd(do, w_down, gu, tag, tie=on_dw("down", dw_down))
    dw_gu = _ffn_dwgu(ht, dgu, tag)
    dh = _ffn_dh(dgu, w_gu, tag, tie=on_dw("gu", dw_gu))
    return dh, dw_gu, dw_down


def _sg_common(u_pre, v_pre, ln_g, ln_b):
    u = _gelu(u_pre)
    v = _gelu(v_pre)
    mu = jnp.mean(v, axis=-1, keepdims=True)
    vc = v - mu
    rstd = lax.rsqrt(jnp.mean(vc * vc, axis=-1, keepdims=True) + EPS)
    vhat = vc * rstd
    vl = vhat * ln_g + ln_b
    return u, vhat, rstd, vl


def _sg_masked_pairs(w):
    t = lax.broadcasted_iota(jnp.int32, (CHUNK, CHUNK), 0)
    s = lax.broadcasted_iota(jnp.int32, (CHUNK, CHUNK), 1)
    causal = s <= t
    wm = [jnp.where(causal, w[g], 0.0).astype(BF16) for g in range(SG_GROUPS)]
    return [jnp.concatenate([wm[2 * j], wm[2 * j + 1]], axis=0) for j in range(SG_GROUPS // 2)], causal


def _sg_mix(vl, pairs, bias):
    tr = vl.shape[0]
    low = lax.broadcasted_iota(jnp.int32, (CHUNK, LANES), 1) < SG_GROUP_DIM
    vb = vl.astype(BF16)
    rows = []
    for c in range(tr // CHUNK):
        slabs = []
        for j in range(SG_GROUPS // 2):
            slab = vb[c * CHUNK:(c + 1) * CHUNK, j * LANES:(j + 1) * LANES]
            m = _dot(pairs[j], slab)
            slabs.append(jnp.where(low, m[:CHUNK], m[CHUNK:]))
        rows.append(jnp.concatenate(slabs, axis=1) + bias)
    return jnp.concatenate(rows, axis=0)


def _sg_fwd(z, ln_g, ln_b, sg_w, bias_full):
    def fn(u_pre, v_pre, ln_g, ln_b, w, bias):
        u, _, _, vl = _sg_common(u_pre, v_pre, ln_g, ln_b)
        pairs, _ = _sg_masked_pairs(w)
        y = u * _sg_mix(vl, pairs, bias)
        return y, y

    return _rowwise(fn, "sg_fwd", 512, [(z, SG_WIDTH, Z_U // SG_WIDTH), (z, SG_WIDTH, Z_V // SG_WIDTH)],
                    [ln_g, ln_b, sg_w, bias_full], [(SG_WIDTH, BF16), (SG_WIDTH, BF16, "T")])


def _sg_bwd(z, dy, ln_g, ln_b, sg_w, bias_full, group_ind):
    def fn(u_pre, v_pre, dy, ln_g, ln_b, w, bias, ind):
        dy = dy.astype(F32)
        u, vhat, rstd, vl = _sg_common(u_pre, v_pre, ln_g, ln_b)
        pairs, causal = _sg_masked_pairs(w)
        mixed = _sg_mix(vl, pairs, bias)
        du_pre = dy * mixed * _gelu_grad(u_pre)
        dmix = dy * u
        tr = dy.shape[0]
        low = lax.broadcasted_iota(jnp.int32, (CHUNK, LANES), 1) < SG_GROUP_DIM
        vb = vl.astype(BF16)
        dw = [jnp.zeros((CHUNK, CHUNK), F32) for _ in range(SG_GROUPS)]
        dbias = jnp.zeros((CHUNK, SG_WIDTH), F32)
        dvl_rows = []
        for c in range(tr // CHUNK):
            dm_c = dmix[c * CHUNK:(c + 1) * CHUNK]
            dbias = dbias + dm_c
            slabs = []
            for j in range(SG_GROUPS // 2):
                slab = vb[c * CHUNK:(c + 1) * CHUNK, j * LANES:(j + 1) * LANES]
                dm = dm_c[:, j * LANES:(j + 1) * LANES]
                d0 = jnp.where(low, dm, 0.0).astype(BF16)
                d1 = jnp.where(low, 0.0, dm).astype(BF16)
                dw[2 * j] = dw[2 * j] + _dot(d0, slab, "nt")
                dw[2 * j + 1] = dw[2 * j + 1] + _dot(d1, slab, "nt")
                slabs.append(_dot(pairs[j], jnp.concatenate([d0, d1], axis=0), "tn"))
            dvl_rows.append(jnp.concatenate(slabs, axis=1))
        dvl = jnp.concatenate(dvl_rows, axis=0)
        dln_g = _rsum(dvl * vhat)
        dln_b = _rsum(dvl)
        dvh = dvl * ln_g
        dv = rstd * (dvh - jnp.mean(dvh, axis=-1, keepdims=True)
                     - vhat * jnp.mean(dvh * vhat, axis=-1, keepdims=True))
        dv_pre = dv * _gelu_grad(v_pre)
        dw = jnp.stack([jnp.where(causal, d, 0.0) for d in dw], axis=0)
        dbias_t = lax.dot_general(dbias, ind, (((1,), (0,)), ((), ())), precision=lax.Precision.HIGHEST,
                                  preferred_element_type=F32)
        return du_pre, dv_pre, dw, dbias_t, dln_g, dln_b

    return _rowwise(fn, "sg_bwd", 512,
                    [(z, SG_WIDTH, Z_U // SG_WIDTH), (z, SG_WIDTH, Z_V // SG_WIDTH), dy],
                    [ln_g, ln_b, sg_w, bias_full, group_ind],
                    [(SG_WIDTH, BF16), (SG_WIDTH, BF16)],
                    [((SG_GROUPS, CHUNK, CHUNK), F32), ((CHUNK, SG_GROUPS), F32), ((1, SG_WIDTH), F32), ((1, SG_WIDTH), F32)])


def _rope(x, c, s1, s2):
    return x * c + pltpu.roll(x, LANES - MLA_ROPE // 2, 1) * s1 + pltpu.roll(x, MLA_ROPE // 2, 1) * s2


def _rope_t(d, c, s1, s2):
    return d * c + pltpu.roll(d * s1, MLA_ROPE // 2, 1) + pltpu.roll(d * s2, LANES - MLA_ROPE // 2, 1)


def _mla_post(q_pre, kv_pre, z, tabs, gq, gk):
    scale = MLA_QK ** -0.5 * LOG2E

    def fn(q_pre, k_pre, v_pre, kr, c, s1, s2, gq, gk):
        qs, ks = [], []
        for h in range(MLA_HEADS):
            sl = slice(h * LANES, (h + 1) * LANES)
            qs.append(_rope(_rms(q_pre[:, sl], gq, MLA_QK), c, s1, s2) * scale)
            ks.append(_rope(_rms(k_pre[:, sl] + kr, gk, MLA_QK), c, s1, s2))
        lane = lax.broadcasted_iota(jnp.int32, v_pre.shape, 1) & (LANES - 1)
        return jnp.concatenate(qs, axis=1), jnp.concatenate(ks, axis=1), jnp.where(lane == ONES_LANE, 1.0, v_pre)

    return _rowwise(fn, "mla_post", 256,
                    [q_pre, (kv_pre, HP, 0), (kv_pre, HP, 1), (z, LANES, Z_KR // LANES), *tabs],
                    [gq, gk], [(HP, BF16)] * 3)


def _mla_post_bwd(q_pre, kv_pre, z, tabs, gq, gk, dq, dk, dv):
    scale = MLA_QK ** -0.5

    def fn(q_pre, k_pre, kr, c, s1, s2, dq, dk, dv, gq, gk):
        lane = lax.broadcasted_iota(jnp.int32, (1, LANES), 1)
        kr_mask = (lane >= KR_LANE) & (lane < KR_LANE + MLA_ROPE)
        dqs, dks = [], []
        dgq = jnp.zeros((1, LANES), F32)
        dgk = jnp.zeros((1, LANES), F32)
        dkr = jnp.zeros(kr.shape, F32)
        for h in range(MLA_HEADS):
            sl = slice(h * LANES, (h + 1) * LANES)
            dqn = _rope_t(dq[:, sl].astype(F32), c, s1, s2) * scale
            dx, dg = _rms_bwd(q_pre[:, sl], gq, dqn, MLA_QK)
            dqs.append(dx)
            dgq = dgq + dg
            dkn = _rope_t(dk[:, sl].astype(F32), c, s1, s2)
            dx, dg = _rms_bwd(k_pre[:, sl] + kr, gk, dkn, MLA_QK)
            dks.append(dx)
            dgk = dgk + dg
            dkr = dkr + dx
        dkr = jnp.where(kr_mask, dkr, 0.0)
        dkv = jnp.concatenate(dks + [dv.astype(F32)], axis=1)
        return jnp.concatenate(dqs, axis=1), dkv, dkr, dgq, dgk

    return _rowwise(fn, "mla_post_bwd", 256,
                    [q_pre, (kv_pre, HP, 0), (z, LANES, Z_KR // LANES), *tabs, dq, dk, dv],
                    [gq, gk], [(HP, BF16), (2 * HP, BF16), (LANES, BF16)],
                    [((1, LANES), F32), ((1, LANES), F32)])


def _pairs(n, lower):
    a, b = [], []
    for o in range(n):
        inner = range(o + 1) if lower else range(o, n)
        for t in inner:
            a.append(o)
            b.append(t)
    return jnp.asarray(np.array(a, np.int32)), jnp.asarray(np.array(b, np.int32))


FLASH_TILE, FLASH_SUB_ROWS = 1024, 512
LOG2E, LN2 = 1.4426950408889634, 0.6931471805599453
ONES_LANE = MLA_V


def _flash_tiles(T):
    tq = _tile(T, FLASH_TILE)
    return tq, _tile(tq, FLASH_SUB_ROWS)


def _col_span(t, sr, rb, diag, key_major):
    if not diag:
        return 0, t
    return (rb * sr, t) if key_major else (0, (rb + 1) * sr)


def _span_iota(sr, rb, c0, c1):
    r = lax.broadcasted_iota(jnp.int32, (sr, c1 - c0), 0) + rb * sr
    c = lax.broadcasted_iota(jnp.int32, (sr, c1 - c0), 1) + c0
    return r, c


def _lanes(x, width):
    return jnp.concatenate([x] * (width // LANES), axis=1)


def _flash_fwd(q, k, v):
    T = q.shape[0]
    tq, sr = _flash_tiles(T)
    n = T // tq
    ii, jj = _pairs(n, True)

    def body(ii_ref, jj_ref, q_ref, k_ref, v_ref, o_ref, ot_ref, lse_ref, m_sc, acc_sc):
        p_ = pl.program_id(1)
        i, j = ii_ref[p_], jj_ref[p_]

        @pl.when(j == 0)
        def _():
            m_sc[...] = jnp.full(m_sc.shape, NEG, F32)
            acc_sc[...] = jnp.zeros(acc_sc.shape, F32)

        def tile(diag):
            for rb in range(tq // sr):
                rows = slice(rb * sr, (rb + 1) * sr)
                c0, c1 = _col_span(tq, sr, rb, diag, False)
                s = _dot(q_ref[rows, :], k_ref[c0:c1, :], "nt")
                if diag:
                    r, c = _span_iota(sr, rb, c0, c1)
                    s = jnp.where(c <= r, s, NEG)
                m = m_sc[rows, :]
                m_new = jnp.maximum(m, jnp.max(s, axis=1, keepdims=True))
                p = jnp.exp2(s - _lanes(m_new, c1 - c0))
                acc_sc[rows, :] = jnp.exp2(m - m_new) * acc_sc[rows, :] + _dot(p, v_ref[c0:c1, :])
                m_sc[rows, :] = m_new

        @pl.when(j < i)
        def _():
            tile(False)

        @pl.when(j == i)
        def _():
            tile(True)
            acc = acc_sc[...]
            lane = lax.broadcasted_iota(jnp.int32, acc.shape, 1)
            l = jnp.sum(jnp.where(lane == ONES_LANE, acc, 0.0), axis=1, keepdims=True)
            o = jnp.where(lane < MLA_V, acc / l, 0.0)
            o_ref[...] = o.astype(o_ref.dtype)
            ot_ref[...] = o.T.astype(ot_ref.dtype)
            lse_ref[...] = m_sc[...] + jnp.log2(l)

    blk = lambda which: pl.BlockSpec((tq, LANES), which)
    qmap = lambda h, p, ii, jj: (ii[p], h)
    kmap = lambda h, p, ii, jj: (jj[p], h)
    return pl.pallas_call(
        body, name="mla_flash_fwd",
        grid_spec=pltpu.PrefetchScalarGridSpec(
            num_scalar_prefetch=2, grid=(MLA_HEADS, int(ii.shape[0])),
            in_specs=[blk(qmap), blk(kmap), blk(kmap)],
            out_specs=[blk(qmap), pl.BlockSpec((LANES, tq), lambda h, p, ii, jj: (h, ii[p])), blk(qmap)],
            scratch_shapes=[pltpu.VMEM((tq, LANES), F32)] * 2),
        out_shape=[jax.ShapeDtypeStruct((T, HP), BF16), jax.ShapeDtypeStruct((HP, T), BF16),
                   jax.ShapeDtypeStruct((T, HP), F32)],
        compiler_params=_cparams(("parallel", "arbitrary")),
    )(ii, jj, q, k, v)


def _flash_dq(q, k, v, do, lse, delta):
    T = q.shape[0]
    tq, sr = _flash_tiles(T)
    n = T // tq
    ii, jj = _pairs(n, True)

    def body(ii_ref, jj_ref, q_ref, k_ref, v_ref, do_ref, lse_ref, dl_ref, dq_ref, acc_sc):
        p_ = pl.program_id(1)
        i, j = ii_ref[p_], jj_ref[p_]

        @pl.when(j == 0)
        def _():
            acc_sc[...] = jnp.zeros(acc_sc.shape, F32)

        def tile(diag):
            for rb in range(tq // sr):
                rows = slice(rb * sr, (rb + 1) * sr)
                c0, c1 = _col_span(tq, sr, rb, diag, False)
                ks = k_ref[c0:c1, :]
                p = jnp.exp2(_dot(q_ref[rows, :], ks, "nt") - _lanes(lse_ref[rows, :], c1 - c0))
                if diag:
                    r, c = _span_iota(sr, rb, c0, c1)
                    p = jnp.where(c <= r, p, 0.0)
                dp = _dot(do_ref[rows, :], v_ref[c0:c1, :], "nt")
                acc_sc[rows, :] += _dot(p * (dp - _lanes(dl_ref[rows, :], c1 - c0)), ks)

        @pl.when(j < i)
        def _():
            tile(False)

        @pl.when(j == i)
        def _():
            tile(True)
            dq_ref[...] = acc_sc[...]

    blk = lambda which: pl.BlockSpec((tq, LANES), which)
    qmap = lambda h, p, ii, jj: (ii[p], h)
    kmap = lambda h, p, ii, jj: (jj[p], h)
    return pl.pallas_call(
        body, name="mla_flash_dq",
        grid_spec=pltpu.PrefetchScalarGridSpec(
            num_scalar_prefetch=2, grid=(MLA_HEADS, int(ii.shape[0])),
            in_specs=[blk(qmap), blk(kmap), blk(kmap), blk(qmap), blk(qmap), blk(qmap)],
            out_specs=blk(qmap),
            scratch_shapes=[pltpu.VMEM((tq, LANES), F32)]),
        out_shape=jax.ShapeDtypeStruct((T, HP), F32),
        compiler_params=_cparams(("parallel", "arbitrary")),
    )(ii, jj, q, k, v, do, lse, delta)


def _flash_dkv(q, k, v, do, lse_row, delta_row):
    T = q.shape[0]
    tq, sr = _flash_tiles(T)
    n = T // tq
    jj, ii = _pairs(n, False)

    def body(jj_ref, ii_ref, q_ref, k_ref, v_ref, do_ref, lse_ref, dl_ref, dk_ref, dv_ref, dk_sc, dv_sc):
        p_ = pl.program_id(1)
        j, i = jj_ref[p_], ii_ref[p_]

        @pl.when(i == j)
        def _():
            dk_sc[...] = jnp.zeros(dk_sc.shape, F32)
            dv_sc[...] = jnp.zeros(dv_sc.shape, F32)

        def tile(diag):
            for rb in range(tq // sr):
                rows = slice(rb * sr, (rb + 1) * sr)
                c0, c1 = _col_span(tq, sr, rb, diag, True)
                qs, dos = q_ref[c0:c1, :], do_ref[c0:c1, :]
                pt = jnp.exp2(_dot(k_ref[rows, :], qs, "nt") - lse_ref[:, c0:c1])
                if diag:
                    r, c = _span_iota(sr, rb, c0, c1)
                    pt = jnp.where(r <= c, pt, 0.0)
                dpt = _dot(v_ref[rows, :], dos, "nt")
                dv_sc[rows, :] += _dot(pt, dos)
                dk_sc[rows, :] += _dot(pt * (dpt - dl_ref[:, c0:c1]), qs)

        @pl.when(i == j)
        def _():
            tile(True)

        @pl.when(i > j)
        def _():
            tile(False)

        @pl.when(i == n - 1)
        def _():
            dk_ref[...] = dk_sc[...] * LN2
            dv_ref[...] = dv_sc[...]

    blk = lambda which: pl.BlockSpec((tq, LANES), which)
    qmap = lambda h, p, jj, ii: (ii[p], h)
    kmap = lambda h, p, jj, ii: (jj[p], h)
    row = pl.BlockSpec((None, 1, tq), lambda h, p, jj, ii: (h, 0, ii[p]))
    return pl.pallas_call(
        body, name="mla_flash_dkv",
        grid_spec=pltpu.PrefetchScalarGridSpec(
            num_scalar_prefetch=2, grid=(MLA_HEADS, int(ii.shape[0])),
            in_specs=[blk(qmap), blk(kmap), blk(kmap), blk(qmap), row, row],
            out_specs=[blk(kmap), blk(kmap)],
            scratch_shapes=[pltpu.VMEM((tq, LANES), F32)] * 2),
        out_shape=[jax.ShapeDtypeStruct((T, HP), F32)] * 2,
        compiler_params=_cparams(("parallel", "arbitrary")),
    )(jj, ii, q, k, v, do, lse_row, delta_row)


def _mem_fwd(z, km, vm, gq):
    scale = MEM_HEAD_DIM ** -0.5

    def fn(qm, km, vm, gq):
        ys = []
        for h in range(MEM_HEADS):
            sl = slice(h * LANES, (h + 1) * LANES)
            q = _rms(qm[:, sl], gq) * scale
            s = _dot(q, km[:, sl], "nt")
            p = jnp.exp(s - jnp.max(s, axis=1, keepdims=True))
            p = p / jnp.sum(p, axis=1, keepdims=True)
            ys.append(_dot(p, vm[:, sl]))
        y = jnp.concatenate(ys, axis=1)
        return y, y

    return _rowwise(fn, "mem_fwd", 512, [(z, MEM_WIDTH, Z_QM // MEM_WIDTH)], [km, vm, gq],
                    [(MEM_WIDTH, BF16), (MEM_WIDTH, BF16, "T")])


def _mem_bwd(z, dy, km, vm, gq):
    scale = MEM_HEAD_DIM ** -0.5

    def fn(qm, dy, km, vm, gq):
        dqs, dks, dvs = [], [], []
        dgq = jnp.zeros((1, LANES), F32)
        for h in range(MEM_HEADS):
            sl = slice(h * LANES, (h + 1) * LANES)
            q = (_rms(qm[:, sl], gq) * scale).astype(BF16)
            dyh = dy[:, sl]
            kh, vh = km[:, sl], vm[:, sl]
            s = _dot(q, kh, "nt")
            p = jnp.exp(s - jnp.max(s, axis=1, keepdims=True))
            p = p / jnp.sum(p, axis=1, keepdims=True)
            dp = _dot(dyh, vh, "nt")
            ds = p * (dp - jnp.sum(p * dp, axis=1, keepdims=True))
            dq = _dot(ds, kh) * scale
            dx, dg = _rms_bwd(qm[:, sl], gq, dq)
            dqs.append(dx)
            dgq = dgq + dg
            st = _dot(kh, q, "nt")
            pt = jnp.exp(st - jnp.max(st, axis=0, keepdims=True))
            pt = pt / jnp.sum(pt, axis=0, keepdims=True)
            dpt = _dot(vh, dyh, "nt")
            dst = pt * (dpt - jnp.sum(pt * dpt, axis=0, keepdims=True))
            dvs.append(_dot(pt, dyh))
            dks.append(_dot(dst, q))
        return jnp.concatenate(dqs, axis=1), jnp.concatenate(dks, axis=1), jnp.concatenate(dvs, axis=1), dgq

    m = km.shape[0]
    return _rowwise(fn, "mem_bwd", 512, [(z, MEM_WIDTH, Z_QM // MEM_WIDTH), dy], [km, vm, gq],
                    [(MEM_WIDTH, BF16)], [((m, MEM_WIDTH), F32), ((m, MEM_WIDTH), F32), ((1, LANES), F32)])


GROUPS = {"ffn1": ["ffn1_w_gu"], "ffn1_down": ["ffn1_w_down"],
          "mix": ["w_in", "mla_w_uq", "mla_w_ukv", "mem_w_kv", "w_branch_a", "w_branch_b", "w_branch_c", "w_out"],
          "ffn2": ["ffn2_w_gu", "ffn2_w_down"]}
GRAD_GROUPS = {"ffn2": GROUPS["ffn2"], "mix": GROUPS["mix"], "ffn1_down": ["ffn1_w_down"], "ffn1_gu": ["ffn1_w_gu"]}


def _local_step(x, mem, positions, loss_target, P, weights, grads_out):
    T = x.shape[0]
    G = {}
    W = dict(weights("ffn1", None))

    half = MLA_ROPE // 2
    inv = ROPE_BASE ** (-jnp.arange(half, dtype=F32) / half)
    ang = positions.astype(F32)[:, None] * inv
    cos, sin = jnp.cos(ang), jnp.sin(ang)
    one, zero = jnp.ones((T, MLA_NOPE), F32), jnp.zeros((T, half), F32)
    pad = LANES - MLA_QK
    tabs = (jnp.concatenate([one, cos, cos, jnp.ones((T, pad), F32)], axis=1),
            jnp.concatenate([jnp.zeros((T, MLA_NOPE), F32), -sin, zero, jnp.zeros((T, pad), F32)], axis=1),
            jnp.concatenate([jnp.zeros((T, MLA_NOPE), F32), zero, sin, jnp.zeros((T, pad), F32)], axis=1))
    gq_p = jnp.pad(P["mla_q_norm"], ((0, 0), (0, pad)))
    gk_p = jnp.pad(P["mla_k_norm"], ((0, 0), (0, pad)))
    bias_full = jnp.repeat(P["sg_b"].T, SG_GROUP_DIM, axis=1)
    group_ind = jnp.repeat(jnp.eye(SG_GROUPS, dtype=F32), SG_GROUP_DIM, axis=0)

    HT = (D_MODEL, BF16, "T")

    def norm2(x, g):
        h = _rms(x, g)
        return h, h

    h1, h1t = _rowwise(norm2, "ffn1_norm", 512, [x], [P["ffn1_norm"]], [(D_MODEL, BF16), HT])
    def ffn1_w_down(after):
        W.update(weights("ffn1_down", after))
        return W["ffn1_w_down"]

    gu1, a1t, o1 = _ffn_fwd(h1, W["ffn1_w_gu"], ffn1_w_down, "ffn1")

    def resid_norm(x, o, g):
        xn = x + 0.5 * o
        h = _rms(xn, g)
        return xn, h, h

    x1, hm, hmt = _rowwise(resid_norm, "mix_norm", 512, [x, o1], [P["mix_norm"]],
                           [(D_MODEL, F32), (D_MODEL, BF16), HT])
    W.update(weights("mix", hm))
    z = _mm(hm, W["w_in"], "nn", BF16, "w_in", tm=1024, tn=1792)

    y_a, y_at = _sg_fwd(z, P["sg_ln_g"], P["sg_ln_b"], P["sg_w"], bias_full)

    def c_norm(cq, ckv, gq, gkv):
        a, b = _rms(cq, gq), _rms(ckv, gkv)
        return a, b, a, b

    cqn, ckvn, cqnt, ckvnt = _rowwise(
        c_norm, "mla_cnorm", 512, [(z, MLA_Q_RANK, Z_CQ // MLA_Q_RANK), (z, MLA_KV_RANK, Z_CKV // MLA_KV_RANK)],
        [P["mla_cq_norm"], P["mla_ckv_norm"]],
        [(MLA_Q_RANK, BF16), (MLA_KV_RANK, BF16), (MLA_Q_RANK, BF16, "T"), (MLA_KV_RANK, BF16, "T")])
    q_pre = _mm(cqn, W["mla_w_uq"], "nn", F32, "mla_uq", tm=1024, tn=1024)
    kv_pre = _mm(ckvn, W["mla_w_ukv"], "nn", F32, "mla_ukv", tm=1024, tn=1024)
    q, k, v = _mla_post(q_pre, kv_pre, z, tabs, gq_p, gk_p)
    y_b, y_bt, lse = _flash_fwd(q, k, v)

    memn, = _rowwise(lambda m, g: _rms(m, g), "mem_norm", 256, [mem], [P["mem_norm"]], [(D_MODEL, BF16)])
    kvm = _mm(memn, W["mem_w_kv"], "nn", F32, "mem_kv")

    def mem_k(kvm, gk):
        ks = [_rms(kvm[:, h * LANES:(h + 1) * LANES], gk) for h in range(MEM_HEADS)]
        return jnp.concatenate(ks, axis=1), kvm[:, MEM_WIDTH:]

    km, vm = _rowwise(mem_k, "mem_knorm", 256, [kvm], [P["mem_k_norm"]], [(MEM_WIDTH, BF16), (MEM_WIDTH, BF16)])
    y_c, y_ct = _mem_fwd(z, km, vm, P["mem_q_norm"])

    pa = _mm(y_a, W["w_branch_a"], "nn", BF16, "branch_a", tm=1024, tn=1024)
    pb = _mm(y_b, W["w_branch_b"], "nn", BF16, "branch_b", tm=1024, tn=1024)
    pc = _mm(y_c, W["w_branch_c"], "nn", BF16, "branch_c", tm=1024, tn=1024)

    def merge(zg, pa, pb, pc, b):
        g = _sigmoid(zg + b)
        m = g[:, :D_MODEL] * pa + g[:, D_MODEL:2 * D_MODEL] * pb + g[:, 2 * D_MODEL:] * pc
        return m, m

    merged, mergedt = _rowwise(merge, "merge", 256, [(z, 3 * D_MODEL, 0), pa, pb, pc], [P["b_gate"]],
                               [(D_MODEL, BF16), HT])
    om = _mm(merged, W["w_out"], "nn", F32, "w_out", tm=1024, tn=1024)

    def resid_norm1(x, o, g):
        xn = x + o
        h = _rms(xn, g)
        return xn, h, h

    x2, h2, h2t = _rowwise(resid_norm1, "ffn2_norm", 512, [x1, om], [P["ffn2_norm"]],
                           [(D_MODEL, F32), (D_MODEL, BF16), HT])
    W.update(weights("ffn2", h2))
    gu2, a2t, o2 = _ffn_fwd(h2, W["ffn2_w_gu"], W["ffn2_w_down"], "ffn2")

    def loss_fn(x2, o2, t):
        e = x2 + 0.5 * o2 - t
        return e * (1.0 / D_MODEL), (e * (0.5 / D_MODEL)).astype(BF16), _rsum(e * e) * (0.5 / D_MODEL)

    dx3, do2, loss_part = _rowwise(loss_fn, "loss", 512, [x2, o2, loss_target], [],
                                   [(D_MODEL, F32), (D_MODEL, BF16)], [((1, D_MODEL), F32)])

    dh2, G["ffn2_w_gu"], G["ffn2_w_down"] = _ffn_bwd(do2, h2t, gu2, a2t, W["ffn2_w_gu"], W["ffn2_w_down"], "ffn2")
    tie = grads_out("ffn2", G)

    def norm_bwd(x, dh, dxo, g, *_):
        dx, dg = _rms_bwd(x, g, dh)
        dx = dx + dxo
        return dx, dx, dg

    dx2, dx2b, G["ffn2_norm"] = _rowwise(norm_bwd, "ffn2_norm_bwd", 512, [x2, dh2, dx3],
                                         [P["ffn2_norm"]] + ([] if tie is None else [tie]),
                                         [(D_MODEL, F32), (D_MODEL, BF16)], [((1, D_MODEL), F32)])

    G["w_out"] = _mm_t(mergedt, dx2b, "w_out_dw", tm=1024, tn=1024)
    dmerged = _mm(dx2b, W["w_out"], "nt", F32, "w_out_dx", tm=1024, tn=1024)

    def merge_bwd(zg, pa, pb, pc, dm, b):
        g = _sigmoid(zg + b)
        ps = jnp.concatenate([pa, pb, pc], axis=1)
        dm3 = jnp.concatenate([dm, dm, dm], axis=1)
        dzg = dm3 * ps * g * (1.0 - g)
        dp = dm3 * g
        return dzg, dp[:, :D_MODEL], dp[:, D_MODEL:2 * D_MODEL], dp[:, 2 * D_MODEL:], _rsum(dzg)

    dzg, dpa, dpb, dpc, G["b_gate"] = _rowwise(
        merge_bwd, "merge_bwd", 256, [(z, 3 * D_MODEL, 0), pa, pb, pc, dmerged], [P["b_gate"]],
        [(3 * D_MODEL, BF16), (D_MODEL, BF16), (D_MODEL, BF16), (D_MODEL, BF16)], [((1, 3 * D_MODEL), F32)])

    G["w_branch_a"] = _mm_t(y_at, dpa, "branch_a_dw", tm=512, tn=1024)
    G["w_branch_b"] = _mm_t(y_bt, dpb, "branch_b_dw", tm=1024, tn=1024)
    G["w_branch_c"] = _mm_t(y_ct, dpc, "branch_c_dw", tm=512, tn=1024)
    dy_a = _mm(dpa, W["w_branch_a"], "nt", BF16, "branch_a_dx", tm=1024, tn=512)
    dy_b = _mm(dpb, W["w_branch_b"], "nt", BF16, "branch_b_dx", tm=1024, tn=1024)
    dy_c = _mm(dpc, W["w_branch_c"], "nt", BF16, "branch_c_dx", tm=1024, tn=512)

    du_pre, dv_pre, G["sg_w"], dbias_t, G["sg_ln_g"], G["sg_ln_b"] = _sg_bwd(
        z, dy_a, P["sg_ln_g"], P["sg_ln_b"], P["sg_w"], bias_full, group_ind)
    G["sg_b"] = dbias_t.T

    dqm, dkm, dvm, G["mem_q_norm"] = _mem_bwd(z, dy_c, km, vm, P["mem_q_norm"])

    def mem_k_bwd(kvm, dkm, dvm, gk):
        dks = []
        dg = jnp.zeros((1, LANES), F32)
        for h in range(MEM_HEADS):
            sl = slice(h * LANES, (h + 1) * LANES)
            dx, d = _rms_bwd(kvm[:, sl], gk, dkm[:, sl])
            dks.append(dx)
            dg = dg + d
        return jnp.concatenate(dks + [dvm], axis=1), dg

    dkvm, G["mem_k_norm"] = _rowwise(mem_k_bwd, "mem_knorm_bwd", 256, [kvm, dkm, dvm], [P["mem_k_norm"]],
                                     [(2 * MEM_WIDTH, BF16)], [((1, LANES), F32)])
    G["mem_w_kv"] = _mm(memn, dkvm, "tn", BF16, "mem_kv_dw")
    dmemn = _mm(dkvm, W["mem_w_kv"], "nt", F32, "mem_kv_dx")
    _, G["mem_norm"] = _rowwise(lambda m, d, g: _rms_bwd(m, g, d), "mem_norm_bwd", 256, [mem, dmemn],
                                [P["mem_norm"]], [(D_MODEL, BF16)], [((1, D_MODEL), F32)])

    def delta_fn(o, do):
        od = o.astype(F32) * do.astype(F32)
        ds = [jnp.broadcast_to(jnp.sum(od[:, h * LANES:(h + 1) * LANES], axis=1, keepdims=True), (od.shape[0], LANES))
              for h in range(MLA_HEADS)]
        return jnp.concatenate(ds, axis=1)

    delta, = _rowwise(delta_fn, "mla_delta", 512, [y_b, dy_b], [], [(HP, F32)])
    rowform = lambda a: a.reshape(T, MLA_HEADS, LANES)[:, :, 0].T.reshape(MLA_HEADS, 1, T)
    dq = _flash_dq(q, k, v, dy_b, lse, delta)
    dk, dv = _flash_dkv(q, k, v, dy_b, rowform(lse), rowform(delta))
    dq_pre, dkv_pre, dkr, dgq, dgk = _mla_post_bwd(q_pre, kv_pre, z, tabs, gq_p, gk_p, dq, dk, dv)
    G["mla_q_norm"], G["mla_k_norm"] = dgq[:, :MLA_QK], dgk[:, :MLA_QK]
    G["mla_w_uq"] = _mm_t(cqnt, dq_pre, "mla_uq_dw", tm=384, tn=1024)
    G["mla_w_ukv"] = _mm_t(ckvnt, dkv_pre, "mla_ukv_dw", tm=256, tn=2048)
    dcqn = _mm(dq_pre, W["mla_w_uq"], "nt", F32, "mla_uq_dx", tm=1024)
    dckvn = _mm(dkv_pre, W["mla_w_ukv"], "nt", F32, "mla_ukv_dx", tm=1024)

    def c_norm_bwd(cq, ckv, dcqn, dckvn, gq, gkv):
        dcq, dgq = _rms_bwd(cq, gq, dcqn)
        dckv, dgkv = _rms_bwd(ckv, gkv, dckvn)
        return dcq, dckv, dgq, dgkv

    dcq, dckv, G["mla_cq_norm"], G["mla_ckv_norm"] = _rowwise(
        c_norm_bwd, "mla_cnorm_bwd", 512,
        [(z, MLA_Q_RANK, Z_CQ // MLA_Q_RANK), (z, MLA_KV_RANK, Z_CKV // MLA_KV_RANK), dcqn, dckvn],
        [P["mla_cq_norm"], P["mla_ckv_norm"]], [(MLA_Q_RANK, BF16), (MLA_KV_RANK, BF16)],
        [((1, MLA_Q_RANK), F32), ((1, MLA_KV_RANK), F32)])

    dz = jnp.concatenate([dzg, du_pre, dv_pre, dqm, dckv, dkr, dcq], axis=1)
    G["w_in"] = _mm_t(hmt, dz, "w_in_dw", tm=1024, tn=1792)
    dhm = _mm(dz, W["w_in"], "nt", F32, "w_in_dx", tm=1024, tn=1024, tk=2688)

    def norm_bwd_half(x, dh, dxo, g):
        dx, dg = _rms_bwd(x, g, dh)
        dx = dx + dxo
        return dx, (0.5 * dx), dg

    dx1, do1, G["mix_norm"] = _rowwise(norm_bwd_half, "mix_norm_bwd", 512, [x1, dhm, dx2], [P["mix_norm"]],
                                       [(D_MODEL, F32), (D_MODEL, BF16)], [((1, D_MODEL), F32)])
    tie = grads_out("mix", G)

    def ffn1_dw(which, dw):
        G["ffn1_w_" + which] = dw
        return grads_out("ffn1_" + which, G)

    dh1, _, _ = _ffn_bwd(do1, h1t, gu1, a1t, W["ffn1_w_gu"], W["ffn1_w_down"], "ffn1", tie, ffn1_dw)

    def norm_bwd_last(x, dh, dxo, g):
        dx, dg = _rms_bwd(x, g, dh)
        return dx + dxo, dg

    grad_x, G["ffn1_norm"] = _rowwise(norm_bwd_last, "ffn1_norm_bwd", 512, [x, dh1, dx1], [P["ffn1_norm"]],
                                      [(D_MODEL, F32)], [((1, D_MODEL), F32)])
    return loss_part, grad_x, G


SHARDED = ["ffn1_w_gu", "ffn1_w_down", "w_in", "mla_w_uq", "mla_w_ukv", "mem_w_kv",
           "w_branch_a", "w_branch_b", "w_branch_c", "w_out", "ffn2_w_gu", "ffn2_w_down"]
ROW_SHARDED = {"ffn1_w_down", "mem_w_kv", "w_out", "ffn2_w_down"}
SMALL = ["ffn1_norm", "mix_norm", "b_gate", "sg_ln_g", "sg_ln_b", "sg_w", "sg_b", "mla_cq_norm",
         "mla_ckv_norm", "mla_q_norm", "mla_k_norm", "mem_norm", "mem_q_norm", "mem_k_norm", "ffn2_norm"]
ORDER = ["ffn1_norm", "ffn1_w_gu", "ffn1_w_down", "mix_norm", "w_in", "b_gate", "sg_ln_g", "sg_ln_b", "sg_w",
         "sg_b", "mla_cq_norm", "mla_w_uq", "mla_ckv_norm", "mla_w_ukv", "mla_q_norm", "mla_k_norm", "mem_norm",
         "mem_w_kv", "mem_q_norm", "mem_k_norm", "w_branch_a", "w_branch_b", "w_branch_c", "w_out", "ffn2_norm",
         "ffn2_w_gu", "ffn2_w_down"]

_IN_U, _IN_V, _IN_CQ, _IN_CKV, _IN_KR, _IN_QM, _IN_G = 0, 512, 1024, 1408, 1664, 1696, 2208
IN_COLS = 5280


def _full_from_slabs(name, slabs):
    n, r, c = slabs.shape
    if name in ROW_SHARDED:
        return slabs.reshape(n * r, c)
    return slabs.transpose(1, 0, 2).reshape(r, n * c)


def _slabs_from_full(name, full):
    if name in ROW_SHARDED:
        return full.reshape(N_DEV, full.shape[0] // N_DEV, full.shape[1])
    r, c = full.shape
    return full.reshape(r, N_DEV, c // N_DEV).transpose(1, 0, 2)


def _compute_layout(full):
    W = dict(full)
    if "w_in" not in full:
        return W
    w = full["w_in"]
    kr = jnp.pad(w[:, _IN_KR:_IN_QM], ((0, 0), (KR_LANE, LANES - KR_LANE - MLA_ROPE)))
    W["w_in"] = jnp.concatenate([w[:, _IN_G:], w[:, _IN_U:_IN_CQ], w[:, _IN_QM:_IN_G], w[:, _IN_CKV:_IN_KR], kr,
                                 w[:, _IN_CQ:_IN_CKV]], axis=1)
    uq = full["mla_w_uq"].reshape(MLA_Q_RANK, MLA_HEADS, MLA_QK)
    W["mla_w_uq"] = jnp.pad(uq, ((0, 0), (0, 0), (0, LANES - MLA_QK))).reshape(MLA_Q_RANK, HP)
    ukv = full["mla_w_ukv"].reshape(MLA_KV_RANK, MLA_HEADS, MLA_NOPE + MLA_V)
    padh = lambda a: jnp.pad(a, ((0, 0), (0, 0), (0, LANES - a.shape[2]))).reshape(MLA_KV_RANK, HP)
    W["mla_w_ukv"] = jnp.concatenate([padh(ukv[:, :, :MLA_NOPE]), padh(ukv[:, :, MLA_NOPE:])], axis=1)
    wb = full["w_branch_b"].reshape(MLA_HEADS, MLA_V, D_MODEL)
    W["w_branch_b"] = jnp.pad(wb, ((0, 0), (0, LANES - MLA_V), (0, 0))).reshape(HP, D_MODEL)
    return W


def _reference_layout(G):
    out = dict(G)
    if "w_in" not in G:
        return out
    g = G["w_in"]
    out["w_in"] = jnp.concatenate([
        g[:, Z_U:Z_QM], g[:, Z_CQ:Z_COLS], g[:, Z_CKV:Z_KR], g[:, Z_KR + KR_LANE:Z_KR + KR_LANE + MLA_ROPE],
        g[:, Z_QM:Z_CKV], g[:, Z_G:Z_U]], axis=1)
    out["mla_w_uq"] = G["mla_w_uq"].reshape(MLA_Q_RANK, MLA_HEADS, LANES)[:, :, :MLA_QK].reshape(MLA_Q_RANK, -1)
    gk = G["mla_w_ukv"][:, :HP].reshape(MLA_KV_RANK, MLA_HEADS, LANES)[:, :, :MLA_NOPE]
    gv = G["mla_w_ukv"][:, HP:].reshape(MLA_KV_RANK, MLA_HEADS, LANES)[:, :, :MLA_V]
    out["mla_w_ukv"] = jnp.concatenate([gk, gv], axis=2).reshape(MLA_KV_RANK, -1)
    out["w_branch_b"] = G["w_branch_b"].reshape(MLA_HEADS, LANES, D_MODEL)[:, :MLA_V].reshape(-1, D_MODEL)
    return out


def _pack(parts):
    flat = []
    for a in parts:
        a = a.reshape(-1)
        flat.append(jnp.pad(a, (0, (-a.shape[0]) % LANES)))
    return jnp.concatenate(flat).reshape(-1, LANES)


def _unpack(packed, shapes):
    flat = packed.reshape(-1)
    out, off = [], 0
    for shp in shapes:
        n = int(np.prod(shp))
        out.append(flat[off:off + n].reshape(shp))
        off += n + (-n) % LANES
    return out


MESH = pl.DeviceIdType.MESH
HBM = pl.BlockSpec(memory_space=pltpu.HBM)


def _all_gather(shards):
    n = len(shards)

    def body(*refs):
        x_refs, out_refs, token_ref = refs[:n], refs[n:2 * n], refs[2 * n]
        send_sems, recv_sems, local_sems = refs[2 * n + 1:]
        x, y, c = lax.axis_index("x"), lax.axis_index("y"), lax.axis_index("c")
        me, sibling = (x, y, c), (x, y, 1 - c)
        chips = [(1 - x, y), (x, 1 - y), (1 - x, 1 - y)]
        token_ref[...] = jnp.zeros_like(token_ref)

        def slot(a, px, py, pc):
            return out_refs[a].at[4 * px + 2 * py + pc]

        def copy(a, k, block, to, src=None):
            return pltpu.make_async_remote_copy(
                src_ref=slot(a, *block) if src is None else src, dst_ref=slot(a, *block),
                send_sem=send_sems.at[7 * a + k], recv_sem=recv_sems.at[7 * a + k], device_id=to, device_id_type=MESH)

        arrays = range(n)
        mine = [pltpu.make_async_copy(x_refs[a], slot(a, *me), local_sems.at[a]) for a in arrays]
        for cp in mine:
            cp.start()
        first = [copy(a, 0, me, sibling, src=x_refs[a]) for a in arrays]
        first += [copy(a, 1 + j, me, (*chip, c), src=x_refs[a]) for j, chip in enumerate(chips) for a in arrays]
        for cp in first:
            cp.start()
        passed = []
        for j, chip in enumerate(chips):
            for a in arrays:
                copy(a, 1 + j, (*chip, c), me).wait_recv()
                passed.append(copy(a, 4 + j, (*chip, c), sibling))
                passed[-1].start()
        for a in arrays:
            copy(a, 0, sibling, me).wait_recv()
        for j, chip in enumerate(chips):
            for a in arrays:
                copy(a, 4 + j, (*chip, 1 - c), me).wait_recv()
        for cp in first + passed:
            cp.wait_send()
        for cp in mine:
            cp.wait()

    res = pl.pallas_call(
        body, name="all_gather_weights",
        out_shape=[jax.ShapeDtypeStruct((N_DEV,) + s.shape, s.dtype) for s in shards]
        + [jax.ShapeDtypeStruct((8, LANES), F32)],
        in_specs=[HBM] * n, out_specs=[HBM] * n + [pl.BlockSpec(memory_space=pltpu.VMEM)],
        scratch_shapes=[pltpu.SemaphoreType.DMA((7 * n,)), pltpu.SemaphoreType.DMA((7 * n,)),
                        pltpu.SemaphoreType.DMA((n,))],
    )(*shards)
    return res[:n], res[n]


SEM = pl.BlockSpec(memory_space=pltpu.SEMAPHORE)
DATAFLOW = pltpu.SideEffectType.DATAFLOW_SIDE_EFFECTING


def _peers():
    x, y, c = lax.axis_index("x"), lax.axis_index("y"), lax.axis_index("c")
    out = []
    for k in range(1, N_DEV):
        px = 1 - x if k & 4 else x
        py = 1 - y if k & 2 else y
        pc = 1 - c if k & 1 else c
        out.append((k, (px, py, pc), 4 * px + 2 * py + pc))
    return 4 * x + 2 * y + c, out


def _send_start(srcs, per_peer, name):
    n = len(srcs)
    lands = [lax.empty((N_DEV,) + (s.shape[1:] if per_peer else s.shape), s.dtype) for s in srcs]

    def body(*refs):
        src_refs, land_refs, send_sems, recv_sems, token = refs[:n], refs[n:2 * n], refs[2 * n], refs[2 * n + 1], refs[-1]
        me, peers = _peers()
        for a in range(n):
            for k, pid, pflat in peers:
                pltpu.make_async_remote_copy(
                    src_ref=src_refs[a].at[pflat] if per_peer else src_refs[a], dst_ref=land_refs[a].at[me],
                    send_sem=send_sems.at[7 * a + k - 1], recv_sem=recv_sems.at[7 * a + k - 1],
                    device_id=pid, device_id_type=MESH).start()
        token[...] = jnp.zeros_like(token)

    hbm = lambda a: pltpu.with_memory_space_constraint(a, pltpu.HBM)
    res = pl.pallas_call(
        body, name=name,
        out_shape=(pltpu.SemaphoreType.DMA((7 * n,)), pltpu.SemaphoreType.DMA((7 * n,)),
                   *[pltpu.HBM(a.shape, a.dtype) for a in srcs + lands], jax.ShapeDtypeStruct((8, LANES), F32)),
        in_specs=(HBM,) * (2 * n), out_specs=(SEM, SEM) + (HBM,) * (2 * n) + (pl.BlockSpec(memory_space=pltpu.VMEM),),
        input_output_aliases={i: 2 + i for i in range(2 * n)},
        compiler_params=pltpu.CompilerParams(has_side_effects=DATAFLOW),
    )(*[hbm(a) for a in srcs + lands])
    return (res[0], res[1], list(res[2:2 + n]), list(res[2 + n:2 + 2 * n])), res[-1]


def _send_wait(started, after, per_peer, name):
    send_sems, recv_sems, srcs_thru, lands_thru = started
    n = len(srcs_thru)

    def body(*refs):
        src_refs, land_refs, send_sems, recv_sems = refs[:n], refs[n:2 * n], refs[2 * n], refs[2 * n + 1]
        me, peers = _peers()
        for a in range(n):
            for k, pid, pflat in peers:
                copy = pltpu.make_async_remote_copy(
                    src_ref=src_refs[a].at[pflat] if per_peer else src_refs[a], dst_ref=land_refs[a].at[pflat],
                    send_sem=send_sems.at[7 * a + k - 1], recv_sem=recv_sems.at[7 * a + k - 1],
                    device_id=pid, device_id_type=MESH)
                copy.wait_send()
                copy.wait_recv()

    outs = pl.pallas_call(
        body, name=name,
        out_shape=tuple(pltpu.HBM(a.shape, a.dtype) for a in srcs_thru + lands_thru),
        in_specs=(HBM,) * (2 * n) + (SEM, SEM, pl.BlockSpec(memory_space=pl.ANY)), out_specs=(HBM,) * (2 * n),
        input_output_aliases={i: i for i in range(2 * n)},
        compiler_params=pltpu.CompilerParams(has_side_effects=DATAFLOW),
    )(*srcs_thru, *lands_thru, send_sems, recv_sems, after)
    me = 4 * lax.axis_index("x") + 2 * lax.axis_index("y") + lax.axis_index("c")
    landed = []
    for src_out, land in zip(outs[:n], outs[n:]):
        own = lax.dynamic_index_in_dim(src_out, me, 0, keepdims=True) if per_peer else src_out[None]
        landed.append(lax.dynamic_update_slice(land, own, (me,) + (0,) * (land.ndim - 1)))
    return landed


def _share_rows(block, name):
    def body(src_ref, out_ref, send_sems, recv_sems, local_sem):
        me, peers = _peers()
        own = pltpu.make_async_copy(src_ref, out_ref.at[me], local_sem)
        own.start()
        copies = [pltpu.make_async_remote_copy(
            src_ref=src_ref, dst_ref=out_ref.at[me], send_sem=send_sems.at[k - 1], recv_sem=recv_sems.at[k - 1],
            device_id=pid, device_id_type=MESH) for k, pid, _ in peers]
        for cp in copies:
            cp.start()
        for cp in copies:
            cp.wait()
        own.wait()

    return pl.pallas_call(
        body, name=name, out_shape=jax.ShapeDtypeStruct((N_DEV,) + block.shape, block.dtype),
        in_specs=[HBM], out_specs=HBM,
        scratch_shapes=[pltpu.SemaphoreType.DMA((N_DEV - 1,)), pltpu.SemaphoreType.DMA((N_DEV - 1,)),
                        pltpu.SemaphoreType.DMA],
    )(block)


def _sum_slots(recv, name, tr):
    n, rows, lanes = recv.shape
    tr = _tile(rows, tr)

    def body(r_ref, o_ref):
        acc = r_ref[0].astype(F32)
        for i in range(1, n):
            acc = acc + r_ref[i].astype(F32)
        o_ref[...] = acc

    return pl.pallas_call(
        body, name=name, grid=(rows // tr,),
        in_specs=[pl.BlockSpec((n, tr, lanes), lambda i: (0, i, 0))],
        out_specs=pl.BlockSpec((tr, lanes), lambda i: (i, 0)),
        out_shape=jax.ShapeDtypeStruct((rows, lanes), F32),
        compiler_params=_cparams(("parallel",)),
    )(recv)


def _adamw_math(w, g, m, v):
    m = ADAM_B1 * m + (1.0 - ADAM_B1) * g
    v = ADAM_B2 * v + (1.0 - ADAM_B2) * (g * g)
    m_hat = m / (1.0 - ADAM_B1 ** ADAM_STEP)
    v_hat = v / (1.0 - ADAM_B2 ** ADAM_STEP)
    return -ADAM_LR * (m_hat / (jnp.sqrt(v_hat) + ADAM_EPS) + ADAM_WD * w), m, v


def _adamw(w, g, m, v, name, tr=256):
    return _rowwise(_adamw_math, name, tr, [w, g, m, v], [], [(w.shape[1], F32)] * 3)


def _adamw_small(ws, gs, ms, vs):
    n = len(ws)

    def body(*refs):
        ins, outs = refs[:4 * n], refs[4 * n:]
        for i in range(n):
            d, m, v = _adamw_math(ins[i][...], ins[n + i][...], ins[2 * n + i][...], ins[3 * n + i][...])
            outs[i][...], outs[n + i][...], outs[2 * n + i][...] = d, m, v

    vmem = pl.BlockSpec(memory_space=pltpu.VMEM)
    res = pl.pallas_call(
        body, name="adamw_small", in_specs=[vmem] * (4 * n), out_specs=[vmem] * (3 * n),
        out_shape=[jax.ShapeDtypeStruct(w.shape, F32) for w in ws] * 3,
    )(*ws, *gs, *ms, *vs)
    return res[:n], res[n:2 * n], res[2 * n:]


def _sum_adamw(recv, w, m, v, name):
    n, r, c = recv.shape
    tr = _tile(r, 256)

    def body(r_ref, w_ref, m_ref, v_ref, g_ref, d_ref, nm_ref, nv_ref):
        g = r_ref[0].astype(F32)
        for i in range(1, n):
            g = g + r_ref[i].astype(F32)
        g_ref[...] = g
        d_ref[...], nm_ref[...], nv_ref[...] = _adamw_math(w_ref[...], g, m_ref[...], v_ref[...])

    row = pl.BlockSpec((tr, c), lambda i: (i, 0))
    return pl.pallas_call(
        body, name=name, grid=(r // tr,),
        in_specs=[pl.BlockSpec((n, tr, c), lambda i: (0, i, 0)), row, row, row], out_specs=[row] * 4,
        out_shape=[jax.ShapeDtypeStruct((r, c), F32)] * 4, compiler_params=_cparams(("parallel",)),
    )(recv, w, m, v)


def kernel(x, mem, positions, ffn1_norm, ffn1_w_gu, ffn1_w_down, mix_norm, w_in, b_gate, sg_ln_g, sg_ln_b, sg_w, sg_b, mla_cq_norm, mla_w_uq, mla_ckv_norm, mla_w_ukv, mla_q_norm, mla_k_norm, mem_norm, mem_w_kv, mem_q_norm, mem_k_norm, w_branch_a, w_branch_b, w_branch_c, w_out, ffn2_norm, ffn2_w_gu, ffn2_w_down, loss_target, m_ffn1_norm, m_ffn1_w_gu, m_ffn1_w_down, m_mix_norm, m_w_in, m_b_gate, m_sg_ln_g, m_sg_ln_b, m_sg_w, m_sg_b, m_mla_cq_norm, m_mla_w_uq, m_mla_ckv_norm, m_mla_w_ukv, m_mla_q_norm, m_mla_k_norm, m_mem_norm, m_mem_w_kv, m_mem_q_norm, m_mem_k_norm, m_w_branch_a, m_w_branch_b, m_w_branch_c, m_w_out, m_ffn2_norm, m_ffn2_w_gu, m_ffn2_w_down, v_ffn1_norm, v_ffn1_w_gu, v_ffn1_w_down, v_mix_norm, v_w_in, v_b_gate, v_sg_ln_g, v_sg_ln_b, v_sg_w, v_sg_b, v_mla_cq_norm, v_mla_w_uq, v_mla_ckv_norm, v_mla_w_ukv, v_mla_q_norm, v_mla_k_norm, v_mem_norm, v_mem_w_kv, v_mem_q_norm, v_mem_k_norm, v_w_branch_a, v_w_branch_b, v_w_branch_c, v_w_out, v_ffn2_norm, v_ffn2_w_gu, v_ffn2_w_down):
    given = dict(locals())
    wts = {n: given[n] for n in ORDER}
    mom = {n: given["m_" + n] for n in ORDER}
    var = {n: given["v_" + n] for n in ORDER}

    def shards(group, zero):
        out = [wts[n][0].astype(BF16) for n in GROUPS[group]]
        return [out[0] + zero.astype(BF16)] + out[1:]

    def full_weights(group, slabs):
        return _compute_layout({n: _full_from_slabs(n, s) for n, s in zip(GROUPS[group], slabs)})

    def zero_of(a):
        return jnp.minimum(jnp.abs(a.reshape(-1)[0]), 0)

    gathered_ffn1, token = _all_gather([wts[n][0].astype(BF16) for n in GROUPS["ffn1"]])
    flight = {}
    flight["ffn1_down"], token = _send_start(shards("ffn1_down", token[0, 0]), False, "gather_ffn1_down_start")
    flight["mix"] = _send_start(shards("mix", token[0, 0]), False, "gather_mix_start")[0]
    recv = {}

    def weights(group, after):
        if group == "ffn1":
            return full_weights(group, gathered_ffn1)
        landed = _send_wait(flight.pop(group), after, False, f"gather_{group}_wait")
        if group == "mix":
            flight["ffn2"] = _send_start(shards("ffn2", zero_of(landed[0])), False, "gather_ffn2_start")[0]
        return full_weights(group, landed)

    small_shapes = [wts[n].shape[1:] for n in SMALL]
    early = SMALL[1:]
    assert SMALL[0] == "ffn1_norm"

    def grads_out(group, G):
        Gr = _reference_layout({n: G[n] for n in GRAD_GROUPS[group]})
        parts = [_slabs_from_full(n, Gr[n]).astype(BF16) for n in GRAD_GROUPS[group]]
        flight["g_" + group], tie = _send_start(parts, True, f"grads_{group}_start")
        if group == "mix":
            small = _pack([G[n].reshape(s) for n, s in zip(early, small_shapes[1:])])
            small = jnp.pad(small, ((0, (-small.shape[0]) % 8), (0, 0)))
            flight["small"], tie = _send_start([small + tie[0, 0]], False, "grads_small_start")
        return tie

    P = {n: wts[n] if wts[n].ndim == 2 else wts[n][0] for n in SMALL}
    loss_part, grad_x, G = _local_step(x[0], mem[0], positions[0], loss_target[0], P, weights, grads_out)

    for group, names in GRAD_GROUPS.items():
        recv.update(zip(names, _send_wait(flight.pop("g_" + group), grad_x, True, f"grads_{group}_wait")))
    early_recv, = _send_wait(flight.pop("small"), grad_x, False, "grads_small_wait")
    last = _share_rows(G["ffn1_norm"].reshape(-1, LANES), "share_ffn1_norm")
    g_small_packed = _sum_slots(jnp.concatenate([last, early_recv], axis=1), "sum_small", 2048)

    grads, delta, new_m, new_v = {}, {}, {}, {}
    for n in SHARDED:
        grads[n], delta[n], new_m[n], new_v[n] = _sum_adamw(recv[n], wts[n][0], mom[n][0], var[n][0], "adamw_" + n)
    grads.update(zip(SMALL, _unpack(g_small_packed, small_shapes)))

    flat2 = lambda d: [d[n].reshape(-1, d[n].shape[-1]) for n in SMALL]
    for dst, vals in zip((delta, new_m, new_v), _adamw_small(flat2(wts), flat2(grads), flat2(mom), flat2(var))):
        dst.update(zip(SMALL, vals))

    loss = lax.psum(jnp.sum(loss_part), ("x", "y", "c"))
    lead = lambda d: [d[n].reshape(wts[n].shape) for n in ORDER]
    return (loss, grad_x[None], *lead(grads), *lead(delta), *lead(new_m), *lead(new_v))
```

```python
import functools

import numpy as np
import jax
import jax.numpy as jnp
from jax import lax
from jax.experimental import pallas as pl
from jax.experimental.pallas import tpu as pltpu

F32, BF16 = jnp.float32, jnp.bfloat16

D_MODEL = 1024
SG_GROUPS, SG_GROUP_DIM, SG_WIDTH, CHUNK = 8, 64, 512, 128
MLA_HEADS, MLA_NOPE, MLA_ROPE, MLA_V, MLA_QK = 8, 64, 32, 64, 96
MLA_Q_RANK, MLA_KV_RANK = 384, 256
MEM_HEADS, MEM_HEAD_DIM, MEM_WIDTH = 4, 128, 512
D_FF = 2816
ROPE_BASE = 10000.0
EPS = 1e-6
NEG = -1e30
ADAM_LR, ADAM_B1, ADAM_B2, ADAM_EPS, ADAM_WD, ADAM_STEP = 0.001, 0.9, 0.999, 1e-08, 0.01, 10

N_DEV = 8
LANES = 128
V7X_VMEM_LIMIT = 56 * 1024 * 1024
HP = MLA_HEADS * LANES

Z_G, Z_U, Z_V, Z_QM, Z_CKV, Z_KR, Z_CQ = 0, 3072, 3584, 4096, 4608, 4864, 4992
Z_COLS = 5376
KR_LANE = 64


def _tile(dim, pref):
    if dim <= pref:
        return dim
    for t in range(pref - pref % LANES, LANES - 1, -LANES):
        if dim % t == 0:
            return t
    for t in range(pref - pref % 8, 7, -8):
        if dim % t == 0:
            return t
    return dim


def _cparams(sem):
    return pltpu.CompilerParams(dimension_semantics=sem, vmem_limit_bytes=V7X_VMEM_LIMIT)


_DN = {"nn": ((1,), (0,)), "nt": ((1,), (1,)), "tn": ((0,), (0,))}


def _dot(a, b, mode="nn"):
    return lax.dot_general(a.astype(BF16), b.astype(BF16), (_DN[mode], ((), ())),
                           preferred_element_type=F32)


def _mm(a, b, mode, out_dtype, name, tm=512, tn=512, tk=2048, tie=None):
    if mode == "tn":
        K, M = a.shape
    else:
        M, K = a.shape
    N = b.shape[0] if mode == "nt" else b.shape[1]
    tm, tn, tk = _tile(M, tm), _tile(N, tn), _tile(K, tk)
    nk = K // tk
    if mode == "tn":
        a_spec = pl.BlockSpec((tk, tm), lambda i, j, k: (k, i))
    else:
        a_spec = pl.BlockSpec((tm, tk), lambda i, j, k: (i, k))
    if mode == "nt":
        b_spec = pl.BlockSpec((tn, tk), lambda i, j, k: (j, k))
    else:
        b_spec = pl.BlockSpec((tk, tn), lambda i, j, k: (k, j))

    ties = [] if tie is None else [tie]

    def body(a_ref, b_ref, *rest):
        o_ref, *scratch = rest[len(ties):]
        p = _dot(a_ref[...], b_ref[...], mode)
        if nk == 1:
            o_ref[...] = p.astype(o_ref.dtype)
        else:
            acc_ref, = scratch
            k = pl.program_id(2)

            @pl.when(k == 0)
            def _():
                acc_ref[...] = p

            @pl.when(k > 0)
            def _():
                acc_ref[...] += p

            @pl.when(k == nk - 1)
            def _():
                o_ref[...] = acc_ref[...].astype(o_ref.dtype)

    return pl.pallas_call(
        body, name=name, grid=(M // tm, N // tn, nk),
        in_specs=[a_spec, b_spec] + [pl.BlockSpec(t.shape, lambda i, j, k: (0, 0)) for t in ties],
        out_specs=pl.BlockSpec((tm, tn), lambda i, j, k: (i, j)),
        out_shape=jax.ShapeDtypeStruct((M, N), out_dtype),
        scratch_shapes=[] if nk == 1 else [pltpu.VMEM((tm, tn), F32)],
        compiler_params=_cparams(("parallel", "parallel", "arbitrary")),
    )(a, b, *ties)


def _mm_t(at, b, name, tm, tn, tk=1024, tie=None):
    return _mm(at, b, "nn", BF16, name, tm=tm, tn=tn, tk=tk, tie=tie)


def _rowwise(fn, name, tr, row_ins, bc_ins, row_outs, acc_outs=()):
    norm = [it if isinstance(it, tuple) else (it, it.shape[1], 0) for it in row_ins]
    rows = norm[0][0].shape[0]
    tr = _tile(rows, tr)
    arrays, in_specs = [], []
    for arr, w, cb in norm:
        arrays.append(arr)
        in_specs.append(pl.BlockSpec((tr, w), lambda i, cb=cb: (i, cb)))
    for arr in bc_ins:
        arrays.append(arr)
        in_specs.append(pl.BlockSpec(arr.shape, lambda i, nd=arr.ndim: (0,) * nd))
    out_shape, out_specs = [], []
    transposed = [len(o) == 3 for o in row_outs]
    for (w, dt, *_), t in zip(row_outs, transposed):
        out_shape.append(jax.ShapeDtypeStruct((w, rows) if t else (rows, w), dt))
        out_specs.append(pl.BlockSpec((w, tr), lambda i: (0, i)) if t else pl.BlockSpec((tr, w), lambda i: (i, 0)))
    for shp, dt in acc_outs:
        out_shape.append(jax.ShapeDtypeStruct(shp, dt))
        out_specs.append(pl.BlockSpec(shp, lambda i, nd=len(shp): (0,) * nd))
    n_in, n_row = len(arrays), len(row_outs)

    def body(*refs):
        vals = fn(*[r[...].astype(F32) for r in refs[:n_in]])
        if not isinstance(vals, (tuple, list)):
            vals = (vals,)
        outs = refs[n_in:]
        for r, v, t in zip(outs[:n_row], vals[:n_row], transposed):
            r[...] = v.astype(F32).T.astype(r.dtype) if t else v.astype(r.dtype)
        if acc_outs:
            accs = list(zip(outs[n_row:], vals[n_row:]))
            i = pl.program_id(0)

            @pl.when(i == 0)
            def _():
                for r, v in accs:
                    r[...] = v.astype(r.dtype)

            @pl.when(i > 0)
            def _():
                for r, v in accs:
                    r[...] += v.astype(r.dtype)

    res = pl.pallas_call(
        body, name=name, grid=(rows // tr,), in_specs=in_specs, out_specs=out_specs,
        out_shape=out_shape, compiler_params=_cparams(("arbitrary",)),
    )(*arrays)
    return res


def _rsum(x):
    return jnp.sum(x, axis=0, keepdims=True)


def _rms(x, g, n=None):
    n = x.shape[-1] if n is None else n
    r = lax.rsqrt(jnp.sum(x * x, axis=-1, keepdims=True) * (1.0 / n) + EPS)
    return x * r * g


def _rms_bwd(x, g, dy, n=None):
    n = x.shape[-1] if n is None else n
    r = lax.rsqrt(jnp.sum(x * x, axis=-1, keepdims=True) * (1.0 / n) + EPS)
    xh = x * r
    dxh = dy * g
    dx = r * (dxh - xh * (jnp.sum(dxh * xh, axis=-1, keepdims=True) * (1.0 / n)))
    return dx, _rsum(dy * xh)


def _gelu(x):
    return 0.5 * x * (1.0 + lax.erf(x * 0.7071067811865476))


def _gelu_grad(x):
    return 0.5 * (1.0 + lax.erf(x * 0.7071067811865476)) + x * jnp.exp(-0.5 * x * x) * 0.3989422804014327


def _sigmoid(x):
    return 1.0 / (1.0 + jnp.exp(-x))


FFN_TM, FFN_TN = 512, 1408
MXU_WIDTH = 256


def _col_chunks(n):
    return [(c, min(c + MXU_WIDTH, n)) for c in range(0, n, MXU_WIDTH)]


def _ffn_gu_act(h, w_gu, tag):
    T = h.shape[0]
    tm, tn = _tile(T, FFN_TM), FFN_TN
    nj = D_FF // tn

    def body(h_ref, wg_ref, wu_ref, gu_ref, a_ref, at_ref):
        h = h_ref[...]
        for c0, c1 in _col_chunks(tn):
            g = _dot(h, wg_ref[:, c0:c1])
            u = _dot(h, wu_ref[:, c0:c1])
            gu_ref[0, :, c0:c1] = g.astype(BF16)
            gu_ref[1, :, c0:c1] = u.astype(BF16)
            a = g * _sigmoid(g) * u
            a_ref[:, c0:c1] = a.astype(BF16)
            at_ref[c0:c1, :] = a.T.astype(BF16)

    return pl.pallas_call(
        body, name=f"{tag}_gu_act", grid=(T // tm, nj),
        in_specs=[pl.BlockSpec((tm, D_MODEL), lambda i, j: (i, 0)),
                  pl.BlockSpec((D_MODEL, tn), lambda i, j: (0, j)),
                  pl.BlockSpec((D_MODEL, tn), lambda i, j: (0, j + nj))],
        out_specs=[pl.BlockSpec((2, tm, tn), lambda i, j: (0, i, j)),
                   pl.BlockSpec((tm, tn), lambda i, j: (i, j)),
                   pl.BlockSpec((tn, tm), lambda i, j: (j, i))],
        out_shape=[jax.ShapeDtypeStruct((2, T, D_FF), BF16), jax.ShapeDtypeStruct((T, D_FF), BF16),
                   jax.ShapeDtypeStruct((D_FF, T), BF16)],
        compiler_params=_cparams(("parallel", "parallel")),
    )(h, w_gu, w_gu)


def _ffn_da_actbwd(do, w_down, gu, tag, tie=None):
    T = do.shape[0]
    tm, tn = _tile(T, FFN_TM), FFN_TN
    ties = [] if tie is None else [tie]

    def body(do_ref, wd_ref, gu_ref, *rest):
        dgu_ref = rest[-1]
        do = do_ref[...]
        for c0, c1 in _col_chunks(tn):
            da = _dot(do, wd_ref[c0:c1, :], "nt")
            g = gu_ref[0, :, c0:c1].astype(F32)
            u = gu_ref[1, :, c0:c1].astype(F32)
            s = _sigmoid(g)
            dgu_ref[0, :, c0:c1] = (da * u * s * (1.0 + g * (1.0 - s))).astype(BF16)
            dgu_ref[1, :, c0:c1] = (da * g * s).astype(BF16)

    return pl.pallas_call(
        body, name=f"{tag}_da_actbwd", grid=(T // tm, D_FF // tn),
        in_specs=[pl.BlockSpec((tm, D_MODEL), lambda i, j: (i, 0)),
                  pl.BlockSpec((tn, D_MODEL), lambda i, j: (j, 0)),
                  pl.BlockSpec((2, tm, tn), lambda i, j: (0, i, j))]
        + [pl.BlockSpec(t.shape, lambda i, j: (0, 0)) for t in ties],
        out_specs=pl.BlockSpec((2, tm, tn), lambda i, j: (0, i, j)),
        out_shape=jax.ShapeDtypeStruct((2, T, D_FF), BF16),
        compiler_params=_cparams(("parallel", "parallel")),
    )(do, w_down, gu, *ties)


def _ffn_dwgu(ht, dgu, tag, tk=1024):
    T = ht.shape[1]
    tn, tk = FFN_TN, _tile(T, tk)
    nj, nk = D_FF // tn, T // tk

    def body(a_ref, b_ref, o_ref, acc_ref):
        k = pl.program_id(1)
        p = _dot(a_ref[...], b_ref[...])

        @pl.when(k == 0)
        def _():
            acc_ref[...] = p

        @pl.when(k > 0)
        def _():
            acc_ref[...] += p

        @pl.when(k == nk - 1)
        def _():
            o_ref[...] = acc_ref[...].astype(o_ref.dtype)

    return pl.pallas_call(
        body, name=f"{tag}_dwgu", grid=(2 * nj, nk),
        in_specs=[pl.BlockSpec((D_MODEL, tk), lambda n, k: (0, k)),
                  pl.BlockSpec((None, tk, tn), lambda n, k: (n // nj, k, n % nj))],
        out_specs=pl.BlockSpec((D_MODEL, tn), lambda n, k: (0, n)),
        out_shape=jax.ShapeDtypeStruct((D_MODEL, 2 * D_FF), BF16),
        scratch_shapes=[pltpu.VMEM((D_MODEL, tn), F32)],
        compiler_params=_cparams(("parallel", "arbitrary")),
    )(ht, dgu)


def _ffn_dh(dgu, w_gu, tag, tm=1024, tie=None):
    T = dgu.shape[1]
    tm, tk = _tile(T, tm), FFN_TN
    nk = D_FF // tk
    ties = [] if tie is None else [tie]

    def body(a_ref, b_ref, *rest):
        o_ref, acc_ref = rest[len(ties):]
        k = pl.program_id(1)
        p = _dot(a_ref[...], b_ref[...], "nt")

        @pl.when(k == 0)
        def _():
            acc_ref[...] = p

        @pl.when(k > 0)
        def _():
            acc_ref[...] += p

        @pl.when(k == 2 * nk - 1)
        def _():
            o_ref[...] = acc_ref[...].astype(o_ref.dtype)

    return pl.pallas_call(
        body, name=f"{tag}_dh", grid=(T // tm, 2 * nk),
        in_specs=[pl.BlockSpec((None, tm, tk), lambda i, k: (k // nk, i, k % nk)),
                  pl.BlockSpec((D_MODEL, tk), lambda i, k: (0, k))]
        + [pl.BlockSpec(t.shape, lambda i, k: (0, 0)) for t in ties],
        out_specs=pl.BlockSpec((tm, D_MODEL), lambda i, k: (i, 0)),
        out_shape=jax.ShapeDtypeStruct((T, D_MODEL), BF16),
        scratch_shapes=[pltpu.VMEM((tm, D_MODEL), F32)],
        compiler_params=_cparams(("parallel", "arbitrary")),
    )(dgu, w_gu, *ties)


def _ffn_fwd(h, w_gu, w_down, tag):
    gu, a, at = _ffn_gu_act(h, w_gu, tag)
    if callable(w_down):
        w_down = w_down(at)
    o = _mm(a, w_down, "nn", BF16, f"{tag}_down", tm=1024, tn=1024, tk=2816)
    return gu, at, o


def _ffn_bwd(do, ht, gu, at, w_gu, w_down, tag, tie=None, on_dw=None):
    on_dw = on_dw or (lambda which, dw: None)
    dw_down = _mm_t(at, do, f"{tag}_dwdown", tm=1408, tn=1024, tie=tie)
    dgu = _ffn_da_actbwd(do, w_down, gu, tag, tie=on_dw("down", dw_down))
    dw_gu = _ffn_dwgu(ht, dgu, tag)
    dh = _ffn_dh(dgu, w_gu, tag, tie=on_dw("gu", dw_gu))
    return dh, dw_gu, dw_down


def _sg_common(u_pre, v_pre, ln_g, ln_b):
    u = _gelu(u_pre)
    v = _gelu(v_pre)
    mu = jnp.mean(v, axis=-1, keepdims=True)
    vc = v - mu
    rstd = lax.rsqrt(jnp.mean(vc * vc, axis=-1, keepdims=True) + EPS)
    vhat = vc * rstd
    vl = vhat * ln_g + ln_b
    return u, vhat, rstd, vl


def _sg_masked_pairs(w):
    t = lax.broadcasted_iota(jnp.int32, (CHUNK, CHUNK), 0)
    s = lax.broadcasted_iota(jnp.int32, (CHUNK, CHUNK), 1)
    causal = s <= t
    wm = [jnp.where(causal, w[g], 0.0).astype(BF16) for g in range(SG_GROUPS)]
    return [jnp.concatenate([wm[2 * j], wm[2 * j + 1]], axis=0) for j in range(SG_GROUPS // 2)], causal


def _sg_mix(vl, pairs, bias):
    tr = vl.shape[0]
    low = lax.broadcasted_iota(jnp.int32, (CHUNK, LANES), 1) < SG_GROUP_DIM
    vb = vl.astype(BF16)
    rows = []
    for c in range(tr // CHUNK):
        slabs = []
        for j in range(SG_GROUPS // 2):
            slab = vb[c * CHUNK:(c + 1) * CHUNK, j * LANES:(j + 1) * LANES]
            m = _dot(pairs[j], slab)
            slabs.append(jnp.where(low, m[:CHUNK], m[CHUNK:]))
        rows.append(jnp.concatenate(slabs, axis=1) + bias)
    return jnp.concatenate(rows, axis=0)


def _sg_fwd(z, ln_g, ln_b, sg_w, bias_full):
    def fn(u_pre, v_pre, ln_g, ln_b, w, bias):
        u, _, _, vl = _sg_common(u_pre, v_pre, ln_g, ln_b)
        pairs, _ = _sg_masked_pairs(w)
        y = u * _sg_mix(vl, pairs, bias)
        return y, y

    return _rowwise(fn, "sg_fwd", 512, [(z, SG_WIDTH, Z_U // SG_WIDTH), (z, SG_WIDTH, Z_V // SG_WIDTH)],
                    [ln_g, ln_b, sg_w, bias_full], [(SG_WIDTH, BF16), (SG_WIDTH, BF16, "T")])


def _sg_bwd(z, dy, ln_g, ln_b, sg_w, bias_full, group_ind):
    def fn(u_pre, v_pre, dy, ln_g, ln_b, w, bias, ind):
        dy = dy.astype(F32)
        u, vhat, rstd, vl = _sg_common(u_pre, v_pre, ln_g, ln_b)
        pairs, causal = _sg_masked_pairs(w)
        mixed = _sg_mix(vl, pairs, bias)
        du_pre = dy * mixed * _gelu_grad(u_pre)
        dmix = dy * u
        tr = dy.shape[0]
        low = lax.broadcasted_iota(jnp.int32, (CHUNK, LANES), 1) < SG_GROUP_DIM
        vb = vl.astype(BF16)
        dw = [jnp.zeros((CHUNK, CHUNK), F32) for _ in range(SG_GROUPS)]
        dbias = jnp.zeros((CHUNK, SG_WIDTH), F32)
        dvl_rows = []
        for c in range(tr // CHUNK):
            dm_c = dmix[c * CHUNK:(c + 1) * CHUNK]
            dbias = dbias + dm_c
            slabs = []
            for j in range(SG_GROUPS // 2):
                slab = vb[c * CHUNK:(c + 1) * CHUNK, j * LANES:(j + 1) * LANES]
                dm = dm_c[:, j * LANES:(j + 1) * LANES]
                d0 = jnp.where(low, dm, 0.0).astype(BF16)
                d1 = jnp.where(low, 0.0, dm).astype(BF16)
                dw[2 * j] = dw[2 * j] + _dot(d0, slab, "nt")
                dw[2 * j + 1] = dw[2 * j + 1] + _dot(d1, slab, "nt")
                slabs.append(_dot(pairs[j], jnp.concatenate([d0, d1], axis=0), "tn"))
            dvl_rows.append(jnp.concatenate(slabs, axis=1))
        dvl = jnp.concatenate(dvl_rows, axis=0)
        dln_g = _rsum(dvl * vhat)
        dln_b = _rsum(dvl)
        dvh = dvl * ln_g
        dv = rstd * (dvh - jnp.mean(dvh, axis=-1, keepdims=True)
                     - vhat * jnp.mean(dvh * vhat, axis=-1, keepdims=True))
        dv_pre = dv * _gelu_grad(v_pre)
        dw = jnp.stack([jnp.where(causal, d, 0.0) for d in dw], axis=0)
        dbias_t = lax.dot_general(dbias, ind, (((1,), (0,)), ((), ())), precision=lax.Precision.HIGHEST,
                                  preferred_element_type=F32)
        return du_pre, dv_pre, dw, dbias_t, dln_g, dln_b

    return _rowwise(fn, "sg_bwd", 512,
                    [(z, SG_WIDTH, Z_U // SG_WIDTH), (z, SG_WIDTH, Z_V // SG_WIDTH), dy],
                    [ln_g, ln_b, sg_w, bias_full, group_ind],
                    [(SG_WIDTH, BF16), (SG_WIDTH, BF16)],
                    [((SG_GROUPS, CHUNK, CHUNK), F32), ((CHUNK, SG_GROUPS), F32), ((1, SG_WIDTH), F32), ((1, SG_WIDTH), F32)])


def _rope(x, c, s1, s2):
    return x * c + pltpu.roll(x, LANES - MLA_ROPE // 2, 1) * s1 + pltpu.roll(x, MLA_ROPE // 2, 1) * s2


def _rope_t(d, c, s1, s2):
    return d * c + pltpu.roll(d * s1, MLA_ROPE // 2, 1) + pltpu.roll(d * s2, LANES - MLA_ROPE // 2, 1)


def _mla_post(q_pre, kv_pre, z, tabs, gq, gk):
    scale = MLA_QK ** -0.5 * LOG2E

    def fn(q_pre, k_pre, v_pre, kr, c, s1, s2, gq, gk):
        qs, ks = [], []
        for h in range(MLA_HEADS):
            sl = slice(h * LANES, (h + 1) * LANES)
            qs.append(_rope(_rms(q_pre[:, sl], gq, MLA_QK), c, s1, s2) * scale)
            ks.append(_rope(_rms(k_pre[:, sl] + kr, gk, MLA_QK), c, s1, s2))
        lane = lax.broadcasted_iota(jnp.int32, v_pre.shape, 1) & (LANES - 1)
        return jnp.concatenate(qs, axis=1), jnp.concatenate(ks, axis=1), jnp.where(lane == ONES_LANE, 1.0, v_pre)

    return _rowwise(fn, "mla_post", 256,
                    [q_pre, (kv_pre, HP, 0), (kv_pre, HP, 1), (z, LANES, Z_KR // LANES), *tabs],
                    [gq, gk], [(HP, BF16)] * 3)


def _mla_post_bwd(q_pre, kv_pre, z, tabs, gq, gk, dq, dk, dv):
    scale = MLA_QK ** -0.5

    def fn(q_pre, k_pre, kr, c, s1, s2, dq, dk, dv, gq, gk):
        lane = lax.broadcasted_iota(jnp.int32, (1, LANES), 1)
        kr_mask = (lane >= KR_LANE) & (lane < KR_LANE + MLA_ROPE)
        dqs, dks = [], []
        dgq = jnp.zeros((1, LANES), F32)
        dgk = jnp.zeros((1, LANES), F32)
        dkr = jnp.zeros(kr.shape, F32)
        for h in range(MLA_HEADS):
            sl = slice(h * LANES, (h + 1) * LANES)
            dqn = _rope_t(dq[:, sl].astype(F32), c, s1, s2) * scale
            dx, dg = _rms_bwd(q_pre[:, sl], gq, dqn, MLA_QK)
            dqs.append(dx)
            dgq = dgq + dg
            dkn = _rope_t(dk[:, sl].astype(F32), c, s1, s2)
            dx, dg = _rms_bwd(k_pre[:, sl] + kr, gk, dkn, MLA_QK)
            dks.append(dx)
            dgk = dgk + dg
            dkr = dkr + dx
        dkr = jnp.where(kr_mask, dkr, 0.0)
        dkv = jnp.concatenate(dks + [dv.astype(F32)], axis=1)
        return jnp.concatenate(dqs, axis=1), dkv, dkr, dgq, dgk

    return _rowwise(fn, "mla_post_bwd", 256,
                    [q_pre, (kv_pre, HP, 0), (z, LANES, Z_KR // LANES), *tabs, dq, dk, dv],
                    [gq, gk], [(HP, BF16), (2 * HP, BF16), (LANES, BF16)],
                    [((1, LANES), F32), ((1, LANES), F32)])


def _pairs(n, lower):
    a, b = [], []
    for o in range(n):
        inner = range(o + 1) if lower else range(o, n)
        for t in inner:
            a.append(o)
            b.append(t)
    return jnp.asarray(np.array(a, np.int32)), jnp.asarray(np.array(b, np.int32))


FLASH_TILE, FLASH_SUB_ROWS = 2048, 512
LOG2E, LN2 = 1.4426950408889634, 0.6931471805599453
ONES_LANE = MLA_V


def _flash_tiles(T):
    tq = _tile(T, FLASH_TILE)
    return tq, _tile(tq, FLASH_SUB_ROWS)


def _col_span(t, sr, rb, diag, key_major):
    if not diag:
        return 0, t
    return (rb * sr, t) if key_major else (0, (rb + 1) * sr)


def _span_iota(sr, rb, c0, c1):
    r = lax.broadcasted_iota(jnp.int32, (sr, c1 - c0), 0) + rb * sr
    c = lax.broadcasted_iota(jnp.int32, (sr, c1 - c0), 1) + c0
    return r, c


def _lanes(x, width):
    return jnp.concatenate([x] * (width // LANES), axis=1)


def _flash_fwd(q, k, v):
    T = q.shape[0]
    tq, sr = _flash_tiles(T)
    n = T // tq
    ii, jj = _pairs(n, True)

    def body(ii_ref, jj_ref, q_ref, k_ref, v_ref, o_ref, ot_ref, lse_ref, m_sc, acc_sc):
        p_ = pl.program_id(1)
        i, j = ii_ref[p_], jj_ref[p_]

        @pl.when(j == 0)
        def _():
            m_sc[...] = jnp.full(m_sc.shape, NEG, F32)
            acc_sc[...] = jnp.zeros(acc_sc.shape, F32)

        def tile(diag):
            for rb in range(tq // sr):
                rows = slice(rb * sr, (rb + 1) * sr)
                c0, c1 = _col_span(tq, sr, rb, diag, False)
                s = _dot(q_ref[rows, :], k_ref[c0:c1, :], "nt")
                if diag:
                    r, c = _span_iota(sr, rb, c0, c1)
                    s = jnp.where(c <= r, s, NEG)
                m = m_sc[rows, :]
                m_new = jnp.maximum(m, jnp.max(s, axis=1, keepdims=True))
                p = jnp.exp2(s - _lanes(m_new, c1 - c0))
                acc_sc[rows, :] = jnp.exp2(m - m_new) * acc_sc[rows, :] + _dot(p, v_ref[c0:c1, :])
                m_sc[rows, :] = m_new

        @pl.when(j < i)
        def _():
            tile(False)

        @pl.when(j == i)
        def _():
            tile(True)
            acc = acc_sc[...]
            lane = lax.broadcasted_iota(jnp.int32, acc.shape, 1)
            l = jnp.sum(jnp.where(lane == ONES_LANE, acc, 0.0), axis=1, keepdims=True)
            o = jnp.where(lane < MLA_V, acc / l, 0.0)
            o_ref[...] = o.astype(o_ref.dtype)
            ot_ref[...] = o.T.astype(ot_ref.dtype)
            lse_ref[...] = m_sc[...] + jnp.log2(l)

    blk = lambda which: pl.BlockSpec((tq, LANES), which)
    qmap = lambda h, p, ii, jj: (ii[p], h)
    kmap = lambda h, p, ii, jj: (jj[p], h)
    return pl.pallas_call(
        body, name="mla_flash_fwd",
        grid_spec=pltpu.PrefetchScalarGridSpec(
            num_scalar_prefetch=2, grid=(MLA_HEADS, int(ii.shape[0])),
            in_specs=[blk(qmap), blk(kmap), blk(kmap)],
            out_specs=[blk(qmap), pl.BlockSpec((LANES, tq), lambda h, p, ii, jj: (h, ii[p])), blk(qmap)],
            scratch_shapes=[pltpu.VMEM((tq, LANES), F32)] * 2),
        out_shape=[jax.ShapeDtypeStruct((T, HP), BF16), jax.ShapeDtypeStruct((HP, T), BF16),
                   jax.ShapeDtypeStruct((T, HP), F32)],
        compiler_params=_cparams(("parallel", "arbitrary")),
    )(ii, jj, q, k, v)


def _flash_dq(q, k, v, do, lse, delta):
    T = q.shape[0]
    tq, sr = _flash_tiles(T)
    n = T // tq
    ii, jj = _pairs(n, True)

    def body(ii_ref, jj_ref, q_ref, k_ref, v_ref, do_ref, lse_ref, dl_ref, dq_ref, acc_sc):
        p_ = pl.program_id(1)
        i, j = ii_ref[p_], jj_ref[p_]

        @pl.when(j == 0)
        def _():
            acc_sc[...] = jnp.zeros(acc_sc.shape, F32)

        def tile(diag):
            for rb in range(tq // sr):
                rows = slice(rb * sr, (rb + 1) * sr)
                c0, c1 = _col_span(tq, sr, rb, diag, False)
                ks = k_ref[c0:c1, :]
                p = jnp.exp2(_dot(q_ref[rows, :], ks, "nt") - _lanes(lse_ref[rows, :], c1 - c0))
                if diag:
                    r, c = _span_iota(sr, rb, c0, c1)
                    p = jnp.where(c <= r, p, 0.0)
                dp = _dot(do_ref[rows, :], v_ref[c0:c1, :], "nt")
                acc_sc[rows, :] += _dot(p * (dp - _lanes(dl_ref[rows, :], c1 - c0)), ks)

        @pl.when(j < i)
        def _():
            tile(False)

        @pl.when(j == i)
        def _():
            tile(True)
            dq_ref[...] = acc_sc[...].astype(dq_ref.dtype)

    blk = lambda which: pl.BlockSpec((tq, LANES), which)
    qmap = lambda h, p, ii, jj: (ii[p], h)
    kmap = lambda h, p, ii, jj: (jj[p], h)
    return pl.pallas_call(
        body, name="mla_flash_dq",
        grid_spec=pltpu.PrefetchScalarGridSpec(
            num_scalar_prefetch=2, grid=(MLA_HEADS, int(ii.shape[0])),
            in_specs=[blk(qmap), blk(kmap), blk(kmap), blk(qmap), blk(qmap), blk(qmap)],
            out_specs=blk(qmap),
            scratch_shapes=[pltpu.VMEM((tq, LANES), F32)]),
        out_shape=jax.ShapeDtypeStruct((T, HP), BF16),
        compiler_params=_cparams(("parallel", "arbitrary")),
    )(ii, jj, q, k, v, do, lse, delta)


def _flash_dkv(q, k, v, do, lse_row, delta_row):
    T = q.shape[0]
    tq, sr = _flash_tiles(T)
    n = T // tq
    jj, ii = _pairs(n, False)

    def body(jj_ref, ii_ref, q_ref, k_ref, v_ref, do_ref, lse_ref, dl_ref, dk_ref, dv_ref, dk_sc, dv_sc):
        p_ = pl.program_id(1)
        j, i = jj_ref[p_], ii_ref[p_]

        @pl.when(i == j)
        def _():
            dk_sc[...] = jnp.zeros(dk_sc.shape, F32)
            dv_sc[...] = jnp.zeros(dv_sc.shape, F32)

        def tile(diag):
            for rb in range(tq // sr):
                rows = slice(rb * sr, (rb + 1) * sr)
                c0, c1 = _col_span(tq, sr, rb, diag, True)
                qs, dos = q_ref[c0:c1, :], do_ref[c0:c1, :]
                pt = jnp.exp2(_dot(k_ref[rows, :], qs, "nt") - lse_ref[:, c0:c1])
                if diag:
                    r, c = _span_iota(sr, rb, c0, c1)
                    pt = jnp.where(r <= c, pt, 0.0)
                dpt = _dot(v_ref[rows, :], dos, "nt")
                dv_sc[rows, :] += _dot(pt, dos)
                dk_sc[rows, :] += _dot(pt * (dpt - dl_ref[:, c0:c1]), qs)

        @pl.when(i == j)
        def _():
            tile(True)

        @pl.when(i > j)
        def _():
            tile(False)

        @pl.when(i == n - 1)
        def _():
            dk_ref[...] = (dk_sc[...] * LN2).astype(dk_ref.dtype)
            dv_ref[...] = dv_sc[...].astype(dv_ref.dtype)

    blk = lambda which: pl.BlockSpec((tq, LANES), which)
    qmap = lambda h, p, jj, ii: (ii[p], h)
    kmap = lambda h, p, jj, ii: (jj[p], h)
    row = pl.BlockSpec((None, 1, tq), lambda h, p, jj, ii: (h, 0, ii[p]))
    return pl.pallas_call(
        body, name="mla_flash_dkv",
        grid_spec=pltpu.PrefetchScalarGridSpec(
            num_scalar_prefetch=2, grid=(MLA_HEADS, int(ii.shape[0])),
            in_specs=[blk(qmap), blk(kmap), blk(kmap), blk(qmap), row, row],
            out_specs=[blk(kmap), blk(kmap)],
            scratch_shapes=[pltpu.VMEM((tq, LANES), F32)] * 2),
        out_shape=[jax.ShapeDtypeStruct((T, HP), BF16)] * 2,
        compiler_params=_cparams(("parallel", "arbitrary")),
    )(jj, ii, q, k, v, do, lse_row, delta_row)


def _mem_fwd(z, km, vm, gq):
    scale = MEM_HEAD_DIM ** -0.5

    def fn(qm, km, vm, gq):
        ys = []
        for h in range(MEM_HEADS):
            sl = slice(h * LANES, (h + 1) * LANES)
            q = _rms(qm[:, sl], gq) * scale
            s = _dot(q, km[:, sl], "nt")
            p = jnp.exp(s - jnp.max(s, axis=1, keepdims=True))
            p = p / jnp.sum(p, axis=1, keepdims=True)
            ys.append(_dot(p, vm[:, sl]))
        y = jnp.concatenate(ys, axis=1)
        return y, y

    return _rowwise(fn, "mem_fwd", 512, [(z, MEM_WIDTH, Z_QM // MEM_WIDTH)], [km, vm, gq],
                    [(MEM_WIDTH, BF16), (MEM_WIDTH, BF16, "T")])


def _mem_bwd(z, dy, km, vm, gq):
    scale = MEM_HEAD_DIM ** -0.5

    def fn(qm, dy, km, vm, gq):
        dqs, dks, dvs = [], [], []
        dgq = jnp.zeros((1, LANES), F32)
        for h in range(MEM_HEADS):
            sl = slice(h * LANES, (h + 1) * LANES)
            q = (_rms(qm[:, sl], gq) * scale).astype(BF16)
            dyh = dy[:, sl]
            kh, vh = km[:, sl], vm[:, sl]
            s = _dot(q, kh, "nt")
            p = jnp.exp(s - jnp.max(s, axis=1, keepdims=True))
            p = p / jnp.sum(p, axis=1, keepdims=True)
            dp = _dot(dyh, vh, "nt")
            ds = p * (dp - jnp.sum(p * dp, axis=1, keepdims=True))
            dq = _dot(ds, kh) * scale
            dx, dg = _rms_bwd(qm[:, sl], gq, dq)
            dqs.append(dx)
            dgq = dgq + dg
            st = _dot(kh, q, "nt")
            pt = jnp.exp(st - jnp.max(st, axis=0, keepdims=True))
            pt = pt / jnp.sum(pt, axis=0, keepdims=True)
            dpt = _dot(vh, dyh, "nt")
            dst = pt * (dpt - jnp.sum(pt * dpt, axis=0, keepdims=True))
            dvs.append(_dot(pt, dyh))
            dks.append(_dot(dst, q))
        return jnp.concatenate(dqs, axis=1), jnp.concatenate(dks, axis=1), jnp.concatenate(dvs, axis=1), dgq

    m = km.shape[0]
    return _rowwise(fn, "mem_bwd", 512, [(z, MEM_WIDTH, Z_QM // MEM_WIDTH), dy], [km, vm, gq],
                    [(MEM_WIDTH, BF16)], [((m, MEM_WIDTH), F32), ((m, MEM_WIDTH), F32), ((1, LANES), F32)])


GROUPS = {"ffn1": ["ffn1_w_gu"], "ffn1_down": ["ffn1_w_down"],
          "mix": ["w_in", "mla_w_uq", "mla_w_ukv", "mem_w_kv", "w_branch_a", "w_branch_b", "w_branch_c", "w_out"],
          "ffn2": ["ffn2_w_gu", "ffn2_w_down"]}
GRAD_GROUPS = {"ffn2": GROUPS["ffn2"], "mix": GROUPS["mix"], "ffn1_down": ["ffn1_w_down"], "ffn1_gu": ["ffn1_w_gu"]}


def _local_step(x, mem, positions, loss_target, P, weights, grads_out):
    T = x.shape[0]
    G = {}
    W = dict(weights("ffn1", None))

    half = MLA_ROPE // 2
    inv = ROPE_BASE ** (-jnp.arange(half, dtype=F32) / half)
    ang = positions.astype(F32)[:, None] * inv
    cos, sin = jnp.cos(ang), jnp.sin(ang)
    one, zero = jnp.ones((T, MLA_NOPE), F32), jnp.zeros((T, half), F32)
    pad = LANES - MLA_QK
    tabs = (jnp.concatenate([one, cos, cos, jnp.ones((T, pad), F32)], axis=1),
            jnp.concatenate([jnp.zeros((T, MLA_NOPE), F32), -sin, zero, jnp.zeros((T, pad), F32)], axis=1),
            jnp.concatenate([jnp.zeros((T, MLA_NOPE), F32), zero, sin, jnp.zeros((T, pad), F32)], axis=1))
    gq_p = jnp.pad(P["mla_q_norm"], ((0, 0), (0, pad)))
    gk_p = jnp.pad(P["mla_k_norm"], ((0, 0), (0, pad)))
    bias_full = jnp.repeat(P["sg_b"].T, SG_GROUP_DIM, axis=1)
    group_ind = jnp.repeat(jnp.eye(SG_GROUPS, dtype=F32), SG_GROUP_DIM, axis=0)

    HT = (D_MODEL, BF16, "T")

    def norm2(x, g):
        h = _rms(x, g)
        return h, h

    h1, h1t = _rowwise(norm2, "ffn1_norm", 512, [x], [P["ffn1_norm"]], [(D_MODEL, BF16), HT])
    def ffn1_w_down(after):
        W.update(weights("ffn1_down", after))
        return W["ffn1_w_down"]

    gu1, a1t, o1 = _ffn_fwd(h1, W["ffn1_w_gu"], ffn1_w_down, "ffn1")

    def resid_norm(x, o, g):
        xn = x + 0.5 * o
        h = _rms(xn, g)
        return xn, h, h

    x1, hm, hmt = _rowwise(resid_norm, "mix_norm", 512, [x, o1], [P["mix_norm"]],
                           [(D_MODEL, F32), (D_MODEL, BF16), HT])
    W.update(weights("mix", hm))
    z = _mm(hm, W["w_in"], "nn", BF16, "w_in", tm=1024, tn=1792)

    y_a, y_at = _sg_fwd(z, P["sg_ln_g"], P["sg_ln_b"], P["sg_w"], bias_full)

    def c_norm(cq, ckv, gq, gkv):
        a, b = _rms(cq, gq), _rms(ckv, gkv)
        return a, b, a, b

    cqn, ckvn, cqnt, ckvnt = _rowwise(
        c_norm, "mla_cnorm", 512, [(z, MLA_Q_RANK, Z_CQ // MLA_Q_RANK), (z, MLA_KV_RANK, Z_CKV // MLA_KV_RANK)],
        [P["mla_cq_norm"], P["mla_ckv_norm"]],
        [(MLA_Q_RANK, BF16), (MLA_KV_RANK, BF16), (MLA_Q_RANK, BF16, "T"), (MLA_KV_RANK, BF16, "T")])
    q_pre = _mm(cqn, W["mla_w_uq"], "nn", BF16, "mla_uq", tm=1024, tn=1024)
    kv_pre = _mm(ckvn, W["mla_w_ukv"], "nn", BF16, "mla_ukv", tm=1024, tn=1024)
    q, k, v = _mla_post(q_pre, kv_pre, z, tabs, gq_p, gk_p)
    y_b, y_bt, lse = _flash_fwd(q, k, v)

    memn, = _rowwise(lambda m, g: _rms(m, g), "mem_norm", 256, [mem], [P["mem_norm"]], [(D_MODEL, BF16)])
    kvm = _mm(memn, W["mem_w_kv"], "nn", F32, "mem_kv")

    def mem_k(kvm, gk):
        ks = [_rms(kvm[:, h * LANES:(h + 1) * LANES], gk) for h in range(MEM_HEADS)]
        return jnp.concatenate(ks, axis=1), kvm[:, MEM_WIDTH:]

    km, vm = _rowwise(mem_k, "mem_knorm", 256, [kvm], [P["mem_k_norm"]], [(MEM_WIDTH, BF16), (MEM_WIDTH, BF16)])
    y_c, y_ct = _mem_fwd(z, km, vm, P["mem_q_norm"])

    pa = _mm(y_a, W["w_branch_a"], "nn", BF16, "branch_a", tm=1024, tn=1024)
    pb = _mm(y_b, W["w_branch_b"], "nn", BF16, "branch_b", tm=1024, tn=1024)
    pc = _mm(y_c, W["w_branch_c"], "nn", BF16, "branch_c", tm=1024, tn=1024)

    def merge(zg, pa, pb, pc, b):
        g = _sigmoid(zg + b)
        m = g[:, :D_MODEL] * pa + g[:, D_MODEL:2 * D_MODEL] * pb + g[:, 2 * D_MODEL:] * pc
        return m, m

    merged, mergedt = _rowwise(merge, "merge", 256, [(z, 3 * D_MODEL, 0), pa, pb, pc], [P["b_gate"]],
                               [(D_MODEL, BF16), HT])
    om = _mm(merged, W["w_out"], "nn", BF16, "w_out", tm=1024, tn=1024)

    def resid_norm1(x, o, g):
        xn = x + o
        h = _rms(xn, g)
        return xn, h, h

    x2, h2, h2t = _rowwise(resid_norm1, "ffn2_norm", 512, [x1, om], [P["ffn2_norm"]],
                           [(D_MODEL, F32), (D_MODEL, BF16), HT])
    W.update(weights("ffn2", h2))
    gu2, a2t, o2 = _ffn_fwd(h2, W["ffn2_w_gu"], W["ffn2_w_down"], "ffn2")

    def loss_fn(x2, o2, t):
        e = x2 + 0.5 * o2 - t
        return e * (1.0 / D_MODEL), (e * (0.5 / D_MODEL)).astype(BF16), _rsum(e * e) * (0.5 / D_MODEL)

    dx3, do2, loss_part = _rowwise(loss_fn, "loss", 512, [x2, o2, loss_target], [],
                                   [(D_MODEL, F32), (D_MODEL, BF16)], [((1, D_MODEL), F32)])

    dh2, G["ffn2_w_gu"], G["ffn2_w_down"] = _ffn_bwd(do2, h2t, gu2, a2t, W["ffn2_w_gu"], W["ffn2_w_down"], "ffn2")
    tie = grads_out("ffn2", G)

    def norm_bwd(x, dh, dxo, g, *_):
        dx, dg = _rms_bwd(x, g, dh)
        dx = dx + dxo
        return dx, dx, dg

    dx2, dx2b, G["ffn2_norm"] = _rowwise(norm_bwd, "ffn2_norm_bwd", 512, [x2, dh2, dx3],
                                         [P["ffn2_norm"]] + ([] if tie is None else [tie]),
                                         [(D_MODEL, F32), (D_MODEL, BF16)], [((1, D_MODEL), F32)])

    G["w_out"] = _mm_t(mergedt, dx2b, "w_out_dw", tm=1024, tn=1024)
    dmerged = _mm(dx2b, W["w_out"], "nt", BF16, "w_out_dx", tm=1024, tn=1024)

    def merge_bwd(zg, pa, pb, pc, dm, b):
        g = _sigmoid(zg + b)
        ps = jnp.concatenate([pa, pb, pc], axis=1)
        dm3 = jnp.concatenate([dm, dm, dm], axis=1)
        dzg = dm3 * ps * g * (1.0 - g)
        dp = dm3 * g
        return dzg, dp[:, :D_MODEL], dp[:, D_MODEL:2 * D_MODEL], dp[:, 2 * D_MODEL:], _rsum(dzg)

    dzg, dpa, dpb, dpc, G["b_gate"] = _rowwise(
        merge_bwd, "merge_bwd", 256, [(z, 3 * D_MODEL, 0), pa, pb, pc, dmerged], [P["b_gate"]],
        [(3 * D_MODEL, BF16), (D_MODEL, BF16), (D_MODEL, BF16), (D_MODEL, BF16)], [((1, 3 * D_MODEL), F32)])

    G["w_branch_a"] = _mm_t(y_at, dpa, "branch_a_dw", tm=512, tn=1024)
    G["w_branch_b"] = _mm_t(y_bt, dpb, "branch_b_dw", tm=1024, tn=1024)
    G["w_branch_c"] = _mm_t(y_ct, dpc, "branch_c_dw", tm=512, tn=1024)
    dy_a = _mm(dpa, W["w_branch_a"], "nt", BF16, "branch_a_dx", tm=1024, tn=512)
    dy_b = _mm(dpb, W["w_branch_b"], "nt", BF16, "branch_b_dx", tm=1024, tn=1024)
    dy_c = _mm(dpc, W["w_branch_c"], "nt", BF16, "branch_c_dx", tm=1024, tn=512)

    du_pre, dv_pre, G["sg_w"], dbias_t, G["sg_ln_g"], G["sg_ln_b"] = _sg_bwd(
        z, dy_a, P["sg_ln_g"], P["sg_ln_b"], P["sg_w"], bias_full, group_ind)
    G["sg_b"] = dbias_t.T

    dqm, dkm, dvm, G["mem_q_norm"] = _mem_bwd(z, dy_c, km, vm, P["mem_q_norm"])

    def mem_k_bwd(kvm, dkm, dvm, gk):
        dks = []
        dg = jnp.zeros((1, LANES), F32)
        for h in range(MEM_HEADS):
            sl = slice(h * LANES, (h + 1) * LANES)
            dx, d = _rms_bwd(kvm[:, sl], gk, dkm[:, sl])
            dks.append(dx)
            dg = dg + d
        return jnp.concatenate(dks + [dvm], axis=1), dg

    dkvm, G["mem_k_norm"] = _rowwise(mem_k_bwd, "mem_knorm_bwd", 256, [kvm, dkm, dvm], [P["mem_k_norm"]],
                                     [(2 * MEM_WIDTH, BF16)], [((1, LANES), F32)])
    G["mem_w_kv"] = _mm(memn, dkvm, "tn", BF16, "mem_kv_dw")
    dmemn = _mm(dkvm, W["mem_w_kv"], "nt", F32, "mem_kv_dx")
    _, G["mem_norm"] = _rowwise(lambda m, d, g: _rms_bwd(m, g, d), "mem_norm_bwd", 256, [mem, dmemn],
                                [P["mem_norm"]], [(D_MODEL, BF16)], [((1, D_MODEL), F32)])

    def delta_fn(o, do):
        od = o.astype(F32) * do.astype(F32)
        ds = [jnp.broadcast_to(jnp.sum(od[:, h * LANES:(h + 1) * LANES], axis=1, keepdims=True), (od.shape[0], LANES))
              for h in range(MLA_HEADS)]
        return jnp.concatenate(ds, axis=1)

    delta, = _rowwise(delta_fn, "mla_delta", 512, [y_b, dy_b], [], [(HP, F32)])
    rowform = lambda a: a.reshape(T, MLA_HEADS, LANES)[:, :, 0].T.reshape(MLA_HEADS, 1, T)
    dq = _flash_dq(q, k, v, dy_b, lse, delta)
    dk, dv = _flash_dkv(q, k, v, dy_b, rowform(lse), rowform(delta))
    dq_pre, dkv_pre, dkr, dgq, dgk = _mla_post_bwd(q_pre, kv_pre, z, tabs, gq_p, gk_p, dq, dk, dv)
    G["mla_q_norm"], G["mla_k_norm"] = dgq[:, :MLA_QK], dgk[:, :MLA_QK]
    G["mla_w_uq"] = _mm_t(cqnt, dq_pre, "mla_uq_dw", tm=384, tn=1024)
    G["mla_w_ukv"] = _mm_t(ckvnt, dkv_pre, "mla_ukv_dw", tm=256, tn=2048)
    dcqn = _mm(dq_pre, W["mla_w_uq"], "nt", BF16, "mla_uq_dx", tm=1024)
    dckvn = _mm(dkv_pre, W["mla_w_ukv"], "nt", BF16, "mla_ukv_dx", tm=1024)

    def c_norm_bwd(cq, ckv, dcqn, dckvn, gq, gkv):
        dcq, dgq = _rms_bwd(cq, gq, dcqn)
        dckv, dgkv = _rms_bwd(ckv, gkv, dckvn)
        return dcq, dckv, dgq, dgkv

    dcq, dckv, G["mla_cq_norm"], G["mla_ckv_norm"] = _rowwise(
        c_norm_bwd, "mla_cnorm_bwd", 512,
        [(z, MLA_Q_RANK, Z_CQ // MLA_Q_RANK), (z, MLA_KV_RANK, Z_CKV // MLA_KV_RANK), dcqn, dckvn],
        [P["mla_cq_norm"], P["mla_ckv_norm"]], [(MLA_Q_RANK, BF16), (MLA_KV_RANK, BF16)],
        [((1, MLA_Q_RANK), F32), ((1, MLA_KV_RANK), F32)])

    dz = jnp.concatenate([dzg, du_pre, dv_pre, dqm, dckv, dkr, dcq], axis=1)
    G["w_in"] = _mm_t(hmt, dz, "w_in_dw", tm=1024, tn=1792)
    dhm = _mm(dz, W["w_in"], "nt", BF16, "w_in_dx", tm=1024, tn=1024, tk=2688)

    def norm_bwd_half(x, dh, dxo, g):
        dx, dg = _rms_bwd(x, g, dh)
        dx = dx + dxo
        return dx, (0.5 * dx), dg

    dx1, do1, G["mix_norm"] = _rowwise(norm_bwd_half, "mix_norm_bwd", 512, [x1, dhm, dx2], [P["mix_norm"]],
                                       [(D_MODEL, F32), (D_MODEL, BF16)], [((1, D_MODEL), F32)])
    tie = grads_out("mix", G)

    def ffn1_dw(which, dw):
        G["ffn1_w_" + which] = dw
        return grads_out("ffn1_" + which, G)

    dh1, _, _ = _ffn_bwd(do1, h1t, gu1, a1t, W["ffn1_w_gu"], W["ffn1_w_down"], "ffn1", tie, ffn1_dw)

    def norm_bwd_last(x, dh, dxo, g):
        dx, dg = _rms_bwd(x, g, dh)
        return dx + dxo, dg

    grad_x, G["ffn1_norm"] = _rowwise(norm_bwd_last, "ffn1_norm_bwd", 512, [x, dh1, dx1], [P["ffn1_norm"]],
                                      [(D_MODEL, F32)], [((1, D_MODEL), F32)])
    return loss_part, grad_x, G


SHARDED = ["ffn1_w_gu", "ffn1_w_down", "w_in", "mla_w_uq", "mla_w_ukv", "mem_w_kv",
           "w_branch_a", "w_branch_b", "w_branch_c", "w_out", "ffn2_w_gu", "ffn2_w_down"]
ROW_SHARDED = {"ffn1_w_down", "mem_w_kv", "w_out", "ffn2_w_down"}
SMALL = ["ffn1_norm", "mix_norm", "b_gate", "sg_ln_g", "sg_ln_b", "sg_w", "sg_b", "mla_cq_norm",
         "mla_ckv_norm", "mla_q_norm", "mla_k_norm", "mem_norm", "mem_q_norm", "mem_k_norm", "ffn2_norm"]
ORDER = ["ffn1_norm", "ffn1_w_gu", "ffn1_w_down", "mix_norm", "w_in", "b_gate", "sg_ln_g", "sg_ln_b", "sg_w",
         "sg_b", "mla_cq_norm", "mla_w_uq", "mla_ckv_norm", "mla_w_ukv", "mla_q_norm", "mla_k_norm", "mem_norm",
         "mem_w_kv", "mem_q_norm", "mem_k_norm", "w_branch_a", "w_branch_b", "w_branch_c", "w_out", "ffn2_norm",
         "ffn2_w_gu", "ffn2_w_down"]

_IN_U, _IN_V, _IN_CQ, _IN_CKV, _IN_KR, _IN_QM, _IN_G = 0, 512, 1024, 1408, 1664, 1696, 2208
IN_COLS = 5280


def _full_from_slabs(name, slabs):
    n, r, c = slabs.shape
    if name in ROW_SHARDED:
        return slabs.reshape(n * r, c)
    return slabs.transpose(1, 0, 2).reshape(r, n * c)


def _slabs_from_full(name, full):
    if name in ROW_SHARDED:
        return full.reshape(N_DEV, full.shape[0] // N_DEV, full.shape[1])
    r, c = full.shape
    return full.reshape(r, N_DEV, c // N_DEV).transpose(1, 0, 2)


def _compute_layout(full):
    W = dict(full)
    if "w_in" not in full:
        return W
    w = full["w_in"]
    kr = jnp.pad(w[:, _IN_KR:_IN_QM], ((0, 0), (KR_LANE, LANES - KR_LANE - MLA_ROPE)))
    W["w_in"] = jnp.concatenate([w[:, _IN_G:], w[:, _IN_U:_IN_CQ], w[:, _IN_QM:_IN_G], w[:, _IN_CKV:_IN_KR], kr,
                                 w[:, _IN_CQ:_IN_CKV]], axis=1)
    uq = full["mla_w_uq"].reshape(MLA_Q_RANK, MLA_HEADS, MLA_QK)
    W["mla_w_uq"] = jnp.pad(uq, ((0, 0), (0, 0), (0, LANES - MLA_QK))).reshape(MLA_Q_RANK, HP)
    ukv = full["mla_w_ukv"].reshape(MLA_KV_RANK, MLA_HEADS, MLA_NOPE + MLA_V)
    padh = lambda a: jnp.pad(a, ((0, 0), (0, 0), (0, LANES - a.shape[2]))).reshape(MLA_KV_RANK, HP)
    W["mla_w_ukv"] = jnp.concatenate([padh(ukv[:, :, :MLA_NOPE]), padh(ukv[:, :, MLA_NOPE:])], axis=1)
    wb = full["w_branch_b"].reshape(MLA_HEADS, MLA_V, D_MODEL)
    W["w_branch_b"] = jnp.pad(wb, ((0, 0), (0, LANES - MLA_V), (0, 0))).reshape(HP, D_MODEL)
    return W


def _reference_layout(G):
    out = dict(G)
    if "w_in" not in G:
        return out
    g = G["w_in"]
    out["w_in"] = jnp.concatenate([
        g[:, Z_U:Z_QM], g[:, Z_CQ:Z_COLS], g[:, Z_CKV:Z_KR], g[:, Z_KR + KR_LANE:Z_KR + KR_LANE + MLA_ROPE],
        g[:, Z_QM:Z_CKV], g[:, Z_G:Z_U]], axis=1)
    out["mla_w_uq"] = G["mla_w_uq"].reshape(MLA_Q_RANK, MLA_HEADS, LANES)[:, :, :MLA_QK].reshape(MLA_Q_RANK, -1)
    gk = G["mla_w_ukv"][:, :HP].reshape(MLA_KV_RANK, MLA_HEADS, LANES)[:, :, :MLA_NOPE]
    gv = G["mla_w_ukv"][:, HP:].reshape(MLA_KV_RANK, MLA_HEADS, LANES)[:, :, :MLA_V]
    out["mla_w_ukv"] = jnp.concatenate([gk, gv], axis=2).reshape(MLA_KV_RANK, -1)
    out["w_branch_b"] = G["w_branch_b"].reshape(MLA_HEADS, LANES, D_MODEL)[:, :MLA_V].reshape(-1, D_MODEL)
    return out


def _pack(parts):
    flat = []
    for a in parts:
        a = a.reshape(-1)
        flat.append(jnp.pad(a, (0, (-a.shape[0]) % LANES)))
    return jnp.concatenate(flat).reshape(-1, LANES)


def _unpack(packed, shapes):
    flat = packed.reshape(-1)
    out, off = [], 0
    for shp in shapes:
        n = int(np.prod(shp))
        out.append(flat[off:off + n].reshape(shp))
        off += n + (-n) % LANES
    return out


MESH = pl.DeviceIdType.MESH
HBM = pl.BlockSpec(memory_space=pltpu.HBM)


def _all_gather(shards):
    n = len(shards)

    def body(*refs):
        x_refs, out_refs, token_ref = refs[:n], refs[n:2 * n], refs[2 * n]
        send_sems, recv_sems, local_sems = refs[2 * n + 1:]
        x, y, c = lax.axis_index("x"), lax.axis_index("y"), lax.axis_index("c")
        me, sibling = (x, y, c), (x, y, 1 - c)
        chips = [(1 - x, y), (x, 1 - y), (1 - x, 1 - y)]
        token_ref[...] = jnp.zeros_like(token_ref)

        def slot(a, px, py, pc):
            return out_refs[a].at[4 * px + 2 * py + pc]

        def copy(a, k, block, to, src=None):
            return pltpu.make_async_remote_copy(
                src_ref=slot(a, *block) if src is None else src, dst_ref=slot(a, *block),
                send_sem=send_sems.at[7 * a + k], recv_sem=recv_sems.at[7 * a + k], device_id=to, device_id_type=MESH)

        arrays = range(n)
        mine = [pltpu.make_async_copy(x_refs[a], slot(a, *me), local_sems.at[a]) for a in arrays]
        for cp in mine:
            cp.start()
        first = [copy(a, 0, me, sibling, src=x_refs[a]) for a in arrays]
        first += [copy(a, 1 + j, me, (*chip, c), src=x_refs[a]) for j, chip in enumerate(chips) for a in arrays]
        for cp in first:
            cp.start()
        passed = []
        for j, chip in enumerate(chips):
            for a in arrays:
                copy(a, 1 + j, (*chip, c), me).wait_recv()
                passed.append(copy(a, 4 + j, (*chip, c), sibling))
                passed[-1].start()
        for a in arrays:
            copy(a, 0, sibling, me).wait_recv()
        for j, chip in enumerate(chips):
            for a in arrays:
                copy(a, 4 + j, (*chip, 1 - c), me).wait_recv()
        for cp in first + passed:
            cp.wait_send()
        for cp in mine:
            cp.wait()

    res = pl.pallas_call(
        body, name="all_gather_weights",
        out_shape=[jax.ShapeDtypeStruct((N_DEV,) + s.shape, s.dtype) for s in shards]
        + [jax.ShapeDtypeStruct((8, LANES), F32)],
        in_specs=[HBM] * n, out_specs=[HBM] * n + [pl.BlockSpec(memory_space=pltpu.VMEM)],
        scratch_shapes=[pltpu.SemaphoreType.DMA((7 * n,)), pltpu.SemaphoreType.DMA((7 * n,)),
                        pltpu.SemaphoreType.DMA((n,))],
    )(*shards)
    return res[:n], res[n]


SEM = pl.BlockSpec(memory_space=pltpu.SEMAPHORE)
DATAFLOW = pltpu.SideEffectType.DATAFLOW_SIDE_EFFECTING


def _peers():
    x, y, c = lax.axis_index("x"), lax.axis_index("y"), lax.axis_index("c")
    out = []
    for k in range(1, N_DEV):
        px = 1 - x if k & 4 else x
        py = 1 - y if k & 2 else y
        pc = 1 - c if k & 1 else c
        out.append((k, (px, py, pc), 4 * px + 2 * py + pc))
    return 4 * x + 2 * y + c, out


def _send_start(srcs, per_peer, name):
    n = len(srcs)
    lands = [lax.empty((N_DEV,) + (s.shape[1:] if per_peer else s.shape), s.dtype) for s in srcs]

    def body(*refs):
        src_refs, land_refs, send_sems, recv_sems, token = refs[:n], refs[n:2 * n], refs[2 * n], refs[2 * n + 1], refs[-1]
        me, peers = _peers()
        for a in range(n):
            for k, pid, pflat in peers:
                pltpu.make_async_remote_copy(
                    src_ref=src_refs[a].at[pflat] if per_peer else src_refs[a], dst_ref=land_refs[a].at[me],
                    send_sem=send_sems.at[7 * a + k - 1], recv_sem=recv_sems.at[7 * a + k - 1],
                    device_id=pid, device_id_type=MESH).start()
        token[...] = jnp.zeros_like(token)

    hbm = lambda a: pltpu.with_memory_space_constraint(a, pltpu.HBM)
    res = pl.pallas_call(
        body, name=name,
        out_shape=(pltpu.SemaphoreType.DMA((7 * n,)), pltpu.SemaphoreType.DMA((7 * n,)),
                   *[pltpu.HBM(a.shape, a.dtype) for a in srcs + lands], jax.ShapeDtypeStruct((8, LANES), F32)),
        in_specs=(HBM,) * (2 * n), out_specs=(SEM, SEM) + (HBM,) * (2 * n) + (pl.BlockSpec(memory_space=pltpu.VMEM),),
        input_output_aliases={i: 2 + i for i in range(2 * n)},
        compiler_params=pltpu.CompilerParams(has_side_effects=DATAFLOW),
    )(*[hbm(a) for a in srcs + lands])
    return (res[0], res[1], list(res[2:2 + n]), list(res[2 + n:2 + 2 * n])), res[-1]


def _send_wait(started, after, per_peer, name):
    send_sems, recv_sems, srcs_thru, lands_thru = started
    n = len(srcs_thru)

    def body(*refs):
        src_refs, land_refs, send_sems, recv_sems = refs[:n], refs[n:2 * n], refs[2 * n], refs[2 * n + 1]
        me, peers = _peers()
        for a in range(n):
            for k, pid, pflat in peers:
                copy = pltpu.make_async_remote_copy(
                    src_ref=src_refs[a].at[pflat] if per_peer else src_refs[a], dst_ref=land_refs[a].at[pflat],
                    send_sem=send_sems.at[7 * a + k - 1], recv_sem=recv_sems.at[7 * a + k - 1],
                    device_id=pid, device_id_type=MESH)
                copy.wait_send()
                copy.wait_recv()

    outs = pl.pallas_call(
        body, name=name,
        out_shape=tuple(pltpu.HBM(a.shape, a.dtype) for a in srcs_thru + lands_thru),
        in_specs=(HBM,) * (2 * n) + (SEM, SEM, pl.BlockSpec(memory_space=pl.ANY)), out_specs=(HBM,) * (2 * n),
        input_output_aliases={i: i for i in range(2 * n)},
        compiler_params=pltpu.CompilerParams(has_side_effects=DATAFLOW),
    )(*srcs_thru, *lands_thru, send_sems, recv_sems, after)
    me = 4 * lax.axis_index("x") + 2 * lax.axis_index("y") + lax.axis_index("c")
    landed = []
    for src_out, land in zip(outs[:n], outs[n:]):
        own = lax.dynamic_index_in_dim(src_out, me, 0, keepdims=True) if per_peer else src_out[None]
        landed.append(lax.dynamic_update_slice(land, own, (me,) + (0,) * (land.ndim - 1)))
    return landed


def _share_rows(block, name):
    def body(src_ref, out_ref, send_sems, recv_sems, local_sem):
        me, peers = _peers()
        own = pltpu.make_async_copy(src_ref, out_ref.at[me], local_sem)
        own.start()
        copies = [pltpu.make_async_remote_copy(
            src_ref=src_ref, dst_ref=out_ref.at[me], send_sem=send_sems.at[k - 1], recv_sem=recv_sems.at[k - 1],
            device_id=pid, device_id_type=MESH) for k, pid, _ in peers]
        for cp in copies:
            cp.start()
        for cp in copies:
            cp.wait()
        own.wait()

    return pl.pallas_call(
        body, name=name, out_shape=jax.ShapeDtypeStruct((N_DEV,) + block.shape, block.dtype),
        in_specs=[HBM], out_specs=HBM,
        scratch_shapes=[pltpu.SemaphoreType.DMA((N_DEV - 1,)), pltpu.SemaphoreType.DMA((N_DEV - 1,)),
                        pltpu.SemaphoreType.DMA],
    )(block)


def _sum_slots(recv, name, tr):
    n, rows, lanes = recv.shape
    tr = _tile(rows, tr)

    def body(r_ref, o_ref):
        acc = r_ref[0].astype(F32)
        for i in range(1, n):
            acc = acc + r_ref[i].astype(F32)
        o_ref[...] = acc

    return pl.pallas_call(
        body, name=name, grid=(rows // tr,),
        in_specs=[pl.BlockSpec((n, tr, lanes), lambda i: (0, i, 0))],
        out_specs=pl.BlockSpec((tr, lanes), lambda i: (i, 0)),
        out_shape=jax.ShapeDtypeStruct((rows, lanes), F32),
        compiler_params=_cparams(("parallel",)),
    )(recv)


def _adamw_math(w, g, m, v):
    m = ADAM_B1 * m + (1.0 - ADAM_B1) * g
    v = ADAM_B2 * v + (1.0 - ADAM_B2) * (g * g)
    m_hat = m / (1.0 - ADAM_B1 ** ADAM_STEP)
    v_hat = v / (1.0 - ADAM_B2 ** ADAM_STEP)
    return -ADAM_LR * (m_hat / (jnp.sqrt(v_hat) + ADAM_EPS) + ADAM_WD * w), m, v


def _adamw(w, g, m, v, name, tr=256):
    return _rowwise(_adamw_math, name, tr, [w, g, m, v], [], [(w.shape[1], F32)] * 3)


def _adamw_small(ws, gs, ms, vs):
    n = len(ws)

    def body(*refs):
        ins, outs = refs[:4 * n], refs[4 * n:]
        for i in range(n):
            d, m, v = _adamw_math(ins[i][...], ins[n + i][...], ins[2 * n + i][...], ins[3 * n + i][...])
            outs[i][...], outs[n + i][...], outs[2 * n + i][...] = d, m, v

    vmem = pl.BlockSpec(memory_space=pltpu.VMEM)
    res = pl.pallas_call(
        body, name="adamw_small", in_specs=[vmem] * (4 * n), out_specs=[vmem] * (3 * n),
        out_shape=[jax.ShapeDtypeStruct(w.shape, F32) for w in ws] * 3,
    )(*ws, *gs, *ms, *vs)
    return res[:n], res[n:2 * n], res[2 * n:]


def _sum_adamw(recv, w, m, v, name):
    n, r, c = recv.shape
    tr = _tile(r, 256)

    def body(r_ref, w_ref, m_ref, v_ref, g_ref, d_ref, nm_ref, nv_ref):
        g = r_ref[0].astype(F32)
        for i in range(1, n):
            g = g + r_ref[i].astype(F32)
        g_ref[...] = g
        d_ref[...], nm_ref[...], nv_ref[...] = _adamw_math(w_ref[...], g, m_ref[...], v_ref[...])

    row = pl.BlockSpec((tr, c), lambda i: (i, 0))
    return pl.pallas_call(
        body, name=name, grid=(r // tr,),
        in_specs=[pl.BlockSpec((n, tr, c), lambda i: (0, i, 0)), row, row, row], out_specs=[row] * 4,
        out_shape=[jax.ShapeDtypeStruct((r, c), F32)] * 4, compiler_params=_cparams(("parallel",)),
    )(recv, w, m, v)


def kernel(x, mem, positions, ffn1_norm, ffn1_w_gu, ffn1_w_down, mix_norm, w_in, b_gate, sg_ln_g, sg_ln_b, sg_w, sg_b, mla_cq_norm, mla_w_uq, mla_ckv_norm, mla_w_ukv, mla_q_norm, mla_k_norm, mem_norm, mem_w_kv, mem_q_norm, mem_k_norm, w_branch_a, w_branch_b, w_branch_c, w_out, ffn2_norm, ffn2_w_gu, ffn2_w_down, loss_target, m_ffn1_norm, m_ffn1_w_gu, m_ffn1_w_down, m_mix_norm, m_w_in, m_b_gate, m_sg_ln_g, m_sg_ln_b, m_sg_w, m_sg_b, m_mla_cq_norm, m_mla_w_uq, m_mla_ckv_norm, m_mla_w_ukv, m_mla_q_norm, m_mla_k_norm, m_mem_norm, m_mem_w_kv, m_mem_q_norm, m_mem_k_norm, m_w_branch_a, m_w_branch_b, m_w_branch_c, m_w_out, m_ffn2_norm, m_ffn2_w_gu, m_ffn2_w_down, v_ffn1_norm, v_ffn1_w_gu, v_ffn1_w_down, v_mix_norm, v_w_in, v_b_gate, v_sg_ln_g, v_sg_ln_b, v_sg_w, v_sg_b, v_mla_cq_norm, v_mla_w_uq, v_mla_ckv_norm, v_mla_w_ukv, v_mla_q_norm, v_mla_k_norm, v_mem_norm, v_mem_w_kv, v_mem_q_norm, v_mem_k_norm, v_w_branch_a, v_w_branch_b, v_w_branch_c, v_w_out, v_ffn2_norm, v_ffn2_w_gu, v_ffn2_w_down):
    given = dict(locals())
    wts = {n: given[n] for n in ORDER}
    mom = {n: given["m_" + n] for n in ORDER}
    var = {n: given["v_" + n] for n in ORDER}

    def shards(group, zero):
        out = [wts[n][0].astype(BF16) for n in GROUPS[group]]
        return [out[0] + zero.astype(BF16)] + out[1:]

    def full_weights(group, slabs):
        return _compute_layout({n: _full_from_slabs(n, s) for n, s in zip(GROUPS[group], slabs)})

    def zero_of(a):
        return jnp.minimum(jnp.abs(a.reshape(-1)[0]), 0)

    gathered_ffn1, token = _all_gather([wts[n][0].astype(BF16) for n in GROUPS["ffn1"]])
    flight = {}
    flight["ffn1_down"], token = _send_start(shards("ffn1_down", token[0, 0]), False, "gather_ffn1_down_start")
    flight["mix"] = _send_start(shards("mix", token[0, 0]), False, "gather_mix_start")[0]
    recv = {}

    def weights(group, after):
        if group == "ffn1":
            return full_weights(group, gathered_ffn1)
        landed = _send_wait(flight.pop(group), after, False, f"gather_{group}_wait")
        if group == "mix":
            flight["ffn2"] = _send_start(shards("ffn2", zero_of(landed[0])), False, "gather_ffn2_start")[0]
        return full_weights(group, landed)

    small_shapes = [wts[n].shape[1:] for n in SMALL]
    early = SMALL[1:]
    assert SMALL[0] == "ffn1_norm"

    def grads_out(group, G):
        Gr = _reference_layout({n: G[n] for n in GRAD_GROUPS[group]})
        parts = [_slabs_from_full(n, Gr[n]).astype(BF16) for n in GRAD_GROUPS[group]]
        flight["g_" + group], tie = _send_start(parts, True, f"grads_{group}_start")
        if group == "mix":
            small = _pack([G[n].reshape(s) for n, s in zip(early, small_shapes[1:])])
            small = jnp.pad(small, ((0, (-small.shape[0]) % 8), (0, 0)))
            flight["small"], tie = _send_start([small + tie[0, 0]], False, "grads_small_start")
        return tie

    P = {n: wts[n] if wts[n].ndim == 2 else wts[n][0] for n in SMALL}
    loss_part, grad_x, G = _local_step(x[0], mem[0], positions[0], loss_target[0], P, weights, grads_out)

    for group, names in GRAD_GROUPS.items():
        recv.update(zip(names, _send_wait(flight.pop("g_" + group), grad_x, True, f"grads_{group}_wait")))
    early_recv, = _send_wait(flight.pop("small"), grad_x, False, "grads_small_wait")
    last = _share_rows(G["ffn1_norm"].reshape(-1, LANES), "share_ffn1_norm")
    g_small_packed = _sum_slots(jnp.concatenate([last, early_recv], axis=1), "sum_small", 2048)

    grads, delta, new_m, new_v = {}, {}, {}, {}
    for n in SHARDED:
        grads[n], delta[n], new_m[n], new_v[n] = _sum_adamw(recv[n], wts[n][0], mom[n][0], var[n][0], "adamw_" + n)
    grads.update(zip(SMALL, _unpack(g_small_packed, small_shapes)))

    flat2 = lambda d: [d[n].reshape(-1, d[n].shape[-1]) for n in SMALL]
    for dst, vals in zip((delta, new_m, new_v), _adamw_small(flat2(wts), flat2(grads), flat2(mom), flat2(var))):
        dst.update(zip(SMALL, vals))

    loss = lax.psum(jnp.sum(loss_part), ("x", "y", "c"))
    lead = lambda d: [d[n].reshape(wts[n].shape) for n in ORDER]
    return (loss, grad_x[None], *lead(grads), *lead(delta), *lead(new_m), *lead(new_v))
```

```python
import functools

import numpy as np
import jax
import jax.numpy as jnp
from jax import lax
from jax.experimental import pallas as pl
from jax.experimental.pallas import tpu as pltpu

F32, BF16 = jnp.float32, jnp.bfloat16

D_MODEL = 1024
SG_GROUPS, SG_GROUP_DIM, SG_WIDTH, CHUNK = 8, 64, 512, 128
MLA_HEADS, MLA_NOPE, MLA_ROPE, MLA_V, MLA_QK = 8, 64, 32, 64, 96
MLA_Q_RANK, MLA_KV_RANK = 384, 256
MEM_HEADS, MEM_HEAD_DIM, MEM_WIDTH = 4, 128, 512
D_FF = 2816
ROPE_BASE = 10000.0
EPS = 1e-6
NEG = -1e30
ADAM_LR, ADAM_B1, ADAM_B2, ADAM_EPS, ADAM_WD, ADAM_STEP = 0.001, 0.9, 0.999, 1e-08, 0.01, 10

N_DEV = 8
LANES = 128
V7X_VMEM_LIMIT = 56 * 1024 * 1024
HP = MLA_HEADS * LANES

Z_G, Z_U, Z_V, Z_QM, Z_CKV, Z_KR, Z_CQ = 0, 3072, 3584, 4096, 4608, 4864, 4992
Z_COLS = 5376
KR_LANE = 64


def _tile(dim, pref):
    if dim <= pref:
        return dim
    for t in range(pref - pref % LANES, LANES - 1, -LANES):
        if dim % t == 0:
            return t
    for t in range(pref - pref % 8, 7, -8):
        if dim % t == 0:
            return t
    return dim


def _cparams(sem):
    return pltpu.CompilerParams(dimension_semantics=sem, vmem_limit_bytes=V7X_VMEM_LIMIT)


_DN = {"nn": ((1,), (0,)), "nt": ((1,), (1,)), "tn": ((0,), (0,))}


def _dot(a, b, mode="nn"):
    return lax.dot_general(a.astype(BF16), b.astype(BF16), (_DN[mode], ((), ())),
                           preferred_element_type=F32)


def _mm(a, b, mode, out_dtype, name, tm=512, tn=512, tk=2048, tie=None):
    if mode == "tn":
        K, M = a.shape
    else:
        M, K = a.shape
    N = b.shape[0] if mode == "nt" else b.shape[1]
    tm, tn, tk = _tile(M, tm), _tile(N, tn), _tile(K, tk)
    nk = K // tk
    if mode == "tn":
        a_spec = pl.BlockSpec((tk, tm), lambda i, j, k: (k, i))
    else:
        a_spec = pl.BlockSpec((tm, tk), lambda i, j, k: (i, k))
    if mode == "nt":
        b_spec = pl.BlockSpec((tn, tk), lambda i, j, k: (j, k))
    else:
        b_spec = pl.BlockSpec((tk, tn), lambda i, j, k: (k, j))

    ties = [] if tie is None else [tie]

    def body(a_ref, b_ref, *rest):
        o_ref, *scratch = rest[len(ties):]
        p = _dot(a_ref[...], b_ref[...], mode)
        if nk == 1:
            o_ref[...] = p.astype(o_ref.dtype)
        else:
            acc_ref, = scratch
            k = pl.program_id(2)

            @pl.when(k == 0)
            def _():
                acc_ref[...] = p

            @pl.when(k > 0)
            def _():
                acc_ref[...] += p

            @pl.when(k == nk - 1)
            def _():
                o_ref[...] = acc_ref[...].astype(o_ref.dtype)

    return pl.pallas_call(
        body, name=name, grid=(M // tm, N // tn, nk),
        in_specs=[a_spec, b_spec] + [pl.BlockSpec(t.shape, lambda i, j, k: (0, 0)) for t in ties],
        out_specs=pl.BlockSpec((tm, tn), lambda i, j, k: (i, j)),
        out_shape=jax.ShapeDtypeStruct((M, N), out_dtype),
        scratch_shapes=[] if nk == 1 else [pltpu.VMEM((tm, tn), F32)],
        compiler_params=_cparams(("parallel", "parallel", "arbitrary")),
    )(a, b, *ties)


def _mm_t(at, b, name, tm, tn, tk=1024, tie=None):
    return _mm(at, b, "nn", BF16, name, tm=tm, tn=tn, tk=tk, tie=tie)


def _rowwise(fn, name, tr, row_ins, bc_ins, row_outs, acc_outs=()):
    norm = [it if isinstance(it, tuple) else (it, it.shape[1], 0) for it in row_ins]
    rows = norm[0][0].shape[0]
    tr = _tile(rows, tr)
    arrays, in_specs = [], []
    for arr, w, cb in norm:
        arrays.append(arr)
        in_specs.append(pl.BlockSpec((tr, w), lambda i, cb=cb: (i, cb)))
    for arr in bc_ins:
        arrays.append(arr)
        in_specs.append(pl.BlockSpec(arr.shape, lambda i, nd=arr.ndim: (0,) * nd))
    out_shape, out_specs = [], []
    transposed = [len(o) == 3 for o in row_outs]
    for (w, dt, *_), t in zip(row_outs, transposed):
        out_shape.append(jax.ShapeDtypeStruct((w, rows) if t else (rows, w), dt))
        out_specs.append(pl.BlockSpec((w, tr), lambda i: (0, i)) if t else pl.BlockSpec((tr, w), lambda i: (i, 0)))
    for shp, dt in acc_outs:
        out_shape.append(jax.ShapeDtypeStruct(shp, dt))
        out_specs.append(pl.BlockSpec(shp, lambda i, nd=len(shp): (0,) * nd))
    n_in, n_row = len(arrays), len(row_outs)

    def body(*refs):
        vals = fn(*[r[...].astype(F32) for r in refs[:n_in]])
        if not isinstance(vals, (tuple, list)):
            vals = (vals,)
        outs = refs[n_in:]
        for r, v, t in zip(outs[:n_row], vals[:n_row], transposed):
            r[...] = v.astype(F32).T.astype(r.dtype) if t else v.astype(r.dtype)
        if acc_outs:
            accs = list(zip(outs[n_row:], vals[n_row:]))
            i = pl.program_id(0)

            @pl.when(i == 0)
            def _():
                for r, v in accs:
                    r[...] = v.astype(r.dtype)

            @pl.when(i > 0)
            def _():
                for r, v in accs:
                    r[...] += v.astype(r.dtype)

    res = pl.pallas_call(
        body, name=name, grid=(rows // tr,), in_specs=in_specs, out_specs=out_specs,
        out_shape=out_shape, compiler_params=_cparams(("arbitrary",)),
    )(*arrays)
    return res


def _rsum(x):
    return jnp.sum(x, axis=0, keepdims=True)


def _rms(x, g, n=None):
    n = x.shape[-1] if n is None else n
    r = lax.rsqrt(jnp.sum(x * x, axis=-1, keepdims=True) * (1.0 / n) + EPS)
    return x * r * g


def _rms_bwd(x, g, dy, n=None):
    n = x.shape[-1] if n is None else n
    r = lax.rsqrt(jnp.sum(x * x, axis=-1, keepdims=True) * (1.0 / n) + EPS)
    xh = x * r
    dxh = dy * g
    dx = r * (dxh - xh * (jnp.sum(dxh * xh, axis=-1, keepdims=True) * (1.0 / n)))
    return dx, _rsum(dy * xh)


def _gelu(x):
    return 0.5 * x * (1.0 + lax.erf(x * 0.7071067811865476))


def _gelu_grad(x):
    return 0.5 * (1.0 + lax.erf(x * 0.7071067811865476)) + x * jnp.exp(-0.5 * x * x) * 0.3989422804014327


def _sigmoid(x):
    return 1.0 / (1.0 + jnp.exp(-x))


FFN_TM, FFN_TN = 512, 1408
MXU_WIDTH = 256


def _col_chunks(n):
    return [(c, min(c + MXU_WIDTH, n)) for c in range(0, n, MXU_WIDTH)]


def _ffn_gu_act(h, w_gu, tag):
    T = h.shape[0]
    tm, tn = _tile(T, FFN_TM), FFN_TN
    nj = D_FF // tn

    def body(h_ref, wg_ref, wu_ref, gu_ref, a_ref, at_ref):
        h = h_ref[...]
        for c0, c1 in _col_chunks(tn):
            g = _dot(h, wg_ref[:, c0:c1])
            u = _dot(h, wu_ref[:, c0:c1])
            gu_ref[0, :, c0:c1] = g.astype(BF16)
            gu_ref[1, :, c0:c1] = u.astype(BF16)
            a = g * _sigmoid(g) * u
            a_ref[:, c0:c1] = a.astype(BF16)
            at_ref[c0:c1, :] = a.T.astype(BF16)

    return pl.pallas_call(
        body, name=f"{tag}_gu_act", grid=(T // tm, nj),
        in_specs=[pl.BlockSpec((tm, D_MODEL), lambda i, j: (i, 0)),
                  pl.BlockSpec((D_MODEL, tn), lambda i, j: (0, j)),
                  pl.BlockSpec((D_MODEL, tn), lambda i, j: (0, j + nj))],
        out_specs=[pl.BlockSpec((2, tm, tn), lambda i, j: (0, i, j)),
                   pl.BlockSpec((tm, tn), lambda i, j: (i, j)),
                   pl.BlockSpec((tn, tm), lambda i, j: (j, i))],
        out_shape=[jax.ShapeDtypeStruct((2, T, D_FF), BF16), jax.ShapeDtypeStruct((T, D_FF), BF16),
                   jax.ShapeDtypeStruct((D_FF, T), BF16)],
        compiler_params=_cparams(("parallel", "parallel")),
    )(h, w_gu, w_gu)


def _ffn_da_actbwd(do, w_down, gu, tag, tie=None):
    T = do.shape[0]
    tm, tn = _tile(T, FFN_TM), FFN_TN
    ties = [] if tie is None else [tie]

    def body(do_ref, wd_ref, gu_ref, *rest):
        dgu_ref = rest[-1]
        do = do_ref[...]
        for c0, c1 in _col_chunks(tn):
            da = _dot(do, wd_ref[c0:c1, :], "nt")
            g = gu_ref[0, :, c0:c1].astype(F32)
            u = gu_ref[1, :, c0:c1].astype(F32)
            s = _sigmoid(g)
            dgu_ref[0, :, c0:c1] = (da * u * s * (1.0 + g * (1.0 - s))).astype(BF16)
            dgu_ref[1, :, c0:c1] = (da * g * s).astype(BF16)

    return pl.pallas_call(
        body, name=f"{tag}_da_actbwd", grid=(T // tm, D_FF // tn),
        in_specs=[pl.BlockSpec((tm, D_MODEL), lambda i, j: (i, 0)),
                  pl.BlockSpec((tn, D_MODEL), lambda i, j: (j, 0)),
                  pl.BlockSpec((2, tm, tn), lambda i, j: (0, i, j))]
        + [pl.BlockSpec(t.shape, lambda i, j: (0, 0)) for t in ties],
        out_specs=pl.BlockSpec((2, tm, tn), lambda i, j: (0, i, j)),
        out_shape=jax.ShapeDtypeStruct((2, T, D_FF), BF16),
        compiler_params=_cparams(("parallel", "parallel")),
    )(do, w_down, gu, *ties)


def _ffn_dwgu(ht, dgu, tag, tk=1024):
    T = ht.shape[1]
    tn, tk = FFN_TN, _tile(T, tk)
    nj, nk = D_FF // tn, T // tk

    def body(a_ref, b_ref, o_ref, acc_ref):
        k = pl.program_id(1)
        p = _dot(a_ref[...], b_ref[...])

        @pl.when(k == 0)
        def _():
            acc_ref[...] = p

        @pl.when(k > 0)
        def _():
            acc_ref[...] += p

        @pl.when(k == nk - 1)
        def _():
            o_ref[...] = acc_ref[...].astype(o_ref.dtype)

    return pl.pallas_call(
        body, name=f"{tag}_dwgu", grid=(2 * nj, nk),
        in_specs=[pl.BlockSpec((D_MODEL, tk), lambda n, k: (0, k)),
                  pl.BlockSpec((None, tk, tn), lambda n, k: (n // nj, k, n % nj))],
        out_specs=pl.BlockSpec((D_MODEL, tn), lambda n, k: (0, n)),
        out_shape=jax.ShapeDtypeStruct((D_MODEL, 2 * D_FF), BF16),
        scratch_shapes=[pltpu.VMEM((D_MODEL, tn), F32)],
        compiler_params=_cparams(("parallel", "arbitrary")),
    )(ht, dgu)


def _ffn_dh(dgu, w_gu, tag, tm=1024, tie=None):
    T = dgu.shape[1]
    tm, tk = _tile(T, tm), FFN_TN
    nk = D_FF // tk
    ties = [] if tie is None else [tie]

    def body(a_ref, b_ref, *rest):
        o_ref, acc_ref = rest[len(ties):]
        k = pl.program_id(1)
        p = _dot(a_ref[...], b_ref[...], "nt")

        @pl.when(k == 0)
        def _():
            acc_ref[...] = p

        @pl.when(k > 0)
        def _():
            acc_ref[...] += p

        @pl.when(k == 2 * nk - 1)
        def _():
            o_ref[...] = acc_ref[...].astype(o_ref.dtype)

    return pl.pallas_call(
        body, name=f"{tag}_dh", grid=(T // tm, 2 * nk),
        in_specs=[pl.BlockSpec((None, tm, tk), lambda i, k: (k // nk, i, k % nk)),
                  pl.BlockSpec((D_MODEL, tk), lambda i, k: (0, k))]
        + [pl.BlockSpec(t.shape, lambda i, k: (0, 0)) for t in ties],
        out_specs=pl.BlockSpec((tm, D_MODEL), lambda i, k: (i, 0)),
        out_shape=jax.ShapeDtypeStruct((T, D_MODEL), BF16),
        scratch_shapes=[pltpu.VMEM((tm, D_MODEL), F32)],
        compiler_params=_cparams(("parallel", "arbitrary")),
    )(dgu, w_gu, *ties)


def _ffn_fwd(h, w_gu, w_down, tag):
    gu, a, at = _ffn_gu_act(h, w_gu, tag)
    if callable(w_down):
        w_down = w_down(at)
    o = _mm(a, w_down, "nn", BF16, f"{tag}_down", tm=1024, tn=1024, tk=2816)
    return gu, at, o


def _ffn_bwd(do, ht, gu, at, w_gu, w_down, tag, tie=None, on_dw=None):
    on_dw = on_dw or (lambda which, dw: None)
    dw_down = _mm_t(at, do, f"{tag}_dwdown", tm=1408, tn=1024, tie=tie)
    dgu = _ffn_da_actbwd(do, w_down, gu, tag, tie=on_dw("down", dw_down))
    dw_gu = _ffn_dwgu(ht, dgu, tag)
    dh = _ffn_dh(dgu, w_gu, tag, tie=on_dw("gu", dw_gu))
    return dh, dw_gu, dw_down


def _sg_common(u_pre, v_pre, ln_g, ln_b):
    u = _gelu(u_pre)
    v = _gelu(v_pre)
    mu = jnp.mean(v, axis=-1, keepdims=True)
    vc = v - mu
    rstd = lax.rsqrt(jnp.mean(vc * vc, axis=-1, keepdims=True) + EPS)
    vhat = vc * rstd
    vl = vhat * ln_g + ln_b
    return u, vhat, rstd, vl


def _sg_masked_pairs(w):
    t = lax.broadcasted_iota(jnp.int32, (CHUNK, CHUNK), 0)
    s = lax.broadcasted_iota(jnp.int32, (CHUNK, CHUNK), 1)
    causal = s <= t
    wm = [jnp.where(causal, w[g], 0.0).astype(BF16) for g in range(SG_GROUPS)]
    return [jnp.concatenate([wm[2 * j], wm[2 * j + 1]], axis=0) for j in range(SG_GROUPS // 2)], causal


def _sg_mix(vl, pairs, bias):
    tr = vl.shape[0]
    low = lax.broadcasted_iota(jnp.int32, (CHUNK, LANES), 1) < SG_GROUP_DIM
    vb = vl.astype(BF16)
    rows = []
    for c in range(tr // CHUNK):
        slabs = []
        for j in range(SG_GROUPS // 2):
            slab = vb[c * CHUNK:(c + 1) * CHUNK, j * LANES:(j + 1) * LANES]
            m = _dot(pairs[j], slab)
            slabs.append(jnp.where(low, m[:CHUNK], m[CHUNK:]))
        rows.append(jnp.concatenate(slabs, axis=1) + bias)
    return jnp.concatenate(rows, axis=0)


def _sg_fwd(z, ln_g, ln_b, sg_w, bias_full):
    def fn(u_pre, v_pre, ln_g, ln_b, w, bias):
        u, _, _, vl = _sg_common(u_pre, v_pre, ln_g, ln_b)
        pairs, _ = _sg_masked_pairs(w)
        y = u * _sg_mix(vl, pairs, bias)
        return y, y

    return _rowwise(fn, "sg_fwd", 512, [(z, SG_WIDTH, Z_U // SG_WIDTH), (z, SG_WIDTH, Z_V // SG_WIDTH)],
                    [ln_g, ln_b, sg_w, bias_full], [(SG_WIDTH, BF16), (SG_WIDTH, BF16, "T")])


def _sg_bwd(z, dy, ln_g, ln_b, sg_w, bias_full, group_ind):
    def fn(u_pre, v_pre, dy, ln_g, ln_b, w, bias, ind):
        dy = dy.astype(F32)
        u, vhat, rstd, vl = _sg_common(u_pre, v_pre, ln_g, ln_b)
        pairs, causal = _sg_masked_pairs(w)
        mixed = _sg_mix(vl, pairs, bias)
        du_pre = dy * mixed * _gelu_grad(u_pre)
        dmix = dy * u
        tr = dy.shape[0]
        low = lax.broadcasted_iota(jnp.int32, (CHUNK, LANES), 1) < SG_GROUP_DIM
        vb = vl.astype(BF16)
        dw = [jnp.zeros((CHUNK, CHUNK), F32) for _ in range(SG_GROUPS)]
        dbias = jnp.zeros((CHUNK, SG_WIDTH), F32)
        dvl_rows = []
        for c in range(tr // CHUNK):
            dm_c = dmix[c * CHUNK:(c + 1) * CHUNK]
            dbias = dbias + dm_c
            slabs = []
            for j in range(SG_GROUPS // 2):
                slab = vb[c * CHUNK:(c + 1) * CHUNK, j * LANES:(j + 1) * LANES]
                dm = dm_c[:, j * LANES:(j + 1) * LANES]
                d0 = jnp.where(low, dm, 0.0).astype(BF16)
                d1 = jnp.where(low, 0.0, dm).astype(BF16)
                dw[2 * j] = dw[2 * j] + _dot(d0, slab, "nt")
                dw[2 * j + 1] = dw[2 * j + 1] + _dot(d1, slab, "nt")
                slabs.append(_dot(pairs[j], jnp.concatenate([d0, d1], axis=0), "tn"))
            dvl_rows.append(jnp.concatenate(slabs, axis=1))
        dvl = jnp.concatenate(dvl_rows, axis=0)
        dln_g = _rsum(dvl * vhat)
        dln_b = _rsum(dvl)
        dvh = dvl * ln_g
        dv = rstd * (dvh - jnp.mean(dvh, axis=-1, keepdims=True)
                     - vhat * jnp.mean(dvh * vhat, axis=-1, keepdims=True))
        dv_pre = dv * _gelu_grad(v_pre)
        dw = jnp.stack([jnp.where(causal, d, 0.0) for d in dw], axis=0)
        dbias_t = lax.dot_general(dbias, ind, (((1,), (0,)), ((), ())), precision=lax.Precision.HIGHEST,
                                  preferred_element_type=F32)
        return du_pre, dv_pre, dw, dbias_t, dln_g, dln_b

    return _rowwise(fn, "sg_bwd", 512,
                    [(z, SG_WIDTH, Z_U // SG_WIDTH), (z, SG_WIDTH, Z_V // SG_WIDTH), dy],
                    [ln_g, ln_b, sg_w, bias_full, group_ind],
                    [(SG_WIDTH, BF16), (SG_WIDTH, BF16)],
                    [((SG_GROUPS, CHUNK, CHUNK), F32), ((CHUNK, SG_GROUPS), F32), ((1, SG_WIDTH), F32), ((1, SG_WIDTH), F32)])


def _rope(x, c, s1, s2):
    return x * c + pltpu.roll(x, LANES - MLA_ROPE // 2, 1) * s1 + pltpu.roll(x, MLA_ROPE // 2, 1) * s2


def _rope_t(d, c, s1, s2):
    return d * c + pltpu.roll(d * s1, MLA_ROPE // 2, 1) + pltpu.roll(d * s2, LANES - MLA_ROPE // 2, 1)


def _mla_post(q_pre, kv_pre, z, tabs, gq, gk):
    scale = MLA_QK ** -0.5 * LOG2E

    def fn(q_pre, k_pre, v_pre, kr, c, s1, s2, gq, gk):
        qs, ks = [], []
        for h in range(MLA_HEADS):
            sl = slice(h * LANES, (h + 1) * LANES)
            qs.append(_rope(_rms(q_pre[:, sl], gq, MLA_QK), c, s1, s2) * scale)
            ks.append(_rope(_rms(k_pre[:, sl] + kr, gk, MLA_QK), c, s1, s2))
        lane = lax.broadcasted_iota(jnp.int32, v_pre.shape, 1) & (LANES - 1)
        return jnp.concatenate(qs, axis=1), jnp.concatenate(ks, axis=1), jnp.where(lane == ONES_LANE, 1.0, v_pre)

    return _rowwise(fn, "mla_post", 256,
                    [q_pre, (kv_pre, HP, 0), (kv_pre, HP, 1), (z, LANES, Z_KR // LANES), *tabs],
                    [gq, gk], [(HP, BF16)] * 3)


def _mla_post_bwd(q_pre, kv_pre, z, tabs, gq, gk, dq, dk, dv):
    scale = MLA_QK ** -0.5

    def fn(q_pre, k_pre, kr, c, s1, s2, dq, dk, dv, gq, gk):
        lane = lax.broadcasted_iota(jnp.int32, (1, LANES), 1)
        kr_mask = (lane >= KR_LANE) & (lane < KR_LANE + MLA_ROPE)
        dqs, dks = [], []
        dgq = jnp.zeros((1, LANES), F32)
        dgk = jnp.zeros((1, LANES), F32)
        dkr = jnp.zeros(kr.shape, F32)
        for h in range(MLA_HEADS):
            sl = slice(h * LANES, (h + 1) * LANES)
            dqn = _rope_t(dq[:, sl].astype(F32), c, s1, s2) * scale
            dx, dg = _rms_bwd(q_pre[:, sl], gq, dqn, MLA_QK)
            dqs.append(dx)
            dgq = dgq + dg
            dkn = _rope_t(dk[:, sl].astype(F32), c, s1, s2)
            dx, dg = _rms_bwd(k_pre[:, sl] + kr, gk, dkn, MLA_QK)
            dks.append(dx)
            dgk = dgk + dg
            dkr = dkr + dx
        dkr = jnp.where(kr_mask, dkr, 0.0)
        dkv = jnp.concatenate(dks + [dv.astype(F32)], axis=1)
        return jnp.concatenate(dqs, axis=1), dkv, dkr, dgq, dgk

    return _rowwise(fn, "mla_post_bwd", 256,
                    [q_pre, (kv_pre, HP, 0), (z, LANES, Z_KR // LANES), *tabs, dq, dk, dv],
                    [gq, gk], [(HP, BF16), (2 * HP, BF16), (LANES, BF16)],
                    [((1, LANES), F32), ((1, LANES), F32)])


def _pairs(n, lower):
    a, b = [], []
    for o in range(n):
        inner = range(o + 1) if lower else range(o, n)
        for t in inner:
            a.append(o)
            b.append(t)
    return jnp.asarray(np.array(a, np.int32)), jnp.asarray(np.array(b, np.int32))


FLASH_TILE, FLASH_SUB_ROWS = 4096, 512
LOG2E, LN2 = 1.4426950408889634, 0.6931471805599453
ONES_LANE = MLA_V


def _flash_tiles(T):
    tq = _tile(T, FLASH_TILE)
    return tq, _tile(tq, FLASH_SUB_ROWS)


def _col_span(t, sr, rb, diag, key_major):
    if not diag:
        return 0, t
    return (rb * sr, t) if key_major else (0, (rb + 1) * sr)


def _span_iota(sr, rb, c0, c1):
    r = lax.broadcasted_iota(jnp.int32, (sr, c1 - c0), 0) + rb * sr
    c = lax.broadcasted_iota(jnp.int32, (sr, c1 - c0), 1) + c0
    return r, c


def _lanes(x, width):
    return jnp.concatenate([x] * (width // LANES), axis=1)


def _flash_fwd(q, k, v):
    T = q.shape[0]
    tq, sr = _flash_tiles(T)
    n = T // tq
    ii, jj = _pairs(n, True)

    def body(ii_ref, jj_ref, q_ref, k_ref, v_ref, o_ref, ot_ref, lse_ref, m_sc, acc_sc):
        p_ = pl.program_id(1)
        i, j = ii_ref[p_], jj_ref[p_]

        @pl.when(j == 0)
        def _():
            m_sc[...] = jnp.full(m_sc.shape, NEG, F32)
            acc_sc[...] = jnp.zeros(acc_sc.shape, F32)

        def tile(diag):
            for rb in range(tq // sr):
                rows = slice(rb * sr, (rb + 1) * sr)
                c0, c1 = _col_span(tq, sr, rb, diag, False)
                s = _dot(q_ref[rows, :], k_ref[c0:c1, :], "nt")
                if diag:
                    r, c = _span_iota(sr, rb, c0, c1)
                    s = jnp.where(c <= r, s, NEG)
                m = m_sc[rows, :]
                m_new = jnp.maximum(m, jnp.max(s, axis=1, keepdims=True))
                p = jnp.exp2(s - _lanes(m_new, c1 - c0))
                acc_sc[rows, :] = jnp.exp2(m - m_new) * acc_sc[rows, :] + _dot(p, v_ref[c0:c1, :])
                m_sc[rows, :] = m_new

        @pl.when(j < i)
        def _():
            tile(False)

        @pl.when(j == i)
        def _():
            tile(True)
            acc = acc_sc[...]
            lane = lax.broadcasted_iota(jnp.int32, acc.shape, 1)
            l = jnp.sum(jnp.where(lane == ONES_LANE, acc, 0.0), axis=1, keepdims=True)
            o = jnp.where(lane < MLA_V, acc / l, 0.0)
            o_ref[...] = o.astype(o_ref.dtype)
            ot_ref[...] = o.T.astype(ot_ref.dtype)
            lse_ref[...] = m_sc[...] + jnp.log2(l)

    blk = lambda which: pl.BlockSpec((tq, LANES), which)
    qmap = lambda h, p, ii, jj: (ii[p], h)
    kmap = lambda h, p, ii, jj: (jj[p], h)
    return pl.pallas_call(
        body, name="mla_flash_fwd",
        grid_spec=pltpu.PrefetchScalarGridSpec(
            num_scalar_prefetch=2, grid=(MLA_HEADS, int(ii.shape[0])),
            in_specs=[blk(qmap), blk(kmap), blk(kmap)],
            out_specs=[blk(qmap), pl.BlockSpec((LANES, tq), lambda h, p, ii, jj: (h, ii[p])), blk(qmap)],
            scratch_shapes=[pltpu.VMEM((tq, LANES), F32)] * 2),
        out_shape=[jax.ShapeDtypeStruct((T, HP), BF16), jax.ShapeDtypeStruct((HP, T), BF16),
                   jax.ShapeDtypeStruct((T, HP), F32)],
        compiler_params=_cparams(("parallel", "arbitrary")),
    )(ii, jj, q, k, v)


def _flash_dq(q, k, v, do, lse, delta):
    T = q.shape[0]
    tq, sr = _flash_tiles(T)
    n = T // tq
    ii, jj = _pairs(n, True)

    def body(ii_ref, jj_ref, q_ref, k_ref, v_ref, do_ref, lse_ref, dl_ref, dq_ref, acc_sc):
        p_ = pl.program_id(1)
        i, j = ii_ref[p_], jj_ref[p_]

        @pl.when(j == 0)
        def _():
            acc_sc[...] = jnp.zeros(acc_sc.shape, F32)

        def tile(diag):
            for rb in range(tq // sr):
                rows = slice(rb * sr, (rb + 1) * sr)
                c0, c1 = _col_span(tq, sr, rb, diag, False)
                ks = k_ref[c0:c1, :]
                p = jnp.exp2(_dot(q_ref[rows, :], ks, "nt") - _lanes(lse_ref[rows, :], c1 - c0))
                if diag:
                    r, c = _span_iota(sr, rb, c0, c1)
                    p = jnp.where(c <= r, p, 0.0)
                dp = _dot(do_ref[rows, :], v_ref[c0:c1, :], "nt")
                acc_sc[rows, :] += _dot(p * (dp - _lanes(dl_ref[rows, :], c1 - c0)), ks)

        @pl.when(j < i)
        def _():
            tile(False)

        @pl.when(j == i)
        def _():
            tile(True)
            dq_ref[...] = acc_sc[...].astype(dq_ref.dtype)

    blk = lambda which: pl.BlockSpec((tq, LANES), which)
    qmap = lambda h, p, ii, jj: (ii[p], h)
    kmap = lambda h, p, ii, jj: (jj[p], h)
    return pl.pallas_call(
        body, name="mla_flash_dq",
        grid_spec=pltpu.PrefetchScalarGridSpec(
            num_scalar_prefetch=2, grid=(MLA_HEADS, int(ii.shape[0])),
            in_specs=[blk(qmap), blk(kmap), blk(kmap), blk(qmap), blk(qmap), blk(qmap)],
            out_specs=blk(qmap),
            scratch_shapes=[pltpu.VMEM((tq, LANES), F32)]),
        out_shape=jax.ShapeDtypeStruct((T, HP), BF16),
        compiler_params=_cparams(("parallel", "arbitrary")),
    )(ii, jj, q, k, v, do, lse, delta)


def _flash_dkv(q, k, v, do, lse_row, delta_row):
    T = q.shape[0]
    tq, sr = _flash_tiles(T)
    n = T // tq
    jj, ii = _pairs(n, False)

    def body(jj_ref, ii_ref, q_ref, k_ref, v_ref, do_ref, lse_ref, dl_ref, dk_ref, dv_ref, dk_sc, dv_sc):
        p_ = pl.program_id(1)
        j, i = jj_ref[p_], ii_ref[p_]

        @pl.when(i == j)
        def _():
            dk_sc[...] = jnp.zeros(dk_sc.shape, F32)
            dv_sc[...] = jnp.zeros(dv_sc.shape, F32)

        def tile(diag):
            for rb in range(tq // sr):
                rows = slice(rb * sr, (rb + 1) * sr)
                c0, c1 = _col_span(tq, sr, rb, diag, True)
                qs, dos = q_ref[c0:c1, :], do_ref[c0:c1, :]
                pt = jnp.exp2(_dot(k_ref[rows, :], qs, "nt") - lse_ref[:, c0:c1])
                if diag:
                    r, c = _span_iota(sr, rb, c0, c1)
                    pt = jnp.where(r <= c, pt, 0.0)
                dpt = _dot(v_ref[rows, :], dos, "nt")
                dv_sc[rows, :] += _dot(pt, dos)
                dk_sc[rows, :] += _dot(pt * (dpt - dl_ref[:, c0:c1]), qs)

        @pl.when(i == j)
        def _():
            tile(True)

        @pl.when(i > j)
        def _():
            tile(False)

        @pl.when(i == n - 1)
        def _():
            dk_ref[...] = (dk_sc[...] * LN2).astype(dk_ref.dtype)
            dv_ref[...] = dv_sc[...].astype(dv_ref.dtype)

    blk = lambda which: pl.BlockSpec((tq, LANES), which)
    qmap = lambda h, p, jj, ii: (ii[p], h)
    kmap = lambda h, p, jj, ii: (jj[p], h)
    row = pl.BlockSpec((None, 1, tq), lambda h, p, jj, ii: (h, 0, ii[p]))
    return pl.pallas_call(
        body, name="mla_flash_dkv",
        grid_spec=pltpu.PrefetchScalarGridSpec(
            num_scalar_prefetch=2, grid=(MLA_HEADS, int(ii.shape[0])),
            in_specs=[blk(qmap), blk(kmap), blk(kmap), blk(qmap), row, row],
            out_specs=[blk(kmap), blk(kmap)],
            scratch_shapes=[pltpu.VMEM((tq, LANES), F32)] * 2),
        out_shape=[jax.ShapeDtypeStruct((T, HP), BF16)] * 2,
        compiler_params=_cparams(("parallel", "arbitrary")),
    )(jj, ii, q, k, v, do, lse_row, delta_row)


def _mem_fwd(z, km, vm, gq):
    scale = MEM_HEAD_DIM ** -0.5

    def fn(qm, km, vm, gq):
        ys = []
        for h in range(MEM_HEADS):
            sl = slice(h * LANES, (h + 1) * LANES)
            q = _rms(qm[:, sl], gq) * scale
            s = _dot(q, km[:, sl], "nt")
            p = jnp.exp(s - jnp.max(s, axis=1, keepdims=True))
            p = p / jnp.sum(p, axis=1, keepdims=True)
            ys.append(_dot(p, vm[:, sl]))
        y = jnp.concatenate(ys, axis=1)
        return y, y

    return _rowwise(fn, "mem_fwd", 512, [(z, MEM_WIDTH, Z_QM // MEM_WIDTH)], [km, vm, gq],
                    [(MEM_WIDTH, BF16), (MEM_WIDTH, BF16, "T")])


def _mem_bwd(z, dy, km, vm, gq):
    scale = MEM_HEAD_DIM ** -0.5

    def fn(qm, dy, km, vm, gq):
        dqs, dks, dvs = [], [], []
        dgq = jnp.zeros((1, LANES), F32)
        for h in range(MEM_HEADS):
            sl = slice(h * LANES, (h + 1) * LANES)
            q = (_rms(qm[:, sl], gq) * scale).astype(BF16)
            dyh = dy[:, sl]
            kh, vh = km[:, sl], vm[:, sl]
            s = _dot(q, kh, "nt")
            p = jnp.exp(s - jnp.max(s, axis=1, keepdims=True))
            p = p / jnp.sum(p, axis=1, keepdims=True)
            dp = _dot(dyh, vh, "nt")
            ds = p * (dp - jnp.sum(p * dp, axis=1, keepdims=True))
            dq = _dot(ds, kh) * scale
            dx, dg = _rms_bwd(qm[:, sl], gq, dq)
            dqs.append(dx)
            dgq = dgq + dg
            st = _dot(kh, q, "nt")
            pt = jnp.exp(st - jnp.max(st, axis=0, keepdims=True))
            pt = pt / jnp.sum(pt, axis=0, keepdims=True)
            dpt = _dot(vh, dyh, "nt")
            dst = pt * (dpt - jnp.sum(pt * dpt, axis=0, keepdims=True))
            dvs.append(_dot(pt, dyh))
            dks.append(_dot(dst, q))
        return jnp.concatenate(dqs, axis=1), jnp.concatenate(dks, axis=1), jnp.concatenate(dvs, axis=1), dgq

    m = km.shape[0]
    return _rowwise(fn, "mem_bwd", 512, [(z, MEM_WIDTH, Z_QM // MEM_WIDTH), dy], [km, vm, gq],
                    [(MEM_WIDTH, BF16)], [((m, MEM_WIDTH), F32), ((m, MEM_WIDTH), F32), ((1, LANES), F32)])


GROUPS = {"ffn1": ["ffn1_w_gu"], "ffn1_down": ["ffn1_w_down"],
          "mix": ["w_in", "mla_w_uq", "mla_w_ukv", "mem_w_kv", "w_branch_a", "w_branch_b", "w_branch_c", "w_out"],
          "ffn2": ["ffn2_w_gu", "ffn2_w_down"]}
GRAD_GROUPS = {"ffn2": GROUPS["ffn2"], "mix": GROUPS["mix"], "ffn1_down": ["ffn1_w_down"], "ffn1_gu": ["ffn1_w_gu"]}


def _local_step(x, mem, positions, loss_target, P, weights, grads_out):
    T = x.shape[0]
    G = {}
    W = dict(weights("ffn1", None))

    half = MLA_ROPE // 2
    inv = ROPE_BASE ** (-jnp.arange(half, dtype=F32) / half)
    ang = positions.astype(F32)[:, None] * inv
    cos, sin = jnp.cos(ang), jnp.sin(ang)
    one, zero = jnp.ones((T, MLA_NOPE), F32), jnp.zeros((T, half), F32)
    pad = LANES - MLA_QK
    tabs = (jnp.concatenate([one, cos, cos, jnp.ones((T, pad), F32)], axis=1),
            jnp.concatenate([jnp.zeros((T, MLA_NOPE), F32), -sin, zero, jnp.zeros((T, pad), F32)], axis=1),
            jnp.concatenate([jnp.zeros((T, MLA_NOPE), F32), zero, sin, jnp.zeros((T, pad), F32)], axis=1))
    gq_p = jnp.pad(P["mla_q_norm"], ((0, 0), (0, pad)))
    gk_p = jnp.pad(P["mla_k_norm"], ((0, 0), (0, pad)))
    bias_full = jnp.repeat(P["sg_b"].T, SG_GROUP_DIM, axis=1)
    group_ind = jnp.repeat(jnp.eye(SG_GROUPS, dtype=F32), SG_GROUP_DIM, axis=0)

    HT = (D_MODEL, BF16, "T")

    def norm2(x, g):
        h = _rms(x, g)
        return h, h

    h1, h1t = _rowwise(norm2, "ffn1_norm", 512, [x], [P["ffn1_norm"]], [(D_MODEL, BF16), HT])
    def ffn1_w_down(after):
        W.update(weights("ffn1_down", after))
        return W["ffn1_w_down"]

    gu1, a1t, o1 = _ffn_fwd(h1, W["ffn1_w_gu"], ffn1_w_down, "ffn1")

    def resid_norm(x, o, g):
        xn = x + 0.5 * o
        h = _rms(xn, g)
        return xn, h, h

    x1, hm, hmt = _rowwise(resid_norm, "mix_norm", 512, [x, o1], [P["mix_norm"]],
                           [(D_MODEL, F32), (D_MODEL, BF16), HT])
    W.update(weights("mix", hm))
    z = _mm(hm, W["w_in"], "nn", BF16, "w_in", tm=1024, tn=1792)

    y_a, y_at = _sg_fwd(z, P["sg_ln_g"], P["sg_ln_b"], P["sg_w"], bias_full)

    def c_norm(cq, ckv, gq, gkv):
        a, b = _rms(cq, gq), _rms(ckv, gkv)
        return a, b, a, b

    cqn, ckvn, cqnt, ckvnt = _rowwise(
        c_norm, "mla_cnorm", 512, [(z, MLA_Q_RANK, Z_CQ // MLA_Q_RANK), (z, MLA_KV_RANK, Z_CKV // MLA_KV_RANK)],
        [P["mla_cq_norm"], P["mla_ckv_norm"]],
        [(MLA_Q_RANK, BF16), (MLA_KV_RANK, BF16), (MLA_Q_RANK, BF16, "T"), (MLA_KV_RANK, BF16, "T")])
    q_pre = _mm(cqn, W["mla_w_uq"], "nn", BF16, "mla_uq", tm=1024, tn=1024)
    kv_pre = _mm(ckvn, W["mla_w_ukv"], "nn", BF16, "mla_ukv", tm=1024, tn=1024)
    q, k, v = _mla_post(q_pre, kv_pre, z, tabs, gq_p, gk_p)
    y_b, y_bt, lse = _flash_fwd(q, k, v)

    memn, = _rowwise(lambda m, g: _rms(m, g), "mem_norm", 256, [mem], [P["mem_norm"]], [(D_MODEL, BF16)])
    kvm = _mm(memn, W["mem_w_kv"], "nn", F32, "mem_kv")

    def mem_k(kvm, gk):
        ks = [_rms(kvm[:, h * LANES:(h + 1) * LANES], gk) for h in range(MEM_HEADS)]
        return jnp.concatenate(ks, axis=1), kvm[:, MEM_WIDTH:]

    km, vm = _rowwise(mem_k, "mem_knorm", 256, [kvm], [P["mem_k_norm"]], [(MEM_WIDTH, BF16), (MEM_WIDTH, BF16)])
    y_c, y_ct = _mem_fwd(z, km, vm, P["mem_q_norm"])

    pa = _mm(y_a, W["w_branch_a"], "nn", BF16, "branch_a", tm=1024, tn=1024)
    pb = _mm(y_b, W["w_branch_b"], "nn", BF16, "branch_b", tm=1024, tn=1024)
    pc = _mm(y_c, W["w_branch_c"], "nn", BF16, "branch_c", tm=1024, tn=1024)

    def merge(zg, pa, pb, pc, b):
        g = _sigmoid(zg + b)
        m = g[:, :D_MODEL] * pa + g[:, D_MODEL:2 * D_MODEL] * pb + g[:, 2 * D_MODEL:] * pc
        return m, m

    merged, mergedt = _rowwise(merge, "merge", 256, [(z, 3 * D_MODEL, 0), pa, pb, pc], [P["b_gate"]],
                               [(D_MODEL, BF16), HT])
    om = _mm(merged, W["w_out"], "nn", BF16, "w_out", tm=1024, tn=1024)

    def resid_norm1(x, o, g):
        xn = x + o
        h = _rms(xn, g)
        return xn, h, h

    x2, h2, h2t = _rowwise(resid_norm1, "ffn2_norm", 512, [x1, om], [P["ffn2_norm"]],
                           [(D_MODEL, F32), (D_MODEL, BF16), HT])
    W.update(weights("ffn2", h2))
    gu2, a2t, o2 = _ffn_fwd(h2, W["ffn2_w_gu"], W["ffn2_w_down"], "ffn2")

    def loss_fn(x2, o2, t):
        e = x2 + 0.5 * o2 - t
        return e * (1.0 / D_MODEL), (e * (0.5 / D_MODEL)).astype(BF16), _rsum(e * e) * (0.5 / D_MODEL)

    dx3, do2, loss_part = _rowwise(loss_fn, "loss", 512, [x2, o2, loss_target], [],
                                   [(D_MODEL, F32), (D_MODEL, BF16)], [((1, D_MODEL), F32)])

    dh2, G["ffn2_w_gu"], G["ffn2_w_down"] = _ffn_bwd(do2, h2t, gu2, a2t, W["ffn2_w_gu"], W["ffn2_w_down"], "ffn2")
    tie = grads_out("ffn2", G)

    def norm_bwd(x, dh, dxo, g, *_):
        dx, dg = _rms_bwd(x, g, dh)
        dx = dx + dxo
        return dx, dx, dg

    dx2, dx2b, G["ffn2_norm"] = _rowwise(norm_bwd, "ffn2_norm_bwd", 512, [x2, dh2, dx3],
                                         [P["ffn2_norm"]] + ([] if tie is None else [tie]),
                                         [(D_MODEL, F32), (D_MODEL, BF16)], [((1, D_MODEL), F32)])

    G["w_out"] = _mm_t(mergedt, dx2b, "w_out_dw", tm=1024, tn=1024)
    dmerged = _mm(dx2b, W["w_out"], "nt", BF16, "w_out_dx", tm=1024, tn=1024)

    def merge_bwd(zg, pa, pb, pc, dm, b):
        g = _sigmoid(zg + b)
        ps = jnp.concatenate([pa, pb, pc], axis=1)
        dm3 = jnp.concatenate([dm, dm, dm], axis=1)
        dzg = dm3 * ps * g * (1.0 - g)
        dp = dm3 * g
        return dzg, dp[:, :D_MODEL], dp[:, D_MODEL:2 * D_MODEL], dp[:, 2 * D_MODEL:], _rsum(dzg)

    dzg, dpa, dpb, dpc, G["b_gate"] = _rowwise(
        merge_bwd, "merge_bwd", 256, [(z, 3 * D_MODEL, 0), pa, pb, pc, dmerged], [P["b_gate"]],
        [(3 * D_MODEL, BF16), (D_MODEL, BF16), (D_MODEL, BF16), (D_MODEL, BF16)], [((1, 3 * D_MODEL), F32)])

    G["w_branch_a"] = _mm_t(y_at, dpa, "branch_a_dw", tm=512, tn=1024)
    G["w_branch_b"] = _mm_t(y_bt, dpb, "branch_b_dw", tm=1024, tn=1024)
    G["w_branch_c"] = _mm_t(y_ct, dpc, "branch_c_dw", tm=512, tn=1024)
    dy_a = _mm(dpa, W["w_branch_a"], "nt", BF16, "branch_a_dx", tm=1024, tn=512)
    dy_b = _mm(dpb, W["w_branch_b"], "nt", BF16, "branch_b_dx", tm=1024, tn=1024)
    dy_c = _mm(dpc, W["w_branch_c"], "nt", BF16, "branch_c_dx", tm=1024, tn=512)

    du_pre, dv_pre, G["sg_w"], dbias_t, G["sg_ln_g"], G["sg_ln_b"] = _sg_bwd(
        z, dy_a, P["sg_ln_g"], P["sg_ln_b"], P["sg_w"], bias_full, group_ind)
    G["sg_b"] = dbias_t.T

    dqm, dkm, dvm, G["mem_q_norm"] = _mem_bwd(z, dy_c, km, vm, P["mem_q_norm"])

    def mem_k_bwd(kvm, dkm, dvm, gk):
        dks = []
        dg = jnp.zeros((1, LANES), F32)
        for h in range(MEM_HEADS):
            sl = slice(h * LANES, (h + 1) * LANES)
            dx, d = _rms_bwd(kvm[:, sl], gk, dkm[:, sl])
            dks.append(dx)
            dg = dg + d
        return jnp.concatenate(dks + [dvm], axis=1), dg

    dkvm, G["mem_k_norm"] = _rowwise(mem_k_bwd, "mem_knorm_bwd", 256, [kvm, dkm, dvm], [P["mem_k_norm"]],
                                     [(2 * MEM_WIDTH, BF16)], [((1, LANES), F32)])
    G["mem_w_kv"] = _mm(memn, dkvm, "tn", BF16, "mem_kv_dw")
    dmemn = _mm(dkvm, W["mem_w_kv"], "nt", F32, "mem_kv_dx")
    _, G["mem_norm"] = _rowwise(lambda m, d, g: _rms_bwd(m, g, d), "mem_norm_bwd", 256, [mem, dmemn],
                                [P["mem_norm"]], [(D_MODEL, BF16)], [((1, D_MODEL), F32)])

    def delta_fn(o, do):
        od = o.astype(F32) * do.astype(F32)
        ds = [jnp.broadcast_to(jnp.sum(od[:, h * LANES:(h + 1) * LANES], axis=1, keepdims=True), (od.shape[0], LANES))
              for h in range(MLA_HEADS)]
        return jnp.concatenate(ds, axis=1)

    delta, = _rowwise(delta_fn, "mla_delta", 512, [y_b, dy_b], [], [(HP, F32)])
    rowform = lambda a: a.reshape(T, MLA_HEADS, LANES)[:, :, 0].T.reshape(MLA_HEADS, 1, T)
    dq = _flash_dq(q, k, v, dy_b, lse, delta)
    dk, dv = _flash_dkv(q, k, v, dy_b, rowform(lse), rowform(delta))
    dq_pre, dkv_pre, dkr, dgq, dgk = _mla_post_bwd(q_pre, kv_pre, z, tabs, gq_p, gk_p, dq, dk, dv)
    G["mla_q_norm"], G["mla_k_norm"] = dgq[:, :MLA_QK], dgk[:, :MLA_QK]
    G["mla_w_uq"] = _mm_t(cqnt, dq_pre, "mla_uq_dw", tm=384, tn=1024)
    G["mla_w_ukv"] = _mm_t(ckvnt, dkv_pre, "mla_ukv_dw", tm=256, tn=2048)
    dcqn = _mm(dq_pre, W["mla_w_uq"], "nt", BF16, "mla_uq_dx", tm=1024)
    dckvn = _mm(dkv_pre, W["mla_w_ukv"], "nt", BF16, "mla_ukv_dx", tm=1024)

    def c_norm_bwd(cq, ckv, dcqn, dckvn, gq, gkv):
        dcq, dgq = _rms_bwd(cq, gq, dcqn)
        dckv, dgkv = _rms_bwd(ckv, gkv, dckvn)
        return dcq, dckv, dgq, dgkv

    dcq, dckv, G["mla_cq_norm"], G["mla_ckv_norm"] = _rowwise(
        c_norm_bwd, "mla_cnorm_bwd", 512,
        [(z, MLA_Q_RANK, Z_CQ // MLA_Q_RANK), (z, MLA_KV_RANK, Z_CKV // MLA_KV_RANK), dcqn, dckvn],
        [P["mla_cq_norm"], P["mla_ckv_norm"]], [(MLA_Q_RANK, BF16), (MLA_KV_RANK, BF16)],
        [((1, MLA_Q_RANK), F32), ((1, MLA_KV_RANK), F32)])

    dz = jnp.concatenate([dzg, du_pre, dv_pre, dqm, dckv, dkr, dcq], axis=1)
    G["w_in"] = _mm_t(hmt, dz, "w_in_dw", tm=1024, tn=1792)
    dhm = _mm(dz, W["w_in"], "nt", BF16, "w_in_dx", tm=1024, tn=1024, tk=2688)

    def norm_bwd_half(x, dh, dxo, g):
        dx, dg = _rms_bwd(x, g, dh)
        dx = dx + dxo
        return dx, (0.5 * dx), dg

    dx1, do1, G["mix_norm"] = _rowwise(norm_bwd_half, "mix_norm_bwd", 512, [x1, dhm, dx2], [P["mix_norm"]],
                                       [(D_MODEL, F32), (D_MODEL, BF16)], [((1, D_MODEL), F32)])
    tie = grads_out("mix", G)

    def ffn1_dw(which, dw):
        G["ffn1_w_" + which] = dw
        return grads_out("ffn1_" + which, G)

    dh1, _, _ = _ffn_bwd(do1, h1t, gu1, a1t, W["ffn1_w_gu"], W["ffn1_w_down"], "ffn1", tie, ffn1_dw)

    def norm_bwd_last(x, dh, dxo, g):
        dx, dg = _rms_bwd(x, g, dh)
        return dx + dxo, dg

    grad_x, G["ffn1_norm"] = _rowwise(norm_bwd_last, "ffn1_norm_bwd", 512, [x, dh1, dx1], [P["ffn1_norm"]],
                                      [(D_MODEL, F32)], [((1, D_MODEL), F32)])
    return loss_part, grad_x, G


SHARDED = ["ffn1_w_gu", "ffn1_w_down", "w_in", "mla_w_uq", "mla_w_ukv", "mem_w_kv",
           "w_branch_a", "w_branch_b", "w_branch_c", "w_out", "ffn2_w_gu", "ffn2_w_down"]
ROW_SHARDED = {"ffn1_w_down", "mem_w_kv", "w_out", "ffn2_w_down"}
SMALL = ["ffn1_norm", "mix_norm", "b_gate", "sg_ln_g", "sg_ln_b", "sg_w", "sg_b", "mla_cq_norm",
         "mla_ckv_norm", "mla_q_norm", "mla_k_norm", "mem_norm", "mem_q_norm", "mem_k_norm", "ffn2_norm"]
ORDER = ["ffn1_norm", "ffn1_w_gu", "ffn1_w_down", "mix_norm", "w_in", "b_gate", "sg_ln_g", "sg_ln_b", "sg_w",
         "sg_b", "mla_cq_norm", "mla_w_uq", "mla_ckv_norm", "mla_w_ukv", "mla_q_norm", "mla_k_norm", "mem_norm",
         "mem_w_kv", "mem_q_norm", "mem_k_norm", "w_branch_a", "w_branch_b", "w_branch_c", "w_out", "ffn2_norm",
         "ffn2_w_gu", "ffn2_w_down"]

_IN_U, _IN_V, _IN_CQ, _IN_CKV, _IN_KR, _IN_QM, _IN_G = 0, 512, 1024, 1408, 1664, 1696, 2208
IN_COLS = 5280


def _full_from_slabs(name, slabs):
    n, r, c = slabs.shape
    if name in ROW_SHARDED:
        return slabs.reshape(n * r, c)
    return slabs.transpose(1, 0, 2).reshape(r, n * c)


def _slabs_from_full(name, full):
    if name in ROW_SHARDED:
        return full.reshape(N_DEV, full.shape[0] // N_DEV, full.shape[1])
    r, c = full.shape
    return full.reshape(r, N_DEV, c // N_DEV).transpose(1, 0, 2)


def _compute_layout(full):
    W = dict(full)
    if "w_in" not in full:
        return W
    w = full["w_in"]
    kr = jnp.pad(w[:, _IN_KR:_IN_QM], ((0, 0), (KR_LANE, LANES - KR_LANE - MLA_ROPE)))
    W["w_in"] = jnp.concatenate([w[:, _IN_G:], w[:, _IN_U:_IN_CQ], w[:, _IN_QM:_IN_G], w[:, _IN_CKV:_IN_KR], kr,
                                 w[:, _IN_CQ:_IN_CKV]], axis=1)
    uq = full["mla_w_uq"].reshape(MLA_Q_RANK, MLA_HEADS, MLA_QK)
    W["mla_w_uq"] = jnp.pad(uq, ((0, 0), (0, 0), (0, LANES - MLA_QK))).reshape(MLA_Q_RANK, HP)
    ukv = full["mla_w_ukv"].reshape(MLA_KV_RANK, MLA_HEADS, MLA_NOPE + MLA_V)
    padh = lambda a: jnp.pad(a, ((0, 0), (0, 0), (0, LANES - a.shape[2]))).reshape(MLA_KV_RANK, HP)
    W["mla_w_ukv"] = jnp.concatenate([padh(ukv[:, :, :MLA_NOPE]), padh(ukv[:, :, MLA_NOPE:])], axis=1)
    wb = full["w_branch_b"].reshape(MLA_HEADS, MLA_V, D_MODEL)
    W["w_branch_b"] = jnp.pad(wb, ((0, 0), (0, LANES - MLA_V), (0, 0))).reshape(HP, D_MODEL)
    return W


def _reference_layout(G):
    out = dict(G)
    if "w_in" not in G:
        return out
    g = G["w_in"]
    out["w_in"] = jnp.concatenate([
        g[:, Z_U:Z_QM], g[:, Z_CQ:Z_COLS], g[:, Z_CKV:Z_KR], g[:, Z_KR + KR_LANE:Z_KR + KR_LANE + MLA_ROPE],
        g[:, Z_QM:Z_CKV], g[:, Z_G:Z_U]], axis=1)
    out["mla_w_uq"] = G["mla_w_uq"].reshape(MLA_Q_RANK, MLA_HEADS, LANES)[:, :, :MLA_QK].reshape(MLA_Q_RANK, -1)
    gk = G["mla_w_ukv"][:, :HP].reshape(MLA_KV_RANK, MLA_HEADS, LANES)[:, :, :MLA_NOPE]
    gv = G["mla_w_ukv"][:, HP:].reshape(MLA_KV_RANK, MLA_HEADS, LANES)[:, :, :MLA_V]
    out["mla_w_ukv"] = jnp.concatenate([gk, gv], axis=2).reshape(MLA_KV_RANK, -1)
    out["w_branch_b"] = G["w_branch_b"].reshape(MLA_HEADS, LANES, D_MODEL)[:, :MLA_V].reshape(-1, D_MODEL)
    return out


def _pack(parts):
    flat = []
    for a in parts:
        a = a.reshape(-1)
        flat.append(jnp.pad(a, (0, (-a.shape[0]) % LANES)))
    return jnp.concatenate(flat).reshape(-1, LANES)


def _unpack(packed, shapes):
    flat = packed.reshape(-1)
    out, off = [], 0
    for shp in shapes:
        n = int(np.prod(shp))
        out.append(flat[off:off + n].reshape(shp))
        off += n + (-n) % LANES
    return out


MESH = pl.DeviceIdType.MESH
HBM = pl.BlockSpec(memory_space=pltpu.HBM)


def _all_gather(shards):
    n = len(shards)

    def body(*refs):
        x_refs, out_refs, token_ref = refs[:n], refs[n:2 * n], refs[2 * n]
        send_sems, recv_sems, local_sems = refs[2 * n + 1:]
        x, y, c = lax.axis_index("x"), lax.axis_index("y"), lax.axis_index("c")
        me, sibling = (x, y, c), (x, y, 1 - c)
        chips = [(1 - x, y), (x, 1 - y), (1 - x, 1 - y)]
        token_ref[...] = jnp.zeros_like(token_ref)

        def slot(a, px, py, pc):
            return out_refs[a].at[4 * px + 2 * py + pc]

        def copy(a, k, block, to, src=None):
            return pltpu.make_async_remote_copy(
                src_ref=slot(a, *block) if src is None else src, dst_ref=slot(a, *block),
                send_sem=send_sems.at[7 * a + k], recv_sem=recv_sems.at[7 * a + k], device_id=to, device_id_type=MESH)

        arrays = range(n)
        mine = [pltpu.make_async_copy(x_refs[a], slot(a, *me), local_sems.at[a]) for a in arrays]
        for cp in mine:
            cp.start()
        first = [copy(a, 0, me, sibling, src=x_refs[a]) for a in arrays]
        first += [copy(a, 1 + j, me, (*chip, c), src=x_refs[a]) for j, chip in enumerate(chips) for a in arrays]
        for cp in first:
            cp.start()
        passed = []
        for j, chip in enumerate(chips):
            for a in arrays:
                copy(a, 1 + j, (*chip, c), me).wait_recv()
                passed.append(copy(a, 4 + j, (*chip, c), sibling))
                passed[-1].start()
        for a in arrays:
            copy(a, 0, sibling, me).wait_recv()
        for j, chip in enumerate(chips):
            for a in arrays:
                copy(a, 4 + j, (*chip, 1 - c), me).wait_recv()
        for cp in first + passed:
            cp.wait_send()
        for cp in mine:
            cp.wait()

    res = pl.pallas_call(
        body, name="all_gather_weights",
        out_shape=[jax.ShapeDtypeStruct((N_DEV,) + s.shape, s.dtype) for s in shards]
        + [jax.ShapeDtypeStruct((8, LANES), F32)],
        in_specs=[HBM] * n, out_specs=[HBM] * n + [pl.BlockSpec(memory_space=pltpu.VMEM)],
        scratch_shapes=[pltpu.SemaphoreType.DMA((7 * n,)), pltpu.SemaphoreType.DMA((7 * n,)),
                        pltpu.SemaphoreType.DMA((n,))],
    )(*shards)
    return res[:n], res[n]


SEM = pl.BlockSpec(memory_space=pltpu.SEMAPHORE)
DATAFLOW = pltpu.SideEffectType.DATAFLOW_SIDE_EFFECTING


def _peers():
    x, y, c = lax.axis_index("x"), lax.axis_index("y"), lax.axis_index("c")
    out = []
    for k in range(1, N_DEV):
        px = 1 - x if k & 4 else x
        py = 1 - y if k & 2 else y
        pc = 1 - c if k & 1 else c
        out.append((k, (px, py, pc), 4 * px + 2 * py + pc))
    return 4 * x + 2 * y + c, out


def _send_start(srcs, per_peer, name):
    n = len(srcs)
    lands = [lax.empty((N_DEV,) + (s.shape[1:] if per_peer else s.shape), s.dtype) for s in srcs]

    def body(*refs):
        src_refs, land_refs, send_sems, recv_sems, token = refs[:n], refs[n:2 * n], refs[2 * n], refs[2 * n + 1], refs[-1]
        me, peers = _peers()
        for a in range(n):
            for k, pid, pflat in peers:
                pltpu.make_async_remote_copy(
                    src_ref=src_refs[a].at[pflat] if per_peer else src_refs[a], dst_ref=land_refs[a].at[me],
                    send_sem=send_sems.at[7 * a + k - 1], recv_sem=recv_sems.at[7 * a + k - 1],
                    device_id=pid, device_id_type=MESH).start()
        token[...] = jnp.zeros_like(token)

    hbm = lambda a: pltpu.with_memory_space_constraint(a, pltpu.HBM)
    res = pl.pallas_call(
        body, name=name,
        out_shape=(pltpu.SemaphoreType.DMA((7 * n,)), pltpu.SemaphoreType.DMA((7 * n,)),
                   *[pltpu.HBM(a.shape, a.dtype) for a in srcs + lands], jax.ShapeDtypeStruct((8, LANES), F32)),
        in_specs=(HBM,) * (2 * n), out_specs=(SEM, SEM) + (HBM,) * (2 * n) + (pl.BlockSpec(memory_space=pltpu.VMEM),),
        input_output_aliases={i: 2 + i for i in range(2 * n)},
        compiler_params=pltpu.CompilerParams(has_side_effects=DATAFLOW),
    )(*[hbm(a) for a in srcs + lands])
    return (res[0], res[1], list(res[2:2 + n]), list(res[2 + n:2 + 2 * n])), res[-1]


def _send_wait(started, after, per_peer, name):
    send_sems, recv_sems, srcs_thru, lands_thru = started
    n = len(srcs_thru)

    def body(*refs):
        src_refs, land_refs, send_sems, recv_sems = refs[:n], refs[n:2 * n], refs[2 * n], refs[2 * n + 1]
        me, peers = _peers()
        for a in range(n):
            for k, pid, pflat in peers:
                copy = pltpu.make_async_remote_copy(
                    src_ref=src_refs[a].at[pflat] if per_peer else src_refs[a], dst_ref=land_refs[a].at[pflat],
                    send_sem=send_sems.at[7 * a + k - 1], recv_sem=recv_sems.at[7 * a + k - 1],
                    device_id=pid, device_id_type=MESH)
                copy.wait_send()
                copy.wait_recv()

    outs = pl.pallas_call(
        body, name=name,
        out_shape=tuple(pltpu.HBM(a.shape, a.dtype) for a in srcs_thru + lands_thru),
        in_specs=(HBM,) * (2 * n) + (SEM, SEM, pl.BlockSpec(memory_space=pl.ANY)), out_specs=(HBM,) * (2 * n),
        input_output_aliases={i: i for i in range(2 * n)},
        compiler_params=pltpu.CompilerParams(has_side_effects=DATAFLOW),
    )(*srcs_thru, *lands_thru, send_sems, recv_sems, after)
    me = 4 * lax.axis_index("x") + 2 * lax.axis_index("y") + lax.axis_index("c")
    landed = []
    for src_out, land in zip(outs[:n], outs[n:]):
        own = lax.dynamic_index_in_dim(src_out, me, 0, keepdims=True) if per_peer else src_out[None]
        landed.append(lax.dynamic_update_slice(land, own, (me,) + (0,) * (land.ndim - 1)))
    return landed


def _share_rows(block, name):
    def body(src_ref, out_ref, send_sems, recv_sems, local_sem):
        me, peers = _peers()
        own = pltpu.make_async_copy(src_ref, out_ref.at[me], local_sem)
        own.start()
        copies = [pltpu.make_async_remote_copy(
            src_ref=src_ref, dst_ref=out_ref.at[me], send_sem=send_sems.at[k - 1], recv_sem=recv_sems.at[k - 1],
            device_id=pid, device_id_type=MESH) for k, pid, _ in peers]
        for cp in copies:
            cp.start()
        for cp in copies:
            cp.wait()
        own.wait()

    return pl.pallas_call(
        body, name=name, out_shape=jax.ShapeDtypeStruct((N_DEV,) + block.shape, block.dtype),
        in_specs=[HBM], out_specs=HBM,
        scratch_shapes=[pltpu.SemaphoreType.DMA((N_DEV - 1,)), pltpu.SemaphoreType.DMA((N_DEV - 1,)),
                        pltpu.SemaphoreType.DMA],
    )(block)


def _sum_slots(recv, name, tr):
    n, rows, lanes = recv.shape
    tr = _tile(rows, tr)

    def body(r_ref, o_ref):
        acc = r_ref[0].astype(F32)
        for i in range(1, n):
            acc = acc + r_ref[i].astype(F32)
        o_ref[...] = acc

    return pl.pallas_call(
        body, name=name, grid=(rows // tr,),
        in_specs=[pl.BlockSpec((n, tr, lanes), lambda i: (0, i, 0))],
        out_specs=pl.BlockSpec((tr, lanes), lambda i: (i, 0)),
        out_shape=jax.ShapeDtypeStruct((rows, lanes), F32),
        compiler_params=_cparams(("parallel",)),
    )(recv)


def _adamw_math(w, g, m, v):
    m = ADAM_B1 * m + (1.0 - ADAM_B1) * g
    v = ADAM_B2 * v + (1.0 - ADAM_B2) * (g * g)
    m_hat = m / (1.0 - ADAM_B1 ** ADAM_STEP)
    v_hat = v / (1.0 - ADAM_B2 ** ADAM_STEP)
    return -ADAM_LR * (m_hat / (jnp.sqrt(v_hat) + ADAM_EPS) + ADAM_WD * w), m, v


def _adamw(w, g, m, v, name, tr=256):
    return _rowwise(_adamw_math, name, tr, [w, g, m, v], [], [(w.shape[1], F32)] * 3)


def _adamw_small(ws, gs, ms, vs):
    n = len(ws)

    def body(*refs):
        ins, outs = refs[:4 * n], refs[4 * n:]
        for i in range(n):
            d, m, v = _adamw_math(ins[i][...], ins[n + i][...], ins[2 * n + i][...], ins[3 * n + i][...])
            outs[i][...], outs[n + i][...], outs[2 * n + i][...] = d, m, v

    vmem = pl.BlockSpec(memory_space=pltpu.VMEM)
    res = pl.pallas_call(
        body, name="adamw_small", in_specs=[vmem] * (4 * n), out_specs=[vmem] * (3 * n),
        out_shape=[jax.ShapeDtypeStruct(w.shape, F32) for w in ws] * 3,
    )(*ws, *gs, *ms, *vs)
    return res[:n], res[n:2 * n], res[2 * n:]


def _sum_adamw(recv, w, m, v, name):
    n, r, c = recv.shape
    tr = _tile(r, 256)

    def body(r_ref, w_ref, m_ref, v_ref, g_ref, d_ref, nm_ref, nv_ref):
        g = r_ref[0].astype(F32)
        for i in range(1, n):
            g = g + r_ref[i].astype(F32)
        g_ref[...] = g
        d_ref[...], nm_ref[...], nv_ref[...] = _adamw_math(w_ref[...], g, m_ref[...], v_ref[...])

    row = pl.BlockSpec((tr, c), lambda i: (i, 0))
    return pl.pallas_call(
        body, name=name, grid=(r // tr,),
        in_specs=[pl.BlockSpec((n, tr, c), lambda i: (0, i, 0)), row, row, row], out_specs=[row] * 4,
        out_shape=[jax.ShapeDtypeStruct((r, c), F32)] * 4, compiler_params=_cparams(("parallel",)),
    )(recv, w, m, v)


def kernel(x, mem, positions, ffn1_norm, ffn1_w_gu, ffn1_w_down, mix_norm, w_in, b_gate, sg_ln_g, sg_ln_b, sg_w, sg_b, mla_cq_norm, mla_w_uq, mla_ckv_norm, mla_w_ukv, mla_q_norm, mla_k_norm, mem_norm, mem_w_kv, mem_q_norm, mem_k_norm, w_branch_a, w_branch_b, w_branch_c, w_out, ffn2_norm, ffn2_w_gu, ffn2_w_down, loss_target, m_ffn1_norm, m_ffn1_w_gu, m_ffn1_w_down, m_mix_norm, m_w_in, m_b_gate, m_sg_ln_g, m_sg_ln_b, m_sg_w, m_sg_b, m_mla_cq_norm, m_mla_w_uq, m_mla_ckv_norm, m_mla_w_ukv, m_mla_q_norm, m_mla_k_norm, m_mem_norm, m_mem_w_kv, m_mem_q_norm, m_mem_k_norm, m_w_branch_a, m_w_branch_b, m_w_branch_c, m_w_out, m_ffn2_norm, m_ffn2_w_gu, m_ffn2_w_down, v_ffn1_norm, v_ffn1_w_gu, v_ffn1_w_down, v_mix_norm, v_w_in, v_b_gate, v_sg_ln_g, v_sg_ln_b, v_sg_w, v_sg_b, v_mla_cq_norm, v_mla_w_uq, v_mla_ckv_norm, v_mla_w_ukv, v_mla_q_norm, v_mla_k_norm, v_mem_norm, v_mem_w_kv, v_mem_q_norm, v_mem_k_norm, v_w_branch_a, v_w_branch_b, v_w_branch_c, v_w_out, v_ffn2_norm, v_ffn2_w_gu, v_ffn2_w_down):
    given = dict(locals())
    wts = {n: given[n] for n in ORDER}
    mom = {n: given["m_" + n] for n in ORDER}
    var = {n: given["v_" + n] for n in ORDER}

    def shards(group, zero):
        out = [wts[n][0].astype(BF16) for n in GROUPS[group]]
        return [out[0] + zero.astype(BF16)] + out[1:]

    def full_weights(group, slabs):
        return _compute_layout({n: _full_from_slabs(n, s) for n, s in zip(GROUPS[group], slabs)})

    def zero_of(a):
        return jnp.minimum(jnp.abs(a.reshape(-1)[0]), 0)

    gathered_ffn1, token = _all_gather([wts[n][0].astype(BF16) for n in GROUPS["ffn1"]])
    flight = {}
    flight["ffn1_down"], token = _send_start(shards("ffn1_down", token[0, 0]), False, "gather_ffn1_down_start")
    flight["mix"] = _send_start(shards("mix", token[0, 0]), False, "gather_mix_start")[0]
    recv = {}

    def weights(group, after):
        if group == "ffn1":
            return full_weights(group, gathered_ffn1)
        landed = _send_wait(flight.pop(group), after, False, f"gather_{group}_wait")
        if group == "mix":
            flight["ffn2"] = _send_start(shards("ffn2", zero_of(landed[0])), False, "gather_ffn2_start")[0]
        return full_weights(group, landed)

    small_shapes = [wts[n].shape[1:] for n in SMALL]
    early = SMALL[1:]
    assert SMALL[0] == "ffn1_norm"

    def grads_out(group, G):
        Gr = _reference_layout({n: G[n] for n in GRAD_GROUPS[group]})
        parts = [_slabs_from_full(n, Gr[n]).astype(BF16) for n in GRAD_GROUPS[group]]
        flight["g_" + group], tie = _send_start(parts, True, f"grads_{group}_start")
        if group == "mix":
            small = _pack([G[n].reshape(s) for n, s in zip(early, small_shapes[1:])])
            small = jnp.pad(small, ((0, (-small.shape[0]) % 8), (0, 0)))
            flight["small"], tie = _send_start([small + tie[0, 0]], False, "grads_small_start")
        return tie

    P = {n: wts[n] if wts[n].ndim == 2 else wts[n][0] for n in SMALL}
    loss_part, grad_x, G = _local_step(x[0], mem[0], positions[0], loss_target[0], P, weights, grads_out)

    for group, names in GRAD_GROUPS.items():
        recv.update(zip(names, _send_wait(flight.pop("g_" + group), grad_x, True, f"grads_{group}_wait")))
    early_recv, = _send_wait(flight.pop("small"), grad_x, False, "grads_small_wait")
    last = _share_rows(G["ffn1_norm"].reshape(-1, LANES), "share_ffn1_norm")
    g_small_packed = _sum_slots(jnp.concatenate([last, early_recv], axis=1), "sum_small", 2048)

    grads, delta, new_m, new_v = {}, {}, {}, {}
    for n in SHARDED:
        grads[n], delta[n], new_m[n], new_v[n] = _sum_adamw(recv[n], wts[n][0], mom[n][0], var[n][0], "adamw_" + n)
    grads.update(zip(SMALL, _unpack(g_small_packed, small_shapes)))

    flat2 = lambda d: [d[n].reshape(-1, d[n].shape[-1]) for n in SMALL]
    for dst, vals in zip((delta, new_m, new_v), _adamw_small(flat2(wts), flat2(grads), flat2(mom), flat2(var))):
        dst.update(zip(SMALL, vals))

    loss = lax.psum(jnp.sum(loss_part), ("x", "y", "c"))
    lead = lambda d: [d[n].reshape(wts[n].shape) for n in ORDER]
    return (loss, grad_x[None], *lead(grads), *lead(delta), *lead(new_m), *lead(new_v))
```

```python
import functools

import numpy as np
import jax
import jax.numpy as jnp
from jax import lax
from jax.experimental import pallas as pl
from jax.experimental.pallas import tpu as pltpu

F32, BF16 = jnp.float32, jnp.bfloat16

D_MODEL = 1024
SG_GROUPS, SG_GROUP_DIM, SG_WIDTH, CHUNK = 8, 64, 512, 128
MLA_HEADS, MLA_NOPE, MLA_ROPE, MLA_V, MLA_QK = 8, 64, 32, 64, 96
MLA_Q_RANK, MLA_KV_RANK = 384, 256
MEM_HEADS, MEM_HEAD_DIM, MEM_WIDTH = 4, 128, 512
D_FF = 2816
ROPE_BASE = 10000.0
EPS = 1e-6
NEG = -1e30
ADAM_LR, ADAM_B1, ADAM_B2, ADAM_EPS, ADAM_WD, ADAM_STEP = 0.001, 0.9, 0.999, 1e-08, 0.01, 10

N_DEV = 8
LANES = 128
V7X_VMEM_LIMIT = 56 * 1024 * 1024
HP = MLA_HEADS * LANES

Z_G, Z_U, Z_V, Z_QM, Z_CKV, Z_KR, Z_CQ = 0, 3072, 3584, 4096, 4608, 4864, 4992
Z_COLS = 5376
KR_LANE = 64


def _tile(dim, pref):
    if dim <= pref:
        return dim
    for t in range(pref - pref % LANES, LANES - 1, -LANES):
        if dim % t == 0:
            return t
    for t in range(pref - pref % 8, 7, -8):
        if dim % t == 0:
            return t
    return dim


def _cparams(sem):
    return pltpu.CompilerParams(dimension_semantics=sem, vmem_limit_bytes=V7X_VMEM_LIMIT)


_DN = {"nn": ((1,), (0,)), "nt": ((1,), (1,)), "tn": ((0,), (0,))}


def _dot(a, b, mode="nn"):
    return lax.dot_general(a.astype(BF16), b.astype(BF16), (_DN[mode], ((), ())),
                           preferred_element_type=F32)


def _mm(a, b, mode, out_dtype, name, tm=512, tn=512, tk=2048, tie=None):
    if mode == "tn":
        K, M = a.shape
    else:
        M, K = a.shape
    N = b.shape[0] if mode == "nt" else b.shape[1]
    tm, tn, tk = _tile(M, tm), _tile(N, tn), _tile(K, tk)
    nk = K // tk
    if mode == "tn":
        a_spec = pl.BlockSpec((tk, tm), lambda i, j, k: (k, i))
    else:
        a_spec = pl.BlockSpec((tm, tk), lambda i, j, k: (i, k))
    if mode == "nt":
        b_spec = pl.BlockSpec((tn, tk), lambda i, j, k: (j, k))
    else:
        b_spec = pl.BlockSpec((tk, tn), lambda i, j, k: (k, j))

    ties = [] if tie is None else [tie]

    def body(a_ref, b_ref, *rest):
        o_ref, *scratch = rest[len(ties):]
        p = _dot(a_ref[...], b_ref[...], mode)
        if nk == 1:
            o_ref[...] = p.astype(o_ref.dtype)
        else:
            acc_ref, = scratch
            k = pl.program_id(2)

            @pl.when(k == 0)
            def _():
                acc_ref[...] = p

            @pl.when(k > 0)
            def _():
                acc_ref[...] += p

            @pl.when(k == nk - 1)
            def _():
                o_ref[...] = acc_ref[...].astype(o_ref.dtype)

    return pl.pallas_call(
        body, name=name, grid=(M // tm, N // tn, nk),
        in_specs=[a_spec, b_spec] + [pl.BlockSpec(t.shape, lambda i, j, k: (0, 0)) for t in ties],
        out_specs=pl.BlockSpec((tm, tn), lambda i, j, k: (i, j)),
        out_shape=jax.ShapeDtypeStruct((M, N), out_dtype),
        scratch_shapes=[] if nk == 1 else [pltpu.VMEM((tm, tn), F32)],
        compiler_params=_cparams(("parallel", "parallel", "arbitrary")),
    )(a, b, *ties)


def _mm_t(at, b, name, tm, tn, tk=1024, tie=None):
    return _mm(at, b, "nn", BF16, name, tm=tm, tn=tn, tk=tk, tie=tie)


def _rowwise(fn, name, tr, row_ins, bc_ins, row_outs, acc_outs=()):
    norm = [it if isinstance(it, tuple) else (it, it.shape[1], 0) for it in row_ins]
    rows = norm[0][0].shape[0]
    tr = _tile(rows, tr)
    arrays, in_specs = [], []
    for arr, w, cb in norm:
        arrays.append(arr)
        in_specs.append(pl.BlockSpec((tr, w), lambda i, cb=cb: (i, cb)))
    for arr in bc_ins:
        arrays.append(arr)
        in_specs.append(pl.BlockSpec(arr.shape, lambda i, nd=arr.ndim: (0,) * nd))
    out_shape, out_specs = [], []
    transposed = [len(o) == 3 for o in row_outs]
    for (w, dt, *_), t in zip(row_outs, transposed):
        out_shape.append(jax.ShapeDtypeStruct((w, rows) if t else (rows, w), dt))
        out_specs.append(pl.BlockSpec((w, tr), lambda i: (0, i)) if t else pl.BlockSpec((tr, w), lambda i: (i, 0)))
    for shp, dt in acc_outs:
        out_shape.append(jax.ShapeDtypeStruct(shp, dt))
        out_specs.append(pl.BlockSpec(shp, lambda i, nd=len(shp): (0,) * nd))
    n_in, n_row = len(arrays), len(row_outs)

    def body(*refs):
        vals = fn(*[r[...].astype(F32) for r in refs[:n_in]])
        if not isinstance(vals, (tuple, list)):
            vals = (vals,)
        outs = refs[n_in:]
        for r, v, t in zip(outs[:n_row], vals[:n_row], transposed):
            r[...] = v.astype(F32).T.astype(r.dtype) if t else v.astype(r.dtype)
        if acc_outs:
            accs = list(zip(outs[n_row:], vals[n_row:]))
            i = pl.program_id(0)

            @pl.when(i == 0)
            def _():
                for r, v in accs:
                    r[...] = v.astype(r.dtype)

            @pl.when(i > 0)
            def _():
                for r, v in accs:
                    r[...] += v.astype(r.dtype)

    res = pl.pallas_call(
        body, name=name, grid=(rows // tr,), in_specs=in_specs, out_specs=out_specs,
        out_shape=out_shape, compiler_params=_cparams(("arbitrary",)),
    )(*arrays)
    return res


def _rsum(x):
    return jnp.sum(x, axis=0, keepdims=True)


def _rms(x, g, n=None):
    n = x.shape[-1] if n is None else n
    r = lax.rsqrt(jnp.sum(x * x, axis=-1, keepdims=True) * (1.0 / n) + EPS)
    return x * r * g


def _rms_bwd(x, g, dy, n=None):
    n = x.shape[-1] if n is None else n
    r = lax.rsqrt(jnp.sum(x * x, axis=-1, keepdims=True) * (1.0 / n) + EPS)
    xh = x * r
    dxh = dy * g
    dx = r * (dxh - xh * (jnp.sum(dxh * xh, axis=-1, keepdims=True) * (1.0 / n)))
    return dx, _rsum(dy * xh)


def _gelu(x):
    return 0.5 * x * (1.0 + lax.erf(x * 0.7071067811865476))


def _gelu_grad(x):
    return 0.5 * (1.0 + lax.erf(x * 0.7071067811865476)) + x * jnp.exp(-0.5 * x * x) * 0.3989422804014327


def _sigmoid(x):
    return 1.0 / (1.0 + jnp.exp(-x))


FFN_TM, FFN_TN = 512, 1408
MXU_WIDTH = 256


def _col_chunks(n):
    return [(c, min(c + MXU_WIDTH, n)) for c in range(0, n, MXU_WIDTH)]


def _ffn_gu_act(h, w_gu, tag):
    T = h.shape[0]
    tm, tn = _tile(T, FFN_TM), FFN_TN
    nj = D_FF // tn

    def body(h_ref, wg_ref, wu_ref, gu_ref, a_ref, at_ref):
        h = h_ref[...]
        for c0, c1 in _col_chunks(tn):
            g = _dot(h, wg_ref[:, c0:c1])
            u = _dot(h, wu_ref[:, c0:c1])
            gu_ref[0, :, c0:c1] = g.astype(BF16)
            gu_ref[1, :, c0:c1] = u.astype(BF16)
            a = g * _sigmoid(g) * u
            a_ref[:, c0:c1] = a.astype(BF16)
            at_ref[c0:c1, :] = a.T.astype(BF16)

    return pl.pallas_call(
        body, name=f"{tag}_gu_act", grid=(T // tm, nj),
        in_specs=[pl.BlockSpec((tm, D_MODEL), lambda i, j: (i, 0)),
                  pl.BlockSpec((D_MODEL, tn), lambda i, j: (0, j)),
                  pl.BlockSpec((D_MODEL, tn), lambda i, j: (0, j + nj))],
        out_specs=[pl.BlockSpec((2, tm, tn), lambda i, j: (0, i, j)),
                   pl.BlockSpec((tm, tn), lambda i, j: (i, j)),
                   pl.BlockSpec((tn, tm), lambda i, j: (j, i))],
        out_shape=[jax.ShapeDtypeStruct((2, T, D_FF), BF16), jax.ShapeDtypeStruct((T, D_FF), BF16),
                   jax.ShapeDtypeStruct((D_FF, T), BF16)],
        compiler_params=_cparams(("parallel", "parallel")),
    )(h, w_gu, w_gu)


def _ffn_da_actbwd(do, w_down, gu, tag, tie=None):
    T = do.shape[0]
    tm, tn = _tile(T, FFN_TM), FFN_TN
    ties = [] if tie is None else [tie]

    def body(do_ref, wd_ref, gu_ref, *rest):
        dgu_ref = rest[-1]
        do = do_ref[...]
        for c0, c1 in _col_chunks(tn):
            da = _dot(do, wd_ref[c0:c1, :], "nt")
            g = gu_ref[0, :, c0:c1].astype(F32)
            u = gu_ref[1, :, c0:c1].astype(F32)
            s = _sigmoid(g)
            dgu_ref[0, :, c0:c1] = (da * u * s * (1.0 + g * (1.0 - s))).astype(BF16)
            dgu_ref[1, :, c0:c1] = (da * g * s).astype(BF16)

    return pl.pallas_call(
        body, name=f"{tag}_da_actbwd", grid=(T // tm, D_FF // tn),
        in_specs=[pl.BlockSpec((tm, D_MODEL), lambda i, j: (i, 0)),
                  pl.BlockSpec((tn, D_MODEL), lambda i, j: (j, 0)),
                  pl.BlockSpec((2, tm, tn), lambda i, j: (0, i, j))]
        + [pl.BlockSpec(t.shape, lambda i, j: (0, 0)) for t in ties],
        out_specs=pl.BlockSpec((2, tm, tn), lambda i, j: (0, i, j)),
        out_shape=jax.ShapeDtypeStruct((2, T, D_FF), BF16),
        compiler_params=_cparams(("parallel", "parallel")),
    )(do, w_down, gu, *ties)


def _ffn_dwgu(ht, dgu, tag, tk=2048):
    T = ht.shape[1]
    tn, tk = FFN_TN, _tile(T, tk)
    nj, nk = D_FF // tn, T // tk

    def body(a_ref, b_ref, o_ref, acc_ref):
        k = pl.program_id(1)
        p = _dot(a_ref[...], b_ref[...])

        @pl.when(k == 0)
        def _():
            acc_ref[...] = p

        @pl.when(k > 0)
        def _():
            acc_ref[...] += p

        @pl.when(k == nk - 1)
        def _():
            o_ref[...] = acc_ref[...].astype(o_ref.dtype)

    return pl.pallas_call(
        body, name=f"{tag}_dwgu", grid=(2 * nj, nk),
        in_specs=[pl.BlockSpec((D_MODEL, tk), lambda n, k: (0, k)),
                  pl.BlockSpec((None, tk, tn), lambda n, k: (n // nj, k, n % nj))],
        out_specs=pl.BlockSpec((D_MODEL, tn), lambda n, k: (0, n)),
        out_shape=jax.ShapeDtypeStruct((D_MODEL, 2 * D_FF), BF16),
        scratch_shapes=[pltpu.VMEM((D_MODEL, tn), F32)],
        compiler_params=_cparams(("parallel", "arbitrary")),
    )(ht, dgu)


def _ffn_dh(dgu, w_gu, tag, tm=2048, tie=None):
    T = dgu.shape[1]
    tm, tk = _tile(T, tm), FFN_TN
    nk = D_FF // tk
    ties = [] if tie is None else [tie]

    def body(a_ref, b_ref, *rest):
        o_ref, acc_ref = rest[len(ties):]
        k = pl.program_id(1)
        p = _dot(a_ref[...], b_ref[...], "nt")

        @pl.when(k == 0)
        def _():
            acc_ref[...] = p

        @pl.when(k > 0)
        def _():
            acc_ref[...] += p

        @pl.when(k == 2 * nk - 1)
        def _():
            o_ref[...] = acc_ref[...].astype(o_ref.dtype)

    return pl.pallas_call(
        body, name=f"{tag}_dh", grid=(T // tm, 2 * nk),
        in_specs=[pl.BlockSpec((None, tm, tk), lambda i, k: (k // nk, i, k % nk)),
                  pl.BlockSpec((D_MODEL, tk), lambda i, k: (0, k))]
        + [pl.BlockSpec(t.shape, lambda i, k: (0, 0)) for t in ties],
        out_specs=pl.BlockSpec((tm, D_MODEL), lambda i, k: (i, 0)),
        out_shape=jax.ShapeDtypeStruct((T, D_MODEL), BF16),
        scratch_shapes=[pltpu.VMEM((tm, D_MODEL), F32)],
        compiler_params=_cparams(("parallel", "arbitrary")),
    )(dgu, w_gu, *ties)


def _ffn_fwd(h, w_gu, w_down, tag):
    gu, a, at = _ffn_gu_act(h, w_gu, tag)
    if callable(w_down):
        w_down = w_down(at)
    o = _mm(a, w_down, "nn", BF16, f"{tag}_down", tm=1024, tn=1024, tk=2816)
    return gu, at, o


def _ffn_bwd(do, ht, gu, at, w_gu, w_down, tag, tie=None, on_dw=None):
    on_dw = on_dw or (lambda which, dw: None)
    dw_down = _mm_t(at, do, f"{tag}_dwdown", tm=1408, tn=1024, tk=2048, tie=tie)
    dgu = _ffn_da_actbwd(do, w_down, gu, tag, tie=on_dw("down", dw_down))
    dw_gu = _ffn_dwgu(ht, dgu, tag)
    dh = _ffn_dh(dgu, w_gu, tag, tie=on_dw("gu", dw_gu))
    return dh, dw_gu, dw_down


def _sg_common(u_pre, v_pre, ln_g, ln_b):
    u = _gelu(u_pre)
    v = _gelu(v_pre)
    mu = jnp.mean(v, axis=-1, keepdims=True)
    vc = v - mu
    rstd = lax.rsqrt(jnp.mean(vc * vc, axis=-1, keepdims=True) + EPS)
    vhat = vc * rstd
    vl = vhat * ln_g + ln_b
    return u, vhat, rstd, vl


def _sg_masked_pairs(w):
    t = lax.broadcasted_iota(jnp.int32, (CHUNK, CHUNK), 0)
    s = lax.broadcasted_iota(jnp.int32, (CHUNK, CHUNK), 1)
    causal = s <= t
    wm = [jnp.where(causal, w[g], 0.0).astype(BF16) for g in range(SG_GROUPS)]
    return [jnp.concatenate([wm[2 * j], wm[2 * j + 1]], axis=0) for j in range(SG_GROUPS // 2)], causal


def _sg_mix(vl, pairs, bias):
    tr = vl.shape[0]
    low = lax.broadcasted_iota(jnp.int32, (CHUNK, LANES), 1) < SG_GROUP_DIM
    vb = vl.astype(BF16)
    rows = []
    for c in range(tr // CHUNK):
        slabs = []
        for j in range(SG_GROUPS // 2):
            slab = vb[c * CHUNK:(c + 1) * CHUNK, j * LANES:(j + 1) * LANES]
            m = _dot(pairs[j], slab)
            slabs.append(jnp.where(low, m[:CHUNK], m[CHUNK:]))
        rows.append(jnp.concatenate(slabs, axis=1) + bias)
    return jnp.concatenate(rows, axis=0)


def _sg_fwd(z, ln_g, ln_b, sg_w, bias_full):
    def fn(u_pre, v_pre, ln_g, ln_b, w, bias):
        u, _, _, vl = _sg_common(u_pre, v_pre, ln_g, ln_b)
        pairs, _ = _sg_masked_pairs(w)
        y = u * _sg_mix(vl, pairs, bias)
        return y, y

    return _rowwise(fn, "sg_fwd", 512, [(z, SG_WIDTH, Z_U // SG_WIDTH), (z, SG_WIDTH, Z_V // SG_WIDTH)],
                    [ln_g, ln_b, sg_w, bias_full], [(SG_WIDTH, BF16), (SG_WIDTH, BF16, "T")])


def _sg_bwd(z, dy, ln_g, ln_b, sg_w, bias_full, group_ind):
    def fn(u_pre, v_pre, dy, ln_g, ln_b, w, bias, ind):
        dy = dy.astype(F32)
        u, vhat, rstd, vl = _sg_common(u_pre, v_pre, ln_g, ln_b)
        pairs, causal = _sg_masked_pairs(w)
        mixed = _sg_mix(vl, pairs, bias)
        du_pre = dy * mixed * _gelu_grad(u_pre)
        dmix = dy * u
        tr = dy.shape[0]
        low = lax.broadcasted_iota(jnp.int32, (CHUNK, LANES), 1) < SG_GROUP_DIM
        vb = vl.astype(BF16)
        dw = [jnp.zeros((CHUNK, CHUNK), F32) for _ in range(SG_GROUPS)]
        dbias = jnp.zeros((CHUNK, SG_WIDTH), F32)
        dvl_rows = []
        for c in range(tr // CHUNK):
            dm_c = dmix[c * CHUNK:(c + 1) * CHUNK]
            dbias = dbias + dm_c
            slabs = []
            for j in range(SG_GROUPS // 2):
                slab = vb[c * CHUNK:(c + 1) * CHUNK, j * LANES:(j + 1) * LANES]
                dm = dm_c[:, j * LANES:(j + 1) * LANES]
                d0 = jnp.where(low, dm, 0.0).astype(BF16)
                d1 = jnp.where(low, 0.0, dm).astype(BF16)
                dw[2 * j] = dw[2 * j] + _dot(d0, slab, "nt")
                dw[2 * j + 1] = dw[2 * j + 1] + _dot(d1, slab, "nt")
                slabs.append(_dot(pairs[j], jnp.concatenate([d0, d1], axis=0), "tn"))
            dvl_rows.append(jnp.concatenate(slabs, axis=1))
        dvl = jnp.concatenate(dvl_rows, axis=0)
        dln_g = _rsum(dvl * vhat)
        dln_b = _rsum(dvl)
        dvh = dvl * ln_g
        dv = rstd * (dvh - jnp.mean(dvh, axis=-1, keepdims=True)
                     - vhat * jnp.mean(dvh * vhat, axis=-1, keepdims=True))
        dv_pre = dv * _gelu_grad(v_pre)
        dw = jnp.stack([jnp.where(causal, d, 0.0) for d in dw], axis=0)
        dbias_t = lax.dot_general(dbias, ind, (((1,), (0,)), ((), ())), precision=lax.Precision.HIGHEST,
                                  preferred_element_type=F32)
        return du_pre, dv_pre, dw, dbias_t, dln_g, dln_b

    return _rowwise(fn, "sg_bwd", 512,
                    [(z, SG_WIDTH, Z_U // SG_WIDTH), (z, SG_WIDTH, Z_V // SG_WIDTH), dy],
                    [ln_g, ln_b, sg_w, bias_full, group_ind],
                    [(SG_WIDTH, BF16), (SG_WIDTH, BF16)],
                    [((SG_GROUPS, CHUNK, CHUNK), F32), ((CHUNK, SG_GROUPS), F32), ((1, SG_WIDTH), F32), ((1, SG_WIDTH), F32)])


def _rope(x, c, s1, s2):
    return x * c + pltpu.roll(x, LANES - MLA_ROPE // 2, 1) * s1 + pltpu.roll(x, MLA_ROPE // 2, 1) * s2


def _rope_t(d, c, s1, s2):
    return d * c + pltpu.roll(d * s1, MLA_ROPE // 2, 1) + pltpu.roll(d * s2, LANES - MLA_ROPE // 2, 1)


def _mla_post(q_pre, kv_pre, z, tabs, gq, gk):
    scale = MLA_QK ** -0.5 * LOG2E

    def fn(q_pre, k_pre, v_pre, kr, c, s1, s2, gq, gk):
        qs, ks = [], []
        for h in range(MLA_HEADS):
            sl = slice(h * LANES, (h + 1) * LANES)
            qs.append(_rope(_rms(q_pre[:, sl], gq, MLA_QK), c, s1, s2) * scale)
            ks.append(_rope(_rms(k_pre[:, sl] + kr, gk, MLA_QK), c, s1, s2))
        lane = lax.broadcasted_iota(jnp.int32, v_pre.shape, 1) & (LANES - 1)
        return jnp.concatenate(qs, axis=1), jnp.concatenate(ks, axis=1), jnp.where(lane == ONES_LANE, 1.0, v_pre)

    return _rowwise(fn, "mla_post", 256,
                    [q_pre, (kv_pre, HP, 0), (kv_pre, HP, 1), (z, LANES, Z_KR // LANES), *tabs],
                    [gq, gk], [(HP, BF16)] * 3)


def _mla_post_bwd(q_pre, kv_pre, z, tabs, gq, gk, dq, dk, dv):
    scale = MLA_QK ** -0.5

    def fn(q_pre, k_pre, kr, c, s1, s2, dq, dk, dv, gq, gk):
        lane = lax.broadcasted_iota(jnp.int32, (1, LANES), 1)
        kr_mask = (lane >= KR_LANE) & (lane < KR_LANE + MLA_ROPE)
        dqs, dks = [], []
        dgq = jnp.zeros((1, LANES), F32)
        dgk = jnp.zeros((1, LANES), F32)
        dkr = jnp.zeros(kr.shape, F32)
        for h in range(MLA_HEADS):
            sl = slice(h * LANES, (h + 1) * LANES)
            dqn = _rope_t(dq[:, sl].astype(F32), c, s1, s2) * scale
            dx, dg = _rms_bwd(q_pre[:, sl], gq, dqn, MLA_QK)
            dqs.append(dx)
            dgq = dgq + dg
            dkn = _rope_t(dk[:, sl].astype(F32), c, s1, s2)
            dx, dg = _rms_bwd(k_pre[:, sl] + kr, gk, dkn, MLA_QK)
            dks.append(dx)
            dgk = dgk + dg
            dkr = dkr + dx
        dkr = jnp.where(kr_mask, dkr, 0.0)
        dkv = jnp.concatenate(dks + [dv.astype(F32)], axis=1)
        return jnp.concatenate(dqs, axis=1), dkv, dkr, dgq, dgk

    return _rowwise(fn, "mla_post_bwd", 256,
                    [q_pre, (kv_pre, HP, 0), (z, LANES, Z_KR // LANES), *tabs, dq, dk, dv],
                    [gq, gk], [(HP, BF16), (2 * HP, BF16), (LANES, BF16)],
                    [((1, LANES), F32), ((1, LANES), F32)])


def _pairs(n, lower):
    a, b = [], []
    for o in range(n):
        inner = range(o + 1) if lower else range(o, n)
        for t in inner:
            a.append(o)
            b.append(t)
    return jnp.asarray(np.array(a, np.int32)), jnp.asarray(np.array(b, np.int32))


FLASH_TILE, FLASH_SUB_ROWS = 2048, 512
LOG2E, LN2 = 1.4426950408889634, 0.6931471805599453
ONES_LANE = MLA_V


def _flash_tiles(T):
    tq = _tile(T, FLASH_TILE)
    return tq, _tile(tq, FLASH_SUB_ROWS)


def _col_span(t, sr, rb, diag, key_major):
    if not diag:
        return 0, t
    return (rb * sr, t) if key_major else (0, (rb + 1) * sr)


def _span_iota(sr, rb, c0, c1):
    r = lax.broadcasted_iota(jnp.int32, (sr, c1 - c0), 0) + rb * sr
    c = lax.broadcasted_iota(jnp.int32, (sr, c1 - c0), 1) + c0
    return r, c


def _lanes(x, width):
    return jnp.concatenate([x] * (width // LANES), axis=1)


def _flash_fwd(q, k, v):
    T = q.shape[0]
    tq, sr = _flash_tiles(T)
    n = T // tq
    ii, jj = _pairs(n, True)

    def body(ii_ref, jj_ref, q_ref, k_ref, v_ref, o_ref, ot_ref, lse_ref, m_sc, acc_sc):
        p_ = pl.program_id(1)
        i, j = ii_ref[p_], jj_ref[p_]

        @pl.when(j == 0)
        def _():
            m_sc[...] = jnp.full(m_sc.shape, NEG, F32)
            acc_sc[...] = jnp.zeros(acc_sc.shape, F32)

        def tile(diag):
            for rb in range(tq // sr):
                rows = slice(rb * sr, (rb + 1) * sr)
                c0, c1 = _col_span(tq, sr, rb, diag, False)
                s = _dot(q_ref[rows, :], k_ref[c0:c1, :], "nt")
                if diag:
                    r, c = _span_iota(sr, rb, c0, c1)
                    s = jnp.where(c <= r, s, NEG)
                m = m_sc[rows, :]
                m_new = jnp.maximum(m, jnp.max(s, axis=1, keepdims=True))
                p = jnp.exp2(s - _lanes(m_new, c1 - c0))
                acc_sc[rows, :] = jnp.exp2(m - m_new) * acc_sc[rows, :] + _dot(p, v_ref[c0:c1, :])
                m_sc[rows, :] = m_new

        @pl.when(j < i)
        def _():
            tile(False)

        @pl.when(j == i)
        def _():
            tile(True)
            acc = acc_sc[...]
            lane = lax.broadcasted_iota(jnp.int32, acc.shape, 1)
            l = jnp.sum(jnp.where(lane == ONES_LANE, acc, 0.0), axis=1, keepdims=True)
            o = jnp.where(lane < MLA_V, acc / l, 0.0)
            o_ref[...] = o.astype(o_ref.dtype)
            ot_ref[...] = o.T.astype(ot_ref.dtype)
            lse_ref[...] = m_sc[...] + jnp.log2(l)

    blk = lambda which: pl.BlockSpec((tq, LANES), which)
    qmap = lambda h, p, ii, jj: (ii[p], h)
    kmap = lambda h, p, ii, jj: (jj[p], h)
    return pl.pallas_call(
        body, name="mla_flash_fwd",
        grid_spec=pltpu.PrefetchScalarGridSpec(
            num_scalar_prefetch=2, grid=(MLA_HEADS, int(ii.shape[0])),
            in_specs=[blk(qmap), blk(kmap), blk(kmap)],
            out_specs=[blk(qmap), pl.BlockSpec((LANES, tq), lambda h, p, ii, jj: (h, ii[p])), blk(qmap)],
            scratch_shapes=[pltpu.VMEM((tq, LANES), F32)] * 2),
        out_shape=[jax.ShapeDtypeStruct((T, HP), BF16), jax.ShapeDtypeStruct((HP, T), BF16),
                   jax.ShapeDtypeStruct((T, HP), F32)],
        compiler_params=_cparams(("parallel", "arbitrary")),
    )(ii, jj, q, k, v)


def _flash_dq(q, k, v, do, lse, delta):
    T = q.shape[0]
    tq, sr = _flash_tiles(T)
    n = T // tq
    ii, jj = _pairs(n, True)

    def body(ii_ref, jj_ref, q_ref, k_ref, v_ref, do_ref, lse_ref, dl_ref, dq_ref, acc_sc):
        p_ = pl.program_id(1)
        i, j = ii_ref[p_], jj_ref[p_]

        @pl.when(j == 0)
        def _():
            acc_sc[...] = jnp.zeros(acc_sc.shape, F32)

        def tile(diag):
            for rb in range(tq // sr):
                rows = slice(rb * sr, (rb + 1) * sr)
                c0, c1 = _col_span(tq, sr, rb, diag, False)
                ks = k_ref[c0:c1, :]
                p = jnp.exp2(_dot(q_ref[rows, :], ks, "nt") - _lanes(lse_ref[rows, :], c1 - c0))
                if diag:
                    r, c = _span_iota(sr, rb, c0, c1)
                    p = jnp.where(c <= r, p, 0.0)
                dp = _dot(do_ref[rows, :], v_ref[c0:c1, :], "nt")
                acc_sc[rows, :] += _dot(p * (dp - _lanes(dl_ref[rows, :], c1 - c0)), ks)

        @pl.when(j < i)
        def _():
            tile(False)

        @pl.when(j == i)
        def _():
            tile(True)
            dq_ref[...] = acc_sc[...].astype(dq_ref.dtype)

    blk = lambda which: pl.BlockSpec((tq, LANES), which)
    qmap = lambda h, p, ii, jj: (ii[p], h)
    kmap = lambda h, p, ii, jj: (jj[p], h)
    return pl.pallas_call(
        body, name="mla_flash_dq",
        grid_spec=pltpu.PrefetchScalarGridSpec(
            num_scalar_prefetch=2, grid=(MLA_HEADS, int(ii.shape[0])),
            in_specs=[blk(qmap), blk(kmap), blk(kmap), blk(qmap), blk(qmap), blk(qmap)],
            out_specs=blk(qmap),
            scratch_shapes=[pltpu.VMEM((tq, LANES), F32)]),
        out_shape=jax.ShapeDtypeStruct((T, HP), BF16),
        compiler_params=_cparams(("parallel", "arbitrary")),
    )(ii, jj, q, k, v, do, lse, delta)


def _flash_dkv(q, k, v, do, lse_row, delta_row):
    T = q.shape[0]
    tq, sr = _flash_tiles(T)
    n = T // tq
    jj, ii = _pairs(n, False)

    def body(jj_ref, ii_ref, q_ref, k_ref, v_ref, do_ref, lse_ref, dl_ref, dk_ref, dv_ref, dk_sc, dv_sc):
        p_ = pl.program_id(1)
        j, i = jj_ref[p_], ii_ref[p_]

        @pl.when(i == j)
        def _():
            dk_sc[...] = jnp.zeros(dk_sc.shape, F32)
            dv_sc[...] = jnp.zeros(dv_sc.shape, F32)

        def tile(diag):
            for rb in range(tq // sr):
                rows = slice(rb * sr, (rb + 1) * sr)
                c0, c1 = _col_span(tq, sr, rb, diag, True)
                qs, dos = q_ref[c0:c1, :], do_ref[c0:c1, :]
                pt = jnp.exp2(_dot(k_ref[rows, :], qs, "nt") - lse_ref[:, c0:c1])
                if diag:
                    r, c = _span_iota(sr, rb, c0, c1)
                    pt = jnp.where(r <= c, pt, 0.0)
                dpt = _dot(v_ref[rows, :], dos, "nt")
                dv_sc[rows, :] += _dot(pt, dos)
                dk_sc[rows, :] += _dot(pt * (dpt - dl_ref[:, c0:c1]), qs)

        @pl.when(i == j)
        def _():
            tile(True)

        @pl.when(i > j)
        def _():
            tile(False)

        @pl.when(i == n - 1)
        def _():
            dk_ref[...] = (dk_sc[...] * LN2).astype(dk_ref.dtype)
            dv_ref[...] = dv_sc[...].astype(dv_ref.dtype)

    blk = lambda which: pl.BlockSpec((tq, LANES), which)
    qmap = lambda h, p, jj, ii: (ii[p], h)
    kmap = lambda h, p, jj, ii: (jj[p], h)
    row = pl.BlockSpec((None, 1, tq), lambda h, p, jj, ii: (h, 0, ii[p]))
    return pl.pallas_call(
        body, name="mla_flash_dkv",
        grid_spec=pltpu.PrefetchScalarGridSpec(
            num_scalar_prefetch=2, grid=(MLA_HEADS, int(ii.shape[0])),
            in_specs=[blk(qmap), blk(kmap), blk(kmap), blk(qmap), row, row],
            out_specs=[blk(kmap), blk(kmap)],
            scratch_shapes=[pltpu.VMEM((tq, LANES), F32)] * 2),
        out_shape=[jax.ShapeDtypeStruct((T, HP), BF16)] * 2,
        compiler_params=_cparams(("parallel", "arbitrary")),
    )(jj, ii, q, k, v, do, lse_row, delta_row)


def _mem_fwd(z, km, vm, gq):
    scale = MEM_HEAD_DIM ** -0.5

    def fn(qm, km, vm, gq):
        ys = []
        for h in range(MEM_HEADS):
            sl = slice(h * LANES, (h + 1) * LANES)
            q = _rms(qm[:, sl], gq) * scale
            s = _dot(q, km[:, sl], "nt")
            p = jnp.exp(s - jnp.max(s, axis=1, keepdims=True))
            p = p / jnp.sum(p, axis=1, keepdims=True)
            ys.append(_dot(p, vm[:, sl]))
        y = jnp.concatenate(ys, axis=1)
        return y, y

    return _rowwise(fn, "mem_fwd", 512, [(z, MEM_WIDTH, Z_QM // MEM_WIDTH)], [km, vm, gq],
                    [(MEM_WIDTH, BF16), (MEM_WIDTH, BF16, "T")])


def _mem_bwd(z, dy, km, vm, gq):
    scale = MEM_HEAD_DIM ** -0.5

    def fn(qm, dy, km, vm, gq):
        dqs, dks, dvs = [], [], []
        dgq = jnp.zeros((1, LANES), F32)
        for h in range(MEM_HEADS):
            sl = slice(h * LANES, (h + 1) * LANES)
            q = (_rms(qm[:, sl], gq) * scale).astype(BF16)
            dyh = dy[:, sl]
            kh, vh = km[:, sl], vm[:, sl]
            s = _dot(q, kh, "nt")
            p = jnp.exp(s - jnp.max(s, axis=1, keepdims=True))
            p = p / jnp.sum(p, axis=1, keepdims=True)
            dp = _dot(dyh, vh, "nt")
            ds = p * (dp - jnp.sum(p * dp, axis=1, keepdims=True))
            dq = _dot(ds, kh) * scale
            dx, dg = _rms_bwd(qm[:, sl], gq, dq)
            dqs.append(dx)
            dgq = dgq + dg
            st = _dot(kh, q, "nt")
            pt = jnp.exp(st - jnp.max(st, axis=0, keepdims=True))
            pt = pt / jnp.sum(pt, axis=0, keepdims=True)
            dpt = _dot(vh, dyh, "nt")
            dst = pt * (dpt - jnp.sum(pt * dpt, axis=0, keepdims=True))
            dvs.append(_dot(pt, dyh))
            dks.append(_dot(dst, q))
        return jnp.concatenate(dqs, axis=1), jnp.concatenate(dks, axis=1), jnp.concatenate(dvs, axis=1), dgq

    m = km.shape[0]
    return _rowwise(fn, "mem_bwd", 512, [(z, MEM_WIDTH, Z_QM // MEM_WIDTH), dy], [km, vm, gq],
                    [(MEM_WIDTH, BF16)], [((m, MEM_WIDTH), F32), ((m, MEM_WIDTH), F32), ((1, LANES), F32)])


GROUPS = {"ffn1": ["ffn1_w_gu"], "ffn1_down": ["ffn1_w_down"],
          "mix": ["w_in", "mla_w_uq", "mla_w_ukv", "mem_w_kv", "w_branch_a", "w_branch_b", "w_branch_c", "w_out"],
          "ffn2": ["ffn2_w_gu", "ffn2_w_down"]}
GRAD_GROUPS = {"ffn2": GROUPS["ffn2"], "mix": GROUPS["mix"], "ffn1_down": ["ffn1_w_down"], "ffn1_gu": ["ffn1_w_gu"]}


def _local_step(x, mem, positions, loss_target, P, weights, grads_out):
    T = x.shape[0]
    G = {}
    W = dict(weights("ffn1", None))

    half = MLA_ROPE // 2
    inv = ROPE_BASE ** (-jnp.arange(half, dtype=F32) / half)
    ang = positions.astype(F32)[:, None] * inv
    cos, sin = jnp.cos(ang), jnp.sin(ang)
    one, zero = jnp.ones((T, MLA_NOPE), F32), jnp.zeros((T, half), F32)
    pad = LANES - MLA_QK
    tabs = (jnp.concatenate([one, cos, cos, jnp.ones((T, pad), F32)], axis=1),
            jnp.concatenate([jnp.zeros((T, MLA_NOPE), F32), -sin, zero, jnp.zeros((T, pad), F32)], axis=1),
            jnp.concatenate([jnp.zeros((T, MLA_NOPE), F32), zero, sin, jnp.zeros((T, pad), F32)], axis=1))
    gq_p = jnp.pad(P["mla_q_norm"], ((0, 0), (0, pad)))
    gk_p = jnp.pad(P["mla_k_norm"], ((0, 0), (0, pad)))
    bias_full = jnp.repeat(P["sg_b"].T, SG_GROUP_DIM, axis=1)
    group_ind = jnp.repeat(jnp.eye(SG_GROUPS, dtype=F32), SG_GROUP_DIM, axis=0)

    HT = (D_MODEL, BF16, "T")

    def norm2(x, g):
        h = _rms(x, g)
        return h, h

    h1, h1t = _rowwise(norm2, "ffn1_norm", 512, [x], [P["ffn1_norm"]], [(D_MODEL, BF16), HT])
    def ffn1_w_down(after):
        W.update(weights("ffn1_down", after))
        return W["ffn1_w_down"]

    gu1, a1t, o1 = _ffn_fwd(h1, W["ffn1_w_gu"], ffn1_w_down, "ffn1")

    def resid_norm(x, o, g):
        xn = x + 0.5 * o
        h = _rms(xn, g)
        return xn, h, h

    x1, hm, hmt = _rowwise(resid_norm, "mix_norm", 512, [x, o1], [P["mix_norm"]],
                           [(D_MODEL, F32), (D_MODEL, BF16), HT])
    W.update(weights("mix", hm))
    z = _mm(hm, W["w_in"], "nn", BF16, "w_in", tm=1024, tn=1792)

    y_a, y_at = _sg_fwd(z, P["sg_ln_g"], P["sg_ln_b"], P["sg_w"], bias_full)

    def c_norm(cq, ckv, gq, gkv):
        a, b = _rms(cq, gq), _rms(ckv, gkv)
        return a, b, a, b

    cqn, ckvn, cqnt, ckvnt = _rowwise(
        c_norm, "mla_cnorm", 512, [(z, MLA_Q_RANK, Z_CQ // MLA_Q_RANK), (z, MLA_KV_RANK, Z_CKV // MLA_KV_RANK)],
        [P["mla_cq_norm"], P["mla_ckv_norm"]],
        [(MLA_Q_RANK, BF16), (MLA_KV_RANK, BF16), (MLA_Q_RANK, BF16, "T"), (MLA_KV_RANK, BF16, "T")])
    q_pre = _mm(cqn, W["mla_w_uq"], "nn", BF16, "mla_uq", tm=1024, tn=1024)
    kv_pre = _mm(ckvn, W["mla_w_ukv"], "nn", BF16, "mla_ukv", tm=1024, tn=1024)
    q, k, v = _mla_post(q_pre, kv_pre, z, tabs, gq_p, gk_p)
    y_b, y_bt, lse = _flash_fwd(q, k, v)

    memn, = _rowwise(lambda m, g: _rms(m, g), "mem_norm", 256, [mem], [P["mem_norm"]], [(D_MODEL, BF16)])
    kvm = _mm(memn, W["mem_w_kv"], "nn", F32, "mem_kv")

    def mem_k(kvm, gk):
        ks = [_rms(kvm[:, h * LANES:(h + 1) * LANES], gk) for h in range(MEM_HEADS)]
        return jnp.concatenate(ks, axis=1), kvm[:, MEM_WIDTH:]

    km, vm = _rowwise(mem_k, "mem_knorm", 256, [kvm], [P["mem_k_norm"]], [(MEM_WIDTH, BF16), (MEM_WIDTH, BF16)])
    y_c, y_ct = _mem_fwd(z, km, vm, P["mem_q_norm"])

    pa = _mm(y_a, W["w_branch_a"], "nn", BF16, "branch_a", tm=1024, tn=1024)
    pb = _mm(y_b, W["w_branch_b"], "nn", BF16, "branch_b", tm=1024, tn=1024)
    pc = _mm(y_c, W["w_branch_c"], "nn", BF16, "branch_c", tm=1024, tn=1024)

    def merge(zg, pa, pb, pc, b):
        g = _sigmoid(zg + b)
        m = g[:, :D_MODEL] * pa + g[:, D_MODEL:2 * D_MODEL] * pb + g[:, 2 * D_MODEL:] * pc
        return m, m

    merged, mergedt = _rowwise(merge, "merge", 256, [(z, 3 * D_MODEL, 0), pa, pb, pc], [P["b_gate"]],
                               [(D_MODEL, BF16), HT])
    om = _mm(merged, W["w_out"], "nn", BF16, "w_out", tm=1024, tn=1024)

    def resid_norm1(x, o, g):
        xn = x + o
        h = _rms(xn, g)
        return xn, h, h

    x2, h2, h2t = _rowwise(resid_norm1, "ffn2_norm", 512, [x1, om], [P["ffn2_norm"]],
                           [(D_MODEL, F32), (D_MODEL, BF16), HT])
    W.update(weights("ffn2", h2))
    gu2, a2t, o2 = _ffn_fwd(h2, W["ffn2_w_gu"], W["ffn2_w_down"], "ffn2")

    def loss_fn(x2, o2, t):
        e = x2 + 0.5 * o2 - t
        return e * (1.0 / D_MODEL), (e * (0.5 / D_MODEL)).astype(BF16), _rsum(e * e) * (0.5 / D_MODEL)

    dx3, do2, loss_part = _rowwise(loss_fn, "loss", 512, [x2, o2, loss_target], [],
                                   [(D_MODEL, F32), (D_MODEL, BF16)], [((1, D_MODEL), F32)])

    dh2, G["ffn2_w_gu"], G["ffn2_w_down"] = _ffn_bwd(do2, h2t, gu2, a2t, W["ffn2_w_gu"], W["ffn2_w_down"], "ffn2")
    tie = grads_out("ffn2", G)

    def norm_bwd(x, dh, dxo, g, *_):
        dx, dg = _rms_bwd(x, g, dh)
        dx = dx + dxo
        return dx, dx, dg

    dx2, dx2b, G["ffn2_norm"] = _rowwise(norm_bwd, "ffn2_norm_bwd", 512, [x2, dh2, dx3],
                                         [P["ffn2_norm"]] + ([] if tie is None else [tie]),
                                         [(D_MODEL, F32), (D_MODEL, BF16)], [((1, D_MODEL), F32)])

    G["w_out"] = _mm_t(mergedt, dx2b, "w_out_dw", tm=1024, tn=1024)
    dmerged = _mm(dx2b, W["w_out"], "nt", BF16, "w_out_dx", tm=1024, tn=1024)

    def merge_bwd(zg, pa, pb, pc, dm, b):
        g = _sigmoid(zg + b)
        ps = jnp.concatenate([pa, pb, pc], axis=1)
        dm3 = jnp.concatenate([dm, dm, dm], axis=1)
        dzg = dm3 * ps * g * (1.0 - g)
        dp = dm3 * g
        return dzg, dp[:, :D_MODEL], dp[:, D_MODEL:2 * D_MODEL], dp[:, 2 * D_MODEL:], _rsum(dzg)

    dzg, dpa, dpb, dpc, G["b_gate"] = _rowwise(
        merge_bwd, "merge_bwd", 256, [(z, 3 * D_MODEL, 0), pa, pb, pc, dmerged], [P["b_gate"]],
        [(3 * D_MODEL, BF16), (D_MODEL, BF16), (D_MODEL, BF16), (D_MODEL, BF16)], [((1, 3 * D_MODEL), F32)])

    G["w_branch_a"] = _mm_t(y_at, dpa, "branch_a_dw", tm=512, tn=1024)
    G["w_branch_b"] = _mm_t(y_bt, dpb, "branch_b_dw", tm=1024, tn=1024)
    G["w_branch_c"] = _mm_t(y_ct, dpc, "branch_c_dw", tm=512, tn=1024)
    dy_a = _mm(dpa, W["w_branch_a"], "nt", BF16, "branch_a_dx", tm=1024, tn=512)
    dy_b = _mm(dpb, W["w_branch_b"], "nt", BF16, "branch_b_dx", tm=1024, tn=1024)
    dy_c = _mm(dpc, W["w_branch_c"], "nt", BF16, "branch_c_dx", tm=1024, tn=512)

    du_pre, dv_pre, G["sg_w"], dbias_t, G["sg_ln_g"], G["sg_ln_b"] = _sg_bwd(
        z, dy_a, P["sg_ln_g"], P["sg_ln_b"], P["sg_w"], bias_full, group_ind)
    G["sg_b"] = dbias_t.T

    dqm, dkm, dvm, G["mem_q_norm"] = _mem_bwd(z, dy_c, km, vm, P["mem_q_norm"])

    def mem_k_bwd(kvm, dkm, dvm, gk):
        dks = []
        dg = jnp.zeros((1, LANES), F32)
        for h in range(MEM_HEADS):
            sl = slice(h * LANES, (h + 1) * LANES)
            dx, d = _rms_bwd(kvm[:, sl], gk, dkm[:, sl])
            dks.append(dx)
            dg = dg + d
        return jnp.concatenate(dks + [dvm], axis=1), dg

    dkvm, G["mem_k_norm"] = _rowwise(mem_k_bwd, "mem_knorm_bwd", 256, [kvm, dkm, dvm], [P["mem_k_norm"]],
                                     [(2 * MEM_WIDTH, BF16)], [((1, LANES), F32)])
    G["mem_w_kv"] = _mm(memn, dkvm, "tn", BF16, "mem_kv_dw")
    dmemn = _mm(dkvm, W["mem_w_kv"], "nt", F32, "mem_kv_dx")
    _, G["mem_norm"] = _rowwise(lambda m, d, g: _rms_bwd(m, g, d), "mem_norm_bwd", 256, [mem, dmemn],
                                [P["mem_norm"]], [(D_MODEL, BF16)], [((1, D_MODEL), F32)])

    def delta_fn(o, do):
        od = o.astype(F32) * do.astype(F32)
        ds = [jnp.broadcast_to(jnp.sum(od[:, h * LANES:(h + 1) * LANES], axis=1, keepdims=True), (od.shape[0], LANES))
              for h in range(MLA_HEADS)]
        return jnp.concatenate(ds, axis=1)

    delta, = _rowwise(delta_fn, "mla_delta", 512, [y_b, dy_b], [], [(HP, F32)])
    rowform = lambda a: a.reshape(T, MLA_HEADS, LANES)[:, :, 0].T.reshape(MLA_HEADS, 1, T)
    dq = _flash_dq(q, k, v, dy_b, lse, delta)
    dk, dv = _flash_dkv(q, k, v, dy_b, rowform(lse), rowform(delta))
    dq_pre, dkv_pre, dkr, dgq, dgk = _mla_post_bwd(q_pre, kv_pre, z, tabs, gq_p, gk_p, dq, dk, dv)
    G["mla_q_norm"], G["mla_k_norm"] = dgq[:, :MLA_QK], dgk[:, :MLA_QK]
    G["mla_w_uq"] = _mm_t(cqnt, dq_pre, "mla_uq_dw", tm=384, tn=1024)
    G["mla_w_ukv"] = _mm_t(ckvnt, dkv_pre, "mla_ukv_dw", tm=256, tn=2048)
    dcqn = _mm(dq_pre, W["mla_w_uq"], "nt", BF16, "mla_uq_dx", tm=1024)
    dckvn = _mm(dkv_pre, W["mla_w_ukv"], "nt", BF16, "mla_ukv_dx", tm=1024)

    def c_norm_bwd(cq, ckv, dcqn, dckvn, gq, gkv):
        dcq, dgq = _rms_bwd(cq, gq, dcqn)
        dckv, dgkv = _rms_bwd(ckv, gkv, dckvn)
        return dcq, dckv, dgq, dgkv

    dcq, dckv, G["mla_cq_norm"], G["mla_ckv_norm"] = _rowwise(
        c_norm_bwd, "mla_cnorm_bwd", 512,
        [(z, MLA_Q_RANK, Z_CQ // MLA_Q_RANK), (z, MLA_KV_RANK, Z_CKV // MLA_KV_RANK), dcqn, dckvn],
        [P["mla_cq_norm"], P["mla_ckv_norm"]], [(MLA_Q_RANK, BF16), (MLA_KV_RANK, BF16)],
        [((1, MLA_Q_RANK), F32), ((1, MLA_KV_RANK), F32)])

    dz = jnp.concatenate([dzg, du_pre, dv_pre, dqm, dckv, dkr, dcq], axis=1)
    G["w_in"] = _mm_t(hmt, dz, "w_in_dw", tm=1024, tn=1792, tk=2048)
    dhm = _mm(dz, W["w_in"], "nt", BF16, "w_in_dx", tm=1024, tn=1024, tk=2688)

    def norm_bwd_half(x, dh, dxo, g):
        dx, dg = _rms_bwd(x, g, dh)
        dx = dx + dxo
        return dx, (0.5 * dx), dg

    dx1, do1, G["mix_norm"] = _rowwise(norm_bwd_half, "mix_norm_bwd", 512, [x1, dhm, dx2], [P["mix_norm"]],
                                       [(D_MODEL, F32), (D_MODEL, BF16)], [((1, D_MODEL), F32)])
    tie = grads_out("mix", G)

    def ffn1_dw(which, dw):
        G["ffn1_w_" + which] = dw
        return grads_out("ffn1_" + which, G)

    dh1, _, _ = _ffn_bwd(do1, h1t, gu1, a1t, W["ffn1_w_gu"], W["ffn1_w_down"], "ffn1", tie, ffn1_dw)

    def norm_bwd_last(x, dh, dxo, g):
        dx, dg = _rms_bwd(x, g, dh)
        return dx + dxo, dg

    grad_x, G["ffn1_norm"] = _rowwise(norm_bwd_last, "ffn1_norm_bwd", 512, [x, dh1, dx1], [P["ffn1_norm"]],
                                      [(D_MODEL, F32)], [((1, D_MODEL), F32)])
    return loss_part, grad_x, G


SHARDED = ["ffn1_w_gu", "ffn1_w_down", "w_in", "mla_w_uq", "mla_w_ukv", "mem_w_kv",
           "w_branch_a", "w_branch_b", "w_branch_c", "w_out", "ffn2_w_gu", "ffn2_w_down"]
ROW_SHARDED = {"ffn1_w_down", "mem_w_kv", "w_out", "ffn2_w_down"}
SMALL = ["ffn1_norm", "mix_norm", "b_gate", "sg_ln_g", "sg_ln_b", "sg_w", "sg_b", "mla_cq_norm",
         "mla_ckv_norm", "mla_q_norm", "mla_k_norm", "mem_norm", "mem_q_norm", "mem_k_norm", "ffn2_norm"]
ORDER = ["ffn1_norm", "ffn1_w_gu", "ffn1_w_down", "mix_norm", "w_in", "b_gate", "sg_ln_g", "sg_ln_b", "sg_w",
         "sg_b", "mla_cq_norm", "mla_w_uq", "mla_ckv_norm", "mla_w_ukv", "mla_q_norm", "mla_k_norm", "mem_norm",
         "mem_w_kv", "mem_q_norm", "mem_k_norm", "w_branch_a", "w_branch_b", "w_branch_c", "w_out", "ffn2_norm",
         "ffn2_w_gu", "ffn2_w_down"]

_IN_U, _IN_V, _IN_CQ, _IN_CKV, _IN_KR, _IN_QM, _IN_G = 0, 512, 1024, 1408, 1664, 1696, 2208
IN_COLS = 5280


def _full_from_slabs(name, slabs):
    n, r, c = slabs.shape
    if name in ROW_SHARDED:
        return slabs.reshape(n * r, c)
    return slabs.transpose(1, 0, 2).reshape(r, n * c)


def _slabs_from_full(name, full):
    if name in ROW_SHARDED:
        return full.reshape(N_DEV, full.shape[0] // N_DEV, full.shape[1])
    r, c = full.shape
    return full.reshape(r, N_DEV, c // N_DEV).transpose(1, 0, 2)


def _compute_layout(full):
    W = dict(full)
    if "w_in" not in full:
        return W
    w = full["w_in"]
    kr = jnp.pad(w[:, _IN_KR:_IN_QM], ((0, 0), (KR_LANE, LANES - KR_LANE - MLA_ROPE)))
    W["w_in"] = jnp.concatenate([w[:, _IN_G:], w[:, _IN_U:_IN_CQ], w[:, _IN_QM:_IN_G], w[:, _IN_CKV:_IN_KR], kr,
                                 w[:, _IN_CQ:_IN_CKV]], axis=1)
    uq = full["mla_w_uq"].reshape(MLA_Q_RANK, MLA_HEADS, MLA_QK)
    W["mla_w_uq"] = jnp.pad(uq, ((0, 0), (0, 0), (0, LANES - MLA_QK))).reshape(MLA_Q_RANK, HP)
    ukv = full["mla_w_ukv"].reshape(MLA_KV_RANK, MLA_HEADS, MLA_NOPE + MLA_V)
    padh = lambda a: jnp.pad(a, ((0, 0), (0, 0), (0, LANES - a.shape[2]))).reshape(MLA_KV_RANK, HP)
    W["mla_w_ukv"] = jnp.concatenate([padh(ukv[:, :, :MLA_NOPE]), padh(ukv[:, :, MLA_NOPE:])], axis=1)
    wb = full["w_branch_b"].reshape(MLA_HEADS, MLA_V, D_MODEL)
    W["w_branch_b"] = jnp.pad(wb, ((0, 0), (0, LANES - MLA_V), (0, 0))).reshape(HP, D_MODEL)
    return W


def _reference_layout(G):
    out = dict(G)
    if "w_in" not in G:
        return out
    g = G["w_in"]
    out["w_in"] = jnp.concatenate([
        g[:, Z_U:Z_QM], g[:, Z_CQ:Z_COLS], g[:, Z_CKV:Z_KR], g[:, Z_KR + KR_LANE:Z_KR + KR_LANE + MLA_ROPE],
        g[:, Z_QM:Z_CKV], g[:, Z_G:Z_U]], axis=1)
    out["mla_w_uq"] = G["mla_w_uq"].reshape(MLA_Q_RANK, MLA_HEADS, LANES)[:, :, :MLA_QK].reshape(MLA_Q_RANK, -1)
    gk = G["mla_w_ukv"][:, :HP].reshape(MLA_KV_RANK, MLA_HEADS, LANES)[:, :, :MLA_NOPE]
    gv = G["mla_w_ukv"][:, HP:].reshape(MLA_KV_RANK, MLA_HEADS, LANES)[:, :, :MLA_V]
    out["mla_w_ukv"] = jnp.concatenate([gk, gv], axis=2).reshape(MLA_KV_RANK, -1)
    out["w_branch_b"] = G["w_branch_b"].reshape(MLA_HEADS, LANES, D_MODEL)[:, :MLA_V].reshape(-1, D_MODEL)
    return out


def _pack(parts):
    flat = []
    for a in parts:
        a = a.reshape(-1)
        flat.append(jnp.pad(a, (0, (-a.shape[0]) % LANES)))
    return jnp.concatenate(flat).reshape(-1, LANES)


def _unpack(packed, shapes):
    flat = packed.reshape(-1)
    out, off = [], 0
    for shp in shapes:
        n = int(np.prod(shp))
        out.append(flat[off:off + n].reshape(shp))
        off += n + (-n) % LANES
    return out


MESH = pl.DeviceIdType.MESH
HBM = pl.BlockSpec(memory_space=pltpu.HBM)


def _all_gather(shards):
    n = len(shards)

    def body(*refs):
        x_refs, out_refs, token_ref = refs[:n], refs[n:2 * n], refs[2 * n]
        send_sems, recv_sems, local_sems = refs[2 * n + 1:]
        x, y, c = lax.axis_index("x"), lax.axis_index("y"), lax.axis_index("c")
        me, sibling = (x, y, c), (x, y, 1 - c)
        chips = [(1 - x, y), (x, 1 - y), (1 - x, 1 - y)]
        token_ref[...] = jnp.zeros_like(token_ref)

        def slot(a, px, py, pc):
            return out_refs[a].at[4 * px + 2 * py + pc]

        def copy(a, k, block, to, src=None):
            return pltpu.make_async_remote_copy(
                src_ref=slot(a, *block) if src is None else src, dst_ref=slot(a, *block),
                send_sem=send_sems.at[7 * a + k], recv_sem=recv_sems.at[7 * a + k], device_id=to, device_id_type=MESH)

        arrays = range(n)
        mine = [pltpu.make_async_copy(x_refs[a], slot(a, *me), local_sems.at[a]) for a in arrays]
        for cp in mine:
            cp.start()
        first = [copy(a, 0, me, sibling, src=x_refs[a]) for a in arrays]
        first += [copy(a, 1 + j, me, (*chip, c), src=x_refs[a]) for j, chip in enumerate(chips) for a in arrays]
        for cp in first:
            cp.start()
        passed = []
        for j, chip in enumerate(chips):
            for a in arrays:
                copy(a, 1 + j, (*chip, c), me).wait_recv()
                passed.append(copy(a, 4 + j, (*chip, c), sibling))
                passed[-1].start()
        for a in arrays:
            copy(a, 0, sibling, me).wait_recv()
        for j, chip in enumerate(chips):
            for a in arrays:
                copy(a, 4 + j, (*chip, 1 - c), me).wait_recv()
        for cp in first + passed:
            cp.wait_send()
        for cp in mine:
            cp.wait()

    res = pl.pallas_call(
        body, name="all_gather_weights",
        out_shape=[jax.ShapeDtypeStruct((N_DEV,) + s.shape, s.dtype) for s in shards]
        + [jax.ShapeDtypeStruct((8, LANES), F32)],
        in_specs=[HBM] * n, out_specs=[HBM] * n + [pl.BlockSpec(memory_space=pltpu.VMEM)],
        scratch_shapes=[pltpu.SemaphoreType.DMA((7 * n,)), pltpu.SemaphoreType.DMA((7 * n,)),
                        pltpu.SemaphoreType.DMA((n,))],
    )(*shards)
    return res[:n], res[n]


SEM = pl.BlockSpec(memory_space=pltpu.SEMAPHORE)
DATAFLOW = pltpu.SideEffectType.DATAFLOW_SIDE_EFFECTING


def _peers():
    x, y, c = lax.axis_index("x"), lax.axis_index("y"), lax.axis_index("c")
    out = []
    for k in range(1, N_DEV):
        px = 1 - x if k & 4 else x
        py = 1 - y if k & 2 else y
        pc = 1 - c if k & 1 else c
        out.append((k, (px, py, pc), 4 * px + 2 * py + pc))
    return 4 * x + 2 * y + c, out


def _send_start(srcs, per_peer, name):
    n = len(srcs)
    lands = [lax.empty((N_DEV,) + (s.shape[1:] if per_peer else s.shape), s.dtype) for s in srcs]

    def body(*refs):
        src_refs, land_refs, send_sems, recv_sems, token = refs[:n], refs[n:2 * n], refs[2 * n], refs[2 * n + 1], refs[-1]
        me, peers = _peers()
        for a in range(n):
            for k, pid, pflat in peers:
                pltpu.make_async_remote_copy(
                    src_ref=src_refs[a].at[pflat] if per_peer else src_refs[a], dst_ref=land_refs[a].at[me],
                    send_sem=send_sems.at[7 * a + k - 1], recv_sem=recv_sems.at[7 * a + k - 1],
                    device_id=pid, device_id_type=MESH).start()
        token[...] = jnp.zeros_like(token)

    hbm = lambda a: pltpu.with_memory_space_constraint(a, pltpu.HBM)
    res = pl.pallas_call(
        body, name=name,
        out_shape=(pltpu.SemaphoreType.DMA((7 * n,)), pltpu.SemaphoreType.DMA((7 * n,)),
                   *[pltpu.HBM(a.shape, a.dtype) for a in srcs + lands], jax.ShapeDtypeStruct((8, LANES), F32)),
        in_specs=(HBM,) * (2 * n), out_specs=(SEM, SEM) + (HBM,) * (2 * n) + (pl.BlockSpec(memory_space=pltpu.VMEM),),
        input_output_aliases={i: 2 + i for i in range(2 * n)},
        compiler_params=pltpu.CompilerParams(has_side_effects=DATAFLOW),
    )(*[hbm(a) for a in srcs + lands])
    return (res[0], res[1], list(res[2:2 + n]), list(res[2 + n:2 + 2 * n])), res[-1]


def _send_wait(started, after, per_peer, name):
    send_sems, recv_sems, srcs_thru, lands_thru = started
    n = len(srcs_thru)

    def body(*refs):
        src_refs, land_refs, send_sems, recv_sems = refs[:n], refs[n:2 * n], refs[2 * n], refs[2 * n + 1]
        me, peers = _peers()
        for a in range(n):
            for k, pid, pflat in peers:
                copy = pltpu.make_async_remote_copy(
                    src_ref=src_refs[a].at[pflat] if per_peer else src_refs[a], dst_ref=land_refs[a].at[pflat],
                    send_sem=send_sems.at[7 * a + k - 1], recv_sem=recv_sems.at[7 * a + k - 1],
                    device_id=pid, device_id_type=MESH)
                copy.wait_send()
                copy.wait_recv()

    outs = pl.pallas_call(
        body, name=name,
        out_shape=tuple(pltpu.HBM(a.shape, a.dtype) for a in srcs_thru + lands_thru),
        in_specs=(HBM,) * (2 * n) + (SEM, SEM, pl.BlockSpec(memory_space=pl.ANY)), out_specs=(HBM,) * (2 * n),
        input_output_aliases={i: i for i in range(2 * n)},
        compiler_params=pltpu.CompilerParams(has_side_effects=DATAFLOW),
    )(*srcs_thru, *lands_thru, send_sems, recv_sems, after)
    me = 4 * lax.axis_index("x") + 2 * lax.axis_index("y") + lax.axis_index("c")
    landed = []
    for src_out, land in zip(outs[:n], outs[n:]):
        own = lax.dynamic_index_in_dim(src_out, me, 0, keepdims=True) if per_peer else src_out[None]
        landed.append(lax.dynamic_update_slice(land, own, (me,) + (0,) * (land.ndim - 1)))
    return landed


def _share_rows(block, name):
    def body(src_ref, out_ref, send_sems, recv_sems, local_sem):
        me, peers = _peers()
        own = pltpu.make_async_copy(src_ref, out_ref.at[me], local_sem)
        own.start()
        copies = [pltpu.make_async_remote_copy(
            src_ref=src_ref, dst_ref=out_ref.at[me], send_sem=send_sems.at[k - 1], recv_sem=recv_sems.at[k - 1],
            device_id=pid, device_id_type=MESH) for k, pid, _ in peers]
        for cp in copies:
            cp.start()
        for cp in copies:
            cp.wait()
        own.wait()

    return pl.pallas_call(
        body, name=name, out_shape=jax.ShapeDtypeStruct((N_DEV,) + block.shape, block.dtype),
        in_specs=[HBM], out_specs=HBM,
        scratch_shapes=[pltpu.SemaphoreType.DMA((N_DEV - 1,)), pltpu.SemaphoreType.DMA((N_DEV - 1,)),
                        pltpu.SemaphoreType.DMA],
    )(block)


def _sum_slots(recv, name, tr):
    n, rows, lanes = recv.shape
    tr = _tile(rows, tr)

    def body(r_ref, o_ref):
        acc = r_ref[0].astype(F32)
        for i in range(1, n):
            acc = acc + r_ref[i].astype(F32)
        o_ref[...] = acc

    return pl.pallas_call(
        body, name=name, grid=(rows // tr,),
        in_specs=[pl.BlockSpec((n, tr, lanes), lambda i: (0, i, 0))],
        out_specs=pl.BlockSpec((tr, lanes), lambda i: (i, 0)),
        out_shape=jax.ShapeDtypeStruct((rows, lanes), F32),
        compiler_params=_cparams(("parallel",)),
    )(recv)


def _adamw_math(w, g, m, v):
    m = ADAM_B1 * m + (1.0 - ADAM_B1) * g
    v = ADAM_B2 * v + (1.0 - ADAM_B2) * (g * g)
    m_hat = m / (1.0 - ADAM_B1 ** ADAM_STEP)
    v_hat = v / (1.0 - ADAM_B2 ** ADAM_STEP)
    return -ADAM_LR * (m_hat / (jnp.sqrt(v_hat) + ADAM_EPS) + ADAM_WD * w), m, v


def _adamw(w, g, m, v, name, tr=256):
    return _rowwise(_adamw_math, name, tr, [w, g, m, v], [], [(w.shape[1], F32)] * 3)


def _adamw_small(ws, gs, ms, vs):
    n = len(ws)

    def body(*refs):
        ins, outs = refs[:4 * n], refs[4 * n:]
        for i in range(n):
            d, m, v = _adamw_math(ins[i][...], ins[n + i][...], ins[2 * n + i][...], ins[3 * n + i][...])
            outs[i][...], outs[n + i][...], outs[2 * n + i][...] = d, m, v

    vmem = pl.BlockSpec(memory_space=pltpu.VMEM)
    res = pl.pallas_call(
        body, name="adamw_small", in_specs=[vmem] * (4 * n), out_specs=[vmem] * (3 * n),
        out_shape=[jax.ShapeDtypeStruct(w.shape, F32) for w in ws] * 3,
    )(*ws, *gs, *ms, *vs)
    return res[:n], res[n:2 * n], res[2 * n:]


def _sum_adamw(recv, w, m, v, name):
    n, r, c = recv.shape
    tr = _tile(r, 256)

    def body(r_ref, w_ref, m_ref, v_ref, g_ref, d_ref, nm_ref, nv_ref):
        g = r_ref[0].astype(F32)
        for i in range(1, n):
            g = g + r_ref[i].astype(F32)
        g_ref[...] = g
        d_ref[...], nm_ref[...], nv_ref[...] = _adamw_math(w_ref[...], g, m_ref[...], v_ref[...])

    row = pl.BlockSpec((tr, c), lambda i: (i, 0))
    return pl.pallas_call(
        body, name=name, grid=(r // tr,),
        in_specs=[pl.BlockSpec((n, tr, c), lambda i: (0, i, 0)), row, row, row], out_specs=[row] * 4,
        out_shape=[jax.ShapeDtypeStruct((r, c), F32)] * 4, compiler_params=_cparams(("parallel",)),
    )(recv, w, m, v)


def kernel(x, mem, positions, ffn1_norm, ffn1_w_gu, ffn1_w_down, mix_norm, w_in, b_gate, sg_ln_g, sg_ln_b, sg_w, sg_b, mla_cq_norm, mla_w_uq, mla_ckv_norm, mla_w_ukv, mla_q_norm, mla_k_norm, mem_norm, mem_w_kv, mem_q_norm, mem_k_norm, w_branch_a, w_branch_b, w_branch_c, w_out, ffn2_norm, ffn2_w_gu, ffn2_w_down, loss_target, m_ffn1_norm, m_ffn1_w_gu, m_ffn1_w_down, m_mix_norm, m_w_in, m_b_gate, m_sg_ln_g, m_sg_ln_b, m_sg_w, m_sg_b, m_mla_cq_norm, m_mla_w_uq, m_mla_ckv_norm, m_mla_w_ukv, m_mla_q_norm, m_mla_k_norm, m_mem_norm, m_mem_w_kv, m_mem_q_norm, m_mem_k_norm, m_w_branch_a, m_w_branch_b, m_w_branch_c, m_w_out, m_ffn2_norm, m_ffn2_w_gu, m_ffn2_w_down, v_ffn1_norm, v_ffn1_w_gu, v_ffn1_w_down, v_mix_norm, v_w_in, v_b_gate, v_sg_ln_g, v_sg_ln_b, v_sg_w, v_sg_b, v_mla_cq_norm, v_mla_w_uq, v_mla_ckv_norm, v_mla_w_ukv, v_mla_q_norm, v_mla_k_norm, v_mem_norm, v_mem_w_kv, v_mem_q_norm, v_mem_k_norm, v_w_branch_a, v_w_branch_b, v_w_branch_c, v_w_out, v_ffn2_norm, v_ffn2_w_gu, v_ffn2_w_down):
    given = dict(locals())
    wts = {n: given[n] for n in ORDER}
    mom = {n: given["m_" + n] for n in ORDER}
    var = {n: given["v_" + n] for n in ORDER}

    def shards(group, zero):
        out = [wts[n][0].astype(BF16) for n in GROUPS[group]]
        return [out[0] + zero.astype(BF16)] + out[1:]

    def full_weights(group, slabs):
        return _compute_layout({n: _full_from_slabs(n, s) for n, s in zip(GROUPS[group], slabs)})

    def zero_of(a):
        return jnp.minimum(jnp.abs(a.reshape(-1)[0]), 0)

    gathered_ffn1, token = _all_gather([wts[n][0].astype(BF16) for n in GROUPS["ffn1"]])
    flight = {}
    flight["ffn1_down"], token = _send_start(shards("ffn1_down", token[0, 0]), False, "gather_ffn1_down_start")
    flight["mix"] = _send_start(shards("mix", token[0, 0]), False, "gather_mix_start")[0]
    recv = {}

    def weights(group, after):
        if group == "ffn1":
            return full_weights(group, gathered_ffn1)
        landed = _send_wait(flight.pop(group), after, False, f"gather_{group}_wait")
        if group == "mix":
            flight["ffn2"] = _send_start(shards("ffn2", zero_of(landed[0])), False, "gather_ffn2_start")[0]
        return full_weights(group, landed)

    small_shapes = [wts[n].shape[1:] for n in SMALL]
    early = SMALL[1:]
    assert SMALL[0] == "ffn1_norm"

    def grads_out(group, G):
        Gr = _reference_layout({n: G[n] for n in GRAD_GROUPS[group]})
        parts = [_slabs_from_full(n, Gr[n]).astype(BF16) for n in GRAD_GROUPS[group]]
        flight["g_" + group], tie = _send_start(parts, True, f"grads_{group}_start")
        if group == "mix":
            small = _pack([G[n].reshape(s) for n, s in zip(early, small_shapes[1:])])
            small = jnp.pad(small, ((0, (-small.shape[0]) % 8), (0, 0)))
            flight["small"], tie = _send_start([small + tie[0, 0]], False, "grads_small_start")
        return tie

    P = {n: wts[n] if wts[n].ndim == 2 else wts[n][0] for n in SMALL}
    loss_part, grad_x, G = _local_step(x[0], mem[0], positions[0], loss_target[0], P, weights, grads_out)

    for group, names in GRAD_GROUPS.items():
        recv.update(zip(names, _send_wait(flight.pop("g_" + group), grad_x, True, f"grads_{group}_wait")))
    early_recv, = _send_wait(flight.pop("small"), grad_x, False, "grads_small_wait")
    last = _share_rows(G["ffn1_norm"].reshape(-1, LANES), "share_ffn1_norm")
    g_small_packed = _sum_slots(jnp.concatenate([last, early_recv], axis=1), "sum_small", 2048)

    grads, delta, new_m, new_v = {}, {}, {}, {}
    for n in SHARDED:
        grads[n], delta[n], new_m[n], new_v[n] = _sum_adamw(recv[n], wts[n][0], mom[n][0], var[n][0], "adamw_" + n)
    grads.update(zip(SMALL, _unpack(g_small_packed, small_shapes)))

    flat2 = lambda d: [d[n].reshape(-1, d[n].shape[-1]) for n in SMALL]
    for dst, vals in zip((delta, new_m, new_v), _adamw_small(flat2(wts), flat2(grads), flat2(mom), flat2(var))):
        dst.update(zip(SMALL, vals))

    loss = lax.psum(jnp.sum(loss_part), ("x", "y", "c"))
    lead = lambda d: [d[n].reshape(wts[n].shape) for n in ORDER]
    return (loss, grad_x[None], *lead(grads), *lead(delta), *lead(new_m), *lead(new_v))
```

```python
import functools

import numpy as np
import jax
import jax.numpy as jnp
from jax import lax
from jax.experimental import pallas as pl
from jax.experimental.pallas import tpu as pltpu

F32, BF16 = jnp.float32, jnp.bfloat16

D_MODEL = 1024
SG_GROUPS, SG_GROUP_DIM, SG_WIDTH, CHUNK = 8, 64, 512, 128
MLA_HEADS, MLA_NOPE, MLA_ROPE, MLA_V, MLA_QK = 8, 64, 32, 64, 96
MLA_Q_RANK, MLA_KV_RANK = 384, 256
MEM_HEADS, MEM_HEAD_DIM, MEM_WIDTH = 4, 128, 512
D_FF = 2816
ROPE_BASE = 10000.0
EPS = 1e-6
NEG = -1e30
ADAM_LR, ADAM_B1, ADAM_B2, ADAM_EPS, ADAM_WD, ADAM_STEP = 0.001, 0.9, 0.999, 1e-08, 0.01, 10

N_DEV = 8
LANES = 128
V7X_VMEM_LIMIT = 56 * 1024 * 1024
HP = MLA_HEADS * LANES

Z_G, Z_U, Z_V, Z_QM, Z_CKV, Z_KR, Z_CQ = 0, 3072, 3584, 4096, 4608, 4864, 4992
Z_COLS = 5376
KR_LANE = 64


def _tile(dim, pref):
    if dim <= pref:
        return dim
    for t in range(pref - pref % LANES, LANES - 1, -LANES):
        if dim % t == 0:
            return t
    for t in range(pref - pref % 8, 7, -8):
        if dim % t == 0:
            return t
    return dim


def _cparams(sem):
    return pltpu.CompilerParams(dimension_semantics=sem, vmem_limit_bytes=V7X_VMEM_LIMIT)


_DN = {"nn": ((1,), (0,)), "nt": ((1,), (1,)), "tn": ((0,), (0,))}


def _dot(a, b, mode="nn"):
    return lax.dot_general(a.astype(BF16), b.astype(BF16), (_DN[mode], ((), ())),
                           preferred_element_type=F32)


def _mm(a, b, mode, out_dtype, name, tm=512, tn=512, tk=2048, tie=None):
    if mode == "tn":
        K, M = a.shape
    else:
        M, K = a.shape
    N = b.shape[0] if mode == "nt" else b.shape[1]
    tm, tn, tk = _tile(M, tm), _tile(N, tn), _tile(K, tk)
    nk = K // tk
    if mode == "tn":
        a_spec = pl.BlockSpec((tk, tm), lambda i, j, k: (k, i))
    else:
        a_spec = pl.BlockSpec((tm, tk), lambda i, j, k: (i, k))
    if mode == "nt":
        b_spec = pl.BlockSpec((tn, tk), lambda i, j, k: (j, k))
    else:
        b_spec = pl.BlockSpec((tk, tn), lambda i, j, k: (k, j))

    ties = [] if tie is None else [tie]

    def body(a_ref, b_ref, *rest):
        o_ref, *scratch = rest[len(ties):]
        p = _dot(a_ref[...], b_ref[...], mode)
        if nk == 1:
            o_ref[...] = p.astype(o_ref.dtype)
        else:
            acc_ref, = scratch
            k = pl.program_id(2)

            @pl.when(k == 0)
            def _():
                acc_ref[...] = p

            @pl.when(k > 0)
            def _():
                acc_ref[...] += p

            @pl.when(k == nk - 1)
            def _():
                o_ref[...] = acc_ref[...].astype(o_ref.dtype)

    return pl.pallas_call(
        body, name=name, grid=(M // tm, N // tn, nk),
        in_specs=[a_spec, b_spec] + [pl.BlockSpec(t.shape, lambda i, j, k: (0, 0)) for t in ties],
        out_specs=pl.BlockSpec((tm, tn), lambda i, j, k: (i, j)),
        out_shape=jax.ShapeDtypeStruct((M, N), out_dtype),
        scratch_shapes=[] if nk == 1 else [pltpu.VMEM((tm, tn), F32)],
        compiler_params=_cparams(("parallel", "parallel", "arbitrary")),
    )(a, b, *ties)


def _mm_t(at, b, name, tm, tn, tk=1024, tie=None):
    return _mm(at, b, "nn", BF16, name, tm=tm, tn=tn, tk=tk, tie=tie)


def _rowwise(fn, name, tr, row_ins, bc_ins, row_outs, acc_outs=()):
    norm = [it if isinstance(it, tuple) else (it, it.shape[1], 0) for it in row_ins]
    rows = norm[0][0].shape[0]
    tr = _tile(rows, tr)
    arrays, in_specs = [], []
    for arr, w, cb in norm:
        arrays.append(arr)
        in_specs.append(pl.BlockSpec((tr, w), lambda i, cb=cb: (i, cb)))
    for arr in bc_ins:
        arrays.append(arr)
        in_specs.append(pl.BlockSpec(arr.shape, lambda i, nd=arr.ndim: (0,) * nd))
    out_shape, out_specs = [], []
    transposed = [len(o) == 3 for o in row_outs]
    for (w, dt, *_), t in zip(row_outs, transposed):
        out_shape.append(jax.ShapeDtypeStruct((w, rows) if t else (rows, w), dt))
        out_specs.append(pl.BlockSpec((w, tr), lambda i: (0, i)) if t else pl.BlockSpec((tr, w), lambda i: (i, 0)))
    for shp, dt in acc_outs:
        out_shape.append(jax.ShapeDtypeStruct(shp, dt))
        out_specs.append(pl.BlockSpec(shp, lambda i, nd=len(shp): (0,) * nd))
    n_in, n_row = len(arrays), len(row_outs)

    def body(*refs):
        vals = fn(*[r[...].astype(F32) for r in refs[:n_in]])
        if not isinstance(vals, (tuple, list)):
            vals = (vals,)
        outs = refs[n_in:]
        for r, v, t in zip(outs[:n_row], vals[:n_row], transposed):
            r[...] = v.astype(F32).T.astype(r.dtype) if t else v.astype(r.dtype)
        if acc_outs:
            accs = list(zip(outs[n_row:], vals[n_row:]))
            i = pl.program_id(0)

            @pl.when(i == 0)
            def _():
                for r, v in accs:
                    r[...] = v.astype(r.dtype)

            @pl.when(i > 0)
            def _():
                for r, v in accs:
                    r[...] += v.astype(r.dtype)

    res = pl.pallas_call(
        body, name=name, grid=(rows // tr,), in_specs=in_specs, out_specs=out_specs,
        out_shape=out_shape, compiler_params=_cparams(("arbitrary",)),
    )(*arrays)
    return res


def _rsum(x):
    return jnp.sum(x, axis=0, keepdims=True)


def _rms(x, g, n=None):
    n = x.shape[-1] if n is None else n
    r = lax.rsqrt(jnp.sum(x * x, axis=-1, keepdims=True) * (1.0 / n) + EPS)
    return x * r * g


def _rms_bwd(x, g, dy, n=None):
    n = x.shape[-1] if n is None else n
    r = lax.rsqrt(jnp.sum(x * x, axis=-1, keepdims=True) * (1.0 / n) + EPS)
    xh = x * r
    dxh = dy * g
    dx = r * (dxh - xh * (jnp.sum(dxh * xh, axis=-1, keepdims=True) * (1.0 / n)))
    return dx, _rsum(dy * xh)


def _gelu(x):
    return 0.5 * x * (1.0 + lax.erf(x * 0.7071067811865476))


def _gelu_grad(x):
    return 0.5 * (1.0 + lax.erf(x * 0.7071067811865476)) + x * jnp.exp(-0.5 * x * x) * 0.3989422804014327


def _sigmoid(x):
    return 1.0 / (1.0 + jnp.exp(-x))


FFN_TM, FFN_TN = 512, 1408
MXU_WIDTH = 256


def _col_chunks(n):
    return [(c, min(c + MXU_WIDTH, n)) for c in range(0, n, MXU_WIDTH)]


def _ffn_gu_act(h, w_gu, tag):
    T = h.shape[0]
    tm, tn = _tile(T, FFN_TM), FFN_TN
    nj = D_FF // tn

    def body(h_ref, wg_ref, wu_ref, gu_ref, a_ref, at_ref):
        h = h_ref[...]
        for c0, c1 in _col_chunks(tn):
            g = _dot(h, wg_ref[:, c0:c1])
            u = _dot(h, wu_ref[:, c0:c1])
            gu_ref[0, :, c0:c1] = g.astype(BF16)
            gu_ref[1, :, c0:c1] = u.astype(BF16)
            a = g * _sigmoid(g) * u
            a_ref[:, c0:c1] = a.astype(BF16)
            at_ref[c0:c1, :] = a.T.astype(BF16)

    return pl.pallas_call(
        body, name=f"{tag}_gu_act", grid=(T // tm, nj),
        in_specs=[pl.BlockSpec((tm, D_MODEL), lambda i, j: (i, 0)),
                  pl.BlockSpec((D_MODEL, tn), lambda i, j: (0, j)),
                  pl.BlockSpec((D_MODEL, tn), lambda i, j: (0, j + nj))],
        out_specs=[pl.BlockSpec((2, tm, tn), lambda i, j: (0, i, j)),
                   pl.BlockSpec((tm, tn), lambda i, j: (i, j)),
                   pl.BlockSpec((tn, tm), lambda i, j: (j, i))],
        out_shape=[jax.ShapeDtypeStruct((2, T, D_FF), BF16), jax.ShapeDtypeStruct((T, D_FF), BF16),
                   jax.ShapeDtypeStruct((D_FF, T), BF16)],
        compiler_params=_cparams(("parallel", "parallel")),
    )(h, w_gu, w_gu)


def _ffn_da_actbwd(do, w_down, gu, tag, tie=None):
    T = do.shape[0]
    tm, tn = _tile(T, FFN_TM), FFN_TN
    ties = [] if tie is None else [tie]

    def body(do_ref, wd_ref, gu_ref, *rest):
        dgu_ref = rest[-1]
        do = do_ref[...]
        for c0, c1 in _col_chunks(tn):
            da = _dot(do, wd_ref[c0:c1, :], "nt")
            g = gu_ref[0, :, c0:c1].astype(F32)
            u = gu_ref[1, :, c0:c1].astype(F32)
            s = _sigmoid(g)
            dgu_ref[0, :, c0:c1] = (da * u * s * (1.0 + g * (1.0 - s))).astype(BF16)
            dgu_ref[1, :, c0:c1] = (da * g * s).astype(BF16)

    return pl.pallas_call(
        body, name=f"{tag}_da_actbwd", grid=(T // tm, D_FF // tn),
        in_specs=[pl.BlockSpec((tm, D_MODEL), lambda i, j: (i, 0)),
                  pl.BlockSpec((tn, D_MODEL), lambda i, j: (j, 0)),
                  pl.BlockSpec((2, tm, tn), lambda i, j: (0, i, j))]
        + [pl.BlockSpec(t.shape, lambda i, j: (0, 0)) for t in ties],
        out_specs=pl.BlockSpec((2, tm, tn), lambda i, j: (0, i, j)),
        out_shape=jax.ShapeDtypeStruct((2, T, D_FF), BF16),
        compiler_params=_cparams(("parallel", "parallel")),
    )(do, w_down, gu, *ties)


def _ffn_dwgu(ht, dgu, tag, tk=2048):
    T = ht.shape[1]
    tn, tk = FFN_TN, _tile(T, tk)
    nj, nk = D_FF // tn, T // tk

    def body(a_ref, b_ref, o_ref, acc_ref):
        k = pl.program_id(1)
        p = _dot(a_ref[...], b_ref[...])

        @pl.when(k == 0)
        def _():
            acc_ref[...] = p

        @pl.when(k > 0)
        def _():
            acc_ref[...] += p

        @pl.when(k == nk - 1)
        def _():
            o_ref[...] = acc_ref[...].astype(o_ref.dtype)

    return pl.pallas_call(
        body, name=f"{tag}_dwgu", grid=(2 * nj, nk),
        in_specs=[pl.BlockSpec((D_MODEL, tk), lambda n, k: (0, k)),
                  pl.BlockSpec((None, tk, tn), lambda n, k: (n // nj, k, n % nj))],
        out_specs=pl.BlockSpec((D_MODEL, tn), lambda n, k: (0, n)),
        out_shape=jax.ShapeDtypeStruct((D_MODEL, 2 * D_FF), BF16),
        scratch_shapes=[pltpu.VMEM((D_MODEL, tn), F32)],
        compiler_params=_cparams(("parallel", "arbitrary")),
    )(ht, dgu)


def _ffn_dh(dgu, w_gu, tag, tm=2048, tie=None):
    T = dgu.shape[1]
    tm, tk = _tile(T, tm), FFN_TN
    nk = D_FF // tk
    ties = [] if tie is None else [tie]

    def body(a_ref, b_ref, *rest):
        o_ref, acc_ref = rest[len(ties):]
        k = pl.program_id(1)
        p = _dot(a_ref[...], b_ref[...], "nt")

        @pl.when(k == 0)
        def _():
            acc_ref[...] = p

        @pl.when(k > 0)
        def _():
            acc_ref[...] += p

        @pl.when(k == 2 * nk - 1)
        def _():
            o_ref[...] = acc_ref[...].astype(o_ref.dtype)

    return pl.pallas_call(
        body, name=f"{tag}_dh", grid=(T // tm, 2 * nk),
        in_specs=[pl.BlockSpec((None, tm, tk), lambda i, k: (k // nk, i, k % nk)),
                  pl.BlockSpec((D_MODEL, tk), lambda i, k: (0, k))]
        + [pl.BlockSpec(t.shape, lambda i, k: (0, 0)) for t in ties],
        out_specs=pl.BlockSpec((tm, D_MODEL), lambda i, k: (i, 0)),
        out_shape=jax.ShapeDtypeStruct((T, D_MODEL), BF16),
        scratch_shapes=[pltpu.VMEM((tm, D_MODEL), F32)],
        compiler_params=_cparams(("parallel", "arbitrary")),
    )(dgu, w_gu, *ties)


def _ffn_fwd(h, w_gu, w_down, tag):
    gu, a, at = _ffn_gu_act(h, w_gu, tag)
    if callable(w_down):
        w_down = w_down(at)
    o = _mm(a, w_down, "nn", BF16, f"{tag}_down", tm=1024, tn=1024, tk=2816)
    return gu, at, o


def _ffn_bwd(do, ht, gu, at, w_gu, w_down, tag, tie=None, on_dw=None):
    on_dw = on_dw or (lambda which, dw: None)
    dw_down = _mm_t(at, do, f"{tag}_dwdown", tm=1408, tn=1024, tk=2048, tie=tie)
    dgu = _ffn_da_actbwd(do, w_down, gu, tag, tie=on_dw("down", dw_down))
    dw_gu = _ffn_dwgu(ht, dgu, tag)
    dh = _ffn_dh(dgu, w_gu, tag, tie=on_dw("gu", dw_gu))
    return dh, dw_gu, dw_down


def _sg_common(u_pre, v_pre, ln_g, ln_b):
    u = _gelu(u_pre)
    v = _gelu(v_pre)
    mu = jnp.mean(v, axis=-1, keepdims=True)
    vc = v - mu
    rstd = lax.rsqrt(jnp.mean(vc * vc, axis=-1, keepdims=True) + EPS)
    vhat = vc * rstd
    vl = vhat * ln_g + ln_b
    return u, vhat, rstd, vl


def _sg_masked_pairs(w):
    t = lax.broadcasted_iota(jnp.int32, (CHUNK, CHUNK), 0)
    s = lax.broadcasted_iota(jnp.int32, (CHUNK, CHUNK), 1)
    causal = s <= t
    wm = [jnp.where(causal, w[g], 0.0).astype(BF16) for g in range(SG_GROUPS)]
    return [jnp.concatenate([wm[2 * j], wm[2 * j + 1]], axis=0) for j in range(SG_GROUPS // 2)], causal


def _sg_mix(vl, pairs, bias):
    tr = vl.shape[0]
    low = lax.broadcasted_iota(jnp.int32, (CHUNK, LANES), 1) < SG_GROUP_DIM
    vb = vl.astype(BF16)
    rows = []
    for c in range(tr // CHUNK):
        slabs = []
        for j in range(SG_GROUPS // 2):
            slab = vb[c * CHUNK:(c + 1) * CHUNK, j * LANES:(j + 1) * LANES]
            m = _dot(pairs[j], slab)
            slabs.append(jnp.where(low, m[:CHUNK], m[CHUNK:]))
        rows.append(jnp.concatenate(slabs, axis=1) + bias)
    return jnp.concatenate(rows, axis=0)


def _sg_fwd(z, ln_g, ln_b, sg_w, bias_full):
    def fn(u_pre, v_pre, ln_g, ln_b, w, bias):
        u, _, _, vl = _sg_common(u_pre, v_pre, ln_g, ln_b)
        pairs, _ = _sg_masked_pairs(w)
        y = u * _sg_mix(vl, pairs, bias)
        return y, y

    return _rowwise(fn, "sg_fwd", 512, [(z, SG_WIDTH, Z_U // SG_WIDTH), (z, SG_WIDTH, Z_V // SG_WIDTH)],
                    [ln_g, ln_b, sg_w, bias_full], [(SG_WIDTH, BF16), (SG_WIDTH, BF16, "T")])


def _sg_bwd(z, dy, ln_g, ln_b, sg_w, bias_full, group_ind):
    def fn(u_pre, v_pre, dy, ln_g, ln_b, w, bias, ind):
        dy = dy.astype(F32)
        u, vhat, rstd, vl = _sg_common(u_pre, v_pre, ln_g, ln_b)
        pairs, causal = _sg_masked_pairs(w)
        mixed = _sg_mix(vl, pairs, bias)
        du_pre = dy * mixed * _gelu_grad(u_pre)
        dmix = dy * u
        tr = dy.shape[0]
        low = lax.broadcasted_iota(jnp.int32, (CHUNK, LANES), 1) < SG_GROUP_DIM
        vb = vl.astype(BF16)
        dw = [jnp.zeros((CHUNK, CHUNK), F32) for _ in range(SG_GROUPS)]
        dbias = jnp.zeros((CHUNK, SG_WIDTH), F32)
        dvl_rows = []
        for c in range(tr // CHUNK):
            dm_c = dmix[c * CHUNK:(c + 1) * CHUNK]
            dbias = dbias + dm_c
            slabs = []
            for j in range(SG_GROUPS // 2):
                slab = vb[c * CHUNK:(c + 1) * CHUNK, j * LANES:(j + 1) * LANES]
                dm = dm_c[:, j * LANES:(j + 1) * LANES]
                d0 = jnp.where(low, dm, 0.0).astype(BF16)
                d1 = jnp.where(low, 0.0, dm).astype(BF16)
                dw[2 * j] = dw[2 * j] + _dot(d0, slab, "nt")
                dw[2 * j + 1] = dw[2 * j + 1] + _dot(d1, slab, "nt")
                slabs.append(_dot(pairs[j], jnp.concatenate([d0, d1], axis=0), "tn"))
            dvl_rows.append(jnp.concatenate(slabs, axis=1))
        dvl = jnp.concatenate(dvl_rows, axis=0)
        dln_g = _rsum(dvl * vhat)
        dln_b = _rsum(dvl)
        dvh = dvl * ln_g
        dv = rstd * (dvh - jnp.mean(dvh, axis=-1, keepdims=True)
                     - vhat * jnp.mean(dvh * vhat, axis=-1, keepdims=True))
        dv_pre = dv * _gelu_grad(v_pre)
        dw = jnp.stack([jnp.where(causal, d, 0.0) for d in dw], axis=0)
        dbias_t = lax.dot_general(dbias, ind, (((1,), (0,)), ((), ())), precision=lax.Precision.HIGHEST,
                                  preferred_element_type=F32)
        return du_pre, dv_pre, dw, dbias_t, dln_g, dln_b

    return _rowwise(fn, "sg_bwd", 512,
                    [(z, SG_WIDTH, Z_U // SG_WIDTH), (z, SG_WIDTH, Z_V // SG_WIDTH), dy],
                    [ln_g, ln_b, sg_w, bias_full, group_ind],
                    [(SG_WIDTH, BF16), (SG_WIDTH, BF16)],
                    [((SG_GROUPS, CHUNK, CHUNK), F32), ((CHUNK, SG_GROUPS), F32), ((1, SG_WIDTH), F32), ((1, SG_WIDTH), F32)])


def _rope(x, c, s1, s2):
    return x * c + pltpu.roll(x, LANES - MLA_ROPE // 2, 1) * s1 + pltpu.roll(x, MLA_ROPE // 2, 1) * s2


def _rope_t(d, c, s1, s2):
    return d * c + pltpu.roll(d * s1, MLA_ROPE // 2, 1) + pltpu.roll(d * s2, LANES - MLA_ROPE // 2, 1)


def _mla_post(q_pre, kv_pre, z, tabs, gq, gk):
    scale = MLA_QK ** -0.5 * LOG2E

    def fn(q_pre, k_pre, v_pre, kr, c, s1, s2, gq, gk):
        qs, ks = [], []
        for h in range(MLA_HEADS):
            sl = slice(h * LANES, (h + 1) * LANES)
            qs.append(_rope(_rms(q_pre[:, sl], gq, MLA_QK), c, s1, s2) * scale)
            ks.append(_rope(_rms(k_pre[:, sl] + kr, gk, MLA_QK), c, s1, s2))
        lane = lax.broadcasted_iota(jnp.int32, v_pre.shape, 1) & (LANES - 1)
        return jnp.concatenate(qs, axis=1), jnp.concatenate(ks, axis=1), jnp.where(lane == ONES_LANE, 1.0, v_pre)

    return _rowwise(fn, "mla_post", 256,
                    [q_pre, (kv_pre, HP, 0), (kv_pre, HP, 1), (z, LANES, Z_KR // LANES), *tabs],
                    [gq, gk], [(HP, BF16)] * 3)


def _mla_post_bwd(q_pre, kv_pre, z, tabs, gq, gk, dq, dk, dv):
    scale = MLA_QK ** -0.5

    def fn(q_pre, k_pre, kr, c, s1, s2, dq, dk, dv, gq, gk):
        lane = lax.broadcasted_iota(jnp.int32, (1, LANES), 1)
        kr_mask = (lane >= KR_LANE) & (lane < KR_LANE + MLA_ROPE)
        dqs, dks = [], []
        dgq = jnp.zeros((1, LANES), F32)
        dgk = jnp.zeros((1, LANES), F32)
        dkr = jnp.zeros(kr.shape, F32)
        for h in range(MLA_HEADS):
            sl = slice(h * LANES, (h + 1) * LANES)
            dqn = _rope_t(dq[:, sl].astype(F32), c, s1, s2) * scale
            dx, dg = _rms_bwd(q_pre[:, sl], gq, dqn, MLA_QK)
            dqs.append(dx)
            dgq = dgq + dg
            dkn = _rope_t(dk[:, sl].astype(F32), c, s1, s2)
            dx, dg = _rms_bwd(k_pre[:, sl] + kr, gk, dkn, MLA_QK)
            dks.append(dx)
            dgk = dgk + dg
            dkr = dkr + dx
        dkr = jnp.where(kr_mask, dkr, 0.0)
        dkv = jnp.concatenate(dks + [dv.astype(F32)], axis=1)
        return jnp.concatenate(dqs, axis=1), dkv, dkr, dgq, dgk

    return _rowwise(fn, "mla_post_bwd", 256,
                    [q_pre, (kv_pre, HP, 0), (z, LANES, Z_KR // LANES), *tabs, dq, dk, dv],
                    [gq, gk], [(HP, BF16), (2 * HP, BF16), (LANES, BF16)],
                    [((1, LANES), F32), ((1, LANES), F32)])


def _pairs(n, lower):
    a, b = [], []
    for o in range(n):
        inner = range(o + 1) if lower else range(o, n)
        for t in inner:
            a.append(o)
            b.append(t)
    return jnp.asarray(np.array(a, np.int32)), jnp.asarray(np.array(b, np.int32))


FLASH_TILE, FLASH_SUB_ROWS = 2048, 512
LOG2E, LN2 = 1.4426950408889634, 0.6931471805599453
ONES_LANE = MLA_V


def _flash_tiles(T):
    tq = _tile(T, FLASH_TILE)
    return tq, _tile(tq, FLASH_SUB_ROWS)


def _col_span(t, sr, rb, diag, key_major):
    if not diag:
        return 0, t
    return (rb * sr, t) if key_major else (0, (rb + 1) * sr)


def _span_iota(sr, rb, c0, c1):
    r = lax.broadcasted_iota(jnp.int32, (sr, c1 - c0), 0) + rb * sr
    c = lax.broadcasted_iota(jnp.int32, (sr, c1 - c0), 1) + c0
    return r, c


def _lanes(x, width):
    return jnp.concatenate([x] * (width // LANES), axis=1)


def _flash_fwd(q, k, v):
    T = q.shape[0]
    tq, sr = _flash_tiles(T)
    n = T // tq
    ii, jj = _pairs(n, True)

    def body(ii_ref, jj_ref, q_ref, k_ref, v_ref, o_ref, ot_ref, lse_ref, lset_ref, m_sc, acc_sc):
        p_ = pl.program_id(1)
        i, j = ii_ref[p_], jj_ref[p_]

        @pl.when(j == 0)
        def _():
            m_sc[...] = jnp.full(m_sc.shape, NEG, F32)
            acc_sc[...] = jnp.zeros(acc_sc.shape, F32)

        def tile(diag):
            for rb in range(tq // sr):
                rows = slice(rb * sr, (rb + 1) * sr)
                c0, c1 = _col_span(tq, sr, rb, diag, False)
                s = _dot(q_ref[rows, :], k_ref[c0:c1, :], "nt")
                if diag:
                    r, c = _span_iota(sr, rb, c0, c1)
                    s = jnp.where(c <= r, s, NEG)
                m = m_sc[rows, :]
                m_new = jnp.maximum(m, jnp.max(s, axis=1, keepdims=True))
                p = jnp.exp2(s - _lanes(m_new, c1 - c0))
                acc_sc[rows, :] = jnp.exp2(m - m_new) * acc_sc[rows, :] + _dot(p, v_ref[c0:c1, :])
                m_sc[rows, :] = m_new

        @pl.when(j < i)
        def _():
            tile(False)

        @pl.when(j == i)
        def _():
            tile(True)
            acc = acc_sc[...]
            lane = lax.broadcasted_iota(jnp.int32, acc.shape, 1)
            l = jnp.sum(jnp.where(lane == ONES_LANE, acc, 0.0), axis=1, keepdims=True)
            o = jnp.where(lane < MLA_V, acc / l, 0.0)
            o_ref[...] = o.astype(o_ref.dtype)
            ot_ref[...] = o.T.astype(ot_ref.dtype)
            lse = m_sc[...] + jnp.log2(l)
            lse_ref[...] = lse
            lset_ref[...] = lse.T[:8]

    blk = lambda which: pl.BlockSpec((tq, LANES), which)
    qmap = lambda h, p, ii, jj: (ii[p], h)
    kmap = lambda h, p, ii, jj: (jj[p], h)
    tmap = lambda h, p, ii, jj: (h, ii[p])
    return pl.pallas_call(
        body, name="mla_flash_fwd",
        grid_spec=pltpu.PrefetchScalarGridSpec(
            num_scalar_prefetch=2, grid=(MLA_HEADS, int(ii.shape[0])),
            in_specs=[blk(qmap), blk(kmap), blk(kmap)],
            out_specs=[blk(qmap), pl.BlockSpec((LANES, tq), tmap), blk(qmap), pl.BlockSpec((8, tq), tmap)],
            scratch_shapes=[pltpu.VMEM((tq, LANES), F32)] * 2),
        out_shape=[jax.ShapeDtypeStruct((T, HP), BF16), jax.ShapeDtypeStruct((HP, T), BF16),
                   jax.ShapeDtypeStruct((T, HP), F32), jax.ShapeDtypeStruct((8 * MLA_HEADS, T), F32)],
        compiler_params=_cparams(("parallel", "arbitrary")),
    )(ii, jj, q, k, v)


def _flash_dq(q, k, v, do, lse, delta):
    T = q.shape[0]
    tq, sr = _flash_tiles(T)
    n = T // tq
    ii, jj = _pairs(n, True)

    def body(ii_ref, jj_ref, q_ref, k_ref, v_ref, do_ref, lse_ref, dl_ref, dq_ref, acc_sc):
        p_ = pl.program_id(1)
        i, j = ii_ref[p_], jj_ref[p_]

        @pl.when(j == 0)
        def _():
            acc_sc[...] = jnp.zeros(acc_sc.shape, F32)

        def tile(diag):
            for rb in range(tq // sr):
                rows = slice(rb * sr, (rb + 1) * sr)
                c0, c1 = _col_span(tq, sr, rb, diag, False)
                ks = k_ref[c0:c1, :]
                p = jnp.exp2(_dot(q_ref[rows, :], ks, "nt") - _lanes(lse_ref[rows, :], c1 - c0))
                if diag:
                    r, c = _span_iota(sr, rb, c0, c1)
                    p = jnp.where(c <= r, p, 0.0)
                dp = _dot(do_ref[rows, :], v_ref[c0:c1, :], "nt")
                acc_sc[rows, :] += _dot(p * (dp - _lanes(dl_ref[rows, :], c1 - c0)), ks)

        @pl.when(j < i)
        def _():
            tile(False)

        @pl.when(j == i)
        def _():
            tile(True)
            dq_ref[...] = acc_sc[...].astype(dq_ref.dtype)

    blk = lambda which: pl.BlockSpec((tq, LANES), which)
    qmap = lambda h, p, ii, jj: (ii[p], h)
    kmap = lambda h, p, ii, jj: (jj[p], h)
    return pl.pallas_call(
        body, name="mla_flash_dq",
        grid_spec=pltpu.PrefetchScalarGridSpec(
            num_scalar_prefetch=2, grid=(MLA_HEADS, int(ii.shape[0])),
            in_specs=[blk(qmap), blk(kmap), blk(kmap), blk(qmap), blk(qmap), blk(qmap)],
            out_specs=blk(qmap),
            scratch_shapes=[pltpu.VMEM((tq, LANES), F32)]),
        out_shape=jax.ShapeDtypeStruct((T, HP), BF16),
        compiler_params=_cparams(("parallel", "arbitrary")),
    )(ii, jj, q, k, v, do, lse, delta)


def _flash_dkv(q, k, v, do, lse_t, delta_t):
    T = q.shape[0]
    tq, sr = _flash_tiles(T)
    n = T // tq
    jj, ii = _pairs(n, False)

    def body(jj_ref, ii_ref, q_ref, k_ref, v_ref, do_ref, lse_ref, dl_ref, dk_ref, dv_ref, dk_sc, dv_sc):
        p_ = pl.program_id(1)
        j, i = jj_ref[p_], ii_ref[p_]

        @pl.when(i == j)
        def _():
            dk_sc[...] = jnp.zeros(dk_sc.shape, F32)
            dv_sc[...] = jnp.zeros(dv_sc.shape, F32)

        def tile(diag):
            for rb in range(tq // sr):
                rows = slice(rb * sr, (rb + 1) * sr)
                c0, c1 = _col_span(tq, sr, rb, diag, True)
                qs, dos = q_ref[c0:c1, :], do_ref[c0:c1, :]
                pt = jnp.exp2(_dot(k_ref[rows, :], qs, "nt") - lse_ref[:1, c0:c1])
                if diag:
                    r, c = _span_iota(sr, rb, c0, c1)
                    pt = jnp.where(r <= c, pt, 0.0)
                dpt = _dot(v_ref[rows, :], dos, "nt")
                dv_sc[rows, :] += _dot(pt, dos)
                dk_sc[rows, :] += _dot(pt * (dpt - dl_ref[:1, c0:c1]), qs)

        @pl.when(i == j)
        def _():
            tile(True)

        @pl.when(i > j)
        def _():
            tile(False)

        @pl.when(i == n - 1)
        def _():
            dk_ref[...] = (dk_sc[...] * LN2).astype(dk_ref.dtype)
            dv_ref[...] = dv_sc[...].astype(dv_ref.dtype)

    blk = lambda which: pl.BlockSpec((tq, LANES), which)
    qmap = lambda h, p, jj, ii: (ii[p], h)
    kmap = lambda h, p, jj, ii: (jj[p], h)
    lse_rows = pl.BlockSpec((8, tq), lambda h, p, jj, ii: (h, ii[p]))
    delta_rows = pl.BlockSpec((8, tq), lambda h, p, jj, ii: (h * (LANES // 8), ii[p]))
    return pl.pallas_call(
        body, name="mla_flash_dkv",
        grid_spec=pltpu.PrefetchScalarGridSpec(
            num_scalar_prefetch=2, grid=(MLA_HEADS, int(ii.shape[0])),
            in_specs=[blk(qmap), blk(kmap), blk(kmap), blk(qmap), lse_rows, delta_rows],
            out_specs=[blk(kmap), blk(kmap)],
            scratch_shapes=[pltpu.VMEM((tq, LANES), F32)] * 2),
        out_shape=[jax.ShapeDtypeStruct((T, HP), BF16)] * 2,
        compiler_params=_cparams(("parallel", "arbitrary")),
    )(jj, ii, q, k, v, do, lse_t, delta_t)


def _mem_fwd(z, km, vm, gq):
    scale = MEM_HEAD_DIM ** -0.5

    def fn(qm, km, vm, gq):
        ys = []
        for h in range(MEM_HEADS):
            sl = slice(h * LANES, (h + 1) * LANES)
            q = _rms(qm[:, sl], gq) * scale
            s = _dot(q, km[:, sl], "nt")
            p = jnp.exp(s - jnp.max(s, axis=1, keepdims=True))
            p = p / jnp.sum(p, axis=1, keepdims=True)
            ys.append(_dot(p, vm[:, sl]))
        y = jnp.concatenate(ys, axis=1)
        return y, y

    return _rowwise(fn, "mem_fwd", 512, [(z, MEM_WIDTH, Z_QM // MEM_WIDTH)], [km, vm, gq],
                    [(MEM_WIDTH, BF16), (MEM_WIDTH, BF16, "T")])


def _mem_bwd(z, dy, km, vm, gq):
    scale = MEM_HEAD_DIM ** -0.5

    def fn(qm, dy, km, vm, gq):
        dqs, dks, dvs = [], [], []
        dgq = jnp.zeros((1, LANES), F32)
        for h in range(MEM_HEADS):
            sl = slice(h * LANES, (h + 1) * LANES)
            q = (_rms(qm[:, sl], gq) * scale).astype(BF16)
            dyh = dy[:, sl]
            kh, vh = km[:, sl], vm[:, sl]
            s = _dot(q, kh, "nt")
            p = jnp.exp(s - jnp.max(s, axis=1, keepdims=True))
            p = p / jnp.sum(p, axis=1, keepdims=True)
            dp = _dot(dyh, vh, "nt")
            ds = p * (dp - jnp.sum(p * dp, axis=1, keepdims=True))
            dq = _dot(ds, kh) * scale
            dx, dg = _rms_bwd(qm[:, sl], gq, dq)
            dqs.append(dx)
            dgq = dgq + dg
            st = _dot(kh, q, "nt")
            pt = jnp.exp(st - jnp.max(st, axis=0, keepdims=True))
            pt = pt / jnp.sum(pt, axis=0, keepdims=True)
            dpt = _dot(vh, dyh, "nt")
            dst = pt * (dpt - jnp.sum(pt * dpt, axis=0, keepdims=True))
            dvs.append(_dot(pt, dyh))
            dks.append(_dot(dst, q))
        return jnp.concatenate(dqs, axis=1), jnp.concatenate(dks, axis=1), jnp.concatenate(dvs, axis=1), dgq

    m = km.shape[0]
    return _rowwise(fn, "mem_bwd", 512, [(z, MEM_WIDTH, Z_QM // MEM_WIDTH), dy], [km, vm, gq],
                    [(MEM_WIDTH, BF16)], [((m, MEM_WIDTH), F32), ((m, MEM_WIDTH), F32), ((1, LANES), F32)])


GROUPS = {"ffn1": ["ffn1_w_gu"], "ffn1_down": ["ffn1_w_down"],
          "mix": ["w_in", "mla_w_uq", "mla_w_ukv", "mem_w_kv", "w_branch_a", "w_branch_b", "w_branch_c", "w_out"],
          "ffn2": ["ffn2_w_gu", "ffn2_w_down"]}
GRAD_GROUPS = {"ffn2": GROUPS["ffn2"], "mix": GROUPS["mix"], "ffn1_down": ["ffn1_w_down"], "ffn1_gu": ["ffn1_w_gu"]}


def _local_step(x, mem, positions, loss_target, P, weights, grads_out):
    T = x.shape[0]
    G = {}
    W = dict(weights("ffn1", None))

    half = MLA_ROPE // 2
    inv = ROPE_BASE ** (-jnp.arange(half, dtype=F32) / half)
    ang = positions.astype(F32)[:, None] * inv
    cos, sin = jnp.cos(ang), jnp.sin(ang)
    one, zero = jnp.ones((T, MLA_NOPE), F32), jnp.zeros((T, half), F32)
    pad = LANES - MLA_QK
    tabs = (jnp.concatenate([one, cos, cos, jnp.ones((T, pad), F32)], axis=1),
            jnp.concatenate([jnp.zeros((T, MLA_NOPE), F32), -sin, zero, jnp.zeros((T, pad), F32)], axis=1),
            jnp.concatenate([jnp.zeros((T, MLA_NOPE), F32), zero, sin, jnp.zeros((T, pad), F32)], axis=1))
    gq_p = jnp.pad(P["mla_q_norm"], ((0, 0), (0, pad)))
    gk_p = jnp.pad(P["mla_k_norm"], ((0, 0), (0, pad)))
    bias_full = jnp.repeat(P["sg_b"].T, SG_GROUP_DIM, axis=1)
    group_ind = jnp.repeat(jnp.eye(SG_GROUPS, dtype=F32), SG_GROUP_DIM, axis=0)

    HT = (D_MODEL, BF16, "T")

    def norm2(x, g):
        h = _rms(x, g)
        return h, h

    h1, h1t = _rowwise(norm2, "ffn1_norm", 512, [x], [P["ffn1_norm"]], [(D_MODEL, BF16), HT])
    def ffn1_w_down(after):
        W.update(weights("ffn1_down", after))
        return W["ffn1_w_down"]

    gu1, a1t, o1 = _ffn_fwd(h1, W["ffn1_w_gu"], ffn1_w_down, "ffn1")

    def resid_norm(x, o, g):
        xn = x + 0.5 * o
        h = _rms(xn, g)
        return xn, h, h

    x1, hm, hmt = _rowwise(resid_norm, "mix_norm", 512, [x, o1], [P["mix_norm"]],
                           [(D_MODEL, F32), (D_MODEL, BF16), HT])
    W.update(weights("mix", hm))
    z = _mm(hm, W["w_in"], "nn", BF16, "w_in", tm=1024, tn=1792)

    y_a, y_at = _sg_fwd(z, P["sg_ln_g"], P["sg_ln_b"], P["sg_w"], bias_full)

    def c_norm(cq, ckv, gq, gkv):
        a, b = _rms(cq, gq), _rms(ckv, gkv)
        return a, b, a, b

    cqn, ckvn, cqnt, ckvnt = _rowwise(
        c_norm, "mla_cnorm", 512, [(z, MLA_Q_RANK, Z_CQ // MLA_Q_RANK), (z, MLA_KV_RANK, Z_CKV // MLA_KV_RANK)],
        [P["mla_cq_norm"], P["mla_ckv_norm"]],
        [(MLA_Q_RANK, BF16), (MLA_KV_RANK, BF16), (MLA_Q_RANK, BF16, "T"), (MLA_KV_RANK, BF16, "T")])
    q_pre = _mm(cqn, W["mla_w_uq"], "nn", BF16, "mla_uq", tm=1024, tn=1024)
    kv_pre = _mm(ckvn, W["mla_w_ukv"], "nn", BF16, "mla_ukv", tm=1024, tn=1024)
    q, k, v = _mla_post(q_pre, kv_pre, z, tabs, gq_p, gk_p)
    y_b, y_bt, lse, lse_t = _flash_fwd(q, k, v)

    memn, = _rowwise(lambda m, g: _rms(m, g), "mem_norm", 256, [mem], [P["mem_norm"]], [(D_MODEL, BF16)])
    kvm = _mm(memn, W["mem_w_kv"], "nn", F32, "mem_kv")

    def mem_k(kvm, gk):
        ks = [_rms(kvm[:, h * LANES:(h + 1) * LANES], gk) for h in range(MEM_HEADS)]
        return jnp.concatenate(ks, axis=1), kvm[:, MEM_WIDTH:]

    km, vm = _rowwise(mem_k, "mem_knorm", 256, [kvm], [P["mem_k_norm"]], [(MEM_WIDTH, BF16), (MEM_WIDTH, BF16)])
    y_c, y_ct = _mem_fwd(z, km, vm, P["mem_q_norm"])

    pa = _mm(y_a, W["w_branch_a"], "nn", BF16, "branch_a", tm=1024, tn=1024)
    pb = _mm(y_b, W["w_branch_b"], "nn", BF16, "branch_b", tm=1024, tn=1024)
    pc = _mm(y_c, W["w_branch_c"], "nn", BF16, "branch_c", tm=1024, tn=1024)

    def merge(zg, pa, pb, pc, b):
        g = _sigmoid(zg + b)
        m = g[:, :D_MODEL] * pa + g[:, D_MODEL:2 * D_MODEL] * pb + g[:, 2 * D_MODEL:] * pc
        return m, m

    merged, mergedt = _rowwise(merge, "merge", 256, [(z, 3 * D_MODEL, 0), pa, pb, pc], [P["b_gate"]],
                               [(D_MODEL, BF16), HT])
    om = _mm(merged, W["w_out"], "nn", BF16, "w_out", tm=1024, tn=1024)

    def resid_norm1(x, o, g):
        xn = x + o
        h = _rms(xn, g)
        return xn, h, h

    x2, h2, h2t = _rowwise(resid_norm1, "ffn2_norm", 512, [x1, om], [P["ffn2_norm"]],
                           [(D_MODEL, F32), (D_MODEL, BF16), HT])
    W.update(weights("ffn2", h2))
    gu2, a2t, o2 = _ffn_fwd(h2, W["ffn2_w_gu"], W["ffn2_w_down"], "ffn2")

    def loss_fn(x2, o2, t):
        e = x2 + 0.5 * o2 - t
        return e * (1.0 / D_MODEL), (e * (0.5 / D_MODEL)).astype(BF16), _rsum(e * e) * (0.5 / D_MODEL)

    dx3, do2, loss_part = _rowwise(loss_fn, "loss", 512, [x2, o2, loss_target], [],
                                   [(D_MODEL, F32), (D_MODEL, BF16)], [((1, D_MODEL), F32)])

    dh2, G["ffn2_w_gu"], G["ffn2_w_down"] = _ffn_bwd(do2, h2t, gu2, a2t, W["ffn2_w_gu"], W["ffn2_w_down"], "ffn2")
    tie = grads_out("ffn2", G)

    def norm_bwd(x, dh, dxo, g, *_):
        dx, dg = _rms_bwd(x, g, dh)
        dx = dx + dxo
        return dx, dx, dg

    dx2, dx2b, G["ffn2_norm"] = _rowwise(norm_bwd, "ffn2_norm_bwd", 512, [x2, dh2, dx3],
                                         [P["ffn2_norm"]] + ([] if tie is None else [tie]),
                                         [(D_MODEL, F32), (D_MODEL, BF16)], [((1, D_MODEL), F32)])

    G["w_out"] = _mm_t(mergedt, dx2b, "w_out_dw", tm=1024, tn=1024)
    dmerged = _mm(dx2b, W["w_out"], "nt", BF16, "w_out_dx", tm=1024, tn=1024)

    def merge_bwd(zg, pa, pb, pc, dm, b):
        g = _sigmoid(zg + b)
        ps = jnp.concatenate([pa, pb, pc], axis=1)
        dm3 = jnp.concatenate([dm, dm, dm], axis=1)
        dzg = dm3 * ps * g * (1.0 - g)
        dp = dm3 * g
        return dzg, dp[:, :D_MODEL], dp[:, D_MODEL:2 * D_MODEL], dp[:, 2 * D_MODEL:], _rsum(dzg)

    dzg, dpa, dpb, dpc, G["b_gate"] = _rowwise(
        merge_bwd, "merge_bwd", 256, [(z, 3 * D_MODEL, 0), pa, pb, pc, dmerged], [P["b_gate"]],
        [(3 * D_MODEL, BF16), (D_MODEL, BF16), (D_MODEL, BF16), (D_MODEL, BF16)], [((1, 3 * D_MODEL), F32)])

    G["w_branch_a"] = _mm_t(y_at, dpa, "branch_a_dw", tm=512, tn=1024)
    G["w_branch_b"] = _mm_t(y_bt, dpb, "branch_b_dw", tm=1024, tn=1024)
    G["w_branch_c"] = _mm_t(y_ct, dpc, "branch_c_dw", tm=512, tn=1024)
    dy_a = _mm(dpa, W["w_branch_a"], "nt", BF16, "branch_a_dx", tm=1024, tn=512)
    dy_b = _mm(dpb, W["w_branch_b"], "nt", BF16, "branch_b_dx", tm=1024, tn=1024)
    dy_c = _mm(dpc, W["w_branch_c"], "nt", BF16, "branch_c_dx", tm=1024, tn=512)

    du_pre, dv_pre, G["sg_w"], dbias_t, G["sg_ln_g"], G["sg_ln_b"] = _sg_bwd(
        z, dy_a, P["sg_ln_g"], P["sg_ln_b"], P["sg_w"], bias_full, group_ind)
    G["sg_b"] = dbias_t.T

    dqm, dkm, dvm, G["mem_q_norm"] = _mem_bwd(z, dy_c, km, vm, P["mem_q_norm"])

    def mem_k_bwd(kvm, dkm, dvm, gk):
        dks = []
        dg = jnp.zeros((1, LANES), F32)
        for h in range(MEM_HEADS):
            sl = slice(h * LANES, (h + 1) * LANES)
            dx, d = _rms_bwd(kvm[:, sl], gk, dkm[:, sl])
            dks.append(dx)
            dg = dg + d
        return jnp.concatenate(dks + [dvm], axis=1), dg

    dkvm, G["mem_k_norm"] = _rowwise(mem_k_bwd, "mem_knorm_bwd", 256, [kvm, dkm, dvm], [P["mem_k_norm"]],
                                     [(2 * MEM_WIDTH, BF16)], [((1, LANES), F32)])
    G["mem_w_kv"] = _mm(memn, dkvm, "tn", BF16, "mem_kv_dw")
    dmemn = _mm(dkvm, W["mem_w_kv"], "nt", F32, "mem_kv_dx")
    _, G["mem_norm"] = _rowwise(lambda m, d, g: _rms_bwd(m, g, d), "mem_norm_bwd", 256, [mem, dmemn],
                                [P["mem_norm"]], [(D_MODEL, BF16)], [((1, D_MODEL), F32)])

    def delta_fn(o, do):
        od = o.astype(F32) * do.astype(F32)
        ds = [jnp.broadcast_to(jnp.sum(od[:, h * LANES:(h + 1) * LANES], axis=1, keepdims=True), (od.shape[0], LANES))
              for h in range(MLA_HEADS)]
        d = jnp.concatenate(ds, axis=1)
        return d, d

    delta, delta_t = _rowwise(delta_fn, "mla_delta", 512, [y_b, dy_b], [], [(HP, F32), (HP, F32, "T")])
    dq = _flash_dq(q, k, v, dy_b, lse, delta)
    dk, dv = _flash_dkv(q, k, v, dy_b, lse_t, delta_t)
    dq_pre, dkv_pre, dkr, dgq, dgk = _mla_post_bwd(q_pre, kv_pre, z, tabs, gq_p, gk_p, dq, dk, dv)
    G["mla_q_norm"], G["mla_k_norm"] = dgq[:, :MLA_QK], dgk[:, :MLA_QK]
    G["mla_w_uq"] = _mm_t(cqnt, dq_pre, "mla_uq_dw", tm=384, tn=1024)
    G["mla_w_ukv"] = _mm_t(ckvnt, dkv_pre, "mla_ukv_dw", tm=256, tn=2048)
    dcqn = _mm(dq_pre, W["mla_w_uq"], "nt", BF16, "mla_uq_dx", tm=1024)
    dckvn = _mm(dkv_pre, W["mla_w_ukv"], "nt", BF16, "mla_ukv_dx", tm=1024)

    def c_norm_bwd(cq, ckv, dcqn, dckvn, gq, gkv):
        dcq, dgq = _rms_bwd(cq, gq, dcqn)
        dckv, dgkv = _rms_bwd(ckv, gkv, dckvn)
        return dcq, dckv, dgq, dgkv

    dcq, dckv, G["mla_cq_norm"], G["mla_ckv_norm"] = _rowwise(
        c_norm_bwd, "mla_cnorm_bwd", 512,
        [(z, MLA_Q_RANK, Z_CQ // MLA_Q_RANK), (z, MLA_KV_RANK, Z_CKV // MLA_KV_RANK), dcqn, dckvn],
        [P["mla_cq_norm"], P["mla_ckv_norm"]], [(MLA_Q_RANK, BF16), (MLA_KV_RANK, BF16)],
        [((1, MLA_Q_RANK), F32), ((1, MLA_KV_RANK), F32)])

    dz = jnp.concatenate([dzg, du_pre, dv_pre, dqm, dckv, dkr, dcq], axis=1)
    G["w_in"] = _mm_t(hmt, dz, "w_in_dw", tm=1024, tn=1792, tk=2048)
    dhm = _mm(dz, W["w_in"], "nt", BF16, "w_in_dx", tm=1024, tn=1024, tk=2688)

    def norm_bwd_half(x, dh, dxo, g):
        dx, dg = _rms_bwd(x, g, dh)
        dx = dx + dxo
        return dx, (0.5 * dx), dg

    dx1, do1, G["mix_norm"] = _rowwise(norm_bwd_half, "mix_norm_bwd", 512, [x1, dhm, dx2], [P["mix_norm"]],
                                       [(D_MODEL, F32), (D_MODEL, BF16)], [((1, D_MODEL), F32)])
    tie = grads_out("mix", G)

    def ffn1_dw(which, dw):
        G["ffn1_w_" + which] = dw
        return grads_out("ffn1_" + which, G)

    dh1, _, _ = _ffn_bwd(do1, h1t, gu1, a1t, W["ffn1_w_gu"], W["ffn1_w_down"], "ffn1", tie, ffn1_dw)

    def norm_bwd_last(x, dh, dxo, g):
        dx, dg = _rms_bwd(x, g, dh)
        return dx + dxo, dg

    grad_x, G["ffn1_norm"] = _rowwise(norm_bwd_last, "ffn1_norm_bwd", 512, [x, dh1, dx1], [P["ffn1_norm"]],
                                      [(D_MODEL, F32)], [((1, D_MODEL), F32)])
    return loss_part, grad_x, G


SHARDED = ["ffn1_w_gu", "ffn1_w_down", "w_in", "mla_w_uq", "mla_w_ukv", "mem_w_kv",
           "w_branch_a", "w_branch_b", "w_branch_c", "w_out", "ffn2_w_gu", "ffn2_w_down"]
ROW_SHARDED = {"ffn1_w_down", "mem_w_kv", "w_out", "ffn2_w_down"}
SMALL = ["ffn1_norm", "mix_norm", "b_gate", "sg_ln_g", "sg_ln_b", "sg_w", "sg_b", "mla_cq_norm",
         "mla_ckv_norm", "mla_q_norm", "mla_k_norm", "mem_norm", "mem_q_norm", "mem_k_norm", "ffn2_norm"]
ORDER = ["ffn1_norm", "ffn1_w_gu", "ffn1_w_down", "mix_norm", "w_in", "b_gate", "sg_ln_g", "sg_ln_b", "sg_w",
         "sg_b", "mla_cq_norm", "mla_w_uq", "mla_ckv_norm", "mla_w_ukv", "mla_q_norm", "mla_k_norm", "mem_norm",
         "mem_w_kv", "mem_q_norm", "mem_k_norm", "w_branch_a", "w_branch_b", "w_branch_c", "w_out", "ffn2_norm",
         "ffn2_w_gu", "ffn2_w_down"]

_IN_U, _IN_V, _IN_CQ, _IN_CKV, _IN_KR, _IN_QM, _IN_G = 0, 512, 1024, 1408, 1664, 1696, 2208
IN_COLS = 5280


def _full_from_slabs(name, slabs):
    n, r, c = slabs.shape
    if name in ROW_SHARDED:
        return slabs.reshape(n * r, c)
    return slabs.transpose(1, 0, 2).reshape(r, n * c)


def _slabs_from_full(name, full):
    if name in ROW_SHARDED:
        return full.reshape(N_DEV, full.shape[0] // N_DEV, full.shape[1])
    r, c = full.shape
    return full.reshape(r, N_DEV, c // N_DEV).transpose(1, 0, 2)


def _compute_layout(full):
    W = dict(full)
    if "w_in" not in full:
        return W
    w = full["w_in"]
    kr = jnp.pad(w[:, _IN_KR:_IN_QM], ((0, 0), (KR_LANE, LANES - KR_LANE - MLA_ROPE)))
    W["w_in"] = jnp.concatenate([w[:, _IN_G:], w[:, _IN_U:_IN_CQ], w[:, _IN_QM:_IN_G], w[:, _IN_CKV:_IN_KR], kr,
                                 w[:, _IN_CQ:_IN_CKV]], axis=1)
    uq = full["mla_w_uq"].reshape(MLA_Q_RANK, MLA_HEADS, MLA_QK)
    W["mla_w_uq"] = jnp.pad(uq, ((0, 0), (0, 0), (0, LANES - MLA_QK))).reshape(MLA_Q_RANK, HP)
    ukv = full["mla_w_ukv"].reshape(MLA_KV_RANK, MLA_HEADS, MLA_NOPE + MLA_V)
    padh = lambda a: jnp.pad(a, ((0, 0), (0, 0), (0, LANES - a.shape[2]))).reshape(MLA_KV_RANK, HP)
    W["mla_w_ukv"] = jnp.concatenate([padh(ukv[:, :, :MLA_NOPE]), padh(ukv[:, :, MLA_NOPE:])], axis=1)
    wb = full["w_branch_b"].reshape(MLA_HEADS, MLA_V, D_MODEL)
    W["w_branch_b"] = jnp.pad(wb, ((0, 0), (0, LANES - MLA_V), (0, 0))).reshape(HP, D_MODEL)
    return W


def _reference_layout(G):
    out = dict(G)
    if "w_in" not in G:
        return out
    g = G["w_in"]
    out["w_in"] = jnp.concatenate([
        g[:, Z_U:Z_QM], g[:, Z_CQ:Z_COLS], g[:, Z_CKV:Z_KR], g[:, Z_KR + KR_LANE:Z_KR + KR_LANE + MLA_ROPE],
        g[:, Z_QM:Z_CKV], g[:, Z_G:Z_U]], axis=1)
    out["mla_w_uq"] = G["mla_w_uq"].reshape(MLA_Q_RANK, MLA_HEADS, LANES)[:, :, :MLA_QK].reshape(MLA_Q_RANK, -1)
    gk = G["mla_w_ukv"][:, :HP].reshape(MLA_KV_RANK, MLA_HEADS, LANES)[:, :, :MLA_NOPE]
    gv = G["mla_w_ukv"][:, HP:].reshape(MLA_KV_RANK, MLA_HEADS, LANES)[:, :, :MLA_V]
    out["mla_w_ukv"] = jnp.concatenate([gk, gv], axis=2).reshape(MLA_KV_RANK, -1)
    out["w_branch_b"] = G["w_branch_b"].reshape(MLA_HEADS, LANES, D_MODEL)[:, :MLA_V].reshape(-1, D_MODEL)
    return out


def _pack(parts):
    flat = []
    for a in parts:
        a = a.reshape(-1)
        flat.append(jnp.pad(a, (0, (-a.shape[0]) % LANES)))
    return jnp.concatenate(flat).reshape(-1, LANES)


def _unpack(packed, shapes):
    flat = packed.reshape(-1)
    out, off = [], 0
    for shp in shapes:
        n = int(np.prod(shp))
        out.append(flat[off:off + n].reshape(shp))
        off += n + (-n) % LANES
    return out


MESH = pl.DeviceIdType.MESH
HBM = pl.BlockSpec(memory_space=pltpu.HBM)


def _all_gather(shards):
    n = len(shards)

    def body(*refs):
        x_refs, out_refs, token_ref = refs[:n], refs[n:2 * n], refs[2 * n]
        send_sems, recv_sems, local_sems = refs[2 * n + 1:]
        x, y, c = lax.axis_index("x"), lax.axis_index("y"), lax.axis_index("c")
        me, sibling = (x, y, c), (x, y, 1 - c)
        chips = [(1 - x, y), (x, 1 - y), (1 - x, 1 - y)]
        token_ref[...] = jnp.zeros_like(token_ref)

        def slot(a, px, py, pc):
            return out_refs[a].at[4 * px + 2 * py + pc]

        def copy(a, k, block, to, src=None):
            return pltpu.make_async_remote_copy(
                src_ref=slot(a, *block) if src is None else src, dst_ref=slot(a, *block),
                send_sem=send_sems.at[7 * a + k], recv_sem=recv_sems.at[7 * a + k], device_id=to, device_id_type=MESH)

        arrays = range(n)
        mine = [pltpu.make_async_copy(x_refs[a], slot(a, *me), local_sems.at[a]) for a in arrays]
        for cp in mine:
            cp.start()
        first = [copy(a, 0, me, sibling, src=x_refs[a]) for a in arrays]
        first += [copy(a, 1 + j, me, (*chip, c), src=x_refs[a]) for j, chip in enumerate(chips) for a in arrays]
        for cp in first:
            cp.start()
        passed = []
        for j, chip in enumerate(chips):
            for a in arrays:
                copy(a, 1 + j, (*chip, c), me).wait_recv()
                passed.append(copy(a, 4 + j, (*chip, c), sibling))
                passed[-1].start()
        for a in arrays:
            copy(a, 0, sibling, me).wait_recv()
        for j, chip in enumerate(chips):
            for a in arrays:
                copy(a, 4 + j, (*chip, 1 - c), me).wait_recv()
        for cp in first + passed:
            cp.wait_send()
        for cp in mine:
            cp.wait()

    res = pl.pallas_call(
        body, name="all_gather_weights",
        out_shape=[jax.ShapeDtypeStruct((N_DEV,) + s.shape, s.dtype) for s in shards]
        + [jax.ShapeDtypeStruct((8, LANES), F32)],
        in_specs=[HBM] * n, out_specs=[HBM] * n + [pl.BlockSpec(memory_space=pltpu.VMEM)],
        scratch_shapes=[pltpu.SemaphoreType.DMA((7 * n,)), pltpu.SemaphoreType.DMA((7 * n,)),
                        pltpu.SemaphoreType.DMA((n,))],
    )(*shards)
    return res[:n], res[n]


SEM = pl.BlockSpec(memory_space=pltpu.SEMAPHORE)
DATAFLOW = pltpu.SideEffectType.DATAFLOW_SIDE_EFFECTING


def _peers():
    x, y, c = lax.axis_index("x"), lax.axis_index("y"), lax.axis_index("c")
    out = []
    for k in range(1, N_DEV):
        px = 1 - x if k & 4 else x
        py = 1 - y if k & 2 else y
        pc = 1 - c if k & 1 else c
        out.append((k, (px, py, pc), 4 * px + 2 * py + pc))
    return 4 * x + 2 * y + c, out


def _send_start(srcs, per_peer, name):
    n = len(srcs)
    lands = [lax.empty((N_DEV,) + (s.shape[1:] if per_peer else s.shape), s.dtype) for s in srcs]

    def body(*refs):
        src_refs, land_refs, send_sems, recv_sems, token = refs[:n], refs[n:2 * n], refs[2 * n], refs[2 * n + 1], refs[-1]
        me, peers = _peers()
        for a in range(n):
            for k, pid, pflat in peers:
                pltpu.make_async_remote_copy(
                    src_ref=src_refs[a].at[pflat] if per_peer else src_refs[a], dst_ref=land_refs[a].at[me],
                    send_sem=send_sems.at[7 * a + k - 1], recv_sem=recv_sems.at[7 * a + k - 1],
                    device_id=pid, device_id_type=MESH).start()
        token[...] = jnp.zeros_like(token)

    hbm = lambda a: pltpu.with_memory_space_constraint(a, pltpu.HBM)
    res = pl.pallas_call(
        body, name=name,
        out_shape=(pltpu.SemaphoreType.DMA((7 * n,)), pltpu.SemaphoreType.DMA((7 * n,)),
                   *[pltpu.HBM(a.shape, a.dtype) for a in srcs + lands], jax.ShapeDtypeStruct((8, LANES), F32)),
        in_specs=(HBM,) * (2 * n), out_specs=(SEM, SEM) + (HBM,) * (2 * n) + (pl.BlockSpec(memory_space=pltpu.VMEM),),
        input_output_aliases={i: 2 + i for i in range(2 * n)},
        compiler_params=pltpu.CompilerParams(has_side_effects=DATAFLOW),
    )(*[hbm(a) for a in srcs + lands])
    return (res[0], res[1], list(res[2:2 + n]), list(res[2 + n:2 + 2 * n])), res[-1]


def _send_wait(started, after, per_peer, name):
    send_sems, recv_sems, srcs_thru, lands_thru = started
    n = len(srcs_thru)

    def body(*refs):
        src_refs, land_refs, send_sems, recv_sems = refs[:n], refs[n:2 * n], refs[2 * n], refs[2 * n + 1]
        me, peers = _peers()
        for a in range(n):
            for k, pid, pflat in peers:
                copy = pltpu.make_async_remote_copy(
                    src_ref=src_refs[a].at[pflat] if per_peer else src_refs[a], dst_ref=land_refs[a].at[pflat],
                    send_sem=send_sems.at[7 * a + k - 1], recv_sem=recv_sems.at[7 * a + k - 1],
                    device_id=pid, device_id_type=MESH)
                copy.wait_send()
                copy.wait_recv()

    outs = pl.pallas_call(
        body, name=name,
        out_shape=tuple(pltpu.HBM(a.shape, a.dtype) for a in srcs_thru + lands_thru),
        in_specs=(HBM,) * (2 * n) + (SEM, SEM, pl.BlockSpec(memory_space=pl.ANY)), out_specs=(HBM,) * (2 * n),
        input_output_aliases={i: i for i in range(2 * n)},
        compiler_params=pltpu.CompilerParams(has_side_effects=DATAFLOW),
    )(*srcs_thru, *lands_thru, send_sems, recv_sems, after)
    me = 4 * lax.axis_index("x") + 2 * lax.axis_index("y") + lax.axis_index("c")
    landed = []
    for src_out, land in zip(outs[:n], outs[n:]):
        own = lax.dynamic_index_in_dim(src_out, me, 0, keepdims=True) if per_peer else src_out[None]
        landed.append(lax.dynamic_update_slice(land, own, (me,) + (0,) * (land.ndim - 1)))
    return landed


def _share_rows(block, name):
    def body(src_ref, out_ref, send_sems, recv_sems, local_sem):
        me, peers = _peers()
        own = pltpu.make_async_copy(src_ref, out_ref.at[me], local_sem)
        own.start()
        copies = [pltpu.make_async_remote_copy(
            src_ref=src_ref, dst_ref=out_ref.at[me], send_sem=send_sems.at[k - 1], recv_sem=recv_sems.at[k - 1],
            device_id=pid, device_id_type=MESH) for k, pid, _ in peers]
        for cp in copies:
            cp.start()
        for cp in copies:
            cp.wait()
        own.wait()

    return pl.pallas_call(
        body, name=name, out_shape=jax.ShapeDtypeStruct((N_DEV,) + block.shape, block.dtype),
        in_specs=[HBM], out_specs=HBM,
        scratch_shapes=[pltpu.SemaphoreType.DMA((N_DEV - 1,)), pltpu.SemaphoreType.DMA((N_DEV - 1,)),
                        pltpu.SemaphoreType.DMA],
    )(block)


def _sum_slots(recv, name, tr):
    n, rows, lanes = recv.shape
    tr = _tile(rows, tr)

    def body(r_ref, o_ref):
        acc = r_ref[0].astype(F32)
        for i in range(1, n):
            acc = acc + r_ref[i].astype(F32)
        o_ref[...] = acc

    return pl.pallas_call(
        body, name=name, grid=(rows // tr,),
        in_specs=[pl.BlockSpec((n, tr, lanes), lambda i: (0, i, 0))],
        out_specs=pl.BlockSpec((tr, lanes), lambda i: (i, 0)),
        out_shape=jax.ShapeDtypeStruct((rows, lanes), F32),
        compiler_params=_cparams(("parallel",)),
    )(recv)


def _adamw_math(w, g, m, v):
    m = ADAM_B1 * m + (1.0 - ADAM_B1) * g
    v = ADAM_B2 * v + (1.0 - ADAM_B2) * (g * g)
    m_hat = m / (1.0 - ADAM_B1 ** ADAM_STEP)
    v_hat = v / (1.0 - ADAM_B2 ** ADAM_STEP)
    return -ADAM_LR * (m_hat / (jnp.sqrt(v_hat) + ADAM_EPS) + ADAM_WD * w), m, v


def _adamw(w, g, m, v, name, tr=256):
    return _rowwise(_adamw_math, name, tr, [w, g, m, v], [], [(w.shape[1], F32)] * 3)


def _adamw_small(ws, gs, ms, vs):
    n = len(ws)

    def body(*refs):
        ins, outs = refs[:4 * n], refs[4 * n:]
        for i in range(n):
            d, m, v = _adamw_math(ins[i][...], ins[n + i][...], ins[2 * n + i][...], ins[3 * n + i][...])
            outs[i][...], outs[n + i][...], outs[2 * n + i][...] = d, m, v

    vmem = pl.BlockSpec(memory_space=pltpu.VMEM)
    res = pl.pallas_call(
        body, name="adamw_small", in_specs=[vmem] * (4 * n), out_specs=[vmem] * (3 * n),
        out_shape=[jax.ShapeDtypeStruct(w.shape, F32) for w in ws] * 3,
    )(*ws, *gs, *ms, *vs)
    return res[:n], res[n:2 * n], res[2 * n:]


def _sum_adamw(recv, w, m, v, name):
    n, r, c = recv.shape
    tr = _tile(r, 256)

    def body(r_ref, w_ref, m_ref, v_ref, g_ref, d_ref, nm_ref, nv_ref):
        g = r_ref[0].astype(F32)
        for i in range(1, n):
            g = g + r_ref[i].astype(F32)
        g_ref[...] = g
        d_ref[...], nm_ref[...], nv_ref[...] = _adamw_math(w_ref[...], g, m_ref[...], v_ref[...])

    row = pl.BlockSpec((None, tr, c), lambda i: (0, i, 0))
    return pl.pallas_call(
        body, name=name, grid=(r // tr,),
        in_specs=[pl.BlockSpec((n, tr, c), lambda i: (0, i, 0)), row, row, row], out_specs=[row] * 4,
        out_shape=[jax.ShapeDtypeStruct((1, r, c), F32)] * 4, compiler_params=_cparams(("parallel",)),
    )(recv, w, m, v)


def kernel(x, mem, positions, ffn1_norm, ffn1_w_gu, ffn1_w_down, mix_norm, w_in, b_gate, sg_ln_g, sg_ln_b, sg_w, sg_b, mla_cq_norm, mla_w_uq, mla_ckv_norm, mla_w_ukv, mla_q_norm, mla_k_norm, mem_norm, mem_w_kv, mem_q_norm, mem_k_norm, w_branch_a, w_branch_b, w_branch_c, w_out, ffn2_norm, ffn2_w_gu, ffn2_w_down, loss_target, m_ffn1_norm, m_ffn1_w_gu, m_ffn1_w_down, m_mix_norm, m_w_in, m_b_gate, m_sg_ln_g, m_sg_ln_b, m_sg_w, m_sg_b, m_mla_cq_norm, m_mla_w_uq, m_mla_ckv_norm, m_mla_w_ukv, m_mla_q_norm, m_mla_k_norm, m_mem_norm, m_mem_w_kv, m_mem_q_norm, m_mem_k_norm, m_w_branch_a, m_w_branch_b, m_w_branch_c, m_w_out, m_ffn2_norm, m_ffn2_w_gu, m_ffn2_w_down, v_ffn1_norm, v_ffn1_w_gu, v_ffn1_w_down, v_mix_norm, v_w_in, v_b_gate, v_sg_ln_g, v_sg_ln_b, v_sg_w, v_sg_b, v_mla_cq_norm, v_mla_w_uq, v_mla_ckv_norm, v_mla_w_ukv, v_mla_q_norm, v_mla_k_norm, v_mem_norm, v_mem_w_kv, v_mem_q_norm, v_mem_k_norm, v_w_branch_a, v_w_branch_b, v_w_branch_c, v_w_out, v_ffn2_norm, v_ffn2_w_gu, v_ffn2_w_down):
    given = dict(locals())
    wts = {n: given[n] for n in ORDER}
    mom = {n: given["m_" + n] for n in ORDER}
    var = {n: given["v_" + n] for n in ORDER}

    def shards(group, zero):
        out = [wts[n][0].astype(BF16) for n in GROUPS[group]]
        return [out[0] + zero.astype(BF16)] + out[1:]

    def full_weights(group, slabs):
        return _compute_layout({n: _full_from_slabs(n, s) for n, s in zip(GROUPS[group], slabs)})

    def zero_of(a):
        return jnp.minimum(jnp.abs(a.reshape(-1)[0]), 0)

    gathered_ffn1, token = _all_gather([wts[n][0].astype(BF16) for n in GROUPS["ffn1"]])
    flight = {}
    flight["ffn1_down"], token = _send_start(shards("ffn1_down", token[0, 0]), False, "gather_ffn1_down_start")
    flight["mix"] = _send_start(shards("mix", token[0, 0]), False, "gather_mix_start")[0]
    recv = {}

    def weights(group, after):
        if group == "ffn1":
            return full_weights(group, gathered_ffn1)
        landed = _send_wait(flight.pop(group), after, False, f"gather_{group}_wait")
        if group == "mix":
            flight["ffn2"] = _send_start(shards("ffn2", zero_of(landed[0])), False, "gather_ffn2_start")[0]
        return full_weights(group, landed)

    small_shapes = [wts[n].shape[1:] for n in SMALL]
    early = SMALL[1:]
    assert SMALL[0] == "ffn1_norm"

    def grads_out(group, G):
        Gr = _reference_layout({n: G[n] for n in GRAD_GROUPS[group]})
        parts = [_slabs_from_full(n, Gr[n]).astype(BF16) for n in GRAD_GROUPS[group]]
        flight["g_" + group], tie = _send_start(parts, True, f"grads_{group}_start")
        if group == "mix":
            small = _pack([G[n].reshape(s) for n, s in zip(early, small_shapes[1:])])
            small = jnp.pad(small, ((0, (-small.shape[0]) % 8), (0, 0)))
            flight["small"], tie = _send_start([small + tie[0, 0]], False, "grads_small_start")
        return tie

    P = {n: wts[n] if wts[n].ndim == 2 else wts[n][0] for n in SMALL}
    loss_part, grad_x, G = _local_step(x[0], mem[0], positions[0], loss_target[0], P, weights, grads_out)

    for group, names in GRAD_GROUPS.items():
        recv.update(zip(names, _send_wait(flight.pop("g_" + group), grad_x, True, f"grads_{group}_wait")))
    early_recv, = _send_wait(flight.pop("small"), grad_x, False, "grads_small_wait")
    last = _share_rows(G["ffn1_norm"].reshape(-1, LANES), "share_ffn1_norm")
    g_small_packed = _sum_slots(jnp.concatenate([last, early_recv], axis=1), "sum_small", 2048)

    grads, delta, new_m, new_v = {}, {}, {}, {}
    for n in SHARDED:
        grads[n], delta[n], new_m[n], new_v[n] = _sum_adamw(recv[n], wts[n], mom[n], var[n], "adamw_" + n)
    grads.update(zip(SMALL, _unpack(g_small_packed, small_shapes)))

    flat2 = lambda d: [d[n].reshape(-1, d[n].shape[-1]) for n in SMALL]
    for dst, vals in zip((delta, new_m, new_v), _adamw_small(flat2(wts), flat2(grads), flat2(mom), flat2(var))):
        dst.update(zip(SMALL, vals))

    loss = lax.psum(jnp.sum(loss_part), ("x", "y", "c"))
    lead = lambda d: [d[n].reshape(wts[n].shape) for n in ORDER]
    return (loss, grad_x[None], *lead(grads), *lead(delta), *lead(new_m), *lead(new_v))
```

```python
import functools

import numpy as np
import jax
import jax.numpy as jnp
from jax import lax
from jax.experimental import pallas as pl
from jax.experimental.pallas import tpu as pltpu

F32, BF16 = jnp.float32, jnp.bfloat16

D_MODEL = 1024
SG_GROUPS, SG_GROUP_DIM, SG_WIDTH, CHUNK = 8, 64, 512, 128
MLA_HEADS, MLA_NOPE, MLA_ROPE, MLA_V, MLA_QK = 8, 64, 32, 64, 96
MLA_Q_RANK, MLA_KV_RANK = 384, 256
MEM_HEADS, MEM_HEAD_DIM, MEM_WIDTH = 4, 128, 512
D_FF = 2816
ROPE_BASE = 10000.0
EPS = 1e-6
NEG = -1e30
ADAM_LR, ADAM_B1, ADAM_B2, ADAM_EPS, ADAM_WD, ADAM_STEP = 0.001, 0.9, 0.999, 1e-08, 0.01, 10

N_DEV = 8
LANES = 128
V7X_VMEM_LIMIT = 56 * 1024 * 1024
HP = MLA_HEADS * LANES

Z_G, Z_U, Z_V, Z_QM, Z_CKV, Z_KR, Z_CQ = 0, 3072, 3584, 4096, 4608, 4864, 4992
Z_COLS = 5376
KR_LANE = 64


def _tile(dim, pref):
    if dim <= pref:
        return dim
    for t in range(pref - pref % LANES, LANES - 1, -LANES):
        if dim % t == 0:
            return t
    for t in range(pref - pref % 8, 7, -8):
        if dim % t == 0:
            return t
    return dim


def _cparams(sem):
    return pltpu.CompilerParams(dimension_semantics=sem, vmem_limit_bytes=V7X_VMEM_LIMIT)


_DN = {"nn": ((1,), (0,)), "nt": ((1,), (1,)), "tn": ((0,), (0,))}


def _dot(a, b, mode="nn"):
    return lax.dot_general(a.astype(BF16), b.astype(BF16), (_DN[mode], ((), ())),
                           preferred_element_type=F32)


def _mm(a, b, mode, out_dtype, name, tm=512, tn=512, tk=2048, tie=None):
    if mode == "tn":
        K, M = a.shape
    else:
        M, K = a.shape
    N = b.shape[0] if mode == "nt" else b.shape[1]
    tm, tn, tk = _tile(M, tm), _tile(N, tn), _tile(K, tk)
    nk = K // tk
    if mode == "tn":
        a_spec = pl.BlockSpec((tk, tm), lambda i, j, k: (k, i))
    else:
        a_spec = pl.BlockSpec((tm, tk), lambda i, j, k: (i, k))
    if mode == "nt":
        b_spec = pl.BlockSpec((tn, tk), lambda i, j, k: (j, k))
    else:
        b_spec = pl.BlockSpec((tk, tn), lambda i, j, k: (k, j))

    ties = [] if tie is None else [tie]

    def body(a_ref, b_ref, *rest):
        o_ref, *scratch = rest[len(ties):]
        p = _dot(a_ref[...], b_ref[...], mode)
        if nk == 1:
            o_ref[...] = p.astype(o_ref.dtype)
        else:
            acc_ref, = scratch
            k = pl.program_id(2)

            @pl.when(k == 0)
            def _():
                acc_ref[...] = p

            @pl.when(k > 0)
            def _():
                acc_ref[...] += p

            @pl.when(k == nk - 1)
            def _():
                o_ref[...] = acc_ref[...].astype(o_ref.dtype)

    return pl.pallas_call(
        body, name=name, grid=(M // tm, N // tn, nk),
        in_specs=[a_spec, b_spec] + [pl.BlockSpec(t.shape, lambda i, j, k: (0, 0)) for t in ties],
        out_specs=pl.BlockSpec((tm, tn), lambda i, j, k: (i, j)),
        out_shape=jax.ShapeDtypeStruct((M, N), out_dtype),
        scratch_shapes=[] if nk == 1 else [pltpu.VMEM((tm, tn), F32)],
        compiler_params=_cparams(("parallel", "parallel", "arbitrary")),
    )(a, b, *ties)


def _mm_t(at, b, name, tm, tn, tk=1024, tie=None):
    return _mm(at, b, "nn", BF16, name, tm=tm, tn=tn, tk=tk, tie=tie)


def _rowwise(fn, name, tr, row_ins, bc_ins, row_outs, acc_outs=()):
    norm = [it if isinstance(it, tuple) else (it, it.shape[1], 0) for it in row_ins]
    rows = norm[0][0].shape[0]
    tr = _tile(rows, tr)
    arrays, in_specs = [], []
    for arr, w, cb in norm:
        arrays.append(arr)
        in_specs.append(pl.BlockSpec((tr, w), lambda i, cb=cb: (i, cb)))
    for arr in bc_ins:
        arrays.append(arr)
        in_specs.append(pl.BlockSpec(arr.shape, lambda i, nd=arr.ndim: (0,) * nd))
    n_in, n_row = len(arrays), len(row_outs)
    out_shape, out_specs, aliases = [], [], {}
    transposed = [len(o) == 3 for o in row_outs]
    for k, o in enumerate(row_outs):
        if o[0] == "into":
            _, target, w, cb = o
            aliases[len(arrays)] = k
            arrays.append(target)
            in_specs.append(pl.BlockSpec(memory_space=pl.ANY))
            out_shape.append(jax.ShapeDtypeStruct(target.shape, target.dtype))
            out_specs.append(pl.BlockSpec((tr, w), lambda i, cb=cb: (i, cb)))
        elif transposed[k]:
            out_shape.append(jax.ShapeDtypeStruct((o[0], rows), o[1]))
            out_specs.append(pl.BlockSpec((o[0], tr), lambda i: (0, i)))
        else:
            out_shape.append(jax.ShapeDtypeStruct((rows, o[0]), o[1]))
            out_specs.append(pl.BlockSpec((tr, o[0]), lambda i: (i, 0)))
    for shp, dt in acc_outs:
        out_shape.append(jax.ShapeDtypeStruct(shp, dt))
        out_specs.append(pl.BlockSpec(shp, lambda i, nd=len(shp): (0,) * nd))

    def body(*refs):
        vals = fn(*[r[...].astype(F32) for r in refs[:n_in]])
        if not isinstance(vals, (tuple, list)):
            vals = (vals,)
        outs = refs[len(arrays):]
        for r, v, t in zip(outs[:n_row], vals[:n_row], transposed):
            r[...] = v.astype(F32).T.astype(r.dtype) if t else v.astype(r.dtype)
        if acc_outs:
            accs = list(zip(outs[n_row:], vals[n_row:]))
            i = pl.program_id(0)

            @pl.when(i == 0)
            def _():
                for r, v in accs:
                    r[...] = v.astype(r.dtype)

            @pl.when(i > 0)
            def _():
                for r, v in accs:
                    r[...] += v.astype(r.dtype)

    res = pl.pallas_call(
        body, name=name, grid=(rows // tr,), in_specs=in_specs, out_specs=out_specs,
        out_shape=out_shape, input_output_aliases=aliases, compiler_params=_cparams(("arbitrary",)),
    )(*arrays)
    return res


def _rsum(x):
    return jnp.sum(x, axis=0, keepdims=True)


def _rms(x, g, n=None):
    n = x.shape[-1] if n is None else n
    r = lax.rsqrt(jnp.sum(x * x, axis=-1, keepdims=True) * (1.0 / n) + EPS)
    return x * r * g


def _rms_bwd(x, g, dy, n=None):
    n = x.shape[-1] if n is None else n
    r = lax.rsqrt(jnp.sum(x * x, axis=-1, keepdims=True) * (1.0 / n) + EPS)
    xh = x * r
    dxh = dy * g
    dx = r * (dxh - xh * (jnp.sum(dxh * xh, axis=-1, keepdims=True) * (1.0 / n)))
    return dx, _rsum(dy * xh)


def _gelu(x):
    return 0.5 * x * (1.0 + lax.erf(x * 0.7071067811865476))


def _gelu_grad(x):
    return 0.5 * (1.0 + lax.erf(x * 0.7071067811865476)) + x * jnp.exp(-0.5 * x * x) * 0.3989422804014327


def _sigmoid(x):
    return 1.0 / (1.0 + jnp.exp(-x))


FFN_TM, FFN_TN = 512, 1408
MXU_WIDTH = 256


def _col_chunks(n):
    return [(c, min(c + MXU_WIDTH, n)) for c in range(0, n, MXU_WIDTH)]


def _ffn_gu_act(h, w_gu, tag):
    T = h.shape[0]
    tm, tn = _tile(T, FFN_TM), FFN_TN
    nj = D_FF // tn

    def body(h_ref, wg_ref, wu_ref, gu_ref, a_ref, at_ref):
        h = h_ref[...]
        for c0, c1 in _col_chunks(tn):
            g = _dot(h, wg_ref[:, c0:c1])
            u = _dot(h, wu_ref[:, c0:c1])
            gu_ref[0, :, c0:c1] = g.astype(BF16)
            gu_ref[1, :, c0:c1] = u.astype(BF16)
            a = g * _sigmoid(g) * u
            a_ref[:, c0:c1] = a.astype(BF16)
            at_ref[c0:c1, :] = a.T.astype(BF16)

    return pl.pallas_call(
        body, name=f"{tag}_gu_act", grid=(T // tm, nj),
        in_specs=[pl.BlockSpec((tm, D_MODEL), lambda i, j: (i, 0)),
                  pl.BlockSpec((D_MODEL, tn), lambda i, j: (0, j)),
                  pl.BlockSpec((D_MODEL, tn), lambda i, j: (0, j + nj))],
        out_specs=[pl.BlockSpec((2, tm, tn), lambda i, j: (0, i, j)),
                   pl.BlockSpec((tm, tn), lambda i, j: (i, j)),
                   pl.BlockSpec((tn, tm), lambda i, j: (j, i))],
        out_shape=[jax.ShapeDtypeStruct((2, T, D_FF), BF16), jax.ShapeDtypeStruct((T, D_FF), BF16),
                   jax.ShapeDtypeStruct((D_FF, T), BF16)],
        compiler_params=_cparams(("parallel", "parallel")),
    )(h, w_gu, w_gu)


def _ffn_da_actbwd(do, w_down, gu, tag, tie=None):
    T = do.shape[0]
    tm, tn = _tile(T, FFN_TM), FFN_TN
    ties = [] if tie is None else [tie]

    def body(do_ref, wd_ref, gu_ref, *rest):
        dgu_ref = rest[-1]
        do = do_ref[...]
        for c0, c1 in _col_chunks(tn):
            da = _dot(do, wd_ref[c0:c1, :], "nt")
            g = gu_ref[0, :, c0:c1].astype(F32)
            u = gu_ref[1, :, c0:c1].astype(F32)
            s = _sigmoid(g)
            dgu_ref[0, :, c0:c1] = (da * u * s * (1.0 + g * (1.0 - s))).astype(BF16)
            dgu_ref[1, :, c0:c1] = (da * g * s).astype(BF16)

    return pl.pallas_call(
        body, name=f"{tag}_da_actbwd", grid=(T // tm, D_FF // tn),
        in_specs=[pl.BlockSpec((tm, D_MODEL), lambda i, j: (i, 0)),
                  pl.BlockSpec((tn, D_MODEL), lambda i, j: (j, 0)),
                  pl.BlockSpec((2, tm, tn), lambda i, j: (0, i, j))]
        + [pl.BlockSpec(t.shape, lambda i, j: (0, 0)) for t in ties],
        out_specs=pl.BlockSpec((2, tm, tn), lambda i, j: (0, i, j)),
        out_shape=jax.ShapeDtypeStruct((2, T, D_FF), BF16),
        compiler_params=_cparams(("parallel", "parallel")),
    )(do, w_down, gu, *ties)


def _ffn_dwgu(ht, dgu, tag, tk=2048):
    T = ht.shape[1]
    tn, tk = FFN_TN, _tile(T, tk)
    nj, nk = D_FF // tn, T // tk

    def body(a_ref, b_ref, o_ref, acc_ref):
        k = pl.program_id(1)
        p = _dot(a_ref[...], b_ref[...])

        @pl.when(k == 0)
        def _():
            acc_ref[...] = p

        @pl.when(k > 0)
        def _():
            acc_ref[...] += p

        @pl.when(k == nk - 1)
        def _():
            o_ref[...] = acc_ref[...].astype(o_ref.dtype)

    return pl.pallas_call(
        body, name=f"{tag}_dwgu", grid=(2 * nj, nk),
        in_specs=[pl.BlockSpec((D_MODEL, tk), lambda n, k: (0, k)),
                  pl.BlockSpec((None, tk, tn), lambda n, k: (n // nj, k, n % nj))],
        out_specs=pl.BlockSpec((D_MODEL, tn), lambda n, k: (0, n)),
        out_shape=jax.ShapeDtypeStruct((D_MODEL, 2 * D_FF), BF16),
        scratch_shapes=[pltpu.VMEM((D_MODEL, tn), F32)],
        compiler_params=_cparams(("parallel", "arbitrary")),
    )(ht, dgu)


def _ffn_dh(dgu, w_gu, tag, tm=2048, tie=None):
    T = dgu.shape[1]
    tm, tk = _tile(T, tm), FFN_TN
    nk = D_FF // tk
    ties = [] if tie is None else [tie]

    def body(a_ref, b_ref, *rest):
        o_ref, acc_ref = rest[len(ties):]
        k = pl.program_id(1)
        p = _dot(a_ref[...], b_ref[...], "nt")

        @pl.when(k == 0)
        def _():
            acc_ref[...] = p

        @pl.when(k > 0)
        def _():
            acc_ref[...] += p

        @pl.when(k == 2 * nk - 1)
        def _():
            o_ref[...] = acc_ref[...].astype(o_ref.dtype)

    return pl.pallas_call(
        body, name=f"{tag}_dh", grid=(T // tm, 2 * nk),
        in_specs=[pl.BlockSpec((None, tm, tk), lambda i, k: (k // nk, i, k % nk)),
                  pl.BlockSpec((D_MODEL, tk), lambda i, k: (0, k))]
        + [pl.BlockSpec(t.shape, lambda i, k: (0, 0)) for t in ties],
        out_specs=pl.BlockSpec((tm, D_MODEL), lambda i, k: (i, 0)),
        out_shape=jax.ShapeDtypeStruct((T, D_MODEL), BF16),
        scratch_shapes=[pltpu.VMEM((tm, D_MODEL), F32)],
        compiler_params=_cparams(("parallel", "arbitrary")),
    )(dgu, w_gu, *ties)


def _ffn_fwd(h, w_gu, w_down, tag):
    gu, a, at = _ffn_gu_act(h, w_gu, tag)
    if callable(w_down):
        w_down = w_down(at)
    o = _mm(a, w_down, "nn", BF16, f"{tag}_down", tm=1024, tn=1024, tk=2816)
    return gu, at, o


def _ffn_bwd(do, ht, gu, at, w_gu, w_down, tag, tie=None, on_dw=None):
    on_dw = on_dw or (lambda which, dw: None)
    dw_down = _mm_t(at, do, f"{tag}_dwdown", tm=1408, tn=1024, tk=2048, tie=tie)
    dgu = _ffn_da_actbwd(do, w_down, gu, tag, tie=on_dw("down", dw_down))
    dw_gu = _ffn_dwgu(ht, dgu, tag)
    dh = _ffn_dh(dgu, w_gu, tag, tie=on_dw("gu", dw_gu))
    return dh, dw_gu, dw_down


def _sg_common(u_pre, v_pre, ln_g, ln_b):
    u = _gelu(u_pre)
    v = _gelu(v_pre)
    mu = jnp.mean(v, axis=-1, keepdims=True)
    vc = v - mu
    rstd = lax.rsqrt(jnp.mean(vc * vc, axis=-1, keepdims=True) + EPS)
    vhat = vc * rstd
    vl = vhat * ln_g + ln_b
    return u, vhat, rstd, vl


def _sg_masked_pairs(w):
    t = lax.broadcasted_iota(jnp.int32, (CHUNK, CHUNK), 0)
    s = lax.broadcasted_iota(jnp.int32, (CHUNK, CHUNK), 1)
    causal = s <= t
    wm = [jnp.where(causal, w[g], 0.0).astype(BF16) for g in range(SG_GROUPS)]
    return [jnp.concatenate([wm[2 * j], wm[2 * j + 1]], axis=0) for j in range(SG_GROUPS // 2)], causal


def _sg_mix(vl, pairs, bias):
    tr = vl.shape[0]
    low = lax.broadcasted_iota(jnp.int32, (CHUNK, LANES), 1) < SG_GROUP_DIM
    vb = vl.astype(BF16)
    rows = []
    for c in range(tr // CHUNK):
        slabs = []
        for j in range(SG_GROUPS // 2):
            slab = vb[c * CHUNK:(c + 1) * CHUNK, j * LANES:(j + 1) * LANES]
            m = _dot(pairs[j], slab)
            slabs.append(jnp.where(low, m[:CHUNK], m[CHUNK:]))
        rows.append(jnp.concatenate(slabs, axis=1) + bias)
    return jnp.concatenate(rows, axis=0)


def _sg_fwd(z, ln_g, ln_b, sg_w, bias_full):
    def fn(u_pre, v_pre, ln_g, ln_b, w, bias):
        u, _, _, vl = _sg_common(u_pre, v_pre, ln_g, ln_b)
        pairs, _ = _sg_masked_pairs(w)
        y = u * _sg_mix(vl, pairs, bias)
        return y, y

    return _rowwise(fn, "sg_fwd", 512, [(z, SG_WIDTH, Z_U // SG_WIDTH), (z, SG_WIDTH, Z_V // SG_WIDTH)],
                    [ln_g, ln_b, sg_w, bias_full], [(SG_WIDTH, BF16), (SG_WIDTH, BF16, "T")])


def _sg_bwd(z, dy, ln_g, ln_b, sg_w, bias_full, group_ind, dz):
    def fn(u_pre, v_pre, dy, ln_g, ln_b, w, bias, ind):
        dy = dy.astype(F32)
        u, vhat, rstd, vl = _sg_common(u_pre, v_pre, ln_g, ln_b)
        pairs, causal = _sg_masked_pairs(w)
        mixed = _sg_mix(vl, pairs, bias)
        du_pre = dy * mixed * _gelu_grad(u_pre)
        dmix = dy * u
        tr = dy.shape[0]
        low = lax.broadcasted_iota(jnp.int32, (CHUNK, LANES), 1) < SG_GROUP_DIM
        vb = vl.astype(BF16)
        dw = [jnp.zeros((CHUNK, CHUNK), F32) for _ in range(SG_GROUPS)]
        dbias = jnp.zeros((CHUNK, SG_WIDTH), F32)
        dvl_rows = []
        for c in range(tr // CHUNK):
            dm_c = dmix[c * CHUNK:(c + 1) * CHUNK]
            dbias = dbias + dm_c
            slabs = []
            for j in range(SG_GROUPS // 2):
                slab = vb[c * CHUNK:(c + 1) * CHUNK, j * LANES:(j + 1) * LANES]
                dm = dm_c[:, j * LANES:(j + 1) * LANES]
                d0 = jnp.where(low, dm, 0.0).astype(BF16)
                d1 = jnp.where(low, 0.0, dm).astype(BF16)
                dw[2 * j] = dw[2 * j] + _dot(d0, slab, "nt")
                dw[2 * j + 1] = dw[2 * j + 1] + _dot(d1, slab, "nt")
                slabs.append(_dot(pairs[j], jnp.concatenate([d0, d1], axis=0), "tn"))
            dvl_rows.append(jnp.concatenate(slabs, axis=1))
        dvl = jnp.concatenate(dvl_rows, axis=0)
        dln_g = _rsum(dvl * vhat)
        dln_b = _rsum(dvl)
        dvh = dvl * ln_g
        dv = rstd * (dvh - jnp.mean(dvh, axis=-1, keepdims=True)
                     - vhat * jnp.mean(dvh * vhat, axis=-1, keepdims=True))
        dv_pre = dv * _gelu_grad(v_pre)
        dw = jnp.stack([jnp.where(causal, d, 0.0) for d in dw], axis=0)
        dbias_t = lax.dot_general(dbias, ind, (((1,), (0,)), ((), ())), precision=lax.Precision.HIGHEST,
                                  preferred_element_type=F32)
        return jnp.concatenate([du_pre, dv_pre], axis=1), dw, dbias_t, dln_g, dln_b

    return _rowwise(fn, "sg_bwd", 512,
                    [(z, SG_WIDTH, Z_U // SG_WIDTH), (z, SG_WIDTH, Z_V // SG_WIDTH), dy],
                    [ln_g, ln_b, sg_w, bias_full, group_ind],
                    [("into", dz, 2 * SG_WIDTH, Z_U // (2 * SG_WIDTH))],
                    [((SG_GROUPS, CHUNK, CHUNK), F32), ((CHUNK, SG_GROUPS), F32), ((1, SG_WIDTH), F32), ((1, SG_WIDTH), F32)])


def _rope(x, c, s1, s2):
    return x * c + pltpu.roll(x, LANES - MLA_ROPE // 2, 1) * s1 + pltpu.roll(x, MLA_ROPE // 2, 1) * s2


def _rope_t(d, c, s1, s2):
    return d * c + pltpu.roll(d * s1, MLA_ROPE // 2, 1) + pltpu.roll(d * s2, LANES - MLA_ROPE // 2, 1)


def _mla_post(q_pre, kv_pre, z, tabs, gq, gk):
    scale = MLA_QK ** -0.5 * LOG2E

    def fn(q_pre, k_pre, v_pre, kr, c, s1, s2, gq, gk):
        qs, ks = [], []
        for h in range(MLA_HEADS):
            sl = slice(h * LANES, (h + 1) * LANES)
            qs.append(_rope(_rms(q_pre[:, sl], gq, MLA_QK), c, s1, s2) * scale)
            ks.append(_rope(_rms(k_pre[:, sl] + kr, gk, MLA_QK), c, s1, s2))
        lane = lax.broadcasted_iota(jnp.int32, v_pre.shape, 1) & (LANES - 1)
        return jnp.concatenate(qs, axis=1), jnp.concatenate(ks, axis=1), jnp.where(lane == ONES_LANE, 1.0, v_pre)

    return _rowwise(fn, "mla_post", 256,
                    [q_pre, (kv_pre, HP, 0), (kv_pre, HP, 1), (z, LANES, Z_KR // LANES), *tabs],
                    [gq, gk], [(HP, BF16)] * 3)


def _mla_post_bwd(q_pre, kv_pre, z, tabs, gq, gk, dq, dk, dv):
    scale = MLA_QK ** -0.5

    def fn(q_pre, k_pre, kr, c, s1, s2, dq, dk, dv, gq, gk):
        lane = lax.broadcasted_iota(jnp.int32, (1, LANES), 1)
        kr_mask = (lane >= KR_LANE) & (lane < KR_LANE + MLA_ROPE)
        dqs, dks = [], []
        dgq = jnp.zeros((1, LANES), F32)
        dgk = jnp.zeros((1, LANES), F32)
        dkr = jnp.zeros(kr.shape, F32)
        for h in range(MLA_HEADS):
            sl = slice(h * LANES, (h + 1) * LANES)
            dqn = _rope_t(dq[:, sl].astype(F32), c, s1, s2) * scale
            dx, dg = _rms_bwd(q_pre[:, sl], gq, dqn, MLA_QK)
            dqs.append(dx)
            dgq = dgq + dg
            dkn = _rope_t(dk[:, sl].astype(F32), c, s1, s2)
            dx, dg = _rms_bwd(k_pre[:, sl] + kr, gk, dkn, MLA_QK)
            dks.append(dx)
            dgk = dgk + dg
            dkr = dkr + dx
        dkr = jnp.where(kr_mask, dkr, 0.0)
        dkv = jnp.concatenate(dks + [dv.astype(F32)], axis=1)
        return jnp.concatenate(dqs, axis=1), dkv, dkr, dgq, dgk

    return _rowwise(fn, "mla_post_bwd", 256,
                    [q_pre, (kv_pre, HP, 0), (z, LANES, Z_KR // LANES), *tabs, dq, dk, dv],
                    [gq, gk], [(HP, BF16), (2 * HP, BF16), (LANES, BF16)],
                    [((1, LANES), F32), ((1, LANES), F32)])


def _pairs(n, lower):
    a, b = [], []
    for o in range(n):
        inner = range(o + 1) if lower else range(o, n)
        for t in inner:
            a.append(o)
            b.append(t)
    return jnp.asarray(np.array(a, np.int32)), jnp.asarray(np.array(b, np.int32))


FLASH_TILE, FLASH_SUB_ROWS = 2048, 512
LOG2E, LN2 = 1.4426950408889634, 0.6931471805599453
ONES_LANE = MLA_V


def _flash_tiles(T):
    tq = _tile(T, FLASH_TILE)
    return tq, _tile(tq, FLASH_SUB_ROWS)


def _col_span(t, sr, rb, diag, key_major):
    if not diag:
        return 0, t
    return (rb * sr, t) if key_major else (0, (rb + 1) * sr)


def _span_iota(sr, rb, c0, c1):
    r = lax.broadcasted_iota(jnp.int32, (sr, c1 - c0), 0) + rb * sr
    c = lax.broadcasted_iota(jnp.int32, (sr, c1 - c0), 1) + c0
    return r, c


def _lanes(x, width):
    return jnp.concatenate([x] * (width // LANES), axis=1)


def _flash_fwd(q, k, v):
    T = q.shape[0]
    tq, sr = _flash_tiles(T)
    n = T // tq
    ii, jj = _pairs(n, True)

    def body(ii_ref, jj_ref, q_ref, k_ref, v_ref, o_ref, ot_ref, lse_ref, lset_ref, m_sc, acc_sc):
        p_ = pl.program_id(1)
        i, j = ii_ref[p_], jj_ref[p_]

        @pl.when(j == 0)
        def _():
            m_sc[...] = jnp.full(m_sc.shape, NEG, F32)
            acc_sc[...] = jnp.zeros(acc_sc.shape, F32)

        def tile(diag):
            for rb in range(tq // sr):
                rows = slice(rb * sr, (rb + 1) * sr)
                c0, c1 = _col_span(tq, sr, rb, diag, False)
                s = _dot(q_ref[rows, :], k_ref[c0:c1, :], "nt")
                if diag:
                    r, c = _span_iota(sr, rb, c0, c1)
                    s = jnp.where(c <= r, s, NEG)
                m = m_sc[rows, :]
                m_new = jnp.maximum(m, jnp.max(s, axis=1, keepdims=True))
                p = jnp.exp2(s - _lanes(m_new, c1 - c0))
                acc_sc[rows, :] = jnp.exp2(m - m_new) * acc_sc[rows, :] + _dot(p, v_ref[c0:c1, :])
                m_sc[rows, :] = m_new

        @pl.when(j < i)
        def _():
            tile(False)

        @pl.when(j == i)
        def _():
            tile(True)
            acc = acc_sc[...]
            lane = lax.broadcasted_iota(jnp.int32, acc.shape, 1)
            l = jnp.sum(jnp.where(lane == ONES_LANE, acc, 0.0), axis=1, keepdims=True)
            o = jnp.where(lane < MLA_V, acc / l, 0.0)
            o_ref[...] = o.astype(o_ref.dtype)
            ot_ref[...] = o.T.astype(ot_ref.dtype)
            lse = m_sc[...] + jnp.log2(l)
            lse_ref[...] = lse
            lset_ref[...] = lse.T[:8]

    blk = lambda which: pl.BlockSpec((tq, LANES), which)
    qmap = lambda h, p, ii, jj: (ii[p], h)
    kmap = lambda h, p, ii, jj: (jj[p], h)
    tmap = lambda h, p, ii, jj: (h, ii[p])
    return pl.pallas_call(
        body, name="mla_flash_fwd",
        grid_spec=pltpu.PrefetchScalarGridSpec(
            num_scalar_prefetch=2, grid=(MLA_HEADS, int(ii.shape[0])),
            in_specs=[blk(qmap), blk(kmap), blk(kmap)],
            out_specs=[blk(qmap), pl.BlockSpec((LANES, tq), tmap), blk(qmap), pl.BlockSpec((8, tq), tmap)],
            scratch_shapes=[pltpu.VMEM((tq, LANES), F32)] * 2),
        out_shape=[jax.ShapeDtypeStruct((T, HP), BF16), jax.ShapeDtypeStruct((HP, T), BF16),
                   jax.ShapeDtypeStruct((T, HP), F32), jax.ShapeDtypeStruct((8 * MLA_HEADS, T), F32)],
        compiler_params=_cparams(("parallel", "arbitrary")),
    )(ii, jj, q, k, v)


def _flash_dq(q, k, v, do, lse, delta):
    T = q.shape[0]
    tq, sr = _flash_tiles(T)
    n = T // tq
    ii, jj = _pairs(n, True)

    def body(ii_ref, jj_ref, q_ref, k_ref, v_ref, do_ref, lse_ref, dl_ref, dq_ref, acc_sc):
        p_ = pl.program_id(1)
        i, j = ii_ref[p_], jj_ref[p_]

        @pl.when(j == 0)
        def _():
            acc_sc[...] = jnp.zeros(acc_sc.shape, F32)

        def tile(diag):
            for rb in range(tq // sr):
                rows = slice(rb * sr, (rb + 1) * sr)
                c0, c1 = _col_span(tq, sr, rb, diag, False)
                ks = k_ref[c0:c1, :]
                p = jnp.exp2(_dot(q_ref[rows, :], ks, "nt") - _lanes(lse_ref[rows, :], c1 - c0))
                if diag:
                    r, c = _span_iota(sr, rb, c0, c1)
                    p = jnp.where(c <= r, p, 0.0)
                dp = _dot(do_ref[rows, :], v_ref[c0:c1, :], "nt")
                acc_sc[rows, :] += _dot(p * (dp - _lanes(dl_ref[rows, :], c1 - c0)), ks)

        @pl.when(j < i)
        def _():
            tile(False)

        @pl.when(j == i)
        def _():
            tile(True)
            dq_ref[...] = acc_sc[...].astype(dq_ref.dtype)

    blk = lambda which: pl.BlockSpec((tq, LANES), which)
    qmap = lambda h, p, ii, jj: (ii[p], h)
    kmap = lambda h, p, ii, jj: (jj[p], h)
    return pl.pallas_call(
        body, name="mla_flash_dq",
        grid_spec=pltpu.PrefetchScalarGridSpec(
            num_scalar_prefetch=2, grid=(MLA_HEADS, int(ii.shape[0])),
            in_specs=[blk(qmap), blk(kmap), blk(kmap), blk(qmap), blk(qmap), blk(qmap)],
            out_specs=blk(qmap),
            scratch_shapes=[pltpu.VMEM((tq, LANES), F32)]),
        out_shape=jax.ShapeDtypeStruct((T, HP), BF16),
        compiler_params=_cparams(("parallel", "arbitrary")),
    )(ii, jj, q, k, v, do, lse, delta)


def _flash_dkv(q, k, v, do, lse_t, delta_t):
    T = q.shape[0]
    tq, sr = _flash_tiles(T)
    n = T // tq
    jj, ii = _pairs(n, False)

    def body(jj_ref, ii_ref, q_ref, k_ref, v_ref, do_ref, lse_ref, dl_ref, dk_ref, dv_ref, dk_sc, dv_sc):
        p_ = pl.program_id(1)
        j, i = jj_ref[p_], ii_ref[p_]

        @pl.when(i == j)
        def _():
            dk_sc[...] = jnp.zeros(dk_sc.shape, F32)
            dv_sc[...] = jnp.zeros(dv_sc.shape, F32)

        def tile(diag):
            for rb in range(tq // sr):
                rows = slice(rb * sr, (rb + 1) * sr)
                c0, c1 = _col_span(tq, sr, rb, diag, True)
                qs, dos = q_ref[c0:c1, :], do_ref[c0:c1, :]
                pt = jnp.exp2(_dot(k_ref[rows, :], qs, "nt") - lse_ref[:1, c0:c1])
                if diag:
                    r, c = _span_iota(sr, rb, c0, c1)
                    pt = jnp.where(r <= c, pt, 0.0)
                dpt = _dot(v_ref[rows, :], dos, "nt")
                dv_sc[rows, :] += _dot(pt, dos)
                dk_sc[rows, :] += _dot(pt * (dpt - dl_ref[:1, c0:c1]), qs)

        @pl.when(i == j)
        def _():
            tile(True)

        @pl.when(i > j)
        def _():
            tile(False)

        @pl.when(i == n - 1)
        def _():
            dk_ref[...] = (dk_sc[...] * LN2).astype(dk_ref.dtype)
            dv_ref[...] = dv_sc[...].astype(dv_ref.dtype)

    blk = lambda which: pl.BlockSpec((tq, LANES), which)
    qmap = lambda h, p, jj, ii: (ii[p], h)
    kmap = lambda h, p, jj, ii: (jj[p], h)
    lse_rows = pl.BlockSpec((8, tq), lambda h, p, jj, ii: (h, ii[p]))
    delta_rows = pl.BlockSpec((8, tq), lambda h, p, jj, ii: (h * (LANES // 8), ii[p]))
    return pl.pallas_call(
        body, name="mla_flash_dkv",
        grid_spec=pltpu.PrefetchScalarGridSpec(
            num_scalar_prefetch=2, grid=(MLA_HEADS, int(ii.shape[0])),
            in_specs=[blk(qmap), blk(kmap), blk(kmap), blk(qmap), lse_rows, delta_rows],
            out_specs=[blk(kmap), blk(kmap)],
            scratch_shapes=[pltpu.VMEM((tq, LANES), F32)] * 2),
        out_shape=[jax.ShapeDtypeStruct((T, HP), BF16)] * 2,
        compiler_params=_cparams(("parallel", "arbitrary")),
    )(jj, ii, q, k, v, do, lse_t, delta_t)


def _mem_fwd(z, km, vm, gq):
    scale = MEM_HEAD_DIM ** -0.5

    def fn(qm, km, vm, gq):
        ys = []
        for h in range(MEM_HEADS):
            sl = slice(h * LANES, (h + 1) * LANES)
            q = _rms(qm[:, sl], gq) * scale
            s = _dot(q, km[:, sl], "nt")
            p = jnp.exp(s - jnp.max(s, axis=1, keepdims=True))
            p = p / jnp.sum(p, axis=1, keepdims=True)
            ys.append(_dot(p, vm[:, sl]))
        y = jnp.concatenate(ys, axis=1)
        return y, y

    return _rowwise(fn, "mem_fwd", 512, [(z, MEM_WIDTH, Z_QM // MEM_WIDTH)], [km, vm, gq],
                    [(MEM_WIDTH, BF16), (MEM_WIDTH, BF16, "T")])


def _mem_bwd(z, dy, km, vm, gq, dz):
    scale = MEM_HEAD_DIM ** -0.5

    def fn(qm, dy, km, vm, gq):
        dqs, dks, dvs = [], [], []
        dgq = jnp.zeros((1, LANES), F32)
        for h in range(MEM_HEADS):
            sl = slice(h * LANES, (h + 1) * LANES)
            q = (_rms(qm[:, sl], gq) * scale).astype(BF16)
            dyh = dy[:, sl]
            kh, vh = km[:, sl], vm[:, sl]
            s = _dot(q, kh, "nt")
            p = jnp.exp(s - jnp.max(s, axis=1, keepdims=True))
            p = p / jnp.sum(p, axis=1, keepdims=True)
            dp = _dot(dyh, vh, "nt")
            ds = p * (dp - jnp.sum(p * dp, axis=1, keepdims=True))
            dq = _dot(ds, kh) * scale
            dx, dg = _rms_bwd(qm[:, sl], gq, dq)
            dqs.append(dx)
            dgq = dgq + dg
            st = _dot(kh, q, "nt")
            pt = jnp.exp(st - jnp.max(st, axis=0, keepdims=True))
            pt = pt / jnp.sum(pt, axis=0, keepdims=True)
            dpt = _dot(vh, dyh, "nt")
            dst = pt * (dpt - jnp.sum(pt * dpt, axis=0, keepdims=True))
            dvs.append(_dot(pt, dyh))
            dks.append(_dot(dst, q))
        return jnp.concatenate(dqs, axis=1), jnp.concatenate(dks, axis=1), jnp.concatenate(dvs, axis=1), dgq

    m = km.shape[0]
    return _rowwise(fn, "mem_bwd", 512, [(z, MEM_WIDTH, Z_QM // MEM_WIDTH), dy], [km, vm, gq],
                    [("into", dz, MEM_WIDTH, Z_QM // MEM_WIDTH)],
                    [((m, MEM_WIDTH), F32), ((m, MEM_WIDTH), F32), ((1, LANES), F32)])


GROUPS = {"ffn1": ["ffn1_w_gu"], "ffn1_down": ["ffn1_w_down"],
          "mix": ["w_in", "mla_w_uq", "mla_w_ukv", "mem_w_kv", "w_branch_a", "w_branch_b", "w_branch_c", "w_out"],
          "ffn2": ["ffn2_w_gu", "ffn2_w_down"]}
GRAD_GROUPS = {"ffn2": GROUPS["ffn2"], "mix": GROUPS["mix"], "ffn1_down": ["ffn1_w_down"], "ffn1_gu": ["ffn1_w_gu"]}


def _local_step(x, mem, positions, loss_target, P, weights, grads_out):
    T = x.shape[0]
    G = {}
    W = dict(weights("ffn1", None))

    half = MLA_ROPE // 2
    inv = ROPE_BASE ** (-jnp.arange(half, dtype=F32) / half)
    ang = positions.astype(F32)[:, None] * inv
    cos, sin = jnp.cos(ang), jnp.sin(ang)
    one, zero = jnp.ones((T, MLA_NOPE), F32), jnp.zeros((T, half), F32)
    pad = LANES - MLA_QK
    tabs = (jnp.concatenate([one, cos, cos, jnp.ones((T, pad), F32)], axis=1),
            jnp.concatenate([jnp.zeros((T, MLA_NOPE), F32), -sin, zero, jnp.zeros((T, pad), F32)], axis=1),
            jnp.concatenate([jnp.zeros((T, MLA_NOPE), F32), zero, sin, jnp.zeros((T, pad), F32)], axis=1))
    gq_p = jnp.pad(P["mla_q_norm"], ((0, 0), (0, pad)))
    gk_p = jnp.pad(P["mla_k_norm"], ((0, 0), (0, pad)))
    bias_full = jnp.repeat(P["sg_b"].T, SG_GROUP_DIM, axis=1)
    group_ind = jnp.repeat(jnp.eye(SG_GROUPS, dtype=F32), SG_GROUP_DIM, axis=0)

    HT = (D_MODEL, BF16, "T")

    def norm2(x, g):
        h = _rms(x, g)
        return h, h

    h1, h1t = _rowwise(norm2, "ffn1_norm", 512, [x], [P["ffn1_norm"]], [(D_MODEL, BF16), HT])
    def ffn1_w_down(after):
        W.update(weights("ffn1_down", after))
        return W["ffn1_w_down"]

    gu1, a1t, o1 = _ffn_fwd(h1, W["ffn1_w_gu"], ffn1_w_down, "ffn1")

    def resid_norm(x, o, g):
        xn = x + 0.5 * o
        h = _rms(xn, g)
        return xn, h, h

    x1, hm, hmt = _rowwise(resid_norm, "mix_norm", 512, [x, o1], [P["mix_norm"]],
                           [(D_MODEL, F32), (D_MODEL, BF16), HT])
    W.update(weights("mix", hm))
    z = _mm(hm, W["w_in"], "nn", BF16, "w_in", tm=1024, tn=1792)

    y_a, y_at = _sg_fwd(z, P["sg_ln_g"], P["sg_ln_b"], P["sg_w"], bias_full)

    def c_norm(cq, ckv, gq, gkv):
        a, b = _rms(cq, gq), _rms(ckv, gkv)
        return a, b, a, b

    cqn, ckvn, cqnt, ckvnt = _rowwise(
        c_norm, "mla_cnorm", 512, [(z, MLA_Q_RANK, Z_CQ // MLA_Q_RANK), (z, MLA_KV_RANK, Z_CKV // MLA_KV_RANK)],
        [P["mla_cq_norm"], P["mla_ckv_norm"]],
        [(MLA_Q_RANK, BF16), (MLA_KV_RANK, BF16), (MLA_Q_RANK, BF16, "T"), (MLA_KV_RANK, BF16, "T")])
    q_pre = _mm(cqn, W["mla_w_uq"], "nn", BF16, "mla_uq", tm=1024, tn=1024)
    kv_pre = _mm(ckvn, W["mla_w_ukv"], "nn", BF16, "mla_ukv", tm=1024, tn=1024)
    q, k, v = _mla_post(q_pre, kv_pre, z, tabs, gq_p, gk_p)
    y_b, y_bt, lse, lse_t = _flash_fwd(q, k, v)

    memn, = _rowwise(lambda m, g: _rms(m, g), "mem_norm", 256, [mem], [P["mem_norm"]], [(D_MODEL, BF16)])
    kvm = _mm(memn, W["mem_w_kv"], "nn", F32, "mem_kv")

    def mem_k(kvm, gk):
        ks = [_rms(kvm[:, h * LANES:(h + 1) * LANES], gk) for h in range(MEM_HEADS)]
        return jnp.concatenate(ks, axis=1), kvm[:, MEM_WIDTH:]

    km, vm = _rowwise(mem_k, "mem_knorm", 256, [kvm], [P["mem_k_norm"]], [(MEM_WIDTH, BF16), (MEM_WIDTH, BF16)])
    y_c, y_ct = _mem_fwd(z, km, vm, P["mem_q_norm"])

    pa = _mm(y_a, W["w_branch_a"], "nn", BF16, "branch_a", tm=1024, tn=1024)
    pb = _mm(y_b, W["w_branch_b"], "nn", BF16, "branch_b", tm=1024, tn=1024)
    pc = _mm(y_c, W["w_branch_c"], "nn", BF16, "branch_c", tm=1024, tn=1024)

    def merge(zg, pa, pb, pc, b):
        g = _sigmoid(zg + b)
        m = g[:, :D_MODEL] * pa + g[:, D_MODEL:2 * D_MODEL] * pb + g[:, 2 * D_MODEL:] * pc
        return m, m

    merged, mergedt = _rowwise(merge, "merge", 256, [(z, 3 * D_MODEL, 0), pa, pb, pc], [P["b_gate"]],
                               [(D_MODEL, BF16), HT])
    om = _mm(merged, W["w_out"], "nn", BF16, "w_out", tm=1024, tn=1024)

    def resid_norm1(x, o, g):
        xn = x + o
        h = _rms(xn, g)
        return xn, h, h

    x2, h2, h2t = _rowwise(resid_norm1, "ffn2_norm", 512, [x1, om], [P["ffn2_norm"]],
                           [(D_MODEL, F32), (D_MODEL, BF16), HT])
    W.update(weights("ffn2", h2))
    gu2, a2t, o2 = _ffn_fwd(h2, W["ffn2_w_gu"], W["ffn2_w_down"], "ffn2")

    def loss_fn(x2, o2, t):
        e = x2 + 0.5 * o2 - t
        return e * (1.0 / D_MODEL), (e * (0.5 / D_MODEL)).astype(BF16), _rsum(e * e) * (0.5 / D_MODEL)

    dx3, do2, loss_part = _rowwise(loss_fn, "loss", 512, [x2, o2, loss_target], [],
                                   [(D_MODEL, F32), (D_MODEL, BF16)], [((1, D_MODEL), F32)])

    dh2, G["ffn2_w_gu"], G["ffn2_w_down"] = _ffn_bwd(do2, h2t, gu2, a2t, W["ffn2_w_gu"], W["ffn2_w_down"], "ffn2")
    tie = grads_out("ffn2", G)

    def norm_bwd(x, dh, dxo, g, *_):
        dx, dg = _rms_bwd(x, g, dh)
        dx = dx + dxo
        return dx, dx, dg

    dx2, dx2b, G["ffn2_norm"] = _rowwise(norm_bwd, "ffn2_norm_bwd", 512, [x2, dh2, dx3],
                                         [P["ffn2_norm"]] + ([] if tie is None else [tie]),
                                         [(D_MODEL, F32), (D_MODEL, BF16)], [((1, D_MODEL), F32)])

    G["w_out"] = _mm_t(mergedt, dx2b, "w_out_dw", tm=1024, tn=1024)
    dmerged = _mm(dx2b, W["w_out"], "nt", BF16, "w_out_dx", tm=1024, tn=1024)

    def merge_bwd(zg, pa, pb, pc, dm, b):
        g = _sigmoid(zg + b)
        ps = jnp.concatenate([pa, pb, pc], axis=1)
        dm3 = jnp.concatenate([dm, dm, dm], axis=1)
        dzg = dm3 * ps * g * (1.0 - g)
        dp = dm3 * g
        return dzg, dp[:, :D_MODEL], dp[:, D_MODEL:2 * D_MODEL], dp[:, 2 * D_MODEL:], _rsum(dzg)

    dz = lax.empty((T, Z_COLS), BF16)
    dz, dpa, dpb, dpc, G["b_gate"] = _rowwise(
        merge_bwd, "merge_bwd", 256, [(z, 3 * D_MODEL, 0), pa, pb, pc, dmerged], [P["b_gate"]],
        [("into", dz, 3 * D_MODEL, 0), (D_MODEL, BF16), (D_MODEL, BF16), (D_MODEL, BF16)], [((1, 3 * D_MODEL), F32)])

    G["w_branch_a"] = _mm_t(y_at, dpa, "branch_a_dw", tm=512, tn=1024)
    G["w_branch_b"] = _mm_t(y_bt, dpb, "branch_b_dw", tm=1024, tn=1024)
    G["w_branch_c"] = _mm_t(y_ct, dpc, "branch_c_dw", tm=512, tn=1024)
    dy_a = _mm(dpa, W["w_branch_a"], "nt", BF16, "branch_a_dx", tm=1024, tn=512)
    dy_b = _mm(dpb, W["w_branch_b"], "nt", BF16, "branch_b_dx", tm=1024, tn=1024)
    dy_c = _mm(dpc, W["w_branch_c"], "nt", BF16, "branch_c_dx", tm=1024, tn=512)

    dz, G["sg_w"], dbias_t, G["sg_ln_g"], G["sg_ln_b"] = _sg_bwd(
        z, dy_a, P["sg_ln_g"], P["sg_ln_b"], P["sg_w"], bias_full, group_ind, dz)
    G["sg_b"] = dbias_t.T

    dz, dkm, dvm, G["mem_q_norm"] = _mem_bwd(z, dy_c, km, vm, P["mem_q_norm"], dz)

    def mem_k_bwd(kvm, dkm, dvm, gk):
        dks = []
        dg = jnp.zeros((1, LANES), F32)
        for h in range(MEM_HEADS):
            sl = slice(h * LANES, (h + 1) * LANES)
            dx, d = _rms_bwd(kvm[:, sl], gk, dkm[:, sl])
            dks.append(dx)
            dg = dg + d
        return jnp.concatenate(dks + [dvm], axis=1), dg

    dkvm, G["mem_k_norm"] = _rowwise(mem_k_bwd, "mem_knorm_bwd", 256, [kvm, dkm, dvm], [P["mem_k_norm"]],
                                     [(2 * MEM_WIDTH, BF16)], [((1, LANES), F32)])
    G["mem_w_kv"] = _mm(memn, dkvm, "tn", BF16, "mem_kv_dw")
    dmemn = _mm(dkvm, W["mem_w_kv"], "nt", F32, "mem_kv_dx")
    _, G["mem_norm"] = _rowwise(lambda m, d, g: _rms_bwd(m, g, d), "mem_norm_bwd", 256, [mem, dmemn],
                                [P["mem_norm"]], [(D_MODEL, BF16)], [((1, D_MODEL), F32)])

    def delta_fn(o, do):
        od = o.astype(F32) * do.astype(F32)
        ds = [jnp.broadcast_to(jnp.sum(od[:, h * LANES:(h + 1) * LANES], axis=1, keepdims=True), (od.shape[0], LANES))
              for h in range(MLA_HEADS)]
        d = jnp.concatenate(ds, axis=1)
        return d, d

    delta, delta_t = _rowwise(delta_fn, "mla_delta", 512, [y_b, dy_b], [], [(HP, F32), (HP, F32, "T")])
    dq = _flash_dq(q, k, v, dy_b, lse, delta)
    dk, dv = _flash_dkv(q, k, v, dy_b, lse_t, delta_t)
    dq_pre, dkv_pre, dkr, dgq, dgk = _mla_post_bwd(q_pre, kv_pre, z, tabs, gq_p, gk_p, dq, dk, dv)
    G["mla_q_norm"], G["mla_k_norm"] = dgq[:, :MLA_QK], dgk[:, :MLA_QK]
    G["mla_w_uq"] = _mm_t(cqnt, dq_pre, "mla_uq_dw", tm=384, tn=1024)
    G["mla_w_ukv"] = _mm_t(ckvnt, dkv_pre, "mla_ukv_dw", tm=256, tn=2048)
    dcqn = _mm(dq_pre, W["mla_w_uq"], "nt", BF16, "mla_uq_dx", tm=1024)
    dckvn = _mm(dkv_pre, W["mla_w_ukv"], "nt", BF16, "mla_ukv_dx", tm=1024)

    def c_norm_bwd(cq, ckv, dcqn, dckvn, dkr, gq, gkv):
        dcq, dgq = _rms_bwd(cq, gq, dcqn)
        dckv, dgkv = _rms_bwd(ckv, gkv, dckvn)
        return jnp.concatenate([dckv, dkr, dcq], axis=1), dgq, dgkv

    tail = Z_COLS - Z_CKV
    dz, G["mla_cq_norm"], G["mla_ckv_norm"] = _rowwise(
        c_norm_bwd, "mla_cnorm_bwd", 512,
        [(z, MLA_Q_RANK, Z_CQ // MLA_Q_RANK), (z, MLA_KV_RANK, Z_CKV // MLA_KV_RANK), dcqn, dckvn, dkr],
        [P["mla_cq_norm"], P["mla_ckv_norm"]], [("into", dz, tail, Z_CKV // tail)],
        [((1, MLA_Q_RANK), F32), ((1, MLA_KV_RANK), F32)])
    G["w_in"] = _mm_t(hmt, dz, "w_in_dw", tm=1024, tn=1792, tk=2048)
    dhm = _mm(dz, W["w_in"], "nt", BF16, "w_in_dx", tm=1024, tn=1024, tk=2688)

    def norm_bwd_half(x, dh, dxo, g):
        dx, dg = _rms_bwd(x, g, dh)
        dx = dx + dxo
        return dx, (0.5 * dx), dg

    dx1, do1, G["mix_norm"] = _rowwise(norm_bwd_half, "mix_norm_bwd", 512, [x1, dhm, dx2], [P["mix_norm"]],
                                       [(D_MODEL, F32), (D_MODEL, BF16)], [((1, D_MODEL), F32)])
    tie = grads_out("mix", G)

    def ffn1_dw(which, dw):
        G["ffn1_w_" + which] = dw
        return grads_out("ffn1_" + which, G)

    dh1, _, _ = _ffn_bwd(do1, h1t, gu1, a1t, W["ffn1_w_gu"], W["ffn1_w_down"], "ffn1", tie, ffn1_dw)

    def norm_bwd_last(x, dh, dxo, g):
        dx, dg = _rms_bwd(x, g, dh)
        return dx + dxo, dg

    grad_x, G["ffn1_norm"] = _rowwise(norm_bwd_last, "ffn1_norm_bwd", 512, [x, dh1, dx1], [P["ffn1_norm"]],
                                      [(D_MODEL, F32)], [((1, D_MODEL), F32)])
    return loss_part, grad_x, G


SHARDED = ["ffn1_w_gu", "ffn1_w_down", "w_in", "mla_w_uq", "mla_w_ukv", "mem_w_kv",
           "w_branch_a", "w_branch_b", "w_branch_c", "w_out", "ffn2_w_gu", "ffn2_w_down"]
ROW_SHARDED = {"ffn1_w_down", "mem_w_kv", "w_out", "ffn2_w_down"}
SMALL = ["ffn1_norm", "mix_norm", "b_gate", "sg_ln_g", "sg_ln_b", "sg_w", "sg_b", "mla_cq_norm",
         "mla_ckv_norm", "mla_q_norm", "mla_k_norm", "mem_norm", "mem_q_norm", "mem_k_norm", "ffn2_norm"]
ORDER = ["ffn1_norm", "ffn1_w_gu", "ffn1_w_down", "mix_norm", "w_in", "b_gate", "sg_ln_g", "sg_ln_b", "sg_w",
         "sg_b", "mla_cq_norm", "mla_w_uq", "mla_ckv_norm", "mla_w_ukv", "mla_q_norm", "mla_k_norm", "mem_norm",
         "mem_w_kv", "mem_q_norm", "mem_k_norm", "w_branch_a", "w_branch_b", "w_branch_c", "w_out", "ffn2_norm",
         "ffn2_w_gu", "ffn2_w_down"]

_IN_U, _IN_V, _IN_CQ, _IN_CKV, _IN_KR, _IN_QM, _IN_G = 0, 512, 1024, 1408, 1664, 1696, 2208
IN_COLS = 5280


def _full_from_slabs(name, slabs):
    n, r, c = slabs.shape
    if name in ROW_SHARDED:
        return slabs.reshape(n * r, c)
    return slabs.transpose(1, 0, 2).reshape(r, n * c)


def _slabs_from_full(name, full):
    if name in ROW_SHARDED:
        return full.reshape(N_DEV, full.shape[0] // N_DEV, full.shape[1])
    r, c = full.shape
    return full.reshape(r, N_DEV, c // N_DEV).transpose(1, 0, 2)


def _compute_layout(full):
    W = dict(full)
    if "w_in" not in full:
        return W
    w = full["w_in"]
    kr = jnp.pad(w[:, _IN_KR:_IN_QM], ((0, 0), (KR_LANE, LANES - KR_LANE - MLA_ROPE)))
    W["w_in"] = jnp.concatenate([w[:, _IN_G:], w[:, _IN_U:_IN_CQ], w[:, _IN_QM:_IN_G], w[:, _IN_CKV:_IN_KR], kr,
                                 w[:, _IN_CQ:_IN_CKV]], axis=1)
    uq = full["mla_w_uq"].reshape(MLA_Q_RANK, MLA_HEADS, MLA_QK)
    W["mla_w_uq"] = jnp.pad(uq, ((0, 0), (0, 0), (0, LANES - MLA_QK))).reshape(MLA_Q_RANK, HP)
    ukv = full["mla_w_ukv"].reshape(MLA_KV_RANK, MLA_HEADS, MLA_NOPE + MLA_V)
    padh = lambda a: jnp.pad(a, ((0, 0), (0, 0), (0, LANES - a.shape[2]))).reshape(MLA_KV_RANK, HP)
    W["mla_w_ukv"] = jnp.concatenate([padh(ukv[:, :, :MLA_NOPE]), padh(ukv[:, :, MLA_NOPE:])], axis=1)
    wb = full["w_branch_b"].reshape(MLA_HEADS, MLA_V, D_MODEL)
    W["w_branch_b"] = jnp.pad(wb, ((0, 0), (0, LANES - MLA_V), (0, 0))).reshape(HP, D_MODEL)
    return W


def _reference_layout(G):
    out = dict(G)
    if "w_in" not in G:
        return out
    g = G["w_in"]
    out["w_in"] = jnp.concatenate([
        g[:, Z_U:Z_QM], g[:, Z_CQ:Z_COLS], g[:, Z_CKV:Z_KR], g[:, Z_KR + KR_LANE:Z_KR + KR_LANE + MLA_ROPE],
        g[:, Z_QM:Z_CKV], g[:, Z_G:Z_U]], axis=1)
    out["mla_w_uq"] = G["mla_w_uq"].reshape(MLA_Q_RANK, MLA_HEADS, LANES)[:, :, :MLA_QK].reshape(MLA_Q_RANK, -1)
    gk = G["mla_w_ukv"][:, :HP].reshape(MLA_KV_RANK, MLA_HEADS, LANES)[:, :, :MLA_NOPE]
    gv = G["mla_w_ukv"][:, HP:].reshape(MLA_KV_RANK, MLA_HEADS, LANES)[:, :, :MLA_V]
    out["mla_w_ukv"] = jnp.concatenate([gk, gv], axis=2).reshape(MLA_KV_RANK, -1)
    out["w_branch_b"] = G["w_branch_b"].reshape(MLA_HEADS, LANES, D_MODEL)[:, :MLA_V].reshape(-1, D_MODEL)
    return out


def _pack(parts):
    flat = []
    for a in parts:
        a = a.reshape(-1)
        flat.append(jnp.pad(a, (0, (-a.shape[0]) % LANES)))
    return jnp.concatenate(flat).reshape(-1, LANES)


def _unpack(packed, shapes):
    flat = packed.reshape(-1)
    out, off = [], 0
    for shp in shapes:
        n = int(np.prod(shp))
        out.append(flat[off:off + n].reshape(shp))
        off += n + (-n) % LANES
    return out


MESH = pl.DeviceIdType.MESH
HBM = pl.BlockSpec(memory_space=pltpu.HBM)


def _all_gather(shards):
    n = len(shards)

    def body(*refs):
        x_refs, out_refs, token_ref = refs[:n], refs[n:2 * n], refs[2 * n]
        send_sems, recv_sems, local_sems = refs[2 * n + 1:]
        x, y, c = lax.axis_index("x"), lax.axis_index("y"), lax.axis_index("c")
        me, sibling = (x, y, c), (x, y, 1 - c)
        chips = [(1 - x, y), (x, 1 - y), (1 - x, 1 - y)]
        token_ref[...] = jnp.zeros_like(token_ref)

        def slot(a, px, py, pc):
            return out_refs[a].at[4 * px + 2 * py + pc]

        def copy(a, k, block, to, src=None):
            return pltpu.make_async_remote_copy(
                src_ref=slot(a, *block) if src is None else src, dst_ref=slot(a, *block),
                send_sem=send_sems.at[7 * a + k], recv_sem=recv_sems.at[7 * a + k], device_id=to, device_id_type=MESH)

        arrays = range(n)
        mine = [pltpu.make_async_copy(x_refs[a], slot(a, *me), local_sems.at[a]) for a in arrays]
        for cp in mine:
            cp.start()
        first = [copy(a, 0, me, sibling, src=x_refs[a]) for a in arrays]
        first += [copy(a, 1 + j, me, (*chip, c), src=x_refs[a]) for j, chip in enumerate(chips) for a in arrays]
        for cp in first:
            cp.start()
        passed = []
        for j, chip in enumerate(chips):
            for a in arrays:
                copy(a, 1 + j, (*chip, c), me).wait_recv()
                passed.append(copy(a, 4 + j, (*chip, c), sibling))
                passed[-1].start()
        for a in arrays:
            copy(a, 0, sibling, me).wait_recv()
        for j, chip in enumerate(chips):
            for a in arrays:
                copy(a, 4 + j, (*chip, 1 - c), me).wait_recv()
        for cp in first + passed:
            cp.wait_send()
        for cp in mine:
            cp.wait()

    res = pl.pallas_call(
        body, name="all_gather_weights",
        out_shape=[jax.ShapeDtypeStruct((N_DEV,) + s.shape, s.dtype) for s in shards]
        + [jax.ShapeDtypeStruct((8, LANES), F32)],
        in_specs=[HBM] * n, out_specs=[HBM] * n + [pl.BlockSpec(memory_space=pltpu.VMEM)],
        scratch_shapes=[pltpu.SemaphoreType.DMA((7 * n,)), pltpu.SemaphoreType.DMA((7 * n,)),
                        pltpu.SemaphoreType.DMA((n,))],
    )(*shards)
    return res[:n], res[n]


SEM = pl.BlockSpec(memory_space=pltpu.SEMAPHORE)
DATAFLOW = pltpu.SideEffectType.DATAFLOW_SIDE_EFFECTING


def _peers():
    x, y, c = lax.axis_index("x"), lax.axis_index("y"), lax.axis_index("c")
    out = []
    for k in range(1, N_DEV):
        px = 1 - x if k & 4 else x
        py = 1 - y if k & 2 else y
        pc = 1 - c if k & 1 else c
        out.append((k, (px, py, pc), 4 * px + 2 * py + pc))
    return 4 * x + 2 * y + c, out


def _send_start(srcs, per_peer, name):
    n = len(srcs)
    lands = [lax.empty((N_DEV,) + (s.shape[1:] if per_peer else s.shape), s.dtype) for s in srcs]

    def body(*refs):
        src_refs, land_refs, send_sems, recv_sems, token = refs[:n], refs[n:2 * n], refs[2 * n], refs[2 * n + 1], refs[-1]
        me, peers = _peers()
        for a in range(n):
            for k, pid, pflat in peers:
                pltpu.make_async_remote_copy(
                    src_ref=src_refs[a].at[pflat] if per_peer else src_refs[a], dst_ref=land_refs[a].at[me],
                    send_sem=send_sems.at[7 * a + k - 1], recv_sem=recv_sems.at[7 * a + k - 1],
                    device_id=pid, device_id_type=MESH).start()
        token[...] = jnp.zeros_like(token)

    hbm = lambda a: pltpu.with_memory_space_constraint(a, pltpu.HBM)
    res = pl.pallas_call(
        body, name=name,
        out_shape=(pltpu.SemaphoreType.DMA((7 * n,)), pltpu.SemaphoreType.DMA((7 * n,)),
                   *[pltpu.HBM(a.shape, a.dtype) for a in srcs + lands], jax.ShapeDtypeStruct((8, LANES), F32)),
        in_specs=(HBM,) * (2 * n), out_specs=(SEM, SEM) + (HBM,) * (2 * n) + (pl.BlockSpec(memory_space=pltpu.VMEM),),
        input_output_aliases={i: 2 + i for i in range(2 * n)},
        compiler_params=pltpu.CompilerParams(has_side_effects=DATAFLOW),
    )(*[hbm(a) for a in srcs + lands])
    return (res[0], res[1], list(res[2:2 + n]), list(res[2 + n:2 + 2 * n])), res[-1]


def _send_wait(started, after, per_peer, name):
    send_sems, recv_sems, srcs_thru, lands_thru = started
    n = len(srcs_thru)

    def body(*refs):
        src_refs, land_refs, send_sems, recv_sems = refs[:n], refs[n:2 * n], refs[2 * n], refs[2 * n + 1]
        me, peers = _peers()
        for a in range(n):
            for k, pid, pflat in peers:
                copy = pltpu.make_async_remote_copy(
                    src_ref=src_refs[a].at[pflat] if per_peer else src_refs[a], dst_ref=land_refs[a].at[pflat],
                    send_sem=send_sems.at[7 * a + k - 1], recv_sem=recv_sems.at[7 * a + k - 1],
                    device_id=pid, device_id_type=MESH)
                copy.wait_send()
                copy.wait_recv()

    outs = pl.pallas_call(
        body, name=name,
        out_shape=tuple(pltpu.HBM(a.shape, a.dtype) for a in srcs_thru + lands_thru),
        in_specs=(HBM,) * (2 * n) + (SEM, SEM, pl.BlockSpec(memory_space=pl.ANY)), out_specs=(HBM,) * (2 * n),
        input_output_aliases={i: i for i in range(2 * n)},
        compiler_params=pltpu.CompilerParams(has_side_effects=DATAFLOW),
    )(*srcs_thru, *lands_thru, send_sems, recv_sems, after)
    me = 4 * lax.axis_index("x") + 2 * lax.axis_index("y") + lax.axis_index("c")
    landed = []
    for src_out, land in zip(outs[:n], outs[n:]):
        own = lax.dynamic_index_in_dim(src_out, me, 0, keepdims=True) if per_peer else src_out[None]
        landed.append(lax.dynamic_update_slice(land, own, (me,) + (0,) * (land.ndim - 1)))
    return landed


def _share_rows(block, name):
    def body(src_ref, out_ref, send_sems, recv_sems, local_sem):
        me, peers = _peers()
        own = pltpu.make_async_copy(src_ref, out_ref.at[me], local_sem)
        own.start()
        copies = [pltpu.make_async_remote_copy(
            src_ref=src_ref, dst_ref=out_ref.at[me], send_sem=send_sems.at[k - 1], recv_sem=recv_sems.at[k - 1],
            device_id=pid, device_id_type=MESH) for k, pid, _ in peers]
        for cp in copies:
            cp.start()
        for cp in copies:
            cp.wait()
        own.wait()

    return pl.pallas_call(
        body, name=name, out_shape=jax.ShapeDtypeStruct((N_DEV,) + block.shape, block.dtype),
        in_specs=[HBM], out_specs=HBM,
        scratch_shapes=[pltpu.SemaphoreType.DMA((N_DEV - 1,)), pltpu.SemaphoreType.DMA((N_DEV - 1,)),
                        pltpu.SemaphoreType.DMA],
    )(block)


def _sum_slots(recv, name, tr):
    n, rows, lanes = recv.shape
    tr = _tile(rows, tr)

    def body(r_ref, o_ref):
        acc = r_ref[0].astype(F32)
        for i in range(1, n):
            acc = acc + r_ref[i].astype(F32)
        o_ref[...] = acc

    return pl.pallas_call(
        body, name=name, grid=(rows // tr,),
        in_specs=[pl.BlockSpec((n, tr, lanes), lambda i: (0, i, 0))],
        out_specs=pl.BlockSpec((tr, lanes), lambda i: (i, 0)),
        out_shape=jax.ShapeDtypeStruct((rows, lanes), F32),
        compiler_params=_cparams(("parallel",)),
    )(recv)


def _adamw_math(w, g, m, v):
    m = ADAM_B1 * m + (1.0 - ADAM_B1) * g
    v = ADAM_B2 * v + (1.0 - ADAM_B2) * (g * g)
    m_hat = m / (1.0 - ADAM_B1 ** ADAM_STEP)
    v_hat = v / (1.0 - ADAM_B2 ** ADAM_STEP)
    return -ADAM_LR * (m_hat / (jnp.sqrt(v_hat) + ADAM_EPS) + ADAM_WD * w), m, v


def _adamw(w, g, m, v, name, tr=256):
    return _rowwise(_adamw_math, name, tr, [w, g, m, v], [], [(w.shape[1], F32)] * 3)


def _adamw_small(ws, gs, ms, vs):
    n = len(ws)

    def body(*refs):
        ins, outs = refs[:4 * n], refs[4 * n:]
        for i in range(n):
            d, m, v = _adamw_math(ins[i][...], ins[n + i][...], ins[2 * n + i][...], ins[3 * n + i][...])
            outs[i][...], outs[n + i][...], outs[2 * n + i][...] = d, m, v

    vmem = pl.BlockSpec(memory_space=pltpu.VMEM)
    res = pl.pallas_call(
        body, name="adamw_small", in_specs=[vmem] * (4 * n), out_specs=[vmem] * (3 * n),
        out_shape=[jax.ShapeDtypeStruct(w.shape, F32) for w in ws] * 3,
    )(*ws, *gs, *ms, *vs)
    return res[:n], res[n:2 * n], res[2 * n:]


def _sum_adamw(recv, w, m, v, name):
    n, r, c = recv.shape
    tr = _tile(r, 256)

    def body(r_ref, w_ref, m_ref, v_ref, g_ref, d_ref, nm_ref, nv_ref):
        g = r_ref[0].astype(F32)
        for i in range(1, n):
            g = g + r_ref[i].astype(F32)
        g_ref[...] = g
        d_ref[...], nm_ref[...], nv_ref[...] = _adamw_math(w_ref[...], g, m_ref[...], v_ref[...])

    row = pl.BlockSpec((None, tr, c), lambda i: (0, i, 0))
    return pl.pallas_call(
        body, name=name, grid=(r // tr,),
        in_specs=[pl.BlockSpec((n, tr, c), lambda i: (0, i, 0)), row, row, row], out_specs=[row] * 4,
        out_shape=[jax.ShapeDtypeStruct((1, r, c), F32)] * 4, compiler_params=_cparams(("parallel",)),
    )(recv, w, m, v)


def kernel(x, mem, positions, ffn1_norm, ffn1_w_gu, ffn1_w_down, mix_norm, w_in, b_gate, sg_ln_g, sg_ln_b, sg_w, sg_b, mla_cq_norm, mla_w_uq, mla_ckv_norm, mla_w_ukv, mla_q_norm, mla_k_norm, mem_norm, mem_w_kv, mem_q_norm, mem_k_norm, w_branch_a, w_branch_b, w_branch_c, w_out, ffn2_norm, ffn2_w_gu, ffn2_w_down, loss_target, m_ffn1_norm, m_ffn1_w_gu, m_ffn1_w_down, m_mix_norm, m_w_in, m_b_gate, m_sg_ln_g, m_sg_ln_b, m_sg_w, m_sg_b, m_mla_cq_norm, m_mla_w_uq, m_mla_ckv_norm, m_mla_w_ukv, m_mla_q_norm, m_mla_k_norm, m_mem_norm, m_mem_w_kv, m_mem_q_norm, m_mem_k_norm, m_w_branch_a, m_w_branch_b, m_w_branch_c, m_w_out, m_ffn2_norm, m_ffn2_w_gu, m_ffn2_w_down, v_ffn1_norm, v_ffn1_w_gu, v_ffn1_w_down, v_mix_norm, v_w_in, v_b_gate, v_sg_ln_g, v_sg_ln_b, v_sg_w, v_sg_b, v_mla_cq_norm, v_mla_w_uq, v_mla_ckv_norm, v_mla_w_ukv, v_mla_q_norm, v_mla_k_norm, v_mem_norm, v_mem_w_kv, v_mem_q_norm, v_mem_k_norm, v_w_branch_a, v_w_branch_b, v_w_branch_c, v_w_out, v_ffn2_norm, v_ffn2_w_gu, v_ffn2_w_down):
    given = dict(locals())
    wts = {n: given[n] for n in ORDER}
    mom = {n: given["m_" + n] for n in ORDER}
    var = {n: given["v_" + n] for n in ORDER}

    def shards(group, zero):
        out = [wts[n][0].astype(BF16) for n in GROUPS[group]]
        return [out[0] + zero.astype(BF16)] + out[1:]

    def full_weights(group, slabs):
        return _compute_layout({n: _full_from_slabs(n, s) for n, s in zip(GROUPS[group], slabs)})

    def zero_of(a):
        return jnp.minimum(jnp.abs(a.reshape(-1)[0]), 0)

    gathered_ffn1, token = _all_gather([wts[n][0].astype(BF16) for n in GROUPS["ffn1"]])
    flight = {}
    flight["ffn1_down"], token = _send_start(shards("ffn1_down", token[0, 0]), False, "gather_ffn1_down_start")
    flight["mix"] = _send_start(shards("mix", token[0, 0]), False, "gather_mix_start")[0]
    recv = {}

    def weights(group, after):
        if group == "ffn1":
            return full_weights(group, gathered_ffn1)
        landed = _send_wait(flight.pop(group), after, False, f"gather_{group}_wait")
        if group == "mix":
            flight["ffn2"] = _send_start(shards("ffn2", zero_of(landed[0])), False, "gather_ffn2_start")[0]
        return full_weights(group, landed)

    small_shapes = [wts[n].shape[1:] for n in SMALL]
    early = SMALL[1:]
    assert SMALL[0] == "ffn1_norm"

    def grads_out(group, G):
        Gr = _reference_layout({n: G[n] for n in GRAD_GROUPS[group]})
        parts = [_slabs_from_full(n, Gr[n]).astype(BF16) for n in GRAD_GROUPS[group]]
        flight["g_" + group], tie = _send_start(parts, True, f"grads_{group}_start")
        if group == "mix":
            small = _pack([G[n].reshape(s) for n, s in zip(early, small_shapes[1:])])
            small = jnp.pad(small, ((0, (-small.shape[0]) % 8), (0, 0)))
            flight["small"], tie = _send_start([small + tie[0, 0]], False, "grads_small_start")
        return tie

    P = {n: wts[n] if wts[n].ndim == 2 else wts[n][0] for n in SMALL}
    loss_part, grad_x, G = _local_step(x[0], mem[0], positions[0], loss_target[0], P, weights, grads_out)

    for group, names in GRAD_GROUPS.items():
        recv.update(zip(names, _send_wait(flight.pop("g_" + group), grad_x, True, f"grads_{group}_wait")))
    early_recv, = _send_wait(flight.pop("small"), grad_x, False, "grads_small_wait")
    last = _share_rows(G["ffn1_norm"].reshape(-1, LANES), "share_ffn1_norm")
    g_small_packed = _sum_slots(jnp.concatenate([last, early_recv], axis=1), "sum_small", 2048)

    grads, delta, new_m, new_v = {}, {}, {}, {}
    for n in SHARDED:
        grads[n], delta[n], new_m[n], new_v[n] = _sum_adamw(recv[n], wts[n], mom[n], var[n], "adamw_" + n)
    grads.update(zip(SMALL, _unpack(g_small_packed, small_shapes)))

    flat2 = lambda d: [d[n].reshape(-1, d[n].shape[-1]) for n in SMALL]
    for dst, vals in zip((delta, new_m, new_v), _adamw_small(flat2(wts), flat2(grads), flat2(mom), flat2(var))):
        dst.update(zip(SMALL, vals))

    loss = lax.psum(jnp.sum(loss_part), ("x", "y", "c"))
    lead = lambda d: [d[n].reshape(wts[n].shape) for n in ORDER]
    return (loss, grad_x[None], *lead(grads), *lead(delta), *lead(new_m), *lead(new_v))
```

```python
import functools

import numpy as np
import jax
import jax.numpy as jnp
from jax import lax
from jax.experimental import pallas as pl
from jax.experimental.pallas import tpu as pltpu

F32, BF16 = jnp.float32, jnp.bfloat16

D_MODEL = 1024
SG_GROUPS, SG_GROUP_DIM, SG_WIDTH, CHUNK = 8, 64, 512, 128
MLA_HEADS, MLA_NOPE, MLA_ROPE, MLA_V, MLA_QK = 8, 64, 32, 64, 96
MLA_Q_RANK, MLA_KV_RANK = 384, 256
MEM_HEADS, MEM_HEAD_DIM, MEM_WIDTH = 4, 128, 512
D_FF = 2816
ROPE_BASE = 10000.0
EPS = 1e-6
NEG = -1e30
ADAM_LR, ADAM_B1, ADAM_B2, ADAM_EPS, ADAM_WD, ADAM_STEP = 0.001, 0.9, 0.999, 1e-08, 0.01, 10

N_DEV = 8
LANES = 128
V7X_VMEM_LIMIT = 56 * 1024 * 1024
HP = MLA_HEADS * LANES

Z_G, Z_U, Z_V, Z_QM, Z_CKV, Z_KR, Z_CQ = 0, 3072, 3584, 4096, 4608, 4864, 4992
Z_COLS = 5376
KR_LANE = 64


def _tile(dim, pref):
    if dim <= pref:
        return dim
    for t in range(pref - pref % LANES, LANES - 1, -LANES):
        if dim % t == 0:
            return t
    for t in range(pref - pref % 8, 7, -8):
        if dim % t == 0:
            return t
    return dim


def _cparams(sem):
    return pltpu.CompilerParams(dimension_semantics=sem, vmem_limit_bytes=V7X_VMEM_LIMIT)


_DN = {"nn": ((1,), (0,)), "nt": ((1,), (1,)), "tn": ((0,), (0,))}


def _dot(a, b, mode="nn"):
    return lax.dot_general(a.astype(BF16), b.astype(BF16), (_DN[mode], ((), ())),
                           preferred_element_type=F32)


def _mm(a, b, mode, out_dtype, name, tm=512, tn=512, tk=2048, tie=None):
    if mode == "tn":
        K, M = a.shape
    else:
        M, K = a.shape
    N = b.shape[0] if mode == "nt" else b.shape[1]
    tm, tn, tk = _tile(M, tm), _tile(N, tn), _tile(K, tk)
    nk = K // tk
    if mode == "tn":
        a_spec = pl.BlockSpec((tk, tm), lambda i, j, k: (k, i))
    else:
        a_spec = pl.BlockSpec((tm, tk), lambda i, j, k: (i, k))
    if mode == "nt":
        b_spec = pl.BlockSpec((tn, tk), lambda i, j, k: (j, k))
    else:
        b_spec = pl.BlockSpec((tk, tn), lambda i, j, k: (k, j))

    ties = [] if tie is None else [tie]

    def body(a_ref, b_ref, *rest):
        o_ref, *scratch = rest[len(ties):]
        p = _dot(a_ref[...], b_ref[...], mode)
        if nk == 1:
            o_ref[...] = p.astype(o_ref.dtype)
        else:
            acc_ref, = scratch
            k = pl.program_id(2)

            @pl.when(k == 0)
            def _():
                acc_ref[...] = p

            @pl.when(k > 0)
            def _():
                acc_ref[...] += p

            @pl.when(k == nk - 1)
            def _():
                o_ref[...] = acc_ref[...].astype(o_ref.dtype)

    return pl.pallas_call(
        body, name=name, grid=(M // tm, N // tn, nk),
        in_specs=[a_spec, b_spec] + [pl.BlockSpec(t.shape, lambda i, j, k: (0, 0)) for t in ties],
        out_specs=pl.BlockSpec((tm, tn), lambda i, j, k: (i, j)),
        out_shape=jax.ShapeDtypeStruct((M, N), out_dtype),
        scratch_shapes=[] if nk == 1 else [pltpu.VMEM((tm, tn), F32)],
        compiler_params=_cparams(("parallel", "parallel", "arbitrary")),
    )(a, b, *ties)


def _mm_t(at, b, name, tm, tn, tk=1024, tie=None):
    return _mm(at, b, "nn", BF16, name, tm=tm, tn=tn, tk=tk, tie=tie)


def _rowwise(fn, name, tr, row_ins, bc_ins, row_outs, acc_outs=()):
    norm = [it if isinstance(it, tuple) else (it, it.shape[1], 0) for it in row_ins]
    rows = norm[0][0].shape[0]
    tr = _tile(rows, tr)
    arrays, in_specs = [], []
    for arr, w, cb in norm:
        arrays.append(arr)
        in_specs.append(pl.BlockSpec((tr, w), lambda i, cb=cb: (i, cb)))
    for arr in bc_ins:
        arrays.append(arr)
        in_specs.append(pl.BlockSpec(arr.shape, lambda i, nd=arr.ndim: (0,) * nd))
    n_in, n_row = len(arrays), len(row_outs)
    out_shape, out_specs, aliases = [], [], {}
    transposed = [len(o) == 3 for o in row_outs]
    for k, o in enumerate(row_outs):
        if o[0] == "into":
            _, target, w, cb = o
            aliases[len(arrays)] = k
            arrays.append(target)
            in_specs.append(pl.BlockSpec(memory_space=pl.ANY))
            out_shape.append(jax.ShapeDtypeStruct(target.shape, target.dtype))
            out_specs.append(pl.BlockSpec((tr, w), lambda i, cb=cb: (i, cb)))
        elif transposed[k]:
            out_shape.append(jax.ShapeDtypeStruct((o[0], rows), o[1]))
            out_specs.append(pl.BlockSpec((o[0], tr), lambda i: (0, i)))
        else:
            out_shape.append(jax.ShapeDtypeStruct((rows, o[0]), o[1]))
            out_specs.append(pl.BlockSpec((tr, o[0]), lambda i: (i, 0)))
    for shp, dt in acc_outs:
        out_shape.append(jax.ShapeDtypeStruct(shp, dt))
        out_specs.append(pl.BlockSpec(shp, lambda i, nd=len(shp): (0,) * nd))

    def body(*refs):
        vals = fn(*[r[...].astype(F32) for r in refs[:n_in]])
        if not isinstance(vals, (tuple, list)):
            vals = (vals,)
        outs = refs[len(arrays):]
        for r, v, t in zip(outs[:n_row], vals[:n_row], transposed):
            r[...] = v.astype(F32).T.astype(r.dtype) if t else v.astype(r.dtype)
        if acc_outs:
            accs = list(zip(outs[n_row:], vals[n_row:]))
            i = pl.program_id(0)

            @pl.when(i == 0)
            def _():
                for r, v in accs:
                    r[...] = v.astype(r.dtype)

            @pl.when(i > 0)
            def _():
                for r, v in accs:
                    r[...] += v.astype(r.dtype)

    res = pl.pallas_call(
        body, name=name, grid=(rows // tr,), in_specs=in_specs, out_specs=out_specs,
        out_shape=out_shape, input_output_aliases=aliases, compiler_params=_cparams(("arbitrary",)),
    )(*arrays)
    return res


def _rsum(x):
    return jnp.sum(x, axis=0, keepdims=True)


def _rms(x, g, n=None):
    n = x.shape[-1] if n is None else n
    r = lax.rsqrt(jnp.sum(x * x, axis=-1, keepdims=True) * (1.0 / n) + EPS)
    return x * r * g


def _rms_bwd(x, g, dy, n=None):
    n = x.shape[-1] if n is None else n
    r = lax.rsqrt(jnp.sum(x * x, axis=-1, keepdims=True) * (1.0 / n) + EPS)
    xh = x * r
    dxh = dy * g
    dx = r * (dxh - xh * (jnp.sum(dxh * xh, axis=-1, keepdims=True) * (1.0 / n)))
    return dx, _rsum(dy * xh)


def _gelu(x):
    return 0.5 * x * (1.0 + lax.erf(x * 0.7071067811865476))


def _gelu_grad(x):
    return 0.5 * (1.0 + lax.erf(x * 0.7071067811865476)) + x * jnp.exp(-0.5 * x * x) * 0.3989422804014327


def _sigmoid(x):
    return 0.5 * jnp.tanh(0.5 * x) + 0.5


FFN_TM, FFN_TN = 512, 1408
MXU_WIDTH = 256


def _col_chunks(n):
    return [(c, min(c + MXU_WIDTH, n)) for c in range(0, n, MXU_WIDTH)]


def _ffn_gu_act(h, w_gu, tag):
    T = h.shape[0]
    tm, tn = _tile(T, FFN_TM), FFN_TN
    nj = D_FF // tn

    def body(h_ref, wg_ref, wu_ref, gu_ref, a_ref, at_ref):
        h = h_ref[...]
        for c0, c1 in _col_chunks(tn):
            g = _dot(h, wg_ref[:, c0:c1])
            u = _dot(h, wu_ref[:, c0:c1])
            gu_ref[0, :, c0:c1] = g.astype(BF16)
            gu_ref[1, :, c0:c1] = u.astype(BF16)
            a = g * _sigmoid(g) * u
            a_ref[:, c0:c1] = a.astype(BF16)
            at_ref[c0:c1, :] = a.T.astype(BF16)

    return pl.pallas_call(
        body, name=f"{tag}_gu_act", grid=(T // tm, nj),
        in_specs=[pl.BlockSpec((tm, D_MODEL), lambda i, j: (i, 0)),
                  pl.BlockSpec((D_MODEL, tn), lambda i, j: (0, j)),
                  pl.BlockSpec((D_MODEL, tn), lambda i, j: (0, j + nj))],
        out_specs=[pl.BlockSpec((2, tm, tn), lambda i, j: (0, i, j)),
                   pl.BlockSpec((tm, tn), lambda i, j: (i, j)),
                   pl.BlockSpec((tn, tm), lambda i, j: (j, i))],
        out_shape=[jax.ShapeDtypeStruct((2, T, D_FF), BF16), jax.ShapeDtypeStruct((T, D_FF), BF16),
                   jax.ShapeDtypeStruct((D_FF, T), BF16)],
        compiler_params=_cparams(("parallel", "parallel")),
    )(h, w_gu, w_gu)


def _ffn_da_actbwd(do, w_down, gu, tag, tie=None):
    T = do.shape[0]
    tm, tn = _tile(T, FFN_TM), FFN_TN
    ties = [] if tie is None else [tie]

    def body(do_ref, wd_ref, gu_ref, *rest):
        dgu_ref = rest[-1]
        do = do_ref[...]
        for c0, c1 in _col_chunks(tn):
            da = _dot(do, wd_ref[c0:c1, :], "nt")
            g = gu_ref[0, :, c0:c1].astype(F32)
            u = gu_ref[1, :, c0:c1].astype(F32)
            s = _sigmoid(g)
            dgu_ref[0, :, c0:c1] = (da * u * s * (1.0 + g * (1.0 - s))).astype(BF16)
            dgu_ref[1, :, c0:c1] = (da * g * s).astype(BF16)

    return pl.pallas_call(
        body, name=f"{tag}_da_actbwd", grid=(T // tm, D_FF // tn),
        in_specs=[pl.BlockSpec((tm, D_MODEL), lambda i, j: (i, 0)),
                  pl.BlockSpec((tn, D_MODEL), lambda i, j: (j, 0)),
                  pl.BlockSpec((2, tm, tn), lambda i, j: (0, i, j))]
        + [pl.BlockSpec(t.shape, lambda i, j: (0, 0)) for t in ties],
        out_specs=pl.BlockSpec((2, tm, tn), lambda i, j: (0, i, j)),
        out_shape=jax.ShapeDtypeStruct((2, T, D_FF), BF16),
        compiler_params=_cparams(("parallel", "parallel")),
    )(do, w_down, gu, *ties)


def _ffn_dwgu(ht, dgu, tag, tk=2048):
    T = ht.shape[1]
    tn, tk = FFN_TN, _tile(T, tk)
    nj, nk = D_FF // tn, T // tk

    def body(a_ref, b_ref, o_ref, acc_ref):
        k = pl.program_id(1)
        p = _dot(a_ref[...], b_ref[...])

        @pl.when(k == 0)
        def _():
            acc_ref[...] = p

        @pl.when(k > 0)
        def _():
            acc_ref[...] += p

        @pl.when(k == nk - 1)
        def _():
            o_ref[...] = acc_ref[...].astype(o_ref.dtype)

    return pl.pallas_call(
        body, name=f"{tag}_dwgu", grid=(2 * nj, nk),
        in_specs=[pl.BlockSpec((D_MODEL, tk), lambda n, k: (0, k)),
                  pl.BlockSpec((None, tk, tn), lambda n, k: (n // nj, k, n % nj))],
        out_specs=pl.BlockSpec((D_MODEL, tn), lambda n, k: (0, n)),
        out_shape=jax.ShapeDtypeStruct((D_MODEL, 2 * D_FF), BF16),
        scratch_shapes=[pltpu.VMEM((D_MODEL, tn), F32)],
        compiler_params=_cparams(("parallel", "arbitrary")),
    )(ht, dgu)


def _ffn_dh(dgu, w_gu, tag, tm=2048, tie=None):
    T = dgu.shape[1]
    tm, tk = _tile(T, tm), FFN_TN
    nk = D_FF // tk
    ties = [] if tie is None else [tie]

    def body(a_ref, b_ref, *rest):
        o_ref, acc_ref = rest[len(ties):]
        k = pl.program_id(1)
        p = _dot(a_ref[...], b_ref[...], "nt")

        @pl.when(k == 0)
        def _():
            acc_ref[...] = p

        @pl.when(k > 0)
        def _():
            acc_ref[...] += p

        @pl.when(k == 2 * nk - 1)
        def _():
            o_ref[...] = acc_ref[...].astype(o_ref.dtype)

    return pl.pallas_call(
        body, name=f"{tag}_dh", grid=(T // tm, 2 * nk),
        in_specs=[pl.BlockSpec((None, tm, tk), lambda i, k: (k // nk, i, k % nk)),
                  pl.BlockSpec((D_MODEL, tk), lambda i, k: (0, k))]
        + [pl.BlockSpec(t.shape, lambda i, k: (0, 0)) for t in ties],
        out_specs=pl.BlockSpec((tm, D_MODEL), lambda i, k: (i, 0)),
        out_shape=jax.ShapeDtypeStruct((T, D_MODEL), BF16),
        scratch_shapes=[pltpu.VMEM((tm, D_MODEL), F32)],
        compiler_params=_cparams(("parallel", "arbitrary")),
    )(dgu, w_gu, *ties)


def _ffn_fwd(h, w_gu, w_down, tag):
    gu, a, at = _ffn_gu_act(h, w_gu, tag)
    if callable(w_down):
        w_down = w_down(at)
    o = _mm(a, w_down, "nn", BF16, f"{tag}_down", tm=1024, tn=1024, tk=2816)
    return gu, at, o


def _ffn_bwd(do, ht, gu, at, w_gu, w_down, tag, tie=None, on_dw=None):
    on_dw = on_dw or (lambda which, dw: None)
    dw_down = _mm_t(at, do, f"{tag}_dwdown", tm=1408, tn=1024, tk=2048, tie=tie)
    dgu = _ffn_da_actbwd(do, w_down, gu, tag, tie=on_dw("down", dw_down))
    dw_gu = _ffn_dwgu(ht, dgu, tag)
    dh = _ffn_dh(dgu, w_gu, tag, tie=on_dw("gu", dw_gu))
    return dh, dw_gu, dw_down


def _sg_common(u_pre, v_pre, ln_g, ln_b):
    u = _gelu(u_pre)
    v = _gelu(v_pre)
    mu = jnp.mean(v, axis=-1, keepdims=True)
    vc = v - mu
    rstd = lax.rsqrt(jnp.mean(vc * vc, axis=-1, keepdims=True) + EPS)
    vhat = vc * rstd
    vl = vhat * ln_g + ln_b
    return u, vhat, rstd, vl


def _sg_masked_pairs(w):
    t = lax.broadcasted_iota(jnp.int32, (CHUNK, CHUNK), 0)
    s = lax.broadcasted_iota(jnp.int32, (CHUNK, CHUNK), 1)
    causal = s <= t
    wm = [jnp.where(causal, w[g], 0.0).astype(BF16) for g in range(SG_GROUPS)]
    return [jnp.concatenate([wm[2 * j], wm[2 * j + 1]], axis=0) for j in range(SG_GROUPS // 2)], causal


def _sg_mix(vl, pairs, bias):
    tr = vl.shape[0]
    low = lax.broadcasted_iota(jnp.int32, (CHUNK, LANES), 1) < SG_GROUP_DIM
    vb = vl.astype(BF16)
    rows = []
    for c in range(tr // CHUNK):
        slabs = []
        for j in range(SG_GROUPS // 2):
            slab = vb[c * CHUNK:(c + 1) * CHUNK, j * LANES:(j + 1) * LANES]
            m = _dot(pairs[j], slab)
            slabs.append(jnp.where(low, m[:CHUNK], m[CHUNK:]))
        rows.append(jnp.concatenate(slabs, axis=1) + bias)
    return jnp.concatenate(rows, axis=0)


def _sg_fwd(z, ln_g, ln_b, sg_w, bias_full):
    def fn(u_pre, v_pre, ln_g, ln_b, w, bias):
        u, _, _, vl = _sg_common(u_pre, v_pre, ln_g, ln_b)
        pairs, _ = _sg_masked_pairs(w)
        y = u * _sg_mix(vl, pairs, bias)
        return y, y

    return _rowwise(fn, "sg_fwd", 512, [(z, SG_WIDTH, Z_U // SG_WIDTH), (z, SG_WIDTH, Z_V // SG_WIDTH)],
                    [ln_g, ln_b, sg_w, bias_full], [(SG_WIDTH, BF16), (SG_WIDTH, BF16, "T")])


def _sg_bwd(z, dy, ln_g, ln_b, sg_w, bias_full, group_ind, dz):
    def fn(u_pre, v_pre, dy, ln_g, ln_b, w, bias, ind):
        dy = dy.astype(F32)
        u, vhat, rstd, vl = _sg_common(u_pre, v_pre, ln_g, ln_b)
        pairs, causal = _sg_masked_pairs(w)
        mixed = _sg_mix(vl, pairs, bias)
        du_pre = dy * mixed * _gelu_grad(u_pre)
        dmix = dy * u
        tr = dy.shape[0]
        low = lax.broadcasted_iota(jnp.int32, (CHUNK, LANES), 1) < SG_GROUP_DIM
        vb = vl.astype(BF16)
        dw = [jnp.zeros((CHUNK, CHUNK), F32) for _ in range(SG_GROUPS)]
        dbias = jnp.zeros((CHUNK, SG_WIDTH), F32)
        dvl_rows = []
        for c in range(tr // CHUNK):
            dm_c = dmix[c * CHUNK:(c + 1) * CHUNK]
            dbias = dbias + dm_c
            slabs = []
            for j in range(SG_GROUPS // 2):
                slab = vb[c * CHUNK:(c + 1) * CHUNK, j * LANES:(j + 1) * LANES]
                dm = dm_c[:, j * LANES:(j + 1) * LANES]
                d0 = jnp.where(low, dm, 0.0).astype(BF16)
                d1 = jnp.where(low, 0.0, dm).astype(BF16)
                dw[2 * j] = dw[2 * j] + _dot(d0, slab, "nt")
                dw[2 * j + 1] = dw[2 * j + 1] + _dot(d1, slab, "nt")
                slabs.append(_dot(pairs[j], jnp.concatenate([d0, d1], axis=0), "tn"))
            dvl_rows.append(jnp.concatenate(slabs, axis=1))
        dvl = jnp.concatenate(dvl_rows, axis=0)
        dln_g = _rsum(dvl * vhat)
        dln_b = _rsum(dvl)
        dvh = dvl * ln_g
        dv = rstd * (dvh - jnp.mean(dvh, axis=-1, keepdims=True)
                     - vhat * jnp.mean(dvh * vhat, axis=-1, keepdims=True))
        dv_pre = dv * _gelu_grad(v_pre)
        dw = jnp.stack([jnp.where(causal, d, 0.0) for d in dw], axis=0)
        dbias_t = lax.dot_general(dbias, ind, (((1,), (0,)), ((), ())), precision=lax.Precision.HIGHEST,
                                  preferred_element_type=F32)
        return jnp.concatenate([du_pre, dv_pre], axis=1), dw, dbias_t, dln_g, dln_b

    return _rowwise(fn, "sg_bwd", 512,
                    [(z, SG_WIDTH, Z_U // SG_WIDTH), (z, SG_WIDTH, Z_V // SG_WIDTH), dy],
                    [ln_g, ln_b, sg_w, bias_full, group_ind],
                    [("into", dz, 2 * SG_WIDTH, Z_U // (2 * SG_WIDTH))],
                    [((SG_GROUPS, CHUNK, CHUNK), F32), ((CHUNK, SG_GROUPS), F32), ((1, SG_WIDTH), F32), ((1, SG_WIDTH), F32)])


def _rope(x, c, s1, s2):
    return x * c + pltpu.roll(x, LANES - MLA_ROPE // 2, 1) * s1 + pltpu.roll(x, MLA_ROPE // 2, 1) * s2


def _rope_t(d, c, s1, s2):
    return d * c + pltpu.roll(d * s1, MLA_ROPE // 2, 1) + pltpu.roll(d * s2, LANES - MLA_ROPE // 2, 1)


def _mla_post(q_pre, kv_pre, z, tabs, gq, gk):
    scale = MLA_QK ** -0.5 * LOG2E

    def fn(q_pre, k_pre, v_pre, kr, c, s1, s2, gq, gk):
        qs, ks = [], []
        for h in range(MLA_HEADS):
            sl = slice(h * LANES, (h + 1) * LANES)
            qs.append(_rope(_rms(q_pre[:, sl], gq, MLA_QK), c, s1, s2) * scale)
            ks.append(_rope(_rms(k_pre[:, sl] + kr, gk, MLA_QK), c, s1, s2))
        lane = lax.broadcasted_iota(jnp.int32, v_pre.shape, 1) & (LANES - 1)
        return jnp.concatenate(qs, axis=1), jnp.concatenate(ks, axis=1), jnp.where(lane == ONES_LANE, 1.0, v_pre)

    return _rowwise(fn, "mla_post", 256,
                    [q_pre, (kv_pre, HP, 0), (kv_pre, HP, 1), (z, LANES, Z_KR // LANES), *tabs],
                    [gq, gk], [(HP, BF16)] * 3)


def _mla_post_bwd(q_pre, kv_pre, z, tabs, gq, gk, dq, dk, dv):
    scale = MLA_QK ** -0.5

    def fn(q_pre, k_pre, kr, c, s1, s2, dq, dk, dv, gq, gk):
        lane = lax.broadcasted_iota(jnp.int32, (1, LANES), 1)
        kr_mask = (lane >= KR_LANE) & (lane < KR_LANE + MLA_ROPE)
        dqs, dks = [], []
        dgq = jnp.zeros((1, LANES), F32)
        dgk = jnp.zeros((1, LANES), F32)
        dkr = jnp.zeros(kr.shape, F32)
        for h in range(MLA_HEADS):
            sl = slice(h * LANES, (h + 1) * LANES)
            dqn = _rope_t(dq[:, sl].astype(F32), c, s1, s2) * scale
            dx, dg = _rms_bwd(q_pre[:, sl], gq, dqn, MLA_QK)
            dqs.append(dx)
            dgq = dgq + dg
            dkn = _rope_t(dk[:, sl].astype(F32), c, s1, s2)
            dx, dg = _rms_bwd(k_pre[:, sl] + kr, gk, dkn, MLA_QK)
            dks.append(dx)
            dgk = dgk + dg
            dkr = dkr + dx
        dkr = jnp.where(kr_mask, dkr, 0.0)
        dkv = jnp.concatenate(dks + [dv.astype(F32)], axis=1)
        return jnp.concatenate(dqs, axis=1), dkv, dkr, dgq, dgk

    return _rowwise(fn, "mla_post_bwd", 256,
                    [q_pre, (kv_pre, HP, 0), (z, LANES, Z_KR // LANES), *tabs, dq, dk, dv],
                    [gq, gk], [(HP, BF16), (2 * HP, BF16), (LANES, BF16)],
                    [((1, LANES), F32), ((1, LANES), F32)])


def _pairs(n, lower):
    a, b = [], []
    for o in range(n):
        inner = range(o + 1) if lower else range(o, n)
        for t in inner:
            a.append(o)
            b.append(t)
    return jnp.asarray(np.array(a, np.int32)), jnp.asarray(np.array(b, np.int32))


FLASH_TILE, FLASH_SUB_ROWS = 2048, 512
LOG2E, LN2 = 1.4426950408889634, 0.6931471805599453
ONES_LANE = MLA_V


def _flash_tiles(T):
    tq = _tile(T, FLASH_TILE)
    return tq, _tile(tq, FLASH_SUB_ROWS)


def _col_span(t, sr, rb, diag, key_major):
    if not diag:
        return 0, t
    return (rb * sr, t) if key_major else (0, (rb + 1) * sr)


def _span_iota(sr, rb, c0, c1):
    r = lax.broadcasted_iota(jnp.int32, (sr, c1 - c0), 0) + rb * sr
    c = lax.broadcasted_iota(jnp.int32, (sr, c1 - c0), 1) + c0
    return r, c


def _lanes(x, width):
    return jnp.concatenate([x] * (width // LANES), axis=1)


def _flash_fwd(q, k, v):
    T = q.shape[0]
    tq, sr = _flash_tiles(T)
    n = T // tq
    ii, jj = _pairs(n, True)

    def body(ii_ref, jj_ref, q_ref, k_ref, v_ref, o_ref, ot_ref, lse_ref, lset_ref, m_sc, acc_sc):
        p_ = pl.program_id(1)
        i, j = ii_ref[p_], jj_ref[p_]

        @pl.when(j == 0)
        def _():
            m_sc[...] = jnp.full(m_sc.shape, NEG, F32)
            acc_sc[...] = jnp.zeros(acc_sc.shape, F32)

        def tile(diag):
            for rb in range(tq // sr):
                rows = slice(rb * sr, (rb + 1) * sr)
                c0, c1 = _col_span(tq, sr, rb, diag, False)
                s = _dot(q_ref[rows, :], k_ref[c0:c1, :], "nt")
                if diag:
                    r, c = _span_iota(sr, rb, c0, c1)
                    s = jnp.where(c <= r, s, NEG)
                m = m_sc[rows, :]
                m_new = jnp.maximum(m, jnp.max(s, axis=1, keepdims=True))
                p = jnp.exp2(s - _lanes(m_new, c1 - c0))
                acc_sc[rows, :] = jnp.exp2(m - m_new) * acc_sc[rows, :] + _dot(p, v_ref[c0:c1, :])
                m_sc[rows, :] = m_new

        @pl.when(j < i)
        def _():
            tile(False)

        @pl.when(j == i)
        def _():
            tile(True)
            acc = acc_sc[...]
            lane = lax.broadcasted_iota(jnp.int32, acc.shape, 1)
            l = jnp.sum(jnp.where(lane == ONES_LANE, acc, 0.0), axis=1, keepdims=True)
            o = jnp.where(lane < MLA_V, acc / l, 0.0)
            o_ref[...] = o.astype(o_ref.dtype)
            ot_ref[...] = o.T.astype(ot_ref.dtype)
            lse = m_sc[...] + jnp.log2(l)
            lse_ref[...] = lse
            lset_ref[...] = lse.T[:8]

    blk = lambda which: pl.BlockSpec((tq, LANES), which)
    qmap = lambda h, p, ii, jj: (ii[p], h)
    kmap = lambda h, p, ii, jj: (jj[p], h)
    tmap = lambda h, p, ii, jj: (h, ii[p])
    return pl.pallas_call(
        body, name="mla_flash_fwd",
        grid_spec=pltpu.PrefetchScalarGridSpec(
            num_scalar_prefetch=2, grid=(MLA_HEADS, int(ii.shape[0])),
            in_specs=[blk(qmap), blk(kmap), blk(kmap)],
            out_specs=[blk(qmap), pl.BlockSpec((LANES, tq), tmap), blk(qmap), pl.BlockSpec((8, tq), tmap)],
            scratch_shapes=[pltpu.VMEM((tq, LANES), F32)] * 2),
        out_shape=[jax.ShapeDtypeStruct((T, HP), BF16), jax.ShapeDtypeStruct((HP, T), BF16),
                   jax.ShapeDtypeStruct((T, HP), F32), jax.ShapeDtypeStruct((8 * MLA_HEADS, T), F32)],
        compiler_params=_cparams(("parallel", "arbitrary")),
    )(ii, jj, q, k, v)


def _flash_dq(q, k, v, do, lse, delta):
    T = q.shape[0]
    tq, sr = _flash_tiles(T)
    n = T // tq
    ii, jj = _pairs(n, True)

    def body(ii_ref, jj_ref, q_ref, k_ref, v_ref, do_ref, lse_ref, dl_ref, dq_ref, acc_sc):
        p_ = pl.program_id(1)
        i, j = ii_ref[p_], jj_ref[p_]

        @pl.when(j == 0)
        def _():
            acc_sc[...] = jnp.zeros(acc_sc.shape, F32)

        def tile(diag):
            for rb in range(tq // sr):
                rows = slice(rb * sr, (rb + 1) * sr)
                c0, c1 = _col_span(tq, sr, rb, diag, False)
                ks = k_ref[c0:c1, :]
                p = jnp.exp2(_dot(q_ref[rows, :], ks, "nt") - _lanes(lse_ref[rows, :], c1 - c0))
                if diag:
                    r, c = _span_iota(sr, rb, c0, c1)
                    p = jnp.where(c <= r, p, 0.0)
                dp = _dot(do_ref[rows, :], v_ref[c0:c1, :], "nt")
                acc_sc[rows, :] += _dot(p * (dp - _lanes(dl_ref[rows, :], c1 - c0)), ks)

        @pl.when(j < i)
        def _():
            tile(False)

        @pl.when(j == i)
        def _():
            tile(True)
            dq_ref[...] = acc_sc[...].astype(dq_ref.dtype)

    blk = lambda which: pl.BlockSpec((tq, LANES), which)
    qmap = lambda h, p, ii, jj: (ii[p], h)
    kmap = lambda h, p, ii, jj: (jj[p], h)
    return pl.pallas_call(
        body, name="mla_flash_dq",
        grid_spec=pltpu.PrefetchScalarGridSpec(
            num_scalar_prefetch=2, grid=(MLA_HEADS, int(ii.shape[0])),
            in_specs=[blk(qmap), blk(kmap), blk(kmap), blk(qmap), blk(qmap), blk(qmap)],
            out_specs=blk(qmap),
            scratch_shapes=[pltpu.VMEM((tq, LANES), F32)]),
        out_shape=jax.ShapeDtypeStruct((T, HP), BF16),
        compiler_params=_cparams(("parallel", "arbitrary")),
    )(ii, jj, q, k, v, do, lse, delta)


def _flash_dkv(q, k, v, do, lse_t, delta_t):
    T = q.shape[0]
    tq, sr = _flash_tiles(T)
    n = T // tq
    jj, ii = _pairs(n, False)

    def body(jj_ref, ii_ref, q_ref, k_ref, v_ref, do_ref, lse_ref, dl_ref, dk_ref, dv_ref, dk_sc, dv_sc):
        p_ = pl.program_id(1)
        j, i = jj_ref[p_], ii_ref[p_]

        @pl.when(i == j)
        def _():
            dk_sc[...] = jnp.zeros(dk_sc.shape, F32)
            dv_sc[...] = jnp.zeros(dv_sc.shape, F32)

        def tile(diag):
            for rb in range(tq // sr):
                rows = slice(rb * sr, (rb + 1) * sr)
                c0, c1 = _col_span(tq, sr, rb, diag, True)
                qs, dos = q_ref[c0:c1, :], do_ref[c0:c1, :]
                pt = jnp.exp2(_dot(k_ref[rows, :], qs, "nt") - lse_ref[:1, c0:c1])
                if diag:
                    r, c = _span_iota(sr, rb, c0, c1)
                    pt = jnp.where(r <= c, pt, 0.0)
                dpt = _dot(v_ref[rows, :], dos, "nt")
                dv_sc[rows, :] += _dot(pt, dos)
                dk_sc[rows, :] += _dot(pt * (dpt - dl_ref[:1, c0:c1]), qs)

        @pl.when(i == j)
        def _():
            tile(True)

        @pl.when(i > j)
        def _():
            tile(False)

        @pl.when(i == n - 1)
        def _():
            dk_ref[...] = (dk_sc[...] * LN2).astype(dk_ref.dtype)
            dv_ref[...] = dv_sc[...].astype(dv_ref.dtype)

    blk = lambda which: pl.BlockSpec((tq, LANES), which)
    qmap = lambda h, p, jj, ii: (ii[p], h)
    kmap = lambda h, p, jj, ii: (jj[p], h)
    lse_rows = pl.BlockSpec((8, tq), lambda h, p, jj, ii: (h, ii[p]))
    delta_rows = pl.BlockSpec((8, tq), lambda h, p, jj, ii: (h * (LANES // 8), ii[p]))
    return pl.pallas_call(
        body, name="mla_flash_dkv",
        grid_spec=pltpu.PrefetchScalarGridSpec(
            num_scalar_prefetch=2, grid=(MLA_HEADS, int(ii.shape[0])),
            in_specs=[blk(qmap), blk(kmap), blk(kmap), blk(qmap), lse_rows, delta_rows],
            out_specs=[blk(kmap), blk(kmap)],
            scratch_shapes=[pltpu.VMEM((tq, LANES), F32)] * 2),
        out_shape=[jax.ShapeDtypeStruct((T, HP), BF16)] * 2,
        compiler_params=_cparams(("parallel", "arbitrary")),
    )(jj, ii, q, k, v, do, lse_t, delta_t)


def _mem_fwd(z, km, vm, gq):
    scale = MEM_HEAD_DIM ** -0.5

    def fn(qm, km, vm, gq):
        ys = []
        for h in range(MEM_HEADS):
            sl = slice(h * LANES, (h + 1) * LANES)
            q = _rms(qm[:, sl], gq) * scale
            s = _dot(q, km[:, sl], "nt")
            p = jnp.exp(s - jnp.max(s, axis=1, keepdims=True))
            p = p / jnp.sum(p, axis=1, keepdims=True)
            ys.append(_dot(p, vm[:, sl]))
        y = jnp.concatenate(ys, axis=1)
        return y, y

    return _rowwise(fn, "mem_fwd", 512, [(z, MEM_WIDTH, Z_QM // MEM_WIDTH)], [km, vm, gq],
                    [(MEM_WIDTH, BF16), (MEM_WIDTH, BF16, "T")])


def _mem_bwd(z, dy, km, vm, gq, dz):
    scale = MEM_HEAD_DIM ** -0.5

    def fn(qm, dy, km, vm, gq):
        dqs, dks, dvs = [], [], []
        dgq = jnp.zeros((1, LANES), F32)
        for h in range(MEM_HEADS):
            sl = slice(h * LANES, (h + 1) * LANES)
            q = (_rms(qm[:, sl], gq) * scale).astype(BF16)
            dyh = dy[:, sl]
            kh, vh = km[:, sl], vm[:, sl]
            s = _dot(q, kh, "nt")
            p = jnp.exp(s - jnp.max(s, axis=1, keepdims=True))
            p = p / jnp.sum(p, axis=1, keepdims=True)
            dp = _dot(dyh, vh, "nt")
            ds = p * (dp - jnp.sum(p * dp, axis=1, keepdims=True))
            dq = _dot(ds, kh) * scale
            dx, dg = _rms_bwd(qm[:, sl], gq, dq)
            dqs.append(dx)
            dgq = dgq + dg
            st = _dot(kh, q, "nt")
            pt = jnp.exp(st - jnp.max(st, axis=0, keepdims=True))
            pt = pt / jnp.sum(pt, axis=0, keepdims=True)
            dpt = _dot(vh, dyh, "nt")
            dst = pt * (dpt - jnp.sum(pt * dpt, axis=0, keepdims=True))
            dvs.append(_dot(pt, dyh))
            dks.append(_dot(dst, q))
        return jnp.concatenate(dqs, axis=1), jnp.concatenate(dks, axis=1), jnp.concatenate(dvs, axis=1), dgq

    m = km.shape[0]
    return _rowwise(fn, "mem_bwd", 512, [(z, MEM_WIDTH, Z_QM // MEM_WIDTH), dy], [km, vm, gq],
                    [("into", dz, MEM_WIDTH, Z_QM // MEM_WIDTH)],
                    [((m, MEM_WIDTH), F32), ((m, MEM_WIDTH), F32), ((1, LANES), F32)])


GROUPS = {"ffn1": ["ffn1_w_gu"], "ffn1_down": ["ffn1_w_down"],
          "mix": ["w_in", "mla_w_uq", "mla_w_ukv", "mem_w_kv", "w_branch_a", "w_branch_b", "w_branch_c", "w_out"],
          "ffn2": ["ffn2_w_gu", "ffn2_w_down"]}
GRAD_GROUPS = {"ffn2": GROUPS["ffn2"], "mix": GROUPS["mix"], "ffn1_down": ["ffn1_w_down"], "ffn1_gu": ["ffn1_w_gu"]}


def _local_step(x, mem, positions, loss_target, P, weights, grads_out):
    T = x.shape[0]
    G = {}
    W = dict(weights("ffn1", None))

    half = MLA_ROPE // 2
    inv = ROPE_BASE ** (-jnp.arange(half, dtype=F32) / half)
    ang = positions.astype(F32)[:, None] * inv
    cos, sin = jnp.cos(ang), jnp.sin(ang)
    one, zero = jnp.ones((T, MLA_NOPE), F32), jnp.zeros((T, half), F32)
    pad = LANES - MLA_QK
    tabs = (jnp.concatenate([one, cos, cos, jnp.ones((T, pad), F32)], axis=1),
            jnp.concatenate([jnp.zeros((T, MLA_NOPE), F32), -sin, zero, jnp.zeros((T, pad), F32)], axis=1),
            jnp.concatenate([jnp.zeros((T, MLA_NOPE), F32), zero, sin, jnp.zeros((T, pad), F32)], axis=1))
    gq_p = jnp.pad(P["mla_q_norm"], ((0, 0), (0, pad)))
    gk_p = jnp.pad(P["mla_k_norm"], ((0, 0), (0, pad)))
    bias_full = jnp.repeat(P["sg_b"].T, SG_GROUP_DIM, axis=1)
    group_ind = jnp.repeat(jnp.eye(SG_GROUPS, dtype=F32), SG_GROUP_DIM, axis=0)

    HT = (D_MODEL, BF16, "T")

    def norm2(x, g):
        h = _rms(x, g)
        return h, h

    h1, h1t = _rowwise(norm2, "ffn1_norm", 512, [x], [P["ffn1_norm"]], [(D_MODEL, BF16), HT])
    def ffn1_w_down(after):
        W.update(weights("ffn1_down", after))
        return W["ffn1_w_down"]

    gu1, a1t, o1 = _ffn_fwd(h1, W["ffn1_w_gu"], ffn1_w_down, "ffn1")

    def resid_norm(x, o, g):
        xn = x + 0.5 * o
        h = _rms(xn, g)
        return xn, h, h

    x1, hm, hmt = _rowwise(resid_norm, "mix_norm", 512, [x, o1], [P["mix_norm"]],
                           [(D_MODEL, F32), (D_MODEL, BF16), HT])
    W.update(weights("mix", hm))
    z = _mm(hm, W["w_in"], "nn", BF16, "w_in", tm=1024, tn=1792)

    y_a, y_at = _sg_fwd(z, P["sg_ln_g"], P["sg_ln_b"], P["sg_w"], bias_full)

    def c_norm(cq, ckv, gq, gkv):
        a, b = _rms(cq, gq), _rms(ckv, gkv)
        return a, b, a, b

    cqn, ckvn, cqnt, ckvnt = _rowwise(
        c_norm, "mla_cnorm", 512, [(z, MLA_Q_RANK, Z_CQ // MLA_Q_RANK), (z, MLA_KV_RANK, Z_CKV // MLA_KV_RANK)],
        [P["mla_cq_norm"], P["mla_ckv_norm"]],
        [(MLA_Q_RANK, BF16), (MLA_KV_RANK, BF16), (MLA_Q_RANK, BF16, "T"), (MLA_KV_RANK, BF16, "T")])
    q_pre = _mm(cqn, W["mla_w_uq"], "nn", BF16, "mla_uq", tm=1024, tn=1024)
    kv_pre = _mm(ckvn, W["mla_w_ukv"], "nn", BF16, "mla_ukv", tm=1024, tn=1024)
    q, k, v = _mla_post(q_pre, kv_pre, z, tabs, gq_p, gk_p)
    y_b, y_bt, lse, lse_t = _flash_fwd(q, k, v)

    memn, = _rowwise(lambda m, g: _rms(m, g), "mem_norm", 256, [mem], [P["mem_norm"]], [(D_MODEL, BF16)])
    kvm = _mm(memn, W["mem_w_kv"], "nn", F32, "mem_kv")

    def mem_k(kvm, gk):
        ks = [_rms(kvm[:, h * LANES:(h + 1) * LANES], gk) for h in range(MEM_HEADS)]
        return jnp.concatenate(ks, axis=1), kvm[:, MEM_WIDTH:]

    km, vm = _rowwise(mem_k, "mem_knorm", 256, [kvm], [P["mem_k_norm"]], [(MEM_WIDTH, BF16), (MEM_WIDTH, BF16)])
    y_c, y_ct = _mem_fwd(z, km, vm, P["mem_q_norm"])

    pa = _mm(y_a, W["w_branch_a"], "nn", BF16, "branch_a", tm=1024, tn=1024)
    pb = _mm(y_b, W["w_branch_b"], "nn", BF16, "branch_b", tm=1024, tn=1024)
    pc = _mm(y_c, W["w_branch_c"], "nn", BF16, "branch_c", tm=1024, tn=1024)

    def merge(zg, pa, pb, pc, b):
        g = _sigmoid(zg + b)
        m = g[:, :D_MODEL] * pa + g[:, D_MODEL:2 * D_MODEL] * pb + g[:, 2 * D_MODEL:] * pc
        return m, m

    merged, mergedt = _rowwise(merge, "merge", 256, [(z, 3 * D_MODEL, 0), pa, pb, pc], [P["b_gate"]],
                               [(D_MODEL, BF16), HT])
    om = _mm(merged, W["w_out"], "nn", BF16, "w_out", tm=1024, tn=1024)

    def resid_norm1(x, o, g):
        xn = x + o
        h = _rms(xn, g)
        return xn, h, h

    x2, h2, h2t = _rowwise(resid_norm1, "ffn2_norm", 512, [x1, om], [P["ffn2_norm"]],
                           [(D_MODEL, F32), (D_MODEL, BF16), HT])
    W.update(weights("ffn2", h2))
    gu2, a2t, o2 = _ffn_fwd(h2, W["ffn2_w_gu"], W["ffn2_w_down"], "ffn2")

    def loss_fn(x2, o2, t):
        e = x2 + 0.5 * o2 - t
        return e * (1.0 / D_MODEL), (e * (0.5 / D_MODEL)).astype(BF16), _rsum(e * e) * (0.5 / D_MODEL)

    dx3, do2, loss_part = _rowwise(loss_fn, "loss", 512, [x2, o2, loss_target], [],
                                   [(D_MODEL, F32), (D_MODEL, BF16)], [((1, D_MODEL), F32)])

    dh2, G["ffn2_w_gu"], G["ffn2_w_down"] = _ffn_bwd(do2, h2t, gu2, a2t, W["ffn2_w_gu"], W["ffn2_w_down"], "ffn2")
    tie = grads_out("ffn2", G)

    def norm_bwd(x, dh, dxo, g, *_):
        dx, dg = _rms_bwd(x, g, dh)
        dx = dx + dxo
        return dx, dx, dg

    dx2, dx2b, G["ffn2_norm"] = _rowwise(norm_bwd, "ffn2_norm_bwd", 512, [x2, dh2, dx3],
                                         [P["ffn2_norm"]] + ([] if tie is None else [tie]),
                                         [(D_MODEL, F32), (D_MODEL, BF16)], [((1, D_MODEL), F32)])

    G["w_out"] = _mm_t(mergedt, dx2b, "w_out_dw", tm=1024, tn=1024)
    dmerged = _mm(dx2b, W["w_out"], "nt", BF16, "w_out_dx", tm=1024, tn=1024)

    def merge_bwd(zg, pa, pb, pc, dm, b):
        g = _sigmoid(zg + b)
        ps = jnp.concatenate([pa, pb, pc], axis=1)
        dm3 = jnp.concatenate([dm, dm, dm], axis=1)
        dzg = dm3 * ps * g * (1.0 - g)
        dp = dm3 * g
        return dzg, dp[:, :D_MODEL], dp[:, D_MODEL:2 * D_MODEL], dp[:, 2 * D_MODEL:], _rsum(dzg)

    dz = lax.empty((T, Z_COLS), BF16)
    dz, dpa, dpb, dpc, G["b_gate"] = _rowwise(
        merge_bwd, "merge_bwd", 256, [(z, 3 * D_MODEL, 0), pa, pb, pc, dmerged], [P["b_gate"]],
        [("into", dz, 3 * D_MODEL, 0), (D_MODEL, BF16), (D_MODEL, BF16), (D_MODEL, BF16)], [((1, 3 * D_MODEL), F32)])

    G["w_branch_a"] = _mm_t(y_at, dpa, "branch_a_dw", tm=512, tn=1024)
    G["w_branch_b"] = _mm_t(y_bt, dpb, "branch_b_dw", tm=1024, tn=1024)
    G["w_branch_c"] = _mm_t(y_ct, dpc, "branch_c_dw", tm=512, tn=1024)
    dy_a = _mm(dpa, W["w_branch_a"], "nt", BF16, "branch_a_dx", tm=1024, tn=512)
    dy_b = _mm(dpb, W["w_branch_b"], "nt", BF16, "branch_b_dx", tm=1024, tn=1024)
    dy_c = _mm(dpc, W["w_branch_c"], "nt", BF16, "branch_c_dx", tm=1024, tn=512)

    dz, G["sg_w"], dbias_t, G["sg_ln_g"], G["sg_ln_b"] = _sg_bwd(
        z, dy_a, P["sg_ln_g"], P["sg_ln_b"], P["sg_w"], bias_full, group_ind, dz)
    G["sg_b"] = dbias_t.T

    dz, dkm, dvm, G["mem_q_norm"] = _mem_bwd(z, dy_c, km, vm, P["mem_q_norm"], dz)

    def mem_k_bwd(kvm, dkm, dvm, gk):
        dks = []
        dg = jnp.zeros((1, LANES), F32)
        for h in range(MEM_HEADS):
            sl = slice(h * LANES, (h + 1) * LANES)
            dx, d = _rms_bwd(kvm[:, sl], gk, dkm[:, sl])
            dks.append(dx)
            dg = dg + d
        return jnp.concatenate(dks + [dvm], axis=1), dg

    dkvm, G["mem_k_norm"] = _rowwise(mem_k_bwd, "mem_knorm_bwd", 256, [kvm, dkm, dvm], [P["mem_k_norm"]],
                                     [(2 * MEM_WIDTH, BF16)], [((1, LANES), F32)])
    G["mem_w_kv"] = _mm(memn, dkvm, "tn", BF16, "mem_kv_dw")
    dmemn = _mm(dkvm, W["mem_w_kv"], "nt", F32, "mem_kv_dx")
    _, G["mem_norm"] = _rowwise(lambda m, d, g: _rms_bwd(m, g, d), "mem_norm_bwd", 256, [mem, dmemn],
                                [P["mem_norm"]], [(D_MODEL, BF16)], [((1, D_MODEL), F32)])

    def delta_fn(o, do):
        od = o.astype(F32) * do.astype(F32)
        ds = [jnp.broadcast_to(jnp.sum(od[:, h * LANES:(h + 1) * LANES], axis=1, keepdims=True), (od.shape[0], LANES))
              for h in range(MLA_HEADS)]
        d = jnp.concatenate(ds, axis=1)
        return d, d

    delta, delta_t = _rowwise(delta_fn, "mla_delta", 512, [y_b, dy_b], [], [(HP, F32), (HP, F32, "T")])
    dq = _flash_dq(q, k, v, dy_b, lse, delta)
    dk, dv = _flash_dkv(q, k, v, dy_b, lse_t, delta_t)
    dq_pre, dkv_pre, dkr, dgq, dgk = _mla_post_bwd(q_pre, kv_pre, z, tabs, gq_p, gk_p, dq, dk, dv)
    G["mla_q_norm"], G["mla_k_norm"] = dgq[:, :MLA_QK], dgk[:, :MLA_QK]
    G["mla_w_uq"] = _mm_t(cqnt, dq_pre, "mla_uq_dw", tm=384, tn=1024)
    G["mla_w_ukv"] = _mm_t(ckvnt, dkv_pre, "mla_ukv_dw", tm=256, tn=2048)
    dcqn = _mm(dq_pre, W["mla_w_uq"], "nt", BF16, "mla_uq_dx", tm=1024)
    dckvn = _mm(dkv_pre, W["mla_w_ukv"], "nt", BF16, "mla_ukv_dx", tm=1024)

    def c_norm_bwd(cq, ckv, dcqn, dckvn, dkr, gq, gkv):
        dcq, dgq = _rms_bwd(cq, gq, dcqn)
        dckv, dgkv = _rms_bwd(ckv, gkv, dckvn)
        return jnp.concatenate([dckv, dkr, dcq], axis=1), dgq, dgkv

    tail = Z_COLS - Z_CKV
    dz, G["mla_cq_norm"], G["mla_ckv_norm"] = _rowwise(
        c_norm_bwd, "mla_cnorm_bwd", 512,
        [(z, MLA_Q_RANK, Z_CQ // MLA_Q_RANK), (z, MLA_KV_RANK, Z_CKV // MLA_KV_RANK), dcqn, dckvn, dkr],
        [P["mla_cq_norm"], P["mla_ckv_norm"]], [("into", dz, tail, Z_CKV // tail)],
        [((1, MLA_Q_RANK), F32), ((1, MLA_KV_RANK), F32)])
    G["w_in"] = _mm_t(hmt, dz, "w_in_dw", tm=1024, tn=1792, tk=2048)
    dhm = _mm(dz, W["w_in"], "nt", BF16, "w_in_dx", tm=1024, tn=1024, tk=2688)

    def norm_bwd_half(x, dh, dxo, g):
        dx, dg = _rms_bwd(x, g, dh)
        dx = dx + dxo
        return dx, (0.5 * dx), dg

    dx1, do1, G["mix_norm"] = _rowwise(norm_bwd_half, "mix_norm_bwd", 512, [x1, dhm, dx2], [P["mix_norm"]],
                                       [(D_MODEL, F32), (D_MODEL, BF16)], [((1, D_MODEL), F32)])
    tie = grads_out("mix", G)

    def ffn1_dw(which, dw):
        G["ffn1_w_" + which] = dw
        return grads_out("ffn1_" + which, G)

    dh1, _, _ = _ffn_bwd(do1, h1t, gu1, a1t, W["ffn1_w_gu"], W["ffn1_w_down"], "ffn1", tie, ffn1_dw)

    def norm_bwd_last(x, dh, dxo, g):
        dx, dg = _rms_bwd(x, g, dh)
        return dx + dxo, dg

    grad_x, G["ffn1_norm"] = _rowwise(norm_bwd_last, "ffn1_norm_bwd", 512, [x, dh1, dx1], [P["ffn1_norm"]],
                                      [(D_MODEL, F32)], [((1, D_MODEL), F32)])
    return loss_part, grad_x, G


SHARDED = ["ffn1_w_gu", "ffn1_w_down", "w_in", "mla_w_uq", "mla_w_ukv", "mem_w_kv",
           "w_branch_a", "w_branch_b", "w_branch_c", "w_out", "ffn2_w_gu", "ffn2_w_down"]
ROW_SHARDED = {"ffn1_w_down", "mem_w_kv", "w_out", "ffn2_w_down"}
SMALL = ["ffn1_norm", "mix_norm", "b_gate", "sg_ln_g", "sg_ln_b", "sg_w", "sg_b", "mla_cq_norm",
         "mla_ckv_norm", "mla_q_norm", "mla_k_norm", "mem_norm", "mem_q_norm", "mem_k_norm", "ffn2_norm"]
ORDER = ["ffn1_norm", "ffn1_w_gu", "ffn1_w_down", "mix_norm", "w_in", "b_gate", "sg_ln_g", "sg_ln_b", "sg_w",
         "sg_b", "mla_cq_norm", "mla_w_uq", "mla_ckv_norm", "mla_w_ukv", "mla_q_norm", "mla_k_norm", "mem_norm",
         "mem_w_kv", "mem_q_norm", "mem_k_norm", "w_branch_a", "w_branch_b", "w_branch_c", "w_out", "ffn2_norm",
         "ffn2_w_gu", "ffn2_w_down"]

_IN_U, _IN_V, _IN_CQ, _IN_CKV, _IN_KR, _IN_QM, _IN_G = 0, 512, 1024, 1408, 1664, 1696, 2208
IN_COLS = 5280


def _full_from_slabs(name, slabs):
    n, r, c = slabs.shape
    if name in ROW_SHARDED:
        return slabs.reshape(n * r, c)
    return slabs.transpose(1, 0, 2).reshape(r, n * c)


def _slabs_from_full(name, full):
    if name in ROW_SHARDED:
        return full.reshape(N_DEV, full.shape[0] // N_DEV, full.shape[1])
    r, c = full.shape
    return full.reshape(r, N_DEV, c // N_DEV).transpose(1, 0, 2)


def _compute_layout(full):
    W = dict(full)
    if "w_in" not in full:
        return W
    w = full["w_in"]
    kr = jnp.pad(w[:, _IN_KR:_IN_QM], ((0, 0), (KR_LANE, LANES - KR_LANE - MLA_ROPE)))
    W["w_in"] = jnp.concatenate([w[:, _IN_G:], w[:, _IN_U:_IN_CQ], w[:, _IN_QM:_IN_G], w[:, _IN_CKV:_IN_KR], kr,
                                 w[:, _IN_CQ:_IN_CKV]], axis=1)
    uq = full["mla_w_uq"].reshape(MLA_Q_RANK, MLA_HEADS, MLA_QK)
    W["mla_w_uq"] = jnp.pad(uq, ((0, 0), (0, 0), (0, LANES - MLA_QK))).reshape(MLA_Q_RANK, HP)
    ukv = full["mla_w_ukv"].reshape(MLA_KV_RANK, MLA_HEADS, MLA_NOPE + MLA_V)
    padh = lambda a: jnp.pad(a, ((0, 0), (0, 0), (0, LANES - a.shape[2]))).reshape(MLA_KV_RANK, HP)
    W["mla_w_ukv"] = jnp.concatenate([padh(ukv[:, :, :MLA_NOPE]), padh(ukv[:, :, MLA_NOPE:])], axis=1)
    wb = full["w_branch_b"].reshape(MLA_HEADS, MLA_V, D_MODEL)
    W["w_branch_b"] = jnp.pad(wb, ((0, 0), (0, LANES - MLA_V), (0, 0))).reshape(HP, D_MODEL)
    return W


def _reference_layout(G):
    out = dict(G)
    if "w_in" not in G:
        return out
    g = G["w_in"]
    out["w_in"] = jnp.concatenate([
        g[:, Z_U:Z_QM], g[:, Z_CQ:Z_COLS], g[:, Z_CKV:Z_KR], g[:, Z_KR + KR_LANE:Z_KR + KR_LANE + MLA_ROPE],
        g[:, Z_QM:Z_CKV], g[:, Z_G:Z_U]], axis=1)
    out["mla_w_uq"] = G["mla_w_uq"].reshape(MLA_Q_RANK, MLA_HEADS, LANES)[:, :, :MLA_QK].reshape(MLA_Q_RANK, -1)
    gk = G["mla_w_ukv"][:, :HP].reshape(MLA_KV_RANK, MLA_HEADS, LANES)[:, :, :MLA_NOPE]
    gv = G["mla_w_ukv"][:, HP:].reshape(MLA_KV_RANK, MLA_HEADS, LANES)[:, :, :MLA_V]
    out["mla_w_ukv"] = jnp.concatenate([gk, gv], axis=2).reshape(MLA_KV_RANK, -1)
    out["w_branch_b"] = G["w_branch_b"].reshape(MLA_HEADS, LANES, D_MODEL)[:, :MLA_V].reshape(-1, D_MODEL)
    return out


def _pack(parts):
    flat = []
    for a in parts:
        a = a.reshape(-1)
        flat.append(jnp.pad(a, (0, (-a.shape[0]) % LANES)))
    return jnp.concatenate(flat).reshape(-1, LANES)


def _unpack(packed, shapes):
    flat = packed.reshape(-1)
    out, off = [], 0
    for shp in shapes:
        n = int(np.prod(shp))
        out.append(flat[off:off + n].reshape(shp))
        off += n + (-n) % LANES
    return out


MESH = pl.DeviceIdType.MESH
HBM = pl.BlockSpec(memory_space=pltpu.HBM)


def _all_gather(shards):
    n = len(shards)

    def body(*refs):
        x_refs, out_refs, token_ref = refs[:n], refs[n:2 * n], refs[2 * n]
        send_sems, recv_sems, local_sems = refs[2 * n + 1:]
        x, y, c = lax.axis_index("x"), lax.axis_index("y"), lax.axis_index("c")
        me, sibling = (x, y, c), (x, y, 1 - c)
        chips = [(1 - x, y), (x, 1 - y), (1 - x, 1 - y)]
        token_ref[...] = jnp.zeros_like(token_ref)

        def slot(a, px, py, pc):
            return out_refs[a].at[4 * px + 2 * py + pc]

        def copy(a, k, block, to, src=None):
            return pltpu.make_async_remote_copy(
                src_ref=slot(a, *block) if src is None else src, dst_ref=slot(a, *block),
                send_sem=send_sems.at[7 * a + k], recv_sem=recv_sems.at[7 * a + k], device_id=to, device_id_type=MESH)

        arrays = range(n)
        mine = [pltpu.make_async_copy(x_refs[a], slot(a, *me), local_sems.at[a]) for a in arrays]
        for cp in mine:
            cp.start()
        first = [copy(a, 0, me, sibling, src=x_refs[a]) for a in arrays]
        first += [copy(a, 1 + j, me, (*chip, c), src=x_refs[a]) for j, chip in enumerate(chips) for a in arrays]
        for cp in first:
            cp.start()
        passed = []
        for j, chip in enumerate(chips):
            for a in arrays:
                copy(a, 1 + j, (*chip, c), me).wait_recv()
                passed.append(copy(a, 4 + j, (*chip, c), sibling))
                passed[-1].start()
        for a in arrays:
            copy(a, 0, sibling, me).wait_recv()
        for j, chip in enumerate(chips):
            for a in arrays:
                copy(a, 4 + j, (*chip, 1 - c), me).wait_recv()
        for cp in first + passed:
            cp.wait_send()
        for cp in mine:
            cp.wait()

    res = pl.pallas_call(
        body, name="all_gather_weights",
        out_shape=[jax.ShapeDtypeStruct((N_DEV,) + s.shape, s.dtype) for s in shards]
        + [jax.ShapeDtypeStruct((8, LANES), F32)],
        in_specs=[HBM] * n, out_specs=[HBM] * n + [pl.BlockSpec(memory_space=pltpu.VMEM)],
        scratch_shapes=[pltpu.SemaphoreType.DMA((7 * n,)), pltpu.SemaphoreType.DMA((7 * n,)),
                        pltpu.SemaphoreType.DMA((n,))],
    )(*shards)
    return res[:n], res[n]


SEM = pl.BlockSpec(memory_space=pltpu.SEMAPHORE)
DATAFLOW = pltpu.SideEffectType.DATAFLOW_SIDE_EFFECTING


def _peers():
    x, y, c = lax.axis_index("x"), lax.axis_index("y"), lax.axis_index("c")
    out = []
    for k in range(1, N_DEV):
        px = 1 - x if k & 4 else x
        py = 1 - y if k & 2 else y
        pc = 1 - c if k & 1 else c
        out.append((k, (px, py, pc), 4 * px + 2 * py + pc))
    return 4 * x + 2 * y + c, out


def _send_start(srcs, per_peer, name):
    n = len(srcs)
    lands = [lax.empty((N_DEV,) + (s.shape[1:] if per_peer else s.shape), s.dtype) for s in srcs]

    def body(*refs):
        src_refs, land_refs, send_sems, recv_sems, token = refs[:n], refs[n:2 * n], refs[2 * n], refs[2 * n + 1], refs[-1]
        me, peers = _peers()
        for a in range(n):
            for k, pid, pflat in peers:
                pltpu.make_async_remote_copy(
                    src_ref=src_refs[a].at[pflat] if per_peer else src_refs[a], dst_ref=land_refs[a].at[me],
                    send_sem=send_sems.at[7 * a + k - 1], recv_sem=recv_sems.at[7 * a + k - 1],
                    device_id=pid, device_id_type=MESH).start()
        token[...] = jnp.zeros_like(token)

    hbm = lambda a: pltpu.with_memory_space_constraint(a, pltpu.HBM)
    res = pl.pallas_call(
        body, name=name,
        out_shape=(pltpu.SemaphoreType.DMA((7 * n,)), pltpu.SemaphoreType.DMA((7 * n,)),
                   *[pltpu.HBM(a.shape, a.dtype) for a in srcs + lands], jax.ShapeDtypeStruct((8, LANES), F32)),
        in_specs=(HBM,) * (2 * n), out_specs=(SEM, SEM) + (HBM,) * (2 * n) + (pl.BlockSpec(memory_space=pltpu.VMEM),),
        input_output_aliases={i: 2 + i for i in range(2 * n)},
        compiler_params=pltpu.CompilerParams(has_side_effects=DATAFLOW),
    )(*[hbm(a) for a in srcs + lands])
    return (res[0], res[1], list(res[2:2 + n]), list(res[2 + n:2 + 2 * n])), res[-1]


def _send_wait(started, after, per_peer, name):
    send_sems, recv_sems, srcs_thru, lands_thru = started
    n = len(srcs_thru)

    def body(*refs):
        src_refs, land_refs, send_sems, recv_sems = refs[:n], refs[n:2 * n], refs[2 * n], refs[2 * n + 1]
        me, peers = _peers()
        for a in range(n):
            for k, pid, pflat in peers:
                copy = pltpu.make_async_remote_copy(
                    src_ref=src_refs[a].at[pflat] if per_peer else src_refs[a], dst_ref=land_refs[a].at[pflat],
                    send_sem=send_sems.at[7 * a + k - 1], recv_sem=recv_sems.at[7 * a + k - 1],
                    device_id=pid, device_id_type=MESH)
                copy.wait_send()
                copy.wait_recv()

    outs = pl.pallas_call(
        body, name=name,
        out_shape=tuple(pltpu.HBM(a.shape, a.dtype) for a in srcs_thru + lands_thru),
        in_specs=(HBM,) * (2 * n) + (SEM, SEM, pl.BlockSpec(memory_space=pl.ANY)), out_specs=(HBM,) * (2 * n),
        input_output_aliases={i: i for i in range(2 * n)},
        compiler_params=pltpu.CompilerParams(has_side_effects=DATAFLOW),
    )(*srcs_thru, *lands_thru, send_sems, recv_sems, after)
    me = 4 * lax.axis_index("x") + 2 * lax.axis_index("y") + lax.axis_index("c")
    landed = []
    for src_out, land in zip(outs[:n], outs[n:]):
        own = lax.dynamic_index_in_dim(src_out, me, 0, keepdims=True) if per_peer else src_out[None]
        landed.append(lax.dynamic_update_slice(land, own, (me,) + (0,) * (land.ndim - 1)))
    return landed


def _share_rows(block, name):
    def body(src_ref, out_ref, send_sems, recv_sems, local_sem):
        me, peers = _peers()
        own = pltpu.make_async_copy(src_ref, out_ref.at[me], local_sem)
        own.start()
        copies = [pltpu.make_async_remote_copy(
            src_ref=src_ref, dst_ref=out_ref.at[me], send_sem=send_sems.at[k - 1], recv_sem=recv_sems.at[k - 1],
            device_id=pid, device_id_type=MESH) for k, pid, _ in peers]
        for cp in copies:
            cp.start()
        for cp in copies:
            cp.wait()
        own.wait()

    return pl.pallas_call(
        body, name=name, out_shape=jax.ShapeDtypeStruct((N_DEV,) + block.shape, block.dtype),
        in_specs=[HBM], out_specs=HBM,
        scratch_shapes=[pltpu.SemaphoreType.DMA((N_DEV - 1,)), pltpu.SemaphoreType.DMA((N_DEV - 1,)),
                        pltpu.SemaphoreType.DMA],
    )(block)


def _sum_slots(recv, name, tr):
    n, rows, lanes = recv.shape
    tr = _tile(rows, tr)

    def body(r_ref, o_ref):
        acc = r_ref[0].astype(F32)
        for i in range(1, n):
            acc = acc + r_ref[i].astype(F32)
        o_ref[...] = acc

    return pl.pallas_call(
        body, name=name, grid=(rows // tr,),
        in_specs=[pl.BlockSpec((n, tr, lanes), lambda i: (0, i, 0))],
        out_specs=pl.BlockSpec((tr, lanes), lambda i: (i, 0)),
        out_shape=jax.ShapeDtypeStruct((rows, lanes), F32),
        compiler_params=_cparams(("parallel",)),
    )(recv)


def _adamw_math(w, g, m, v):
    m = ADAM_B1 * m + (1.0 - ADAM_B1) * g
    v = ADAM_B2 * v + (1.0 - ADAM_B2) * (g * g)
    m_hat = m / (1.0 - ADAM_B1 ** ADAM_STEP)
    v_hat = v / (1.0 - ADAM_B2 ** ADAM_STEP)
    return -ADAM_LR * (m_hat / (jnp.sqrt(v_hat) + ADAM_EPS) + ADAM_WD * w), m, v


def _adamw(w, g, m, v, name, tr=256):
    return _rowwise(_adamw_math, name, tr, [w, g, m, v], [], [(w.shape[1], F32)] * 3)


def _adamw_small(ws, gs, ms, vs):
    n = len(ws)

    def body(*refs):
        ins, outs = refs[:4 * n], refs[4 * n:]
        for i in range(n):
            d, m, v = _adamw_math(ins[i][...], ins[n + i][...], ins[2 * n + i][...], ins[3 * n + i][...])
            outs[i][...], outs[n + i][...], outs[2 * n + i][...] = d, m, v

    vmem = pl.BlockSpec(memory_space=pltpu.VMEM)
    res = pl.pallas_call(
        body, name="adamw_small", in_specs=[vmem] * (4 * n), out_specs=[vmem] * (3 * n),
        out_shape=[jax.ShapeDtypeStruct(w.shape, F32) for w in ws] * 3,
    )(*ws, *gs, *ms, *vs)
    return res[:n], res[n:2 * n], res[2 * n:]


def _sum_adamw(recv, w, m, v, name):
    n, r, c = recv.shape
    tr = _tile(r, 256)

    def body(r_ref, w_ref, m_ref, v_ref, g_ref, d_ref, nm_ref, nv_ref):
        g = r_ref[0].astype(F32)
        for i in range(1, n):
            g = g + r_ref[i].astype(F32)
        g_ref[...] = g
        d_ref[...], nm_ref[...], nv_ref[...] = _adamw_math(w_ref[...], g, m_ref[...], v_ref[...])

    row = pl.BlockSpec((None, tr, c), lambda i: (0, i, 0))
    return pl.pallas_call(
        body, name=name, grid=(r // tr,),
        in_specs=[pl.BlockSpec((n, tr, c), lambda i: (0, i, 0)), row, row, row], out_specs=[row] * 4,
        out_shape=[jax.ShapeDtypeStruct((1, r, c), F32)] * 4, compiler_params=_cparams(("parallel",)),
    )(recv, w, m, v)


def kernel(x, mem, positions, ffn1_norm, ffn1_w_gu, ffn1_w_down, mix_norm, w_in, b_gate, sg_ln_g, sg_ln_b, sg_w, sg_b, mla_cq_norm, mla_w_uq, mla_ckv_norm, mla_w_ukv, mla_q_norm, mla_k_norm, mem_norm, mem_w_kv, mem_q_norm, mem_k_norm, w_branch_a, w_branch_b, w_branch_c, w_out, ffn2_norm, ffn2_w_gu, ffn2_w_down, loss_target, m_ffn1_norm, m_ffn1_w_gu, m_ffn1_w_down, m_mix_norm, m_w_in, m_b_gate, m_sg_ln_g, m_sg_ln_b, m_sg_w, m_sg_b, m_mla_cq_norm, m_mla_w_uq, m_mla_ckv_norm, m_mla_w_ukv, m_mla_q_norm, m_mla_k_norm, m_mem_norm, m_mem_w_kv, m_mem_q_norm, m_mem_k_norm, m_w_branch_a, m_w_branch_b, m_w_branch_c, m_w_out, m_ffn2_norm, m_ffn2_w_gu, m_ffn2_w_down, v_ffn1_norm, v_ffn1_w_gu, v_ffn1_w_down, v_mix_norm, v_w_in, v_b_gate, v_sg_ln_g, v_sg_ln_b, v_sg_w, v_sg_b, v_mla_cq_norm, v_mla_w_uq, v_mla_ckv_norm, v_mla_w_ukv, v_mla_q_norm, v_mla_k_norm, v_mem_norm, v_mem_w_kv, v_mem_q_norm, v_mem_k_norm, v_w_branch_a, v_w_branch_b, v_w_branch_c, v_w_out, v_ffn2_norm, v_ffn2_w_gu, v_ffn2_w_down):
    given = dict(locals())
    wts = {n: given[n] for n in ORDER}
    mom = {n: given["m_" + n] for n in ORDER}
    var = {n: given["v_" + n] for n in ORDER}

    def shards(group, zero):
        out = [wts[n][0].astype(BF16) for n in GROUPS[group]]
        return [out[0] + zero.astype(BF16)] + out[1:]

    def full_weights(group, slabs):
        return _compute_layout({n: _full_from_slabs(n, s) for n, s in zip(GROUPS[group], slabs)})

    def zero_of(a):
        return jnp.minimum(jnp.abs(a.reshape(-1)[0]), 0)

    gathered_ffn1, token = _all_gather([wts[n][0].astype(BF16) for n in GROUPS["ffn1"]])
    flight = {}
    flight["ffn1_down"], token = _send_start(shards("ffn1_down", token[0, 0]), False, "gather_ffn1_down_start")
    flight["mix"] = _send_start(shards("mix", token[0, 0]), False, "gather_mix_start")[0]
    recv = {}

    def weights(group, after):
        if group == "ffn1":
            return full_weights(group, gathered_ffn1)
        landed = _send_wait(flight.pop(group), after, False, f"gather_{group}_wait")
        if group == "mix":
            flight["ffn2"] = _send_start(shards("ffn2", zero_of(landed[0])), False, "gather_ffn2_start")[0]
        return full_weights(group, landed)

    small_shapes = [wts[n].shape[1:] for n in SMALL]
    early = SMALL[1:]
    assert SMALL[0] == "ffn1_norm"

    def grads_out(group, G):
        Gr = _reference_layout({n: G[n] for n in GRAD_GROUPS[group]})
        parts = [_slabs_from_full(n, Gr[n]).astype(BF16) for n in GRAD_GROUPS[group]]
        flight["g_" + group], tie = _send_start(parts, True, f"grads_{group}_start")
        if group == "mix":
            small = _pack([G[n].reshape(s) for n, s in zip(early, small_shapes[1:])])
            small = jnp.pad(small, ((0, (-small.shape[0]) % 8), (0, 0)))
            flight["small"], tie = _send_start([small + tie[0, 0]], False, "grads_small_start")
        return tie

    P = {n: wts[n] if wts[n].ndim == 2 else wts[n][0] for n in SMALL}
    loss_part, grad_x, G = _local_step(x[0], mem[0], positions[0], loss_target[0], P, weights, grads_out)

    for group, names in GRAD_GROUPS.items():
        recv.update(zip(names, _send_wait(flight.pop("g_" + group), grad_x, True, f"grads_{group}_wait")))
    early_recv, = _send_wait(flight.pop("small"), grad_x, False, "grads_small_wait")
    last = _share_rows(G["ffn1_norm"].reshape(-1, LANES), "share_ffn1_norm")
    g_small_packed = _sum_slots(jnp.concatenate([last, early_recv], axis=1), "sum_small", 2048)

    grads, delta, new_m, new_v = {}, {}, {}, {}
    for n in SHARDED:
        grads[n], delta[n], new_m[n], new_v[n] = _sum_adamw(recv[n], wts[n], mom[n], var[n], "adamw_" + n)
    grads.update(zip(SMALL, _unpack(g_small_packed, small_shapes)))

    flat2 = lambda d: [d[n].reshape(-1, d[n].shape[-1]) for n in SMALL]
    for dst, vals in zip((delta, new_m, new_v), _adamw_small(flat2(wts), flat2(grads), flat2(mom), flat2(var))):
        dst.update(zip(SMALL, vals))

    loss = lax.psum(jnp.sum(loss_part), ("x", "y", "c"))
    lead = lambda d: [d[n].reshape(wts[n].shape) for n in ORDER]
    return (loss, grad_x[None], *lead(grads), *lead(delta), *lead(new_m), *lead(new_v))
```

```python
import functools

import numpy as np
import jax
import jax.numpy as jnp
from jax import lax
from jax.experimental import pallas as pl
from jax.experimental.pallas import tpu as pltpu

F32, BF16 = jnp.float32, jnp.bfloat16

D_MODEL = 1024
SG_GROUPS, SG_GROUP_DIM, SG_WIDTH, CHUNK = 8, 64, 512, 128
MLA_HEADS, MLA_NOPE, MLA_ROPE, MLA_V, MLA_QK = 8, 64, 32, 64, 96
MLA_Q_RANK, MLA_KV_RANK = 384, 256
MEM_HEADS, MEM_HEAD_DIM, MEM_WIDTH = 4, 128, 512
D_FF = 2816
ROPE_BASE = 10000.0
EPS = 1e-6
NEG = -1e30
ADAM_LR, ADAM_B1, ADAM_B2, ADAM_EPS, ADAM_WD, ADAM_STEP = 0.001, 0.9, 0.999, 1e-08, 0.01, 10

N_DEV = 8
LANES = 128
V7X_VMEM_LIMIT = 56 * 1024 * 1024
HP = MLA_HEADS * LANES

Z_G, Z_U, Z_V, Z_QM, Z_CKV, Z_KR, Z_CQ = 0, 3072, 3584, 4096, 4608, 4864, 4992
Z_COLS = 5376
KR_LANE = 64


def _tile(dim, pref):
    if dim <= pref:
        return dim
    for t in range(pref - pref % LANES, LANES - 1, -LANES):
        if dim % t == 0:
            return t
    for t in range(pref - pref % 8, 7, -8):
        if dim % t == 0:
            return t
    return dim


def _cparams(sem):
    return pltpu.CompilerParams(dimension_semantics=sem, vmem_limit_bytes=V7X_VMEM_LIMIT)


_DN = {"nn": ((1,), (0,)), "nt": ((1,), (1,)), "tn": ((0,), (0,))}


def _dot(a, b, mode="nn"):
    return lax.dot_general(a.astype(BF16), b.astype(BF16), (_DN[mode], ((), ())),
                           preferred_element_type=F32)


def _mm(a, b, mode, out_dtype, name, tm=512, tn=512, tk=2048, tie=None):
    if mode == "tn":
        K, M = a.shape
    else:
        M, K = a.shape
    N = b.shape[0] if mode == "nt" else b.shape[1]
    tm, tn, tk = _tile(M, tm), _tile(N, tn), _tile(K, tk)
    nk = K // tk
    if mode == "tn":
        a_spec = pl.BlockSpec((tk, tm), lambda i, j, k: (k, i))
    else:
        a_spec = pl.BlockSpec((tm, tk), lambda i, j, k: (i, k))
    if mode == "nt":
        b_spec = pl.BlockSpec((tn, tk), lambda i, j, k: (j, k))
    else:
        b_spec = pl.BlockSpec((tk, tn), lambda i, j, k: (k, j))

    ties = [] if tie is None else [tie]

    def body(a_ref, b_ref, *rest):
        o_ref, *scratch = rest[len(ties):]
        p = _dot(a_ref[...], b_ref[...], mode)
        if nk == 1:
            o_ref[...] = p.astype(o_ref.dtype)
        else:
            acc_ref, = scratch
            k = pl.program_id(2)

            @pl.when(k == 0)
            def _():
                acc_ref[...] = p

            @pl.when(k > 0)
            def _():
                acc_ref[...] += p

            @pl.when(k == nk - 1)
            def _():
                o_ref[...] = acc_ref[...].astype(o_ref.dtype)

    return pl.pallas_call(
        body, name=name, grid=(M // tm, N // tn, nk),
        in_specs=[a_spec, b_spec] + [pl.BlockSpec(t.shape, lambda i, j, k: (0, 0)) for t in ties],
        out_specs=pl.BlockSpec((tm, tn), lambda i, j, k: (i, j)),
        out_shape=jax.ShapeDtypeStruct((M, N), out_dtype),
        scratch_shapes=[] if nk == 1 else [pltpu.VMEM((tm, tn), F32)],
        compiler_params=_cparams(("parallel", "parallel", "arbitrary")),
    )(a, b, *ties)


def _mm_t(at, b, name, tm, tn, tk=1024, tie=None):
    return _mm(at, b, "nn", BF16, name, tm=tm, tn=tn, tk=tk, tie=tie)


def _rowwise(fn, name, tr, row_ins, bc_ins, row_outs, acc_outs=()):
    norm = [it if isinstance(it, tuple) else (it, it.shape[1], 0) for it in row_ins]
    rows = norm[0][0].shape[0]
    tr = _tile(rows, tr)
    arrays, in_specs = [], []
    for arr, w, cb in norm:
        arrays.append(arr)
        in_specs.append(pl.BlockSpec((tr, w), lambda i, cb=cb: (i, cb)))
    for arr in bc_ins:
        arrays.append(arr)
        in_specs.append(pl.BlockSpec(arr.shape, lambda i, nd=arr.ndim: (0,) * nd))
    n_in, n_row = len(arrays), len(row_outs)
    out_shape, out_specs, aliases = [], [], {}
    transposed = [len(o) == 3 for o in row_outs]
    for k, o in enumerate(row_outs):
        if o[0] == "into":
            _, target, w, cb = o
            aliases[len(arrays)] = k
            arrays.append(target)
            in_specs.append(pl.BlockSpec(memory_space=pl.ANY))
            out_shape.append(jax.ShapeDtypeStruct(target.shape, target.dtype))
            out_specs.append(pl.BlockSpec((tr, w), lambda i, cb=cb: (i, cb)))
        elif transposed[k]:
            out_shape.append(jax.ShapeDtypeStruct((o[0], rows), o[1]))
            out_specs.append(pl.BlockSpec((o[0], tr), lambda i: (0, i)))
        else:
            out_shape.append(jax.ShapeDtypeStruct((rows, o[0]), o[1]))
            out_specs.append(pl.BlockSpec((tr, o[0]), lambda i: (i, 0)))
    for shp, dt in acc_outs:
        out_shape.append(jax.ShapeDtypeStruct(shp, dt))
        out_specs.append(pl.BlockSpec(shp, lambda i, nd=len(shp): (0,) * nd))

    def body(*refs):
        vals = fn(*[r[...].astype(F32) for r in refs[:n_in]])
        if not isinstance(vals, (tuple, list)):
            vals = (vals,)
        outs = refs[len(arrays):]
        for r, v, t in zip(outs[:n_row], vals[:n_row], transposed):
            r[...] = v.astype(F32).T.astype(r.dtype) if t else v.astype(r.dtype)
        if acc_outs:
            accs = list(zip(outs[n_row:], vals[n_row:]))
            i = pl.program_id(0)

            @pl.when(i == 0)
            def _():
                for r, v in accs:
                    r[...] = v.astype(r.dtype)

            @pl.when(i > 0)
            def _():
                for r, v in accs:
                    r[...] += v.astype(r.dtype)

    res = pl.pallas_call(
        body, name=name, grid=(rows // tr,), in_specs=in_specs, out_specs=out_specs,
        out_shape=out_shape, input_output_aliases=aliases, compiler_params=_cparams(("arbitrary",)),
    )(*arrays)
    return res


def _rsum(x):
    return jnp.sum(x, axis=0, keepdims=True)


def _rms(x, g, n=None):
    n = x.shape[-1] if n is None else n
    r = lax.rsqrt(jnp.sum(x * x, axis=-1, keepdims=True) * (1.0 / n) + EPS)
    return x * r * g


def _rms_bwd(x, g, dy, n=None):
    n = x.shape[-1] if n is None else n
    r = lax.rsqrt(jnp.sum(x * x, axis=-1, keepdims=True) * (1.0 / n) + EPS)
    xh = x * r
    dxh = dy * g
    dx = r * (dxh - xh * (jnp.sum(dxh * xh, axis=-1, keepdims=True) * (1.0 / n)))
    return dx, _rsum(dy * xh)


def _gelu(x):
    return 0.5 * x * (1.0 + lax.erf(x * 0.7071067811865476))


def _gelu_grad(x):
    return 0.5 * (1.0 + lax.erf(x * 0.7071067811865476)) + x * jnp.exp(-0.5 * x * x) * 0.3989422804014327


def _sigmoid(x):
    return 0.5 * jnp.tanh(0.5 * x) + 0.5


FFN_TM, FFN_TN = 1024, 1408
MXU_WIDTH = 256


def _col_chunks(n):
    return [(c, min(c + MXU_WIDTH, n)) for c in range(0, n, MXU_WIDTH)]


def _ffn_gu_act(h, w_gu, tag):
    T = h.shape[0]
    tm, tn = _tile(T, FFN_TM), FFN_TN
    nj = D_FF // tn

    def body(h_ref, wg_ref, wu_ref, gu_ref, a_ref, at_ref):
        h = h_ref[...]
        for c0, c1 in _col_chunks(tn):
            g = _dot(h, wg_ref[:, c0:c1])
            u = _dot(h, wu_ref[:, c0:c1])
            gu_ref[0, :, c0:c1] = g.astype(BF16)
            gu_ref[1, :, c0:c1] = u.astype(BF16)
            a = g * _sigmoid(g) * u
            a_ref[:, c0:c1] = a.astype(BF16)
            at_ref[c0:c1, :] = a.T.astype(BF16)

    return pl.pallas_call(
        body, name=f"{tag}_gu_act", grid=(T // tm, nj),
        in_specs=[pl.BlockSpec((tm, D_MODEL), lambda i, j: (i, 0)),
                  pl.BlockSpec((D_MODEL, tn), lambda i, j: (0, j)),
                  pl.BlockSpec((D_MODEL, tn), lambda i, j: (0, j + nj))],
        out_specs=[pl.BlockSpec((2, tm, tn), lambda i, j: (0, i, j)),
                   pl.BlockSpec((tm, tn), lambda i, j: (i, j)),
                   pl.BlockSpec((tn, tm), lambda i, j: (j, i))],
        out_shape=[jax.ShapeDtypeStruct((2, T, D_FF), BF16), jax.ShapeDtypeStruct((T, D_FF), BF16),
                   jax.ShapeDtypeStruct((D_FF, T), BF16)],
        compiler_params=_cparams(("parallel", "parallel")),
    )(h, w_gu, w_gu)


def _ffn_da_actbwd(do, w_down, gu, tag, tie=None):
    T = do.shape[0]
    tm, tn = _tile(T, FFN_TM), FFN_TN
    ties = [] if tie is None else [tie]

    def body(do_ref, wd_ref, gu_ref, *rest):
        dgu_ref = rest[-1]
        do = do_ref[...]
        for c0, c1 in _col_chunks(tn):
            da = _dot(do, wd_ref[c0:c1, :], "nt")
            g = gu_ref[0, :, c0:c1].astype(F32)
            u = gu_ref[1, :, c0:c1].astype(F32)
            s = _sigmoid(g)
            dgu_ref[0, :, c0:c1] = (da * u * s * (1.0 + g * (1.0 - s))).astype(BF16)
            dgu_ref[1, :, c0:c1] = (da * g * s).astype(BF16)

    return pl.pallas_call(
        body, name=f"{tag}_da_actbwd", grid=(T // tm, D_FF // tn),
        in_specs=[pl.BlockSpec((tm, D_MODEL), lambda i, j: (i, 0)),
                  pl.BlockSpec((tn, D_MODEL), lambda i, j: (j, 0)),
                  pl.BlockSpec((2, tm, tn), lambda i, j: (0, i, j))]
        + [pl.BlockSpec(t.shape, lambda i, j: (0, 0)) for t in ties],
        out_specs=pl.BlockSpec((2, tm, tn), lambda i, j: (0, i, j)),
        out_shape=jax.ShapeDtypeStruct((2, T, D_FF), BF16),
        compiler_params=_cparams(("parallel", "parallel")),
    )(do, w_down, gu, *ties)


def _ffn_dwgu(ht, dgu, tag, tk=2048):
    T = ht.shape[1]
    tn, tk = FFN_TN, _tile(T, tk)
    nj, nk = D_FF // tn, T // tk

    def body(a_ref, b_ref, o_ref, acc_ref):
        k = pl.program_id(1)
        p = _dot(a_ref[...], b_ref[...])

        @pl.when(k == 0)
        def _():
            acc_ref[...] = p

        @pl.when(k > 0)
        def _():
            acc_ref[...] += p

        @pl.when(k == nk - 1)
        def _():
            o_ref[...] = acc_ref[...].astype(o_ref.dtype)

    return pl.pallas_call(
        body, name=f"{tag}_dwgu", grid=(2 * nj, nk),
        in_specs=[pl.BlockSpec((D_MODEL, tk), lambda n, k: (0, k)),
                  pl.BlockSpec((None, tk, tn), lambda n, k: (n // nj, k, n % nj))],
        out_specs=pl.BlockSpec((D_MODEL, tn), lambda n, k: (0, n)),
        out_shape=jax.ShapeDtypeStruct((D_MODEL, 2 * D_FF), BF16),
        scratch_shapes=[pltpu.VMEM((D_MODEL, tn), F32)],
        compiler_params=_cparams(("parallel", "arbitrary")),
    )(ht, dgu)


def _ffn_dh(dgu, w_gu, tag, tm=2048, tie=None):
    T = dgu.shape[1]
    tm, tk = _tile(T, tm), FFN_TN
    nk = D_FF // tk
    ties = [] if tie is None else [tie]

    def body(a_ref, b_ref, *rest):
        o_ref, acc_ref = rest[len(ties):]
        k = pl.program_id(1)
        p = _dot(a_ref[...], b_ref[...], "nt")

        @pl.when(k == 0)
        def _():
            acc_ref[...] = p

        @pl.when(k > 0)
        def _():
            acc_ref[...] += p

        @pl.when(k == 2 * nk - 1)
        def _():
            o_ref[...] = acc_ref[...].astype(o_ref.dtype)

    return pl.pallas_call(
        body, name=f"{tag}_dh", grid=(T // tm, 2 * nk),
        in_specs=[pl.BlockSpec((None, tm, tk), lambda i, k: (k // nk, i, k % nk)),
                  pl.BlockSpec((D_MODEL, tk), lambda i, k: (0, k))]
        + [pl.BlockSpec(t.shape, lambda i, k: (0, 0)) for t in ties],
        out_specs=pl.BlockSpec((tm, D_MODEL), lambda i, k: (i, 0)),
        out_shape=jax.ShapeDtypeStruct((T, D_MODEL), BF16),
        scratch_shapes=[pltpu.VMEM((tm, D_MODEL), F32)],
        compiler_params=_cparams(("parallel", "arbitrary")),
    )(dgu, w_gu, *ties)


def _ffn_fwd(h, w_gu, w_down, tag):
    gu, a, at = _ffn_gu_act(h, w_gu, tag)
    if callable(w_down):
        w_down = w_down(at)
    o = _mm(a, w_down, "nn", BF16, f"{tag}_down", tm=1024, tn=1024, tk=2816)
    return gu, at, o


def _ffn_bwd(do, ht, gu, at, w_gu, w_down, tag, tie=None, on_dw=None):
    on_dw = on_dw or (lambda which, dw: None)
    dw_down = _mm_t(at, do, f"{tag}_dwdown", tm=1408, tn=1024, tk=2048, tie=tie)
    dgu = _ffn_da_actbwd(do, w_down, gu, tag, tie=on_dw("down", dw_down))
    dw_gu = _ffn_dwgu(ht, dgu, tag)
    dh = _ffn_dh(dgu, w_gu, tag, tie=on_dw("gu", dw_gu))
    return dh, dw_gu, dw_down


def _sg_common(u_pre, v_pre, ln_g, ln_b):
    u = _gelu(u_pre)
    v = _gelu(v_pre)
    mu = jnp.mean(v, axis=-1, keepdims=True)
    vc = v - mu
    rstd = lax.rsqrt(jnp.mean(vc * vc, axis=-1, keepdims=True) + EPS)
    vhat = vc * rstd
    vl = vhat * ln_g + ln_b
    return u, vhat, rstd, vl


def _sg_masked_pairs(w):
    t = lax.broadcasted_iota(jnp.int32, (CHUNK, CHUNK), 0)
    s = lax.broadcasted_iota(jnp.int32, (CHUNK, CHUNK), 1)
    causal = s <= t
    wm = [jnp.where(causal, w[g], 0.0).astype(BF16) for g in range(SG_GROUPS)]
    return [jnp.concatenate([wm[2 * j], wm[2 * j + 1]], axis=0) for j in range(SG_GROUPS // 2)], causal


def _sg_mix(vl, pairs, bias):
    tr = vl.shape[0]
    low = lax.broadcasted_iota(jnp.int32, (CHUNK, LANES), 1) < SG_GROUP_DIM
    vb = vl.astype(BF16)
    rows = []
    for c in range(tr // CHUNK):
        slabs = []
        for j in range(SG_GROUPS // 2):
            slab = vb[c * CHUNK:(c + 1) * CHUNK, j * LANES:(j + 1) * LANES]
            m = _dot(pairs[j], slab)
            slabs.append(jnp.where(low, m[:CHUNK], m[CHUNK:]))
        rows.append(jnp.concatenate(slabs, axis=1) + bias)
    return jnp.concatenate(rows, axis=0)


def _sg_fwd(z, ln_g, ln_b, sg_w, bias_full):
    def fn(u_pre, v_pre, ln_g, ln_b, w, bias):
        u, _, _, vl = _sg_common(u_pre, v_pre, ln_g, ln_b)
        pairs, _ = _sg_masked_pairs(w)
        y = u * _sg_mix(vl, pairs, bias)
        return y, y

    return _rowwise(fn, "sg_fwd", 512, [(z, SG_WIDTH, Z_U // SG_WIDTH), (z, SG_WIDTH, Z_V // SG_WIDTH)],
                    [ln_g, ln_b, sg_w, bias_full], [(SG_WIDTH, BF16), (SG_WIDTH, BF16, "T")])


def _sg_bwd(z, dy, ln_g, ln_b, sg_w, bias_full, group_ind, dz):
    def fn(u_pre, v_pre, dy, ln_g, ln_b, w, bias, ind):
        dy = dy.astype(F32)
        u, vhat, rstd, vl = _sg_common(u_pre, v_pre, ln_g, ln_b)
        pairs, causal = _sg_masked_pairs(w)
        mixed = _sg_mix(vl, pairs, bias)
        du_pre = dy * mixed * _gelu_grad(u_pre)
        dmix = dy * u
        tr = dy.shape[0]
        low = lax.broadcasted_iota(jnp.int32, (CHUNK, LANES), 1) < SG_GROUP_DIM
        vb = vl.astype(BF16)
        dw = [jnp.zeros((CHUNK, CHUNK), F32) for _ in range(SG_GROUPS)]
        dbias = jnp.zeros((CHUNK, SG_WIDTH), F32)
        dvl_rows = []
        for c in range(tr // CHUNK):
            dm_c = dmix[c * CHUNK:(c + 1) * CHUNK]
            dbias = dbias + dm_c
            slabs = []
            for j in range(SG_GROUPS // 2):
                slab = vb[c * CHUNK:(c + 1) * CHUNK, j * LANES:(j + 1) * LANES]
                dm = dm_c[:, j * LANES:(j + 1) * LANES]
                d0 = jnp.where(low, dm, 0.0).astype(BF16)
                d1 = jnp.where(low, 0.0, dm).astype(BF16)
                dw[2 * j] = dw[2 * j] + _dot(d0, slab, "nt")
                dw[2 * j + 1] = dw[2 * j + 1] + _dot(d1, slab, "nt")
                slabs.append(_dot(pairs[j], jnp.concatenate([d0, d1], axis=0), "tn"))
            dvl_rows.append(jnp.concatenate(slabs, axis=1))
        dvl = jnp.concatenate(dvl_rows, axis=0)
        dln_g = _rsum(dvl * vhat)
        dln_b = _rsum(dvl)
        dvh = dvl * ln_g
        dv = rstd * (dvh - jnp.mean(dvh, axis=-1, keepdims=True)
                     - vhat * jnp.mean(dvh * vhat, axis=-1, keepdims=True))
        dv_pre = dv * _gelu_grad(v_pre)
        dw = jnp.stack([jnp.where(causal, d, 0.0) for d in dw], axis=0)
        dbias_t = lax.dot_general(dbias, ind, (((1,), (0,)), ((), ())), precision=lax.Precision.HIGHEST,
                                  preferred_element_type=F32)
        return jnp.concatenate([du_pre, dv_pre], axis=1), dw, dbias_t, dln_g, dln_b

    return _rowwise(fn, "sg_bwd", 512,
                    [(z, SG_WIDTH, Z_U // SG_WIDTH), (z, SG_WIDTH, Z_V // SG_WIDTH), dy],
                    [ln_g, ln_b, sg_w, bias_full, group_ind],
                    [("into", dz, 2 * SG_WIDTH, Z_U // (2 * SG_WIDTH))],
                    [((SG_GROUPS, CHUNK, CHUNK), F32), ((CHUNK, SG_GROUPS), F32), ((1, SG_WIDTH), F32), ((1, SG_WIDTH), F32)])


def _rope(x, c, s1, s2):
    return x * c + pltpu.roll(x, LANES - MLA_ROPE // 2, 1) * s1 + pltpu.roll(x, MLA_ROPE // 2, 1) * s2


def _rope_t(d, c, s1, s2):
    return d * c + pltpu.roll(d * s1, MLA_ROPE // 2, 1) + pltpu.roll(d * s2, LANES - MLA_ROPE // 2, 1)


def _mla_post(q_pre, kv_pre, z, tabs, gq, gk):
    scale = MLA_QK ** -0.5 * LOG2E

    def fn(q_pre, k_pre, v_pre, kr, c, s1, s2, gq, gk):
        qs, ks = [], []
        for h in range(MLA_HEADS):
            sl = slice(h * LANES, (h + 1) * LANES)
            qs.append(_rope(_rms(q_pre[:, sl], gq, MLA_QK), c, s1, s2) * scale)
            ks.append(_rope(_rms(k_pre[:, sl] + kr, gk, MLA_QK), c, s1, s2))
        lane = lax.broadcasted_iota(jnp.int32, v_pre.shape, 1) & (LANES - 1)
        return jnp.concatenate(qs, axis=1), jnp.concatenate(ks, axis=1), jnp.where(lane == ONES_LANE, 1.0, v_pre)

    return _rowwise(fn, "mla_post", 256,
                    [q_pre, (kv_pre, HP, 0), (kv_pre, HP, 1), (z, LANES, Z_KR // LANES), *tabs],
                    [gq, gk], [(HP, BF16)] * 3)


def _mla_post_bwd(q_pre, kv_pre, z, tabs, gq, gk, dq, dk, dv):
    scale = MLA_QK ** -0.5

    def fn(q_pre, k_pre, kr, c, s1, s2, dq, dk, dv, gq, gk):
        lane = lax.broadcasted_iota(jnp.int32, (1, LANES), 1)
        kr_mask = (lane >= KR_LANE) & (lane < KR_LANE + MLA_ROPE)
        dqs, dks = [], []
        dgq = jnp.zeros((1, LANES), F32)
        dgk = jnp.zeros((1, LANES), F32)
        dkr = jnp.zeros(kr.shape, F32)
        for h in range(MLA_HEADS):
            sl = slice(h * LANES, (h + 1) * LANES)
            dqn = _rope_t(dq[:, sl].astype(F32), c, s1, s2) * scale
            dx, dg = _rms_bwd(q_pre[:, sl], gq, dqn, MLA_QK)
            dqs.append(dx)
            dgq = dgq + dg
            dkn = _rope_t(dk[:, sl].astype(F32), c, s1, s2)
            dx, dg = _rms_bwd(k_pre[:, sl] + kr, gk, dkn, MLA_QK)
            dks.append(dx)
            dgk = dgk + dg
            dkr = dkr + dx
        dkr = jnp.where(kr_mask, dkr, 0.0)
        dkv = jnp.concatenate(dks + [dv.astype(F32)], axis=1)
        return jnp.concatenate(dqs, axis=1), dkv, dkr, dgq, dgk

    return _rowwise(fn, "mla_post_bwd", 256,
                    [q_pre, (kv_pre, HP, 0), (z, LANES, Z_KR // LANES), *tabs, dq, dk, dv],
                    [gq, gk], [(HP, BF16), (2 * HP, BF16), (LANES, BF16)],
                    [((1, LANES), F32), ((1, LANES), F32)])


def _pairs(n, lower):
    a, b = [], []
    for o in range(n):
        inner = range(o + 1) if lower else range(o, n)
        for t in inner:
            a.append(o)
            b.append(t)
    return jnp.asarray(np.array(a, np.int32)), jnp.asarray(np.array(b, np.int32))


FLASH_TILE, FLASH_SUB_ROWS = 2048, 512
LOG2E, LN2 = 1.4426950408889634, 0.6931471805599453
ONES_LANE = MLA_V


def _flash_tiles(T):
    tq = _tile(T, FLASH_TILE)
    return tq, _tile(tq, FLASH_SUB_ROWS)


def _col_span(t, sr, rb, diag, key_major):
    if not diag:
        return 0, t
    return (rb * sr, t) if key_major else (0, (rb + 1) * sr)


def _span_iota(sr, rb, c0, c1):
    r = lax.broadcasted_iota(jnp.int32, (sr, c1 - c0), 0) + rb * sr
    c = lax.broadcasted_iota(jnp.int32, (sr, c1 - c0), 1) + c0
    return r, c


def _lanes(x, width):
    return jnp.concatenate([x] * (width // LANES), axis=1)


def _flash_fwd(q, k, v):
    T = q.shape[0]
    tq, sr = _flash_tiles(T)
    n = T // tq
    ii, jj = _pairs(n, True)

    def body(ii_ref, jj_ref, q_ref, k_ref, v_ref, o_ref, ot_ref, lse_ref, lset_ref, m_sc, acc_sc):
        p_ = pl.program_id(1)
        i, j = ii_ref[p_], jj_ref[p_]

        @pl.when(j == 0)
        def _():
            m_sc[...] = jnp.full(m_sc.shape, NEG, F32)
            acc_sc[...] = jnp.zeros(acc_sc.shape, F32)

        def tile(diag):
            for rb in range(tq // sr):
                rows = slice(rb * sr, (rb + 1) * sr)
                c0, c1 = _col_span(tq, sr, rb, diag, False)
                s = _dot(q_ref[rows, :], k_ref[c0:c1, :], "nt")
                if diag:
                    r, c = _span_iota(sr, rb, c0, c1)
                    s = jnp.where(c <= r, s, NEG)
                m = m_sc[rows, :]
                m_new = jnp.maximum(m, jnp.max(s, axis=1, keepdims=True))
                p = jnp.exp2(s - _lanes(m_new, c1 - c0))
                acc_sc[rows, :] = jnp.exp2(m - m_new) * acc_sc[rows, :] + _dot(p, v_ref[c0:c1, :])
                m_sc[rows, :] = m_new

        @pl.when(j < i)
        def _():
            tile(False)

        @pl.when(j == i)
        def _():
            tile(True)
            acc = acc_sc[...]
            lane = lax.broadcasted_iota(jnp.int32, acc.shape, 1)
            l = jnp.sum(jnp.where(lane == ONES_LANE, acc, 0.0), axis=1, keepdims=True)
            o = jnp.where(lane < MLA_V, acc / l, 0.0)
            o_ref[...] = o.astype(o_ref.dtype)
            ot_ref[...] = o.T.astype(ot_ref.dtype)
            lse = m_sc[...] + jnp.log2(l)
            lse_ref[...] = lse
            lset_ref[...] = lse.T[:8]

    blk = lambda which: pl.BlockSpec((tq, LANES), which)
    qmap = lambda h, p, ii, jj: (ii[p], h)
    kmap = lambda h, p, ii, jj: (jj[p], h)
    tmap = lambda h, p, ii, jj: (h, ii[p])
    return pl.pallas_call(
        body, name="mla_flash_fwd",
        grid_spec=pltpu.PrefetchScalarGridSpec(
            num_scalar_prefetch=2, grid=(MLA_HEADS, int(ii.shape[0])),
            in_specs=[blk(qmap), blk(kmap), blk(kmap)],
            out_specs=[blk(qmap), pl.BlockSpec((LANES, tq), tmap), blk(qmap), pl.BlockSpec((8, tq), tmap)],
            scratch_shapes=[pltpu.VMEM((tq, LANES), F32)] * 2),
        out_shape=[jax.ShapeDtypeStruct((T, HP), BF16), jax.ShapeDtypeStruct((HP, T), BF16),
                   jax.ShapeDtypeStruct((T, HP), F32), jax.ShapeDtypeStruct((8 * MLA_HEADS, T), F32)],
        compiler_params=_cparams(("parallel", "arbitrary")),
    )(ii, jj, q, k, v)


def _flash_dq(q, k, v, do, lse, delta):
    T = q.shape[0]
    tq, sr = _flash_tiles(T)
    n = T // tq
    ii, jj = _pairs(n, True)

    def body(ii_ref, jj_ref, q_ref, k_ref, v_ref, do_ref, lse_ref, dl_ref, dq_ref, acc_sc):
        p_ = pl.program_id(1)
        i, j = ii_ref[p_], jj_ref[p_]

        @pl.when(j == 0)
        def _():
            acc_sc[...] = jnp.zeros(acc_sc.shape, F32)

        def tile(diag):
            for rb in range(tq // sr):
                rows = slice(rb * sr, (rb + 1) * sr)
                c0, c1 = _col_span(tq, sr, rb, diag, False)
                ks = k_ref[c0:c1, :]
                p = jnp.exp2(_dot(q_ref[rows, :], ks, "nt") - _lanes(lse_ref[rows, :], c1 - c0))
                if diag:
                    r, c = _span_iota(sr, rb, c0, c1)
                    p = jnp.where(c <= r, p, 0.0)
                dp = _dot(do_ref[rows, :], v_ref[c0:c1, :], "nt")
                acc_sc[rows, :] += _dot(p * (dp - _lanes(dl_ref[rows, :], c1 - c0)), ks)

        @pl.when(j < i)
        def _():
            tile(False)

        @pl.when(j == i)
        def _():
            tile(True)
            dq_ref[...] = acc_sc[...].astype(dq_ref.dtype)

    blk = lambda which: pl.BlockSpec((tq, LANES), which)
    qmap = lambda h, p, ii, jj: (ii[p], h)
    kmap = lambda h, p, ii, jj: (jj[p], h)
    return pl.pallas_call(
        body, name="mla_flash_dq",
        grid_spec=pltpu.PrefetchScalarGridSpec(
            num_scalar_prefetch=2, grid=(MLA_HEADS, int(ii.shape[0])),
            in_specs=[blk(qmap), blk(kmap), blk(kmap), blk(qmap), blk(qmap), blk(qmap)],
            out_specs=blk(qmap),
            scratch_shapes=[pltpu.VMEM((tq, LANES), F32)]),
        out_shape=jax.ShapeDtypeStruct((T, HP), BF16),
        compiler_params=_cparams(("parallel", "arbitrary")),
    )(ii, jj, q, k, v, do, lse, delta)


def _flash_dkv(q, k, v, do, lse_t, delta_t):
    T = q.shape[0]
    tq, sr = _flash_tiles(T)
    n = T // tq
    jj, ii = _pairs(n, False)

    def body(jj_ref, ii_ref, q_ref, k_ref, v_ref, do_ref, lse_ref, dl_ref, dk_ref, dv_ref, dk_sc, dv_sc):
        p_ = pl.program_id(1)
        j, i = jj_ref[p_], ii_ref[p_]

        @pl.when(i == j)
        def _():
            dk_sc[...] = jnp.zeros(dk_sc.shape, F32)
            dv_sc[...] = jnp.zeros(dv_sc.shape, F32)

        def tile(diag):
            for rb in range(tq // sr):
                rows = slice(rb * sr, (rb + 1) * sr)
                c0, c1 = _col_span(tq, sr, rb, diag, True)
                qs, dos = q_ref[c0:c1, :], do_ref[c0:c1, :]
                pt = jnp.exp2(_dot(k_ref[rows, :], qs, "nt") - lse_ref[:1, c0:c1])
                if diag:
                    r, c = _span_iota(sr, rb, c0, c1)
                    pt = jnp.where(r <= c, pt, 0.0)
                dpt = _dot(v_ref[rows, :], dos, "nt")
                dv_sc[rows, :] += _dot(pt, dos)
                dk_sc[rows, :] += _dot(pt * (dpt - dl_ref[:1, c0:c1]), qs)

        @pl.when(i == j)
        def _():
            tile(True)

        @pl.when(i > j)
        def _():
            tile(False)

        @pl.when(i == n - 1)
        def _():
            dk_ref[...] = (dk_sc[...] * LN2).astype(dk_ref.dtype)
            dv_ref[...] = dv_sc[...].astype(dv_ref.dtype)

    blk = lambda which: pl.BlockSpec((tq, LANES), which)
    qmap = lambda h, p, jj, ii: (ii[p], h)
    kmap = lambda h, p, jj, ii: (jj[p], h)
    lse_rows = pl.BlockSpec((8, tq), lambda h, p, jj, ii: (h, ii[p]))
    delta_rows = pl.BlockSpec((8, tq), lambda h, p, jj, ii: (h * (LANES // 8), ii[p]))
    return pl.pallas_call(
        body, name="mla_flash_dkv",
        grid_spec=pltpu.PrefetchScalarGridSpec(
            num_scalar_prefetch=2, grid=(MLA_HEADS, int(ii.shape[0])),
            in_specs=[blk(qmap), blk(kmap), blk(kmap), blk(qmap), lse_rows, delta_rows],
            out_specs=[blk(kmap), blk(kmap)],
            scratch_shapes=[pltpu.VMEM((tq, LANES), F32)] * 2),
        out_shape=[jax.ShapeDtypeStruct((T, HP), BF16)] * 2,
        compiler_params=_cparams(("parallel", "arbitrary")),
    )(jj, ii, q, k, v, do, lse_t, delta_t)


def _mem_fwd(z, km, vm, gq):
    scale = MEM_HEAD_DIM ** -0.5

    def fn(qm, km, vm, gq):
        ys = []
        for h in range(MEM_HEADS):
            sl = slice(h * LANES, (h + 1) * LANES)
            q = _rms(qm[:, sl], gq) * scale
            s = _dot(q, km[:, sl], "nt")
            p = jnp.exp(s - jnp.max(s, axis=1, keepdims=True))
            p = p / jnp.sum(p, axis=1, keepdims=True)
            ys.append(_dot(p, vm[:, sl]))
        y = jnp.concatenate(ys, axis=1)
        return y, y

    return _rowwise(fn, "mem_fwd", 512, [(z, MEM_WIDTH, Z_QM // MEM_WIDTH)], [km, vm, gq],
                    [(MEM_WIDTH, BF16), (MEM_WIDTH, BF16, "T")])


def _mem_bwd(z, dy, km, vm, gq, dz):
    scale = MEM_HEAD_DIM ** -0.5

    def fn(qm, dy, km, vm, gq):
        dqs, dks, dvs = [], [], []
        dgq = jnp.zeros((1, LANES), F32)
        for h in range(MEM_HEADS):
            sl = slice(h * LANES, (h + 1) * LANES)
            q = (_rms(qm[:, sl], gq) * scale).astype(BF16)
            dyh = dy[:, sl]
            kh, vh = km[:, sl], vm[:, sl]
            s = _dot(q, kh, "nt")
            p = jnp.exp(s - jnp.max(s, axis=1, keepdims=True))
            p = p / jnp.sum(p, axis=1, keepdims=True)
            dp = _dot(dyh, vh, "nt")
            ds = p * (dp - jnp.sum(p * dp, axis=1, keepdims=True))
            dq = _dot(ds, kh) * scale
            dx, dg = _rms_bwd(qm[:, sl], gq, dq)
            dqs.append(dx)
            dgq = dgq + dg
            st = _dot(kh, q, "nt")
            pt = jnp.exp(st - jnp.max(st, axis=0, keepdims=True))
            pt = pt / jnp.sum(pt, axis=0, keepdims=True)
            dpt = _dot(vh, dyh, "nt")
            dst = pt * (dpt - jnp.sum(pt * dpt, axis=0, keepdims=True))
            dvs.append(_dot(pt, dyh))
            dks.append(_dot(dst, q))
        return jnp.concatenate(dqs, axis=1), jnp.concatenate(dks, axis=1), jnp.concatenate(dvs, axis=1), dgq

    m = km.shape[0]
    return _rowwise(fn, "mem_bwd", 512, [(z, MEM_WIDTH, Z_QM // MEM_WIDTH), dy], [km, vm, gq],
                    [("into", dz, MEM_WIDTH, Z_QM // MEM_WIDTH)],
                    [((m, MEM_WIDTH), F32), ((m, MEM_WIDTH), F32), ((1, LANES), F32)])


GROUPS = {"ffn1": ["ffn1_w_gu"], "ffn1_down": ["ffn1_w_down"],
          "mix": ["w_in", "mla_w_uq", "mla_w_ukv", "mem_w_kv", "w_branch_a", "w_branch_b", "w_branch_c", "w_out"],
          "ffn2": ["ffn2_w_gu", "ffn2_w_down"]}
GRAD_GROUPS = {"ffn2": GROUPS["ffn2"], "mix": GROUPS["mix"], "ffn1_down": ["ffn1_w_down"], "ffn1_gu": ["ffn1_w_gu"]}


def _local_step(x, mem, positions, loss_target, P, weights, grads_out):
    T = x.shape[0]
    G = {}
    W = dict(weights("ffn1", None))

    half = MLA_ROPE // 2
    inv = ROPE_BASE ** (-jnp.arange(half, dtype=F32) / half)
    ang = positions.astype(F32)[:, None] * inv
    cos, sin = jnp.cos(ang), jnp.sin(ang)
    one, zero = jnp.ones((T, MLA_NOPE), F32), jnp.zeros((T, half), F32)
    pad = LANES - MLA_QK
    tabs = (jnp.concatenate([one, cos, cos, jnp.ones((T, pad), F32)], axis=1),
            jnp.concatenate([jnp.zeros((T, MLA_NOPE), F32), -sin, zero, jnp.zeros((T, pad), F32)], axis=1),
            jnp.concatenate([jnp.zeros((T, MLA_NOPE), F32), zero, sin, jnp.zeros((T, pad), F32)], axis=1))
    gq_p = jnp.pad(P["mla_q_norm"], ((0, 0), (0, pad)))
    gk_p = jnp.pad(P["mla_k_norm"], ((0, 0), (0, pad)))
    bias_full = jnp.repeat(P["sg_b"].T, SG_GROUP_DIM, axis=1)
    group_ind = jnp.repeat(jnp.eye(SG_GROUPS, dtype=F32), SG_GROUP_DIM, axis=0)

    HT = (D_MODEL, BF16, "T")

    def norm2(x, g):
        h = _rms(x, g)
        return h, h

    h1, h1t = _rowwise(norm2, "ffn1_norm", 512, [x], [P["ffn1_norm"]], [(D_MODEL, BF16), HT])
    def ffn1_w_down(after):
        W.update(weights("ffn1_down", after))
        return W["ffn1_w_down"]

    gu1, a1t, o1 = _ffn_fwd(h1, W["ffn1_w_gu"], ffn1_w_down, "ffn1")

    def resid_norm(x, o, g):
        xn = x + 0.5 * o
        h = _rms(xn, g)
        return xn, h, h

    x1, hm, hmt = _rowwise(resid_norm, "mix_norm", 512, [x, o1], [P["mix_norm"]],
                           [(D_MODEL, F32), (D_MODEL, BF16), HT])
    W.update(weights("mix", hm))
    z = _mm(hm, W["w_in"], "nn", BF16, "w_in", tm=1024, tn=1792)

    y_a, y_at = _sg_fwd(z, P["sg_ln_g"], P["sg_ln_b"], P["sg_w"], bias_full)

    def c_norm(cq, ckv, gq, gkv):
        a, b = _rms(cq, gq), _rms(ckv, gkv)
        return a, b, a, b

    cqn, ckvn, cqnt, ckvnt = _rowwise(
        c_norm, "mla_cnorm", 512, [(z, MLA_Q_RANK, Z_CQ // MLA_Q_RANK), (z, MLA_KV_RANK, Z_CKV // MLA_KV_RANK)],
        [P["mla_cq_norm"], P["mla_ckv_norm"]],
        [(MLA_Q_RANK, BF16), (MLA_KV_RANK, BF16), (MLA_Q_RANK, BF16, "T"), (MLA_KV_RANK, BF16, "T")])
    q_pre = _mm(cqn, W["mla_w_uq"], "nn", BF16, "mla_uq", tm=1024, tn=1024)
    kv_pre = _mm(ckvn, W["mla_w_ukv"], "nn", BF16, "mla_ukv", tm=1024, tn=1024)
    q, k, v = _mla_post(q_pre, kv_pre, z, tabs, gq_p, gk_p)
    y_b, y_bt, lse, lse_t = _flash_fwd(q, k, v)

    memn, = _rowwise(lambda m, g: _rms(m, g), "mem_norm", 256, [mem], [P["mem_norm"]], [(D_MODEL, BF16)])
    kvm = _mm(memn, W["mem_w_kv"], "nn", F32, "mem_kv")

    def mem_k(kvm, gk):
        ks = [_rms(kvm[:, h * LANES:(h + 1) * LANES], gk) for h in range(MEM_HEADS)]
        return jnp.concatenate(ks, axis=1), kvm[:, MEM_WIDTH:]

    km, vm = _rowwise(mem_k, "mem_knorm", 256, [kvm], [P["mem_k_norm"]], [(MEM_WIDTH, BF16), (MEM_WIDTH, BF16)])
    y_c, y_ct = _mem_fwd(z, km, vm, P["mem_q_norm"])

    pa = _mm(y_a, W["w_branch_a"], "nn", BF16, "branch_a", tm=1024, tn=1024)
    pb = _mm(y_b, W["w_branch_b"], "nn", BF16, "branch_b", tm=1024, tn=1024)
    pc = _mm(y_c, W["w_branch_c"], "nn", BF16, "branch_c", tm=1024, tn=1024)

    def merge(zg, pa, pb, pc, b):
        g = _sigmoid(zg + b)
        m = g[:, :D_MODEL] * pa + g[:, D_MODEL:2 * D_MODEL] * pb + g[:, 2 * D_MODEL:] * pc
        return m, m

    merged, mergedt = _rowwise(merge, "merge", 256, [(z, 3 * D_MODEL, 0), pa, pb, pc], [P["b_gate"]],
                               [(D_MODEL, BF16), HT])
    om = _mm(merged, W["w_out"], "nn", BF16, "w_out", tm=1024, tn=1024)

    def resid_norm1(x, o, g):
        xn = x + o
        h = _rms(xn, g)
        return xn, h, h

    x2, h2, h2t = _rowwise(resid_norm1, "ffn2_norm", 512, [x1, om], [P["ffn2_norm"]],
                           [(D_MODEL, F32), (D_MODEL, BF16), HT])
    W.update(weights("ffn2", h2))
    gu2, a2t, o2 = _ffn_fwd(h2, W["ffn2_w_gu"], W["ffn2_w_down"], "ffn2")

    def loss_fn(x2, o2, t):
        e = x2 + 0.5 * o2 - t
        return e * (1.0 / D_MODEL), (e * (0.5 / D_MODEL)).astype(BF16), _rsum(e * e) * (0.5 / D_MODEL)

    dx3, do2, loss_part = _rowwise(loss_fn, "loss", 512, [x2, o2, loss_target], [],
                                   [(D_MODEL, F32), (D_MODEL, BF16)], [((1, D_MODEL), F32)])

    dh2, G["ffn2_w_gu"], G["ffn2_w_down"] = _ffn_bwd(do2, h2t, gu2, a2t, W["ffn2_w_gu"], W["ffn2_w_down"], "ffn2")
    tie = grads_out("ffn2", G)

    def norm_bwd(x, dh, dxo, g, *_):
        dx, dg = _rms_bwd(x, g, dh)
        dx = dx + dxo
        return dx, dx, dg

    dx2, dx2b, G["ffn2_norm"] = _rowwise(norm_bwd, "ffn2_norm_bwd", 512, [x2, dh2, dx3],
                                         [P["ffn2_norm"]] + ([] if tie is None else [tie]),
                                         [(D_MODEL, F32), (D_MODEL, BF16)], [((1, D_MODEL), F32)])

    G["w_out"] = _mm_t(mergedt, dx2b, "w_out_dw", tm=1024, tn=1024)
    dmerged = _mm(dx2b, W["w_out"], "nt", BF16, "w_out_dx", tm=1024, tn=1024)

    def merge_bwd(zg, pa, pb, pc, dm, b):
        g = _sigmoid(zg + b)
        ps = jnp.concatenate([pa, pb, pc], axis=1)
        dm3 = jnp.concatenate([dm, dm, dm], axis=1)
        dzg = dm3 * ps * g * (1.0 - g)
        dp = dm3 * g
        return dzg, dp[:, :D_MODEL], dp[:, D_MODEL:2 * D_MODEL], dp[:, 2 * D_MODEL:], _rsum(dzg)

    dz = lax.empty((T, Z_COLS), BF16)
    dz, dpa, dpb, dpc, G["b_gate"] = _rowwise(
        merge_bwd, "merge_bwd", 256, [(z, 3 * D_MODEL, 0), pa, pb, pc, dmerged], [P["b_gate"]],
        [("into", dz, 3 * D_MODEL, 0), (D_MODEL, BF16), (D_MODEL, BF16), (D_MODEL, BF16)], [((1, 3 * D_MODEL), F32)])

    G["w_branch_a"] = _mm_t(y_at, dpa, "branch_a_dw", tm=512, tn=1024)
    G["w_branch_b"] = _mm_t(y_bt, dpb, "branch_b_dw", tm=1024, tn=1024)
    G["w_branch_c"] = _mm_t(y_ct, dpc, "branch_c_dw", tm=512, tn=1024)
    dy_a = _mm(dpa, W["w_branch_a"], "nt", BF16, "branch_a_dx", tm=1024, tn=512)
    dy_b = _mm(dpb, W["w_branch_b"], "nt", BF16, "branch_b_dx", tm=1024, tn=1024)
    dy_c = _mm(dpc, W["w_branch_c"], "nt", BF16, "branch_c_dx", tm=1024, tn=512)

    dz, G["sg_w"], dbias_t, G["sg_ln_g"], G["sg_ln_b"] = _sg_bwd(
        z, dy_a, P["sg_ln_g"], P["sg_ln_b"], P["sg_w"], bias_full, group_ind, dz)
    G["sg_b"] = dbias_t.T

    dz, dkm, dvm, G["mem_q_norm"] = _mem_bwd(z, dy_c, km, vm, P["mem_q_norm"], dz)

    def mem_k_bwd(kvm, dkm, dvm, gk):
        dks = []
        dg = jnp.zeros((1, LANES), F32)
        for h in range(MEM_HEADS):
            sl = slice(h * LANES, (h + 1) * LANES)
            dx, d = _rms_bwd(kvm[:, sl], gk, dkm[:, sl])
            dks.append(dx)
            dg = dg + d
        return jnp.concatenate(dks + [dvm], axis=1), dg

    dkvm, G["mem_k_norm"] = _rowwise(mem_k_bwd, "mem_knorm_bwd", 256, [kvm, dkm, dvm], [P["mem_k_norm"]],
                                     [(2 * MEM_WIDTH, BF16)], [((1, LANES), F32)])
    G["mem_w_kv"] = _mm(memn, dkvm, "tn", BF16, "mem_kv_dw")
    dmemn = _mm(dkvm, W["mem_w_kv"], "nt", F32, "mem_kv_dx")
    _, G["mem_norm"] = _rowwise(lambda m, d, g: _rms_bwd(m, g, d), "mem_norm_bwd", 256, [mem, dmemn],
                                [P["mem_norm"]], [(D_MODEL, BF16)], [((1, D_MODEL), F32)])

    def delta_fn(o, do):
        od = o.astype(F32) * do.astype(F32)
        ds = [jnp.broadcast_to(jnp.sum(od[:, h * LANES:(h + 1) * LANES], axis=1, keepdims=True), (od.shape[0], LANES))
              for h in range(MLA_HEADS)]
        d = jnp.concatenate(ds, axis=1)
        return d, d

    delta, delta_t = _rowwise(delta_fn, "mla_delta", 512, [y_b, dy_b], [], [(HP, F32), (HP, F32, "T")])
    dq = _flash_dq(q, k, v, dy_b, lse, delta)
    dk, dv = _flash_dkv(q, k, v, dy_b, lse_t, delta_t)
    dq_pre, dkv_pre, dkr, dgq, dgk = _mla_post_bwd(q_pre, kv_pre, z, tabs, gq_p, gk_p, dq, dk, dv)
    G["mla_q_norm"], G["mla_k_norm"] = dgq[:, :MLA_QK], dgk[:, :MLA_QK]
    G["mla_w_uq"] = _mm_t(cqnt, dq_pre, "mla_uq_dw", tm=384, tn=1024)
    G["mla_w_ukv"] = _mm_t(ckvnt, dkv_pre, "mla_ukv_dw", tm=256, tn=2048)
    dcqn = _mm(dq_pre, W["mla_w_uq"], "nt", BF16, "mla_uq_dx", tm=1024)
    dckvn = _mm(dkv_pre, W["mla_w_ukv"], "nt", BF16, "mla_ukv_dx", tm=1024)

    def c_norm_bwd(cq, ckv, dcqn, dckvn, dkr, gq, gkv):
        dcq, dgq = _rms_bwd(cq, gq, dcqn)
        dckv, dgkv = _rms_bwd(ckv, gkv, dckvn)
        return jnp.concatenate([dckv, dkr, dcq], axis=1), dgq, dgkv

    tail = Z_COLS - Z_CKV
    dz, G["mla_cq_norm"], G["mla_ckv_norm"] = _rowwise(
        c_norm_bwd, "mla_cnorm_bwd", 512,
        [(z, MLA_Q_RANK, Z_CQ // MLA_Q_RANK), (z, MLA_KV_RANK, Z_CKV // MLA_KV_RANK), dcqn, dckvn, dkr],
        [P["mla_cq_norm"], P["mla_ckv_norm"]], [("into", dz, tail, Z_CKV // tail)],
        [((1, MLA_Q_RANK), F32), ((1, MLA_KV_RANK), F32)])
    G["w_in"] = _mm_t(hmt, dz, "w_in_dw", tm=1024, tn=1792, tk=2048)
    dhm = _mm(dz, W["w_in"], "nt", BF16, "w_in_dx", tm=1024, tn=1024, tk=2688)

    def norm_bwd_half(x, dh, dxo, g):
        dx, dg = _rms_bwd(x, g, dh)
        dx = dx + dxo
        return dx, (0.5 * dx), dg

    dx1, do1, G["mix_norm"] = _rowwise(norm_bwd_half, "mix_norm_bwd", 512, [x1, dhm, dx2], [P["mix_norm"]],
                                       [(D_MODEL, F32), (D_MODEL, BF16)], [((1, D_MODEL), F32)])
    tie = grads_out("mix", G)

    def ffn1_dw(which, dw):
        G["ffn1_w_" + which] = dw
        return grads_out("ffn1_" + which, G)

    dh1, _, _ = _ffn_bwd(do1, h1t, gu1, a1t, W["ffn1_w_gu"], W["ffn1_w_down"], "ffn1", tie, ffn1_dw)

    def norm_bwd_last(x, dh, dxo, g):
        dx, dg = _rms_bwd(x, g, dh)
        return dx + dxo, dg

    grad_x, G["ffn1_norm"] = _rowwise(norm_bwd_last, "ffn1_norm_bwd", 512, [x, dh1, dx1], [P["ffn1_norm"]],
                                      [(D_MODEL, F32)], [((1, D_MODEL), F32)])
    return loss_part, grad_x, G


SHARDED = ["ffn1_w_gu", "ffn1_w_down", "w_in", "mla_w_uq", "mla_w_ukv", "mem_w_kv",
           "w_branch_a", "w_branch_b", "w_branch_c", "w_out", "ffn2_w_gu", "ffn2_w_down"]
ROW_SHARDED = {"ffn1_w_down", "mem_w_kv", "w_out", "ffn2_w_down"}
SMALL = ["ffn1_norm", "mix_norm", "b_gate", "sg_ln_g", "sg_ln_b", "sg_w", "sg_b", "mla_cq_norm",
         "mla_ckv_norm", "mla_q_norm", "mla_k_norm", "mem_norm", "mem_q_norm", "mem_k_norm", "ffn2_norm"]
ORDER = ["ffn1_norm", "ffn1_w_gu", "ffn1_w_down", "mix_norm", "w_in", "b_gate", "sg_ln_g", "sg_ln_b", "sg_w",
         "sg_b", "mla_cq_norm", "mla_w_uq", "mla_ckv_norm", "mla_w_ukv", "mla_q_norm", "mla_k_norm", "mem_norm",
         "mem_w_kv", "mem_q_norm", "mem_k_norm", "w_branch_a", "w_branch_b", "w_branch_c", "w_out", "ffn2_norm",
         "ffn2_w_gu", "ffn2_w_down"]

_IN_U, _IN_V, _IN_CQ, _IN_CKV, _IN_KR, _IN_QM, _IN_G = 0, 512, 1024, 1408, 1664, 1696, 2208
IN_COLS = 5280


def _full_from_slabs(name, slabs):
    n, r, c = slabs.shape
    if name in ROW_SHARDED:
        return slabs.reshape(n * r, c)
    return slabs.transpose(1, 0, 2).reshape(r, n * c)


def _slabs_from_full(name, full):
    if name in ROW_SHARDED:
        return full.reshape(N_DEV, full.shape[0] // N_DEV, full.shape[1])
    r, c = full.shape
    return full.reshape(r, N_DEV, c // N_DEV).transpose(1, 0, 2)


def _compute_layout(full):
    W = dict(full)
    if "w_in" not in full:
        return W
    w = full["w_in"]
    kr = jnp.pad(w[:, _IN_KR:_IN_QM], ((0, 0), (KR_LANE, LANES - KR_LANE - MLA_ROPE)))
    W["w_in"] = jnp.concatenate([w[:, _IN_G:], w[:, _IN_U:_IN_CQ], w[:, _IN_QM:_IN_G], w[:, _IN_CKV:_IN_KR], kr,
                                 w[:, _IN_CQ:_IN_CKV]], axis=1)
    uq = full["mla_w_uq"].reshape(MLA_Q_RANK, MLA_HEADS, MLA_QK)
    W["mla_w_uq"] = jnp.pad(uq, ((0, 0), (0, 0), (0, LANES - MLA_QK))).reshape(MLA_Q_RANK, HP)
    ukv = full["mla_w_ukv"].reshape(MLA_KV_RANK, MLA_HEADS, MLA_NOPE + MLA_V)
    padh = lambda a: jnp.pad(a, ((0, 0), (0, 0), (0, LANES - a.shape[2]))).reshape(MLA_KV_RANK, HP)
    W["mla_w_ukv"] = jnp.concatenate([padh(ukv[:, :, :MLA_NOPE]), padh(ukv[:, :, MLA_NOPE:])], axis=1)
    wb = full["w_branch_b"].reshape(MLA_HEADS, MLA_V, D_MODEL)
    W["w_branch_b"] = jnp.pad(wb, ((0, 0), (0, LANES - MLA_V), (0, 0))).reshape(HP, D_MODEL)
    return W


def _reference_layout(G):
    out = dict(G)
    if "w_in" not in G:
        return out
    g = G["w_in"]
    out["w_in"] = jnp.concatenate([
        g[:, Z_U:Z_QM], g[:, Z_CQ:Z_COLS], g[:, Z_CKV:Z_KR], g[:, Z_KR + KR_LANE:Z_KR + KR_LANE + MLA_ROPE],
        g[:, Z_QM:Z_CKV], g[:, Z_G:Z_U]], axis=1)
    out["mla_w_uq"] = G["mla_w_uq"].reshape(MLA_Q_RANK, MLA_HEADS, LANES)[:, :, :MLA_QK].reshape(MLA_Q_RANK, -1)
    gk = G["mla_w_ukv"][:, :HP].reshape(MLA_KV_RANK, MLA_HEADS, LANES)[:, :, :MLA_NOPE]
    gv = G["mla_w_ukv"][:, HP:].reshape(MLA_KV_RANK, MLA_HEADS, LANES)[:, :, :MLA_V]
    out["mla_w_ukv"] = jnp.concatenate([gk, gv], axis=2).reshape(MLA_KV_RANK, -1)
    out["w_branch_b"] = G["w_branch_b"].reshape(MLA_HEADS, LANES, D_MODEL)[:, :MLA_V].reshape(-1, D_MODEL)
    return out


def _pack(parts):
    flat = []
    for a in parts:
        a = a.reshape(-1)
        flat.append(jnp.pad(a, (0, (-a.shape[0]) % LANES)))
    return jnp.concatenate(flat).reshape(-1, LANES)


def _unpack(packed, shapes):
    flat = packed.reshape(-1)
    out, off = [], 0
    for shp in shapes:
        n = int(np.prod(shp))
        out.append(flat[off:off + n].reshape(shp))
        off += n + (-n) % LANES
    return out


MESH = pl.DeviceIdType.MESH
HBM = pl.BlockSpec(memory_space=pltpu.HBM)


def _all_gather(shards):
    n = len(shards)

    def body(*refs):
        x_refs, out_refs, token_ref = refs[:n], refs[n:2 * n], refs[2 * n]
        send_sems, recv_sems, local_sems = refs[2 * n + 1:]
        x, y, c = lax.axis_index("x"), lax.axis_index("y"), lax.axis_index("c")
        me, sibling = (x, y, c), (x, y, 1 - c)
        chips = [(1 - x, y), (x, 1 - y), (1 - x, 1 - y)]
        token_ref[...] = jnp.zeros_like(token_ref)

        def slot(a, px, py, pc):
            return out_refs[a].at[4 * px + 2 * py + pc]

        def copy(a, k, block, to, src=None):
            return pltpu.make_async_remote_copy(
                src_ref=slot(a, *block) if src is None else src, dst_ref=slot(a, *block),
                send_sem=send_sems.at[7 * a + k], recv_sem=recv_sems.at[7 * a + k], device_id=to, device_id_type=MESH)

        arrays = range(n)
        mine = [pltpu.make_async_copy(x_refs[a], slot(a, *me), local_sems.at[a]) for a in arrays]
        for cp in mine:
            cp.start()
        first = [copy(a, 0, me, sibling, src=x_refs[a]) for a in arrays]
        first += [copy(a, 1 + j, me, (*chip, c), src=x_refs[a]) for j, chip in enumerate(chips) for a in arrays]
        for cp in first:
            cp.start()
        passed = []
        for j, chip in enumerate(chips):
            for a in arrays:
                copy(a, 1 + j, (*chip, c), me).wait_recv()
                passed.append(copy(a, 4 + j, (*chip, c), sibling))
                passed[-1].start()
        for a in arrays:
            copy(a, 0, sibling, me).wait_recv()
        for j, chip in enumerate(chips):
            for a in arrays:
                copy(a, 4 + j, (*chip, 1 - c), me).wait_recv()
        for cp in first + passed:
            cp.wait_send()
        for cp in mine:
            cp.wait()

    res = pl.pallas_call(
        body, name="all_gather_weights",
        out_shape=[jax.ShapeDtypeStruct((N_DEV,) + s.shape, s.dtype) for s in shards]
        + [jax.ShapeDtypeStruct((8, LANES), F32)],
        in_specs=[HBM] * n, out_specs=[HBM] * n + [pl.BlockSpec(memory_space=pltpu.VMEM)],
        scratch_shapes=[pltpu.SemaphoreType.DMA((7 * n,)), pltpu.SemaphoreType.DMA((7 * n,)),
                        pltpu.SemaphoreType.DMA((n,))],
    )(*shards)
    return res[:n], res[n]


SEM = pl.BlockSpec(memory_space=pltpu.SEMAPHORE)
DATAFLOW = pltpu.SideEffectType.DATAFLOW_SIDE_EFFECTING


def _peers():
    x, y, c = lax.axis_index("x"), lax.axis_index("y"), lax.axis_index("c")
    out = []
    for k in range(1, N_DEV):
        px = 1 - x if k & 4 else x
        py = 1 - y if k & 2 else y
        pc = 1 - c if k & 1 else c
        out.append((k, (px, py, pc), 4 * px + 2 * py + pc))
    return 4 * x + 2 * y + c, out


def _send_start(srcs, per_peer, name):
    n = len(srcs)
    lands = [lax.empty((N_DEV,) + (s.shape[1:] if per_peer else s.shape), s.dtype) for s in srcs]

    def body(*refs):
        src_refs, land_refs, send_sems, recv_sems, token = refs[:n], refs[n:2 * n], refs[2 * n], refs[2 * n + 1], refs[-1]
        me, peers = _peers()
        for a in range(n):
            for k, pid, pflat in peers:
                pltpu.make_async_remote_copy(
                    src_ref=src_refs[a].at[pflat] if per_peer else src_refs[a], dst_ref=land_refs[a].at[me],
                    send_sem=send_sems.at[7 * a + k - 1], recv_sem=recv_sems.at[7 * a + k - 1],
                    device_id=pid, device_id_type=MESH).start()
        token[...] = jnp.zeros_like(token)

    hbm = lambda a: pltpu.with_memory_space_constraint(a, pltpu.HBM)
    res = pl.pallas_call(
        body, name=name,
        out_shape=(pltpu.SemaphoreType.DMA((7 * n,)), pltpu.SemaphoreType.DMA((7 * n,)),
                   *[pltpu.HBM(a.shape, a.dtype) for a in srcs + lands], jax.ShapeDtypeStruct((8, LANES), F32)),
        in_specs=(HBM,) * (2 * n), out_specs=(SEM, SEM) + (HBM,) * (2 * n) + (pl.BlockSpec(memory_space=pltpu.VMEM),),
        input_output_aliases={i: 2 + i for i in range(2 * n)},
        compiler_params=pltpu.CompilerParams(has_side_effects=DATAFLOW),
    )(*[hbm(a) for a in srcs + lands])
    return (res[0], res[1], list(res[2:2 + n]), list(res[2 + n:2 + 2 * n])), res[-1]


def _send_wait(started, after, per_peer, name):
    send_sems, recv_sems, srcs_thru, lands_thru = started
    n = len(srcs_thru)

    def body(*refs):
        src_refs, land_refs, send_sems, recv_sems = refs[:n], refs[n:2 * n], refs[2 * n], refs[2 * n + 1]
        me, peers = _peers()
        for a in range(n):
            for k, pid, pflat in peers:
                copy = pltpu.make_async_remote_copy(
                    src_ref=src_refs[a].at[pflat] if per_peer else src_refs[a], dst_ref=land_refs[a].at[pflat],
                    send_sem=send_sems.at[7 * a + k - 1], recv_sem=recv_sems.at[7 * a + k - 1],
                    device_id=pid, device_id_type=MESH)
                copy.wait_send()
                copy.wait_recv()

    outs = pl.pallas_call(
        body, name=name,
        out_shape=tuple(pltpu.HBM(a.shape, a.dtype) for a in srcs_thru + lands_thru),
        in_specs=(HBM,) * (2 * n) + (SEM, SEM, pl.BlockSpec(memory_space=pl.ANY)), out_specs=(HBM,) * (2 * n),
        input_output_aliases={i: i for i in range(2 * n)},
        compiler_params=pltpu.CompilerParams(has_side_effects=DATAFLOW),
    )(*srcs_thru, *lands_thru, send_sems, recv_sems, after)
    me = 4 * lax.axis_index("x") + 2 * lax.axis_index("y") + lax.axis_index("c")
    landed = []
    for src_out, land in zip(outs[:n], outs[n:]):
        own = lax.dynamic_index_in_dim(src_out, me, 0, keepdims=True) if per_peer else src_out[None]
        landed.append(lax.dynamic_update_slice(land, own, (me,) + (0,) * (land.ndim - 1)))
    return landed


def _share_rows(block, name):
    def body(src_ref, out_ref, send_sems, recv_sems, local_sem):
        me, peers = _peers()
        own = pltpu.make_async_copy(src_ref, out_ref.at[me], local_sem)
        own.start()
        copies = [pltpu.make_async_remote_copy(
            src_ref=src_ref, dst_ref=out_ref.at[me], send_sem=send_sems.at[k - 1], recv_sem=recv_sems.at[k - 1],
            device_id=pid, device_id_type=MESH) for k, pid, _ in peers]
        for cp in copies:
            cp.start()
        for cp in copies:
            cp.wait()
        own.wait()

    return pl.pallas_call(
        body, name=name, out_shape=jax.ShapeDtypeStruct((N_DEV,) + block.shape, block.dtype),
        in_specs=[HBM], out_specs=HBM,
        scratch_shapes=[pltpu.SemaphoreType.DMA((N_DEV - 1,)), pltpu.SemaphoreType.DMA((N_DEV - 1,)),
                        pltpu.SemaphoreType.DMA],
    )(block)


def _sum_slots(recv, name, tr):
    n, rows, lanes = recv.shape
    tr = _tile(rows, tr)

    def body(r_ref, o_ref):
        acc = r_ref[0].astype(F32)
        for i in range(1, n):
            acc = acc + r_ref[i].astype(F32)
        o_ref[...] = acc

    return pl.pallas_call(
        body, name=name, grid=(rows // tr,),
        in_specs=[pl.BlockSpec((n, tr, lanes), lambda i: (0, i, 0))],
        out_specs=pl.BlockSpec((tr, lanes), lambda i: (i, 0)),
        out_shape=jax.ShapeDtypeStruct((rows, lanes), F32),
        compiler_params=_cparams(("parallel",)),
    )(recv)


def _adamw_math(w, g, m, v):
    m = ADAM_B1 * m + (1.0 - ADAM_B1) * g
    v = ADAM_B2 * v + (1.0 - ADAM_B2) * (g * g)
    m_hat = m / (1.0 - ADAM_B1 ** ADAM_STEP)
    v_hat = v / (1.0 - ADAM_B2 ** ADAM_STEP)
    return -ADAM_LR * (m_hat / (jnp.sqrt(v_hat) + ADAM_EPS) + ADAM_WD * w), m, v


def _adamw(w, g, m, v, name, tr=256):
    return _rowwise(_adamw_math, name, tr, [w, g, m, v], [], [(w.shape[1], F32)] * 3)


def _adamw_small(ws, gs, ms, vs):
    n = len(ws)

    def body(*refs):
        ins, outs = refs[:4 * n], refs[4 * n:]
        for i in range(n):
            d, m, v = _adamw_math(ins[i][...], ins[n + i][...], ins[2 * n + i][...], ins[3 * n + i][...])
            outs[i][...], outs[n + i][...], outs[2 * n + i][...] = d, m, v

    vmem = pl.BlockSpec(memory_space=pltpu.VMEM)
    res = pl.pallas_call(
        body, name="adamw_small", in_specs=[vmem] * (4 * n), out_specs=[vmem] * (3 * n),
        out_shape=[jax.ShapeDtypeStruct(w.shape, F32) for w in ws] * 3,
    )(*ws, *gs, *ms, *vs)
    return res[:n], res[n:2 * n], res[2 * n:]


def _sum_adamw(recv, w, m, v, name):
    n, r, c = recv.shape
    tr = _tile(r, 256)

    def body(r_ref, w_ref, m_ref, v_ref, g_ref, d_ref, nm_ref, nv_ref):
        g = r_ref[0].astype(F32)
        for i in range(1, n):
            g = g + r_ref[i].astype(F32)
        g_ref[...] = g
        d_ref[...], nm_ref[...], nv_ref[...] = _adamw_math(w_ref[...], g, m_ref[...], v_ref[...])

    row = pl.BlockSpec((None, tr, c), lambda i: (0, i, 0))
    return pl.pallas_call(
        body, name=name, grid=(r // tr,),
        in_specs=[pl.BlockSpec((n, tr, c), lambda i: (0, i, 0)), row, row, row], out_specs=[row] * 4,
        out_shape=[jax.ShapeDtypeStruct((1, r, c), F32)] * 4, compiler_params=_cparams(("parallel",)),
    )(recv, w, m, v)


def kernel(x, mem, positions, ffn1_norm, ffn1_w_gu, ffn1_w_down, mix_norm, w_in, b_gate, sg_ln_g, sg_ln_b, sg_w, sg_b, mla_cq_norm, mla_w_uq, mla_ckv_norm, mla_w_ukv, mla_q_norm, mla_k_norm, mem_norm, mem_w_kv, mem_q_norm, mem_k_norm, w_branch_a, w_branch_b, w_branch_c, w_out, ffn2_norm, ffn2_w_gu, ffn2_w_down, loss_target, m_ffn1_norm, m_ffn1_w_gu, m_ffn1_w_down, m_mix_norm, m_w_in, m_b_gate, m_sg_ln_g, m_sg_ln_b, m_sg_w, m_sg_b, m_mla_cq_norm, m_mla_w_uq, m_mla_ckv_norm, m_mla_w_ukv, m_mla_q_norm, m_mla_k_norm, m_mem_norm, m_mem_w_kv, m_mem_q_norm, m_mem_k_norm, m_w_branch_a, m_w_branch_b, m_w_branch_c, m_w_out, m_ffn2_norm, m_ffn2_w_gu, m_ffn2_w_down, v_ffn1_norm, v_ffn1_w_gu, v_ffn1_w_down, v_mix_norm, v_w_in, v_b_gate, v_sg_ln_g, v_sg_ln_b, v_sg_w, v_sg_b, v_mla_cq_norm, v_mla_w_uq, v_mla_ckv_norm, v_mla_w_ukv, v_mla_q_norm, v_mla_k_norm, v_mem_norm, v_mem_w_kv, v_mem_q_norm, v_mem_k_norm, v_w_branch_a, v_w_branch_b, v_w_branch_c, v_w_out, v_ffn2_norm, v_ffn2_w_gu, v_ffn2_w_down):
    given = dict(locals())
    wts = {n: given[n] for n in ORDER}
    mom = {n: given["m_" + n] for n in ORDER}
    var = {n: given["v_" + n] for n in ORDER}

    def shards(group, zero):
        out = [wts[n][0].astype(BF16) for n in GROUPS[group]]
        return [out[0] + zero.astype(BF16)] + out[1:]

    def full_weights(group, slabs):
        return _compute_layout({n: _full_from_slabs(n, s) for n, s in zip(GROUPS[group], slabs)})

    def zero_of(a):
        return jnp.minimum(jnp.abs(a.reshape(-1)[0]), 0)

    gathered_ffn1, token = _all_gather([wts[n][0].astype(BF16) for n in GROUPS["ffn1"]])
    flight = {}
    flight["ffn1_down"], token = _send_start(shards("ffn1_down", token[0, 0]), False, "gather_ffn1_down_start")
    flight["mix"] = _send_start(shards("mix", token[0, 0]), False, "gather_mix_start")[0]
    recv = {}

    def weights(group, after):
        if group == "ffn1":
            return full_weights(group, gathered_ffn1)
        landed = _send_wait(flight.pop(group), after, False, f"gather_{group}_wait")
        if group == "mix":
            flight["ffn2"] = _send_start(shards("ffn2", zero_of(landed[0])), False, "gather_ffn2_start")[0]
        return full_weights(group, landed)

    small_shapes = [wts[n].shape[1:] for n in SMALL]
    early = SMALL[1:]
    assert SMALL[0] == "ffn1_norm"

    def grads_out(group, G):
        Gr = _reference_layout({n: G[n] for n in GRAD_GROUPS[group]})
        parts = [_slabs_from_full(n, Gr[n]).astype(BF16) for n in GRAD_GROUPS[group]]
        flight["g_" + group], tie = _send_start(parts, True, f"grads_{group}_start")
        if group == "mix":
            small = _pack([G[n].reshape(s) for n, s in zip(early, small_shapes[1:])])
            small = jnp.pad(small, ((0, (-small.shape[0]) % 8), (0, 0)))
            flight["small"], tie = _send_start([small + tie[0, 0]], False, "grads_small_start")
        return tie

    P = {n: wts[n] if wts[n].ndim == 2 else wts[n][0] for n in SMALL}
    loss_part, grad_x, G = _local_step(x[0], mem[0], positions[0], loss_target[0], P, weights, grads_out)

    for group, names in GRAD_GROUPS.items():
        recv.update(zip(names, _send_wait(flight.pop("g_" + group), grad_x, True, f"grads_{group}_wait")))
    early_recv, = _send_wait(flight.pop("small"), grad_x, False, "grads_small_wait")
    last = _share_rows(G["ffn1_norm"].reshape(-1, LANES), "share_ffn1_norm")
    g_small_packed = _sum_slots(jnp.concatenate([last, early_recv], axis=1), "sum_small", 2048)

    grads, delta, new_m, new_v = {}, {}, {}, {}
    for n in SHARDED:
        grads[n], delta[n], new_m[n], new_v[n] = _sum_adamw(recv[n], wts[n], mom[n], var[n], "adamw_" + n)
    grads.update(zip(SMALL, _unpack(g_small_packed, small_shapes)))

    flat2 = lambda d: [d[n].reshape(-1, d[n].shape[-1]) for n in SMALL]
    for dst, vals in zip((delta, new_m, new_v), _adamw_small(flat2(wts), flat2(grads), flat2(mom), flat2(var))):
        dst.update(zip(SMALL, vals))

    loss = lax.psum(jnp.sum(loss_part), ("x", "y", "c"))
    lead = lambda d: [d[n].reshape(wts[n].shape) for n in ORDER]
    return (loss, grad_x[None], *lead(grads), *lead(delta), *lead(new_m), *lead(new_v))
```

```python
import functools

import numpy as np
import jax
import jax.numpy as jnp
from jax import lax
from jax.experimental import pallas as pl
from jax.experimental.pallas import tpu as pltpu

F32, BF16 = jnp.float32, jnp.bfloat16

D_MODEL = 1024
SG_GROUPS, SG_GROUP_DIM, SG_WIDTH, CHUNK = 8, 64, 512, 128
MLA_HEADS, MLA_NOPE, MLA_ROPE, MLA_V, MLA_QK = 8, 64, 32, 64, 96
MLA_Q_RANK, MLA_KV_RANK = 384, 256
MEM_HEADS, MEM_HEAD_DIM, MEM_WIDTH = 4, 128, 512
D_FF = 2816
ROPE_BASE = 10000.0
EPS = 1e-6
NEG = -1e30
ADAM_LR, ADAM_B1, ADAM_B2, ADAM_EPS, ADAM_WD, ADAM_STEP = 0.001, 0.9, 0.999, 1e-08, 0.01, 10

N_DEV = 8
LANES = 128
V7X_VMEM_LIMIT = 56 * 1024 * 1024
HP = MLA_HEADS * LANES

Z_G, Z_U, Z_V, Z_QM, Z_CKV, Z_KR, Z_CQ = 0, 3072, 3584, 4096, 4608, 4864, 4992
Z_COLS = 5376
KR_LANE = 64


def _tile(dim, pref):
    if dim <= pref:
        return dim
    for t in range(pref - pref % LANES, LANES - 1, -LANES):
        if dim % t == 0:
            return t
    for t in range(pref - pref % 8, 7, -8):
        if dim % t == 0:
            return t
    return dim


def _cparams(sem):
    return pltpu.CompilerParams(dimension_semantics=sem, vmem_limit_bytes=V7X_VMEM_LIMIT)


_DN = {"nn": ((1,), (0,)), "nt": ((1,), (1,)), "tn": ((0,), (0,))}


def _dot(a, b, mode="nn"):
    return lax.dot_general(a.astype(BF16), b.astype(BF16), (_DN[mode], ((), ())),
                           preferred_element_type=F32)


def _mm(a, b, mode, out_dtype, name, tm=512, tn=512, tk=2048, tie=None):
    if mode == "tn":
        K, M = a.shape
    else:
        M, K = a.shape
    N = b.shape[0] if mode == "nt" else b.shape[1]
    tm, tn, tk = _tile(M, tm), _tile(N, tn), _tile(K, tk)
    nk = K // tk
    if mode == "tn":
        a_spec = pl.BlockSpec((tk, tm), lambda i, j, k: (k, i))
    else:
        a_spec = pl.BlockSpec((tm, tk), lambda i, j, k: (i, k))
    if mode == "nt":
        b_spec = pl.BlockSpec((tn, tk), lambda i, j, k: (j, k))
    else:
        b_spec = pl.BlockSpec((tk, tn), lambda i, j, k: (k, j))

    ties = [] if tie is None else [tie]

    def body(a_ref, b_ref, *rest):
        o_ref, *scratch = rest[len(ties):]
        p = _dot(a_ref[...], b_ref[...], mode)
        if nk == 1:
            o_ref[...] = p.astype(o_ref.dtype)
        else:
            acc_ref, = scratch
            k = pl.program_id(2)

            @pl.when(k == 0)
            def _():
                acc_ref[...] = p

            @pl.when(k > 0)
            def _():
                acc_ref[...] += p

            @pl.when(k == nk - 1)
            def _():
                o_ref[...] = acc_ref[...].astype(o_ref.dtype)

    return pl.pallas_call(
        body, name=name, grid=(M // tm, N // tn, nk),
        in_specs=[a_spec, b_spec] + [pl.BlockSpec(t.shape, lambda i, j, k: (0, 0)) for t in ties],
        out_specs=pl.BlockSpec((tm, tn), lambda i, j, k: (i, j)),
        out_shape=jax.ShapeDtypeStruct((M, N), out_dtype),
        scratch_shapes=[] if nk == 1 else [pltpu.VMEM((tm, tn), F32)],
        compiler_params=_cparams(("parallel", "parallel", "arbitrary")),
    )(a, b, *ties)


def _mm_t(at, b, name, tm, tn, tk=1024, tie=None):
    return _mm(at, b, "nn", BF16, name, tm=tm, tn=tn, tk=tk, tie=tie)


def _rowwise(fn, name, tr, row_ins, bc_ins, row_outs, acc_outs=()):
    norm = [it if isinstance(it, tuple) else (it, it.shape[1], 0) for it in row_ins]
    rows = norm[0][0].shape[0]
    tr = _tile(rows, tr)
    arrays, in_specs = [], []
    for arr, w, cb in norm:
        arrays.append(arr)
        in_specs.append(pl.BlockSpec((tr, w), lambda i, cb=cb: (i, cb)))
    for arr in bc_ins:
        arrays.append(arr)
        in_specs.append(pl.BlockSpec(arr.shape, lambda i, nd=arr.ndim: (0,) * nd))
    n_in, n_row = len(arrays), len(row_outs)
    out_shape, out_specs, aliases = [], [], {}
    transposed = [len(o) == 3 for o in row_outs]
    for k, o in enumerate(row_outs):
        if o[0] == "into":
            _, target, w, cb = o
            aliases[len(arrays)] = k
            arrays.append(target)
            in_specs.append(pl.BlockSpec(memory_space=pl.ANY))
            out_shape.append(jax.ShapeDtypeStruct(target.shape, target.dtype))
            out_specs.append(pl.BlockSpec((tr, w), lambda i, cb=cb: (i, cb)))
        elif transposed[k]:
            out_shape.append(jax.ShapeDtypeStruct((o[0], rows), o[1]))
            out_specs.append(pl.BlockSpec((o[0], tr), lambda i: (0, i)))
        else:
            out_shape.append(jax.ShapeDtypeStruct((rows, o[0]), o[1]))
            out_specs.append(pl.BlockSpec((tr, o[0]), lambda i: (i, 0)))
    for shp, dt in acc_outs:
        out_shape.append(jax.ShapeDtypeStruct(shp, dt))
        out_specs.append(pl.BlockSpec(shp, lambda i, nd=len(shp): (0,) * nd))

    def body(*refs):
        vals = fn(*[r[...].astype(F32) for r in refs[:n_in]])
        if not isinstance(vals, (tuple, list)):
            vals = (vals,)
        outs = refs[len(arrays):]
        for r, v, t in zip(outs[:n_row], vals[:n_row], transposed):
            r[...] = v.astype(F32).T.astype(r.dtype) if t else v.astype(r.dtype)
        if acc_outs:
            accs = list(zip(outs[n_row:], vals[n_row:]))
            i = pl.program_id(0)

            @pl.when(i == 0)
            def _():
                for r, v in accs:
                    r[...] = v.astype(r.dtype)

            @pl.when(i > 0)
            def _():
                for r, v in accs:
                    r[...] += v.astype(r.dtype)

    res = pl.pallas_call(
        body, name=name, grid=(rows // tr,), in_specs=in_specs, out_specs=out_specs,
        out_shape=out_shape, input_output_aliases=aliases, compiler_params=_cparams(("arbitrary",)),
    )(*arrays)
    return res


def _rsum(x):
    return jnp.sum(x, axis=0, keepdims=True)


def _rms(x, g, n=None):
    n = x.shape[-1] if n is None else n
    r = lax.rsqrt(jnp.sum(x * x, axis=-1, keepdims=True) * (1.0 / n) + EPS)
    return x * r * g


def _rms_bwd(x, g, dy, n=None):
    n = x.shape[-1] if n is None else n
    r = lax.rsqrt(jnp.sum(x * x, axis=-1, keepdims=True) * (1.0 / n) + EPS)
    xh = x * r
    dxh = dy * g
    dx = r * (dxh - xh * (jnp.sum(dxh * xh, axis=-1, keepdims=True) * (1.0 / n)))
    return dx, _rsum(dy * xh)


def _gelu(x):
    return 0.5 * x * (1.0 + lax.erf(x * 0.7071067811865476))


def _gelu_grad(x):
    return 0.5 * (1.0 + lax.erf(x * 0.7071067811865476)) + x * jnp.exp(-0.5 * x * x) * 0.3989422804014327


def _sigmoid(x):
    return 0.5 * jnp.tanh(0.5 * x) + 0.5


FFN_TM, FFN_TN = 1024, 1408
MXU_WIDTH = 256


def _col_chunks(n):
    return [(c, min(c + MXU_WIDTH, n)) for c in range(0, n, MXU_WIDTH)]


def _ffn_gu_act(h, w_gu, tag):
    T = h.shape[0]
    tm, tn = _tile(T, FFN_TM), FFN_TN
    nj = D_FF // tn

    def body(h_ref, wg_ref, wu_ref, gu_ref, a_ref, at_ref):
        h = h_ref[...]
        for c0, c1 in _col_chunks(tn):
            g = _dot(h, wg_ref[:, c0:c1])
            u = _dot(h, wu_ref[:, c0:c1])
            gu_ref[0, :, c0:c1] = g.astype(BF16)
            gu_ref[1, :, c0:c1] = u.astype(BF16)
            a = g * _sigmoid(g) * u
            a_ref[:, c0:c1] = a.astype(BF16)
            at_ref[c0:c1, :] = a.T.astype(BF16)

    return pl.pallas_call(
        body, name=f"{tag}_gu_act", grid=(T // tm, nj),
        in_specs=[pl.BlockSpec((tm, D_MODEL), lambda i, j: (i, 0)),
                  pl.BlockSpec((D_MODEL, tn), lambda i, j: (0, j)),
                  pl.BlockSpec((D_MODEL, tn), lambda i, j: (0, j + nj))],
        out_specs=[pl.BlockSpec((2, tm, tn), lambda i, j: (0, i, j)),
                   pl.BlockSpec((tm, tn), lambda i, j: (i, j)),
                   pl.BlockSpec((tn, tm), lambda i, j: (j, i))],
        out_shape=[jax.ShapeDtypeStruct((2, T, D_FF), BF16), jax.ShapeDtypeStruct((T, D_FF), BF16),
                   jax.ShapeDtypeStruct((D_FF, T), BF16)],
        compiler_params=_cparams(("parallel", "parallel")),
    )(h, w_gu, w_gu)


def _ffn_da_actbwd(do, w_down, gu, tag, tie=None):
    T = do.shape[0]
    tm, tn = _tile(T, FFN_TM), FFN_TN
    ties = [] if tie is None else [tie]

    def body(do_ref, wd_ref, gu_ref, *rest):
        dgu_ref = rest[-1]
        do = do_ref[...]
        for c0, c1 in _col_chunks(tn):
            da = _dot(do, wd_ref[c0:c1, :], "nt")
            g = gu_ref[0, :, c0:c1].astype(F32)
            u = gu_ref[1, :, c0:c1].astype(F32)
            s = _sigmoid(g)
            dgu_ref[0, :, c0:c1] = (da * u * s * (1.0 + g * (1.0 - s))).astype(BF16)
            dgu_ref[1, :, c0:c1] = (da * g * s).astype(BF16)

    return pl.pallas_call(
        body, name=f"{tag}_da_actbwd", grid=(T // tm, D_FF // tn),
        in_specs=[pl.BlockSpec((tm, D_MODEL), lambda i, j: (i, 0)),
                  pl.BlockSpec((tn, D_MODEL), lambda i, j: (j, 0)),
                  pl.BlockSpec((2, tm, tn), lambda i, j: (0, i, j))]
        + [pl.BlockSpec(t.shape, lambda i, j: (0, 0)) for t in ties],
        out_specs=pl.BlockSpec((2, tm, tn), lambda i, j: (0, i, j)),
        out_shape=jax.ShapeDtypeStruct((2, T, D_FF), BF16),
        compiler_params=_cparams(("parallel", "parallel")),
    )(do, w_down, gu, *ties)


def _ffn_dwgu(ht, dgu, tag, tk=2048):
    T = ht.shape[1]
    tn, tk = FFN_TN, _tile(T, tk)
    nj, nk = D_FF // tn, T // tk

    def body(a_ref, b_ref, o_ref, acc_ref):
        k = pl.program_id(1)
        p = _dot(a_ref[...], b_ref[...])

        @pl.when(k == 0)
        def _():
            acc_ref[...] = p

        @pl.when(k > 0)
        def _():
            acc_ref[...] += p

        @pl.when(k == nk - 1)
        def _():
            o_ref[...] = acc_ref[...].astype(o_ref.dtype)

    return pl.pallas_call(
        body, name=f"{tag}_dwgu", grid=(2 * nj, nk),
        in_specs=[pl.BlockSpec((D_MODEL, tk), lambda n, k: (0, k)),
                  pl.BlockSpec((None, tk, tn), lambda n, k: (n // nj, k, n % nj))],
        out_specs=pl.BlockSpec((D_MODEL, tn), lambda n, k: (0, n)),
        out_shape=jax.ShapeDtypeStruct((D_MODEL, 2 * D_FF), BF16),
        scratch_shapes=[pltpu.VMEM((D_MODEL, tn), F32)],
        compiler_params=_cparams(("parallel", "arbitrary")),
    )(ht, dgu)


def _ffn_dh(dgu, w_gu, tag, tm=2048, tie=None):
    T = dgu.shape[1]
    tm, tk = _tile(T, tm), FFN_TN
    nk = D_FF // tk
    ties = [] if tie is None else [tie]

    def body(a_ref, b_ref, *rest):
        o_ref, acc_ref = rest[len(ties):]
        k = pl.program_id(1)
        p = _dot(a_ref[...], b_ref[...], "nt")

        @pl.when(k == 0)
        def _():
            acc_ref[...] = p

        @pl.when(k > 0)
        def _():
            acc_ref[...] += p

        @pl.when(k == 2 * nk - 1)
        def _():
            o_ref[...] = acc_ref[...].astype(o_ref.dtype)

    return pl.pallas_call(
        body, name=f"{tag}_dh", grid=(T // tm, 2 * nk),
        in_specs=[pl.BlockSpec((None, tm, tk), lambda i, k: (k // nk, i, k % nk)),
                  pl.BlockSpec((D_MODEL, tk), lambda i, k: (0, k))]
        + [pl.BlockSpec(t.shape, lambda i, k: (0, 0)) for t in ties],
        out_specs=pl.BlockSpec((tm, D_MODEL), lambda i, k: (i, 0)),
        out_shape=jax.ShapeDtypeStruct((T, D_MODEL), BF16),
        scratch_shapes=[pltpu.VMEM((tm, D_MODEL), F32)],
        compiler_params=_cparams(("parallel", "arbitrary")),
    )(dgu, w_gu, *ties)


def _ffn_fwd(h, w_gu, w_down, tag):
    gu, a, at = _ffn_gu_act(h, w_gu, tag)
    if callable(w_down):
        w_down = w_down(at)
    o = _mm(a, w_down, "nn", BF16, f"{tag}_down", tm=1024, tn=1024, tk=2816)
    return gu, at, o


def _ffn_bwd(do, ht, gu, at, w_gu, w_down, tag, tie=None, on_dw=None):
    on_dw = on_dw or (lambda which, dw: None)
    dw_down = _mm_t(at, do, f"{tag}_dwdown", tm=1408, tn=1024, tk=2048, tie=tie)
    dgu = _ffn_da_actbwd(do, w_down, gu, tag, tie=on_dw("down", dw_down))
    dw_gu = _ffn_dwgu(ht, dgu, tag)
    dh = _ffn_dh(dgu, w_gu, tag, tie=on_dw("gu", dw_gu))
    return dh, dw_gu, dw_down


def _sg_common(u_pre, v_pre, ln_g, ln_b):
    u = _gelu(u_pre)
    v = _gelu(v_pre)
    mu = jnp.mean(v, axis=-1, keepdims=True)
    vc = v - mu
    rstd = lax.rsqrt(jnp.mean(vc * vc, axis=-1, keepdims=True) + EPS)
    vhat = vc * rstd
    vl = vhat * ln_g + ln_b
    return u, vhat, rstd, vl


def _sg_masked_pairs(w):
    t = lax.broadcasted_iota(jnp.int32, (CHUNK, CHUNK), 0)
    s = lax.broadcasted_iota(jnp.int32, (CHUNK, CHUNK), 1)
    causal = s <= t
    wm = [jnp.where(causal, w[g], 0.0).astype(BF16) for g in range(SG_GROUPS)]
    return [jnp.concatenate([wm[2 * j], wm[2 * j + 1]], axis=0) for j in range(SG_GROUPS // 2)], causal


def _sg_mix(vl, pairs, bias):
    tr = vl.shape[0]
    low = lax.broadcasted_iota(jnp.int32, (CHUNK, LANES), 1) < SG_GROUP_DIM
    vb = vl.astype(BF16)
    rows = []
    for c in range(tr // CHUNK):
        slabs = []
        for j in range(SG_GROUPS // 2):
            slab = vb[c * CHUNK:(c + 1) * CHUNK, j * LANES:(j + 1) * LANES]
            m = _dot(pairs[j], slab)
            slabs.append(jnp.where(low, m[:CHUNK], m[CHUNK:]))
        rows.append(jnp.concatenate(slabs, axis=1) + bias)
    return jnp.concatenate(rows, axis=0)


def _sg_fwd(z, ln_g, ln_b, sg_w, bias_full):
    def fn(u_pre, v_pre, ln_g, ln_b, w, bias):
        u, _, _, vl = _sg_common(u_pre, v_pre, ln_g, ln_b)
        pairs, _ = _sg_masked_pairs(w)
        y = u * _sg_mix(vl, pairs, bias)
        return y, y

    return _rowwise(fn, "sg_fwd", 512, [(z, SG_WIDTH, Z_U // SG_WIDTH), (z, SG_WIDTH, Z_V // SG_WIDTH)],
                    [ln_g, ln_b, sg_w, bias_full], [(SG_WIDTH, BF16), (SG_WIDTH, BF16, "T")])


def _sg_bwd(z, dy, ln_g, ln_b, sg_w, bias_full, group_ind, dz):
    def fn(u_pre, v_pre, dy, ln_g, ln_b, w, bias, ind):
        dy = dy.astype(F32)
        u, vhat, rstd, vl = _sg_common(u_pre, v_pre, ln_g, ln_b)
        pairs, causal = _sg_masked_pairs(w)
        mixed = _sg_mix(vl, pairs, bias)
        du_pre = dy * mixed * _gelu_grad(u_pre)
        dmix = dy * u
        tr = dy.shape[0]
        low = lax.broadcasted_iota(jnp.int32, (CHUNK, LANES), 1) < SG_GROUP_DIM
        vb = vl.astype(BF16)
        dw = [jnp.zeros((CHUNK, CHUNK), F32) for _ in range(SG_GROUPS)]
        dbias = jnp.zeros((CHUNK, SG_WIDTH), F32)
        dvl_rows = []
        for c in range(tr // CHUNK):
            dm_c = dmix[c * CHUNK:(c + 1) * CHUNK]
            dbias = dbias + dm_c
            slabs = []
            for j in range(SG_GROUPS // 2):
                slab = vb[c * CHUNK:(c + 1) * CHUNK, j * LANES:(j + 1) * LANES]
                dm = dm_c[:, j * LANES:(j + 1) * LANES]
                d0 = jnp.where(low, dm, 0.0).astype(BF16)
                d1 = jnp.where(low, 0.0, dm).astype(BF16)
                dw[2 * j] = dw[2 * j] + _dot(d0, slab, "nt")
                dw[2 * j + 1] = dw[2 * j + 1] + _dot(d1, slab, "nt")
                slabs.append(_dot(pairs[j], jnp.concatenate([d0, d1], axis=0), "tn"))
            dvl_rows.append(jnp.concatenate(slabs, axis=1))
        dvl = jnp.concatenate(dvl_rows, axis=0)
        dln_g = _rsum(dvl * vhat)
        dln_b = _rsum(dvl)
        dvh = dvl * ln_g
        dv = rstd * (dvh - jnp.mean(dvh, axis=-1, keepdims=True)
                     - vhat * jnp.mean(dvh * vhat, axis=-1, keepdims=True))
        dv_pre = dv * _gelu_grad(v_pre)
        dw = jnp.stack([jnp.where(causal, d, 0.0) for d in dw], axis=0)
        dbias_t = lax.dot_general(dbias, ind, (((1,), (0,)), ((), ())), precision=lax.Precision.HIGHEST,
                                  preferred_element_type=F32)
        return jnp.concatenate([du_pre, dv_pre], axis=1), dw, dbias_t, dln_g, dln_b

    return _rowwise(fn, "sg_bwd", 512,
                    [(z, SG_WIDTH, Z_U // SG_WIDTH), (z, SG_WIDTH, Z_V // SG_WIDTH), dy],
                    [ln_g, ln_b, sg_w, bias_full, group_ind],
                    [("into", dz, 2 * SG_WIDTH, Z_U // (2 * SG_WIDTH))],
                    [((SG_GROUPS, CHUNK, CHUNK), F32), ((CHUNK, SG_GROUPS), F32), ((1, SG_WIDTH), F32), ((1, SG_WIDTH), F32)])


def _rope(x, c, s1, s2):
    return x * c + pltpu.roll(x, LANES - MLA_ROPE // 2, 1) * s1 + pltpu.roll(x, MLA_ROPE // 2, 1) * s2


def _rope_t(d, c, s1, s2):
    return d * c + pltpu.roll(d * s1, MLA_ROPE // 2, 1) + pltpu.roll(d * s2, LANES - MLA_ROPE // 2, 1)


def _mla_post(q_pre, kv_pre, z, tabs, gq, gk):
    scale = MLA_QK ** -0.5 * LOG2E
    T = q_pre.shape[0]
    tr = _tile(T, 256)

    def body(q_ref, k_ref, v_ref, kr_ref, c_ref, s1_ref, s2_ref, gq_ref, gk_ref, qo_ref, ko_ref, vo_ref):
        kr = kr_ref[...].astype(F32)
        c, s1, s2, gq, gk = c_ref[...], s1_ref[...], s2_ref[...], gq_ref[...], gk_ref[...]
        ones_lane = lax.broadcasted_iota(jnp.int32, (tr, LANES), 1) == ONES_LANE
        for h in range(MLA_HEADS):
            sl = slice(h * LANES, (h + 1) * LANES)
            qo_ref[:, sl] = (_rope(_rms(q_ref[:, sl].astype(F32), gq, MLA_QK), c, s1, s2) * scale).astype(BF16)
            ko_ref[:, sl] = _rope(_rms(k_ref[:, sl].astype(F32) + kr, gk, MLA_QK), c, s1, s2).astype(BF16)
            vo_ref[:, sl] = jnp.where(ones_lane, 1.0, v_ref[:, sl].astype(F32)).astype(BF16)

    wide = lambda cb: pl.BlockSpec((tr, HP), lambda i, cb=cb: (i, cb))
    lanes = lambda cb: pl.BlockSpec((tr, LANES), lambda i, cb=cb: (i, cb))
    gain = pl.BlockSpec((1, LANES), lambda i: (0, 0))
    return pl.pallas_call(
        body, name="mla_post", grid=(T // tr,),
        in_specs=[wide(0), wide(0), wide(1), lanes(Z_KR // LANES), lanes(0), lanes(0), lanes(0), gain, gain],
        out_specs=[wide(0)] * 3, out_shape=[jax.ShapeDtypeStruct((T, HP), BF16)] * 3,
        compiler_params=_cparams(("parallel",)),
    )(q_pre, kv_pre, kv_pre, z, *tabs, gq, gk)


def _mla_post_bwd(q_pre, kv_pre, z, tabs, gq, gk, dq, dk, dv):
    scale = MLA_QK ** -0.5
    T = q_pre.shape[0]
    tr = _tile(T, 256)

    def body(q_ref, k_ref, kr_ref, c_ref, s1_ref, s2_ref, dq_ref, dk_ref, dv_ref, gq_ref, gk_ref,
             dqo_ref, dkvo_ref, dkro_ref, dgq_ref, dgk_ref):
        kr = kr_ref[...].astype(F32)
        c, s1, s2, gq, gk = c_ref[...], s1_ref[...], s2_ref[...], gq_ref[...], gk_ref[...]
        lane = lax.broadcasted_iota(jnp.int32, (1, LANES), 1)
        kr_mask = (lane >= KR_LANE) & (lane < KR_LANE + MLA_ROPE)
        dgq = jnp.zeros((1, LANES), F32)
        dgk = jnp.zeros((1, LANES), F32)
        dkr = jnp.zeros((tr, LANES), F32)
        for h in range(MLA_HEADS):
            sl = slice(h * LANES, (h + 1) * LANES)
            dqn = _rope_t(dq_ref[:, sl].astype(F32), c, s1, s2) * scale
            dx, dg = _rms_bwd(q_ref[:, sl].astype(F32), gq, dqn, MLA_QK)
            dqo_ref[:, sl] = dx.astype(BF16)
            dgq = dgq + dg
            dkn = _rope_t(dk_ref[:, sl].astype(F32), c, s1, s2)
            dx, dg = _rms_bwd(k_ref[:, sl].astype(F32) + kr, gk, dkn, MLA_QK)
            dkvo_ref[:, sl] = dx.astype(BF16)
            dkvo_ref[:, HP + h * LANES:HP + (h + 1) * LANES] = dv_ref[:, sl]
            dgk = dgk + dg
            dkr = dkr + dx
        dkro_ref[...] = jnp.where(kr_mask, dkr, 0.0).astype(BF16)
        i = pl.program_id(0)

        @pl.when(i == 0)
        def _():
            dgq_ref[...] = dgq
            dgk_ref[...] = dgk

        @pl.when(i > 0)
        def _():
            dgq_ref[...] += dgq
            dgk_ref[...] += dgk

    wide = lambda cb: pl.BlockSpec((tr, HP), lambda i, cb=cb: (i, cb))
    lanes = lambda cb: pl.BlockSpec((tr, LANES), lambda i, cb=cb: (i, cb))
    gain = pl.BlockSpec((1, LANES), lambda i: (0, 0))
    return pl.pallas_call(
        body, name="mla_post_bwd", grid=(T // tr,),
        in_specs=[wide(0), wide(0), lanes(Z_KR // LANES), lanes(0), lanes(0), lanes(0), wide(0), wide(0), wide(0),
                  gain, gain],
        out_specs=[wide(0), pl.BlockSpec((tr, 2 * HP), lambda i: (i, 0)), lanes(0), gain, gain],
        out_shape=[jax.ShapeDtypeStruct((T, HP), BF16), jax.ShapeDtypeStruct((T, 2 * HP), BF16),
                   jax.ShapeDtypeStruct((T, LANES), BF16), jax.ShapeDtypeStruct((1, LANES), F32),
                   jax.ShapeDtypeStruct((1, LANES), F32)],
        compiler_params=_cparams(("arbitrary",)),
    )(q_pre, kv_pre, z, *tabs, dq, dk, dv, gq, gk)


def _pairs(n, lower):
    a, b = [], []
    for o in range(n):
        inner = range(o + 1) if lower else range(o, n)
        for t in inner:
            a.append(o)
            b.append(t)
    return jnp.asarray(np.array(a, np.int32)), jnp.asarray(np.array(b, np.int32))


FLASH_TILE, FLASH_SUB_ROWS = 2048, 512
LOG2E, LN2 = 1.4426950408889634, 0.6931471805599453
ONES_LANE = MLA_V


def _flash_tiles(T):
    tq = _tile(T, FLASH_TILE)
    return tq, _tile(tq, FLASH_SUB_ROWS)


def _col_span(t, sr, rb, diag, key_major):
    if not diag:
        return 0, t
    return (rb * sr, t) if key_major else (0, (rb + 1) * sr)


def _span_iota(sr, rb, c0, c1):
    r = lax.broadcasted_iota(jnp.int32, (sr, c1 - c0), 0) + rb * sr
    c = lax.broadcasted_iota(jnp.int32, (sr, c1 - c0), 1) + c0
    return r, c


def _lanes(x, width):
    return jnp.concatenate([x] * (width // LANES), axis=1)


def _flash_fwd(q, k, v):
    T = q.shape[0]
    tq, sr = _flash_tiles(T)
    n = T // tq
    ii, jj = _pairs(n, True)

    def body(ii_ref, jj_ref, q_ref, k_ref, v_ref, o_ref, ot_ref, lse_ref, lset_ref, m_sc, acc_sc):
        p_ = pl.program_id(1)
        i, j = ii_ref[p_], jj_ref[p_]

        @pl.when(j == 0)
        def _():
            m_sc[...] = jnp.full(m_sc.shape, NEG, F32)
            acc_sc[...] = jnp.zeros(acc_sc.shape, F32)

        def tile(diag):
            for rb in range(tq // sr):
                rows = slice(rb * sr, (rb + 1) * sr)
                c0, c1 = _col_span(tq, sr, rb, diag, False)
                s = _dot(q_ref[rows, :], k_ref[c0:c1, :], "nt")
                if diag:
                    r, c = _span_iota(sr, rb, c0, c1)
                    s = jnp.where(c <= r, s, NEG)
                m = m_sc[rows, :]
                m_new = jnp.maximum(m, jnp.max(s, axis=1, keepdims=True))
                p = jnp.exp2(s - _lanes(m_new, c1 - c0))
                acc_sc[rows, :] = jnp.exp2(m - m_new) * acc_sc[rows, :] + _dot(p, v_ref[c0:c1, :])
                m_sc[rows, :] = m_new

        @pl.when(j < i)
        def _():
            tile(False)

        @pl.when(j == i)
        def _():
            tile(True)
            acc = acc_sc[...]
            lane = lax.broadcasted_iota(jnp.int32, acc.shape, 1)
            l = jnp.sum(jnp.where(lane == ONES_LANE, acc, 0.0), axis=1, keepdims=True)
            o = jnp.where(lane < MLA_V, acc / l, 0.0)
            o_ref[...] = o.astype(o_ref.dtype)
            ot_ref[...] = o.T.astype(ot_ref.dtype)
            lse = m_sc[...] + jnp.log2(l)
            lse_ref[...] = lse
            lset_ref[...] = lse.T[:8]

    blk = lambda which: pl.BlockSpec((tq, LANES), which)
    qmap = lambda h, p, ii, jj: (ii[p], h)
    kmap = lambda h, p, ii, jj: (jj[p], h)
    tmap = lambda h, p, ii, jj: (h, ii[p])
    return pl.pallas_call(
        body, name="mla_flash_fwd",
        grid_spec=pltpu.PrefetchScalarGridSpec(
            num_scalar_prefetch=2, grid=(MLA_HEADS, int(ii.shape[0])),
            in_specs=[blk(qmap), blk(kmap), blk(kmap)],
            out_specs=[blk(qmap), pl.BlockSpec((LANES, tq), tmap), blk(qmap), pl.BlockSpec((8, tq), tmap)],
            scratch_shapes=[pltpu.VMEM((tq, LANES), F32)] * 2),
        out_shape=[jax.ShapeDtypeStruct((T, HP), BF16), jax.ShapeDtypeStruct((HP, T), BF16),
                   jax.ShapeDtypeStruct((T, HP), F32), jax.ShapeDtypeStruct((8 * MLA_HEADS, T), F32)],
        compiler_params=_cparams(("parallel", "arbitrary")),
    )(ii, jj, q, k, v)


def _flash_dq(q, k, v, do, lse, delta):
    T = q.shape[0]
    tq, sr = _flash_tiles(T)
    n = T // tq
    ii, jj = _pairs(n, True)

    def body(ii_ref, jj_ref, q_ref, k_ref, v_ref, do_ref, lse_ref, dl_ref, dq_ref, acc_sc):
        p_ = pl.program_id(1)
        i, j = ii_ref[p_], jj_ref[p_]

        @pl.when(j == 0)
        def _():
            acc_sc[...] = jnp.zeros(acc_sc.shape, F32)

        def tile(diag):
            for rb in range(tq // sr):
                rows = slice(rb * sr, (rb + 1) * sr)
                c0, c1 = _col_span(tq, sr, rb, diag, False)
                ks = k_ref[c0:c1, :]
                p = jnp.exp2(_dot(q_ref[rows, :], ks, "nt") - _lanes(lse_ref[rows, :], c1 - c0))
                if diag:
                    r, c = _span_iota(sr, rb, c0, c1)
                    p = jnp.where(c <= r, p, 0.0)
                dp = _dot(do_ref[rows, :], v_ref[c0:c1, :], "nt")
                acc_sc[rows, :] += _dot(p * (dp - _lanes(dl_ref[rows, :], c1 - c0)), ks)

        @pl.when(j < i)
        def _():
            tile(False)

        @pl.when(j == i)
        def _():
            tile(True)
            dq_ref[...] = acc_sc[...].astype(dq_ref.dtype)

    blk = lambda which: pl.BlockSpec((tq, LANES), which)
    qmap = lambda h, p, ii, jj: (ii[p], h)
    kmap = lambda h, p, ii, jj: (jj[p], h)
    return pl.pallas_call(
        body, name="mla_flash_dq",
        grid_spec=pltpu.PrefetchScalarGridSpec(
            num_scalar_prefetch=2, grid=(MLA_HEADS, int(ii.shape[0])),
            in_specs=[blk(qmap), blk(kmap), blk(kmap), blk(qmap), blk(qmap), blk(qmap)],
            out_specs=blk(qmap),
            scratch_shapes=[pltpu.VMEM((tq, LANES), F32)]),
        out_shape=jax.ShapeDtypeStruct((T, HP), BF16),
        compiler_params=_cparams(("parallel", "arbitrary")),
    )(ii, jj, q, k, v, do, lse, delta)


def _flash_dkv(q, k, v, do, lse_t, delta_t):
    T = q.shape[0]
    tq, sr = _flash_tiles(T)
    n = T // tq
    jj, ii = _pairs(n, False)

    def body(jj_ref, ii_ref, q_ref, k_ref, v_ref, do_ref, lse_ref, dl_ref, dk_ref, dv_ref, dk_sc, dv_sc):
        p_ = pl.program_id(1)
        j, i = jj_ref[p_], ii_ref[p_]

        @pl.when(i == j)
        def _():
            dk_sc[...] = jnp.zeros(dk_sc.shape, F32)
            dv_sc[...] = jnp.zeros(dv_sc.shape, F32)

        def tile(diag):
            for rb in range(tq // sr):
                rows = slice(rb * sr, (rb + 1) * sr)
                c0, c1 = _col_span(tq, sr, rb, diag, True)
                qs, dos = q_ref[c0:c1, :], do_ref[c0:c1, :]
                pt = jnp.exp2(_dot(k_ref[rows, :], qs, "nt") - lse_ref[:1, c0:c1])
                if diag:
                    r, c = _span_iota(sr, rb, c0, c1)
                    pt = jnp.where(r <= c, pt, 0.0)
                dpt = _dot(v_ref[rows, :], dos, "nt")
                dv_sc[rows, :] += _dot(pt, dos)
                dk_sc[rows, :] += _dot(pt * (dpt - dl_ref[:1, c0:c1]), qs)

        @pl.when(i == j)
        def _():
            tile(True)

        @pl.when(i > j)
        def _():
            tile(False)

        @pl.when(i == n - 1)
        def _():
            dk_ref[...] = (dk_sc[...] * LN2).astype(dk_ref.dtype)
            dv_ref[...] = dv_sc[...].astype(dv_ref.dtype)

    blk = lambda which: pl.BlockSpec((tq, LANES), which)
    qmap = lambda h, p, jj, ii: (ii[p], h)
    kmap = lambda h, p, jj, ii: (jj[p], h)
    lse_rows = pl.BlockSpec((8, tq), lambda h, p, jj, ii: (h, ii[p]))
    delta_rows = pl.BlockSpec((8, tq), lambda h, p, jj, ii: (h * (LANES // 8), ii[p]))
    return pl.pallas_call(
        body, name="mla_flash_dkv",
        grid_spec=pltpu.PrefetchScalarGridSpec(
            num_scalar_prefetch=2, grid=(MLA_HEADS, int(ii.shape[0])),
            in_specs=[blk(qmap), blk(kmap), blk(kmap), blk(qmap), lse_rows, delta_rows],
            out_specs=[blk(kmap), blk(kmap)],
            scratch_shapes=[pltpu.VMEM((tq, LANES), F32)] * 2),
        out_shape=[jax.ShapeDtypeStruct((T, HP), BF16)] * 2,
        compiler_params=_cparams(("parallel", "arbitrary")),
    )(jj, ii, q, k, v, do, lse_t, delta_t)


def _mem_fwd(z, km, vm, gq):
    scale = MEM_HEAD_DIM ** -0.5

    def fn(qm, km, vm, gq):
        ys = []
        for h in range(MEM_HEADS):
            sl = slice(h * LANES, (h + 1) * LANES)
            q = _rms(qm[:, sl], gq) * scale
            s = _dot(q, km[:, sl], "nt")
            p = jnp.exp(s - jnp.max(s, axis=1, keepdims=True))
            p = p / jnp.sum(p, axis=1, keepdims=True)
            ys.append(_dot(p, vm[:, sl]))
        y = jnp.concatenate(ys, axis=1)
        return y, y

    return _rowwise(fn, "mem_fwd", 512, [(z, MEM_WIDTH, Z_QM // MEM_WIDTH)], [km, vm, gq],
                    [(MEM_WIDTH, BF16), (MEM_WIDTH, BF16, "T")])


def _mem_bwd(z, dy, km, vm, gq, dz):
    scale = MEM_HEAD_DIM ** -0.5

    def fn(qm, dy, km, vm, gq):
        dqs, dks, dvs = [], [], []
        dgq = jnp.zeros((1, LANES), F32)
        for h in range(MEM_HEADS):
            sl = slice(h * LANES, (h + 1) * LANES)
            q = (_rms(qm[:, sl], gq) * scale).astype(BF16)
            dyh = dy[:, sl]
            kh, vh = km[:, sl], vm[:, sl]
            s = _dot(q, kh, "nt")
            p = jnp.exp(s - jnp.max(s, axis=1, keepdims=True))
            p = p / jnp.sum(p, axis=1, keepdims=True)
            dp = _dot(dyh, vh, "nt")
            ds = p * (dp - jnp.sum(p * dp, axis=1, keepdims=True))
            dq = _dot(ds, kh) * scale
            dx, dg = _rms_bwd(qm[:, sl], gq, dq)
            dqs.append(dx)
            dgq = dgq + dg
            st = _dot(kh, q, "nt")
            pt = jnp.exp(st - jnp.max(st, axis=0, keepdims=True))
            pt = pt / jnp.sum(pt, axis=0, keepdims=True)
            dpt = _dot(vh, dyh, "nt")
            dst = pt * (dpt - jnp.sum(pt * dpt, axis=0, keepdims=True))
            dvs.append(_dot(pt, dyh))
            dks.append(_dot(dst, q))
        return jnp.concatenate(dqs, axis=1), jnp.concatenate(dks, axis=1), jnp.concatenate(dvs, axis=1), dgq

    m = km.shape[0]
    return _rowwise(fn, "mem_bwd", 512, [(z, MEM_WIDTH, Z_QM // MEM_WIDTH), dy], [km, vm, gq],
                    [("into", dz, MEM_WIDTH, Z_QM // MEM_WIDTH)],
                    [((m, MEM_WIDTH), F32), ((m, MEM_WIDTH), F32), ((1, LANES), F32)])


GROUPS = {"ffn1": ["ffn1_w_gu"], "ffn1_down": ["ffn1_w_down"],
          "mix": ["w_in", "mla_w_uq", "mla_w_ukv", "mem_w_kv", "w_branch_a", "w_branch_b", "w_branch_c", "w_out"],
          "ffn2": ["ffn2_w_gu", "ffn2_w_down"]}
GRAD_GROUPS = {"ffn2": GROUPS["ffn2"], "mix": GROUPS["mix"], "ffn1_down": ["ffn1_w_down"], "ffn1_gu": ["ffn1_w_gu"]}


def _local_step(x, mem, positions, loss_target, P, weights, grads_out):
    T = x.shape[0]
    G = {}
    W = dict(weights("ffn1", None))

    half = MLA_ROPE // 2
    inv = ROPE_BASE ** (-jnp.arange(half, dtype=F32) / half)
    ang = positions.astype(F32)[:, None] * inv
    cos, sin = jnp.cos(ang), jnp.sin(ang)
    one, zero = jnp.ones((T, MLA_NOPE), F32), jnp.zeros((T, half), F32)
    pad = LANES - MLA_QK
    tabs = (jnp.concatenate([one, cos, cos, jnp.ones((T, pad), F32)], axis=1),
            jnp.concatenate([jnp.zeros((T, MLA_NOPE), F32), -sin, zero, jnp.zeros((T, pad), F32)], axis=1),
            jnp.concatenate([jnp.zeros((T, MLA_NOPE), F32), zero, sin, jnp.zeros((T, pad), F32)], axis=1))
    gq_p = jnp.pad(P["mla_q_norm"], ((0, 0), (0, pad)))
    gk_p = jnp.pad(P["mla_k_norm"], ((0, 0), (0, pad)))
    bias_full = jnp.repeat(P["sg_b"].T, SG_GROUP_DIM, axis=1)
    group_ind = jnp.repeat(jnp.eye(SG_GROUPS, dtype=F32), SG_GROUP_DIM, axis=0)

    HT = (D_MODEL, BF16, "T")

    def norm2(x, g):
        h = _rms(x, g)
        return h, h

    h1, h1t = _rowwise(norm2, "ffn1_norm", 512, [x], [P["ffn1_norm"]], [(D_MODEL, BF16), HT])
    def ffn1_w_down(after):
        W.update(weights("ffn1_down", after))
        return W["ffn1_w_down"]

    gu1, a1t, o1 = _ffn_fwd(h1, W["ffn1_w_gu"], ffn1_w_down, "ffn1")

    def resid_norm(x, o, g):
        xn = x + 0.5 * o
        h = _rms(xn, g)
        return xn, h, h

    x1, hm, hmt = _rowwise(resid_norm, "mix_norm", 512, [x, o1], [P["mix_norm"]],
                           [(D_MODEL, F32), (D_MODEL, BF16), HT])
    W.update(weights("mix", hm))
    z = _mm(hm, W["w_in"], "nn", BF16, "w_in", tm=1024, tn=1792)

    y_a, y_at = _sg_fwd(z, P["sg_ln_g"], P["sg_ln_b"], P["sg_w"], bias_full)

    def c_norm(cq, ckv, gq, gkv):
        a, b = _rms(cq, gq), _rms(ckv, gkv)
        return a, b, a, b

    cqn, ckvn, cqnt, ckvnt = _rowwise(
        c_norm, "mla_cnorm", 512, [(z, MLA_Q_RANK, Z_CQ // MLA_Q_RANK), (z, MLA_KV_RANK, Z_CKV // MLA_KV_RANK)],
        [P["mla_cq_norm"], P["mla_ckv_norm"]],
        [(MLA_Q_RANK, BF16), (MLA_KV_RANK, BF16), (MLA_Q_RANK, BF16, "T"), (MLA_KV_RANK, BF16, "T")])
    q_pre = _mm(cqn, W["mla_w_uq"], "nn", BF16, "mla_uq", tm=1024, tn=1024)
    kv_pre = _mm(ckvn, W["mla_w_ukv"], "nn", BF16, "mla_ukv", tm=1024, tn=1024)
    q, k, v = _mla_post(q_pre, kv_pre, z, tabs, gq_p, gk_p)
    y_b, y_bt, lse, lse_t = _flash_fwd(q, k, v)

    memn, = _rowwise(lambda m, g: _rms(m, g), "mem_norm", 256, [mem], [P["mem_norm"]], [(D_MODEL, BF16)])
    kvm = _mm(memn, W["mem_w_kv"], "nn", F32, "mem_kv")

    def mem_k(kvm, gk):
        ks = [_rms(kvm[:, h * LANES:(h + 1) * LANES], gk) for h in range(MEM_HEADS)]
        return jnp.concatenate(ks, axis=1), kvm[:, MEM_WIDTH:]

    km, vm = _rowwise(mem_k, "mem_knorm", 256, [kvm], [P["mem_k_norm"]], [(MEM_WIDTH, BF16), (MEM_WIDTH, BF16)])
    y_c, y_ct = _mem_fwd(z, km, vm, P["mem_q_norm"])

    pa = _mm(y_a, W["w_branch_a"], "nn", BF16, "branch_a", tm=1024, tn=1024)
    pb = _mm(y_b, W["w_branch_b"], "nn", BF16, "branch_b", tm=1024, tn=1024)
    pc = _mm(y_c, W["w_branch_c"], "nn", BF16, "branch_c", tm=1024, tn=1024)

    def merge(zg, pa, pb, pc, b):
        g = _sigmoid(zg + b)
        m = g[:, :D_MODEL] * pa + g[:, D_MODEL:2 * D_MODEL] * pb + g[:, 2 * D_MODEL:] * pc
        return m, m

    merged, mergedt = _rowwise(merge, "merge", 256, [(z, 3 * D_MODEL, 0), pa, pb, pc], [P["b_gate"]],
                               [(D_MODEL, BF16), HT])
    om = _mm(merged, W["w_out"], "nn", BF16, "w_out", tm=1024, tn=1024)

    def resid_norm1(x, o, g):
        xn = x + o
        h = _rms(xn, g)
        return xn, h, h

    x2, h2, h2t = _rowwise(resid_norm1, "ffn2_norm", 512, [x1, om], [P["ffn2_norm"]],
                           [(D_MODEL, F32), (D_MODEL, BF16), HT])
    W.update(weights("ffn2", h2))
    gu2, a2t, o2 = _ffn_fwd(h2, W["ffn2_w_gu"], W["ffn2_w_down"], "ffn2")

    def loss_fn(x2, o2, t):
        e = x2 + 0.5 * o2 - t
        return e * (1.0 / D_MODEL), (e * (0.5 / D_MODEL)).astype(BF16), _rsum(e * e) * (0.5 / D_MODEL)

    dx3, do2, loss_part = _rowwise(loss_fn, "loss", 512, [x2, o2, loss_target], [],
                                   [(D_MODEL, F32), (D_MODEL, BF16)], [((1, D_MODEL), F32)])

    dh2, G["ffn2_w_gu"], G["ffn2_w_down"] = _ffn_bwd(do2, h2t, gu2, a2t, W["ffn2_w_gu"], W["ffn2_w_down"], "ffn2")
    tie = grads_out("ffn2", G)

    def norm_bwd(x, dh, dxo, g, *_):
        dx, dg = _rms_bwd(x, g, dh)
        dx = dx + dxo
        return dx, dx, dg

    dx2, dx2b, G["ffn2_norm"] = _rowwise(norm_bwd, "ffn2_norm_bwd", 512, [x2, dh2, dx3],
                                         [P["ffn2_norm"]] + ([] if tie is None else [tie]),
                                         [(D_MODEL, F32), (D_MODEL, BF16)], [((1, D_MODEL), F32)])

    G["w_out"] = _mm_t(mergedt, dx2b, "w_out_dw", tm=1024, tn=1024)
    dmerged = _mm(dx2b, W["w_out"], "nt", BF16, "w_out_dx", tm=1024, tn=1024)

    def merge_bwd(zg, pa, pb, pc, dm, b):
        g = _sigmoid(zg + b)
        ps = jnp.concatenate([pa, pb, pc], axis=1)
        dm3 = jnp.concatenate([dm, dm, dm], axis=1)
        dzg = dm3 * ps * g * (1.0 - g)
        dp = dm3 * g
        return dzg, dp[:, :D_MODEL], dp[:, D_MODEL:2 * D_MODEL], dp[:, 2 * D_MODEL:], _rsum(dzg)

    dz = lax.empty((T, Z_COLS), BF16)
    dz, dpa, dpb, dpc, G["b_gate"] = _rowwise(
        merge_bwd, "merge_bwd", 256, [(z, 3 * D_MODEL, 0), pa, pb, pc, dmerged], [P["b_gate"]],
        [("into", dz, 3 * D_MODEL, 0), (D_MODEL, BF16), (D_MODEL, BF16), (D_MODEL, BF16)], [((1, 3 * D_MODEL), F32)])

    G["w_branch_a"] = _mm_t(y_at, dpa, "branch_a_dw", tm=512, tn=1024)
    G["w_branch_b"] = _mm_t(y_bt, dpb, "branch_b_dw", tm=1024, tn=1024)
    G["w_branch_c"] = _mm_t(y_ct, dpc, "branch_c_dw", tm=512, tn=1024)
    dy_a = _mm(dpa, W["w_branch_a"], "nt", BF16, "branch_a_dx", tm=1024, tn=512)
    dy_b = _mm(dpb, W["w_branch_b"], "nt", BF16, "branch_b_dx", tm=1024, tn=1024)
    dy_c = _mm(dpc, W["w_branch_c"], "nt", BF16, "branch_c_dx", tm=1024, tn=512)

    dz, G["sg_w"], dbias_t, G["sg_ln_g"], G["sg_ln_b"] = _sg_bwd(
        z, dy_a, P["sg_ln_g"], P["sg_ln_b"], P["sg_w"], bias_full, group_ind, dz)
    G["sg_b"] = dbias_t.T

    dz, dkm, dvm, G["mem_q_norm"] = _mem_bwd(z, dy_c, km, vm, P["mem_q_norm"], dz)

    def mem_k_bwd(kvm, dkm, dvm, gk):
        dks = []
        dg = jnp.zeros((1, LANES), F32)
        for h in range(MEM_HEADS):
            sl = slice(h * LANES, (h + 1) * LANES)
            dx, d = _rms_bwd(kvm[:, sl], gk, dkm[:, sl])
            dks.append(dx)
            dg = dg + d
        return jnp.concatenate(dks + [dvm], axis=1), dg

    dkvm, G["mem_k_norm"] = _rowwise(mem_k_bwd, "mem_knorm_bwd", 256, [kvm, dkm, dvm], [P["mem_k_norm"]],
                                     [(2 * MEM_WIDTH, BF16)], [((1, LANES), F32)])
    G["mem_w_kv"] = _mm(memn, dkvm, "tn", BF16, "mem_kv_dw")
    dmemn = _mm(dkvm, W["mem_w_kv"], "nt", F32, "mem_kv_dx")
    _, G["mem_norm"] = _rowwise(lambda m, d, g: _rms_bwd(m, g, d), "mem_norm_bwd", 256, [mem, dmemn],
                                [P["mem_norm"]], [(D_MODEL, BF16)], [((1, D_MODEL), F32)])

    def delta_fn(o, do):
        od = o.astype(F32) * do.astype(F32)
        ds = [jnp.broadcast_to(jnp.sum(od[:, h * LANES:(h + 1) * LANES], axis=1, keepdims=True), (od.shape[0], LANES))
              for h in range(MLA_HEADS)]
        d = jnp.concatenate(ds, axis=1)
        return d, d

    delta, delta_t = _rowwise(delta_fn, "mla_delta", 512, [y_b, dy_b], [], [(HP, F32), (HP, F32, "T")])
    dq = _flash_dq(q, k, v, dy_b, lse, delta)
    dk, dv = _flash_dkv(q, k, v, dy_b, lse_t, delta_t)
    dq_pre, dkv_pre, dkr, dgq, dgk = _mla_post_bwd(q_pre, kv_pre, z, tabs, gq_p, gk_p, dq, dk, dv)
    G["mla_q_norm"], G["mla_k_norm"] = dgq[:, :MLA_QK], dgk[:, :MLA_QK]
    G["mla_w_uq"] = _mm_t(cqnt, dq_pre, "mla_uq_dw", tm=384, tn=1024)
    G["mla_w_ukv"] = _mm_t(ckvnt, dkv_pre, "mla_ukv_dw", tm=256, tn=2048)
    dcqn = _mm(dq_pre, W["mla_w_uq"], "nt", BF16, "mla_uq_dx", tm=1024)
    dckvn = _mm(dkv_pre, W["mla_w_ukv"], "nt", BF16, "mla_ukv_dx", tm=1024)

    def c_norm_bwd(cq, ckv, dcqn, dckvn, dkr, gq, gkv):
        dcq, dgq = _rms_bwd(cq, gq, dcqn)
        dckv, dgkv = _rms_bwd(ckv, gkv, dckvn)
        return jnp.concatenate([dckv, dkr, dcq], axis=1), dgq, dgkv

    tail = Z_COLS - Z_CKV
    dz, G["mla_cq_norm"], G["mla_ckv_norm"] = _rowwise(
        c_norm_bwd, "mla_cnorm_bwd", 512,
        [(z, MLA_Q_RANK, Z_CQ // MLA_Q_RANK), (z, MLA_KV_RANK, Z_CKV // MLA_KV_RANK), dcqn, dckvn, dkr],
        [P["mla_cq_norm"], P["mla_ckv_norm"]], [("into", dz, tail, Z_CKV // tail)],
        [((1, MLA_Q_RANK), F32), ((1, MLA_KV_RANK), F32)])
    G["w_in"] = _mm_t(hmt, dz, "w_in_dw", tm=1024, tn=1792, tk=2048)
    dhm = _mm(dz, W["w_in"], "nt", BF16, "w_in_dx", tm=1024, tn=1024, tk=2688)

    def norm_bwd_half(x, dh, dxo, g):
        dx, dg = _rms_bwd(x, g, dh)
        dx = dx + dxo
        return dx, (0.5 * dx), dg

    dx1, do1, G["mix_norm"] = _rowwise(norm_bwd_half, "mix_norm_bwd", 512, [x1, dhm, dx2], [P["mix_norm"]],
                                       [(D_MODEL, F32), (D_MODEL, BF16)], [((1, D_MODEL), F32)])
    tie = grads_out("mix", G)

    def ffn1_dw(which, dw):
        G["ffn1_w_" + which] = dw
        return grads_out("ffn1_" + which, G)

    dh1, _, _ = _ffn_bwd(do1, h1t, gu1, a1t, W["ffn1_w_gu"], W["ffn1_w_down"], "ffn1", tie, ffn1_dw)

    def norm_bwd_last(x, dh, dxo, g):
        dx, dg = _rms_bwd(x, g, dh)
        return dx + dxo, dg

    grad_x, G["ffn1_norm"] = _rowwise(norm_bwd_last, "ffn1_norm_bwd", 512, [x, dh1, dx1], [P["ffn1_norm"]],
                                      [(D_MODEL, F32)], [((1, D_MODEL), F32)])
    return loss_part, grad_x, G


SHARDED = ["ffn1_w_gu", "ffn1_w_down", "w_in", "mla_w_uq", "mla_w_ukv", "mem_w_kv",
           "w_branch_a", "w_branch_b", "w_branch_c", "w_out", "ffn2_w_gu", "ffn2_w_down"]
ROW_SHARDED = {"ffn1_w_down", "mem_w_kv", "w_out", "ffn2_w_down"}
SMALL = ["ffn1_norm", "mix_norm", "b_gate", "sg_ln_g", "sg_ln_b", "sg_w", "sg_b", "mla_cq_norm",
         "mla_ckv_norm", "mla_q_norm", "mla_k_norm", "mem_norm", "mem_q_norm", "mem_k_norm", "ffn2_norm"]
ORDER = ["ffn1_norm", "ffn1_w_gu", "ffn1_w_down", "mix_norm", "w_in", "b_gate", "sg_ln_g", "sg_ln_b", "sg_w",
         "sg_b", "mla_cq_norm", "mla_w_uq", "mla_ckv_norm", "mla_w_ukv", "mla_q_norm", "mla_k_norm", "mem_norm",
         "mem_w_kv", "mem_q_norm", "mem_k_norm", "w_branch_a", "w_branch_b", "w_branch_c", "w_out", "ffn2_norm",
         "ffn2_w_gu", "ffn2_w_down"]

_IN_U, _IN_V, _IN_CQ, _IN_CKV, _IN_KR, _IN_QM, _IN_G = 0, 512, 1024, 1408, 1664, 1696, 2208
IN_COLS = 5280


def _full_from_slabs(name, slabs):
    n, r, c = slabs.shape
    if name in ROW_SHARDED:
        return slabs.reshape(n * r, c)
    return slabs.transpose(1, 0, 2).reshape(r, n * c)


def _slabs_from_full(name, full):
    if name in ROW_SHARDED:
        return full.reshape(N_DEV, full.shape[0] // N_DEV, full.shape[1])
    r, c = full.shape
    return full.reshape(r, N_DEV, c // N_DEV).transpose(1, 0, 2)


def _compute_layout(full):
    W = dict(full)
    if "w_in" not in full:
        return W
    w = full["w_in"]
    kr = jnp.pad(w[:, _IN_KR:_IN_QM], ((0, 0), (KR_LANE, LANES - KR_LANE - MLA_ROPE)))
    W["w_in"] = jnp.concatenate([w[:, _IN_G:], w[:, _IN_U:_IN_CQ], w[:, _IN_QM:_IN_G], w[:, _IN_CKV:_IN_KR], kr,
                                 w[:, _IN_CQ:_IN_CKV]], axis=1)
    uq = full["mla_w_uq"].reshape(MLA_Q_RANK, MLA_HEADS, MLA_QK)
    W["mla_w_uq"] = jnp.pad(uq, ((0, 0), (0, 0), (0, LANES - MLA_QK))).reshape(MLA_Q_RANK, HP)
    ukv = full["mla_w_ukv"].reshape(MLA_KV_RANK, MLA_HEADS, MLA_NOPE + MLA_V)
    padh = lambda a: jnp.pad(a, ((0, 0), (0, 0), (0, LANES - a.shape[2]))).reshape(MLA_KV_RANK, HP)
    W["mla_w_ukv"] = jnp.concatenate([padh(ukv[:, :, :MLA_NOPE]), padh(ukv[:, :, MLA_NOPE:])], axis=1)
    wb = full["w_branch_b"].reshape(MLA_HEADS, MLA_V, D_MODEL)
    W["w_branch_b"] = jnp.pad(wb, ((0, 0), (0, LANES - MLA_V), (0, 0))).reshape(HP, D_MODEL)
    return W


def _reference_layout(G):
    out = dict(G)
    if "w_in" not in G:
        return out
    g = G["w_in"]
    out["w_in"] = jnp.concatenate([
        g[:, Z_U:Z_QM], g[:, Z_CQ:Z_COLS], g[:, Z_CKV:Z_KR], g[:, Z_KR + KR_LANE:Z_KR + KR_LANE + MLA_ROPE],
        g[:, Z_QM:Z_CKV], g[:, Z_G:Z_U]], axis=1)
    out["mla_w_uq"] = G["mla_w_uq"].reshape(MLA_Q_RANK, MLA_HEADS, LANES)[:, :, :MLA_QK].reshape(MLA_Q_RANK, -1)
    gk = G["mla_w_ukv"][:, :HP].reshape(MLA_KV_RANK, MLA_HEADS, LANES)[:, :, :MLA_NOPE]
    gv = G["mla_w_ukv"][:, HP:].reshape(MLA_KV_RANK, MLA_HEADS, LANES)[:, :, :MLA_V]
    out["mla_w_ukv"] = jnp.concatenate([gk, gv], axis=2).reshape(MLA_KV_RANK, -1)
    out["w_branch_b"] = G["w_branch_b"].reshape(MLA_HEADS, LANES, D_MODEL)[:, :MLA_V].reshape(-1, D_MODEL)
    return out


def _pack(parts):
    flat = []
    for a in parts:
        a = a.reshape(-1)
        flat.append(jnp.pad(a, (0, (-a.shape[0]) % LANES)))
    return jnp.concatenate(flat).reshape(-1, LANES)


def _unpack(packed, shapes):
    flat = packed.reshape(-1)
    out, off = [], 0
    for shp in shapes:
        n = int(np.prod(shp))
        out.append(flat[off:off + n].reshape(shp))
        off += n + (-n) % LANES
    return out


MESH = pl.DeviceIdType.MESH
HBM = pl.BlockSpec(memory_space=pltpu.HBM)


def _all_gather(shards):
    n = len(shards)

    def body(*refs):
        x_refs, out_refs, token_ref = refs[:n], refs[n:2 * n], refs[2 * n]
        send_sems, recv_sems, local_sems = refs[2 * n + 1:]
        x, y, c = lax.axis_index("x"), lax.axis_index("y"), lax.axis_index("c")
        me, sibling = (x, y, c), (x, y, 1 - c)
        chips = [(1 - x, y), (x, 1 - y), (1 - x, 1 - y)]
        token_ref[...] = jnp.zeros_like(token_ref)

        def slot(a, px, py, pc):
            return out_refs[a].at[4 * px + 2 * py + pc]

        def copy(a, k, block, to, src=None):
            return pltpu.make_async_remote_copy(
                src_ref=slot(a, *block) if src is None else src, dst_ref=slot(a, *block),
                send_sem=send_sems.at[7 * a + k], recv_sem=recv_sems.at[7 * a + k], device_id=to, device_id_type=MESH)

        arrays = range(n)
        mine = [pltpu.make_async_copy(x_refs[a], slot(a, *me), local_sems.at[a]) for a in arrays]
        for cp in mine:
            cp.start()
        first = [copy(a, 0, me, sibling, src=x_refs[a]) for a in arrays]
        first += [copy(a, 1 + j, me, (*chip, c), src=x_refs[a]) for j, chip in enumerate(chips) for a in arrays]
        for cp in first:
            cp.start()
        passed = []
        for j, chip in enumerate(chips):
            for a in arrays:
                copy(a, 1 + j, (*chip, c), me).wait_recv()
                passed.append(copy(a, 4 + j, (*chip, c), sibling))
                passed[-1].start()
        for a in arrays:
            copy(a, 0, sibling, me).wait_recv()
        for j, chip in enumerate(chips):
            for a in arrays:
                copy(a, 4 + j, (*chip, 1 - c), me).wait_recv()
        for cp in first + passed:
            cp.wait_send()
        for cp in mine:
            cp.wait()

    res = pl.pallas_call(
        body, name="all_gather_weights",
        out_shape=[jax.ShapeDtypeStruct((N_DEV,) + s.shape, s.dtype) for s in shards]
        + [jax.ShapeDtypeStruct((8, LANES), F32)],
        in_specs=[HBM] * n, out_specs=[HBM] * n + [pl.BlockSpec(memory_space=pltpu.VMEM)],
        scratch_shapes=[pltpu.SemaphoreType.DMA((7 * n,)), pltpu.SemaphoreType.DMA((7 * n,)),
                        pltpu.SemaphoreType.DMA((n,))],
    )(*shards)
    return res[:n], res[n]


SEM = pl.BlockSpec(memory_space=pltpu.SEMAPHORE)
DATAFLOW = pltpu.SideEffectType.DATAFLOW_SIDE_EFFECTING


def _peers():
    x, y, c = lax.axis_index("x"), lax.axis_index("y"), lax.axis_index("c")
    out = []
    for k in range(1, N_DEV):
        px = 1 - x if k & 4 else x
        py = 1 - y if k & 2 else y
        pc = 1 - c if k & 1 else c
        out.append((k, (px, py, pc), 4 * px + 2 * py + pc))
    return 4 * x + 2 * y + c, out


def _send_start(srcs, per_peer, name):
    n = len(srcs)
    lands = [lax.empty((N_DEV,) + (s.shape[1:] if per_peer else s.shape), s.dtype) for s in srcs]

    def body(*refs):
        src_refs, land_refs, send_sems, recv_sems, token = refs[:n], refs[n:2 * n], refs[2 * n], refs[2 * n + 1], refs[-1]
        me, peers = _peers()
        for a in range(n):
            for k, pid, pflat in peers:
                pltpu.make_async_remote_copy(
                    src_ref=src_refs[a].at[pflat] if per_peer else src_refs[a], dst_ref=land_refs[a].at[me],
                    send_sem=send_sems.at[7 * a + k - 1], recv_sem=recv_sems.at[7 * a + k - 1],
                    device_id=pid, device_id_type=MESH).start()
        token[...] = jnp.zeros_like(token)

    hbm = lambda a: pltpu.with_memory_space_constraint(a, pltpu.HBM)
    res = pl.pallas_call(
        body, name=name,
        out_shape=(pltpu.SemaphoreType.DMA((7 * n,)), pltpu.SemaphoreType.DMA((7 * n,)),
                   *[pltpu.HBM(a.shape, a.dtype) for a in srcs + lands], jax.ShapeDtypeStruct((8, LANES), F32)),
        in_specs=(HBM,) * (2 * n), out_specs=(SEM, SEM) + (HBM,) * (2 * n) + (pl.BlockSpec(memory_space=pltpu.VMEM),),
        input_output_aliases={i: 2 + i for i in range(2 * n)},
        compiler_params=pltpu.CompilerParams(has_side_effects=DATAFLOW),
    )(*[hbm(a) for a in srcs + lands])
    return (res[0], res[1], list(res[2:2 + n]), list(res[2 + n:2 + 2 * n])), res[-1]


def _send_wait(started, after, per_peer, name):
    send_sems, recv_sems, srcs_thru, lands_thru = started
    n = len(srcs_thru)

    def body(*refs):
        src_refs, land_refs, send_sems, recv_sems = refs[:n], refs[n:2 * n], refs[2 * n], refs[2 * n + 1]
        me, peers = _peers()
        for a in range(n):
            for k, pid, pflat in peers:
                copy = pltpu.make_async_remote_copy(
                    src_ref=src_refs[a].at[pflat] if per_peer else src_refs[a], dst_ref=land_refs[a].at[pflat],
                    send_sem=send_sems.at[7 * a + k - 1], recv_sem=recv_sems.at[7 * a + k - 1],
                    device_id=pid, device_id_type=MESH)
                copy.wait_send()
                copy.wait_recv()

    outs = pl.pallas_call(
        body, name=name,
        out_shape=tuple(pltpu.HBM(a.shape, a.dtype) for a in srcs_thru + lands_thru),
        in_specs=(HBM,) * (2 * n) + (SEM, SEM, pl.BlockSpec(memory_space=pl.ANY)), out_specs=(HBM,) * (2 * n),
        input_output_aliases={i: i for i in range(2 * n)},
        compiler_params=pltpu.CompilerParams(has_side_effects=DATAFLOW),
    )(*srcs_thru, *lands_thru, send_sems, recv_sems, after)
    me = 4 * lax.axis_index("x") + 2 * lax.axis_index("y") + lax.axis_index("c")
    landed = []
    for src_out, land in zip(outs[:n], outs[n:]):
        own = lax.dynamic_index_in_dim(src_out, me, 0, keepdims=True) if per_peer else src_out[None]
        landed.append(lax.dynamic_update_slice(land, own, (me,) + (0,) * (land.ndim - 1)))
    return landed


def _share_rows(block, name):
    def body(src_ref, out_ref, send_sems, recv_sems, local_sem):
        me, peers = _peers()
        own = pltpu.make_async_copy(src_ref, out_ref.at[me], local_sem)
        own.start()
        copies = [pltpu.make_async_remote_copy(
            src_ref=src_ref, dst_ref=out_ref.at[me], send_sem=send_sems.at[k - 1], recv_sem=recv_sems.at[k - 1],
            device_id=pid, device_id_type=MESH) for k, pid, _ in peers]
        for cp in copies:
            cp.start()
        for cp in copies:
            cp.wait()
        own.wait()

    return pl.pallas_call(
        body, name=name, out_shape=jax.ShapeDtypeStruct((N_DEV,) + block.shape, block.dtype),
        in_specs=[HBM], out_specs=HBM,
        scratch_shapes=[pltpu.SemaphoreType.DMA((N_DEV - 1,)), pltpu.SemaphoreType.DMA((N_DEV - 1,)),
                        pltpu.SemaphoreType.DMA],
    )(block)


def _sum_slots(recv, name, tr):
    n, rows, lanes = recv.shape
    tr = _tile(rows, tr)

    def body(r_ref, o_ref):
        acc = r_ref[0].astype(F32)
        for i in range(1, n):
            acc = acc + r_ref[i].astype(F32)
        o_ref[...] = acc

    return pl.pallas_call(
        body, name=name, grid=(rows // tr,),
        in_specs=[pl.BlockSpec((n, tr, lanes), lambda i: (0, i, 0))],
        out_specs=pl.BlockSpec((tr, lanes), lambda i: (i, 0)),
        out_shape=jax.ShapeDtypeStruct((rows, lanes), F32),
        compiler_params=_cparams(("parallel",)),
    )(recv)


def _adamw_math(w, g, m, v):
    m = ADAM_B1 * m + (1.0 - ADAM_B1) * g
    v = ADAM_B2 * v + (1.0 - ADAM_B2) * (g * g)
    m_hat = m / (1.0 - ADAM_B1 ** ADAM_STEP)
    v_hat = v / (1.0 - ADAM_B2 ** ADAM_STEP)
    return -ADAM_LR * (m_hat / (jnp.sqrt(v_hat) + ADAM_EPS) + ADAM_WD * w), m, v


def _adamw(w, g, m, v, name, tr=256):
    return _rowwise(_adamw_math, name, tr, [w, g, m, v], [], [(w.shape[1], F32)] * 3)


def _adamw_small(ws, gs, ms, vs):
    n = len(ws)

    def body(*refs):
        ins, outs = refs[:4 * n], refs[4 * n:]
        for i in range(n):
            d, m, v = _adamw_math(ins[i][...], ins[n + i][...], ins[2 * n + i][...], ins[3 * n + i][...])
            outs[i][...], outs[n + i][...], outs[2 * n + i][...] = d, m, v

    vmem = pl.BlockSpec(memory_space=pltpu.VMEM)
    res = pl.pallas_call(
        body, name="adamw_small", in_specs=[vmem] * (4 * n), out_specs=[vmem] * (3 * n),
        out_shape=[jax.ShapeDtypeStruct(w.shape, F32) for w in ws] * 3,
    )(*ws, *gs, *ms, *vs)
    return res[:n], res[n:2 * n], res[2 * n:]


def _sum_adamw(recv, w, m, v, name):
    n, r, c = recv.shape
    tr = _tile(r, 256)

    def body(r_ref, w_ref, m_ref, v_ref, g_ref, d_ref, nm_ref, nv_ref):
        g = r_ref[0].astype(F32)
        for i in range(1, n):
            g = g + r_ref[i].astype(F32)
        g_ref[...] = g
        d_ref[...], nm_ref[...], nv_ref[...] = _adamw_math(w_ref[...], g, m_ref[...], v_ref[...])

    row = pl.BlockSpec((None, tr, c), lambda i: (0, i, 0))
    return pl.pallas_call(
        body, name=name, grid=(r // tr,),
        in_specs=[pl.BlockSpec((n, tr, c), lambda i: (0, i, 0)), row, row, row], out_specs=[row] * 4,
        out_shape=[jax.ShapeDtypeStruct((1, r, c), F32)] * 4, compiler_params=_cparams(("parallel",)),
    )(recv, w, m, v)


def kernel(x, mem, positions, ffn1_norm, ffn1_w_gu, ffn1_w_down, mix_norm, w_in, b_gate, sg_ln_g, sg_ln_b, sg_w, sg_b, mla_cq_norm, mla_w_uq, mla_ckv_norm, mla_w_ukv, mla_q_norm, mla_k_norm, mem_norm, mem_w_kv, mem_q_norm, mem_k_norm, w_branch_a, w_branch_b, w_branch_c, w_out, ffn2_norm, ffn2_w_gu, ffn2_w_down, loss_target, m_ffn1_norm, m_ffn1_w_gu, m_ffn1_w_down, m_mix_norm, m_w_in, m_b_gate, m_sg_ln_g, m_sg_ln_b, m_sg_w, m_sg_b, m_mla_cq_norm, m_mla_w_uq, m_mla_ckv_norm, m_mla_w_ukv, m_mla_q_norm, m_mla_k_norm, m_mem_norm, m_mem_w_kv, m_mem_q_norm, m_mem_k_norm, m_w_branch_a, m_w_branch_b, m_w_branch_c, m_w_out, m_ffn2_norm, m_ffn2_w_gu, m_ffn2_w_down, v_ffn1_norm, v_ffn1_w_gu, v_ffn1_w_down, v_mix_norm, v_w_in, v_b_gate, v_sg_ln_g, v_sg_ln_b, v_sg_w, v_sg_b, v_mla_cq_norm, v_mla_w_uq, v_mla_ckv_norm, v_mla_w_ukv, v_mla_q_norm, v_mla_k_norm, v_mem_norm, v_mem_w_kv, v_mem_q_norm, v_mem_k_norm, v_w_branch_a, v_w_branch_b, v_w_branch_c, v_w_out, v_ffn2_norm, v_ffn2_w_gu, v_ffn2_w_down):
    given = dict(locals())
    wts = {n: given[n] for n in ORDER}
    mom = {n: given["m_" + n] for n in ORDER}
    var = {n: given["v_" + n] for n in ORDER}

    def shards(group, zero):
        out = [wts[n][0].astype(BF16) for n in GROUPS[group]]
        return [out[0] + zero.astype(BF16)] + out[1:]

    def full_weights(group, slabs):
        return _compute_layout({n: _full_from_slabs(n, s) for n, s in zip(GROUPS[group], slabs)})

    def zero_of(a):
        return jnp.minimum(jnp.abs(a.reshape(-1)[0]), 0)

    gathered_ffn1, token = _all_gather([wts[n][0].astype(BF16) for n in GROUPS["ffn1"]])
    flight = {}
    flight["ffn1_down"], token = _send_start(shards("ffn1_down", token[0, 0]), False, "gather_ffn1_down_start")
    flight["mix"] = _send_start(shards("mix", token[0, 0]), False, "gather_mix_start")[0]
    recv = {}

    def weights(group, after):
        if group == "ffn1":
            return full_weights(group, gathered_ffn1)
        landed = _send_wait(flight.pop(group), after, False, f"gather_{group}_wait")
        if group == "mix":
            flight["ffn2"] = _send_start(shards("ffn2", zero_of(landed[0])), False, "gather_ffn2_start")[0]
        return full_weights(group, landed)

    small_shapes = [wts[n].shape[1:] for n in SMALL]
    early = SMALL[1:]
    assert SMALL[0] == "ffn1_norm"

    def grads_out(group, G):
        Gr = _reference_layout({n: G[n] for n in GRAD_GROUPS[group]})
        parts = [_slabs_from_full(n, Gr[n]).astype(BF16) for n in GRAD_GROUPS[group]]
        flight["g_" + group], tie = _send_start(parts, True, f"grads_{group}_start")
        if group == "mix":
            small = _pack([G[n].reshape(s) for n, s in zip(early, small_shapes[1:])])
            small = jnp.pad(small, ((0, (-small.shape[0]) % 8), (0, 0)))
            flight["small"], tie = _send_start([small + tie[0, 0]], False, "grads_small_start")
        return tie

    P = {n: wts[n] if wts[n].ndim == 2 else wts[n][0] for n in SMALL}
    loss_part, grad_x, G = _local_step(x[0], mem[0], positions[0], loss_target[0], P, weights, grads_out)

    for group, names in GRAD_GROUPS.items():
        recv.update(zip(names, _send_wait(flight.pop("g_" + group), grad_x, True, f"grads_{group}_wait")))
    early_recv, = _send_wait(flight.pop("small"), grad_x, False, "grads_small_wait")
    last = _share_rows(G["ffn1_norm"].reshape(-1, LANES), "share_ffn1_norm")
    g_small_packed = _sum_slots(jnp.concatenate([last, early_recv], axis=1), "sum_small", 2048)

    grads, delta, new_m, new_v = {}, {}, {}, {}
    for n in SHARDED:
        grads[n], delta[n], new_m[n], new_v[n] = _sum_adamw(recv[n], wts[n], mom[n], var[n], "adamw_" + n)
    grads.update(zip(SMALL, _unpack(g_small_packed, small_shapes)))

    flat2 = lambda d: [d[n].reshape(-1, d[n].shape[-1]) for n in SMALL]
    for dst, vals in zip((delta, new_m, new_v), _adamw_small(flat2(wts), flat2(grads), flat2(mom), flat2(var))):
        dst.update(zip(SMALL, vals))

    loss = lax.psum(jnp.sum(loss_part), ("x", "y", "c"))
    lead = lambda d: [d[n].reshape(wts[n].shape) for n in ORDER]
    return (loss, grad_x[None], *lead(grads), *lead(delta), *lead(new_m), *lead(new_v))
```

```python
import functools

import numpy as np
import jax
import jax.numpy as jnp
from jax import lax
from jax.experimental import pallas as pl
from jax.experimental.pallas import tpu as pltpu

F32, BF16 = jnp.float32, jnp.bfloat16

D_MODEL = 1024
SG_GROUPS, SG_GROUP_DIM, SG_WIDTH, CHUNK = 8, 64, 512, 128
MLA_HEADS, MLA_NOPE, MLA_ROPE, MLA_V, MLA_QK = 8, 64, 32, 64, 96
MLA_Q_RANK, MLA_KV_RANK = 384, 256
MEM_HEADS, MEM_HEAD_DIM, MEM_WIDTH = 4, 128, 512
D_FF = 2816
ROPE_BASE = 10000.0
EPS = 1e-6
NEG = -1e30
ADAM_LR, ADAM_B1, ADAM_B2, ADAM_EPS, ADAM_WD, ADAM_STEP = 0.001, 0.9, 0.999, 1e-08, 0.01, 10

N_DEV = 8
LANES = 128
V7X_VMEM_LIMIT = 56 * 1024 * 1024
HP = MLA_HEADS * LANES

Z_G, Z_U, Z_V, Z_QM, Z_CKV, Z_KR, Z_CQ = 0, 3072, 3584, 4096, 4608, 4864, 4992
Z_COLS = 5376
KR_LANE = 64


def _tile(dim, pref):
    if dim <= pref:
        return dim
    for t in range(pref - pref % LANES, LANES - 1, -LANES):
        if dim % t == 0:
            return t
    for t in range(pref - pref % 8, 7, -8):
        if dim % t == 0:
            return t
    return dim


def _cparams(sem):
    return pltpu.CompilerParams(dimension_semantics=sem, vmem_limit_bytes=V7X_VMEM_LIMIT)


_DN = {"nn": ((1,), (0,)), "nt": ((1,), (1,)), "tn": ((0,), (0,))}


def _dot(a, b, mode="nn"):
    return lax.dot_general(a.astype(BF16), b.astype(BF16), (_DN[mode], ((), ())),
                           preferred_element_type=F32)


def _mm(a, b, mode, out_dtype, name, tm=512, tn=512, tk=2048, tie=None):
    if mode == "tn":
        K, M = a.shape
    else:
        M, K = a.shape
    N = b.shape[0] if mode == "nt" else b.shape[1]
    tm, tn, tk = _tile(M, tm), _tile(N, tn), _tile(K, tk)
    nk = K // tk
    if mode == "tn":
        a_spec = pl.BlockSpec((tk, tm), lambda i, j, k: (k, i))
    else:
        a_spec = pl.BlockSpec((tm, tk), lambda i, j, k: (i, k))
    if mode == "nt":
        b_spec = pl.BlockSpec((tn, tk), lambda i, j, k: (j, k))
    else:
        b_spec = pl.BlockSpec((tk, tn), lambda i, j, k: (k, j))

    ties = [] if tie is None else [tie]

    def body(a_ref, b_ref, *rest):
        o_ref, *scratch = rest[len(ties):]
        p = _dot(a_ref[...], b_ref[...], mode)
        if nk == 1:
            o_ref[...] = p.astype(o_ref.dtype)
        else:
            acc_ref, = scratch
            k = pl.program_id(2)

            @pl.when(k == 0)
            def _():
                acc_ref[...] = p

            @pl.when(k > 0)
            def _():
                acc_ref[...] += p

            @pl.when(k == nk - 1)
            def _():
                o_ref[...] = acc_ref[...].astype(o_ref.dtype)

    return pl.pallas_call(
        body, name=name, grid=(M // tm, N // tn, nk),
        in_specs=[a_spec, b_spec] + [pl.BlockSpec(t.shape, lambda i, j, k: (0, 0)) for t in ties],
        out_specs=pl.BlockSpec((tm, tn), lambda i, j, k: (i, j)),
        out_shape=jax.ShapeDtypeStruct((M, N), out_dtype),
        scratch_shapes=[] if nk == 1 else [pltpu.VMEM((tm, tn), F32)],
        compiler_params=_cparams(("parallel", "parallel", "arbitrary")),
    )(a, b, *ties)


def _mm_t(at, b, name, tm, tn, tk=1024, tie=None):
    return _mm(at, b, "nn", BF16, name, tm=tm, tn=tn, tk=tk, tie=tie)


def _rowwise(fn, name, tr, row_ins, bc_ins, row_outs, acc_outs=()):
    norm = [it if isinstance(it, tuple) else (it, it.shape[1], 0) for it in row_ins]
    rows = norm[0][0].shape[0]
    tr = _tile(rows, tr)
    arrays, in_specs = [], []
    for arr, w, cb in norm:
        arrays.append(arr)
        in_specs.append(pl.BlockSpec((tr, w), lambda i, cb=cb: (i, cb)))
    for arr in bc_ins:
        arrays.append(arr)
        in_specs.append(pl.BlockSpec(arr.shape, lambda i, nd=arr.ndim: (0,) * nd))
    n_in, n_row = len(arrays), len(row_outs)
    out_shape, out_specs, aliases = [], [], {}
    transposed = [len(o) == 3 for o in row_outs]
    for k, o in enumerate(row_outs):
        if o[0] == "into":
            _, target, w, cb = o
            aliases[len(arrays)] = k
            arrays.append(target)
            in_specs.append(pl.BlockSpec(memory_space=pl.ANY))
            out_shape.append(jax.ShapeDtypeStruct(target.shape, target.dtype))
            out_specs.append(pl.BlockSpec((tr, w), lambda i, cb=cb: (i, cb)))
        elif transposed[k]:
            out_shape.append(jax.ShapeDtypeStruct((o[0], rows), o[1]))
            out_specs.append(pl.BlockSpec((o[0], tr), lambda i: (0, i)))
        else:
            out_shape.append(jax.ShapeDtypeStruct((rows, o[0]), o[1]))
            out_specs.append(pl.BlockSpec((tr, o[0]), lambda i: (i, 0)))
    for shp, dt in acc_outs:
        out_shape.append(jax.ShapeDtypeStruct(shp, dt))
        out_specs.append(pl.BlockSpec(shp, lambda i, nd=len(shp): (0,) * nd))

    def body(*refs):
        vals = fn(*[r[...].astype(F32) for r in refs[:n_in]])
        if not isinstance(vals, (tuple, list)):
            vals = (vals,)
        outs = refs[len(arrays):]
        for r, v, t in zip(outs[:n_row], vals[:n_row], transposed):
            r[...] = v.astype(F32).T.astype(r.dtype) if t else v.astype(r.dtype)
        if acc_outs:
            accs = list(zip(outs[n_row:], vals[n_row:]))
            i = pl.program_id(0)

            @pl.when(i == 0)
            def _():
                for r, v in accs:
                    r[...] = v.astype(r.dtype)

            @pl.when(i > 0)
            def _():
                for r, v in accs:
                    r[...] += v.astype(r.dtype)

    res = pl.pallas_call(
        body, name=name, grid=(rows // tr,), in_specs=in_specs, out_specs=out_specs,
        out_shape=out_shape, input_output_aliases=aliases, compiler_params=_cparams(("arbitrary",)),
    )(*arrays)
    return res


def _rsum(x):
    return jnp.sum(x, axis=0, keepdims=True)


def _rms(x, g, n=None):
    n = x.shape[-1] if n is None else n
    r = lax.rsqrt(jnp.sum(x * x, axis=-1, keepdims=True) * (1.0 / n) + EPS)
    return x * r * g


def _rms_bwd(x, g, dy, n=None):
    n = x.shape[-1] if n is None else n
    r = lax.rsqrt(jnp.sum(x * x, axis=-1, keepdims=True) * (1.0 / n) + EPS)
    xh = x * r
    dxh = dy * g
    dx = r * (dxh - xh * (jnp.sum(dxh * xh, axis=-1, keepdims=True) * (1.0 / n)))
    return dx, _rsum(dy * xh)


def _gelu(x):
    return 0.5 * x * (1.0 + lax.erf(x * 0.7071067811865476))


def _gelu_grad(x):
    return 0.5 * (1.0 + lax.erf(x * 0.7071067811865476)) + x * jnp.exp(-0.5 * x * x) * 0.3989422804014327


def _sigmoid(x):
    return 0.5 * jnp.tanh(0.5 * x) + 0.5


FFN_TM, FFN_TN = 1024, 1408
MXU_WIDTH = 256


def _col_chunks(n):
    return [(c, min(c + MXU_WIDTH, n)) for c in range(0, n, MXU_WIDTH)]


def _ffn_gu_act(h, w_gu, tag):
    T = h.shape[0]
    tm, tn = _tile(T, FFN_TM), FFN_TN
    nj = D_FF // tn

    def body(h_ref, wg_ref, wu_ref, gu_ref, a_ref, at_ref):
        h = h_ref[...]
        for c0, c1 in _col_chunks(tn):
            g = _dot(h, wg_ref[:, c0:c1])
            u = _dot(h, wu_ref[:, c0:c1])
            gu_ref[0, :, c0:c1] = g.astype(BF16)
            gu_ref[1, :, c0:c1] = u.astype(BF16)
            a = g * _sigmoid(g) * u
            a_ref[:, c0:c1] = a.astype(BF16)
            at_ref[c0:c1, :] = a.T.astype(BF16)

    return pl.pallas_call(
        body, name=f"{tag}_gu_act", grid=(T // tm, nj),
        in_specs=[pl.BlockSpec((tm, D_MODEL), lambda i, j: (i, 0)),
                  pl.BlockSpec((D_MODEL, tn), lambda i, j: (0, j)),
                  pl.BlockSpec((D_MODEL, tn), lambda i, j: (0, j + nj))],
        out_specs=[pl.BlockSpec((2, tm, tn), lambda i, j: (0, i, j)),
                   pl.BlockSpec((tm, tn), lambda i, j: (i, j)),
                   pl.BlockSpec((tn, tm), lambda i, j: (j, i))],
        out_shape=[jax.ShapeDtypeStruct((2, T, D_FF), BF16), jax.ShapeDtypeStruct((T, D_FF), BF16),
                   jax.ShapeDtypeStruct((D_FF, T), BF16)],
        compiler_params=_cparams(("parallel", "parallel")),
    )(h, w_gu, w_gu)


def _ffn_da_actbwd(do, w_down, gu, tag, tie=None):
    T = do.shape[0]
    tm, tn = _tile(T, FFN_TM), FFN_TN
    ties = [] if tie is None else [tie]

    def body(do_ref, wd_ref, gu_ref, *rest):
        dgu_ref = rest[-1]
        do = do_ref[...]
        for c0, c1 in _col_chunks(tn):
            da = _dot(do, wd_ref[c0:c1, :], "nt")
            g = gu_ref[0, :, c0:c1].astype(F32)
            u = gu_ref[1, :, c0:c1].astype(F32)
            s = _sigmoid(g)
            dgu_ref[0, :, c0:c1] = (da * u * s * (1.0 + g * (1.0 - s))).astype(BF16)
            dgu_ref[1, :, c0:c1] = (da * g * s).astype(BF16)

    return pl.pallas_call(
        body, name=f"{tag}_da_actbwd", grid=(T // tm, D_FF // tn),
        in_specs=[pl.BlockSpec((tm, D_MODEL), lambda i, j: (i, 0)),
                  pl.BlockSpec((tn, D_MODEL), lambda i, j: (j, 0)),
                  pl.BlockSpec((2, tm, tn), lambda i, j: (0, i, j))]
        + [pl.BlockSpec(t.shape, lambda i, j: (0, 0)) for t in ties],
        out_specs=pl.BlockSpec((2, tm, tn), lambda i, j: (0, i, j)),
        out_shape=jax.ShapeDtypeStruct((2, T, D_FF), BF16),
        compiler_params=_cparams(("parallel", "parallel")),
    )(do, w_down, gu, *ties)


def _ffn_dwgu(ht, dgu, tag, tk=2048):
    T = ht.shape[1]
    tn, tk = FFN_TN, _tile(T, tk)
    nj, nk = D_FF // tn, T // tk

    def body(a_ref, b_ref, o_ref, acc_ref):
        k = pl.program_id(1)
        p = _dot(a_ref[...], b_ref[...])

        @pl.when(k == 0)
        def _():
            acc_ref[...] = p

        @pl.when(k > 0)
        def _():
            acc_ref[...] += p

        @pl.when(k == nk - 1)
        def _():
            o_ref[...] = acc_ref[...].astype(o_ref.dtype)

    return pl.pallas_call(
        body, name=f"{tag}_dwgu", grid=(2 * nj, nk),
        in_specs=[pl.BlockSpec((D_MODEL, tk), lambda n, k: (0, k)),
                  pl.BlockSpec((None, tk, tn), lambda n, k: (n // nj, k, n % nj))],
        out_specs=pl.BlockSpec((D_MODEL, tn), lambda n, k: (0, n)),
        out_shape=jax.ShapeDtypeStruct((D_MODEL, 2 * D_FF), BF16),
        scratch_shapes=[pltpu.VMEM((D_MODEL, tn), F32)],
        compiler_params=_cparams(("parallel", "arbitrary")),
    )(ht, dgu)


def _ffn_dh(dgu, w_gu, tag, tm=2048, tie=None):
    T = dgu.shape[1]
    tm, tk = _tile(T, tm), FFN_TN
    nk = D_FF // tk
    ties = [] if tie is None else [tie]

    def body(a_ref, b_ref, *rest):
        o_ref, acc_ref = rest[len(ties):]
        k = pl.program_id(1)
        p = _dot(a_ref[...], b_ref[...], "nt")

        @pl.when(k == 0)
        def _():
            acc_ref[...] = p

        @pl.when(k > 0)
        def _():
            acc_ref[...] += p

        @pl.when(k == 2 * nk - 1)
        def _():
            o_ref[...] = acc_ref[...].astype(o_ref.dtype)

    return pl.pallas_call(
        body, name=f"{tag}_dh", grid=(T // tm, 2 * nk),
        in_specs=[pl.BlockSpec((None, tm, tk), lambda i, k: (k // nk, i, k % nk)),
                  pl.BlockSpec((D_MODEL, tk), lambda i, k: (0, k))]
        + [pl.BlockSpec(t.shape, lambda i, k: (0, 0)) for t in ties],
        out_specs=pl.BlockSpec((tm, D_MODEL), lambda i, k: (i, 0)),
        out_shape=jax.ShapeDtypeStruct((T, D_MODEL), BF16),
        scratch_shapes=[pltpu.VMEM((tm, D_MODEL), F32)],
        compiler_params=_cparams(("parallel", "arbitrary")),
    )(dgu, w_gu, *ties)


def _ffn_fwd(h, w_gu, w_down, tag):
    gu, a, at = _ffn_gu_act(h, w_gu, tag)
    if callable(w_down):
        w_down = w_down(at)
    o = _mm(a, w_down, "nn", BF16, f"{tag}_down", tm=1024, tn=1024, tk=2816)
    return gu, at, o


def _ffn_bwd(do, ht, gu, at, w_gu, w_down, tag, tie=None, on_dw=None):
    on_dw = on_dw or (lambda which, dw: None)
    dw_down = _mm_t(at, do, f"{tag}_dwdown", tm=1408, tn=1024, tk=2048, tie=tie)
    dgu = _ffn_da_actbwd(do, w_down, gu, tag, tie=on_dw("down", dw_down))
    dw_gu = _ffn_dwgu(ht, dgu, tag)
    dh = _ffn_dh(dgu, w_gu, tag, tie=on_dw("gu", dw_gu))
    return dh, dw_gu, dw_down


def _sg_common(u_pre, v_pre, ln_g, ln_b):
    u = _gelu(u_pre)
    v = _gelu(v_pre)
    mu = jnp.mean(v, axis=-1, keepdims=True)
    vc = v - mu
    rstd = lax.rsqrt(jnp.mean(vc * vc, axis=-1, keepdims=True) + EPS)
    vhat = vc * rstd
    vl = vhat * ln_g + ln_b
    return u, vhat, rstd, vl


def _sg_masked_pairs(w):
    t = lax.broadcasted_iota(jnp.int32, (CHUNK, CHUNK), 0)
    s = lax.broadcasted_iota(jnp.int32, (CHUNK, CHUNK), 1)
    causal = s <= t
    wm = [jnp.where(causal, w[g], 0.0).astype(BF16) for g in range(SG_GROUPS)]
    return [jnp.concatenate([wm[2 * j], wm[2 * j + 1]], axis=0) for j in range(SG_GROUPS // 2)], causal


def _sg_mix(vl, pairs, bias):
    tr = vl.shape[0]
    low = lax.broadcasted_iota(jnp.int32, (CHUNK, LANES), 1) < SG_GROUP_DIM
    vb = vl.astype(BF16)
    rows = []
    for c in range(tr // CHUNK):
        slabs = []
        for j in range(SG_GROUPS // 2):
            slab = vb[c * CHUNK:(c + 1) * CHUNK, j * LANES:(j + 1) * LANES]
            m = _dot(pairs[j], slab)
            slabs.append(jnp.where(low, m[:CHUNK], m[CHUNK:]))
        rows.append(jnp.concatenate(slabs, axis=1) + bias)
    return jnp.concatenate(rows, axis=0)


def _sg_fwd(z, ln_g, ln_b, sg_w, bias_full):
    def fn(u_pre, v_pre, ln_g, ln_b, w, bias):
        u, _, _, vl = _sg_common(u_pre, v_pre, ln_g, ln_b)
        pairs, _ = _sg_masked_pairs(w)
        y = u * _sg_mix(vl, pairs, bias)
        return y, y

    return _rowwise(fn, "sg_fwd", 512, [(z, SG_WIDTH, Z_U // SG_WIDTH), (z, SG_WIDTH, Z_V // SG_WIDTH)],
                    [ln_g, ln_b, sg_w, bias_full], [(SG_WIDTH, BF16), (SG_WIDTH, BF16, "T")])


def _sg_bwd(z, dy, ln_g, ln_b, sg_w, bias_full, group_ind, dz):
    def fn(u_pre, v_pre, dy, ln_g, ln_b, w, bias, ind):
        dy = dy.astype(F32)
        u, vhat, rstd, vl = _sg_common(u_pre, v_pre, ln_g, ln_b)
        pairs, causal = _sg_masked_pairs(w)
        mixed = _sg_mix(vl, pairs, bias)
        du_pre = dy * mixed * _gelu_grad(u_pre)
        dmix = dy * u
        tr = dy.shape[0]
        low = lax.broadcasted_iota(jnp.int32, (CHUNK, LANES), 1) < SG_GROUP_DIM
        vb = vl.astype(BF16)
        dw = [jnp.zeros((CHUNK, CHUNK), F32) for _ in range(SG_GROUPS)]
        dbias = jnp.zeros((CHUNK, SG_WIDTH), F32)
        dvl_rows = []
        for c in range(tr // CHUNK):
            dm_c = dmix[c * CHUNK:(c + 1) * CHUNK]
            dbias = dbias + dm_c
            slabs = []
            for j in range(SG_GROUPS // 2):
                slab = vb[c * CHUNK:(c + 1) * CHUNK, j * LANES:(j + 1) * LANES]
                dm = dm_c[:, j * LANES:(j + 1) * LANES]
                d0 = jnp.where(low, dm, 0.0).astype(BF16)
                d1 = jnp.where(low, 0.0, dm).astype(BF16)
                dw[2 * j] = dw[2 * j] + _dot(d0, slab, "nt")
                dw[2 * j + 1] = dw[2 * j + 1] + _dot(d1, slab, "nt")
                slabs.append(_dot(pairs[j], jnp.concatenate([d0, d1], axis=0), "tn"))
            dvl_rows.append(jnp.concatenate(slabs, axis=1))
        dvl = jnp.concatenate(dvl_rows, axis=0)
        dln_g = _rsum(dvl * vhat)
        dln_b = _rsum(dvl)
        dvh = dvl * ln_g
        dv = rstd * (dvh - jnp.mean(dvh, axis=-1, keepdims=True)
                     - vhat * jnp.mean(dvh * vhat, axis=-1, keepdims=True))
        dv_pre = dv * _gelu_grad(v_pre)
        dw = jnp.stack([jnp.where(causal, d, 0.0) for d in dw], axis=0)
        dbias_t = lax.dot_general(dbias, ind, (((1,), (0,)), ((), ())), precision=lax.Precision.HIGHEST,
                                  preferred_element_type=F32)
        return jnp.concatenate([du_pre, dv_pre], axis=1), dw, dbias_t, dln_g, dln_b

    return _rowwise(fn, "sg_bwd", 512,
                    [(z, SG_WIDTH, Z_U // SG_WIDTH), (z, SG_WIDTH, Z_V // SG_WIDTH), dy],
                    [ln_g, ln_b, sg_w, bias_full, group_ind],
                    [("into", dz, 2 * SG_WIDTH, Z_U // (2 * SG_WIDTH))],
                    [((SG_GROUPS, CHUNK, CHUNK), F32), ((CHUNK, SG_GROUPS), F32), ((1, SG_WIDTH), F32), ((1, SG_WIDTH), F32)])


def _dot_f32(a, b):
    return lax.dot_general(a, b, (((1,), (0,)), ((), ())), precision=lax.Precision.HIGHEST,
                           preferred_element_type=F32)


def _head_consts():
    j = lax.broadcasted_iota(jnp.int32, (LANES, LANES), 0)
    i = lax.broadcasted_iota(jnp.int32, (LANES, LANES), 1)
    half = MLA_ROPE // 2
    first = (i >= MLA_NOPE) & (i < MLA_NOPE + half) & (j == i + half)
    second = (i >= MLA_NOPE + half) & (i < MLA_QK) & (j == i - half)
    return jnp.ones((LANES, LANES), F32), jnp.where(first | second, 1.0, 0.0).astype(F32)


def _rope(x, c, s, perm):
    return x * c + _dot_f32(x, perm) * s


def _rope_t(d, c, s, perm):
    return d * c + _dot_f32(d * s, perm)


def _rms_head(x, g, ones):
    r = lax.rsqrt(_dot_f32(x * x, ones) * (1.0 / MLA_QK) + EPS)
    return x * r * g


def _rms_head_bwd(x, g, dy, ones):
    r = lax.rsqrt(_dot_f32(x * x, ones) * (1.0 / MLA_QK) + EPS)
    xh = x * r
    dxh = dy * g
    dx = r * (dxh - xh * (_dot_f32(dxh * xh, ones) * (1.0 / MLA_QK)))
    return dx, _rsum(dy * xh)


def _mla_post(q_pre, kv_pre, z, tabs, gq, gk):
    scale = MLA_QK ** -0.5 * LOG2E
    T = q_pre.shape[0]
    tr = _tile(T, 256)

    def body(q_ref, k_ref, v_ref, kr_ref, c_ref, s1_ref, s2_ref, gq_ref, gk_ref, qo_ref, ko_ref, vo_ref):
        kr = kr_ref[...].astype(F32)
        c, s, gq, gk = c_ref[...], s1_ref[...] + s2_ref[...], gq_ref[...], gk_ref[...]
        ones, perm = _head_consts()
        ones_lane = lax.broadcasted_iota(jnp.int32, (tr, LANES), 1) == ONES_LANE
        for h in range(MLA_HEADS):
            sl = slice(h * LANES, (h + 1) * LANES)
            qo_ref[:, sl] = (_rope(_rms_head(q_ref[:, sl].astype(F32), gq, ones), c, s, perm) * scale).astype(BF16)
            ko_ref[:, sl] = _rope(_rms_head(k_ref[:, sl].astype(F32) + kr, gk, ones), c, s, perm).astype(BF16)
            vo_ref[:, sl] = jnp.where(ones_lane, 1.0, v_ref[:, sl].astype(F32)).astype(BF16)

    wide = lambda cb: pl.BlockSpec((tr, HP), lambda i, cb=cb: (i, cb))
    lanes = lambda cb: pl.BlockSpec((tr, LANES), lambda i, cb=cb: (i, cb))
    gain = pl.BlockSpec((1, LANES), lambda i: (0, 0))
    return pl.pallas_call(
        body, name="mla_post", grid=(T // tr,),
        in_specs=[wide(0), wide(0), wide(1), lanes(Z_KR // LANES), lanes(0), lanes(0), lanes(0), gain, gain],
        out_specs=[wide(0)] * 3, out_shape=[jax.ShapeDtypeStruct((T, HP), BF16)] * 3,
        compiler_params=_cparams(("parallel",)),
    )(q_pre, kv_pre, kv_pre, z, *tabs, gq, gk)


def _mla_post_bwd(q_pre, kv_pre, z, tabs, gq, gk, dq, dk, dv):
    scale = MLA_QK ** -0.5
    T = q_pre.shape[0]
    tr = _tile(T, 256)

    def body(q_ref, k_ref, kr_ref, c_ref, s1_ref, s2_ref, dq_ref, dk_ref, dv_ref, gq_ref, gk_ref,
             dqo_ref, dkvo_ref, dkro_ref, dgq_ref, dgk_ref):
        kr = kr_ref[...].astype(F32)
        c, s, gq, gk = c_ref[...], s1_ref[...] + s2_ref[...], gq_ref[...], gk_ref[...]
        ones, perm = _head_consts()
        lane = lax.broadcasted_iota(jnp.int32, (1, LANES), 1)
        kr_mask = (lane >= KR_LANE) & (lane < KR_LANE + MLA_ROPE)
        dgq = jnp.zeros((1, LANES), F32)
        dgk = jnp.zeros((1, LANES), F32)
        dkr = jnp.zeros((tr, LANES), F32)
        for h in range(MLA_HEADS):
            sl = slice(h * LANES, (h + 1) * LANES)
            dqn = _rope_t(dq_ref[:, sl].astype(F32), c, s, perm) * scale
            dx, dg = _rms_head_bwd(q_ref[:, sl].astype(F32), gq, dqn, ones)
            dqo_ref[:, sl] = dx.astype(BF16)
            dgq = dgq + dg
            dkn = _rope_t(dk_ref[:, sl].astype(F32), c, s, perm)
            dx, dg = _rms_head_bwd(k_ref[:, sl].astype(F32) + kr, gk, dkn, ones)
            dkvo_ref[:, sl] = dx.astype(BF16)
            dkvo_ref[:, HP + h * LANES:HP + (h + 1) * LANES] = dv_ref[:, sl]
            dgk = dgk + dg
            dkr = dkr + dx
        dkro_ref[...] = jnp.where(kr_mask, dkr, 0.0).astype(BF16)
        i = pl.program_id(0)

        @pl.when(i == 0)
        def _():
            dgq_ref[...] = dgq
            dgk_ref[...] = dgk

        @pl.when(i > 0)
        def _():
            dgq_ref[...] += dgq
            dgk_ref[...] += dgk

    wide = lambda cb: pl.BlockSpec((tr, HP), lambda i, cb=cb: (i, cb))
    lanes = lambda cb: pl.BlockSpec((tr, LANES), lambda i, cb=cb: (i, cb))
    gain = pl.BlockSpec((1, LANES), lambda i: (0, 0))
    return pl.pallas_call(
        body, name="mla_post_bwd", grid=(T // tr,),
        in_specs=[wide(0), wide(0), lanes(Z_KR // LANES), lanes(0), lanes(0), lanes(0), wide(0), wide(0), wide(0),
                  gain, gain],
        out_specs=[wide(0), pl.BlockSpec((tr, 2 * HP), lambda i: (i, 0)), lanes(0), gain, gain],
        out_shape=[jax.ShapeDtypeStruct((T, HP), BF16), jax.ShapeDtypeStruct((T, 2 * HP), BF16),
                   jax.ShapeDtypeStruct((T, LANES), BF16), jax.ShapeDtypeStruct((1, LANES), F32),
                   jax.ShapeDtypeStruct((1, LANES), F32)],
        compiler_params=_cparams(("arbitrary",)),
    )(q_pre, kv_pre, z, *tabs, dq, dk, dv, gq, gk)


def _pairs(n, lower):
    a, b = [], []
    for o in range(n):
        inner = range(o + 1) if lower else range(o, n)
        for t in inner:
            a.append(o)
            b.append(t)
    return jnp.asarray(np.array(a, np.int32)), jnp.asarray(np.array(b, np.int32))


FLASH_TILE, FLASH_SUB_ROWS = 2048, 512
LOG2E, LN2 = 1.4426950408889634, 0.6931471805599453
ONES_LANE = MLA_V


def _flash_tiles(T):
    tq = _tile(T, FLASH_TILE)
    return tq, _tile(tq, FLASH_SUB_ROWS)


def _col_span(t, sr, rb, diag, key_major):
    if not diag:
        return 0, t
    return (rb * sr, t) if key_major else (0, (rb + 1) * sr)


def _span_iota(sr, rb, c0, c1):
    r = lax.broadcasted_iota(jnp.int32, (sr, c1 - c0), 0) + rb * sr
    c = lax.broadcasted_iota(jnp.int32, (sr, c1 - c0), 1) + c0
    return r, c


def _lanes(x, width):
    return jnp.concatenate([x] * (width // LANES), axis=1)


def _flash_fwd(q, k, v):
    T = q.shape[0]
    tq, sr = _flash_tiles(T)
    n = T // tq
    ii, jj = _pairs(n, True)

    def body(ii_ref, jj_ref, q_ref, k_ref, v_ref, o_ref, ot_ref, lse_ref, lset_ref, m_sc, acc_sc):
        p_ = pl.program_id(1)
        i, j = ii_ref[p_], jj_ref[p_]

        @pl.when(j == 0)
        def _():
            m_sc[...] = jnp.full(m_sc.shape, NEG, F32)
            acc_sc[...] = jnp.zeros(acc_sc.shape, F32)

        def tile(diag):
            for rb in range(tq // sr):
                rows = slice(rb * sr, (rb + 1) * sr)
                c0, c1 = _col_span(tq, sr, rb, diag, False)
                s = _dot(q_ref[rows, :], k_ref[c0:c1, :], "nt")
                if diag:
                    r, c = _span_iota(sr, rb, c0, c1)
                    s = jnp.where(c <= r, s, NEG)
                m = m_sc[rows, :]
                m_new = jnp.maximum(m, jnp.max(s, axis=1, keepdims=True))
                p = jnp.exp2(s - _lanes(m_new, c1 - c0))
                acc_sc[rows, :] = jnp.exp2(m - m_new) * acc_sc[rows, :] + _dot(p, v_ref[c0:c1, :])
                m_sc[rows, :] = m_new

        @pl.when(j < i)
        def _():
            tile(False)

        @pl.when(j == i)
        def _():
            tile(True)
            acc = acc_sc[...]
            lane = lax.broadcasted_iota(jnp.int32, acc.shape, 1)
            l = jnp.sum(jnp.where(lane == ONES_LANE, acc, 0.0), axis=1, keepdims=True)
            o = jnp.where(lane < MLA_V, acc / l, 0.0)
            o_ref[...] = o.astype(o_ref.dtype)
            ot_ref[...] = o.T.astype(ot_ref.dtype)
            lse = m_sc[...] + jnp.log2(l)
            lse_ref[...] = lse
            lset_ref[...] = lse.T[:8]

    blk = lambda which: pl.BlockSpec((tq, LANES), which)
    qmap = lambda h, p, ii, jj: (ii[p], h)
    kmap = lambda h, p, ii, jj: (jj[p], h)
    tmap = lambda h, p, ii, jj: (h, ii[p])
    return pl.pallas_call(
        body, name="mla_flash_fwd",
        grid_spec=pltpu.PrefetchScalarGridSpec(
            num_scalar_prefetch=2, grid=(MLA_HEADS, int(ii.shape[0])),
            in_specs=[blk(qmap), blk(kmap), blk(kmap)],
            out_specs=[blk(qmap), pl.BlockSpec((LANES, tq), tmap), blk(qmap), pl.BlockSpec((8, tq), tmap)],
            scratch_shapes=[pltpu.VMEM((tq, LANES), F32)] * 2),
        out_shape=[jax.ShapeDtypeStruct((T, HP), BF16), jax.ShapeDtypeStruct((HP, T), BF16),
                   jax.ShapeDtypeStruct((T, HP), F32), jax.ShapeDtypeStruct((8 * MLA_HEADS, T), F32)],
        compiler_params=_cparams(("parallel", "arbitrary")),
    )(ii, jj, q, k, v)


def _flash_dq(q, k, v, do, lse, delta):
    T = q.shape[0]
    tq, sr = _flash_tiles(T)
    n = T // tq
    ii, jj = _pairs(n, True)

    def body(ii_ref, jj_ref, q_ref, k_ref, v_ref, do_ref, lse_ref, dl_ref, dq_ref, acc_sc):
        p_ = pl.program_id(1)
        i, j = ii_ref[p_], jj_ref[p_]

        @pl.when(j == 0)
        def _():
            acc_sc[...] = jnp.zeros(acc_sc.shape, F32)

        def tile(diag):
            for rb in range(tq // sr):
                rows = slice(rb * sr, (rb + 1) * sr)
                c0, c1 = _col_span(tq, sr, rb, diag, False)
                ks = k_ref[c0:c1, :]
                p = jnp.exp2(_dot(q_ref[rows, :], ks, "nt") - _lanes(lse_ref[rows, :], c1 - c0))
                if diag:
                    r, c = _span_iota(sr, rb, c0, c1)
                    p = jnp.where(c <= r, p, 0.0)
                dp = _dot(do_ref[rows, :], v_ref[c0:c1, :], "nt")
                acc_sc[rows, :] += _dot(p * (dp - _lanes(dl_ref[rows, :], c1 - c0)), ks)

        @pl.when(j < i)
        def _():
            tile(False)

        @pl.when(j == i)
        def _():
            tile(True)
            dq_ref[...] = acc_sc[...].astype(dq_ref.dtype)

    blk = lambda which: pl.BlockSpec((tq, LANES), which)
    qmap = lambda h, p, ii, jj: (ii[p], h)
    kmap = lambda h, p, ii, jj: (jj[p], h)
    return pl.pallas_call(
        body, name="mla_flash_dq",
        grid_spec=pltpu.PrefetchScalarGridSpec(
            num_scalar_prefetch=2, grid=(MLA_HEADS, int(ii.shape[0])),
            in_specs=[blk(qmap), blk(kmap), blk(kmap), blk(qmap), blk(qmap), blk(qmap)],
            out_specs=blk(qmap),
            scratch_shapes=[pltpu.VMEM((tq, LANES), F32)]),
        out_shape=jax.ShapeDtypeStruct((T, HP), BF16),
        compiler_params=_cparams(("parallel", "arbitrary")),
    )(ii, jj, q, k, v, do, lse, delta)


def _flash_dkv(q, k, v, do, lse_t, delta_t):
    T = q.shape[0]
    tq, sr = _flash_tiles(T)
    n = T // tq
    jj, ii = _pairs(n, False)

    def body(jj_ref, ii_ref, q_ref, k_ref, v_ref, do_ref, lse_ref, dl_ref, dk_ref, dv_ref, dk_sc, dv_sc):
        p_ = pl.program_id(1)
        j, i = jj_ref[p_], ii_ref[p_]

        @pl.when(i == j)
        def _():
            dk_sc[...] = jnp.zeros(dk_sc.shape, F32)
            dv_sc[...] = jnp.zeros(dv_sc.shape, F32)

        def tile(diag):
            for rb in range(tq // sr):
                rows = slice(rb * sr, (rb + 1) * sr)
                c0, c1 = _col_span(tq, sr, rb, diag, True)
                qs, dos = q_ref[c0:c1, :], do_ref[c0:c1, :]
                pt = jnp.exp2(_dot(k_ref[rows, :], qs, "nt") - lse_ref[:1, c0:c1])
                if diag:
                    r, c = _span_iota(sr, rb, c0, c1)
                    pt = jnp.where(r <= c, pt, 0.0)
                dpt = _dot(v_ref[rows, :], dos, "nt")
                dv_sc[rows, :] += _dot(pt, dos)
                dk_sc[rows, :] += _dot(pt * (dpt - dl_ref[:1, c0:c1]), qs)

        @pl.when(i == j)
        def _():
            tile(True)

        @pl.when(i > j)
        def _():
            tile(False)

        @pl.when(i == n - 1)
        def _():
            dk_ref[...] = (dk_sc[...] * LN2).astype(dk_ref.dtype)
            dv_ref[...] = dv_sc[...].astype(dv_ref.dtype)

    blk = lambda which: pl.BlockSpec((tq, LANES), which)
    qmap = lambda h, p, jj, ii: (ii[p], h)
    kmap = lambda h, p, jj, ii: (jj[p], h)
    lse_rows = pl.BlockSpec((8, tq), lambda h, p, jj, ii: (h, ii[p]))
    delta_rows = pl.BlockSpec((8, tq), lambda h, p, jj, ii: (h * (LANES // 8), ii[p]))
    return pl.pallas_call(
        body, name="mla_flash_dkv",
        grid_spec=pltpu.PrefetchScalarGridSpec(
            num_scalar_prefetch=2, grid=(MLA_HEADS, int(ii.shape[0])),
            in_specs=[blk(qmap), blk(kmap), blk(kmap), blk(qmap), lse_rows, delta_rows],
            out_specs=[blk(kmap), blk(kmap)],
            scratch_shapes=[pltpu.VMEM((tq, LANES), F32)] * 2),
        out_shape=[jax.ShapeDtypeStruct((T, HP), BF16)] * 2,
        compiler_params=_cparams(("parallel", "arbitrary")),
    )(jj, ii, q, k, v, do, lse_t, delta_t)


def _mem_fwd(z, km, vm, gq):
    scale = MEM_HEAD_DIM ** -0.5

    def fn(qm, km, vm, gq):
        ys = []
        for h in range(MEM_HEADS):
            sl = slice(h * LANES, (h + 1) * LANES)
            q = _rms(qm[:, sl], gq) * scale
            s = _dot(q, km[:, sl], "nt")
            p = jnp.exp(s - jnp.max(s, axis=1, keepdims=True))
            p = p / jnp.sum(p, axis=1, keepdims=True)
            ys.append(_dot(p, vm[:, sl]))
        y = jnp.concatenate(ys, axis=1)
        return y, y

    return _rowwise(fn, "mem_fwd", 512, [(z, MEM_WIDTH, Z_QM // MEM_WIDTH)], [km, vm, gq],
                    [(MEM_WIDTH, BF16), (MEM_WIDTH, BF16, "T")])


def _mem_bwd(z, dy, km, vm, gq, dz):
    scale = MEM_HEAD_DIM ** -0.5

    def fn(qm, dy, km, vm, gq):
        dqs, dks, dvs = [], [], []
        dgq = jnp.zeros((1, LANES), F32)
        for h in range(MEM_HEADS):
            sl = slice(h * LANES, (h + 1) * LANES)
            q = (_rms(qm[:, sl], gq) * scale).astype(BF16)
            dyh = dy[:, sl]
            kh, vh = km[:, sl], vm[:, sl]
            s = _dot(q, kh, "nt")
            p = jnp.exp(s - jnp.max(s, axis=1, keepdims=True))
            p = p / jnp.sum(p, axis=1, keepdims=True)
            dp = _dot(dyh, vh, "nt")
            ds = p * (dp - jnp.sum(p * dp, axis=1, keepdims=True))
            dq = _dot(ds, kh) * scale
            dx, dg = _rms_bwd(qm[:, sl], gq, dq)
            dqs.append(dx)
            dgq = dgq + dg
            st = _dot(kh, q, "nt")
            pt = jnp.exp(st - jnp.max(st, axis=0, keepdims=True))
            pt = pt / jnp.sum(pt, axis=0, keepdims=True)
            dpt = _dot(vh, dyh, "nt")
            dst = pt * (dpt - jnp.sum(pt * dpt, axis=0, keepdims=True))
            dvs.append(_dot(pt, dyh))
            dks.append(_dot(dst, q))
        return jnp.concatenate(dqs, axis=1), jnp.concatenate(dks, axis=1), jnp.concatenate(dvs, axis=1), dgq

    m = km.shape[0]
    return _rowwise(fn, "mem_bwd", 512, [(z, MEM_WIDTH, Z_QM // MEM_WIDTH), dy], [km, vm, gq],
                    [("into", dz, MEM_WIDTH, Z_QM // MEM_WIDTH)],
                    [((m, MEM_WIDTH), F32), ((m, MEM_WIDTH), F32), ((1, LANES), F32)])


GROUPS = {"ffn1": ["ffn1_w_gu"], "ffn1_down": ["ffn1_w_down"],
          "mix": ["w_in", "mla_w_uq", "mla_w_ukv", "mem_w_kv", "w_branch_a", "w_branch_b", "w_branch_c", "w_out"],
          "ffn2": ["ffn2_w_gu", "ffn2_w_down"]}
GRAD_GROUPS = {"ffn2": GROUPS["ffn2"], "mix": GROUPS["mix"], "ffn1_down": ["ffn1_w_down"], "ffn1_gu": ["ffn1_w_gu"]}


def _local_step(x, mem, positions, loss_target, P, weights, grads_out):
    T = x.shape[0]
    G = {}
    W = dict(weights("ffn1", None))

    half = MLA_ROPE // 2
    inv = ROPE_BASE ** (-jnp.arange(half, dtype=F32) / half)
    ang = positions.astype(F32)[:, None] * inv
    cos, sin = jnp.cos(ang), jnp.sin(ang)
    one, zero = jnp.ones((T, MLA_NOPE), F32), jnp.zeros((T, half), F32)
    pad = LANES - MLA_QK
    tabs = (jnp.concatenate([one, cos, cos, jnp.ones((T, pad), F32)], axis=1),
            jnp.concatenate([jnp.zeros((T, MLA_NOPE), F32), -sin, zero, jnp.zeros((T, pad), F32)], axis=1),
            jnp.concatenate([jnp.zeros((T, MLA_NOPE), F32), zero, sin, jnp.zeros((T, pad), F32)], axis=1))
    gq_p = jnp.pad(P["mla_q_norm"], ((0, 0), (0, pad)))
    gk_p = jnp.pad(P["mla_k_norm"], ((0, 0), (0, pad)))
    bias_full = jnp.repeat(P["sg_b"].T, SG_GROUP_DIM, axis=1)
    group_ind = jnp.repeat(jnp.eye(SG_GROUPS, dtype=F32), SG_GROUP_DIM, axis=0)

    HT = (D_MODEL, BF16, "T")

    def norm2(x, g):
        h = _rms(x, g)
        return h, h

    h1, h1t = _rowwise(norm2, "ffn1_norm", 512, [x], [P["ffn1_norm"]], [(D_MODEL, BF16), HT])
    def ffn1_w_down(after):
        W.update(weights("ffn1_down", after))
        return W["ffn1_w_down"]

    gu1, a1t, o1 = _ffn_fwd(h1, W["ffn1_w_gu"], ffn1_w_down, "ffn1")

    def resid_norm(x, o, g):
        xn = x + 0.5 * o
        h = _rms(xn, g)
        return xn, h, h

    x1, hm, hmt = _rowwise(resid_norm, "mix_norm", 512, [x, o1], [P["mix_norm"]],
                           [(D_MODEL, F32), (D_MODEL, BF16), HT])
    W.update(weights("mix", hm))
    z = _mm(hm, W["w_in"], "nn", BF16, "w_in", tm=1024, tn=1792)

    y_a, y_at = _sg_fwd(z, P["sg_ln_g"], P["sg_ln_b"], P["sg_w"], bias_full)

    def c_norm(cq, ckv, gq, gkv):
        a, b = _rms(cq, gq), _rms(ckv, gkv)
        return a, b, a, b

    cqn, ckvn, cqnt, ckvnt = _rowwise(
        c_norm, "mla_cnorm", 512, [(z, MLA_Q_RANK, Z_CQ // MLA_Q_RANK), (z, MLA_KV_RANK, Z_CKV // MLA_KV_RANK)],
        [P["mla_cq_norm"], P["mla_ckv_norm"]],
        [(MLA_Q_RANK, BF16), (MLA_KV_RANK, BF16), (MLA_Q_RANK, BF16, "T"), (MLA_KV_RANK, BF16, "T")])
    q_pre = _mm(cqn, W["mla_w_uq"], "nn", BF16, "mla_uq", tm=1024, tn=1024)
    kv_pre = _mm(ckvn, W["mla_w_ukv"], "nn", BF16, "mla_ukv", tm=1024, tn=1024)
    q, k, v = _mla_post(q_pre, kv_pre, z, tabs, gq_p, gk_p)
    y_b, y_bt, lse, lse_t = _flash_fwd(q, k, v)

    memn, = _rowwise(lambda m, g: _rms(m, g), "mem_norm", 256, [mem], [P["mem_norm"]], [(D_MODEL, BF16)])
    kvm = _mm(memn, W["mem_w_kv"], "nn", F32, "mem_kv")

    def mem_k(kvm, gk):
        ks = [_rms(kvm[:, h * LANES:(h + 1) * LANES], gk) for h in range(MEM_HEADS)]
        return jnp.concatenate(ks, axis=1), kvm[:, MEM_WIDTH:]

    km, vm = _rowwise(mem_k, "mem_knorm", 256, [kvm], [P["mem_k_norm"]], [(MEM_WIDTH, BF16), (MEM_WIDTH, BF16)])
    y_c, y_ct = _mem_fwd(z, km, vm, P["mem_q_norm"])

    pa = _mm(y_a, W["w_branch_a"], "nn", BF16, "branch_a", tm=1024, tn=1024)
    pb = _mm(y_b, W["w_branch_b"], "nn", BF16, "branch_b", tm=1024, tn=1024)
    pc = _mm(y_c, W["w_branch_c"], "nn", BF16, "branch_c", tm=1024, tn=1024)

    def merge(zg, pa, pb, pc, b):
        g = _sigmoid(zg + b)
        m = g[:, :D_MODEL] * pa + g[:, D_MODEL:2 * D_MODEL] * pb + g[:, 2 * D_MODEL:] * pc
        return m, m

    merged, mergedt = _rowwise(merge, "merge", 256, [(z, 3 * D_MODEL, 0), pa, pb, pc], [P["b_gate"]],
                               [(D_MODEL, BF16), HT])
    om = _mm(merged, W["w_out"], "nn", BF16, "w_out", tm=1024, tn=1024)

    def resid_norm1(x, o, g):
        xn = x + o
        h = _rms(xn, g)
        return xn, h, h

    x2, h2, h2t = _rowwise(resid_norm1, "ffn2_norm", 512, [x1, om], [P["ffn2_norm"]],
                           [(D_MODEL, F32), (D_MODEL, BF16), HT])
    W.update(weights("ffn2", h2))
    gu2, a2t, o2 = _ffn_fwd(h2, W["ffn2_w_gu"], W["ffn2_w_down"], "ffn2")

    def loss_fn(x2, o2, t):
        e = x2 + 0.5 * o2 - t
        return e * (1.0 / D_MODEL), (e * (0.5 / D_MODEL)).astype(BF16), _rsum(e * e) * (0.5 / D_MODEL)

    dx3, do2, loss_part = _rowwise(loss_fn, "loss", 512, [x2, o2, loss_target], [],
                                   [(D_MODEL, F32), (D_MODEL, BF16)], [((1, D_MODEL), F32)])

    dh2, G["ffn2_w_gu"], G["ffn2_w_down"] = _ffn_bwd(do2, h2t, gu2, a2t, W["ffn2_w_gu"], W["ffn2_w_down"], "ffn2")
    tie = grads_out("ffn2", G)

    def norm_bwd(x, dh, dxo, g, *_):
        dx, dg = _rms_bwd(x, g, dh)
        dx = dx + dxo
        return dx, dx, dg

    dx2, dx2b, G["ffn2_norm"] = _rowwise(norm_bwd, "ffn2_norm_bwd", 512, [x2, dh2, dx3],
                                         [P["ffn2_norm"]] + ([] if tie is None else [tie]),
                                         [(D_MODEL, F32), (D_MODEL, BF16)], [((1, D_MODEL), F32)])

    G["w_out"] = _mm_t(mergedt, dx2b, "w_out_dw", tm=1024, tn=1024)
    dmerged = _mm(dx2b, W["w_out"], "nt", BF16, "w_out_dx", tm=1024, tn=1024)

    def merge_bwd(zg, pa, pb, pc, dm, b):
        g = _sigmoid(zg + b)
        ps = jnp.concatenate([pa, pb, pc], axis=1)
        dm3 = jnp.concatenate([dm, dm, dm], axis=1)
        dzg = dm3 * ps * g * (1.0 - g)
        dp = dm3 * g
        return dzg, dp[:, :D_MODEL], dp[:, D_MODEL:2 * D_MODEL], dp[:, 2 * D_MODEL:], _rsum(dzg)

    dz = lax.empty((T, Z_COLS), BF16)
    dz, dpa, dpb, dpc, G["b_gate"] = _rowwise(
        merge_bwd, "merge_bwd", 256, [(z, 3 * D_MODEL, 0), pa, pb, pc, dmerged], [P["b_gate"]],
        [("into", dz, 3 * D_MODEL, 0), (D_MODEL, BF16), (D_MODEL, BF16), (D_MODEL, BF16)], [((1, 3 * D_MODEL), F32)])

    G["w_branch_a"] = _mm_t(y_at, dpa, "branch_a_dw", tm=512, tn=1024)
    G["w_branch_b"] = _mm_t(y_bt, dpb, "branch_b_dw", tm=1024, tn=1024)
    G["w_branch_c"] = _mm_t(y_ct, dpc, "branch_c_dw", tm=512, tn=1024)
    dy_a = _mm(dpa, W["w_branch_a"], "nt", BF16, "branch_a_dx", tm=1024, tn=512)
    dy_b = _mm(dpb, W["w_branch_b"], "nt", BF16, "branch_b_dx", tm=1024, tn=1024)
    dy_c = _mm(dpc, W["w_branch_c"], "nt", BF16, "branch_c_dx", tm=1024, tn=512)

    dz, G["sg_w"], dbias_t, G["sg_ln_g"], G["sg_ln_b"] = _sg_bwd(
        z, dy_a, P["sg_ln_g"], P["sg_ln_b"], P["sg_w"], bias_full, group_ind, dz)
    G["sg_b"] = dbias_t.T

    dz, dkm, dvm, G["mem_q_norm"] = _mem_bwd(z, dy_c, km, vm, P["mem_q_norm"], dz)

    def mem_k_bwd(kvm, dkm, dvm, gk):
        dks = []
        dg = jnp.zeros((1, LANES), F32)
        for h in range(MEM_HEADS):
            sl = slice(h * LANES, (h + 1) * LANES)
            dx, d = _rms_bwd(kvm[:, sl], gk, dkm[:, sl])
            dks.append(dx)
            dg = dg + d
        return jnp.concatenate(dks + [dvm], axis=1), dg

    dkvm, G["mem_k_norm"] = _rowwise(mem_k_bwd, "mem_knorm_bwd", 256, [kvm, dkm, dvm], [P["mem_k_norm"]],
                                     [(2 * MEM_WIDTH, BF16)], [((1, LANES), F32)])
    G["mem_w_kv"] = _mm(memn, dkvm, "tn", BF16, "mem_kv_dw")
    dmemn = _mm(dkvm, W["mem_w_kv"], "nt", F32, "mem_kv_dx")
    _, G["mem_norm"] = _rowwise(lambda m, d, g: _rms_bwd(m, g, d), "mem_norm_bwd", 256, [mem, dmemn],
                                [P["mem_norm"]], [(D_MODEL, BF16)], [((1, D_MODEL), F32)])

    def delta_fn(o, do):
        od = o.astype(F32) * do.astype(F32)
        ds = [jnp.broadcast_to(jnp.sum(od[:, h * LANES:(h + 1) * LANES], axis=1, keepdims=True), (od.shape[0], LANES))
              for h in range(MLA_HEADS)]
        d = jnp.concatenate(ds, axis=1)
        return d, d

    delta, delta_t = _rowwise(delta_fn, "mla_delta", 512, [y_b, dy_b], [], [(HP, F32), (HP, F32, "T")])
    dq = _flash_dq(q, k, v, dy_b, lse, delta)
    dk, dv = _flash_dkv(q, k, v, dy_b, lse_t, delta_t)
    dq_pre, dkv_pre, dkr, dgq, dgk = _mla_post_bwd(q_pre, kv_pre, z, tabs, gq_p, gk_p, dq, dk, dv)
    G["mla_q_norm"], G["mla_k_norm"] = dgq[:, :MLA_QK], dgk[:, :MLA_QK]
    G["mla_w_uq"] = _mm_t(cqnt, dq_pre, "mla_uq_dw", tm=384, tn=1024)
    G["mla_w_ukv"] = _mm_t(ckvnt, dkv_pre, "mla_ukv_dw", tm=256, tn=2048)
    dcqn = _mm(dq_pre, W["mla_w_uq"], "nt", BF16, "mla_uq_dx", tm=1024)
    dckvn = _mm(dkv_pre, W["mla_w_ukv"], "nt", BF16, "mla_ukv_dx", tm=1024)

    def c_norm_bwd(cq, ckv, dcqn, dckvn, dkr, gq, gkv):
        dcq, dgq = _rms_bwd(cq, gq, dcqn)
        dckv, dgkv = _rms_bwd(ckv, gkv, dckvn)
        return jnp.concatenate([dckv, dkr, dcq], axis=1), dgq, dgkv

    tail = Z_COLS - Z_CKV
    dz, G["mla_cq_norm"], G["mla_ckv_norm"] = _rowwise(
        c_norm_bwd, "mla_cnorm_bwd", 512,
        [(z, MLA_Q_RANK, Z_CQ // MLA_Q_RANK), (z, MLA_KV_RANK, Z_CKV // MLA_KV_RANK), dcqn, dckvn, dkr],
        [P["mla_cq_norm"], P["mla_ckv_norm"]], [("into", dz, tail, Z_CKV // tail)],
        [((1, MLA_Q_RANK), F32), ((1, MLA_KV_RANK), F32)])
    G["w_in"] = _mm_t(hmt, dz, "w_in_dw", tm=1024, tn=1792, tk=2048)
    dhm = _mm(dz, W["w_in"], "nt", BF16, "w_in_dx", tm=1024, tn=1024, tk=2688)

    def norm_bwd_half(x, dh, dxo, g):
        dx, dg = _rms_bwd(x, g, dh)
        dx = dx + dxo
        return dx, (0.5 * dx), dg

    dx1, do1, G["mix_norm"] = _rowwise(norm_bwd_half, "mix_norm_bwd", 512, [x1, dhm, dx2], [P["mix_norm"]],
                                       [(D_MODEL, F32), (D_MODEL, BF16)], [((1, D_MODEL), F32)])
    tie = grads_out("mix", G)

    def ffn1_dw(which, dw):
        G["ffn1_w_" + which] = dw
        return grads_out("ffn1_" + which, G)

    dh1, _, _ = _ffn_bwd(do1, h1t, gu1, a1t, W["ffn1_w_gu"], W["ffn1_w_down"], "ffn1", tie, ffn1_dw)

    def norm_bwd_last(x, dh, dxo, g):
        dx, dg = _rms_bwd(x, g, dh)
        return dx + dxo, dg

    grad_x, G["ffn1_norm"] = _rowwise(norm_bwd_last, "ffn1_norm_bwd", 512, [x, dh1, dx1], [P["ffn1_norm"]],
                                      [(D_MODEL, F32)], [((1, D_MODEL), F32)])
    return loss_part, grad_x, G


SHARDED = ["ffn1_w_gu", "ffn1_w_down", "w_in", "mla_w_uq", "mla_w_ukv", "mem_w_kv",
           "w_branch_a", "w_branch_b", "w_branch_c", "w_out", "ffn2_w_gu", "ffn2_w_down"]
ROW_SHARDED = {"ffn1_w_down", "mem_w_kv", "w_out", "ffn2_w_down"}
SMALL = ["ffn1_norm", "mix_norm", "b_gate", "sg_ln_g", "sg_ln_b", "sg_w", "sg_b", "mla_cq_norm",
         "mla_ckv_norm", "mla_q_norm", "mla_k_norm", "mem_norm", "mem_q_norm", "mem_k_norm", "ffn2_norm"]
ORDER = ["ffn1_norm", "ffn1_w_gu", "ffn1_w_down", "mix_norm", "w_in", "b_gate", "sg_ln_g", "sg_ln_b", "sg_w",
         "sg_b", "mla_cq_norm", "mla_w_uq", "mla_ckv_norm", "mla_w_ukv", "mla_q_norm", "mla_k_norm", "mem_norm",
         "mem_w_kv", "mem_q_norm", "mem_k_norm", "w_branch_a", "w_branch_b", "w_branch_c", "w_out", "ffn2_norm",
         "ffn2_w_gu", "ffn2_w_down"]

_IN_U, _IN_V, _IN_CQ, _IN_CKV, _IN_KR, _IN_QM, _IN_G = 0, 512, 1024, 1408, 1664, 1696, 2208
IN_COLS = 5280


def _full_from_slabs(name, slabs):
    n, r, c = slabs.shape
    if name in ROW_SHARDED:
        return slabs.reshape(n * r, c)
    return slabs.transpose(1, 0, 2).reshape(r, n * c)


def _slabs_from_full(name, full):
    if name in ROW_SHARDED:
        return full.reshape(N_DEV, full.shape[0] // N_DEV, full.shape[1])
    r, c = full.shape
    return full.reshape(r, N_DEV, c // N_DEV).transpose(1, 0, 2)


def _compute_layout(full):
    W = dict(full)
    if "w_in" not in full:
        return W
    w = full["w_in"]
    kr = jnp.pad(w[:, _IN_KR:_IN_QM], ((0, 0), (KR_LANE, LANES - KR_LANE - MLA_ROPE)))
    W["w_in"] = jnp.concatenate([w[:, _IN_G:], w[:, _IN_U:_IN_CQ], w[:, _IN_QM:_IN_G], w[:, _IN_CKV:_IN_KR], kr,
                                 w[:, _IN_CQ:_IN_CKV]], axis=1)
    uq = full["mla_w_uq"].reshape(MLA_Q_RANK, MLA_HEADS, MLA_QK)
    W["mla_w_uq"] = jnp.pad(uq, ((0, 0), (0, 0), (0, LANES - MLA_QK))).reshape(MLA_Q_RANK, HP)
    ukv = full["mla_w_ukv"].reshape(MLA_KV_RANK, MLA_HEADS, MLA_NOPE + MLA_V)
    padh = lambda a: jnp.pad(a, ((0, 0), (0, 0), (0, LANES - a.shape[2]))).reshape(MLA_KV_RANK, HP)
    W["mla_w_ukv"] = jnp.concatenate([padh(ukv[:, :, :MLA_NOPE]), padh(ukv[:, :, MLA_NOPE:])], axis=1)
    wb = full["w_branch_b"].reshape(MLA_HEADS, MLA_V, D_MODEL)
    W["w_branch_b"] = jnp.pad(wb, ((0, 0), (0, LANES - MLA_V), (0, 0))).reshape(HP, D_MODEL)
    return W


def _reference_layout(G):
    out = dict(G)
    if "w_in" not in G:
        return out
    g = G["w_in"]
    out["w_in"] = jnp.concatenate([
        g[:, Z_U:Z_QM], g[:, Z_CQ:Z_COLS], g[:, Z_CKV:Z_KR], g[:, Z_KR + KR_LANE:Z_KR + KR_LANE + MLA_ROPE],
        g[:, Z_QM:Z_CKV], g[:, Z_G:Z_U]], axis=1)
    out["mla_w_uq"] = G["mla_w_uq"].reshape(MLA_Q_RANK, MLA_HEADS, LANES)[:, :, :MLA_QK].reshape(MLA_Q_RANK, -1)
    gk = G["mla_w_ukv"][:, :HP].reshape(MLA_KV_RANK, MLA_HEADS, LANES)[:, :, :MLA_NOPE]
    gv = G["mla_w_ukv"][:, HP:].reshape(MLA_KV_RANK, MLA_HEADS, LANES)[:, :, :MLA_V]
    out["mla_w_ukv"] = jnp.concatenate([gk, gv], axis=2).reshape(MLA_KV_RANK, -1)
    out["w_branch_b"] = G["w_branch_b"].reshape(MLA_HEADS, LANES, D_MODEL)[:, :MLA_V].reshape(-1, D_MODEL)
    return out


def _pack(parts):
    flat = []
    for a in parts:
        a = a.reshape(-1)
        flat.append(jnp.pad(a, (0, (-a.shape[0]) % LANES)))
    return jnp.concatenate(flat).reshape(-1, LANES)


def _unpack(packed, shapes):
    flat = packed.reshape(-1)
    out, off = [], 0
    for shp in shapes:
        n = int(np.prod(shp))
        out.append(flat[off:off + n].reshape(shp))
        off += n + (-n) % LANES
    return out


MESH = pl.DeviceIdType.MESH
HBM = pl.BlockSpec(memory_space=pltpu.HBM)


def _all_gather(shards):
    n = len(shards)

    def body(*refs):
        x_refs, out_refs, token_ref = refs[:n], refs[n:2 * n], refs[2 * n]
        send_sems, recv_sems, local_sems = refs[2 * n + 1:]
        x, y, c = lax.axis_index("x"), lax.axis_index("y"), lax.axis_index("c")
        me, sibling = (x, y, c), (x, y, 1 - c)
        chips = [(1 - x, y), (x, 1 - y), (1 - x, 1 - y)]
        token_ref[...] = jnp.zeros_like(token_ref)

        def slot(a, px, py, pc):
            return out_refs[a].at[4 * px + 2 * py + pc]

        def copy(a, k, block, to, src=None):
            return pltpu.make_async_remote_copy(
                src_ref=slot(a, *block) if src is None else src, dst_ref=slot(a, *block),
                send_sem=send_sems.at[7 * a + k], recv_sem=recv_sems.at[7 * a + k], device_id=to, device_id_type=MESH)

        arrays = range(n)
        mine = [pltpu.make_async_copy(x_refs[a], slot(a, *me), local_sems.at[a]) for a in arrays]
        for cp in mine:
            cp.start()
        first = [copy(a, 0, me, sibling, src=x_refs[a]) for a in arrays]
        first += [copy(a, 1 + j, me, (*chip, c), src=x_refs[a]) for j, chip in enumerate(chips) for a in arrays]
        for cp in first:
            cp.start()
        passed = []
        for j, chip in enumerate(chips):
            for a in arrays:
                copy(a, 1 + j, (*chip, c), me).wait_recv()
                passed.append(copy(a, 4 + j, (*chip, c), sibling))
                passed[-1].start()
        for a in arrays:
            copy(a, 0, sibling, me).wait_recv()
        for j, chip in enumerate(chips):
            for a in arrays:
                copy(a, 4 + j, (*chip, 1 - c), me).wait_recv()
        for cp in first + passed:
            cp.wait_send()
        for cp in mine:
            cp.wait()

    res = pl.pallas_call(
        body, name="all_gather_weights",
        out_shape=[jax.ShapeDtypeStruct((N_DEV,) + s.shape, s.dtype) for s in shards]
        + [jax.ShapeDtypeStruct((8, LANES), F32)],
        in_specs=[HBM] * n, out_specs=[HBM] * n + [pl.BlockSpec(memory_space=pltpu.VMEM)],
        scratch_shapes=[pltpu.SemaphoreType.DMA((7 * n,)), pltpu.SemaphoreType.DMA((7 * n,)),
                        pltpu.SemaphoreType.DMA((n,))],
    )(*shards)
    return res[:n], res[n]


SEM = pl.BlockSpec(memory_space=pltpu.SEMAPHORE)
DATAFLOW = pltpu.SideEffectType.DATAFLOW_SIDE_EFFECTING


def _peers():
    x, y, c = lax.axis_index("x"), lax.axis_index("y"), lax.axis_index("c")
    out = []
    for k in range(1, N_DEV):
        px = 1 - x if k & 4 else x
        py = 1 - y if k & 2 else y
        pc = 1 - c if k & 1 else c
        out.append((k, (px, py, pc), 4 * px + 2 * py + pc))
    return 4 * x + 2 * y + c, out


def _send_start(srcs, per_peer, name):
    n = len(srcs)
    lands = [lax.empty((N_DEV,) + (s.shape[1:] if per_peer else s.shape), s.dtype) for s in srcs]

    def body(*refs):
        src_refs, land_refs, send_sems, recv_sems, token = refs[:n], refs[n:2 * n], refs[2 * n], refs[2 * n + 1], refs[-1]
        me, peers = _peers()
        for a in range(n):
            for k, pid, pflat in peers:
                pltpu.make_async_remote_copy(
                    src_ref=src_refs[a].at[pflat] if per_peer else src_refs[a], dst_ref=land_refs[a].at[me],
                    send_sem=send_sems.at[7 * a + k - 1], recv_sem=recv_sems.at[7 * a + k - 1],
                    device_id=pid, device_id_type=MESH).start()
        token[...] = jnp.zeros_like(token)

    hbm = lambda a: pltpu.with_memory_space_constraint(a, pltpu.HBM)
    res = pl.pallas_call(
        body, name=name,
        out_shape=(pltpu.SemaphoreType.DMA((7 * n,)), pltpu.SemaphoreType.DMA((7 * n,)),
                   *[pltpu.HBM(a.shape, a.dtype) for a in srcs + lands], jax.ShapeDtypeStruct((8, LANES), F32)),
        in_specs=(HBM,) * (2 * n), out_specs=(SEM, SEM) + (HBM,) * (2 * n) + (pl.BlockSpec(memory_space=pltpu.VMEM),),
        input_output_aliases={i: 2 + i for i in range(2 * n)},
        compiler_params=pltpu.CompilerParams(has_side_effects=DATAFLOW),
    )(*[hbm(a) for a in srcs + lands])
    return (res[0], res[1], list(res[2:2 + n]), list(res[2 + n:2 + 2 * n])), res[-1]


def _send_wait(started, after, per_peer, name):
    send_sems, recv_sems, srcs_thru, lands_thru = started
    n = len(srcs_thru)

    def body(*refs):
        src_refs, land_refs, send_sems, recv_sems = refs[:n], refs[n:2 * n], refs[2 * n], refs[2 * n + 1]
        me, peers = _peers()
        for a in range(n):
            for k, pid, pflat in peers:
                copy = pltpu.make_async_remote_copy(
                    src_ref=src_refs[a].at[pflat] if per_peer else src_refs[a], dst_ref=land_refs[a].at[pflat],
                    send_sem=send_sems.at[7 * a + k - 1], recv_sem=recv_sems.at[7 * a + k - 1],
                    device_id=pid, device_id_type=MESH)
                copy.wait_send()
                copy.wait_recv()

    outs = pl.pallas_call(
        body, name=name,
        out_shape=tuple(pltpu.HBM(a.shape, a.dtype) for a in srcs_thru + lands_thru),
        in_specs=(HBM,) * (2 * n) + (SEM, SEM, pl.BlockSpec(memory_space=pl.ANY)), out_specs=(HBM,) * (2 * n),
        input_output_aliases={i: i for i in range(2 * n)},
        compiler_params=pltpu.CompilerParams(has_side_effects=DATAFLOW),
    )(*srcs_thru, *lands_thru, send_sems, recv_sems, after)
    me = 4 * lax.axis_index("x") + 2 * lax.axis_index("y") + lax.axis_index("c")
    landed = []
    for src_out, land in zip(outs[:n], outs[n:]):
        own = lax.dynamic_index_in_dim(src_out, me, 0, keepdims=True) if per_peer else src_out[None]
        landed.append(lax.dynamic_update_slice(land, own, (me,) + (0,) * (land.ndim - 1)))
    return landed


def _share_rows(block, name):
    def body(src_ref, out_ref, send_sems, recv_sems, local_sem):
        me, peers = _peers()
        own = pltpu.make_async_copy(src_ref, out_ref.at[me], local_sem)
        own.start()
        copies = [pltpu.make_async_remote_copy(
            src_ref=src_ref, dst_ref=out_ref.at[me], send_sem=send_sems.at[k - 1], recv_sem=recv_sems.at[k - 1],
            device_id=pid, device_id_type=MESH) for k, pid, _ in peers]
        for cp in copies:
            cp.start()
        for cp in copies:
            cp.wait()
        own.wait()

    return pl.pallas_call(
        body, name=name, out_shape=jax.ShapeDtypeStruct((N_DEV,) + block.shape, block.dtype),
        in_specs=[HBM], out_specs=HBM,
        scratch_shapes=[pltpu.SemaphoreType.DMA((N_DEV - 1,)), pltpu.SemaphoreType.DMA((N_DEV - 1,)),
                        pltpu.SemaphoreType.DMA],
    )(block)


def _sum_slots(recv, name, tr):
    n, rows, lanes = recv.shape
    tr = _tile(rows, tr)

    def body(r_ref, o_ref):
        acc = r_ref[0].astype(F32)
        for i in range(1, n):
            acc = acc + r_ref[i].astype(F32)
        o_ref[...] = acc

    return pl.pallas_call(
        body, name=name, grid=(rows // tr,),
        in_specs=[pl.BlockSpec((n, tr, lanes), lambda i: (0, i, 0))],
        out_specs=pl.BlockSpec((tr, lanes), lambda i: (i, 0)),
        out_shape=jax.ShapeDtypeStruct((rows, lanes), F32),
        compiler_params=_cparams(("parallel",)),
    )(recv)


def _adamw_math(w, g, m, v):
    m = ADAM_B1 * m + (1.0 - ADAM_B1) * g
    v = ADAM_B2 * v + (1.0 - ADAM_B2) * (g * g)
    m_hat = m / (1.0 - ADAM_B1 ** ADAM_STEP)
    v_hat = v / (1.0 - ADAM_B2 ** ADAM_STEP)
    return -ADAM_LR * (m_hat / (jnp.sqrt(v_hat) + ADAM_EPS) + ADAM_WD * w), m, v


def _adamw(w, g, m, v, name, tr=256):
    return _rowwise(_adamw_math, name, tr, [w, g, m, v], [], [(w.shape[1], F32)] * 3)


def _adamw_small(ws, gs, ms, vs):
    n = len(ws)

    def body(*refs):
        ins, outs = refs[:4 * n], refs[4 * n:]
        for i in range(n):
            d, m, v = _adamw_math(ins[i][...], ins[n + i][...], ins[2 * n + i][...], ins[3 * n + i][...])
            outs[i][...], outs[n + i][...], outs[2 * n + i][...] = d, m, v

    vmem = pl.BlockSpec(memory_space=pltpu.VMEM)
    res = pl.pallas_call(
        body, name="adamw_small", in_specs=[vmem] * (4 * n), out_specs=[vmem] * (3 * n),
        out_shape=[jax.ShapeDtypeStruct(w.shape, F32) for w in ws] * 3,
    )(*ws, *gs, *ms, *vs)
    return res[:n], res[n:2 * n], res[2 * n:]


def _sum_adamw(recv, w, m, v, name):
    n, r, c = recv.shape
    tr = _tile(r, 256)

    def body(r_ref, w_ref, m_ref, v_ref, g_ref, d_ref, nm_ref, nv_ref):
        g = r_ref[0].astype(F32)
        for i in range(1, n):
            g = g + r_ref[i].astype(F32)
        g_ref[...] = g
        d_ref[...], nm_ref[...], nv_ref[...] = _adamw_math(w_ref[...], g, m_ref[...], v_ref[...])

    row = pl.BlockSpec((None, tr, c), lambda i: (0, i, 0))
    return pl.pallas_call(
        body, name=name, grid=(r // tr,),
        in_specs=[pl.BlockSpec((n, tr, c), lambda i: (0, i, 0)), row, row, row], out_specs=[row] * 4,
        out_shape=[jax.ShapeDtypeStruct((1, r, c), F32)] * 4, compiler_params=_cparams(("parallel",)),
    )(recv, w, m, v)


def kernel(x, mem, positions, ffn1_norm, ffn1_w_gu, ffn1_w_down, mix_norm, w_in, b_gate, sg_ln_g, sg_ln_b, sg_w, sg_b, mla_cq_norm, mla_w_uq, mla_ckv_norm, mla_w_ukv, mla_q_norm, mla_k_norm, mem_norm, mem_w_kv, mem_q_norm, mem_k_norm, w_branch_a, w_branch_b, w_branch_c, w_out, ffn2_norm, ffn2_w_gu, ffn2_w_down, loss_target, m_ffn1_norm, m_ffn1_w_gu, m_ffn1_w_down, m_mix_norm, m_w_in, m_b_gate, m_sg_ln_g, m_sg_ln_b, m_sg_w, m_sg_b, m_mla_cq_norm, m_mla_w_uq, m_mla_ckv_norm, m_mla_w_ukv, m_mla_q_norm, m_mla_k_norm, m_mem_norm, m_mem_w_kv, m_mem_q_norm, m_mem_k_norm, m_w_branch_a, m_w_branch_b, m_w_branch_c, m_w_out, m_ffn2_norm, m_ffn2_w_gu, m_ffn2_w_down, v_ffn1_norm, v_ffn1_w_gu, v_ffn1_w_down, v_mix_norm, v_w_in, v_b_gate, v_sg_ln_g, v_sg_ln_b, v_sg_w, v_sg_b, v_mla_cq_norm, v_mla_w_uq, v_mla_ckv_norm, v_mla_w_ukv, v_mla_q_norm, v_mla_k_norm, v_mem_norm, v_mem_w_kv, v_mem_q_norm, v_mem_k_norm, v_w_branch_a, v_w_branch_b, v_w_branch_c, v_w_out, v_ffn2_norm, v_ffn2_w_gu, v_ffn2_w_down):
    given = dict(locals())
    wts = {n: given[n] for n in ORDER}
    mom = {n: given["m_" + n] for n in ORDER}
    var = {n: given["v_" + n] for n in ORDER}

    def shards(group, zero):
        out = [wts[n][0].astype(BF16) for n in GROUPS[group]]
        return [out[0] + zero.astype(BF16)] + out[1:]

    def full_weights(group, slabs):
        return _compute_layout({n: _full_from_slabs(n, s) for n, s in zip(GROUPS[group], slabs)})

    def zero_of(a):
        return jnp.minimum(jnp.abs(a.reshape(-1)[0]), 0)

    gathered_ffn1, token = _all_gather([wts[n][0].astype(BF16) for n in GROUPS["ffn1"]])
    flight = {}
    flight["ffn1_down"], token = _send_start(shards("ffn1_down", token[0, 0]), False, "gather_ffn1_down_start")
    flight["mix"] = _send_start(shards("mix", token[0, 0]), False, "gather_mix_start")[0]
    recv = {}

    def weights(group, after):
        if group == "ffn1":
            return full_weights(group, gathered_ffn1)
        landed = _send_wait(flight.pop(group), after, False, f"gather_{group}_wait")
        if group == "mix":
            flight["ffn2"] = _send_start(shards("ffn2", zero_of(landed[0])), False, "gather_ffn2_start")[0]
        return full_weights(group, landed)

    small_shapes = [wts[n].shape[1:] for n in SMALL]
    early = SMALL[1:]
    assert SMALL[0] == "ffn1_norm"

    def grads_out(group, G):
        Gr = _reference_layout({n: G[n] for n in GRAD_GROUPS[group]})
        parts = [_slabs_from_full(n, Gr[n]).astype(BF16) for n in GRAD_GROUPS[group]]
        flight["g_" + group], tie = _send_start(parts, True, f"grads_{group}_start")
        if group == "mix":
            small = _pack([G[n].reshape(s) for n, s in zip(early, small_shapes[1:])])
            small = jnp.pad(small, ((0, (-small.shape[0]) % 8), (0, 0)))
            flight["small"], tie = _send_start([small + tie[0, 0]], False, "grads_small_start")
        return tie

    P = {n: wts[n] if wts[n].ndim == 2 else wts[n][0] for n in SMALL}
    loss_part, grad_x, G = _local_step(x[0], mem[0], positions[0], loss_target[0], P, weights, grads_out)

    for group, names in GRAD_GROUPS.items():
        recv.update(zip(names, _send_wait(flight.pop("g_" + group), grad_x, True, f"grads_{group}_wait")))
    early_recv, = _send_wait(flight.pop("small"), grad_x, False, "grads_small_wait")
    last = _share_rows(G["ffn1_norm"].reshape(-1, LANES), "share_ffn1_norm")
    g_small_packed = _sum_slots(jnp.concatenate([last, early_recv], axis=1), "sum_small", 2048)

    grads, delta, new_m, new_v = {}, {}, {}, {}
    for n in SHARDED:
        grads[n], delta[n], new_m[n], new_v[n] = _sum_adamw(recv[n], wts[n], mom[n], var[n], "adamw_" + n)
    grads.update(zip(SMALL, _unpack(g_small_packed, small_shapes)))

    flat2 = lambda d: [d[n].reshape(-1, d[n].shape[-1]) for n in SMALL]
    for dst, vals in zip((delta, new_m, new_v), _adamw_small(flat2(wts), flat2(grads), flat2(mom), flat2(var))):
        dst.update(zip(SMALL, vals))

    loss = lax.psum(jnp.sum(loss_part), ("x", "y", "c"))
    lead = lambda d: [d[n].reshape(wts[n].shape) for n in ORDER]
    return (loss, grad_x[None], *lead(grads), *lead(delta), *lead(new_m), *lead(new_v))
```

```python
import functools

import numpy as np
import jax
import jax.numpy as jnp
from jax import lax
from jax.experimental import pallas as pl
from jax.experimental.pallas import tpu as pltpu

F32, BF16 = jnp.float32, jnp.bfloat16

D_MODEL = 1024
SG_GROUPS, SG_GROUP_DIM, SG_WIDTH, CHUNK = 8, 64, 512, 128
MLA_HEADS, MLA_NOPE, MLA_ROPE, MLA_V, MLA_QK = 8, 64, 32, 64, 96
MLA_Q_RANK, MLA_KV_RANK = 384, 256
MEM_HEADS, MEM_HEAD_DIM, MEM_WIDTH = 4, 128, 512
D_FF = 2816
ROPE_BASE = 10000.0
EPS = 1e-6
NEG = -1e30
ADAM_LR, ADAM_B1, ADAM_B2, ADAM_EPS, ADAM_WD, ADAM_STEP = 0.001, 0.9, 0.999, 1e-08, 0.01, 10

N_DEV = 8
LANES = 128
V7X_VMEM_LIMIT = 56 * 1024 * 1024
HP = MLA_HEADS * LANES

Z_G, Z_U, Z_V, Z_QM, Z_CKV, Z_KR, Z_CQ = 0, 3072, 3584, 4096, 4608, 4864, 4992
Z_COLS = 5376
KR_LANE = 64


def _tile(dim, pref):
    if dim <= pref:
        return dim
    for t in range(pref - pref % LANES, LANES - 1, -LANES):
        if dim % t == 0:
            return t
    for t in range(pref - pref % 8, 7, -8):
        if dim % t == 0:
            return t
    return dim


def _cparams(sem):
    return pltpu.CompilerParams(dimension_semantics=sem, vmem_limit_bytes=V7X_VMEM_LIMIT)


_DN = {"nn": ((1,), (0,)), "nt": ((1,), (1,)), "tn": ((0,), (0,))}


def _dot(a, b, mode="nn"):
    return lax.dot_general(a.astype(BF16), b.astype(BF16), (_DN[mode], ((), ())),
                           preferred_element_type=F32)


def _mm(a, b, mode, out_dtype, name, tm=512, tn=512, tk=2048, tie=None):
    if mode == "tn":
        K, M = a.shape
    else:
        M, K = a.shape
    N = b.shape[0] if mode == "nt" else b.shape[1]
    tm, tn, tk = _tile(M, tm), _tile(N, tn), _tile(K, tk)
    nk = K // tk
    if mode == "tn":
        a_spec = pl.BlockSpec((tk, tm), lambda i, j, k: (k, i))
    else:
        a_spec = pl.BlockSpec((tm, tk), lambda i, j, k: (i, k))
    if mode == "nt":
        b_spec = pl.BlockSpec((tn, tk), lambda i, j, k: (j, k))
    else:
        b_spec = pl.BlockSpec((tk, tn), lambda i, j, k: (k, j))

    ties = [] if tie is None else [tie]

    def body(a_ref, b_ref, *rest):
        o_ref, *scratch = rest[len(ties):]
        p = _dot(a_ref[...], b_ref[...], mode)
        if nk == 1:
            o_ref[...] = p.astype(o_ref.dtype)
        else:
            acc_ref, = scratch
            k = pl.program_id(2)

            @pl.when(k == 0)
            def _():
                acc_ref[...] = p

            @pl.when(k > 0)
            def _():
                acc_ref[...] += p

            @pl.when(k == nk - 1)
            def _():
                o_ref[...] = acc_ref[...].astype(o_ref.dtype)

    return pl.pallas_call(
        body, name=name, grid=(M // tm, N // tn, nk),
        in_specs=[a_spec, b_spec] + [pl.BlockSpec(t.shape, lambda i, j, k: (0, 0)) for t in ties],
        out_specs=pl.BlockSpec((tm, tn), lambda i, j, k: (i, j)),
        out_shape=jax.ShapeDtypeStruct((M, N), out_dtype),
        scratch_shapes=[] if nk == 1 else [pltpu.VMEM((tm, tn), F32)],
        compiler_params=_cparams(("parallel", "parallel", "arbitrary")),
    )(a, b, *ties)


def _mm_t(at, b, name, tm, tn, tk=1024, tie=None):
    return _mm(at, b, "nn", BF16, name, tm=tm, tn=tn, tk=tk, tie=tie)


def _rowwise(fn, name, tr, row_ins, bc_ins, row_outs, acc_outs=()):
    norm = [it if isinstance(it, tuple) else (it, it.shape[1], 0) for it in row_ins]
    rows = norm[0][0].shape[0]
    tr = _tile(rows, tr)
    arrays, in_specs = [], []
    for arr, w, cb in norm:
        arrays.append(arr)
        in_specs.append(pl.BlockSpec((tr, w), lambda i, cb=cb: (i, cb)))
    for arr in bc_ins:
        arrays.append(arr)
        in_specs.append(pl.BlockSpec(arr.shape, lambda i, nd=arr.ndim: (0,) * nd))
    n_in, n_row = len(arrays), len(row_outs)
    out_shape, out_specs, aliases = [], [], {}
    transposed = [len(o) == 3 for o in row_outs]
    for k, o in enumerate(row_outs):
        if o[0] == "into":
            _, target, w, cb = o
            aliases[len(arrays)] = k
            arrays.append(target)
            in_specs.append(pl.BlockSpec(memory_space=pl.ANY))
            out_shape.append(jax.ShapeDtypeStruct(target.shape, target.dtype))
            out_specs.append(pl.BlockSpec((tr, w), lambda i, cb=cb: (i, cb)))
        elif transposed[k]:
            out_shape.append(jax.ShapeDtypeStruct((o[0], rows), o[1]))
            out_specs.append(pl.BlockSpec((o[0], tr), lambda i: (0, i)))
        else:
            out_shape.append(jax.ShapeDtypeStruct((rows, o[0]), o[1]))
            out_specs.append(pl.BlockSpec((tr, o[0]), lambda i: (i, 0)))
    for shp, dt in acc_outs:
        out_shape.append(jax.ShapeDtypeStruct(shp, dt))
        out_specs.append(pl.BlockSpec(shp, lambda i, nd=len(shp): (0,) * nd))

    def body(*refs):
        vals = fn(*[r[...].astype(F32) for r in refs[:n_in]])
        if not isinstance(vals, (tuple, list)):
            vals = (vals,)
        outs = refs[len(arrays):]
        for r, v, t in zip(outs[:n_row], vals[:n_row], transposed):
            r[...] = v.astype(F32).T.astype(r.dtype) if t else v.astype(r.dtype)
        if acc_outs:
            accs = list(zip(outs[n_row:], vals[n_row:]))
            i = pl.program_id(0)

            @pl.when(i == 0)
            def _():
                for r, v in accs:
                    r[...] = v.astype(r.dtype)

            @pl.when(i > 0)
            def _():
                for r, v in accs:
                    r[...] += v.astype(r.dtype)

    res = pl.pallas_call(
        body, name=name, grid=(rows // tr,), in_specs=in_specs, out_specs=out_specs,
        out_shape=out_shape, input_output_aliases=aliases, compiler_params=_cparams(("arbitrary",)),
    )(*arrays)
    return res


def _rsum(x):
    return jnp.sum(x, axis=0, keepdims=True)


def _rms(x, g, n=None):
    n = x.shape[-1] if n is None else n
    r = lax.rsqrt(jnp.sum(x * x, axis=-1, keepdims=True) * (1.0 / n) + EPS)
    return x * r * g


def _rms_bwd(x, g, dy, n=None):
    n = x.shape[-1] if n is None else n
    r = lax.rsqrt(jnp.sum(x * x, axis=-1, keepdims=True) * (1.0 / n) + EPS)
    xh = x * r
    dxh = dy * g
    dx = r * (dxh - xh * (jnp.sum(dxh * xh, axis=-1, keepdims=True) * (1.0 / n)))
    return dx, _rsum(dy * xh)


def _gelu(x):
    return 0.5 * x * (1.0 + lax.erf(x * 0.7071067811865476))


def _gelu_grad(x):
    return 0.5 * (1.0 + lax.erf(x * 0.7071067811865476)) + x * jnp.exp(-0.5 * x * x) * 0.3989422804014327


def _sigmoid(x):
    return 0.5 * jnp.tanh(0.5 * x) + 0.5


FFN_TM, FFN_TN = 1024, 1408
MXU_WIDTH = 256


def _col_chunks(n):
    return [(c, min(c + MXU_WIDTH, n)) for c in range(0, n, MXU_WIDTH)]


def _ffn_gu_act(h, w_gu, tag):
    T = h.shape[0]
    tm, tn = _tile(T, FFN_TM), FFN_TN
    nj = D_FF // tn

    def body(h_ref, wg_ref, wu_ref, gu_ref, a_ref, at_ref):
        h = h_ref[...]
        for c0, c1 in _col_chunks(tn):
            g = _dot(h, wg_ref[:, c0:c1])
            u = _dot(h, wu_ref[:, c0:c1])
            gu_ref[0, :, c0:c1] = g.astype(BF16)
            gu_ref[1, :, c0:c1] = u.astype(BF16)
            a = g * _sigmoid(g) * u
            a_ref[:, c0:c1] = a.astype(BF16)
            at_ref[c0:c1, :] = a.T.astype(BF16)

    return pl.pallas_call(
        body, name=f"{tag}_gu_act", grid=(T // tm, nj),
        in_specs=[pl.BlockSpec((tm, D_MODEL), lambda i, j: (i, 0)),
                  pl.BlockSpec((D_MODEL, tn), lambda i, j: (0, j)),
                  pl.BlockSpec((D_MODEL, tn), lambda i, j: (0, j + nj))],
        out_specs=[pl.BlockSpec((2, tm, tn), lambda i, j: (0, i, j)),
                   pl.BlockSpec((tm, tn), lambda i, j: (i, j)),
                   pl.BlockSpec((tn, tm), lambda i, j: (j, i))],
        out_shape=[jax.ShapeDtypeStruct((2, T, D_FF), BF16), jax.ShapeDtypeStruct((T, D_FF), BF16),
                   jax.ShapeDtypeStruct((D_FF, T), BF16)],
        compiler_params=_cparams(("parallel", "parallel")),
    )(h, w_gu, w_gu)


def _ffn_da_actbwd(do, w_down, gu, tag, tie=None):
    T = do.shape[0]
    tm, tn = _tile(T, FFN_TM), FFN_TN
    ties = [] if tie is None else [tie]

    def body(do_ref, wd_ref, gu_ref, *rest):
        dgu_ref = rest[-1]
        do = do_ref[...]
        for c0, c1 in _col_chunks(tn):
            da = _dot(do, wd_ref[c0:c1, :], "nt")
            g = gu_ref[0, :, c0:c1].astype(F32)
            u = gu_ref[1, :, c0:c1].astype(F32)
            s = _sigmoid(g)
            dgu_ref[0, :, c0:c1] = (da * u * s * (1.0 + g * (1.0 - s))).astype(BF16)
            dgu_ref[1, :, c0:c1] = (da * g * s).astype(BF16)

    return pl.pallas_call(
        body, name=f"{tag}_da_actbwd", grid=(T // tm, D_FF // tn),
        in_specs=[pl.BlockSpec((tm, D_MODEL), lambda i, j: (i, 0)),
                  pl.BlockSpec((tn, D_MODEL), lambda i, j: (j, 0)),
                  pl.BlockSpec((2, tm, tn), lambda i, j: (0, i, j))]
        + [pl.BlockSpec(t.shape, lambda i, j: (0, 0)) for t in ties],
        out_specs=pl.BlockSpec((2, tm, tn), lambda i, j: (0, i, j)),
        out_shape=jax.ShapeDtypeStruct((2, T, D_FF), BF16),
        compiler_params=_cparams(("parallel", "parallel")),
    )(do, w_down, gu, *ties)


def _ffn_dwgu(ht, dgu, tag, tk=2048):
    T = ht.shape[1]
    tn, tk = FFN_TN, _tile(T, tk)
    nj, nk = D_FF // tn, T // tk

    def body(a_ref, b_ref, o_ref, acc_ref):
        k = pl.program_id(1)
        p = _dot(a_ref[...], b_ref[...])

        @pl.when(k == 0)
        def _():
            acc_ref[...] = p

        @pl.when(k > 0)
        def _():
            acc_ref[...] += p

        @pl.when(k == nk - 1)
        def _():
            o_ref[...] = acc_ref[...].astype(o_ref.dtype)

    return pl.pallas_call(
        body, name=f"{tag}_dwgu", grid=(2 * nj, nk),
        in_specs=[pl.BlockSpec((D_MODEL, tk), lambda n, k: (0, k)),
                  pl.BlockSpec((None, tk, tn), lambda n, k: (n // nj, k, n % nj))],
        out_specs=pl.BlockSpec((D_MODEL, tn), lambda n, k: (0, n)),
        out_shape=jax.ShapeDtypeStruct((D_MODEL, 2 * D_FF), BF16),
        scratch_shapes=[pltpu.VMEM((D_MODEL, tn), F32)],
        compiler_params=_cparams(("parallel", "arbitrary")),
    )(ht, dgu)


def _ffn_dh(dgu, w_gu, tag, tm=2048, tie=None):
    T = dgu.shape[1]
    tm, tk = _tile(T, tm), FFN_TN
    nk = D_FF // tk
    ties = [] if tie is None else [tie]

    def body(a_ref, b_ref, *rest):
        o_ref, acc_ref = rest[len(ties):]
        k = pl.program_id(1)
        p = _dot(a_ref[...], b_ref[...], "nt")

        @pl.when(k == 0)
        def _():
            acc_ref[...] = p

        @pl.when(k > 0)
        def _():
            acc_ref[...] += p

        @pl.when(k == 2 * nk - 1)
        def _():
            o_ref[...] = acc_ref[...].astype(o_ref.dtype)

    return pl.pallas_call(
        body, name=f"{tag}_dh", grid=(T // tm, 2 * nk),
        in_specs=[pl.BlockSpec((None, tm, tk), lambda i, k: (k // nk, i, k % nk)),
                  pl.BlockSpec((D_MODEL, tk), lambda i, k: (0, k))]
        + [pl.BlockSpec(t.shape, lambda i, k: (0, 0)) for t in ties],
        out_specs=pl.BlockSpec((tm, D_MODEL), lambda i, k: (i, 0)),
        out_shape=jax.ShapeDtypeStruct((T, D_MODEL), BF16),
        scratch_shapes=[pltpu.VMEM((tm, D_MODEL), F32)],
        compiler_params=_cparams(("parallel", "arbitrary")),
    )(dgu, w_gu, *ties)


def _ffn_fwd(h, w_gu, w_down, tag):
    gu, a, at = _ffn_gu_act(h, w_gu, tag)
    if callable(w_down):
        w_down = w_down(at)
    o = _mm(a, w_down, "nn", BF16, f"{tag}_down", tm=1024, tn=1024, tk=2816)
    return gu, at, o


def _ffn_bwd(do, ht, gu, at, w_gu, w_down, tag, tie=None, on_dw=None):
    on_dw = on_dw or (lambda which, dw: None)
    dw_down = _mm_t(at, do, f"{tag}_dwdown", tm=1408, tn=1024, tk=2048, tie=tie)
    dgu = _ffn_da_actbwd(do, w_down, gu, tag, tie=on_dw("down", dw_down))
    dw_gu = _ffn_dwgu(ht, dgu, tag)
    dh = _ffn_dh(dgu, w_gu, tag, tie=on_dw("gu", dw_gu))
    return dh, dw_gu, dw_down


def _sg_common(u_pre, v_pre, ln_g, ln_b):
    u = _gelu(u_pre)
    v = _gelu(v_pre)
    mu = jnp.mean(v, axis=-1, keepdims=True)
    vc = v - mu
    rstd = lax.rsqrt(jnp.mean(vc * vc, axis=-1, keepdims=True) + EPS)
    vhat = vc * rstd
    vl = vhat * ln_g + ln_b
    return u, vhat, rstd, vl


def _sg_masked_pairs(w):
    t = lax.broadcasted_iota(jnp.int32, (CHUNK, CHUNK), 0)
    s = lax.broadcasted_iota(jnp.int32, (CHUNK, CHUNK), 1)
    causal = s <= t
    wm = [jnp.where(causal, w[g], 0.0).astype(BF16) for g in range(SG_GROUPS)]
    return [jnp.concatenate([wm[2 * j], wm[2 * j + 1]], axis=0) for j in range(SG_GROUPS // 2)], causal


def _sg_mix(vl, pairs, bias):
    tr = vl.shape[0]
    low = lax.broadcasted_iota(jnp.int32, (CHUNK, LANES), 1) < SG_GROUP_DIM
    vb = vl.astype(BF16)
    rows = []
    for c in range(tr // CHUNK):
        slabs = []
        for j in range(SG_GROUPS // 2):
            slab = vb[c * CHUNK:(c + 1) * CHUNK, j * LANES:(j + 1) * LANES]
            m = _dot(pairs[j], slab)
            slabs.append(jnp.where(low, m[:CHUNK], m[CHUNK:]))
        rows.append(jnp.concatenate(slabs, axis=1) + bias)
    return jnp.concatenate(rows, axis=0)


def _sg_fwd(z, ln_g, ln_b, sg_w, bias_full):
    def fn(u_pre, v_pre, ln_g, ln_b, w, bias):
        u, _, _, vl = _sg_common(u_pre, v_pre, ln_g, ln_b)
        pairs, _ = _sg_masked_pairs(w)
        y = u * _sg_mix(vl, pairs, bias)
        return y, y

    return _rowwise(fn, "sg_fwd", 512, [(z, SG_WIDTH, Z_U // SG_WIDTH), (z, SG_WIDTH, Z_V // SG_WIDTH)],
                    [ln_g, ln_b, sg_w, bias_full], [(SG_WIDTH, BF16), (SG_WIDTH, BF16, "T")])


def _sg_bwd(z, dy, ln_g, ln_b, sg_w, bias_full, group_ind, dz):
    def fn(u_pre, v_pre, dy, ln_g, ln_b, w, bias, ind):
        dy = dy.astype(F32)
        u, vhat, rstd, vl = _sg_common(u_pre, v_pre, ln_g, ln_b)
        pairs, causal = _sg_masked_pairs(w)
        mixed = _sg_mix(vl, pairs, bias)
        du_pre = dy * mixed * _gelu_grad(u_pre)
        dmix = dy * u
        tr = dy.shape[0]
        low = lax.broadcasted_iota(jnp.int32, (CHUNK, LANES), 1) < SG_GROUP_DIM
        vb = vl.astype(BF16)
        dw = [jnp.zeros((CHUNK, CHUNK), F32) for _ in range(SG_GROUPS)]
        dbias = jnp.zeros((CHUNK, SG_WIDTH), F32)
        dvl_rows = []
        for c in range(tr // CHUNK):
            dm_c = dmix[c * CHUNK:(c + 1) * CHUNK]
            dbias = dbias + dm_c
            slabs = []
            for j in range(SG_GROUPS // 2):
                slab = vb[c * CHUNK:(c + 1) * CHUNK, j * LANES:(j + 1) * LANES]
                dm = dm_c[:, j * LANES:(j + 1) * LANES]
                d0 = jnp.where(low, dm, 0.0).astype(BF16)
                d1 = jnp.where(low, 0.0, dm).astype(BF16)
                dw[2 * j] = dw[2 * j] + _dot(d0, slab, "nt")
                dw[2 * j + 1] = dw[2 * j + 1] + _dot(d1, slab, "nt")
                slabs.append(_dot(pairs[j], jnp.concatenate([d0, d1], axis=0), "tn"))
            dvl_rows.append(jnp.concatenate(slabs, axis=1))
        dvl = jnp.concatenate(dvl_rows, axis=0)
        dln_g = _rsum(dvl * vhat)
        dln_b = _rsum(dvl)
        dvh = dvl * ln_g
        dv = rstd * (dvh - jnp.mean(dvh, axis=-1, keepdims=True)
                     - vhat * jnp.mean(dvh * vhat, axis=-1, keepdims=True))
        dv_pre = dv * _gelu_grad(v_pre)
        dw = jnp.stack([jnp.where(causal, d, 0.0) for d in dw], axis=0)
        dbias_t = lax.dot_general(dbias, ind, (((1,), (0,)), ((), ())), precision=lax.Precision.HIGHEST,
                                  preferred_element_type=F32)
        return jnp.concatenate([du_pre, dv_pre], axis=1), dw, dbias_t, dln_g, dln_b

    return _rowwise(fn, "sg_bwd", 512,
                    [(z, SG_WIDTH, Z_U // SG_WIDTH), (z, SG_WIDTH, Z_V // SG_WIDTH), dy],
                    [ln_g, ln_b, sg_w, bias_full, group_ind],
                    [("into", dz, 2 * SG_WIDTH, Z_U // (2 * SG_WIDTH))],
                    [((SG_GROUPS, CHUNK, CHUNK), F32), ((CHUNK, SG_GROUPS), F32), ((1, SG_WIDTH), F32), ((1, SG_WIDTH), F32)])


def _rope(x, c, s1, s2):
    return x * c + pltpu.roll(x, LANES - MLA_ROPE // 2, 1) * s1 + pltpu.roll(x, MLA_ROPE // 2, 1) * s2


def _rope_t(d, c, s1, s2):
    return d * c + pltpu.roll(d * s1, MLA_ROPE // 2, 1) + pltpu.roll(d * s2, LANES - MLA_ROPE // 2, 1)


def _mla_post(q_pre, kv_pre, z, tabs, gq, gk):
    scale = MLA_QK ** -0.5 * LOG2E
    T = q_pre.shape[0]
    tr = _tile(T, 256)

    def body(q_ref, k_ref, v_ref, kr_ref, c_ref, s1_ref, s2_ref, gq_ref, gk_ref, qo_ref, ko_ref, vo_ref):
        kr = kr_ref[...].astype(F32)
        c, s1, s2, gq, gk = c_ref[...], s1_ref[...], s2_ref[...], gq_ref[...], gk_ref[...]
        ones_lane = lax.broadcasted_iota(jnp.int32, (tr, LANES), 1) == ONES_LANE
        for h in range(MLA_HEADS):
            sl = slice(h * LANES, (h + 1) * LANES)
            qo_ref[:, sl] = (_rope(_rms(q_ref[:, sl].astype(F32), gq, MLA_QK), c, s1, s2) * scale).astype(BF16)
            ko_ref[:, sl] = _rope(_rms(k_ref[:, sl].astype(F32) + kr, gk, MLA_QK), c, s1, s2).astype(BF16)
            vo_ref[:, sl] = jnp.where(ones_lane, 1.0, v_ref[:, sl].astype(F32)).astype(BF16)

    wide = lambda cb: pl.BlockSpec((tr, HP), lambda i, cb=cb: (i, cb))
    lanes = lambda cb: pl.BlockSpec((tr, LANES), lambda i, cb=cb: (i, cb))
    gain = pl.BlockSpec((1, LANES), lambda i: (0, 0))
    return pl.pallas_call(
        body, name="mla_post", grid=(T // tr,),
        in_specs=[wide(0), wide(0), wide(1), lanes(Z_KR // LANES), lanes(0), lanes(0), lanes(0), gain, gain],
        out_specs=[wide(0)] * 3, out_shape=[jax.ShapeDtypeStruct((T, HP), BF16)] * 3,
        compiler_params=_cparams(("parallel",)),
    )(q_pre, kv_pre, kv_pre, z, *tabs, gq, gk)


def _mla_post_bwd(q_pre, kv_pre, z, tabs, gq, gk, dq, dk, dv):
    scale = MLA_QK ** -0.5
    T = q_pre.shape[0]
    tr = _tile(T, 256)

    def body(q_ref, k_ref, kr_ref, c_ref, s1_ref, s2_ref, dq_ref, dk_ref, dv_ref, gq_ref, gk_ref,
             dqo_ref, dkvo_ref, dkro_ref, dgq_ref, dgk_ref):
        kr = kr_ref[...].astype(F32)
        c, s1, s2, gq, gk = c_ref[...], s1_ref[...], s2_ref[...], gq_ref[...], gk_ref[...]
        lane = lax.broadcasted_iota(jnp.int32, (1, LANES), 1)
        kr_mask = (lane >= KR_LANE) & (lane < KR_LANE + MLA_ROPE)
        dgq = jnp.zeros((1, LANES), F32)
        dgk = jnp.zeros((1, LANES), F32)
        dkr = jnp.zeros((tr, LANES), F32)
        for h in range(MLA_HEADS):
            sl = slice(h * LANES, (h + 1) * LANES)
            dqn = _rope_t(dq_ref[:, sl].astype(F32), c, s1, s2) * scale
            dx, dg = _rms_bwd(q_ref[:, sl].astype(F32), gq, dqn, MLA_QK)
            dqo_ref[:, sl] = dx.astype(BF16)
            dgq = dgq + dg
            dkn = _rope_t(dk_ref[:, sl].astype(F32), c, s1, s2)
            dx, dg = _rms_bwd(k_ref[:, sl].astype(F32) + kr, gk, dkn, MLA_QK)
            dkvo_ref[:, sl] = dx.astype(BF16)
            dkvo_ref[:, HP + h * LANES:HP + (h + 1) * LANES] = dv_ref[:, sl]
            dgk = dgk + dg
            dkr = dkr + dx
        dkro_ref[...] = jnp.where(kr_mask, dkr, 0.0).astype(BF16)
        i = pl.program_id(0)

        @pl.when(i == 0)
        def _():
            dgq_ref[...] = dgq
            dgk_ref[...] = dgk

        @pl.when(i > 0)
        def _():
            dgq_ref[...] += dgq
            dgk_ref[...] += dgk

    wide = lambda cb: pl.BlockSpec((tr, HP), lambda i, cb=cb: (i, cb))
    lanes = lambda cb: pl.BlockSpec((tr, LANES), lambda i, cb=cb: (i, cb))
    gain = pl.BlockSpec((1, LANES), lambda i: (0, 0))
    return pl.pallas_call(
        body, name="mla_post_bwd", grid=(T // tr,),
        in_specs=[wide(0), wide(0), lanes(Z_KR // LANES), lanes(0), lanes(0), lanes(0), wide(0), wide(0), wide(0),
                  gain, gain],
        out_specs=[wide(0), pl.BlockSpec((tr, 2 * HP), lambda i: (i, 0)), lanes(0), gain, gain],
        out_shape=[jax.ShapeDtypeStruct((T, HP), BF16), jax.ShapeDtypeStruct((T, 2 * HP), BF16),
                   jax.ShapeDtypeStruct((T, LANES), BF16), jax.ShapeDtypeStruct((1, LANES), F32),
                   jax.ShapeDtypeStruct((1, LANES), F32)],
        compiler_params=_cparams(("arbitrary",)),
    )(q_pre, kv_pre, z, *tabs, dq, dk, dv, gq, gk)


def _pairs(n, lower):
    a, b = [], []
    for o in range(n):
        inner = range(o + 1) if lower else range(o, n)
        for t in inner:
            a.append(o)
            b.append(t)
    return jnp.asarray(np.array(a, np.int32)), jnp.asarray(np.array(b, np.int32))


FLASH_TILE, FLASH_SUB_ROWS = 2048, 512
LOG2E, LN2 = 1.4426950408889634, 0.6931471805599453
ONES_LANE = MLA_V


def _flash_tiles(T):
    tq = _tile(T, FLASH_TILE)
    return tq, _tile(tq, FLASH_SUB_ROWS)


def _col_span(t, sr, rb, diag, key_major):
    if not diag:
        return 0, t
    return (rb * sr, t) if key_major else (0, (rb + 1) * sr)


def _span_iota(sr, rb, c0, c1):
    r = lax.broadcasted_iota(jnp.int32, (sr, c1 - c0), 0) + rb * sr
    c = lax.broadcasted_iota(jnp.int32, (sr, c1 - c0), 1) + c0
    return r, c


def _lanes(x, width):
    return jnp.concatenate([x] * (width // LANES), axis=1)


def _flash_fwd(q, k, v):
    T = q.shape[0]
    tq, sr = _flash_tiles(T)
    n = T // tq
    ii, jj = _pairs(n, True)

    def body(ii_ref, jj_ref, q_ref, k_ref, v_ref, o_ref, ot_ref, lse_ref, lset_ref, m_sc, acc_sc):
        p_ = pl.program_id(1)
        i, j = ii_ref[p_], jj_ref[p_]

        @pl.when(j == 0)
        def _():
            m_sc[...] = jnp.full(m_sc.shape, NEG, F32)
            acc_sc[...] = jnp.zeros(acc_sc.shape, F32)

        def tile(diag):
            nrb = tq // sr

            def scores(rb):
                c0, c1 = _col_span(tq, sr, rb, diag, False)
                return _dot(q_ref[rb * sr:(rb + 1) * sr, :], k_ref[c0:c1, :], "nt")

            s_next = scores(0)
            for rb in range(nrb):
                rows = slice(rb * sr, (rb + 1) * sr)
                c0, c1 = _col_span(tq, sr, rb, diag, False)
                s, s_next = s_next, (scores(rb + 1) if rb + 1 < nrb else None)
                if diag:
                    r, c = _span_iota(sr, rb, c0, c1)
                    s = jnp.where(c <= r, s, NEG)
                m = m_sc[rows, :]
                m_new = jnp.maximum(m, jnp.max(s, axis=1, keepdims=True))
                p = jnp.exp2(s - _lanes(m_new, c1 - c0))
                acc_sc[rows, :] = jnp.exp2(m - m_new) * acc_sc[rows, :] + _dot(p, v_ref[c0:c1, :])
                m_sc[rows, :] = m_new

        @pl.when(j < i)
        def _():
            tile(False)

        @pl.when(j == i)
        def _():
            tile(True)
            acc = acc_sc[...]
            lane = lax.broadcasted_iota(jnp.int32, acc.shape, 1)
            l = jnp.sum(jnp.where(lane == ONES_LANE, acc, 0.0), axis=1, keepdims=True)
            o = jnp.where(lane < MLA_V, acc / l, 0.0)
            o_ref[...] = o.astype(o_ref.dtype)
            ot_ref[...] = o.T.astype(ot_ref.dtype)
            lse = m_sc[...] + jnp.log2(l)
            lse_ref[...] = lse
            lset_ref[...] = lse.T[:8]

    blk = lambda which: pl.BlockSpec((tq, LANES), which)
    qmap = lambda h, p, ii, jj: (ii[p], h)
    kmap = lambda h, p, ii, jj: (jj[p], h)
    tmap = lambda h, p, ii, jj: (h, ii[p])
    return pl.pallas_call(
        body, name="mla_flash_fwd",
        grid_spec=pltpu.PrefetchScalarGridSpec(
            num_scalar_prefetch=2, grid=(MLA_HEADS, int(ii.shape[0])),
            in_specs=[blk(qmap), blk(kmap), blk(kmap)],
            out_specs=[blk(qmap), pl.BlockSpec((LANES, tq), tmap), blk(qmap), pl.BlockSpec((8, tq), tmap)],
            scratch_shapes=[pltpu.VMEM((tq, LANES), F32)] * 2),
        out_shape=[jax.ShapeDtypeStruct((T, HP), BF16), jax.ShapeDtypeStruct((HP, T), BF16),
                   jax.ShapeDtypeStruct((T, HP), F32), jax.ShapeDtypeStruct((8 * MLA_HEADS, T), F32)],
        compiler_params=_cparams(("parallel", "arbitrary")),
    )(ii, jj, q, k, v)


def _flash_dq(q, k, v, do, lse, delta):
    T = q.shape[0]
    tq, sr = _flash_tiles(T)
    n = T // tq
    ii, jj = _pairs(n, True)

    def body(ii_ref, jj_ref, q_ref, k_ref, v_ref, do_ref, lse_ref, dl_ref, dq_ref, acc_sc):
        p_ = pl.program_id(1)
        i, j = ii_ref[p_], jj_ref[p_]

        @pl.when(j == 0)
        def _():
            acc_sc[...] = jnp.zeros(acc_sc.shape, F32)

        def tile(diag):
            for rb in range(tq // sr):
                rows = slice(rb * sr, (rb + 1) * sr)
                c0, c1 = _col_span(tq, sr, rb, diag, False)
                ks = k_ref[c0:c1, :]
                p = jnp.exp2(_dot(q_ref[rows, :], ks, "nt") - _lanes(lse_ref[rows, :], c1 - c0))
                if diag:
                    r, c = _span_iota(sr, rb, c0, c1)
                    p = jnp.where(c <= r, p, 0.0)
                dp = _dot(do_ref[rows, :], v_ref[c0:c1, :], "nt")
                acc_sc[rows, :] += _dot(p * (dp - _lanes(dl_ref[rows, :], c1 - c0)), ks)

        @pl.when(j < i)
        def _():
            tile(False)

        @pl.when(j == i)
        def _():
            tile(True)
            dq_ref[...] = acc_sc[...].astype(dq_ref.dtype)

    blk = lambda which: pl.BlockSpec((tq, LANES), which)
    qmap = lambda h, p, ii, jj: (ii[p], h)
    kmap = lambda h, p, ii, jj: (jj[p], h)
    return pl.pallas_call(
        body, name="mla_flash_dq",
        grid_spec=pltpu.PrefetchScalarGridSpec(
            num_scalar_prefetch=2, grid=(MLA_HEADS, int(ii.shape[0])),
            in_specs=[blk(qmap), blk(kmap), blk(kmap), blk(qmap), blk(qmap), blk(qmap)],
            out_specs=blk(qmap),
            scratch_shapes=[pltpu.VMEM((tq, LANES), F32)]),
        out_shape=jax.ShapeDtypeStruct((T, HP), BF16),
        compiler_params=_cparams(("parallel", "arbitrary")),
    )(ii, jj, q, k, v, do, lse, delta)


def _flash_dkv(q, k, v, do, lse_t, delta_t):
    T = q.shape[0]
    tq, sr = _flash_tiles(T)
    n = T // tq
    jj, ii = _pairs(n, False)

    def body(jj_ref, ii_ref, q_ref, k_ref, v_ref, do_ref, lse_ref, dl_ref, dk_ref, dv_ref, dk_sc, dv_sc):
        p_ = pl.program_id(1)
        j, i = jj_ref[p_], ii_ref[p_]

        @pl.when(i == j)
        def _():
            dk_sc[...] = jnp.zeros(dk_sc.shape, F32)
            dv_sc[...] = jnp.zeros(dv_sc.shape, F32)

        def tile(diag):
            for rb in range(tq // sr):
                rows = slice(rb * sr, (rb + 1) * sr)
                c0, c1 = _col_span(tq, sr, rb, diag, True)
                qs, dos = q_ref[c0:c1, :], do_ref[c0:c1, :]
                pt = jnp.exp2(_dot(k_ref[rows, :], qs, "nt") - lse_ref[:1, c0:c1])
                if diag:
                    r, c = _span_iota(sr, rb, c0, c1)
                    pt = jnp.where(r <= c, pt, 0.0)
                dpt = _dot(v_ref[rows, :], dos, "nt")
                dv_sc[rows, :] += _dot(pt, dos)
                dk_sc[rows, :] += _dot(pt * (dpt - dl_ref[:1, c0:c1]), qs)

        @pl.when(i == j)
        def _():
            tile(True)

        @pl.when(i > j)
        def _():
            tile(False)

        @pl.when(i == n - 1)
        def _():
            dk_ref[...] = (dk_sc[...] * LN2).astype(dk_ref.dtype)
            dv_ref[...] = dv_sc[...].astype(dv_ref.dtype)

    blk = lambda which: pl.BlockSpec((tq, LANES), which)
    qmap = lambda h, p, jj, ii: (ii[p], h)
    kmap = lambda h, p, jj, ii: (jj[p], h)
    lse_rows = pl.BlockSpec((8, tq), lambda h, p, jj, ii: (h, ii[p]))
    delta_rows = pl.BlockSpec((8, tq), lambda h, p, jj, ii: (h * (LANES // 8), ii[p]))
    return pl.pallas_call(
        body, name="mla_flash_dkv",
        grid_spec=pltpu.PrefetchScalarGridSpec(
            num_scalar_prefetch=2, grid=(MLA_HEADS, int(ii.shape[0])),
            in_specs=[blk(qmap), blk(kmap), blk(kmap), blk(qmap), lse_rows, delta_rows],
            out_specs=[blk(kmap), blk(kmap)],
            scratch_shapes=[pltpu.VMEM((tq, LANES), F32)] * 2),
        out_shape=[jax.ShapeDtypeStruct((T, HP), BF16)] * 2,
        compiler_params=_cparams(("parallel", "arbitrary")),
    )(jj, ii, q, k, v, do, lse_t, delta_t)


def _mem_fwd(z, km, vm, gq):
    scale = MEM_HEAD_DIM ** -0.5

    def fn(qm, km, vm, gq):
        ys = []
        for h in range(MEM_HEADS):
            sl = slice(h * LANES, (h + 1) * LANES)
            q = _rms(qm[:, sl], gq) * scale
            s = _dot(q, km[:, sl], "nt")
            p = jnp.exp(s - jnp.max(s, axis=1, keepdims=True))
            p = p / jnp.sum(p, axis=1, keepdims=True)
            ys.append(_dot(p, vm[:, sl]))
        y = jnp.concatenate(ys, axis=1)
        return y, y

    return _rowwise(fn, "mem_fwd", 512, [(z, MEM_WIDTH, Z_QM // MEM_WIDTH)], [km, vm, gq],
                    [(MEM_WIDTH, BF16), (MEM_WIDTH, BF16, "T")])


def _mem_bwd(z, dy, km, vm, gq, dz):
    scale = MEM_HEAD_DIM ** -0.5

    def fn(qm, dy, km, vm, gq):
        dqs, dks, dvs = [], [], []
        dgq = jnp.zeros((1, LANES), F32)
        for h in range(MEM_HEADS):
            sl = slice(h * LANES, (h + 1) * LANES)
            q = (_rms(qm[:, sl], gq) * scale).astype(BF16)
            dyh = dy[:, sl]
            kh, vh = km[:, sl], vm[:, sl]
            s = _dot(q, kh, "nt")
            p = jnp.exp(s - jnp.max(s, axis=1, keepdims=True))
            p = p / jnp.sum(p, axis=1, keepdims=True)
            dp = _dot(dyh, vh, "nt")
            ds = p * (dp - jnp.sum(p * dp, axis=1, keepdims=True))
            dq = _dot(ds, kh) * scale
            dx, dg = _rms_bwd(qm[:, sl], gq, dq)
            dqs.append(dx)
            dgq = dgq + dg
            st = _dot(kh, q, "nt")
            pt = jnp.exp(st - jnp.max(st, axis=0, keepdims=True))
            pt = pt / jnp.sum(pt, axis=0, keepdims=True)
            dpt = _dot(vh, dyh, "nt")
            dst = pt * (dpt - jnp.sum(pt * dpt, axis=0, keepdims=True))
            dvs.append(_dot(pt, dyh))
            dks.append(_dot(dst, q))
        return jnp.concatenate(dqs, axis=1), jnp.concatenate(dks, axis=1), jnp.concatenate(dvs, axis=1), dgq

    m = km.shape[0]
    return _rowwise(fn, "mem_bwd", 512, [(z, MEM_WIDTH, Z_QM // MEM_WIDTH), dy], [km, vm, gq],
                    [("into", dz, MEM_WIDTH, Z_QM // MEM_WIDTH)],
                    [((m, MEM_WIDTH), F32), ((m, MEM_WIDTH), F32), ((1, LANES), F32)])


GROUPS = {"ffn1": ["ffn1_w_gu"], "ffn1_down": ["ffn1_w_down"],
          "mix": ["w_in", "mla_w_uq", "mla_w_ukv", "mem_w_kv", "w_branch_a", "w_branch_b", "w_branch_c", "w_out"],
          "ffn2": ["ffn2_w_gu", "ffn2_w_down"]}
GRAD_GROUPS = {"ffn2": GROUPS["ffn2"], "mix": GROUPS["mix"], "ffn1_down": ["ffn1_w_down"], "ffn1_gu": ["ffn1_w_gu"]}


def _local_step(x, mem, positions, loss_target, P, weights, grads_out):
    T = x.shape[0]
    G = {}
    W = dict(weights("ffn1", None))

    half = MLA_ROPE // 2
    inv = ROPE_BASE ** (-jnp.arange(half, dtype=F32) / half)
    ang = positions.astype(F32)[:, None] * inv
    cos, sin = jnp.cos(ang), jnp.sin(ang)
    one, zero = jnp.ones((T, MLA_NOPE), F32), jnp.zeros((T, half), F32)
    pad = LANES - MLA_QK
    tabs = (jnp.concatenate([one, cos, cos, jnp.ones((T, pad), F32)], axis=1),
            jnp.concatenate([jnp.zeros((T, MLA_NOPE), F32), -sin, zero, jnp.zeros((T, pad), F32)], axis=1),
            jnp.concatenate([jnp.zeros((T, MLA_NOPE), F32), zero, sin, jnp.zeros((T, pad), F32)], axis=1))
    gq_p = jnp.pad(P["mla_q_norm"], ((0, 0), (0, pad)))
    gk_p = jnp.pad(P["mla_k_norm"], ((0, 0), (0, pad)))
    bias_full = jnp.repeat(P["sg_b"].T, SG_GROUP_DIM, axis=1)
    group_ind = jnp.repeat(jnp.eye(SG_GROUPS, dtype=F32), SG_GROUP_DIM, axis=0)

    HT = (D_MODEL, BF16, "T")

    def norm2(x, g):
        h = _rms(x, g)
        return h, h

    h1, h1t = _rowwise(norm2, "ffn1_norm", 512, [x], [P["ffn1_norm"]], [(D_MODEL, BF16), HT])
    def ffn1_w_down(after):
        W.update(weights("ffn1_down", after))
        return W["ffn1_w_down"]

    gu1, a1t, o1 = _ffn_fwd(h1, W["ffn1_w_gu"], ffn1_w_down, "ffn1")

    def resid_norm(x, o, g):
        xn = x + 0.5 * o
        h = _rms(xn, g)
        return xn, h, h

    x1, hm, hmt = _rowwise(resid_norm, "mix_norm", 512, [x, o1], [P["mix_norm"]],
                           [(D_MODEL, F32), (D_MODEL, BF16), HT])
    W.update(weights("mix", hm))
    z = _mm(hm, W["w_in"], "nn", BF16, "w_in", tm=1024, tn=1792)

    y_a, y_at = _sg_fwd(z, P["sg_ln_g"], P["sg_ln_b"], P["sg_w"], bias_full)

    def c_norm(cq, ckv, gq, gkv):
        a, b = _rms(cq, gq), _rms(ckv, gkv)
        return a, b, a, b

    cqn, ckvn, cqnt, ckvnt = _rowwise(
        c_norm, "mla_cnorm", 512, [(z, MLA_Q_RANK, Z_CQ // MLA_Q_RANK), (z, MLA_KV_RANK, Z_CKV // MLA_KV_RANK)],
        [P["mla_cq_norm"], P["mla_ckv_norm"]],
        [(MLA_Q_RANK, BF16), (MLA_KV_RANK, BF16), (MLA_Q_RANK, BF16, "T"), (MLA_KV_RANK, BF16, "T")])
    q_pre = _mm(cqn, W["mla_w_uq"], "nn", BF16, "mla_uq", tm=1024, tn=1024)
    kv_pre = _mm(ckvn, W["mla_w_ukv"], "nn", BF16, "mla_ukv", tm=1024, tn=1024)
    q, k, v = _mla_post(q_pre, kv_pre, z, tabs, gq_p, gk_p)
    y_b, y_bt, lse, lse_t = _flash_fwd(q, k, v)

    memn, = _rowwise(lambda m, g: _rms(m, g), "mem_norm", 256, [mem], [P["mem_norm"]], [(D_MODEL, BF16)])
    kvm = _mm(memn, W["mem_w_kv"], "nn", F32, "mem_kv")

    def mem_k(kvm, gk):
        ks = [_rms(kvm[:, h * LANES:(h + 1) * LANES], gk) for h in range(MEM_HEADS)]
        return jnp.concatenate(ks, axis=1), kvm[:, MEM_WIDTH:]

    km, vm = _rowwise(mem_k, "mem_knorm", 256, [kvm], [P["mem_k_norm"]], [(MEM_WIDTH, BF16), (MEM_WIDTH, BF16)])
    y_c, y_ct = _mem_fwd(z, km, vm, P["mem_q_norm"])

    pa = _mm(y_a, W["w_branch_a"], "nn", BF16, "branch_a", tm=1024, tn=1024)
    pb = _mm(y_b, W["w_branch_b"], "nn", BF16, "branch_b", tm=1024, tn=1024)
    pc = _mm(y_c, W["w_branch_c"], "nn", BF16, "branch_c", tm=1024, tn=1024)

    def merge(zg, pa, pb, pc, b):
        g = _sigmoid(zg + b)
        m = g[:, :D_MODEL] * pa + g[:, D_MODEL:2 * D_MODEL] * pb + g[:, 2 * D_MODEL:] * pc
        return m, m

    merged, mergedt = _rowwise(merge, "merge", 256, [(z, 3 * D_MODEL, 0), pa, pb, pc], [P["b_gate"]],
                               [(D_MODEL, BF16), HT])
    om = _mm(merged, W["w_out"], "nn", BF16, "w_out", tm=1024, tn=1024)

    def resid_norm1(x, o, g):
        xn = x + o
        h = _rms(xn, g)
        return xn, h, h

    x2, h2, h2t = _rowwise(resid_norm1, "ffn2_norm", 512, [x1, om], [P["ffn2_norm"]],
                           [(D_MODEL, F32), (D_MODEL, BF16), HT])
    W.update(weights("ffn2", h2))
    gu2, a2t, o2 = _ffn_fwd(h2, W["ffn2_w_gu"], W["ffn2_w_down"], "ffn2")

    def loss_fn(x2, o2, t):
        e = x2 + 0.5 * o2 - t
        return e * (1.0 / D_MODEL), (e * (0.5 / D_MODEL)).astype(BF16), _rsum(e * e) * (0.5 / D_MODEL)

    dx3, do2, loss_part = _rowwise(loss_fn, "loss", 512, [x2, o2, loss_target], [],
                                   [(D_MODEL, F32), (D_MODEL, BF16)], [((1, D_MODEL), F32)])

    dh2, G["ffn2_w_gu"], G["ffn2_w_down"] = _ffn_bwd(do2, h2t, gu2, a2t, W["ffn2_w_gu"], W["ffn2_w_down"], "ffn2")
    tie = grads_out("ffn2", G)

    def norm_bwd(x, dh, dxo, g, *_):
        dx, dg = _rms_bwd(x, g, dh)
        dx = dx + dxo
        return dx, dx, dg

    dx2, dx2b, G["ffn2_norm"] = _rowwise(norm_bwd, "ffn2_norm_bwd", 512, [x2, dh2, dx3],
                                         [P["ffn2_norm"]] + ([] if tie is None else [tie]),
                                         [(D_MODEL, F32), (D_MODEL, BF16)], [((1, D_MODEL), F32)])

    G["w_out"] = _mm_t(mergedt, dx2b, "w_out_dw", tm=1024, tn=1024)
    dmerged = _mm(dx2b, W["w_out"], "nt", BF16, "w_out_dx", tm=1024, tn=1024)

    def merge_bwd(zg, pa, pb, pc, dm, b):
        g = _sigmoid(zg + b)
        ps = jnp.concatenate([pa, pb, pc], axis=1)
        dm3 = jnp.concatenate([dm, dm, dm], axis=1)
        dzg = dm3 * ps * g * (1.0 - g)
        dp = dm3 * g
        return dzg, dp[:, :D_MODEL], dp[:, D_MODEL:2 * D_MODEL], dp[:, 2 * D_MODEL:], _rsum(dzg)

    dz = lax.empty((T, Z_COLS), BF16)
    dz, dpa, dpb, dpc, G["b_gate"] = _rowwise(
        merge_bwd, "merge_bwd", 256, [(z, 3 * D_MODEL, 0), pa, pb, pc, dmerged], [P["b_gate"]],
        [("into", dz, 3 * D_MODEL, 0), (D_MODEL, BF16), (D_MODEL, BF16), (D_MODEL, BF16)], [((1, 3 * D_MODEL), F32)])

    G["w_branch_a"] = _mm_t(y_at, dpa, "branch_a_dw", tm=512, tn=1024)
    G["w_branch_b"] = _mm_t(y_bt, dpb, "branch_b_dw", tm=1024, tn=1024)
    G["w_branch_c"] = _mm_t(y_ct, dpc, "branch_c_dw", tm=512, tn=1024)
    dy_a = _mm(dpa, W["w_branch_a"], "nt", BF16, "branch_a_dx", tm=1024, tn=512)
    dy_b = _mm(dpb, W["w_branch_b"], "nt", BF16, "branch_b_dx", tm=1024, tn=1024)
    dy_c = _mm(dpc, W["w_branch_c"], "nt", BF16, "branch_c_dx", tm=1024, tn=512)

    dz, G["sg_w"], dbias_t, G["sg_ln_g"], G["sg_ln_b"] = _sg_bwd(
        z, dy_a, P["sg_ln_g"], P["sg_ln_b"], P["sg_w"], bias_full, group_ind, dz)
    G["sg_b"] = dbias_t.T

    dz, dkm, dvm, G["mem_q_norm"] = _mem_bwd(z, dy_c, km, vm, P["mem_q_norm"], dz)

    def mem_k_bwd(kvm, dkm, dvm, gk):
        dks = []
        dg = jnp.zeros((1, LANES), F32)
        for h in range(MEM_HEADS):
            sl = slice(h * LANES, (h + 1) * LANES)
            dx, d = _rms_bwd(kvm[:, sl], gk, dkm[:, sl])
            dks.append(dx)
            dg = dg + d
        return jnp.concatenate(dks + [dvm], axis=1), dg

    dkvm, G["mem_k_norm"] = _rowwise(mem_k_bwd, "mem_knorm_bwd", 256, [kvm, dkm, dvm], [P["mem_k_norm"]],
                                     [(2 * MEM_WIDTH, BF16)], [((1, LANES), F32)])
    G["mem_w_kv"] = _mm(memn, dkvm, "tn", BF16, "mem_kv_dw")
    dmemn = _mm(dkvm, W["mem_w_kv"], "nt", F32, "mem_kv_dx")
    _, G["mem_norm"] = _rowwise(lambda m, d, g: _rms_bwd(m, g, d), "mem_norm_bwd", 256, [mem, dmemn],
                                [P["mem_norm"]], [(D_MODEL, BF16)], [((1, D_MODEL), F32)])

    def delta_fn(o, do):
        od = o.astype(F32) * do.astype(F32)
        ds = [jnp.broadcast_to(jnp.sum(od[:, h * LANES:(h + 1) * LANES], axis=1, keepdims=True), (od.shape[0], LANES))
              for h in range(MLA_HEADS)]
        d = jnp.concatenate(ds, axis=1)
        return d, d

    delta, delta_t = _rowwise(delta_fn, "mla_delta", 512, [y_b, dy_b], [], [(HP, F32), (HP, F32, "T")])
    dq = _flash_dq(q, k, v, dy_b, lse, delta)
    dk, dv = _flash_dkv(q, k, v, dy_b, lse_t, delta_t)
    dq_pre, dkv_pre, dkr, dgq, dgk = _mla_post_bwd(q_pre, kv_pre, z, tabs, gq_p, gk_p, dq, dk, dv)
    G["mla_q_norm"], G["mla_k_norm"] = dgq[:, :MLA_QK], dgk[:, :MLA_QK]
    G["mla_w_uq"] = _mm_t(cqnt, dq_pre, "mla_uq_dw", tm=384, tn=1024)
    G["mla_w_ukv"] = _mm_t(ckvnt, dkv_pre, "mla_ukv_dw", tm=256, tn=2048)
    dcqn = _mm(dq_pre, W["mla_w_uq"], "nt", BF16, "mla_uq_dx", tm=1024)
    dckvn = _mm(dkv_pre, W["mla_w_ukv"], "nt", BF16, "mla_ukv_dx", tm=1024)

    def c_norm_bwd(cq, ckv, dcqn, dckvn, dkr, gq, gkv):
        dcq, dgq = _rms_bwd(cq, gq, dcqn)
        dckv, dgkv = _rms_bwd(ckv, gkv, dckvn)
        return jnp.concatenate([dckv, dkr, dcq], axis=1), dgq, dgkv

    tail = Z_COLS - Z_CKV
    dz, G["mla_cq_norm"], G["mla_ckv_norm"] = _rowwise(
        c_norm_bwd, "mla_cnorm_bwd", 512,
        [(z, MLA_Q_RANK, Z_CQ // MLA_Q_RANK), (z, MLA_KV_RANK, Z_CKV // MLA_KV_RANK), dcqn, dckvn, dkr],
        [P["mla_cq_norm"], P["mla_ckv_norm"]], [("into", dz, tail, Z_CKV // tail)],
        [((1, MLA_Q_RANK), F32), ((1, MLA_KV_RANK), F32)])
    G["w_in"] = _mm_t(hmt, dz, "w_in_dw", tm=1024, tn=1792, tk=2048)
    dhm = _mm(dz, W["w_in"], "nt", BF16, "w_in_dx", tm=1024, tn=1024, tk=2688)

    def norm_bwd_half(x, dh, dxo, g):
        dx, dg = _rms_bwd(x, g, dh)
        dx = dx + dxo
        return dx, (0.5 * dx), dg

    dx1, do1, G["mix_norm"] = _rowwise(norm_bwd_half, "mix_norm_bwd", 512, [x1, dhm, dx2], [P["mix_norm"]],
                                       [(D_MODEL, F32), (D_MODEL, BF16)], [((1, D_MODEL), F32)])
    tie = grads_out("mix", G)

    def ffn1_dw(which, dw):
        G["ffn1_w_" + which] = dw
        return grads_out("ffn1_" + which, G)

    dh1, _, _ = _ffn_bwd(do1, h1t, gu1, a1t, W["ffn1_w_gu"], W["ffn1_w_down"], "ffn1", tie, ffn1_dw)

    def norm_bwd_last(x, dh, dxo, g):
        dx, dg = _rms_bwd(x, g, dh)
        return dx + dxo, dg

    grad_x, G["ffn1_norm"] = _rowwise(norm_bwd_last, "ffn1_norm_bwd", 512, [x, dh1, dx1], [P["ffn1_norm"]],
                                      [(D_MODEL, F32)], [((1, D_MODEL), F32)])
    return loss_part, grad_x, G


SHARDED = ["ffn1_w_gu", "ffn1_w_down", "w_in", "mla_w_uq", "mla_w_ukv", "mem_w_kv",
           "w_branch_a", "w_branch_b", "w_branch_c", "w_out", "ffn2_w_gu", "ffn2_w_down"]
ROW_SHARDED = {"ffn1_w_down", "mem_w_kv", "w_out", "ffn2_w_down"}
SMALL = ["ffn1_norm", "mix_norm", "b_gate", "sg_ln_g", "sg_ln_b", "sg_w", "sg_b", "mla_cq_norm",
         "mla_ckv_norm", "mla_q_norm", "mla_k_norm", "mem_norm", "mem_q_norm", "mem_k_norm", "ffn2_norm"]
ORDER = ["ffn1_norm", "ffn1_w_gu", "ffn1_w_down", "mix_norm", "w_in", "b_gate", "sg_ln_g", "sg_ln_b", "sg_w",
         "sg_b", "mla_cq_norm", "mla_w_uq", "mla_ckv_norm", "mla_w_ukv", "mla_q_norm", "mla_k_norm", "mem_norm",
         "mem_w_kv", "mem_q_norm", "mem_k_norm", "w_branch_a", "w_branch_b", "w_branch_c", "w_out", "ffn2_norm",
         "ffn2_w_gu", "ffn2_w_down"]

_IN_U, _IN_V, _IN_CQ, _IN_CKV, _IN_KR, _IN_QM, _IN_G = 0, 512, 1024, 1408, 1664, 1696, 2208
IN_COLS = 5280


def _full_from_slabs(name, slabs):
    n, r, c = slabs.shape
    if name in ROW_SHARDED:
        return slabs.reshape(n * r, c)
    return slabs.transpose(1, 0, 2).reshape(r, n * c)


def _slabs_from_full(name, full):
    if name in ROW_SHARDED:
        return full.reshape(N_DEV, full.shape[0] // N_DEV, full.shape[1])
    r, c = full.shape
    return full.reshape(r, N_DEV, c // N_DEV).transpose(1, 0, 2)


def _compute_layout(full):
    W = dict(full)
    if "w_in" not in full:
        return W
    w = full["w_in"]
    kr = jnp.pad(w[:, _IN_KR:_IN_QM], ((0, 0), (KR_LANE, LANES - KR_LANE - MLA_ROPE)))
    W["w_in"] = jnp.concatenate([w[:, _IN_G:], w[:, _IN_U:_IN_CQ], w[:, _IN_QM:_IN_G], w[:, _IN_CKV:_IN_KR], kr,
                                 w[:, _IN_CQ:_IN_CKV]], axis=1)
    uq = full["mla_w_uq"].reshape(MLA_Q_RANK, MLA_HEADS, MLA_QK)
    W["mla_w_uq"] = jnp.pad(uq, ((0, 0), (0, 0), (0, LANES - MLA_QK))).reshape(MLA_Q_RANK, HP)
    ukv = full["mla_w_ukv"].reshape(MLA_KV_RANK, MLA_HEADS, MLA_NOPE + MLA_V)
    padh = lambda a: jnp.pad(a, ((0, 0), (0, 0), (0, LANES - a.shape[2]))).reshape(MLA_KV_RANK, HP)
    W["mla_w_ukv"] = jnp.concatenate([padh(ukv[:, :, :MLA_NOPE]), padh(ukv[:, :, MLA_NOPE:])], axis=1)
    wb = full["w_branch_b"].reshape(MLA_HEADS, MLA_V, D_MODEL)
    W["w_branch_b"] = jnp.pad(wb, ((0, 0), (0, LANES - MLA_V), (0, 0))).reshape(HP, D_MODEL)
    return W


def _reference_layout(G):
    out = dict(G)
    if "w_in" not in G:
        return out
    g = G["w_in"]
    out["w_in"] = jnp.concatenate([
        g[:, Z_U:Z_QM], g[:, Z_CQ:Z_COLS], g[:, Z_CKV:Z_KR], g[:, Z_KR + KR_LANE:Z_KR + KR_LANE + MLA_ROPE],
        g[:, Z_QM:Z_CKV], g[:, Z_G:Z_U]], axis=1)
    out["mla_w_uq"] = G["mla_w_uq"].reshape(MLA_Q_RANK, MLA_HEADS, LANES)[:, :, :MLA_QK].reshape(MLA_Q_RANK, -1)
    gk = G["mla_w_ukv"][:, :HP].reshape(MLA_KV_RANK, MLA_HEADS, LANES)[:, :, :MLA_NOPE]
    gv = G["mla_w_ukv"][:, HP:].reshape(MLA_KV_RANK, MLA_HEADS, LANES)[:, :, :MLA_V]
    out["mla_w_ukv"] = jnp.concatenate([gk, gv], axis=2).reshape(MLA_KV_RANK, -1)
    out["w_branch_b"] = G["w_branch_b"].reshape(MLA_HEADS, LANES, D_MODEL)[:, :MLA_V].reshape(-1, D_MODEL)
    return out


def _pack(parts):
    flat = []
    for a in parts:
        a = a.reshape(-1)
        flat.append(jnp.pad(a, (0, (-a.shape[0]) % LANES)))
    return jnp.concatenate(flat).reshape(-1, LANES)


def _unpack(packed, shapes):
    flat = packed.reshape(-1)
    out, off = [], 0
    for shp in shapes:
        n = int(np.prod(shp))
        out.append(flat[off:off + n].reshape(shp))
        off += n + (-n) % LANES
    return out


MESH = pl.DeviceIdType.MESH
HBM = pl.BlockSpec(memory_space=pltpu.HBM)


def _all_gather(shards):
    n = len(shards)

    def body(*refs):
        x_refs, out_refs, token_ref = refs[:n], refs[n:2 * n], refs[2 * n]
        send_sems, recv_sems, local_sems = refs[2 * n + 1:]
        x, y, c = lax.axis_index("x"), lax.axis_index("y"), lax.axis_index("c")
        me, sibling = (x, y, c), (x, y, 1 - c)
        chips = [(1 - x, y), (x, 1 - y), (1 - x, 1 - y)]
        token_ref[...] = jnp.zeros_like(token_ref)

        def slot(a, px, py, pc):
            return out_refs[a].at[4 * px + 2 * py + pc]

        def copy(a, k, block, to, src=None):
            return pltpu.make_async_remote_copy(
                src_ref=slot(a, *block) if src is None else src, dst_ref=slot(a, *block),
                send_sem=send_sems.at[7 * a + k], recv_sem=recv_sems.at[7 * a + k], device_id=to, device_id_type=MESH)

        arrays = range(n)
        mine = [pltpu.make_async_copy(x_refs[a], slot(a, *me), local_sems.at[a]) for a in arrays]
        for cp in mine:
            cp.start()
        first = [copy(a, 0, me, sibling, src=x_refs[a]) for a in arrays]
        first += [copy(a, 1 + j, me, (*chip, c), src=x_refs[a]) for j, chip in enumerate(chips) for a in arrays]
        for cp in first:
            cp.start()
        passed = []
        for j, chip in enumerate(chips):
            for a in arrays:
                copy(a, 1 + j, (*chip, c), me).wait_recv()
                passed.append(copy(a, 4 + j, (*chip, c), sibling))
                passed[-1].start()
        for a in arrays:
            copy(a, 0, sibling, me).wait_recv()
        for j, chip in enumerate(chips):
            for a in arrays:
                copy(a, 4 + j, (*chip, 1 - c), me).wait_recv()
        for cp in first + passed:
            cp.wait_send()
        for cp in mine:
            cp.wait()

    res = pl.pallas_call(
        body, name="all_gather_weights",
        out_shape=[jax.ShapeDtypeStruct((N_DEV,) + s.shape, s.dtype) for s in shards]
        + [jax.ShapeDtypeStruct((8, LANES), F32)],
        in_specs=[HBM] * n, out_specs=[HBM] * n + [pl.BlockSpec(memory_space=pltpu.VMEM)],
        scratch_shapes=[pltpu.SemaphoreType.DMA((7 * n,)), pltpu.SemaphoreType.DMA((7 * n,)),
                        pltpu.SemaphoreType.DMA((n,))],
    )(*shards)
    return res[:n], res[n]


SEM = pl.BlockSpec(memory_space=pltpu.SEMAPHORE)
DATAFLOW = pltpu.SideEffectType.DATAFLOW_SIDE_EFFECTING


def _peers():
    x, y, c = lax.axis_index("x"), lax.axis_index("y"), lax.axis_index("c")
    out = []
    for k in range(1, N_DEV):
        px = 1 - x if k & 4 else x
        py = 1 - y if k & 2 else y
        pc = 1 - c if k & 1 else c
        out.append((k, (px, py, pc), 4 * px + 2 * py + pc))
    return 4 * x + 2 * y + c, out


def _send_start(srcs, per_peer, name):
    n = len(srcs)
    lands = [lax.empty((N_DEV,) + (s.shape[1:] if per_peer else s.shape), s.dtype) for s in srcs]

    def body(*refs):
        src_refs, land_refs, send_sems, recv_sems, token = refs[:n], refs[n:2 * n], refs[2 * n], refs[2 * n + 1], refs[-1]
        me, peers = _peers()
        for a in range(n):
            for k, pid, pflat in peers:
                pltpu.make_async_remote_copy(
                    src_ref=src_refs[a].at[pflat] if per_peer else src_refs[a], dst_ref=land_refs[a].at[me],
                    send_sem=send_sems.at[7 * a + k - 1], recv_sem=recv_sems.at[7 * a + k - 1],
                    device_id=pid, device_id_type=MESH).start()
        token[...] = jnp.zeros_like(token)

    hbm = lambda a: pltpu.with_memory_space_constraint(a, pltpu.HBM)
    res = pl.pallas_call(
        body, name=name,
        out_shape=(pltpu.SemaphoreType.DMA((7 * n,)), pltpu.SemaphoreType.DMA((7 * n,)),
                   *[pltpu.HBM(a.shape, a.dtype) for a in srcs + lands], jax.ShapeDtypeStruct((8, LANES), F32)),
        in_specs=(HBM,) * (2 * n), out_specs=(SEM, SEM) + (HBM,) * (2 * n) + (pl.BlockSpec(memory_space=pltpu.VMEM),),
        input_output_aliases={i: 2 + i for i in range(2 * n)},
        compiler_params=pltpu.CompilerParams(has_side_effects=DATAFLOW),
    )(*[hbm(a) for a in srcs + lands])
    return (res[0], res[1], list(res[2:2 + n]), list(res[2 + n:2 + 2 * n])), res[-1]


def _send_wait(started, after, per_peer, name):
    send_sems, recv_sems, srcs_thru, lands_thru = started
    n = len(srcs_thru)

    def body(*refs):
        src_refs, land_refs, send_sems, recv_sems = refs[:n], refs[n:2 * n], refs[2 * n], refs[2 * n + 1]
        me, peers = _peers()
        for a in range(n):
            for k, pid, pflat in peers:
                copy = pltpu.make_async_remote_copy(
                    src_ref=src_refs[a].at[pflat] if per_peer else src_refs[a], dst_ref=land_refs[a].at[pflat],
                    send_sem=send_sems.at[7 * a + k - 1], recv_sem=recv_sems.at[7 * a + k - 1],
                    device_id=pid, device_id_type=MESH)
                copy.wait_send()
                copy.wait_recv()

    outs = pl.pallas_call(
        body, name=name,
        out_shape=tuple(pltpu.HBM(a.shape, a.dtype) for a in srcs_thru + lands_thru),
        in_specs=(HBM,) * (2 * n) + (SEM, SEM, pl.BlockSpec(memory_space=pl.ANY)), out_specs=(HBM,) * (2 * n),
        input_output_aliases={i: i for i in range(2 * n)},
        compiler_params=pltpu.CompilerParams(has_side_effects=DATAFLOW),
    )(*srcs_thru, *lands_thru, send_sems, recv_sems, after)
    me = 4 * lax.axis_index("x") + 2 * lax.axis_index("y") + lax.axis_index("c")
    landed = []
    for src_out, land in zip(outs[:n], outs[n:]):
        own = lax.dynamic_index_in_dim(src_out, me, 0, keepdims=True) if per_peer else src_out[None]
        landed.append(lax.dynamic_update_slice(land, own, (me,) + (0,) * (land.ndim - 1)))
    return landed


def _share_rows(block, name):
    def body(src_ref, out_ref, send_sems, recv_sems, local_sem):
        me, peers = _peers()
        own = pltpu.make_async_copy(src_ref, out_ref.at[me], local_sem)
        own.start()
        copies = [pltpu.make_async_remote_copy(
            src_ref=src_ref, dst_ref=out_ref.at[me], send_sem=send_sems.at[k - 1], recv_sem=recv_sems.at[k - 1],
            device_id=pid, device_id_type=MESH) for k, pid, _ in peers]
        for cp in copies:
            cp.start()
        for cp in copies:
            cp.wait()
        own.wait()

    return pl.pallas_call(
        body, name=name, out_shape=jax.ShapeDtypeStruct((N_DEV,) + block.shape, block.dtype),
        in_specs=[HBM], out_specs=HBM,
        scratch_shapes=[pltpu.SemaphoreType.DMA((N_DEV - 1,)), pltpu.SemaphoreType.DMA((N_DEV - 1,)),
                        pltpu.SemaphoreType.DMA],
    )(block)


def _sum_slots(recv, name, tr):
    n, rows, lanes = recv.shape
    tr = _tile(rows, tr)

    def body(r_ref, o_ref):
        acc = r_ref[0].astype(F32)
        for i in range(1, n):
            acc = acc + r_ref[i].astype(F32)
        o_ref[...] = acc

    return pl.pallas_call(
        body, name=name, grid=(rows // tr,),
        in_specs=[pl.BlockSpec((n, tr, lanes), lambda i: (0, i, 0))],
        out_specs=pl.BlockSpec((tr, lanes), lambda i: (i, 0)),
        out_shape=jax.ShapeDtypeStruct((rows, lanes), F32),
        compiler_params=_cparams(("parallel",)),
    )(recv)


def _adamw_math(w, g, m, v):
    m = ADAM_B1 * m + (1.0 - ADAM_B1) * g
    v = ADAM_B2 * v + (1.0 - ADAM_B2) * (g * g)
    m_hat = m / (1.0 - ADAM_B1 ** ADAM_STEP)
    v_hat = v / (1.0 - ADAM_B2 ** ADAM_STEP)
    return -ADAM_LR * (m_hat / (jnp.sqrt(v_hat) + ADAM_EPS) + ADAM_WD * w), m, v


def _adamw(w, g, m, v, name, tr=256):
    return _rowwise(_adamw_math, name, tr, [w, g, m, v], [], [(w.shape[1], F32)] * 3)


def _adamw_small(ws, gs, ms, vs):
    n = len(ws)

    def body(*refs):
        ins, outs = refs[:4 * n], refs[4 * n:]
        for i in range(n):
            d, m, v = _adamw_math(ins[i][...], ins[n + i][...], ins[2 * n + i][...], ins[3 * n + i][...])
            outs[i][...], outs[n + i][...], outs[2 * n + i][...] = d, m, v

    vmem = pl.BlockSpec(memory_space=pltpu.VMEM)
    res = pl.pallas_call(
        body, name="adamw_small", in_specs=[vmem] * (4 * n), out_specs=[vmem] * (3 * n),
        out_shape=[jax.ShapeDtypeStruct(w.shape, F32) for w in ws] * 3,
    )(*ws, *gs, *ms, *vs)
    return res[:n], res[n:2 * n], res[2 * n:]


def _sum_adamw(recv, w, m, v, name):
    n, r, c = recv.shape
    tr = _tile(r, 256)

    def body(r_ref, w_ref, m_ref, v_ref, g_ref, d_ref, nm_ref, nv_ref):
        g = r_ref[0].astype(F32)
        for i in range(1, n):
            g = g + r_ref[i].astype(F32)
        g_ref[...] = g
        d_ref[...], nm_ref[...], nv_ref[...] = _adamw_math(w_ref[...], g, m_ref[...], v_ref[...])

    row = pl.BlockSpec((None, tr, c), lambda i: (0, i, 0))
    return pl.pallas_call(
        body, name=name, grid=(r // tr,),
        in_specs=[pl.BlockSpec((n, tr, c), lambda i: (0, i, 0)), row, row, row], out_specs=[row] * 4,
        out_shape=[jax.ShapeDtypeStruct((1, r, c), F32)] * 4, compiler_params=_cparams(("parallel",)),
    )(recv, w, m, v)


def kernel(x, mem, positions, ffn1_norm, ffn1_w_gu, ffn1_w_down, mix_norm, w_in, b_gate, sg_ln_g, sg_ln_b, sg_w, sg_b, mla_cq_norm, mla_w_uq, mla_ckv_norm, mla_w_ukv, mla_q_norm, mla_k_norm, mem_norm, mem_w_kv, mem_q_norm, mem_k_norm, w_branch_a, w_branch_b, w_branch_c, w_out, ffn2_norm, ffn2_w_gu, ffn2_w_down, loss_target, m_ffn1_norm, m_ffn1_w_gu, m_ffn1_w_down, m_mix_norm, m_w_in, m_b_gate, m_sg_ln_g, m_sg_ln_b, m_sg_w, m_sg_b, m_mla_cq_norm, m_mla_w_uq, m_mla_ckv_norm, m_mla_w_ukv, m_mla_q_norm, m_mla_k_norm, m_mem_norm, m_mem_w_kv, m_mem_q_norm, m_mem_k_norm, m_w_branch_a, m_w_branch_b, m_w_branch_c, m_w_out, m_ffn2_norm, m_ffn2_w_gu, m_ffn2_w_down, v_ffn1_norm, v_ffn1_w_gu, v_ffn1_w_down, v_mix_norm, v_w_in, v_b_gate, v_sg_ln_g, v_sg_ln_b, v_sg_w, v_sg_b, v_mla_cq_norm, v_mla_w_uq, v_mla_ckv_norm, v_mla_w_ukv, v_mla_q_norm, v_mla_k_norm, v_mem_norm, v_mem_w_kv, v_mem_q_norm, v_mem_k_norm, v_w_branch_a, v_w_branch_b, v_w_branch_c, v_w_out, v_ffn2_norm, v_ffn2_w_gu, v_ffn2_w_down):
    given = dict(locals())
    wts = {n: given[n] for n in ORDER}
    mom = {n: given["m_" + n] for n in ORDER}
    var = {n: given["v_" + n] for n in ORDER}

    def shards(group, zero):
        out = [wts[n][0].astype(BF16) for n in GROUPS[group]]
        return [out[0] + zero.astype(BF16)] + out[1:]

    def full_weights(group, slabs):
        return _compute_layout({n: _full_from_slabs(n, s) for n, s in zip(GROUPS[group], slabs)})

    def zero_of(a):
        return jnp.minimum(jnp.abs(a.reshape(-1)[0]), 0)

    gathered_ffn1, token = _all_gather([wts[n][0].astype(BF16) for n in GROUPS["ffn1"]])
    flight = {}
    flight["ffn1_down"], token = _send_start(shards("ffn1_down", token[0, 0]), False, "gather_ffn1_down_start")
    flight["mix"] = _send_start(shards("mix", token[0, 0]), False, "gather_mix_start")[0]
    recv = {}

    def weights(group, after):
        if group == "ffn1":
            return full_weights(group, gathered_ffn1)
        landed = _send_wait(flight.pop(group), after, False, f"gather_{group}_wait")
        if group == "mix":
            flight["ffn2"] = _send_start(shards("ffn2", zero_of(landed[0])), False, "gather_ffn2_start")[0]
        return full_weights(group, landed)

    small_shapes = [wts[n].shape[1:] for n in SMALL]
    early = SMALL[1:]
    assert SMALL[0] == "ffn1_norm"

    def grads_out(group, G):
        Gr = _reference_layout({n: G[n] for n in GRAD_GROUPS[group]})
        parts = [_slabs_from_full(n, Gr[n]).astype(BF16) for n in GRAD_GROUPS[group]]
        flight["g_" + group], tie = _send_start(parts, True, f"grads_{group}_start")
        if group == "mix":
            small = _pack([G[n].reshape(s) for n, s in zip(early, small_shapes[1:])])
            small = jnp.pad(small, ((0, (-small.shape[0]) % 8), (0, 0)))
            flight["small"], tie = _send_start([small + tie[0, 0]], False, "grads_small_start")
        return tie

    P = {n: wts[n] if wts[n].ndim == 2 else wts[n][0] for n in SMALL}
    loss_part, grad_x, G = _local_step(x[0], mem[0], positions[0], loss_target[0], P, weights, grads_out)

    for group, names in GRAD_GROUPS.items():
        recv.update(zip(names, _send_wait(flight.pop("g_" + group), grad_x, True, f"grads_{group}_wait")))
    early_recv, = _send_wait(flight.pop("small"), grad_x, False, "grads_small_wait")
    last = _share_rows(G["ffn1_norm"].reshape(-1, LANES), "share_ffn1_norm")
    g_small_packed = _sum_slots(jnp.concatenate([last, early_recv], axis=1), "sum_small", 2048)

    grads, delta, new_m, new_v = {}, {}, {}, {}
    for n in SHARDED:
        grads[n], delta[n], new_m[n], new_v[n] = _sum_adamw(recv[n], wts[n], mom[n], var[n], "adamw_" + n)
    grads.update(zip(SMALL, _unpack(g_small_packed, small_shapes)))

    flat2 = lambda d: [d[n].reshape(-1, d[n].shape[-1]) for n in SMALL]
    for dst, vals in zip((delta, new_m, new_v), _adamw_small(flat2(wts), flat2(grads), flat2(mom), flat2(var))):
        dst.update(zip(SMALL, vals))

    loss = lax.psum(jnp.sum(loss_part), ("x", "y", "c"))
    lead = lambda d: [d[n].reshape(wts[n].shape) for n in ORDER]
    return (loss, grad_x[None], *lead(grads), *lead(delta), *lead(new_m), *lead(new_v))
```

```python
import functools

import numpy as np
import jax
import jax.numpy as jnp
from jax import lax
from jax.experimental import pallas as pl
from jax.experimental.pallas import tpu as pltpu

F32, BF16 = jnp.float32, jnp.bfloat16

D_MODEL = 1024
SG_GROUPS, SG_GROUP_DIM, SG_WIDTH, CHUNK = 8, 64, 512, 128
MLA_HEADS, MLA_NOPE, MLA_ROPE, MLA_V, MLA_QK = 8, 64, 32, 64, 96
MLA_Q_RANK, MLA_KV_RANK = 384, 256
MEM_HEADS, MEM_HEAD_DIM, MEM_WIDTH = 4, 128, 512
D_FF = 2816
ROPE_BASE = 10000.0
EPS = 1e-6
NEG = -1e30
ADAM_LR, ADAM_B1, ADAM_B2, ADAM_EPS, ADAM_WD, ADAM_STEP = 0.001, 0.9, 0.999, 1e-08, 0.01, 10

N_DEV = 8
LANES = 128
V7X_VMEM_LIMIT = 56 * 1024 * 1024
HP = MLA_HEADS * LANES

Z_G, Z_U, Z_V, Z_QM, Z_CKV, Z_KR, Z_CQ = 0, 3072, 3584, 4096, 4608, 4864, 4992
Z_COLS = 5376
KR_LANE = 64


def _tile(dim, pref):
    if dim <= pref:
        return dim
    for t in range(pref - pref % LANES, LANES - 1, -LANES):
        if dim % t == 0:
            return t
    for t in range(pref - pref % 8, 7, -8):
        if dim % t == 0:
            return t
    return dim


def _cparams(sem):
    return pltpu.CompilerParams(dimension_semantics=sem, vmem_limit_bytes=V7X_VMEM_LIMIT)


_DN = {"nn": ((1,), (0,)), "nt": ((1,), (1,)), "tn": ((0,), (0,))}


def _dot(a, b, mode="nn"):
    return lax.dot_general(a.astype(BF16), b.astype(BF16), (_DN[mode], ((), ())),
                           preferred_element_type=F32)


def _mm(a, b, mode, out_dtype, name, tm=512, tn=512, tk=2048, tie=None):
    if mode == "tn":
        K, M = a.shape
    else:
        M, K = a.shape
    N = b.shape[0] if mode == "nt" else b.shape[1]
    tm, tn, tk = _tile(M, tm), _tile(N, tn), _tile(K, tk)
    nk = K // tk
    if mode == "tn":
        a_spec = pl.BlockSpec((tk, tm), lambda i, j, k: (k, i))
    else:
        a_spec = pl.BlockSpec((tm, tk), lambda i, j, k: (i, k))
    if mode == "nt":
        b_spec = pl.BlockSpec((tn, tk), lambda i, j, k: (j, k))
    else:
        b_spec = pl.BlockSpec((tk, tn), lambda i, j, k: (k, j))

    ties = [] if tie is None else [tie]

    def body(a_ref, b_ref, *rest):
        o_ref, *scratch = rest[len(ties):]
        p = _dot(a_ref[...], b_ref[...], mode)
        if nk == 1:
            o_ref[...] = p.astype(o_ref.dtype)
        else:
            acc_ref, = scratch
            k = pl.program_id(2)

            @pl.when(k == 0)
            def _():
                acc_ref[...] = p

            @pl.when(k > 0)
            def _():
                acc_ref[...] += p

            @pl.when(k == nk - 1)
            def _():
                o_ref[...] = acc_ref[...].astype(o_ref.dtype)

    return pl.pallas_call(
        body, name=name, grid=(M // tm, N // tn, nk),
        in_specs=[a_spec, b_spec] + [pl.BlockSpec(t.shape, lambda i, j, k: (0, 0)) for t in ties],
        out_specs=pl.BlockSpec((tm, tn), lambda i, j, k: (i, j)),
        out_shape=jax.ShapeDtypeStruct((M, N), out_dtype),
        scratch_shapes=[] if nk == 1 else [pltpu.VMEM((tm, tn), F32)],
        compiler_params=_cparams(("parallel", "parallel", "arbitrary")),
    )(a, b, *ties)


def _mm_t(at, b, name, tm, tn, tk=1024, tie=None):
    return _mm(at, b, "nn", BF16, name, tm=tm, tn=tn, tk=tk, tie=tie)


def _rowwise(fn, name, tr, row_ins, bc_ins, row_outs, acc_outs=()):
    norm = [it if isinstance(it, tuple) else (it, it.shape[1], 0) for it in row_ins]
    rows = norm[0][0].shape[0]
    tr = _tile(rows, tr)
    arrays, in_specs = [], []
    for arr, w, cb in norm:
        arrays.append(arr)
        in_specs.append(pl.BlockSpec((tr, w), lambda i, cb=cb: (i, cb)))
    for arr in bc_ins:
        arrays.append(arr)
        in_specs.append(pl.BlockSpec(arr.shape, lambda i, nd=arr.ndim: (0,) * nd))
    n_in, n_row = len(arrays), len(row_outs)
    out_shape, out_specs, aliases = [], [], {}
    transposed = [len(o) == 3 for o in row_outs]
    for k, o in enumerate(row_outs):
        if o[0] == "into":
            _, target, w, cb = o
            aliases[len(arrays)] = k
            arrays.append(target)
            in_specs.append(pl.BlockSpec(memory_space=pl.ANY))
            out_shape.append(jax.ShapeDtypeStruct(target.shape, target.dtype))
            out_specs.append(pl.BlockSpec((tr, w), lambda i, cb=cb: (i, cb)))
        elif transposed[k]:
            out_shape.append(jax.ShapeDtypeStruct((o[0], rows), o[1]))
            out_specs.append(pl.BlockSpec((o[0], tr), lambda i: (0, i)))
        else:
            out_shape.append(jax.ShapeDtypeStruct((rows, o[0]), o[1]))
            out_specs.append(pl.BlockSpec((tr, o[0]), lambda i: (i, 0)))
    for shp, dt in acc_outs:
        out_shape.append(jax.ShapeDtypeStruct(shp, dt))
        out_specs.append(pl.BlockSpec(shp, lambda i, nd=len(shp): (0,) * nd))

    def body(*refs):
        vals = fn(*[r[...].astype(F32) for r in refs[:n_in]])
        if not isinstance(vals, (tuple, list)):
            vals = (vals,)
        outs = refs[len(arrays):]
        for r, v, t in zip(outs[:n_row], vals[:n_row], transposed):
            r[...] = v.astype(F32).T.astype(r.dtype) if t else v.astype(r.dtype)
        if acc_outs:
            accs = list(zip(outs[n_row:], vals[n_row:]))
            i = pl.program_id(0)

            @pl.when(i == 0)
            def _():
                for r, v in accs:
                    r[...] = v.astype(r.dtype)

            @pl.when(i > 0)
            def _():
                for r, v in accs:
                    r[...] += v.astype(r.dtype)

    res = pl.pallas_call(
        body, name=name, grid=(rows // tr,), in_specs=in_specs, out_specs=out_specs,
        out_shape=out_shape, input_output_aliases=aliases, compiler_params=_cparams(("arbitrary",)),
    )(*arrays)
    return res


def _rsum(x):
    return jnp.sum(x, axis=0, keepdims=True)


def _rms(x, g, n=None):
    n = x.shape[-1] if n is None else n
    r = lax.rsqrt(jnp.sum(x * x, axis=-1, keepdims=True) * (1.0 / n) + EPS)
    return x * r * g


def _rms_bwd(x, g, dy, n=None):
    n = x.shape[-1] if n is None else n
    r = lax.rsqrt(jnp.sum(x * x, axis=-1, keepdims=True) * (1.0 / n) + EPS)
    xh = x * r
    dxh = dy * g
    dx = r * (dxh - xh * (jnp.sum(dxh * xh, axis=-1, keepdims=True) * (1.0 / n)))
    return dx, _rsum(dy * xh)


def _gelu(x):
    return 0.5 * x * (1.0 + lax.erf(x * 0.7071067811865476))


def _gelu_grad(x):
    return 0.5 * (1.0 + lax.erf(x * 0.7071067811865476)) + x * jnp.exp(-0.5 * x * x) * 0.3989422804014327


def _sigmoid(x):
    return 0.5 * jnp.tanh(0.5 * x) + 0.5


FFN_TM, FFN_TN = 1024, 1408
MXU_WIDTH = 256


def _col_chunks(n):
    return [(c, min(c + MXU_WIDTH, n)) for c in range(0, n, MXU_WIDTH)]


def _ffn_gu_act(h, w_gu, tag):
    T = h.shape[0]
    tm, tn = _tile(T, FFN_TM), FFN_TN
    nj = D_FF // tn

    def body(h_ref, wg_ref, wu_ref, gu_ref, a_ref, at_ref):
        h = h_ref[...]
        for c0, c1 in _col_chunks(tn):
            g = _dot(h, wg_ref[:, c0:c1])
            u = _dot(h, wu_ref[:, c0:c1])
            gu_ref[0, :, c0:c1] = g.astype(BF16)
            gu_ref[1, :, c0:c1] = u.astype(BF16)
            a = g * _sigmoid(g) * u
            a_ref[:, c0:c1] = a.astype(BF16)
            at_ref[c0:c1, :] = a.T.astype(BF16)

    return pl.pallas_call(
        body, name=f"{tag}_gu_act", grid=(T // tm, nj),
        in_specs=[pl.BlockSpec((tm, D_MODEL), lambda i, j: (i, 0)),
                  pl.BlockSpec((D_MODEL, tn), lambda i, j: (0, j)),
                  pl.BlockSpec((D_MODEL, tn), lambda i, j: (0, j + nj))],
        out_specs=[pl.BlockSpec((2, tm, tn), lambda i, j: (0, i, j)),
                   pl.BlockSpec((tm, tn), lambda i, j: (i, j)),
                   pl.BlockSpec((tn, tm), lambda i, j: (j, i))],
        out_shape=[jax.ShapeDtypeStruct((2, T, D_FF), BF16), jax.ShapeDtypeStruct((T, D_FF), BF16),
                   jax.ShapeDtypeStruct((D_FF, T), BF16)],
        compiler_params=_cparams(("parallel", "parallel")),
    )(h, w_gu, w_gu)


def _ffn_da_actbwd(do, w_down, gu, tag, tie=None):
    T = do.shape[0]
    tm, tn = _tile(T, FFN_TM), FFN_TN
    ties = [] if tie is None else [tie]

    def body(do_ref, wd_ref, gu_ref, *rest):
        dgu_ref = rest[-1]
        do = do_ref[...]
        for c0, c1 in _col_chunks(tn):
            da = _dot(do, wd_ref[c0:c1, :], "nt")
            g = gu_ref[0, :, c0:c1].astype(F32)
            u = gu_ref[1, :, c0:c1].astype(F32)
            s = _sigmoid(g)
            dgu_ref[0, :, c0:c1] = (da * u * s * (1.0 + g * (1.0 - s))).astype(BF16)
            dgu_ref[1, :, c0:c1] = (da * g * s).astype(BF16)

    return pl.pallas_call(
        body, name=f"{tag}_da_actbwd", grid=(T // tm, D_FF // tn),
        in_specs=[pl.BlockSpec((tm, D_MODEL), lambda i, j: (i, 0)),
                  pl.BlockSpec((tn, D_MODEL), lambda i, j: (j, 0)),
                  pl.BlockSpec((2, tm, tn), lambda i, j: (0, i, j))]
        + [pl.BlockSpec(t.shape, lambda i, j: (0, 0)) for t in ties],
        out_specs=pl.BlockSpec((2, tm, tn), lambda i, j: (0, i, j)),
        out_shape=jax.ShapeDtypeStruct((2, T, D_FF), BF16),
        compiler_params=_cparams(("parallel", "parallel")),
    )(do, w_down, gu, *ties)


def _ffn_dwgu(ht, dgu, tag, tk=2048):
    T = ht.shape[1]
    tn, tk = FFN_TN, _tile(T, tk)
    nj, nk = D_FF // tn, T // tk

    def body(a_ref, b_ref, o_ref, acc_ref):
        k = pl.program_id(1)
        p = _dot(a_ref[...], b_ref[...])

        @pl.when(k == 0)
        def _():
            acc_ref[...] = p

        @pl.when(k > 0)
        def _():
            acc_ref[...] += p

        @pl.when(k == nk - 1)
        def _():
            o_ref[...] = acc_ref[...].astype(o_ref.dtype)

    return pl.pallas_call(
        body, name=f"{tag}_dwgu", grid=(2 * nj, nk),
        in_specs=[pl.BlockSpec((D_MODEL, tk), lambda n, k: (0, k)),
                  pl.BlockSpec((None, tk, tn), lambda n, k: (n // nj, k, n % nj))],
        out_specs=pl.BlockSpec((D_MODEL, tn), lambda n, k: (0, n)),
        out_shape=jax.ShapeDtypeStruct((D_MODEL, 2 * D_FF), BF16),
        scratch_shapes=[pltpu.VMEM((D_MODEL, tn), F32)],
        compiler_params=_cparams(("parallel", "arbitrary")),
    )(ht, dgu)


def _ffn_dh(dgu, w_gu, tag, tm=2048, tie=None):
    T = dgu.shape[1]
    tm, tk = _tile(T, tm), FFN_TN
    nk = D_FF // tk
    ties = [] if tie is None else [tie]

    def body(a_ref, b_ref, *rest):
        o_ref, acc_ref = rest[len(ties):]
        k = pl.program_id(1)
        p = _dot(a_ref[...], b_ref[...], "nt")

        @pl.when(k == 0)
        def _():
            acc_ref[...] = p

        @pl.when(k > 0)
        def _():
            acc_ref[...] += p

        @pl.when(k == 2 * nk - 1)
        def _():
            o_ref[...] = acc_ref[...].astype(o_ref.dtype)

    return pl.pallas_call(
        body, name=f"{tag}_dh", grid=(T // tm, 2 * nk),
        in_specs=[pl.BlockSpec((None, tm, tk), lambda i, k: (k // nk, i, k % nk)),
                  pl.BlockSpec((D_MODEL, tk), lambda i, k: (0, k))]
        + [pl.BlockSpec(t.shape, lambda i, k: (0, 0)) for t in ties],
        out_specs=pl.BlockSpec((tm, D_MODEL), lambda i, k: (i, 0)),
        out_shape=jax.ShapeDtypeStruct((T, D_MODEL), BF16),
        scratch_shapes=[pltpu.VMEM((tm, D_MODEL), F32)],
        compiler_params=_cparams(("parallel", "arbitrary")),
    )(dgu, w_gu, *ties)


def _ffn_fwd(h, w_gu, w_down, tag):
    gu, a, at = _ffn_gu_act(h, w_gu, tag)
    if callable(w_down):
        w_down = w_down(at)
    o = _mm(a, w_down, "nn", BF16, f"{tag}_down", tm=1024, tn=1024, tk=2816)
    return gu, at, o


def _ffn_bwd(do, ht, gu, at, w_gu, w_down, tag, tie=None, on_dw=None):
    on_dw = on_dw or (lambda which, dw: None)
    dw_down = _mm_t(at, do, f"{tag}_dwdown", tm=1408, tn=1024, tk=2048, tie=tie)
    dgu = _ffn_da_actbwd(do, w_down, gu, tag, tie=on_dw("down", dw_down))
    dw_gu = _ffn_dwgu(ht, dgu, tag)
    dh = _ffn_dh(dgu, w_gu, tag, tie=on_dw("gu", dw_gu))
    return dh, dw_gu, dw_down


def _sg_common(u_pre, v_pre, ln_g, ln_b):
    u = _gelu(u_pre)
    v = _gelu(v_pre)
    mu = jnp.mean(v, axis=-1, keepdims=True)
    vc = v - mu
    rstd = lax.rsqrt(jnp.mean(vc * vc, axis=-1, keepdims=True) + EPS)
    vhat = vc * rstd
    vl = vhat * ln_g + ln_b
    return u, vhat, rstd, vl


def _sg_masked_pairs(w):
    t = lax.broadcasted_iota(jnp.int32, (CHUNK, CHUNK), 0)
    s = lax.broadcasted_iota(jnp.int32, (CHUNK, CHUNK), 1)
    causal = s <= t
    wm = [jnp.where(causal, w[g], 0.0).astype(BF16) for g in range(SG_GROUPS)]
    return [jnp.concatenate([wm[2 * j], wm[2 * j + 1]], axis=0) for j in range(SG_GROUPS // 2)], causal


def _sg_mix(vl, pairs, bias):
    tr = vl.shape[0]
    low = lax.broadcasted_iota(jnp.int32, (CHUNK, LANES), 1) < SG_GROUP_DIM
    vb = vl.astype(BF16)
    rows = []
    for c in range(tr // CHUNK):
        slabs = []
        for j in range(SG_GROUPS // 2):
            slab = vb[c * CHUNK:(c + 1) * CHUNK, j * LANES:(j + 1) * LANES]
            m = _dot(pairs[j], slab)
            slabs.append(jnp.where(low, m[:CHUNK], m[CHUNK:]))
        rows.append(jnp.concatenate(slabs, axis=1) + bias)
    return jnp.concatenate(rows, axis=0)


def _sg_fwd(z, ln_g, ln_b, sg_w, bias_full):
    def fn(u_pre, v_pre, ln_g, ln_b, w, bias):
        u, _, _, vl = _sg_common(u_pre, v_pre, ln_g, ln_b)
        pairs, _ = _sg_masked_pairs(w)
        y = u * _sg_mix(vl, pairs, bias)
        return y, y

    return _rowwise(fn, "sg_fwd", 512, [(z, SG_WIDTH, Z_U // SG_WIDTH), (z, SG_WIDTH, Z_V // SG_WIDTH)],
                    [ln_g, ln_b, sg_w, bias_full], [(SG_WIDTH, BF16), (SG_WIDTH, BF16, "T")])


def _sg_bwd(z, dy, ln_g, ln_b, sg_w, bias_full, group_ind, dz):
    def fn(u_pre, v_pre, dy, ln_g, ln_b, w, bias, ind):
        dy = dy.astype(F32)
        u, vhat, rstd, vl = _sg_common(u_pre, v_pre, ln_g, ln_b)
        pairs, causal = _sg_masked_pairs(w)
        mixed = _sg_mix(vl, pairs, bias)
        du_pre = dy * mixed * _gelu_grad(u_pre)
        dmix = dy * u
        tr = dy.shape[0]
        low = lax.broadcasted_iota(jnp.int32, (CHUNK, LANES), 1) < SG_GROUP_DIM
        vb = vl.astype(BF16)
        dw = [jnp.zeros((CHUNK, CHUNK), F32) for _ in range(SG_GROUPS)]
        dbias = jnp.zeros((CHUNK, SG_WIDTH), F32)
        dvl_rows = []
        for c in range(tr // CHUNK):
            dm_c = dmix[c * CHUNK:(c + 1) * CHUNK]
            dbias = dbias + dm_c
            slabs = []
            for j in range(SG_GROUPS // 2):
                slab = vb[c * CHUNK:(c + 1) * CHUNK, j * LANES:(j + 1) * LANES]
                dm = dm_c[:, j * LANES:(j + 1) * LANES]
                d0 = jnp.where(low, dm, 0.0).astype(BF16)
                d1 = jnp.where(low, 0.0, dm).astype(BF16)
                dw[2 * j] = dw[2 * j] + _dot(d0, slab, "nt")
                dw[2 * j + 1] = dw[2 * j + 1] + _dot(d1, slab, "nt")
                slabs.append(_dot(pairs[j], jnp.concatenate([d0, d1], axis=0), "tn"))
            dvl_rows.append(jnp.concatenate(slabs, axis=1))
        dvl = jnp.concatenate(dvl_rows, axis=0)
        dln_g = _rsum(dvl * vhat)
        dln_b = _rsum(dvl)
        dvh = dvl * ln_g
        dv = rstd * (dvh - jnp.mean(dvh, axis=-1, keepdims=True)
                     - vhat * jnp.mean(dvh * vhat, axis=-1, keepdims=True))
        dv_pre = dv * _gelu_grad(v_pre)
        dw = jnp.stack([jnp.where(causal, d, 0.0) for d in dw], axis=0)
        dbias_t = lax.dot_general(dbias, ind, (((1,), (0,)), ((), ())), precision=lax.Precision.HIGHEST,
                                  preferred_element_type=F32)
        return jnp.concatenate([du_pre, dv_pre], axis=1), dw, dbias_t, dln_g, dln_b

    return _rowwise(fn, "sg_bwd", 512,
                    [(z, SG_WIDTH, Z_U // SG_WIDTH), (z, SG_WIDTH, Z_V // SG_WIDTH), dy],
                    [ln_g, ln_b, sg_w, bias_full, group_ind],
                    [("into", dz, 2 * SG_WIDTH, Z_U // (2 * SG_WIDTH))],
                    [((SG_GROUPS, CHUNK, CHUNK), F32), ((CHUNK, SG_GROUPS), F32), ((1, SG_WIDTH), F32), ((1, SG_WIDTH), F32)])


def _rope(x, c, s1, s2):
    return x * c + pltpu.roll(x, LANES - MLA_ROPE // 2, 1) * s1 + pltpu.roll(x, MLA_ROPE // 2, 1) * s2


def _rope_t(d, c, s1, s2):
    return d * c + pltpu.roll(d * s1, MLA_ROPE // 2, 1) + pltpu.roll(d * s2, LANES - MLA_ROPE // 2, 1)


def _mla_post(q_pre, kv_pre, z, tabs, gq, gk):
    scale = MLA_QK ** -0.5 * LOG2E
    T = q_pre.shape[0]
    tr = _tile(T, 256)

    def body(q_ref, k_ref, v_ref, kr_ref, c_ref, s1_ref, s2_ref, gq_ref, gk_ref, qo_ref, ko_ref, vo_ref):
        kr = kr_ref[...].astype(F32)
        c, s1, s2, gq, gk = c_ref[...], s1_ref[...], s2_ref[...], gq_ref[...], gk_ref[...]
        ones_lane = lax.broadcasted_iota(jnp.int32, (tr, LANES), 1) == ONES_LANE
        for h in range(MLA_HEADS):
            sl = slice(h * LANES, (h + 1) * LANES)
            qo_ref[:, sl] = (_rope(_rms(q_ref[:, sl].astype(F32), gq, MLA_QK), c, s1, s2) * scale).astype(BF16)
            ko_ref[:, sl] = _rope(_rms(k_ref[:, sl].astype(F32) + kr, gk, MLA_QK), c, s1, s2).astype(BF16)
            vo_ref[:, sl] = jnp.where(ones_lane, 1.0, v_ref[:, sl].astype(F32)).astype(BF16)

    wide = lambda cb: pl.BlockSpec((tr, HP), lambda i, cb=cb: (i, cb))
    lanes = lambda cb: pl.BlockSpec((tr, LANES), lambda i, cb=cb: (i, cb))
    gain = pl.BlockSpec((1, LANES), lambda i: (0, 0))
    return pl.pallas_call(
        body, name="mla_post", grid=(T // tr,),
        in_specs=[wide(0), wide(0), wide(1), lanes(Z_KR // LANES), lanes(0), lanes(0), lanes(0), gain, gain],
        out_specs=[wide(0)] * 3, out_shape=[jax.ShapeDtypeStruct((T, HP), BF16)] * 3,
        compiler_params=_cparams(("parallel",)),
    )(q_pre, kv_pre, kv_pre, z, *tabs, gq, gk)


def _mla_post_bwd(q_pre, kv_pre, z, tabs, gq, gk, dq, dk, dv):
    scale = MLA_QK ** -0.5
    T = q_pre.shape[0]
    tr = _tile(T, 256)

    def body(q_ref, k_ref, kr_ref, c_ref, s1_ref, s2_ref, dq_ref, dk_ref, dv_ref, gq_ref, gk_ref,
             dqo_ref, dkvo_ref, dkro_ref, dgq_ref, dgk_ref):
        kr = kr_ref[...].astype(F32)
        c, s1, s2, gq, gk = c_ref[...], s1_ref[...], s2_ref[...], gq_ref[...], gk_ref[...]
        lane = lax.broadcasted_iota(jnp.int32, (1, LANES), 1)
        kr_mask = (lane >= KR_LANE) & (lane < KR_LANE + MLA_ROPE)
        dgq = jnp.zeros((1, LANES), F32)
        dgk = jnp.zeros((1, LANES), F32)
        dkr = jnp.zeros((tr, LANES), F32)
        for h in range(MLA_HEADS):
            sl = slice(h * LANES, (h + 1) * LANES)
            dqn = _rope_t(dq_ref[:, sl].astype(F32), c, s1, s2) * scale
            dx, dg = _rms_bwd(q_ref[:, sl].astype(F32), gq, dqn, MLA_QK)
            dqo_ref[:, sl] = dx.astype(BF16)
            dgq = dgq + dg
            dkn = _rope_t(dk_ref[:, sl].astype(F32), c, s1, s2)
            dx, dg = _rms_bwd(k_ref[:, sl].astype(F32) + kr, gk, dkn, MLA_QK)
            dkvo_ref[:, sl] = dx.astype(BF16)
            dkvo_ref[:, HP + h * LANES:HP + (h + 1) * LANES] = dv_ref[:, sl]
            dgk = dgk + dg
            dkr = dkr + dx
        dkro_ref[...] = jnp.where(kr_mask, dkr, 0.0).astype(BF16)
        i = pl.program_id(0)

        @pl.when(i == 0)
        def _():
            dgq_ref[...] = dgq
            dgk_ref[...] = dgk

        @pl.when(i > 0)
        def _():
            dgq_ref[...] += dgq
            dgk_ref[...] += dgk

    wide = lambda cb: pl.BlockSpec((tr, HP), lambda i, cb=cb: (i, cb))
    lanes = lambda cb: pl.BlockSpec((tr, LANES), lambda i, cb=cb: (i, cb))
    gain = pl.BlockSpec((1, LANES), lambda i: (0, 0))
    return pl.pallas_call(
        body, name="mla_post_bwd", grid=(T // tr,),
        in_specs=[wide(0), wide(0), lanes(Z_KR // LANES), lanes(0), lanes(0), lanes(0), wide(0), wide(0), wide(0),
                  gain, gain],
        out_specs=[wide(0), pl.BlockSpec((tr, 2 * HP), lambda i: (i, 0)), lanes(0), gain, gain],
        out_shape=[jax.ShapeDtypeStruct((T, HP), BF16), jax.ShapeDtypeStruct((T, 2 * HP), BF16),
                   jax.ShapeDtypeStruct((T, LANES), BF16), jax.ShapeDtypeStruct((1, LANES), F32),
                   jax.ShapeDtypeStruct((1, LANES), F32)],
        compiler_params=_cparams(("arbitrary",)),
    )(q_pre, kv_pre, z, *tabs, dq, dk, dv, gq, gk)


def _pairs(n, lower):
    a, b = [], []
    for o in range(n):
        inner = range(o + 1) if lower else range(o, n)
        for t in inner:
            a.append(o)
            b.append(t)
    return jnp.asarray(np.array(a, np.int32)), jnp.asarray(np.array(b, np.int32))


FLASH_TILE, FLASH_SUB_ROWS = 2048, 512
LOG2E, LN2 = 1.4426950408889634, 0.6931471805599453
ONES_LANE = MLA_V


def _flash_tiles(T):
    tq = _tile(T, FLASH_TILE)
    return tq, _tile(tq, FLASH_SUB_ROWS)


def _col_span(t, sr, rb, diag, key_major):
    if not diag:
        return 0, t
    return (rb * sr, t) if key_major else (0, (rb + 1) * sr)


def _span_iota(sr, rb, c0, c1):
    r = lax.broadcasted_iota(jnp.int32, (sr, c1 - c0), 0) + rb * sr
    c = lax.broadcasted_iota(jnp.int32, (sr, c1 - c0), 1) + c0
    return r, c


def _lanes(x, width):
    return jnp.concatenate([x] * (width // LANES), axis=1)


def _flash_fwd(q, k, v):
    T = q.shape[0]
    tq, sr = _flash_tiles(T)
    n = T // tq
    ii, jj = _pairs(n, True)

    def body(ii_ref, jj_ref, q_ref, k_ref, v_ref, o_ref, ot_ref, lse_ref, lset_ref, m_sc, acc_sc):
        p_ = pl.program_id(1)
        i, j = ii_ref[p_], jj_ref[p_]

        @pl.when(j == 0)
        def _():
            m_sc[...] = jnp.full(m_sc.shape, NEG, F32)
            acc_sc[...] = jnp.zeros(acc_sc.shape, F32)

        def tile(diag):
            nrb = tq // sr

            def scores(rb):
                c0, c1 = _col_span(tq, sr, rb, diag, False)
                return _dot(q_ref[rb * sr:(rb + 1) * sr, :], k_ref[c0:c1, :], "nt")

            s_next = scores(0)
            for rb in range(nrb):
                rows = slice(rb * sr, (rb + 1) * sr)
                c0, c1 = _col_span(tq, sr, rb, diag, False)
                s, s_next = s_next, (scores(rb + 1) if rb + 1 < nrb else None)
                if diag:
                    r, c = _span_iota(sr, rb, c0, c1)
                    s = jnp.where(c <= r, s, NEG)
                m = m_sc[rows, :]
                m_new = jnp.maximum(m, jnp.max(s, axis=1, keepdims=True))
                p = jnp.exp2(s - _lanes(m_new, c1 - c0))
                acc_sc[rows, :] = jnp.exp2(m - m_new) * acc_sc[rows, :] + _dot(p, v_ref[c0:c1, :])
                m_sc[rows, :] = m_new

        @pl.when(j < i)
        def _():
            tile(False)

        @pl.when(j == i)
        def _():
            tile(True)
            acc = acc_sc[...]
            lane = lax.broadcasted_iota(jnp.int32, acc.shape, 1)
            l = jnp.sum(jnp.where(lane == ONES_LANE, acc, 0.0), axis=1, keepdims=True)
            o = jnp.where(lane < MLA_V, acc / l, 0.0)
            o_ref[...] = o.astype(o_ref.dtype)
            ot_ref[...] = o.T.astype(ot_ref.dtype)
            lse = m_sc[...] + jnp.log2(l)
            lse_ref[...] = lse
            lset_ref[...] = lse.T[:8]

    blk = lambda which: pl.BlockSpec((tq, LANES), which)
    qmap = lambda h, p, ii, jj: (ii[p], h)
    kmap = lambda h, p, ii, jj: (jj[p], h)
    tmap = lambda h, p, ii, jj: (h, ii[p])
    return pl.pallas_call(
        body, name="mla_flash_fwd",
        grid_spec=pltpu.PrefetchScalarGridSpec(
            num_scalar_prefetch=2, grid=(MLA_HEADS, int(ii.shape[0])),
            in_specs=[blk(qmap), blk(kmap), blk(kmap)],
            out_specs=[blk(qmap), pl.BlockSpec((LANES, tq), tmap), blk(qmap), pl.BlockSpec((8, tq), tmap)],
            scratch_shapes=[pltpu.VMEM((tq, LANES), F32)] * 2),
        out_shape=[jax.ShapeDtypeStruct((T, HP), BF16), jax.ShapeDtypeStruct((HP, T), BF16),
                   jax.ShapeDtypeStruct((T, HP), F32), jax.ShapeDtypeStruct((8 * MLA_HEADS, T), F32)],
        compiler_params=_cparams(("parallel", "arbitrary")),
    )(ii, jj, q, k, v)


def _flash_dq(q, k, v, do, lse, delta):
    T = q.shape[0]
    tq, sr = _flash_tiles(T)
    n = T // tq
    ii, jj = _pairs(n, True)

    def body(ii_ref, jj_ref, q_ref, k_ref, v_ref, do_ref, lse_ref, dl_ref, dq_ref, acc_sc):
        p_ = pl.program_id(1)
        i, j = ii_ref[p_], jj_ref[p_]

        @pl.when(j == 0)
        def _():
            acc_sc[...] = jnp.zeros(acc_sc.shape, F32)

        def tile(diag):
            nrb = tq // sr

            def products(rb):
                rows = slice(rb * sr, (rb + 1) * sr)
                c0, c1 = _col_span(tq, sr, rb, diag, False)
                return _dot(q_ref[rows, :], k_ref[c0:c1, :], "nt"), _dot(do_ref[rows, :], v_ref[c0:c1, :], "nt")

            nxt = products(0)
            for rb in range(nrb):
                rows = slice(rb * sr, (rb + 1) * sr)
                c0, c1 = _col_span(tq, sr, rb, diag, False)
                (s, dp), nxt = nxt, (products(rb + 1) if rb + 1 < nrb else None)
                p = jnp.exp2(s - _lanes(lse_ref[rows, :], c1 - c0))
                if diag:
                    r, c = _span_iota(sr, rb, c0, c1)
                    p = jnp.where(c <= r, p, 0.0)
                acc_sc[rows, :] += _dot(p * (dp - _lanes(dl_ref[rows, :], c1 - c0)), k_ref[c0:c1, :])

        @pl.when(j < i)
        def _():
            tile(False)

        @pl.when(j == i)
        def _():
            tile(True)
            dq_ref[...] = acc_sc[...].astype(dq_ref.dtype)

    blk = lambda which: pl.BlockSpec((tq, LANES), which)
    qmap = lambda h, p, ii, jj: (ii[p], h)
    kmap = lambda h, p, ii, jj: (jj[p], h)
    return pl.pallas_call(
        body, name="mla_flash_dq",
        grid_spec=pltpu.PrefetchScalarGridSpec(
            num_scalar_prefetch=2, grid=(MLA_HEADS, int(ii.shape[0])),
            in_specs=[blk(qmap), blk(kmap), blk(kmap), blk(qmap), blk(qmap), blk(qmap)],
            out_specs=blk(qmap),
            scratch_shapes=[pltpu.VMEM((tq, LANES), F32)]),
        out_shape=jax.ShapeDtypeStruct((T, HP), BF16),
        compiler_params=_cparams(("parallel", "arbitrary")),
    )(ii, jj, q, k, v, do, lse, delta)


def _flash_dkv(q, k, v, do, lse_t, delta_t):
    T = q.shape[0]
    tq, sr = _flash_tiles(T)
    n = T // tq
    jj, ii = _pairs(n, False)

    def body(jj_ref, ii_ref, q_ref, k_ref, v_ref, do_ref, lse_ref, dl_ref, dk_ref, dv_ref, dk_sc, dv_sc):
        p_ = pl.program_id(1)
        j, i = jj_ref[p_], ii_ref[p_]

        @pl.when(i == j)
        def _():
            dk_sc[...] = jnp.zeros(dk_sc.shape, F32)
            dv_sc[...] = jnp.zeros(dv_sc.shape, F32)

        def tile(diag):
            nrb = tq // sr

            def products(rb):
                rows = slice(rb * sr, (rb + 1) * sr)
                c0, c1 = _col_span(tq, sr, rb, diag, True)
                return _dot(k_ref[rows, :], q_ref[c0:c1, :], "nt"), _dot(v_ref[rows, :], do_ref[c0:c1, :], "nt")

            nxt = products(0)
            for rb in range(nrb):
                rows = slice(rb * sr, (rb + 1) * sr)
                c0, c1 = _col_span(tq, sr, rb, diag, True)
                (st, dpt), nxt = nxt, (products(rb + 1) if rb + 1 < nrb else None)
                pt = jnp.exp2(st - lse_ref[:1, c0:c1])
                if diag:
                    r, c = _span_iota(sr, rb, c0, c1)
                    pt = jnp.where(r <= c, pt, 0.0)
                dv_sc[rows, :] += _dot(pt, do_ref[c0:c1, :])
                dk_sc[rows, :] += _dot(pt * (dpt - dl_ref[:1, c0:c1]), q_ref[c0:c1, :])

        @pl.when(i == j)
        def _():
            tile(True)

        @pl.when(i > j)
        def _():
            tile(False)

        @pl.when(i == n - 1)
        def _():
            dk_ref[...] = (dk_sc[...] * LN2).astype(dk_ref.dtype)
            dv_ref[...] = dv_sc[...].astype(dv_ref.dtype)

    blk = lambda which: pl.BlockSpec((tq, LANES), which)
    qmap = lambda h, p, jj, ii: (ii[p], h)
    kmap = lambda h, p, jj, ii: (jj[p], h)
    lse_rows = pl.BlockSpec((8, tq), lambda h, p, jj, ii: (h, ii[p]))
    delta_rows = pl.BlockSpec((8, tq), lambda h, p, jj, ii: (h * (LANES // 8), ii[p]))
    return pl.pallas_call(
        body, name="mla_flash_dkv",
        grid_spec=pltpu.PrefetchScalarGridSpec(
            num_scalar_prefetch=2, grid=(MLA_HEADS, int(ii.shape[0])),
            in_specs=[blk(qmap), blk(kmap), blk(kmap), blk(qmap), lse_rows, delta_rows],
            out_specs=[blk(kmap), blk(kmap)],
            scratch_shapes=[pltpu.VMEM((tq, LANES), F32)] * 2),
        out_shape=[jax.ShapeDtypeStruct((T, HP), BF16)] * 2,
        compiler_params=_cparams(("parallel", "arbitrary")),
    )(jj, ii, q, k, v, do, lse_t, delta_t)


def _mem_fwd(z, km, vm, gq):
    scale = MEM_HEAD_DIM ** -0.5

    def fn(qm, km, vm, gq):
        ys = []
        for h in range(MEM_HEADS):
            sl = slice(h * LANES, (h + 1) * LANES)
            q = _rms(qm[:, sl], gq) * scale
            s = _dot(q, km[:, sl], "nt")
            p = jnp.exp(s - jnp.max(s, axis=1, keepdims=True))
            p = p / jnp.sum(p, axis=1, keepdims=True)
            ys.append(_dot(p, vm[:, sl]))
        y = jnp.concatenate(ys, axis=1)
        return y, y

    return _rowwise(fn, "mem_fwd", 512, [(z, MEM_WIDTH, Z_QM // MEM_WIDTH)], [km, vm, gq],
                    [(MEM_WIDTH, BF16), (MEM_WIDTH, BF16, "T")])


def _mem_bwd(z, dy, km, vm, gq, dz):
    scale = MEM_HEAD_DIM ** -0.5

    def fn(qm, dy, km, vm, gq):
        dqs, dks, dvs = [], [], []
        dgq = jnp.zeros((1, LANES), F32)
        for h in range(MEM_HEADS):
            sl = slice(h * LANES, (h + 1) * LANES)
            q = (_rms(qm[:, sl], gq) * scale).astype(BF16)
            dyh = dy[:, sl]
            kh, vh = km[:, sl], vm[:, sl]
            s = _dot(q, kh, "nt")
            p = jnp.exp(s - jnp.max(s, axis=1, keepdims=True))
            p = p / jnp.sum(p, axis=1, keepdims=True)
            dp = _dot(dyh, vh, "nt")
            ds = p * (dp - jnp.sum(p * dp, axis=1, keepdims=True))
            dq = _dot(ds, kh) * scale
            dx, dg = _rms_bwd(qm[:, sl], gq, dq)
            dqs.append(dx)
            dgq = dgq + dg
            st = _dot(kh, q, "nt")
            pt = jnp.exp(st - jnp.max(st, axis=0, keepdims=True))
            pt = pt / jnp.sum(pt, axis=0, keepdims=True)
            dpt = _dot(vh, dyh, "nt")
            dst = pt * (dpt - jnp.sum(pt * dpt, axis=0, keepdims=True))
            dvs.append(_dot(pt, dyh))
            dks.append(_dot(dst, q))
        return jnp.concatenate(dqs, axis=1), jnp.concatenate(dks, axis=1), jnp.concatenate(dvs, axis=1), dgq

    m = km.shape[0]
    return _rowwise(fn, "mem_bwd", 512, [(z, MEM_WIDTH, Z_QM // MEM_WIDTH), dy], [km, vm, gq],
                    [("into", dz, MEM_WIDTH, Z_QM // MEM_WIDTH)],
                    [((m, MEM_WIDTH), F32), ((m, MEM_WIDTH), F32), ((1, LANES), F32)])


GROUPS = {"ffn1": ["ffn1_w_gu"], "ffn1_down": ["ffn1_w_down"],
          "mix": ["w_in", "mla_w_uq", "mla_w_ukv", "mem_w_kv", "w_branch_a", "w_branch_b", "w_branch_c", "w_out"],
          "ffn2": ["ffn2_w_gu", "ffn2_w_down"]}
GRAD_GROUPS = {"ffn2": GROUPS["ffn2"], "mix": GROUPS["mix"], "ffn1_down": ["ffn1_w_down"], "ffn1_gu": ["ffn1_w_gu"]}


def _local_step(x, mem, positions, loss_target, P, weights, grads_out):
    T = x.shape[0]
    G = {}
    W = dict(weights("ffn1", None))

    half = MLA_ROPE // 2
    inv = ROPE_BASE ** (-jnp.arange(half, dtype=F32) / half)
    ang = positions.astype(F32)[:, None] * inv
    cos, sin = jnp.cos(ang), jnp.sin(ang)
    one, zero = jnp.ones((T, MLA_NOPE), F32), jnp.zeros((T, half), F32)
    pad = LANES - MLA_QK
    tabs = (jnp.concatenate([one, cos, cos, jnp.ones((T, pad), F32)], axis=1),
            jnp.concatenate([jnp.zeros((T, MLA_NOPE), F32), -sin, zero, jnp.zeros((T, pad), F32)], axis=1),
            jnp.concatenate([jnp.zeros((T, MLA_NOPE), F32), zero, sin, jnp.zeros((T, pad), F32)], axis=1))
    gq_p = jnp.pad(P["mla_q_norm"], ((0, 0), (0, pad)))
    gk_p = jnp.pad(P["mla_k_norm"], ((0, 0), (0, pad)))
    bias_full = jnp.repeat(P["sg_b"].T, SG_GROUP_DIM, axis=1)
    group_ind = jnp.repeat(jnp.eye(SG_GROUPS, dtype=F32), SG_GROUP_DIM, axis=0)

    HT = (D_MODEL, BF16, "T")

    def norm2(x, g):
        h = _rms(x, g)
        return h, h

    h1, h1t = _rowwise(norm2, "ffn1_norm", 512, [x], [P["ffn1_norm"]], [(D_MODEL, BF16), HT])
    def ffn1_w_down(after):
        W.update(weights("ffn1_down", after))
        return W["ffn1_w_down"]

    gu1, a1t, o1 = _ffn_fwd(h1, W["ffn1_w_gu"], ffn1_w_down, "ffn1")

    def resid_norm(x, o, g):
        xn = x + 0.5 * o
        h = _rms(xn, g)
        return xn, h, h

    x1, hm, hmt = _rowwise(resid_norm, "mix_norm", 512, [x, o1], [P["mix_norm"]],
                           [(D_MODEL, F32), (D_MODEL, BF16), HT])
    W.update(weights("mix", hm))
    z = _mm(hm, W["w_in"], "nn", BF16, "w_in", tm=1024, tn=1792)

    y_a, y_at = _sg_fwd(z, P["sg_ln_g"], P["sg_ln_b"], P["sg_w"], bias_full)

    def c_norm(cq, ckv, gq, gkv):
        a, b = _rms(cq, gq), _rms(ckv, gkv)
        return a, b, a, b

    cqn, ckvn, cqnt, ckvnt = _rowwise(
        c_norm, "mla_cnorm", 512, [(z, MLA_Q_RANK, Z_CQ // MLA_Q_RANK), (z, MLA_KV_RANK, Z_CKV // MLA_KV_RANK)],
        [P["mla_cq_norm"], P["mla_ckv_norm"]],
        [(MLA_Q_RANK, BF16), (MLA_KV_RANK, BF16), (MLA_Q_RANK, BF16, "T"), (MLA_KV_RANK, BF16, "T")])
    q_pre = _mm(cqn, W["mla_w_uq"], "nn", BF16, "mla_uq", tm=1024, tn=1024)
    kv_pre = _mm(ckvn, W["mla_w_ukv"], "nn", BF16, "mla_ukv", tm=1024, tn=1024)
    q, k, v = _mla_post(q_pre, kv_pre, z, tabs, gq_p, gk_p)
    y_b, y_bt, lse, lse_t = _flash_fwd(q, k, v)

    memn, = _rowwise(lambda m, g: _rms(m, g), "mem_norm", 256, [mem], [P["mem_norm"]], [(D_MODEL, BF16)])
    kvm = _mm(memn, W["mem_w_kv"], "nn", F32, "mem_kv")

    def mem_k(kvm, gk):
        ks = [_rms(kvm[:, h * LANES:(h + 1) * LANES], gk) for h in range(MEM_HEADS)]
        return jnp.concatenate(ks, axis=1), kvm[:, MEM_WIDTH:]

    km, vm = _rowwise(mem_k, "mem_knorm", 256, [kvm], [P["mem_k_norm"]], [(MEM_WIDTH, BF16), (MEM_WIDTH, BF16)])
    y_c, y_ct = _mem_fwd(z, km, vm, P["mem_q_norm"])

    pa = _mm(y_a, W["w_branch_a"], "nn", BF16, "branch_a", tm=1024, tn=1024)
    pb = _mm(y_b, W["w_branch_b"], "nn", BF16, "branch_b", tm=1024, tn=1024)
    pc = _mm(y_c, W["w_branch_c"], "nn", BF16, "branch_c", tm=1024, tn=1024)

    def merge(zg, pa, pb, pc, b):
        g = _sigmoid(zg + b)
        m = g[:, :D_MODEL] * pa + g[:, D_MODEL:2 * D_MODEL] * pb + g[:, 2 * D_MODEL:] * pc
        return m, m

    merged, mergedt = _rowwise(merge, "merge", 256, [(z, 3 * D_MODEL, 0), pa, pb, pc], [P["b_gate"]],
                               [(D_MODEL, BF16), HT])
    om = _mm(merged, W["w_out"], "nn", BF16, "w_out", tm=1024, tn=1024)

    def resid_norm1(x, o, g):
        xn = x + o
        h = _rms(xn, g)
        return xn, h, h

    x2, h2, h2t = _rowwise(resid_norm1, "ffn2_norm", 512, [x1, om], [P["ffn2_norm"]],
                           [(D_MODEL, F32), (D_MODEL, BF16), HT])
    W.update(weights("ffn2", h2))
    gu2, a2t, o2 = _ffn_fwd(h2, W["ffn2_w_gu"], W["ffn2_w_down"], "ffn2")

    def loss_fn(x2, o2, t):
        e = x2 + 0.5 * o2 - t
        return e * (1.0 / D_MODEL), (e * (0.5 / D_MODEL)).astype(BF16), _rsum(e * e) * (0.5 / D_MODEL)

    dx3, do2, loss_part = _rowwise(loss_fn, "loss", 512, [x2, o2, loss_target], [],
                                   [(D_MODEL, F32), (D_MODEL, BF16)], [((1, D_MODEL), F32)])

    dh2, G["ffn2_w_gu"], G["ffn2_w_down"] = _ffn_bwd(do2, h2t, gu2, a2t, W["ffn2_w_gu"], W["ffn2_w_down"], "ffn2")
    tie = grads_out("ffn2", G)

    def norm_bwd(x, dh, dxo, g, *_):
        dx, dg = _rms_bwd(x, g, dh)
        dx = dx + dxo
        return dx, dx, dg

    dx2, dx2b, G["ffn2_norm"] = _rowwise(norm_bwd, "ffn2_norm_bwd", 512, [x2, dh2, dx3],
                                         [P["ffn2_norm"]] + ([] if tie is None else [tie]),
                                         [(D_MODEL, F32), (D_MODEL, BF16)], [((1, D_MODEL), F32)])

    G["w_out"] = _mm_t(mergedt, dx2b, "w_out_dw", tm=1024, tn=1024)
    dmerged = _mm(dx2b, W["w_out"], "nt", BF16, "w_out_dx", tm=1024, tn=1024)

    def merge_bwd(zg, pa, pb, pc, dm, b):
        g = _sigmoid(zg + b)
        ps = jnp.concatenate([pa, pb, pc], axis=1)
        dm3 = jnp.concatenate([dm, dm, dm], axis=1)
        dzg = dm3 * ps * g * (1.0 - g)
        dp = dm3 * g
        return dzg, dp[:, :D_MODEL], dp[:, D_MODEL:2 * D_MODEL], dp[:, 2 * D_MODEL:], _rsum(dzg)

    dz = lax.empty((T, Z_COLS), BF16)
    dz, dpa, dpb, dpc, G["b_gate"] = _rowwise(
        merge_bwd, "merge_bwd", 256, [(z, 3 * D_MODEL, 0), pa, pb, pc, dmerged], [P["b_gate"]],
        [("into", dz, 3 * D_MODEL, 0), (D_MODEL, BF16), (D_MODEL, BF16), (D_MODEL, BF16)], [((1, 3 * D_MODEL), F32)])

    G["w_branch_a"] = _mm_t(y_at, dpa, "branch_a_dw", tm=512, tn=1024)
    G["w_branch_b"] = _mm_t(y_bt, dpb, "branch_b_dw", tm=1024, tn=1024)
    G["w_branch_c"] = _mm_t(y_ct, dpc, "branch_c_dw", tm=512, tn=1024)
    dy_a = _mm(dpa, W["w_branch_a"], "nt", BF16, "branch_a_dx", tm=1024, tn=512)
    dy_b = _mm(dpb, W["w_branch_b"], "nt", BF16, "branch_b_dx", tm=1024, tn=1024)
    dy_c = _mm(dpc, W["w_branch_c"], "nt", BF16, "branch_c_dx", tm=1024, tn=512)

    dz, G["sg_w"], dbias_t, G["sg_ln_g"], G["sg_ln_b"] = _sg_bwd(
        z, dy_a, P["sg_ln_g"], P["sg_ln_b"], P["sg_w"], bias_full, group_ind, dz)
    G["sg_b"] = dbias_t.T

    dz, dkm, dvm, G["mem_q_norm"] = _mem_bwd(z, dy_c, km, vm, P["mem_q_norm"], dz)

    def mem_k_bwd(kvm, dkm, dvm, gk):
        dks = []
        dg = jnp.zeros((1, LANES), F32)
        for h in range(MEM_HEADS):
            sl = slice(h * LANES, (h + 1) * LANES)
            dx, d = _rms_bwd(kvm[:, sl], gk, dkm[:, sl])
            dks.append(dx)
            dg = dg + d
        return jnp.concatenate(dks + [dvm], axis=1), dg

    dkvm, G["mem_k_norm"] = _rowwise(mem_k_bwd, "mem_knorm_bwd", 256, [kvm, dkm, dvm], [P["mem_k_norm"]],
                                     [(2 * MEM_WIDTH, BF16)], [((1, LANES), F32)])
    G["mem_w_kv"] = _mm(memn, dkvm, "tn", BF16, "mem_kv_dw")
    dmemn = _mm(dkvm, W["mem_w_kv"], "nt", F32, "mem_kv_dx")
    _, G["mem_norm"] = _rowwise(lambda m, d, g: _rms_bwd(m, g, d), "mem_norm_bwd", 256, [mem, dmemn],
                                [P["mem_norm"]], [(D_MODEL, BF16)], [((1, D_MODEL), F32)])

    def delta_fn(o, do):
        od = o.astype(F32) * do.astype(F32)
        ds = [jnp.broadcast_to(jnp.sum(od[:, h * LANES:(h + 1) * LANES], axis=1, keepdims=True), (od.shape[0], LANES))
              for h in range(MLA_HEADS)]
        d = jnp.concatenate(ds, axis=1)
        return d, d

    delta, delta_t = _rowwise(delta_fn, "mla_delta", 512, [y_b, dy_b], [], [(HP, F32), (HP, F32, "T")])
    dq = _flash_dq(q, k, v, dy_b, lse, delta)
    dk, dv = _flash_dkv(q, k, v, dy_b, lse_t, delta_t)
    dq_pre, dkv_pre, dkr, dgq, dgk = _mla_post_bwd(q_pre, kv_pre, z, tabs, gq_p, gk_p, dq, dk, dv)
    G["mla_q_norm"], G["mla_k_norm"] = dgq[:, :MLA_QK], dgk[:, :MLA_QK]
    G["mla_w_uq"] = _mm_t(cqnt, dq_pre, "mla_uq_dw", tm=384, tn=1024)
    G["mla_w_ukv"] = _mm_t(ckvnt, dkv_pre, "mla_ukv_dw", tm=256, tn=2048)
    dcqn = _mm(dq_pre, W["mla_w_uq"], "nt", BF16, "mla_uq_dx", tm=1024)
    dckvn = _mm(dkv_pre, W["mla_w_ukv"], "nt", BF16, "mla_ukv_dx", tm=1024)

    def c_norm_bwd(cq, ckv, dcqn, dckvn, dkr, gq, gkv):
        dcq, dgq = _rms_bwd(cq, gq, dcqn)
        dckv, dgkv = _rms_bwd(ckv, gkv, dckvn)
        return jnp.concatenate([dckv, dkr, dcq], axis=1), dgq, dgkv

    tail = Z_COLS - Z_CKV
    dz, G["mla_cq_norm"], G["mla_ckv_norm"] = _rowwise(
        c_norm_bwd, "mla_cnorm_bwd", 512,
        [(z, MLA_Q_RANK, Z_CQ // MLA_Q_RANK), (z, MLA_KV_RANK, Z_CKV // MLA_KV_RANK), dcqn, dckvn, dkr],
        [P["mla_cq_norm"], P["mla_ckv_norm"]], [("into", dz, tail, Z_CKV // tail)],
        [((1, MLA_Q_RANK), F32), ((1, MLA_KV_RANK), F32)])
    G["w_in"] = _mm_t(hmt, dz, "w_in_dw", tm=1024, tn=1792, tk=2048)
    dhm = _mm(dz, W["w_in"], "nt", BF16, "w_in_dx", tm=1024, tn=1024, tk=2688)

    def norm_bwd_half(x, dh, dxo, g):
        dx, dg = _rms_bwd(x, g, dh)
        dx = dx + dxo
        return dx, (0.5 * dx), dg

    dx1, do1, G["mix_norm"] = _rowwise(norm_bwd_half, "mix_norm_bwd", 512, [x1, dhm, dx2], [P["mix_norm"]],
                                       [(D_MODEL, F32), (D_MODEL, BF16)], [((1, D_MODEL), F32)])
    tie = grads_out("mix", G)

    def ffn1_dw(which, dw):
        G["ffn1_w_" + which] = dw
        return grads_out("ffn1_" + which, G)

    dh1, _, _ = _ffn_bwd(do1, h1t, gu1, a1t, W["ffn1_w_gu"], W["ffn1_w_down"], "ffn1", tie, ffn1_dw)

    def norm_bwd_last(x, dh, dxo, g):
        dx, dg = _rms_bwd(x, g, dh)
        return dx + dxo, dg

    grad_x, G["ffn1_norm"] = _rowwise(norm_bwd_last, "ffn1_norm_bwd", 512, [x, dh1, dx1], [P["ffn1_norm"]],
                                      [(D_MODEL, F32)], [((1, D_MODEL), F32)])
    return loss_part, grad_x, G


SHARDED = ["ffn1_w_gu", "ffn1_w_down", "w_in", "mla_w_uq", "mla_w_ukv", "mem_w_kv",
           "w_branch_a", "w_branch_b", "w_branch_c", "w_out", "ffn2_w_gu", "ffn2_w_down"]
ROW_SHARDED = {"ffn1_w_down", "mem_w_kv", "w_out", "ffn2_w_down"}
SMALL = ["ffn1_norm", "mix_norm", "b_gate", "sg_ln_g", "sg_ln_b", "sg_w", "sg_b", "mla_cq_norm",
         "mla_ckv_norm", "mla_q_norm", "mla_k_norm", "mem_norm", "mem_q_norm", "mem_k_norm", "ffn2_norm"]
ORDER = ["ffn1_norm", "ffn1_w_gu", "ffn1_w_down", "mix_norm", "w_in", "b_gate", "sg_ln_g", "sg_ln_b", "sg_w",
         "sg_b", "mla_cq_norm", "mla_w_uq", "mla_ckv_norm", "mla_w_ukv", "mla_q_norm", "mla_k_norm", "mem_norm",
         "mem_w_kv", "mem_q_norm", "mem_k_norm", "w_branch_a", "w_branch_b", "w_branch_c", "w_out", "ffn2_norm",
         "ffn2_w_gu", "ffn2_w_down"]

_IN_U, _IN_V, _IN_CQ, _IN_CKV, _IN_KR, _IN_QM, _IN_G = 0, 512, 1024, 1408, 1664, 1696, 2208
IN_COLS = 5280


def _full_from_slabs(name, slabs):
    n, r, c = slabs.shape
    if name in ROW_SHARDED:
        return slabs.reshape(n * r, c)
    return slabs.transpose(1, 0, 2).reshape(r, n * c)


def _slabs_from_full(name, full):
    if name in ROW_SHARDED:
        return full.reshape(N_DEV, full.shape[0] // N_DEV, full.shape[1])
    r, c = full.shape
    return full.reshape(r, N_DEV, c // N_DEV).transpose(1, 0, 2)


def _compute_layout(full):
    W = dict(full)
    if "w_in" not in full:
        return W
    w = full["w_in"]
    kr = jnp.pad(w[:, _IN_KR:_IN_QM], ((0, 0), (KR_LANE, LANES - KR_LANE - MLA_ROPE)))
    W["w_in"] = jnp.concatenate([w[:, _IN_G:], w[:, _IN_U:_IN_CQ], w[:, _IN_QM:_IN_G], w[:, _IN_CKV:_IN_KR], kr,
                                 w[:, _IN_CQ:_IN_CKV]], axis=1)
    uq = full["mla_w_uq"].reshape(MLA_Q_RANK, MLA_HEADS, MLA_QK)
    W["mla_w_uq"] = jnp.pad(uq, ((0, 0), (0, 0), (0, LANES - MLA_QK))).reshape(MLA_Q_RANK, HP)
    ukv = full["mla_w_ukv"].reshape(MLA_KV_RANK, MLA_HEADS, MLA_NOPE + MLA_V)
    padh = lambda a: jnp.pad(a, ((0, 0), (0, 0), (0, LANES - a.shape[2]))).reshape(MLA_KV_RANK, HP)
    W["mla_w_ukv"] = jnp.concatenate([padh(ukv[:, :, :MLA_NOPE]), padh(ukv[:, :, MLA_NOPE:])], axis=1)
    wb = full["w_branch_b"].reshape(MLA_HEADS, MLA_V, D_MODEL)
    W["w_branch_b"] = jnp.pad(wb, ((0, 0), (0, LANES - MLA_V), (0, 0))).reshape(HP, D_MODEL)
    return W


def _reference_layout(G):
    out = dict(G)
    if "w_in" not in G:
        return out
    g = G["w_in"]
    out["w_in"] = jnp.concatenate([
        g[:, Z_U:Z_QM], g[:, Z_CQ:Z_COLS], g[:, Z_CKV:Z_KR], g[:, Z_KR + KR_LANE:Z_KR + KR_LANE + MLA_ROPE],
        g[:, Z_QM:Z_CKV], g[:, Z_G:Z_U]], axis=1)
    out["mla_w_uq"] = G["mla_w_uq"].reshape(MLA_Q_RANK, MLA_HEADS, LANES)[:, :, :MLA_QK].reshape(MLA_Q_RANK, -1)
    gk = G["mla_w_ukv"][:, :HP].reshape(MLA_KV_RANK, MLA_HEADS, LANES)[:, :, :MLA_NOPE]
    gv = G["mla_w_ukv"][:, HP:].reshape(MLA_KV_RANK, MLA_HEADS, LANES)[:, :, :MLA_V]
    out["mla_w_ukv"] = jnp.concatenate([gk, gv], axis=2).reshape(MLA_KV_RANK, -1)
    out["w_branch_b"] = G["w_branch_b"].reshape(MLA_HEADS, LANES, D_MODEL)[:, :MLA_V].reshape(-1, D_MODEL)
    return out


def _pack(parts):
    flat = []
    for a in parts:
        a = a.reshape(-1)
        flat.append(jnp.pad(a, (0, (-a.shape[0]) % LANES)))
    return jnp.concatenate(flat).reshape(-1, LANES)


def _unpack(packed, shapes):
    flat = packed.reshape(-1)
    out, off = [], 0
    for shp in shapes:
        n = int(np.prod(shp))
        out.append(flat[off:off + n].reshape(shp))
        off += n + (-n) % LANES
    return out


MESH = pl.DeviceIdType.MESH
HBM = pl.BlockSpec(memory_space=pltpu.HBM)


def _all_gather(shards):
    n = len(shards)

    def body(*refs):
        x_refs, out_refs, token_ref = refs[:n], refs[n:2 * n], refs[2 * n]
        send_sems, recv_sems, local_sems = refs[2 * n + 1:]
        x, y, c = lax.axis_index("x"), lax.axis_index("y"), lax.axis_index("c")
        me, sibling = (x, y, c), (x, y, 1 - c)
        chips = [(1 - x, y), (x, 1 - y), (1 - x, 1 - y)]
        token_ref[...] = jnp.zeros_like(token_ref)

        def slot(a, px, py, pc):
            return out_refs[a].at[4 * px + 2 * py + pc]

        def copy(a, k, block, to, src=None):
            return pltpu.make_async_remote_copy(
                src_ref=slot(a, *block) if src is None else src, dst_ref=slot(a, *block),
                send_sem=send_sems.at[7 * a + k], recv_sem=recv_sems.at[7 * a + k], device_id=to, device_id_type=MESH)

        arrays = range(n)
        mine = [pltpu.make_async_copy(x_refs[a], slot(a, *me), local_sems.at[a]) for a in arrays]
        for cp in mine:
            cp.start()
        first = [copy(a, 0, me, sibling, src=x_refs[a]) for a in arrays]
        first += [copy(a, 1 + j, me, (*chip, c), src=x_refs[a]) for j, chip in enumerate(chips) for a in arrays]
        for cp in first:
            cp.start()
        passed = []
        for j, chip in enumerate(chips):
            for a in arrays:
                copy(a, 1 + j, (*chip, c), me).wait_recv()
                passed.append(copy(a, 4 + j, (*chip, c), sibling))
                passed[-1].start()
        for a in arrays:
            copy(a, 0, sibling, me).wait_recv()
        for j, chip in enumerate(chips):
            for a in arrays:
                copy(a, 4 + j, (*chip, 1 - c), me).wait_recv()
        for cp in first + passed:
            cp.wait_send()
        for cp in mine:
            cp.wait()

    res = pl.pallas_call(
        body, name="all_gather_weights",
        out_shape=[jax.ShapeDtypeStruct((N_DEV,) + s.shape, s.dtype) for s in shards]
        + [jax.ShapeDtypeStruct((8, LANES), F32)],
        in_specs=[HBM] * n, out_specs=[HBM] * n + [pl.BlockSpec(memory_space=pltpu.VMEM)],
        scratch_shapes=[pltpu.SemaphoreType.DMA((7 * n,)), pltpu.SemaphoreType.DMA((7 * n,)),
                        pltpu.SemaphoreType.DMA((n,))],
    )(*shards)
    return res[:n], res[n]


SEM = pl.BlockSpec(memory_space=pltpu.SEMAPHORE)
DATAFLOW = pltpu.SideEffectType.DATAFLOW_SIDE_EFFECTING


def _peers():
    x, y, c = lax.axis_index("x"), lax.axis_index("y"), lax.axis_index("c")
    out = []
    for k in range(1, N_DEV):
        px = 1 - x if k & 4 else x
        py = 1 - y if k & 2 else y
        pc = 1 - c if k & 1 else c
        out.append((k, (px, py, pc), 4 * px + 2 * py + pc))
    return 4 * x + 2 * y + c, out


def _send_start(srcs, per_peer, name):
    n = len(srcs)
    lands = [lax.empty((N_DEV,) + (s.shape[1:] if per_peer else s.shape), s.dtype) for s in srcs]

    def body(*refs):
        src_refs, land_refs, send_sems, recv_sems, token = refs[:n], refs[n:2 * n], refs[2 * n], refs[2 * n + 1], refs[-1]
        me, peers = _peers()
        for a in range(n):
            for k, pid, pflat in peers:
                pltpu.make_async_remote_copy(
                    src_ref=src_refs[a].at[pflat] if per_peer else src_refs[a], dst_ref=land_refs[a].at[me],
                    send_sem=send_sems.at[7 * a + k - 1], recv_sem=recv_sems.at[7 * a + k - 1],
                    device_id=pid, device_id_type=MESH).start()
        token[...] = jnp.zeros_like(token)

    hbm = lambda a: pltpu.with_memory_space_constraint(a, pltpu.HBM)
    res = pl.pallas_call(
        body, name=name,
        out_shape=(pltpu.SemaphoreType.DMA((7 * n,)), pltpu.SemaphoreType.DMA((7 * n,)),
                   *[pltpu.HBM(a.shape, a.dtype) for a in srcs + lands], jax.ShapeDtypeStruct((8, LANES), F32)),
        in_specs=(HBM,) * (2 * n), out_specs=(SEM, SEM) + (HBM,) * (2 * n) + (pl.BlockSpec(memory_space=pltpu.VMEM),),
        input_output_aliases={i: 2 + i for i in range(2 * n)},
        compiler_params=pltpu.CompilerParams(has_side_effects=DATAFLOW),
    )(*[hbm(a) for a in srcs + lands])
    return (res[0], res[1], list(res[2:2 + n]), list(res[2 + n:2 + 2 * n])), res[-1]


def _send_wait(started, after, per_peer, name):
    send_sems, recv_sems, srcs_thru, lands_thru = started
    n = len(srcs_thru)

    def body(*refs):
        src_refs, land_refs, send_sems, recv_sems = refs[:n], refs[n:2 * n], refs[2 * n], refs[2 * n + 1]
        me, peers = _peers()
        for a in range(n):
            for k, pid, pflat in peers:
                copy = pltpu.make_async_remote_copy(
                    src_ref=src_refs[a].at[pflat] if per_peer else src_refs[a], dst_ref=land_refs[a].at[pflat],
                    send_sem=send_sems.at[7 * a + k - 1], recv_sem=recv_sems.at[7 * a + k - 1],
                    device_id=pid, device_id_type=MESH)
                copy.wait_send()
                copy.wait_recv()

    outs = pl.pallas_call(
        body, name=name,
        out_shape=tuple(pltpu.HBM(a.shape, a.dtype) for a in srcs_thru + lands_thru),
        in_specs=(HBM,) * (2 * n) + (SEM, SEM, pl.BlockSpec(memory_space=pl.ANY)), out_specs=(HBM,) * (2 * n),
        input_output_aliases={i: i for i in range(2 * n)},
        compiler_params=pltpu.CompilerParams(has_side_effects=DATAFLOW),
    )(*srcs_thru, *lands_thru, send_sems, recv_sems, after)
    me = 4 * lax.axis_index("x") + 2 * lax.axis_index("y") + lax.axis_index("c")
    landed = []
    for src_out, land in zip(outs[:n], outs[n:]):
        own = lax.dynamic_index_in_dim(src_out, me, 0, keepdims=True) if per_peer else src_out[None]
        landed.append(lax.dynamic_update_slice(land, own, (me,) + (0,) * (land.ndim - 1)))
    return landed


def _share_rows(block, name):
    def body(src_ref, out_ref, send_sems, recv_sems, local_sem):
        me, peers = _peers()
        own = pltpu.make_async_copy(src_ref, out_ref.at[me], local_sem)
        own.start()
        copies = [pltpu.make_async_remote_copy(
            src_ref=src_ref, dst_ref=out_ref.at[me], send_sem=send_sems.at[k - 1], recv_sem=recv_sems.at[k - 1],
            device_id=pid, device_id_type=MESH) for k, pid, _ in peers]
        for cp in copies:
            cp.start()
        for cp in copies:
            cp.wait()
        own.wait()

    return pl.pallas_call(
        body, name=name, out_shape=jax.ShapeDtypeStruct((N_DEV,) + block.shape, block.dtype),
        in_specs=[HBM], out_specs=HBM,
        scratch_shapes=[pltpu.SemaphoreType.DMA((N_DEV - 1,)), pltpu.SemaphoreType.DMA((N_DEV - 1,)),
                        pltpu.SemaphoreType.DMA],
    )(block)


def _sum_slots(recv, name, tr):
    n, rows, lanes = recv.shape
    tr = _tile(rows, tr)

    def body(r_ref, o_ref):
        acc = r_ref[0].astype(F32)
        for i in range(1, n):
            acc = acc + r_ref[i].astype(F32)
        o_ref[...] = acc

    return pl.pallas_call(
        body, name=name, grid=(rows // tr,),
        in_specs=[pl.BlockSpec((n, tr, lanes), lambda i: (0, i, 0))],
        out_specs=pl.BlockSpec((tr, lanes), lambda i: (i, 0)),
        out_shape=jax.ShapeDtypeStruct((rows, lanes), F32),
        compiler_params=_cparams(("parallel",)),
    )(recv)


def _adamw_math(w, g, m, v):
    m = ADAM_B1 * m + (1.0 - ADAM_B1) * g
    v = ADAM_B2 * v + (1.0 - ADAM_B2) * (g * g)
    m_hat = m / (1.0 - ADAM_B1 ** ADAM_STEP)
    v_hat = v / (1.0 - ADAM_B2 ** ADAM_STEP)
    return -ADAM_LR * (m_hat / (jnp.sqrt(v_hat) + ADAM_EPS) + ADAM_WD * w), m, v


def _adamw(w, g, m, v, name, tr=256):
    return _rowwise(_adamw_math, name, tr, [w, g, m, v], [], [(w.shape[1], F32)] * 3)


def _adamw_small(ws, gs, ms, vs):
    n = len(ws)

    def body(*refs):
        ins, outs = refs[:4 * n], refs[4 * n:]
        for i in range(n):
            d, m, v = _adamw_math(ins[i][...], ins[n + i][...], ins[2 * n + i][...], ins[3 * n + i][...])
            outs[i][...], outs[n + i][...], outs[2 * n + i][...] = d, m, v

    vmem = pl.BlockSpec(memory_space=pltpu.VMEM)
    res = pl.pallas_call(
        body, name="adamw_small", in_specs=[vmem] * (4 * n), out_specs=[vmem] * (3 * n),
        out_shape=[jax.ShapeDtypeStruct(w.shape, F32) for w in ws] * 3,
    )(*ws, *gs, *ms, *vs)
    return res[:n], res[n:2 * n], res[2 * n:]


def _sum_adamw(recv, w, m, v, name):
    n, r, c = recv.shape
    tr = _tile(r, 256)

    def body(r_ref, w_ref, m_ref, v_ref, g_ref, d_ref, nm_ref, nv_ref):
        g = r_ref[0].astype(F32)
        for i in range(1, n):
            g = g + r_ref[i].astype(F32)
        g_ref[...] = g
        d_ref[...], nm_ref[...], nv_ref[...] = _adamw_math(w_ref[...], g, m_ref[...], v_ref[...])

    row = pl.BlockSpec((None, tr, c), lambda i: (0, i, 0))
    return pl.pallas_call(
        body, name=name, grid=(r // tr,),
        in_specs=[pl.BlockSpec((n, tr, c), lambda i: (0, i, 0)), row, row, row], out_specs=[row] * 4,
        out_shape=[jax.ShapeDtypeStruct((1, r, c), F32)] * 4, compiler_params=_cparams(("parallel",)),
    )(recv, w, m, v)


def kernel(x, mem, positions, ffn1_norm, ffn1_w_gu, ffn1_w_down, mix_norm, w_in, b_gate, sg_ln_g, sg_ln_b, sg_w, sg_b, mla_cq_norm, mla_w_uq, mla_ckv_norm, mla_w_ukv, mla_q_norm, mla_k_norm, mem_norm, mem_w_kv, mem_q_norm, mem_k_norm, w_branch_a, w_branch_b, w_branch_c, w_out, ffn2_norm, ffn2_w_gu, ffn2_w_down, loss_target, m_ffn1_norm, m_ffn1_w_gu, m_ffn1_w_down, m_mix_norm, m_w_in, m_b_gate, m_sg_ln_g, m_sg_ln_b, m_sg_w, m_sg_b, m_mla_cq_norm, m_mla_w_uq, m_mla_ckv_norm, m_mla_w_ukv, m_mla_q_norm, m_mla_k_norm, m_mem_norm, m_mem_w_kv, m_mem_q_norm, m_mem_k_norm, m_w_branch_a, m_w_branch_b, m_w_branch_c, m_w_out, m_ffn2_norm, m_ffn2_w_gu, m_ffn2_w_down, v_ffn1_norm, v_ffn1_w_gu, v_ffn1_w_down, v_mix_norm, v_w_in, v_b_gate, v_sg_ln_g, v_sg_ln_b, v_sg_w, v_sg_b, v_mla_cq_norm, v_mla_w_uq, v_mla_ckv_norm, v_mla_w_ukv, v_mla_q_norm, v_mla_k_norm, v_mem_norm, v_mem_w_kv, v_mem_q_norm, v_mem_k_norm, v_w_branch_a, v_w_branch_b, v_w_branch_c, v_w_out, v_ffn2_norm, v_ffn2_w_gu, v_ffn2_w_down):
    given = dict(locals())
    wts = {n: given[n] for n in ORDER}
    mom = {n: given["m_" + n] for n in ORDER}
    var = {n: given["v_" + n] for n in ORDER}

    def shards(group, zero):
        out = [wts[n][0].astype(BF16) for n in GROUPS[group]]
        return [out[0] + zero.astype(BF16)] + out[1:]

    def full_weights(group, slabs):
        return _compute_layout({n: _full_from_slabs(n, s) for n, s in zip(GROUPS[group], slabs)})

    def zero_of(a):
        return jnp.minimum(jnp.abs(a.reshape(-1)[0]), 0)

    gathered_ffn1, token = _all_gather([wts[n][0].astype(BF16) for n in GROUPS["ffn1"]])
    flight = {}
    flight["ffn1_down"], token = _send_start(shards("ffn1_down", token[0, 0]), False, "gather_ffn1_down_start")
    flight["mix"] = _send_start(shards("mix", token[0, 0]), False, "gather_mix_start")[0]
    recv = {}

    def weights(group, after):
        if group == "ffn1":
            return full_weights(group, gathered_ffn1)
        landed = _send_wait(flight.pop(group), after, False, f"gather_{group}_wait")
        if group == "mix":
            flight["ffn2"] = _send_start(shards("ffn2", zero_of(landed[0])), False, "gather_ffn2_start")[0]
        return full_weights(group, landed)

    small_shapes = [wts[n].shape[1:] for n in SMALL]
    early = SMALL[1:]
    assert SMALL[0] == "ffn1_norm"

    def grads_out(group, G):
        Gr = _reference_layout({n: G[n] for n in GRAD_GROUPS[group]})
        parts = [_slabs_from_full(n, Gr[n]).astype(BF16) for n in GRAD_GROUPS[group]]
        flight["g_" + group], tie = _send_start(parts, True, f"grads_{group}_start")
        if group == "mix":
            small = _pack([G[n].reshape(s) for n, s in zip(early, small_shapes[1:])])
            small = jnp.pad(small, ((0, (-small.shape[0]) % 8), (0, 0)))
            flight["small"], tie = _send_start([small + tie[0, 0]], False, "grads_small_start")
        return tie

    P = {n: wts[n] if wts[n].ndim == 2 else wts[n][0] for n in SMALL}
    loss_part, grad_x, G = _local_step(x[0], mem[0], positions[0], loss_target[0], P, weights, grads_out)

    for group, names in GRAD_GROUPS.items():
        recv.update(zip(names, _send_wait(flight.pop("g_" + group), grad_x, True, f"grads_{group}_wait")))
    early_recv, = _send_wait(flight.pop("small"), grad_x, False, "grads_small_wait")
    last = _share_rows(G["ffn1_norm"].reshape(-1, LANES), "share_ffn1_norm")
    g_small_packed = _sum_slots(jnp.concatenate([last, early_recv], axis=1), "sum_small", 2048)

    grads, delta, new_m, new_v = {}, {}, {}, {}
    for n in SHARDED:
        grads[n], delta[n], new_m[n], new_v[n] = _sum_adamw(recv[n], wts[n], mom[n], var[n], "adamw_" + n)
    grads.update(zip(SMALL, _unpack(g_small_packed, small_shapes)))

    flat2 = lambda d: [d[n].reshape(-1, d[n].shape[-1]) for n in SMALL]
    for dst, vals in zip((delta, new_m, new_v), _adamw_small(flat2(wts), flat2(grads), flat2(mom), flat2(var))):
        dst.update(zip(SMALL, vals))

    loss = lax.psum(jnp.sum(loss_part), ("x", "y", "c"))
    lead = lambda d: [d[n].reshape(wts[n].shape) for n in ORDER]
    return (loss, grad_x[None], *lead(grads), *lead(delta), *lead(new_m), *lead(new_v))
```

```python
import functools

import numpy as np
import jax
import jax.numpy as jnp
from jax import lax
from jax.experimental import pallas as pl
from jax.experimental.pallas import tpu as pltpu

F32, BF16 = jnp.float32, jnp.bfloat16

D_MODEL = 1024
SG_GROUPS, SG_GROUP_DIM, SG_WIDTH, CHUNK = 8, 64, 512, 128
MLA_HEADS, MLA_NOPE, MLA_ROPE, MLA_V, MLA_QK = 8, 64, 32, 64, 96
MLA_Q_RANK, MLA_KV_RANK = 384, 256
MEM_HEADS, MEM_HEAD_DIM, MEM_WIDTH = 4, 128, 512
D_FF = 2816
ROPE_BASE = 10000.0
EPS = 1e-6
NEG = -1e30
ADAM_LR, ADAM_B1, ADAM_B2, ADAM_EPS, ADAM_WD, ADAM_STEP = 0.001, 0.9, 0.999, 1e-08, 0.01, 10

N_DEV = 8
LANES = 128
V7X_VMEM_LIMIT = 56 * 1024 * 1024
HP = MLA_HEADS * LANES

Z_G, Z_U, Z_V, Z_QM, Z_CKV, Z_KR, Z_CQ = 0, 3072, 3584, 4096, 4608, 4864, 4992
Z_COLS = 5376
KR_LANE = 64


def _tile(dim, pref):
    if dim <= pref:
        return dim
    for t in range(pref - pref % LANES, LANES - 1, -LANES):
        if dim % t == 0:
            return t
    for t in range(pref - pref % 8, 7, -8):
        if dim % t == 0:
            return t
    return dim


def _cparams(sem):
    return pltpu.CompilerParams(dimension_semantics=sem, vmem_limit_bytes=V7X_VMEM_LIMIT)


_DN = {"nn": ((1,), (0,)), "nt": ((1,), (1,)), "tn": ((0,), (0,))}


def _dot(a, b, mode="nn"):
    return lax.dot_general(a.astype(BF16), b.astype(BF16), (_DN[mode], ((), ())),
                           preferred_element_type=F32)


def _mm(a, b, mode, out_dtype, name, tm=512, tn=512, tk=2048, tie=None):
    if mode == "tn":
        K, M = a.shape
    else:
        M, K = a.shape
    N = b.shape[0] if mode == "nt" else b.shape[1]
    tm, tn, tk = _tile(M, tm), _tile(N, tn), _tile(K, tk)
    nk = K // tk
    if mode == "tn":
        a_spec = pl.BlockSpec((tk, tm), lambda i, j, k: (k, i))
    else:
        a_spec = pl.BlockSpec((tm, tk), lambda i, j, k: (i, k))
    if mode == "nt":
        b_spec = pl.BlockSpec((tn, tk), lambda i, j, k: (j, k))
    else:
        b_spec = pl.BlockSpec((tk, tn), lambda i, j, k: (k, j))

    ties = [] if tie is None else [tie]

    def body(a_ref, b_ref, *rest):
        o_ref, *scratch = rest[len(ties):]
        p = _dot(a_ref[...], b_ref[...], mode)
        if nk == 1:
            o_ref[...] = p.astype(o_ref.dtype)
        else:
            acc_ref, = scratch
            k = pl.program_id(2)

            @pl.when(k == 0)
            def _():
                acc_ref[...] = p

            @pl.when(k > 0)
            def _():
                acc_ref[...] += p

            @pl.when(k == nk - 1)
            def _():
                o_ref[...] = acc_ref[...].astype(o_ref.dtype)

    return pl.pallas_call(
        body, name=name, grid=(M // tm, N // tn, nk),
        in_specs=[a_spec, b_spec] + [pl.BlockSpec(t.shape, lambda i, j, k: (0, 0)) for t in ties],
        out_specs=pl.BlockSpec((tm, tn), lambda i, j, k: (i, j)),
        out_shape=jax.ShapeDtypeStruct((M, N), out_dtype),
        scratch_shapes=[] if nk == 1 else [pltpu.VMEM((tm, tn), F32)],
        compiler_params=_cparams(("parallel", "parallel", "arbitrary")),
    )(a, b, *ties)


def _mm_t(at, b, name, tm, tn, tk=1024, tie=None):
    return _mm(at, b, "nn", BF16, name, tm=tm, tn=tn, tk=tk, tie=tie)


def _rowwise(fn, name, tr, row_ins, bc_ins, row_outs, acc_outs=()):
    norm = [it if isinstance(it, tuple) else (it, it.shape[1], 0) for it in row_ins]
    rows = norm[0][0].shape[0]
    tr = _tile(rows, tr)
    arrays, in_specs = [], []
    for arr, w, cb in norm:
        arrays.append(arr)
        in_specs.append(pl.BlockSpec((tr, w), lambda i, cb=cb: (i, cb)))
    for arr in bc_ins:
        arrays.append(arr)
        in_specs.append(pl.BlockSpec(arr.shape, lambda i, nd=arr.ndim: (0,) * nd))
    n_in, n_row = len(arrays), len(row_outs)
    out_shape, out_specs, aliases = [], [], {}
    transposed = [len(o) == 3 for o in row_outs]
    for k, o in enumerate(row_outs):
        if o[0] == "into":
            _, target, w, cb = o
            aliases[len(arrays)] = k
            arrays.append(target)
            in_specs.append(pl.BlockSpec(memory_space=pl.ANY))
            out_shape.append(jax.ShapeDtypeStruct(target.shape, target.dtype))
            out_specs.append(pl.BlockSpec((tr, w), lambda i, cb=cb: (i, cb)))
        elif transposed[k]:
            out_shape.append(jax.ShapeDtypeStruct((o[0], rows), o[1]))
            out_specs.append(pl.BlockSpec((o[0], tr), lambda i: (0, i)))
        else:
            out_shape.append(jax.ShapeDtypeStruct((rows, o[0]), o[1]))
            out_specs.append(pl.BlockSpec((tr, o[0]), lambda i: (i, 0)))
    for shp, dt in acc_outs:
        out_shape.append(jax.ShapeDtypeStruct(shp, dt))
        out_specs.append(pl.BlockSpec(shp, lambda i, nd=len(shp): (0,) * nd))

    def body(*refs):
        vals = fn(*[r[...].astype(F32) for r in refs[:n_in]])
        if not isinstance(vals, (tuple, list)):
            vals = (vals,)
        outs = refs[len(arrays):]
        for r, v, t in zip(outs[:n_row], vals[:n_row], transposed):
            r[...] = v.astype(F32).T.astype(r.dtype) if t else v.astype(r.dtype)
        if acc_outs:
            accs = list(zip(outs[n_row:], vals[n_row:]))
            i = pl.program_id(0)

            @pl.when(i == 0)
            def _():
                for r, v in accs:
                    r[...] = v.astype(r.dtype)

            @pl.when(i > 0)
            def _():
                for r, v in accs:
                    r[...] += v.astype(r.dtype)

    res = pl.pallas_call(
        body, name=name, grid=(rows // tr,), in_specs=in_specs, out_specs=out_specs,
        out_shape=out_shape, input_output_aliases=aliases, compiler_params=_cparams(("arbitrary",)),
    )(*arrays)
    return res


def _rsum(x):
    return jnp.sum(x, axis=0, keepdims=True)


def _rms(x, g, n=None):
    n = x.shape[-1] if n is None else n
    r = lax.rsqrt(jnp.sum(x * x, axis=-1, keepdims=True) * (1.0 / n) + EPS)
    return x * r * g


def _rms_bwd(x, g, dy, n=None):
    n = x.shape[-1] if n is None else n
    r = lax.rsqrt(jnp.sum(x * x, axis=-1, keepdims=True) * (1.0 / n) + EPS)
    xh = x * r
    dxh = dy * g
    dx = r * (dxh - xh * (jnp.sum(dxh * xh, axis=-1, keepdims=True) * (1.0 / n)))
    return dx, _rsum(dy * xh)


def _gelu(x):
    return 0.5 * x * (1.0 + lax.erf(x * 0.7071067811865476))


def _gelu_grad(x):
    return 0.5 * (1.0 + lax.erf(x * 0.7071067811865476)) + x * jnp.exp(-0.5 * x * x) * 0.3989422804014327


def _sigmoid(x):
    return 0.5 * jnp.tanh(0.5 * x) + 0.5


FFN_TM, FFN_TN = 1024, 1408
MXU_WIDTH = 256


def _col_chunks(n):
    return [(c, min(c + MXU_WIDTH, n)) for c in range(0, n, MXU_WIDTH)]


def _ffn_gu_act(h, w_gu, tag):
    T = h.shape[0]
    tm, tn = _tile(T, FFN_TM), FFN_TN
    nj = D_FF // tn

    def body(h_ref, wg_ref, wu_ref, gu_ref, a_ref, at_ref):
        h = h_ref[...]
        for c0, c1 in _col_chunks(tn):
            g = _dot(h, wg_ref[:, c0:c1])
            u = _dot(h, wu_ref[:, c0:c1])
            gu_ref[0, :, c0:c1] = g.astype(BF16)
            gu_ref[1, :, c0:c1] = u.astype(BF16)
            a = g * _sigmoid(g) * u
            a_ref[:, c0:c1] = a.astype(BF16)
            at_ref[c0:c1, :] = a.T.astype(BF16)

    return pl.pallas_call(
        body, name=f"{tag}_gu_act", grid=(T // tm, nj),
        in_specs=[pl.BlockSpec((tm, D_MODEL), lambda i, j: (i, 0)),
                  pl.BlockSpec((D_MODEL, tn), lambda i, j: (0, j)),
                  pl.BlockSpec((D_MODEL, tn), lambda i, j: (0, j + nj))],
        out_specs=[pl.BlockSpec((2, tm, tn), lambda i, j: (0, i, j)),
                   pl.BlockSpec((tm, tn), lambda i, j: (i, j)),
                   pl.BlockSpec((tn, tm), lambda i, j: (j, i))],
        out_shape=[jax.ShapeDtypeStruct((2, T, D_FF), BF16), jax.ShapeDtypeStruct((T, D_FF), BF16),
                   jax.ShapeDtypeStruct((D_FF, T), BF16)],
        compiler_params=_cparams(("parallel", "parallel")),
    )(h, w_gu, w_gu)


def _ffn_da_actbwd(do, w_down, gu, tag, tie=None):
    T = do.shape[0]
    tm, tn = _tile(T, FFN_TM), FFN_TN
    ties = [] if tie is None else [tie]

    def body(do_ref, wd_ref, gu_ref, *rest):
        dgu_ref = rest[-1]
        do = do_ref[...]
        for c0, c1 in _col_chunks(tn):
            da = _dot(do, wd_ref[c0:c1, :], "nt")
            g = gu_ref[0, :, c0:c1].astype(F32)
            u = gu_ref[1, :, c0:c1].astype(F32)
            s = _sigmoid(g)
            dgu_ref[0, :, c0:c1] = (da * u * s * (1.0 + g * (1.0 - s))).astype(BF16)
            dgu_ref[1, :, c0:c1] = (da * g * s).astype(BF16)

    return pl.pallas_call(
        body, name=f"{tag}_da_actbwd", grid=(T // tm, D_FF // tn),
        in_specs=[pl.BlockSpec((tm, D_MODEL), lambda i, j: (i, 0)),
                  pl.BlockSpec((tn, D_MODEL), lambda i, j: (j, 0)),
                  pl.BlockSpec((2, tm, tn), lambda i, j: (0, i, j))]
        + [pl.BlockSpec(t.shape, lambda i, j: (0, 0)) for t in ties],
        out_specs=pl.BlockSpec((2, tm, tn), lambda i, j: (0, i, j)),
        out_shape=jax.ShapeDtypeStruct((2, T, D_FF), BF16),
        compiler_params=_cparams(("parallel", "parallel")),
    )(do, w_down, gu, *ties)


def _ffn_dwgu(ht, dgu, tag, tk=2048):
    T = ht.shape[1]
    tn, tk = FFN_TN, _tile(T, tk)
    nj, nk = D_FF // tn, T // tk

    def body(a_ref, b_ref, o_ref, acc_ref):
        k = pl.program_id(1)
        p = _dot(a_ref[...], b_ref[...])

        @pl.when(k == 0)
        def _():
            acc_ref[...] = p

        @pl.when(k > 0)
        def _():
            acc_ref[...] += p

        @pl.when(k == nk - 1)
        def _():
            o_ref[...] = acc_ref[...].astype(o_ref.dtype)

    return pl.pallas_call(
        body, name=f"{tag}_dwgu", grid=(2 * nj, nk),
        in_specs=[pl.BlockSpec((D_MODEL, tk), lambda n, k: (0, k)),
                  pl.BlockSpec((None, tk, tn), lambda n, k: (n // nj, k, n % nj))],
        out_specs=pl.BlockSpec((D_MODEL, tn), lambda n, k: (0, n)),
        out_shape=jax.ShapeDtypeStruct((D_MODEL, 2 * D_FF), BF16),
        scratch_shapes=[pltpu.VMEM((D_MODEL, tn), F32)],
        compiler_params=_cparams(("parallel", "arbitrary")),
    )(ht, dgu)


def _ffn_dh(dgu, w_gu, tag, tm=2048, tie=None):
    T = dgu.shape[1]
    tm, tk = _tile(T, tm), FFN_TN
    nk = D_FF // tk
    ties = [] if tie is None else [tie]

    def body(a_ref, b_ref, *rest):
        o_ref, acc_ref = rest[len(ties):]
        k = pl.program_id(1)
        p = _dot(a_ref[...], b_ref[...], "nt")

        @pl.when(k == 0)
        def _():
            acc_ref[...] = p

        @pl.when(k > 0)
        def _():
            acc_ref[...] += p

        @pl.when(k == 2 * nk - 1)
        def _():
            o_ref[...] = acc_ref[...].astype(o_ref.dtype)

    return pl.pallas_call(
        body, name=f"{tag}_dh", grid=(T // tm, 2 * nk),
        in_specs=[pl.BlockSpec((None, tm, tk), lambda i, k: (k // nk, i, k % nk)),
                  pl.BlockSpec((D_MODEL, tk), lambda i, k: (0, k))]
        + [pl.BlockSpec(t.shape, lambda i, k: (0, 0)) for t in ties],
        out_specs=pl.BlockSpec((tm, D_MODEL), lambda i, k: (i, 0)),
        out_shape=jax.ShapeDtypeStruct((T, D_MODEL), BF16),
        scratch_shapes=[pltpu.VMEM((tm, D_MODEL), F32)],
        compiler_params=_cparams(("parallel", "arbitrary")),
    )(dgu, w_gu, *ties)


def _ffn_fwd(h, w_gu, w_down, tag):
    gu, a, at = _ffn_gu_act(h, w_gu, tag)
    if callable(w_down):
        w_down = w_down(at)
    o = _mm(a, w_down, "nn", BF16, f"{tag}_down", tm=1024, tn=1024, tk=2816)
    return gu, at, o


def _ffn_bwd(do, ht, gu, at, w_gu, w_down, tag, tie=None, on_dw=None):
    on_dw = on_dw or (lambda which, dw: None)
    dw_down = _mm_t(at, do, f"{tag}_dwdown", tm=1408, tn=1024, tk=2048, tie=tie)
    dgu = _ffn_da_actbwd(do, w_down, gu, tag, tie=on_dw("down", dw_down))
    dw_gu = _ffn_dwgu(ht, dgu, tag)
    dh = _ffn_dh(dgu, w_gu, tag, tie=on_dw("gu", dw_gu))
    return dh, dw_gu, dw_down


def _sg_common(u_pre, v_pre, ln_g, ln_b):
    u = _gelu(u_pre)
    v = _gelu(v_pre)
    mu = jnp.mean(v, axis=-1, keepdims=True)
    vc = v - mu
    rstd = lax.rsqrt(jnp.mean(vc * vc, axis=-1, keepdims=True) + EPS)
    vhat = vc * rstd
    vl = vhat * ln_g + ln_b
    return u, vhat, rstd, vl


def _sg_masked_pairs(w):
    t = lax.broadcasted_iota(jnp.int32, (CHUNK, CHUNK), 0)
    s = lax.broadcasted_iota(jnp.int32, (CHUNK, CHUNK), 1)
    causal = s <= t
    wm = [jnp.where(causal, w[g], 0.0).astype(BF16) for g in range(SG_GROUPS)]
    return [jnp.concatenate([wm[2 * j], wm[2 * j + 1]], axis=0) for j in range(SG_GROUPS // 2)], causal


def _sg_mix(vl, pairs, bias):
    tr = vl.shape[0]
    low = lax.broadcasted_iota(jnp.int32, (CHUNK, LANES), 1) < SG_GROUP_DIM
    vb = vl.astype(BF16)
    rows = []
    for c in range(tr // CHUNK):
        slabs = []
        for j in range(SG_GROUPS // 2):
            slab = vb[c * CHUNK:(c + 1) * CHUNK, j * LANES:(j + 1) * LANES]
            m = _dot(pairs[j], slab)
            slabs.append(jnp.where(low, m[:CHUNK], m[CHUNK:]))
        rows.append(jnp.concatenate(slabs, axis=1) + bias)
    return jnp.concatenate(rows, axis=0)


def _sg_fwd(z, ln_g, ln_b, sg_w, bias_full):
    def fn(u_pre, v_pre, ln_g, ln_b, w, bias):
        u, _, _, vl = _sg_common(u_pre, v_pre, ln_g, ln_b)
        pairs, _ = _sg_masked_pairs(w)
        y = u * _sg_mix(vl, pairs, bias)
        return y, y

    return _rowwise(fn, "sg_fwd", 512, [(z, SG_WIDTH, Z_U // SG_WIDTH), (z, SG_WIDTH, Z_V // SG_WIDTH)],
                    [ln_g, ln_b, sg_w, bias_full], [(SG_WIDTH, BF16), (SG_WIDTH, BF16, "T")])


def _sg_bwd(z, dy, ln_g, ln_b, sg_w, bias_full, group_ind, dz):
    def fn(u_pre, v_pre, dy, ln_g, ln_b, w, bias, ind):
        dy = dy.astype(F32)
        u, vhat, rstd, vl = _sg_common(u_pre, v_pre, ln_g, ln_b)
        pairs, causal = _sg_masked_pairs(w)
        mixed = _sg_mix(vl, pairs, bias)
        du_pre = dy * mixed * _gelu_grad(u_pre)
        dmix = dy * u
        tr = dy.shape[0]
        low = lax.broadcasted_iota(jnp.int32, (CHUNK, LANES), 1) < SG_GROUP_DIM
        vb = vl.astype(BF16)
        dw = [jnp.zeros((CHUNK, CHUNK), F32) for _ in range(SG_GROUPS)]
        dbias = jnp.zeros((CHUNK, SG_WIDTH), F32)
        dvl_rows = []
        for c in range(tr // CHUNK):
            dm_c = dmix[c * CHUNK:(c + 1) * CHUNK]
            dbias = dbias + dm_c
            slabs = []
            for j in range(SG_GROUPS // 2):
                slab = vb[c * CHUNK:(c + 1) * CHUNK, j * LANES:(j + 1) * LANES]
                dm = dm_c[:, j * LANES:(j + 1) * LANES]
                d0 = jnp.where(low, dm, 0.0).astype(BF16)
                d1 = jnp.where(low, 0.0, dm).astype(BF16)
                dw[2 * j] = dw[2 * j] + _dot(d0, slab, "nt")
                dw[2 * j + 1] = dw[2 * j + 1] + _dot(d1, slab, "nt")
                slabs.append(_dot(pairs[j], jnp.concatenate([d0, d1], axis=0), "tn"))
            dvl_rows.append(jnp.concatenate(slabs, axis=1))
        dvl = jnp.concatenate(dvl_rows, axis=0)
        dln_g = _rsum(dvl * vhat)
        dln_b = _rsum(dvl)
        dvh = dvl * ln_g
        dv = rstd * (dvh - jnp.mean(dvh, axis=-1, keepdims=True)
                     - vhat * jnp.mean(dvh * vhat, axis=-1, keepdims=True))
        dv_pre = dv * _gelu_grad(v_pre)
        dw = jnp.stack([jnp.where(causal, d, 0.0) for d in dw], axis=0)
        dbias_t = lax.dot_general(dbias, ind, (((1,), (0,)), ((), ())), precision=lax.Precision.HIGHEST,
                                  preferred_element_type=F32)
        return jnp.concatenate([du_pre, dv_pre], axis=1), dw, dbias_t, dln_g, dln_b

    return _rowwise(fn, "sg_bwd", 512,
                    [(z, SG_WIDTH, Z_U // SG_WIDTH), (z, SG_WIDTH, Z_V // SG_WIDTH), dy],
                    [ln_g, ln_b, sg_w, bias_full, group_ind],
                    [("into", dz, 2 * SG_WIDTH, Z_U // (2 * SG_WIDTH))],
                    [((SG_GROUPS, CHUNK, CHUNK), F32), ((CHUNK, SG_GROUPS), F32), ((1, SG_WIDTH), F32), ((1, SG_WIDTH), F32)])


def _rope(x, c, s1, s2):
    return x * c + pltpu.roll(x, LANES - MLA_ROPE // 2, 1) * s1 + pltpu.roll(x, MLA_ROPE // 2, 1) * s2


def _rope_t(d, c, s1, s2):
    return d * c + pltpu.roll(d * s1, MLA_ROPE // 2, 1) + pltpu.roll(d * s2, LANES - MLA_ROPE // 2, 1)


def _mla_post(q_pre, kv_pre, z, tabs, gq, gk):
    scale = MLA_QK ** -0.5 * LOG2E
    T = q_pre.shape[0]
    tr = _tile(T, 256)

    def body(q_ref, k_ref, v_ref, kr_ref, c_ref, s1_ref, s2_ref, gq_ref, gk_ref, qo_ref, ko_ref, vo_ref):
        kr = kr_ref[...].astype(F32)
        c, s1, s2, gq, gk = c_ref[...], s1_ref[...], s2_ref[...], gq_ref[...], gk_ref[...]
        ones_lane = lax.broadcasted_iota(jnp.int32, (tr, LANES), 1) == ONES_LANE
        for h in range(MLA_HEADS):
            sl = slice(h * LANES, (h + 1) * LANES)
            qo_ref[:, sl] = (_rope(_rms(q_ref[:, sl].astype(F32), gq, MLA_QK), c, s1, s2) * scale).astype(BF16)
            ko_ref[:, sl] = _rope(_rms(k_ref[:, sl].astype(F32) + kr, gk, MLA_QK), c, s1, s2).astype(BF16)
            vo_ref[:, sl] = jnp.where(ones_lane, 1.0, v_ref[:, sl].astype(F32)).astype(BF16)

    wide = lambda cb: pl.BlockSpec((tr, HP), lambda i, cb=cb: (i, cb))
    lanes = lambda cb: pl.BlockSpec((tr, LANES), lambda i, cb=cb: (i, cb))
    gain = pl.BlockSpec((1, LANES), lambda i: (0, 0))
    return pl.pallas_call(
        body, name="mla_post", grid=(T // tr,),
        in_specs=[wide(0), wide(0), wide(1), lanes(Z_KR // LANES), lanes(0), lanes(0), lanes(0), gain, gain],
        out_specs=[wide(0)] * 3, out_shape=[jax.ShapeDtypeStruct((T, HP), BF16)] * 3,
        compiler_params=_cparams(("parallel",)),
    )(q_pre, kv_pre, kv_pre, z, *tabs, gq, gk)


def _mla_post_bwd(q_pre, kv_pre, z, tabs, gq, gk, dq, dk, dv):
    scale = MLA_QK ** -0.5
    T = q_pre.shape[0]
    tr = _tile(T, 256)

    def body(q_ref, k_ref, kr_ref, c_ref, s1_ref, s2_ref, dq_ref, dk_ref, dv_ref, gq_ref, gk_ref,
             dqo_ref, dkvo_ref, dkro_ref, dgq_ref, dgk_ref):
        kr = kr_ref[...].astype(F32)
        c, s1, s2, gq, gk = c_ref[...], s1_ref[...], s2_ref[...], gq_ref[...], gk_ref[...]
        lane = lax.broadcasted_iota(jnp.int32, (1, LANES), 1)
        kr_mask = (lane >= KR_LANE) & (lane < KR_LANE + MLA_ROPE)
        dgq = jnp.zeros((1, LANES), F32)
        dgk = jnp.zeros((1, LANES), F32)
        dkr = jnp.zeros((tr, LANES), F32)
        for h in range(MLA_HEADS):
            sl = slice(h * LANES, (h + 1) * LANES)
            dqn = _rope_t(dq_ref[:, sl].astype(F32), c, s1, s2) * scale
            dx, dg = _rms_bwd(q_ref[:, sl].astype(F32), gq, dqn, MLA_QK)
            dqo_ref[:, sl] = dx.astype(BF16)
            dgq = dgq + dg
            dkn = _rope_t(dk_ref[:, sl].astype(F32), c, s1, s2)
            dx, dg = _rms_bwd(k_ref[:, sl].astype(F32) + kr, gk, dkn, MLA_QK)
            dkvo_ref[:, sl] = dx.astype(BF16)
            dkvo_ref[:, HP + h * LANES:HP + (h + 1) * LANES] = dv_ref[:, sl]
            dgk = dgk + dg
            dkr = dkr + dx
        dkro_ref[...] = jnp.where(kr_mask, dkr, 0.0).astype(BF16)
        i = pl.program_id(0)

        @pl.when(i == 0)
        def _():
            dgq_ref[...] = dgq
            dgk_ref[...] = dgk

        @pl.when(i > 0)
        def _():
            dgq_ref[...] += dgq
            dgk_ref[...] += dgk

    wide = lambda cb: pl.BlockSpec((tr, HP), lambda i, cb=cb: (i, cb))
    lanes = lambda cb: pl.BlockSpec((tr, LANES), lambda i, cb=cb: (i, cb))
    gain = pl.BlockSpec((1, LANES), lambda i: (0, 0))
    return pl.pallas_call(
        body, name="mla_post_bwd", grid=(T // tr,),
        in_specs=[wide(0), wide(0), lanes(Z_KR // LANES), lanes(0), lanes(0), lanes(0), wide(0), wide(0), wide(0),
                  gain, gain],
        out_specs=[wide(0), pl.BlockSpec((tr, 2 * HP), lambda i: (i, 0)), lanes(0), gain, gain],
        out_shape=[jax.ShapeDtypeStruct((T, HP), BF16), jax.ShapeDtypeStruct((T, 2 * HP), BF16),
                   jax.ShapeDtypeStruct((T, LANES), BF16), jax.ShapeDtypeStruct((1, LANES), F32),
                   jax.ShapeDtypeStruct((1, LANES), F32)],
        compiler_params=_cparams(("arbitrary",)),
    )(q_pre, kv_pre, z, *tabs, dq, dk, dv, gq, gk)


def _pairs(n, lower):
    a, b = [], []
    for o in range(n):
        inner = range(o + 1) if lower else range(o, n)
        for t in inner:
            a.append(o)
            b.append(t)
    return jnp.asarray(np.array(a, np.int32)), jnp.asarray(np.array(b, np.int32))


FLASH_TILE, FLASH_SUB_ROWS = 2048, 512
FLASH_FWD_SUB_ROWS = 256
LOG2E, LN2 = 1.4426950408889634, 0.6931471805599453
ONES_LANE = MLA_V


def _flash_tiles(T):
    tq = _tile(T, FLASH_TILE)
    return tq, _tile(tq, FLASH_SUB_ROWS)


def _col_span(t, sr, rb, diag, key_major):
    if not diag:
        return 0, t
    return (rb * sr, t) if key_major else (0, (rb + 1) * sr)


def _span_iota(sr, rb, c0, c1):
    r = lax.broadcasted_iota(jnp.int32, (sr, c1 - c0), 0) + rb * sr
    c = lax.broadcasted_iota(jnp.int32, (sr, c1 - c0), 1) + c0
    return r, c


def _lanes(x, width):
    return jnp.concatenate([x] * (width // LANES), axis=1)


def _flash_fwd(q, k, v):
    T = q.shape[0]
    tq, sr = _flash_tiles(T)
    sr = _tile(tq, FLASH_FWD_SUB_ROWS)
    n = T // tq
    ii, jj = _pairs(n, True)

    def body(ii_ref, jj_ref, q_ref, k_ref, v_ref, o_ref, ot_ref, lse_ref, lset_ref, m_sc, acc_sc):
        p_ = pl.program_id(1)
        i, j = ii_ref[p_], jj_ref[p_]

        @pl.when(j == 0)
        def _():
            m_sc[...] = jnp.full(m_sc.shape, NEG, F32)
            acc_sc[...] = jnp.zeros(acc_sc.shape, F32)

        def tile(diag):
            nrb = tq // sr

            def scores(rb):
                c0, c1 = _col_span(tq, sr, rb, diag, False)
                return _dot(q_ref[rb * sr:(rb + 1) * sr, :], k_ref[c0:c1, :], "nt")

            s_next = scores(0)
            for rb in range(nrb):
                rows = slice(rb * sr, (rb + 1) * sr)
                c0, c1 = _col_span(tq, sr, rb, diag, False)
                s, s_next = s_next, (scores(rb + 1) if rb + 1 < nrb else None)
                if diag:
                    r, c = _span_iota(sr, rb, c0, c1)
                    s = jnp.where(c <= r, s, NEG)
                m = m_sc[rows, :]
                m_new = jnp.maximum(m, jnp.max(s, axis=1, keepdims=True))
                p = jnp.exp2(s - _lanes(m_new, c1 - c0))
                acc_sc[rows, :] = jnp.exp2(m - m_new) * acc_sc[rows, :] + _dot(p, v_ref[c0:c1, :])
                m_sc[rows, :] = m_new

        @pl.when(j < i)
        def _():
            tile(False)

        @pl.when(j == i)
        def _():
            tile(True)
            acc = acc_sc[...]
            lane = lax.broadcasted_iota(jnp.int32, acc.shape, 1)
            l = jnp.sum(jnp.where(lane == ONES_LANE, acc, 0.0), axis=1, keepdims=True)
            o = jnp.where(lane < MLA_V, acc / l, 0.0)
            o_ref[...] = o.astype(o_ref.dtype)
            ot_ref[...] = o.T.astype(ot_ref.dtype)
            lse = m_sc[...] + jnp.log2(l)
            lse_ref[...] = lse
            lset_ref[...] = lse.T[:8]

    blk = lambda which: pl.BlockSpec((tq, LANES), which)
    qmap = lambda h, p, ii, jj: (ii[p], h)
    kmap = lambda h, p, ii, jj: (jj[p], h)
    tmap = lambda h, p, ii, jj: (h, ii[p])
    return pl.pallas_call(
        body, name="mla_flash_fwd",
        grid_spec=pltpu.PrefetchScalarGridSpec(
            num_scalar_prefetch=2, grid=(MLA_HEADS, int(ii.shape[0])),
            in_specs=[blk(qmap), blk(kmap), blk(kmap)],
            out_specs=[blk(qmap), pl.BlockSpec((LANES, tq), tmap), blk(qmap), pl.BlockSpec((8, tq), tmap)],
            scratch_shapes=[pltpu.VMEM((tq, LANES), F32)] * 2),
        out_shape=[jax.ShapeDtypeStruct((T, HP), BF16), jax.ShapeDtypeStruct((HP, T), BF16),
                   jax.ShapeDtypeStruct((T, HP), F32), jax.ShapeDtypeStruct((8 * MLA_HEADS, T), F32)],
        compiler_params=_cparams(("parallel", "arbitrary")),
    )(ii, jj, q, k, v)


def _flash_dq(q, k, v, do, lse, delta):
    T = q.shape[0]
    tq, sr = _flash_tiles(T)
    n = T // tq
    ii, jj = _pairs(n, True)

    def body(ii_ref, jj_ref, q_ref, k_ref, v_ref, do_ref, lse_ref, dl_ref, dq_ref, acc_sc):
        p_ = pl.program_id(1)
        i, j = ii_ref[p_], jj_ref[p_]

        @pl.when(j == 0)
        def _():
            acc_sc[...] = jnp.zeros(acc_sc.shape, F32)

        def tile(diag):
            nrb = tq // sr

            def products(rb):
                rows = slice(rb * sr, (rb + 1) * sr)
                c0, c1 = _col_span(tq, sr, rb, diag, False)
                return _dot(q_ref[rows, :], k_ref[c0:c1, :], "nt"), _dot(do_ref[rows, :], v_ref[c0:c1, :], "nt")

            nxt = products(0)
            for rb in range(nrb):
                rows = slice(rb * sr, (rb + 1) * sr)
                c0, c1 = _col_span(tq, sr, rb, diag, False)
                (s, dp), nxt = nxt, (products(rb + 1) if rb + 1 < nrb else None)
                p = jnp.exp2(s - _lanes(lse_ref[rows, :], c1 - c0))
                if diag:
                    r, c = _span_iota(sr, rb, c0, c1)
                    p = jnp.where(c <= r, p, 0.0)
                acc_sc[rows, :] += _dot(p * (dp - _lanes(dl_ref[rows, :], c1 - c0)), k_ref[c0:c1, :])

        @pl.when(j < i)
        def _():
            tile(False)

        @pl.when(j == i)
        def _():
            tile(True)
            dq_ref[...] = acc_sc[...].astype(dq_ref.dtype)

    blk = lambda which: pl.BlockSpec((tq, LANES), which)
    qmap = lambda h, p, ii, jj: (ii[p], h)
    kmap = lambda h, p, ii, jj: (jj[p], h)
    return pl.pallas_call(
        body, name="mla_flash_dq",
        grid_spec=pltpu.PrefetchScalarGridSpec(
            num_scalar_prefetch=2, grid=(MLA_HEADS, int(ii.shape[0])),
            in_specs=[blk(qmap), blk(kmap), blk(kmap), blk(qmap), blk(qmap), blk(qmap)],
            out_specs=blk(qmap),
            scratch_shapes=[pltpu.VMEM((tq, LANES), F32)]),
        out_shape=jax.ShapeDtypeStruct((T, HP), BF16),
        compiler_params=_cparams(("parallel", "arbitrary")),
    )(ii, jj, q, k, v, do, lse, delta)


def _flash_dkv(q, k, v, do, lse_t, delta_t):
    T = q.shape[0]
    tq, sr = _flash_tiles(T)
    n = T // tq
    jj, ii = _pairs(n, False)

    def body(jj_ref, ii_ref, q_ref, k_ref, v_ref, do_ref, lse_ref, dl_ref, dk_ref, dv_ref, dk_sc, dv_sc):
        p_ = pl.program_id(1)
        j, i = jj_ref[p_], ii_ref[p_]

        @pl.when(i == j)
        def _():
            dk_sc[...] = jnp.zeros(dk_sc.shape, F32)
            dv_sc[...] = jnp.zeros(dv_sc.shape, F32)

        def tile(diag):
            nrb = tq // sr

            def products(rb):
                rows = slice(rb * sr, (rb + 1) * sr)
                c0, c1 = _col_span(tq, sr, rb, diag, True)
                return _dot(k_ref[rows, :], q_ref[c0:c1, :], "nt"), _dot(v_ref[rows, :], do_ref[c0:c1, :], "nt")

            nxt = products(0)
            for rb in range(nrb):
                rows = slice(rb * sr, (rb + 1) * sr)
                c0, c1 = _col_span(tq, sr, rb, diag, True)
                (st, dpt), nxt = nxt, (products(rb + 1) if rb + 1 < nrb else None)
                pt = jnp.exp2(st - lse_ref[:1, c0:c1])
                if diag:
                    r, c = _span_iota(sr, rb, c0, c1)
                    pt = jnp.where(r <= c, pt, 0.0)
                dv_sc[rows, :] += _dot(pt, do_ref[c0:c1, :])
                dk_sc[rows, :] += _dot(pt * (dpt - dl_ref[:1, c0:c1]), q_ref[c0:c1, :])

        @pl.when(i == j)
        def _():
            tile(True)

        @pl.when(i > j)
        def _():
            tile(False)

        @pl.when(i == n - 1)
        def _():
            dk_ref[...] = (dk_sc[...] * LN2).astype(dk_ref.dtype)
            dv_ref[...] = dv_sc[...].astype(dv_ref.dtype)

    blk = lambda which: pl.BlockSpec((tq, LANES), which)
    qmap = lambda h, p, jj, ii: (ii[p], h)
    kmap = lambda h, p, jj, ii: (jj[p], h)
    lse_rows = pl.BlockSpec((8, tq), lambda h, p, jj, ii: (h, ii[p]))
    delta_rows = pl.BlockSpec((8, tq), lambda h, p, jj, ii: (h * (LANES // 8), ii[p]))
    return pl.pallas_call(
        body, name="mla_flash_dkv",
        grid_spec=pltpu.PrefetchScalarGridSpec(
            num_scalar_prefetch=2, grid=(MLA_HEADS, int(ii.shape[0])),
            in_specs=[blk(qmap), blk(kmap), blk(kmap), blk(qmap), lse_rows, delta_rows],
            out_specs=[blk(kmap), blk(kmap)],
            scratch_shapes=[pltpu.VMEM((tq, LANES), F32)] * 2),
        out_shape=[jax.ShapeDtypeStruct((T, HP), BF16)] * 2,
        compiler_params=_cparams(("parallel", "arbitrary")),
    )(jj, ii, q, k, v, do, lse_t, delta_t)


def _mem_fwd(z, km, vm, gq):
    scale = MEM_HEAD_DIM ** -0.5

    def fn(qm, km, vm, gq):
        ys = []
        for h in range(MEM_HEADS):
            sl = slice(h * LANES, (h + 1) * LANES)
            q = _rms(qm[:, sl], gq) * scale
            s = _dot(q, km[:, sl], "nt")
            p = jnp.exp(s - jnp.max(s, axis=1, keepdims=True))
            p = p / jnp.sum(p, axis=1, keepdims=True)
            ys.append(_dot(p, vm[:, sl]))
        y = jnp.concatenate(ys, axis=1)
        return y, y

    return _rowwise(fn, "mem_fwd", 512, [(z, MEM_WIDTH, Z_QM // MEM_WIDTH)], [km, vm, gq],
                    [(MEM_WIDTH, BF16), (MEM_WIDTH, BF16, "T")])


def _mem_bwd(z, dy, km, vm, gq, dz):
    scale = MEM_HEAD_DIM ** -0.5

    def fn(qm, dy, km, vm, gq):
        dqs, dks, dvs = [], [], []
        dgq = jnp.zeros((1, LANES), F32)
        for h in range(MEM_HEADS):
            sl = slice(h * LANES, (h + 1) * LANES)
            q = (_rms(qm[:, sl], gq) * scale).astype(BF16)
            dyh = dy[:, sl]
            kh, vh = km[:, sl], vm[:, sl]
            s = _dot(q, kh, "nt")
            p = jnp.exp(s - jnp.max(s, axis=1, keepdims=True))
            p = p / jnp.sum(p, axis=1, keepdims=True)
            dp = _dot(dyh, vh, "nt")
            ds = p * (dp - jnp.sum(p * dp, axis=1, keepdims=True))
            dq = _dot(ds, kh) * scale
            dx, dg = _rms_bwd(qm[:, sl], gq, dq)
            dqs.append(dx)
            dgq = dgq + dg
            st = _dot(kh, q, "nt")
            pt = jnp.exp(st - jnp.max(st, axis=0, keepdims=True))
            pt = pt / jnp.sum(pt, axis=0, keepdims=True)
            dpt = _dot(vh, dyh, "nt")
            dst = pt * (dpt - jnp.sum(pt * dpt, axis=0, keepdims=True))
            dvs.append(_dot(pt, dyh))
            dks.append(_dot(dst, q))
        return jnp.concatenate(dqs, axis=1), jnp.concatenate(dks, axis=1), jnp.concatenate(dvs, axis=1), dgq

    m = km.shape[0]
    return _rowwise(fn, "mem_bwd", 512, [(z, MEM_WIDTH, Z_QM // MEM_WIDTH), dy], [km, vm, gq],
                    [("into", dz, MEM_WIDTH, Z_QM // MEM_WIDTH)],
                    [((m, MEM_WIDTH), F32), ((m, MEM_WIDTH), F32), ((1, LANES), F32)])


GROUPS = {"ffn1": ["ffn1_w_gu"], "ffn1_down": ["ffn1_w_down"],
          "mix": ["w_in", "mla_w_uq", "mla_w_ukv", "mem_w_kv", "w_branch_a", "w_branch_b", "w_branch_c", "w_out"],
          "ffn2": ["ffn2_w_gu", "ffn2_w_down"]}
GRAD_GROUPS = {"ffn2": GROUPS["ffn2"], "mix": GROUPS["mix"], "ffn1_down": ["ffn1_w_down"], "ffn1_gu": ["ffn1_w_gu"]}


def _local_step(x, mem, positions, loss_target, P, weights, grads_out):
    T = x.shape[0]
    G = {}
    W = dict(weights("ffn1", None))

    half = MLA_ROPE // 2
    inv = ROPE_BASE ** (-jnp.arange(half, dtype=F32) / half)
    ang = positions.astype(F32)[:, None] * inv
    cos, sin = jnp.cos(ang), jnp.sin(ang)
    one, zero = jnp.ones((T, MLA_NOPE), F32), jnp.zeros((T, half), F32)
    pad = LANES - MLA_QK
    tabs = (jnp.concatenate([one, cos, cos, jnp.ones((T, pad), F32)], axis=1),
            jnp.concatenate([jnp.zeros((T, MLA_NOPE), F32), -sin, zero, jnp.zeros((T, pad), F32)], axis=1),
            jnp.concatenate([jnp.zeros((T, MLA_NOPE), F32), zero, sin, jnp.zeros((T, pad), F32)], axis=1))
    gq_p = jnp.pad(P["mla_q_norm"], ((0, 0), (0, pad)))
    gk_p = jnp.pad(P["mla_k_norm"], ((0, 0), (0, pad)))
    bias_full = jnp.repeat(P["sg_b"].T, SG_GROUP_DIM, axis=1)
    group_ind = jnp.repeat(jnp.eye(SG_GROUPS, dtype=F32), SG_GROUP_DIM, axis=0)

    HT = (D_MODEL, BF16, "T")

    def norm2(x, g):
        h = _rms(x, g)
        return h, h

    h1, h1t = _rowwise(norm2, "ffn1_norm", 512, [x], [P["ffn1_norm"]], [(D_MODEL, BF16), HT])
    def ffn1_w_down(after):
        W.update(weights("ffn1_down", after))
        return W["ffn1_w_down"]

    gu1, a1t, o1 = _ffn_fwd(h1, W["ffn1_w_gu"], ffn1_w_down, "ffn1")

    def resid_norm(x, o, g):
        xn = x + 0.5 * o
        h = _rms(xn, g)
        return xn, h, h

    x1, hm, hmt = _rowwise(resid_norm, "mix_norm", 512, [x, o1], [P["mix_norm"]],
                           [(D_MODEL, F32), (D_MODEL, BF16), HT])
    W.update(weights("mix", hm))
    z = _mm(hm, W["w_in"], "nn", BF16, "w_in", tm=1024, tn=1792)

    y_a, y_at = _sg_fwd(z, P["sg_ln_g"], P["sg_ln_b"], P["sg_w"], bias_full)

    def c_norm(cq, ckv, gq, gkv):
        a, b = _rms(cq, gq), _rms(ckv, gkv)
        return a, b, a, b

    cqn, ckvn, cqnt, ckvnt = _rowwise(
        c_norm, "mla_cnorm", 512, [(z, MLA_Q_RANK, Z_CQ // MLA_Q_RANK), (z, MLA_KV_RANK, Z_CKV // MLA_KV_RANK)],
        [P["mla_cq_norm"], P["mla_ckv_norm"]],
        [(MLA_Q_RANK, BF16), (MLA_KV_RANK, BF16), (MLA_Q_RANK, BF16, "T"), (MLA_KV_RANK, BF16, "T")])
    q_pre = _mm(cqn, W["mla_w_uq"], "nn", BF16, "mla_uq", tm=1024, tn=1024)
    kv_pre = _mm(ckvn, W["mla_w_ukv"], "nn", BF16, "mla_ukv", tm=1024, tn=1024)
    q, k, v = _mla_post(q_pre, kv_pre, z, tabs, gq_p, gk_p)
    y_b, y_bt, lse, lse_t = _flash_fwd(q, k, v)

    memn, = _rowwise(lambda m, g: _rms(m, g), "mem_norm", 256, [mem], [P["mem_norm"]], [(D_MODEL, BF16)])
    kvm = _mm(memn, W["mem_w_kv"], "nn", F32, "mem_kv")

    def mem_k(kvm, gk):
        ks = [_rms(kvm[:, h * LANES:(h + 1) * LANES], gk) for h in range(MEM_HEADS)]
        return jnp.concatenate(ks, axis=1), kvm[:, MEM_WIDTH:]

    km, vm = _rowwise(mem_k, "mem_knorm", 256, [kvm], [P["mem_k_norm"]], [(MEM_WIDTH, BF16), (MEM_WIDTH, BF16)])
    y_c, y_ct = _mem_fwd(z, km, vm, P["mem_q_norm"])

    pa = _mm(y_a, W["w_branch_a"], "nn", BF16, "branch_a", tm=1024, tn=1024)
    pb = _mm(y_b, W["w_branch_b"], "nn", BF16, "branch_b", tm=1024, tn=1024)
    pc = _mm(y_c, W["w_branch_c"], "nn", BF16, "branch_c", tm=1024, tn=1024)

    def merge(zg, pa, pb, pc, b):
        g = _sigmoid(zg + b)
        m = g[:, :D_MODEL] * pa + g[:, D_MODEL:2 * D_MODEL] * pb + g[:, 2 * D_MODEL:] * pc
        return m, m

    merged, mergedt = _rowwise(merge, "merge", 256, [(z, 3 * D_MODEL, 0), pa, pb, pc], [P["b_gate"]],
                               [(D_MODEL, BF16), HT])
    om = _mm(merged, W["w_out"], "nn", BF16, "w_out", tm=1024, tn=1024)

    def resid_norm1(x, o, g):
        xn = x + o
        h = _rms(xn, g)
        return xn, h, h

    x2, h2, h2t = _rowwise(resid_norm1, "ffn2_norm", 512, [x1, om], [P["ffn2_norm"]],
                           [(D_MODEL, F32), (D_MODEL, BF16), HT])
    W.update(weights("ffn2", h2))
    gu2, a2t, o2 = _ffn_fwd(h2, W["ffn2_w_gu"], W["ffn2_w_down"], "ffn2")

    def loss_fn(x2, o2, t):
        e = x2 + 0.5 * o2 - t
        return e * (1.0 / D_MODEL), (e * (0.5 / D_MODEL)).astype(BF16), _rsum(e * e) * (0.5 / D_MODEL)

    dx3, do2, loss_part = _rowwise(loss_fn, "loss", 512, [x2, o2, loss_target], [],
                                   [(D_MODEL, F32), (D_MODEL, BF16)], [((1, D_MODEL), F32)])

    dh2, G["ffn2_w_gu"], G["ffn2_w_down"] = _ffn_bwd(do2, h2t, gu2, a2t, W["ffn2_w_gu"], W["ffn2_w_down"], "ffn2")
    tie = grads_out("ffn2", G)

    def norm_bwd(x, dh, dxo, g, *_):
        dx, dg = _rms_bwd(x, g, dh)
        dx = dx + dxo
        return dx, dx, dg

    dx2, dx2b, G["ffn2_norm"] = _rowwise(norm_bwd, "ffn2_norm_bwd", 512, [x2, dh2, dx3],
                                         [P["ffn2_norm"]] + ([] if tie is None else [tie]),
                                         [(D_MODEL, F32), (D_MODEL, BF16)], [((1, D_MODEL), F32)])

    G["w_out"] = _mm_t(mergedt, dx2b, "w_out_dw", tm=1024, tn=1024)
    dmerged = _mm(dx2b, W["w_out"], "nt", BF16, "w_out_dx", tm=1024, tn=1024)

    def merge_bwd(zg, pa, pb, pc, dm, b):
        g = _sigmoid(zg + b)
        ps = jnp.concatenate([pa, pb, pc], axis=1)
        dm3 = jnp.concatenate([dm, dm, dm], axis=1)
        dzg = dm3 * ps * g * (1.0 - g)
        dp = dm3 * g
        return dzg, dp[:, :D_MODEL], dp[:, D_MODEL:2 * D_MODEL], dp[:, 2 * D_MODEL:], _rsum(dzg)

    dz = lax.empty((T, Z_COLS), BF16)
    dz, dpa, dpb, dpc, G["b_gate"] = _rowwise(
        merge_bwd, "merge_bwd", 256, [(z, 3 * D_MODEL, 0), pa, pb, pc, dmerged], [P["b_gate"]],
        [("into", dz, 3 * D_MODEL, 0), (D_MODEL, BF16), (D_MODEL, BF16), (D_MODEL, BF16)], [((1, 3 * D_MODEL), F32)])

    G["w_branch_a"] = _mm_t(y_at, dpa, "branch_a_dw", tm=512, tn=1024)
    G["w_branch_b"] = _mm_t(y_bt, dpb, "branch_b_dw", tm=1024, tn=1024)
    G["w_branch_c"] = _mm_t(y_ct, dpc, "branch_c_dw", tm=512, tn=1024)
    dy_a = _mm(dpa, W["w_branch_a"], "nt", BF16, "branch_a_dx", tm=1024, tn=512)
    dy_b = _mm(dpb, W["w_branch_b"], "nt", BF16, "branch_b_dx", tm=1024, tn=1024)
    dy_c = _mm(dpc, W["w_branch_c"], "nt", BF16, "branch_c_dx", tm=1024, tn=512)

    dz, G["sg_w"], dbias_t, G["sg_ln_g"], G["sg_ln_b"] = _sg_bwd(
        z, dy_a, P["sg_ln_g"], P["sg_ln_b"], P["sg_w"], bias_full, group_ind, dz)
    G["sg_b"] = dbias_t.T

    dz, dkm, dvm, G["mem_q_norm"] = _mem_bwd(z, dy_c, km, vm, P["mem_q_norm"], dz)

    def mem_k_bwd(kvm, dkm, dvm, gk):
        dks = []
        dg = jnp.zeros((1, LANES), F32)
        for h in range(MEM_HEADS):
            sl = slice(h * LANES, (h + 1) * LANES)
            dx, d = _rms_bwd(kvm[:, sl], gk, dkm[:, sl])
            dks.append(dx)
            dg = dg + d
        return jnp.concatenate(dks + [dvm], axis=1), dg

    dkvm, G["mem_k_norm"] = _rowwise(mem_k_bwd, "mem_knorm_bwd", 256, [kvm, dkm, dvm], [P["mem_k_norm"]],
                                     [(2 * MEM_WIDTH, BF16)], [((1, LANES), F32)])
    G["mem_w_kv"] = _mm(memn, dkvm, "tn", BF16, "mem_kv_dw")
    dmemn = _mm(dkvm, W["mem_w_kv"], "nt", F32, "mem_kv_dx")
    _, G["mem_norm"] = _rowwise(lambda m, d, g: _rms_bwd(m, g, d), "mem_norm_bwd", 256, [mem, dmemn],
                                [P["mem_norm"]], [(D_MODEL, BF16)], [((1, D_MODEL), F32)])

    def delta_fn(o, do):
        od = o.astype(F32) * do.astype(F32)
        ds = [jnp.broadcast_to(jnp.sum(od[:, h * LANES:(h + 1) * LANES], axis=1, keepdims=True), (od.shape[0], LANES))
              for h in range(MLA_HEADS)]
        d = jnp.concatenate(ds, axis=1)
        return d, d

    delta, delta_t = _rowwise(delta_fn, "mla_delta", 512, [y_b, dy_b], [], [(HP, F32), (HP, F32, "T")])
    dq = _flash_dq(q, k, v, dy_b, lse, delta)
    dk, dv = _flash_dkv(q, k, v, dy_b, lse_t, delta_t)
    dq_pre, dkv_pre, dkr, dgq, dgk = _mla_post_bwd(q_pre, kv_pre, z, tabs, gq_p, gk_p, dq, dk, dv)
    G["mla_q_norm"], G["mla_k_norm"] = dgq[:, :MLA_QK], dgk[:, :MLA_QK]
    G["mla_w_uq"] = _mm_t(cqnt, dq_pre, "mla_uq_dw", tm=384, tn=1024)
    G["mla_w_ukv"] = _mm_t(ckvnt, dkv_pre, "mla_ukv_dw", tm=256, tn=2048)
    dcqn = _mm(dq_pre, W["mla_w_uq"], "nt", BF16, "mla_uq_dx", tm=1024)
    dckvn = _mm(dkv_pre, W["mla_w_ukv"], "nt", BF16, "mla_ukv_dx", tm=1024)

    def c_norm_bwd(cq, ckv, dcqn, dckvn, dkr, gq, gkv):
        dcq, dgq = _rms_bwd(cq, gq, dcqn)
        dckv, dgkv = _rms_bwd(ckv, gkv, dckvn)
        return jnp.concatenate([dckv, dkr, dcq], axis=1), dgq, dgkv

    tail = Z_COLS - Z_CKV
    dz, G["mla_cq_norm"], G["mla_ckv_norm"] = _rowwise(
        c_norm_bwd, "mla_cnorm_bwd", 512,
        [(z, MLA_Q_RANK, Z_CQ // MLA_Q_RANK), (z, MLA_KV_RANK, Z_CKV // MLA_KV_RANK), dcqn, dckvn, dkr],
        [P["mla_cq_norm"], P["mla_ckv_norm"]], [("into", dz, tail, Z_CKV // tail)],
        [((1, MLA_Q_RANK), F32), ((1, MLA_KV_RANK), F32)])
    G["w_in"] = _mm_t(hmt, dz, "w_in_dw", tm=1024, tn=1792, tk=2048)
    dhm = _mm(dz, W["w_in"], "nt", BF16, "w_in_dx", tm=1024, tn=1024, tk=2688)

    def norm_bwd_half(x, dh, dxo, g):
        dx, dg = _rms_bwd(x, g, dh)
        dx = dx + dxo
        return dx, (0.5 * dx), dg

    dx1, do1, G["mix_norm"] = _rowwise(norm_bwd_half, "mix_norm_bwd", 512, [x1, dhm, dx2], [P["mix_norm"]],
                                       [(D_MODEL, F32), (D_MODEL, BF16)], [((1, D_MODEL), F32)])
    tie = grads_out("mix", G)

    def ffn1_dw(which, dw):
        G["ffn1_w_" + which] = dw
        return grads_out("ffn1_" + which, G)

    dh1, _, _ = _ffn_bwd(do1, h1t, gu1, a1t, W["ffn1_w_gu"], W["ffn1_w_down"], "ffn1", tie, ffn1_dw)

    def norm_bwd_last(x, dh, dxo, g):
        dx, dg = _rms_bwd(x, g, dh)
        return dx + dxo, dg

    grad_x, G["ffn1_norm"] = _rowwise(norm_bwd_last, "ffn1_norm_bwd", 512, [x, dh1, dx1], [P["ffn1_norm"]],
                                      [(D_MODEL, F32)], [((1, D_MODEL), F32)])
    return loss_part, grad_x, G


SHARDED = ["ffn1_w_gu", "ffn1_w_down", "w_in", "mla_w_uq", "mla_w_ukv", "mem_w_kv",
           "w_branch_a", "w_branch_b", "w_branch_c", "w_out", "ffn2_w_gu", "ffn2_w_down"]
ROW_SHARDED = {"ffn1_w_down", "mem_w_kv", "w_out", "ffn2_w_down"}
SMALL = ["ffn1_norm", "mix_norm", "b_gate", "sg_ln_g", "sg_ln_b", "sg_w", "sg_b", "mla_cq_norm",
         "mla_ckv_norm", "mla_q_norm", "mla_k_norm", "mem_norm", "mem_q_norm", "mem_k_norm", "ffn2_norm"]
ORDER = ["ffn1_norm", "ffn1_w_gu", "ffn1_w_down", "mix_norm", "w_in", "b_gate", "sg_ln_g", "sg_ln_b", "sg_w",
         "sg_b", "mla_cq_norm", "mla_w_uq", "mla_ckv_norm", "mla_w_ukv", "mla_q_norm", "mla_k_norm", "mem_norm",
         "mem_w_kv", "mem_q_norm", "mem_k_norm", "w_branch_a", "w_branch_b", "w_branch_c", "w_out", "ffn2_norm",
         "ffn2_w_gu", "ffn2_w_down"]

_IN_U, _IN_V, _IN_CQ, _IN_CKV, _IN_KR, _IN_QM, _IN_G = 0, 512, 1024, 1408, 1664, 1696, 2208
IN_COLS = 5280


def _full_from_slabs(name, slabs):
    n, r, c = slabs.shape
    if name in ROW_SHARDED:
        return slabs.reshape(n * r, c)
    return slabs.transpose(1, 0, 2).reshape(r, n * c)


def _slabs_from_full(name, full):
    if name in ROW_SHARDED:
        return full.reshape(N_DEV, full.shape[0] // N_DEV, full.shape[1])
    r, c = full.shape
    return full.reshape(r, N_DEV, c // N_DEV).transpose(1, 0, 2)


def _compute_layout(full):
    W = dict(full)
    if "w_in" not in full:
        return W
    w = full["w_in"]
    kr = jnp.pad(w[:, _IN_KR:_IN_QM], ((0, 0), (KR_LANE, LANES - KR_LANE - MLA_ROPE)))
    W["w_in"] = jnp.concatenate([w[:, _IN_G:], w[:, _IN_U:_IN_CQ], w[:, _IN_QM:_IN_G], w[:, _IN_CKV:_IN_KR], kr,
                                 w[:, _IN_CQ:_IN_CKV]], axis=1)
    uq = full["mla_w_uq"].reshape(MLA_Q_RANK, MLA_HEADS, MLA_QK)
    W["mla_w_uq"] = jnp.pad(uq, ((0, 0), (0, 0), (0, LANES - MLA_QK))).reshape(MLA_Q_RANK, HP)
    ukv = full["mla_w_ukv"].reshape(MLA_KV_RANK, MLA_HEADS, MLA_NOPE + MLA_V)
    padh = lambda a: jnp.pad(a, ((0, 0), (0, 0), (0, LANES - a.shape[2]))).reshape(MLA_KV_RANK, HP)
    W["mla_w_ukv"] = jnp.concatenate([padh(ukv[:, :, :MLA_NOPE]), padh(ukv[:, :, MLA_NOPE:])], axis=1)
    wb = full["w_branch_b"].reshape(MLA_HEADS, MLA_V, D_MODEL)
    W["w_branch_b"] = jnp.pad(wb, ((0, 0), (0, LANES - MLA_V), (0, 0))).reshape(HP, D_MODEL)
    return W


def _reference_layout(G):
    out = dict(G)
    if "w_in" not in G:
        return out
    g = G["w_in"]
    out["w_in"] = jnp.concatenate([
        g[:, Z_U:Z_QM], g[:, Z_CQ:Z_COLS], g[:, Z_CKV:Z_KR], g[:, Z_KR + KR_LANE:Z_KR + KR_LANE + MLA_ROPE],
        g[:, Z_QM:Z_CKV], g[:, Z_G:Z_U]], axis=1)
    out["mla_w_uq"] = G["mla_w_uq"].reshape(MLA_Q_RANK, MLA_HEADS, LANES)[:, :, :MLA_QK].reshape(MLA_Q_RANK, -1)
    gk = G["mla_w_ukv"][:, :HP].reshape(MLA_KV_RANK, MLA_HEADS, LANES)[:, :, :MLA_NOPE]
    gv = G["mla_w_ukv"][:, HP:].reshape(MLA_KV_RANK, MLA_HEADS, LANES)[:, :, :MLA_V]
    out["mla_w_ukv"] = jnp.concatenate([gk, gv], axis=2).reshape(MLA_KV_RANK, -1)
    out["w_branch_b"] = G["w_branch_b"].reshape(MLA_HEADS, LANES, D_MODEL)[:, :MLA_V].reshape(-1, D_MODEL)
    return out


def _pack(parts):
    flat = []
    for a in parts:
        a = a.reshape(-1)
        flat.append(jnp.pad(a, (0, (-a.shape[0]) % LANES)))
    return jnp.concatenate(flat).reshape(-1, LANES)


def _unpack(packed, shapes):
    flat = packed.reshape(-1)
    out, off = [], 0
    for shp in shapes:
        n = int(np.prod(shp))
        out.append(flat[off:off + n].reshape(shp))
        off += n + (-n) % LANES
    return out


MESH = pl.DeviceIdType.MESH
HBM = pl.BlockSpec(memory_space=pltpu.HBM)


def _all_gather(shards):
    n = len(shards)

    def body(*refs):
        x_refs, out_refs, token_ref = refs[:n], refs[n:2 * n], refs[2 * n]
        send_sems, recv_sems, local_sems = refs[2 * n + 1:]
        x, y, c = lax.axis_index("x"), lax.axis_index("y"), lax.axis_index("c")
        me, sibling = (x, y, c), (x, y, 1 - c)
        chips = [(1 - x, y), (x, 1 - y), (1 - x, 1 - y)]
        token_ref[...] = jnp.zeros_like(token_ref)

        def slot(a, px, py, pc):
            return out_refs[a].at[4 * px + 2 * py + pc]

        def copy(a, k, block, to, src=None):
            return pltpu.make_async_remote_copy(
                src_ref=slot(a, *block) if src is None else src, dst_ref=slot(a, *block),
                send_sem=send_sems.at[7 * a + k], recv_sem=recv_sems.at[7 * a + k], device_id=to, device_id_type=MESH)

        arrays = range(n)
        mine = [pltpu.make_async_copy(x_refs[a], slot(a, *me), local_sems.at[a]) for a in arrays]
        for cp in mine:
            cp.start()
        first = [copy(a, 0, me, sibling, src=x_refs[a]) for a in arrays]
        first += [copy(a, 1 + j, me, (*chip, c), src=x_refs[a]) for j, chip in enumerate(chips) for a in arrays]
        for cp in first:
            cp.start()
        passed = []
        for j, chip in enumerate(chips):
            for a in arrays:
                copy(a, 1 + j, (*chip, c), me).wait_recv()
                passed.append(copy(a, 4 + j, (*chip, c), sibling))
                passed[-1].start()
        for a in arrays:
            copy(a, 0, sibling, me).wait_recv()
        for j, chip in enumerate(chips):
            for a in arrays:
                copy(a, 4 + j, (*chip, 1 - c), me).wait_recv()
        for cp in first + passed:
            cp.wait_send()
        for cp in mine:
            cp.wait()

    res = pl.pallas_call(
        body, name="all_gather_weights",
        out_shape=[jax.ShapeDtypeStruct((N_DEV,) + s.shape, s.dtype) for s in shards]
        + [jax.ShapeDtypeStruct((8, LANES), F32)],
        in_specs=[HBM] * n, out_specs=[HBM] * n + [pl.BlockSpec(memory_space=pltpu.VMEM)],
        scratch_shapes=[pltpu.SemaphoreType.DMA((7 * n,)), pltpu.SemaphoreType.DMA((7 * n,)),
                        pltpu.SemaphoreType.DMA((n,))],
    )(*shards)
    return res[:n], res[n]


SEM = pl.BlockSpec(memory_space=pltpu.SEMAPHORE)
DATAFLOW = pltpu.SideEffectType.DATAFLOW_SIDE_EFFECTING


def _peers():
    x, y, c = lax.axis_index("x"), lax.axis_index("y"), lax.axis_index("c")
    out = []
    for k in range(1, N_DEV):
        px = 1 - x if k & 4 else x
        py = 1 - y if k & 2 else y
        pc = 1 - c if k & 1 else c
        out.append((k, (px, py, pc), 4 * px + 2 * py + pc))
    return 4 * x + 2 * y + c, out


def _send_start(srcs, per_peer, name):
    n = len(srcs)
    lands = [lax.empty((N_DEV,) + (s.shape[1:] if per_peer else s.shape), s.dtype) for s in srcs]

    def body(*refs):
        src_refs, land_refs, send_sems, recv_sems, token = refs[:n], refs[n:2 * n], refs[2 * n], refs[2 * n + 1], refs[-1]
        me, peers = _peers()
        for a in range(n):
            for k, pid, pflat in peers:
                pltpu.make_async_remote_copy(
                    src_ref=src_refs[a].at[pflat] if per_peer else src_refs[a], dst_ref=land_refs[a].at[me],
                    send_sem=send_sems.at[7 * a + k - 1], recv_sem=recv_sems.at[7 * a + k - 1],
                    device_id=pid, device_id_type=MESH).start()
        token[...] = jnp.zeros_like(token)

    hbm = lambda a: pltpu.with_memory_space_constraint(a, pltpu.HBM)
    res = pl.pallas_call(
        body, name=name,
        out_shape=(pltpu.SemaphoreType.DMA((7 * n,)), pltpu.SemaphoreType.DMA((7 * n,)),
                   *[pltpu.HBM(a.shape, a.dtype) for a in srcs + lands], jax.ShapeDtypeStruct((8, LANES), F32)),
        in_specs=(HBM,) * (2 * n), out_specs=(SEM, SEM) + (HBM,) * (2 * n) + (pl.BlockSpec(memory_space=pltpu.VMEM),),
        input_output_aliases={i: 2 + i for i in range(2 * n)},
        compiler_params=pltpu.CompilerParams(has_side_effects=DATAFLOW),
    )(*[hbm(a) for a in srcs + lands])
    return (res[0], res[1], list(res[2:2 + n]), list(res[2 + n:2 + 2 * n])), res[-1]


def _send_wait(started, after, per_peer, name):
    send_sems, recv_sems, srcs_thru, lands_thru = started
    n = len(srcs_thru)

    def body(*refs):
        src_refs, land_refs, send_sems, recv_sems = refs[:n], refs[n:2 * n], refs[2 * n], refs[2 * n + 1]
        me, peers = _peers()
        for a in range(n):
            for k, pid, pflat in peers:
                copy = pltpu.make_async_remote_copy(
                    src_ref=src_refs[a].at[pflat] if per_peer else src_refs[a], dst_ref=land_refs[a].at[pflat],
                    send_sem=send_sems.at[7 * a + k - 1], recv_sem=recv_sems.at[7 * a + k - 1],
                    device_id=pid, device_id_type=MESH)
                copy.wait_send()
                copy.wait_recv()

    outs = pl.pallas_call(
        body, name=name,
        out_shape=tuple(pltpu.HBM(a.shape, a.dtype) for a in srcs_thru + lands_thru),
        in_specs=(HBM,) * (2 * n) + (SEM, SEM, pl.BlockSpec(memory_space=pl.ANY)), out_specs=(HBM,) * (2 * n),
        input_output_aliases={i: i for i in range(2 * n)},
        compiler_params=pltpu.CompilerParams(has_side_effects=DATAFLOW),
    )(*srcs_thru, *lands_thru, send_sems, recv_sems, after)
    me = 4 * lax.axis_index("x") + 2 * lax.axis_index("y") + lax.axis_index("c")
    landed = []
    for src_out, land in zip(outs[:n], outs[n:]):
        own = lax.dynamic_index_in_dim(src_out, me, 0, keepdims=True) if per_peer else src_out[None]
        landed.append(lax.dynamic_update_slice(land, own, (me,) + (0,) * (land.ndim - 1)))
    return landed


def _share_rows(block, name):
    def body(src_ref, out_ref, send_sems, recv_sems, local_sem):
        me, peers = _peers()
        own = pltpu.make_async_copy(src_ref, out_ref.at[me], local_sem)
        own.start()
        copies = [pltpu.make_async_remote_copy(
            src_ref=src_ref, dst_ref=out_ref.at[me], send_sem=send_sems.at[k - 1], recv_sem=recv_sems.at[k - 1],
            device_id=pid, device_id_type=MESH) for k, pid, _ in peers]
        for cp in copies:
            cp.start()
        for cp in copies:
            cp.wait()
        own.wait()

    return pl.pallas_call(
        body, name=name, out_shape=jax.ShapeDtypeStruct((N_DEV,) + block.shape, block.dtype),
        in_specs=[HBM], out_specs=HBM,
        scratch_shapes=[pltpu.SemaphoreType.DMA((N_DEV - 1,)), pltpu.SemaphoreType.DMA((N_DEV - 1,)),
                        pltpu.SemaphoreType.DMA],
    )(block)


def _sum_slots(recv, name, tr):
    n, rows, lanes = recv.shape
    tr = _tile(rows, tr)

    def body(r_ref, o_ref):
        acc = r_ref[0].astype(F32)
        for i in range(1, n):
            acc = acc + r_ref[i].astype(F32)
        o_ref[...] = acc

    return pl.pallas_call(
        body, name=name, grid=(rows // tr,),
        in_specs=[pl.BlockSpec((n, tr, lanes), lambda i: (0, i, 0))],
        out_specs=pl.BlockSpec((tr, lanes), lambda i: (i, 0)),
        out_shape=jax.ShapeDtypeStruct((rows, lanes), F32),
        compiler_params=_cparams(("parallel",)),
    )(recv)


def _adamw_math(w, g, m, v):
    m = ADAM_B1 * m + (1.0 - ADAM_B1) * g
    v = ADAM_B2 * v + (1.0 - ADAM_B2) * (g * g)
    m_hat = m / (1.0 - ADAM_B1 ** ADAM_STEP)
    v_hat = v / (1.0 - ADAM_B2 ** ADAM_STEP)
    return -ADAM_LR * (m_hat / (jnp.sqrt(v_hat) + ADAM_EPS) + ADAM_WD * w), m, v


def _adamw(w, g, m, v, name, tr=256):
    return _rowwise(_adamw_math, name, tr, [w, g, m, v], [], [(w.shape[1], F32)] * 3)


def _adamw_small(ws, gs, ms, vs):
    n = len(ws)

    def body(*refs):
        ins, outs = refs[:4 * n], refs[4 * n:]
        for i in range(n):
            d, m, v = _adamw_math(ins[i][...], ins[n + i][...], ins[2 * n + i][...], ins[3 * n + i][...])
            outs[i][...], outs[n + i][...], outs[2 * n + i][...] = d, m, v

    vmem = pl.BlockSpec(memory_space=pltpu.VMEM)
    res = pl.pallas_call(
        body, name="adamw_small", in_specs=[vmem] * (4 * n), out_specs=[vmem] * (3 * n),
        out_shape=[jax.ShapeDtypeStruct(w.shape, F32) for w in ws] * 3,
    )(*ws, *gs, *ms, *vs)
    return res[:n], res[n:2 * n], res[2 * n:]


def _sum_adamw(recv, w, m, v, name):
    n, r, c = recv.shape
    tr = _tile(r, 256)

    def body(r_ref, w_ref, m_ref, v_ref, g_ref, d_ref, nm_ref, nv_ref):
        g = r_ref[0].astype(F32)
        for i in range(1, n):
            g = g + r_ref[i].astype(F32)
        g_ref[...] = g
        d_ref[...], nm_ref[...], nv_ref[...] = _adamw_math(w_ref[...], g, m_ref[...], v_ref[...])

    row = pl.BlockSpec((None, tr, c), lambda i: (0, i, 0))
    return pl.pallas_call(
        body, name=name, grid=(r // tr,),
        in_specs=[pl.BlockSpec((n, tr, c), lambda i: (0, i, 0)), row, row, row], out_specs=[row] * 4,
        out_shape=[jax.ShapeDtypeStruct((1, r, c), F32)] * 4, compiler_params=_cparams(("parallel",)),
    )(recv, w, m, v)


def kernel(x, mem, positions, ffn1_norm, ffn1_w_gu, ffn1_w_down, mix_norm, w_in, b_gate, sg_ln_g, sg_ln_b, sg_w, sg_b, mla_cq_norm, mla_w_uq, mla_ckv_norm, mla_w_ukv, mla_q_norm, mla_k_norm, mem_norm, mem_w_kv, mem_q_norm, mem_k_norm, w_branch_a, w_branch_b, w_branch_c, w_out, ffn2_norm, ffn2_w_gu, ffn2_w_down, loss_target, m_ffn1_norm, m_ffn1_w_gu, m_ffn1_w_down, m_mix_norm, m_w_in, m_b_gate, m_sg_ln_g, m_sg_ln_b, m_sg_w, m_sg_b, m_mla_cq_norm, m_mla_w_uq, m_mla_ckv_norm, m_mla_w_ukv, m_mla_q_norm, m_mla_k_norm, m_mem_norm, m_mem_w_kv, m_mem_q_norm, m_mem_k_norm, m_w_branch_a, m_w_branch_b, m_w_branch_c, m_w_out, m_ffn2_norm, m_ffn2_w_gu, m_ffn2_w_down, v_ffn1_norm, v_ffn1_w_gu, v_ffn1_w_down, v_mix_norm, v_w_in, v_b_gate, v_sg_ln_g, v_sg_ln_b, v_sg_w, v_sg_b, v_mla_cq_norm, v_mla_w_uq, v_mla_ckv_norm, v_mla_w_ukv, v_mla_q_norm, v_mla_k_norm, v_mem_norm, v_mem_w_kv, v_mem_q_norm, v_mem_k_norm, v_w_branch_a, v_w_branch_b, v_w_branch_c, v_w_out, v_ffn2_norm, v_ffn2_w_gu, v_ffn2_w_down):
    given = dict(locals())
    wts = {n: given[n] for n in ORDER}
    mom = {n: given["m_" + n] for n in ORDER}
    var = {n: given["v_" + n] for n in ORDER}

    def shards(group, zero):
        out = [wts[n][0].astype(BF16) for n in GROUPS[group]]
        return [out[0] + zero.astype(BF16)] + out[1:]

    def full_weights(group, slabs):
        return _compute_layout({n: _full_from_slabs(n, s) for n, s in zip(GROUPS[group], slabs)})

    def zero_of(a):
        return jnp.minimum(jnp.abs(a.reshape(-1)[0]), 0)

    gathered_ffn1, token = _all_gather([wts[n][0].astype(BF16) for n in GROUPS["ffn1"]])
    flight = {}
    flight["ffn1_down"], token = _send_start(shards("ffn1_down", token[0, 0]), False, "gather_ffn1_down_start")
    flight["mix"] = _send_start(shards("mix", token[0, 0]), False, "gather_mix_start")[0]
    recv = {}

    def weights(group, after):
        if group == "ffn1":
            return full_weights(group, gathered_ffn1)
        landed = _send_wait(flight.pop(group), after, False, f"gather_{group}_wait")
        if group == "mix":
            flight["ffn2"] = _send_start(shards("ffn2", zero_of(landed[0])), False, "gather_ffn2_start")[0]
        return full_weights(group, landed)

    small_shapes = [wts[n].shape[1:] for n in SMALL]
    early = SMALL[1:]
    assert SMALL[0] == "ffn1_norm"

    def grads_out(group, G):
        Gr = _reference_layout({n: G[n] for n in GRAD_GROUPS[group]})
        parts = [_slabs_from_full(n, Gr[n]).astype(BF16) for n in GRAD_GROUPS[group]]
        flight["g_" + group], tie = _send_start(parts, True, f"grads_{group}_start")
        if group == "mix":
            small = _pack([G[n].reshape(s) for n, s in zip(early, small_shapes[1:])])
            small = jnp.pad(small, ((0, (-small.shape[0]) % 8), (0, 0)))
            flight["small"], tie = _send_start([small + tie[0, 0]], False, "grads_small_start")
        return tie

    P = {n: wts[n] if wts[n].ndim == 2 else wts[n][0] for n in SMALL}
    loss_part, grad_x, G = _local_step(x[0], mem[0], positions[0], loss_target[0], P, weights, grads_out)

    for group, names in GRAD_GROUPS.items():
        recv.update(zip(names, _send_wait(flight.pop("g_" + group), grad_x, True, f"grads_{group}_wait")))
    early_recv, = _send_wait(flight.pop("small"), grad_x, False, "grads_small_wait")
    last = _share_rows(G["ffn1_norm"].reshape(-1, LANES), "share_ffn1_norm")
    g_small_packed = _sum_slots(jnp.concatenate([last, early_recv], axis=1), "sum_small", 2048)

    grads, delta, new_m, new_v = {}, {}, {}, {}
    for n in SHARDED:
        grads[n], delta[n], new_m[n], new_v[n] = _sum_adamw(recv[n], wts[n], mom[n], var[n], "adamw_" + n)
    grads.update(zip(SMALL, _unpack(g_small_packed, small_shapes)))

    flat2 = lambda d: [d[n].reshape(-1, d[n].shape[-1]) for n in SMALL]
    for dst, vals in zip((delta, new_m, new_v), _adamw_small(flat2(wts), flat2(grads), flat2(mom), flat2(var))):
        dst.update(zip(SMALL, vals))

    loss = lax.psum(jnp.sum(loss_part), ("x", "y", "c"))
    lead = lambda d: [d[n].reshape(wts[n].shape) for n in ORDER]
    return (loss, grad_x[None], *lead(grads), *lead(delta), *lead(new_m), *lead(new_v))
```

```python
import functools

import numpy as np
import jax
import jax.numpy as jnp
from jax import lax
from jax.experimental import pallas as pl
from jax.experimental.pallas import tpu as pltpu

F32, BF16 = jnp.float32, jnp.bfloat16

D_MODEL = 1024
SG_GROUPS, SG_GROUP_DIM, SG_WIDTH, CHUNK = 8, 64, 512, 128
MLA_HEADS, MLA_NOPE, MLA_ROPE, MLA_V, MLA_QK = 8, 64, 32, 64, 96
MLA_Q_RANK, MLA_KV_RANK = 384, 256
MEM_HEADS, MEM_HEAD_DIM, MEM_WIDTH = 4, 128, 512
D_FF = 2816
ROPE_BASE = 10000.0
EPS = 1e-6
NEG = -1e30
ADAM_LR, ADAM_B1, ADAM_B2, ADAM_EPS, ADAM_WD, ADAM_STEP = 0.001, 0.9, 0.999, 1e-08, 0.01, 10

N_DEV = 8
LANES = 128
V7X_VMEM_LIMIT = 56 * 1024 * 1024
HP = MLA_HEADS * LANES

Z_G, Z_U, Z_V, Z_QM, Z_CKV, Z_KR, Z_CQ = 0, 3072, 3584, 4096, 4608, 4864, 4992
Z_COLS = 5376
KR_LANE = 64


def _tile(dim, pref):
    if dim <= pref:
        return dim
    for t in range(pref - pref % LANES, LANES - 1, -LANES):
        if dim % t == 0:
            return t
    for t in range(pref - pref % 8, 7, -8):
        if dim % t == 0:
            return t
    return dim


def _cparams(sem):
    return pltpu.CompilerParams(dimension_semantics=sem, vmem_limit_bytes=V7X_VMEM_LIMIT)


_DN = {"nn": ((1,), (0,)), "nt": ((1,), (1,)), "tn": ((0,), (0,))}


def _dot(a, b, mode="nn"):
    return lax.dot_general(a.astype(BF16), b.astype(BF16), (_DN[mode], ((), ())),
                           preferred_element_type=F32)


def _mm(a, b, mode, out_dtype, name, tm=512, tn=512, tk=2048, tie=None):
    if mode == "tn":
        K, M = a.shape
    else:
        M, K = a.shape
    N = b.shape[0] if mode == "nt" else b.shape[1]
    tm, tn, tk = _tile(M, tm), _tile(N, tn), _tile(K, tk)
    nk = K // tk
    if mode == "tn":
        a_spec = pl.BlockSpec((tk, tm), lambda i, j, k: (k, i))
    else:
        a_spec = pl.BlockSpec((tm, tk), lambda i, j, k: (i, k))
    if mode == "nt":
        b_spec = pl.BlockSpec((tn, tk), lambda i, j, k: (j, k))
    else:
        b_spec = pl.BlockSpec((tk, tn), lambda i, j, k: (k, j))

    ties = [] if tie is None else [tie]

    def body(a_ref, b_ref, *rest):
        o_ref, *scratch = rest[len(ties):]
        p = _dot(a_ref[...], b_ref[...], mode)
        if nk == 1:
            o_ref[...] = p.astype(o_ref.dtype)
        else:
            acc_ref, = scratch
            k = pl.program_id(2)

            @pl.when(k == 0)
            def _():
                acc_ref[...] = p

            @pl.when(k > 0)
            def _():
                acc_ref[...] += p

            @pl.when(k == nk - 1)
            def _():
                o_ref[...] = acc_ref[...].astype(o_ref.dtype)

    return pl.pallas_call(
        body, name=name, grid=(M // tm, N // tn, nk),
        in_specs=[a_spec, b_spec] + [pl.BlockSpec(t.shape, lambda i, j, k: (0, 0)) for t in ties],
        out_specs=pl.BlockSpec((tm, tn), lambda i, j, k: (i, j)),
        out_shape=jax.ShapeDtypeStruct((M, N), out_dtype),
        scratch_shapes=[] if nk == 1 else [pltpu.VMEM((tm, tn), F32)],
        compiler_params=_cparams(("parallel", "parallel", "arbitrary")),
    )(a, b, *ties)


def _mm_t(at, b, name, tm, tn, tk=1024, tie=None):
    return _mm(at, b, "nn", BF16, name, tm=tm, tn=tn, tk=tk, tie=tie)


def _rowwise(fn, name, tr, row_ins, bc_ins, row_outs, acc_outs=()):
    norm = [it if isinstance(it, tuple) else (it, it.shape[1], 0) for it in row_ins]
    rows = norm[0][0].shape[0]
    tr = _tile(rows, tr)
    arrays, in_specs = [], []
    for arr, w, cb in norm:
        arrays.append(arr)
        in_specs.append(pl.BlockSpec((tr, w), lambda i, cb=cb: (i, cb)))
    for arr in bc_ins:
        arrays.append(arr)
        in_specs.append(pl.BlockSpec(arr.shape, lambda i, nd=arr.ndim: (0,) * nd))
    n_in, n_row = len(arrays), len(row_outs)
    out_shape, out_specs, aliases = [], [], {}
    transposed = [len(o) == 3 for o in row_outs]
    for k, o in enumerate(row_outs):
        if o[0] == "into":
            _, target, w, cb = o
            aliases[len(arrays)] = k
            arrays.append(target)
            in_specs.append(pl.BlockSpec(memory_space=pl.ANY))
            out_shape.append(jax.ShapeDtypeStruct(target.shape, target.dtype))
            out_specs.append(pl.BlockSpec((tr, w), lambda i, cb=cb: (i, cb)))
        elif transposed[k]:
            out_shape.append(jax.ShapeDtypeStruct((o[0], rows), o[1]))
            out_specs.append(pl.BlockSpec((o[0], tr), lambda i: (0, i)))
        else:
            out_shape.append(jax.ShapeDtypeStruct((rows, o[0]), o[1]))
            out_specs.append(pl.BlockSpec((tr, o[0]), lambda i: (i, 0)))
    for shp, dt in acc_outs:
        out_shape.append(jax.ShapeDtypeStruct(shp, dt))
        out_specs.append(pl.BlockSpec(shp, lambda i, nd=len(shp): (0,) * nd))

    def body(*refs):
        vals = fn(*[r[...].astype(F32) for r in refs[:n_in]])
        if not isinstance(vals, (tuple, list)):
            vals = (vals,)
        outs = refs[len(arrays):]
        for r, v, t in zip(outs[:n_row], vals[:n_row], transposed):
            r[...] = v.astype(F32).T.astype(r.dtype) if t else v.astype(r.dtype)
        if acc_outs:
            accs = list(zip(outs[n_row:], vals[n_row:]))
            i = pl.program_id(0)

            @pl.when(i == 0)
            def _():
                for r, v in accs:
                    r[...] = v.astype(r.dtype)

            @pl.when(i > 0)
            def _():
                for r, v in accs:
                    r[...] += v.astype(r.dtype)

    res = pl.pallas_call(
        body, name=name, grid=(rows // tr,), in_specs=in_specs, out_specs=out_specs,
        out_shape=out_shape, input_output_aliases=aliases, compiler_params=_cparams(("arbitrary",)),
    )(*arrays)
    return res


def _rsum(x):
    return jnp.sum(x, axis=0, keepdims=True)


def _rms(x, g, n=None):
    n = x.shape[-1] if n is None else n
    r = lax.rsqrt(jnp.sum(x * x, axis=-1, keepdims=True) * (1.0 / n) + EPS)
    return x * r * g


def _rms_bwd(x, g, dy, n=None):
    n = x.shape[-1] if n is None else n
    r = lax.rsqrt(jnp.sum(x * x, axis=-1, keepdims=True) * (1.0 / n) + EPS)
    xh = x * r
    dxh = dy * g
    dx = r * (dxh - xh * (jnp.sum(dxh * xh, axis=-1, keepdims=True) * (1.0 / n)))
    return dx, _rsum(dy * xh)


def _gelu(x):
    return 0.5 * x * (1.0 + lax.erf(x * 0.7071067811865476))


def _gelu_grad(x):
    return 0.5 * (1.0 + lax.erf(x * 0.7071067811865476)) + x * jnp.exp(-0.5 * x * x) * 0.3989422804014327


def _sigmoid(x):
    return 0.5 * jnp.tanh(0.5 * x) + 0.5


FFN_TM, FFN_TN = 1024, 1408
MXU_WIDTH = 256


def _col_chunks(n):
    return [(c, min(c + MXU_WIDTH, n)) for c in range(0, n, MXU_WIDTH)]


def _ffn_gu_act(h, w_gu, tag):
    T = h.shape[0]
    tm, tn = _tile(T, FFN_TM), FFN_TN
    nj = D_FF // tn

    def body(h_ref, wg_ref, wu_ref, gu_ref, a_ref, at_ref):
        h = h_ref[...]
        for c0, c1 in _col_chunks(tn):
            g = _dot(h, wg_ref[:, c0:c1])
            u = _dot(h, wu_ref[:, c0:c1])
            gu_ref[0, :, c0:c1] = g.astype(BF16)
            gu_ref[1, :, c0:c1] = u.astype(BF16)
            a = g * _sigmoid(g) * u
            a_ref[:, c0:c1] = a.astype(BF16)
            at_ref[c0:c1, :] = a.T.astype(BF16)

    return pl.pallas_call(
        body, name=f"{tag}_gu_act", grid=(T // tm, nj),
        in_specs=[pl.BlockSpec((tm, D_MODEL), lambda i, j: (i, 0)),
                  pl.BlockSpec((D_MODEL, tn), lambda i, j: (0, j)),
                  pl.BlockSpec((D_MODEL, tn), lambda i, j: (0, j + nj))],
        out_specs=[pl.BlockSpec((2, tm, tn), lambda i, j: (0, i, j)),
                   pl.BlockSpec((tm, tn), lambda i, j: (i, j)),
                   pl.BlockSpec((tn, tm), lambda i, j: (j, i))],
        out_shape=[jax.ShapeDtypeStruct((2, T, D_FF), BF16), jax.ShapeDtypeStruct((T, D_FF), BF16),
                   jax.ShapeDtypeStruct((D_FF, T), BF16)],
        compiler_params=_cparams(("parallel", "parallel")),
    )(h, w_gu, w_gu)


def _ffn_da_actbwd(do, w_down, gu, tag, tie=None):
    T = do.shape[0]
    tm, tn = _tile(T, FFN_TM), FFN_TN
    ties = [] if tie is None else [tie]

    def body(do_ref, wd_ref, gu_ref, *rest):
        dgu_ref = rest[-1]
        do = do_ref[...]
        for c0, c1 in _col_chunks(tn):
            da = _dot(do, wd_ref[c0:c1, :], "nt")
            g = gu_ref[0, :, c0:c1].astype(F32)
            u = gu_ref[1, :, c0:c1].astype(F32)
            s = _sigmoid(g)
            dgu_ref[0, :, c0:c1] = (da * u * s * (1.0 + g * (1.0 - s))).astype(BF16)
            dgu_ref[1, :, c0:c1] = (da * g * s).astype(BF16)

    return pl.pallas_call(
        body, name=f"{tag}_da_actbwd", grid=(T // tm, D_FF // tn),
        in_specs=[pl.BlockSpec((tm, D_MODEL), lambda i, j: (i, 0)),
                  pl.BlockSpec((tn, D_MODEL), lambda i, j: (j, 0)),
                  pl.BlockSpec((2, tm, tn), lambda i, j: (0, i, j))]
        + [pl.BlockSpec(t.shape, lambda i, j: (0, 0)) for t in ties],
        out_specs=pl.BlockSpec((2, tm, tn), lambda i, j: (0, i, j)),
        out_shape=jax.ShapeDtypeStruct((2, T, D_FF), BF16),
        compiler_params=_cparams(("parallel", "parallel")),
    )(do, w_down, gu, *ties)


def _ffn_dwgu(ht, dgu, tag, tk=2048):
    T = ht.shape[1]
    tn, tk = FFN_TN, _tile(T, tk)
    nj, nk = D_FF // tn, T // tk

    def body(a_ref, b_ref, o_ref, acc_ref):
        k = pl.program_id(1)
        p = _dot(a_ref[...], b_ref[...])

        @pl.when(k == 0)
        def _():
            acc_ref[...] = p

        @pl.when(k > 0)
        def _():
            acc_ref[...] += p

        @pl.when(k == nk - 1)
        def _():
            o_ref[...] = acc_ref[...].astype(o_ref.dtype)

    return pl.pallas_call(
        body, name=f"{tag}_dwgu", grid=(2 * nj, nk),
        in_specs=[pl.BlockSpec((D_MODEL, tk), lambda n, k: (0, k)),
                  pl.BlockSpec((None, tk, tn), lambda n, k: (n // nj, k, n % nj))],
        out_specs=pl.BlockSpec((D_MODEL, tn), lambda n, k: (0, n)),
        out_shape=jax.ShapeDtypeStruct((D_MODEL, 2 * D_FF), BF16),
        scratch_shapes=[pltpu.VMEM((D_MODEL, tn), F32)],
        compiler_params=_cparams(("parallel", "arbitrary")),
    )(ht, dgu)


def _ffn_dh(dgu, w_gu, tag, tm=2048, tie=None):
    T = dgu.shape[1]
    tm, tk = _tile(T, tm), FFN_TN
    nk = D_FF // tk
    ties = [] if tie is None else [tie]

    def body(a_ref, b_ref, *rest):
        o_ref, acc_ref = rest[len(ties):]
        k = pl.program_id(1)
        p = _dot(a_ref[...], b_ref[...], "nt")

        @pl.when(k == 0)
        def _():
            acc_ref[...] = p

        @pl.when(k > 0)
        def _():
            acc_ref[...] += p

        @pl.when(k == 2 * nk - 1)
        def _():
            o_ref[...] = acc_ref[...].astype(o_ref.dtype)

    return pl.pallas_call(
        body, name=f"{tag}_dh", grid=(T // tm, 2 * nk),
        in_specs=[pl.BlockSpec((None, tm, tk), lambda i, k: (k // nk, i, k % nk)),
                  pl.BlockSpec((D_MODEL, tk), lambda i, k: (0, k))]
        + [pl.BlockSpec(t.shape, lambda i, k: (0, 0)) for t in ties],
        out_specs=pl.BlockSpec((tm, D_MODEL), lambda i, k: (i, 0)),
        out_shape=jax.ShapeDtypeStruct((T, D_MODEL), BF16),
        scratch_shapes=[pltpu.VMEM((tm, D_MODEL), F32)],
        compiler_params=_cparams(("parallel", "arbitrary")),
    )(dgu, w_gu, *ties)


def _ffn_fwd(h, w_gu, w_down, tag):
    gu, a, at = _ffn_gu_act(h, w_gu, tag)
    if callable(w_down):
        w_down = w_down(at)
    o = _mm(a, w_down, "nn", BF16, f"{tag}_down", tm=1024, tn=1024, tk=2816)
    return gu, at, o


def _ffn_bwd(do, ht, gu, at, w_gu, w_down, tag, tie=None, on_dw=None):
    on_dw = on_dw or (lambda which, dw: None)
    dw_down = _mm_t(at, do, f"{tag}_dwdown", tm=1408, tn=1024, tk=2048, tie=tie)
    dgu = _ffn_da_actbwd(do, w_down, gu, tag, tie=on_dw("down", dw_down))
    dw_gu = _ffn_dwgu(ht, dgu, tag)
    dh = _ffn_dh(dgu, w_gu, tag, tie=on_dw("gu", dw_gu))
    return dh, dw_gu, dw_down


def _sg_common(u_pre, v_pre, ln_g, ln_b):
    u = _gelu(u_pre)
    v = _gelu(v_pre)
    mu = jnp.mean(v, axis=-1, keepdims=True)
    vc = v - mu
    rstd = lax.rsqrt(jnp.mean(vc * vc, axis=-1, keepdims=True) + EPS)
    vhat = vc * rstd
    vl = vhat * ln_g + ln_b
    return u, vhat, rstd, vl


def _sg_masked_pairs(w):
    t = lax.broadcasted_iota(jnp.int32, (CHUNK, CHUNK), 0)
    s = lax.broadcasted_iota(jnp.int32, (CHUNK, CHUNK), 1)
    causal = s <= t
    wm = [jnp.where(causal, w[g], 0.0).astype(BF16) for g in range(SG_GROUPS)]
    return [jnp.concatenate([wm[2 * j], wm[2 * j + 1]], axis=0) for j in range(SG_GROUPS // 2)], causal


def _sg_mix(vl, pairs, bias):
    tr = vl.shape[0]
    low = lax.broadcasted_iota(jnp.int32, (CHUNK, LANES), 1) < SG_GROUP_DIM
    vb = vl.astype(BF16)
    rows = []
    for c in range(tr // CHUNK):
        slabs = []
        for j in range(SG_GROUPS // 2):
            slab = vb[c * CHUNK:(c + 1) * CHUNK, j * LANES:(j + 1) * LANES]
            m = _dot(pairs[j], slab)
            slabs.append(jnp.where(low, m[:CHUNK], m[CHUNK:]))
        rows.append(jnp.concatenate(slabs, axis=1) + bias)
    return jnp.concatenate(rows, axis=0)


def _sg_fwd(z, ln_g, ln_b, sg_w, bias_full):
    def fn(u_pre, v_pre, ln_g, ln_b, w, bias):
        u, _, _, vl = _sg_common(u_pre, v_pre, ln_g, ln_b)
        pairs, _ = _sg_masked_pairs(w)
        y = u * _sg_mix(vl, pairs, bias)
        return y, y

    return _rowwise(fn, "sg_fwd", 512, [(z, SG_WIDTH, Z_U // SG_WIDTH), (z, SG_WIDTH, Z_V // SG_WIDTH)],
                    [ln_g, ln_b, sg_w, bias_full], [(SG_WIDTH, BF16), (SG_WIDTH, BF16, "T")])


def _sg_bwd(z, dy, ln_g, ln_b, sg_w, bias_full, group_ind, dz):
    def fn(u_pre, v_pre, dy, ln_g, ln_b, w, bias, ind):
        dy = dy.astype(F32)
        u, vhat, rstd, vl = _sg_common(u_pre, v_pre, ln_g, ln_b)
        pairs, causal = _sg_masked_pairs(w)
        mixed = _sg_mix(vl, pairs, bias)
        du_pre = dy * mixed * _gelu_grad(u_pre)
        dmix = dy * u
        tr = dy.shape[0]
        low = lax.broadcasted_iota(jnp.int32, (CHUNK, LANES), 1) < SG_GROUP_DIM
        vb = vl.astype(BF16)
        dw = [jnp.zeros((CHUNK, CHUNK), F32) for _ in range(SG_GROUPS)]
        dbias = jnp.zeros((CHUNK, SG_WIDTH), F32)
        dvl_rows = []
        for c in range(tr // CHUNK):
            dm_c = dmix[c * CHUNK:(c + 1) * CHUNK]
            dbias = dbias + dm_c
            slabs = []
            for j in range(SG_GROUPS // 2):
                slab = vb[c * CHUNK:(c + 1) * CHUNK, j * LANES:(j + 1) * LANES]
                dm = dm_c[:, j * LANES:(j + 1) * LANES]
                d0 = jnp.where(low, dm, 0.0).astype(BF16)
                d1 = jnp.where(low, 0.0, dm).astype(BF16)
                dw[2 * j] = dw[2 * j] + _dot(d0, slab, "nt")
                dw[2 * j + 1] = dw[2 * j + 1] + _dot(d1, slab, "nt")
                slabs.append(_dot(pairs[j], jnp.concatenate([d0, d1], axis=0), "tn"))
            dvl_rows.append(jnp.concatenate(slabs, axis=1))
        dvl = jnp.concatenate(dvl_rows, axis=0)
        dln_g = _rsum(dvl * vhat)
        dln_b = _rsum(dvl)
        dvh = dvl * ln_g
        dv = rstd * (dvh - jnp.mean(dvh, axis=-1, keepdims=True)
                     - vhat * jnp.mean(dvh * vhat, axis=-1, keepdims=True))
        dv_pre = dv * _gelu_grad(v_pre)
        dw = jnp.stack([jnp.where(causal, d, 0.0) for d in dw], axis=0)
        dbias_t = lax.dot_general(dbias, ind, (((1,), (0,)), ((), ())), precision=lax.Precision.HIGHEST,
                                  preferred_element_type=F32)
        return jnp.concatenate([du_pre, dv_pre], axis=1), dw, dbias_t, dln_g, dln_b

    return _rowwise(fn, "sg_bwd", 512,
                    [(z, SG_WIDTH, Z_U // SG_WIDTH), (z, SG_WIDTH, Z_V // SG_WIDTH), dy],
                    [ln_g, ln_b, sg_w, bias_full, group_ind],
                    [("into", dz, 2 * SG_WIDTH, Z_U // (2 * SG_WIDTH))],
                    [((SG_GROUPS, CHUNK, CHUNK), F32), ((CHUNK, SG_GROUPS), F32), ((1, SG_WIDTH), F32), ((1, SG_WIDTH), F32)])


MLA_POST_ROWS = 128


def _rope(x, c, s1, s2):
    return x * c + pltpu.roll(x, LANES - MLA_ROPE // 2, 1) * s1 + pltpu.roll(x, MLA_ROPE // 2, 1) * s2


def _rope_t(d, c, s1, s2):
    return d * c + pltpu.roll(d * s1, MLA_ROPE // 2, 1) + pltpu.roll(d * s2, LANES - MLA_ROPE // 2, 1)


def _mla_post(q_pre, kv_pre, z, tabs, gq, gk):
    scale = MLA_QK ** -0.5 * LOG2E
    T = q_pre.shape[0]
    tr = _tile(T, MLA_POST_ROWS)

    def body(q_ref, k_ref, v_ref, kr_ref, c_ref, s1_ref, s2_ref, gq_ref, gk_ref, qo_ref, ko_ref, vo_ref):
        kr = kr_ref[...].astype(F32)
        c, s1, s2, gq, gk = c_ref[...], s1_ref[...], s2_ref[...], gq_ref[...], gk_ref[...]
        ones_lane = lax.broadcasted_iota(jnp.int32, (tr, LANES), 1) == ONES_LANE
        for h in range(MLA_HEADS):
            sl = slice(h * LANES, (h + 1) * LANES)
            qo_ref[:, sl] = (_rope(_rms(q_ref[:, sl].astype(F32), gq, MLA_QK), c, s1, s2) * scale).astype(BF16)
            ko_ref[:, sl] = _rope(_rms(k_ref[:, sl].astype(F32) + kr, gk, MLA_QK), c, s1, s2).astype(BF16)
            vo_ref[:, sl] = jnp.where(ones_lane, 1.0, v_ref[:, sl].astype(F32)).astype(BF16)

    wide = lambda cb: pl.BlockSpec((tr, HP), lambda i, cb=cb: (i, cb))
    lanes = lambda cb: pl.BlockSpec((tr, LANES), lambda i, cb=cb: (i, cb))
    gain = pl.BlockSpec((1, LANES), lambda i: (0, 0))
    return pl.pallas_call(
        body, name="mla_post", grid=(T // tr,),
        in_specs=[wide(0), wide(0), wide(1), lanes(Z_KR // LANES), lanes(0), lanes(0), lanes(0), gain, gain],
        out_specs=[wide(0)] * 3, out_shape=[jax.ShapeDtypeStruct((T, HP), BF16)] * 3,
        compiler_params=_cparams(("parallel",)),
    )(q_pre, kv_pre, kv_pre, z, *tabs, gq, gk)


def _mla_post_bwd(q_pre, kv_pre, z, tabs, gq, gk, dq, dk, dv):
    scale = MLA_QK ** -0.5
    T = q_pre.shape[0]
    tr = _tile(T, MLA_POST_ROWS)

    def body(q_ref, k_ref, kr_ref, c_ref, s1_ref, s2_ref, dq_ref, dk_ref, dv_ref, gq_ref, gk_ref,
             dqo_ref, dkvo_ref, dkro_ref, dgq_ref, dgk_ref):
        kr = kr_ref[...].astype(F32)
        c, s1, s2, gq, gk = c_ref[...], s1_ref[...], s2_ref[...], gq_ref[...], gk_ref[...]
        lane = lax.broadcasted_iota(jnp.int32, (1, LANES), 1)
        kr_mask = (lane >= KR_LANE) & (lane < KR_LANE + MLA_ROPE)
        dgq = jnp.zeros((1, LANES), F32)
        dgk = jnp.zeros((1, LANES), F32)
        dkr = jnp.zeros((tr, LANES), F32)
        for h in range(MLA_HEADS):
            sl = slice(h * LANES, (h + 1) * LANES)
            dqn = _rope_t(dq_ref[:, sl].astype(F32), c, s1, s2) * scale
            dx, dg = _rms_bwd(q_ref[:, sl].astype(F32), gq, dqn, MLA_QK)
            dqo_ref[:, sl] = dx.astype(BF16)
            dgq = dgq + dg
            dkn = _rope_t(dk_ref[:, sl].astype(F32), c, s1, s2)
            dx, dg = _rms_bwd(k_ref[:, sl].astype(F32) + kr, gk, dkn, MLA_QK)
            dkvo_ref[:, sl] = dx.astype(BF16)
            dkvo_ref[:, HP + h * LANES:HP + (h + 1) * LANES] = dv_ref[:, sl]
            dgk = dgk + dg
            dkr = dkr + dx
        dkro_ref[...] = jnp.where(kr_mask, dkr, 0.0).astype(BF16)
        i = pl.program_id(0)

        @pl.when(i == 0)
        def _():
            dgq_ref[...] = dgq
            dgk_ref[...] = dgk

        @pl.when(i > 0)
        def _():
            dgq_ref[...] += dgq
            dgk_ref[...] += dgk

    wide = lambda cb: pl.BlockSpec((tr, HP), lambda i, cb=cb: (i, cb))
    lanes = lambda cb: pl.BlockSpec((tr, LANES), lambda i, cb=cb: (i, cb))
    gain = pl.BlockSpec((1, LANES), lambda i: (0, 0))
    return pl.pallas_call(
        body, name="mla_post_bwd", grid=(T // tr,),
        in_specs=[wide(0), wide(0), lanes(Z_KR // LANES), lanes(0), lanes(0), lanes(0), wide(0), wide(0), wide(0),
                  gain, gain],
        out_specs=[wide(0), pl.BlockSpec((tr, 2 * HP), lambda i: (i, 0)), lanes(0), gain, gain],
        out_shape=[jax.ShapeDtypeStruct((T, HP), BF16), jax.ShapeDtypeStruct((T, 2 * HP), BF16),
                   jax.ShapeDtypeStruct((T, LANES), BF16), jax.ShapeDtypeStruct((1, LANES), F32),
                   jax.ShapeDtypeStruct((1, LANES), F32)],
        compiler_params=_cparams(("arbitrary",)),
    )(q_pre, kv_pre, z, *tabs, dq, dk, dv, gq, gk)


def _pairs(n, lower):
    a, b = [], []
    for o in range(n):
        inner = range(o + 1) if lower else range(o, n)
        for t in inner:
            a.append(o)
            b.append(t)
    return jnp.asarray(np.array(a, np.int32)), jnp.asarray(np.array(b, np.int32))


FLASH_TILE, FLASH_SUB_ROWS = 2048, 512
LOG2E, LN2 = 1.4426950408889634, 0.6931471805599453
ONES_LANE = MLA_V


def _flash_tiles(T):
    tq = _tile(T, FLASH_TILE)
    return tq, _tile(tq, FLASH_SUB_ROWS)


def _col_span(t, sr, rb, diag, key_major):
    if not diag:
        return 0, t
    return (rb * sr, t) if key_major else (0, (rb + 1) * sr)


def _span_iota(sr, rb, c0, c1):
    r = lax.broadcasted_iota(jnp.int32, (sr, c1 - c0), 0) + rb * sr
    c = lax.broadcasted_iota(jnp.int32, (sr, c1 - c0), 1) + c0
    return r, c


def _lanes(x, width):
    return jnp.concatenate([x] * (width // LANES), axis=1)


def _flash_fwd(q, k, v):
    T = q.shape[0]
    tq, sr = _flash_tiles(T)
    n = T // tq
    ii, jj = _pairs(n, True)

    def body(ii_ref, jj_ref, q_ref, k_ref, v_ref, o_ref, ot_ref, lse_ref, lset_ref, m_sc, acc_sc):
        p_ = pl.program_id(1)
        i, j = ii_ref[p_], jj_ref[p_]

        @pl.when(j == 0)
        def _():
            m_sc[...] = jnp.full(m_sc.shape, NEG, F32)
            acc_sc[...] = jnp.zeros(acc_sc.shape, F32)

        def tile(diag):
            nrb = tq // sr

            def scores(rb):
                c0, c1 = _col_span(tq, sr, rb, diag, False)
                return _dot(q_ref[rb * sr:(rb + 1) * sr, :], k_ref[c0:c1, :], "nt")

            s_next = scores(0)
            for rb in range(nrb):
                rows = slice(rb * sr, (rb + 1) * sr)
                c0, c1 = _col_span(tq, sr, rb, diag, False)
                s, s_next = s_next, (scores(rb + 1) if rb + 1 < nrb else None)
                if diag:
                    r, c = _span_iota(sr, rb, c0, c1)
                    s = jnp.where(c <= r, s, NEG)
                m = m_sc[rows, :]
                m_new = jnp.maximum(m, jnp.max(s, axis=1, keepdims=True))
                p = jnp.exp2(s - _lanes(m_new, c1 - c0))
                acc_sc[rows, :] = jnp.exp2(m - m_new) * acc_sc[rows, :] + _dot(p, v_ref[c0:c1, :])
                m_sc[rows, :] = m_new

        @pl.when(j < i)
        def _():
            tile(False)

        @pl.when(j == i)
        def _():
            tile(True)
            acc = acc_sc[...]
            lane = lax.broadcasted_iota(jnp.int32, acc.shape, 1)
            l = jnp.sum(jnp.where(lane == ONES_LANE, acc, 0.0), axis=1, keepdims=True)
            o = jnp.where(lane < MLA_V, acc / l, 0.0)
            o_ref[...] = o.astype(o_ref.dtype)
            ot_ref[...] = o.T.astype(ot_ref.dtype)
            lse = m_sc[...] + jnp.log2(l)
            lse_ref[...] = lse
            lset_ref[...] = lse.T[:8]

    blk = lambda which: pl.BlockSpec((tq, LANES), which)
    qmap = lambda h, p, ii, jj: (ii[p], h)
    kmap = lambda h, p, ii, jj: (jj[p], h)
    tmap = lambda h, p, ii, jj: (h, ii[p])
    return pl.pallas_call(
        body, name="mla_flash_fwd",
        grid_spec=pltpu.PrefetchScalarGridSpec(
            num_scalar_prefetch=2, grid=(MLA_HEADS, int(ii.shape[0])),
            in_specs=[blk(qmap), blk(kmap), blk(kmap)],
            out_specs=[blk(qmap), pl.BlockSpec((LANES, tq), tmap), blk(qmap), pl.BlockSpec((8, tq), tmap)],
            scratch_shapes=[pltpu.VMEM((tq, LANES), F32)] * 2),
        out_shape=[jax.ShapeDtypeStruct((T, HP), BF16), jax.ShapeDtypeStruct((HP, T), BF16),
                   jax.ShapeDtypeStruct((T, HP), F32), jax.ShapeDtypeStruct((8 * MLA_HEADS, T), F32)],
        compiler_params=_cparams(("parallel", "arbitrary")),
    )(ii, jj, q, k, v)


def _flash_dq(q, k, v, do, lse, delta):
    T = q.shape[0]
    tq, sr = _flash_tiles(T)
    n = T // tq
    ii, jj = _pairs(n, True)

    def body(ii_ref, jj_ref, q_ref, k_ref, v_ref, do_ref, lse_ref, dl_ref, dq_ref, acc_sc):
        p_ = pl.program_id(1)
        i, j = ii_ref[p_], jj_ref[p_]

        @pl.when(j == 0)
        def _():
            acc_sc[...] = jnp.zeros(acc_sc.shape, F32)

        def tile(diag):
            nrb = tq // sr

            def products(rb):
                rows = slice(rb * sr, (rb + 1) * sr)
                c0, c1 = _col_span(tq, sr, rb, diag, False)
                return _dot(q_ref[rows, :], k_ref[c0:c1, :], "nt"), _dot(do_ref[rows, :], v_ref[c0:c1, :], "nt")

            nxt = products(0)
            for rb in range(nrb):
                rows = slice(rb * sr, (rb + 1) * sr)
                c0, c1 = _col_span(tq, sr, rb, diag, False)
                (s, dp), nxt = nxt, (products(rb + 1) if rb + 1 < nrb else None)
                p = jnp.exp2(s - _lanes(lse_ref[rows, :], c1 - c0))
                if diag:
                    r, c = _span_iota(sr, rb, c0, c1)
                    p = jnp.where(c <= r, p, 0.0)
                acc_sc[rows, :] += _dot(p * (dp - _lanes(dl_ref[rows, :], c1 - c0)), k_ref[c0:c1, :])

        @pl.when(j < i)
        def _():
            tile(False)

        @pl.when(j == i)
        def _():
            tile(True)
            dq_ref[...] = acc_sc[...].astype(dq_ref.dtype)

    blk = lambda which: pl.BlockSpec((tq, LANES), which)
    qmap = lambda h, p, ii, jj: (ii[p], h)
    kmap = lambda h, p, ii, jj: (jj[p], h)
    return pl.pallas_call(
        body, name="mla_flash_dq",
        grid_spec=pltpu.PrefetchScalarGridSpec(
            num_scalar_prefetch=2, grid=(MLA_HEADS, int(ii.shape[0])),
            in_specs=[blk(qmap), blk(kmap), blk(kmap), blk(qmap), blk(qmap), blk(qmap)],
            out_specs=blk(qmap),
            scratch_shapes=[pltpu.VMEM((tq, LANES), F32)]),
        out_shape=jax.ShapeDtypeStruct((T, HP), BF16),
        compiler_params=_cparams(("parallel", "arbitrary")),
    )(ii, jj, q, k, v, do, lse, delta)


def _flash_dkv(q, k, v, do, lse_t, delta_t):
    T = q.shape[0]
    tq, sr = _flash_tiles(T)
    n = T // tq
    jj, ii = _pairs(n, False)

    def body(jj_ref, ii_ref, q_ref, k_ref, v_ref, do_ref, lse_ref, dl_ref, dk_ref, dv_ref, dk_sc, dv_sc):
        p_ = pl.program_id(1)
        j, i = jj_ref[p_], ii_ref[p_]

        @pl.when(i == j)
        def _():
            dk_sc[...] = jnp.zeros(dk_sc.shape, F32)
            dv_sc[...] = jnp.zeros(dv_sc.shape, F32)

        def tile(diag):
            nrb = tq // sr

            def products(rb):
                rows = slice(rb * sr, (rb + 1) * sr)
                c0, c1 = _col_span(tq, sr, rb, diag, True)
                return _dot(k_ref[rows, :], q_ref[c0:c1, :], "nt"), _dot(v_ref[rows, :], do_ref[c0:c1, :], "nt")

            nxt = products(0)
            for rb in range(nrb):
                rows = slice(rb * sr, (rb + 1) * sr)
                c0, c1 = _col_span(tq, sr, rb, diag, True)
                (st, dpt), nxt = nxt, (products(rb + 1) if rb + 1 < nrb else None)
                pt = jnp.exp2(st - lse_ref[:1, c0:c1])
                if diag:
                    r, c = _span_iota(sr, rb, c0, c1)
                    pt = jnp.where(r <= c, pt, 0.0)
                dv_sc[rows, :] += _dot(pt, do_ref[c0:c1, :])
                dk_sc[rows, :] += _dot(pt * (dpt - dl_ref[:1, c0:c1]), q_ref[c0:c1, :])

        @pl.when(i == j)
        def _():
            tile(True)

        @pl.when(i > j)
        def _():
            tile(False)

        @pl.when(i == n - 1)
        def _():
            dk_ref[...] = (dk_sc[...] * LN2).astype(dk_ref.dtype)
            dv_ref[...] = dv_sc[...].astype(dv_ref.dtype)

    blk = lambda which: pl.BlockSpec((tq, LANES), which)
    qmap = lambda h, p, jj, ii: (ii[p], h)
    kmap = lambda h, p, jj, ii: (jj[p], h)
    lse_rows = pl.BlockSpec((8, tq), lambda h, p, jj, ii: (h, ii[p]))
    delta_rows = pl.BlockSpec((8, tq), lambda h, p, jj, ii: (h * (LANES // 8), ii[p]))
    return pl.pallas_call(
        body, name="mla_flash_dkv",
        grid_spec=pltpu.PrefetchScalarGridSpec(
            num_scalar_prefetch=2, grid=(MLA_HEADS, int(ii.shape[0])),
            in_specs=[blk(qmap), blk(kmap), blk(kmap), blk(qmap), lse_rows, delta_rows],
            out_specs=[blk(kmap), blk(kmap)],
            scratch_shapes=[pltpu.VMEM((tq, LANES), F32)] * 2),
        out_shape=[jax.ShapeDtypeStruct((T, HP), BF16)] * 2,
        compiler_params=_cparams(("parallel", "arbitrary")),
    )(jj, ii, q, k, v, do, lse_t, delta_t)


def _mem_fwd(z, km, vm, gq):
    scale = MEM_HEAD_DIM ** -0.5

    def fn(qm, km, vm, gq):
        ys = []
        for h in range(MEM_HEADS):
            sl = slice(h * LANES, (h + 1) * LANES)
            q = _rms(qm[:, sl], gq) * scale
            s = _dot(q, km[:, sl], "nt")
            p = jnp.exp(s - jnp.max(s, axis=1, keepdims=True))
            p = p / jnp.sum(p, axis=1, keepdims=True)
            ys.append(_dot(p, vm[:, sl]))
        y = jnp.concatenate(ys, axis=1)
        return y, y

    return _rowwise(fn, "mem_fwd", 512, [(z, MEM_WIDTH, Z_QM // MEM_WIDTH)], [km, vm, gq],
                    [(MEM_WIDTH, BF16), (MEM_WIDTH, BF16, "T")])


def _mem_bwd(z, dy, km, vm, gq, dz):
    scale = MEM_HEAD_DIM ** -0.5

    def fn(qm, dy, km, vm, gq):
        dqs, dks, dvs = [], [], []
        dgq = jnp.zeros((1, LANES), F32)
        for h in range(MEM_HEADS):
            sl = slice(h * LANES, (h + 1) * LANES)
            q = (_rms(qm[:, sl], gq) * scale).astype(BF16)
            dyh = dy[:, sl]
            kh, vh = km[:, sl], vm[:, sl]
            s = _dot(q, kh, "nt")
            p = jnp.exp(s - jnp.max(s, axis=1, keepdims=True))
            p = p / jnp.sum(p, axis=1, keepdims=True)
            dp = _dot(dyh, vh, "nt")
            ds = p * (dp - jnp.sum(p * dp, axis=1, keepdims=True))
            dq = _dot(ds, kh) * scale
            dx, dg = _rms_bwd(qm[:, sl], gq, dq)
            dqs.append(dx)
            dgq = dgq + dg
            st = _dot(kh, q, "nt")
            pt = jnp.exp(st - jnp.max(st, axis=0, keepdims=True))
            pt = pt / jnp.sum(pt, axis=0, keepdims=True)
            dpt = _dot(vh, dyh, "nt")
            dst = pt * (dpt - jnp.sum(pt * dpt, axis=0, keepdims=True))
            dvs.append(_dot(pt, dyh))
            dks.append(_dot(dst, q))
        return jnp.concatenate(dqs, axis=1), jnp.concatenate(dks, axis=1), jnp.concatenate(dvs, axis=1), dgq

    m = km.shape[0]
    return _rowwise(fn, "mem_bwd", 512, [(z, MEM_WIDTH, Z_QM // MEM_WIDTH), dy], [km, vm, gq],
                    [("into", dz, MEM_WIDTH, Z_QM // MEM_WIDTH)],
                    [((m, MEM_WIDTH), F32), ((m, MEM_WIDTH), F32), ((1, LANES), F32)])


GROUPS = {"ffn1": ["ffn1_w_gu"], "ffn1_down": ["ffn1_w_down"],
          "mix": ["w_in", "mla_w_uq", "mla_w_ukv", "mem_w_kv", "w_branch_a", "w_branch_b", "w_branch_c", "w_out"],
          "ffn2": ["ffn2_w_gu", "ffn2_w_down"]}
GRAD_GROUPS = {"ffn2": GROUPS["ffn2"], "mix": GROUPS["mix"], "ffn1_down": ["ffn1_w_down"], "ffn1_gu": ["ffn1_w_gu"]}


def _local_step(x, mem, positions, loss_target, P, weights, grads_out):
    T = x.shape[0]
    G = {}
    W = dict(weights("ffn1", None))

    half = MLA_ROPE // 2
    inv = ROPE_BASE ** (-jnp.arange(half, dtype=F32) / half)
    ang = positions.astype(F32)[:, None] * inv
    cos, sin = jnp.cos(ang), jnp.sin(ang)
    one, zero = jnp.ones((T, MLA_NOPE), F32), jnp.zeros((T, half), F32)
    pad = LANES - MLA_QK
    tabs = (jnp.concatenate([one, cos, cos, jnp.ones((T, pad), F32)], axis=1),
            jnp.concatenate([jnp.zeros((T, MLA_NOPE), F32), -sin, zero, jnp.zeros((T, pad), F32)], axis=1),
            jnp.concatenate([jnp.zeros((T, MLA_NOPE), F32), zero, sin, jnp.zeros((T, pad), F32)], axis=1))
    gq_p = jnp.pad(P["mla_q_norm"], ((0, 0), (0, pad)))
    gk_p = jnp.pad(P["mla_k_norm"], ((0, 0), (0, pad)))
    bias_full = jnp.repeat(P["sg_b"].T, SG_GROUP_DIM, axis=1)
    group_ind = jnp.repeat(jnp.eye(SG_GROUPS, dtype=F32), SG_GROUP_DIM, axis=0)

    HT = (D_MODEL, BF16, "T")

    def norm2(x, g):
        h = _rms(x, g)
        return h, h

    h1, h1t = _rowwise(norm2, "ffn1_norm", 512, [x], [P["ffn1_norm"]], [(D_MODEL, BF16), HT])
    def ffn1_w_down(after):
        W.update(weights("ffn1_down", after))
        return W["ffn1_w_down"]

    gu1, a1t, o1 = _ffn_fwd(h1, W["ffn1_w_gu"], ffn1_w_down, "ffn1")

    def resid_norm(x, o, g):
        xn = x + 0.5 * o
        h = _rms(xn, g)
        return xn, h, h

    x1, hm, hmt = _rowwise(resid_norm, "mix_norm", 512, [x, o1], [P["mix_norm"]],
                           [(D_MODEL, F32), (D_MODEL, BF16), HT])
    W.update(weights("mix", hm))
    z = _mm(hm, W["w_in"], "nn", BF16, "w_in", tm=1024, tn=1792)

    y_a, y_at = _sg_fwd(z, P["sg_ln_g"], P["sg_ln_b"], P["sg_w"], bias_full)

    def c_norm(cq, ckv, gq, gkv):
        a, b = _rms(cq, gq), _rms(ckv, gkv)
        return a, b, a, b

    cqn, ckvn, cqnt, ckvnt = _rowwise(
        c_norm, "mla_cnorm", 512, [(z, MLA_Q_RANK, Z_CQ // MLA_Q_RANK), (z, MLA_KV_RANK, Z_CKV // MLA_KV_RANK)],
        [P["mla_cq_norm"], P["mla_ckv_norm"]],
        [(MLA_Q_RANK, BF16), (MLA_KV_RANK, BF16), (MLA_Q_RANK, BF16, "T"), (MLA_KV_RANK, BF16, "T")])
    q_pre = _mm(cqn, W["mla_w_uq"], "nn", BF16, "mla_uq", tm=1024, tn=1024)
    kv_pre = _mm(ckvn, W["mla_w_ukv"], "nn", BF16, "mla_ukv", tm=1024, tn=1024)
    q, k, v = _mla_post(q_pre, kv_pre, z, tabs, gq_p, gk_p)
    y_b, y_bt, lse, lse_t = _flash_fwd(q, k, v)

    memn, = _rowwise(lambda m, g: _rms(m, g), "mem_norm", 256, [mem], [P["mem_norm"]], [(D_MODEL, BF16)])
    kvm = _mm(memn, W["mem_w_kv"], "nn", F32, "mem_kv")

    def mem_k(kvm, gk):
        ks = [_rms(kvm[:, h * LANES:(h + 1) * LANES], gk) for h in range(MEM_HEADS)]
        return jnp.concatenate(ks, axis=1), kvm[:, MEM_WIDTH:]

    km, vm = _rowwise(mem_k, "mem_knorm", 256, [kvm], [P["mem_k_norm"]], [(MEM_WIDTH, BF16), (MEM_WIDTH, BF16)])
    y_c, y_ct = _mem_fwd(z, km, vm, P["mem_q_norm"])

    pa = _mm(y_a, W["w_branch_a"], "nn", BF16, "branch_a", tm=1024, tn=1024)
    pb = _mm(y_b, W["w_branch_b"], "nn", BF16, "branch_b", tm=1024, tn=1024)
    pc = _mm(y_c, W["w_branch_c"], "nn", BF16, "branch_c", tm=1024, tn=1024)

    def merge(zg, pa, pb, pc, b):
        g = _sigmoid(zg + b)
        m = g[:, :D_MODEL] * pa + g[:, D_MODEL:2 * D_MODEL] * pb + g[:, 2 * D_MODEL:] * pc
        return m, m

    merged, mergedt = _rowwise(merge, "merge", 256, [(z, 3 * D_MODEL, 0), pa, pb, pc], [P["b_gate"]],
                               [(D_MODEL, BF16), HT])
    om = _mm(merged, W["w_out"], "nn", BF16, "w_out", tm=1024, tn=1024)

    def resid_norm1(x, o, g):
        xn = x + o
        h = _rms(xn, g)
        return xn, h, h

    x2, h2, h2t = _rowwise(resid_norm1, "ffn2_norm", 512, [x1, om], [P["ffn2_norm"]],
                           [(D_MODEL, F32), (D_MODEL, BF16), HT])
    W.update(weights("ffn2", h2))
    gu2, a2t, o2 = _ffn_fwd(h2, W["ffn2_w_gu"], W["ffn2_w_down"], "ffn2")

    def loss_fn(x2, o2, t):
        e = x2 + 0.5 * o2 - t
        return e * (1.0 / D_MODEL), (e * (0.5 / D_MODEL)).astype(BF16), _rsum(e * e) * (0.5 / D_MODEL)

    dx3, do2, loss_part = _rowwise(loss_fn, "loss", 512, [x2, o2, loss_target], [],
                                   [(D_MODEL, F32), (D_MODEL, BF16)], [((1, D_MODEL), F32)])

    dh2, G["ffn2_w_gu"], G["ffn2_w_down"] = _ffn_bwd(do2, h2t, gu2, a2t, W["ffn2_w_gu"], W["ffn2_w_down"], "ffn2")
    tie = grads_out("ffn2", G)

    def norm_bwd(x, dh, dxo, g, *_):
        dx, dg = _rms_bwd(x, g, dh)
        dx = dx + dxo
        return dx, dx, dg

    dx2, dx2b, G["ffn2_norm"] = _rowwise(norm_bwd, "ffn2_norm_bwd", 512, [x2, dh2, dx3],
                                         [P["ffn2_norm"]] + ([] if tie is None else [tie]),
                                         [(D_MODEL, F32), (D_MODEL, BF16)], [((1, D_MODEL), F32)])

    G["w_out"] = _mm_t(mergedt, dx2b, "w_out_dw", tm=1024, tn=1024)
    dmerged = _mm(dx2b, W["w_out"], "nt", BF16, "w_out_dx", tm=1024, tn=1024)

    def merge_bwd(zg, pa, pb, pc, dm, b):
        g = _sigmoid(zg + b)
        ps = jnp.concatenate([pa, pb, pc], axis=1)
        dm3 = jnp.concatenate([dm, dm, dm], axis=1)
        dzg = dm3 * ps * g * (1.0 - g)
        dp = dm3 * g
        return dzg, dp[:, :D_MODEL], dp[:, D_MODEL:2 * D_MODEL], dp[:, 2 * D_MODEL:], _rsum(dzg)

    dz = lax.empty((T, Z_COLS), BF16)
    dz, dpa, dpb, dpc, G["b_gate"] = _rowwise(
        merge_bwd, "merge_bwd", 256, [(z, 3 * D_MODEL, 0), pa, pb, pc, dmerged], [P["b_gate"]],
        [("into", dz, 3 * D_MODEL, 0), (D_MODEL, BF16), (D_MODEL, BF16), (D_MODEL, BF16)], [((1, 3 * D_MODEL), F32)])

    G["w_branch_a"] = _mm_t(y_at, dpa, "branch_a_dw", tm=512, tn=1024)
    G["w_branch_b"] = _mm_t(y_bt, dpb, "branch_b_dw", tm=1024, tn=1024)
    G["w_branch_c"] = _mm_t(y_ct, dpc, "branch_c_dw", tm=512, tn=1024)
    dy_a = _mm(dpa, W["w_branch_a"], "nt", BF16, "branch_a_dx", tm=1024, tn=512)
    dy_b = _mm(dpb, W["w_branch_b"], "nt", BF16, "branch_b_dx", tm=1024, tn=1024)
    dy_c = _mm(dpc, W["w_branch_c"], "nt", BF16, "branch_c_dx", tm=1024, tn=512)

    dz, G["sg_w"], dbias_t, G["sg_ln_g"], G["sg_ln_b"] = _sg_bwd(
        z, dy_a, P["sg_ln_g"], P["sg_ln_b"], P["sg_w"], bias_full, group_ind, dz)
    G["sg_b"] = dbias_t.T

    dz, dkm, dvm, G["mem_q_norm"] = _mem_bwd(z, dy_c, km, vm, P["mem_q_norm"], dz)

    def mem_k_bwd(kvm, dkm, dvm, gk):
        dks = []
        dg = jnp.zeros((1, LANES), F32)
        for h in range(MEM_HEADS):
            sl = slice(h * LANES, (h + 1) * LANES)
            dx, d = _rms_bwd(kvm[:, sl], gk, dkm[:, sl])
            dks.append(dx)
            dg = dg + d
        return jnp.concatenate(dks + [dvm], axis=1), dg

    dkvm, G["mem_k_norm"] = _rowwise(mem_k_bwd, "mem_knorm_bwd", 256, [kvm, dkm, dvm], [P["mem_k_norm"]],
                                     [(2 * MEM_WIDTH, BF16)], [((1, LANES), F32)])
    G["mem_w_kv"] = _mm(memn, dkvm, "tn", BF16, "mem_kv_dw")
    dmemn = _mm(dkvm, W["mem_w_kv"], "nt", F32, "mem_kv_dx")
    _, G["mem_norm"] = _rowwise(lambda m, d, g: _rms_bwd(m, g, d), "mem_norm_bwd", 256, [mem, dmemn],
                                [P["mem_norm"]], [(D_MODEL, BF16)], [((1, D_MODEL), F32)])

    def delta_fn(o, do):
        od = o.astype(F32) * do.astype(F32)
        ds = [jnp.broadcast_to(jnp.sum(od[:, h * LANES:(h + 1) * LANES], axis=1, keepdims=True), (od.shape[0], LANES))
              for h in range(MLA_HEADS)]
        d = jnp.concatenate(ds, axis=1)
        return d, d

    delta, delta_t = _rowwise(delta_fn, "mla_delta", 512, [y_b, dy_b], [], [(HP, F32), (HP, F32, "T")])
    dq = _flash_dq(q, k, v, dy_b, lse, delta)
    dk, dv = _flash_dkv(q, k, v, dy_b, lse_t, delta_t)
    dq_pre, dkv_pre, dkr, dgq, dgk = _mla_post_bwd(q_pre, kv_pre, z, tabs, gq_p, gk_p, dq, dk, dv)
    G["mla_q_norm"], G["mla_k_norm"] = dgq[:, :MLA_QK], dgk[:, :MLA_QK]
    G["mla_w_uq"] = _mm_t(cqnt, dq_pre, "mla_uq_dw", tm=384, tn=1024)
    G["mla_w_ukv"] = _mm_t(ckvnt, dkv_pre, "mla_ukv_dw", tm=256, tn=2048)
    dcqn = _mm(dq_pre, W["mla_w_uq"], "nt", BF16, "mla_uq_dx", tm=1024)
    dckvn = _mm(dkv_pre, W["mla_w_ukv"], "nt", BF16, "mla_ukv_dx", tm=1024)

    def c_norm_bwd(cq, ckv, dcqn, dckvn, dkr, gq, gkv):
        dcq, dgq = _rms_bwd(cq, gq, dcqn)
        dckv, dgkv = _rms_bwd(ckv, gkv, dckvn)
        return jnp.concatenate([dckv, dkr, dcq], axis=1), dgq, dgkv

    tail = Z_COLS - Z_CKV
    dz, G["mla_cq_norm"], G["mla_ckv_norm"] = _rowwise(
        c_norm_bwd, "mla_cnorm_bwd", 512,
        [(z, MLA_Q_RANK, Z_CQ // MLA_Q_RANK), (z, MLA_KV_RANK, Z_CKV // MLA_KV_RANK), dcqn, dckvn, dkr],
        [P["mla_cq_norm"], P["mla_ckv_norm"]], [("into", dz, tail, Z_CKV // tail)],
        [((1, MLA_Q_RANK), F32), ((1, MLA_KV_RANK), F32)])
    G["w_in"] = _mm_t(hmt, dz, "w_in_dw", tm=1024, tn=1792, tk=2048)
    dhm = _mm(dz, W["w_in"], "nt", BF16, "w_in_dx", tm=1024, tn=1024, tk=2688)

    def norm_bwd_half(x, dh, dxo, g):
        dx, dg = _rms_bwd(x, g, dh)
        dx = dx + dxo
        return dx, (0.5 * dx), dg

    dx1, do1, G["mix_norm"] = _rowwise(norm_bwd_half, "mix_norm_bwd", 512, [x1, dhm, dx2], [P["mix_norm"]],
                                       [(D_MODEL, F32), (D_MODEL, BF16)], [((1, D_MODEL), F32)])
    tie = grads_out("mix", G)

    def ffn1_dw(which, dw):
        G["ffn1_w_" + which] = dw
        return grads_out("ffn1_" + which, G)

    dh1, _, _ = _ffn_bwd(do1, h1t, gu1, a1t, W["ffn1_w_gu"], W["ffn1_w_down"], "ffn1", tie, ffn1_dw)

    def norm_bwd_last(x, dh, dxo, g):
        dx, dg = _rms_bwd(x, g, dh)
        return dx + dxo, dg

    grad_x, G["ffn1_norm"] = _rowwise(norm_bwd_last, "ffn1_norm_bwd", 512, [x, dh1, dx1], [P["ffn1_norm"]],
                                      [(D_MODEL, F32)], [((1, D_MODEL), F32)])
    return loss_part, grad_x, G


SHARDED = ["ffn1_w_gu", "ffn1_w_down", "w_in", "mla_w_uq", "mla_w_ukv", "mem_w_kv",
           "w_branch_a", "w_branch_b", "w_branch_c", "w_out", "ffn2_w_gu", "ffn2_w_down"]
ROW_SHARDED = {"ffn1_w_down", "mem_w_kv", "w_out", "ffn2_w_down"}
SMALL = ["ffn1_norm", "mix_norm", "b_gate", "sg_ln_g", "sg_ln_b", "sg_w", "sg_b", "mla_cq_norm",
         "mla_ckv_norm", "mla_q_norm", "mla_k_norm", "mem_norm", "mem_q_norm", "mem_k_norm", "ffn2_norm"]
ORDER = ["ffn1_norm", "ffn1_w_gu", "ffn1_w_down", "mix_norm", "w_in", "b_gate", "sg_ln_g", "sg_ln_b", "sg_w",
         "sg_b", "mla_cq_norm", "mla_w_uq", "mla_ckv_norm", "mla_w_ukv", "mla_q_norm", "mla_k_norm", "mem_norm",
         "mem_w_kv", "mem_q_norm", "mem_k_norm", "w_branch_a", "w_branch_b", "w_branch_c", "w_out", "ffn2_norm",
         "ffn2_w_gu", "ffn2_w_down"]

_IN_U, _IN_V, _IN_CQ, _IN_CKV, _IN_KR, _IN_QM, _IN_G = 0, 512, 1024, 1408, 1664, 1696, 2208
IN_COLS = 5280


def _full_from_slabs(name, slabs):
    n, r, c = slabs.shape
    if name in ROW_SHARDED:
        return slabs.reshape(n * r, c)
    return slabs.transpose(1, 0, 2).reshape(r, n * c)


def _slabs_from_full(name, full):
    if name in ROW_SHARDED:
        return full.reshape(N_DEV, full.shape[0] // N_DEV, full.shape[1])
    r, c = full.shape
    return full.reshape(r, N_DEV, c // N_DEV).transpose(1, 0, 2)


def _compute_layout(full):
    W = dict(full)
    if "w_in" not in full:
        return W
    w = full["w_in"]
    kr = jnp.pad(w[:, _IN_KR:_IN_QM], ((0, 0), (KR_LANE, LANES - KR_LANE - MLA_ROPE)))
    W["w_in"] = jnp.concatenate([w[:, _IN_G:], w[:, _IN_U:_IN_CQ], w[:, _IN_QM:_IN_G], w[:, _IN_CKV:_IN_KR], kr,
                                 w[:, _IN_CQ:_IN_CKV]], axis=1)
    uq = full["mla_w_uq"].reshape(MLA_Q_RANK, MLA_HEADS, MLA_QK)
    W["mla_w_uq"] = jnp.pad(uq, ((0, 0), (0, 0), (0, LANES - MLA_QK))).reshape(MLA_Q_RANK, HP)
    ukv = full["mla_w_ukv"].reshape(MLA_KV_RANK, MLA_HEADS, MLA_NOPE + MLA_V)
    padh = lambda a: jnp.pad(a, ((0, 0), (0, 0), (0, LANES - a.shape[2]))).reshape(MLA_KV_RANK, HP)
    W["mla_w_ukv"] = jnp.concatenate([padh(ukv[:, :, :MLA_NOPE]), padh(ukv[:, :, MLA_NOPE:])], axis=1)
    wb = full["w_branch_b"].reshape(MLA_HEADS, MLA_V, D_MODEL)
    W["w_branch_b"] = jnp.pad(wb, ((0, 0), (0, LANES - MLA_V), (0, 0))).reshape(HP, D_MODEL)
    return W


def _reference_layout(G):
    out = dict(G)
    if "w_in" not in G:
        return out
    g = G["w_in"]
    out["w_in"] = jnp.concatenate([
        g[:, Z_U:Z_QM], g[:, Z_CQ:Z_COLS], g[:, Z_CKV:Z_KR], g[:, Z_KR + KR_LANE:Z_KR + KR_LANE + MLA_ROPE],
        g[:, Z_QM:Z_CKV], g[:, Z_G:Z_U]], axis=1)
    out["mla_w_uq"] = G["mla_w_uq"].reshape(MLA_Q_RANK, MLA_HEADS, LANES)[:, :, :MLA_QK].reshape(MLA_Q_RANK, -1)
    gk = G["mla_w_ukv"][:, :HP].reshape(MLA_KV_RANK, MLA_HEADS, LANES)[:, :, :MLA_NOPE]
    gv = G["mla_w_ukv"][:, HP:].reshape(MLA_KV_RANK, MLA_HEADS, LANES)[:, :, :MLA_V]
    out["mla_w_ukv"] = jnp.concatenate([gk, gv], axis=2).reshape(MLA_KV_RANK, -1)
    out["w_branch_b"] = G["w_branch_b"].reshape(MLA_HEADS, LANES, D_MODEL)[:, :MLA_V].reshape(-1, D_MODEL)
    return out


def _pack(parts):
    flat = []
    for a in parts:
        a = a.reshape(-1)
        flat.append(jnp.pad(a, (0, (-a.shape[0]) % LANES)))
    return jnp.concatenate(flat).reshape(-1, LANES)


def _unpack(packed, shapes):
    flat = packed.reshape(-1)
    out, off = [], 0
    for shp in shapes:
        n = int(np.prod(shp))
        out.append(flat[off:off + n].reshape(shp))
        off += n + (-n) % LANES
    return out


MESH = pl.DeviceIdType.MESH
HBM = pl.BlockSpec(memory_space=pltpu.HBM)


def _all_gather(shards):
    n = len(shards)

    def body(*refs):
        x_refs, out_refs, token_ref = refs[:n], refs[n:2 * n], refs[2 * n]
        send_sems, recv_sems, local_sems = refs[2 * n + 1:]
        x, y, c = lax.axis_index("x"), lax.axis_index("y"), lax.axis_index("c")
        me, sibling = (x, y, c), (x, y, 1 - c)
        chips = [(1 - x, y), (x, 1 - y), (1 - x, 1 - y)]
        token_ref[...] = jnp.zeros_like(token_ref)

        def slot(a, px, py, pc):
            return out_refs[a].at[4 * px + 2 * py + pc]

        def copy(a, k, block, to, src=None):
            return pltpu.make_async_remote_copy(
                src_ref=slot(a, *block) if src is None else src, dst_ref=slot(a, *block),
                send_sem=send_sems.at[7 * a + k], recv_sem=recv_sems.at[7 * a + k], device_id=to, device_id_type=MESH)

        arrays = range(n)
        mine = [pltpu.make_async_copy(x_refs[a], slot(a, *me), local_sems.at[a]) for a in arrays]
        for cp in mine:
            cp.start()
        first = [copy(a, 0, me, sibling, src=x_refs[a]) for a in arrays]
        first += [copy(a, 1 + j, me, (*chip, c), src=x_refs[a]) for j, chip in enumerate(chips) for a in arrays]
        for cp in first:
            cp.start()
        passed = []
        for j, chip in enumerate(chips):
            for a in arrays:
                copy(a, 1 + j, (*chip, c), me).wait_recv()
                passed.append(copy(a, 4 + j, (*chip, c), sibling))
                passed[-1].start()
        for a in arrays:
            copy(a, 0, sibling, me).wait_recv()
        for j, chip in enumerate(chips):
            for a in arrays:
                copy(a, 4 + j, (*chip, 1 - c), me).wait_recv()
        for cp in first + passed:
            cp.wait_send()
        for cp in mine:
            cp.wait()

    res = pl.pallas_call(
        body, name="all_gather_weights",
        out_shape=[jax.ShapeDtypeStruct((N_DEV,) + s.shape, s.dtype) for s in shards]
        + [jax.ShapeDtypeStruct((8, LANES), F32)],
        in_specs=[HBM] * n, out_specs=[HBM] * n + [pl.BlockSpec(memory_space=pltpu.VMEM)],
        scratch_shapes=[pltpu.SemaphoreType.DMA((7 * n,)), pltpu.SemaphoreType.DMA((7 * n,)),
                        pltpu.SemaphoreType.DMA((n,))],
    )(*shards)
    return res[:n], res[n]


SEM = pl.BlockSpec(memory_space=pltpu.SEMAPHORE)
DATAFLOW = pltpu.SideEffectType.DATAFLOW_SIDE_EFFECTING


def _peers():
    x, y, c = lax.axis_index("x"), lax.axis_index("y"), lax.axis_index("c")
    out = []
    for k in range(1, N_DEV):
        px = 1 - x if k & 4 else x
        py = 1 - y if k & 2 else y
        pc = 1 - c if k & 1 else c
        out.append((k, (px, py, pc), 4 * px + 2 * py + pc))
    return 4 * x + 2 * y + c, out


def _send_start(srcs, per_peer, name):
    n = len(srcs)
    lands = [lax.empty((N_DEV,) + (s.shape[1:] if per_peer else s.shape), s.dtype) for s in srcs]

    def body(*refs):
        src_refs, land_refs, send_sems, recv_sems, token = refs[:n], refs[n:2 * n], refs[2 * n], refs[2 * n + 1], refs[-1]
        me, peers = _peers()
        for a in range(n):
            for k, pid, pflat in peers:
                pltpu.make_async_remote_copy(
                    src_ref=src_refs[a].at[pflat] if per_peer else src_refs[a], dst_ref=land_refs[a].at[me],
                    send_sem=send_sems.at[7 * a + k - 1], recv_sem=recv_sems.at[7 * a + k - 1],
                    device_id=pid, device_id_type=MESH).start()
        token[...] = jnp.zeros_like(token)

    hbm = lambda a: pltpu.with_memory_space_constraint(a, pltpu.HBM)
    res = pl.pallas_call(
        body, name=name,
        out_shape=(pltpu.SemaphoreType.DMA((7 * n,)), pltpu.SemaphoreType.DMA((7 * n,)),
                   *[pltpu.HBM(a.shape, a.dtype) for a in srcs + lands], jax.ShapeDtypeStruct((8, LANES), F32)),
        in_specs=(HBM,) * (2 * n), out_specs=(SEM, SEM) + (HBM,) * (2 * n) + (pl.BlockSpec(memory_space=pltpu.VMEM),),
        input_output_aliases={i: 2 + i for i in range(2 * n)},
        compiler_params=pltpu.CompilerParams(has_side_effects=DATAFLOW),
    )(*[hbm(a) for a in srcs + lands])
    return (res[0], res[1], list(res[2:2 + n]), list(res[2 + n:2 + 2 * n])), res[-1]


def _send_wait(started, after, per_peer, name):
    send_sems, recv_sems, srcs_thru, lands_thru = started
    n = len(srcs_thru)

    def body(*refs):
        src_refs, land_refs, send_sems, recv_sems = refs[:n], refs[n:2 * n], refs[2 * n], refs[2 * n + 1]
        me, peers = _peers()
        for a in range(n):
            for k, pid, pflat in peers:
                copy = pltpu.make_async_remote_copy(
                    src_ref=src_refs[a].at[pflat] if per_peer else src_refs[a], dst_ref=land_refs[a].at[pflat],
                    send_sem=send_sems.at[7 * a + k - 1], recv_sem=recv_sems.at[7 * a + k - 1],
                    device_id=pid, device_id_type=MESH)
                copy.wait_send()
                copy.wait_recv()

    outs = pl.pallas_call(
        body, name=name,
        out_shape=tuple(pltpu.HBM(a.shape, a.dtype) for a in srcs_thru + lands_thru),
        in_specs=(HBM,) * (2 * n) + (SEM, SEM, pl.BlockSpec(memory_space=pl.ANY)), out_specs=(HBM,) * (2 * n),
        input_output_aliases={i: i for i in range(2 * n)},
        compiler_params=pltpu.CompilerParams(has_side_effects=DATAFLOW),
    )(*srcs_thru, *lands_thru, send_sems, recv_sems, after)
    me = 4 * lax.axis_index("x") + 2 * lax.axis_index("y") + lax.axis_index("c")
    landed = []
    for src_out, land in zip(outs[:n], outs[n:]):
        own = lax.dynamic_index_in_dim(src_out, me, 0, keepdims=True) if per_peer else src_out[None]
        landed.append(lax.dynamic_update_slice(land, own, (me,) + (0,) * (land.ndim - 1)))
    return landed


def _share_rows(block, name):
    def body(src_ref, out_ref, send_sems, recv_sems, local_sem):
        me, peers = _peers()
        own = pltpu.make_async_copy(src_ref, out_ref.at[me], local_sem)
        own.start()
        copies = [pltpu.make_async_remote_copy(
            src_ref=src_ref, dst_ref=out_ref.at[me], send_sem=send_sems.at[k - 1], recv_sem=recv_sems.at[k - 1],
            device_id=pid, device_id_type=MESH) for k, pid, _ in peers]
        for cp in copies:
            cp.start()
        for cp in copies:
            cp.wait()
        own.wait()

    return pl.pallas_call(
        body, name=name, out_shape=jax.ShapeDtypeStruct((N_DEV,) + block.shape, block.dtype),
        in_specs=[HBM], out_specs=HBM,
        scratch_shapes=[pltpu.SemaphoreType.DMA((N_DEV - 1,)), pltpu.SemaphoreType.DMA((N_DEV - 1,)),
                        pltpu.SemaphoreType.DMA],
    )(block)


def _sum_slots(recv, name, tr):
    n, rows, lanes = recv.shape
    tr = _tile(rows, tr)

    def body(r_ref, o_ref):
        acc = r_ref[0].astype(F32)
        for i in range(1, n):
            acc = acc + r_ref[i].astype(F32)
        o_ref[...] = acc

    return pl.pallas_call(
        body, name=name, grid=(rows // tr,),
        in_specs=[pl.BlockSpec((n, tr, lanes), lambda i: (0, i, 0))],
        out_specs=pl.BlockSpec((tr, lanes), lambda i: (i, 0)),
        out_shape=jax.ShapeDtypeStruct((rows, lanes), F32),
        compiler_params=_cparams(("parallel",)),
    )(recv)


def _adamw_math(w, g, m, v):
    m = ADAM_B1 * m + (1.0 - ADAM_B1) * g
    v = ADAM_B2 * v + (1.0 - ADAM_B2) * (g * g)
    m_hat = m / (1.0 - ADAM_B1 ** ADAM_STEP)
    v_hat = v / (1.0 - ADAM_B2 ** ADAM_STEP)
    return -ADAM_LR * (m_hat / (jnp.sqrt(v_hat) + ADAM_EPS) + ADAM_WD * w), m, v


def _adamw(w, g, m, v, name, tr=256):
    return _rowwise(_adamw_math, name, tr, [w, g, m, v], [], [(w.shape[1], F32)] * 3)


def _adamw_small(ws, gs, ms, vs):
    n = len(ws)

    def body(*refs):
        ins, outs = refs[:4 * n], refs[4 * n:]
        for i in range(n):
            d, m, v = _adamw_math(ins[i][...], ins[n + i][...], ins[2 * n + i][...], ins[3 * n + i][...])
            outs[i][...], outs[n + i][...], outs[2 * n + i][...] = d, m, v

    vmem = pl.BlockSpec(memory_space=pltpu.VMEM)
    res = pl.pallas_call(
        body, name="adamw_small", in_specs=[vmem] * (4 * n), out_specs=[vmem] * (3 * n),
        out_shape=[jax.ShapeDtypeStruct(w.shape, F32) for w in ws] * 3,
    )(*ws, *gs, *ms, *vs)
    return res[:n], res[n:2 * n], res[2 * n:]


def _sum_adamw(recv, w, m, v, name):
    n, r, c = recv.shape
    tr = _tile(r, 256)

    def body(r_ref, w_ref, m_ref, v_ref, g_ref, d_ref, nm_ref, nv_ref):
        g = r_ref[0].astype(F32)
        for i in range(1, n):
            g = g + r_ref[i].astype(F32)
        g_ref[...] = g
        d_ref[...], nm_ref[...], nv_ref[...] = _adamw_math(w_ref[...], g, m_ref[...], v_ref[...])

    row = pl.BlockSpec((None, tr, c), lambda i: (0, i, 0))
    return pl.pallas_call(
        body, name=name, grid=(r // tr,),
        in_specs=[pl.BlockSpec((n, tr, c), lambda i: (0, i, 0)), row, row, row], out_specs=[row] * 4,
        out_shape=[jax.ShapeDtypeStruct((1, r, c), F32)] * 4, compiler_params=_cparams(("parallel",)),
    )(recv, w, m, v)


def kernel(x, mem, positions, ffn1_norm, ffn1_w_gu, ffn1_w_down, mix_norm, w_in, b_gate, sg_ln_g, sg_ln_b, sg_w, sg_b, mla_cq_norm, mla_w_uq, mla_ckv_norm, mla_w_ukv, mla_q_norm, mla_k_norm, mem_norm, mem_w_kv, mem_q_norm, mem_k_norm, w_branch_a, w_branch_b, w_branch_c, w_out, ffn2_norm, ffn2_w_gu, ffn2_w_down, loss_target, m_ffn1_norm, m_ffn1_w_gu, m_ffn1_w_down, m_mix_norm, m_w_in, m_b_gate, m_sg_ln_g, m_sg_ln_b, m_sg_w, m_sg_b, m_mla_cq_norm, m_mla_w_uq, m_mla_ckv_norm, m_mla_w_ukv, m_mla_q_norm, m_mla_k_norm, m_mem_norm, m_mem_w_kv, m_mem_q_norm, m_mem_k_norm, m_w_branch_a, m_w_branch_b, m_w_branch_c, m_w_out, m_ffn2_norm, m_ffn2_w_gu, m_ffn2_w_down, v_ffn1_norm, v_ffn1_w_gu, v_ffn1_w_down, v_mix_norm, v_w_in, v_b_gate, v_sg_ln_g, v_sg_ln_b, v_sg_w, v_sg_b, v_mla_cq_norm, v_mla_w_uq, v_mla_ckv_norm, v_mla_w_ukv, v_mla_q_norm, v_mla_k_norm, v_mem_norm, v_mem_w_kv, v_mem_q_norm, v_mem_k_norm, v_w_branch_a, v_w_branch_b, v_w_branch_c, v_w_out, v_ffn2_norm, v_ffn2_w_gu, v_ffn2_w_down):
    given = dict(locals())
    wts = {n: given[n] for n in ORDER}
    mom = {n: given["m_" + n] for n in ORDER}
    var = {n: given["v_" + n] for n in ORDER}

    def shards(group, zero):
        out = [wts[n][0].astype(BF16) for n in GROUPS[group]]
        return [out[0] + zero.astype(BF16)] + out[1:]

    def full_weights(group, slabs):
        return _compute_layout({n: _full_from_slabs(n, s) for n, s in zip(GROUPS[group], slabs)})

    def zero_of(a):
        return jnp.minimum(jnp.abs(a.reshape(-1)[0]), 0)

    gathered_ffn1, token = _all_gather([wts[n][0].astype(BF16) for n in GROUPS["ffn1"]])
    flight = {}
    flight["ffn1_down"], token = _send_start(shards("ffn1_down", token[0, 0]), False, "gather_ffn1_down_start")
    flight["mix"] = _send_start(shards("mix", token[0, 0]), False, "gather_mix_start")[0]
    recv = {}

    def weights(group, after):
        if group == "ffn1":
            return full_weights(group, gathered_ffn1)
        landed = _send_wait(flight.pop(group), after, False, f"gather_{group}_wait")
        if group == "mix":
            flight["ffn2"] = _send_start(shards("ffn2", zero_of(landed[0])), False, "gather_ffn2_start")[0]
        return full_weights(group, landed)

    small_shapes = [wts[n].shape[1:] for n in SMALL]
    early = SMALL[1:]
    assert SMALL[0] == "ffn1_norm"

    def grads_out(group, G):
        Gr = _reference_layout({n: G[n] for n in GRAD_GROUPS[group]})
        parts = [_slabs_from_full(n, Gr[n]).astype(BF16) for n in GRAD_GROUPS[group]]
        flight["g_" + group], tie = _send_start(parts, True, f"grads_{group}_start")
        if group == "mix":
            small = _pack([G[n].reshape(s) for n, s in zip(early, small_shapes[1:])])
            small = jnp.pad(small, ((0, (-small.shape[0]) % 8), (0, 0)))
            flight["small"], tie = _send_start([small + tie[0, 0]], False, "grads_small_start")
        return tie

    P = {n: wts[n] if wts[n].ndim == 2 else wts[n][0] for n in SMALL}
    loss_part, grad_x, G = _local_step(x[0], mem[0], positions[0], loss_target[0], P, weights, grads_out)

    for group, names in GRAD_GROUPS.items():
        recv.update(zip(names, _send_wait(flight.pop("g_" + group), grad_x, True, f"grads_{group}_wait")))
    early_recv, = _send_wait(flight.pop("small"), grad_x, False, "grads_small_wait")
    last = _share_rows(G["ffn1_norm"].reshape(-1, LANES), "share_ffn1_norm")
    g_small_packed = _sum_slots(jnp.concatenate([last, early_recv], axis=1), "sum_small", 2048)

    grads, delta, new_m, new_v = {}, {}, {}, {}
    for n in SHARDED:
        grads[n], delta[n], new_m[n], new_v[n] = _sum_adamw(recv[n], wts[n], mom[n], var[n], "adamw_" + n)
    grads.update(zip(SMALL, _unpack(g_small_packed, small_shapes)))

    flat2 = lambda d: [d[n].reshape(-1, d[n].shape[-1]) for n in SMALL]
    for dst, vals in zip((delta, new_m, new_v), _adamw_small(flat2(wts), flat2(grads), flat2(mom), flat2(var))):
        dst.update(zip(SMALL, vals))

    loss = lax.psum(jnp.sum(loss_part), ("x", "y", "c"))
    lead = lambda d: [d[n].reshape(wts[n].shape) for n in ORDER]
    return (loss, grad_x[None], *lead(grads), *lead(delta), *lead(new_m), *lead(new_v))
```

```python
import functools

import numpy as np
import jax
import jax.numpy as jnp
from jax import lax
from jax.experimental import pallas as pl
from jax.experimental.pallas import tpu as pltpu

F32, BF16 = jnp.float32, jnp.bfloat16

D_MODEL = 1024
SG_GROUPS, SG_GROUP_DIM, SG_WIDTH, CHUNK = 8, 64, 512, 128
MLA_HEADS, MLA_NOPE, MLA_ROPE, MLA_V, MLA_QK = 8, 64, 32, 64, 96
MLA_Q_RANK, MLA_KV_RANK = 384, 256
MEM_HEADS, MEM_HEAD_DIM, MEM_WIDTH = 4, 128, 512
D_FF = 2816
ROPE_BASE = 10000.0
EPS = 1e-6
NEG = -1e30
ADAM_LR, ADAM_B1, ADAM_B2, ADAM_EPS, ADAM_WD, ADAM_STEP = 0.001, 0.9, 0.999, 1e-08, 0.01, 10

N_DEV = 8
LANES = 128
V7X_VMEM_LIMIT = 56 * 1024 * 1024
HP = MLA_HEADS * LANES

Z_G, Z_U, Z_V, Z_QM, Z_CKV, Z_KR, Z_CQ = 0, 3072, 3584, 4096, 4608, 4864, 4992
Z_COLS = 5376
KR_LANE = 64


def _tile(dim, pref):
    if dim <= pref:
        return dim
    for t in range(pref - pref % LANES, LANES - 1, -LANES):
        if dim % t == 0:
            return t
    for t in range(pref - pref % 8, 7, -8):
        if dim % t == 0:
            return t
    return dim


def _cparams(sem):
    return pltpu.CompilerParams(dimension_semantics=sem, vmem_limit_bytes=V7X_VMEM_LIMIT)


_DN = {"nn": ((1,), (0,)), "nt": ((1,), (1,)), "tn": ((0,), (0,))}


def _dot(a, b, mode="nn"):
    return lax.dot_general(a.astype(BF16), b.astype(BF16), (_DN[mode], ((), ())),
                           preferred_element_type=F32)


def _mm(a, b, mode, out_dtype, name, tm=512, tn=512, tk=2048, tie=None):
    if mode == "tn":
        K, M = a.shape
    else:
        M, K = a.shape
    N = b.shape[0] if mode == "nt" else b.shape[1]
    tm, tn, tk = _tile(M, tm), _tile(N, tn), _tile(K, tk)
    nk = K // tk
    if mode == "tn":
        a_spec = pl.BlockSpec((tk, tm), lambda i, j, k: (k, i))
    else:
        a_spec = pl.BlockSpec((tm, tk), lambda i, j, k: (i, k))
    if mode == "nt":
        b_spec = pl.BlockSpec((tn, tk), lambda i, j, k: (j, k))
    else:
        b_spec = pl.BlockSpec((tk, tn), lambda i, j, k: (k, j))

    ties = [] if tie is None else [tie]

    def body(a_ref, b_ref, *rest):
        o_ref, *scratch = rest[len(ties):]
        p = _dot(a_ref[...], b_ref[...], mode)
        if nk == 1:
            o_ref[...] = p.astype(o_ref.dtype)
        else:
            acc_ref, = scratch
            k = pl.program_id(2)

            @pl.when(k == 0)
            def _():
                acc_ref[...] = p

            @pl.when(k > 0)
            def _():
                acc_ref[...] += p

            @pl.when(k == nk - 1)
            def _():
                o_ref[...] = acc_ref[...].astype(o_ref.dtype)

    return pl.pallas_call(
        body, name=name, grid=(M // tm, N // tn, nk),
        in_specs=[a_spec, b_spec] + [pl.BlockSpec(t.shape, lambda i, j, k: (0, 0)) for t in ties],
        out_specs=pl.BlockSpec((tm, tn), lambda i, j, k: (i, j)),
        out_shape=jax.ShapeDtypeStruct((M, N), out_dtype),
        scratch_shapes=[] if nk == 1 else [pltpu.VMEM((tm, tn), F32)],
        compiler_params=_cparams(("parallel", "parallel", "arbitrary")),
    )(a, b, *ties)


def _mm_t(at, b, name, tm, tn, tk=1024, tie=None):
    return _mm(at, b, "nn", BF16, name, tm=tm, tn=tn, tk=tk, tie=tie)


def _rowwise(fn, name, tr, row_ins, bc_ins, row_outs, acc_outs=()):
    norm = [it if isinstance(it, tuple) else (it, it.shape[1], 0) for it in row_ins]
    rows = norm[0][0].shape[0]
    tr = _tile(rows, tr)
    arrays, in_specs = [], []
    for arr, w, cb in norm:
        arrays.append(arr)
        in_specs.append(pl.BlockSpec((tr, w), lambda i, cb=cb: (i, cb)))
    for arr in bc_ins:
        arrays.append(arr)
        in_specs.append(pl.BlockSpec(arr.shape, lambda i, nd=arr.ndim: (0,) * nd))
    n_in, n_row = len(arrays), len(row_outs)
    out_shape, out_specs, aliases = [], [], {}
    transposed = [len(o) == 3 for o in row_outs]
    for k, o in enumerate(row_outs):
        if o[0] == "into":
            _, target, w, cb = o
            aliases[len(arrays)] = k
            arrays.append(target)
            in_specs.append(pl.BlockSpec(memory_space=pl.ANY))
            out_shape.append(jax.ShapeDtypeStruct(target.shape, target.dtype))
            out_specs.append(pl.BlockSpec((tr, w), lambda i, cb=cb: (i, cb)))
        elif transposed[k]:
            out_shape.append(jax.ShapeDtypeStruct((o[0], rows), o[1]))
            out_specs.append(pl.BlockSpec((o[0], tr), lambda i: (0, i)))
        else:
            out_shape.append(jax.ShapeDtypeStruct((rows, o[0]), o[1]))
            out_specs.append(pl.BlockSpec((tr, o[0]), lambda i: (i, 0)))
    for shp, dt in acc_outs:
        out_shape.append(jax.ShapeDtypeStruct(shp, dt))
        out_specs.append(pl.BlockSpec(shp, lambda i, nd=len(shp): (0,) * nd))

    def body(*refs):
        vals = fn(*[r[...].astype(F32) for r in refs[:n_in]])
        if not isinstance(vals, (tuple, list)):
            vals = (vals,)
        outs = refs[len(arrays):]
        for r, v, t in zip(outs[:n_row], vals[:n_row], transposed):
            r[...] = v.astype(F32).T.astype(r.dtype) if t else v.astype(r.dtype)
        if acc_outs:
            accs = list(zip(outs[n_row:], vals[n_row:]))
            i = pl.program_id(0)

            @pl.when(i == 0)
            def _():
                for r, v in accs:
                    r[...] = v.astype(r.dtype)

            @pl.when(i > 0)
            def _():
                for r, v in accs:
                    r[...] += v.astype(r.dtype)

    res = pl.pallas_call(
        body, name=name, grid=(rows // tr,), in_specs=in_specs, out_specs=out_specs,
        out_shape=out_shape, input_output_aliases=aliases, compiler_params=_cparams(("arbitrary",)),
    )(*arrays)
    return res


def _rsum(x):
    return jnp.sum(x, axis=0, keepdims=True)


def _rms(x, g, n=None):
    n = x.shape[-1] if n is None else n
    r = lax.rsqrt(jnp.sum(x * x, axis=-1, keepdims=True) * (1.0 / n) + EPS)
    return x * r * g


def _rms_bwd(x, g, dy, n=None):
    n = x.shape[-1] if n is None else n
    r = lax.rsqrt(jnp.sum(x * x, axis=-1, keepdims=True) * (1.0 / n) + EPS)
    xh = x * r
    dxh = dy * g
    dx = r * (dxh - xh * (jnp.sum(dxh * xh, axis=-1, keepdims=True) * (1.0 / n)))
    return dx, _rsum(dy * xh)


def _gelu(x):
    return 0.5 * x * (1.0 + lax.erf(x * 0.7071067811865476))


def _gelu_grad(x):
    return 0.5 * (1.0 + lax.erf(x * 0.7071067811865476)) + x * jnp.exp(-0.5 * x * x) * 0.3989422804014327


def _sigmoid(x):
    return 0.5 * jnp.tanh(0.5 * x) + 0.5


FFN_TM, FFN_TN = 1024, 1408
MXU_WIDTH = 256


def _col_chunks(n):
    return [(c, min(c + MXU_WIDTH, n)) for c in range(0, n, MXU_WIDTH)]


def _ffn_gu_act(h, w_gu, tag):
    T = h.shape[0]
    tm, tn = _tile(T, FFN_TM), FFN_TN
    nj = D_FF // tn

    def body(h_ref, wg_ref, wu_ref, gu_ref, a_ref, at_ref):
        h = h_ref[...]
        for c0, c1 in _col_chunks(tn):
            g = _dot(h, wg_ref[:, c0:c1])
            u = _dot(h, wu_ref[:, c0:c1])
            gu_ref[0, :, c0:c1] = g.astype(BF16)
            gu_ref[1, :, c0:c1] = u.astype(BF16)
            a = g * _sigmoid(g) * u
            a_ref[:, c0:c1] = a.astype(BF16)
            at_ref[c0:c1, :] = a.T.astype(BF16)

    return pl.pallas_call(
        body, name=f"{tag}_gu_act", grid=(T // tm, nj),
        in_specs=[pl.BlockSpec((tm, D_MODEL), lambda i, j: (i, 0)),
                  pl.BlockSpec((D_MODEL, tn), lambda i, j: (0, j)),
                  pl.BlockSpec((D_MODEL, tn), lambda i, j: (0, j + nj))],
        out_specs=[pl.BlockSpec((2, tm, tn), lambda i, j: (0, i, j)),
                   pl.BlockSpec((tm, tn), lambda i, j: (i, j)),
                   pl.BlockSpec((tn, tm), lambda i, j: (j, i))],
        out_shape=[jax.ShapeDtypeStruct((2, T, D_FF), BF16), jax.ShapeDtypeStruct((T, D_FF), BF16),
                   jax.ShapeDtypeStruct((D_FF, T), BF16)],
        compiler_params=_cparams(("parallel", "parallel")),
    )(h, w_gu, w_gu)


def _ffn_da_actbwd(do, w_down, gu, tag, tie=None):
    T = do.shape[0]
    tm, tn = _tile(T, FFN_TM), FFN_TN
    ties = [] if tie is None else [tie]

    def body(do_ref, wd_ref, gu_ref, *rest):
        dgu_ref = rest[-1]
        do = do_ref[...]
        for c0, c1 in _col_chunks(tn):
            da = _dot(do, wd_ref[c0:c1, :], "nt")
            g = gu_ref[0, :, c0:c1].astype(F32)
            u = gu_ref[1, :, c0:c1].astype(F32)
            s = _sigmoid(g)
            dgu_ref[0, :, c0:c1] = (da * u * s * (1.0 + g * (1.0 - s))).astype(BF16)
            dgu_ref[1, :, c0:c1] = (da * g * s).astype(BF16)

    return pl.pallas_call(
        body, name=f"{tag}_da_actbwd", grid=(T // tm, D_FF // tn),
        in_specs=[pl.BlockSpec((tm, D_MODEL), lambda i, j: (i, 0)),
                  pl.BlockSpec((tn, D_MODEL), lambda i, j: (j, 0)),
                  pl.BlockSpec((2, tm, tn), lambda i, j: (0, i, j))]
        + [pl.BlockSpec(t.shape, lambda i, j: (0, 0)) for t in ties],
        out_specs=pl.BlockSpec((2, tm, tn), lambda i, j: (0, i, j)),
        out_shape=jax.ShapeDtypeStruct((2, T, D_FF), BF16),
        compiler_params=_cparams(("parallel", "parallel")),
    )(do, w_down, gu, *ties)


def _ffn_dwgu(ht, dgu, tag, tk=2048):
    T = ht.shape[1]
    tn, tk = FFN_TN, _tile(T, tk)
    nj, nk = D_FF // tn, T // tk

    def body(a_ref, b_ref, o_ref, acc_ref):
        k = pl.program_id(1)
        p = _dot(a_ref[...], b_ref[...])

        @pl.when(k == 0)
        def _():
            acc_ref[...] = p

        @pl.when(k > 0)
        def _():
            acc_ref[...] += p

        @pl.when(k == nk - 1)
        def _():
            o_ref[...] = acc_ref[...].astype(o_ref.dtype)

    return pl.pallas_call(
        body, name=f"{tag}_dwgu", grid=(2 * nj, nk),
        in_specs=[pl.BlockSpec((D_MODEL, tk), lambda n, k: (0, k)),
                  pl.BlockSpec((None, tk, tn), lambda n, k: (n // nj, k, n % nj))],
        out_specs=pl.BlockSpec((D_MODEL, tn), lambda n, k: (0, n)),
        out_shape=jax.ShapeDtypeStruct((D_MODEL, 2 * D_FF), BF16),
        scratch_shapes=[pltpu.VMEM((D_MODEL, tn), F32)],
        compiler_params=_cparams(("parallel", "arbitrary")),
    )(ht, dgu)


def _ffn_dh(dgu, w_gu, tag, tm=2048, tie=None):
    T = dgu.shape[1]
    tm, tk = _tile(T, tm), FFN_TN
    nk = D_FF // tk
    ties = [] if tie is None else [tie]

    def body(a_ref, b_ref, *rest):
        o_ref, acc_ref = rest[len(ties):]
        k = pl.program_id(1)
        p = _dot(a_ref[...], b_ref[...], "nt")

        @pl.when(k == 0)
        def _():
            acc_ref[...] = p

        @pl.when(k > 0)
        def _():
            acc_ref[...] += p

        @pl.when(k == 2 * nk - 1)
        def _():
            o_ref[...] = acc_ref[...].astype(o_ref.dtype)

    return pl.pallas_call(
        body, name=f"{tag}_dh", grid=(T // tm, 2 * nk),
        in_specs=[pl.BlockSpec((None, tm, tk), lambda i, k: (k // nk, i, k % nk)),
                  pl.BlockSpec((D_MODEL, tk), lambda i, k: (0, k))]
        + [pl.BlockSpec(t.shape, lambda i, k: (0, 0)) for t in ties],
        out_specs=pl.BlockSpec((tm, D_MODEL), lambda i, k: (i, 0)),
        out_shape=jax.ShapeDtypeStruct((T, D_MODEL), BF16),
        scratch_shapes=[pltpu.VMEM((tm, D_MODEL), F32)],
        compiler_params=_cparams(("parallel", "arbitrary")),
    )(dgu, w_gu, *ties)


def _ffn_fwd(h, w_gu, w_down, tag):
    gu, a, at = _ffn_gu_act(h, w_gu, tag)
    if callable(w_down):
        w_down = w_down(at)
    o = _mm(a, w_down, "nn", BF16, f"{tag}_down", tm=1024, tn=1024, tk=2816)
    return gu, at, o


def _ffn_bwd(do, ht, gu, at, w_gu, w_down, tag, tie=None, on_dw=None):
    on_dw = on_dw or (lambda which, dw: None)
    dw_down = _mm_t(at, do, f"{tag}_dwdown", tm=1408, tn=1024, tk=2048, tie=tie)
    dgu = _ffn_da_actbwd(do, w_down, gu, tag, tie=on_dw("down", dw_down))
    dw_gu = _ffn_dwgu(ht, dgu, tag)
    dh = _ffn_dh(dgu, w_gu, tag, tie=on_dw("gu", dw_gu))
    return dh, dw_gu, dw_down


def _sg_common(u_pre, v_pre, ln_g, ln_b):
    u = _gelu(u_pre)
    v = _gelu(v_pre)
    mu = jnp.mean(v, axis=-1, keepdims=True)
    vc = v - mu
    rstd = lax.rsqrt(jnp.mean(vc * vc, axis=-1, keepdims=True) + EPS)
    vhat = vc * rstd
    vl = vhat * ln_g + ln_b
    return u, vhat, rstd, vl


def _sg_masked_pairs(w):
    t = lax.broadcasted_iota(jnp.int32, (CHUNK, CHUNK), 0)
    s = lax.broadcasted_iota(jnp.int32, (CHUNK, CHUNK), 1)
    causal = s <= t
    wm = [jnp.where(causal, w[g], 0.0).astype(BF16) for g in range(SG_GROUPS)]
    return [jnp.concatenate([wm[2 * j], wm[2 * j + 1]], axis=0) for j in range(SG_GROUPS // 2)], causal


def _sg_mix(vl, pairs, bias):
    tr = vl.shape[0]
    low = lax.broadcasted_iota(jnp.int32, (CHUNK, LANES), 1) < SG_GROUP_DIM
    vb = vl.astype(BF16)
    rows = []
    for c in range(tr // CHUNK):
        slabs = []
        for j in range(SG_GROUPS // 2):
            slab = vb[c * CHUNK:(c + 1) * CHUNK, j * LANES:(j + 1) * LANES]
            m = _dot(pairs[j], slab)
            slabs.append(jnp.where(low, m[:CHUNK], m[CHUNK:]))
        rows.append(jnp.concatenate(slabs, axis=1) + bias)
    return jnp.concatenate(rows, axis=0)


def _sg_fwd(z, ln_g, ln_b, sg_w, bias_full):
    def fn(u_pre, v_pre, ln_g, ln_b, w, bias):
        u, _, _, vl = _sg_common(u_pre, v_pre, ln_g, ln_b)
        pairs, _ = _sg_masked_pairs(w)
        y = u * _sg_mix(vl, pairs, bias)
        return y, y

    return _rowwise(fn, "sg_fwd", 512, [(z, SG_WIDTH, Z_U // SG_WIDTH), (z, SG_WIDTH, Z_V // SG_WIDTH)],
                    [ln_g, ln_b, sg_w, bias_full], [(SG_WIDTH, BF16), (SG_WIDTH, BF16, "T")])


def _sg_bwd(z, dy, ln_g, ln_b, sg_w, bias_full, group_ind, dz):
    def fn(u_pre, v_pre, dy, ln_g, ln_b, w, bias, ind):
        dy = dy.astype(F32)
        u, vhat, rstd, vl = _sg_common(u_pre, v_pre, ln_g, ln_b)
        pairs, causal = _sg_masked_pairs(w)
        mixed = _sg_mix(vl, pairs, bias)
        du_pre = dy * mixed * _gelu_grad(u_pre)
        dmix = dy * u
        tr = dy.shape[0]
        low = lax.broadcasted_iota(jnp.int32, (CHUNK, LANES), 1) < SG_GROUP_DIM
        vb = vl.astype(BF16)
        dw = [jnp.zeros((CHUNK, CHUNK), F32) for _ in range(SG_GROUPS)]
        dbias = jnp.zeros((CHUNK, SG_WIDTH), F32)
        dvl_rows = []
        for c in range(tr // CHUNK):
            dm_c = dmix[c * CHUNK:(c + 1) * CHUNK]
            dbias = dbias + dm_c
            slabs = []
            for j in range(SG_GROUPS // 2):
                slab = vb[c * CHUNK:(c + 1) * CHUNK, j * LANES:(j + 1) * LANES]
                dm = dm_c[:, j * LANES:(j + 1) * LANES]
                d0 = jnp.where(low, dm, 0.0).astype(BF16)
                d1 = jnp.where(low, 0.0, dm).astype(BF16)
                dw[2 * j] = dw[2 * j] + _dot(d0, slab, "nt")
                dw[2 * j + 1] = dw[2 * j + 1] + _dot(d1, slab, "nt")
                slabs.append(_dot(pairs[j], jnp.concatenate([d0, d1], axis=0), "tn"))
            dvl_rows.append(jnp.concatenate(slabs, axis=1))
        dvl = jnp.concatenate(dvl_rows, axis=0)
        dln_g = _rsum(dvl * vhat)
        dln_b = _rsum(dvl)
        dvh = dvl * ln_g
        dv = rstd * (dvh - jnp.mean(dvh, axis=-1, keepdims=True)
                     - vhat * jnp.mean(dvh * vhat, axis=-1, keepdims=True))
        dv_pre = dv * _gelu_grad(v_pre)
        dw = jnp.stack([jnp.where(causal, d, 0.0) for d in dw], axis=0)
        dbias_t = lax.dot_general(dbias, ind, (((1,), (0,)), ((), ())), precision=lax.Precision.HIGHEST,
                                  preferred_element_type=F32)
        return jnp.concatenate([du_pre, dv_pre], axis=1), dw, dbias_t, dln_g, dln_b

    return _rowwise(fn, "sg_bwd", 512,
                    [(z, SG_WIDTH, Z_U // SG_WIDTH), (z, SG_WIDTH, Z_V // SG_WIDTH), dy],
                    [ln_g, ln_b, sg_w, bias_full, group_ind],
                    [("into", dz, 2 * SG_WIDTH, Z_U // (2 * SG_WIDTH))],
                    [((SG_GROUPS, CHUNK, CHUNK), F32), ((CHUNK, SG_GROUPS), F32), ((1, SG_WIDTH), F32), ((1, SG_WIDTH), F32)])


MLA_POST_ROWS = 512


def _rope(x, c, s1, s2):
    return x * c + pltpu.roll(x, LANES - MLA_ROPE // 2, 1) * s1 + pltpu.roll(x, MLA_ROPE // 2, 1) * s2


def _rope_t(d, c, s1, s2):
    return d * c + pltpu.roll(d * s1, MLA_ROPE // 2, 1) + pltpu.roll(d * s2, LANES - MLA_ROPE // 2, 1)


def _mla_post(q_pre, kv_pre, z, tabs, gq, gk):
    scale = MLA_QK ** -0.5 * LOG2E
    T = q_pre.shape[0]
    tr = _tile(T, MLA_POST_ROWS)

    def body(q_ref, k_ref, v_ref, kr_ref, c_ref, s1_ref, s2_ref, gq_ref, gk_ref, qo_ref, ko_ref, vo_ref):
        kr = kr_ref[...].astype(F32)
        c, s1, s2, gq, gk = c_ref[...], s1_ref[...], s2_ref[...], gq_ref[...], gk_ref[...]
        ones_lane = lax.broadcasted_iota(jnp.int32, (tr, LANES), 1) == ONES_LANE
        for h in range(MLA_HEADS):
            sl = slice(h * LANES, (h + 1) * LANES)
            qo_ref[:, sl] = (_rope(_rms(q_ref[:, sl].astype(F32), gq, MLA_QK), c, s1, s2) * scale).astype(BF16)
            ko_ref[:, sl] = _rope(_rms(k_ref[:, sl].astype(F32) + kr, gk, MLA_QK), c, s1, s2).astype(BF16)
            vo_ref[:, sl] = jnp.where(ones_lane, 1.0, v_ref[:, sl].astype(F32)).astype(BF16)

    wide = lambda cb: pl.BlockSpec((tr, HP), lambda i, cb=cb: (i, cb))
    lanes = lambda cb: pl.BlockSpec((tr, LANES), lambda i, cb=cb: (i, cb))
    gain = pl.BlockSpec((1, LANES), lambda i: (0, 0))
    return pl.pallas_call(
        body, name="mla_post", grid=(T // tr,),
        in_specs=[wide(0), wide(0), wide(1), lanes(Z_KR // LANES), lanes(0), lanes(0), lanes(0), gain, gain],
        out_specs=[wide(0)] * 3, out_shape=[jax.ShapeDtypeStruct((T, HP), BF16)] * 3,
        compiler_params=_cparams(("parallel",)),
    )(q_pre, kv_pre, kv_pre, z, *tabs, gq, gk)


def _mla_post_bwd(q_pre, kv_pre, z, tabs, gq, gk, dq, dk, dv):
    scale = MLA_QK ** -0.5
    T = q_pre.shape[0]
    tr = _tile(T, MLA_POST_ROWS)

    def body(q_ref, k_ref, kr_ref, c_ref, s1_ref, s2_ref, dq_ref, dk_ref, dv_ref, gq_ref, gk_ref,
             dqo_ref, dkvo_ref, dkro_ref, dgq_ref, dgk_ref):
        kr = kr_ref[...].astype(F32)
        c, s1, s2, gq, gk = c_ref[...], s1_ref[...], s2_ref[...], gq_ref[...], gk_ref[...]
        lane = lax.broadcasted_iota(jnp.int32, (1, LANES), 1)
        kr_mask = (lane >= KR_LANE) & (lane < KR_LANE + MLA_ROPE)
        dgq = jnp.zeros((1, LANES), F32)
        dgk = jnp.zeros((1, LANES), F32)
        dkr = jnp.zeros((tr, LANES), F32)
        for h in range(MLA_HEADS):
            sl = slice(h * LANES, (h + 1) * LANES)
            dqn = _rope_t(dq_ref[:, sl].astype(F32), c, s1, s2) * scale
            dx, dg = _rms_bwd(q_ref[:, sl].astype(F32), gq, dqn, MLA_QK)
            dqo_ref[:, sl] = dx.astype(BF16)
            dgq = dgq + dg
            dkn = _rope_t(dk_ref[:, sl].astype(F32), c, s1, s2)
            dx, dg = _rms_bwd(k_ref[:, sl].astype(F32) + kr, gk, dkn, MLA_QK)
            dkvo_ref[:, sl] = dx.astype(BF16)
            dkvo_ref[:, HP + h * LANES:HP + (h + 1) * LANES] = dv_ref[:, sl]
            dgk = dgk + dg
            dkr = dkr + dx
        dkro_ref[...] = jnp.where(kr_mask, dkr, 0.0).astype(BF16)
        i = pl.program_id(0)

        @pl.when(i == 0)
        def _():
            dgq_ref[...] = dgq
            dgk_ref[...] = dgk

        @pl.when(i > 0)
        def _():
            dgq_ref[...] += dgq
            dgk_ref[...] += dgk

    wide = lambda cb: pl.BlockSpec((tr, HP), lambda i, cb=cb: (i, cb))
    lanes = lambda cb: pl.BlockSpec((tr, LANES), lambda i, cb=cb: (i, cb))
    gain = pl.BlockSpec((1, LANES), lambda i: (0, 0))
    return pl.pallas_call(
        body, name="mla_post_bwd", grid=(T // tr,),
        in_specs=[wide(0), wide(0), lanes(Z_KR // LANES), lanes(0), lanes(0), lanes(0), wide(0), wide(0), wide(0),
                  gain, gain],
        out_specs=[wide(0), pl.BlockSpec((tr, 2 * HP), lambda i: (i, 0)), lanes(0), gain, gain],
        out_shape=[jax.ShapeDtypeStruct((T, HP), BF16), jax.ShapeDtypeStruct((T, 2 * HP), BF16),
                   jax.ShapeDtypeStruct((T, LANES), BF16), jax.ShapeDtypeStruct((1, LANES), F32),
                   jax.ShapeDtypeStruct((1, LANES), F32)],
        compiler_params=_cparams(("arbitrary",)),
    )(q_pre, kv_pre, z, *tabs, dq, dk, dv, gq, gk)


def _pairs(n, lower):
    a, b = [], []
    for o in range(n):
        inner = range(o + 1) if lower else range(o, n)
        for t in inner:
            a.append(o)
            b.append(t)
    return jnp.asarray(np.array(a, np.int32)), jnp.asarray(np.array(b, np.int32))


FLASH_TILE, FLASH_SUB_ROWS = 2048, 512
LOG2E, LN2 = 1.4426950408889634, 0.6931471805599453
ONES_LANE = MLA_V


def _flash_tiles(T):
    tq = _tile(T, FLASH_TILE)
    return tq, _tile(tq, FLASH_SUB_ROWS)


def _col_span(t, sr, rb, diag, key_major):
    if not diag:
        return 0, t
    return (rb * sr, t) if key_major else (0, (rb + 1) * sr)


def _span_iota(sr, rb, c0, c1):
    r = lax.broadcasted_iota(jnp.int32, (sr, c1 - c0), 0) + rb * sr
    c = lax.broadcasted_iota(jnp.int32, (sr, c1 - c0), 1) + c0
    return r, c


def _lanes(x, width):
    return jnp.concatenate([x] * (width // LANES), axis=1)


def _flash_fwd(q, k, v):
    T = q.shape[0]
    tq, sr = _flash_tiles(T)
    n = T // tq
    ii, jj = _pairs(n, True)

    def body(ii_ref, jj_ref, q_ref, k_ref, v_ref, o_ref, ot_ref, lse_ref, lset_ref, m_sc, acc_sc):
        p_ = pl.program_id(1)
        i, j = ii_ref[p_], jj_ref[p_]

        @pl.when(j == 0)
        def _():
            m_sc[...] = jnp.full(m_sc.shape, NEG, F32)
            acc_sc[...] = jnp.zeros(acc_sc.shape, F32)

        def tile(diag):
            nrb = tq // sr

            def scores(rb):
                c0, c1 = _col_span(tq, sr, rb, diag, False)
                return _dot(q_ref[rb * sr:(rb + 1) * sr, :], k_ref[c0:c1, :], "nt")

            s_next = scores(0)
            for rb in range(nrb):
                rows = slice(rb * sr, (rb + 1) * sr)
                c0, c1 = _col_span(tq, sr, rb, diag, False)
                s, s_next = s_next, (scores(rb + 1) if rb + 1 < nrb else None)
                if diag:
                    r, c = _span_iota(sr, rb, c0, c1)
                    s = jnp.where(c <= r, s, NEG)
                m = m_sc[rows, :]
                m_new = jnp.maximum(m, jnp.max(s, axis=1, keepdims=True))
                p = jnp.exp2(s - _lanes(m_new, c1 - c0))
                acc_sc[rows, :] = jnp.exp2(m - m_new) * acc_sc[rows, :] + _dot(p, v_ref[c0:c1, :])
                m_sc[rows, :] = m_new

        @pl.when(j < i)
        def _():
            tile(False)

        @pl.when(j == i)
        def _():
            tile(True)
            acc = acc_sc[...]
            lane = lax.broadcasted_iota(jnp.int32, acc.shape, 1)
            l = jnp.sum(jnp.where(lane == ONES_LANE, acc, 0.0), axis=1, keepdims=True)
            o = jnp.where(lane < MLA_V, acc / l, 0.0)
            o_ref[...] = o.astype(o_ref.dtype)
            ot_ref[...] = o.T.astype(ot_ref.dtype)
            lse = m_sc[...] + jnp.log2(l)
            lse_ref[...] = lse
            lset_ref[...] = lse.T[:8]

    blk = lambda which: pl.BlockSpec((tq, LANES), which)
    qmap = lambda h, p, ii, jj: (ii[p], h)
    kmap = lambda h, p, ii, jj: (jj[p], h)
    tmap = lambda h, p, ii, jj: (h, ii[p])
    return pl.pallas_call(
        body, name="mla_flash_fwd",
        grid_spec=pltpu.PrefetchScalarGridSpec(
            num_scalar_prefetch=2, grid=(MLA_HEADS, int(ii.shape[0])),
            in_specs=[blk(qmap), blk(kmap), blk(kmap)],
            out_specs=[blk(qmap), pl.BlockSpec((LANES, tq), tmap), blk(qmap), pl.BlockSpec((8, tq), tmap)],
            scratch_shapes=[pltpu.VMEM((tq, LANES), F32)] * 2),
        out_shape=[jax.ShapeDtypeStruct((T, HP), BF16), jax.ShapeDtypeStruct((HP, T), BF16),
                   jax.ShapeDtypeStruct((T, HP), F32), jax.ShapeDtypeStruct((8 * MLA_HEADS, T), F32)],
        compiler_params=_cparams(("parallel", "arbitrary")),
    )(ii, jj, q, k, v)


def _flash_dq(q, k, v, do, lse, delta):
    T = q.shape[0]
    tq, sr = _flash_tiles(T)
    n = T // tq
    ii, jj = _pairs(n, True)

    def body(ii_ref, jj_ref, q_ref, k_ref, v_ref, do_ref, lse_ref, dl_ref, dq_ref, acc_sc):
        p_ = pl.program_id(1)
        i, j = ii_ref[p_], jj_ref[p_]

        @pl.when(j == 0)
        def _():
            acc_sc[...] = jnp.zeros(acc_sc.shape, F32)

        def tile(diag):
            nrb = tq // sr

            def products(rb):
                rows = slice(rb * sr, (rb + 1) * sr)
                c0, c1 = _col_span(tq, sr, rb, diag, False)
                return _dot(q_ref[rows, :], k_ref[c0:c1, :], "nt"), _dot(do_ref[rows, :], v_ref[c0:c1, :], "nt")

            nxt = products(0)
            for rb in range(nrb):
                rows = slice(rb * sr, (rb + 1) * sr)
                c0, c1 = _col_span(tq, sr, rb, diag, False)
                (s, dp), nxt = nxt, (products(rb + 1) if rb + 1 < nrb else None)
                p = jnp.exp2(s - _lanes(lse_ref[rows, :], c1 - c0))
                if diag:
                    r, c = _span_iota(sr, rb, c0, c1)
                    p = jnp.where(c <= r, p, 0.0)
                acc_sc[rows, :] += _dot(p * (dp - _lanes(dl_ref[rows, :], c1 - c0)), k_ref[c0:c1, :])

        @pl.when(j < i)
        def _():
            tile(False)

        @pl.when(j == i)
        def _():
            tile(True)
            dq_ref[...] = acc_sc[...].astype(dq_ref.dtype)

    blk = lambda which: pl.BlockSpec((tq, LANES), which)
    qmap = lambda h, p, ii, jj: (ii[p], h)
    kmap = lambda h, p, ii, jj: (jj[p], h)
    return pl.pallas_call(
        body, name="mla_flash_dq",
        grid_spec=pltpu.PrefetchScalarGridSpec(
            num_scalar_prefetch=2, grid=(MLA_HEADS, int(ii.shape[0])),
            in_specs=[blk(qmap), blk(kmap), blk(kmap), blk(qmap), blk(qmap), blk(qmap)],
            out_specs=blk(qmap),
            scratch_shapes=[pltpu.VMEM((tq, LANES), F32)]),
        out_shape=jax.ShapeDtypeStruct((T, HP), BF16),
        compiler_params=_cparams(("parallel", "arbitrary")),
    )(ii, jj, q, k, v, do, lse, delta)


def _flash_dkv(q, k, v, do, lse_t, delta_t):
    T = q.shape[0]
    tq, sr = _flash_tiles(T)
    n = T // tq
    jj, ii = _pairs(n, False)

    def body(jj_ref, ii_ref, q_ref, k_ref, v_ref, do_ref, lse_ref, dl_ref, dk_ref, dv_ref, dk_sc, dv_sc):
        p_ = pl.program_id(1)
        j, i = jj_ref[p_], ii_ref[p_]

        @pl.when(i == j)
        def _():
            dk_sc[...] = jnp.zeros(dk_sc.shape, F32)
            dv_sc[...] = jnp.zeros(dv_sc.shape, F32)

        def tile(diag):
            nrb = tq // sr

            def products(rb):
                rows = slice(rb * sr, (rb + 1) * sr)
                c0, c1 = _col_span(tq, sr, rb, diag, True)
                return _dot(k_ref[rows, :], q_ref[c0:c1, :], "nt"), _dot(v_ref[rows, :], do_ref[c0:c1, :], "nt")

            nxt = products(0)
            for rb in range(nrb):
                rows = slice(rb * sr, (rb + 1) * sr)
                c0, c1 = _col_span(tq, sr, rb, diag, True)
                (st, dpt), nxt = nxt, (products(rb + 1) if rb + 1 < nrb else None)
                pt = jnp.exp2(st - lse_ref[:1, c0:c1])
                if diag:
                    r, c = _span_iota(sr, rb, c0, c1)
                    pt = jnp.where(r <= c, pt, 0.0)
                dv_sc[rows, :] += _dot(pt, do_ref[c0:c1, :])
                dk_sc[rows, :] += _dot(pt * (dpt - dl_ref[:1, c0:c1]), q_ref[c0:c1, :])

        @pl.when(i == j)
        def _():
            tile(True)

        @pl.when(i > j)
        def _():
            tile(False)

        @pl.when(i == n - 1)
        def _():
            dk_ref[...] = (dk_sc[...] * LN2).astype(dk_ref.dtype)
            dv_ref[...] = dv_sc[...].astype(dv_ref.dtype)

    blk = lambda which: pl.BlockSpec((tq, LANES), which)
    qmap = lambda h, p, jj, ii: (ii[p], h)
    kmap = lambda h, p, jj, ii: (jj[p], h)
    lse_rows = pl.BlockSpec((8, tq), lambda h, p, jj, ii: (h, ii[p]))
    delta_rows = pl.BlockSpec((8, tq), lambda h, p, jj, ii: (h * (LANES // 8), ii[p]))
    return pl.pallas_call(
        body, name="mla_flash_dkv",
        grid_spec=pltpu.PrefetchScalarGridSpec(
            num_scalar_prefetch=2, grid=(MLA_HEADS, int(ii.shape[0])),
            in_specs=[blk(qmap), blk(kmap), blk(kmap), blk(qmap), lse_rows, delta_rows],
            out_specs=[blk(kmap), blk(kmap)],
            scratch_shapes=[pltpu.VMEM((tq, LANES), F32)] * 2),
        out_shape=[jax.ShapeDtypeStruct((T, HP), BF16)] * 2,
        compiler_params=_cparams(("parallel", "arbitrary")),
    )(jj, ii, q, k, v, do, lse_t, delta_t)


def _mem_fwd(z, km, vm, gq):
    scale = MEM_HEAD_DIM ** -0.5

    def fn(qm, km, vm, gq):
        ys = []
        for h in range(MEM_HEADS):
            sl = slice(h * LANES, (h + 1) * LANES)
            q = _rms(qm[:, sl], gq) * scale
            s = _dot(q, km[:, sl], "nt")
            p = jnp.exp(s - jnp.max(s, axis=1, keepdims=True))
            p = p / jnp.sum(p, axis=1, keepdims=True)
            ys.append(_dot(p, vm[:, sl]))
        y = jnp.concatenate(ys, axis=1)
        return y, y

    return _rowwise(fn, "mem_fwd", 512, [(z, MEM_WIDTH, Z_QM // MEM_WIDTH)], [km, vm, gq],
                    [(MEM_WIDTH, BF16), (MEM_WIDTH, BF16, "T")])


def _mem_bwd(z, dy, km, vm, gq, dz):
    scale = MEM_HEAD_DIM ** -0.5

    def fn(qm, dy, km, vm, gq):
        dqs, dks, dvs = [], [], []
        dgq = jnp.zeros((1, LANES), F32)
        for h in range(MEM_HEADS):
            sl = slice(h * LANES, (h + 1) * LANES)
            q = (_rms(qm[:, sl], gq) * scale).astype(BF16)
            dyh = dy[:, sl]
            kh, vh = km[:, sl], vm[:, sl]
            s = _dot(q, kh, "nt")
            p = jnp.exp(s - jnp.max(s, axis=1, keepdims=True))
            p = p / jnp.sum(p, axis=1, keepdims=True)
            dp = _dot(dyh, vh, "nt")
            ds = p * (dp - jnp.sum(p * dp, axis=1, keepdims=True))
            dq = _dot(ds, kh) * scale
            dx, dg = _rms_bwd(qm[:, sl], gq, dq)
            dqs.append(dx)
            dgq = dgq + dg
            st = _dot(kh, q, "nt")
            pt = jnp.exp(st - jnp.max(st, axis=0, keepdims=True))
            pt = pt / jnp.sum(pt, axis=0, keepdims=True)
            dpt = _dot(vh, dyh, "nt")
            dst = pt * (dpt - jnp.sum(pt * dpt, axis=0, keepdims=True))
            dvs.append(_dot(pt, dyh))
            dks.append(_dot(dst, q))
        return jnp.concatenate(dqs, axis=1), jnp.concatenate(dks, axis=1), jnp.concatenate(dvs, axis=1), dgq

    m = km.shape[0]
    return _rowwise(fn, "mem_bwd", 512, [(z, MEM_WIDTH, Z_QM // MEM_WIDTH), dy], [km, vm, gq],
                    [("into", dz, MEM_WIDTH, Z_QM // MEM_WIDTH)],
                    [((m, MEM_WIDTH), F32), ((m, MEM_WIDTH), F32), ((1, LANES), F32)])


GROUPS = {"ffn1": ["ffn1_w_gu"], "ffn1_down": ["ffn1_w_down"],
          "mix": ["w_in", "mla_w_uq", "mla_w_ukv", "mem_w_kv", "w_branch_a", "w_branch_b", "w_branch_c", "w_out"],
          "ffn2": ["ffn2_w_gu", "ffn2_w_down"]}
GRAD_GROUPS = {"ffn2": GROUPS["ffn2"], "mix": GROUPS["mix"], "ffn1_down": ["ffn1_w_down"], "ffn1_gu": ["ffn1_w_gu"]}


def _local_step(x, mem, positions, loss_target, P, weights, grads_out):
    T = x.shape[0]
    G = {}
    W = dict(weights("ffn1", None))

    half = MLA_ROPE // 2
    inv = ROPE_BASE ** (-jnp.arange(half, dtype=F32) / half)
    ang = positions.astype(F32)[:, None] * inv
    cos, sin = jnp.cos(ang), jnp.sin(ang)
    one, zero = jnp.ones((T, MLA_NOPE), F32), jnp.zeros((T, half), F32)
    pad = LANES - MLA_QK
    tabs = (jnp.concatenate([one, cos, cos, jnp.ones((T, pad), F32)], axis=1),
            jnp.concatenate([jnp.zeros((T, MLA_NOPE), F32), -sin, zero, jnp.zeros((T, pad), F32)], axis=1),
            jnp.concatenate([jnp.zeros((T, MLA_NOPE), F32), zero, sin, jnp.zeros((T, pad), F32)], axis=1))
    gq_p = jnp.pad(P["mla_q_norm"], ((0, 0), (0, pad)))
    gk_p = jnp.pad(P["mla_k_norm"], ((0, 0), (0, pad)))
    bias_full = jnp.repeat(P["sg_b"].T, SG_GROUP_DIM, axis=1)
    group_ind = jnp.repeat(jnp.eye(SG_GROUPS, dtype=F32), SG_GROUP_DIM, axis=0)

    HT = (D_MODEL, BF16, "T")

    def norm2(x, g):
        h = _rms(x, g)
        return h, h

    h1, h1t = _rowwise(norm2, "ffn1_norm", 512, [x], [P["ffn1_norm"]], [(D_MODEL, BF16), HT])
    def ffn1_w_down(after):
        W.update(weights("ffn1_down", after))
        return W["ffn1_w_down"]

    gu1, a1t, o1 = _ffn_fwd(h1, W["ffn1_w_gu"], ffn1_w_down, "ffn1")

    def resid_norm(x, o, g):
        xn = x + 0.5 * o
        h = _rms(xn, g)
        return xn, h, h

    x1, hm, hmt = _rowwise(resid_norm, "mix_norm", 512, [x, o1], [P["mix_norm"]],
                           [(D_MODEL, F32), (D_MODEL, BF16), HT])
    W.update(weights("mix", hm))
    z = _mm(hm, W["w_in"], "nn", BF16, "w_in", tm=1024, tn=1792)

    y_a, y_at = _sg_fwd(z, P["sg_ln_g"], P["sg_ln_b"], P["sg_w"], bias_full)

    def c_norm(cq, ckv, gq, gkv):
        a, b = _rms(cq, gq), _rms(ckv, gkv)
        return a, b, a, b

    cqn, ckvn, cqnt, ckvnt = _rowwise(
        c_norm, "mla_cnorm", 512, [(z, MLA_Q_RANK, Z_CQ // MLA_Q_RANK), (z, MLA_KV_RANK, Z_CKV // MLA_KV_RANK)],
        [P["mla_cq_norm"], P["mla_ckv_norm"]],
        [(MLA_Q_RANK, BF16), (MLA_KV_RANK, BF16), (MLA_Q_RANK, BF16, "T"), (MLA_KV_RANK, BF16, "T")])
    q_pre = _mm(cqn, W["mla_w_uq"], "nn", BF16, "mla_uq", tm=1024, tn=1024)
    kv_pre = _mm(ckvn, W["mla_w_ukv"], "nn", BF16, "mla_ukv", tm=1024, tn=1024)
    q, k, v = _mla_post(q_pre, kv_pre, z, tabs, gq_p, gk_p)
    y_b, y_bt, lse, lse_t = _flash_fwd(q, k, v)

    memn, = _rowwise(lambda m, g: _rms(m, g), "mem_norm", 256, [mem], [P["mem_norm"]], [(D_MODEL, BF16)])
    kvm = _mm(memn, W["mem_w_kv"], "nn", F32, "mem_kv")

    def mem_k(kvm, gk):
        ks = [_rms(kvm[:, h * LANES:(h + 1) * LANES], gk) for h in range(MEM_HEADS)]
        return jnp.concatenate(ks, axis=1), kvm[:, MEM_WIDTH:]

    km, vm = _rowwise(mem_k, "mem_knorm", 256, [kvm], [P["mem_k_norm"]], [(MEM_WIDTH, BF16), (MEM_WIDTH, BF16)])
    y_c, y_ct = _mem_fwd(z, km, vm, P["mem_q_norm"])

    pa = _mm(y_a, W["w_branch_a"], "nn", BF16, "branch_a", tm=1024, tn=1024)
    pb = _mm(y_b, W["w_branch_b"], "nn", BF16, "branch_b", tm=1024, tn=1024)
    pc = _mm(y_c, W["w_branch_c"], "nn", BF16, "branch_c", tm=1024, tn=1024)

    def merge(zg, pa, pb, pc, b):
        g = _sigmoid(zg + b)
        m = g[:, :D_MODEL] * pa + g[:, D_MODEL:2 * D_MODEL] * pb + g[:, 2 * D_MODEL:] * pc
        return m, m

    merged, mergedt = _rowwise(merge, "merge", 256, [(z, 3 * D_MODEL, 0), pa, pb, pc], [P["b_gate"]],
                               [(D_MODEL, BF16), HT])
    om = _mm(merged, W["w_out"], "nn", BF16, "w_out", tm=1024, tn=1024)

    def resid_norm1(x, o, g):
        xn = x + o
        h = _rms(xn, g)
        return xn, h, h

    x2, h2, h2t = _rowwise(resid_norm1, "ffn2_norm", 512, [x1, om], [P["ffn2_norm"]],
                           [(D_MODEL, F32), (D_MODEL, BF16), HT])
    W.update(weights("ffn2", h2))
    gu2, a2t, o2 = _ffn_fwd(h2, W["ffn2_w_gu"], W["ffn2_w_down"], "ffn2")

    def loss_fn(x2, o2, t):
        e = x2 + 0.5 * o2 - t
        return e * (1.0 / D_MODEL), (e * (0.5 / D_MODEL)).astype(BF16), _rsum(e * e) * (0.5 / D_MODEL)

    dx3, do2, loss_part = _rowwise(loss_fn, "loss", 512, [x2, o2, loss_target], [],
                                   [(D_MODEL, F32), (D_MODEL, BF16)], [((1, D_MODEL), F32)])

    dh2, G["ffn2_w_gu"], G["ffn2_w_down"] = _ffn_bwd(do2, h2t, gu2, a2t, W["ffn2_w_gu"], W["ffn2_w_down"], "ffn2")
    tie = grads_out("ffn2", G)

    def norm_bwd(x, dh, dxo, g, *_):
        dx, dg = _rms_bwd(x, g, dh)
        dx = dx + dxo
        return dx, dx, dg

    dx2, dx2b, G["ffn2_norm"] = _rowwise(norm_bwd, "ffn2_norm_bwd", 512, [x2, dh2, dx3],
                                         [P["ffn2_norm"]] + ([] if tie is None else [tie]),
                                         [(D_MODEL, F32), (D_MODEL, BF16)], [((1, D_MODEL), F32)])

    G["w_out"] = _mm_t(mergedt, dx2b, "w_out_dw", tm=1024, tn=1024)
    dmerged = _mm(dx2b, W["w_out"], "nt", BF16, "w_out_dx", tm=1024, tn=1024)

    def merge_bwd(zg, pa, pb, pc, dm, b):
        g = _sigmoid(zg + b)
        ps = jnp.concatenate([pa, pb, pc], axis=1)
        dm3 = jnp.concatenate([dm, dm, dm], axis=1)
        dzg = dm3 * ps * g * (1.0 - g)
        dp = dm3 * g
        return dzg, dp[:, :D_MODEL], dp[:, D_MODEL:2 * D_MODEL], dp[:, 2 * D_MODEL:], _rsum(dzg)

    dz = lax.empty((T, Z_COLS), BF16)
    dz, dpa, dpb, dpc, G["b_gate"] = _rowwise(
        merge_bwd, "merge_bwd", 256, [(z, 3 * D_MODEL, 0), pa, pb, pc, dmerged], [P["b_gate"]],
        [("into", dz, 3 * D_MODEL, 0), (D_MODEL, BF16), (D_MODEL, BF16), (D_MODEL, BF16)], [((1, 3 * D_MODEL), F32)])

    G["w_branch_a"] = _mm_t(y_at, dpa, "branch_a_dw", tm=512, tn=1024)
    G["w_branch_b"] = _mm_t(y_bt, dpb, "branch_b_dw", tm=1024, tn=1024)
    G["w_branch_c"] = _mm_t(y_ct, dpc, "branch_c_dw", tm=512, tn=1024)
    dy_a = _mm(dpa, W["w_branch_a"], "nt", BF16, "branch_a_dx", tm=1024, tn=512)
    dy_b = _mm(dpb, W["w_branch_b"], "nt", BF16, "branch_b_dx", tm=1024, tn=1024)
    dy_c = _mm(dpc, W["w_branch_c"], "nt", BF16, "branch_c_dx", tm=1024, tn=512)

    dz, G["sg_w"], dbias_t, G["sg_ln_g"], G["sg_ln_b"] = _sg_bwd(
        z, dy_a, P["sg_ln_g"], P["sg_ln_b"], P["sg_w"], bias_full, group_ind, dz)
    G["sg_b"] = dbias_t.T

    dz, dkm, dvm, G["mem_q_norm"] = _mem_bwd(z, dy_c, km, vm, P["mem_q_norm"], dz)

    def mem_k_bwd(kvm, dkm, dvm, gk):
        dks = []
        dg = jnp.zeros((1, LANES), F32)
        for h in range(MEM_HEADS):
            sl = slice(h * LANES, (h + 1) * LANES)
            dx, d = _rms_bwd(kvm[:, sl], gk, dkm[:, sl])
            dks.append(dx)
            dg = dg + d
        return jnp.concatenate(dks + [dvm], axis=1), dg

    dkvm, G["mem_k_norm"] = _rowwise(mem_k_bwd, "mem_knorm_bwd", 256, [kvm, dkm, dvm], [P["mem_k_norm"]],
                                     [(2 * MEM_WIDTH, BF16)], [((1, LANES), F32)])
    G["mem_w_kv"] = _mm(memn, dkvm, "tn", BF16, "mem_kv_dw")
    dmemn = _mm(dkvm, W["mem_w_kv"], "nt", F32, "mem_kv_dx")
    _, G["mem_norm"] = _rowwise(lambda m, d, g: _rms_bwd(m, g, d), "mem_norm_bwd", 256, [mem, dmemn],
                                [P["mem_norm"]], [(D_MODEL, BF16)], [((1, D_MODEL), F32)])

    def delta_fn(o, do):
        od = o.astype(F32) * do.astype(F32)
        ds = [jnp.broadcast_to(jnp.sum(od[:, h * LANES:(h + 1) * LANES], axis=1, keepdims=True), (od.shape[0], LANES))
              for h in range(MLA_HEADS)]
        d = jnp.concatenate(ds, axis=1)
        return d, d

    delta, delta_t = _rowwise(delta_fn, "mla_delta", 512, [y_b, dy_b], [], [(HP, F32), (HP, F32, "T")])
    dq = _flash_dq(q, k, v, dy_b, lse, delta)
    dk, dv = _flash_dkv(q, k, v, dy_b, lse_t, delta_t)
    dq_pre, dkv_pre, dkr, dgq, dgk = _mla_post_bwd(q_pre, kv_pre, z, tabs, gq_p, gk_p, dq, dk, dv)
    G["mla_q_norm"], G["mla_k_norm"] = dgq[:, :MLA_QK], dgk[:, :MLA_QK]
    G["mla_w_uq"] = _mm_t(cqnt, dq_pre, "mla_uq_dw", tm=384, tn=1024)
    G["mla_w_ukv"] = _mm_t(ckvnt, dkv_pre, "mla_ukv_dw", tm=256, tn=2048)
    dcqn = _mm(dq_pre, W["mla_w_uq"], "nt", BF16, "mla_uq_dx", tm=1024)
    dckvn = _mm(dkv_pre, W["mla_w_ukv"], "nt", BF16, "mla_ukv_dx", tm=1024)

    def c_norm_bwd(cq, ckv, dcqn, dckvn, dkr, gq, gkv):
        dcq, dgq = _rms_bwd(cq, gq, dcqn)
        dckv, dgkv = _rms_bwd(ckv, gkv, dckvn)
        return jnp.concatenate([dckv, dkr, dcq], axis=1), dgq, dgkv

    tail = Z_COLS - Z_CKV
    dz, G["mla_cq_norm"], G["mla_ckv_norm"] = _rowwise(
        c_norm_bwd, "mla_cnorm_bwd", 512,
        [(z, MLA_Q_RANK, Z_CQ // MLA_Q_RANK), (z, MLA_KV_RANK, Z_CKV // MLA_KV_RANK), dcqn, dckvn, dkr],
        [P["mla_cq_norm"], P["mla_ckv_norm"]], [("into", dz, tail, Z_CKV // tail)],
        [((1, MLA_Q_RANK), F32), ((1, MLA_KV_RANK), F32)])
    G["w_in"] = _mm_t(hmt, dz, "w_in_dw", tm=1024, tn=1792, tk=2048)
    dhm = _mm(dz, W["w_in"], "nt", BF16, "w_in_dx", tm=1024, tn=1024, tk=2688)

    def norm_bwd_half(x, dh, dxo, g):
        dx, dg = _rms_bwd(x, g, dh)
        dx = dx + dxo
        return dx, (0.5 * dx), dg

    dx1, do1, G["mix_norm"] = _rowwise(norm_bwd_half, "mix_norm_bwd", 512, [x1, dhm, dx2], [P["mix_norm"]],
                                       [(D_MODEL, F32), (D_MODEL, BF16)], [((1, D_MODEL), F32)])
    tie = grads_out("mix", G)

    def ffn1_dw(which, dw):
        G["ffn1_w_" + which] = dw
        return grads_out("ffn1_" + which, G)

    dh1, _, _ = _ffn_bwd(do1, h1t, gu1, a1t, W["ffn1_w_gu"], W["ffn1_w_down"], "ffn1", tie, ffn1_dw)

    def norm_bwd_last(x, dh, dxo, g):
        dx, dg = _rms_bwd(x, g, dh)
        return dx + dxo, dg

    grad_x, G["ffn1_norm"] = _rowwise(norm_bwd_last, "ffn1_norm_bwd", 512, [x, dh1, dx1], [P["ffn1_norm"]],
                                      [(D_MODEL, F32)], [((1, D_MODEL), F32)])
    return loss_part, grad_x, G


SHARDED = ["ffn1_w_gu", "ffn1_w_down", "w_in", "mla_w_uq", "mla_w_ukv", "mem_w_kv",
           "w_branch_a", "w_branch_b", "w_branch_c", "w_out", "ffn2_w_gu", "ffn2_w_down"]
ROW_SHARDED = {"ffn1_w_down", "mem_w_kv", "w_out", "ffn2_w_down"}
SMALL = ["ffn1_norm", "mix_norm", "b_gate", "sg_ln_g", "sg_ln_b", "sg_w", "sg_b", "mla_cq_norm",
         "mla_ckv_norm", "mla_q_norm", "mla_k_norm", "mem_norm", "mem_q_norm", "mem_k_norm", "ffn2_norm"]
ORDER = ["ffn1_norm", "ffn1_w_gu", "ffn1_w_down", "mix_norm", "w_in", "b_gate", "sg_ln_g", "sg_ln_b", "sg_w",
         "sg_b", "mla_cq_norm", "mla_w_uq", "mla_ckv_norm", "mla_w_ukv", "mla_q_norm", "mla_k_norm", "mem_norm",
         "mem_w_kv", "mem_q_norm", "mem_k_norm", "w_branch_a", "w_branch_b", "w_branch_c", "w_out", "ffn2_norm",
         "ffn2_w_gu", "ffn2_w_down"]

_IN_U, _IN_V, _IN_CQ, _IN_CKV, _IN_KR, _IN_QM, _IN_G = 0, 512, 1024, 1408, 1664, 1696, 2208
IN_COLS = 5280


def _full_from_slabs(name, slabs):
    n, r, c = slabs.shape
    if name in ROW_SHARDED:
        return slabs.reshape(n * r, c)
    return slabs.transpose(1, 0, 2).reshape(r, n * c)


def _slabs_from_full(name, full):
    if name in ROW_SHARDED:
        return full.reshape(N_DEV, full.shape[0] // N_DEV, full.shape[1])
    r, c = full.shape
    return full.reshape(r, N_DEV, c // N_DEV).transpose(1, 0, 2)


def _compute_layout(full):
    W = dict(full)
    if "w_in" not in full:
        return W
    w = full["w_in"]
    kr = jnp.pad(w[:, _IN_KR:_IN_QM], ((0, 0), (KR_LANE, LANES - KR_LANE - MLA_ROPE)))
    W["w_in"] = jnp.concatenate([w[:, _IN_G:], w[:, _IN_U:_IN_CQ], w[:, _IN_QM:_IN_G], w[:, _IN_CKV:_IN_KR], kr,
                                 w[:, _IN_CQ:_IN_CKV]], axis=1)
    uq = full["mla_w_uq"].reshape(MLA_Q_RANK, MLA_HEADS, MLA_QK)
    W["mla_w_uq"] = jnp.pad(uq, ((0, 0), (0, 0), (0, LANES - MLA_QK))).reshape(MLA_Q_RANK, HP)
    ukv = full["mla_w_ukv"].reshape(MLA_KV_RANK, MLA_HEADS, MLA_NOPE + MLA_V)
    padh = lambda a: jnp.pad(a, ((0, 0), (0, 0), (0, LANES - a.shape[2]))).reshape(MLA_KV_RANK, HP)
    W["mla_w_ukv"] = jnp.concatenate([padh(ukv[:, :, :MLA_NOPE]), padh(ukv[:, :, MLA_NOPE:])], axis=1)
    wb = full["w_branch_b"].reshape(MLA_HEADS, MLA_V, D_MODEL)
    W["w_branch_b"] = jnp.pad(wb, ((0, 0), (0, LANES - MLA_V), (0, 0))).reshape(HP, D_MODEL)
    return W


def _reference_layout(G):
    out = dict(G)
    if "w_in" not in G:
        return out
    g = G["w_in"]
    out["w_in"] = jnp.concatenate([
        g[:, Z_U:Z_QM], g[:, Z_CQ:Z_COLS], g[:, Z_CKV:Z_KR], g[:, Z_KR + KR_LANE:Z_KR + KR_LANE + MLA_ROPE],
        g[:, Z_QM:Z_CKV], g[:, Z_G:Z_U]], axis=1)
    out["mla_w_uq"] = G["mla_w_uq"].reshape(MLA_Q_RANK, MLA_HEADS, LANES)[:, :, :MLA_QK].reshape(MLA_Q_RANK, -1)
    gk = G["mla_w_ukv"][:, :HP].reshape(MLA_KV_RANK, MLA_HEADS, LANES)[:, :, :MLA_NOPE]
    gv = G["mla_w_ukv"][:, HP:].reshape(MLA_KV_RANK, MLA_HEADS, LANES)[:, :, :MLA_V]
    out["mla_w_ukv"] = jnp.concatenate([gk, gv], axis=2).reshape(MLA_KV_RANK, -1)
    out["w_branch_b"] = G["w_branch_b"].reshape(MLA_HEADS, LANES, D_MODEL)[:, :MLA_V].reshape(-1, D_MODEL)
    return out


def _pack(parts):
    flat = []
    for a in parts:
        a = a.reshape(-1)
        flat.append(jnp.pad(a, (0, (-a.shape[0]) % LANES)))
    return jnp.concatenate(flat).reshape(-1, LANES)


def _unpack(packed, shapes):
    flat = packed.reshape(-1)
    out, off = [], 0
    for shp in shapes:
        n = int(np.prod(shp))
        out.append(flat[off:off + n].reshape(shp))
        off += n + (-n) % LANES
    return out


MESH = pl.DeviceIdType.MESH
HBM = pl.BlockSpec(memory_space=pltpu.HBM)


def _all_gather(shards):
    n = len(shards)

    def body(*refs):
        x_refs, out_refs, token_ref = refs[:n], refs[n:2 * n], refs[2 * n]
        send_sems, recv_sems, local_sems = refs[2 * n + 1:]
        x, y, c = lax.axis_index("x"), lax.axis_index("y"), lax.axis_index("c")
        me, sibling = (x, y, c), (x, y, 1 - c)
        chips = [(1 - x, y), (x, 1 - y), (1 - x, 1 - y)]
        token_ref[...] = jnp.zeros_like(token_ref)

        def slot(a, px, py, pc):
            return out_refs[a].at[4 * px + 2 * py + pc]

        def copy(a, k, block, to, src=None):
            return pltpu.make_async_remote_copy(
                src_ref=slot(a, *block) if src is None else src, dst_ref=slot(a, *block),
                send_sem=send_sems.at[7 * a + k], recv_sem=recv_sems.at[7 * a + k], device_id=to, device_id_type=MESH)

        arrays = range(n)
        mine = [pltpu.make_async_copy(x_refs[a], slot(a, *me), local_sems.at[a]) for a in arrays]
        for cp in mine:
            cp.start()
        first = [copy(a, 0, me, sibling, src=x_refs[a]) for a in arrays]
        first += [copy(a, 1 + j, me, (*chip, c), src=x_refs[a]) for j, chip in enumerate(chips) for a in arrays]
        for cp in first:
            cp.start()
        passed = []
        for j, chip in enumerate(chips):
            for a in arrays:
                copy(a, 1 + j, (*chip, c), me).wait_recv()
                passed.append(copy(a, 4 + j, (*chip, c), sibling))
                passed[-1].start()
        for a in arrays:
            copy(a, 0, sibling, me).wait_recv()
        for j, chip in enumerate(chips):
            for a in arrays:
                copy(a, 4 + j, (*chip, 1 - c), me).wait_recv()
        for cp in first + passed:
            cp.wait_send()
        for cp in mine:
            cp.wait()

    res = pl.pallas_call(
        body, name="all_gather_weights",
        out_shape=[jax.ShapeDtypeStruct((N_DEV,) + s.shape, s.dtype) for s in shards]
        + [jax.ShapeDtypeStruct((8, LANES), F32)],
        in_specs=[HBM] * n, out_specs=[HBM] * n + [pl.BlockSpec(memory_space=pltpu.VMEM)],
        scratch_shapes=[pltpu.SemaphoreType.DMA((7 * n,)), pltpu.SemaphoreType.DMA((7 * n,)),
                        pltpu.SemaphoreType.DMA((n,))],
    )(*shards)
    return res[:n], res[n]


SEM = pl.BlockSpec(memory_space=pltpu.SEMAPHORE)
DATAFLOW = pltpu.SideEffectType.DATAFLOW_SIDE_EFFECTING


def _peers():
    x, y, c = lax.axis_index("x"), lax.axis_index("y"), lax.axis_index("c")
    out = []
    for k in range(1, N_DEV):
        px = 1 - x if k & 4 else x
        py = 1 - y if k & 2 else y
        pc = 1 - c if k & 1 else c
        out.append((k, (px, py, pc), 4 * px + 2 * py + pc))
    return 4 * x + 2 * y + c, out


def _send_start(srcs, per_peer, name):
    n = len(srcs)
    lands = [lax.empty((N_DEV,) + (s.shape[1:] if per_peer else s.shape), s.dtype) for s in srcs]

    def body(*refs):
        src_refs, land_refs, send_sems, recv_sems, token = refs[:n], refs[n:2 * n], refs[2 * n], refs[2 * n + 1], refs[-1]
        me, peers = _peers()
        for a in range(n):
            for k, pid, pflat in peers:
                pltpu.make_async_remote_copy(
                    src_ref=src_refs[a].at[pflat] if per_peer else src_refs[a], dst_ref=land_refs[a].at[me],
                    send_sem=send_sems.at[7 * a + k - 1], recv_sem=recv_sems.at[7 * a + k - 1],
                    device_id=pid, device_id_type=MESH).start()
        token[...] = jnp.zeros_like(token)

    hbm = lambda a: pltpu.with_memory_space_constraint(a, pltpu.HBM)
    res = pl.pallas_call(
        body, name=name,
        out_shape=(pltpu.SemaphoreType.DMA((7 * n,)), pltpu.SemaphoreType.DMA((7 * n,)),
                   *[pltpu.HBM(a.shape, a.dtype) for a in srcs + lands], jax.ShapeDtypeStruct((8, LANES), F32)),
        in_specs=(HBM,) * (2 * n), out_specs=(SEM, SEM) + (HBM,) * (2 * n) + (pl.BlockSpec(memory_space=pltpu.VMEM),),
        input_output_aliases={i: 2 + i for i in range(2 * n)},
        compiler_params=pltpu.CompilerParams(has_side_effects=DATAFLOW),
    )(*[hbm(a) for a in srcs + lands])
    return (res[0], res[1], list(res[2:2 + n]), list(res[2 + n:2 + 2 * n])), res[-1]


def _send_wait(started, after, per_peer, name):
    send_sems, recv_sems, srcs_thru, lands_thru = started
    n = len(srcs_thru)

    def body(*refs):
        src_refs, land_refs, send_sems, recv_sems = refs[:n], refs[n:2 * n], refs[2 * n], refs[2 * n + 1]
        me, peers = _peers()
        for a in range(n):
            for k, pid, pflat in peers:
                copy = pltpu.make_async_remote_copy(
                    src_ref=src_refs[a].at[pflat] if per_peer else src_refs[a], dst_ref=land_refs[a].at[pflat],
                    send_sem=send_sems.at[7 * a + k - 1], recv_sem=recv_sems.at[7 * a + k - 1],
                    device_id=pid, device_id_type=MESH)
                copy.wait_send()
                copy.wait_recv()

    outs = pl.pallas_call(
        body, name=name,
        out_shape=tuple(pltpu.HBM(a.shape, a.dtype) for a in srcs_thru + lands_thru),
        in_specs=(HBM,) * (2 * n) + (SEM, SEM, pl.BlockSpec(memory_space=pl.ANY)), out_specs=(HBM,) * (2 * n),
        input_output_aliases={i: i for i in range(2 * n)},
        compiler_params=pltpu.CompilerParams(has_side_effects=DATAFLOW),
    )(*srcs_thru, *lands_thru, send_sems, recv_sems, after)
    me = 4 * lax.axis_index("x") + 2 * lax.axis_index("y") + lax.axis_index("c")
    landed = []
    for src_out, land in zip(outs[:n], outs[n:]):
        own = lax.dynamic_index_in_dim(src_out, me, 0, keepdims=True) if per_peer else src_out[None]
        landed.append(lax.dynamic_update_slice(land, own, (me,) + (0,) * (land.ndim - 1)))
    return landed


def _share_rows(block, name):
    def body(src_ref, out_ref, send_sems, recv_sems, local_sem):
        me, peers = _peers()
        own = pltpu.make_async_copy(src_ref, out_ref.at[me], local_sem)
        own.start()
        copies = [pltpu.make_async_remote_copy(
            src_ref=src_ref, dst_ref=out_ref.at[me], send_sem=send_sems.at[k - 1], recv_sem=recv_sems.at[k - 1],
            device_id=pid, device_id_type=MESH) for k, pid, _ in peers]
        for cp in copies:
            cp.start()
        for cp in copies:
            cp.wait()
        own.wait()

    return pl.pallas_call(
        body, name=name, out_shape=jax.ShapeDtypeStruct((N_DEV,) + block.shape, block.dtype),
        in_specs=[HBM], out_specs=HBM,
        scratch_shapes=[pltpu.SemaphoreType.DMA((N_DEV - 1,)), pltpu.SemaphoreType.DMA((N_DEV - 1,)),
                        pltpu.SemaphoreType.DMA],
    )(block)


def _sum_slots(recv, name, tr):
    n, rows, lanes = recv.shape
    tr = _tile(rows, tr)

    def body(r_ref, o_ref):
        acc = r_ref[0].astype(F32)
        for i in range(1, n):
            acc = acc + r_ref[i].astype(F32)
        o_ref[...] = acc

    return pl.pallas_call(
        body, name=name, grid=(rows // tr,),
        in_specs=[pl.BlockSpec((n, tr, lanes), lambda i: (0, i, 0))],
        out_specs=pl.BlockSpec((tr, lanes), lambda i: (i, 0)),
        out_shape=jax.ShapeDtypeStruct((rows, lanes), F32),
        compiler_params=_cparams(("parallel",)),
    )(recv)


def _adamw_math(w, g, m, v):
    m = ADAM_B1 * m + (1.0 - ADAM_B1) * g
    v = ADAM_B2 * v + (1.0 - ADAM_B2) * (g * g)
    m_hat = m / (1.0 - ADAM_B1 ** ADAM_STEP)
    v_hat = v / (1.0 - ADAM_B2 ** ADAM_STEP)
    return -ADAM_LR * (m_hat / (jnp.sqrt(v_hat) + ADAM_EPS) + ADAM_WD * w), m, v


def _adamw(w, g, m, v, name, tr=256):
    return _rowwise(_adamw_math, name, tr, [w, g, m, v], [], [(w.shape[1], F32)] * 3)


def _adamw_small(ws, gs, ms, vs):
    n = len(ws)

    def body(*refs):
        ins, outs = refs[:4 * n], refs[4 * n:]
        for i in range(n):
            d, m, v = _adamw_math(ins[i][...], ins[n + i][...], ins[2 * n + i][...], ins[3 * n + i][...])
            outs[i][...], outs[n + i][...], outs[2 * n + i][...] = d, m, v

    vmem = pl.BlockSpec(memory_space=pltpu.VMEM)
    res = pl.pallas_call(
        body, name="adamw_small", in_specs=[vmem] * (4 * n), out_specs=[vmem] * (3 * n),
        out_shape=[jax.ShapeDtypeStruct(w.shape, F32) for w in ws] * 3,
    )(*ws, *gs, *ms, *vs)
    return res[:n], res[n:2 * n], res[2 * n:]


def _sum_adamw(recv, w, m, v, name):
    n, r, c = recv.shape
    tr = _tile(r, 256)

    def body(r_ref, w_ref, m_ref, v_ref, g_ref, d_ref, nm_ref, nv_ref):
        g = r_ref[0].astype(F32)
        for i in range(1, n):
            g = g + r_ref[i].astype(F32)
        g_ref[...] = g
        d_ref[...], nm_ref[...], nv_ref[...] = _adamw_math(w_ref[...], g, m_ref[...], v_ref[...])

    row = pl.BlockSpec((None, tr, c), lambda i: (0, i, 0))
    return pl.pallas_call(
        body, name=name, grid=(r // tr,),
        in_specs=[pl.BlockSpec((n, tr, c), lambda i: (0, i, 0)), row, row, row], out_specs=[row] * 4,
        out_shape=[jax.ShapeDtypeStruct((1, r, c), F32)] * 4, compiler_params=_cparams(("parallel",)),
    )(recv, w, m, v)


def kernel(x, mem, positions, ffn1_norm, ffn1_w_gu, ffn1_w_down, mix_norm, w_in, b_gate, sg_ln_g, sg_ln_b, sg_w, sg_b, mla_cq_norm, mla_w_uq, mla_ckv_norm, mla_w_ukv, mla_q_norm, mla_k_norm, mem_norm, mem_w_kv, mem_q_norm, mem_k_norm, w_branch_a, w_branch_b, w_branch_c, w_out, ffn2_norm, ffn2_w_gu, ffn2_w_down, loss_target, m_ffn1_norm, m_ffn1_w_gu, m_ffn1_w_down, m_mix_norm, m_w_in, m_b_gate, m_sg_ln_g, m_sg_ln_b, m_sg_w, m_sg_b, m_mla_cq_norm, m_mla_w_uq, m_mla_ckv_norm, m_mla_w_ukv, m_mla_q_norm, m_mla_k_norm, m_mem_norm, m_mem_w_kv, m_mem_q_norm, m_mem_k_norm, m_w_branch_a, m_w_branch_b, m_w_branch_c, m_w_out, m_ffn2_norm, m_ffn2_w_gu, m_ffn2_w_down, v_ffn1_norm, v_ffn1_w_gu, v_ffn1_w_down, v_mix_norm, v_w_in, v_b_gate, v_sg_ln_g, v_sg_ln_b, v_sg_w, v_sg_b, v_mla_cq_norm, v_mla_w_uq, v_mla_ckv_norm, v_mla_w_ukv, v_mla_q_norm, v_mla_k_norm, v_mem_norm, v_mem_w_kv, v_mem_q_norm, v_mem_k_norm, v_w_branch_a, v_w_branch_b, v_w_branch_c, v_w_out, v_ffn2_norm, v_ffn2_w_gu, v_ffn2_w_down):
    given = dict(locals())
    wts = {n: given[n] for n in ORDER}
    mom = {n: given["m_" + n] for n in ORDER}
    var = {n: given["v_" + n] for n in ORDER}

    def shards(group, zero):
        out = [wts[n][0].astype(BF16) for n in GROUPS[group]]
        return [out[0] + zero.astype(BF16)] + out[1:]

    def full_weights(group, slabs):
        return _compute_layout({n: _full_from_slabs(n, s) for n, s in zip(GROUPS[group], slabs)})

    def zero_of(a):
        return jnp.minimum(jnp.abs(a.reshape(-1)[0]), 0)

    gathered_ffn1, token = _all_gather([wts[n][0].astype(BF16) for n in GROUPS["ffn1"]])
    flight = {}
    flight["ffn1_down"], token = _send_start(shards("ffn1_down", token[0, 0]), False, "gather_ffn1_down_start")
    flight["mix"] = _send_start(shards("mix", token[0, 0]), False, "gather_mix_start")[0]
    recv = {}

    def weights(group, after):
        if group == "ffn1":
            return full_weights(group, gathered_ffn1)
        landed = _send_wait(flight.pop(group), after, False, f"gather_{group}_wait")
        if group == "mix":
            flight["ffn2"] = _send_start(shards("ffn2", zero_of(landed[0])), False, "gather_ffn2_start")[0]
        return full_weights(group, landed)

    small_shapes = [wts[n].shape[1:] for n in SMALL]
    early = SMALL[1:]
    assert SMALL[0] == "ffn1_norm"

    def grads_out(group, G):
        Gr = _reference_layout({n: G[n] for n in GRAD_GROUPS[group]})
        parts = [_slabs_from_full(n, Gr[n]).astype(BF16) for n in GRAD_GROUPS[group]]
        flight["g_" + group], tie = _send_start(parts, True, f"grads_{group}_start")
        if group == "mix":
            small = _pack([G[n].reshape(s) for n, s in zip(early, small_shapes[1:])])
            small = jnp.pad(small, ((0, (-small.shape[0]) % 8), (0, 0)))
            flight["small"], tie = _send_start([small + tie[0, 0]], False, "grads_small_start")
        return tie

    P = {n: wts[n] if wts[n].ndim == 2 else wts[n][0] for n in SMALL}
    loss_part, grad_x, G = _local_step(x[0], mem[0], positions[0], loss_target[0], P, weights, grads_out)

    for group, names in GRAD_GROUPS.items():
        recv.update(zip(names, _send_wait(flight.pop("g_" + group), grad_x, True, f"grads_{group}_wait")))
    early_recv, = _send_wait(flight.pop("small"), grad_x, False, "grads_small_wait")
    last = _share_rows(G["ffn1_norm"].reshape(-1, LANES), "share_ffn1_norm")
    g_small_packed = _sum_slots(jnp.concatenate([last, early_recv], axis=1), "sum_small", 2048)

    grads, delta, new_m, new_v = {}, {}, {}, {}
    for n in SHARDED:
        grads[n], delta[n], new_m[n], new_v[n] = _sum_adamw(recv[n], wts[n], mom[n], var[n], "adamw_" + n)
    grads.update(zip(SMALL, _unpack(g_small_packed, small_shapes)))

    flat2 = lambda d: [d[n].reshape(-1, d[n].shape[-1]) for n in SMALL]
    for dst, vals in zip((delta, new_m, new_v), _adamw_small(flat2(wts), flat2(grads), flat2(mom), flat2(var))):
        dst.update(zip(SMALL, vals))

    loss = lax.psum(jnp.sum(loss_part), ("x", "y", "c"))
    lead = lambda d: [d[n].reshape(wts[n].shape) for n in ORDER]
    return (loss, grad_x[None], *lead(grads), *lead(delta), *lead(new_m), *lead(new_v))
```

```python
import functools

import numpy as np
import jax
import jax.numpy as jnp
from jax import lax
from jax.experimental import pallas as pl
from jax.experimental.pallas import tpu as pltpu

F32, BF16 = jnp.float32, jnp.bfloat16

D_MODEL = 1024
SG_GROUPS, SG_GROUP_DIM, SG_WIDTH, CHUNK = 8, 64, 512, 128
MLA_HEADS, MLA_NOPE, MLA_ROPE, MLA_V, MLA_QK = 8, 64, 32, 64, 96
MLA_Q_RANK, MLA_KV_RANK = 384, 256
MEM_HEADS, MEM_HEAD_DIM, MEM_WIDTH = 4, 128, 512
D_FF = 2816
ROPE_BASE = 10000.0
EPS = 1e-6
NEG = -1e30
ADAM_LR, ADAM_B1, ADAM_B2, ADAM_EPS, ADAM_WD, ADAM_STEP = 0.001, 0.9, 0.999, 1e-08, 0.01, 10

N_DEV = 8
LANES = 128
V7X_VMEM_LIMIT = 56 * 1024 * 1024
HP = MLA_HEADS * LANES

Z_G, Z_U, Z_V, Z_QM, Z_CKV, Z_KR, Z_CQ = 0, 3072, 3584, 4096, 4608, 4864, 4992
Z_COLS = 5376
KR_LANE = 64


def _tile(dim, pref):
    if dim <= pref:
        return dim
    for t in range(pref - pref % LANES, LANES - 1, -LANES):
        if dim % t == 0:
            return t
    for t in range(pref - pref % 8, 7, -8):
        if dim % t == 0:
            return t
    return dim


def _cparams(sem):
    return pltpu.CompilerParams(dimension_semantics=sem, vmem_limit_bytes=V7X_VMEM_LIMIT)


_DN = {"nn": ((1,), (0,)), "nt": ((1,), (1,)), "tn": ((0,), (0,))}


def _dot(a, b, mode="nn"):
    return lax.dot_general(a.astype(BF16), b.astype(BF16), (_DN[mode], ((), ())),
                           preferred_element_type=F32)


def _mm(a, b, mode, out_dtype, name, tm=512, tn=512, tk=2048, tie=None):
    if mode == "tn":
        K, M = a.shape
    else:
        M, K = a.shape
    N = b.shape[0] if mode == "nt" else b.shape[1]
    tm, tn, tk = _tile(M, tm), _tile(N, tn), _tile(K, tk)
    nk = K // tk
    if mode == "tn":
        a_spec = pl.BlockSpec((tk, tm), lambda i, j, k: (k, i))
    else:
        a_spec = pl.BlockSpec((tm, tk), lambda i, j, k: (i, k))
    if mode == "nt":
        b_spec = pl.BlockSpec((tn, tk), lambda i, j, k: (j, k))
    else:
        b_spec = pl.BlockSpec((tk, tn), lambda i, j, k: (k, j))

    ties = [] if tie is None else [tie]

    def body(a_ref, b_ref, *rest):
        o_ref, *scratch = rest[len(ties):]
        p = _dot(a_ref[...], b_ref[...], mode)
        if nk == 1:
            o_ref[...] = p.astype(o_ref.dtype)
        else:
            acc_ref, = scratch
            k = pl.program_id(2)

            @pl.when(k == 0)
            def _():
                acc_ref[...] = p

            @pl.when(k > 0)
            def _():
                acc_ref[...] += p

            @pl.when(k == nk - 1)
            def _():
                o_ref[...] = acc_ref[...].astype(o_ref.dtype)

    return pl.pallas_call(
        body, name=name, grid=(M // tm, N // tn, nk),
        in_specs=[a_spec, b_spec] + [pl.BlockSpec(t.shape, lambda i, j, k: (0, 0)) for t in ties],
        out_specs=pl.BlockSpec((tm, tn), lambda i, j, k: (i, j)),
        out_shape=jax.ShapeDtypeStruct((M, N), out_dtype),
        scratch_shapes=[] if nk == 1 else [pltpu.VMEM((tm, tn), F32)],
        compiler_params=_cparams(("parallel", "parallel", "arbitrary")),
    )(a, b, *ties)


def _mm_t(at, b, name, tm, tn, tk=1024, tie=None):
    return _mm(at, b, "nn", BF16, name, tm=tm, tn=tn, tk=tk, tie=tie)


def _rowwise(fn, name, tr, row_ins, bc_ins, row_outs, acc_outs=()):
    norm = [it if isinstance(it, tuple) else (it, it.shape[1], 0) for it in row_ins]
    rows = norm[0][0].shape[0]
    tr = _tile(rows, tr)
    arrays, in_specs = [], []
    for arr, w, cb in norm:
        arrays.append(arr)
        in_specs.append(pl.BlockSpec((tr, w), lambda i, cb=cb: (i, cb)))
    for arr in bc_ins:
        arrays.append(arr)
        in_specs.append(pl.BlockSpec(arr.shape, lambda i, nd=arr.ndim: (0,) * nd))
    n_in, n_row = len(arrays), len(row_outs)
    out_shape, out_specs, aliases = [], [], {}
    transposed = [len(o) == 3 for o in row_outs]
    for k, o in enumerate(row_outs):
        if o[0] == "into":
            _, target, w, cb = o
            aliases[len(arrays)] = k
            arrays.append(target)
            in_specs.append(pl.BlockSpec(memory_space=pl.ANY))
            out_shape.append(jax.ShapeDtypeStruct(target.shape, target.dtype))
            out_specs.append(pl.BlockSpec((tr, w), lambda i, cb=cb: (i, cb)))
        elif transposed[k]:
            out_shape.append(jax.ShapeDtypeStruct((o[0], rows), o[1]))
            out_specs.append(pl.BlockSpec((o[0], tr), lambda i: (0, i)))
        else:
            out_shape.append(jax.ShapeDtypeStruct((rows, o[0]), o[1]))
            out_specs.append(pl.BlockSpec((tr, o[0]), lambda i: (i, 0)))
    for shp, dt in acc_outs:
        out_shape.append(jax.ShapeDtypeStruct(shp, dt))
        out_specs.append(pl.BlockSpec(shp, lambda i, nd=len(shp): (0,) * nd))

    def body(*refs):
        vals = fn(*[r[...].astype(F32) for r in refs[:n_in]])
        if not isinstance(vals, (tuple, list)):
            vals = (vals,)
        outs = refs[len(arrays):]
        for r, v, t in zip(outs[:n_row], vals[:n_row], transposed):
            r[...] = v.astype(F32).T.astype(r.dtype) if t else v.astype(r.dtype)
        if acc_outs:
            accs = list(zip(outs[n_row:], vals[n_row:]))
            i = pl.program_id(0)

            @pl.when(i == 0)
            def _():
                for r, v in accs:
                    r[...] = v.astype(r.dtype)

            @pl.when(i > 0)
            def _():
                for r, v in accs:
                    r[...] += v.astype(r.dtype)

    res = pl.pallas_call(
        body, name=name, grid=(rows // tr,), in_specs=in_specs, out_specs=out_specs,
        out_shape=out_shape, input_output_aliases=aliases, compiler_params=_cparams(("arbitrary",)),
    )(*arrays)
    return res


def _rsum(x):
    return jnp.sum(x, axis=0, keepdims=True)


def _rms(x, g, n=None):
    n = x.shape[-1] if n is None else n
    r = lax.rsqrt(jnp.sum(x * x, axis=-1, keepdims=True) * (1.0 / n) + EPS)
    return x * r * g


def _rms_bwd(x, g, dy, n=None):
    n = x.shape[-1] if n is None else n
    r = lax.rsqrt(jnp.sum(x * x, axis=-1, keepdims=True) * (1.0 / n) + EPS)
    xh = x * r
    dxh = dy * g
    dx = r * (dxh - xh * (jnp.sum(dxh * xh, axis=-1, keepdims=True) * (1.0 / n)))
    return dx, _rsum(dy * xh)


def _gelu(x):
    return 0.5 * x * (1.0 + lax.erf(x * 0.7071067811865476))


def _gelu_grad(x):
    return 0.5 * (1.0 + lax.erf(x * 0.7071067811865476)) + x * jnp.exp(-0.5 * x * x) * 0.3989422804014327


def _sigmoid(x):
    return 0.5 * jnp.tanh(0.5 * x) + 0.5


FFN_TM, FFN_TN = 1024, 1408
MXU_WIDTH = 256


def _col_chunks(n):
    return [(c, min(c + MXU_WIDTH, n)) for c in range(0, n, MXU_WIDTH)]


def _ffn_gu_act(h, w_gu, tag):
    T = h.shape[0]
    tm, tn = _tile(T, FFN_TM), FFN_TN
    nj = D_FF // tn

    def body(h_ref, wg_ref, wu_ref, gu_ref, a_ref, at_ref):
        h = h_ref[...]
        for c0, c1 in _col_chunks(tn):
            g = _dot(h, wg_ref[:, c0:c1])
            u = _dot(h, wu_ref[:, c0:c1])
            gu_ref[0, :, c0:c1] = g.astype(BF16)
            gu_ref[1, :, c0:c1] = u.astype(BF16)
            a = g * _sigmoid(g) * u
            a_ref[:, c0:c1] = a.astype(BF16)
            at_ref[c0:c1, :] = a.T.astype(BF16)

    return pl.pallas_call(
        body, name=f"{tag}_gu_act", grid=(T // tm, nj),
        in_specs=[pl.BlockSpec((tm, D_MODEL), lambda i, j: (i, 0)),
                  pl.BlockSpec((D_MODEL, tn), lambda i, j: (0, j)),
                  pl.BlockSpec((D_MODEL, tn), lambda i, j: (0, j + nj))],
        out_specs=[pl.BlockSpec((2, tm, tn), lambda i, j: (0, i, j)),
                   pl.BlockSpec((tm, tn), lambda i, j: (i, j)),
                   pl.BlockSpec((tn, tm), lambda i, j: (j, i))],
        out_shape=[jax.ShapeDtypeStruct((2, T, D_FF), BF16), jax.ShapeDtypeStruct((T, D_FF), BF16),
                   jax.ShapeDtypeStruct((D_FF, T), BF16)],
        compiler_params=_cparams(("parallel", "parallel")),
    )(h, w_gu, w_gu)


def _ffn_da_actbwd(do, w_down, gu, tag, tie=None):
    T = do.shape[0]
    tm, tn = _tile(T, FFN_TM), FFN_TN
    ties = [] if tie is None else [tie]

    def body(do_ref, wd_ref, gu_ref, *rest):
        dgu_ref = rest[-1]
        do = do_ref[...]
        for c0, c1 in _col_chunks(tn):
            da = _dot(do, wd_ref[c0:c1, :], "nt")
            g = gu_ref[0, :, c0:c1].astype(F32)
            u = gu_ref[1, :, c0:c1].astype(F32)
            s = _sigmoid(g)
            dgu_ref[0, :, c0:c1] = (da * u * s * (1.0 + g * (1.0 - s))).astype(BF16)
            dgu_ref[1, :, c0:c1] = (da * g * s).astype(BF16)

    return pl.pallas_call(
        body, name=f"{tag}_da_actbwd", grid=(T // tm, D_FF // tn),
        in_specs=[pl.BlockSpec((tm, D_MODEL), lambda i, j: (i, 0)),
                  pl.BlockSpec((tn, D_MODEL), lambda i, j: (j, 0)),
                  pl.BlockSpec((2, tm, tn), lambda i, j: (0, i, j))]
        + [pl.BlockSpec(t.shape, lambda i, j: (0, 0)) for t in ties],
        out_specs=pl.BlockSpec((2, tm, tn), lambda i, j: (0, i, j)),
        out_shape=jax.ShapeDtypeStruct((2, T, D_FF), BF16),
        compiler_params=_cparams(("parallel", "parallel")),
    )(do, w_down, gu, *ties)


def _ffn_dwgu(ht, dgu, tag, tk=2048):
    T = ht.shape[1]
    tn, tk = FFN_TN, _tile(T, tk)
    nj, nk = D_FF // tn, T // tk

    def body(a_ref, b_ref, o_ref, acc_ref):
        k = pl.program_id(1)
        p = _dot(a_ref[...], b_ref[...])

        @pl.when(k == 0)
        def _():
            acc_ref[...] = p

        @pl.when(k > 0)
        def _():
            acc_ref[...] += p

        @pl.when(k == nk - 1)
        def _():
            o_ref[...] = acc_ref[...].astype(o_ref.dtype)

    return pl.pallas_call(
        body, name=f"{tag}_dwgu", grid=(2 * nj, nk),
        in_specs=[pl.BlockSpec((D_MODEL, tk), lambda n, k: (0, k)),
                  pl.BlockSpec((None, tk, tn), lambda n, k: (n // nj, k, n % nj))],
        out_specs=pl.BlockSpec((D_MODEL, tn), lambda n, k: (0, n)),
        out_shape=jax.ShapeDtypeStruct((D_MODEL, 2 * D_FF), BF16),
        scratch_shapes=[pltpu.VMEM((D_MODEL, tn), F32)],
        compiler_params=_cparams(("parallel", "arbitrary")),
    )(ht, dgu)


def _ffn_dh(dgu, w_gu, tag, tm=2048, tie=None):
    T = dgu.shape[1]
    tm, tk = _tile(T, tm), FFN_TN
    nk = D_FF // tk
    ties = [] if tie is None else [tie]

    def body(a_ref, b_ref, *rest):
        o_ref, acc_ref = rest[len(ties):]
        k = pl.program_id(1)
        p = _dot(a_ref[...], b_ref[...], "nt")

        @pl.when(k == 0)
        def _():
            acc_ref[...] = p

        @pl.when(k > 0)
        def _():
            acc_ref[...] += p

        @pl.when(k == 2 * nk - 1)
        def _():
            o_ref[...] = acc_ref[...].astype(o_ref.dtype)

    return pl.pallas_call(
        body, name=f"{tag}_dh", grid=(T // tm, 2 * nk),
        in_specs=[pl.BlockSpec((None, tm, tk), lambda i, k: (k // nk, i, k % nk)),
                  pl.BlockSpec((D_MODEL, tk), lambda i, k: (0, k))]
        + [pl.BlockSpec(t.shape, lambda i, k: (0, 0)) for t in ties],
        out_specs=pl.BlockSpec((tm, D_MODEL), lambda i, k: (i, 0)),
        out_shape=jax.ShapeDtypeStruct((T, D_MODEL), BF16),
        scratch_shapes=[pltpu.VMEM((tm, D_MODEL), F32)],
        compiler_params=_cparams(("parallel", "arbitrary")),
    )(dgu, w_gu, *ties)


def _ffn_fwd(h, w_gu, w_down, tag):
    gu, a, at = _ffn_gu_act(h, w_gu, tag)
    if callable(w_down):
        w_down = w_down(at)
    o = _mm(a, w_down, "nn", BF16, f"{tag}_down", tm=1024, tn=1024, tk=2816)
    return gu, at, o


def _ffn_bwd(do, ht, gu, at, w_gu, w_down, tag, tie=None, on_dw=None):
    on_dw = on_dw or (lambda which, dw: None)
    dw_down = _mm_t(at, do, f"{tag}_dwdown", tm=1408, tn=1024, tk=2048, tie=tie)
    dgu = _ffn_da_actbwd(do, w_down, gu, tag, tie=on_dw("down", dw_down))
    dw_gu = _ffn_dwgu(ht, dgu, tag)
    dh = _ffn_dh(dgu, w_gu, tag, tie=on_dw("gu", dw_gu))
    return dh, dw_gu, dw_down


def _sg_common(u_pre, v_pre, ln_g, ln_b):
    u = _gelu(u_pre)
    v = _gelu(v_pre)
    mu = jnp.mean(v, axis=-1, keepdims=True)
    vc = v - mu
    rstd = lax.rsqrt(jnp.mean(vc * vc, axis=-1, keepdims=True) + EPS)
    vhat = vc * rstd
    vl = vhat * ln_g + ln_b
    return u, vhat, rstd, vl


def _sg_masked_pairs(w):
    t = lax.broadcasted_iota(jnp.int32, (CHUNK, CHUNK), 0)
    s = lax.broadcasted_iota(jnp.int32, (CHUNK, CHUNK), 1)
    causal = s <= t
    wm = [jnp.where(causal, w[g], 0.0).astype(BF16) for g in range(SG_GROUPS)]
    return [jnp.concatenate([wm[2 * j], wm[2 * j + 1]], axis=0) for j in range(SG_GROUPS // 2)], causal


def _sg_mix(vl, pairs, bias):
    tr = vl.shape[0]
    low = lax.broadcasted_iota(jnp.int32, (CHUNK, LANES), 1) < SG_GROUP_DIM
    vb = vl.astype(BF16)
    rows = []
    for c in range(tr // CHUNK):
        slabs = []
        for j in range(SG_GROUPS // 2):
            slab = vb[c * CHUNK:(c + 1) * CHUNK, j * LANES:(j + 1) * LANES]
            m = _dot(pairs[j], slab)
            slabs.append(jnp.where(low, m[:CHUNK], m[CHUNK:]))
        rows.append(jnp.concatenate(slabs, axis=1) + bias)
    return jnp.concatenate(rows, axis=0)


def _sg_fwd(z, ln_g, ln_b, sg_w, bias_full):
    def fn(u_pre, v_pre, ln_g, ln_b, w, bias):
        u, _, _, vl = _sg_common(u_pre, v_pre, ln_g, ln_b)
        pairs, _ = _sg_masked_pairs(w)
        y = u * _sg_mix(vl, pairs, bias)
        return y, y

    return _rowwise(fn, "sg_fwd", 512, [(z, SG_WIDTH, Z_U // SG_WIDTH), (z, SG_WIDTH, Z_V // SG_WIDTH)],
                    [ln_g, ln_b, sg_w, bias_full], [(SG_WIDTH, BF16), (SG_WIDTH, BF16, "T")])


def _sg_bwd(z, dy, ln_g, ln_b, sg_w, bias_full, group_ind, dz):
    def fn(u_pre, v_pre, dy, ln_g, ln_b, w, bias, ind):
        dy = dy.astype(F32)
        u, vhat, rstd, vl = _sg_common(u_pre, v_pre, ln_g, ln_b)
        pairs, causal = _sg_masked_pairs(w)
        mixed = _sg_mix(vl, pairs, bias)
        du_pre = dy * mixed * _gelu_grad(u_pre)
        dmix = dy * u
        tr = dy.shape[0]
        low = lax.broadcasted_iota(jnp.int32, (CHUNK, LANES), 1) < SG_GROUP_DIM
        vb = vl.astype(BF16)
        dw = [jnp.zeros((CHUNK, CHUNK), F32) for _ in range(SG_GROUPS)]
        dbias = jnp.zeros((CHUNK, SG_WIDTH), F32)
        dvl_rows = []
        for c in range(tr // CHUNK):
            dm_c = dmix[c * CHUNK:(c + 1) * CHUNK]
            dbias = dbias + dm_c
            slabs = []
            for j in range(SG_GROUPS // 2):
                slab = vb[c * CHUNK:(c + 1) * CHUNK, j * LANES:(j + 1) * LANES]
                dm = dm_c[:, j * LANES:(j + 1) * LANES]
                d0 = jnp.where(low, dm, 0.0).astype(BF16)
                d1 = jnp.where(low, 0.0, dm).astype(BF16)
                dw[2 * j] = dw[2 * j] + _dot(d0, slab, "nt")
                dw[2 * j + 1] = dw[2 * j + 1] + _dot(d1, slab, "nt")
                slabs.append(_dot(pairs[j], jnp.concatenate([d0, d1], axis=0), "tn"))
            dvl_rows.append(jnp.concatenate(slabs, axis=1))
        dvl = jnp.concatenate(dvl_rows, axis=0)
        dln_g = _rsum(dvl * vhat)
        dln_b = _rsum(dvl)
        dvh = dvl * ln_g
        dv = rstd * (dvh - jnp.mean(dvh, axis=-1, keepdims=True)
                     - vhat * jnp.mean(dvh * vhat, axis=-1, keepdims=True))
        dv_pre = dv * _gelu_grad(v_pre)
        dw = jnp.stack([jnp.where(causal, d, 0.0) for d in dw], axis=0)
        dbias_t = lax.dot_general(dbias, ind, (((1,), (0,)), ((), ())), precision=lax.Precision.HIGHEST,
                                  preferred_element_type=F32)
        return jnp.concatenate([du_pre, dv_pre], axis=1), dw, dbias_t, dln_g, dln_b

    return _rowwise(fn, "sg_bwd", 512,
                    [(z, SG_WIDTH, Z_U // SG_WIDTH), (z, SG_WIDTH, Z_V // SG_WIDTH), dy],
                    [ln_g, ln_b, sg_w, bias_full, group_ind],
                    [("into", dz, 2 * SG_WIDTH, Z_U // (2 * SG_WIDTH))],
                    [((SG_GROUPS, CHUNK, CHUNK), F32), ((CHUNK, SG_GROUPS), F32), ((1, SG_WIDTH), F32), ((1, SG_WIDTH), F32)])


MLA_POST_ROWS = 1024


def _rope(x, c, s1, s2):
    return x * c + pltpu.roll(x, LANES - MLA_ROPE // 2, 1) * s1 + pltpu.roll(x, MLA_ROPE // 2, 1) * s2


def _rope_t(d, c, s1, s2):
    return d * c + pltpu.roll(d * s1, MLA_ROPE // 2, 1) + pltpu.roll(d * s2, LANES - MLA_ROPE // 2, 1)


def _mla_post(q_pre, kv_pre, z, tabs, gq, gk):
    scale = MLA_QK ** -0.5 * LOG2E
    T = q_pre.shape[0]
    tr = _tile(T, MLA_POST_ROWS)

    def body(q_ref, k_ref, v_ref, kr_ref, c_ref, s1_ref, s2_ref, gq_ref, gk_ref, qo_ref, ko_ref, vo_ref):
        kr = kr_ref[...].astype(F32)
        c, s1, s2, gq, gk = c_ref[...], s1_ref[...], s2_ref[...], gq_ref[...], gk_ref[...]
        ones_lane = lax.broadcasted_iota(jnp.int32, (tr, LANES), 1) == ONES_LANE
        for h in range(MLA_HEADS):
            sl = slice(h * LANES, (h + 1) * LANES)
            qo_ref[:, sl] = (_rope(_rms(q_ref[:, sl].astype(F32), gq, MLA_QK), c, s1, s2) * scale).astype(BF16)
            ko_ref[:, sl] = _rope(_rms(k_ref[:, sl].astype(F32) + kr, gk, MLA_QK), c, s1, s2).astype(BF16)
            vo_ref[:, sl] = jnp.where(ones_lane, 1.0, v_ref[:, sl].astype(F32)).astype(BF16)

    wide = lambda cb: pl.BlockSpec((tr, HP), lambda i, cb=cb: (i, cb))
    lanes = lambda cb: pl.BlockSpec((tr, LANES), lambda i, cb=cb: (i, cb))
    gain = pl.BlockSpec((1, LANES), lambda i: (0, 0))
    return pl.pallas_call(
        body, name="mla_post", grid=(T // tr,),
        in_specs=[wide(0), wide(0), wide(1), lanes(Z_KR // LANES), lanes(0), lanes(0), lanes(0), gain, gain],
        out_specs=[wide(0)] * 3, out_shape=[jax.ShapeDtypeStruct((T, HP), BF16)] * 3,
        compiler_params=_cparams(("parallel",)),
    )(q_pre, kv_pre, kv_pre, z, *tabs, gq, gk)


def _mla_post_bwd(q_pre, kv_pre, z, tabs, gq, gk, dq, dk, dv):
    scale = MLA_QK ** -0.5
    T = q_pre.shape[0]
    tr = _tile(T, MLA_POST_ROWS)

    def body(q_ref, k_ref, kr_ref, c_ref, s1_ref, s2_ref, dq_ref, dk_ref, dv_ref, gq_ref, gk_ref,
             dqo_ref, dkvo_ref, dkro_ref, dgq_ref, dgk_ref):
        kr = kr_ref[...].astype(F32)
        c, s1, s2, gq, gk = c_ref[...], s1_ref[...], s2_ref[...], gq_ref[...], gk_ref[...]
        lane = lax.broadcasted_iota(jnp.int32, (1, LANES), 1)
        kr_mask = (lane >= KR_LANE) & (lane < KR_LANE + MLA_ROPE)
        dgq = jnp.zeros((1, LANES), F32)
        dgk = jnp.zeros((1, LANES), F32)
        dkr = jnp.zeros((tr, LANES), F32)
        for h in range(MLA_HEADS):
            sl = slice(h * LANES, (h + 1) * LANES)
            dqn = _rope_t(dq_ref[:, sl].astype(F32), c, s1, s2) * scale
            dx, dg = _rms_bwd(q_ref[:, sl].astype(F32), gq, dqn, MLA_QK)
            dqo_ref[:, sl] = dx.astype(BF16)
            dgq = dgq + dg
            dkn = _rope_t(dk_ref[:, sl].astype(F32), c, s1, s2)
            dx, dg = _rms_bwd(k_ref[:, sl].astype(F32) + kr, gk, dkn, MLA_QK)
            dkvo_ref[:, sl] = dx.astype(BF16)
            dkvo_ref[:, HP + h * LANES:HP + (h + 1) * LANES] = dv_ref[:, sl]
            dgk = dgk + dg
            dkr = dkr + dx
        dkro_ref[...] = jnp.where(kr_mask, dkr, 0.0).astype(BF16)
        i = pl.program_id(0)

        @pl.when(i == 0)
        def _():
            dgq_ref[...] = dgq
            dgk_ref[...] = dgk

        @pl.when(i > 0)
        def _():
            dgq_ref[...] += dgq
            dgk_ref[...] += dgk

    wide = lambda cb: pl.BlockSpec((tr, HP), lambda i, cb=cb: (i, cb))
    lanes = lambda cb: pl.BlockSpec((tr, LANES), lambda i, cb=cb: (i, cb))
    gain = pl.BlockSpec((1, LANES), lambda i: (0, 0))
    return pl.pallas_call(
        body, name="mla_post_bwd", grid=(T // tr,),
        in_specs=[wide(0), wide(0), lanes(Z_KR // LANES), lanes(0), lanes(0), lanes(0), wide(0), wide(0), wide(0),
                  gain, gain],
        out_specs=[wide(0), pl.BlockSpec((tr, 2 * HP), lambda i: (i, 0)), lanes(0), gain, gain],
        out_shape=[jax.ShapeDtypeStruct((T, HP), BF16), jax.ShapeDtypeStruct((T, 2 * HP), BF16),
                   jax.ShapeDtypeStruct((T, LANES), BF16), jax.ShapeDtypeStruct((1, LANES), F32),
                   jax.ShapeDtypeStruct((1, LANES), F32)],
        compiler_params=_cparams(("arbitrary",)),
    )(q_pre, kv_pre, z, *tabs, dq, dk, dv, gq, gk)


def _pairs(n, lower):
    a, b = [], []
    for o in range(n):
        inner = range(o + 1) if lower else range(o, n)
        for t in inner:
            a.append(o)
            b.append(t)
    return jnp.asarray(np.array(a, np.int32)), jnp.asarray(np.array(b, np.int32))


FLASH_TILE, FLASH_SUB_ROWS = 2048, 512
LOG2E, LN2 = 1.4426950408889634, 0.6931471805599453
ONES_LANE = MLA_V


def _flash_tiles(T):
    tq = _tile(T, FLASH_TILE)
    return tq, _tile(tq, FLASH_SUB_ROWS)


def _col_span(t, sr, rb, diag, key_major):
    if not diag:
        return 0, t
    return (rb * sr, t) if key_major else (0, (rb + 1) * sr)


def _span_iota(sr, rb, c0, c1):
    r = lax.broadcasted_iota(jnp.int32, (sr, c1 - c0), 0) + rb * sr
    c = lax.broadcasted_iota(jnp.int32, (sr, c1 - c0), 1) + c0
    return r, c


def _lanes(x, width):
    return jnp.concatenate([x] * (width // LANES), axis=1)


def _flash_fwd(q, k, v):
    T = q.shape[0]
    tq, sr = _flash_tiles(T)
    n = T // tq
    ii, jj = _pairs(n, True)

    def body(ii_ref, jj_ref, q_ref, k_ref, v_ref, o_ref, ot_ref, lse_ref, lset_ref, m_sc, acc_sc):
        p_ = pl.program_id(1)
        i, j = ii_ref[p_], jj_ref[p_]

        @pl.when(j == 0)
        def _():
            m_sc[...] = jnp.full(m_sc.shape, NEG, F32)
            acc_sc[...] = jnp.zeros(acc_sc.shape, F32)

        def tile(diag):
            nrb = tq // sr

            def scores(rb):
                c0, c1 = _col_span(tq, sr, rb, diag, False)
                return _dot(q_ref[rb * sr:(rb + 1) * sr, :], k_ref[c0:c1, :], "nt")

            s_next = scores(0)
            for rb in range(nrb):
                rows = slice(rb * sr, (rb + 1) * sr)
                c0, c1 = _col_span(tq, sr, rb, diag, False)
                s, s_next = s_next, (scores(rb + 1) if rb + 1 < nrb else None)
                if diag:
                    r, c = _span_iota(sr, rb, c0, c1)
                    s = jnp.where(c <= r, s, NEG)
                m = m_sc[rows, :]
                m_new = jnp.maximum(m, jnp.max(s, axis=1, keepdims=True))
                p = jnp.exp2(s - _lanes(m_new, c1 - c0))
                acc_sc[rows, :] = jnp.exp2(m - m_new) * acc_sc[rows, :] + _dot(p, v_ref[c0:c1, :])
                m_sc[rows, :] = m_new

        @pl.when(j < i)
        def _():
            tile(False)

        @pl.when(j == i)
        def _():
            tile(True)
            acc = acc_sc[...]
            lane = lax.broadcasted_iota(jnp.int32, acc.shape, 1)
            l = jnp.sum(jnp.where(lane == ONES_LANE, acc, 0.0), axis=1, keepdims=True)
            o = jnp.where(lane < MLA_V, acc / l, 0.0)
            o_ref[...] = o.astype(o_ref.dtype)
            ot_ref[...] = o.T.astype(ot_ref.dtype)
            lse = m_sc[...] + jnp.log2(l)
            lse_ref[...] = lse
            lset_ref[...] = lse.T[:8]

    blk = lambda which: pl.BlockSpec((tq, LANES), which)
    qmap = lambda h, p, ii, jj: (ii[p], h)
    kmap = lambda h, p, ii, jj: (jj[p], h)
    tmap = lambda h, p, ii, jj: (h, ii[p])
    return pl.pallas_call(
        body, name="mla_flash_fwd",
        grid_spec=pltpu.PrefetchScalarGridSpec(
            num_scalar_prefetch=2, grid=(MLA_HEADS, int(ii.shape[0])),
            in_specs=[blk(qmap), blk(kmap), blk(kmap)],
            out_specs=[blk(qmap), pl.BlockSpec((LANES, tq), tmap), blk(qmap), pl.BlockSpec((8, tq), tmap)],
            scratch_shapes=[pltpu.VMEM((tq, LANES), F32)] * 2),
        out_shape=[jax.ShapeDtypeStruct((T, HP), BF16), jax.ShapeDtypeStruct((HP, T), BF16),
                   jax.ShapeDtypeStruct((T, HP), F32), jax.ShapeDtypeStruct((8 * MLA_HEADS, T), F32)],
        compiler_params=_cparams(("parallel", "arbitrary")),
    )(ii, jj, q, k, v)


def _flash_dq(q, k, v, do, lse, delta):
    T = q.shape[0]
    tq, sr = _flash_tiles(T)
    n = T // tq
    ii, jj = _pairs(n, True)

    def body(ii_ref, jj_ref, q_ref, k_ref, v_ref, do_ref, lse_ref, dl_ref, dq_ref, acc_sc):
        p_ = pl.program_id(1)
        i, j = ii_ref[p_], jj_ref[p_]

        @pl.when(j == 0)
        def _():
            acc_sc[...] = jnp.zeros(acc_sc.shape, F32)

        def tile(diag):
            nrb = tq // sr

            def products(rb):
                rows = slice(rb * sr, (rb + 1) * sr)
                c0, c1 = _col_span(tq, sr, rb, diag, False)
                return _dot(q_ref[rows, :], k_ref[c0:c1, :], "nt"), _dot(do_ref[rows, :], v_ref[c0:c1, :], "nt")

            nxt = products(0)
            for rb in range(nrb):
                rows = slice(rb * sr, (rb + 1) * sr)
                c0, c1 = _col_span(tq, sr, rb, diag, False)
                (s, dp), nxt = nxt, (products(rb + 1) if rb + 1 < nrb else None)
                p = jnp.exp2(s - _lanes(lse_ref[rows, :], c1 - c0))
                if diag:
                    r, c = _span_iota(sr, rb, c0, c1)
                    p = jnp.where(c <= r, p, 0.0)
                acc_sc[rows, :] += _dot(p * (dp - _lanes(dl_ref[rows, :], c1 - c0)), k_ref[c0:c1, :])

        @pl.when(j < i)
        def _():
            tile(False)

        @pl.when(j == i)
        def _():
            tile(True)
            dq_ref[...] = acc_sc[...].astype(dq_ref.dtype)

    blk = lambda which: pl.BlockSpec((tq, LANES), which)
    qmap = lambda h, p, ii, jj: (ii[p], h)
    kmap = lambda h, p, ii, jj: (jj[p], h)
    return pl.pallas_call(
        body, name="mla_flash_dq",
        grid_spec=pltpu.PrefetchScalarGridSpec(
            num_scalar_prefetch=2, grid=(MLA_HEADS, int(ii.shape[0])),
            in_specs=[blk(qmap), blk(kmap), blk(kmap), blk(qmap), blk(qmap), blk(qmap)],
            out_specs=blk(qmap),
            scratch_shapes=[pltpu.VMEM((tq, LANES), F32)]),
        out_shape=jax.ShapeDtypeStruct((T, HP), BF16),
        compiler_params=_cparams(("parallel", "arbitrary")),
    )(ii, jj, q, k, v, do, lse, delta)


def _flash_dkv(q, k, v, do, lse_t, delta_t):
    T = q.shape[0]
    tq, sr = _flash_tiles(T)
    n = T // tq
    jj, ii = _pairs(n, False)

    def body(jj_ref, ii_ref, q_ref, k_ref, v_ref, do_ref, lse_ref, dl_ref, dk_ref, dv_ref, dk_sc, dv_sc):
        p_ = pl.program_id(1)
        j, i = jj_ref[p_], ii_ref[p_]

        @pl.when(i == j)
        def _():
            dk_sc[...] = jnp.zeros(dk_sc.shape, F32)
            dv_sc[...] = jnp.zeros(dv_sc.shape, F32)

        def tile(diag):
            nrb = tq // sr

            def products(rb):
                rows = slice(rb * sr, (rb + 1) * sr)
                c0, c1 = _col_span(tq, sr, rb, diag, True)
                return _dot(k_ref[rows, :], q_ref[c0:c1, :], "nt"), _dot(v_ref[rows, :], do_ref[c0:c1, :], "nt")

            nxt = products(0)
            for rb in range(nrb):
                rows = slice(rb * sr, (rb + 1) * sr)
                c0, c1 = _col_span(tq, sr, rb, diag, True)
                (st, dpt), nxt = nxt, (products(rb + 1) if rb + 1 < nrb else None)
                pt = jnp.exp2(st - lse_ref[:1, c0:c1])
                if diag:
                    r, c = _span_iota(sr, rb, c0, c1)
                    pt = jnp.where(r <= c, pt, 0.0)
                dv_sc[rows, :] += _dot(pt, do_ref[c0:c1, :])
                dk_sc[rows, :] += _dot(pt * (dpt - dl_ref[:1, c0:c1]), q_ref[c0:c1, :])

        @pl.when(i == j)
        def _():
            tile(True)

        @pl.when(i > j)
        def _():
            tile(False)

        @pl.when(i == n - 1)
        def _():
            dk_ref[...] = (dk_sc[...] * LN2).astype(dk_ref.dtype)
            dv_ref[...] = dv_sc[...].astype(dv_ref.dtype)

    blk = lambda which: pl.BlockSpec((tq, LANES), which)
    qmap = lambda h, p, jj, ii: (ii[p], h)
    kmap = lambda h, p, jj, ii: (jj[p], h)
    lse_rows = pl.BlockSpec((8, tq), lambda h, p, jj, ii: (h, ii[p]))
    delta_rows = pl.BlockSpec((8, tq), lambda h, p, jj, ii: (h * (LANES // 8), ii[p]))
    return pl.pallas_call(
        body, name="mla_flash_dkv",
        grid_spec=pltpu.PrefetchScalarGridSpec(
            num_scalar_prefetch=2, grid=(MLA_HEADS, int(ii.shape[0])),
            in_specs=[blk(qmap), blk(kmap), blk(kmap), blk(qmap), lse_rows, delta_rows],
            out_specs=[blk(kmap), blk(kmap)],
            scratch_shapes=[pltpu.VMEM((tq, LANES), F32)] * 2),
        out_shape=[jax.ShapeDtypeStruct((T, HP), BF16)] * 2,
        compiler_params=_cparams(("parallel", "arbitrary")),
    )(jj, ii, q, k, v, do, lse_t, delta_t)


def _mem_fwd(z, km, vm, gq):
    scale = MEM_HEAD_DIM ** -0.5

    def fn(qm, km, vm, gq):
        ys = []
        for h in range(MEM_HEADS):
            sl = slice(h * LANES, (h + 1) * LANES)
            q = _rms(qm[:, sl], gq) * scale
            s = _dot(q, km[:, sl], "nt")
            p = jnp.exp(s - jnp.max(s, axis=1, keepdims=True))
            p = p / jnp.sum(p, axis=1, keepdims=True)
            ys.append(_dot(p, vm[:, sl]))
        y = jnp.concatenate(ys, axis=1)
        return y, y

    return _rowwise(fn, "mem_fwd", 512, [(z, MEM_WIDTH, Z_QM // MEM_WIDTH)], [km, vm, gq],
                    [(MEM_WIDTH, BF16), (MEM_WIDTH, BF16, "T")])


def _mem_bwd(z, dy, km, vm, gq, dz):
    scale = MEM_HEAD_DIM ** -0.5

    def fn(qm, dy, km, vm, gq):
        dqs, dks, dvs = [], [], []
        dgq = jnp.zeros((1, LANES), F32)
        for h in range(MEM_HEADS):
            sl = slice(h * LANES, (h + 1) * LANES)
            q = (_rms(qm[:, sl], gq) * scale).astype(BF16)
            dyh = dy[:, sl]
            kh, vh = km[:, sl], vm[:, sl]
            s = _dot(q, kh, "nt")
            p = jnp.exp(s - jnp.max(s, axis=1, keepdims=True))
            p = p / jnp.sum(p, axis=1, keepdims=True)
            dp = _dot(dyh, vh, "nt")
            ds = p * (dp - jnp.sum(p * dp, axis=1, keepdims=True))
            dq = _dot(ds, kh) * scale
            dx, dg = _rms_bwd(qm[:, sl], gq, dq)
            dqs.append(dx)
            dgq = dgq + dg
            st = _dot(kh, q, "nt")
            pt = jnp.exp(st - jnp.max(st, axis=0, keepdims=True))
            pt = pt / jnp.sum(pt, axis=0, keepdims=True)
            dpt = _dot(vh, dyh, "nt")
            dst = pt * (dpt - jnp.sum(pt * dpt, axis=0, keepdims=True))
            dvs.append(_dot(pt, dyh))
            dks.append(_dot(dst, q))
        return jnp.concatenate(dqs, axis=1), jnp.concatenate(dks, axis=1), jnp.concatenate(dvs, axis=1), dgq

    m = km.shape[0]
    return _rowwise(fn, "mem_bwd", 512, [(z, MEM_WIDTH, Z_QM // MEM_WIDTH), dy], [km, vm, gq],
                    [("into", dz, MEM_WIDTH, Z_QM // MEM_WIDTH)],
                    [((m, MEM_WIDTH), F32), ((m, MEM_WIDTH), F32), ((1, LANES), F32)])


GROUPS = {"ffn1": ["ffn1_w_gu"], "ffn1_down": ["ffn1_w_down"],
          "mix": ["w_in", "mla_w_uq", "mla_w_ukv", "mem_w_kv", "w_branch_a", "w_branch_b", "w_branch_c", "w_out"],
          "ffn2": ["ffn2_w_gu", "ffn2_w_down"]}
GRAD_GROUPS = {"ffn2": GROUPS["ffn2"], "mix": GROUPS["mix"], "ffn1_down": ["ffn1_w_down"], "ffn1_gu": ["ffn1_w_gu"]}


def _local_step(x, mem, positions, loss_target, P, weights, grads_out):
    T = x.shape[0]
    G = {}
    W = dict(weights("ffn1", None))

    half = MLA_ROPE // 2
    inv = ROPE_BASE ** (-jnp.arange(half, dtype=F32) / half)
    ang = positions.astype(F32)[:, None] * inv
    cos, sin = jnp.cos(ang), jnp.sin(ang)
    one, zero = jnp.ones((T, MLA_NOPE), F32), jnp.zeros((T, half), F32)
    pad = LANES - MLA_QK
    tabs = (jnp.concatenate([one, cos, cos, jnp.ones((T, pad), F32)], axis=1),
            jnp.concatenate([jnp.zeros((T, MLA_NOPE), F32), -sin, zero, jnp.zeros((T, pad), F32)], axis=1),
            jnp.concatenate([jnp.zeros((T, MLA_NOPE), F32), zero, sin, jnp.zeros((T, pad), F32)], axis=1))
    gq_p = jnp.pad(P["mla_q_norm"], ((0, 0), (0, pad)))
    gk_p = jnp.pad(P["mla_k_norm"], ((0, 0), (0, pad)))
    bias_full = jnp.repeat(P["sg_b"].T, SG_GROUP_DIM, axis=1)
    group_ind = jnp.repeat(jnp.eye(SG_GROUPS, dtype=F32), SG_GROUP_DIM, axis=0)

    HT = (D_MODEL, BF16, "T")

    def norm2(x, g):
        h = _rms(x, g)
        return h, h

    h1, h1t = _rowwise(norm2, "ffn1_norm", 512, [x], [P["ffn1_norm"]], [(D_MODEL, BF16), HT])
    def ffn1_w_down(after):
        W.update(weights("ffn1_down", after))
        return W["ffn1_w_down"]

    gu1, a1t, o1 = _ffn_fwd(h1, W["ffn1_w_gu"], ffn1_w_down, "ffn1")

    def resid_norm(x, o, g):
        xn = x + 0.5 * o
        h = _rms(xn, g)
        return xn, h, h

    x1, hm, hmt = _rowwise(resid_norm, "mix_norm", 512, [x, o1], [P["mix_norm"]],
                           [(D_MODEL, F32), (D_MODEL, BF16), HT])
    W.update(weights("mix", hm))
    z = _mm(hm, W["w_in"], "nn", BF16, "w_in", tm=1024, tn=1792)

    y_a, y_at = _sg_fwd(z, P["sg_ln_g"], P["sg_ln_b"], P["sg_w"], bias_full)

    def c_norm(cq, ckv, gq, gkv):
        a, b = _rms(cq, gq), _rms(ckv, gkv)
        return a, b, a, b

    cqn, ckvn, cqnt, ckvnt = _rowwise(
        c_norm, "mla_cnorm", 512, [(z, MLA_Q_RANK, Z_CQ // MLA_Q_RANK), (z, MLA_KV_RANK, Z_CKV // MLA_KV_RANK)],
        [P["mla_cq_norm"], P["mla_ckv_norm"]],
        [(MLA_Q_RANK, BF16), (MLA_KV_RANK, BF16), (MLA_Q_RANK, BF16, "T"), (MLA_KV_RANK, BF16, "T")])
    q_pre = _mm(cqn, W["mla_w_uq"], "nn", BF16, "mla_uq", tm=1024, tn=1024)
    kv_pre = _mm(ckvn, W["mla_w_ukv"], "nn", BF16, "mla_ukv", tm=1024, tn=1024)
    q, k, v = _mla_post(q_pre, kv_pre, z, tabs, gq_p, gk_p)
    y_b, y_bt, lse, lse_t = _flash_fwd(q, k, v)

    memn, = _rowwise(lambda m, g: _rms(m, g), "mem_norm", 256, [mem], [P["mem_norm"]], [(D_MODEL, BF16)])
    kvm = _mm(memn, W["mem_w_kv"], "nn", F32, "mem_kv")

    def mem_k(kvm, gk):
        ks = [_rms(kvm[:, h * LANES:(h + 1) * LANES], gk) for h in range(MEM_HEADS)]
        return jnp.concatenate(ks, axis=1), kvm[:, MEM_WIDTH:]

    km, vm = _rowwise(mem_k, "mem_knorm", 256, [kvm], [P["mem_k_norm"]], [(MEM_WIDTH, BF16), (MEM_WIDTH, BF16)])
    y_c, y_ct = _mem_fwd(z, km, vm, P["mem_q_norm"])

    pa = _mm(y_a, W["w_branch_a"], "nn", BF16, "branch_a", tm=1024, tn=1024)
    pb = _mm(y_b, W["w_branch_b"], "nn", BF16, "branch_b", tm=1024, tn=1024)
    pc = _mm(y_c, W["w_branch_c"], "nn", BF16, "branch_c", tm=1024, tn=1024)

    def merge(zg, pa, pb, pc, b):
        g = _sigmoid(zg + b)
        m = g[:, :D_MODEL] * pa + g[:, D_MODEL:2 * D_MODEL] * pb + g[:, 2 * D_MODEL:] * pc
        return m, m

    merged, mergedt = _rowwise(merge, "merge", 256, [(z, 3 * D_MODEL, 0), pa, pb, pc], [P["b_gate"]],
                               [(D_MODEL, BF16), HT])
    om = _mm(merged, W["w_out"], "nn", BF16, "w_out", tm=1024, tn=1024)

    def resid_norm1(x, o, g):
        xn = x + o
        h = _rms(xn, g)
        return xn, h, h

    x2, h2, h2t = _rowwise(resid_norm1, "ffn2_norm", 512, [x1, om], [P["ffn2_norm"]],
                           [(D_MODEL, F32), (D_MODEL, BF16), HT])
    W.update(weights("ffn2", h2))
    gu2, a2t, o2 = _ffn_fwd(h2, W["ffn2_w_gu"], W["ffn2_w_down"], "ffn2")

    def loss_fn(x2, o2, t):
        e = x2 + 0.5 * o2 - t
        return e * (1.0 / D_MODEL), (e * (0.5 / D_MODEL)).astype(BF16), _rsum(e * e) * (0.5 / D_MODEL)

    dx3, do2, loss_part = _rowwise(loss_fn, "loss", 512, [x2, o2, loss_target], [],
                                   [(D_MODEL, F32), (D_MODEL, BF16)], [((1, D_MODEL), F32)])

    dh2, G["ffn2_w_gu"], G["ffn2_w_down"] = _ffn_bwd(do2, h2t, gu2, a2t, W["ffn2_w_gu"], W["ffn2_w_down"], "ffn2")
    tie = grads_out("ffn2", G)

    def norm_bwd(x, dh, dxo, g, *_):
        dx, dg = _rms_bwd(x, g, dh)
        dx = dx + dxo
        return dx, dx, dg

    dx2, dx2b, G["ffn2_norm"] = _rowwise(norm_bwd, "ffn2_norm_bwd", 512, [x2, dh2, dx3],
                                         [P["ffn2_norm"]] + ([] if tie is None else [tie]),
                                         [(D_MODEL, F32), (D_MODEL, BF16)], [((1, D_MODEL), F32)])

    G["w_out"] = _mm_t(mergedt, dx2b, "w_out_dw", tm=1024, tn=1024)
    dmerged = _mm(dx2b, W["w_out"], "nt", BF16, "w_out_dx", tm=1024, tn=1024)

    def merge_bwd(zg, pa, pb, pc, dm, b):
        g = _sigmoid(zg + b)
        ps = jnp.concatenate([pa, pb, pc], axis=1)
        dm3 = jnp.concatenate([dm, dm, dm], axis=1)
        dzg = dm3 * ps * g * (1.0 - g)
        dp = dm3 * g
        return dzg, dp[:, :D_MODEL], dp[:, D_MODEL:2 * D_MODEL], dp[:, 2 * D_MODEL:], _rsum(dzg)

    dz = lax.empty((T, Z_COLS), BF16)
    dz, dpa, dpb, dpc, G["b_gate"] = _rowwise(
        merge_bwd, "merge_bwd", 256, [(z, 3 * D_MODEL, 0), pa, pb, pc, dmerged], [P["b_gate"]],
        [("into", dz, 3 * D_MODEL, 0), (D_MODEL, BF16), (D_MODEL, BF16), (D_MODEL, BF16)], [((1, 3 * D_MODEL), F32)])

    G["w_branch_a"] = _mm_t(y_at, dpa, "branch_a_dw", tm=512, tn=1024)
    G["w_branch_b"] = _mm_t(y_bt, dpb, "branch_b_dw", tm=1024, tn=1024)
    G["w_branch_c"] = _mm_t(y_ct, dpc, "branch_c_dw", tm=512, tn=1024)
    dy_a = _mm(dpa, W["w_branch_a"], "nt", BF16, "branch_a_dx", tm=1024, tn=512)
    dy_b = _mm(dpb, W["w_branch_b"], "nt", BF16, "branch_b_dx", tm=1024, tn=1024)
    dy_c = _mm(dpc, W["w_branch_c"], "nt", BF16, "branch_c_dx", tm=1024, tn=512)

    dz, G["sg_w"], dbias_t, G["sg_ln_g"], G["sg_ln_b"] = _sg_bwd(
        z, dy_a, P["sg_ln_g"], P["sg_ln_b"], P["sg_w"], bias_full, group_ind, dz)
    G["sg_b"] = dbias_t.T

    dz, dkm, dvm, G["mem_q_norm"] = _mem_bwd(z, dy_c, km, vm, P["mem_q_norm"], dz)

    def mem_k_bwd(kvm, dkm, dvm, gk):
        dks = []
        dg = jnp.zeros((1, LANES), F32)
        for h in range(MEM_HEADS):
            sl = slice(h * LANES, (h + 1) * LANES)
            dx, d = _rms_bwd(kvm[:, sl], gk, dkm[:, sl])
            dks.append(dx)
            dg = dg + d
        return jnp.concatenate(dks + [dvm], axis=1), dg

    dkvm, G["mem_k_norm"] = _rowwise(mem_k_bwd, "mem_knorm_bwd", 256, [kvm, dkm, dvm], [P["mem_k_norm"]],
                                     [(2 * MEM_WIDTH, BF16)], [((1, LANES), F32)])
    G["mem_w_kv"] = _mm(memn, dkvm, "tn", BF16, "mem_kv_dw")
    dmemn = _mm(dkvm, W["mem_w_kv"], "nt", F32, "mem_kv_dx")
    _, G["mem_norm"] = _rowwise(lambda m, d, g: _rms_bwd(m, g, d), "mem_norm_bwd", 256, [mem, dmemn],
                                [P["mem_norm"]], [(D_MODEL, BF16)], [((1, D_MODEL), F32)])

    def delta_fn(o, do):
        od = o.astype(F32) * do.astype(F32)
        ds = [jnp.broadcast_to(jnp.sum(od[:, h * LANES:(h + 1) * LANES], axis=1, keepdims=True), (od.shape[0], LANES))
              for h in range(MLA_HEADS)]
        d = jnp.concatenate(ds, axis=1)
        return d, d

    delta, delta_t = _rowwise(delta_fn, "mla_delta", 512, [y_b, dy_b], [], [(HP, F32), (HP, F32, "T")])
    dq = _flash_dq(q, k, v, dy_b, lse, delta)
    dk, dv = _flash_dkv(q, k, v, dy_b, lse_t, delta_t)
    dq_pre, dkv_pre, dkr, dgq, dgk = _mla_post_bwd(q_pre, kv_pre, z, tabs, gq_p, gk_p, dq, dk, dv)
    G["mla_q_norm"], G["mla_k_norm"] = dgq[:, :MLA_QK], dgk[:, :MLA_QK]
    G["mla_w_uq"] = _mm_t(cqnt, dq_pre, "mla_uq_dw", tm=384, tn=1024)
    G["mla_w_ukv"] = _mm_t(ckvnt, dkv_pre, "mla_ukv_dw", tm=256, tn=2048)
    dcqn = _mm(dq_pre, W["mla_w_uq"], "nt", BF16, "mla_uq_dx", tm=1024)
    dckvn = _mm(dkv_pre, W["mla_w_ukv"], "nt", BF16, "mla_ukv_dx", tm=1024)

    def c_norm_bwd(cq, ckv, dcqn, dckvn, dkr, gq, gkv):
        dcq, dgq = _rms_bwd(cq, gq, dcqn)
        dckv, dgkv = _rms_bwd(ckv, gkv, dckvn)
        return jnp.concatenate([dckv, dkr, dcq], axis=1), dgq, dgkv

    tail = Z_COLS - Z_CKV
    dz, G["mla_cq_norm"], G["mla_ckv_norm"] = _rowwise(
        c_norm_bwd, "mla_cnorm_bwd", 512,
        [(z, MLA_Q_RANK, Z_CQ // MLA_Q_RANK), (z, MLA_KV_RANK, Z_CKV // MLA_KV_RANK), dcqn, dckvn, dkr],
        [P["mla_cq_norm"], P["mla_ckv_norm"]], [("into", dz, tail, Z_CKV // tail)],
        [((1, MLA_Q_RANK), F32), ((1, MLA_KV_RANK), F32)])
    G["w_in"] = _mm_t(hmt, dz, "w_in_dw", tm=1024, tn=1792, tk=2048)
    dhm = _mm(dz, W["w_in"], "nt", BF16, "w_in_dx", tm=1024, tn=1024, tk=2688)

    def norm_bwd_half(x, dh, dxo, g):
        dx, dg = _rms_bwd(x, g, dh)
        dx = dx + dxo
        return dx, (0.5 * dx), dg

    dx1, do1, G["mix_norm"] = _rowwise(norm_bwd_half, "mix_norm_bwd", 512, [x1, dhm, dx2], [P["mix_norm"]],
                                       [(D_MODEL, F32), (D_MODEL, BF16)], [((1, D_MODEL), F32)])
    tie = grads_out("mix", G)

    def ffn1_dw(which, dw):
        G["ffn1_w_" + which] = dw
        return grads_out("ffn1_" + which, G)

    dh1, _, _ = _ffn_bwd(do1, h1t, gu1, a1t, W["ffn1_w_gu"], W["ffn1_w_down"], "ffn1", tie, ffn1_dw)

    def norm_bwd_last(x, dh, dxo, g):
        dx, dg = _rms_bwd(x, g, dh)
        return dx + dxo, dg

    grad_x, G["ffn1_norm"] = _rowwise(norm_bwd_last, "ffn1_norm_bwd", 512, [x, dh1, dx1], [P["ffn1_norm"]],
                                      [(D_MODEL, F32)], [((1, D_MODEL), F32)])
    return loss_part, grad_x, G


SHARDED = ["ffn1_w_gu", "ffn1_w_down", "w_in", "mla_w_uq", "mla_w_ukv", "mem_w_kv",
           "w_branch_a", "w_branch_b", "w_branch_c", "w_out", "ffn2_w_gu", "ffn2_w_down"]
ROW_SHARDED = {"ffn1_w_down", "mem_w_kv", "w_out", "ffn2_w_down"}
SMALL = ["ffn1_norm", "mix_norm", "b_gate", "sg_ln_g", "sg_ln_b", "sg_w", "sg_b", "mla_cq_norm",
         "mla_ckv_norm", "mla_q_norm", "mla_k_norm", "mem_norm", "mem_q_norm", "mem_k_norm", "ffn2_norm"]
ORDER = ["ffn1_norm", "ffn1_w_gu", "ffn1_w_down", "mix_norm", "w_in", "b_gate", "sg_ln_g", "sg_ln_b", "sg_w",
         "sg_b", "mla_cq_norm", "mla_w_uq", "mla_ckv_norm", "mla_w_ukv", "mla_q_norm", "mla_k_norm", "mem_norm",
         "mem_w_kv", "mem_q_norm", "mem_k_norm", "w_branch_a", "w_branch_b", "w_branch_c", "w_out", "ffn2_norm",
         "ffn2_w_gu", "ffn2_w_down"]

_IN_U, _IN_V, _IN_CQ, _IN_CKV, _IN_KR, _IN_QM, _IN_G = 0, 512, 1024, 1408, 1664, 1696, 2208
IN_COLS = 5280


def _full_from_slabs(name, slabs):
    n, r, c = slabs.shape
    if name in ROW_SHARDED:
        return slabs.reshape(n * r, c)
    return slabs.transpose(1, 0, 2).reshape(r, n * c)


def _slabs_from_full(name, full):
    if name in ROW_SHARDED:
        return full.reshape(N_DEV, full.shape[0] // N_DEV, full.shape[1])
    r, c = full.shape
    return full.reshape(r, N_DEV, c // N_DEV).transpose(1, 0, 2)


def _compute_layout(full):
    W = dict(full)
    if "w_in" not in full:
        return W
    w = full["w_in"]
    kr = jnp.pad(w[:, _IN_KR:_IN_QM], ((0, 0), (KR_LANE, LANES - KR_LANE - MLA_ROPE)))
    W["w_in"] = jnp.concatenate([w[:, _IN_G:], w[:, _IN_U:_IN_CQ], w[:, _IN_QM:_IN_G], w[:, _IN_CKV:_IN_KR], kr,
                                 w[:, _IN_CQ:_IN_CKV]], axis=1)
    uq = full["mla_w_uq"].reshape(MLA_Q_RANK, MLA_HEADS, MLA_QK)
    W["mla_w_uq"] = jnp.pad(uq, ((0, 0), (0, 0), (0, LANES - MLA_QK))).reshape(MLA_Q_RANK, HP)
    ukv = full["mla_w_ukv"].reshape(MLA_KV_RANK, MLA_HEADS, MLA_NOPE + MLA_V)
    padh = lambda a: jnp.pad(a, ((0, 0), (0, 0), (0, LANES - a.shape[2]))).reshape(MLA_KV_RANK, HP)
    W["mla_w_ukv"] = jnp.concatenate([padh(ukv[:, :, :MLA_NOPE]), padh(ukv[:, :, MLA_NOPE:])], axis=1)
    wb = full["w_branch_b"].reshape(MLA_HEADS, MLA_V, D_MODEL)
    W["w_branch_b"] = jnp.pad(wb, ((0, 0), (0, LANES - MLA_V), (0, 0))).reshape(HP, D_MODEL)
    return W


def _reference_layout(G):
    out = dict(G)
    if "w_in" not in G:
        return out
    g = G["w_in"]
    out["w_in"] = jnp.concatenate([
        g[:, Z_U:Z_QM], g[:, Z_CQ:Z_COLS], g[:, Z_CKV:Z_KR], g[:, Z_KR + KR_LANE:Z_KR + KR_LANE + MLA_ROPE],
        g[:, Z_QM:Z_CKV], g[:, Z_G:Z_U]], axis=1)
    out["mla_w_uq"] = G["mla_w_uq"].reshape(MLA_Q_RANK, MLA_HEADS, LANES)[:, :, :MLA_QK].reshape(MLA_Q_RANK, -1)
    gk = G["mla_w_ukv"][:, :HP].reshape(MLA_KV_RANK, MLA_HEADS, LANES)[:, :, :MLA_NOPE]
    gv = G["mla_w_ukv"][:, HP:].reshape(MLA_KV_RANK, MLA_HEADS, LANES)[:, :, :MLA_V]
    out["mla_w_ukv"] = jnp.concatenate([gk, gv], axis=2).reshape(MLA_KV_RANK, -1)
    out["w_branch_b"] = G["w_branch_b"].reshape(MLA_HEADS, LANES, D_MODEL)[:, :MLA_V].reshape(-1, D_MODEL)
    return out


def _pack(parts):
    flat = []
    for a in parts:
        a = a.reshape(-1)
        flat.append(jnp.pad(a, (0, (-a.shape[0]) % LANES)))
    return jnp.concatenate(flat).reshape(-1, LANES)


def _unpack(packed, shapes):
    flat = packed.reshape(-1)
    out, off = [], 0
    for shp in shapes:
        n = int(np.prod(shp))
        out.append(flat[off:off + n].reshape(shp))
        off += n + (-n) % LANES
    return out


MESH = pl.DeviceIdType.MESH
HBM = pl.BlockSpec(memory_space=pltpu.HBM)


def _all_gather(shards):
    n = len(shards)

    def body(*refs):
        x_refs, out_refs, token_ref = refs[:n], refs[n:2 * n], refs[2 * n]
        send_sems, recv_sems, local_sems = refs[2 * n + 1:]
        x, y, c = lax.axis_index("x"), lax.axis_index("y"), lax.axis_index("c")
        me, sibling = (x, y, c), (x, y, 1 - c)
        chips = [(1 - x, y), (x, 1 - y), (1 - x, 1 - y)]
        token_ref[...] = jnp.zeros_like(token_ref)

        def slot(a, px, py, pc):
            return out_refs[a].at[4 * px + 2 * py + pc]

        def copy(a, k, block, to, src=None):
            return pltpu.make_async_remote_copy(
                src_ref=slot(a, *block) if src is None else src, dst_ref=slot(a, *block),
                send_sem=send_sems.at[7 * a + k], recv_sem=recv_sems.at[7 * a + k], device_id=to, device_id_type=MESH)

        arrays = range(n)
        mine = [pltpu.make_async_copy(x_refs[a], slot(a, *me), local_sems.at[a]) for a in arrays]
        for cp in mine:
            cp.start()
        first = [copy(a, 0, me, sibling, src=x_refs[a]) for a in arrays]
        first += [copy(a, 1 + j, me, (*chip, c), src=x_refs[a]) for j, chip in enumerate(chips) for a in arrays]
        for cp in first:
            cp.start()
        passed = []
        for j, chip in enumerate(chips):
            for a in arrays:
                copy(a, 1 + j, (*chip, c), me).wait_recv()
                passed.append(copy(a, 4 + j, (*chip, c), sibling))
                passed[-1].start()
        for a in arrays:
            copy(a, 0, sibling, me).wait_recv()
        for j, chip in enumerate(chips):
            for a in arrays:
                copy(a, 4 + j, (*chip, 1 - c), me).wait_recv()
        for cp in first + passed:
            cp.wait_send()
        for cp in mine:
            cp.wait()

    res = pl.pallas_call(
        body, name="all_gather_weights",
        out_shape=[jax.ShapeDtypeStruct((N_DEV,) + s.shape, s.dtype) for s in shards]
        + [jax.ShapeDtypeStruct((8, LANES), F32)],
        in_specs=[HBM] * n, out_specs=[HBM] * n + [pl.BlockSpec(memory_space=pltpu.VMEM)],
        scratch_shapes=[pltpu.SemaphoreType.DMA((7 * n,)), pltpu.SemaphoreType.DMA((7 * n,)),
                        pltpu.SemaphoreType.DMA((n,))],
    )(*shards)
    return res[:n], res[n]


SEM = pl.BlockSpec(memory_space=pltpu.SEMAPHORE)
DATAFLOW = pltpu.SideEffectType.DATAFLOW_SIDE_EFFECTING


def _peers():
    x, y, c = lax.axis_index("x"), lax.axis_index("y"), lax.axis_index("c")
    out = []
    for k in range(1, N_DEV):
        px = 1 - x if k & 4 else x
        py = 1 - y if k & 2 else y
        pc = 1 - c if k & 1 else c
        out.append((k, (px, py, pc), 4 * px + 2 * py + pc))
    return 4 * x + 2 * y + c, out


def _send_start(srcs, per_peer, name):
    n = len(srcs)
    lands = [lax.empty((N_DEV,) + (s.shape[1:] if per_peer else s.shape), s.dtype) for s in srcs]

    def body(*refs):
        src_refs, land_refs, send_sems, recv_sems, token = refs[:n], refs[n:2 * n], refs[2 * n], refs[2 * n + 1], refs[-1]
        me, peers = _peers()
        for a in range(n):
            for k, pid, pflat in peers:
                pltpu.make_async_remote_copy(
                    src_ref=src_refs[a].at[pflat] if per_peer else src_refs[a], dst_ref=land_refs[a].at[me],
                    send_sem=send_sems.at[7 * a + k - 1], recv_sem=recv_sems.at[7 * a + k - 1],
                    device_id=pid, device_id_type=MESH).start()
        token[...] = jnp.zeros_like(token)

    hbm = lambda a: pltpu.with_memory_space_constraint(a, pltpu.HBM)
    res = pl.pallas_call(
        body, name=name,
        out_shape=(pltpu.SemaphoreType.DMA((7 * n,)), pltpu.SemaphoreType.DMA((7 * n,)),
                   *[pltpu.HBM(a.shape, a.dtype) for a in srcs + lands], jax.ShapeDtypeStruct((8, LANES), F32)),
        in_specs=(HBM,) * (2 * n), out_specs=(SEM, SEM) + (HBM,) * (2 * n) + (pl.BlockSpec(memory_space=pltpu.VMEM),),
        input_output_aliases={i: 2 + i for i in range(2 * n)},
        compiler_params=pltpu.CompilerParams(has_side_effects=DATAFLOW),
    )(*[hbm(a) for a in srcs + lands])
    return (res[0], res[1], list(res[2:2 + n]), list(res[2 + n:2 + 2 * n])), res[-1]


def _send_wait(started, after, per_peer, name):
    send_sems, recv_sems, srcs_thru, lands_thru = started
    n = len(srcs_thru)

    def body(*refs):
        src_refs, land_refs, send_sems, recv_sems = refs[:n], refs[n:2 * n], refs[2 * n], refs[2 * n + 1]
        me, peers = _peers()
        for a in range(n):
            for k, pid, pflat in peers:
                copy = pltpu.make_async_remote_copy(
                    src_ref=src_refs[a].at[pflat] if per_peer else src_refs[a], dst_ref=land_refs[a].at[pflat],
                    send_sem=send_sems.at[7 * a + k - 1], recv_sem=recv_sems.at[7 * a + k - 1],
                    device_id=pid, device_id_type=MESH)
                copy.wait_send()
                copy.wait_recv()

    outs = pl.pallas_call(
        body, name=name,
        out_shape=tuple(pltpu.HBM(a.shape, a.dtype) for a in srcs_thru + lands_thru),
        in_specs=(HBM,) * (2 * n) + (SEM, SEM, pl.BlockSpec(memory_space=pl.ANY)), out_specs=(HBM,) * (2 * n),
        input_output_aliases={i: i for i in range(2 * n)},
        compiler_params=pltpu.CompilerParams(has_side_effects=DATAFLOW),
    )(*srcs_thru, *lands_thru, send_sems, recv_sems, after)
    me = 4 * lax.axis_index("x") + 2 * lax.axis_index("y") + lax.axis_index("c")
    landed = []
    for src_out, land in zip(outs[:n], outs[n:]):
        own = lax.dynamic_index_in_dim(src_out, me, 0, keepdims=True) if per_peer else src_out[None]
        landed.append(lax.dynamic_update_slice(land, own, (me,) + (0,) * (land.ndim - 1)))
    return landed


def _share_rows(block, name):
    def body(src_ref, out_ref, send_sems, recv_sems, local_sem):
        me, peers = _peers()
        own = pltpu.make_async_copy(src_ref, out_ref.at[me], local_sem)
        own.start()
        copies = [pltpu.make_async_remote_copy(
            src_ref=src_ref, dst_ref=out_ref.at[me], send_sem=send_sems.at[k - 1], recv_sem=recv_sems.at[k - 1],
            device_id=pid, device_id_type=MESH) for k, pid, _ in peers]
        for cp in copies:
            cp.start()
        for cp in copies:
            cp.wait()
        own.wait()

    return pl.pallas_call(
        body, name=name, out_shape=jax.ShapeDtypeStruct((N_DEV,) + block.shape, block.dtype),
        in_specs=[HBM], out_specs=HBM,
        scratch_shapes=[pltpu.SemaphoreType.DMA((N_DEV - 1,)), pltpu.SemaphoreType.DMA((N_DEV - 1,)),
                        pltpu.SemaphoreType.DMA],
    )(block)


def _sum_slots(recv, name, tr):
    n, rows, lanes = recv.shape
    tr = _tile(rows, tr)

    def body(r_ref, o_ref):
        acc = r_ref[0].astype(F32)
        for i in range(1, n):
            acc = acc + r_ref[i].astype(F32)
        o_ref[...] = acc

    return pl.pallas_call(
        body, name=name, grid=(rows // tr,),
        in_specs=[pl.BlockSpec((n, tr, lanes), lambda i: (0, i, 0))],
        out_specs=pl.BlockSpec((tr, lanes), lambda i: (i, 0)),
        out_shape=jax.ShapeDtypeStruct((rows, lanes), F32),
        compiler_params=_cparams(("parallel",)),
    )(recv)


def _adamw_math(w, g, m, v):
    m = ADAM_B1 * m + (1.0 - ADAM_B1) * g
    v = ADAM_B2 * v + (1.0 - ADAM_B2) * (g * g)
    m_hat = m / (1.0 - ADAM_B1 ** ADAM_STEP)
    v_hat = v / (1.0 - ADAM_B2 ** ADAM_STEP)
    return -ADAM_LR * (m_hat / (jnp.sqrt(v_hat) + ADAM_EPS) + ADAM_WD * w), m, v


def _adamw(w, g, m, v, name, tr=256):
    return _rowwise(_adamw_math, name, tr, [w, g, m, v], [], [(w.shape[1], F32)] * 3)


def _adamw_small(ws, gs, ms, vs):
    n = len(ws)

    def body(*refs):
        ins, outs = refs[:4 * n], refs[4 * n:]
        for i in range(n):
            d, m, v = _adamw_math(ins[i][...], ins[n + i][...], ins[2 * n + i][...], ins[3 * n + i][...])
            outs[i][...], outs[n + i][...], outs[2 * n + i][...] = d, m, v

    vmem = pl.BlockSpec(memory_space=pltpu.VMEM)
    res = pl.pallas_call(
        body, name="adamw_small", in_specs=[vmem] * (4 * n), out_specs=[vmem] * (3 * n),
        out_shape=[jax.ShapeDtypeStruct(w.shape, F32) for w in ws] * 3,
    )(*ws, *gs, *ms, *vs)
    return res[:n], res[n:2 * n], res[2 * n:]


def _sum_adamw(recv, w, m, v, name):
    n, r, c = recv.shape
    tr = _tile(r, 256)

    def body(r_ref, w_ref, m_ref, v_ref, g_ref, d_ref, nm_ref, nv_ref):
        g = r_ref[0].astype(F32)
        for i in range(1, n):
            g = g + r_ref[i].astype(F32)
        g_ref[...] = g
        d_ref[...], nm_ref[...], nv_ref[...] = _adamw_math(w_ref[...], g, m_ref[...], v_ref[...])

    row = pl.BlockSpec((None, tr, c), lambda i: (0, i, 0))
    return pl.pallas_call(
        body, name=name, grid=(r // tr,),
        in_specs=[pl.BlockSpec((n, tr, c), lambda i: (0, i, 0)), row, row, row], out_specs=[row] * 4,
        out_shape=[jax.ShapeDtypeStruct((1, r, c), F32)] * 4, compiler_params=_cparams(("parallel",)),
    )(recv, w, m, v)


def kernel(x, mem, positions, ffn1_norm, ffn1_w_gu, ffn1_w_down, mix_norm, w_in, b_gate, sg_ln_g, sg_ln_b, sg_w, sg_b, mla_cq_norm, mla_w_uq, mla_ckv_norm, mla_w_ukv, mla_q_norm, mla_k_norm, mem_norm, mem_w_kv, mem_q_norm, mem_k_norm, w_branch_a, w_branch_b, w_branch_c, w_out, ffn2_norm, ffn2_w_gu, ffn2_w_down, loss_target, m_ffn1_norm, m_ffn1_w_gu, m_ffn1_w_down, m_mix_norm, m_w_in, m_b_gate, m_sg_ln_g, m_sg_ln_b, m_sg_w, m_sg_b, m_mla_cq_norm, m_mla_w_uq, m_mla_ckv_norm, m_mla_w_ukv, m_mla_q_norm, m_mla_k_norm, m_mem_norm, m_mem_w_kv, m_mem_q_norm, m_mem_k_norm, m_w_branch_a, m_w_branch_b, m_w_branch_c, m_w_out, m_ffn2_norm, m_ffn2_w_gu, m_ffn2_w_down, v_ffn1_norm, v_ffn1_w_gu, v_ffn1_w_down, v_mix_norm, v_w_in, v_b_gate, v_sg_ln_g, v_sg_ln_b, v_sg_w, v_sg_b, v_mla_cq_norm, v_mla_w_uq, v_mla_ckv_norm, v_mla_w_ukv, v_mla_q_norm, v_mla_k_norm, v_mem_norm, v_mem_w_kv, v_mem_q_norm, v_mem_k_norm, v_w_branch_a, v_w_branch_b, v_w_branch_c, v_w_out, v_ffn2_norm, v_ffn2_w_gu, v_ffn2_w_down):
    given = dict(locals())
    wts = {n: given[n] for n in ORDER}
    mom = {n: given["m_" + n] for n in ORDER}
    var = {n: given["v_" + n] for n in ORDER}

    def shards(group, zero):
        out = [wts[n][0].astype(BF16) for n in GROUPS[group]]
        return [out[0] + zero.astype(BF16)] + out[1:]

    def full_weights(group, slabs):
        return _compute_layout({n: _full_from_slabs(n, s) for n, s in zip(GROUPS[group], slabs)})

    def zero_of(a):
        return jnp.minimum(jnp.abs(a.reshape(-1)[0]), 0)

    gathered_ffn1, token = _all_gather([wts[n][0].astype(BF16) for n in GROUPS["ffn1"]])
    flight = {}
    flight["ffn1_down"], token = _send_start(shards("ffn1_down", token[0, 0]), False, "gather_ffn1_down_start")
    flight["mix"] = _send_start(shards("mix", token[0, 0]), False, "gather_mix_start")[0]
    recv = {}

    def weights(group, after):
        if group == "ffn1":
            return full_weights(group, gathered_ffn1)
        landed = _send_wait(flight.pop(group), after, False, f"gather_{group}_wait")
        if group == "mix":
            flight["ffn2"] = _send_start(shards("ffn2", zero_of(landed[0])), False, "gather_ffn2_start")[0]
        return full_weights(group, landed)

    small_shapes = [wts[n].shape[1:] for n in SMALL]
    early = SMALL[1:]
    assert SMALL[0] == "ffn1_norm"

    def grads_out(group, G):
        Gr = _reference_layout({n: G[n] for n in GRAD_GROUPS[group]})
        parts = [_slabs_from_full(n, Gr[n]).astype(BF16) for n in GRAD_GROUPS[group]]
        flight["g_" + group], tie = _send_start(parts, True, f"grads_{group}_start")
        if group == "mix":
            small = _pack([G[n].reshape(s) for n, s in zip(early, small_shapes[1:])])
            small = jnp.pad(small, ((0, (-small.shape[0]) % 8), (0, 0)))
            flight["small"], tie = _send_start([small + tie[0, 0]], False, "grads_small_start")
        return tie

    P = {n: wts[n] if wts[n].ndim == 2 else wts[n][0] for n in SMALL}
    loss_part, grad_x, G = _local_step(x[0], mem[0], positions[0], loss_target[0], P, weights, grads_out)

    for group, names in GRAD_GROUPS.items():
        recv.update(zip(names, _send_wait(flight.pop("g_" + group), grad_x, True, f"grads_{group}_wait")))
    early_recv, = _send_wait(flight.pop("small"), grad_x, False, "grads_small_wait")
    last = _share_rows(G["ffn1_norm"].reshape(-1, LANES), "share_ffn1_norm")
    g_small_packed = _sum_slots(jnp.concatenate([last, early_recv], axis=1), "sum_small", 2048)

    grads, delta, new_m, new_v = {}, {}, {}, {}
    for n in SHARDED:
        grads[n], delta[n], new_m[n], new_v[n] = _sum_adamw(recv[n], wts[n], mom[n], var[n], "adamw_" + n)
    grads.update(zip(SMALL, _unpack(g_small_packed, small_shapes)))

    flat2 = lambda d: [d[n].reshape(-1, d[n].shape[-1]) for n in SMALL]
    for dst, vals in zip((delta, new_m, new_v), _adamw_small(flat2(wts), flat2(grads), flat2(mom), flat2(var))):
        dst.update(zip(SMALL, vals))

    loss = lax.psum(jnp.sum(loss_part), ("x", "y", "c"))
    lead = lambda d: [d[n].reshape(wts[n].shape) for n in ORDER]
    return (loss, grad_x[None], *lead(grads), *lead(delta), *lead(new_m), *lead(new_v))
```

```python
import functools

import numpy as np
import jax
import jax.numpy as jnp
from jax import lax
from jax.experimental import pallas as pl
from jax.experimental.pallas import tpu as pltpu

F32, BF16 = jnp.float32, jnp.bfloat16

D_MODEL = 1024
SG_GROUPS, SG_GROUP_DIM, SG_WIDTH, CHUNK = 8, 64, 512, 128
MLA_HEADS, MLA_NOPE, MLA_ROPE, MLA_V, MLA_QK = 8, 64, 32, 64, 96
MLA_Q_RANK, MLA_KV_RANK = 384, 256
MEM_HEADS, MEM_HEAD_DIM, MEM_WIDTH = 4, 128, 512
D_FF = 2816
ROPE_BASE = 10000.0
EPS = 1e-6
NEG = -1e30
ADAM_LR, ADAM_B1, ADAM_B2, ADAM_EPS, ADAM_WD, ADAM_STEP = 0.001, 0.9, 0.999, 1e-08, 0.01, 10

N_DEV = 8
LANES = 128
V7X_VMEM_LIMIT = 56 * 1024 * 1024
HP = MLA_HEADS * LANES

Z_G, Z_U, Z_V, Z_QM, Z_CKV, Z_KR, Z_CQ = 0, 3072, 3584, 4096, 4608, 4864, 4992
Z_COLS = 5376
KR_LANE = 64


def _tile(dim, pref):
    if dim <= pref:
        return dim
    for t in range(pref - pref % LANES, LANES - 1, -LANES):
        if dim % t == 0:
            return t
    for t in range(pref - pref % 8, 7, -8):
        if dim % t == 0:
            return t
    return dim


def _cparams(sem):
    return pltpu.CompilerParams(dimension_semantics=sem, vmem_limit_bytes=V7X_VMEM_LIMIT)


_DN = {"nn": ((1,), (0,)), "nt": ((1,), (1,)), "tn": ((0,), (0,))}


def _dot(a, b, mode="nn"):
    return lax.dot_general(a.astype(BF16), b.astype(BF16), (_DN[mode], ((), ())),
                           preferred_element_type=F32)


def _mm(a, b, mode, out_dtype, name, tm=512, tn=512, tk=2048, tie=None):
    if mode == "tn":
        K, M = a.shape
    else:
        M, K = a.shape
    N = b.shape[0] if mode == "nt" else b.shape[1]
    tm, tn, tk = _tile(M, tm), _tile(N, tn), _tile(K, tk)
    nk = K // tk
    if mode == "tn":
        a_spec = pl.BlockSpec((tk, tm), lambda i, j, k: (k, i))
    else:
        a_spec = pl.BlockSpec((tm, tk), lambda i, j, k: (i, k))
    if mode == "nt":
        b_spec = pl.BlockSpec((tn, tk), lambda i, j, k: (j, k))
    else:
        b_spec = pl.BlockSpec((tk, tn), lambda i, j, k: (k, j))

    ties = [] if tie is None else [tie]

    def body(a_ref, b_ref, *rest):
        o_ref, *scratch = rest[len(ties):]
        p = _dot(a_ref[...], b_ref[...], mode)
        if nk == 1:
            o_ref[...] = p.astype(o_ref.dtype)
        else:
            acc_ref, = scratch
            k = pl.program_id(2)

            @pl.when(k == 0)
            def _():
                acc_ref[...] = p

            @pl.when(k > 0)
            def _():
                acc_ref[...] += p

            @pl.when(k == nk - 1)
            def _():
                o_ref[...] = acc_ref[...].astype(o_ref.dtype)

    return pl.pallas_call(
        body, name=name, grid=(M // tm, N // tn, nk),
        in_specs=[a_spec, b_spec] + [pl.BlockSpec(t.shape, lambda i, j, k: (0, 0)) for t in ties],
        out_specs=pl.BlockSpec((tm, tn), lambda i, j, k: (i, j)),
        out_shape=jax.ShapeDtypeStruct((M, N), out_dtype),
        scratch_shapes=[] if nk == 1 else [pltpu.VMEM((tm, tn), F32)],
        compiler_params=_cparams(("parallel", "parallel", "arbitrary")),
    )(a, b, *ties)


def _mm_t(at, b, name, tm, tn, tk=1024, tie=None):
    return _mm(at, b, "nn", BF16, name, tm=tm, tn=tn, tk=tk, tie=tie)


def _rowwise(fn, name, tr, row_ins, bc_ins, row_outs, acc_outs=()):
    norm = [it if isinstance(it, tuple) else (it, it.shape[1], 0) for it in row_ins]
    rows = norm[0][0].shape[0]
    tr = _tile(rows, tr)
    arrays, in_specs = [], []
    for arr, w, cb in norm:
        arrays.append(arr)
        in_specs.append(pl.BlockSpec((tr, w), lambda i, cb=cb: (i, cb)))
    for arr in bc_ins:
        arrays.append(arr)
        in_specs.append(pl.BlockSpec(arr.shape, lambda i, nd=arr.ndim: (0,) * nd))
    n_in, n_row = len(arrays), len(row_outs)
    out_shape, out_specs, aliases = [], [], {}
    transposed = [len(o) == 3 for o in row_outs]
    for k, o in enumerate(row_outs):
        if o[0] == "into":
            _, target, w, cb = o
            aliases[len(arrays)] = k
            arrays.append(target)
            in_specs.append(pl.BlockSpec(memory_space=pl.ANY))
            out_shape.append(jax.ShapeDtypeStruct(target.shape, target.dtype))
            out_specs.append(pl.BlockSpec((tr, w), lambda i, cb=cb: (i, cb)))
        elif transposed[k]:
            out_shape.append(jax.ShapeDtypeStruct((o[0], rows), o[1]))
            out_specs.append(pl.BlockSpec((o[0], tr), lambda i: (0, i)))
        else:
            out_shape.append(jax.ShapeDtypeStruct((rows, o[0]), o[1]))
            out_specs.append(pl.BlockSpec((tr, o[0]), lambda i: (i, 0)))
    for shp, dt in acc_outs:
        out_shape.append(jax.ShapeDtypeStruct(shp, dt))
        out_specs.append(pl.BlockSpec(shp, lambda i, nd=len(shp): (0,) * nd))

    def body(*refs):
        vals = fn(*[r[...].astype(F32) for r in refs[:n_in]])
        if not isinstance(vals, (tuple, list)):
            vals = (vals,)
        outs = refs[len(arrays):]
        for r, v, t in zip(outs[:n_row], vals[:n_row], transposed):
            r[...] = v.astype(F32).T.astype(r.dtype) if t else v.astype(r.dtype)
        if acc_outs:
            accs = list(zip(outs[n_row:], vals[n_row:]))
            i = pl.program_id(0)

            @pl.when(i == 0)
            def _():
                for r, v in accs:
                    r[...] = v.astype(r.dtype)

            @pl.when(i > 0)
            def _():
                for r, v in accs:
                    r[...] += v.astype(r.dtype)

    res = pl.pallas_call(
        body, name=name, grid=(rows // tr,), in_specs=in_specs, out_specs=out_specs,
        out_shape=out_shape, input_output_aliases=aliases, compiler_params=_cparams(("arbitrary",)),
    )(*arrays)
    return res


def _rsum(x):
    return jnp.sum(x, axis=0, keepdims=True)


def _rms(x, g, n=None):
    n = x.shape[-1] if n is None else n
    r = lax.rsqrt(jnp.sum(x * x, axis=-1, keepdims=True) * (1.0 / n) + EPS)
    return x * r * g


def _rms_bwd(x, g, dy, n=None):
    n = x.shape[-1] if n is None else n
    r = lax.rsqrt(jnp.sum(x * x, axis=-1, keepdims=True) * (1.0 / n) + EPS)
    xh = x * r
    dxh = dy * g
    dx = r * (dxh - xh * (jnp.sum(dxh * xh, axis=-1, keepdims=True) * (1.0 / n)))
    return dx, _rsum(dy * xh)


def _gelu(x):
    return 0.5 * x * (1.0 + lax.erf(x * 0.7071067811865476))


def _gelu_grad(x):
    return 0.5 * (1.0 + lax.erf(x * 0.7071067811865476)) + x * jnp.exp(-0.5 * x * x) * 0.3989422804014327


def _sigmoid(x):
    return 0.5 * jnp.tanh(0.5 * x) + 0.5


FFN_TM, FFN_TN = 1024, 1408
MXU_WIDTH = 256


def _col_chunks(n):
    return [(c, min(c + MXU_WIDTH, n)) for c in range(0, n, MXU_WIDTH)]


def _ffn_gu_act(h, w_gu, tag):
    T = h.shape[0]
    tm, tn = _tile(T, FFN_TM), FFN_TN
    nj = D_FF // tn

    def body(h_ref, wg_ref, wu_ref, gu_ref, a_ref, at_ref):
        h = h_ref[...]
        for c0, c1 in _col_chunks(tn):
            g = _dot(h, wg_ref[:, c0:c1])
            u = _dot(h, wu_ref[:, c0:c1])
            gu_ref[0, :, c0:c1] = g.astype(BF16)
            gu_ref[1, :, c0:c1] = u.astype(BF16)
            a = g * _sigmoid(g) * u
            a_ref[:, c0:c1] = a.astype(BF16)
            at_ref[c0:c1, :] = a.T.astype(BF16)

    return pl.pallas_call(
        body, name=f"{tag}_gu_act", grid=(T // tm, nj),
        in_specs=[pl.BlockSpec((tm, D_MODEL), lambda i, j: (i, 0)),
                  pl.BlockSpec((D_MODEL, tn), lambda i, j: (0, j)),
                  pl.BlockSpec((D_MODEL, tn), lambda i, j: (0, j + nj))],
        out_specs=[pl.BlockSpec((2, tm, tn), lambda i, j: (0, i, j)),
                   pl.BlockSpec((tm, tn), lambda i, j: (i, j)),
                   pl.BlockSpec((tn, tm), lambda i, j: (j, i))],
        out_shape=[jax.ShapeDtypeStruct((2, T, D_FF), BF16), jax.ShapeDtypeStruct((T, D_FF), BF16),
                   jax.ShapeDtypeStruct((D_FF, T), BF16)],
        compiler_params=_cparams(("parallel", "parallel")),
    )(h, w_gu, w_gu)


def _ffn_da_actbwd(do, w_down, gu, tag, tie=None):
    T = do.shape[0]
    tm, tn = _tile(T, FFN_TM), FFN_TN
    ties = [] if tie is None else [tie]

    def body(do_ref, wd_ref, gu_ref, *rest):
        dgu_ref = rest[-1]
        do = do_ref[...]
        for c0, c1 in _col_chunks(tn):
            da = _dot(do, wd_ref[c0:c1, :], "nt")
            g = gu_ref[0, :, c0:c1].astype(F32)
            u = gu_ref[1, :, c0:c1].astype(F32)
            s = _sigmoid(g)
            dgu_ref[0, :, c0:c1] = (da * u * s * (1.0 + g * (1.0 - s))).astype(BF16)
            dgu_ref[1, :, c0:c1] = (da * g * s).astype(BF16)

    return pl.pallas_call(
        body, name=f"{tag}_da_actbwd", grid=(T // tm, D_FF // tn),
        in_specs=[pl.BlockSpec((tm, D_MODEL), lambda i, j: (i, 0)),
                  pl.BlockSpec((tn, D_MODEL), lambda i, j: (j, 0)),
                  pl.BlockSpec((2, tm, tn), lambda i, j: (0, i, j))]
        + [pl.BlockSpec(t.shape, lambda i, j: (0, 0)) for t in ties],
        out_specs=pl.BlockSpec((2, tm, tn), lambda i, j: (0, i, j)),
        out_shape=jax.ShapeDtypeStruct((2, T, D_FF), BF16),
        compiler_params=_cparams(("parallel", "parallel")),
    )(do, w_down, gu, *ties)


def _ffn_dwgu(ht, dgu, tag, tk=2048):
    T = ht.shape[1]
    tn, tk = FFN_TN, _tile(T, tk)
    nj, nk = D_FF // tn, T // tk

    def body(a_ref, b_ref, o_ref, acc_ref):
        k = pl.program_id(1)
        p = _dot(a_ref[...], b_ref[...])

        @pl.when(k == 0)
        def _():
            acc_ref[...] = p

        @pl.when(k > 0)
        def _():
            acc_ref[...] += p

        @pl.when(k == nk - 1)
        def _():
            o_ref[...] = acc_ref[...].astype(o_ref.dtype)

    return pl.pallas_call(
        body, name=f"{tag}_dwgu", grid=(2 * nj, nk),
        in_specs=[pl.BlockSpec((D_MODEL, tk), lambda n, k: (0, k)),
                  pl.BlockSpec((None, tk, tn), lambda n, k: (n // nj, k, n % nj))],
        out_specs=pl.BlockSpec((D_MODEL, tn), lambda n, k: (0, n)),
        out_shape=jax.ShapeDtypeStruct((D_MODEL, 2 * D_FF), BF16),
        scratch_shapes=[pltpu.VMEM((D_MODEL, tn), F32)],
        compiler_params=_cparams(("parallel", "arbitrary")),
    )(ht, dgu)


def _ffn_dh(dgu, w_gu, tag, tm=2048, tie=None):
    T = dgu.shape[1]
    tm, tk = _tile(T, tm), FFN_TN
    nk = D_FF // tk
    ties = [] if tie is None else [tie]

    def body(a_ref, b_ref, *rest):
        o_ref, acc_ref = rest[len(ties):]
        k = pl.program_id(1)
        p = _dot(a_ref[...], b_ref[...], "nt")

        @pl.when(k == 0)
        def _():
            acc_ref[...] = p

        @pl.when(k > 0)
        def _():
            acc_ref[...] += p

        @pl.when(k == 2 * nk - 1)
        def _():
            o_ref[...] = acc_ref[...].astype(o_ref.dtype)

    return pl.pallas_call(
        body, name=f"{tag}_dh", grid=(T // tm, 2 * nk),
        in_specs=[pl.BlockSpec((None, tm, tk), lambda i, k: (k // nk, i, k % nk)),
                  pl.BlockSpec((D_MODEL, tk), lambda i, k: (0, k))]
        + [pl.BlockSpec(t.shape, lambda i, k: (0, 0)) for t in ties],
        out_specs=pl.BlockSpec((tm, D_MODEL), lambda i, k: (i, 0)),
        out_shape=jax.ShapeDtypeStruct((T, D_MODEL), BF16),
        scratch_shapes=[pltpu.VMEM((tm, D_MODEL), F32)],
        compiler_params=_cparams(("parallel", "arbitrary")),
    )(dgu, w_gu, *ties)


def _ffn_fwd(h, w_gu, w_down, tag):
    gu, a, at = _ffn_gu_act(h, w_gu, tag)
    if callable(w_down):
        w_down = w_down(at)
    o = _mm(a, w_down, "nn", BF16, f"{tag}_down", tm=1024, tn=1024, tk=2816)
    return gu, at, o


def _ffn_bwd(do, ht, gu, at, w_gu, w_down, tag, tie=None, on_dw=None):
    on_dw = on_dw or (lambda which, dw: None)
    dw_down = _mm_t(at, do, f"{tag}_dwdown", tm=1408, tn=1024, tk=2048, tie=tie)
    dgu = _ffn_da_actbwd(do, w_down, gu, tag, tie=on_dw("down", dw_down))
    dw_gu = _ffn_dwgu(ht, dgu, tag)
    dh = _ffn_dh(dgu, w_gu, tag, tie=on_dw("gu", dw_gu))
    return dh, dw_gu, dw_down


def _sg_common(u_pre, v_pre, ln_g, ln_b):
    u = _gelu(u_pre)
    v = _gelu(v_pre)
    mu = jnp.mean(v, axis=-1, keepdims=True)
    vc = v - mu
    rstd = lax.rsqrt(jnp.mean(vc * vc, axis=-1, keepdims=True) + EPS)
    vhat = vc * rstd
    vl = vhat * ln_g + ln_b
    return u, vhat, rstd, vl


def _sg_masked_pairs(w):
    t = lax.broadcasted_iota(jnp.int32, (CHUNK, CHUNK), 0)
    s = lax.broadcasted_iota(jnp.int32, (CHUNK, CHUNK), 1)
    causal = s <= t
    wm = [jnp.where(causal, w[g], 0.0).astype(BF16) for g in range(SG_GROUPS)]
    return [jnp.concatenate([wm[2 * j], wm[2 * j + 1]], axis=0) for j in range(SG_GROUPS // 2)], causal


def _sg_mix(vl, pairs, bias):
    tr = vl.shape[0]
    low = lax.broadcasted_iota(jnp.int32, (CHUNK, LANES), 1) < SG_GROUP_DIM
    vb = vl.astype(BF16)
    rows = []
    for c in range(tr // CHUNK):
        slabs = []
        for j in range(SG_GROUPS // 2):
            slab = vb[c * CHUNK:(c + 1) * CHUNK, j * LANES:(j + 1) * LANES]
            m = _dot(pairs[j], slab)
            slabs.append(jnp.where(low, m[:CHUNK], m[CHUNK:]))
        rows.append(jnp.concatenate(slabs, axis=1) + bias)
    return jnp.concatenate(rows, axis=0)


def _sg_fwd(z, ln_g, ln_b, sg_w, bias_full):
    def fn(u_pre, v_pre, ln_g, ln_b, w, bias):
        u, _, _, vl = _sg_common(u_pre, v_pre, ln_g, ln_b)
        pairs, _ = _sg_masked_pairs(w)
        y = u * _sg_mix(vl, pairs, bias)
        return y, y

    return _rowwise(fn, "sg_fwd", 512, [(z, SG_WIDTH, Z_U // SG_WIDTH), (z, SG_WIDTH, Z_V // SG_WIDTH)],
                    [ln_g, ln_b, sg_w, bias_full], [(SG_WIDTH, BF16), (SG_WIDTH, BF16, "T")])


def _sg_bwd(z, dy, ln_g, ln_b, sg_w, bias_full, group_ind, dz):
    def fn(u_pre, v_pre, dy, ln_g, ln_b, w, bias, ind):
        dy = dy.astype(F32)
        u, vhat, rstd, vl = _sg_common(u_pre, v_pre, ln_g, ln_b)
        pairs, causal = _sg_masked_pairs(w)
        mixed = _sg_mix(vl, pairs, bias)
        du_pre = dy * mixed * _gelu_grad(u_pre)
        dmix = dy * u
        tr = dy.shape[0]
        low = lax.broadcasted_iota(jnp.int32, (CHUNK, LANES), 1) < SG_GROUP_DIM
        vb = vl.astype(BF16)
        dw = [jnp.zeros((CHUNK, CHUNK), F32) for _ in range(SG_GROUPS)]
        dbias = jnp.zeros((CHUNK, SG_WIDTH), F32)
        dvl_rows = []
        for c in range(tr // CHUNK):
            dm_c = dmix[c * CHUNK:(c + 1) * CHUNK]
            dbias = dbias + dm_c
            slabs = []
            for j in range(SG_GROUPS // 2):
                slab = vb[c * CHUNK:(c + 1) * CHUNK, j * LANES:(j + 1) * LANES]
                dm = dm_c[:, j * LANES:(j + 1) * LANES]
                d0 = jnp.where(low, dm, 0.0).astype(BF16)
                d1 = jnp.where(low, 0.0, dm).astype(BF16)
                dw[2 * j] = dw[2 * j] + _dot(d0, slab, "nt")
                dw[2 * j + 1] = dw[2 * j + 1] + _dot(d1, slab, "nt")
                slabs.append(_dot(pairs[j], jnp.concatenate([d0, d1], axis=0), "tn"))
            dvl_rows.append(jnp.concatenate(slabs, axis=1))
        dvl = jnp.concatenate(dvl_rows, axis=0)
        dln_g = _rsum(dvl * vhat)
        dln_b = _rsum(dvl)
        dvh = dvl * ln_g
        dv = rstd * (dvh - jnp.mean(dvh, axis=-1, keepdims=True)
                     - vhat * jnp.mean(dvh * vhat, axis=-1, keepdims=True))
        dv_pre = dv * _gelu_grad(v_pre)
        dw = jnp.stack([jnp.where(causal, d, 0.0) for d in dw], axis=0)
        dbias_t = lax.dot_general(dbias, ind, (((1,), (0,)), ((), ())), precision=lax.Precision.HIGHEST,
                                  preferred_element_type=F32)
        return jnp.concatenate([du_pre, dv_pre], axis=1), dw, dbias_t, dln_g, dln_b

    return _rowwise(fn, "sg_bwd", 512,
                    [(z, SG_WIDTH, Z_U // SG_WIDTH), (z, SG_WIDTH, Z_V // SG_WIDTH), dy],
                    [ln_g, ln_b, sg_w, bias_full, group_ind],
                    [("into", dz, 2 * SG_WIDTH, Z_U // (2 * SG_WIDTH))],
                    [((SG_GROUPS, CHUNK, CHUNK), F32), ((CHUNK, SG_GROUPS), F32), ((1, SG_WIDTH), F32), ((1, SG_WIDTH), F32)])


MLA_POST_ROWS = 512


def _rope(x, c, s1, s2):
    return x * c + pltpu.roll(x, LANES - MLA_ROPE // 2, 1) * s1 + pltpu.roll(x, MLA_ROPE // 2, 1) * s2


def _rope_t(d, c, s1, s2):
    return d * c + pltpu.roll(d * s1, MLA_ROPE // 2, 1) + pltpu.roll(d * s2, LANES - MLA_ROPE // 2, 1)


def _mla_post(q_pre, kv_pre, z, tabs, gq, gk):
    scale = MLA_QK ** -0.5 * LOG2E
    T = q_pre.shape[0]
    tr = _tile(T, MLA_POST_ROWS)

    def body(q_ref, k_ref, v_ref, kr_ref, c_ref, s1_ref, s2_ref, gq_ref, gk_ref, qo_ref, ko_ref, vo_ref):
        kr = kr_ref[...].astype(F32)
        c, s1, s2, gq, gk = c_ref[...], s1_ref[...], s2_ref[...], gq_ref[...], gk_ref[...]
        ones_lane = lax.broadcasted_iota(jnp.int32, (tr, LANES), 1) == ONES_LANE
        for h in range(MLA_HEADS):
            sl = slice(h * LANES, (h + 1) * LANES)
            qo_ref[:, sl] = (_rope(_rms(q_ref[:, sl].astype(F32), gq, MLA_QK), c, s1, s2) * scale).astype(BF16)
            ko_ref[:, sl] = _rope(_rms(k_ref[:, sl].astype(F32) + kr, gk, MLA_QK), c, s1, s2).astype(BF16)
            vo_ref[:, sl] = jnp.where(ones_lane, 1.0, v_ref[:, sl].astype(F32)).astype(BF16)

    wide = lambda cb: pl.BlockSpec((tr, HP), lambda i, cb=cb: (i, cb))
    lanes = lambda cb: pl.BlockSpec((tr, LANES), lambda i, cb=cb: (i, cb))
    gain = pl.BlockSpec((1, LANES), lambda i: (0, 0))
    return pl.pallas_call(
        body, name="mla_post", grid=(T // tr,),
        in_specs=[wide(0), wide(0), wide(1), lanes(Z_KR // LANES), lanes(0), lanes(0), lanes(0), gain, gain],
        out_specs=[wide(0)] * 3, out_shape=[jax.ShapeDtypeStruct((T, HP), BF16)] * 3,
        compiler_params=_cparams(("parallel",)),
    )(q_pre, kv_pre, kv_pre, z, *tabs, gq, gk)


def _mla_post_bwd(q_pre, kv_pre, z, tabs, gq, gk, dq, dk, dv):
    scale = MLA_QK ** -0.5
    T = q_pre.shape[0]
    tr = _tile(T, MLA_POST_ROWS)

    def body(q_ref, k_ref, kr_ref, c_ref, s1_ref, s2_ref, dq_ref, dk_ref, dv_ref, gq_ref, gk_ref,
             dqo_ref, dkvo_ref, dkro_ref, dgq_ref, dgk_ref):
        kr = kr_ref[...].astype(F32)
        c, s1, s2, gq, gk = c_ref[...], s1_ref[...], s2_ref[...], gq_ref[...], gk_ref[...]
        lane = lax.broadcasted_iota(jnp.int32, (1, LANES), 1)
        kr_mask = (lane >= KR_LANE) & (lane < KR_LANE + MLA_ROPE)
        dgq = jnp.zeros((1, LANES), F32)
        dgk = jnp.zeros((1, LANES), F32)
        dkr = jnp.zeros((tr, LANES), F32)
        for h in range(MLA_HEADS):
            sl = slice(h * LANES, (h + 1) * LANES)
            dqn = _rope_t(dq_ref[:, sl].astype(F32), c, s1, s2) * scale
            dx, dg = _rms_bwd(q_ref[:, sl].astype(F32), gq, dqn, MLA_QK)
            dqo_ref[:, sl] = dx.astype(BF16)
            dgq = dgq + dg
            dkn = _rope_t(dk_ref[:, sl].astype(F32), c, s1, s2)
            dx, dg = _rms_bwd(k_ref[:, sl].astype(F32) + kr, gk, dkn, MLA_QK)
            dkvo_ref[:, sl] = dx.astype(BF16)
            dkvo_ref[:, HP + h * LANES:HP + (h + 1) * LANES] = dv_ref[:, sl]
            dgk = dgk + dg
            dkr = dkr + dx
        dkro_ref[...] = jnp.where(kr_mask, dkr, 0.0).astype(BF16)
        i = pl.program_id(0)

        @pl.when(i == 0)
        def _():
            dgq_ref[...] = dgq
            dgk_ref[...] = dgk

        @pl.when(i > 0)
        def _():
            dgq_ref[...] += dgq
            dgk_ref[...] += dgk

    wide = lambda cb: pl.BlockSpec((tr, HP), lambda i, cb=cb: (i, cb))
    lanes = lambda cb: pl.BlockSpec((tr, LANES), lambda i, cb=cb: (i, cb))
    gain = pl.BlockSpec((1, LANES), lambda i: (0, 0))
    return pl.pallas_call(
        body, name="mla_post_bwd", grid=(T // tr,),
        in_specs=[wide(0), wide(0), lanes(Z_KR // LANES), lanes(0), lanes(0), lanes(0), wide(0), wide(0), wide(0),
                  gain, gain],
        out_specs=[wide(0), pl.BlockSpec((tr, 2 * HP), lambda i: (i, 0)), lanes(0), gain, gain],
        out_shape=[jax.ShapeDtypeStruct((T, HP), BF16), jax.ShapeDtypeStruct((T, 2 * HP), BF16),
                   jax.ShapeDtypeStruct((T, LANES), BF16), jax.ShapeDtypeStruct((1, LANES), F32),
                   jax.ShapeDtypeStruct((1, LANES), F32)],
        compiler_params=_cparams(("arbitrary",)),
    )(q_pre, kv_pre, z, *tabs, dq, dk, dv, gq, gk)


def _pairs(n, lower):
    a, b = [], []
    for o in range(n):
        inner = range(o + 1) if lower else range(o, n)
        for t in inner:
            a.append(o)
            b.append(t)
    return jnp.asarray(np.array(a, np.int32)), jnp.asarray(np.array(b, np.int32))


FLASH_TILE, FLASH_SUB_ROWS = 2048, 512
LOG2E, LN2 = 1.4426950408889634, 0.6931471805599453
ONES_LANE = MLA_V


def _flash_tiles(T):
    tq = _tile(T, FLASH_TILE)
    return tq, _tile(tq, FLASH_SUB_ROWS)


def _col_span(t, sr, rb, diag, key_major):
    if not diag:
        return 0, t
    return (rb * sr, t) if key_major else (0, (rb + 1) * sr)


def _span_iota(sr, rb, c0, c1):
    r = lax.broadcasted_iota(jnp.int32, (sr, c1 - c0), 0) + rb * sr
    c = lax.broadcasted_iota(jnp.int32, (sr, c1 - c0), 1) + c0
    return r, c


def _lanes(x, width):
    return jnp.concatenate([x] * (width // LANES), axis=1)


def _flash_fwd(q, k, v):
    T = q.shape[0]
    tq, sr = _flash_tiles(T)
    n = T // tq
    ii, jj = _pairs(n, True)

    def body(ii_ref, jj_ref, q_ref, k_ref, v_ref, o_ref, ot_ref, lse_ref, lset_ref, m_sc, acc_sc):
        p_ = pl.program_id(1)
        i, j = ii_ref[p_], jj_ref[p_]

        @pl.when(j == 0)
        def _():
            m_sc[...] = jnp.full(m_sc.shape, NEG, F32)
            acc_sc[...] = jnp.zeros(acc_sc.shape, F32)

        def tile(diag):
            nrb = tq // sr

            def scores(rb):
                c0, c1 = _col_span(tq, sr, rb, diag, False)
                return _dot(q_ref[rb * sr:(rb + 1) * sr, :], k_ref[c0:c1, :], "nt")

            s_next = scores(0)
            for rb in range(nrb):
                rows = slice(rb * sr, (rb + 1) * sr)
                c0, c1 = _col_span(tq, sr, rb, diag, False)
                s, s_next = s_next, (scores(rb + 1) if rb + 1 < nrb else None)
                if diag:
                    r, c = _span_iota(sr, rb, c0, c1)
                    s = jnp.where(c <= r, s, NEG)
                m = m_sc[rows, :]
                m_new = jnp.maximum(m, jnp.max(s, axis=1, keepdims=True))
                p = jnp.exp2(s - _lanes(m_new, c1 - c0))
                acc_sc[rows, :] = jnp.exp2(m - m_new) * acc_sc[rows, :] + _dot(p, v_ref[c0:c1, :])
                m_sc[rows, :] = m_new

        @pl.when(j < i)
        def _():
            tile(False)

        @pl.when(j == i)
        def _():
            tile(True)
            acc = acc_sc[...]
            lane = lax.broadcasted_iota(jnp.int32, acc.shape, 1)
            l = jnp.sum(jnp.where(lane == ONES_LANE, acc, 0.0), axis=1, keepdims=True)
            o = jnp.where(lane < MLA_V, acc / l, 0.0)
            o_ref[...] = o.astype(o_ref.dtype)
            ot_ref[...] = o.T.astype(ot_ref.dtype)
            lse = m_sc[...] + jnp.log2(l)
            lse_ref[...] = lse
            lset_ref[...] = lse.T[:8]

    blk = lambda which: pl.BlockSpec((tq, LANES), which)
    qmap = lambda h, p, ii, jj: (ii[p], h)
    kmap = lambda h, p, ii, jj: (jj[p], h)
    tmap = lambda h, p, ii, jj: (h, ii[p])
    return pl.pallas_call(
        body, name="mla_flash_fwd",
        grid_spec=pltpu.PrefetchScalarGridSpec(
            num_scalar_prefetch=2, grid=(MLA_HEADS, int(ii.shape[0])),
            in_specs=[blk(qmap), blk(kmap), blk(kmap)],
            out_specs=[blk(qmap), pl.BlockSpec((LANES, tq), tmap), blk(qmap), pl.BlockSpec((8, tq), tmap)],
            scratch_shapes=[pltpu.VMEM((tq, LANES), F32)] * 2),
        out_shape=[jax.ShapeDtypeStruct((T, HP), BF16), jax.ShapeDtypeStruct((HP, T), BF16),
                   jax.ShapeDtypeStruct((T, HP), F32), jax.ShapeDtypeStruct((8 * MLA_HEADS, T), F32)],
        compiler_params=_cparams(("parallel", "arbitrary")),
    )(ii, jj, q, k, v)


def _flash_dq(q, k, v, do, lse, delta):
    T = q.shape[0]
    tq, sr = _flash_tiles(T)
    n = T // tq
    ii, jj = _pairs(n, True)

    def body(ii_ref, jj_ref, q_ref, k_ref, v_ref, do_ref, lse_ref, dl_ref, dq_ref, acc_sc):
        p_ = pl.program_id(1)
        i, j = ii_ref[p_], jj_ref[p_]

        @pl.when(j == 0)
        def _():
            acc_sc[...] = jnp.zeros(acc_sc.shape, F32)

        def tile(diag):
            nrb = tq // sr

            def products(rb):
                rows = slice(rb * sr, (rb + 1) * sr)
                c0, c1 = _col_span(tq, sr, rb, diag, False)
                return _dot(q_ref[rows, :], k_ref[c0:c1, :], "nt"), _dot(do_ref[rows, :], v_ref[c0:c1, :], "nt")

            nxt = products(0)
            for rb in range(nrb):
                rows = slice(rb * sr, (rb + 1) * sr)
                c0, c1 = _col_span(tq, sr, rb, diag, False)
                (s, dp), nxt = nxt, (products(rb + 1) if rb + 1 < nrb else None)
                p = jnp.exp2(s - _lanes(lse_ref[rows, :], c1 - c0))
                if diag:
                    r, c = _span_iota(sr, rb, c0, c1)
                    p = jnp.where(c <= r, p, 0.0)
                acc_sc[rows, :] += _dot(p * (dp - _lanes(dl_ref[rows, :], c1 - c0)), k_ref[c0:c1, :])

        @pl.when(j < i)
        def _():
            tile(False)

        @pl.when(j == i)
        def _():
            tile(True)
            dq_ref[...] = acc_sc[...].astype(dq_ref.dtype)

    blk = lambda which: pl.BlockSpec((tq, LANES), which)
    qmap = lambda h, p, ii, jj: (ii[p], h)
    kmap = lambda h, p, ii, jj: (jj[p], h)
    return pl.pallas_call(
        body, name="mla_flash_dq",
        grid_spec=pltpu.PrefetchScalarGridSpec(
            num_scalar_prefetch=2, grid=(MLA_HEADS, int(ii.shape[0])),
            in_specs=[blk(qmap), blk(kmap), blk(kmap), blk(qmap), blk(qmap), blk(qmap)],
            out_specs=blk(qmap),
            scratch_shapes=[pltpu.VMEM((tq, LANES), F32)]),
        out_shape=jax.ShapeDtypeStruct((T, HP), BF16),
        compiler_params=_cparams(("parallel", "arbitrary")),
    )(ii, jj, q, k, v, do, lse, delta)


def _flash_dkv(q, k, v, do, lse_t, delta_t):
    T = q.shape[0]
    tq, sr = _flash_tiles(T)
    n = T // tq
    jj, ii = _pairs(n, False)

    def body(jj_ref, ii_ref, q_ref, k_ref, v_ref, do_ref, lse_ref, dl_ref, dk_ref, dv_ref, dk_sc, dv_sc):
        p_ = pl.program_id(1)
        j, i = jj_ref[p_], ii_ref[p_]

        @pl.when(i == j)
        def _():
            dk_sc[...] = jnp.zeros(dk_sc.shape, F32)
            dv_sc[...] = jnp.zeros(dv_sc.shape, F32)

        def tile(diag):
            nrb = tq // sr

            def products(rb):
                rows = slice(rb * sr, (rb + 1) * sr)
                c0, c1 = _col_span(tq, sr, rb, diag, True)
                return _dot(k_ref[rows, :], q_ref[c0:c1, :], "nt"), _dot(v_ref[rows, :], do_ref[c0:c1, :], "nt")

            nxt = products(0)
            for rb in range(nrb):
                rows = slice(rb * sr, (rb + 1) * sr)
                c0, c1 = _col_span(tq, sr, rb, diag, True)
                (st, dpt), nxt = nxt, (products(rb + 1) if rb + 1 < nrb else None)
                pt = jnp.exp2(st - lse_ref[:1, c0:c1])
                if diag:
                    r, c = _span_iota(sr, rb, c0, c1)
                    pt = jnp.where(r <= c, pt, 0.0)
                dv_sc[rows, :] += _dot(pt, do_ref[c0:c1, :])
                dk_sc[rows, :] += _dot(pt * (dpt - dl_ref[:1, c0:c1]), q_ref[c0:c1, :])

        @pl.when(i == j)
        def _():
            tile(True)

        @pl.when(i > j)
        def _():
            tile(False)

        @pl.when(i == n - 1)
        def _():
            dk_ref[...] = (dk_sc[...] * LN2).astype(dk_ref.dtype)
            dv_ref[...] = dv_sc[...].astype(dv_ref.dtype)

    blk = lambda which: pl.BlockSpec((tq, LANES), which)
    qmap = lambda h, p, jj, ii: (ii[p], h)
    kmap = lambda h, p, jj, ii: (jj[p], h)
    lse_rows = pl.BlockSpec((8, tq), lambda h, p, jj, ii: (h, ii[p]))
    delta_rows = pl.BlockSpec((8, tq), lambda h, p, jj, ii: (h * (LANES // 8), ii[p]))
    return pl.pallas_call(
        body, name="mla_flash_dkv",
        grid_spec=pltpu.PrefetchScalarGridSpec(
            num_scalar_prefetch=2, grid=(MLA_HEADS, int(ii.shape[0])),
            in_specs=[blk(qmap), blk(kmap), blk(kmap), blk(qmap), lse_rows, delta_rows],
            out_specs=[blk(kmap), blk(kmap)],
            scratch_shapes=[pltpu.VMEM((tq, LANES), F32)] * 2),
        out_shape=[jax.ShapeDtypeStruct((T, HP), BF16)] * 2,
        compiler_params=_cparams(("parallel", "arbitrary")),
    )(jj, ii, q, k, v, do, lse_t, delta_t)


def _mem_fwd(z, km, vm, gq):
    scale = MEM_HEAD_DIM ** -0.5

    def fn(qm, km, vm, gq):
        ys = []
        for h in range(MEM_HEADS):
            sl = slice(h * LANES, (h + 1) * LANES)
            q = _rms(qm[:, sl], gq) * scale
            s = _dot(q, km[:, sl], "nt")
            p = jnp.exp(s - jnp.max(s, axis=1, keepdims=True))
            p = p / jnp.sum(p, axis=1, keepdims=True)
            ys.append(_dot(p, vm[:, sl]))
        y = jnp.concatenate(ys, axis=1)
        return y, y

    return _rowwise(fn, "mem_fwd", 512, [(z, MEM_WIDTH, Z_QM // MEM_WIDTH)], [km, vm, gq],
                    [(MEM_WIDTH, BF16), (MEM_WIDTH, BF16, "T")])


def _mem_bwd(z, dy, km, vm, gq, dz):
    scale = MEM_HEAD_DIM ** -0.5

    def fn(qm, dy, km, vm, gq):
        dqs, dks, dvs = [], [], []
        dgq = jnp.zeros((1, LANES), F32)
        for h in range(MEM_HEADS):
            sl = slice(h * LANES, (h + 1) * LANES)
            q = (_rms(qm[:, sl], gq) * scale).astype(BF16)
            dyh = dy[:, sl]
            kh, vh = km[:, sl], vm[:, sl]
            s = _dot(q, kh, "nt")
            p = jnp.exp(s - jnp.max(s, axis=1, keepdims=True))
            p = p / jnp.sum(p, axis=1, keepdims=True)
            dp = _dot(dyh, vh, "nt")
            ds = p * (dp - jnp.sum(p * dp, axis=1, keepdims=True))
            dq = _dot(ds, kh) * scale
            dx, dg = _rms_bwd(qm[:, sl], gq, dq)
            dqs.append(dx)
            dgq = dgq + dg
            st = _dot(kh, q, "nt")
            pt = jnp.exp(st - jnp.max(st, axis=0, keepdims=True))
            pt = pt / jnp.sum(pt, axis=0, keepdims=True)
            dpt = _dot(vh, dyh, "nt")
            dst = pt * (dpt - jnp.sum(pt * dpt, axis=0, keepdims=True))
            dvs.append(_dot(pt, dyh))
            dks.append(_dot(dst, q))
        return jnp.concatenate(dqs, axis=1), jnp.concatenate(dks, axis=1), jnp.concatenate(dvs, axis=1), dgq

    m = km.shape[0]
    return _rowwise(fn, "mem_bwd", 512, [(z, MEM_WIDTH, Z_QM // MEM_WIDTH), dy], [km, vm, gq],
                    [("into", dz, MEM_WIDTH, Z_QM // MEM_WIDTH)],
                    [((m, MEM_WIDTH), F32), ((m, MEM_WIDTH), F32), ((1, LANES), F32)])


GROUPS = {"ffn1": ["ffn1_w_gu"], "ffn1_down": ["ffn1_w_down"],
          "mix": ["w_in", "mla_w_uq", "mla_w_ukv", "mem_w_kv", "w_branch_a", "w_branch_b", "w_branch_c", "w_out"],
          "ffn2": ["ffn2_w_gu", "ffn2_w_down"]}
GRAD_GROUPS = {"ffn2": GROUPS["ffn2"], "mix": GROUPS["mix"], "ffn1_down": ["ffn1_w_down"], "ffn1_gu": ["ffn1_w_gu"]}


def _local_step(x, mem, positions, loss_target, P, weights, grads_out):
    T = x.shape[0]
    G = {}
    W = dict(weights("ffn1", None))

    half = MLA_ROPE // 2
    inv = ROPE_BASE ** (-jnp.arange(half, dtype=F32) / half)
    ang = positions.astype(F32)[:, None] * inv
    cos, sin = jnp.cos(ang), jnp.sin(ang)
    one, zero = jnp.ones((T, MLA_NOPE), F32), jnp.zeros((T, half), F32)
    pad = LANES - MLA_QK
    tabs = (jnp.concatenate([one, cos, cos, jnp.ones((T, pad), F32)], axis=1),
            jnp.concatenate([jnp.zeros((T, MLA_NOPE), F32), -sin, zero, jnp.zeros((T, pad), F32)], axis=1),
            jnp.concatenate([jnp.zeros((T, MLA_NOPE), F32), zero, sin, jnp.zeros((T, pad), F32)], axis=1))
    gq_p = jnp.pad(P["mla_q_norm"], ((0, 0), (0, pad)))
    gk_p = jnp.pad(P["mla_k_norm"], ((0, 0), (0, pad)))
    bias_full = jnp.repeat(P["sg_b"].T, SG_GROUP_DIM, axis=1)
    group_ind = jnp.repeat(jnp.eye(SG_GROUPS, dtype=F32), SG_GROUP_DIM, axis=0)

    HT = (D_MODEL, BF16, "T")

    def norm2(x, g):
        h = _rms(x, g)
        return h, h

    h1, h1t = _rowwise(norm2, "ffn1_norm", 512, [x], [P["ffn1_norm"]], [(D_MODEL, BF16), HT])
    def ffn1_w_down(after):
        W.update(weights("ffn1_down", after))
        return W["ffn1_w_down"]

    gu1, a1t, o1 = _ffn_fwd(h1, W["ffn1_w_gu"], ffn1_w_down, "ffn1")

    def resid_norm(x, o, g):
        xn = x + 0.5 * o
        h = _rms(xn, g)
        return xn, h, h

    x1, hm, hmt = _rowwise(resid_norm, "mix_norm", 512, [x, o1], [P["mix_norm"]],
                           [(D_MODEL, F32), (D_MODEL, BF16), HT])
    W.update(weights("mix", hm))
    z = _mm(hm, W["w_in"], "nn", BF16, "w_in", tm=1024, tn=1792)

    y_a, y_at = _sg_fwd(z, P["sg_ln_g"], P["sg_ln_b"], P["sg_w"], bias_full)

    def c_norm(cq, ckv, gq, gkv):
        a, b = _rms(cq, gq), _rms(ckv, gkv)
        return a, b, a, b

    cqn, ckvn, cqnt, ckvnt = _rowwise(
        c_norm, "mla_cnorm", 1024, [(z, MLA_Q_RANK, Z_CQ // MLA_Q_RANK), (z, MLA_KV_RANK, Z_CKV // MLA_KV_RANK)],
        [P["mla_cq_norm"], P["mla_ckv_norm"]],
        [(MLA_Q_RANK, BF16), (MLA_KV_RANK, BF16), (MLA_Q_RANK, BF16, "T"), (MLA_KV_RANK, BF16, "T")])
    q_pre = _mm(cqn, W["mla_w_uq"], "nn", BF16, "mla_uq", tm=1024, tn=1024)
    kv_pre = _mm(ckvn, W["mla_w_ukv"], "nn", BF16, "mla_ukv", tm=1024, tn=1024)
    q, k, v = _mla_post(q_pre, kv_pre, z, tabs, gq_p, gk_p)
    y_b, y_bt, lse, lse_t = _flash_fwd(q, k, v)

    memn, = _rowwise(lambda m, g: _rms(m, g), "mem_norm", 256, [mem], [P["mem_norm"]], [(D_MODEL, BF16)])
    kvm = _mm(memn, W["mem_w_kv"], "nn", F32, "mem_kv")

    def mem_k(kvm, gk):
        ks = [_rms(kvm[:, h * LANES:(h + 1) * LANES], gk) for h in range(MEM_HEADS)]
        return jnp.concatenate(ks, axis=1), kvm[:, MEM_WIDTH:]

    km, vm = _rowwise(mem_k, "mem_knorm", 256, [kvm], [P["mem_k_norm"]], [(MEM_WIDTH, BF16), (MEM_WIDTH, BF16)])
    y_c, y_ct = _mem_fwd(z, km, vm, P["mem_q_norm"])

    pa = _mm(y_a, W["w_branch_a"], "nn", BF16, "branch_a", tm=1024, tn=1024)
    pb = _mm(y_b, W["w_branch_b"], "nn", BF16, "branch_b", tm=1024, tn=1024)
    pc = _mm(y_c, W["w_branch_c"], "nn", BF16, "branch_c", tm=1024, tn=1024)

    def merge(zg, pa, pb, pc, b):
        g = _sigmoid(zg + b)
        m = g[:, :D_MODEL] * pa + g[:, D_MODEL:2 * D_MODEL] * pb + g[:, 2 * D_MODEL:] * pc
        return m, m

    merged, mergedt = _rowwise(merge, "merge", 512, [(z, 3 * D_MODEL, 0), pa, pb, pc], [P["b_gate"]],
                               [(D_MODEL, BF16), HT])
    om = _mm(merged, W["w_out"], "nn", BF16, "w_out", tm=1024, tn=1024)

    def resid_norm1(x, o, g):
        xn = x + o
        h = _rms(xn, g)
        return xn, h, h

    x2, h2, h2t = _rowwise(resid_norm1, "ffn2_norm", 512, [x1, om], [P["ffn2_norm"]],
                           [(D_MODEL, F32), (D_MODEL, BF16), HT])
    W.update(weights("ffn2", h2))
    gu2, a2t, o2 = _ffn_fwd(h2, W["ffn2_w_gu"], W["ffn2_w_down"], "ffn2")

    def loss_fn(x2, o2, t):
        e = x2 + 0.5 * o2 - t
        return e * (1.0 / D_MODEL), (e * (0.5 / D_MODEL)).astype(BF16), _rsum(e * e) * (0.5 / D_MODEL)

    dx3, do2, loss_part = _rowwise(loss_fn, "loss", 512, [x2, o2, loss_target], [],
                                   [(D_MODEL, F32), (D_MODEL, BF16)], [((1, D_MODEL), F32)])

    dh2, G["ffn2_w_gu"], G["ffn2_w_down"] = _ffn_bwd(do2, h2t, gu2, a2t, W["ffn2_w_gu"], W["ffn2_w_down"], "ffn2")
    tie = grads_out("ffn2", G)

    def norm_bwd(x, dh, dxo, g, *_):
        dx, dg = _rms_bwd(x, g, dh)
        dx = dx + dxo
        return dx, dx, dg

    dx2, dx2b, G["ffn2_norm"] = _rowwise(norm_bwd, "ffn2_norm_bwd", 512, [x2, dh2, dx3],
                                         [P["ffn2_norm"]] + ([] if tie is None else [tie]),
                                         [(D_MODEL, F32), (D_MODEL, BF16)], [((1, D_MODEL), F32)])

    G["w_out"] = _mm_t(mergedt, dx2b, "w_out_dw", tm=1024, tn=1024)
    dmerged = _mm(dx2b, W["w_out"], "nt", BF16, "w_out_dx", tm=1024, tn=1024)

    def merge_bwd(zg, pa, pb, pc, dm, b):
        g = _sigmoid(zg + b)
        ps = jnp.concatenate([pa, pb, pc], axis=1)
        dm3 = jnp.concatenate([dm, dm, dm], axis=1)
        dzg = dm3 * ps * g * (1.0 - g)
        dp = dm3 * g
        return dzg, dp[:, :D_MODEL], dp[:, D_MODEL:2 * D_MODEL], dp[:, 2 * D_MODEL:], _rsum(dzg)

    dz = lax.empty((T, Z_COLS), BF16)
    dz, dpa, dpb, dpc, G["b_gate"] = _rowwise(
        merge_bwd, "merge_bwd", 256, [(z, 3 * D_MODEL, 0), pa, pb, pc, dmerged], [P["b_gate"]],
        [("into", dz, 3 * D_MODEL, 0), (D_MODEL, BF16), (D_MODEL, BF16), (D_MODEL, BF16)], [((1, 3 * D_MODEL), F32)])

    G["w_branch_a"] = _mm_t(y_at, dpa, "branch_a_dw", tm=512, tn=1024)
    G["w_branch_b"] = _mm_t(y_bt, dpb, "branch_b_dw", tm=1024, tn=1024)
    G["w_branch_c"] = _mm_t(y_ct, dpc, "branch_c_dw", tm=512, tn=1024)
    dy_a = _mm(dpa, W["w_branch_a"], "nt", BF16, "branch_a_dx", tm=1024, tn=512)
    dy_b = _mm(dpb, W["w_branch_b"], "nt", BF16, "branch_b_dx", tm=1024, tn=1024)
    dy_c = _mm(dpc, W["w_branch_c"], "nt", BF16, "branch_c_dx", tm=1024, tn=512)

    dz, G["sg_w"], dbias_t, G["sg_ln_g"], G["sg_ln_b"] = _sg_bwd(
        z, dy_a, P["sg_ln_g"], P["sg_ln_b"], P["sg_w"], bias_full, group_ind, dz)
    G["sg_b"] = dbias_t.T

    dz, dkm, dvm, G["mem_q_norm"] = _mem_bwd(z, dy_c, km, vm, P["mem_q_norm"], dz)

    def mem_k_bwd(kvm, dkm, dvm, gk):
        dks = []
        dg = jnp.zeros((1, LANES), F32)
        for h in range(MEM_HEADS):
            sl = slice(h * LANES, (h + 1) * LANES)
            dx, d = _rms_bwd(kvm[:, sl], gk, dkm[:, sl])
            dks.append(dx)
            dg = dg + d
        return jnp.concatenate(dks + [dvm], axis=1), dg

    dkvm, G["mem_k_norm"] = _rowwise(mem_k_bwd, "mem_knorm_bwd", 256, [kvm, dkm, dvm], [P["mem_k_norm"]],
                                     [(2 * MEM_WIDTH, BF16)], [((1, LANES), F32)])
    G["mem_w_kv"] = _mm(memn, dkvm, "tn", BF16, "mem_kv_dw")
    dmemn = _mm(dkvm, W["mem_w_kv"], "nt", F32, "mem_kv_dx")
    _, G["mem_norm"] = _rowwise(lambda m, d, g: _rms_bwd(m, g, d), "mem_norm_bwd", 256, [mem, dmemn],
                                [P["mem_norm"]], [(D_MODEL, BF16)], [((1, D_MODEL), F32)])

    def delta_fn(o, do):
        od = o.astype(F32) * do.astype(F32)
        ds = [jnp.broadcast_to(jnp.sum(od[:, h * LANES:(h + 1) * LANES], axis=1, keepdims=True), (od.shape[0], LANES))
              for h in range(MLA_HEADS)]
        d = jnp.concatenate(ds, axis=1)
        return d, d

    delta, delta_t = _rowwise(delta_fn, "mla_delta", 1024, [y_b, dy_b], [], [(HP, F32), (HP, F32, "T")])
    dq = _flash_dq(q, k, v, dy_b, lse, delta)
    dk, dv = _flash_dkv(q, k, v, dy_b, lse_t, delta_t)
    dq_pre, dkv_pre, dkr, dgq, dgk = _mla_post_bwd(q_pre, kv_pre, z, tabs, gq_p, gk_p, dq, dk, dv)
    G["mla_q_norm"], G["mla_k_norm"] = dgq[:, :MLA_QK], dgk[:, :MLA_QK]
    G["mla_w_uq"] = _mm_t(cqnt, dq_pre, "mla_uq_dw", tm=384, tn=1024)
    G["mla_w_ukv"] = _mm_t(ckvnt, dkv_pre, "mla_ukv_dw", tm=256, tn=2048)
    dcqn = _mm(dq_pre, W["mla_w_uq"], "nt", BF16, "mla_uq_dx", tm=1024)
    dckvn = _mm(dkv_pre, W["mla_w_ukv"], "nt", BF16, "mla_ukv_dx", tm=1024)

    def c_norm_bwd(cq, ckv, dcqn, dckvn, dkr, gq, gkv):
        dcq, dgq = _rms_bwd(cq, gq, dcqn)
        dckv, dgkv = _rms_bwd(ckv, gkv, dckvn)
        return jnp.concatenate([dckv, dkr, dcq], axis=1), dgq, dgkv

    tail = Z_COLS - Z_CKV
    dz, G["mla_cq_norm"], G["mla_ckv_norm"] = _rowwise(
        c_norm_bwd, "mla_cnorm_bwd", 1024,
        [(z, MLA_Q_RANK, Z_CQ // MLA_Q_RANK), (z, MLA_KV_RANK, Z_CKV // MLA_KV_RANK), dcqn, dckvn, dkr],
        [P["mla_cq_norm"], P["mla_ckv_norm"]], [("into", dz, tail, Z_CKV // tail)],
        [((1, MLA_Q_RANK), F32), ((1, MLA_KV_RANK), F32)])
    G["w_in"] = _mm_t(hmt, dz, "w_in_dw", tm=1024, tn=1792, tk=2048)
    dhm = _mm(dz, W["w_in"], "nt", BF16, "w_in_dx", tm=1024, tn=1024, tk=2688)

    def norm_bwd_half(x, dh, dxo, g):
        dx, dg = _rms_bwd(x, g, dh)
        dx = dx + dxo
        return dx, (0.5 * dx), dg

    dx1, do1, G["mix_norm"] = _rowwise(norm_bwd_half, "mix_norm_bwd", 512, [x1, dhm, dx2], [P["mix_norm"]],
                                       [(D_MODEL, F32), (D_MODEL, BF16)], [((1, D_MODEL), F32)])
    tie = grads_out("mix", G)

    def ffn1_dw(which, dw):
        G["ffn1_w_" + which] = dw
        return grads_out("ffn1_" + which, G)

    dh1, _, _ = _ffn_bwd(do1, h1t, gu1, a1t, W["ffn1_w_gu"], W["ffn1_w_down"], "ffn1", tie, ffn1_dw)

    def norm_bwd_last(x, dh, dxo, g):
        dx, dg = _rms_bwd(x, g, dh)
        return dx + dxo, dg

    grad_x, G["ffn1_norm"] = _rowwise(norm_bwd_last, "ffn1_norm_bwd", 512, [x, dh1, dx1], [P["ffn1_norm"]],
                                      [(D_MODEL, F32)], [((1, D_MODEL), F32)])
    return loss_part, grad_x, G


SHARDED = ["ffn1_w_gu", "ffn1_w_down", "w_in", "mla_w_uq", "mla_w_ukv", "mem_w_kv",
           "w_branch_a", "w_branch_b", "w_branch_c", "w_out", "ffn2_w_gu", "ffn2_w_down"]
ROW_SHARDED = {"ffn1_w_down", "mem_w_kv", "w_out", "ffn2_w_down"}
SMALL = ["ffn1_norm", "mix_norm", "b_gate", "sg_ln_g", "sg_ln_b", "sg_w", "sg_b", "mla_cq_norm",
         "mla_ckv_norm", "mla_q_norm", "mla_k_norm", "mem_norm", "mem_q_norm", "mem_k_norm", "ffn2_norm"]
ORDER = ["ffn1_norm", "ffn1_w_gu", "ffn1_w_down", "mix_norm", "w_in", "b_gate", "sg_ln_g", "sg_ln_b", "sg_w",
         "sg_b", "mla_cq_norm", "mla_w_uq", "mla_ckv_norm", "mla_w_ukv", "mla_q_norm", "mla_k_norm", "mem_norm",
         "mem_w_kv", "mem_q_norm", "mem_k_norm", "w_branch_a", "w_branch_b", "w_branch_c", "w_out", "ffn2_norm",
         "ffn2_w_gu", "ffn2_w_down"]

_IN_U, _IN_V, _IN_CQ, _IN_CKV, _IN_KR, _IN_QM, _IN_G = 0, 512, 1024, 1408, 1664, 1696, 2208
IN_COLS = 5280


def _full_from_slabs(name, slabs):
    n, r, c = slabs.shape
    if name in ROW_SHARDED:
        return slabs.reshape(n * r, c)
    return slabs.transpose(1, 0, 2).reshape(r, n * c)


def _slabs_from_full(name, full):
    if name in ROW_SHARDED:
        return full.reshape(N_DEV, full.shape[0] // N_DEV, full.shape[1])
    r, c = full.shape
    return full.reshape(r, N_DEV, c // N_DEV).transpose(1, 0, 2)


def _compute_layout(full):
    W = dict(full)
    if "w_in" not in full:
        return W
    w = full["w_in"]
    kr = jnp.pad(w[:, _IN_KR:_IN_QM], ((0, 0), (KR_LANE, LANES - KR_LANE - MLA_ROPE)))
    W["w_in"] = jnp.concatenate([w[:, _IN_G:], w[:, _IN_U:_IN_CQ], w[:, _IN_QM:_IN_G], w[:, _IN_CKV:_IN_KR], kr,
                                 w[:, _IN_CQ:_IN_CKV]], axis=1)
    uq = full["mla_w_uq"].reshape(MLA_Q_RANK, MLA_HEADS, MLA_QK)
    W["mla_w_uq"] = jnp.pad(uq, ((0, 0), (0, 0), (0, LANES - MLA_QK))).reshape(MLA_Q_RANK, HP)
    ukv = full["mla_w_ukv"].reshape(MLA_KV_RANK, MLA_HEADS, MLA_NOPE + MLA_V)
    padh = lambda a: jnp.pad(a, ((0, 0), (0, 0), (0, LANES - a.shape[2]))).reshape(MLA_KV_RANK, HP)
    W["mla_w_ukv"] = jnp.concatenate([padh(ukv[:, :, :MLA_NOPE]), padh(ukv[:, :, MLA_NOPE:])], axis=1)
    wb = full["w_branch_b"].reshape(MLA_HEADS, MLA_V, D_MODEL)
    W["w_branch_b"] = jnp.pad(wb, ((0, 0), (0, LANES - MLA_V), (0, 0))).reshape(HP, D_MODEL)
    return W


def _reference_layout(G):
    out = dict(G)
    if "w_in" not in G:
        return out
    g = G["w_in"]
    out["w_in"] = jnp.concatenate([
        g[:, Z_U:Z_QM], g[:, Z_CQ:Z_COLS], g[:, Z_CKV:Z_KR], g[:, Z_KR + KR_LANE:Z_KR + KR_LANE + MLA_ROPE],
        g[:, Z_QM:Z_CKV], g[:, Z_G:Z_U]], axis=1)
    out["mla_w_uq"] = G["mla_w_uq"].reshape(MLA_Q_RANK, MLA_HEADS, LANES)[:, :, :MLA_QK].reshape(MLA_Q_RANK, -1)
    gk = G["mla_w_ukv"][:, :HP].reshape(MLA_KV_RANK, MLA_HEADS, LANES)[:, :, :MLA_NOPE]
    gv = G["mla_w_ukv"][:, HP:].reshape(MLA_KV_RANK, MLA_HEADS, LANES)[:, :, :MLA_V]
    out["mla_w_ukv"] = jnp.concatenate([gk, gv], axis=2).reshape(MLA_KV_RANK, -1)
    out["w_branch_b"] = G["w_branch_b"].reshape(MLA_HEADS, LANES, D_MODEL)[:, :MLA_V].reshape(-1, D_MODEL)
    return out


def _pack(parts):
    flat = []
    for a in parts:
        a = a.reshape(-1)
        flat.append(jnp.pad(a, (0, (-a.shape[0]) % LANES)))
    return jnp.concatenate(flat).reshape(-1, LANES)


def _unpack(packed, shapes):
    flat = packed.reshape(-1)
    out, off = [], 0
    for shp in shapes:
        n = int(np.prod(shp))
        out.append(flat[off:off + n].reshape(shp))
        off += n + (-n) % LANES
    return out


MESH = pl.DeviceIdType.MESH
HBM = pl.BlockSpec(memory_space=pltpu.HBM)


def _all_gather(shards):
    n = len(shards)

    def body(*refs):
        x_refs, out_refs, token_ref = refs[:n], refs[n:2 * n], refs[2 * n]
        send_sems, recv_sems, local_sems = refs[2 * n + 1:]
        x, y, c = lax.axis_index("x"), lax.axis_index("y"), lax.axis_index("c")
        me, sibling = (x, y, c), (x, y, 1 - c)
        chips = [(1 - x, y), (x, 1 - y), (1 - x, 1 - y)]
        token_ref[...] = jnp.zeros_like(token_ref)

        def slot(a, px, py, pc):
            return out_refs[a].at[4 * px + 2 * py + pc]

        def copy(a, k, block, to, src=None):
            return pltpu.make_async_remote_copy(
                src_ref=slot(a, *block) if src is None else src, dst_ref=slot(a, *block),
                send_sem=send_sems.at[7 * a + k], recv_sem=recv_sems.at[7 * a + k], device_id=to, device_id_type=MESH)

        arrays = range(n)
        mine = [pltpu.make_async_copy(x_refs[a], slot(a, *me), local_sems.at[a]) for a in arrays]
        for cp in mine:
            cp.start()
        first = [copy(a, 0, me, sibling, src=x_refs[a]) for a in arrays]
        first += [copy(a, 1 + j, me, (*chip, c), src=x_refs[a]) for j, chip in enumerate(chips) for a in arrays]
        for cp in first:
            cp.start()
        passed = []
        for j, chip in enumerate(chips):
            for a in arrays:
                copy(a, 1 + j, (*chip, c), me).wait_recv()
                passed.append(copy(a, 4 + j, (*chip, c), sibling))
                passed[-1].start()
        for a in arrays:
            copy(a, 0, sibling, me).wait_recv()
        for j, chip in enumerate(chips):
            for a in arrays:
                copy(a, 4 + j, (*chip, 1 - c), me).wait_recv()
        for cp in first + passed:
            cp.wait_send()
        for cp in mine:
            cp.wait()

    res = pl.pallas_call(
        body, name="all_gather_weights",
        out_shape=[jax.ShapeDtypeStruct((N_DEV,) + s.shape, s.dtype) for s in shards]
        + [jax.ShapeDtypeStruct((8, LANES), F32)],
        in_specs=[HBM] * n, out_specs=[HBM] * n + [pl.BlockSpec(memory_space=pltpu.VMEM)],
        scratch_shapes=[pltpu.SemaphoreType.DMA((7 * n,)), pltpu.SemaphoreType.DMA((7 * n,)),
                        pltpu.SemaphoreType.DMA((n,))],
    )(*shards)
    return res[:n], res[n]


SEM = pl.BlockSpec(memory_space=pltpu.SEMAPHORE)
DATAFLOW = pltpu.SideEffectType.DATAFLOW_SIDE_EFFECTING


def _peers():
    x, y, c = lax.axis_index("x"), lax.axis_index("y"), lax.axis_index("c")
    out = []
    for k in range(1, N_DEV):
        px = 1 - x if k & 4 else x
        py = 1 - y if k & 2 else y
        pc = 1 - c if k & 1 else c
        out.append((k, (px, py, pc), 4 * px + 2 * py + pc))
    return 4 * x + 2 * y + c, out


def _send_start(srcs, per_peer, name):
    n = len(srcs)
    lands = [lax.empty((N_DEV,) + (s.shape[1:] if per_peer else s.shape), s.dtype) for s in srcs]

    def body(*refs):
        src_refs, land_refs, send_sems, recv_sems, token = refs[:n], refs[n:2 * n], refs[2 * n], refs[2 * n + 1], refs[-1]
        me, peers = _peers()
        for a in range(n):
            for k, pid, pflat in peers:
                pltpu.make_async_remote_copy(
                    src_ref=src_refs[a].at[pflat] if per_peer else src_refs[a], dst_ref=land_refs[a].at[me],
                    send_sem=send_sems.at[7 * a + k - 1], recv_sem=recv_sems.at[7 * a + k - 1],
                    device_id=pid, device_id_type=MESH).start()
        token[...] = jnp.zeros_like(token)

    hbm = lambda a: pltpu.with_memory_space_constraint(a, pltpu.HBM)
    res = pl.pallas_call(
        body, name=name,
        out_shape=(pltpu.SemaphoreType.DMA((7 * n,)), pltpu.SemaphoreType.DMA((7 * n,)),
                   *[pltpu.HBM(a.shape, a.dtype) for a in srcs + lands], jax.ShapeDtypeStruct((8, LANES), F32)),
        in_specs=(HBM,) * (2 * n), out_specs=(SEM, SEM) + (HBM,) * (2 * n) + (pl.BlockSpec(memory_space=pltpu.VMEM),),
        input_output_aliases={i: 2 + i for i in range(2 * n)},
        compiler_params=pltpu.CompilerParams(has_side_effects=DATAFLOW),
    )(*[hbm(a) for a in srcs + lands])
    return (res[0], res[1], list(res[2:2 + n]), list(res[2 + n:2 + 2 * n])), res[-1]


def _send_wait(started, after, per_peer, name):
    send_sems, recv_sems, srcs_thru, lands_thru = started
    n = len(srcs_thru)

    def body(*refs):
        src_refs, land_refs, send_sems, recv_sems = refs[:n], refs[n:2 * n], refs[2 * n], refs[2 * n + 1]
        me, peers = _peers()
        for a in range(n):
            for k, pid, pflat in peers:
                copy = pltpu.make_async_remote_copy(
                    src_ref=src_refs[a].at[pflat] if per_peer else src_refs[a], dst_ref=land_refs[a].at[pflat],
                    send_sem=send_sems.at[7 * a + k - 1], recv_sem=recv_sems.at[7 * a + k - 1],
                    device_id=pid, device_id_type=MESH)
                copy.wait_send()
                copy.wait_recv()

    outs = pl.pallas_call(
        body, name=name,
        out_shape=tuple(pltpu.HBM(a.shape, a.dtype) for a in srcs_thru + lands_thru),
        in_specs=(HBM,) * (2 * n) + (SEM, SEM, pl.BlockSpec(memory_space=pl.ANY)), out_specs=(HBM,) * (2 * n),
        input_output_aliases={i: i for i in range(2 * n)},
        compiler_params=pltpu.CompilerParams(has_side_effects=DATAFLOW),
    )(*srcs_thru, *lands_thru, send_sems, recv_sems, after)
    me = 4 * lax.axis_index("x") + 2 * lax.axis_index("y") + lax.axis_index("c")
    landed = []
    for src_out, land in zip(outs[:n], outs[n:]):
        own = lax.dynamic_index_in_dim(src_out, me, 0, keepdims=True) if per_peer else src_out[None]
        landed.append(lax.dynamic_update_slice(land, own, (me,) + (0,) * (land.ndim - 1)))
    return landed


def _share_rows(block, name):
    def body(src_ref, out_ref, send_sems, recv_sems, local_sem):
        me, peers = _peers()
        own = pltpu.make_async_copy(src_ref, out_ref.at[me], local_sem)
        own.start()
        copies = [pltpu.make_async_remote_copy(
            src_ref=src_ref, dst_ref=out_ref.at[me], send_sem=send_sems.at[k - 1], recv_sem=recv_sems.at[k - 1],
            device_id=pid, device_id_type=MESH) for k, pid, _ in peers]
        for cp in copies:
            cp.start()
        for cp in copies:
            cp.wait()
        own.wait()

    return pl.pallas_call(
        body, name=name, out_shape=jax.ShapeDtypeStruct((N_DEV,) + block.shape, block.dtype),
        in_specs=[HBM], out_specs=HBM,
        scratch_shapes=[pltpu.SemaphoreType.DMA((N_DEV - 1,)), pltpu.SemaphoreType.DMA((N_DEV - 1,)),
                        pltpu.SemaphoreType.DMA],
    )(block)


def _sum_slots(recv, name, tr):
    n, rows, lanes = recv.shape
    tr = _tile(rows, tr)

    def body(r_ref, o_ref):
        acc = r_ref[0].astype(F32)
        for i in range(1, n):
            acc = acc + r_ref[i].astype(F32)
        o_ref[...] = acc

    return pl.pallas_call(
        body, name=name, grid=(rows // tr,),
        in_specs=[pl.BlockSpec((n, tr, lanes), lambda i: (0, i, 0))],
        out_specs=pl.BlockSpec((tr, lanes), lambda i: (i, 0)),
        out_shape=jax.ShapeDtypeStruct((rows, lanes), F32),
        compiler_params=_cparams(("parallel",)),
    )(recv)


def _adamw_math(w, g, m, v):
    m = ADAM_B1 * m + (1.0 - ADAM_B1) * g
    v = ADAM_B2 * v + (1.0 - ADAM_B2) * (g * g)
    m_hat = m / (1.0 - ADAM_B1 ** ADAM_STEP)
    v_hat = v / (1.0 - ADAM_B2 ** ADAM_STEP)
    return -ADAM_LR * (m_hat / (jnp.sqrt(v_hat) + ADAM_EPS) + ADAM_WD * w), m, v


def _adamw(w, g, m, v, name, tr=256):
    return _rowwise(_adamw_math, name, tr, [w, g, m, v], [], [(w.shape[1], F32)] * 3)


def _adamw_small(ws, gs, ms, vs):
    n = len(ws)

    def body(*refs):
        ins, outs = refs[:4 * n], refs[4 * n:]
        for i in range(n):
            d, m, v = _adamw_math(ins[i][...], ins[n + i][...], ins[2 * n + i][...], ins[3 * n + i][...])
            outs[i][...], outs[n + i][...], outs[2 * n + i][...] = d, m, v

    vmem = pl.BlockSpec(memory_space=pltpu.VMEM)
    res = pl.pallas_call(
        body, name="adamw_small", in_specs=[vmem] * (4 * n), out_specs=[vmem] * (3 * n),
        out_shape=[jax.ShapeDtypeStruct(w.shape, F32) for w in ws] * 3,
    )(*ws, *gs, *ms, *vs)
    return res[:n], res[n:2 * n], res[2 * n:]


def _sum_adamw(recv, w, m, v, name):
    n, r, c = recv.shape
    tr = _tile(r, 256)

    def body(r_ref, w_ref, m_ref, v_ref, g_ref, d_ref, nm_ref, nv_ref):
        g = r_ref[0].astype(F32)
        for i in range(1, n):
            g = g + r_ref[i].astype(F32)
        g_ref[...] = g
        d_ref[...], nm_ref[...], nv_ref[...] = _adamw_math(w_ref[...], g, m_ref[...], v_ref[...])

    row = pl.BlockSpec((None, tr, c), lambda i: (0, i, 0))
    return pl.pallas_call(
        body, name=name, grid=(r // tr,),
        in_specs=[pl.BlockSpec((n, tr, c), lambda i: (0, i, 0)), row, row, row], out_specs=[row] * 4,
        out_shape=[jax.ShapeDtypeStruct((1, r, c), F32)] * 4, compiler_params=_cparams(("parallel",)),
    )(recv, w, m, v)


def kernel(x, mem, positions, ffn1_norm, ffn1_w_gu, ffn1_w_down, mix_norm, w_in, b_gate, sg_ln_g, sg_ln_b, sg_w, sg_b, mla_cq_norm, mla_w_uq, mla_ckv_norm, mla_w_ukv, mla_q_norm, mla_k_norm, mem_norm, mem_w_kv, mem_q_norm, mem_k_norm, w_branch_a, w_branch_b, w_branch_c, w_out, ffn2_norm, ffn2_w_gu, ffn2_w_down, loss_target, m_ffn1_norm, m_ffn1_w_gu, m_ffn1_w_down, m_mix_norm, m_w_in, m_b_gate, m_sg_ln_g, m_sg_ln_b, m_sg_w, m_sg_b, m_mla_cq_norm, m_mla_w_uq, m_mla_ckv_norm, m_mla_w_ukv, m_mla_q_norm, m_mla_k_norm, m_mem_norm, m_mem_w_kv, m_mem_q_norm, m_mem_k_norm, m_w_branch_a, m_w_branch_b, m_w_branch_c, m_w_out, m_ffn2_norm, m_ffn2_w_gu, m_ffn2_w_down, v_ffn1_norm, v_ffn1_w_gu, v_ffn1_w_down, v_mix_norm, v_w_in, v_b_gate, v_sg_ln_g, v_sg_ln_b, v_sg_w, v_sg_b, v_mla_cq_norm, v_mla_w_uq, v_mla_ckv_norm, v_mla_w_ukv, v_mla_q_norm, v_mla_k_norm, v_mem_norm, v_mem_w_kv, v_mem_q_norm, v_mem_k_norm, v_w_branch_a, v_w_branch_b, v_w_branch_c, v_w_out, v_ffn2_norm, v_ffn2_w_gu, v_ffn2_w_down):
    given = dict(locals())
    wts = {n: given[n] for n in ORDER}
    mom = {n: given["m_" + n] for n in ORDER}
    var = {n: given["v_" + n] for n in ORDER}

    def shards(group, zero):
        out = [wts[n][0].astype(BF16) for n in GROUPS[group]]
        return [out[0] + zero.astype(BF16)] + out[1:]

    def full_weights(group, slabs):
        return _compute_layout({n: _full_from_slabs(n, s) for n, s in zip(GROUPS[group], slabs)})

    def zero_of(a):
        return jnp.minimum(jnp.abs(a.reshape(-1)[0]), 0)

    gathered_ffn1, token = _all_gather([wts[n][0].astype(BF16) for n in GROUPS["ffn1"]])
    flight = {}
    flight["ffn1_down"], token = _send_start(shards("ffn1_down", token[0, 0]), False, "gather_ffn1_down_start")
    flight["mix"] = _send_start(shards("mix", token[0, 0]), False, "gather_mix_start")[0]
    recv = {}

    def weights(group, after):
        if group == "ffn1":
            return full_weights(group, gathered_ffn1)
        landed = _send_wait(flight.pop(group), after, False, f"gather_{group}_wait")
        if group == "mix":
            flight["ffn2"] = _send_start(shards("ffn2", zero_of(landed[0])), False, "gather_ffn2_start")[0]
        return full_weights(group, landed)

    small_shapes = [wts[n].shape[1:] for n in SMALL]
    early = SMALL[1:]
    assert SMALL[0] == "ffn1_norm"

    def grads_out(group, G):
        Gr = _reference_layout({n: G[n] for n in GRAD_GROUPS[group]})
        parts = [_slabs_from_full(n, Gr[n]).astype(BF16) for n in GRAD_GROUPS[group]]
        flight["g_" + group], tie = _send_start(parts, True, f"grads_{group}_start")
        if group == "mix":
            small = _pack([G[n].reshape(s) for n, s in zip(early, small_shapes[1:])])
            small = jnp.pad(small, ((0, (-small.shape[0]) % 8), (0, 0)))
            flight["small"], tie = _send_start([small + tie[0, 0]], False, "grads_small_start")
        return tie

    P = {n: wts[n] if wts[n].ndim == 2 else wts[n][0] for n in SMALL}
    loss_part, grad_x, G = _local_step(x[0], mem[0], positions[0], loss_target[0], P, weights, grads_out)

    for group, names in GRAD_GROUPS.items():
        recv.update(zip(names, _send_wait(flight.pop("g_" + group), grad_x, True, f"grads_{group}_wait")))
    early_recv, = _send_wait(flight.pop("small"), grad_x, False, "grads_small_wait")
    last = _share_rows(G["ffn1_norm"].reshape(-1, LANES), "share_ffn1_norm")
    g_small_packed = _sum_slots(jnp.concatenate([last, early_recv], axis=1), "sum_small", 2048)

    grads, delta, new_m, new_v = {}, {}, {}, {}
    for n in SHARDED:
        grads[n], delta[n], new_m[n], new_v[n] = _sum_adamw(recv[n], wts[n], mom[n], var[n], "adamw_" + n)
    grads.update(zip(SMALL, _unpack(g_small_packed, small_shapes)))

    flat2 = lambda d: [d[n].reshape(-1, d[n].shape[-1]) for n in SMALL]
    for dst, vals in zip((delta, new_m, new_v), _adamw_small(flat2(wts), flat2(grads), flat2(mom), flat2(var))):
        dst.update(zip(SMALL, vals))

    loss = lax.psum(jnp.sum(loss_part), ("x", "y", "c"))
    lead = lambda d: [d[n].reshape(wts[n].shape) for n in ORDER]
    return (loss, grad_x[None], *lead(grads), *lead(delta), *lead(new_m), *lead(new_v))
```

```python
import functools

import numpy as np
import jax
import jax.numpy as jnp
from jax import lax
from jax.experimental import pallas as pl
from jax.experimental.pallas import tpu as pltpu

F32, BF16 = jnp.float32, jnp.bfloat16

D_MODEL = 1024
SG_GROUPS, SG_GROUP_DIM, SG_WIDTH, CHUNK = 8, 64, 512, 128
MLA_HEADS, MLA_NOPE, MLA_ROPE, MLA_V, MLA_QK = 8, 64, 32, 64, 96
MLA_Q_RANK, MLA_KV_RANK = 384, 256
MEM_HEADS, MEM_HEAD_DIM, MEM_WIDTH = 4, 128, 512
D_FF = 2816
ROPE_BASE = 10000.0
EPS = 1e-6
NEG = -1e30
ADAM_LR, ADAM_B1, ADAM_B2, ADAM_EPS, ADAM_WD, ADAM_STEP = 0.001, 0.9, 0.999, 1e-08, 0.01, 10

N_DEV = 8
LANES = 128
V7X_VMEM_LIMIT = 56 * 1024 * 1024
HP = MLA_HEADS * LANES

Z_G, Z_U, Z_V, Z_QM, Z_CKV, Z_KR, Z_CQ = 0, 3072, 3584, 4096, 4608, 4864, 4992
Z_COLS = 5376
KR_LANE = 64


def _tile(dim, pref):
    if dim <= pref:
        return dim
    for t in range(pref - pref % LANES, LANES - 1, -LANES):
        if dim % t == 0:
            return t
    for t in range(pref - pref % 8, 7, -8):
        if dim % t == 0:
            return t
    return dim


def _cparams(sem):
    return pltpu.CompilerParams(dimension_semantics=sem, vmem_limit_bytes=V7X_VMEM_LIMIT)


_DN = {"nn": ((1,), (0,)), "nt": ((1,), (1,)), "tn": ((0,), (0,))}


def _dot(a, b, mode="nn"):
    return lax.dot_general(a.astype(BF16), b.astype(BF16), (_DN[mode], ((), ())),
                           preferred_element_type=F32)


def _mm(a, b, mode, out_dtype, name, tm=512, tn=512, tk=2048, tie=None):
    if mode == "tn":
        K, M = a.shape
    else:
        M, K = a.shape
    N = b.shape[0] if mode == "nt" else b.shape[1]
    tm, tn, tk = _tile(M, tm), _tile(N, tn), _tile(K, tk)
    nk = K // tk
    if mode == "tn":
        a_spec = pl.BlockSpec((tk, tm), lambda i, j, k: (k, i))
    else:
        a_spec = pl.BlockSpec((tm, tk), lambda i, j, k: (i, k))
    if mode == "nt":
        b_spec = pl.BlockSpec((tn, tk), lambda i, j, k: (j, k))
    else:
        b_spec = pl.BlockSpec((tk, tn), lambda i, j, k: (k, j))

    ties = [] if tie is None else [tie]

    def body(a_ref, b_ref, *rest):
        o_ref, *scratch = rest[len(ties):]
        p = _dot(a_ref[...], b_ref[...], mode)
        if nk == 1:
            o_ref[...] = p.astype(o_ref.dtype)
        else:
            acc_ref, = scratch
            k = pl.program_id(2)

            @pl.when(k == 0)
            def _():
                acc_ref[...] = p

            @pl.when(k > 0)
            def _():
                acc_ref[...] += p

            @pl.when(k == nk - 1)
            def _():
                o_ref[...] = acc_ref[...].astype(o_ref.dtype)

    return pl.pallas_call(
        body, name=name, grid=(M // tm, N // tn, nk),
        in_specs=[a_spec, b_spec] + [pl.BlockSpec(t.shape, lambda i, j, k: (0, 0)) for t in ties],
        out_specs=pl.BlockSpec((tm, tn), lambda i, j, k: (i, j)),
        out_shape=jax.ShapeDtypeStruct((M, N), out_dtype),
        scratch_shapes=[] if nk == 1 else [pltpu.VMEM((tm, tn), F32)],
        compiler_params=_cparams(("parallel", "parallel", "arbitrary")),
    )(a, b, *ties)


def _mm_t(at, b, name, tm, tn, tk=1024, tie=None):
    return _mm(at, b, "nn", BF16, name, tm=tm, tn=tn, tk=tk, tie=tie)


def _rowwise(fn, name, tr, row_ins, bc_ins, row_outs, acc_outs=()):
    norm = [it if isinstance(it, tuple) else (it, it.shape[1], 0) for it in row_ins]
    rows = norm[0][0].shape[0]
    tr = _tile(rows, tr)
    arrays, in_specs = [], []
    for arr, w, cb in norm:
        arrays.append(arr)
        in_specs.append(pl.BlockSpec((tr, w), lambda i, cb=cb: (i, cb)))
    for arr in bc_ins:
        arrays.append(arr)
        in_specs.append(pl.BlockSpec(arr.shape, lambda i, nd=arr.ndim: (0,) * nd))
    n_in, n_row = len(arrays), len(row_outs)
    out_shape, out_specs, aliases = [], [], {}
    transposed = [len(o) == 3 for o in row_outs]
    for k, o in enumerate(row_outs):
        if o[0] == "into":
            _, target, w, cb = o
            aliases[len(arrays)] = k
            arrays.append(target)
            in_specs.append(pl.BlockSpec(memory_space=pl.ANY))
            out_shape.append(jax.ShapeDtypeStruct(target.shape, target.dtype))
            out_specs.append(pl.BlockSpec((tr, w), lambda i, cb=cb: (i, cb)))
        elif transposed[k]:
            out_shape.append(jax.ShapeDtypeStruct((o[0], rows), o[1]))
            out_specs.append(pl.BlockSpec((o[0], tr), lambda i: (0, i)))
        else:
            out_shape.append(jax.ShapeDtypeStruct((rows, o[0]), o[1]))
            out_specs.append(pl.BlockSpec((tr, o[0]), lambda i: (i, 0)))
    for shp, dt in acc_outs:
        out_shape.append(jax.ShapeDtypeStruct(shp, dt))
        out_specs.append(pl.BlockSpec(shp, lambda i, nd=len(shp): (0,) * nd))

    def body(*refs):
        vals = fn(*[r[...].astype(F32) for r in refs[:n_in]])
        if not isinstance(vals, (tuple, list)):
            vals = (vals,)
        outs = refs[len(arrays):]
        for r, v, t in zip(outs[:n_row], vals[:n_row], transposed):
            r[...] = v.astype(F32).T.astype(r.dtype) if t else v.astype(r.dtype)
        if acc_outs:
            accs = list(zip(outs[n_row:], vals[n_row:]))
            i = pl.program_id(0)

            @pl.when(i == 0)
            def _():
                for r, v in accs:
                    r[...] = v.astype(r.dtype)

            @pl.when(i > 0)
            def _():
                for r, v in accs:
                    r[...] += v.astype(r.dtype)

    res = pl.pallas_call(
        body, name=name, grid=(rows // tr,), in_specs=in_specs, out_specs=out_specs,
        out_shape=out_shape, input_output_aliases=aliases, compiler_params=_cparams(("arbitrary",)),
    )(*arrays)
    return res


def _rsum(x):
    return jnp.sum(x, axis=0, keepdims=True)


def _rms(x, g, n=None):
    n = x.shape[-1] if n is None else n
    r = lax.rsqrt(jnp.sum(x * x, axis=-1, keepdims=True) * (1.0 / n) + EPS)
    return x * r * g


def _rms_bwd(x, g, dy, n=None):
    n = x.shape[-1] if n is None else n
    r = lax.rsqrt(jnp.sum(x * x, axis=-1, keepdims=True) * (1.0 / n) + EPS)
    xh = x * r
    dxh = dy * g
    dx = r * (dxh - xh * (jnp.sum(dxh * xh, axis=-1, keepdims=True) * (1.0 / n)))
    return dx, _rsum(dy * xh)


def _gelu(x):
    return 0.5 * x * (1.0 + lax.erf(x * 0.7071067811865476))


def _gelu_grad(x):
    return 0.5 * (1.0 + lax.erf(x * 0.7071067811865476)) + x * jnp.exp(-0.5 * x * x) * 0.3989422804014327


def _sigmoid(x):
    return 0.5 * jnp.tanh(0.5 * x) + 0.5


FFN_TM, FFN_TN = 1024, 1408
MXU_WIDTH = 256


def _col_chunks(n):
    return [(c, min(c + MXU_WIDTH, n)) for c in range(0, n, MXU_WIDTH)]


def _ffn_gu_act(h, w_gu, tag):
    T = h.shape[0]
    tm, tn = _tile(T, FFN_TM), FFN_TN
    nj = D_FF // tn
    steps = (T // tm) * nj
    ring = 3

    def body(h_ref, w_hbm, gu_ref, a_ref, at_ref, wg_buf, wu_buf, sem_g, sem_u):
        s = pl.program_id(0) * nj + pl.program_id(1)

        def fetch(t):
            slot, jt = lax.rem(t, ring), lax.rem(t, nj)
            cg = pl.multiple_of(jt * tn, LANES)
            cu = pl.multiple_of((jt + nj) * tn, LANES)
            return (pltpu.make_async_copy(w_hbm.at[:, pl.ds(cg, tn)], wg_buf.at[slot], sem_g.at[slot]),
                    pltpu.make_async_copy(w_hbm.at[:, pl.ds(cu, tn)], wu_buf.at[slot], sem_u.at[slot]))

        @pl.when(s == 0)
        def _():
            for t in range(min(ring - 1, steps)):
                for cp in fetch(jnp.int32(t)):
                    cp.start()

        @pl.when(s + ring - 1 < steps)
        def _():
            for cp in fetch(s + ring - 1):
                cp.start()

        for cp in fetch(s):
            cp.wait()
        slot = lax.rem(s, ring)
        h = h_ref[...]
        for c0, c1 in _col_chunks(tn):
            g = _dot(h, wg_buf[slot, :, c0:c1])
            u = _dot(h, wu_buf[slot, :, c0:c1])
            gu_ref[0, :, c0:c1] = g.astype(BF16)
            gu_ref[1, :, c0:c1] = u.astype(BF16)
            a = g * _sigmoid(g) * u
            a_ref[:, c0:c1] = a.astype(BF16)
            at_ref[c0:c1, :] = a.T.astype(BF16)

    return pl.pallas_call(
        body, name=f"{tag}_gu_act", grid=(T // tm, nj),
        in_specs=[pl.BlockSpec((tm, D_MODEL), lambda i, j: (i, 0)), pl.BlockSpec(memory_space=pl.ANY)],
        out_specs=[pl.BlockSpec((2, tm, tn), lambda i, j: (0, i, j)),
                   pl.BlockSpec((tm, tn), lambda i, j: (i, j)),
                   pl.BlockSpec((tn, tm), lambda i, j: (j, i))],
        out_shape=[jax.ShapeDtypeStruct((2, T, D_FF), BF16), jax.ShapeDtypeStruct((T, D_FF), BF16),
                   jax.ShapeDtypeStruct((D_FF, T), BF16)],
        scratch_shapes=[pltpu.VMEM((ring, D_MODEL, tn), BF16), pltpu.VMEM((ring, D_MODEL, tn), BF16),
                        pltpu.SemaphoreType.DMA((ring,)), pltpu.SemaphoreType.DMA((ring,))],
        compiler_params=_cparams(("arbitrary", "arbitrary")),
    )(h, w_gu)


def _ffn_da_actbwd(do, w_down, gu, tag, tie=None):
    T = do.shape[0]
    tm, tn = _tile(T, FFN_TM), FFN_TN
    ties = [] if tie is None else [tie]

    def body(do_ref, wd_ref, gu_ref, *rest):
        dgu_ref = rest[-1]
        do = do_ref[...]
        for c0, c1 in _col_chunks(tn):
            da = _dot(do, wd_ref[c0:c1, :], "nt")
            g = gu_ref[0, :, c0:c1].astype(F32)
            u = gu_ref[1, :, c0:c1].astype(F32)
            s = _sigmoid(g)
            dgu_ref[0, :, c0:c1] = (da * u * s * (1.0 + g * (1.0 - s))).astype(BF16)
            dgu_ref[1, :, c0:c1] = (da * g * s).astype(BF16)

    return pl.pallas_call(
        body, name=f"{tag}_da_actbwd", grid=(T // tm, D_FF // tn),
        in_specs=[pl.BlockSpec((tm, D_MODEL), lambda i, j: (i, 0)),
                  pl.BlockSpec((tn, D_MODEL), lambda i, j: (j, 0)),
                  pl.BlockSpec((2, tm, tn), lambda i, j: (0, i, j))]
        + [pl.BlockSpec(t.shape, lambda i, j: (0, 0)) for t in ties],
        out_specs=pl.BlockSpec((2, tm, tn), lambda i, j: (0, i, j)),
        out_shape=jax.ShapeDtypeStruct((2, T, D_FF), BF16),
        compiler_params=_cparams(("parallel", "parallel")),
    )(do, w_down, gu, *ties)


def _ffn_dwgu(ht, dgu, tag, tk=2048):
    T = ht.shape[1]
    tn, tk = FFN_TN, _tile(T, tk)
    nj, nk = D_FF // tn, T // tk

    def body(a_ref, b_ref, o_ref, acc_ref):
        k = pl.program_id(1)
        p = _dot(a_ref[...], b_ref[...])

        @pl.when(k == 0)
        def _():
            acc_ref[...] = p

        @pl.when(k > 0)
        def _():
            acc_ref[...] += p

        @pl.when(k == nk - 1)
        def _():
            o_ref[...] = acc_ref[...].astype(o_ref.dtype)

    return pl.pallas_call(
        body, name=f"{tag}_dwgu", grid=(2 * nj, nk),
        in_specs=[pl.BlockSpec((D_MODEL, tk), lambda n, k: (0, k)),
                  pl.BlockSpec((None, tk, tn), lambda n, k: (n // nj, k, n % nj))],
        out_specs=pl.BlockSpec((D_MODEL, tn), lambda n, k: (0, n)),
        out_shape=jax.ShapeDtypeStruct((D_MODEL, 2 * D_FF), BF16),
        scratch_shapes=[pltpu.VMEM((D_MODEL, tn), F32)],
        compiler_params=_cparams(("parallel", "arbitrary")),
    )(ht, dgu)


def _ffn_dh(dgu, w_gu, tag, tm=2048, tie=None):
    T = dgu.shape[1]
    tm, tk = _tile(T, tm), FFN_TN
    nk = D_FF // tk
    ties = [] if tie is None else [tie]

    def body(a_ref, b_ref, *rest):
        o_ref, acc_ref = rest[len(ties):]
        k = pl.program_id(1)
        p = _dot(a_ref[...], b_ref[...], "nt")

        @pl.when(k == 0)
        def _():
            acc_ref[...] = p

        @pl.when(k > 0)
        def _():
            acc_ref[...] += p

        @pl.when(k == 2 * nk - 1)
        def _():
            o_ref[...] = acc_ref[...].astype(o_ref.dtype)

    return pl.pallas_call(
        body, name=f"{tag}_dh", grid=(T // tm, 2 * nk),
        in_specs=[pl.BlockSpec((None, tm, tk), lambda i, k: (k // nk, i, k % nk)),
                  pl.BlockSpec((D_MODEL, tk), lambda i, k: (0, k))]
        + [pl.BlockSpec(t.shape, lambda i, k: (0, 0)) for t in ties],
        out_specs=pl.BlockSpec((tm, D_MODEL), lambda i, k: (i, 0)),
        out_shape=jax.ShapeDtypeStruct((T, D_MODEL), BF16),
        scratch_shapes=[pltpu.VMEM((tm, D_MODEL), F32)],
        compiler_params=_cparams(("parallel", "arbitrary")),
    )(dgu, w_gu, *ties)


def _ffn_fwd(h, w_gu, w_down, tag):
    gu, a, at = _ffn_gu_act(h, w_gu, tag)
    if callable(w_down):
        w_down = w_down(at)
    o = _mm(a, w_down, "nn", BF16, f"{tag}_down", tm=1024, tn=1024, tk=2816)
    return gu, at, o


def _ffn_bwd(do, ht, gu, at, w_gu, w_down, tag, tie=None, on_dw=None):
    on_dw = on_dw or (lambda which, dw: None)
    dw_down = _mm_t(at, do, f"{tag}_dwdown", tm=1408, tn=1024, tk=2048, tie=tie)
    dgu = _ffn_da_actbwd(do, w_down, gu, tag, tie=on_dw("down", dw_down))
    dw_gu = _ffn_dwgu(ht, dgu, tag)
    dh = _ffn_dh(dgu, w_gu, tag, tie=on_dw("gu", dw_gu))
    return dh, dw_gu, dw_down


def _sg_common(u_pre, v_pre, ln_g, ln_b):
    u = _gelu(u_pre)
    v = _gelu(v_pre)
    mu = jnp.mean(v, axis=-1, keepdims=True)
    vc = v - mu
    rstd = lax.rsqrt(jnp.mean(vc * vc, axis=-1, keepdims=True) + EPS)
    vhat = vc * rstd
    vl = vhat * ln_g + ln_b
    return u, vhat, rstd, vl


def _sg_masked_pairs(w):
    t = lax.broadcasted_iota(jnp.int32, (CHUNK, CHUNK), 0)
    s = lax.broadcasted_iota(jnp.int32, (CHUNK, CHUNK), 1)
    causal = s <= t
    wm = [jnp.where(causal, w[g], 0.0).astype(BF16) for g in range(SG_GROUPS)]
    return [jnp.concatenate([wm[2 * j], wm[2 * j + 1]], axis=0) for j in range(SG_GROUPS // 2)], causal


def _sg_mix(vl, pairs, bias):
    tr = vl.shape[0]
    low = lax.broadcasted_iota(jnp.int32, (CHUNK, LANES), 1) < SG_GROUP_DIM
    vb = vl.astype(BF16)
    rows = []
    for c in range(tr // CHUNK):
        slabs = []
        for j in range(SG_GROUPS // 2):
            slab = vb[c * CHUNK:(c + 1) * CHUNK, j * LANES:(j + 1) * LANES]
            m = _dot(pairs[j], slab)
            slabs.append(jnp.where(low, m[:CHUNK], m[CHUNK:]))
        rows.append(jnp.concatenate(slabs, axis=1) + bias)
    return jnp.concatenate(rows, axis=0)


def _sg_fwd(z, ln_g, ln_b, sg_w, bias_full):
    def fn(u_pre, v_pre, ln_g, ln_b, w, bias):
        u, _, _, vl = _sg_common(u_pre, v_pre, ln_g, ln_b)
        pairs, _ = _sg_masked_pairs(w)
        y = u * _sg_mix(vl, pairs, bias)
        return y, y

    return _rowwise(fn, "sg_fwd", 512, [(z, SG_WIDTH, Z_U // SG_WIDTH), (z, SG_WIDTH, Z_V // SG_WIDTH)],
                    [ln_g, ln_b, sg_w, bias_full], [(SG_WIDTH, BF16), (SG_WIDTH, BF16, "T")])


def _sg_bwd(z, dy, ln_g, ln_b, sg_w, bias_full, group_ind, dz):
    def fn(u_pre, v_pre, dy, ln_g, ln_b, w, bias, ind):
        dy = dy.astype(F32)
        u, vhat, rstd, vl = _sg_common(u_pre, v_pre, ln_g, ln_b)
        pairs, causal = _sg_masked_pairs(w)
        mixed = _sg_mix(vl, pairs, bias)
        du_pre = dy * mixed * _gelu_grad(u_pre)
        dmix = dy * u
        tr = dy.shape[0]
        low = lax.broadcasted_iota(jnp.int32, (CHUNK, LANES), 1) < SG_GROUP_DIM
        vb = vl.astype(BF16)
        dw = [jnp.zeros((CHUNK, CHUNK), F32) for _ in range(SG_GROUPS)]
        dbias = jnp.zeros((CHUNK, SG_WIDTH), F32)
        dvl_rows = []
        for c in range(tr // CHUNK):
            dm_c = dmix[c * CHUNK:(c + 1) * CHUNK]
            dbias = dbias + dm_c
            slabs = []
            for j in range(SG_GROUPS // 2):
                slab = vb[c * CHUNK:(c + 1) * CHUNK, j * LANES:(j + 1) * LANES]
                dm = dm_c[:, j * LANES:(j + 1) * LANES]
                d0 = jnp.where(low, dm, 0.0).astype(BF16)
                d1 = jnp.where(low, 0.0, dm).astype(BF16)
                dw[2 * j] = dw[2 * j] + _dot(d0, slab, "nt")
                dw[2 * j + 1] = dw[2 * j + 1] + _dot(d1, slab, "nt")
                slabs.append(_dot(pairs[j], jnp.concatenate([d0, d1], axis=0), "tn"))
            dvl_rows.append(jnp.concatenate(slabs, axis=1))
        dvl = jnp.concatenate(dvl_rows, axis=0)
        dln_g = _rsum(dvl * vhat)
        dln_b = _rsum(dvl)
        dvh = dvl * ln_g
        dv = rstd * (dvh - jnp.mean(dvh, axis=-1, keepdims=True)
                     - vhat * jnp.mean(dvh * vhat, axis=-1, keepdims=True))
        dv_pre = dv * _gelu_grad(v_pre)
        dw = jnp.stack([jnp.where(causal, d, 0.0) for d in dw], axis=0)
        dbias_t = lax.dot_general(dbias, ind, (((1,), (0,)), ((), ())), precision=lax.Precision.HIGHEST,
                                  preferred_element_type=F32)
        return jnp.concatenate([du_pre, dv_pre], axis=1), dw, dbias_t, dln_g, dln_b

    return _rowwise(fn, "sg_bwd", 512,
                    [(z, SG_WIDTH, Z_U // SG_WIDTH), (z, SG_WIDTH, Z_V // SG_WIDTH), dy],
                    [ln_g, ln_b, sg_w, bias_full, group_ind],
                    [("into", dz, 2 * SG_WIDTH, Z_U // (2 * SG_WIDTH))],
                    [((SG_GROUPS, CHUNK, CHUNK), F32), ((CHUNK, SG_GROUPS), F32), ((1, SG_WIDTH), F32), ((1, SG_WIDTH), F32)])


MLA_POST_ROWS = 512


def _rope(x, c, s1, s2):
    return x * c + pltpu.roll(x, LANES - MLA_ROPE // 2, 1) * s1 + pltpu.roll(x, MLA_ROPE // 2, 1) * s2


def _rope_t(d, c, s1, s2):
    return d * c + pltpu.roll(d * s1, MLA_ROPE // 2, 1) + pltpu.roll(d * s2, LANES - MLA_ROPE // 2, 1)


def _mla_post(q_pre, kv_pre, z, tabs, gq, gk):
    scale = MLA_QK ** -0.5 * LOG2E
    T = q_pre.shape[0]
    tr = _tile(T, MLA_POST_ROWS)

    def body(q_ref, k_ref, v_ref, kr_ref, c_ref, s1_ref, s2_ref, gq_ref, gk_ref, qo_ref, ko_ref, vo_ref):
        kr = kr_ref[...].astype(F32)
        c, s1, s2, gq, gk = c_ref[...], s1_ref[...], s2_ref[...], gq_ref[...], gk_ref[...]
        ones_lane = lax.broadcasted_iota(jnp.int32, (tr, LANES), 1) == ONES_LANE
        for h in range(MLA_HEADS):
            sl = slice(h * LANES, (h + 1) * LANES)
            qo_ref[:, sl] = (_rope(_rms(q_ref[:, sl].astype(F32), gq, MLA_QK), c, s1, s2) * scale).astype(BF16)
            ko_ref[:, sl] = _rope(_rms(k_ref[:, sl].astype(F32) + kr, gk, MLA_QK), c, s1, s2).astype(BF16)
            vo_ref[:, sl] = jnp.where(ones_lane, 1.0, v_ref[:, sl].astype(F32)).astype(BF16)

    wide = lambda cb: pl.BlockSpec((tr, HP), lambda i, cb=cb: (i, cb))
    lanes = lambda cb: pl.BlockSpec((tr, LANES), lambda i, cb=cb: (i, cb))
    gain = pl.BlockSpec((1, LANES), lambda i: (0, 0))
    return pl.pallas_call(
        body, name="mla_post", grid=(T // tr,),
        in_specs=[wide(0), wide(0), wide(1), lanes(Z_KR // LANES), lanes(0), lanes(0), lanes(0), gain, gain],
        out_specs=[wide(0)] * 3, out_shape=[jax.ShapeDtypeStruct((T, HP), BF16)] * 3,
        compiler_params=_cparams(("parallel",)),
    )(q_pre, kv_pre, kv_pre, z, *tabs, gq, gk)


def _mla_post_bwd(q_pre, kv_pre, z, tabs, gq, gk, dq, dk, dv):
    scale = MLA_QK ** -0.5
    T = q_pre.shape[0]
    tr = _tile(T, MLA_POST_ROWS)

    def body(q_ref, k_ref, kr_ref, c_ref, s1_ref, s2_ref, dq_ref, dk_ref, dv_ref, gq_ref, gk_ref,
             dqo_ref, dkvo_ref, dkro_ref, dgq_ref, dgk_ref):
        kr = kr_ref[...].astype(F32)
        c, s1, s2, gq, gk = c_ref[...], s1_ref[...], s2_ref[...], gq_ref[...], gk_ref[...]
        lane = lax.broadcasted_iota(jnp.int32, (1, LANES), 1)
        kr_mask = (lane >= KR_LANE) & (lane < KR_LANE + MLA_ROPE)
        dgq = jnp.zeros((1, LANES), F32)
        dgk = jnp.zeros((1, LANES), F32)
        dkr = jnp.zeros((tr, LANES), F32)
        for h in range(MLA_HEADS):
            sl = slice(h * LANES, (h + 1) * LANES)
            dqn = _rope_t(dq_ref[:, sl].astype(F32), c, s1, s2) * scale
            dx, dg = _rms_bwd(q_ref[:, sl].astype(F32), gq, dqn, MLA_QK)
            dqo_ref[:, sl] = dx.astype(BF16)
            dgq = dgq + dg
            dkn = _rope_t(dk_ref[:, sl].astype(F32), c, s1, s2)
            dx, dg = _rms_bwd(k_ref[:, sl].astype(F32) + kr, gk, dkn, MLA_QK)
            dkvo_ref[:, sl] = dx.astype(BF16)
            dkvo_ref[:, HP + h * LANES:HP + (h + 1) * LANES] = dv_ref[:, sl]
            dgk = dgk + dg
            dkr = dkr + dx
        dkro_ref[...] = jnp.where(kr_mask, dkr, 0.0).astype(BF16)
        i = pl.program_id(0)

        @pl.when(i == 0)
        def _():
            dgq_ref[...] = dgq
            dgk_ref[...] = dgk

        @pl.when(i > 0)
        def _():
            dgq_ref[...] += dgq
            dgk_ref[...] += dgk

    wide = lambda cb: pl.BlockSpec((tr, HP), lambda i, cb=cb: (i, cb))
    lanes = lambda cb: pl.BlockSpec((tr, LANES), lambda i, cb=cb: (i, cb))
    gain = pl.BlockSpec((1, LANES), lambda i: (0, 0))
    return pl.pallas_call(
        body, name="mla_post_bwd", grid=(T // tr,),
        in_specs=[wide(0), wide(0), lanes(Z_KR // LANES), lanes(0), lanes(0), lanes(0), wide(0), wide(0), wide(0),
                  gain, gain],
        out_specs=[wide(0), pl.BlockSpec((tr, 2 * HP), lambda i: (i, 0)), lanes(0), gain, gain],
        out_shape=[jax.ShapeDtypeStruct((T, HP), BF16), jax.ShapeDtypeStruct((T, 2 * HP), BF16),
                   jax.ShapeDtypeStruct((T, LANES), BF16), jax.ShapeDtypeStruct((1, LANES), F32),
                   jax.ShapeDtypeStruct((1, LANES), F32)],
        compiler_params=_cparams(("arbitrary",)),
    )(q_pre, kv_pre, z, *tabs, dq, dk, dv, gq, gk)


def _pairs(n, lower):
    a, b = [], []
    for o in range(n):
        inner = range(o + 1) if lower else range(o, n)
        for t in inner:
            a.append(o)
            b.append(t)
    return jnp.asarray(np.array(a, np.int32)), jnp.asarray(np.array(b, np.int32))


FLASH_TILE, FLASH_SUB_ROWS = 2048, 512
LOG2E, LN2 = 1.4426950408889634, 0.6931471805599453
ONES_LANE = MLA_V


def _flash_tiles(T):
    tq = _tile(T, FLASH_TILE)
    return tq, _tile(tq, FLASH_SUB_ROWS)


def _col_span(t, sr, rb, diag, key_major):
    if not diag:
        return 0, t
    return (rb * sr, t) if key_major else (0, (rb + 1) * sr)


def _span_iota(sr, rb, c0, c1):
    r = lax.broadcasted_iota(jnp.int32, (sr, c1 - c0), 0) + rb * sr
    c = lax.broadcasted_iota(jnp.int32, (sr, c1 - c0), 1) + c0
    return r, c


def _lanes(x, width):
    return jnp.concatenate([x] * (width // LANES), axis=1)


def _flash_fwd(q, k, v):
    T = q.shape[0]
    tq, sr = _flash_tiles(T)
    n = T // tq
    ii, jj = _pairs(n, True)

    def body(ii_ref, jj_ref, q_ref, k_ref, v_ref, o_ref, ot_ref, lse_ref, lset_ref, m_sc, acc_sc):
        p_ = pl.program_id(1)
        i, j = ii_ref[p_], jj_ref[p_]

        @pl.when(j == 0)
        def _():
            m_sc[...] = jnp.full(m_sc.shape, NEG, F32)
            acc_sc[...] = jnp.zeros(acc_sc.shape, F32)

        def tile(diag):
            nrb = tq // sr

            def scores(rb):
                c0, c1 = _col_span(tq, sr, rb, diag, False)
                return _dot(q_ref[rb * sr:(rb + 1) * sr, :], k_ref[c0:c1, :], "nt")

            s_next = scores(0)
            for rb in range(nrb):
                rows = slice(rb * sr, (rb + 1) * sr)
                c0, c1 = _col_span(tq, sr, rb, diag, False)
                s, s_next = s_next, (scores(rb + 1) if rb + 1 < nrb else None)
                if diag:
                    r, c = _span_iota(sr, rb, c0, c1)
                    s = jnp.where(c <= r, s, NEG)
                m = m_sc[rows, :]
                m_new = jnp.maximum(m, jnp.max(s, axis=1, keepdims=True))
                p = jnp.exp2(s - _lanes(m_new, c1 - c0))
                acc_sc[rows, :] = jnp.exp2(m - m_new) * acc_sc[rows, :] + _dot(p, v_ref[c0:c1, :])
                m_sc[rows, :] = m_new

        @pl.when(j < i)
        def _():
            tile(False)

        @pl.when(j == i)
        def _():
            tile(True)
            acc = acc_sc[...]
            lane = lax.broadcasted_iota(jnp.int32, acc.shape, 1)
            l = jnp.sum(jnp.where(lane == ONES_LANE, acc, 0.0), axis=1, keepdims=True)
            o = jnp.where(lane < MLA_V, acc / l, 0.0)
            o_ref[...] = o.astype(o_ref.dtype)
            ot_ref[...] = o.T.astype(ot_ref.dtype)
            lse = m_sc[...] + jnp.log2(l)
            lse_ref[...] = lse
            lset_ref[...] = lse.T[:8]

    blk = lambda which: pl.BlockSpec((tq, LANES), which)
    qmap = lambda h, p, ii, jj: (ii[p], h)
    kmap = lambda h, p, ii, jj: (jj[p], h)
    tmap = lambda h, p, ii, jj: (h, ii[p])
    return pl.pallas_call(
        body, name="mla_flash_fwd",
        grid_spec=pltpu.PrefetchScalarGridSpec(
            num_scalar_prefetch=2, grid=(MLA_HEADS, int(ii.shape[0])),
            in_specs=[blk(qmap), blk(kmap), blk(kmap)],
            out_specs=[blk(qmap), pl.BlockSpec((LANES, tq), tmap), blk(qmap), pl.BlockSpec((8, tq), tmap)],
            scratch_shapes=[pltpu.VMEM((tq, LANES), F32)] * 2),
        out_shape=[jax.ShapeDtypeStruct((T, HP), BF16), jax.ShapeDtypeStruct((HP, T), BF16),
                   jax.ShapeDtypeStruct((T, HP), F32), jax.ShapeDtypeStruct((8 * MLA_HEADS, T), F32)],
        compiler_params=_cparams(("parallel", "arbitrary")),
    )(ii, jj, q, k, v)


def _flash_dq(q, k, v, do, lse, delta):
    T = q.shape[0]
    tq, sr = _flash_tiles(T)
    n = T // tq
    ii, jj = _pairs(n, True)

    def body(ii_ref, jj_ref, q_ref, k_ref, v_ref, do_ref, lse_ref, dl_ref, dq_ref, acc_sc):
        p_ = pl.program_id(1)
        i, j = ii_ref[p_], jj_ref[p_]

        @pl.when(j == 0)
        def _():
            acc_sc[...] = jnp.zeros(acc_sc.shape, F32)

        def tile(diag):
            nrb = tq // sr

            def products(rb):
                rows = slice(rb * sr, (rb + 1) * sr)
                c0, c1 = _col_span(tq, sr, rb, diag, False)
                return _dot(q_ref[rows, :], k_ref[c0:c1, :], "nt"), _dot(do_ref[rows, :], v_ref[c0:c1, :], "nt")

            nxt = products(0)
            for rb in range(nrb):
                rows = slice(rb * sr, (rb + 1) * sr)
                c0, c1 = _col_span(tq, sr, rb, diag, False)
                (s, dp), nxt = nxt, (products(rb + 1) if rb + 1 < nrb else None)
                p = jnp.exp2(s - _lanes(lse_ref[rows, :], c1 - c0))
                if diag:
                    r, c = _span_iota(sr, rb, c0, c1)
                    p = jnp.where(c <= r, p, 0.0)
                acc_sc[rows, :] += _dot(p * (dp - _lanes(dl_ref[rows, :], c1 - c0)), k_ref[c0:c1, :])

        @pl.when(j < i)
        def _():
            tile(False)

        @pl.when(j == i)
        def _():
            tile(True)
            dq_ref[...] = acc_sc[...].astype(dq_ref.dtype)

    blk = lambda which: pl.BlockSpec((tq, LANES), which)
    qmap = lambda h, p, ii, jj: (ii[p], h)
    kmap = lambda h, p, ii, jj: (jj[p], h)
    return pl.pallas_call(
        body, name="mla_flash_dq",
        grid_spec=pltpu.PrefetchScalarGridSpec(
            num_scalar_prefetch=2, grid=(MLA_HEADS, int(ii.shape[0])),
            in_specs=[blk(qmap), blk(kmap), blk(kmap), blk(qmap), blk(qmap), blk(qmap)],
            out_specs=blk(qmap),
            scratch_shapes=[pltpu.VMEM((tq, LANES), F32)]),
        out_shape=jax.ShapeDtypeStruct((T, HP), BF16),
        compiler_params=_cparams(("parallel", "arbitrary")),
    )(ii, jj, q, k, v, do, lse, delta)


def _flash_dkv(q, k, v, do, lse_t, delta_t):
    T = q.shape[0]
    tq, sr = _flash_tiles(T)
    n = T // tq
    jj, ii = _pairs(n, False)

    def body(jj_ref, ii_ref, q_ref, k_ref, v_ref, do_ref, lse_ref, dl_ref, dk_ref, dv_ref, dk_sc, dv_sc):
        p_ = pl.program_id(1)
        j, i = jj_ref[p_], ii_ref[p_]

        @pl.when(i == j)
        def _():
            dk_sc[...] = jnp.zeros(dk_sc.shape, F32)
            dv_sc[...] = jnp.zeros(dv_sc.shape, F32)

        def tile(diag):
            nrb = tq // sr

            def products(rb):
                rows = slice(rb * sr, (rb + 1) * sr)
                c0, c1 = _col_span(tq, sr, rb, diag, True)
                return _dot(k_ref[rows, :], q_ref[c0:c1, :], "nt"), _dot(v_ref[rows, :], do_ref[c0:c1, :], "nt")

            nxt = products(0)
            for rb in range(nrb):
                rows = slice(rb * sr, (rb + 1) * sr)
                c0, c1 = _col_span(tq, sr, rb, diag, True)
                (st, dpt), nxt = nxt, (products(rb + 1) if rb + 1 < nrb else None)
                pt = jnp.exp2(st - lse_ref[:1, c0:c1])
                if diag:
                    r, c = _span_iota(sr, rb, c0, c1)
                    pt = jnp.where(r <= c, pt, 0.0)
                dv_sc[rows, :] += _dot(pt, do_ref[c0:c1, :])
                dk_sc[rows, :] += _dot(pt * (dpt - dl_ref[:1, c0:c1]), q_ref[c0:c1, :])

        @pl.when(i == j)
        def _():
            tile(True)

        @pl.when(i > j)
        def _():
            tile(False)

        @pl.when(i == n - 1)
        def _():
            dk_ref[...] = (dk_sc[...] * LN2).astype(dk_ref.dtype)
            dv_ref[...] = dv_sc[...].astype(dv_ref.dtype)

    blk = lambda which: pl.BlockSpec((tq, LANES), which)
    qmap = lambda h, p, jj, ii: (ii[p], h)
    kmap = lambda h, p, jj, ii: (jj[p], h)
    lse_rows = pl.BlockSpec((8, tq), lambda h, p, jj, ii: (h, ii[p]))
    delta_rows = pl.BlockSpec((8, tq), lambda h, p, jj, ii: (h * (LANES // 8), ii[p]))
    return pl.pallas_call(
        body, name="mla_flash_dkv",
        grid_spec=pltpu.PrefetchScalarGridSpec(
            num_scalar_prefetch=2, grid=(MLA_HEADS, int(ii.shape[0])),
            in_specs=[blk(qmap), blk(kmap), blk(kmap), blk(qmap), lse_rows, delta_rows],
            out_specs=[blk(kmap), blk(kmap)],
            scratch_shapes=[pltpu.VMEM((tq, LANES), F32)] * 2),
        out_shape=[jax.ShapeDtypeStruct((T, HP), BF16)] * 2,
        compiler_params=_cparams(("parallel", "arbitrary")),
    )(jj, ii, q, k, v, do, lse_t, delta_t)


def _mem_fwd(z, km, vm, gq):
    scale = MEM_HEAD_DIM ** -0.5

    def fn(qm, km, vm, gq):
        ys = []
        for h in range(MEM_HEADS):
            sl = slice(h * LANES, (h + 1) * LANES)
            q = _rms(qm[:, sl], gq) * scale
            s = _dot(q, km[:, sl], "nt")
            p = jnp.exp(s - jnp.max(s, axis=1, keepdims=True))
            p = p / jnp.sum(p, axis=1, keepdims=True)
            ys.append(_dot(p, vm[:, sl]))
        y = jnp.concatenate(ys, axis=1)
        return y, y

    return _rowwise(fn, "mem_fwd", 512, [(z, MEM_WIDTH, Z_QM // MEM_WIDTH)], [km, vm, gq],
                    [(MEM_WIDTH, BF16), (MEM_WIDTH, BF16, "T")])


def _mem_bwd(z, dy, km, vm, gq, dz):
    scale = MEM_HEAD_DIM ** -0.5

    def fn(qm, dy, km, vm, gq):
        dqs, dks, dvs = [], [], []
        dgq = jnp.zeros((1, LANES), F32)
        for h in range(MEM_HEADS):
            sl = slice(h * LANES, (h + 1) * LANES)
            q = (_rms(qm[:, sl], gq) * scale).astype(BF16)
            dyh = dy[:, sl]
            kh, vh = km[:, sl], vm[:, sl]
            s = _dot(q, kh, "nt")
            p = jnp.exp(s - jnp.max(s, axis=1, keepdims=True))
            p = p / jnp.sum(p, axis=1, keepdims=True)
            dp = _dot(dyh, vh, "nt")
            ds = p * (dp - jnp.sum(p * dp, axis=1, keepdims=True))
            dq = _dot(ds, kh) * scale
            dx, dg = _rms_bwd(qm[:, sl], gq, dq)
            dqs.append(dx)
            dgq = dgq + dg
            st = _dot(kh, q, "nt")
            pt = jnp.exp(st - jnp.max(st, axis=0, keepdims=True))
            pt = pt / jnp.sum(pt, axis=0, keepdims=True)
            dpt = _dot(vh, dyh, "nt")
            dst = pt * (dpt - jnp.sum(pt * dpt, axis=0, keepdims=True))
            dvs.append(_dot(pt, dyh))
            dks.append(_dot(dst, q))
        return jnp.concatenate(dqs, axis=1), jnp.concatenate(dks, axis=1), jnp.concatenate(dvs, axis=1), dgq

    m = km.shape[0]
    return _rowwise(fn, "mem_bwd", 512, [(z, MEM_WIDTH, Z_QM // MEM_WIDTH), dy], [km, vm, gq],
                    [("into", dz, MEM_WIDTH, Z_QM // MEM_WIDTH)],
                    [((m, MEM_WIDTH), F32), ((m, MEM_WIDTH), F32), ((1, LANES), F32)])


GROUPS = {"ffn1": ["ffn1_w_gu"], "ffn1_down": ["ffn1_w_down"],
          "mix": ["w_in", "mla_w_uq", "mla_w_ukv", "mem_w_kv", "w_branch_a", "w_branch_b", "w_branch_c", "w_out"],
          "ffn2": ["ffn2_w_gu", "ffn2_w_down"]}
GRAD_GROUPS = {"ffn2": GROUPS["ffn2"], "mix": GROUPS["mix"], "ffn1_down": ["ffn1_w_down"], "ffn1_gu": ["ffn1_w_gu"]}


def _local_step(x, mem, positions, loss_target, P, weights, grads_out):
    T = x.shape[0]
    G = {}
    W = dict(weights("ffn1", None))

    half = MLA_ROPE // 2
    inv = ROPE_BASE ** (-jnp.arange(half, dtype=F32) / half)
    ang = positions.astype(F32)[:, None] * inv
    cos, sin = jnp.cos(ang), jnp.sin(ang)
    one, zero = jnp.ones((T, MLA_NOPE), F32), jnp.zeros((T, half), F32)
    pad = LANES - MLA_QK
    tabs = (jnp.concatenate([one, cos, cos, jnp.ones((T, pad), F32)], axis=1),
            jnp.concatenate([jnp.zeros((T, MLA_NOPE), F32), -sin, zero, jnp.zeros((T, pad), F32)], axis=1),
            jnp.concatenate([jnp.zeros((T, MLA_NOPE), F32), zero, sin, jnp.zeros((T, pad), F32)], axis=1))
    gq_p = jnp.pad(P["mla_q_norm"], ((0, 0), (0, pad)))
    gk_p = jnp.pad(P["mla_k_norm"], ((0, 0), (0, pad)))
    bias_full = jnp.repeat(P["sg_b"].T, SG_GROUP_DIM, axis=1)
    group_ind = jnp.repeat(jnp.eye(SG_GROUPS, dtype=F32), SG_GROUP_DIM, axis=0)

    HT = (D_MODEL, BF16, "T")

    def norm2(x, g):
        h = _rms(x, g)
        return h, h

    h1, h1t = _rowwise(norm2, "ffn1_norm", 512, [x], [P["ffn1_norm"]], [(D_MODEL, BF16), HT])
    def ffn1_w_down(after):
        W.update(weights("ffn1_down", after))
        return W["ffn1_w_down"]

    gu1, a1t, o1 = _ffn_fwd(h1, W["ffn1_w_gu"], ffn1_w_down, "ffn1")

    def resid_norm(x, o, g):
        xn = x + 0.5 * o
        h = _rms(xn, g)
        return xn, h, h

    x1, hm, hmt = _rowwise(resid_norm, "mix_norm", 512, [x, o1], [P["mix_norm"]],
                           [(D_MODEL, F32), (D_MODEL, BF16), HT])
    W.update(weights("mix", hm))
    z = _mm(hm, W["w_in"], "nn", BF16, "w_in", tm=1024, tn=1792)

    y_a, y_at = _sg_fwd(z, P["sg_ln_g"], P["sg_ln_b"], P["sg_w"], bias_full)

    def c_norm(cq, ckv, gq, gkv):
        a, b = _rms(cq, gq), _rms(ckv, gkv)
        return a, b, a, b

    cqn, ckvn, cqnt, ckvnt = _rowwise(
        c_norm, "mla_cnorm", 512, [(z, MLA_Q_RANK, Z_CQ // MLA_Q_RANK), (z, MLA_KV_RANK, Z_CKV // MLA_KV_RANK)],
        [P["mla_cq_norm"], P["mla_ckv_norm"]],
        [(MLA_Q_RANK, BF16), (MLA_KV_RANK, BF16), (MLA_Q_RANK, BF16, "T"), (MLA_KV_RANK, BF16, "T")])
    q_pre = _mm(cqn, W["mla_w_uq"], "nn", BF16, "mla_uq", tm=1024, tn=1024)
    kv_pre = _mm(ckvn, W["mla_w_ukv"], "nn", BF16, "mla_ukv", tm=1024, tn=1024)
    q, k, v = _mla_post(q_pre, kv_pre, z, tabs, gq_p, gk_p)
    y_b, y_bt, lse, lse_t = _flash_fwd(q, k, v)

    memn, = _rowwise(lambda m, g: _rms(m, g), "mem_norm", 256, [mem], [P["mem_norm"]], [(D_MODEL, BF16)])
    kvm = _mm(memn, W["mem_w_kv"], "nn", F32, "mem_kv")

    def mem_k(kvm, gk):
        ks = [_rms(kvm[:, h * LANES:(h + 1) * LANES], gk) for h in range(MEM_HEADS)]
        return jnp.concatenate(ks, axis=1), kvm[:, MEM_WIDTH:]

    km, vm = _rowwise(mem_k, "mem_knorm", 256, [kvm], [P["mem_k_norm"]], [(MEM_WIDTH, BF16), (MEM_WIDTH, BF16)])
    y_c, y_ct = _mem_fwd(z, km, vm, P["mem_q_norm"])

    pa = _mm(y_a, W["w_branch_a"], "nn", BF16, "branch_a", tm=1024, tn=1024)
    pb = _mm(y_b, W["w_branch_b"], "nn", BF16, "branch_b", tm=1024, tn=1024)
    pc = _mm(y_c, W["w_branch_c"], "nn", BF16, "branch_c", tm=1024, tn=1024)

    def merge(zg, pa, pb, pc, b):
        g = _sigmoid(zg + b)
        m = g[:, :D_MODEL] * pa + g[:, D_MODEL:2 * D_MODEL] * pb + g[:, 2 * D_MODEL:] * pc
        return m, m

    merged, mergedt = _rowwise(merge, "merge", 256, [(z, 3 * D_MODEL, 0), pa, pb, pc], [P["b_gate"]],
                               [(D_MODEL, BF16), HT])
    om = _mm(merged, W["w_out"], "nn", BF16, "w_out", tm=1024, tn=1024)

    def resid_norm1(x, o, g):
        xn = x + o
        h = _rms(xn, g)
        return xn, h, h

    x2, h2, h2t = _rowwise(resid_norm1, "ffn2_norm", 512, [x1, om], [P["ffn2_norm"]],
                           [(D_MODEL, F32), (D_MODEL, BF16), HT])
    W.update(weights("ffn2", h2))
    gu2, a2t, o2 = _ffn_fwd(h2, W["ffn2_w_gu"], W["ffn2_w_down"], "ffn2")

    def loss_fn(x2, o2, t):
        e = x2 + 0.5 * o2 - t
        return e * (1.0 / D_MODEL), (e * (0.5 / D_MODEL)).astype(BF16), _rsum(e * e) * (0.5 / D_MODEL)

    dx3, do2, loss_part = _rowwise(loss_fn, "loss", 512, [x2, o2, loss_target], [],
                                   [(D_MODEL, F32), (D_MODEL, BF16)], [((1, D_MODEL), F32)])

    dh2, G["ffn2_w_gu"], G["ffn2_w_down"] = _ffn_bwd(do2, h2t, gu2, a2t, W["ffn2_w_gu"], W["ffn2_w_down"], "ffn2")
    tie = grads_out("ffn2", G)

    def norm_bwd(x, dh, dxo, g, *_):
        dx, dg = _rms_bwd(x, g, dh)
        dx = dx + dxo
        return dx, dx, dg

    dx2, dx2b, G["ffn2_norm"] = _rowwise(norm_bwd, "ffn2_norm_bwd", 512, [x2, dh2, dx3],
                                         [P["ffn2_norm"]] + ([] if tie is None else [tie]),
                                         [(D_MODEL, F32), (D_MODEL, BF16)], [((1, D_MODEL), F32)])

    G["w_out"] = _mm_t(mergedt, dx2b, "w_out_dw", tm=1024, tn=1024)
    dmerged = _mm(dx2b, W["w_out"], "nt", BF16, "w_out_dx", tm=1024, tn=1024)

    def merge_bwd(zg, pa, pb, pc, dm, b):
        g = _sigmoid(zg + b)
        ps = jnp.concatenate([pa, pb, pc], axis=1)
        dm3 = jnp.concatenate([dm, dm, dm], axis=1)
        dzg = dm3 * ps * g * (1.0 - g)
        dp = dm3 * g
        return dzg, dp[:, :D_MODEL], dp[:, D_MODEL:2 * D_MODEL], dp[:, 2 * D_MODEL:], _rsum(dzg)

    dz = lax.empty((T, Z_COLS), BF16)
    dz, dpa, dpb, dpc, G["b_gate"] = _rowwise(
        merge_bwd, "merge_bwd", 256, [(z, 3 * D_MODEL, 0), pa, pb, pc, dmerged], [P["b_gate"]],
        [("into", dz, 3 * D_MODEL, 0), (D_MODEL, BF16), (D_MODEL, BF16), (D_MODEL, BF16)], [((1, 3 * D_MODEL), F32)])

    G["w_branch_a"] = _mm_t(y_at, dpa, "branch_a_dw", tm=512, tn=1024)
    G["w_branch_b"] = _mm_t(y_bt, dpb, "branch_b_dw", tm=1024, tn=1024)
    G["w_branch_c"] = _mm_t(y_ct, dpc, "branch_c_dw", tm=512, tn=1024)
    dy_a = _mm(dpa, W["w_branch_a"], "nt", BF16, "branch_a_dx", tm=1024, tn=512)
    dy_b = _mm(dpb, W["w_branch_b"], "nt", BF16, "branch_b_dx", tm=1024, tn=1024)
    dy_c = _mm(dpc, W["w_branch_c"], "nt", BF16, "branch_c_dx", tm=1024, tn=512)

    dz, G["sg_w"], dbias_t, G["sg_ln_g"], G["sg_ln_b"] = _sg_bwd(
        z, dy_a, P["sg_ln_g"], P["sg_ln_b"], P["sg_w"], bias_full, group_ind, dz)
    G["sg_b"] = dbias_t.T

    dz, dkm, dvm, G["mem_q_norm"] = _mem_bwd(z, dy_c, km, vm, P["mem_q_norm"], dz)

    def mem_k_bwd(kvm, dkm, dvm, gk):
        dks = []
        dg = jnp.zeros((1, LANES), F32)
        for h in range(MEM_HEADS):
            sl = slice(h * LANES, (h + 1) * LANES)
            dx, d = _rms_bwd(kvm[:, sl], gk, dkm[:, sl])
            dks.append(dx)
            dg = dg + d
        return jnp.concatenate(dks + [dvm], axis=1), dg

    dkvm, G["mem_k_norm"] = _rowwise(mem_k_bwd, "mem_knorm_bwd", 256, [kvm, dkm, dvm], [P["mem_k_norm"]],
                                     [(2 * MEM_WIDTH, BF16)], [((1, LANES), F32)])
    G["mem_w_kv"] = _mm(memn, dkvm, "tn", BF16, "mem_kv_dw")
    dmemn = _mm(dkvm, W["mem_w_kv"], "nt", F32, "mem_kv_dx")
    _, G["mem_norm"] = _rowwise(lambda m, d, g: _rms_bwd(m, g, d), "mem_norm_bwd", 256, [mem, dmemn],
                                [P["mem_norm"]], [(D_MODEL, BF16)], [((1, D_MODEL), F32)])

    def delta_fn(o, do):
        od = o.astype(F32) * do.astype(F32)
        ds = [jnp.broadcast_to(jnp.sum(od[:, h * LANES:(h + 1) * LANES], axis=1, keepdims=True), (od.shape[0], LANES))
              for h in range(MLA_HEADS)]
        d = jnp.concatenate(ds, axis=1)
        return d, d

    delta, delta_t = _rowwise(delta_fn, "mla_delta", 512, [y_b, dy_b], [], [(HP, F32), (HP, F32, "T")])
    dq = _flash_dq(q, k, v, dy_b, lse, delta)
    dk, dv = _flash_dkv(q, k, v, dy_b, lse_t, delta_t)
    dq_pre, dkv_pre, dkr, dgq, dgk = _mla_post_bwd(q_pre, kv_pre, z, tabs, gq_p, gk_p, dq, dk, dv)
    G["mla_q_norm"], G["mla_k_norm"] = dgq[:, :MLA_QK], dgk[:, :MLA_QK]
    G["mla_w_uq"] = _mm_t(cqnt, dq_pre, "mla_uq_dw", tm=384, tn=1024)
    G["mla_w_ukv"] = _mm_t(ckvnt, dkv_pre, "mla_ukv_dw", tm=256, tn=2048)
    dcqn = _mm(dq_pre, W["mla_w_uq"], "nt", BF16, "mla_uq_dx", tm=1024)
    dckvn = _mm(dkv_pre, W["mla_w_ukv"], "nt", BF16, "mla_ukv_dx", tm=1024)

    def c_norm_bwd(cq, ckv, dcqn, dckvn, dkr, gq, gkv):
        dcq, dgq = _rms_bwd(cq, gq, dcqn)
        dckv, dgkv = _rms_bwd(ckv, gkv, dckvn)
        return jnp.concatenate([dckv, dkr, dcq], axis=1), dgq, dgkv

    tail = Z_COLS - Z_CKV
    dz, G["mla_cq_norm"], G["mla_ckv_norm"] = _rowwise(
        c_norm_bwd, "mla_cnorm_bwd", 512,
        [(z, MLA_Q_RANK, Z_CQ // MLA_Q_RANK), (z, MLA_KV_RANK, Z_CKV // MLA_KV_RANK), dcqn, dckvn, dkr],
        [P["mla_cq_norm"], P["mla_ckv_norm"]], [("into", dz, tail, Z_CKV // tail)],
        [((1, MLA_Q_RANK), F32), ((1, MLA_KV_RANK), F32)])
    G["w_in"] = _mm_t(hmt, dz, "w_in_dw", tm=1024, tn=1792, tk=2048)
    dhm = _mm(dz, W["w_in"], "nt", BF16, "w_in_dx", tm=1024, tn=1024, tk=2688)

    def norm_bwd_half(x, dh, dxo, g):
        dx, dg = _rms_bwd(x, g, dh)
        dx = dx + dxo
        return dx, (0.5 * dx), dg

    dx1, do1, G["mix_norm"] = _rowwise(norm_bwd_half, "mix_norm_bwd", 512, [x1, dhm, dx2], [P["mix_norm"]],
                                       [(D_MODEL, F32), (D_MODEL, BF16)], [((1, D_MODEL), F32)])
    tie = grads_out("mix", G)

    def ffn1_dw(which, dw):
        G["ffn1_w_" + which] = dw
        return grads_out("ffn1_" + which, G)

    dh1, _, _ = _ffn_bwd(do1, h1t, gu1, a1t, W["ffn1_w_gu"], W["ffn1_w_down"], "ffn1", tie, ffn1_dw)

    def norm_bwd_last(x, dh, dxo, g):
        dx, dg = _rms_bwd(x, g, dh)
        return dx + dxo, dg

    grad_x, G["ffn1_norm"] = _rowwise(norm_bwd_last, "ffn1_norm_bwd", 512, [x, dh1, dx1], [P["ffn1_norm"]],
                                      [(D_MODEL, F32)], [((1, D_MODEL), F32)])
    return loss_part, grad_x, G


SHARDED = ["ffn1_w_gu", "ffn1_w_down", "w_in", "mla_w_uq", "mla_w_ukv", "mem_w_kv",
           "w_branch_a", "w_branch_b", "w_branch_c", "w_out", "ffn2_w_gu", "ffn2_w_down"]
ROW_SHARDED = {"ffn1_w_down", "mem_w_kv", "w_out", "ffn2_w_down"}
SMALL = ["ffn1_norm", "mix_norm", "b_gate", "sg_ln_g", "sg_ln_b", "sg_w", "sg_b", "mla_cq_norm",
         "mla_ckv_norm", "mla_q_norm", "mla_k_norm", "mem_norm", "mem_q_norm", "mem_k_norm", "ffn2_norm"]
ORDER = ["ffn1_norm", "ffn1_w_gu", "ffn1_w_down", "mix_norm", "w_in", "b_gate", "sg_ln_g", "sg_ln_b", "sg_w",
         "sg_b", "mla_cq_norm", "mla_w_uq", "mla_ckv_norm", "mla_w_ukv", "mla_q_norm", "mla_k_norm", "mem_norm",
         "mem_w_kv", "mem_q_norm", "mem_k_norm", "w_branch_a", "w_branch_b", "w_branch_c", "w_out", "ffn2_norm",
         "ffn2_w_gu", "ffn2_w_down"]

_IN_U, _IN_V, _IN_CQ, _IN_CKV, _IN_KR, _IN_QM, _IN_G = 0, 512, 1024, 1408, 1664, 1696, 2208
IN_COLS = 5280


def _full_from_slabs(name, slabs):
    n, r, c = slabs.shape
    if name in ROW_SHARDED:
        return slabs.reshape(n * r, c)
    return slabs.transpose(1, 0, 2).reshape(r, n * c)


def _slabs_from_full(name, full):
    if name in ROW_SHARDED:
        return full.reshape(N_DEV, full.shape[0] // N_DEV, full.shape[1])
    r, c = full.shape
    return full.reshape(r, N_DEV, c // N_DEV).transpose(1, 0, 2)


def _compute_layout(full):
    W = dict(full)
    if "w_in" not in full:
        return W
    w = full["w_in"]
    kr = jnp.pad(w[:, _IN_KR:_IN_QM], ((0, 0), (KR_LANE, LANES - KR_LANE - MLA_ROPE)))
    W["w_in"] = jnp.concatenate([w[:, _IN_G:], w[:, _IN_U:_IN_CQ], w[:, _IN_QM:_IN_G], w[:, _IN_CKV:_IN_KR], kr,
                                 w[:, _IN_CQ:_IN_CKV]], axis=1)
    uq = full["mla_w_uq"].reshape(MLA_Q_RANK, MLA_HEADS, MLA_QK)
    W["mla_w_uq"] = jnp.pad(uq, ((0, 0), (0, 0), (0, LANES - MLA_QK))).reshape(MLA_Q_RANK, HP)
    ukv = full["mla_w_ukv"].reshape(MLA_KV_RANK, MLA_HEADS, MLA_NOPE + MLA_V)
    padh = lambda a: jnp.pad(a, ((0, 0), (0, 0), (0, LANES - a.shape[2]))).reshape(MLA_KV_RANK, HP)
    W["mla_w_ukv"] = jnp.concatenate([padh(ukv[:, :, :MLA_NOPE]), padh(ukv[:, :, MLA_NOPE:])], axis=1)
    wb = full["w_branch_b"].reshape(MLA_HEADS, MLA_V, D_MODEL)
    W["w_branch_b"] = jnp.pad(wb, ((0, 0), (0, LANES - MLA_V), (0, 0))).reshape(HP, D_MODEL)
    return W


def _reference_layout(G):
    out = dict(G)
    if "w_in" not in G:
        return out
    g = G["w_in"]
    out["w_in"] = jnp.concatenate([
        g[:, Z_U:Z_QM], g[:, Z_CQ:Z_COLS], g[:, Z_CKV:Z_KR], g[:, Z_KR + KR_LANE:Z_KR + KR_LANE + MLA_ROPE],
        g[:, Z_QM:Z_CKV], g[:, Z_G:Z_U]], axis=1)
    out["mla_w_uq"] = G["mla_w_uq"].reshape(MLA_Q_RANK, MLA_HEADS, LANES)[:, :, :MLA_QK].reshape(MLA_Q_RANK, -1)
    gk = G["mla_w_ukv"][:, :HP].reshape(MLA_KV_RANK, MLA_HEADS, LANES)[:, :, :MLA_NOPE]
    gv = G["mla_w_ukv"][:, HP:].reshape(MLA_KV_RANK, MLA_HEADS, LANES)[:, :, :MLA_V]
    out["mla_w_ukv"] = jnp.concatenate([gk, gv], axis=2).reshape(MLA_KV_RANK, -1)
    out["w_branch_b"] = G["w_branch_b"].reshape(MLA_HEADS, LANES, D_MODEL)[:, :MLA_V].reshape(-1, D_MODEL)
    return out


def _pack(parts):
    flat = []
    for a in parts:
        a = a.reshape(-1)
        flat.append(jnp.pad(a, (0, (-a.shape[0]) % LANES)))
    return jnp.concatenate(flat).reshape(-1, LANES)


def _unpack(packed, shapes):
    flat = packed.reshape(-1)
    out, off = [], 0
    for shp in shapes:
        n = int(np.prod(shp))
        out.append(flat[off:off + n].reshape(shp))
        off += n + (-n) % LANES
    return out


MESH = pl.DeviceIdType.MESH
HBM = pl.BlockSpec(memory_space=pltpu.HBM)


def _all_gather(shards):
    n = len(shards)

    def body(*refs):
        x_refs, out_refs, token_ref = refs[:n], refs[n:2 * n], refs[2 * n]
        send_sems, recv_sems, local_sems = refs[2 * n + 1:]
        x, y, c = lax.axis_index("x"), lax.axis_index("y"), lax.axis_index("c")
        me, sibling = (x, y, c), (x, y, 1 - c)
        chips = [(1 - x, y), (x, 1 - y), (1 - x, 1 - y)]
        token_ref[...] = jnp.zeros_like(token_ref)

        def slot(a, px, py, pc):
            return out_refs[a].at[4 * px + 2 * py + pc]

        def copy(a, k, block, to, src=None):
            return pltpu.make_async_remote_copy(
                src_ref=slot(a, *block) if src is None else src, dst_ref=slot(a, *block),
                send_sem=send_sems.at[7 * a + k], recv_sem=recv_sems.at[7 * a + k], device_id=to, device_id_type=MESH)

        arrays = range(n)
        mine = [pltpu.make_async_copy(x_refs[a], slot(a, *me), local_sems.at[a]) for a in arrays]
        for cp in mine:
            cp.start()
        first = [copy(a, 0, me, sibling, src=x_refs[a]) for a in arrays]
        first += [copy(a, 1 + j, me, (*chip, c), src=x_refs[a]) for j, chip in enumerate(chips) for a in arrays]
        for cp in first:
            cp.start()
        passed = []
        for j, chip in enumerate(chips):
            for a in arrays:
                copy(a, 1 + j, (*chip, c), me).wait_recv()
                passed.append(copy(a, 4 + j, (*chip, c), sibling))
                passed[-1].start()
        for a in arrays:
            copy(a, 0, sibling, me).wait_recv()
        for j, chip in enumerate(chips):
            for a in arrays:
                copy(a, 4 + j, (*chip, 1 - c), me).wait_recv()
        for cp in first + passed:
            cp.wait_send()
        for cp in mine:
            cp.wait()

    res = pl.pallas_call(
        body, name="all_gather_weights",
        out_shape=[jax.ShapeDtypeStruct((N_DEV,) + s.shape, s.dtype) for s in shards]
        + [jax.ShapeDtypeStruct((8, LANES), F32)],
        in_specs=[HBM] * n, out_specs=[HBM] * n + [pl.BlockSpec(memory_space=pltpu.VMEM)],
        scratch_shapes=[pltpu.SemaphoreType.DMA((7 * n,)), pltpu.SemaphoreType.DMA((7 * n,)),
                        pltpu.SemaphoreType.DMA((n,))],
    )(*shards)
    return res[:n], res[n]


SEM = pl.BlockSpec(memory_space=pltpu.SEMAPHORE)
DATAFLOW = pltpu.SideEffectType.DATAFLOW_SIDE_EFFECTING


def _peers():
    x, y, c = lax.axis_index("x"), lax.axis_index("y"), lax.axis_index("c")
    out = []
    for k in range(1, N_DEV):
        px = 1 - x if k & 4 else x
        py = 1 - y if k & 2 else y
        pc = 1 - c if k & 1 else c
        out.append((k, (px, py, pc), 4 * px + 2 * py + pc))
    return 4 * x + 2 * y + c, out


def _send_start(srcs, per_peer, name):
    n = len(srcs)
    lands = [lax.empty((N_DEV,) + (s.shape[1:] if per_peer else s.shape), s.dtype) for s in srcs]

    def body(*refs):
        src_refs, land_refs, send_sems, recv_sems, token = refs[:n], refs[n:2 * n], refs[2 * n], refs[2 * n + 1], refs[-1]
        me, peers = _peers()
        for a in range(n):
            for k, pid, pflat in peers:
                pltpu.make_async_remote_copy(
                    src_ref=src_refs[a].at[pflat] if per_peer else src_refs[a], dst_ref=land_refs[a].at[me],
                    send_sem=send_sems.at[7 * a + k - 1], recv_sem=recv_sems.at[7 * a + k - 1],
                    device_id=pid, device_id_type=MESH).start()
        token[...] = jnp.zeros_like(token)

    hbm = lambda a: pltpu.with_memory_space_constraint(a, pltpu.HBM)
    res = pl.pallas_call(
        body, name=name,
        out_shape=(pltpu.SemaphoreType.DMA((7 * n,)), pltpu.SemaphoreType.DMA((7 * n,)),
                   *[pltpu.HBM(a.shape, a.dtype) for a in srcs + lands], jax.ShapeDtypeStruct((8, LANES), F32)),
        in_specs=(HBM,) * (2 * n), out_specs=(SEM, SEM) + (HBM,) * (2 * n) + (pl.BlockSpec(memory_space=pltpu.VMEM),),
        input_output_aliases={i: 2 + i for i in range(2 * n)},
        compiler_params=pltpu.CompilerParams(has_side_effects=DATAFLOW),
    )(*[hbm(a) for a in srcs + lands])
    return (res[0], res[1], list(res[2:2 + n]), list(res[2 + n:2 + 2 * n])), res[-1]


def _send_wait(started, after, per_peer, name):
    send_sems, recv_sems, srcs_thru, lands_thru = started
    n = len(srcs_thru)

    def body(*refs):
        src_refs, land_refs, send_sems, recv_sems = refs[:n], refs[n:2 * n], refs[2 * n], refs[2 * n + 1]
        me, peers = _peers()
        for a in range(n):
            for k, pid, pflat in peers:
                copy = pltpu.make_async_remote_copy(
                    src_ref=src_refs[a].at[pflat] if per_peer else src_refs[a], dst_ref=land_refs[a].at[pflat],
                    send_sem=send_sems.at[7 * a + k - 1], recv_sem=recv_sems.at[7 * a + k - 1],
                    device_id=pid, device_id_type=MESH)
                copy.wait_send()
                copy.wait_recv()

    outs = pl.pallas_call(
        body, name=name,
        out_shape=tuple(pltpu.HBM(a.shape, a.dtype) for a in srcs_thru + lands_thru),
        in_specs=(HBM,) * (2 * n) + (SEM, SEM, pl.BlockSpec(memory_space=pl.ANY)), out_specs=(HBM,) * (2 * n),
        input_output_aliases={i: i for i in range(2 * n)},
        compiler_params=pltpu.CompilerParams(has_side_effects=DATAFLOW),
    )(*srcs_thru, *lands_thru, send_sems, recv_sems, after)
    me = 4 * lax.axis_index("x") + 2 * lax.axis_index("y") + lax.axis_index("c")
    landed = []
    for src_out, land in zip(outs[:n], outs[n:]):
        own = lax.dynamic_index_in_dim(src_out, me, 0, keepdims=True) if per_peer else src_out[None]
        landed.append(lax.dynamic_update_slice(land, own, (me,) + (0,) * (land.ndim - 1)))
    return landed


def _share_rows(block, name):
    def body(src_ref, out_ref, send_sems, recv_sems, local_sem):
        me, peers = _peers()
        own = pltpu.make_async_copy(src_ref, out_ref.at[me], local_sem)
        own.start()
        copies = [pltpu.make_async_remote_copy(
            src_ref=src_ref, dst_ref=out_ref.at[me], send_sem=send_sems.at[k - 1], recv_sem=recv_sems.at[k - 1],
            device_id=pid, device_id_type=MESH) for k, pid, _ in peers]
        for cp in copies:
            cp.start()
        for cp in copies:
            cp.wait()
        own.wait()

    return pl.pallas_call(
        body, name=name, out_shape=jax.ShapeDtypeStruct((N_DEV,) + block.shape, block.dtype),
        in_specs=[HBM], out_specs=HBM,
        scratch_shapes=[pltpu.SemaphoreType.DMA((N_DEV - 1,)), pltpu.SemaphoreType.DMA((N_DEV - 1,)),
                        pltpu.SemaphoreType.DMA],
    )(block)


def _sum_slots(recv, name, tr):
    n, rows, lanes = recv.shape
    tr = _tile(rows, tr)

    def body(r_ref, o_ref):
        acc = r_ref[0].astype(F32)
        for i in range(1, n):
            acc = acc + r_ref[i].astype(F32)
        o_ref[...] = acc

    return pl.pallas_call(
        body, name=name, grid=(rows // tr,),
        in_specs=[pl.BlockSpec((n, tr, lanes), lambda i: (0, i, 0))],
        out_specs=pl.BlockSpec((tr, lanes), lambda i: (i, 0)),
        out_shape=jax.ShapeDtypeStruct((rows, lanes), F32),
        compiler_params=_cparams(("parallel",)),
    )(recv)


def _adamw_math(w, g, m, v):
    m = ADAM_B1 * m + (1.0 - ADAM_B1) * g
    v = ADAM_B2 * v + (1.0 - ADAM_B2) * (g * g)
    m_hat = m / (1.0 - ADAM_B1 ** ADAM_STEP)
    v_hat = v / (1.0 - ADAM_B2 ** ADAM_STEP)
    return -ADAM_LR * (m_hat / (jnp.sqrt(v_hat) + ADAM_EPS) + ADAM_WD * w), m, v


def _adamw(w, g, m, v, name, tr=256):
    return _rowwise(_adamw_math, name, tr, [w, g, m, v], [], [(w.shape[1], F32)] * 3)


def _adamw_small(ws, gs, ms, vs):
    n = len(ws)

    def body(*refs):
        ins, outs = refs[:4 * n], refs[4 * n:]
        for i in range(n):
            d, m, v = _adamw_math(ins[i][...], ins[n + i][...], ins[2 * n + i][...], ins[3 * n + i][...])
            outs[i][...], outs[n + i][...], outs[2 * n + i][...] = d, m, v

    vmem = pl.BlockSpec(memory_space=pltpu.VMEM)
    res = pl.pallas_call(
        body, name="adamw_small", in_specs=[vmem] * (4 * n), out_specs=[vmem] * (3 * n),
        out_shape=[jax.ShapeDtypeStruct(w.shape, F32) for w in ws] * 3,
    )(*ws, *gs, *ms, *vs)
    return res[:n], res[n:2 * n], res[2 * n:]


def _sum_adamw(recv, w, m, v, name):
    n, r, c = recv.shape
    tr = _tile(r, 256)

    def body(r_ref, w_ref, m_ref, v_ref, g_ref, d_ref, nm_ref, nv_ref):
        g = r_ref[0].astype(F32)
        for i in range(1, n):
            g = g + r_ref[i].astype(F32)
        g_ref[...] = g
        d_ref[...], nm_ref[...], nv_ref[...] = _adamw_math(w_ref[...], g, m_ref[...], v_ref[...])

    row = pl.BlockSpec((None, tr, c), lambda i: (0, i, 0))
    return pl.pallas_call(
        body, name=name, grid=(r // tr,),
        in_specs=[pl.BlockSpec((n, tr, c), lambda i: (0, i, 0)), row, row, row], out_specs=[row] * 4,
        out_shape=[jax.ShapeDtypeStruct((1, r, c), F32)] * 4, compiler_params=_cparams(("parallel",)),
    )(recv, w, m, v)


def kernel(x, mem, positions, ffn1_norm, ffn1_w_gu, ffn1_w_down, mix_norm, w_in, b_gate, sg_ln_g, sg_ln_b, sg_w, sg_b, mla_cq_norm, mla_w_uq, mla_ckv_norm, mla_w_ukv, mla_q_norm, mla_k_norm, mem_norm, mem_w_kv, mem_q_norm, mem_k_norm, w_branch_a, w_branch_b, w_branch_c, w_out, ffn2_norm, ffn2_w_gu, ffn2_w_down, loss_target, m_ffn1_norm, m_ffn1_w_gu, m_ffn1_w_down, m_mix_norm, m_w_in, m_b_gate, m_sg_ln_g, m_sg_ln_b, m_sg_w, m_sg_b, m_mla_cq_norm, m_mla_w_uq, m_mla_ckv_norm, m_mla_w_ukv, m_mla_q_norm, m_mla_k_norm, m_mem_norm, m_mem_w_kv, m_mem_q_norm, m_mem_k_norm, m_w_branch_a, m_w_branch_b, m_w_branch_c, m_w_out, m_ffn2_norm, m_ffn2_w_gu, m_ffn2_w_down, v_ffn1_norm, v_ffn1_w_gu, v_ffn1_w_down, v_mix_norm, v_w_in, v_b_gate, v_sg_ln_g, v_sg_ln_b, v_sg_w, v_sg_b, v_mla_cq_norm, v_mla_w_uq, v_mla_ckv_norm, v_mla_w_ukv, v_mla_q_norm, v_mla_k_norm, v_mem_norm, v_mem_w_kv, v_mem_q_norm, v_mem_k_norm, v_w_branch_a, v_w_branch_b, v_w_branch_c, v_w_out, v_ffn2_norm, v_ffn2_w_gu, v_ffn2_w_down):
    given = dict(locals())
    wts = {n: given[n] for n in ORDER}
    mom = {n: given["m_" + n] for n in ORDER}
    var = {n: given["v_" + n] for n in ORDER}

    def shards(group, zero):
        out = [wts[n][0].astype(BF16) for n in GROUPS[group]]
        return [out[0] + zero.astype(BF16)] + out[1:]

    def full_weights(group, slabs):
        return _compute_layout({n: _full_from_slabs(n, s) for n, s in zip(GROUPS[group], slabs)})

    def zero_of(a):
        return jnp.minimum(jnp.abs(a.reshape(-1)[0]), 0)

    gathered_ffn1, token = _all_gather([wts[n][0].astype(BF16) for n in GROUPS["ffn1"]])
    flight = {}
    flight["ffn1_down"], token = _send_start(shards("ffn1_down", token[0, 0]), False, "gather_ffn1_down_start")
    flight["mix"] = _send_start(shards("mix", token[0, 0]), False, "gather_mix_start")[0]
    recv = {}

    def weights(group, after):
        if group == "ffn1":
            return full_weights(group, gathered_ffn1)
        landed = _send_wait(flight.pop(group), after, False, f"gather_{group}_wait")
        if group == "mix":
            flight["ffn2"] = _send_start(shards("ffn2", zero_of(landed[0])), False, "gather_ffn2_start")[0]
        return full_weights(group, landed)

    small_shapes = [wts[n].shape[1:] for n in SMALL]
    early = SMALL[1:]
    assert SMALL[0] == "ffn1_norm"

    def grads_out(group, G):
        Gr = _reference_layout({n: G[n] for n in GRAD_GROUPS[group]})
        parts = [_slabs_from_full(n, Gr[n]).astype(BF16) for n in GRAD_GROUPS[group]]
        flight["g_" + group], tie = _send_start(parts, True, f"grads_{group}_start")
        if group == "mix":
            small = _pack([G[n].reshape(s) for n, s in zip(early, small_shapes[1:])])
            small = jnp.pad(small, ((0, (-small.shape[0]) % 8), (0, 0)))
            flight["small"], tie = _send_start([small + tie[0, 0]], False, "grads_small_start")
        return tie

    P = {n: wts[n] if wts[n].ndim == 2 else wts[n][0] for n in SMALL}
    loss_part, grad_x, G = _local_step(x[0], mem[0], positions[0], loss_target[0], P, weights, grads_out)

    for group, names in GRAD_GROUPS.items():
        recv.update(zip(names, _send_wait(flight.pop("g_" + group), grad_x, True, f"grads_{group}_wait")))
    early_recv, = _send_wait(flight.pop("small"), grad_x, False, "grads_small_wait")
    last = _share_rows(G["ffn1_norm"].reshape(-1, LANES), "share_ffn1_norm")
    g_small_packed = _sum_slots(jnp.concatenate([last, early_recv], axis=1), "sum_small", 2048)

    grads, delta, new_m, new_v = {}, {}, {}, {}
    for n in SHARDED:
        grads[n], delta[n], new_m[n], new_v[n] = _sum_adamw(recv[n], wts[n], mom[n], var[n], "adamw_" + n)
    grads.update(zip(SMALL, _unpack(g_small_packed, small_shapes)))

    flat2 = lambda d: [d[n].reshape(-1, d[n].shape[-1]) for n in SMALL]
    for dst, vals in zip((delta, new_m, new_v), _adamw_small(flat2(wts), flat2(grads), flat2(mom), flat2(var))):
        dst.update(zip(SMALL, vals))

    loss = lax.psum(jnp.sum(loss_part), ("x", "y", "c"))
    lead = lambda d: [d[n].reshape(wts[n].shape) for n in ORDER]
    return (loss, grad_x[None], *lead(grads), *lead(delta), *lead(new_m), *lead(new_v))
```

```python
import functools

import numpy as np
import jax
import jax.numpy as jnp
from jax import lax
from jax.experimental import pallas as pl
from jax.experimental.pallas import tpu as pltpu

F32, BF16 = jnp.float32, jnp.bfloat16

D_MODEL = 1024
SG_GROUPS, SG_GROUP_DIM, SG_WIDTH, CHUNK = 8, 64, 512, 128
MLA_HEADS, MLA_NOPE, MLA_ROPE, MLA_V, MLA_QK = 8, 64, 32, 64, 96
MLA_Q_RANK, MLA_KV_RANK = 384, 256
MEM_HEADS, MEM_HEAD_DIM, MEM_WIDTH = 4, 128, 512
D_FF = 2816
ROPE_BASE = 10000.0
EPS = 1e-6
NEG = -1e30
ADAM_LR, ADAM_B1, ADAM_B2, ADAM_EPS, ADAM_WD, ADAM_STEP = 0.001, 0.9, 0.999, 1e-08, 0.01, 10

N_DEV = 8
LANES = 128
V7X_VMEM_LIMIT = 56 * 1024 * 1024
HP = MLA_HEADS * LANES

Z_G, Z_U, Z_V, Z_QM, Z_CKV, Z_KR, Z_CQ = 0, 3072, 3584, 4096, 4608, 4864, 4992
Z_COLS = 5376
KR_LANE = 64


def _tile(dim, pref):
    if dim <= pref:
        return dim
    for t in range(pref - pref % LANES, LANES - 1, -LANES):
        if dim % t == 0:
            return t
    for t in range(pref - pref % 8, 7, -8):
        if dim % t == 0:
            return t
    return dim


def _cparams(sem):
    return pltpu.CompilerParams(dimension_semantics=sem, vmem_limit_bytes=V7X_VMEM_LIMIT)


_DN = {"nn": ((1,), (0,)), "nt": ((1,), (1,)), "tn": ((0,), (0,))}


def _dot(a, b, mode="nn"):
    return lax.dot_general(a.astype(BF16), b.astype(BF16), (_DN[mode], ((), ())),
                           preferred_element_type=F32)


def _mm(a, b, mode, out_dtype, name, tm=512, tn=512, tk=2048, tie=None):
    if mode == "tn":
        K, M = a.shape
    else:
        M, K = a.shape
    N = b.shape[0] if mode == "nt" else b.shape[1]
    tm, tn, tk = _tile(M, tm), _tile(N, tn), _tile(K, tk)
    nk = K // tk
    if mode == "tn":
        a_spec = pl.BlockSpec((tk, tm), lambda i, j, k: (k, i))
    else:
        a_spec = pl.BlockSpec((tm, tk), lambda i, j, k: (i, k))
    if mode == "nt":
        b_spec = pl.BlockSpec((tn, tk), lambda i, j, k: (j, k))
    else:
        b_spec = pl.BlockSpec((tk, tn), lambda i, j, k: (k, j))

    ties = [] if tie is None else [tie]

    def body(a_ref, b_ref, *rest):
        o_ref, *scratch = rest[len(ties):]
        p = _dot(a_ref[...], b_ref[...], mode)
        if nk == 1:
            o_ref[...] = p.astype(o_ref.dtype)
        else:
            acc_ref, = scratch
            k = pl.program_id(2)

            @pl.when(k == 0)
            def _():
                acc_ref[...] = p

            @pl.when(k > 0)
            def _():
                acc_ref[...] += p

            @pl.when(k == nk - 1)
            def _():
                o_ref[...] = acc_ref[...].astype(o_ref.dtype)

    return pl.pallas_call(
        body, name=name, grid=(M // tm, N // tn, nk),
        in_specs=[a_spec, b_spec] + [pl.BlockSpec(t.shape, lambda i, j, k: (0, 0)) for t in ties],
        out_specs=pl.BlockSpec((tm, tn), lambda i, j, k: (i, j)),
        out_shape=jax.ShapeDtypeStruct((M, N), out_dtype),
        scratch_shapes=[] if nk == 1 else [pltpu.VMEM((tm, tn), F32)],
        compiler_params=_cparams(("parallel", "parallel", "arbitrary")),
    )(a, b, *ties)


def _mm_t(at, b, name, tm, tn, tk=1024, tie=None):
    return _mm(at, b, "nn", BF16, name, tm=tm, tn=tn, tk=tk, tie=tie)


def _rowwise(fn, name, tr, row_ins, bc_ins, row_outs, acc_outs=()):
    norm = [it if isinstance(it, tuple) else (it, it.shape[1], 0) for it in row_ins]
    rows = norm[0][0].shape[0]
    tr = _tile(rows, tr)
    arrays, in_specs = [], []
    for arr, w, cb in norm:
        arrays.append(arr)
        in_specs.append(pl.BlockSpec((tr, w), lambda i, cb=cb: (i, cb)))
    for arr in bc_ins:
        arrays.append(arr)
        in_specs.append(pl.BlockSpec(arr.shape, lambda i, nd=arr.ndim: (0,) * nd))
    n_in, n_row = len(arrays), len(row_outs)
    out_shape, out_specs, aliases = [], [], {}
    transposed = [len(o) == 3 for o in row_outs]
    for k, o in enumerate(row_outs):
        if o[0] == "into":
            _, target, w, cb = o
            aliases[len(arrays)] = k
            arrays.append(target)
            in_specs.append(pl.BlockSpec(memory_space=pl.ANY))
            out_shape.append(jax.ShapeDtypeStruct(target.shape, target.dtype))
            out_specs.append(pl.BlockSpec((tr, w), lambda i, cb=cb: (i, cb)))
        elif transposed[k]:
            out_shape.append(jax.ShapeDtypeStruct((o[0], rows), o[1]))
            out_specs.append(pl.BlockSpec((o[0], tr), lambda i: (0, i)))
        else:
            out_shape.append(jax.ShapeDtypeStruct((rows, o[0]), o[1]))
            out_specs.append(pl.BlockSpec((tr, o[0]), lambda i: (i, 0)))
    for shp, dt in acc_outs:
        out_shape.append(jax.ShapeDtypeStruct(shp, dt))
        out_specs.append(pl.BlockSpec(shp, lambda i, nd=len(shp): (0,) * nd))

    def body(*refs):
        vals = fn(*[r[...].astype(F32) for r in refs[:n_in]])
        if not isinstance(vals, (tuple, list)):
            vals = (vals,)
        outs = refs[len(arrays):]
        for r, v, t in zip(outs[:n_row], vals[:n_row], transposed):
            r[...] = v.astype(F32).T.astype(r.dtype) if t else v.astype(r.dtype)
        if acc_outs:
            accs = list(zip(outs[n_row:], vals[n_row:]))
            i = pl.program_id(0)

            @pl.when(i == 0)
            def _():
                for r, v in accs:
                    r[...] = v.astype(r.dtype)

            @pl.when(i > 0)
            def _():
                for r, v in accs:
                    r[...] += v.astype(r.dtype)

    res = pl.pallas_call(
        body, name=name, grid=(rows // tr,), in_specs=in_specs, out_specs=out_specs,
        out_shape=out_shape, input_output_aliases=aliases, compiler_params=_cparams(("arbitrary",)),
    )(*arrays)
    return res


def _rsum(x):
    return jnp.sum(x, axis=0, keepdims=True)


def _rms(x, g, n=None):
    n = x.shape[-1] if n is None else n
    r = lax.rsqrt(jnp.sum(x * x, axis=-1, keepdims=True) * (1.0 / n) + EPS)
    return x * r * g


def _rms_bwd(x, g, dy, n=None):
    n = x.shape[-1] if n is None else n
    r = lax.rsqrt(jnp.sum(x * x, axis=-1, keepdims=True) * (1.0 / n) + EPS)
    xh = x * r
    dxh = dy * g
    dx = r * (dxh - xh * (jnp.sum(dxh * xh, axis=-1, keepdims=True) * (1.0 / n)))
    return dx, _rsum(dy * xh)


def _gelu(x):
    return 0.5 * x * (1.0 + lax.erf(x * 0.7071067811865476))


def _gelu_grad(x):
    return 0.5 * (1.0 + lax.erf(x * 0.7071067811865476)) + x * jnp.exp(-0.5 * x * x) * 0.3989422804014327


def _sigmoid(x):
    return 0.5 * jnp.tanh(0.5 * x) + 0.5


FFN_TM, FFN_TN = 1024, 1408
MXU_WIDTH = 256


def _col_chunks(n):
    return [(c, min(c + MXU_WIDTH, n)) for c in range(0, n, MXU_WIDTH)]


def _ffn_gu_act(h, w_gu, tag):
    T = h.shape[0]
    tm, tn = _tile(T, FFN_TM), FFN_TN
    nj = D_FF // tn

    def body(h_ref, wg_ref, wu_ref, gu_ref, a_ref):
        h = h_ref[...]
        for c0, c1 in _col_chunks(tn):
            g = _dot(h, wg_ref[:, c0:c1])
            u = _dot(h, wu_ref[:, c0:c1])
            gu_ref[0, :, c0:c1] = g.astype(BF16)
            gu_ref[1, :, c0:c1] = u.astype(BF16)
            a_ref[:, c0:c1] = (g * _sigmoid(g) * u).astype(BF16)

    return pl.pallas_call(
        body, name=f"{tag}_gu_act", grid=(T // tm, nj),
        in_specs=[pl.BlockSpec((tm, D_MODEL), lambda i, j: (i, 0)),
                  pl.BlockSpec((D_MODEL, tn), lambda i, j: (0, j)),
                  pl.BlockSpec((D_MODEL, tn), lambda i, j: (0, j + nj))],
        out_specs=[pl.BlockSpec((2, tm, tn), lambda i, j: (0, i, j)),
                   pl.BlockSpec((tm, tn), lambda i, j: (i, j))],
        out_shape=[jax.ShapeDtypeStruct((2, T, D_FF), BF16), jax.ShapeDtypeStruct((T, D_FF), BF16)],
        compiler_params=_cparams(("parallel", "parallel")),
    )(h, w_gu, w_gu)


def _ffn_da_actbwd(do, w_down, gu, tag, tie=None):
    T = do.shape[0]
    tm, tn = _tile(T, FFN_TM), FFN_TN
    ties = [] if tie is None else [tie]

    def body(do_ref, wd_ref, gu_ref, *rest):
        dgu_ref = rest[-1]
        do = do_ref[...]
        for c0, c1 in _col_chunks(tn):
            da = _dot(do, wd_ref[c0:c1, :], "nt")
            g = gu_ref[0, :, c0:c1].astype(F32)
            u = gu_ref[1, :, c0:c1].astype(F32)
            s = _sigmoid(g)
            dgu_ref[0, :, c0:c1] = (da * u * s * (1.0 + g * (1.0 - s))).astype(BF16)
            dgu_ref[1, :, c0:c1] = (da * g * s).astype(BF16)

    return pl.pallas_call(
        body, name=f"{tag}_da_actbwd", grid=(T // tm, D_FF // tn),
        in_specs=[pl.BlockSpec((tm, D_MODEL), lambda i, j: (i, 0)),
                  pl.BlockSpec((tn, D_MODEL), lambda i, j: (j, 0)),
                  pl.BlockSpec((2, tm, tn), lambda i, j: (0, i, j))]
        + [pl.BlockSpec(t.shape, lambda i, j: (0, 0)) for t in ties],
        out_specs=pl.BlockSpec((2, tm, tn), lambda i, j: (0, i, j)),
        out_shape=jax.ShapeDtypeStruct((2, T, D_FF), BF16),
        compiler_params=_cparams(("parallel", "parallel")),
    )(do, w_down, gu, *ties)


def _ffn_dwgu(ht, dgu, tag, tk=2048):
    T = ht.shape[1]
    tn, tk = FFN_TN, _tile(T, tk)
    nj, nk = D_FF // tn, T // tk

    def body(a_ref, b_ref, o_ref, acc_ref):
        k = pl.program_id(1)
        p = _dot(a_ref[...], b_ref[...])

        @pl.when(k == 0)
        def _():
            acc_ref[...] = p

        @pl.when(k > 0)
        def _():
            acc_ref[...] += p

        @pl.when(k == nk - 1)
        def _():
            o_ref[...] = acc_ref[...].astype(o_ref.dtype)

    return pl.pallas_call(
        body, name=f"{tag}_dwgu", grid=(2 * nj, nk),
        in_specs=[pl.BlockSpec((D_MODEL, tk), lambda n, k: (0, k)),
                  pl.BlockSpec((None, tk, tn), lambda n, k: (n // nj, k, n % nj))],
        out_specs=pl.BlockSpec((D_MODEL, tn), lambda n, k: (0, n)),
        out_shape=jax.ShapeDtypeStruct((D_MODEL, 2 * D_FF), BF16),
        scratch_shapes=[pltpu.VMEM((D_MODEL, tn), F32)],
        compiler_params=_cparams(("parallel", "arbitrary")),
    )(ht, dgu)


def _ffn_dh(dgu, w_gu, tag, tm=2048, tie=None):
    T = dgu.shape[1]
    tm, tk = _tile(T, tm), FFN_TN
    nk = D_FF // tk
    ties = [] if tie is None else [tie]

    def body(a_ref, b_ref, *rest):
        o_ref, acc_ref = rest[len(ties):]
        k = pl.program_id(1)
        p = _dot(a_ref[...], b_ref[...], "nt")

        @pl.when(k == 0)
        def _():
            acc_ref[...] = p

        @pl.when(k > 0)
        def _():
            acc_ref[...] += p

        @pl.when(k == 2 * nk - 1)
        def _():
            o_ref[...] = acc_ref[...].astype(o_ref.dtype)

    return pl.pallas_call(
        body, name=f"{tag}_dh", grid=(T // tm, 2 * nk),
        in_specs=[pl.BlockSpec((None, tm, tk), lambda i, k: (k // nk, i, k % nk)),
                  pl.BlockSpec((D_MODEL, tk), lambda i, k: (0, k))]
        + [pl.BlockSpec(t.shape, lambda i, k: (0, 0)) for t in ties],
        out_specs=pl.BlockSpec((tm, D_MODEL), lambda i, k: (i, 0)),
        out_shape=jax.ShapeDtypeStruct((T, D_MODEL), BF16),
        scratch_shapes=[pltpu.VMEM((tm, D_MODEL), F32)],
        compiler_params=_cparams(("parallel", "arbitrary")),
    )(dgu, w_gu, *ties)


def _ffn_fwd(h, w_gu, w_down, tag):
    gu, a = _ffn_gu_act(h, w_gu, tag)
    if callable(w_down):
        w_down = w_down(a)
    o = _mm(a, w_down, "nn", BF16, f"{tag}_down", tm=1024, tn=1024, tk=2816)
    return gu, a, o


def _ffn_bwd(do, dot, ht, gu, a, w_gu, w_down, tag, tie=None, on_dw=None):
    on_dw = on_dw or (lambda which, dw: None)
    dw_down = _mm_t(dot, a, f"{tag}_dwdown", tm=1024, tn=1408, tk=2048, tie=tie).T
    dgu = _ffn_da_actbwd(do, w_down, gu, tag, tie=on_dw("down", dw_down))
    dw_gu = _ffn_dwgu(ht, dgu, tag)
    dh = _ffn_dh(dgu, w_gu, tag, tie=on_dw("gu", dw_gu))
    return dh, dw_gu, dw_down


def _sg_common(u_pre, v_pre, ln_g, ln_b):
    u = _gelu(u_pre)
    v = _gelu(v_pre)
    mu = jnp.mean(v, axis=-1, keepdims=True)
    vc = v - mu
    rstd = lax.rsqrt(jnp.mean(vc * vc, axis=-1, keepdims=True) + EPS)
    vhat = vc * rstd
    vl = vhat * ln_g + ln_b
    return u, vhat, rstd, vl


def _sg_masked_pairs(w):
    t = lax.broadcasted_iota(jnp.int32, (CHUNK, CHUNK), 0)
    s = lax.broadcasted_iota(jnp.int32, (CHUNK, CHUNK), 1)
    causal = s <= t
    wm = [jnp.where(causal, w[g], 0.0).astype(BF16) for g in range(SG_GROUPS)]
    return [jnp.concatenate([wm[2 * j], wm[2 * j + 1]], axis=0) for j in range(SG_GROUPS // 2)], causal


def _sg_mix(vl, pairs, bias):
    tr = vl.shape[0]
    low = lax.broadcasted_iota(jnp.int32, (CHUNK, LANES), 1) < SG_GROUP_DIM
    vb = vl.astype(BF16)
    rows = []
    for c in range(tr // CHUNK):
        slabs = []
        for j in range(SG_GROUPS // 2):
            slab = vb[c * CHUNK:(c + 1) * CHUNK, j * LANES:(j + 1) * LANES]
            m = _dot(pairs[j], slab)
            slabs.append(jnp.where(low, m[:CHUNK], m[CHUNK:]))
        rows.append(jnp.concatenate(slabs, axis=1) + bias)
    return jnp.concatenate(rows, axis=0)


def _sg_fwd(z, ln_g, ln_b, sg_w, bias_full):
    def fn(u_pre, v_pre, ln_g, ln_b, w, bias):
        u, _, _, vl = _sg_common(u_pre, v_pre, ln_g, ln_b)
        pairs, _ = _sg_masked_pairs(w)
        y = u * _sg_mix(vl, pairs, bias)
        return y, y

    return _rowwise(fn, "sg_fwd", 512, [(z, SG_WIDTH, Z_U // SG_WIDTH), (z, SG_WIDTH, Z_V // SG_WIDTH)],
                    [ln_g, ln_b, sg_w, bias_full], [(SG_WIDTH, BF16), (SG_WIDTH, BF16, "T")])


def _sg_bwd(z, dy, ln_g, ln_b, sg_w, bias_full, group_ind, dz):
    def fn(u_pre, v_pre, dy, ln_g, ln_b, w, bias, ind):
        dy = dy.astype(F32)
        u, vhat, rstd, vl = _sg_common(u_pre, v_pre, ln_g, ln_b)
        pairs, causal = _sg_masked_pairs(w)
        mixed = _sg_mix(vl, pairs, bias)
        du_pre = dy * mixed * _gelu_grad(u_pre)
        dmix = dy * u
        tr = dy.shape[0]
        low = lax.broadcasted_iota(jnp.int32, (CHUNK, LANES), 1) < SG_GROUP_DIM
        vb = vl.astype(BF16)
        dw = [jnp.zeros((CHUNK, CHUNK), F32) for _ in range(SG_GROUPS)]
        dbias = jnp.zeros((CHUNK, SG_WIDTH), F32)
        dvl_rows = []
        for c in range(tr // CHUNK):
            dm_c = dmix[c * CHUNK:(c + 1) * CHUNK]
            dbias = dbias + dm_c
            slabs = []
            for j in range(SG_GROUPS // 2):
                slab = vb[c * CHUNK:(c + 1) * CHUNK, j * LANES:(j + 1) * LANES]
                dm = dm_c[:, j * LANES:(j + 1) * LANES]
                d0 = jnp.where(low, dm, 0.0).astype(BF16)
                d1 = jnp.where(low, 0.0, dm).astype(BF16)
                dw[2 * j] = dw[2 * j] + _dot(d0, slab, "nt")
                dw[2 * j + 1] = dw[2 * j + 1] + _dot(d1, slab, "nt")
                slabs.append(_dot(pairs[j], jnp.concatenate([d0, d1], axis=0), "tn"))
            dvl_rows.append(jnp.concatenate(slabs, axis=1))
        dvl = jnp.concatenate(dvl_rows, axis=0)
        dln_g = _rsum(dvl * vhat)
        dln_b = _rsum(dvl)
        dvh = dvl * ln_g
        dv = rstd * (dvh - jnp.mean(dvh, axis=-1, keepdims=True)
                     - vhat * jnp.mean(dvh * vhat, axis=-1, keepdims=True))
        dv_pre = dv * _gelu_grad(v_pre)
        dw = jnp.stack([jnp.where(causal, d, 0.0) for d in dw], axis=0)
        dbias_t = lax.dot_general(dbias, ind, (((1,), (0,)), ((), ())), precision=lax.Precision.HIGHEST,
                                  preferred_element_type=F32)
        return jnp.concatenate([du_pre, dv_pre], axis=1), dw, dbias_t, dln_g, dln_b

    return _rowwise(fn, "sg_bwd", 512,
                    [(z, SG_WIDTH, Z_U // SG_WIDTH), (z, SG_WIDTH, Z_V // SG_WIDTH), dy],
                    [ln_g, ln_b, sg_w, bias_full, group_ind],
                    [("into", dz, 2 * SG_WIDTH, Z_U // (2 * SG_WIDTH))],
                    [((SG_GROUPS, CHUNK, CHUNK), F32), ((CHUNK, SG_GROUPS), F32), ((1, SG_WIDTH), F32), ((1, SG_WIDTH), F32)])


MLA_POST_ROWS = 512


def _rope(x, c, s1, s2):
    return x * c + pltpu.roll(x, LANES - MLA_ROPE // 2, 1) * s1 + pltpu.roll(x, MLA_ROPE // 2, 1) * s2


def _rope_t(d, c, s1, s2):
    return d * c + pltpu.roll(d * s1, MLA_ROPE // 2, 1) + pltpu.roll(d * s2, LANES - MLA_ROPE // 2, 1)


def _mla_post(q_pre, kv_pre, z, tabs, gq, gk):
    scale = MLA_QK ** -0.5 * LOG2E
    T = q_pre.shape[0]
    tr = _tile(T, MLA_POST_ROWS)

    def body(q_ref, k_ref, v_ref, kr_ref, c_ref, s1_ref, s2_ref, gq_ref, gk_ref, qo_ref, ko_ref, vo_ref):
        kr = kr_ref[...].astype(F32)
        c, s1, s2, gq, gk = c_ref[...], s1_ref[...], s2_ref[...], gq_ref[...], gk_ref[...]
        ones_lane = lax.broadcasted_iota(jnp.int32, (tr, LANES), 1) == ONES_LANE
        for h in range(MLA_HEADS):
            sl = slice(h * LANES, (h + 1) * LANES)
            qo_ref[:, sl] = (_rope(_rms(q_ref[:, sl].astype(F32), gq, MLA_QK), c, s1, s2) * scale).astype(BF16)
            ko_ref[:, sl] = _rope(_rms(k_ref[:, sl].astype(F32) + kr, gk, MLA_QK), c, s1, s2).astype(BF16)
            vo_ref[:, sl] = jnp.where(ones_lane, 1.0, v_ref[:, sl].astype(F32)).astype(BF16)

    wide = lambda cb: pl.BlockSpec((tr, HP), lambda i, cb=cb: (i, cb))
    lanes = lambda cb: pl.BlockSpec((tr, LANES), lambda i, cb=cb: (i, cb))
    gain = pl.BlockSpec((1, LANES), lambda i: (0, 0))
    return pl.pallas_call(
        body, name="mla_post", grid=(T // tr,),
        in_specs=[wide(0), wide(0), wide(1), lanes(Z_KR // LANES), lanes(0), lanes(0), lanes(0), gain, gain],
        out_specs=[wide(0)] * 3, out_shape=[jax.ShapeDtypeStruct((T, HP), BF16)] * 3,
        compiler_params=_cparams(("parallel",)),
    )(q_pre, kv_pre, kv_pre, z, *tabs, gq, gk)


def _mla_post_bwd(q_pre, kv_pre, z, tabs, gq, gk, dq, dk, dv):
    scale = MLA_QK ** -0.5
    T = q_pre.shape[0]
    tr = _tile(T, MLA_POST_ROWS)

    def body(q_ref, k_ref, kr_ref, c_ref, s1_ref, s2_ref, dq_ref, dk_ref, dv_ref, gq_ref, gk_ref,
             dqo_ref, dkvo_ref, dkro_ref, dgq_ref, dgk_ref):
        kr = kr_ref[...].astype(F32)
        c, s1, s2, gq, gk = c_ref[...], s1_ref[...], s2_ref[...], gq_ref[...], gk_ref[...]
        lane = lax.broadcasted_iota(jnp.int32, (1, LANES), 1)
        kr_mask = (lane >= KR_LANE) & (lane < KR_LANE + MLA_ROPE)
        dgq = jnp.zeros((1, LANES), F32)
        dgk = jnp.zeros((1, LANES), F32)
        dkr = jnp.zeros((tr, LANES), F32)
        for h in range(MLA_HEADS):
            sl = slice(h * LANES, (h + 1) * LANES)
            dqn = _rope_t(dq_ref[:, sl].astype(F32), c, s1, s2) * scale
            dx, dg = _rms_bwd(q_ref[:, sl].astype(F32), gq, dqn, MLA_QK)
            dqo_ref[:, sl] = dx.astype(BF16)
            dgq = dgq + dg
            dkn = _rope_t(dk_ref[:, sl].astype(F32), c, s1, s2)
            dx, dg = _rms_bwd(k_ref[:, sl].astype(F32) + kr, gk, dkn, MLA_QK)
            dkvo_ref[:, sl] = dx.astype(BF16)
            dkvo_ref[:, HP + h * LANES:HP + (h + 1) * LANES] = dv_ref[:, sl]
            dgk = dgk + dg
            dkr = dkr + dx
        dkro_ref[...] = jnp.where(kr_mask, dkr, 0.0).astype(BF16)
        i = pl.program_id(0)

        @pl.when(i == 0)
        def _():
            dgq_ref[...] = dgq
            dgk_ref[...] = dgk

        @pl.when(i > 0)
        def _():
            dgq_ref[...] += dgq
            dgk_ref[...] += dgk

    wide = lambda cb: pl.BlockSpec((tr, HP), lambda i, cb=cb: (i, cb))
    lanes = lambda cb: pl.BlockSpec((tr, LANES), lambda i, cb=cb: (i, cb))
    gain = pl.BlockSpec((1, LANES), lambda i: (0, 0))
    return pl.pallas_call(
        body, name="mla_post_bwd", grid=(T // tr,),
        in_specs=[wide(0), wide(0), lanes(Z_KR // LANES), lanes(0), lanes(0), lanes(0), wide(0), wide(0), wide(0),
                  gain, gain],
        out_specs=[wide(0), pl.BlockSpec((tr, 2 * HP), lambda i: (i, 0)), lanes(0), gain, gain],
        out_shape=[jax.ShapeDtypeStruct((T, HP), BF16), jax.ShapeDtypeStruct((T, 2 * HP), BF16),
                   jax.ShapeDtypeStruct((T, LANES), BF16), jax.ShapeDtypeStruct((1, LANES), F32),
                   jax.ShapeDtypeStruct((1, LANES), F32)],
        compiler_params=_cparams(("arbitrary",)),
    )(q_pre, kv_pre, z, *tabs, dq, dk, dv, gq, gk)


def _pairs(n, lower):
    a, b = [], []
    for o in range(n):
        inner = range(o + 1) if lower else range(o, n)
        for t in inner:
            a.append(o)
            b.append(t)
    return jnp.asarray(np.array(a, np.int32)), jnp.asarray(np.array(b, np.int32))


FLASH_TILE, FLASH_SUB_ROWS = 2048, 512
LOG2E, LN2 = 1.4426950408889634, 0.6931471805599453
ONES_LANE = MLA_V


def _flash_tiles(T):
    tq = _tile(T, FLASH_TILE)
    return tq, _tile(tq, FLASH_SUB_ROWS)


def _col_span(t, sr, rb, diag, key_major):
    if not diag:
        return 0, t
    return (rb * sr, t) if key_major else (0, (rb + 1) * sr)


def _span_iota(sr, rb, c0, c1):
    r = lax.broadcasted_iota(jnp.int32, (sr, c1 - c0), 0) + rb * sr
    c = lax.broadcasted_iota(jnp.int32, (sr, c1 - c0), 1) + c0
    return r, c


def _lanes(x, width):
    return jnp.concatenate([x] * (width // LANES), axis=1)


def _flash_fwd(q, k, v):
    T = q.shape[0]
    tq, sr = _flash_tiles(T)
    n = T // tq
    ii, jj = _pairs(n, True)

    def body(ii_ref, jj_ref, q_ref, k_ref, v_ref, o_ref, ot_ref, lse_ref, lset_ref, m_sc, acc_sc):
        p_ = pl.program_id(1)
        i, j = ii_ref[p_], jj_ref[p_]

        @pl.when(j == 0)
        def _():
            m_sc[...] = jnp.full(m_sc.shape, NEG, F32)
            acc_sc[...] = jnp.zeros(acc_sc.shape, F32)

        def tile(diag):
            nrb = tq // sr

            def scores(rb):
                c0, c1 = _col_span(tq, sr, rb, diag, False)
                return _dot(q_ref[rb * sr:(rb + 1) * sr, :], k_ref[c0:c1, :], "nt")

            s_next = scores(0)
            for rb in range(nrb):
                rows = slice(rb * sr, (rb + 1) * sr)
                c0, c1 = _col_span(tq, sr, rb, diag, False)
                s, s_next = s_next, (scores(rb + 1) if rb + 1 < nrb else None)
                if diag:
                    r, c = _span_iota(sr, rb, c0, c1)
                    s = jnp.where(c <= r, s, NEG)
                m = m_sc[rows, :]
                m_new = jnp.maximum(m, jnp.max(s, axis=1, keepdims=True))
                p = jnp.exp2(s - _lanes(m_new, c1 - c0))
                acc_sc[rows, :] = jnp.exp2(m - m_new) * acc_sc[rows, :] + _dot(p, v_ref[c0:c1, :])
                m_sc[rows, :] = m_new

        @pl.when(j < i)
        def _():
            tile(False)

        @pl.when(j == i)
        def _():
            tile(True)
            acc = acc_sc[...]
            lane = lax.broadcasted_iota(jnp.int32, acc.shape, 1)
            l = jnp.sum(jnp.where(lane == ONES_LANE, acc, 0.0), axis=1, keepdims=True)
            o = jnp.where(lane < MLA_V, acc / l, 0.0)
            o_ref[...] = o.astype(o_ref.dtype)
            ot_ref[...] = o.T.astype(ot_ref.dtype)
            lse = m_sc[...] + jnp.log2(l)
            lse_ref[...] = lse
            lset_ref[...] = lse.T[:8]

    blk = lambda which: pl.BlockSpec((tq, LANES), which)
    qmap = lambda h, p, ii, jj: (ii[p], h)
    kmap = lambda h, p, ii, jj: (jj[p], h)
    tmap = lambda h, p, ii, jj: (h, ii[p])
    return pl.pallas_call(
        body, name="mla_flash_fwd",
        grid_spec=pltpu.PrefetchScalarGridSpec(
            num_scalar_prefetch=2, grid=(MLA_HEADS, int(ii.shape[0])),
            in_specs=[blk(qmap), blk(kmap), blk(kmap)],
            out_specs=[blk(qmap), pl.BlockSpec((LANES, tq), tmap), blk(qmap), pl.BlockSpec((8, tq), tmap)],
            scratch_shapes=[pltpu.VMEM((tq, LANES), F32)] * 2),
        out_shape=[jax.ShapeDtypeStruct((T, HP), BF16), jax.ShapeDtypeStruct((HP, T), BF16),
                   jax.ShapeDtypeStruct((T, HP), F32), jax.ShapeDtypeStruct((8 * MLA_HEADS, T), F32)],
        compiler_params=_cparams(("parallel", "arbitrary")),
    )(ii, jj, q, k, v)


def _flash_dq(q, k, v, do, lse, delta):
    T = q.shape[0]
    tq, sr = _flash_tiles(T)
    n = T // tq
    ii, jj = _pairs(n, True)

    def body(ii_ref, jj_ref, q_ref, k_ref, v_ref, do_ref, lse_ref, dl_ref, dq_ref, acc_sc):
        p_ = pl.program_id(1)
        i, j = ii_ref[p_], jj_ref[p_]

        @pl.when(j == 0)
        def _():
            acc_sc[...] = jnp.zeros(acc_sc.shape, F32)

        def tile(diag):
            nrb = tq // sr

            def products(rb):
                rows = slice(rb * sr, (rb + 1) * sr)
                c0, c1 = _col_span(tq, sr, rb, diag, False)
                return _dot(q_ref[rows, :], k_ref[c0:c1, :], "nt"), _dot(do_ref[rows, :], v_ref[c0:c1, :], "nt")

            nxt = products(0)
            for rb in range(nrb):
                rows = slice(rb * sr, (rb + 1) * sr)
                c0, c1 = _col_span(tq, sr, rb, diag, False)
                (s, dp), nxt = nxt, (products(rb + 1) if rb + 1 < nrb else None)
                p = jnp.exp2(s - _lanes(lse_ref[rows, :], c1 - c0))
                if diag:
                    r, c = _span_iota(sr, rb, c0, c1)
                    p = jnp.where(c <= r, p, 0.0)
                acc_sc[rows, :] += _dot(p * (dp - _lanes(dl_ref[rows, :], c1 - c0)), k_ref[c0:c1, :])

        @pl.when(j < i)
        def _():
            tile(False)

        @pl.when(j == i)
        def _():
            tile(True)
            dq_ref[...] = acc_sc[...].astype(dq_ref.dtype)

    blk = lambda which: pl.BlockSpec((tq, LANES), which)
    qmap = lambda h, p, ii, jj: (ii[p], h)
    kmap = lambda h, p, ii, jj: (jj[p], h)
    return pl.pallas_call(
        body, name="mla_flash_dq",
        grid_spec=pltpu.PrefetchScalarGridSpec(
            num_scalar_prefetch=2, grid=(MLA_HEADS, int(ii.shape[0])),
            in_specs=[blk(qmap), blk(kmap), blk(kmap), blk(qmap), blk(qmap), blk(qmap)],
            out_specs=blk(qmap),
            scratch_shapes=[pltpu.VMEM((tq, LANES), F32)]),
        out_shape=jax.ShapeDtypeStruct((T, HP), BF16),
        compiler_params=_cparams(("parallel", "arbitrary")),
    )(ii, jj, q, k, v, do, lse, delta)


def _flash_dkv(q, k, v, do, lse_t, delta_t):
    T = q.shape[0]
    tq, sr = _flash_tiles(T)
    n = T // tq
    jj, ii = _pairs(n, False)

    def body(jj_ref, ii_ref, q_ref, k_ref, v_ref, do_ref, lse_ref, dl_ref, dk_ref, dv_ref, dk_sc, dv_sc):
        p_ = pl.program_id(1)
        j, i = jj_ref[p_], ii_ref[p_]

        @pl.when(i == j)
        def _():
            dk_sc[...] = jnp.zeros(dk_sc.shape, F32)
            dv_sc[...] = jnp.zeros(dv_sc.shape, F32)

        def tile(diag):
            nrb = tq // sr

            def products(rb):
                rows = slice(rb * sr, (rb + 1) * sr)
                c0, c1 = _col_span(tq, sr, rb, diag, True)
                return _dot(k_ref[rows, :], q_ref[c0:c1, :], "nt"), _dot(v_ref[rows, :], do_ref[c0:c1, :], "nt")

            nxt = products(0)
            for rb in range(nrb):
                rows = slice(rb * sr, (rb + 1) * sr)
                c0, c1 = _col_span(tq, sr, rb, diag, True)
                (st, dpt), nxt = nxt, (products(rb + 1) if rb + 1 < nrb else None)
                pt = jnp.exp2(st - lse_ref[:1, c0:c1])
                if diag:
                    r, c = _span_iota(sr, rb, c0, c1)
                    pt = jnp.where(r <= c, pt, 0.0)
                dv_sc[rows, :] += _dot(pt, do_ref[c0:c1, :])
                dk_sc[rows, :] += _dot(pt * (dpt - dl_ref[:1, c0:c1]), q_ref[c0:c1, :])

        @pl.when(i == j)
        def _():
            tile(True)

        @pl.when(i > j)
        def _():
            tile(False)

        @pl.when(i == n - 1)
        def _():
            dk_ref[...] = (dk_sc[...] * LN2).astype(dk_ref.dtype)
            dv_ref[...] = dv_sc[...].astype(dv_ref.dtype)

    blk = lambda which: pl.BlockSpec((tq, LANES), which)
    qmap = lambda h, p, jj, ii: (ii[p], h)
    kmap = lambda h, p, jj, ii: (jj[p], h)
    lse_rows = pl.BlockSpec((8, tq), lambda h, p, jj, ii: (h, ii[p]))
    delta_rows = pl.BlockSpec((8, tq), lambda h, p, jj, ii: (h * (LANES // 8), ii[p]))
    return pl.pallas_call(
        body, name="mla_flash_dkv",
        grid_spec=pltpu.PrefetchScalarGridSpec(
            num_scalar_prefetch=2, grid=(MLA_HEADS, int(ii.shape[0])),
            in_specs=[blk(qmap), blk(kmap), blk(kmap), blk(qmap), lse_rows, delta_rows],
            out_specs=[blk(kmap), blk(kmap)],
            scratch_shapes=[pltpu.VMEM((tq, LANES), F32)] * 2),
        out_shape=[jax.ShapeDtypeStruct((T, HP), BF16)] * 2,
        compiler_params=_cparams(("parallel", "arbitrary")),
    )(jj, ii, q, k, v, do, lse_t, delta_t)


def _mem_fwd(z, km, vm, gq):
    scale = MEM_HEAD_DIM ** -0.5

    def fn(qm, km, vm, gq):
        ys = []
        for h in range(MEM_HEADS):
            sl = slice(h * LANES, (h + 1) * LANES)
            q = _rms(qm[:, sl], gq) * scale
            s = _dot(q, km[:, sl], "nt")
            p = jnp.exp(s - jnp.max(s, axis=1, keepdims=True))
            p = p / jnp.sum(p, axis=1, keepdims=True)
            ys.append(_dot(p, vm[:, sl]))
        y = jnp.concatenate(ys, axis=1)
        return y, y

    return _rowwise(fn, "mem_fwd", 512, [(z, MEM_WIDTH, Z_QM // MEM_WIDTH)], [km, vm, gq],
                    [(MEM_WIDTH, BF16), (MEM_WIDTH, BF16, "T")])


def _mem_bwd(z, dy, km, vm, gq, dz):
    scale = MEM_HEAD_DIM ** -0.5

    def fn(qm, dy, km, vm, gq):
        dqs, dks, dvs = [], [], []
        dgq = jnp.zeros((1, LANES), F32)
        for h in range(MEM_HEADS):
            sl = slice(h * LANES, (h + 1) * LANES)
            q = (_rms(qm[:, sl], gq) * scale).astype(BF16)
            dyh = dy[:, sl]
            kh, vh = km[:, sl], vm[:, sl]
            s = _dot(q, kh, "nt")
            p = jnp.exp(s - jnp.max(s, axis=1, keepdims=True))
            p = p / jnp.sum(p, axis=1, keepdims=True)
            dp = _dot(dyh, vh, "nt")
            ds = p * (dp - jnp.sum(p * dp, axis=1, keepdims=True))
            dq = _dot(ds, kh) * scale
            dx, dg = _rms_bwd(qm[:, sl], gq, dq)
            dqs.append(dx)
            dgq = dgq + dg
            st = _dot(kh, q, "nt")
            pt = jnp.exp(st - jnp.max(st, axis=0, keepdims=True))
            pt = pt / jnp.sum(pt, axis=0, keepdims=True)
            dpt = _dot(vh, dyh, "nt")
            dst = pt * (dpt - jnp.sum(pt * dpt, axis=0, keepdims=True))
            dvs.append(_dot(pt, dyh))
            dks.append(_dot(dst, q))
        return jnp.concatenate(dqs, axis=1), jnp.concatenate(dks, axis=1), jnp.concatenate(dvs, axis=1), dgq

    m = km.shape[0]
    return _rowwise(fn, "mem_bwd", 512, [(z, MEM_WIDTH, Z_QM // MEM_WIDTH), dy], [km, vm, gq],
                    [("into", dz, MEM_WIDTH, Z_QM // MEM_WIDTH)],
                    [((m, MEM_WIDTH), F32), ((m, MEM_WIDTH), F32), ((1, LANES), F32)])


GROUPS = {"ffn1": ["ffn1_w_gu"], "ffn1_down": ["ffn1_w_down"],
          "mix": ["w_in", "mla_w_uq", "mla_w_ukv", "mem_w_kv", "w_branch_a", "w_branch_b", "w_branch_c", "w_out"],
          "ffn2": ["ffn2_w_gu", "ffn2_w_down"]}
GRAD_GROUPS = {"ffn2": GROUPS["ffn2"], "mix": GROUPS["mix"], "ffn1_down": ["ffn1_w_down"], "ffn1_gu": ["ffn1_w_gu"]}


def _local_step(x, mem, positions, loss_target, P, weights, grads_out):
    T = x.shape[0]
    G = {}
    W = dict(weights("ffn1", None))

    half = MLA_ROPE // 2
    inv = ROPE_BASE ** (-jnp.arange(half, dtype=F32) / half)
    ang = positions.astype(F32)[:, None] * inv
    cos, sin = jnp.cos(ang), jnp.sin(ang)
    one, zero = jnp.ones((T, MLA_NOPE), F32), jnp.zeros((T, half), F32)
    pad = LANES - MLA_QK
    tabs = (jnp.concatenate([one, cos, cos, jnp.ones((T, pad), F32)], axis=1),
            jnp.concatenate([jnp.zeros((T, MLA_NOPE), F32), -sin, zero, jnp.zeros((T, pad), F32)], axis=1),
            jnp.concatenate([jnp.zeros((T, MLA_NOPE), F32), zero, sin, jnp.zeros((T, pad), F32)], axis=1))
    gq_p = jnp.pad(P["mla_q_norm"], ((0, 0), (0, pad)))
    gk_p = jnp.pad(P["mla_k_norm"], ((0, 0), (0, pad)))
    bias_full = jnp.repeat(P["sg_b"].T, SG_GROUP_DIM, axis=1)
    group_ind = jnp.repeat(jnp.eye(SG_GROUPS, dtype=F32), SG_GROUP_DIM, axis=0)

    HT = (D_MODEL, BF16, "T")

    def norm2(x, g):
        h = _rms(x, g)
        return h, h

    h1, h1t = _rowwise(norm2, "ffn1_norm", 512, [x], [P["ffn1_norm"]], [(D_MODEL, BF16), HT])
    def ffn1_w_down(after):
        W.update(weights("ffn1_down", after))
        return W["ffn1_w_down"]

    gu1, a1t, o1 = _ffn_fwd(h1, W["ffn1_w_gu"], ffn1_w_down, "ffn1")

    def resid_norm(x, o, g):
        xn = x + 0.5 * o
        h = _rms(xn, g)
        return xn, h, h

    x1, hm, hmt = _rowwise(resid_norm, "mix_norm", 512, [x, o1], [P["mix_norm"]],
                           [(D_MODEL, F32), (D_MODEL, BF16), HT])
    W.update(weights("mix", hm))
    z = _mm(hm, W["w_in"], "nn", BF16, "w_in", tm=1024, tn=1792)

    y_a, y_at = _sg_fwd(z, P["sg_ln_g"], P["sg_ln_b"], P["sg_w"], bias_full)

    def c_norm(cq, ckv, gq, gkv):
        a, b = _rms(cq, gq), _rms(ckv, gkv)
        return a, b, a, b

    cqn, ckvn, cqnt, ckvnt = _rowwise(
        c_norm, "mla_cnorm", 512, [(z, MLA_Q_RANK, Z_CQ // MLA_Q_RANK), (z, MLA_KV_RANK, Z_CKV // MLA_KV_RANK)],
        [P["mla_cq_norm"], P["mla_ckv_norm"]],
        [(MLA_Q_RANK, BF16), (MLA_KV_RANK, BF16), (MLA_Q_RANK, BF16, "T"), (MLA_KV_RANK, BF16, "T")])
    q_pre = _mm(cqn, W["mla_w_uq"], "nn", BF16, "mla_uq", tm=1024, tn=1024)
    kv_pre = _mm(ckvn, W["mla_w_ukv"], "nn", BF16, "mla_ukv", tm=1024, tn=1024)
    q, k, v = _mla_post(q_pre, kv_pre, z, tabs, gq_p, gk_p)
    y_b, y_bt, lse, lse_t = _flash_fwd(q, k, v)

    memn, = _rowwise(lambda m, g: _rms(m, g), "mem_norm", 256, [mem], [P["mem_norm"]], [(D_MODEL, BF16)])
    kvm = _mm(memn, W["mem_w_kv"], "nn", F32, "mem_kv")

    def mem_k(kvm, gk):
        ks = [_rms(kvm[:, h * LANES:(h + 1) * LANES], gk) for h in range(MEM_HEADS)]
        return jnp.concatenate(ks, axis=1), kvm[:, MEM_WIDTH:]

    km, vm = _rowwise(mem_k, "mem_knorm", 256, [kvm], [P["mem_k_norm"]], [(MEM_WIDTH, BF16), (MEM_WIDTH, BF16)])
    y_c, y_ct = _mem_fwd(z, km, vm, P["mem_q_norm"])

    pa = _mm(y_a, W["w_branch_a"], "nn", BF16, "branch_a", tm=1024, tn=1024)
    pb = _mm(y_b, W["w_branch_b"], "nn", BF16, "branch_b", tm=1024, tn=1024)
    pc = _mm(y_c, W["w_branch_c"], "nn", BF16, "branch_c", tm=1024, tn=1024)

    def merge(zg, pa, pb, pc, b):
        g = _sigmoid(zg + b)
        m = g[:, :D_MODEL] * pa + g[:, D_MODEL:2 * D_MODEL] * pb + g[:, 2 * D_MODEL:] * pc
        return m, m

    merged, mergedt = _rowwise(merge, "merge", 256, [(z, 3 * D_MODEL, 0), pa, pb, pc], [P["b_gate"]],
                               [(D_MODEL, BF16), HT])
    om = _mm(merged, W["w_out"], "nn", BF16, "w_out", tm=1024, tn=1024)

    def resid_norm1(x, o, g):
        xn = x + o
        h = _rms(xn, g)
        return xn, h, h

    x2, h2, h2t = _rowwise(resid_norm1, "ffn2_norm", 512, [x1, om], [P["ffn2_norm"]],
                           [(D_MODEL, F32), (D_MODEL, BF16), HT])
    W.update(weights("ffn2", h2))
    gu2, a2t, o2 = _ffn_fwd(h2, W["ffn2_w_gu"], W["ffn2_w_down"], "ffn2")

    def loss_fn(x2, o2, t):
        e = x2 + 0.5 * o2 - t
        do = e * (0.5 / D_MODEL)
        return e * (1.0 / D_MODEL), do, do, _rsum(e * e) * (0.5 / D_MODEL)

    dx3, do2, do2t, loss_part = _rowwise(loss_fn, "loss", 512, [x2, o2, loss_target], [],
                                         [(D_MODEL, F32), (D_MODEL, BF16), HT], [((1, D_MODEL), F32)])

    dh2, G["ffn2_w_gu"], G["ffn2_w_down"] = _ffn_bwd(do2, do2t, h2t, gu2, a2t, W["ffn2_w_gu"], W["ffn2_w_down"],
                                                     "ffn2")
    tie = grads_out("ffn2", G)

    def norm_bwd(x, dh, dxo, g, *_):
        dx, dg = _rms_bwd(x, g, dh)
        dx = dx + dxo
        return dx, dx, dg

    dx2, dx2b, G["ffn2_norm"] = _rowwise(norm_bwd, "ffn2_norm_bwd", 512, [x2, dh2, dx3],
                                         [P["ffn2_norm"]] + ([] if tie is None else [tie]),
                                         [(D_MODEL, F32), (D_MODEL, BF16)], [((1, D_MODEL), F32)])

    G["w_out"] = _mm_t(mergedt, dx2b, "w_out_dw", tm=1024, tn=1024)
    dmerged = _mm(dx2b, W["w_out"], "nt", BF16, "w_out_dx", tm=1024, tn=1024)

    def merge_bwd(zg, pa, pb, pc, dm, b):
        g = _sigmoid(zg + b)
        ps = jnp.concatenate([pa, pb, pc], axis=1)
        dm3 = jnp.concatenate([dm, dm, dm], axis=1)
        dzg = dm3 * ps * g * (1.0 - g)
        dp = dm3 * g
        return dzg, dp[:, :D_MODEL], dp[:, D_MODEL:2 * D_MODEL], dp[:, 2 * D_MODEL:], _rsum(dzg)

    dz = lax.empty((T, Z_COLS), BF16)
    dz, dpa, dpb, dpc, G["b_gate"] = _rowwise(
        merge_bwd, "merge_bwd", 256, [(z, 3 * D_MODEL, 0), pa, pb, pc, dmerged], [P["b_gate"]],
        [("into", dz, 3 * D_MODEL, 0), (D_MODEL, BF16), (D_MODEL, BF16), (D_MODEL, BF16)], [((1, 3 * D_MODEL), F32)])

    G["w_branch_a"] = _mm_t(y_at, dpa, "branch_a_dw", tm=512, tn=1024)
    G["w_branch_b"] = _mm_t(y_bt, dpb, "branch_b_dw", tm=1024, tn=1024)
    G["w_branch_c"] = _mm_t(y_ct, dpc, "branch_c_dw", tm=512, tn=1024)
    dy_a = _mm(dpa, W["w_branch_a"], "nt", BF16, "branch_a_dx", tm=1024, tn=512)
    dy_b = _mm(dpb, W["w_branch_b"], "nt", BF16, "branch_b_dx", tm=1024, tn=1024)
    dy_c = _mm(dpc, W["w_branch_c"], "nt", BF16, "branch_c_dx", tm=1024, tn=512)

    dz, G["sg_w"], dbias_t, G["sg_ln_g"], G["sg_ln_b"] = _sg_bwd(
        z, dy_a, P["sg_ln_g"], P["sg_ln_b"], P["sg_w"], bias_full, group_ind, dz)
    G["sg_b"] = dbias_t.T

    dz, dkm, dvm, G["mem_q_norm"] = _mem_bwd(z, dy_c, km, vm, P["mem_q_norm"], dz)

    def mem_k_bwd(kvm, dkm, dvm, gk):
        dks = []
        dg = jnp.zeros((1, LANES), F32)
        for h in range(MEM_HEADS):
            sl = slice(h * LANES, (h + 1) * LANES)
            dx, d = _rms_bwd(kvm[:, sl], gk, dkm[:, sl])
            dks.append(dx)
            dg = dg + d
        return jnp.concatenate(dks + [dvm], axis=1), dg

    dkvm, G["mem_k_norm"] = _rowwise(mem_k_bwd, "mem_knorm_bwd", 256, [kvm, dkm, dvm], [P["mem_k_norm"]],
                                     [(2 * MEM_WIDTH, BF16)], [((1, LANES), F32)])
    G["mem_w_kv"] = _mm(memn, dkvm, "tn", BF16, "mem_kv_dw")
    dmemn = _mm(dkvm, W["mem_w_kv"], "nt", F32, "mem_kv_dx")
    _, G["mem_norm"] = _rowwise(lambda m, d, g: _rms_bwd(m, g, d), "mem_norm_bwd", 256, [mem, dmemn],
                                [P["mem_norm"]], [(D_MODEL, BF16)], [((1, D_MODEL), F32)])

    def delta_fn(o, do):
        od = o.astype(F32) * do.astype(F32)
        ds = [jnp.broadcast_to(jnp.sum(od[:, h * LANES:(h + 1) * LANES], axis=1, keepdims=True), (od.shape[0], LANES))
              for h in range(MLA_HEADS)]
        d = jnp.concatenate(ds, axis=1)
        return d, d

    delta, delta_t = _rowwise(delta_fn, "mla_delta", 512, [y_b, dy_b], [], [(HP, F32), (HP, F32, "T")])
    dq = _flash_dq(q, k, v, dy_b, lse, delta)
    dk, dv = _flash_dkv(q, k, v, dy_b, lse_t, delta_t)
    dq_pre, dkv_pre, dkr, dgq, dgk = _mla_post_bwd(q_pre, kv_pre, z, tabs, gq_p, gk_p, dq, dk, dv)
    G["mla_q_norm"], G["mla_k_norm"] = dgq[:, :MLA_QK], dgk[:, :MLA_QK]
    G["mla_w_uq"] = _mm_t(cqnt, dq_pre, "mla_uq_dw", tm=384, tn=1024)
    G["mla_w_ukv"] = _mm_t(ckvnt, dkv_pre, "mla_ukv_dw", tm=256, tn=2048)
    dcqn = _mm(dq_pre, W["mla_w_uq"], "nt", BF16, "mla_uq_dx", tm=1024)
    dckvn = _mm(dkv_pre, W["mla_w_ukv"], "nt", BF16, "mla_ukv_dx", tm=1024)

    def c_norm_bwd(cq, ckv, dcqn, dckvn, dkr, gq, gkv):
        dcq, dgq = _rms_bwd(cq, gq, dcqn)
        dckv, dgkv = _rms_bwd(ckv, gkv, dckvn)
        return jnp.concatenate([dckv, dkr, dcq], axis=1), dgq, dgkv

    tail = Z_COLS - Z_CKV
    dz, G["mla_cq_norm"], G["mla_ckv_norm"] = _rowwise(
        c_norm_bwd, "mla_cnorm_bwd", 512,
        [(z, MLA_Q_RANK, Z_CQ // MLA_Q_RANK), (z, MLA_KV_RANK, Z_CKV // MLA_KV_RANK), dcqn, dckvn, dkr],
        [P["mla_cq_norm"], P["mla_ckv_norm"]], [("into", dz, tail, Z_CKV // tail)],
        [((1, MLA_Q_RANK), F32), ((1, MLA_KV_RANK), F32)])
    G["w_in"] = _mm_t(hmt, dz, "w_in_dw", tm=1024, tn=1792, tk=2048)
    dhm = _mm(dz, W["w_in"], "nt", BF16, "w_in_dx", tm=1024, tn=1024, tk=2688)

    def norm_bwd_half(x, dh, dxo, g):
        dx, dg = _rms_bwd(x, g, dh)
        dx = dx + dxo
        return dx, (0.5 * dx), (0.5 * dx), dg

    dx1, do1, do1t, G["mix_norm"] = _rowwise(norm_bwd_half, "mix_norm_bwd", 512, [x1, dhm, dx2], [P["mix_norm"]],
                                             [(D_MODEL, F32), (D_MODEL, BF16), HT], [((1, D_MODEL), F32)])
    tie = grads_out("mix", G)

    def ffn1_dw(which, dw):
        G["ffn1_w_" + which] = dw
        return grads_out("ffn1_" + which, G)

    dh1, _, _ = _ffn_bwd(do1, do1t, h1t, gu1, a1t, W["ffn1_w_gu"], W["ffn1_w_down"], "ffn1", tie, ffn1_dw)

    def norm_bwd_last(x, dh, dxo, g):
        dx, dg = _rms_bwd(x, g, dh)
        return dx + dxo, dg

    grad_x, G["ffn1_norm"] = _rowwise(norm_bwd_last, "ffn1_norm_bwd", 512, [x, dh1, dx1], [P["ffn1_norm"]],
                                      [(D_MODEL, F32)], [((1, D_MODEL), F32)])
    return loss_part, grad_x, G


SHARDED = ["ffn1_w_gu", "ffn1_w_down", "w_in", "mla_w_uq", "mla_w_ukv", "mem_w_kv",
           "w_branch_a", "w_branch_b", "w_branch_c", "w_out", "ffn2_w_gu", "ffn2_w_down"]
ROW_SHARDED = {"ffn1_w_down", "mem_w_kv", "w_out", "ffn2_w_down"}
SMALL = ["ffn1_norm", "mix_norm", "b_gate", "sg_ln_g", "sg_ln_b", "sg_w", "sg_b", "mla_cq_norm",
         "mla_ckv_norm", "mla_q_norm", "mla_k_norm", "mem_norm", "mem_q_norm", "mem_k_norm", "ffn2_norm"]
ORDER = ["ffn1_norm", "ffn1_w_gu", "ffn1_w_down", "mix_norm", "w_in", "b_gate", "sg_ln_g", "sg_ln_b", "sg_w",
         "sg_b", "mla_cq_norm", "mla_w_uq", "mla_ckv_norm", "mla_w_ukv", "mla_q_norm", "mla_k_norm", "mem_norm",
         "mem_w_kv", "mem_q_norm", "mem_k_norm", "w_branch_a", "w_branch_b", "w_branch_c", "w_out", "ffn2_norm",
         "ffn2_w_gu", "ffn2_w_down"]

_IN_U, _IN_V, _IN_CQ, _IN_CKV, _IN_KR, _IN_QM, _IN_G = 0, 512, 1024, 1408, 1664, 1696, 2208
IN_COLS = 5280


def _full_from_slabs(name, slabs):
    n, r, c = slabs.shape
    if name in ROW_SHARDED:
        return slabs.reshape(n * r, c)
    return slabs.transpose(1, 0, 2).reshape(r, n * c)


def _slabs_from_full(name, full):
    if name in ROW_SHARDED:
        return full.reshape(N_DEV, full.shape[0] // N_DEV, full.shape[1])
    r, c = full.shape
    return full.reshape(r, N_DEV, c // N_DEV).transpose(1, 0, 2)


def _compute_layout(full):
    W = dict(full)
    if "w_in" not in full:
        return W
    w = full["w_in"]
    kr = jnp.pad(w[:, _IN_KR:_IN_QM], ((0, 0), (KR_LANE, LANES - KR_LANE - MLA_ROPE)))
    W["w_in"] = jnp.concatenate([w[:, _IN_G:], w[:, _IN_U:_IN_CQ], w[:, _IN_QM:_IN_G], w[:, _IN_CKV:_IN_KR], kr,
                                 w[:, _IN_CQ:_IN_CKV]], axis=1)
    uq = full["mla_w_uq"].reshape(MLA_Q_RANK, MLA_HEADS, MLA_QK)
    W["mla_w_uq"] = jnp.pad(uq, ((0, 0), (0, 0), (0, LANES - MLA_QK))).reshape(MLA_Q_RANK, HP)
    ukv = full["mla_w_ukv"].reshape(MLA_KV_RANK, MLA_HEADS, MLA_NOPE + MLA_V)
    padh = lambda a: jnp.pad(a, ((0, 0), (0, 0), (0, LANES - a.shape[2]))).reshape(MLA_KV_RANK, HP)
    W["mla_w_ukv"] = jnp.concatenate([padh(ukv[:, :, :MLA_NOPE]), padh(ukv[:, :, MLA_NOPE:])], axis=1)
    wb = full["w_branch_b"].reshape(MLA_HEADS, MLA_V, D_MODEL)
    W["w_branch_b"] = jnp.pad(wb, ((0, 0), (0, LANES - MLA_V), (0, 0))).reshape(HP, D_MODEL)
    return W


def _reference_layout(G):
    out = dict(G)
    if "w_in" not in G:
        return out
    g = G["w_in"]
    out["w_in"] = jnp.concatenate([
        g[:, Z_U:Z_QM], g[:, Z_CQ:Z_COLS], g[:, Z_CKV:Z_KR], g[:, Z_KR + KR_LANE:Z_KR + KR_LANE + MLA_ROPE],
        g[:, Z_QM:Z_CKV], g[:, Z_G:Z_U]], axis=1)
    out["mla_w_uq"] = G["mla_w_uq"].reshape(MLA_Q_RANK, MLA_HEADS, LANES)[:, :, :MLA_QK].reshape(MLA_Q_RANK, -1)
    gk = G["mla_w_ukv"][:, :HP].reshape(MLA_KV_RANK, MLA_HEADS, LANES)[:, :, :MLA_NOPE]
    gv = G["mla_w_ukv"][:, HP:].reshape(MLA_KV_RANK, MLA_HEADS, LANES)[:, :, :MLA_V]
    out["mla_w_ukv"] = jnp.concatenate([gk, gv], axis=2).reshape(MLA_KV_RANK, -1)
    out["w_branch_b"] = G["w_branch_b"].reshape(MLA_HEADS, LANES, D_MODEL)[:, :MLA_V].reshape(-1, D_MODEL)
    return out


def _pack(parts):
    flat = []
    for a in parts:
        a = a.reshape(-1)
        flat.append(jnp.pad(a, (0, (-a.shape[0]) % LANES)))
    return jnp.concatenate(flat).reshape(-1, LANES)


def _unpack(packed, shapes):
    flat = packed.reshape(-1)
    out, off = [], 0
    for shp in shapes:
        n = int(np.prod(shp))
        out.append(flat[off:off + n].reshape(shp))
        off += n + (-n) % LANES
    return out


MESH = pl.DeviceIdType.MESH
HBM = pl.BlockSpec(memory_space=pltpu.HBM)


def _all_gather(shards):
    n = len(shards)

    def body(*refs):
        x_refs, out_refs, token_ref = refs[:n], refs[n:2 * n], refs[2 * n]
        send_sems, recv_sems, local_sems = refs[2 * n + 1:]
        x, y, c = lax.axis_index("x"), lax.axis_index("y"), lax.axis_index("c")
        me, sibling = (x, y, c), (x, y, 1 - c)
        chips = [(1 - x, y), (x, 1 - y), (1 - x, 1 - y)]
        token_ref[...] = jnp.zeros_like(token_ref)

        def slot(a, px, py, pc):
            return out_refs[a].at[4 * px + 2 * py + pc]

        def copy(a, k, block, to, src=None):
            return pltpu.make_async_remote_copy(
                src_ref=slot(a, *block) if src is None else src, dst_ref=slot(a, *block),
                send_sem=send_sems.at[7 * a + k], recv_sem=recv_sems.at[7 * a + k], device_id=to, device_id_type=MESH)

        arrays = range(n)
        mine = [pltpu.make_async_copy(x_refs[a], slot(a, *me), local_sems.at[a]) for a in arrays]
        for cp in mine:
            cp.start()
        first = [copy(a, 0, me, sibling, src=x_refs[a]) for a in arrays]
        first += [copy(a, 1 + j, me, (*chip, c), src=x_refs[a]) for j, chip in enumerate(chips) for a in arrays]
        for cp in first:
            cp.start()
        passed = []
        for j, chip in enumerate(chips):
            for a in arrays:
                copy(a, 1 + j, (*chip, c), me).wait_recv()
                passed.append(copy(a, 4 + j, (*chip, c), sibling))
                passed[-1].start()
        for a in arrays:
            copy(a, 0, sibling, me).wait_recv()
        for j, chip in enumerate(chips):
            for a in arrays:
                copy(a, 4 + j, (*chip, 1 - c), me).wait_recv()
        for cp in first + passed:
            cp.wait_send()
        for cp in mine:
            cp.wait()

    res = pl.pallas_call(
        body, name="all_gather_weights",
        out_shape=[jax.ShapeDtypeStruct((N_DEV,) + s.shape, s.dtype) for s in shards]
        + [jax.ShapeDtypeStruct((8, LANES), F32)],
        in_specs=[HBM] * n, out_specs=[HBM] * n + [pl.BlockSpec(memory_space=pltpu.VMEM)],
        scratch_shapes=[pltpu.SemaphoreType.DMA((7 * n,)), pltpu.SemaphoreType.DMA((7 * n,)),
                        pltpu.SemaphoreType.DMA((n,))],
    )(*shards)
    return res[:n], res[n]


SEM = pl.BlockSpec(memory_space=pltpu.SEMAPHORE)
DATAFLOW = pltpu.SideEffectType.DATAFLOW_SIDE_EFFECTING


def _peers():
    x, y, c = lax.axis_index("x"), lax.axis_index("y"), lax.axis_index("c")
    out = []
    for k in range(1, N_DEV):
        px = 1 - x if k & 4 else x
        py = 1 - y if k & 2 else y
        pc = 1 - c if k & 1 else c
        out.append((k, (px, py, pc), 4 * px + 2 * py + pc))
    return 4 * x + 2 * y + c, out


def _send_start(srcs, per_peer, name):
    n = len(srcs)
    lands = [lax.empty((N_DEV,) + (s.shape[1:] if per_peer else s.shape), s.dtype) for s in srcs]

    def body(*refs):
        src_refs, land_refs, send_sems, recv_sems, token = refs[:n], refs[n:2 * n], refs[2 * n], refs[2 * n + 1], refs[-1]
        me, peers = _peers()
        for a in range(n):
            for k, pid, pflat in peers:
                pltpu.make_async_remote_copy(
                    src_ref=src_refs[a].at[pflat] if per_peer else src_refs[a], dst_ref=land_refs[a].at[me],
                    send_sem=send_sems.at[7 * a + k - 1], recv_sem=recv_sems.at[7 * a + k - 1],
                    device_id=pid, device_id_type=MESH).start()
        token[...] = jnp.zeros_like(token)

    hbm = lambda a: pltpu.with_memory_space_constraint(a, pltpu.HBM)
    res = pl.pallas_call(
        body, name=name,
        out_shape=(pltpu.SemaphoreType.DMA((7 * n,)), pltpu.SemaphoreType.DMA((7 * n,)),
                   *[pltpu.HBM(a.shape, a.dtype) for a in srcs + lands], jax.ShapeDtypeStruct((8, LANES), F32)),
        in_specs=(HBM,) * (2 * n), out_specs=(SEM, SEM) + (HBM,) * (2 * n) + (pl.BlockSpec(memory_space=pltpu.VMEM),),
        input_output_aliases={i: 2 + i for i in range(2 * n)},
        compiler_params=pltpu.CompilerParams(has_side_effects=DATAFLOW),
    )(*[hbm(a) for a in srcs + lands])
    return (res[0], res[1], list(res[2:2 + n]), list(res[2 + n:2 + 2 * n])), res[-1]


def _send_wait(started, after, per_peer, name):
    send_sems, recv_sems, srcs_thru, lands_thru = started
    n = len(srcs_thru)

    def body(*refs):
        src_refs, land_refs, send_sems, recv_sems = refs[:n], refs[n:2 * n], refs[2 * n], refs[2 * n + 1]
        me, peers = _peers()
        for a in range(n):
            for k, pid, pflat in peers:
                copy = pltpu.make_async_remote_copy(
                    src_ref=src_refs[a].at[pflat] if per_peer else src_refs[a], dst_ref=land_refs[a].at[pflat],
                    send_sem=send_sems.at[7 * a + k - 1], recv_sem=recv_sems.at[7 * a + k - 1],
                    device_id=pid, device_id_type=MESH)
                copy.wait_send()
                copy.wait_recv()

    outs = pl.pallas_call(
        body, name=name,
        out_shape=tuple(pltpu.HBM(a.shape, a.dtype) for a in srcs_thru + lands_thru),
        in_specs=(HBM,) * (2 * n) + (SEM, SEM, pl.BlockSpec(memory_space=pl.ANY)), out_specs=(HBM,) * (2 * n),
        input_output_aliases={i: i for i in range(2 * n)},
        compiler_params=pltpu.CompilerParams(has_side_effects=DATAFLOW),
    )(*srcs_thru, *lands_thru, send_sems, recv_sems, after)
    me = 4 * lax.axis_index("x") + 2 * lax.axis_index("y") + lax.axis_index("c")
    landed = []
    for src_out, land in zip(outs[:n], outs[n:]):
        own = lax.dynamic_index_in_dim(src_out, me, 0, keepdims=True) if per_peer else src_out[None]
        landed.append(lax.dynamic_update_slice(land, own, (me,) + (0,) * (land.ndim - 1)))
    return landed


def _share_rows(block, name):
    def body(src_ref, out_ref, send_sems, recv_sems, local_sem):
        me, peers = _peers()
        own = pltpu.make_async_copy(src_ref, out_ref.at[me], local_sem)
        own.start()
        copies = [pltpu.make_async_remote_copy(
            src_ref=src_ref, dst_ref=out_ref.at[me], send_sem=send_sems.at[k - 1], recv_sem=recv_sems.at[k - 1],
            device_id=pid, device_id_type=MESH) for k, pid, _ in peers]
        for cp in copies:
            cp.start()
        for cp in copies:
            cp.wait()
        own.wait()

    return pl.pallas_call(
        body, name=name, out_shape=jax.ShapeDtypeStruct((N_DEV,) + block.shape, block.dtype),
        in_specs=[HBM], out_specs=HBM,
        scratch_shapes=[pltpu.SemaphoreType.DMA((N_DEV - 1,)), pltpu.SemaphoreType.DMA((N_DEV - 1,)),
                        pltpu.SemaphoreType.DMA],
    )(block)


def _sum_slots(recv, name, tr):
    n, rows, lanes = recv.shape
    tr = _tile(rows, tr)

    def body(r_ref, o_ref):
        acc = r_ref[0].astype(F32)
        for i in range(1, n):
            acc = acc + r_ref[i].astype(F32)
        o_ref[...] = acc

    return pl.pallas_call(
        body, name=name, grid=(rows // tr,),
        in_specs=[pl.BlockSpec((n, tr, lanes), lambda i: (0, i, 0))],
        out_specs=pl.BlockSpec((tr, lanes), lambda i: (i, 0)),
        out_shape=jax.ShapeDtypeStruct((rows, lanes), F32),
        compiler_params=_cparams(("parallel",)),
    )(recv)


def _adamw_math(w, g, m, v):
    m = ADAM_B1 * m + (1.0 - ADAM_B1) * g
    v = ADAM_B2 * v + (1.0 - ADAM_B2) * (g * g)
    m_hat = m / (1.0 - ADAM_B1 ** ADAM_STEP)
    v_hat = v / (1.0 - ADAM_B2 ** ADAM_STEP)
    return -ADAM_LR * (m_hat / (jnp.sqrt(v_hat) + ADAM_EPS) + ADAM_WD * w), m, v


def _adamw(w, g, m, v, name, tr=256):
    return _rowwise(_adamw_math, name, tr, [w, g, m, v], [], [(w.shape[1], F32)] * 3)


def _adamw_small(ws, gs, ms, vs):
    n = len(ws)

    def body(*refs):
        ins, outs = refs[:4 * n], refs[4 * n:]
        for i in range(n):
            d, m, v = _adamw_math(ins[i][...], ins[n + i][...], ins[2 * n + i][...], ins[3 * n + i][...])
            outs[i][...], outs[n + i][...], outs[2 * n + i][...] = d, m, v

    vmem = pl.BlockSpec(memory_space=pltpu.VMEM)
    res = pl.pallas_call(
        body, name="adamw_small", in_specs=[vmem] * (4 * n), out_specs=[vmem] * (3 * n),
        out_shape=[jax.ShapeDtypeStruct(w.shape, F32) for w in ws] * 3,
    )(*ws, *gs, *ms, *vs)
    return res[:n], res[n:2 * n], res[2 * n:]


def _sum_adamw(recv, w, m, v, name):
    n, r, c = recv.shape
    tr = _tile(r, 256)

    def body(r_ref, w_ref, m_ref, v_ref, g_ref, d_ref, nm_ref, nv_ref):
        g = r_ref[0].astype(F32)
        for i in range(1, n):
            g = g + r_ref[i].astype(F32)
        g_ref[...] = g
        d_ref[...], nm_ref[...], nv_ref[...] = _adamw_math(w_ref[...], g, m_ref[...], v_ref[...])

    row = pl.BlockSpec((None, tr, c), lambda i: (0, i, 0))
    return pl.pallas_call(
        body, name=name, grid=(r // tr,),
        in_specs=[pl.BlockSpec((n, tr, c), lambda i: (0, i, 0)), row, row, row], out_specs=[row] * 4,
        out_shape=[jax.ShapeDtypeStruct((1, r, c), F32)] * 4, compiler_params=_cparams(("parallel",)),
    )(recv, w, m, v)


def kernel(x, mem, positions, ffn1_norm, ffn1_w_gu, ffn1_w_down, mix_norm, w_in, b_gate, sg_ln_g, sg_ln_b, sg_w, sg_b, mla_cq_norm, mla_w_uq, mla_ckv_norm, mla_w_ukv, mla_q_norm, mla_k_norm, mem_norm, mem_w_kv, mem_q_norm, mem_k_norm, w_branch_a, w_branch_b, w_branch_c, w_out, ffn2_norm, ffn2_w_gu, ffn2_w_down, loss_target, m_ffn1_norm, m_ffn1_w_gu, m_ffn1_w_down, m_mix_norm, m_w_in, m_b_gate, m_sg_ln_g, m_sg_ln_b, m_sg_w, m_sg_b, m_mla_cq_norm, m_mla_w_uq, m_mla_ckv_norm, m_mla_w_ukv, m_mla_q_norm, m_mla_k_norm, m_mem_norm, m_mem_w_kv, m_mem_q_norm, m_mem_k_norm, m_w_branch_a, m_w_branch_b, m_w_branch_c, m_w_out, m_ffn2_norm, m_ffn2_w_gu, m_ffn2_w_down, v_ffn1_norm, v_ffn1_w_gu, v_ffn1_w_down, v_mix_norm, v_w_in, v_b_gate, v_sg_ln_g, v_sg_ln_b, v_sg_w, v_sg_b, v_mla_cq_norm, v_mla_w_uq, v_mla_ckv_norm, v_mla_w_ukv, v_mla_q_norm, v_mla_k_norm, v_mem_norm, v_mem_w_kv, v_mem_q_norm, v_mem_k_norm, v_w_branch_a, v_w_branch_b, v_w_branch_c, v_w_out, v_ffn2_norm, v_ffn2_w_gu, v_ffn2_w_down):
    given = dict(locals())
    wts = {n: given[n] for n in ORDER}
    mom = {n: given["m_" + n] for n in ORDER}
    var = {n: given["v_" + n] for n in ORDER}

    def shards(group, zero):
        out = [wts[n][0].astype(BF16) for n in GROUPS[group]]
        return [out[0] + zero.astype(BF16)] + out[1:]

    def full_weights(group, slabs):
        return _compute_layout({n: _full_from_slabs(n, s) for n, s in zip(GROUPS[group], slabs)})

    def zero_of(a):
        return jnp.minimum(jnp.abs(a.reshape(-1)[0]), 0)

    gathered_ffn1, token = _all_gather([wts[n][0].astype(BF16) for n in GROUPS["ffn1"]])
    flight = {}
    flight["ffn1_down"], token = _send_start(shards("ffn1_down", token[0, 0]), False, "gather_ffn1_down_start")
    flight["mix"] = _send_start(shards("mix", token[0, 0]), False, "gather_mix_start")[0]
    recv = {}

    def weights(group, after):
        if group == "ffn1":
            return full_weights(group, gathered_ffn1)
        landed = _send_wait(flight.pop(group), after, False, f"gather_{group}_wait")
        if group == "mix":
            flight["ffn2"] = _send_start(shards("ffn2", zero_of(landed[0])), False, "gather_ffn2_start")[0]
        return full_weights(group, landed)

    small_shapes = [wts[n].shape[1:] for n in SMALL]
    early = SMALL[1:]
    assert SMALL[0] == "ffn1_norm"

    def grads_out(group, G):
        Gr = _reference_layout({n: G[n] for n in GRAD_GROUPS[group]})
        parts = [_slabs_from_full(n, Gr[n]).astype(BF16) for n in GRAD_GROUPS[group]]
        flight["g_" + group], tie = _send_start(parts, True, f"grads_{group}_start")
        if group == "mix":
            small = _pack([G[n].reshape(s) for n, s in zip(early, small_shapes[1:])])
            small = jnp.pad(small, ((0, (-small.shape[0]) % 8), (0, 0)))
            flight["small"], tie = _send_start([small + tie[0, 0]], False, "grads_small_start")
        return tie

    P = {n: wts[n] if wts[n].ndim == 2 else wts[n][0] for n in SMALL}
    loss_part, grad_x, G = _local_step(x[0], mem[0], positions[0], loss_target[0], P, weights, grads_out)

    for group, names in GRAD_GROUPS.items():
        recv.update(zip(names, _send_wait(flight.pop("g_" + group), grad_x, True, f"grads_{group}_wait")))
    early_recv, = _send_wait(flight.pop("small"), grad_x, False, "grads_small_wait")
    last = _share_rows(G["ffn1_norm"].reshape(-1, LANES), "share_ffn1_norm")
    g_small_packed = _sum_slots(jnp.concatenate([last, early_recv], axis=1), "sum_small", 2048)

    grads, delta, new_m, new_v = {}, {}, {}, {}
    for n in SHARDED:
        grads[n], delta[n], new_m[n], new_v[n] = _sum_adamw(recv[n], wts[n], mom[n], var[n], "adamw_" + n)
    grads.update(zip(SMALL, _unpack(g_small_packed, small_shapes)))

    flat2 = lambda d: [d[n].reshape(-1, d[n].shape[-1]) for n in SMALL]
    for dst, vals in zip((delta, new_m, new_v), _adamw_small(flat2(wts), flat2(grads), flat2(mom), flat2(var))):
        dst.update(zip(SMALL, vals))

    loss = lax.psum(jnp.sum(loss_part), ("x", "y", "c"))
    lead = lambda d: [d[n].reshape(wts[n].shape) for n in ORDER]
    return (loss, grad_x[None], *lead(grads), *lead(delta), *lead(new_m), *lead(new_v))
```
